```python
import jax, jax.numpy as jnp
from jax import lax
import numpy as np

D_MODEL = 2048
BATCH = 8
SEQ = 4096
DEPTH = 1

HEAD_DIM = 64
ATTN_WIDTH = D_MODEL // 2
CONV_WIDTH_CH = D_MODEL - ATTN_WIDTH
N_Q_HEADS = ATTN_WIDTH // HEAD_DIM
N_KV_HEADS = 4
GROUP = N_Q_HEADS // N_KV_HEADS
KV_WIDTH = N_KV_HEADS * HEAD_DIM
N_CONV_GROUPS = CONV_WIDTH_CH // HEAD_DIM
CONV_WIDTH = 3
WINDOW = 128
ROT_DIM = HEAD_DIM // 4
ROPE_THETA = 500000.0
D_FF = ((8 * D_MODEL // 3 + 255) // 256) * 256
IN_WIDTH = ATTN_WIDTH + 2 * KV_WIDTH + 3 * CONV_WIDTH_CH
SPLITS = [ATTN_WIDTH,
          ATTN_WIDTH + KV_WIDTH,
          ATTN_WIDTH + 2 * KV_WIDTH,
          ATTN_WIDTH + 2 * KV_WIDTH + CONV_WIDTH_CH,
          ATTN_WIDTH + 2 * KV_WIDTH + 2 * CONV_WIDTH_CH]
ATTN_SCALE = HEAD_DIM ** -0.5
DEEPNORM_ALPHA = (2 * DEPTH) ** 0.25
DEEPNORM_BETA = (8 * DEPTH) ** -0.25
LN_EPS = 1e-5
RMS_EPS = 1e-6

kernel_name = "hymba_conv_swa_sink_deepnorm_layer"


def _layer_norm(x, g, b):
    xf = x.astype(jnp.float32)
    mu = jnp.mean(xf, axis=-1, keepdims=True)
    var = jnp.mean(jnp.square(xf - mu), axis=-1, keepdims=True)
    y = (xf - mu) * lax.rsqrt(var + LN_EPS)
    return (y * g.astype(jnp.float32) + b.astype(jnp.float32)).astype(x.dtype)


def _rms_norm(x, g):
    xf = x.astype(jnp.float32)
    y = xf * lax.rsqrt(jnp.mean(jnp.square(xf), axis=-1, keepdims=True) + RMS_EPS)
    return (y * g.astype(jnp.float32)).astype(x.dtype)


def _partial_rope(t, positions):
    inv_freq = ROPE_THETA ** (-jnp.arange(0, ROT_DIM, 2, dtype=jnp.float32) / ROT_DIM)
    ang = positions.astype(jnp.float32)[..., None] * inv_freq
    cos = jnp.cos(ang)[:, :, None, :]
    sin = jnp.sin(ang)[:, :, None, :]
    tr = t[..., :ROT_DIM].astype(jnp.float32)
    t1, t2 = tr[..., :ROT_DIM // 2], tr[..., ROT_DIM // 2:]
    rot = jnp.concatenate([t1 * cos - t2 * sin, t2 * cos + t1 * sin], axis=-1).astype(t.dtype)
    return jnp.concatenate([rot, t[..., ROT_DIM:]], axis=-1)


def _sliding_window_attention(q, k, v, sinks):
    b, s = q.shape[0], q.shape[1]
    nb = s // WINDOW
    qb = q.reshape(b, nb, WINDOW, N_KV_HEADS, GROUP, HEAD_DIM)
    kb = k.reshape(b, nb, WINDOW, N_KV_HEADS, HEAD_DIM)
    vb = v.reshape(b, nb, WINDOW, N_KV_HEADS, HEAD_DIM)

    def with_prev(t):
        prev = jnp.concatenate([jnp.zeros_like(t[:, :1]), t[:, :-1]], axis=1)
        return jnp.concatenate([prev, t], axis=2)

    kk, vv = with_prev(kb), with_prev(vb)
    scores = jnp.einsum('bnqhgd,bnkhd->bnhgqk', qb, kk,
                        preferred_element_type=jnp.float32) * ATTN_SCALE
    qi = jnp.arange(WINDOW)[:, None]
    kj = jnp.arange(2 * WINDOW)[None, :]
    rel = qi + WINDOW - kj
    band = (rel >= 0) & (rel < WINDOW)
    first_block = (jnp.arange(nb) == 0)[:, None, None]
    valid = band[None] & ~(first_block & (kj < WINDOW)[None])
    scores = jnp.where(valid[None, :, None, None], scores, -jnp.inf)
    sink_col = jnp.broadcast_to(
        sinks.astype(jnp.float32).reshape(N_KV_HEADS, GROUP)[None, None, :, :, None, None],
        scores.shape[:-1] + (1,))
    probs = jax.nn.softmax(jnp.concatenate([scores, sink_col], axis=-1), axis=-1)[..., :-1]
    out = jnp.einsum('bnhgqk,bnkhd->bnqhgd', probs.astype(v.dtype), vv)
    return out.reshape(b, s, N_Q_HEADS * HEAD_DIM)


def _short_gated_conv(c_gate, b_gate, u, conv_w):
    z = c_gate * u
    s = z.shape[1]
    zp = jnp.pad(z, ((0, 0), (CONV_WIDTH - 1, 0), (0, 0)))
    y = conv_w[0] * zp[:, 0:s] + conv_w[1] * zp[:, 1:s + 1] + conv_w[2] * zp[:, 2:s + 2]
    return b_gate * y


def _fwd_setup_inputs(seed: int = 0) -> dict:
    key = jax.random.key(seed)
    ks = jax.random.split(key, 16)
    f32 = jnp.float32
    x = jax.random.normal(ks[0], (BATCH, SEQ, D_MODEL), f32)
    offset = jax.random.randint(ks[1], (BATCH, 1), 0, 1024, dtype=jnp.int32)
    positions = (offset + jnp.arange(SEQ, dtype=jnp.int32)[None, :]).astype(jnp.int32)
    w_in = jax.random.normal(ks[2], (DEPTH, D_MODEL, IN_WIDTH), f32) * D_MODEL ** -0.5
    conv_w = jax.random.normal(ks[3], (DEPTH, CONV_WIDTH, CONV_WIDTH_CH), f32) * CONV_WIDTH ** -0.5
    sinks = jax.random.normal(ks[4], (DEPTH, N_Q_HEADS), f32) * 0.5
    g_attn = 1.0 + 0.02 * jax.random.normal(ks[5], (DEPTH, ATTN_WIDTH), f32)
    g_conv = 1.0 + 0.02 * jax.random.normal(ks[6], (DEPTH, CONV_WIDTH_CH), f32)
    w_out = jax.random.normal(ks[7], (DEPTH, D_MODEL, D_MODEL), f32) * (DEEPNORM_BETA * D_MODEL ** -0.5)
    ln1_g = 1.0 + 0.02 * jax.random.normal(ks[8], (DEPTH, D_MODEL), f32)
    ln1_b = 0.02 * jax.random.normal(ks[9], (DEPTH, D_MODEL), f32)
    w_gate = jax.random.normal(ks[10], (DEPTH, D_MODEL, D_FF), f32) * D_MODEL ** -0.5
    w_up = jax.random.normal(ks[11], (DEPTH, D_MODEL, D_FF), f32) * D_MODEL ** -0.5
    w_down = jax.random.normal(ks[12], (DEPTH, D_FF, D_MODEL), f32) * (DEEPNORM_BETA * D_FF ** -0.5)
    ln2_g = 1.0 + 0.02 * jax.random.normal(ks[13], (DEPTH, D_MODEL), f32)
    ln2_b = 0.02 * jax.random.normal(ks[14], (DEPTH, D_MODEL), f32)
    return {"x": x, "positions": positions, "w_in": w_in, "conv_w": conv_w,
            "sinks": sinks, "g_attn": g_attn, "g_conv": g_conv, "w_out": w_out,
            "ln1_g": ln1_g, "ln1_b": ln1_b, "w_gate": w_gate, "w_up": w_up,
            "w_down": w_down, "ln2_g": ln2_g, "ln2_b": ln2_b}


def _fwd_reference(x, positions, w_in, conv_w, sinks, g_attn, g_conv, w_out,
              ln1_g, ln1_b, w_gate, w_up, w_down, ln2_g, ln2_b):
    b, s, _ = x.shape
    h = x
    for l in range(DEPTH):
        proj = h @ w_in[l]
        q, k, v, c_gate, b_gate, u = jnp.split(proj, SPLITS, axis=-1)
        q = _partial_rope(q.reshape(b, s, N_Q_HEADS, HEAD_DIM), positions)
        k = _partial_rope(k.reshape(b, s, N_KV_HEADS, HEAD_DIM), positions)
        v = v.reshape(b, s, N_KV_HEADS, HEAD_DIM)
        attn = _sliding_window_attention(q, k, v, sinks[l])
        conv = _short_gated_conv(c_gate, b_gate, u, conv_w[l])
        mixed = jnp.concatenate([_rms_norm(attn, g_attn[l]), _rms_norm(conv, g_conv[l])], axis=-1)
        mix_out = mixed @ w_out[l]
        h = _layer_norm(DEEPNORM_ALPHA * h + mix_out, ln1_g[l], ln1_b[l])
        ffn = (jax.nn.silu(h @ w_gate[l]) * (h @ w_up[l])) @ w_down[l]
        h = _layer_norm(DEEPNORM_ALPHA * h + ffn, ln2_g[l], ln2_b[l])
    return h


import jax as _jax
import jax.numpy as _jnp

TWIN_FORMAT = 'train_step'
FWD_PARAMS = ['x', 'positions', 'w_in', 'conv_w', 'sinks', 'g_attn', 'g_conv', 'w_out', 'ln1_g', 'ln1_b', 'w_gate', 'w_up', 'w_down', 'ln2_g', 'ln2_b']
TWIN_WEIGHTS = ['w_in', 'conv_w', 'sinks', 'g_attn', 'g_conv', 'w_out', 'ln1_g', 'ln1_b', 'w_gate', 'w_up', 'w_down', 'ln2_g', 'ln2_b']
TWIN_DIFF_INPUT = 'x'
TWIN_INPUTS = ['x', 'positions', 'w_in', 'conv_w', 'sinks', 'g_attn', 'g_conv', 'w_out', 'ln1_g', 'ln1_b', 'w_gate', 'w_up', 'w_down', 'ln2_g', 'ln2_b', 'loss_target', 'm_w_in', 'm_conv_w', 'm_sinks', 'm_g_attn', 'm_g_conv', 'm_w_out', 'm_ln1_g', 'm_ln1_b', 'm_w_gate', 'm_w_up', 'm_w_down', 'm_ln2_g', 'm_ln2_b', 'v_w_in', 'v_conv_w', 'v_sinks', 'v_g_attn', 'v_g_conv', 'v_w_out', 'v_ln1_g', 'v_ln1_b', 'v_w_gate', 'v_w_up', 'v_w_down', 'v_ln2_g', 'v_ln2_b']
TWIN_OUTPUTS = ['loss', 'grad_x', 'grad_w_in', 'grad_conv_w', 'grad_sinks', 'grad_g_attn', 'grad_g_conv', 'grad_w_out', 'grad_ln1_g', 'grad_ln1_b', 'grad_w_gate', 'grad_w_up', 'grad_w_down', 'grad_ln2_g', 'grad_ln2_b', 'delta_w_in', 'delta_conv_w', 'delta_sinks', 'delta_g_attn', 'delta_g_conv', 'delta_w_out', 'delta_ln1_g', 'delta_ln1_b', 'delta_w_gate', 'delta_w_up', 'delta_w_down', 'delta_ln2_g', 'delta_ln2_b', 'new_m_w_in', 'new_m_conv_w', 'new_m_sinks', 'new_m_g_attn', 'new_m_g_conv', 'new_m_w_out', 'new_m_ln1_g', 'new_m_ln1_b', 'new_m_w_gate', 'new_m_w_up', 'new_m_w_down', 'new_m_ln2_g', 'new_m_ln2_b', 'new_v_w_in', 'new_v_conv_w', 'new_v_sinks', 'new_v_g_attn', 'new_v_g_conv', 'new_v_w_out', 'new_v_ln1_g', 'new_v_ln1_b', 'new_v_w_gate', 'new_v_w_up', 'new_v_w_down', 'new_v_ln2_g', 'new_v_ln2_b']
TWIN_LEAF_KINDS = {'loss': 'loss', 'grad_x': 'grad_x', 'grad_w_in': 'grad_w', 'grad_conv_w': 'grad_w', 'grad_sinks': 'grad_w', 'grad_g_attn': 'grad_w', 'grad_g_conv': 'grad_w', 'grad_w_out': 'grad_w', 'grad_ln1_g': 'grad_w', 'grad_ln1_b': 'grad_w', 'grad_w_gate': 'grad_w', 'grad_w_up': 'grad_w', 'grad_w_down': 'grad_w', 'grad_ln2_g': 'grad_w', 'grad_ln2_b': 'grad_w', 'delta_w_in': 'delta_w', 'delta_conv_w': 'delta_w', 'delta_sinks': 'delta_w', 'delta_g_attn': 'delta_w', 'delta_g_conv': 'delta_w', 'delta_w_out': 'delta_w', 'delta_ln1_g': 'delta_w', 'delta_ln1_b': 'delta_w', 'delta_w_gate': 'delta_w', 'delta_w_up': 'delta_w', 'delta_w_down': 'delta_w', 'delta_ln2_g': 'delta_w', 'delta_ln2_b': 'delta_w', 'new_m_w_in': 'new_m', 'new_m_conv_w': 'new_m', 'new_m_sinks': 'new_m', 'new_m_g_attn': 'new_m', 'new_m_g_conv': 'new_m', 'new_m_w_out': 'new_m', 'new_m_ln1_g': 'new_m', 'new_m_ln1_b': 'new_m', 'new_m_w_gate': 'new_m', 'new_m_w_up': 'new_m', 'new_m_w_down': 'new_m', 'new_m_ln2_g': 'new_m', 'new_m_ln2_b': 'new_m', 'new_v_w_in': 'new_v', 'new_v_conv_w': 'new_v', 'new_v_sinks': 'new_v', 'new_v_g_attn': 'new_v', 'new_v_g_conv': 'new_v', 'new_v_w_out': 'new_v', 'new_v_ln1_g': 'new_v', 'new_v_ln1_b': 'new_v', 'new_v_w_gate': 'new_v', 'new_v_w_up': 'new_v', 'new_v_w_down': 'new_v', 'new_v_ln2_g': 'new_v', 'new_v_ln2_b': 'new_v'}


def _forward(args):
    return _fwd_reference(*[args[k] for k in FWD_PARAMS])


def _output_shape():
    def fwd():
        inp = _fwd_setup_inputs(0)
        return _fwd_reference(*[inp[k] for k in FWD_PARAMS])
    out = _jax.eval_shape(fwd)
    return out.shape, out.dtype

N_MICROBATCH = 1
ADAM_LR = 0.001
ADAM_B1 = 0.9
ADAM_B2 = 0.999
ADAM_EPS = 1e-08
ADAM_WD = 0.01
ADAM_STEP = 10
PER_EXAMPLE_BATCH_AXIS = {'x': 0, 'positions': 0, 'loss_target': 0}
SHARED_INPUTS = []
_WEIGHT_DTYPES = {'w_in': _jnp.float32, 'conv_w': _jnp.float32, 'sinks': _jnp.float32, 'g_attn': _jnp.float32, 'g_conv': _jnp.float32, 'w_out': _jnp.float32, 'ln1_g': _jnp.float32, 'ln1_b': _jnp.float32, 'w_gate': _jnp.float32, 'w_up': _jnp.float32, 'w_down': _jnp.float32, 'ln2_g': _jnp.float32, 'ln2_b': _jnp.float32}
MOMENT_SCALE = {'w_in': 4.772449e-02, 'conv_w': 4.207751e-02, 'sinks': 1.418773e-02, 'g_attn': 4.233038e-02, 'g_conv': 4.458684e-02, 'w_out': 6.992915e-02, 'ln1_g': 5.437723e-01, 'ln1_b': 2.774332e-01, 'w_gate': 1.576985e-02, 'w_up': 1.531647e-02, 'w_down': 4.268123e-02, 'ln2_g': 1.600763e+01, 'ln2_b': 4.122829e-01}


def _to_microbatches(a, axis):
    t = _jnp.moveaxis(a, axis, 0)
    t = t.reshape((N_MICROBATCH, t.shape[0] // N_MICROBATCH) + t.shape[1:])
    return _jnp.moveaxis(t, 1, axis + 1)


def setup_inputs(seed: int = 0) -> dict:
    inp = _fwd_setup_inputs(seed)
    key = _jax.random.fold_in(_jax.random.key(seed), 7919)
    shape, _ = _output_shape()
    out = dict(inp)
    out["loss_target"] = _jax.random.normal(_jax.random.fold_in(key, 0), shape, _jnp.float32)
    for i, name in enumerate(TWIN_WEIGHTS):
        w = inp[name].astype(_jnp.float32)
        if MOMENT_SCALE is None:
            s = _jnp.sqrt(_jnp.mean(_jnp.square(w)) + 1e-30)
        else:
            s = MOMENT_SCALE[name]
        km, kv = _jax.random.split(_jax.random.fold_in(key, i + 1))
        out[name] = w
        out["m_" + name] = s * _jax.random.normal(km, w.shape, _jnp.float32)
        out["v_" + name] = (s * s) * _jax.random.uniform(kv, w.shape, _jnp.float32, 0.5, 1.5)
    if N_MICROBATCH > 1:
        for name, axis in PER_EXAMPLE_BATCH_AXIS.items():
            out[name] = _to_microbatches(out[name], axis)
    return {'x': out['x'], 'positions': out['positions'], 'w_in': out['w_in'], 'conv_w': out['conv_w'], 'sinks': out['sinks'], 'g_attn': out['g_attn'], 'g_conv': out['g_conv'], 'w_out': out['w_out'], 'ln1_g': out['ln1_g'], 'ln1_b': out['ln1_b'], 'w_gate': out['w_gate'], 'w_up': out['w_up'], 'w_down': out['w_down'], 'ln2_g': out['ln2_g'], 'ln2_b': out['ln2_b'], 'loss_target': out['loss_target'], 'm_w_in': out['m_w_in'], 'm_conv_w': out['m_conv_w'], 'm_sinks': out['m_sinks'], 'm_g_attn': out['m_g_attn'], 'm_g_conv': out['m_g_conv'], 'm_w_out': out['m_w_out'], 'm_ln1_g': out['m_ln1_g'], 'm_ln1_b': out['m_ln1_b'], 'm_w_gate': out['m_w_gate'], 'm_w_up': out['m_w_up'], 'm_w_down': out['m_w_down'], 'm_ln2_g': out['m_ln2_g'], 'm_ln2_b': out['m_ln2_b'], 'v_w_in': out['v_w_in'], 'v_conv_w': out['v_conv_w'], 'v_sinks': out['v_sinks'], 'v_g_attn': out['v_g_attn'], 'v_g_conv': out['v_g_conv'], 'v_w_out': out['v_w_out'], 'v_ln1_g': out['v_ln1_g'], 'v_ln1_b': out['v_ln1_b'], 'v_w_gate': out['v_w_gate'], 'v_w_up': out['v_w_up'], 'v_w_down': out['v_w_down'], 'v_ln2_g': out['v_ln2_g'], 'v_ln2_b': out['v_ln2_b']}


def _loss(weights, diff, rest, loss_target):
    with _jax.named_scope("forward"):
        args = {**rest, TWIN_DIFF_INPUT: diff, **{k: w.astype(_WEIGHT_DTYPES[k]) for k, w in weights.items()}}
        y = _forward(args)
    with _jax.named_scope("loss_head"):
        err = _jnp.square(y.astype(_jnp.float32) - loss_target)
        return 0.5 * _jnp.sum(_jnp.mean(err, axis=-1)) if err.ndim else 0.5 * err


def _adamw(w, g, m, v):
    m = ADAM_B1 * m + (1.0 - ADAM_B1) * g
    v = ADAM_B2 * v + (1.0 - ADAM_B2) * _jnp.square(g)
    m_hat = m / (1.0 - ADAM_B1 ** ADAM_STEP)
    v_hat = v / (1.0 - ADAM_B2 ** ADAM_STEP)
    delta = -ADAM_LR * (m_hat / (_jnp.sqrt(v_hat) + ADAM_EPS) + ADAM_WD * w)
    return delta, m, v


def reference(x, positions, w_in, conv_w, sinks, g_attn, g_conv, w_out, ln1_g, ln1_b, w_gate, w_up, w_down, ln2_g, ln2_b, loss_target, m_w_in, m_conv_w, m_sinks, m_g_attn, m_g_conv, m_w_out, m_ln1_g, m_ln1_b, m_w_gate, m_w_up, m_w_down, m_ln2_g, m_ln2_b, v_w_in, v_conv_w, v_sinks, v_g_attn, v_g_conv, v_w_out, v_ln1_g, v_ln1_b, v_w_gate, v_w_up, v_w_down, v_ln2_g, v_ln2_b):
    given = dict(x=x, positions=positions, w_in=w_in, conv_w=conv_w, sinks=sinks, g_attn=g_attn, g_conv=g_conv, w_out=w_out, ln1_g=ln1_g, ln1_b=ln1_b, w_gate=w_gate, w_up=w_up, w_down=w_down, ln2_g=ln2_g, ln2_b=ln2_b, loss_target=loss_target, m_w_in=m_w_in, m_conv_w=m_conv_w, m_sinks=m_sinks, m_g_attn=m_g_attn, m_g_conv=m_g_conv, m_w_out=m_w_out, m_ln1_g=m_ln1_g, m_ln1_b=m_ln1_b, m_w_gate=m_w_gate, m_w_up=m_w_up, m_w_down=m_w_down, m_ln2_g=m_ln2_g, m_ln2_b=m_ln2_b, v_w_in=v_w_in, v_conv_w=v_conv_w, v_sinks=v_sinks, v_g_attn=v_g_attn, v_g_conv=v_g_conv, v_w_out=v_w_out, v_ln1_g=v_ln1_g, v_ln1_b=v_ln1_b, v_w_gate=v_w_gate, v_w_up=v_w_up, v_w_down=v_w_down, v_ln2_g=v_ln2_g, v_ln2_b=v_ln2_b)
    weights = {n: given[n] for n in TWIN_WEIGHTS}
    shared = {n: given[n] for n in SHARED_INPUTS}
    per_example = {n: given[n] for n in ['x', 'positions']}
    grad_fn = _jax.value_and_grad(_loss, argnums=(0, 1))

    def one_microbatch(ex, loss_target):
        ex = dict(ex)
        diff = ex.pop(TWIN_DIFF_INPUT)
        return grad_fn(weights, diff, {**shared, **ex}, loss_target)

    if N_MICROBATCH == 1:
        loss, (grad_w, grad_x) = one_microbatch(per_example, given["loss_target"])
    else:
        def body(carry, xs):
            loss_sum, grad_sum = carry
            l_k, (gw_k, gx_k) = one_microbatch(xs[0], xs[1])
            with _jax.named_scope("update"):
                return (loss_sum + l_k, _jax.tree.map(_jnp.add, grad_sum, gw_k)), gx_k

        init = (_jnp.zeros((), _jnp.float32), _jax.tree.map(_jnp.zeros_like, weights))
        (loss, grad_w), grad_x = _jax.lax.scan(body, init, (per_example, given["loss_target"]))
    with _jax.named_scope("update"):
        delta_w, new_m, new_v = {}, {}, {}
        for n in TWIN_WEIGHTS:
            delta_w[n], new_m[n], new_v[n] = _adamw(weights[n], grad_w[n], given["m_" + n], given["v_" + n])
    return (loss, grad_x, *[grad_w[n] for n in TWIN_WEIGHTS], *[delta_w[n] for n in TWIN_WEIGHTS],
            *[new_m[n] for n in TWIN_WEIGHTS], *[new_v[n] for n in TWIN_WEIGHTS])
```

```python
import functools

import numpy as np
import jax
import jax.numpy as jnp
from jax import lax
from jax.experimental import pallas as pl
from jax.experimental.pallas import tpu as pltpu

F32 = jnp.float32
BF16 = jnp.bfloat16
MESH = pl.DeviceIdType.MESH

HEAD_DIM = 64
N_KV_HEADS = 4
GROUP = 4
WINDOW = 128
ROT_DIM = 16
ROPE_THETA = 500000.0
ATTN_SCALE = HEAD_DIM ** -0.5
ALPHA = 2.0 ** 0.25
LN_EPS = 1e-5
RMS_EPS = 1e-6
ADAM_LR = 0.001
ADAM_B1 = 0.9
ADAM_B2 = 0.999
ADAM_EPS = 1e-08
ADAM_WD = 0.01
ADAM_STEP = 10
N_CHIPS = 4
NEG_BIG = -1e30

V7X_VMEM_BYTES = 64 * 1024 * 1024
VMEM_LIMIT = V7X_VMEM_BYTES - 6 * 1024 * 1024

TM = 512
TK_TOK = 512
TB_CONV = 256
TR_ELT = 256
ROW_CHUNK = 128


def _params(sem):
    return pltpu.CompilerParams(dimension_semantics=sem, vmem_limit_bytes=VMEM_LIMIT)


def _row_tile(rows, target):
    best = None
    for t in range(16, min(rows, target) + 1, 16):
        if rows % t == 0:
            best = t
    assert best is not None, (rows, target)
    return best


def _dot(a, b):
    return jnp.dot(a, b, preferred_element_type=F32)


def _dot_nt(a, b):
    return lax.dot_general(a, b, (((1,), (1,)), ((), ())), preferred_element_type=F32)


def _dot_tn(a, b):
    return lax.dot_general(a, b, (((0,), (0,)), ((), ())), preferred_element_type=F32)


def _mesh_pos():
    x, y, c = lax.axis_index("x"), lax.axis_index("y"), lax.axis_index("c")
    chips = [(1 - x, y), (x, 1 - y), (1 - x, 1 - y)]
    return x, y, c, chips


def _chip_id(px, py):
    return 2 * px + py


def _rope(t, cos, sgn_sin, sign):
    w = t.shape[1]
    lane = lax.broadcasted_iota(jnp.int32, t.shape, 1) & (HEAD_DIM - 1)
    partner = jnp.where(lane < ROT_DIM // 2, pltpu.roll(t, w - ROT_DIM // 2, 1), pltpu.roll(t, ROT_DIM // 2, 1))
    return t * cos + sign * (partner * sgn_sin)


def _tile_lanes(t, n):
    return jnp.concatenate([t] * n, axis=1)


def _sigmoid(g):
    return 1.0 / (1.0 + jnp.exp(-g))


def _for_row_chunks(n_rows, fn):
    def step(r, carry):
        fn(pl.ds(pl.multiple_of(r * ROW_CHUNK, ROW_CHUNK), ROW_CHUNK))
        return carry

    lax.fori_loop(0, n_rows // ROW_CHUNK, step, 0)


def _ln_fwd(pre):
    mu = jnp.mean(pre, axis=-1, keepdims=True)
    cen = pre - mu
    var = jnp.mean(cen * cen, axis=-1, keepdims=True)
    rstd = lax.rsqrt(var + LN_EPS)
    return cen * rstd, rstd


def _ln_bwd(dy, xhat, rstd, g):
    dxhat = dy * g
    m1 = jnp.mean(dxhat, axis=-1, keepdims=True)
    m2 = jnp.mean(dxhat * xhat, axis=-1, keepdims=True)
    return rstd * (dxhat - m1 - xhat * m2)


def _cast_weight(w, name):
    _, r, c = w.shape
    tr = _row_tile(r, TR_ELT)

    def body(w_ref, o_ref):
        o_ref[...] = w_ref[...].astype(BF16)

    return pl.pallas_call(
        body, name=name, grid=(r // tr,),
        in_specs=[pl.BlockSpec((None, tr, c), lambda i: (0, i, 0))],
        out_specs=pl.BlockSpec((tr, c), lambda i: (i, 0)),
        out_shape=jax.ShapeDtypeStruct((r, c), BF16),
        compiler_params=_params(("parallel",)),
    )(w)


_HBM = pl.BlockSpec(memory_space=pltpu.HBM)
_VMEM = pl.BlockSpec(memory_space=pltpu.VMEM)


def _allgather_weights(own):
    n = len(own)
    shapes = [w.shape for w in own]

    def body(*refs):
        ins, outs = refs[:n], refs[n:2 * n]
        send_sems, recv_sems, local_sems = refs[2 * n:]
        x, y, c, chips = _mesh_pos()
        me = _chip_id(x, y)
        sibling = (x, y, 1 - c)
        theirs = [_chip_id(*chip) for chip in chips]

        def half(w, which):
            hr = shapes[w][0] // 2
            return pl.ds(which * hr, hr)

        def remote(w, k, src, dst, to):
            return pltpu.make_async_remote_copy(
                src_ref=src, dst_ref=dst, send_sem=send_sems.at[w, k], recv_sem=recv_sems.at[w, k],
                device_id=to, device_id_type=MESH)

        local, first, passed = [], [], []
        for w in range(n):
            cp = pltpu.make_async_copy(ins[w], outs[w].at[me], local_sems.at[w])
            cp.start()
            local.append(cp)
            for k, chip in enumerate(chips):
                cp = remote(w, k, ins[w].at[half(w, c)], outs[w].at[me, half(w, c)], (*chip, c))
                cp.start()
                first.append(cp)
        for w in range(n):
            for k in range(3):
                landed = outs[w].at[theirs[k], half(w, c)]
                remote(w, k, landed, landed, sibling).wait_recv()
                cp = remote(w, 3 + k, landed, landed, sibling)
                cp.start()
                passed.append(cp)
        for w in range(n):
            for k in range(3):
                landed = outs[w].at[theirs[k], half(w, 1 - c)]
                remote(w, 3 + k, landed, landed, sibling).wait_recv()
        for cp in first + passed:
            cp.wait_send()
        for cp in local:
            cp.wait()

    return pl.pallas_call(
        body, name="allgather_weights",
        in_specs=[_HBM] * n, out_specs=[_HBM] * n,
        out_shape=[jax.ShapeDtypeStruct((N_CHIPS,) + s, BF16) for s in shapes],
        scratch_shapes=[pltpu.SemaphoreType.DMA((n, 6)), pltpu.SemaphoreType.DMA((n, 6)),
                        pltpu.SemaphoreType.DMA((n,))],
    )(*own)


def _allgather_conv_w(cw):
    _, kw, cs = cw.shape

    def body(cw_ref, out_ref, send_sems, recv_sems):
        x, y, c, chips = _mesh_pos()
        me = _chip_id(x, y)
        out_ref[pl.ds(me, 1)] = cw_ref[...]
        copies = []
        for k, chip in enumerate(chips):
            cp = pltpu.make_async_remote_copy(
                src_ref=cw_ref.at[0], dst_ref=out_ref.at[me], send_sem=send_sems.at[k], recv_sem=recv_sems.at[k],
                device_id=(*chip, c), device_id_type=MESH)
            cp.start()
            copies.append(cp)
        for k, chip in enumerate(chips):
            pltpu.make_async_remote_copy(
                src_ref=cw_ref.at[0], dst_ref=out_ref.at[_chip_id(*chip)], send_sem=send_sems.at[k],
                recv_sem=recv_sems.at[k], device_id=(*chip, c), device_id_type=MESH).wait_recv()
        for cp in copies:
            cp.wait_send()

    return pl.pallas_call(
        body, name="allgather_conv_w",
        in_specs=[_VMEM], out_specs=_VMEM,
        out_shape=jax.ShapeDtypeStruct((N_CHIPS, kw, cs), F32),
        scratch_shapes=[pltpu.SemaphoreType.DMA((3,)), pltpu.SemaphoreType.DMA((3,))],
    )(cw)


def _exchange_halves(parts):
    n = len(parts)
    shapes = [p.shape for p in parts]

    def body(*refs):
        ins, outs = refs[:n], refs[n:2 * n]
        send_sems, recv_sems = refs[2 * n:]
        x, y, c, _ = _mesh_pos()
        copies = []
        for w in range(n):
            hr = shapes[w][1] // 2
            cp = pltpu.make_async_remote_copy(
                src_ref=ins[w].at[:, pl.ds((1 - c) * hr, hr)], dst_ref=outs[w],
                send_sem=send_sems.at[w], recv_sem=recv_sems.at[w],
                device_id=(x, y, 1 - c), device_id_type=MESH)
            cp.start()
            copies.append(cp)
        for cp in copies:
            cp.wait()

    return pl.pallas_call(
        body, name="exchange_halves",
        in_specs=[_HBM] * n, out_specs=[_HBM] * n,
        out_shape=[jax.ShapeDtypeStruct((s[0], s[1] // 2, s[2]), BF16) for s in shapes],
        scratch_shapes=[pltpu.SemaphoreType.DMA((n,)), pltpu.SemaphoreType.DMA((n,))],
    )(*parts)


def _add_halves(part, got, cvec, name):
    ns, r, cdim = part.shape
    hr = r // 2
    tr = _row_tile(hr, TR_ELT)
    nblk = hr // tr

    def body(c_ref, a_ref, b_ref, o_ref):
        o_ref[...] = (a_ref[...].astype(F32) + b_ref[...].astype(F32)).astype(BF16)

    grid_spec = pltpu.PrefetchScalarGridSpec(
        num_scalar_prefetch=1, grid=(ns, nblk),
        in_specs=[pl.BlockSpec((None, tr, cdim), lambda s, i, c_ref: (s, c_ref[0] * nblk + i, 0)),
                  pl.BlockSpec((None, tr, cdim), lambda s, i, c_ref: (s, i, 0))],
        out_specs=pl.BlockSpec((None, tr, cdim), lambda s, i, c_ref: (s, i, 0)))
    return pl.pallas_call(
        body, name=name, grid_spec=grid_spec,
        out_shape=jax.ShapeDtypeStruct((ns, hr, cdim), BF16),
        compiler_params=_params(("parallel", "parallel")),
    )(cvec, part, got)


def _scatter_chip_sums(sums):
    n = len(sums)
    shapes = [s.shape for s in sums]

    def body(*refs):
        ins, outs = refs[:n], refs[n:2 * n]
        send_sems, recv_sems, local_sems = refs[2 * n:]
        x, y, c, chips = _mesh_pos()
        me = _chip_id(x, y)
        sibling = (x, y, 1 - c)
        theirs = [_chip_id(*chip) for chip in chips]

        def rows(w, which):
            hr = shapes[w][1]
            return pl.ds(which * hr, hr)

        def remote(w, k, src, dst, to):
            return pltpu.make_async_remote_copy(
                src_ref=src, dst_ref=dst, send_sem=send_sems.at[w, k], recv_sem=recv_sems.at[w, k],
                device_id=to, device_id_type=MESH)

        local, first, passed = [], [], []
        for w in range(n):
            mine = outs[w].at[me, rows(w, c)]
            cp = pltpu.make_async_copy(ins[w].at[me], mine, local_sems.at[w])
            cp.start()
            local.append(cp)
            cp = remote(w, 6, ins[w].at[me], mine, sibling)
            cp.start()
            first.append(cp)
            for k, chip in enumerate(chips):
                cp = remote(w, k, ins[w].at[theirs[k]], mine, (*chip, c))
                cp.start()
                first.append(cp)
        for w in range(n):
            for k in range(3):
                landed = outs[w].at[theirs[k], rows(w, c)]
                remote(w, k, landed, landed, sibling).wait_recv()
                cp = remote(w, 3 + k, landed, landed, sibling)
                cp.start()
                passed.append(cp)
        for w in range(n):
            landed = outs[w].at[me, rows(w, 1 - c)]
            remote(w, 6, landed, landed, sibling).wait_recv()
            for k in range(3):
                landed = outs[w].at[theirs[k], rows(w, 1 - c)]
                remote(w, 3 + k, landed, landed, sibling).wait_recv()
        for cp in first + passed:
            cp.wait_send()
        for cp in local:
            cp.wait()

    return pl.pallas_call(
        body, name="scatter_chip_sums",
        in_specs=[_HBM] * n, out_specs=[_HBM] * n,
        out_shape=[jax.ShapeDtypeStruct((N_CHIPS, 2 * s[1], s[2]), BF16) for s in shapes],
        scratch_shapes=[pltpu.SemaphoreType.DMA((n, 7)), pltpu.SemaphoreType.DMA((n, 7)),
                        pltpu.SemaphoreType.DMA((n,))],
    )(*sums)


SMALL_ROWS = 8


def _allreduce_small(gl2g, gl2b, gl1g, gl1b, g_ac, gcw, gsink, loss):
    d = gl2g.shape[1]
    hd = d // 2
    nq = gsink.shape[1]

    def body(a_ref, b_ref, c_ref, d_ref, e_ref, cw_ref, sk_ref, ls_ref, out_ref, mine, gath, send_sems, recv_sems):
        x, y, c, _ = _mesh_pos()
        me = 4 * x + 2 * y + c
        mine[...] = jnp.zeros_like(mine)
        mine[0:1, :] = a_ref[...]
        mine[1:2, :] = b_ref[...]
        mine[2:3, :] = c_ref[...]
        mine[3:4, :] = d_ref[...]
        mine[4:5, :] = e_ref[...]
        mine[5:6, 0:hd] = cw_ref[0:1, :]
        mine[5:6, hd:d] = cw_ref[1:2, :]
        mine[6:7, 0:hd] = cw_ref[2:3, :]
        mine[6:7, hd:hd + nq] = sk_ref[...]
        mine[6:7, hd + 128:hd + 256] = ls_ref[...]
        gath[pl.ds(me, 1)] = mine[...][None]
        copies = []
        for r in range(1, 8):
            peer = ((1 - x) if r & 4 else x, (1 - y) if r & 2 else y, (1 - c) if r & 1 else c)
            cp = pltpu.make_async_remote_copy(
                src_ref=mine, dst_ref=gath.at[me], send_sem=send_sems.at[r - 1], recv_sem=recv_sems.at[r - 1],
                device_id=peer, device_id_type=MESH)
            cp.start()
            copies.append(cp)
        for r in range(1, 8):
            peer = ((1 - x) if r & 4 else x, (1 - y) if r & 2 else y, (1 - c) if r & 1 else c)
            peer_id = 4 * peer[0] + 2 * peer[1] + peer[2]
            pltpu.make_async_remote_copy(
                src_ref=mine, dst_ref=gath.at[peer_id], send_sem=send_sems.at[r - 1], recv_sem=recv_sems.at[r - 1],
                device_id=peer, device_id_type=MESH).wait_recv()
        for cp in copies:
            cp.wait_send()
        total = gath[0]
        for dev in range(1, 8):
            total = total + gath[dev]
        out_ref[...] = total

    return pl.pallas_call(
        body, name="allreduce_small",
        in_specs=[_VMEM] * 8, out_specs=_VMEM,
        out_shape=jax.ShapeDtypeStruct((SMALL_ROWS, d), F32),
        scratch_shapes=[pltpu.VMEM((SMALL_ROWS, d), F32), pltpu.VMEM((8, SMALL_ROWS, d), F32),
                        pltpu.SemaphoreType.DMA((7,)), pltpu.SemaphoreType.DMA((7,))],
    )(gl2g, gl2b, gl1g, gl1b, g_ac, gcw, gsink, loss)


def _adamw(w, g, m, v):
    m = ADAM_B1 * m + (1.0 - ADAM_B1) * g
    v = ADAM_B2 * v + (1.0 - ADAM_B2) * (g * g)
    m_hat = m / (1.0 - ADAM_B1 ** ADAM_STEP)
    v_hat = v / (1.0 - ADAM_B2 ** ADAM_STEP)
    delta = -ADAM_LR * (m_hat / (jnp.sqrt(v_hat) + ADAM_EPS) + ADAM_WD * w)
    return delta, m, v


def _adamw_shard(w, m, v, sums, name):
    _, r, c = w.shape
    tr = _row_tile(r, TR_ELT)

    def body(w_ref, m_ref, v_ref, s_ref, g_out, d_out, m_out, v_out):
        g = s_ref[0].astype(F32)
        for s in range(1, N_CHIPS):
            g = g + s_ref[s].astype(F32)
        delta, nm, nv = _adamw(w_ref[...], g, m_ref[...], v_ref[...])
        g_out[...] = g
        d_out[...] = delta
        m_out[...] = nm
        v_out[...] = nv

    blk = pl.BlockSpec((None, tr, c), lambda i: (0, i, 0))
    return pl.pallas_call(
        body, name=name, grid=(r // tr,),
        in_specs=[blk, blk, blk, pl.BlockSpec((N_CHIPS, tr, c), lambda i: (0, i, 0))],
        out_specs=[blk] * 4,
        out_shape=[jax.ShapeDtypeStruct((1, r, c), F32)] * 4,
        compiler_params=_params(("parallel",)),
    )(w, m, v, sums)


def _adamw_small(red, params):
    names = ["sinks", "g_attn", "g_conv", "ln1_g", "ln1_b", "ln2_g", "ln2_b", "conv_w"]
    d = red.shape[1]
    hd = d // 2
    flat = []
    for nme in names:
        flat.extend(params[nme])
    nq = params["sinks"][0].shape[1]
    cs = params["conv_w"][0].shape[2]

    def body(*refs):
        red_ref = refs[0]
        ins = refs[1:1 + 3 * len(names)]
        outs = refs[1 + 3 * len(names):]
        x, y, _, _ = _mesh_pos()
        me = _chip_id(x, y)

        def conv_tap(row, base):
            picked = red_ref[row:row + 1, base:base + cs]
            for s in range(1, N_CHIPS):
                picked = jnp.where(me == s, red_ref[row:row + 1, base + s * cs:base + (s + 1) * cs], picked)
            return picked

        grads = {
            "sinks": red_ref[6:7, hd:hd + nq],
            "g_attn": red_ref[4:5, 0:hd],
            "g_conv": red_ref[4:5, hd:d],
            "ln1_g": red_ref[2:3, :],
            "ln1_b": red_ref[3:4, :],
            "ln2_g": red_ref[0:1, :],
            "ln2_b": red_ref[1:2, :],
        }
        for i, nme in enumerate(names):
            w_ref, m_ref, v_ref = ins[3 * i:3 * i + 3]
            g_out, d_out, m_out, v_out = outs[4 * i:4 * i + 4]
            if nme == "conv_w":
                for tap, (row, base) in enumerate([(5, 0), (5, hd), (6, 0)]):
                    g = conv_tap(row, base)
                    delta, nm, nv = _adamw(w_ref[0, tap:tap + 1, :], g, m_ref[0, tap:tap + 1, :], v_ref[0, tap:tap + 1, :])
                    g_out[0, tap:tap + 1, :] = g
                    d_out[0, tap:tap + 1, :] = delta
                    m_out[0, tap:tap + 1, :] = nm
                    v_out[0, tap:tap + 1, :] = nv
            else:
                g = grads[nme]
                delta, nm, nv = _adamw(w_ref[...], g, m_ref[...], v_ref[...])
                g_out[...] = g
                d_out[...] = delta
                m_out[...] = nm
                v_out[...] = nv

    out_shape = []
    for nme in names:
        out_shape.extend([jax.ShapeDtypeStruct(params[nme][0].shape, F32)] * 4)
    outs = pl.pallas_call(
        body, name="adamw_small",
        in_specs=[_VMEM] * (1 + len(flat)), out_specs=[_VMEM] * len(out_shape),
        out_shape=out_shape,
    )(red, *flat)
    return {nme: tuple(outs[4 * i:4 * i + 4]) for i, nme in enumerate(names)}


def _rope_tables(pos_col):
    s = pos_col.shape[0]
    w = N_KV_HEADS * HEAD_DIM
    tb = min(512, s)
    inv_freq = (ROPE_THETA ** (-np.arange(0, ROT_DIM, 2, dtype=np.float32) / ROT_DIM)).astype(np.float32)

    def body(pos_ref, cos_ref, sin_ref):
        pos = pos_ref[...].astype(F32)
        lane = lax.broadcasted_iota(jnp.int32, (tb, w), 1) & (HEAD_DIM - 1)
        fidx = lane & (ROT_DIM // 2 - 1)
        inv = jnp.zeros((tb, w), F32)
        for k in range(ROT_DIM // 2):
            inv = jnp.where(fidx == k, float(inv_freq[k]), inv)
        ang = pos * inv
        rot = lane < ROT_DIM
        cos_ref[...] = jnp.where(rot, jnp.cos(ang), 1.0)
        sin_v = jnp.sin(ang)
        sin_ref[...] = jnp.where(lane < ROT_DIM // 2, -sin_v, jnp.where(rot, sin_v, 0.0))

    return pl.pallas_call(
        body, name="rope_tables", grid=(s // tb,),
        in_specs=[pl.BlockSpec((tb, 1), lambda i: (i, 0))],
        out_specs=[pl.BlockSpec((tb, w), lambda i: (i, 0))] * 2,
        out_shape=[jax.ShapeDtypeStruct((s, w), F32)] * 2,
        compiler_params=_params(("parallel",)),
    )(pos_col)


def _in_proj(x, w_in_g):
    _, s, d = x.shape
    ns, _, ncol = w_in_g.shape
    tm = min(TM, s)

    def body(x_ref, w_ref, o_ref):
        o_ref[...] = _dot(x_ref[...].astype(BF16), w_ref[...])

    return pl.pallas_call(
        body, name="in_proj", grid=(s // tm, ns),
        in_specs=[pl.BlockSpec((None, tm, d), lambda i, j: (0, i, 0)),
                  pl.BlockSpec((None, d, ncol), lambda i, j: (j, 0, 0))],
        out_specs=pl.BlockSpec((tm, ncol), lambda i, j: (i, j)),
        out_shape=jax.ShapeDtypeStruct((s, ns * ncol), F32),
        compiler_params=_params(("parallel", "arbitrary")),
    )(x, w_in_g)


def _attention_scores(q, k_prev, k_cur, sinks_ref, h, first, valid):
    heads = [q[:, (GROUP * h + g) * HEAD_DIM:(GROUP * h + g + 1) * HEAD_DIM] for g in range(GROUP)]
    q4 = jnp.concatenate(heads, axis=0).astype(BF16)
    kk = jnp.concatenate([k_prev[:, h * HEAD_DIM:(h + 1) * HEAD_DIM], k_cur[:, h * HEAD_DIM:(h + 1) * HEAD_DIM]],
                         axis=0).astype(BF16)
    s = _dot_nt(q4, kk) * ATTN_SCALE
    s = jnp.where(valid, s, NEG_BIG)
    sink = jnp.concatenate(
        [jnp.broadcast_to(sinks_ref[0:1, GROUP * h + g:GROUP * h + g + 1], (WINDOW, 1)) for g in range(GROUP)], axis=0)
    m = jnp.maximum(jnp.max(s, axis=1, keepdims=True), sink)
    p = jnp.exp(s - m)
    p_sink = jnp.exp(sink - m)
    inv_l = 1.0 / (jnp.sum(p, axis=1, keepdims=True) + p_sink)
    return q4, kk, p * inv_l, p_sink * inv_l


def _band_mask(first):
    rows = GROUP * WINDOW
    qi = lax.broadcasted_iota(jnp.int32, (rows, 2 * WINDOW), 0) & (WINDOW - 1)
    kj = lax.broadcasted_iota(jnp.int32, (rows, 2 * WINDOW), 1)
    rel = qi + WINDOW - kj
    band = (rel >= 0) & (rel < WINDOW)
    return band & jnp.logical_not(first & (kj < WINDOW))


def _attention_fwd(proj, cos_t, sin_t, sinks):
    s = proj.shape[0]
    qw = GROUP * N_KV_HEADS * HEAD_DIM
    kvw = N_KV_HEADS * HEAD_DIM
    nb = s // WINDOW

    def body(cur_ref, prev_ref, cos_ref, sin_ref, cosp_ref, sinp_ref, sinks_ref, o_ref):
        n = pl.program_id(0)
        first = n == 0
        cur = cur_ref[...]
        cos, sin = cos_ref[...], sin_ref[...]
        q = _rope(cur[:, :qw], _tile_lanes(cos, GROUP), _tile_lanes(sin, GROUP), 1.0)
        k_cur = _rope(cur[:, qw:qw + kvw], cos, sin, 1.0)
        v_cur = cur[:, qw + kvw:]
        prev = prev_ref[...]
        k_prev = _rope(prev[:, :kvw], cosp_ref[...], sinp_ref[...], 1.0)
        v_prev = prev[:, kvw:]
        valid = _band_mask(first)
        outs = []
        for h in range(N_KV_HEADS):
            _, _, probs, _ = _attention_scores(q, k_prev, k_cur, sinks_ref, h, first, valid)
            vv = jnp.concatenate([v_prev[:, h * HEAD_DIM:(h + 1) * HEAD_DIM], v_cur[:, h * HEAD_DIM:(h + 1) * HEAD_DIM]],
                                 axis=0).astype(BF16)
            o = _dot(probs.astype(BF16), vv)
            outs.extend([o[g * WINDOW:(g + 1) * WINDOW] for g in range(GROUP)])
        o_ref[...] = jnp.concatenate(outs, axis=1)

    tbl = pl.BlockSpec((WINDOW, kvw), lambda n: (n, 0))
    tbl_prev = pl.BlockSpec((WINDOW, kvw), lambda n: (jnp.maximum(n - 1, 0), 0))
    return pl.pallas_call(
        body, name="attention_fwd", grid=(nb,),
        in_specs=[pl.BlockSpec((WINDOW, qw + 2 * kvw), lambda n: (n, 0)),
                  pl.BlockSpec((WINDOW, 2 * kvw), lambda n: (jnp.maximum(n - 1, 0), (qw // (2 * kvw)))),
                  tbl, tbl, tbl_prev, tbl_prev, _VMEM],
        out_specs=pl.BlockSpec((WINDOW, qw), lambda n: (n, 0)),
        out_shape=jax.ShapeDtypeStruct((s, qw), F32),
        compiler_params=_params(("parallel",)),
    )(proj, proj, cos_t, sin_t, cos_t, sin_t, sinks)


def _conv_taps(cw_ref):
    return [jnp.concatenate([cw_ref[s, k:k + 1, :] for s in range(N_CHIPS)], axis=1) for k in range(3)]


def _shift_down(z, halo, steps):
    rows = z.shape[0]
    row = lax.broadcasted_iota(jnp.int32, z.shape, 0)
    out = pltpu.roll(z, steps, 0)
    for r in range(steps):
        out = jnp.where(row == r, halo[8 - steps + r:8 - steps + r + 1, :], out)
    return out


def _shift_up(z, halo, steps):
    rows = z.shape[0]
    row = lax.broadcasted_iota(jnp.int32, z.shape, 0)
    out = pltpu.roll(z, rows - steps, 0)
    for r in range(steps):
        out = jnp.where(row == rows - steps + r, halo[r:r + 1, :], out)
    return out


def _split_cbu(lo, hi, cw):
    c_gate = lo[:, :cw]
    b_gate = jnp.concatenate([lo[:, cw:], hi[:, :2 * cw - lo.shape[1]]], axis=1)
    u = hi[:, 2 * cw - lo.shape[1]:]
    return c_gate, b_gate, u


def _conv_norm(proj, attn, cw_full, g_ac):
    s, in_w = proj.shape
    cw = attn.shape[1]
    blk_w = in_w // 3
    tb = min(TB_CONV, s)

    def body(lo_ref, hi_ref, lo_h_ref, hi_h_ref, attn_ref, cw_ref, g_ref, mixed_ref, ac_ref, rstd_ref):
        i = pl.program_id(0)
        c_gate, b_gate, u = _split_cbu(lo_ref[...], hi_ref[...], cw)
        c_h, _, u_h = _split_cbu(lo_h_ref[...], hi_h_ref[...], cw)
        z = c_gate * u
        z_h = jnp.where(i == 0, 0.0, c_h * u_h)
        w0, w1, w2 = _conv_taps(cw_ref)
        y = w0 * _shift_down(z, z_h, 2) + w1 * _shift_down(z, z_h, 1) + w2 * z
        conv = b_gate * y
        a = attn_ref[...]
        r_a = lax.rsqrt(jnp.mean(a * a, axis=-1, keepdims=True) + RMS_EPS)
        r_c = lax.rsqrt(jnp.mean(conv * conv, axis=-1, keepdims=True) + RMS_EPS)
        g = g_ref[...]
        mixed_ref[...] = jnp.concatenate([a * r_a * g[:, :cw], conv * r_c * g[:, cw:]], axis=1).astype(BF16)
        ac_ref[...] = jnp.concatenate([a, conv], axis=1)
        rstd_ref[0] = r_a
        rstd_ref[1] = r_c

    halo_idx = lambda i: jnp.maximum(i * (tb // 8) - 1, 0)
    return pl.pallas_call(
        body, name="conv_norm", grid=(s // tb,),
        in_specs=[pl.BlockSpec((tb, blk_w), lambda i: (i, 1)),
                  pl.BlockSpec((tb, blk_w), lambda i: (i, 2)),
                  pl.BlockSpec((8, blk_w), lambda i: (halo_idx(i), 1)),
                  pl.BlockSpec((8, blk_w), lambda i: (halo_idx(i), 2)),
                  pl.BlockSpec((tb, cw), lambda i: (i, 0)),
                  _VMEM, _VMEM],
        out_specs=[pl.BlockSpec((tb, 2 * cw), lambda i: (i, 0)),
                   pl.BlockSpec((tb, 2 * cw), lambda i: (i, 0)),
                   pl.BlockSpec((2, tb, 1), lambda i: (0, i, 0))],
        out_shape=[jax.ShapeDtypeStruct((s, 2 * cw), BF16), jax.ShapeDtypeStruct((s, 2 * cw), F32),
                   jax.ShapeDtypeStruct((2, s, 1), F32)],
        compiler_params=_params(("parallel",)),
    )(proj, proj, proj, proj, attn, cw_full, g_ac)


def _out_proj_ln(mixed, w_out_g, x, ln_g, ln_b):
    s, d = mixed.shape
    tm = min(TM, s)
    tk = min(512, d)
    nk = d // tk

    def body(a_ref, w_ref, x_ref, g_ref, b_ref, xhat_ref, h_ref, rstd_ref, acc):
        k = pl.program_id(1)

        @pl.when(k == 0)
        def _():
            acc[...] = jnp.zeros_like(acc)

        acc[...] += _dot(a_ref[...], w_ref[...])

        @pl.when(k == nk - 1)
        def _():
            def rows_fn(rows):
                xhat, rstd = _ln_fwd(ALPHA * x_ref[rows, :] + acc[rows, :])
                xhat_ref[rows, :] = xhat
                h_ref[rows, :] = (xhat * g_ref[...] + b_ref[...]).astype(BF16)
                rstd_ref[rows, :] = rstd

            _for_row_chunks(tm, rows_fn)

    row = pl.BlockSpec((tm, d), lambda i, k: (i, 0))
    return pl.pallas_call(
        body, name="out_proj_ln", grid=(s // tm, nk),
        in_specs=[pl.BlockSpec((tm, tk), lambda i, k: (i, k)),
                  pl.BlockSpec((tk, d), lambda i, k: (k, 0)),
                  pl.BlockSpec((None, tm, d), lambda i, k: (0, i, 0)),
                  _VMEM, _VMEM],
        out_specs=[row, row, pl.BlockSpec((tm, 1), lambda i, k: (i, 0))],
        out_shape=[jax.ShapeDtypeStruct((s, d), F32), jax.ShapeDtypeStruct((s, d), BF16),
                   jax.ShapeDtypeStruct((s, 1), F32)],
        scratch_shapes=[pltpu.VMEM((tm, d), F32)],
        compiler_params=_params(("parallel", "arbitrary")),
    )(mixed, w_out_g, x, ln_g, ln_b)


def _gate_up(h1, w_gate_g, w_up_g):
    s, d = h1.shape
    ns, _, fs = w_gate_g.shape
    tm = min(TM, s)
    tk = min(1024, d)
    nk = d // tk

    def body(h_ref, wg_ref, wu_ref, act_ref, g_ref, u_ref, acc_g, acc_u):
        k = pl.program_id(2)

        @pl.when(k == 0)
        def _():
            acc_g[...] = jnp.zeros_like(acc_g)
            acc_u[...] = jnp.zeros_like(acc_u)

        h = h_ref[...]
        acc_g[...] += _dot(h, wg_ref[...])
        acc_u[...] += _dot(h, wu_ref[...])

        @pl.when(k == nk - 1)
        def _():
            def rows_fn(rows):
                g, u = acc_g[rows, :], acc_u[rows, :]
                act_ref[rows, :] = (g * _sigmoid(g) * u).astype(BF16)
                g_ref[rows, :] = g.astype(BF16)
                u_ref[rows, :] = u.astype(BF16)

            _for_row_chunks(tm, rows_fn)

    wspec = pl.BlockSpec((None, tk, fs), lambda i, j, k: (j, k, 0))
    ospec = pl.BlockSpec((tm, fs), lambda i, j, k: (i, j))
    return pl.pallas_call(
        body, name="gate_up", grid=(s // tm, ns, nk),
        in_specs=[pl.BlockSpec((tm, tk), lambda i, j, k: (i, k)), wspec, wspec],
        out_specs=[ospec] * 3,
        out_shape=[jax.ShapeDtypeStruct((s, ns * fs), BF16)] * 3,
        scratch_shapes=[pltpu.VMEM((tm, fs), F32)] * 2,
        compiler_params=_params(("parallel", "arbitrary", "arbitrary")),
    )(h1, w_gate_g, w_up_g)


def _down_ln_loss(act, w_down_g, xhat1, ln1_g, ln1_b, ln2_g, ln2_b, target):
    s, f = act.shape
    d = xhat1.shape[1]
    tm = min(TM, s)
    tk = 512
    nk = f // tk

    def body(a_ref, w_ref, xh_ref, g1_ref, b1_ref, g2_ref, b2_ref, t_ref, dpre_ref, loss_ref, gg_ref, gb_ref, acc):
        i, k = pl.program_id(0), pl.program_id(1)

        @pl.when(k == 0)
        def _():
            acc[...] = jnp.zeros_like(acc)

        acc[...] += _dot(a_ref[...], w_ref[...])

        @pl.when(k == nk - 1)
        def _():
            @pl.when(i == 0)
            def _():
                loss_ref[...] = jnp.zeros_like(loss_ref)
                gg_ref[...] = jnp.zeros_like(gg_ref)
                gb_ref[...] = jnp.zeros_like(gb_ref)

            def rows_fn(rows):
                h1 = xh_ref[rows, :] * g1_ref[...] + b1_ref[...]
                xhat, rstd = _ln_fwd(ALPHA * h1 + acc[rows, :])
                g2 = g2_ref[...]
                diff = xhat * g2 + b2_ref[...] - t_ref[rows, :]
                dy = diff * (1.0 / d)
                dpre_ref[rows, :] = _ln_bwd(dy, xhat, rstd, g2)
                sq = jnp.sum(jnp.sum(diff * diff, axis=1, keepdims=True), axis=0, keepdims=True)
                loss_ref[...] += jnp.broadcast_to(sq * (0.5 / d), (1, 128))
                gg_ref[...] += jnp.sum(dy * xhat, axis=0, keepdims=True)
                gb_ref[...] += jnp.sum(dy, axis=0, keepdims=True)

            _for_row_chunks(tm, rows_fn)

    row = pl.BlockSpec((tm, d), lambda i, k: (i, 0))
    vec = pl.BlockSpec((1, d), lambda i, k: (0, 0))
    return pl.pallas_call(
        body, name="down_ln_loss", grid=(s // tm, nk),
        in_specs=[pl.BlockSpec((tm, tk), lambda i, k: (i, k)),
                  pl.BlockSpec((tk, d), lambda i, k: (k, 0)),
                  row, _VMEM, _VMEM, _VMEM, _VMEM,
                  pl.BlockSpec((None, tm, d), lambda i, k: (0, i, 0))],
        out_specs=[row, pl.BlockSpec((1, 128), lambda i, k: (0, 0)), vec, vec],
        out_shape=[jax.ShapeDtypeStruct((s, d), F32), jax.ShapeDtypeStruct((1, 128), F32),
                   jax.ShapeDtypeStruct((1, d), F32), jax.ShapeDtypeStruct((1, d), F32)],
        scratch_shapes=[pltpu.VMEM((tm, d), F32)],
        compiler_params=_params(("arbitrary", "arbitrary")),
    )(act, w_down_g, xhat1, ln1_g, ln1_b, ln2_g, ln2_b, target)


def _dact_silu_bwd(dpre2, w_down_g, gate, up):
    s, d = dpre2.shape
    f = gate.shape[1]
    fs = f // N_CHIPS
    tm = min(TM, s)

    def body(dp_ref, w_ref, g_ref, u_ref, dg_ref, du_ref):
        d_act = _dot_nt(dp_ref[...].astype(BF16), w_ref[...])
        g = g_ref[...].astype(F32)
        u = u_ref[...].astype(F32)
        sg = _sigmoid(g)
        dg_ref[...] = (d_act * u * (sg * (1.0 + g * (1.0 - sg)))).astype(BF16)
        du_ref[...] = (d_act * (g * sg)).astype(BF16)

    blk = pl.BlockSpec((tm, fs), lambda i, j: (i, j))
    return pl.pallas_call(
        body, name="dact_silu_bwd", grid=(s // tm, N_CHIPS),
        in_specs=[pl.BlockSpec((tm, d), lambda i, j: (i, 0)),
                  pl.BlockSpec((fs, d), lambda i, j: (j, 0)), blk, blk],
        out_specs=[blk, blk],
        out_shape=[jax.ShapeDtypeStruct((s, f), BF16)] * 2,
        compiler_params=_params(("parallel", "arbitrary")),
    )(dpre2, w_down_g, gate, up)


def _grad_rows(a, b, name, row_blocks=1):
    s, m = a.shape
    n = b.shape[1]
    ms = m // N_CHIPS
    tmw = ms // row_blocks
    tk = min(TK_TOK, s)
    nk = s // tk

    def body(a_ref, b_ref, o_ref, acc):
        k = pl.program_id(2)

        @pl.when(k == 0)
        def _():
            acc[...] = jnp.zeros_like(acc)

        acc[...] += _dot_tn(a_ref[...].astype(BF16), b_ref[...].astype(BF16))

        @pl.when(k == nk - 1)
        def _():
            o_ref[...] = acc[...].astype(BF16)

    return pl.pallas_call(
        body, name=name, grid=(N_CHIPS, row_blocks, nk),
        in_specs=[pl.BlockSpec((tk, tmw), lambda j, r, k: (k, j * row_blocks + r)),
                  pl.BlockSpec((tk, n), lambda j, r, k: (k, 0))],
        out_specs=pl.BlockSpec((None, tmw, n), lambda j, r, k: (j, r, 0)),
        out_shape=jax.ShapeDtypeStruct((N_CHIPS, ms, n), BF16),
        scratch_shapes=[pltpu.VMEM((tmw, n), F32)],
        compiler_params=_params(("parallel", "parallel", "arbitrary")),
    )(a, b)


def _grad_cols(a, bs, name, a_3d=False, row_blocks=2):
    s, m = a.shape[-2:]
    n = bs[0].shape[1]
    ns = n // N_CHIPS
    nb = len(bs)
    tmw = m // row_blocks
    tk = min(TK_TOK, s)
    nk = s // tk

    def body(*refs):
        a_ref, b_refs, o_refs, accs = refs[0], refs[1:1 + nb], refs[1 + nb:1 + 2 * nb], refs[1 + 2 * nb:]
        k = pl.program_id(2)

        @pl.when(k == 0)
        def _():
            for acc in accs:
                acc[...] = jnp.zeros_like(acc)

        at = a_ref[...].astype(BF16)
        for b_ref, acc in zip(b_refs, accs):
            acc[...] += _dot_tn(at, b_ref[...].astype(BF16))

        @pl.when(k == nk - 1)
        def _():
            for o_ref, acc in zip(o_refs, accs):
                o_ref[...] = acc[...].astype(BF16)

    if a_3d:
        a_spec = pl.BlockSpec((None, tk, tmw), lambda j, r, k: (0, k, r))
    else:
        a_spec = pl.BlockSpec((tk, tmw), lambda j, r, k: (k, r))
    return pl.pallas_call(
        body, name=name, grid=(N_CHIPS, row_blocks, nk),
        in_specs=[a_spec] + [pl.BlockSpec((tk, ns), lambda j, r, k: (k, j))] * nb,
        out_specs=[pl.BlockSpec((None, tmw, ns), lambda j, r, k: (j, r, 0))] * nb,
        out_shape=[jax.ShapeDtypeStruct((N_CHIPS, m, ns), BF16)] * nb,
        scratch_shapes=[pltpu.VMEM((tmw, ns), F32)] * nb,
        compiler_params=_params(("parallel", "parallel", "arbitrary")),
    )(a, *bs)


def _dh1_ln_bwd(d_gate, d_up, w_gate_g, w_up_g, dpre2, xhat1, rstd1, ln1_g, tm_max=256):
    s, f = d_gate.shape
    d = dpre2.shape[1]
    fs = f // N_CHIPS
    tm = min(tm_max, s)

    def body(dg_ref, du_ref, wg_ref, wu_ref, dp2_ref, xh_ref, rs_ref, g_ref, dpre_ref, gg_ref, gb_ref, acc):
        i, j = pl.program_id(0), pl.program_id(1)

        @pl.when(j == 0)
        def _():
            acc[...] = jnp.zeros_like(acc)

        acc[...] += _dot_nt(dg_ref[...], wg_ref[...]) + _dot_nt(du_ref[...], wu_ref[...])

        @pl.when(j == N_CHIPS - 1)
        def _():
            @pl.when(i == 0)
            def _():
                gg_ref[...] = jnp.zeros_like(gg_ref)
                gb_ref[...] = jnp.zeros_like(gb_ref)

            def rows_fn(rows):
                dh = acc[rows, :] + ALPHA * dp2_ref[rows, :]
                xhat = xh_ref[rows, :]
                dpre_ref[rows, :] = _ln_bwd(dh, xhat, rs_ref[rows, :], g_ref[...])
                gg_ref[...] += jnp.sum(dh * xhat, axis=0, keepdims=True)
                gb_ref[...] += jnp.sum(dh, axis=0, keepdims=True)

            _for_row_chunks(tm, rows_fn)

    row = pl.BlockSpec((tm, d), lambda i, j: (i, 0))
    vec = pl.BlockSpec((1, d), lambda i, j: (0, 0))
    act_blk = pl.BlockSpec((tm, fs), lambda i, j: (i, j))
    w_blk = pl.BlockSpec((None, d, fs), lambda i, j: (j, 0, 0))
    return pl.pallas_call(
        body, name="dh1_ln_bwd", grid=(s // tm, N_CHIPS),
        in_specs=[act_blk, act_blk, w_blk, w_blk, row, row, pl.BlockSpec((tm, 1), lambda i, j: (i, 0)), _VMEM],
        out_specs=[row, vec, vec],
        out_shape=[jax.ShapeDtypeStruct((s, d), F32), jax.ShapeDtypeStruct((1, d), F32),
                   jax.ShapeDtypeStruct((1, d), F32)],
        scratch_shapes=[pltpu.VMEM((tm, d), F32)],
        compiler_params=_params(("arbitrary", "arbitrary")),
    )(d_gate, d_up, w_gate_g, w_up_g, dpre2, xhat1, rstd1, ln1_g)


def _dmixed_rms_bwd(dpre1, w_out_g, ac, rstd, g_ac):
    s, d = dpre1.shape
    hd = d // 2
    tm = min(TM, s)

    def body(dp_ref, w_ref, ac_ref, rs_ref, g_ref, dac_ref, gg_ref):
        i = pl.program_id(1)
        dm = _dot_nt(dp_ref[...].astype(BF16), w_ref[...])
        pre = ac_ref[...]
        r = rs_ref[...]
        gdm = dm * g_ref[...]
        dac_ref[...] = r * gdm - pre * (r * r * r) * jnp.mean(gdm * pre, axis=-1, keepdims=True)
        gg = jnp.sum(dm * pre * r, axis=0, keepdims=True)

        @pl.when(i == 0)
        def _():
            gg_ref[...] = gg

        @pl.when(i > 0)
        def _():
            gg_ref[...] += gg

    return pl.pallas_call(
        body, name="dmixed_rms_bwd", grid=(2, s // tm),
        in_specs=[pl.BlockSpec((tm, d), lambda h, i: (i, 0)),
                  pl.BlockSpec((hd, d), lambda h, i: (h, 0)),
                  pl.BlockSpec((tm, hd), lambda h, i: (i, h)),
                  pl.BlockSpec((None, tm, 1), lambda h, i: (h, i, 0)),
                  pl.BlockSpec((1, hd), lambda h, i: (0, h))],
        out_specs=[pl.BlockSpec((tm, hd), lambda h, i: (i, h)),
                   pl.BlockSpec((1, hd), lambda h, i: (0, h))],
        out_shape=[jax.ShapeDtypeStruct((s, d), F32), jax.ShapeDtypeStruct((1, d), F32)],
        compiler_params=_params(("arbitrary", "arbitrary")),
    )(dpre1, w_out_g, ac, rstd, g_ac)


def _attention_bwd(proj, d_ac, cos_t, sin_t, sinks):
    s = proj.shape[0]
    qw = GROUP * N_KV_HEADS * HEAD_DIM
    kvw = N_KV_HEADS * HEAD_DIM
    nb = s // WINDOW
    nq = GROUP * N_KV_HEADS

    def body(cur_ref, prev_ref, do_ref, cos_ref, sin_ref, cosp_ref, sinp_ref, sinks_ref,
             dq_ref, dcur_ref, dprev_ref, dsink_ref):
        n = pl.program_id(0)
        first = n == 0
        cur = cur_ref[...]
        cos, sin = cos_ref[...], sin_ref[...]
        cos_q, sin_q = _tile_lanes(cos, GROUP), _tile_lanes(sin, GROUP)
        q = _rope(cur[:, :qw], cos_q, sin_q, 1.0)
        k_cur = _rope(cur[:, qw:qw + kvw], cos, sin, 1.0)
        v_cur = cur[:, qw + kvw:]
        prev = prev_ref[...]
        k_prev = _rope(prev[:, :kvw], cosp_ref[...], sinp_ref[...], 1.0)
        v_prev = prev[:, kvw:]
        d_out = do_ref[...]
        valid = _band_mask(first)
        dq_parts, dk_parts, dv_parts, dsink_parts = [], [], [], []
        for h in range(N_KV_HEADS):
            q4, kk, probs, p_sink = _attention_scores(q, k_prev, k_cur, sinks_ref, h, first, valid)
            vv = jnp.concatenate([v_prev[:, h * HEAD_DIM:(h + 1) * HEAD_DIM], v_cur[:, h * HEAD_DIM:(h + 1) * HEAD_DIM]],
                                 axis=0).astype(BF16)
            do4 = jnp.concatenate(
                [d_out[:, (GROUP * h + g) * HEAD_DIM:(GROUP * h + g + 1) * HEAD_DIM] for g in range(GROUP)],
                axis=0).astype(BF16)
            d_probs = _dot_nt(do4, vv)
            delta = jnp.sum(probs * d_probs, axis=1, keepdims=True)
            d_s = (probs * (d_probs - delta) * ATTN_SCALE).astype(BF16)
            dq4 = _dot(d_s, kk)
            dq_parts.extend([dq4[g * WINDOW:(g + 1) * WINDOW] for g in range(GROUP)])
            dk_parts.append(_dot_tn(d_s, q4))
            dv_parts.append(_dot_tn(probs.astype(BF16), do4))
            ds_sink = -p_sink * delta
            dsink_parts.extend([jnp.sum(ds_sink[g * WINDOW:(g + 1) * WINDOW], axis=0, keepdims=True)
                                for g in range(GROUP)])
        dq_ref[...] = _rope(jnp.concatenate(dq_parts, axis=1), cos_q, sin_q, -1.0)
        dk = jnp.concatenate(dk_parts, axis=1)
        dv = jnp.concatenate(dv_parts, axis=1)
        dprev_ref[...] = jnp.concatenate([dk[:WINDOW], dv[:WINDOW]], axis=1)
        dcur_ref[...] = jnp.concatenate([dk[WINDOW:], dv[WINDOW:]], axis=1)
        dsink = jnp.concatenate(dsink_parts, axis=1)

        @pl.when(first)
        def _():
            dsink_ref[...] = dsink

        @pl.when(n > 0)
        def _():
            dsink_ref[...] += dsink

    tbl = pl.BlockSpec((WINDOW, kvw), lambda n: (n, 0))
    tbl_prev = pl.BlockSpec((WINDOW, kvw), lambda n: (jnp.maximum(n - 1, 0), 0))
    kv_blk = pl.BlockSpec((WINDOW, 2 * kvw), lambda n: (n, 0))
    return pl.pallas_call(
        body, name="attention_bwd", grid=(nb,),
        in_specs=[pl.BlockSpec((WINDOW, qw + 2 * kvw), lambda n: (n, 0)),
                  pl.BlockSpec((WINDOW, 2 * kvw), lambda n: (jnp.maximum(n - 1, 0), (qw // (2 * kvw)))),
                  pl.BlockSpec((WINDOW, qw), lambda n: (n, 0)),
                  tbl, tbl, tbl_prev, tbl_prev, _VMEM],
        out_specs=[pl.BlockSpec((WINDOW, qw), lambda n: (n, 0)), kv_blk, kv_blk,
                   pl.BlockSpec((1, nq), lambda n: (0, 0))],
        out_shape=[jax.ShapeDtypeStruct((s, qw), F32), jax.ShapeDtypeStruct((s, 2 * kvw), F32),
                   jax.ShapeDtypeStruct((s, 2 * kvw), F32), jax.ShapeDtypeStruct((1, nq), F32)],
        compiler_params=_params(("arbitrary",)),
    )(proj, proj, d_ac, cos_t, sin_t, cos_t, sin_t, sinks)


def _dproj_assemble(proj, d_ac, dq, dkv_cur, dkv_prev, cos_t, sin_t, cw_full):
    s, in_w = proj.shape
    cw = dq.shape[1]
    kvw = N_KV_HEADS * HEAD_DIM
    blk_w = in_w // 3
    tb = WINDOW
    nb = s // tb

    def body(lo_ref, hi_ref, lo_p_ref, hi_p_ref, lo_n_ref, hi_n_ref, dconv_ref, dconv_n_ref,
             dq_ref, dcur_ref, dprev_n_ref, cos_ref, sin_ref, cw_ref, dproj_ref, gcw_ref):
        i = pl.program_id(0)
        last = i == nb - 1
        c_gate, b_gate, u = _split_cbu(lo_ref[...], hi_ref[...], cw)
        c_p, _, u_p = _split_cbu(lo_p_ref[...], hi_p_ref[...], cw)
        _, b_n, _ = _split_cbu(lo_n_ref[...], hi_n_ref[...], cw)
        z = c_gate * u
        z_p = jnp.where(i == 0, 0.0, c_p * u_p)
        z1 = _shift_down(z, z_p, 1)
        z2 = _shift_down(z, z_p, 2)
        w0, w1, w2 = _conv_taps(cw_ref)
        y = w0 * z2 + w1 * z1 + w2 * z
        d_conv = dconv_ref[...]
        d_b = d_conv * y
        d_y = d_conv * b_gate
        d_y_n = jnp.where(last, 0.0, dconv_n_ref[...] * b_n)
        d_z = w2 * d_y + w1 * _shift_up(d_y, d_y_n, 1) + w0 * _shift_up(d_y, d_y_n, 2)
        d_c = d_z * u
        d_u = d_z * c_gate
        gcw = jnp.concatenate([jnp.sum(d_y * z2, axis=0, keepdims=True), jnp.sum(d_y * z1, axis=0, keepdims=True),
                               jnp.sum(d_y * z, axis=0, keepdims=True)], axis=0)

        @pl.when(i == 0)
        def _():
            gcw_ref[...] = gcw

        @pl.when(i > 0)
        def _():
            gcw_ref[...] += gcw

        dkv = dcur_ref[...] + jnp.where(last, 0.0, dprev_n_ref[...])
        dk = _rope(dkv[:, :kvw], cos_ref[...], sin_ref[...], -1.0)
        dproj_ref[...] = jnp.concatenate([dq_ref[...], dk, dkv[:, kvw:], d_c, d_b, d_u], axis=1).astype(BF16)

    prev8 = lambda i: jnp.maximum(i * (tb // 8) - 1, 0)
    next8 = lambda i: jnp.minimum((i + 1) * (tb // 8), s // 8 - 1)
    nxt = lambda i: jnp.minimum(i + 1, nb - 1)
    return pl.pallas_call(
        body, name="dproj_assemble", grid=(nb,),
        in_specs=[pl.BlockSpec((tb, blk_w), lambda i: (i, 1)),
                  pl.BlockSpec((tb, blk_w), lambda i: (i, 2)),
                  pl.BlockSpec((8, blk_w), lambda i: (prev8(i), 1)),
                  pl.BlockSpec((8, blk_w), lambda i: (prev8(i), 2)),
                  pl.BlockSpec((8, blk_w), lambda i: (next8(i), 1)),
                  pl.BlockSpec((8, blk_w), lambda i: (next8(i), 2)),
                  pl.BlockSpec((tb, cw), lambda i: (i, 1)),
                  pl.BlockSpec((8, cw), lambda i: (next8(i), 1)),
                  pl.BlockSpec((tb, cw), lambda i: (i, 0)),
                  pl.BlockSpec((tb, 2 * kvw), lambda i: (i, 0)),
                  pl.BlockSpec((tb, 2 * kvw), lambda i: (nxt(i), 0)),
                  pl.BlockSpec((tb, kvw), lambda i: (i, 0)),
                  pl.BlockSpec((tb, kvw), lambda i: (i, 0)),
                  _VMEM],
        out_specs=[pl.BlockSpec((tb, in_w), lambda i: (i, 0)),
                   pl.BlockSpec((3, cw), lambda i: (0, 0))],
        out_shape=[jax.ShapeDtypeStruct((s, in_w), BF16), jax.ShapeDtypeStruct((3, cw), F32)],
        compiler_params=_params(("arbitrary",)),
    )(proj, proj, proj, proj, proj, proj, d_ac, d_ac, dq, dkv_cur, dkv_prev, cos_t, sin_t, cw_full)


def _dx(d_proj, w_in_g, dpre1):
    s, in_w = d_proj.shape
    ns, d, ncol = w_in_g.shape
    tm = min(TM, s)

    def body(dp_ref, w_ref, r_ref, o_ref, acc):
        j = pl.program_id(1)

        @pl.when(j == 0)
        def _():
            acc[...] = jnp.zeros_like(acc)

        acc[...] += _dot_nt(dp_ref[...], w_ref[...])

        @pl.when(j == ns - 1)
        def _():
            o_ref[...] = acc[...] + ALPHA * r_ref[...]

    return pl.pallas_call(
        body, name="dx", grid=(s // tm, ns),
        in_specs=[pl.BlockSpec((tm, ncol), lambda i, j: (i, j)),
                  pl.BlockSpec((None, d, ncol), lambda i, j: (j, 0, 0)),
                  pl.BlockSpec((tm, d), lambda i, j: (i, 0))],
        out_specs=pl.BlockSpec((None, tm, d), lambda i, j: (0, i, 0)),
        out_shape=jax.ShapeDtypeStruct((1, s, d), F32),
        scratch_shapes=[pltpu.VMEM((tm, d), F32)],
        compiler_params=_params(("parallel", "arbitrary")),
    )(d_proj, w_in_g, dpre1)


def kernel(x, positions, w_in, conv_w, sinks, g_attn, g_conv, w_out, ln1_g, ln1_b, w_gate, w_up, w_down, ln2_g, ln2_b, loss_target, m_w_in, m_conv_w, m_sinks, m_g_attn, m_g_conv, m_w_out, m_ln1_g, m_ln1_b, m_w_gate, m_w_up, m_w_down, m_ln2_g, m_ln2_b, v_w_in, v_conv_w, v_sinks, v_g_attn, v_g_conv, v_w_out, v_ln1_g, v_ln1_b, v_w_gate, v_w_up, v_w_down, v_ln2_g, v_ln2_b):
    s = x.shape[1]
    d = x.shape[2]

    own = [_cast_weight(w_in, "cast_w_in"), _cast_weight(w_out, "cast_w_out"), _cast_weight(w_gate, "cast_w_gate"),
           _cast_weight(w_up, "cast_w_up"), _cast_weight(w_down, "cast_w_down")]
    w_in_g, w_out_g, w_gate_g, w_up_g, w_down_g = _allgather_weights(own)
    w_out_full = w_out_g.reshape(d, d)
    w_down_full = w_down_g.reshape(-1, d)
    cw_full = _allgather_conv_w(conv_w)
    g_ac = jnp.concatenate([g_attn, g_conv], axis=1)

    cos_t, sin_t = _rope_tables(positions.reshape(s, 1))
    proj = _in_proj(x, w_in_g)
    attn = _attention_fwd(proj, cos_t, sin_t, sinks)
    mixed, ac, rstd_ac = _conv_norm(proj, attn, cw_full, g_ac)
    xhat1, h1, rstd1 = _out_proj_ln(mixed, w_out_full, x, ln1_g, ln1_b)
    act, gate, up = _gate_up(h1, w_gate_g, w_up_g)
    dpre2, loss_part, g_ln2_g, g_ln2_b = _down_ln_loss(act, w_down_full, xhat1, ln1_g, ln1_b, ln2_g, ln2_b, loss_target)

    d_gate, d_up = _dact_silu_bwd(dpre2, w_down_full, gate, up)
    p_down = _grad_rows(act, dpre2, "grad_w_down")
    p_gate, p_up = _grad_cols(h1, [d_gate, d_up], "grad_w_gate_up")
    dpre1, g_ln1_g, g_ln1_b = _dh1_ln_bwd(d_gate, d_up, w_gate_g, w_up_g, dpre2, xhat1, rstd1, ln1_g)
    d_ac, g_g_ac = _dmixed_rms_bwd(dpre1, w_out_full, ac, rstd_ac, g_ac)
    p_out = _grad_rows(mixed, dpre1, "grad_w_out")
    dq, dkv_cur, dkv_prev, g_sinks = _attention_bwd(proj, d_ac, cos_t, sin_t, sinks)
    d_proj, g_conv_w = _dproj_assemble(proj, d_ac, dq, dkv_cur, dkv_prev, cos_t, sin_t, cw_full)
    (p_in,) = _grad_cols(x, [d_proj], "grad_w_in", a_3d=True)
    grad_x = _dx(d_proj, w_in_g, dpre1)

    parts = [p_in, p_out, p_gate, p_up, p_down]
    got = _exchange_halves(parts)
    cvec = lax.axis_index("c").astype(jnp.int32).reshape(1)
    names = ["w_in", "w_out", "w_gate", "w_up", "w_down"]
    chip_sums = [_add_halves(p, g, cvec, "add_halves_" + nme) for p, g, nme in zip(parts, got, names)]
    sums = _scatter_chip_sums(chip_sums)
    red = _allreduce_small(g_ln2_g, g_ln2_b, g_ln1_g, g_ln1_b, g_g_ac, g_conv_w, g_sinks, loss_part)

    big = {}
    for nme, w, m, v, sm in zip(names, [w_in, w_out, w_gate, w_up, w_down],
                                [m_w_in, m_w_out, m_w_gate, m_w_up, m_w_down],
                                [v_w_in, v_w_out, v_w_gate, v_w_up, v_w_down], sums):
        big[nme] = _adamw_shard(w, m, v, sm, "adamw_" + nme)
    small = _adamw_small(red, {
        "sinks": (sinks, m_sinks, v_sinks), "g_attn": (g_attn, m_g_attn, v_g_attn),
        "g_conv": (g_conv, m_g_conv, v_g_conv), "ln1_g": (ln1_g, m_ln1_g, v_ln1_g),
        "ln1_b": (ln1_b, m_ln1_b, v_ln1_b), "ln2_g": (ln2_g, m_ln2_g, v_ln2_g),
        "ln2_b": (ln2_b, m_ln2_b, v_ln2_b), "conv_w": (conv_w, m_conv_w, v_conv_w)})
    res = {**big, **small}
    order = ["w_in", "conv_w", "sinks", "g_attn", "g_conv", "w_out", "ln1_g", "ln1_b", "w_gate", "w_up", "w_down",
             "ln2_g", "ln2_b"]
    loss = red[6, d // 2 + 128]
    return (loss, grad_x, *[res[n][0] for n in order], *[res[n][1] for n in order],
            *[res[n][2] for n in order], *[res[n][3] for n in order])
```

```python
import functools

import numpy as np
import jax
import jax.numpy as jnp
from jax import lax
from jax.experimental import pallas as pl
from jax.experimental.pallas import tpu as pltpu

F32 = jnp.float32
BF16 = jnp.bfloat16
MESH = pl.DeviceIdType.MESH

HEAD_DIM = 64
N_KV_HEADS = 4
GROUP = 4
WINDOW = 128
ROT_DIM = 16
ROPE_THETA = 500000.0
ATTN_SCALE = HEAD_DIM ** -0.5
ALPHA = 2.0 ** 0.25
LN_EPS = 1e-5
RMS_EPS = 1e-6
ADAM_LR = 0.001
ADAM_B1 = 0.9
ADAM_B2 = 0.999
ADAM_EPS = 1e-08
ADAM_WD = 0.01
ADAM_STEP = 10
N_CHIPS = 4
NEG_BIG = -1e30

V7X_VMEM_BYTES = 64 * 1024 * 1024
VMEM_LIMIT = V7X_VMEM_BYTES - 6 * 1024 * 1024

TM = 512
TK_TOK = 512
TB_CONV = 256
TR_ELT = 256
ROW_CHUNK = 128


def _params(sem):
    return pltpu.CompilerParams(dimension_semantics=sem, vmem_limit_bytes=VMEM_LIMIT)


def _row_tile(rows, target):
    best = None
    for t in range(16, min(rows, target) + 1, 16):
        if rows % t == 0:
            best = t
    assert best is not None, (rows, target)
    return best


def _dot(a, b):
    return jnp.dot(a, b, preferred_element_type=F32)


def _dot_nt(a, b):
    return lax.dot_general(a, b, (((1,), (1,)), ((), ())), preferred_element_type=F32)


def _dot_tn(a, b):
    return lax.dot_general(a, b, (((0,), (0,)), ((), ())), preferred_element_type=F32)


def _mesh_pos():
    x, y, c = lax.axis_index("x"), lax.axis_index("y"), lax.axis_index("c")
    chips = [(1 - x, y), (x, 1 - y), (1 - x, 1 - y)]
    return x, y, c, chips


def _chip_id(px, py):
    return 2 * px + py


def _rope(t, cos, sgn_sin, sign):
    w = t.shape[1]
    lane = lax.broadcasted_iota(jnp.int32, t.shape, 1) & (HEAD_DIM - 1)
    partner = jnp.where(lane < ROT_DIM // 2, pltpu.roll(t, w - ROT_DIM // 2, 1), pltpu.roll(t, ROT_DIM // 2, 1))
    return t * cos + sign * (partner * sgn_sin)


def _tile_lanes(t, n):
    return jnp.concatenate([t] * n, axis=1)


def _sigmoid(g):
    return 1.0 / (1.0 + jnp.exp(-g))


def _for_row_chunks(n_rows, fn):
    def step(r, carry):
        fn(pl.ds(pl.multiple_of(r * ROW_CHUNK, ROW_CHUNK), ROW_CHUNK))
        return carry

    lax.fori_loop(0, n_rows // ROW_CHUNK, step, 0)


def _ln_fwd(pre):
    mu = jnp.mean(pre, axis=-1, keepdims=True)
    cen = pre - mu
    var = jnp.mean(cen * cen, axis=-1, keepdims=True)
    rstd = lax.rsqrt(var + LN_EPS)
    return cen * rstd, rstd


def _ln_bwd(dy, xhat, rstd, g):
    dxhat = dy * g
    m1 = jnp.mean(dxhat, axis=-1, keepdims=True)
    m2 = jnp.mean(dxhat * xhat, axis=-1, keepdims=True)
    return rstd * (dxhat - m1 - xhat * m2)


def _cast_weight(w, chip_vec, name):
    _, r, c = w.shape
    tr = _row_tile(r, TR_ELT)

    def body(chip_ref, w_ref, o_ref):
        o_ref[...] = w_ref[...].astype(BF16)

    grid_spec = pltpu.PrefetchScalarGridSpec(
        num_scalar_prefetch=1, grid=(r // tr,),
        in_specs=[pl.BlockSpec((None, tr, c), lambda i, chip_ref: (0, i, 0))],
        out_specs=pl.BlockSpec((None, tr, c), lambda i, chip_ref: (chip_ref[0], i, 0)))
    return pl.pallas_call(
        body, name=name, grid_spec=grid_spec,
        out_shape=jax.ShapeDtypeStruct((N_CHIPS, r, c), BF16),
        compiler_params=_params(("parallel",)),
    )(chip_vec, w)


_HBM = pl.BlockSpec(memory_space=pltpu.HBM)
_VMEM = pl.BlockSpec(memory_space=pltpu.VMEM)


_SEM = pl.BlockSpec(memory_space=pltpu.SEMAPHORE)
_ANY = pl.BlockSpec(memory_space=pl.ANY)
_EFFECT = pltpu.SideEffectType.DATAFLOW_SIDE_EFFECTING


def _chip_copy(buf, k, chip_of_src, half_rows, send_sems, recv_sems, to):
    part = buf.at[chip_of_src, half_rows]
    return pltpu.make_async_remote_copy(
        src_ref=part, dst_ref=part, send_sem=send_sems.at[k], recv_sem=recv_sems.at[k], device_id=to, device_id_type=MESH)


def _half_rows(buf, which):
    hr = buf.shape[1] // 2
    return pl.ds(which * hr, hr)


def _gather_start(bufs):
    n = len(bufs)

    def body(*refs):
        ins = refs[:n]
        sends, recvs = refs[n:2 * n], refs[2 * n:3 * n]
        token = refs[4 * n]
        x, y, c, chips = _mesh_pos()
        me = _chip_id(x, y)
        for w in range(n):
            for k, chip in enumerate(chips):
                _chip_copy(ins[w], k, me, _half_rows(ins[w], c), sends[w], recvs[w], (*chip, c)).start()
        token[...] = jnp.zeros_like(token)

    outs = pl.pallas_call(
        body, name="gather_start",
        in_specs=[_HBM] * n,
        out_specs=[_SEM] * (2 * n) + [_HBM] * n + [_VMEM],
        out_shape=[pltpu.SemaphoreType.DMA((3,))] * (2 * n) + [pltpu.HBM(b.shape, b.dtype) for b in bufs]
        + [jax.ShapeDtypeStruct((8, 128), F32)],
        input_output_aliases={w: 2 * n + w for w in range(n)},
        compiler_params=pltpu.CompilerParams(has_side_effects=_EFFECT),
    )(*[pltpu.with_memory_space_constraint(b, pltpu.HBM) for b in bufs])
    return [(outs[w], outs[n + w], outs[2 * n + w]) for w in range(n)], outs[3 * n]


def _gather_wait(send_sems, recv_sems, buf, after, name):
    def body(buf_ref, send_ref, recv_ref, after_ref, out_ref):
        x, y, c, chips = _mesh_pos()
        me = _chip_id(x, y)
        for k, chip in enumerate(chips):
            _chip_copy(buf_ref, k, me, _half_rows(buf_ref, c), send_ref, recv_ref, (*chip, c)).wait_send()
        for k, chip in enumerate(chips):
            _chip_copy(buf_ref, k, _chip_id(*chip), _half_rows(buf_ref, c), send_ref, recv_ref, (*chip, c)).wait_recv()

    return pl.pallas_call(
        body, name=name,
        in_specs=[_HBM, _SEM, _SEM, _ANY], out_specs=_HBM,
        out_shape=pltpu.HBM(buf.shape, buf.dtype),
        input_output_aliases={0: 0},
        compiler_params=pltpu.CompilerParams(has_side_effects=_EFFECT),
    )(buf, send_sems, recv_sems, after)


def _sibling_fill(buf, name, own_too=False):
    n_copies = 4 if own_too else 3

    def body(buf_ref, out_ref, send_sems, recv_sems):
        x, y, c, chips = _mesh_pos()
        sibling = (x, y, 1 - c)
        slots = [_chip_id(*chip) for chip in chips] + ([_chip_id(x, y)] if own_too else [])
        copies = []
        for k, slot in enumerate(slots):
            cp = _chip_copy(out_ref, k, slot, _half_rows(out_ref, c), send_sems, recv_sems, sibling)
            cp.start()
            copies.append(cp)
        for k, slot in enumerate(slots):
            _chip_copy(out_ref, k, slot, _half_rows(out_ref, 1 - c), send_sems, recv_sems, sibling).wait_recv()
        for cp in copies:
            cp.wait_send()

    return pl.pallas_call(
        body, name=name,
        in_specs=[_HBM], out_specs=_HBM,
        out_shape=jax.ShapeDtypeStruct(buf.shape, buf.dtype),
        input_output_aliases={0: 0},
        scratch_shapes=[pltpu.SemaphoreType.DMA((n_copies,)), pltpu.SemaphoreType.DMA((n_copies,))],
    )(buf)


def _allgather_conv_w(cw):
    _, kw, cs = cw.shape

    def body(cw_ref, out_ref, send_sems, recv_sems):
        x, y, c, chips = _mesh_pos()
        me = _chip_id(x, y)
        out_ref[pl.ds(me, 1)] = cw_ref[...]
        copies = []
        for k, chip in enumerate(chips):
            cp = pltpu.make_async_remote_copy(
                src_ref=cw_ref.at[0], dst_ref=out_ref.at[me], send_sem=send_sems.at[k], recv_sem=recv_sems.at[k],
                device_id=(*chip, c), device_id_type=MESH)
            cp.start()
            copies.append(cp)
        for k, chip in enumerate(chips):
            pltpu.make_async_remote_copy(
                src_ref=cw_ref.at[0], dst_ref=out_ref.at[_chip_id(*chip)], send_sem=send_sems.at[k],
                recv_sem=recv_sems.at[k], device_id=(*chip, c), device_id_type=MESH).wait_recv()
        for cp in copies:
            cp.wait_send()

    return pl.pallas_call(
        body, name="allgather_conv_w",
        in_specs=[_VMEM], out_specs=_VMEM,
        out_shape=jax.ShapeDtypeStruct((N_CHIPS, kw, cs), F32),
        scratch_shapes=[pltpu.SemaphoreType.DMA((3,)), pltpu.SemaphoreType.DMA((3,))],
    )(cw)


def _exchange_halves(parts):
    n = len(parts)
    shapes = [p.shape for p in parts]

    def body(*refs):
        ins, outs = refs[:n], refs[n:2 * n]
        send_sems, recv_sems = refs[2 * n:]
        x, y, c, _ = _mesh_pos()
        copies = []
        for w in range(n):
            hr = shapes[w][1] // 2
            cp = pltpu.make_async_remote_copy(
                src_ref=ins[w].at[:, pl.ds((1 - c) * hr, hr)], dst_ref=outs[w],
                send_sem=send_sems.at[w], recv_sem=recv_sems.at[w],
                device_id=(x, y, 1 - c), device_id_type=MESH)
            cp.start()
            copies.append(cp)
        for cp in copies:
            cp.wait()

    return pl.pallas_call(
        body, name="exchange_halves",
        in_specs=[_HBM] * n, out_specs=[_HBM] * n,
        out_shape=[jax.ShapeDtypeStruct((s[0], s[1] // 2, s[2]), BF16) for s in shapes],
        scratch_shapes=[pltpu.SemaphoreType.DMA((n,)), pltpu.SemaphoreType.DMA((n,))],
    )(*parts)


def _add_halves(part, got, cvec, name):
    ns, r, cdim = part.shape
    hr = r // 2
    tr = _row_tile(hr, TR_ELT)
    nblk = hr // tr

    def body(c_ref, a_ref, b_ref, o_ref):
        o_ref[...] = (a_ref[...].astype(F32) + b_ref[...].astype(F32)).astype(BF16)

    grid_spec = pltpu.PrefetchScalarGridSpec(
        num_scalar_prefetch=1, grid=(ns, nblk),
        in_specs=[pl.BlockSpec((None, tr, cdim), lambda s, i, c_ref: (s, c_ref[0] * nblk + i, 0)),
                  pl.BlockSpec((None, tr, cdim), lambda s, i, c_ref: (s, i, 0))],
        out_specs=pl.BlockSpec((None, tr, cdim), lambda s, i, c_ref: (s, i, 0)))
    return pl.pallas_call(
        body, name=name, grid_spec=grid_spec,
        out_shape=jax.ShapeDtypeStruct((ns, hr, cdim), BF16),
        compiler_params=_params(("parallel", "parallel")),
    )(cvec, part, got)


def _scatter_chip_sums(sums):
    n = len(sums)
    shapes = [s.shape for s in sums]

    def body(*refs):
        ins, outs = refs[:n], refs[n:2 * n]
        send_sems, recv_sems, local_sems = refs[2 * n:]
        x, y, c, chips = _mesh_pos()
        me = _chip_id(x, y)
        sibling = (x, y, 1 - c)
        theirs = [_chip_id(*chip) for chip in chips]

        def rows(w, which):
            hr = shapes[w][1]
            return pl.ds(which * hr, hr)

        def remote(w, k, src, dst, to):
            return pltpu.make_async_remote_copy(
                src_ref=src, dst_ref=dst, send_sem=send_sems.at[w, k], recv_sem=recv_sems.at[w, k],
                device_id=to, device_id_type=MESH)

        local, first, passed = [], [], []
        for w in range(n):
            mine = outs[w].at[me, rows(w, c)]
            cp = pltpu.make_async_copy(ins[w].at[me], mine, local_sems.at[w])
            cp.start()
            local.append(cp)
            cp = remote(w, 6, ins[w].at[me], mine, sibling)
            cp.start()
            first.append(cp)
            for k, chip in enumerate(chips):
                cp = remote(w, k, ins[w].at[theirs[k]], mine, (*chip, c))
                cp.start()
                first.append(cp)
        for w in range(n):
            for k in range(3):
                landed = outs[w].at[theirs[k], rows(w, c)]
                remote(w, k, landed, landed, sibling).wait_recv()
                cp = remote(w, 3 + k, landed, landed, sibling)
                cp.start()
                passed.append(cp)
        for w in range(n):
            landed = outs[w].at[me, rows(w, 1 - c)]
            remote(w, 6, landed, landed, sibling).wait_recv()
            for k in range(3):
                landed = outs[w].at[theirs[k], rows(w, 1 - c)]
                remote(w, 3 + k, landed, landed, sibling).wait_recv()
        for cp in first + passed:
            cp.wait_send()
        for cp in local:
            cp.wait()

    return pl.pallas_call(
        body, name="scatter_chip_sums",
        in_specs=[_HBM] * n, out_specs=[_HBM] * n,
        out_shape=[jax.ShapeDtypeStruct((N_CHIPS, 2 * s[1], s[2]), BF16) for s in shapes],
        scratch_shapes=[pltpu.SemaphoreType.DMA((n, 7)), pltpu.SemaphoreType.DMA((n, 7)),
                        pltpu.SemaphoreType.DMA((n,))],
    )(*sums)


SMALL_ROWS = 8


def _allreduce_small(gl2g, gl2b, gl1g, gl1b, g_ac, gcw, gsink, loss):
    d = gl2g.shape[1]
    hd = d // 2
    nq = gsink.shape[1]

    def body(a_ref, b_ref, c_ref, d_ref, e_ref, cw_ref, sk_ref, ls_ref, out_ref, mine, gath, send_sems, recv_sems):
        x, y, c, _ = _mesh_pos()
        me = 4 * x + 2 * y + c
        mine[...] = jnp.zeros_like(mine)
        mine[0:1, :] = a_ref[...]
        mine[1:2, :] = b_ref[...]
        mine[2:3, :] = c_ref[...]
        mine[3:4, :] = d_ref[...]
        mine[4:5, :] = e_ref[...]
        mine[5:6, 0:hd] = cw_ref[0:1, :]
        mine[5:6, hd:d] = cw_ref[1:2, :]
        mine[6:7, 0:hd] = cw_ref[2:3, :]
        mine[6:7, hd:hd + nq] = sk_ref[...]
        mine[6:7, hd + 128:hd + 256] = ls_ref[...]
        gath[pl.ds(me, 1)] = mine[...][None]
        copies = []
        for r in range(1, 8):
            peer = ((1 - x) if r & 4 else x, (1 - y) if r & 2 else y, (1 - c) if r & 1 else c)
            cp = pltpu.make_async_remote_copy(
                src_ref=mine, dst_ref=gath.at[me], send_sem=send_sems.at[r - 1], recv_sem=recv_sems.at[r - 1],
                device_id=peer, device_id_type=MESH)
            cp.start()
            copies.append(cp)
        for r in range(1, 8):
            peer = ((1 - x) if r & 4 else x, (1 - y) if r & 2 else y, (1 - c) if r & 1 else c)
            peer_id = 4 * peer[0] + 2 * peer[1] + peer[2]
            pltpu.make_async_remote_copy(
                src_ref=mine, dst_ref=gath.at[peer_id], send_sem=send_sems.at[r - 1], recv_sem=recv_sems.at[r - 1],
                device_id=peer, device_id_type=MESH).wait_recv()
        for cp in copies:
            cp.wait_send()
        total = gath[0]
        for dev in range(1, 8):
            total = total + gath[dev]
        out_ref[...] = total

    return pl.pallas_call(
        body, name="allreduce_small",
        in_specs=[_VMEM] * 8, out_specs=_VMEM,
        out_shape=jax.ShapeDtypeStruct((SMALL_ROWS, d), F32),
        scratch_shapes=[pltpu.VMEM((SMALL_ROWS, d), F32), pltpu.VMEM((8, SMALL_ROWS, d), F32),
                        pltpu.SemaphoreType.DMA((7,)), pltpu.SemaphoreType.DMA((7,))],
    )(gl2g, gl2b, gl1g, gl1b, g_ac, gcw, gsink, loss)


def _adamw(w, g, m, v):
    m = ADAM_B1 * m + (1.0 - ADAM_B1) * g
    v = ADAM_B2 * v + (1.0 - ADAM_B2) * (g * g)
    m_hat = m / (1.0 - ADAM_B1 ** ADAM_STEP)
    v_hat = v / (1.0 - ADAM_B2 ** ADAM_STEP)
    delta = -ADAM_LR * (m_hat / (jnp.sqrt(v_hat) + ADAM_EPS) + ADAM_WD * w)
    return delta, m, v


def _adamw_shard(w, m, v, sums, name):
    _, r, c = w.shape
    tr = _row_tile(r, TR_ELT)

    def body(w_ref, m_ref, v_ref, s_ref, g_out, d_out, m_out, v_out):
        g = s_ref[0].astype(F32)
        for s in range(1, N_CHIPS):
            g = g + s_ref[s].astype(F32)
        delta, nm, nv = _adamw(w_ref[...], g, m_ref[...], v_ref[...])
        g_out[...] = g
        d_out[...] = delta
        m_out[...] = nm
        v_out[...] = nv

    blk = pl.BlockSpec((None, tr, c), lambda i: (0, i, 0))
    return pl.pallas_call(
        body, name=name, grid=(r // tr,),
        in_specs=[blk, blk, blk, pl.BlockSpec((N_CHIPS, tr, c), lambda i: (0, i, 0))],
        out_specs=[blk] * 4,
        out_shape=[jax.ShapeDtypeStruct((1, r, c), F32)] * 4,
        compiler_params=_params(("parallel",)),
    )(w, m, v, sums)


def _adamw_small(red, params):
    names = ["sinks", "g_attn", "g_conv", "ln1_g", "ln1_b", "ln2_g", "ln2_b", "conv_w"]
    d = red.shape[1]
    hd = d // 2
    flat = []
    for nme in names:
        flat.extend(params[nme])
    nq = params["sinks"][0].shape[1]
    cs = params["conv_w"][0].shape[2]

    def body(*refs):
        red_ref = refs[0]
        ins = refs[1:1 + 3 * len(names)]
        outs = refs[1 + 3 * len(names):]
        x, y, _, _ = _mesh_pos()
        me = _chip_id(x, y)

        def conv_tap(row, base):
            picked = red_ref[row:row + 1, base:base + cs]
            for s in range(1, N_CHIPS):
                picked = jnp.where(me == s, red_ref[row:row + 1, base + s * cs:base + (s + 1) * cs], picked)
            return picked

        grads = {
            "sinks": red_ref[6:7, hd:hd + nq],
            "g_attn": red_ref[4:5, 0:hd],
            "g_conv": red_ref[4:5, hd:d],
            "ln1_g": red_ref[2:3, :],
            "ln1_b": red_ref[3:4, :],
            "ln2_g": red_ref[0:1, :],
            "ln2_b": red_ref[1:2, :],
        }
        for i, nme in enumerate(names):
            w_ref, m_ref, v_ref = ins[3 * i:3 * i + 3]
            g_out, d_out, m_out, v_out = outs[4 * i:4 * i + 4]
            if nme == "conv_w":
                for tap, (row, base) in enumerate([(5, 0), (5, hd), (6, 0)]):
                    g = conv_tap(row, base)
                    delta, nm, nv = _adamw(w_ref[0, tap:tap + 1, :], g, m_ref[0, tap:tap + 1, :], v_ref[0, tap:tap + 1, :])
                    g_out[0, tap:tap + 1, :] = g
                    d_out[0, tap:tap + 1, :] = delta
                    m_out[0, tap:tap + 1, :] = nm
                    v_out[0, tap:tap + 1, :] = nv
            else:
                g = grads[nme]
                delta, nm, nv = _adamw(w_ref[...], g, m_ref[...], v_ref[...])
                g_out[...] = g
                d_out[...] = delta
                m_out[...] = nm
                v_out[...] = nv

    out_shape = []
    for nme in names:
        out_shape.extend([jax.ShapeDtypeStruct(params[nme][0].shape, F32)] * 4)
    outs = pl.pallas_call(
        body, name="adamw_small",
        in_specs=[_VMEM] * (1 + len(flat)), out_specs=[_VMEM] * len(out_shape),
        out_shape=out_shape,
    )(red, *flat)
    return {nme: tuple(outs[4 * i:4 * i + 4]) for i, nme in enumerate(names)}


def _rope_tables(pos_col):
    s = pos_col.shape[0]
    w = N_KV_HEADS * HEAD_DIM
    tb = min(512, s)
    inv_freq = (ROPE_THETA ** (-np.arange(0, ROT_DIM, 2, dtype=np.float32) / ROT_DIM)).astype(np.float32)

    def body(pos_ref, cos_ref, sin_ref):
        pos = pos_ref[...].astype(F32)
        lane = lax.broadcasted_iota(jnp.int32, (tb, w), 1) & (HEAD_DIM - 1)
        fidx = lane & (ROT_DIM // 2 - 1)
        inv = jnp.zeros((tb, w), F32)
        for k in range(ROT_DIM // 2):
            inv = jnp.where(fidx == k, float(inv_freq[k]), inv)
        ang = pos * inv
        rot = lane < ROT_DIM
        cos_ref[...] = jnp.where(rot, jnp.cos(ang), 1.0)
        sin_v = jnp.sin(ang)
        sin_ref[...] = jnp.where(lane < ROT_DIM // 2, -sin_v, jnp.where(rot, sin_v, 0.0))

    return pl.pallas_call(
        body, name="rope_tables", grid=(s // tb,),
        in_specs=[pl.BlockSpec((tb, 1), lambda i: (i, 0))],
        out_specs=[pl.BlockSpec((tb, w), lambda i: (i, 0))] * 2,
        out_shape=[jax.ShapeDtypeStruct((s, w), F32)] * 2,
        compiler_params=_params(("parallel",)),
    )(pos_col)


def _in_proj(x, w_in_g):
    _, s, d = x.shape
    ns, _, ncol = w_in_g.shape
    tm = min(TM, s)

    def body(x_ref, w_ref, o_ref):
        o_ref[...] = _dot(x_ref[...].astype(BF16), w_ref[...])

    return pl.pallas_call(
        body, name="in_proj", grid=(s // tm, ns),
        in_specs=[pl.BlockSpec((None, tm, d), lambda i, j: (0, i, 0)),
                  pl.BlockSpec((None, d, ncol), lambda i, j: (j, 0, 0))],
        out_specs=pl.BlockSpec((tm, ncol), lambda i, j: (i, j)),
        out_shape=jax.ShapeDtypeStruct((s, ns * ncol), F32),
        compiler_params=_params(("parallel", "arbitrary")),
    )(x, w_in_g)


def _attention_scores(q, k_prev, k_cur, sinks_ref, h, first, valid):
    heads = [q[:, (GROUP * h + g) * HEAD_DIM:(GROUP * h + g + 1) * HEAD_DIM] for g in range(GROUP)]
    q4 = jnp.concatenate(heads, axis=0).astype(BF16)
    kk = jnp.concatenate([k_prev[:, h * HEAD_DIM:(h + 1) * HEAD_DIM], k_cur[:, h * HEAD_DIM:(h + 1) * HEAD_DIM]],
                         axis=0).astype(BF16)
    s = _dot_nt(q4, kk) * ATTN_SCALE
    s = jnp.where(valid, s, NEG_BIG)
    sink = jnp.concatenate(
        [jnp.broadcast_to(sinks_ref[0:1, GROUP * h + g:GROUP * h + g + 1], (WINDOW, 1)) for g in range(GROUP)], axis=0)
    m = jnp.maximum(jnp.max(s, axis=1, keepdims=True), sink)
    p = jnp.exp(s - m)
    p_sink = jnp.exp(sink - m)
    inv_l = 1.0 / (jnp.sum(p, axis=1, keepdims=True) + p_sink)
    return q4, kk, p * inv_l, p_sink * inv_l


def _band_mask(first):
    rows = GROUP * WINDOW
    qi = lax.broadcasted_iota(jnp.int32, (rows, 2 * WINDOW), 0) & (WINDOW - 1)
    kj = lax.broadcasted_iota(jnp.int32, (rows, 2 * WINDOW), 1)
    rel = qi + WINDOW - kj
    band = (rel >= 0) & (rel < WINDOW)
    return band & jnp.logical_not(first & (kj < WINDOW))


def _attention_fwd(proj, cos_t, sin_t, sinks):
    s = proj.shape[0]
    qw = GROUP * N_KV_HEADS * HEAD_DIM
    kvw = N_KV_HEADS * HEAD_DIM
    nb = s // WINDOW

    def body(cur_ref, prev_ref, cos_ref, sin_ref, cosp_ref, sinp_ref, sinks_ref, o_ref):
        n = pl.program_id(0)
        first = n == 0
        cur = cur_ref[...]
        cos, sin = cos_ref[...], sin_ref[...]
        q = _rope(cur[:, :qw], _tile_lanes(cos, GROUP), _tile_lanes(sin, GROUP), 1.0)
        k_cur = _rope(cur[:, qw:qw + kvw], cos, sin, 1.0)
        v_cur = cur[:, qw + kvw:]
        prev = prev_ref[...]
        k_prev = _rope(prev[:, :kvw], cosp_ref[...], sinp_ref[...], 1.0)
        v_prev = prev[:, kvw:]
        valid = _band_mask(first)
        outs = []
        for h in range(N_KV_HEADS):
            _, _, probs, _ = _attention_scores(q, k_prev, k_cur, sinks_ref, h, first, valid)
            vv = jnp.concatenate([v_prev[:, h * HEAD_DIM:(h + 1) * HEAD_DIM], v_cur[:, h * HEAD_DIM:(h + 1) * HEAD_DIM]],
                                 axis=0).astype(BF16)
            o = _dot(probs.astype(BF16), vv)
            outs.extend([o[g * WINDOW:(g + 1) * WINDOW] for g in range(GROUP)])
        o_ref[...] = jnp.concatenate(outs, axis=1)

    tbl = pl.BlockSpec((WINDOW, kvw), lambda n: (n, 0))
    tbl_prev = pl.BlockSpec((WINDOW, kvw), lambda n: (jnp.maximum(n - 1, 0), 0))
    return pl.pallas_call(
        body, name="attention_fwd", grid=(nb,),
        in_specs=[pl.BlockSpec((WINDOW, qw + 2 * kvw), lambda n: (n, 0)),
                  pl.BlockSpec((WINDOW, 2 * kvw), lambda n: (jnp.maximum(n - 1, 0), (qw // (2 * kvw)))),
                  tbl, tbl, tbl_prev, tbl_prev, _VMEM],
        out_specs=pl.BlockSpec((WINDOW, qw), lambda n: (n, 0)),
        out_shape=jax.ShapeDtypeStruct((s, qw), F32),
        compiler_params=_params(("parallel",)),
    )(proj, proj, cos_t, sin_t, cos_t, sin_t, sinks)


def _conv_taps(cw_ref):
    return [jnp.concatenate([cw_ref[s, k:k + 1, :] for s in range(N_CHIPS)], axis=1) for k in range(3)]


def _shift_down(z, halo, steps):
    rows = z.shape[0]
    row = lax.broadcasted_iota(jnp.int32, z.shape, 0)
    out = pltpu.roll(z, steps, 0)
    for r in range(steps):
        out = jnp.where(row == r, halo[8 - steps + r:8 - steps + r + 1, :], out)
    return out


def _shift_up(z, halo, steps):
    rows = z.shape[0]
    row = lax.broadcasted_iota(jnp.int32, z.shape, 0)
    out = pltpu.roll(z, rows - steps, 0)
    for r in range(steps):
        out = jnp.where(row == rows - steps + r, halo[r:r + 1, :], out)
    return out


def _split_cbu(lo, hi, cw):
    c_gate = lo[:, :cw]
    b_gate = jnp.concatenate([lo[:, cw:], hi[:, :2 * cw - lo.shape[1]]], axis=1)
    u = hi[:, 2 * cw - lo.shape[1]:]
    return c_gate, b_gate, u


def _conv_norm(proj, attn, cw_full, g_ac):
    s, in_w = proj.shape
    cw = attn.shape[1]
    blk_w = in_w // 3
    tb = min(TB_CONV, s)

    def body(lo_ref, hi_ref, lo_h_ref, hi_h_ref, attn_ref, cw_ref, g_ref, mixed_ref, ac_ref, rstd_ref):
        i = pl.program_id(0)
        c_gate, b_gate, u = _split_cbu(lo_ref[...], hi_ref[...], cw)
        c_h, _, u_h = _split_cbu(lo_h_ref[...], hi_h_ref[...], cw)
        z = c_gate * u
        z_h = jnp.where(i == 0, 0.0, c_h * u_h)
        w0, w1, w2 = _conv_taps(cw_ref)
        y = w0 * _shift_down(z, z_h, 2) + w1 * _shift_down(z, z_h, 1) + w2 * z
        conv = b_gate * y
        a = attn_ref[...]
        r_a = lax.rsqrt(jnp.mean(a * a, axis=-1, keepdims=True) + RMS_EPS)
        r_c = lax.rsqrt(jnp.mean(conv * conv, axis=-1, keepdims=True) + RMS_EPS)
        g = g_ref[...]
        mixed_ref[...] = jnp.concatenate([a * r_a * g[:, :cw], conv * r_c * g[:, cw:]], axis=1).astype(BF16)
        ac_ref[...] = jnp.concatenate([a, conv], axis=1)
        rstd_ref[0] = r_a
        rstd_ref[1] = r_c

    halo_idx = lambda i: jnp.maximum(i * (tb // 8) - 1, 0)
    return pl.pallas_call(
        body, name="conv_norm", grid=(s // tb,),
        in_specs=[pl.BlockSpec((tb, blk_w), lambda i: (i, 1)),
                  pl.BlockSpec((tb, blk_w), lambda i: (i, 2)),
                  pl.BlockSpec((8, blk_w), lambda i: (halo_idx(i), 1)),
                  pl.BlockSpec((8, blk_w), lambda i: (halo_idx(i), 2)),
                  pl.BlockSpec((tb, cw), lambda i: (i, 0)),
                  _VMEM, _VMEM],
        out_specs=[pl.BlockSpec((tb, 2 * cw), lambda i: (i, 0)),
                   pl.BlockSpec((tb, 2 * cw), lambda i: (i, 0)),
                   pl.BlockSpec((2, tb, 1), lambda i: (0, i, 0))],
        out_shape=[jax.ShapeDtypeStruct((s, 2 * cw), BF16), jax.ShapeDtypeStruct((s, 2 * cw), F32),
                   jax.ShapeDtypeStruct((2, s, 1), F32)],
        compiler_params=_params(("parallel",)),
    )(proj, proj, proj, proj, attn, cw_full, g_ac)


def _out_proj_ln(mixed, w_out_g, x, ln_g, ln_b):
    s, d = mixed.shape
    tm = min(TM, s)
    tk = min(512, d)
    nk = d // tk

    def body(a_ref, w_ref, x_ref, g_ref, b_ref, xhat_ref, h_ref, rstd_ref, acc):
        k = pl.program_id(1)

        @pl.when(k == 0)
        def _():
            acc[...] = jnp.zeros_like(acc)

        acc[...] += _dot(a_ref[...], w_ref[...])

        @pl.when(k == nk - 1)
        def _():
            def rows_fn(rows):
                xhat, rstd = _ln_fwd(ALPHA * x_ref[rows, :] + acc[rows, :])
                xhat_ref[rows, :] = xhat
                h_ref[rows, :] = (xhat * g_ref[...] + b_ref[...]).astype(BF16)
                rstd_ref[rows, :] = rstd

            _for_row_chunks(tm, rows_fn)

    row = pl.BlockSpec((tm, d), lambda i, k: (i, 0))
    return pl.pallas_call(
        body, name="out_proj_ln", grid=(s // tm, nk),
        in_specs=[pl.BlockSpec((tm, tk), lambda i, k: (i, k)),
                  pl.BlockSpec((tk, d), lambda i, k: (k, 0)),
                  pl.BlockSpec((None, tm, d), lambda i, k: (0, i, 0)),
                  _VMEM, _VMEM],
        out_specs=[row, row, pl.BlockSpec((tm, 1), lambda i, k: (i, 0))],
        out_shape=[jax.ShapeDtypeStruct((s, d), F32), jax.ShapeDtypeStruct((s, d), BF16),
                   jax.ShapeDtypeStruct((s, 1), F32)],
        scratch_shapes=[pltpu.VMEM((tm, d), F32)],
        compiler_params=_params(("parallel", "arbitrary")),
    )(mixed, w_out_g, x, ln_g, ln_b)


def _gate_up(h1, w_gate_g, w_up_g):
    s, d = h1.shape
    ns, _, fs = w_gate_g.shape
    tm = min(TM, s)
    tk = min(1024, d)
    nk = d // tk

    def body(h_ref, wg_ref, wu_ref, act_ref, g_ref, u_ref, acc_g, acc_u):
        k = pl.program_id(2)

        @pl.when(k == 0)
        def _():
            acc_g[...] = jnp.zeros_like(acc_g)
            acc_u[...] = jnp.zeros_like(acc_u)

        h = h_ref[...]
        acc_g[...] += _dot(h, wg_ref[...])
        acc_u[...] += _dot(h, wu_ref[...])

        @pl.when(k == nk - 1)
        def _():
            def rows_fn(rows):
                g, u = acc_g[rows, :], acc_u[rows, :]
                act_ref[rows, :] = (g * _sigmoid(g) * u).astype(BF16)
                g_ref[rows, :] = g.astype(BF16)
                u_ref[rows, :] = u.astype(BF16)

            _for_row_chunks(tm, rows_fn)

    wspec = pl.BlockSpec((None, tk, fs), lambda i, j, k: (j, k, 0))
    ospec = pl.BlockSpec((tm, fs), lambda i, j, k: (i, j))
    return pl.pallas_call(
        body, name="gate_up", grid=(s // tm, ns, nk),
        in_specs=[pl.BlockSpec((tm, tk), lambda i, j, k: (i, k)), wspec, wspec],
        out_specs=[ospec] * 3,
        out_shape=[jax.ShapeDtypeStruct((s, ns * fs), BF16)] * 3,
        scratch_shapes=[pltpu.VMEM((tm, fs), F32)] * 2,
        compiler_params=_params(("parallel", "arbitrary", "arbitrary")),
    )(h1, w_gate_g, w_up_g)


def _down_ln_loss(act, w_down_g, xhat1, ln1_g, ln1_b, ln2_g, ln2_b, target):
    s, f = act.shape
    d = xhat1.shape[1]
    tm = min(TM, s)
    tk = 512
    nk = f // tk

    def body(a_ref, w_ref, xh_ref, g1_ref, b1_ref, g2_ref, b2_ref, t_ref, dpre_ref, loss_ref, gg_ref, gb_ref, acc):
        i, k = pl.program_id(0), pl.program_id(1)

        @pl.when(k == 0)
        def _():
            acc[...] = jnp.zeros_like(acc)

        acc[...] += _dot(a_ref[...], w_ref[...])

        @pl.when(k == nk - 1)
        def _():
            @pl.when(i == 0)
            def _():
                loss_ref[...] = jnp.zeros_like(loss_ref)
                gg_ref[...] = jnp.zeros_like(gg_ref)
                gb_ref[...] = jnp.zeros_like(gb_ref)

            def rows_fn(rows):
                h1 = xh_ref[rows, :] * g1_ref[...] + b1_ref[...]
                xhat, rstd = _ln_fwd(ALPHA * h1 + acc[rows, :])
                g2 = g2_ref[...]
                diff = xhat * g2 + b2_ref[...] - t_ref[rows, :]
                dy = diff * (1.0 / d)
                dpre_ref[rows, :] = _ln_bwd(dy, xhat, rstd, g2)
                sq = jnp.sum(jnp.sum(diff * diff, axis=1, keepdims=True), axis=0, keepdims=True)
                loss_ref[...] += jnp.broadcast_to(sq * (0.5 / d), (1, 128))
                gg_ref[...] += jnp.sum(dy * xhat, axis=0, keepdims=True)
                gb_ref[...] += jnp.sum(dy, axis=0, keepdims=True)

            _for_row_chunks(tm, rows_fn)

    row = pl.BlockSpec((tm, d), lambda i, k: (i, 0))
    vec = pl.BlockSpec((1, d), lambda i, k: (0, 0))
    return pl.pallas_call(
        body, name="down_ln_loss", grid=(s // tm, nk),
        in_specs=[pl.BlockSpec((tm, tk), lambda i, k: (i, k)),
                  pl.BlockSpec((tk, d), lambda i, k: (k, 0)),
                  row, _VMEM, _VMEM, _VMEM, _VMEM,
                  pl.BlockSpec((None, tm, d), lambda i, k: (0, i, 0))],
        out_specs=[row, pl.BlockSpec((1, 128), lambda i, k: (0, 0)), vec, vec],
        out_shape=[jax.ShapeDtypeStruct((s, d), F32), jax.ShapeDtypeStruct((1, 128), F32),
                   jax.ShapeDtypeStruct((1, d), F32), jax.ShapeDtypeStruct((1, d), F32)],
        scratch_shapes=[pltpu.VMEM((tm, d), F32)],
        compiler_params=_params(("arbitrary", "arbitrary")),
    )(act, w_down_g, xhat1, ln1_g, ln1_b, ln2_g, ln2_b, target)


def _dact_silu_bwd(dpre2, w_down_g, gate, up):
    s, d = dpre2.shape
    f = gate.shape[1]
    fs = f // N_CHIPS
    tm = min(TM, s)

    def body(dp_ref, w_ref, g_ref, u_ref, dg_ref, du_ref):
        d_act = _dot_nt(dp_ref[...].astype(BF16), w_ref[...])
        g = g_ref[...].astype(F32)
        u = u_ref[...].astype(F32)
        sg = _sigmoid(g)
        dg_ref[...] = (d_act * u * (sg * (1.0 + g * (1.0 - sg)))).astype(BF16)
        du_ref[...] = (d_act * (g * sg)).astype(BF16)

    blk = pl.BlockSpec((tm, fs), lambda i, j: (i, j))
    return pl.pallas_call(
        body, name="dact_silu_bwd", grid=(s // tm, N_CHIPS),
        in_specs=[pl.BlockSpec((tm, d), lambda i, j: (i, 0)),
                  pl.BlockSpec((fs, d), lambda i, j: (j, 0)), blk, blk],
        out_specs=[blk, blk],
        out_shape=[jax.ShapeDtypeStruct((s, f), BF16)] * 2,
        compiler_params=_params(("parallel", "arbitrary")),
    )(dpre2, w_down_g, gate, up)


def _grad_rows(a, b, name, row_blocks=1):
    s, m = a.shape
    n = b.shape[1]
    ms = m // N_CHIPS
    tmw = ms // row_blocks
    tk = min(TK_TOK, s)
    nk = s // tk

    def body(a_ref, b_ref, o_ref, acc):
        k = pl.program_id(2)

        @pl.when(k == 0)
        def _():
            acc[...] = jnp.zeros_like(acc)

        acc[...] += _dot_tn(a_ref[...].astype(BF16), b_ref[...].astype(BF16))

        @pl.when(k == nk - 1)
        def _():
            o_ref[...] = acc[...].astype(BF16)

    return pl.pallas_call(
        body, name=name, grid=(N_CHIPS, row_blocks, nk),
        in_specs=[pl.BlockSpec((tk, tmw), lambda j, r, k: (k, j * row_blocks + r)),
                  pl.BlockSpec((tk, n), lambda j, r, k: (k, 0))],
        out_specs=pl.BlockSpec((None, tmw, n), lambda j, r, k: (j, r, 0)),
        out_shape=jax.ShapeDtypeStruct((N_CHIPS, ms, n), BF16),
        scratch_shapes=[pltpu.VMEM((tmw, n), F32)],
        compiler_params=_params(("parallel", "parallel", "arbitrary")),
    )(a, b)


def _grad_cols(a, bs, name, a_3d=False, row_blocks=2):
    s, m = a.shape[-2:]
    n = bs[0].shape[1]
    ns = n // N_CHIPS
    nb = len(bs)
    tmw = m // row_blocks
    tk = min(TK_TOK, s)
    nk = s // tk

    def body(*refs):
        a_ref, b_refs, o_refs, accs = refs[0], refs[1:1 + nb], refs[1 + nb:1 + 2 * nb], refs[1 + 2 * nb:]
        k = pl.program_id(2)

        @pl.when(k == 0)
        def _():
            for acc in accs:
                acc[...] = jnp.zeros_like(acc)

        at = a_ref[...].astype(BF16)
        for b_ref, acc in zip(b_refs, accs):
            acc[...] += _dot_tn(at, b_ref[...].astype(BF16))

        @pl.when(k == nk - 1)
        def _():
            for o_ref, acc in zip(o_refs, accs):
                o_ref[...] = acc[...].astype(BF16)

    if a_3d:
        a_spec = pl.BlockSpec((None, tk, tmw), lambda j, r, k: (0, k, r))
    else:
        a_spec = pl.BlockSpec((tk, tmw), lambda j, r, k: (k, r))
    return pl.pallas_call(
        body, name=name, grid=(N_CHIPS, row_blocks, nk),
        in_specs=[a_spec] + [pl.BlockSpec((tk, ns), lambda j, r, k: (k, j))] * nb,
        out_specs=[pl.BlockSpec((None, tmw, ns), lambda j, r, k: (j, r, 0))] * nb,
        out_shape=[jax.ShapeDtypeStruct((N_CHIPS, m, ns), BF16)] * nb,
        scratch_shapes=[pltpu.VMEM((tmw, ns), F32)] * nb,
        compiler_params=_params(("parallel", "parallel", "arbitrary")),
    )(a, *bs)


def _dh1_ln_bwd(d_gate, d_up, w_gate_g, w_up_g, dpre2, xhat1, rstd1, ln1_g, tm_max=256):
    s, f = d_gate.shape
    d = dpre2.shape[1]
    fs = f // N_CHIPS
    tm = min(tm_max, s)

    def body(dg_ref, du_ref, wg_ref, wu_ref, dp2_ref, xh_ref, rs_ref, g_ref, dpre_ref, gg_ref, gb_ref, acc):
        i, j = pl.program_id(0), pl.program_id(1)

        @pl.when(j == 0)
        def _():
            acc[...] = jnp.zeros_like(acc)

        acc[...] += _dot_nt(dg_ref[...], wg_ref[...]) + _dot_nt(du_ref[...], wu_ref[...])

        @pl.when(j == N_CHIPS - 1)
        def _():
            @pl.when(i == 0)
            def _():
                gg_ref[...] = jnp.zeros_like(gg_ref)
                gb_ref[...] = jnp.zeros_like(gb_ref)

            def rows_fn(rows):
                dh = acc[rows, :] + ALPHA * dp2_ref[rows, :]
                xhat = xh_ref[rows, :]
                dpre_ref[rows, :] = _ln_bwd(dh, xhat, rs_ref[rows, :], g_ref[...])
                gg_ref[...] += jnp.sum(dh * xhat, axis=0, keepdims=True)
                gb_ref[...] += jnp.sum(dh, axis=0, keepdims=True)

            _for_row_chunks(tm, rows_fn)

    row = pl.BlockSpec((tm, d), lambda i, j: (i, 0))
    vec = pl.BlockSpec((1, d), lambda i, j: (0, 0))
    act_blk = pl.BlockSpec((tm, fs), lambda i, j: (i, j))
    w_blk = pl.BlockSpec((None, d, fs), lambda i, j: (j, 0, 0))
    return pl.pallas_call(
        body, name="dh1_ln_bwd", grid=(s // tm, N_CHIPS),
        in_specs=[act_blk, act_blk, w_blk, w_blk, row, row, pl.BlockSpec((tm, 1), lambda i, j: (i, 0)), _VMEM],
        out_specs=[row, vec, vec],
        out_shape=[jax.ShapeDtypeStruct((s, d), F32), jax.ShapeDtypeStruct((1, d), F32),
                   jax.ShapeDtypeStruct((1, d), F32)],
        scratch_shapes=[pltpu.VMEM((tm, d), F32)],
        compiler_params=_params(("arbitrary", "arbitrary")),
    )(d_gate, d_up, w_gate_g, w_up_g, dpre2, xhat1, rstd1, ln1_g)


def _dmixed_rms_bwd(dpre1, w_out_g, ac, rstd, g_ac):
    s, d = dpre1.shape
    hd = d // 2
    tm = min(TM, s)

    def body(dp_ref, w_ref, ac_ref, rs_ref, g_ref, dac_ref, gg_ref):
        i = pl.program_id(1)
        dm = _dot_nt(dp_ref[...].astype(BF16), w_ref[...])
        pre = ac_ref[...]
        r = rs_ref[...]
        gdm = dm * g_ref[...]
        dac_ref[...] = r * gdm - pre * (r * r * r) * jnp.mean(gdm * pre, axis=-1, keepdims=True)
        gg = jnp.sum(dm * pre * r, axis=0, keepdims=True)

        @pl.when(i == 0)
        def _():
            gg_ref[...] = gg

        @pl.when(i > 0)
        def _():
            gg_ref[...] += gg

    return pl.pallas_call(
        body, name="dmixed_rms_bwd", grid=(2, s // tm),
        in_specs=[pl.BlockSpec((tm, d), lambda h, i: (i, 0)),
                  pl.BlockSpec((hd, d), lambda h, i: (h, 0)),
                  pl.BlockSpec((tm, hd), lambda h, i: (i, h)),
                  pl.BlockSpec((None, tm, 1), lambda h, i: (h, i, 0)),
                  pl.BlockSpec((1, hd), lambda h, i: (0, h))],
        out_specs=[pl.BlockSpec((tm, hd), lambda h, i: (i, h)),
                   pl.BlockSpec((1, hd), lambda h, i: (0, h))],
        out_shape=[jax.ShapeDtypeStruct((s, d), F32), jax.ShapeDtypeStruct((1, d), F32)],
        compiler_params=_params(("arbitrary", "arbitrary")),
    )(dpre1, w_out_g, ac, rstd, g_ac)


def _attention_bwd(proj, d_ac, cos_t, sin_t, sinks):
    s = proj.shape[0]
    qw = GROUP * N_KV_HEADS * HEAD_DIM
    kvw = N_KV_HEADS * HEAD_DIM
    nb = s // WINDOW
    nq = GROUP * N_KV_HEADS

    def body(cur_ref, prev_ref, do_ref, cos_ref, sin_ref, cosp_ref, sinp_ref, sinks_ref,
             dq_ref, dcur_ref, dprev_ref, dsink_ref):
        n = pl.program_id(0)
        first = n == 0
        cur = cur_ref[...]
        cos, sin = cos_ref[...], sin_ref[...]
        cos_q, sin_q = _tile_lanes(cos, GROUP), _tile_lanes(sin, GROUP)
        q = _rope(cur[:, :qw], cos_q, sin_q, 1.0)
        k_cur = _rope(cur[:, qw:qw + kvw], cos, sin, 1.0)
        v_cur = cur[:, qw + kvw:]
        prev = prev_ref[...]
        k_prev = _rope(prev[:, :kvw], cosp_ref[...], sinp_ref[...], 1.0)
        v_prev = prev[:, kvw:]
        d_out = do_ref[...]
        valid = _band_mask(first)
        dq_parts, dk_parts, dv_parts, dsink_parts = [], [], [], []
        for h in range(N_KV_HEADS):
            q4, kk, probs, p_sink = _attention_scores(q, k_prev, k_cur, sinks_ref, h, first, valid)
            vv = jnp.concatenate([v_prev[:, h * HEAD_DIM:(h + 1) * HEAD_DIM], v_cur[:, h * HEAD_DIM:(h + 1) * HEAD_DIM]],
                                 axis=0).astype(BF16)
            do4 = jnp.concatenate(
                [d_out[:, (GROUP * h + g) * HEAD_DIM:(GROUP * h + g + 1) * HEAD_DIM] for g in range(GROUP)],
                axis=0).astype(BF16)
            d_probs = _dot_nt(do4, vv)
            delta = jnp.sum(probs * d_probs, axis=1, keepdims=True)
            d_s = (probs * (d_probs - delta) * ATTN_SCALE).astype(BF16)
            dq4 = _dot(d_s, kk)
            dq_parts.extend([dq4[g * WINDOW:(g + 1) * WINDOW] for g in range(GROUP)])
            dk_parts.append(_dot_tn(d_s, q4))
            dv_parts.append(_dot_tn(probs.astype(BF16), do4))
            ds_sink = -p_sink * delta
            dsink_parts.extend([jnp.sum(ds_sink[g * WINDOW:(g + 1) * WINDOW], axis=0, keepdims=True)
                                for g in range(GROUP)])
        dq_ref[...] = _rope(jnp.concatenate(dq_parts, axis=1), cos_q, sin_q, -1.0)
        dk = jnp.concatenate(dk_parts, axis=1)
        dv = jnp.concatenate(dv_parts, axis=1)
        dprev_ref[...] = jnp.concatenate([dk[:WINDOW], dv[:WINDOW]], axis=1)
        dcur_ref[...] = jnp.concatenate([dk[WINDOW:], dv[WINDOW:]], axis=1)
        dsink = jnp.concatenate(dsink_parts, axis=1)

        @pl.when(first)
        def _():
            dsink_ref[...] = dsink

        @pl.when(n > 0)
        def _():
            dsink_ref[...] += dsink

    tbl = pl.BlockSpec((WINDOW, kvw), lambda n: (n, 0))
    tbl_prev = pl.BlockSpec((WINDOW, kvw), lambda n: (jnp.maximum(n - 1, 0), 0))
    kv_blk = pl.BlockSpec((WINDOW, 2 * kvw), lambda n: (n, 0))
    return pl.pallas_call(
        body, name="attention_bwd", grid=(nb,),
        in_specs=[pl.BlockSpec((WINDOW, qw + 2 * kvw), lambda n: (n, 0)),
                  pl.BlockSpec((WINDOW, 2 * kvw), lambda n: (jnp.maximum(n - 1, 0), (qw // (2 * kvw)))),
                  pl.BlockSpec((WINDOW, qw), lambda n: (n, 0)),
                  tbl, tbl, tbl_prev, tbl_prev, _VMEM],
        out_specs=[pl.BlockSpec((WINDOW, qw), lambda n: (n, 0)), kv_blk, kv_blk,
                   pl.BlockSpec((1, nq), lambda n: (0, 0))],
        out_shape=[jax.ShapeDtypeStruct((s, qw), F32), jax.ShapeDtypeStruct((s, 2 * kvw), F32),
                   jax.ShapeDtypeStruct((s, 2 * kvw), F32), jax.ShapeDtypeStruct((1, nq), F32)],
        compiler_params=_params(("arbitrary",)),
    )(proj, proj, d_ac, cos_t, sin_t, cos_t, sin_t, sinks)


def _dproj_assemble(proj, d_ac, dq, dkv_cur, dkv_prev, cos_t, sin_t, cw_full):
    s, in_w = proj.shape
    cw = dq.shape[1]
    kvw = N_KV_HEADS * HEAD_DIM
    blk_w = in_w // 3
    tb = WINDOW
    nb = s // tb

    def body(lo_ref, hi_ref, lo_p_ref, hi_p_ref, lo_n_ref, hi_n_ref, dconv_ref, dconv_n_ref,
             dq_ref, dcur_ref, dprev_n_ref, cos_ref, sin_ref, cw_ref, dproj_ref, gcw_ref):
        i = pl.program_id(0)
        last = i == nb - 1
        c_gate, b_gate, u = _split_cbu(lo_ref[...], hi_ref[...], cw)
        c_p, _, u_p = _split_cbu(lo_p_ref[...], hi_p_ref[...], cw)
        _, b_n, _ = _split_cbu(lo_n_ref[...], hi_n_ref[...], cw)
        z = c_gate * u
        z_p = jnp.where(i == 0, 0.0, c_p * u_p)
        z1 = _shift_down(z, z_p, 1)
        z2 = _shift_down(z, z_p, 2)
        w0, w1, w2 = _conv_taps(cw_ref)
        y = w0 * z2 + w1 * z1 + w2 * z
        d_conv = dconv_ref[...]
        d_b = d_conv * y
        d_y = d_conv * b_gate
        d_y_n = jnp.where(last, 0.0, dconv_n_ref[...] * b_n)
        d_z = w2 * d_y + w1 * _shift_up(d_y, d_y_n, 1) + w0 * _shift_up(d_y, d_y_n, 2)
        d_c = d_z * u
        d_u = d_z * c_gate
        gcw = jnp.concatenate([jnp.sum(d_y * z2, axis=0, keepdims=True), jnp.sum(d_y * z1, axis=0, keepdims=True),
                               jnp.sum(d_y * z, axis=0, keepdims=True)], axis=0)

        @pl.when(i == 0)
        def _():
            gcw_ref[...] = gcw

        @pl.when(i > 0)
        def _():
            gcw_ref[...] += gcw

        dkv = dcur_ref[...] + jnp.where(last, 0.0, dprev_n_ref[...])
        dk = _rope(dkv[:, :kvw], cos_ref[...], sin_ref[...], -1.0)
        dproj_ref[...] = jnp.concatenate([dq_ref[...], dk, dkv[:, kvw:], d_c, d_b, d_u], axis=1).astype(BF16)

    prev8 = lambda i: jnp.maximum(i * (tb // 8) - 1, 0)
    next8 = lambda i: jnp.minimum((i + 1) * (tb // 8), s // 8 - 1)
    nxt = lambda i: jnp.minimum(i + 1, nb - 1)
    return pl.pallas_call(
        body, name="dproj_assemble", grid=(nb,),
        in_specs=[pl.BlockSpec((tb, blk_w), lambda i: (i, 1)),
                  pl.BlockSpec((tb, blk_w), lambda i: (i, 2)),
                  pl.BlockSpec((8, blk_w), lambda i: (prev8(i), 1)),
                  pl.BlockSpec((8, blk_w), lambda i: (prev8(i), 2)),
                  pl.BlockSpec((8, blk_w), lambda i: (next8(i), 1)),
                  pl.BlockSpec((8, blk_w), lambda i: (next8(i), 2)),
                  pl.BlockSpec((tb, cw), lambda i: (i, 1)),
                  pl.BlockSpec((8, cw), lambda i: (next8(i), 1)),
                  pl.BlockSpec((tb, cw), lambda i: (i, 0)),
                  pl.BlockSpec((tb, 2 * kvw), lambda i: (i, 0)),
                  pl.BlockSpec((tb, 2 * kvw), lambda i: (nxt(i), 0)),
                  pl.BlockSpec((tb, kvw), lambda i: (i, 0)),
                  pl.BlockSpec((tb, kvw), lambda i: (i, 0)),
                  _VMEM],
        out_specs=[pl.BlockSpec((tb, in_w), lambda i: (i, 0)),
                   pl.BlockSpec((3, cw), lambda i: (0, 0))],
        out_shape=[jax.ShapeDtypeStruct((s, in_w), BF16), jax.ShapeDtypeStruct((3, cw), F32)],
        compiler_params=_params(("arbitrary",)),
    )(proj, proj, proj, proj, proj, proj, d_ac, d_ac, dq, dkv_cur, dkv_prev, cos_t, sin_t, cw_full)


def _dx(d_proj, w_in_g, dpre1):
    s, in_w = d_proj.shape
    ns, d, ncol = w_in_g.shape
    tm = min(TM, s)

    def body(dp_ref, w_ref, r_ref, o_ref, acc):
        j = pl.program_id(1)

        @pl.when(j == 0)
        def _():
            acc[...] = jnp.zeros_like(acc)

        acc[...] += _dot_nt(dp_ref[...], w_ref[...])

        @pl.when(j == ns - 1)
        def _():
            o_ref[...] = acc[...] + ALPHA * r_ref[...]

    return pl.pallas_call(
        body, name="dx", grid=(s // tm, ns),
        in_specs=[pl.BlockSpec((tm, ncol), lambda i, j: (i, j)),
                  pl.BlockSpec((None, d, ncol), lambda i, j: (j, 0, 0)),
                  pl.BlockSpec((tm, d), lambda i, j: (i, 0))],
        out_specs=pl.BlockSpec((None, tm, d), lambda i, j: (0, i, 0)),
        out_shape=jax.ShapeDtypeStruct((1, s, d), F32),
        scratch_shapes=[pltpu.VMEM((tm, d), F32)],
        compiler_params=_params(("parallel", "arbitrary")),
    )(d_proj, w_in_g, dpre1)


def kernel(x, positions, w_in, conv_w, sinks, g_attn, g_conv, w_out, ln1_g, ln1_b, w_gate, w_up, w_down, ln2_g, ln2_b, loss_target, m_w_in, m_conv_w, m_sinks, m_g_attn, m_g_conv, m_w_out, m_ln1_g, m_ln1_b, m_w_gate, m_w_up, m_w_down, m_ln2_g, m_ln2_b, v_w_in, v_conv_w, v_sinks, v_g_attn, v_g_conv, v_w_out, v_ln1_g, v_ln1_b, v_w_gate, v_w_up, v_w_down, v_ln2_g, v_ln2_b):
    s = x.shape[1]
    d = x.shape[2]

    chip_vec = _chip_id(lax.axis_index("x"), lax.axis_index("y")).astype(jnp.int32).reshape(1)
    wnames = ["w_in", "w_out", "w_gate", "w_up", "w_down"]
    bufs = [_cast_weight(w, chip_vec, "cast_" + nme) for w, nme in zip([w_in, w_out, w_gate, w_up, w_down], wnames)]
    flights, token = _gather_start(bufs)

    def gathered(i, after):
        send_sems, recv_sems, buf = flights[i]
        buf = _gather_wait(send_sems, recv_sems, buf, after, "gather_wait_" + wnames[i])
        return _sibling_fill(buf, "sibling_fill_" + wnames[i])

    cw_full = _allgather_conv_w(conv_w)
    g_ac = jnp.concatenate([g_attn, g_conv], axis=1)

    cos_t, sin_t = _rope_tables(positions.reshape(s, 1) + token[0:1, 0:1].astype(jnp.int32))
    w_in_g = gathered(0, cos_t)
    proj = _in_proj(x, w_in_g)
    w_out_full = gathered(1, proj).reshape(d, d)
    attn = _attention_fwd(proj, cos_t, sin_t, sinks)
    mixed, ac, rstd_ac = _conv_norm(proj, attn, cw_full, g_ac)
    w_gate_g = gathered(2, mixed)
    w_up_g = gathered(3, mixed)
    xhat1, h1, rstd1 = _out_proj_ln(mixed, w_out_full, x, ln1_g, ln1_b)
    w_down_full = gathered(4, h1).reshape(-1, d)
    act, gate, up = _gate_up(h1, w_gate_g, w_up_g)
    dpre2, loss_part, g_ln2_g, g_ln2_b = _down_ln_loss(act, w_down_full, xhat1, ln1_g, ln1_b, ln2_g, ln2_b, loss_target)

    d_gate, d_up = _dact_silu_bwd(dpre2, w_down_full, gate, up)
    p_down = _grad_rows(act, dpre2, "grad_w_down")
    p_gate, p_up = _grad_cols(h1, [d_gate, d_up], "grad_w_gate_up")
    dpre1, g_ln1_g, g_ln1_b = _dh1_ln_bwd(d_gate, d_up, w_gate_g, w_up_g, dpre2, xhat1, rstd1, ln1_g)
    d_ac, g_g_ac = _dmixed_rms_bwd(dpre1, w_out_full, ac, rstd_ac, g_ac)
    p_out = _grad_rows(mixed, dpre1, "grad_w_out")
    dq, dkv_cur, dkv_prev, g_sinks = _attention_bwd(proj, d_ac, cos_t, sin_t, sinks)
    d_proj, g_conv_w = _dproj_assemble(proj, d_ac, dq, dkv_cur, dkv_prev, cos_t, sin_t, cw_full)
    (p_in,) = _grad_cols(x, [d_proj], "grad_w_in", a_3d=True)
    grad_x = _dx(d_proj, w_in_g, dpre1)

    parts = [p_in, p_out, p_gate, p_up, p_down]
    got = _exchange_halves(parts)
    cvec = lax.axis_index("c").astype(jnp.int32).reshape(1)
    names = ["w_in", "w_out", "w_gate", "w_up", "w_down"]
    chip_sums = [_add_halves(p, g, cvec, "add_halves_" + nme) for p, g, nme in zip(parts, got, names)]
    sums = _scatter_chip_sums(chip_sums)
    red = _allreduce_small(g_ln2_g, g_ln2_b, g_ln1_g, g_ln1_b, g_g_ac, g_conv_w, g_sinks, loss_part)

    big = {}
    for nme, w, m, v, sm in zip(names, [w_in, w_out, w_gate, w_up, w_down],
                                [m_w_in, m_w_out, m_w_gate, m_w_up, m_w_down],
                                [v_w_in, v_w_out, v_w_gate, v_w_up, v_w_down], sums):
        big[nme] = _adamw_shard(w, m, v, sm, "adamw_" + nme)
    small = _adamw_small(red, {
        "sinks": (sinks, m_sinks, v_sinks), "g_attn": (g_attn, m_g_attn, v_g_attn),
        "g_conv": (g_conv, m_g_conv, v_g_conv), "ln1_g": (ln1_g, m_ln1_g, v_ln1_g),
        "ln1_b": (ln1_b, m_ln1_b, v_ln1_b), "ln2_g": (ln2_g, m_ln2_g, v_ln2_g),
        "ln2_b": (ln2_b, m_ln2_b, v_ln2_b), "conv_w": (conv_w, m_conv_w, v_conv_w)})
    res = {**big, **small}
    order = ["w_in", "conv_w", "sinks", "g_attn", "g_conv", "w_out", "ln1_g", "ln1_b", "w_gate", "w_up", "w_down",
             "ln2_g", "ln2_b"]
    loss = red[6, d // 2 + 128]
    return (loss, grad_x, *[res[n][0] for n in order], *[res[n][1] for n in order],
            *[res[n][2] for n in order], *[res[n][3] for n in order])
```

```python
import functools

import numpy as np
import jax
import jax.numpy as jnp
from jax import lax
from jax.experimental import pallas as pl
from jax.experimental.pallas import tpu as pltpu

F32 = jnp.float32
BF16 = jnp.bfloat16
MESH = pl.DeviceIdType.MESH

HEAD_DIM = 64
N_KV_HEADS = 4
GROUP = 4
WINDOW = 128
ROT_DIM = 16
ROPE_THETA = 500000.0
ATTN_SCALE = HEAD_DIM ** -0.5
ALPHA = 2.0 ** 0.25
LN_EPS = 1e-5
RMS_EPS = 1e-6
ADAM_LR = 0.001
ADAM_B1 = 0.9
ADAM_B2 = 0.999
ADAM_EPS = 1e-08
ADAM_WD = 0.01
ADAM_STEP = 10
N_CHIPS = 4
NEG_BIG = -1e30

V7X_VMEM_BYTES = 64 * 1024 * 1024
VMEM_LIMIT = V7X_VMEM_BYTES - 6 * 1024 * 1024

TM = 512
TK_TOK = 512
TB_CONV = 256
TR_ELT = 256
ROW_CHUNK = 128


def _params(sem):
    return pltpu.CompilerParams(dimension_semantics=sem, vmem_limit_bytes=VMEM_LIMIT)


def _row_tile(rows, target):
    best = None
    for t in range(16, min(rows, target) + 1, 16):
        if rows % t == 0:
            best = t
    assert best is not None, (rows, target)
    return best


def _dot(a, b):
    return jnp.dot(a, b, preferred_element_type=F32)


def _dot_nt(a, b):
    return lax.dot_general(a, b, (((1,), (1,)), ((), ())), preferred_element_type=F32)


def _dot_tn(a, b):
    return lax.dot_general(a, b, (((0,), (0,)), ((), ())), preferred_element_type=F32)


def _mesh_pos():
    x, y, c = lax.axis_index("x"), lax.axis_index("y"), lax.axis_index("c")
    chips = [(1 - x, y), (x, 1 - y), (1 - x, 1 - y)]
    return x, y, c, chips


def _chip_id(px, py):
    return 2 * px + py


def _rope(t, cos, sgn_sin, sign):
    w = t.shape[1]
    lane = lax.broadcasted_iota(jnp.int32, t.shape, 1) & (HEAD_DIM - 1)
    partner = jnp.where(lane < ROT_DIM // 2, pltpu.roll(t, w - ROT_DIM // 2, 1), pltpu.roll(t, ROT_DIM // 2, 1))
    return t * cos + sign * (partner * sgn_sin)


def _tile_lanes(t, n):
    return jnp.concatenate([t] * n, axis=1)


def _sigmoid(g):
    return 1.0 / (1.0 + jnp.exp(-g))


def _for_row_chunks(n_rows, fn):
    def step(r, carry):
        fn(pl.ds(pl.multiple_of(r * ROW_CHUNK, ROW_CHUNK), ROW_CHUNK))
        return carry

    lax.fori_loop(0, n_rows // ROW_CHUNK, step, 0)


def _ln_fwd(pre):
    mu = jnp.mean(pre, axis=-1, keepdims=True)
    cen = pre - mu
    var = jnp.mean(cen * cen, axis=-1, keepdims=True)
    rstd = lax.rsqrt(var + LN_EPS)
    return cen * rstd, rstd


def _ln_bwd(dy, xhat, rstd, g):
    dxhat = dy * g
    m1 = jnp.mean(dxhat, axis=-1, keepdims=True)
    m2 = jnp.mean(dxhat * xhat, axis=-1, keepdims=True)
    return rstd * (dxhat - m1 - xhat * m2)


def _cast_weight(w, chip_vec, name):
    _, r, c = w.shape
    tr = _row_tile(r, TR_ELT)

    def body(chip_ref, w_ref, o_ref):
        o_ref[...] = w_ref[...].astype(BF16)

    grid_spec = pltpu.PrefetchScalarGridSpec(
        num_scalar_prefetch=1, grid=(r // tr,),
        in_specs=[pl.BlockSpec((None, tr, c), lambda i, chip_ref: (0, i, 0))],
        out_specs=pl.BlockSpec((None, tr, c), lambda i, chip_ref: (chip_ref[0], i, 0)))
    return pl.pallas_call(
        body, name=name, grid_spec=grid_spec,
        out_shape=jax.ShapeDtypeStruct((N_CHIPS, r, c), BF16),
        compiler_params=_params(("parallel",)),
    )(chip_vec, w)


_HBM = pl.BlockSpec(memory_space=pltpu.HBM)
_VMEM = pl.BlockSpec(memory_space=pltpu.VMEM)


_SEM = pl.BlockSpec(memory_space=pltpu.SEMAPHORE)
_ANY = pl.BlockSpec(memory_space=pl.ANY)
_EFFECT = pltpu.SideEffectType.DATAFLOW_SIDE_EFFECTING


def _chip_copy(buf, k, chip_of_src, half_rows, send_sems, recv_sems, to):
    part = buf.at[chip_of_src, half_rows]
    return pltpu.make_async_remote_copy(
        src_ref=part, dst_ref=part, send_sem=send_sems.at[k], recv_sem=recv_sems.at[k], device_id=to, device_id_type=MESH)


def _half_rows(buf, which):
    hr = buf.shape[1] // 2
    return pl.ds(which * hr, hr)


def _gather_start(bufs, after):
    n = len(bufs)

    def body(*refs):
        ins = refs[:n]
        sends, recvs = refs[n + 1:2 * n + 1], refs[2 * n + 1:3 * n + 1]
        token = refs[4 * n + 1]
        x, y, c, chips = _mesh_pos()
        me = _chip_id(x, y)
        for w in range(n):
            for k, chip in enumerate(chips):
                _chip_copy(ins[w], k, me, _half_rows(ins[w], c), sends[w], recvs[w], (*chip, c)).start()
        token[...] = jnp.zeros_like(token)

    outs = pl.pallas_call(
        body, name="gather_start",
        in_specs=[_HBM] * n + [_ANY],
        out_specs=[_SEM] * (2 * n) + [_HBM] * n + [_VMEM],
        out_shape=[pltpu.SemaphoreType.DMA((3,))] * (2 * n) + [pltpu.HBM(b.shape, b.dtype) for b in bufs]
        + [jax.ShapeDtypeStruct((8, 128), F32)],
        input_output_aliases={w: 2 * n + w for w in range(n)},
        compiler_params=pltpu.CompilerParams(has_side_effects=_EFFECT),
    )(*[pltpu.with_memory_space_constraint(b, pltpu.HBM) for b in bufs], after)
    return [(outs[w], outs[n + w], outs[2 * n + w]) for w in range(n)], outs[3 * n]


def _gather_wait(send_sems, recv_sems, buf, after, name):
    def body(buf_ref, send_ref, recv_ref, after_ref, out_ref):
        x, y, c, chips = _mesh_pos()
        me = _chip_id(x, y)
        for k, chip in enumerate(chips):
            _chip_copy(buf_ref, k, me, _half_rows(buf_ref, c), send_ref, recv_ref, (*chip, c)).wait_send()
        for k, chip in enumerate(chips):
            _chip_copy(buf_ref, k, _chip_id(*chip), _half_rows(buf_ref, c), send_ref, recv_ref, (*chip, c)).wait_recv()

    return pl.pallas_call(
        body, name=name,
        in_specs=[_HBM, _SEM, _SEM, _ANY], out_specs=_HBM,
        out_shape=pltpu.HBM(buf.shape, buf.dtype),
        input_output_aliases={0: 0},
        compiler_params=pltpu.CompilerParams(has_side_effects=_EFFECT),
    )(buf, send_sems, recv_sems, after)


def _sibling_fill(buf, name, own_too=False):
    n_copies = 4 if own_too else 3

    def body(buf_ref, out_ref, send_sems, recv_sems):
        x, y, c, chips = _mesh_pos()
        sibling = (x, y, 1 - c)
        slots = [_chip_id(*chip) for chip in chips] + ([_chip_id(x, y)] if own_too else [])
        copies = []
        for k, slot in enumerate(slots):
            cp = _chip_copy(out_ref, k, slot, _half_rows(out_ref, c), send_sems, recv_sems, sibling)
            cp.start()
            copies.append(cp)
        for k, slot in enumerate(slots):
            _chip_copy(out_ref, k, slot, _half_rows(out_ref, 1 - c), send_sems, recv_sems, sibling).wait_recv()
        for cp in copies:
            cp.wait_send()

    return pl.pallas_call(
        body, name=name,
        in_specs=[_HBM], out_specs=_HBM,
        out_shape=jax.ShapeDtypeStruct(buf.shape, buf.dtype),
        input_output_aliases={0: 0},
        scratch_shapes=[pltpu.SemaphoreType.DMA((n_copies,)), pltpu.SemaphoreType.DMA((n_copies,))],
    )(buf)


def _allgather_conv_w(cw):
    _, kw, cs = cw.shape

    def body(cw_ref, out_ref, send_sems, recv_sems):
        x, y, c, chips = _mesh_pos()
        me = _chip_id(x, y)
        out_ref[pl.ds(me, 1)] = cw_ref[...]
        copies = []
        for k, chip in enumerate(chips):
            cp = pltpu.make_async_remote_copy(
                src_ref=cw_ref.at[0], dst_ref=out_ref.at[me], send_sem=send_sems.at[k], recv_sem=recv_sems.at[k],
                device_id=(*chip, c), device_id_type=MESH)
            cp.start()
            copies.append(cp)
        for k, chip in enumerate(chips):
            pltpu.make_async_remote_copy(
                src_ref=cw_ref.at[0], dst_ref=out_ref.at[_chip_id(*chip)], send_sem=send_sems.at[k],
                recv_sem=recv_sems.at[k], device_id=(*chip, c), device_id_type=MESH).wait_recv()
        for cp in copies:
            cp.wait_send()

    return pl.pallas_call(
        body, name="allgather_conv_w",
        in_specs=[_VMEM], out_specs=_VMEM,
        out_shape=jax.ShapeDtypeStruct((N_CHIPS, kw, cs), F32),
        scratch_shapes=[pltpu.SemaphoreType.DMA((3,)), pltpu.SemaphoreType.DMA((3,))],
    )(cw)


def _exchange_halves(parts, name):
    n = len(parts)
    shapes = [p.shape for p in parts]

    def body(*refs):
        ins, outs = refs[:n], refs[n:2 * n]
        send_sems, recv_sems = refs[2 * n:]
        x, y, c, _ = _mesh_pos()
        copies = []
        for w in range(n):
            hr = shapes[w][1] // 2
            cp = pltpu.make_async_remote_copy(
                src_ref=ins[w].at[:, pl.ds((1 - c) * hr, hr)], dst_ref=outs[w],
                send_sem=send_sems.at[w], recv_sem=recv_sems.at[w],
                device_id=(x, y, 1 - c), device_id_type=MESH)
            cp.start()
            copies.append(cp)
        for cp in copies:
            cp.wait()

    return pl.pallas_call(
        body, name=name,
        in_specs=[_HBM] * n, out_specs=[_HBM] * n,
        out_shape=[jax.ShapeDtypeStruct((s[0], s[1] // 2, s[2]), BF16) for s in shapes],
        scratch_shapes=[pltpu.SemaphoreType.DMA((n,)), pltpu.SemaphoreType.DMA((n,))],
    )(*parts)


def _add_halves(part, got, cvec, name):
    ns, r, cdim = part.shape
    hr = r // 2
    tr = _row_tile(hr, TR_ELT)
    nblk = hr // tr

    def body(c_ref, a_ref, b_ref, o_ref):
        o_ref[...] = (a_ref[...].astype(F32) + b_ref[...].astype(F32)).astype(BF16)

    grid_spec = pltpu.PrefetchScalarGridSpec(
        num_scalar_prefetch=1, grid=(ns, nblk),
        in_specs=[pl.BlockSpec((None, tr, cdim), lambda s, i, c_ref: (s, c_ref[0] * nblk + i, 0)),
                  pl.BlockSpec((None, tr, cdim), lambda s, i, c_ref: (s, i, 0))],
        out_specs=pl.BlockSpec((None, tr, cdim), lambda s, i, c_ref: (s, i, 0)))
    return pl.pallas_call(
        body, name=name, grid_spec=grid_spec,
        out_shape=jax.ShapeDtypeStruct((ns, hr, cdim), BF16),
        compiler_params=_params(("parallel", "parallel")),
    )(cvec, part, got)


def _scatter_copy(sums_ref, land_ref, k, src_slot, dst_slot, c, send_sems, recv_sems, to):
    return pltpu.make_async_remote_copy(
        src_ref=sums_ref.at[src_slot], dst_ref=land_ref.at[dst_slot, _half_rows(land_ref, c)],
        send_sem=send_sems.at[k], recv_sem=recv_sems.at[k], device_id=to, device_id_type=MESH)


def _scatter_start(sums, name):
    ns, hr, cdim = sums.shape
    land = lax.empty((ns, 2 * hr, cdim), sums.dtype)

    def body(sums_ref, land_ref, send_sems, recv_sems, sums_thru, land_thru):
        x, y, c, chips = _mesh_pos()
        me = _chip_id(x, y)
        for k, chip in enumerate(chips):
            _scatter_copy(sums_ref, land_ref, k, _chip_id(*chip), me, c, send_sems, recv_sems, (*chip, c)).start()

    return pl.pallas_call(
        body, name=name,
        in_specs=[_HBM, _HBM], out_specs=[_SEM, _SEM, _HBM, _HBM],
        out_shape=[pltpu.SemaphoreType.DMA((3,)), pltpu.SemaphoreType.DMA((3,)),
                   pltpu.HBM(sums.shape, sums.dtype), pltpu.HBM(land.shape, land.dtype)],
        input_output_aliases={0: 2, 1: 3},
        compiler_params=pltpu.CompilerParams(has_side_effects=_EFFECT),
    )(pltpu.with_memory_space_constraint(sums, pltpu.HBM), pltpu.with_memory_space_constraint(land, pltpu.HBM))


def _scatter_wait(send_sems, recv_sems, sums, land, after, name):
    def body(sums_ref, land_ref, send_ref, recv_ref, after_ref, sums_out, land_out):
        x, y, c, chips = _mesh_pos()
        me = _chip_id(x, y)
        for k, chip in enumerate(chips):
            _scatter_copy(sums_ref, land_ref, k, _chip_id(*chip), me, c, send_ref, recv_ref, (*chip, c)).wait_send()
        for k, chip in enumerate(chips):
            _scatter_copy(sums_ref, land_ref, k, me, _chip_id(*chip), c, send_ref, recv_ref, (*chip, c)).wait_recv()

    return pl.pallas_call(
        body, name=name,
        in_specs=[_HBM, _HBM, _SEM, _SEM, _ANY], out_specs=[_HBM, _HBM],
        out_shape=[pltpu.HBM(sums.shape, sums.dtype), pltpu.HBM(land.shape, land.dtype)],
        input_output_aliases={0: 0, 1: 1},
        compiler_params=pltpu.CompilerParams(has_side_effects=_EFFECT),
    )(sums, land, send_sems, recv_sems, after)


def _complete_chip_sums(sums, lands):
    n = len(sums)

    def body(*refs):
        sums_refs, outs = refs[:n], refs[2 * n:3 * n]
        send_sems, recv_sems, local_sems = refs[3 * n:]
        x, y, c, chips = _mesh_pos()
        me = _chip_id(x, y)
        sibling = (x, y, 1 - c)
        slots = [_chip_id(*chip) for chip in chips]
        local, sent = [], []
        for w in range(n):
            out = outs[w]
            cp = pltpu.make_async_copy(sums_refs[w].at[me], out.at[me, _half_rows(out, c)], local_sems.at[w])
            cp.start()
            local.append(cp)
            cp = _scatter_copy(sums_refs[w], out, 3, me, me, c, send_sems.at[w], recv_sems.at[w], sibling)
            cp.start()
            sent.append(cp)
            for k, slot in enumerate(slots):
                cp = _chip_copy(out, k, slot, _half_rows(out, c), send_sems.at[w], recv_sems.at[w], sibling)
                cp.start()
                sent.append(cp)
        for w in range(n):
            out = outs[w]
            _scatter_copy(sums_refs[w], out, 3, me, me, 1 - c, send_sems.at[w], recv_sems.at[w], sibling).wait_recv()
            for k, slot in enumerate(slots):
                _chip_copy(out, k, slot, _half_rows(out, 1 - c), send_sems.at[w], recv_sems.at[w], sibling).wait_recv()
        for cp in sent:
            cp.wait_send()
        for cp in local:
            cp.wait()

    return pl.pallas_call(
        body, name="complete_chip_sums",
        in_specs=[_HBM] * (2 * n), out_specs=[_HBM] * n,
        out_shape=[jax.ShapeDtypeStruct(b.shape, b.dtype) for b in lands],
        input_output_aliases={n + w: w for w in range(n)},
        scratch_shapes=[pltpu.SemaphoreType.DMA((n, 4)), pltpu.SemaphoreType.DMA((n, 4)),
                        pltpu.SemaphoreType.DMA((n,))],
    )(*sums, *lands)


SMALL_ROWS = 8


def _allreduce_small(gl2g, gl2b, gl1g, gl1b, g_ac, gcw, gsink, loss):
    d = gl2g.shape[1]
    hd = d // 2
    nq = gsink.shape[1]

    def body(a_ref, b_ref, c_ref, d_ref, e_ref, cw_ref, sk_ref, ls_ref, out_ref, mine, gath, send_sems, recv_sems):
        x, y, c, _ = _mesh_pos()
        me = 4 * x + 2 * y + c
        mine[...] = jnp.zeros_like(mine)
        mine[0:1, :] = a_ref[...]
        mine[1:2, :] = b_ref[...]
        mine[2:3, :] = c_ref[...]
        mine[3:4, :] = d_ref[...]
        mine[4:5, :] = e_ref[...]
        mine[5:6, 0:hd] = cw_ref[0:1, :]
        mine[5:6, hd:d] = cw_ref[1:2, :]
        mine[6:7, 0:hd] = cw_ref[2:3, :]
        mine[6:7, hd:hd + nq] = sk_ref[...]
        mine[6:7, hd + 128:hd + 256] = ls_ref[...]
        gath[pl.ds(me, 1)] = mine[...][None]
        copies = []
        for r in range(1, 8):
            peer = ((1 - x) if r & 4 else x, (1 - y) if r & 2 else y, (1 - c) if r & 1 else c)
            cp = pltpu.make_async_remote_copy(
                src_ref=mine, dst_ref=gath.at[me], send_sem=send_sems.at[r - 1], recv_sem=recv_sems.at[r - 1],
                device_id=peer, device_id_type=MESH)
            cp.start()
            copies.append(cp)
        for r in range(1, 8):
            peer = ((1 - x) if r & 4 else x, (1 - y) if r & 2 else y, (1 - c) if r & 1 else c)
            peer_id = 4 * peer[0] + 2 * peer[1] + peer[2]
            pltpu.make_async_remote_copy(
                src_ref=mine, dst_ref=gath.at[peer_id], send_sem=send_sems.at[r - 1], recv_sem=recv_sems.at[r - 1],
                device_id=peer, device_id_type=MESH).wait_recv()
        for cp in copies:
            cp.wait_send()
        total = gath[0]
        for dev in range(1, 8):
            total = total + gath[dev]
        out_ref[...] = total

    return pl.pallas_call(
        body, name="allreduce_small",
        in_specs=[_VMEM] * 8, out_specs=_VMEM,
        out_shape=jax.ShapeDtypeStruct((SMALL_ROWS, d), F32),
        scratch_shapes=[pltpu.VMEM((SMALL_ROWS, d), F32), pltpu.VMEM((8, SMALL_ROWS, d), F32),
                        pltpu.SemaphoreType.DMA((7,)), pltpu.SemaphoreType.DMA((7,))],
    )(gl2g, gl2b, gl1g, gl1b, g_ac, gcw, gsink, loss)


def _adamw(w, g, m, v):
    m = ADAM_B1 * m + (1.0 - ADAM_B1) * g
    v = ADAM_B2 * v + (1.0 - ADAM_B2) * (g * g)
    m_hat = m / (1.0 - ADAM_B1 ** ADAM_STEP)
    v_hat = v / (1.0 - ADAM_B2 ** ADAM_STEP)
    delta = -ADAM_LR * (m_hat / (jnp.sqrt(v_hat) + ADAM_EPS) + ADAM_WD * w)
    return delta, m, v


def _adamw_shard(w, m, v, sums, name):
    _, r, c = w.shape
    tr = _row_tile(r, TR_ELT)

    def body(w_ref, m_ref, v_ref, s_ref, g_out, d_out, m_out, v_out):
        g = s_ref[0].astype(F32)
        for s in range(1, N_CHIPS):
            g = g + s_ref[s].astype(F32)
        delta, nm, nv = _adamw(w_ref[...], g, m_ref[...], v_ref[...])
        g_out[...] = g
        d_out[...] = delta
        m_out[...] = nm
        v_out[...] = nv

    blk = pl.BlockSpec((None, tr, c), lambda i: (0, i, 0))
    return pl.pallas_call(
        body, name=name, grid=(r // tr,),
        in_specs=[blk, blk, blk, pl.BlockSpec((N_CHIPS, tr, c), lambda i: (0, i, 0))],
        out_specs=[blk] * 4,
        out_shape=[jax.ShapeDtypeStruct((1, r, c), F32)] * 4,
        compiler_params=_params(("parallel",)),
    )(w, m, v, sums)


def _adamw_small(red, params):
    names = ["sinks", "g_attn", "g_conv", "ln1_g", "ln1_b", "ln2_g", "ln2_b", "conv_w"]
    d = red.shape[1]
    hd = d // 2
    flat = []
    for nme in names:
        flat.extend(params[nme])
    nq = params["sinks"][0].shape[1]
    cs = params["conv_w"][0].shape[2]

    def body(*refs):
        red_ref = refs[0]
        ins = refs[1:1 + 3 * len(names)]
        outs = refs[1 + 3 * len(names):]
        x, y, _, _ = _mesh_pos()
        me = _chip_id(x, y)

        def conv_tap(row, base):
            picked = red_ref[row:row + 1, base:base + cs]
            for s in range(1, N_CHIPS):
                picked = jnp.where(me == s, red_ref[row:row + 1, base + s * cs:base + (s + 1) * cs], picked)
            return picked

        grads = {
            "sinks": red_ref[6:7, hd:hd + nq],
            "g_attn": red_ref[4:5, 0:hd],
            "g_conv": red_ref[4:5, hd:d],
            "ln1_g": red_ref[2:3, :],
            "ln1_b": red_ref[3:4, :],
            "ln2_g": red_ref[0:1, :],
            "ln2_b": red_ref[1:2, :],
        }
        for i, nme in enumerate(names):
            w_ref, m_ref, v_ref = ins[3 * i:3 * i + 3]
            g_out, d_out, m_out, v_out = outs[4 * i:4 * i + 4]
            if nme == "conv_w":
                for tap, (row, base) in enumerate([(5, 0), (5, hd), (6, 0)]):
                    g = conv_tap(row, base)
                    delta, nm, nv = _adamw(w_ref[0, tap:tap + 1, :], g, m_ref[0, tap:tap + 1, :], v_ref[0, tap:tap + 1, :])
                    g_out[0, tap:tap + 1, :] = g
                    d_out[0, tap:tap + 1, :] = delta
                    m_out[0, tap:tap + 1, :] = nm
                    v_out[0, tap:tap + 1, :] = nv
            else:
                g = grads[nme]
                delta, nm, nv = _adamw(w_ref[...], g, m_ref[...], v_ref[...])
                g_out[...] = g
                d_out[...] = delta
                m_out[...] = nm
                v_out[...] = nv

    out_shape = []
    for nme in names:
        out_shape.extend([jax.ShapeDtypeStruct(params[nme][0].shape, F32)] * 4)
    outs = pl.pallas_call(
        body, name="adamw_small",
        in_specs=[_VMEM] * (1 + len(flat)), out_specs=[_VMEM] * len(out_shape),
        out_shape=out_shape,
    )(red, *flat)
    return {nme: tuple(outs[4 * i:4 * i + 4]) for i, nme in enumerate(names)}


def _rope_tables(pos_col):
    s = pos_col.shape[0]
    w = N_KV_HEADS * HEAD_DIM
    tb = min(512, s)
    inv_freq = (ROPE_THETA ** (-np.arange(0, ROT_DIM, 2, dtype=np.float32) / ROT_DIM)).astype(np.float32)

    def body(pos_ref, cos_ref, sin_ref):
        pos = pos_ref[...].astype(F32)
        lane = lax.broadcasted_iota(jnp.int32, (tb, w), 1) & (HEAD_DIM - 1)
        fidx = lane & (ROT_DIM // 2 - 1)
        inv = jnp.zeros((tb, w), F32)
        for k in range(ROT_DIM // 2):
            inv = jnp.where(fidx == k, float(inv_freq[k]), inv)
        ang = pos * inv
        rot = lane < ROT_DIM
        cos_ref[...] = jnp.where(rot, jnp.cos(ang), 1.0)
        sin_v = jnp.sin(ang)
        sin_ref[...] = jnp.where(lane < ROT_DIM // 2, -sin_v, jnp.where(rot, sin_v, 0.0))

    return pl.pallas_call(
        body, name="rope_tables", grid=(s // tb,),
        in_specs=[pl.BlockSpec((tb, 1), lambda i: (i, 0))],
        out_specs=[pl.BlockSpec((tb, w), lambda i: (i, 0))] * 2,
        out_shape=[jax.ShapeDtypeStruct((s, w), F32)] * 2,
        compiler_params=_params(("parallel",)),
    )(pos_col)


def _in_proj(x, w_in_g):
    _, s, d = x.shape
    ns, _, ncol = w_in_g.shape
    tm = min(TM, s)

    def body(x_ref, w_ref, o_ref):
        o_ref[...] = _dot(x_ref[...].astype(BF16), w_ref[...])

    return pl.pallas_call(
        body, name="in_proj", grid=(s // tm, ns),
        in_specs=[pl.BlockSpec((None, tm, d), lambda i, j: (0, i, 0)),
                  pl.BlockSpec((None, d, ncol), lambda i, j: (j, 0, 0))],
        out_specs=pl.BlockSpec((tm, ncol), lambda i, j: (i, j)),
        out_shape=jax.ShapeDtypeStruct((s, ns * ncol), F32),
        compiler_params=_params(("parallel", "arbitrary")),
    )(x, w_in_g)


def _attention_scores(q, k_prev, k_cur, sinks_ref, h, first, valid):
    heads = [q[:, (GROUP * h + g) * HEAD_DIM:(GROUP * h + g + 1) * HEAD_DIM] for g in range(GROUP)]
    q4 = jnp.concatenate(heads, axis=0).astype(BF16)
    kk = jnp.concatenate([k_prev[:, h * HEAD_DIM:(h + 1) * HEAD_DIM], k_cur[:, h * HEAD_DIM:(h + 1) * HEAD_DIM]],
                         axis=0).astype(BF16)
    s = _dot_nt(q4, kk) * ATTN_SCALE
    s = jnp.where(valid, s, NEG_BIG)
    sink = jnp.concatenate(
        [jnp.broadcast_to(sinks_ref[0:1, GROUP * h + g:GROUP * h + g + 1], (WINDOW, 1)) for g in range(GROUP)], axis=0)
    m = jnp.maximum(jnp.max(s, axis=1, keepdims=True), sink)
    p = jnp.exp(s - m)
    p_sink = jnp.exp(sink - m)
    inv_l = 1.0 / (jnp.sum(p, axis=1, keepdims=True) + p_sink)
    return q4, kk, p * inv_l, p_sink * inv_l


def _band_mask(first):
    rows = GROUP * WINDOW
    qi = lax.broadcasted_iota(jnp.int32, (rows, 2 * WINDOW), 0) & (WINDOW - 1)
    kj = lax.broadcasted_iota(jnp.int32, (rows, 2 * WINDOW), 1)
    rel = qi + WINDOW - kj
    band = (rel >= 0) & (rel < WINDOW)
    return band & jnp.logical_not(first & (kj < WINDOW))


def _attention_fwd(proj, cos_t, sin_t, sinks):
    s = proj.shape[0]
    qw = GROUP * N_KV_HEADS * HEAD_DIM
    kvw = N_KV_HEADS * HEAD_DIM
    nb = s // WINDOW

    def body(cur_ref, prev_ref, cos_ref, sin_ref, cosp_ref, sinp_ref, sinks_ref, o_ref):
        n = pl.program_id(0)
        first = n == 0
        cur = cur_ref[...]
        cos, sin = cos_ref[...], sin_ref[...]
        q = _rope(cur[:, :qw], _tile_lanes(cos, GROUP), _tile_lanes(sin, GROUP), 1.0)
        k_cur = _rope(cur[:, qw:qw + kvw], cos, sin, 1.0)
        v_cur = cur[:, qw + kvw:]
        prev = prev_ref[...]
        k_prev = _rope(prev[:, :kvw], cosp_ref[...], sinp_ref[...], 1.0)
        v_prev = prev[:, kvw:]
        valid = _band_mask(first)
        outs = []
        for h in range(N_KV_HEADS):
            _, _, probs, _ = _attention_scores(q, k_prev, k_cur, sinks_ref, h, first, valid)
            vv = jnp.concatenate([v_prev[:, h * HEAD_DIM:(h + 1) * HEAD_DIM], v_cur[:, h * HEAD_DIM:(h + 1) * HEAD_DIM]],
                                 axis=0).astype(BF16)
            o = _dot(probs.astype(BF16), vv)
            outs.extend([o[g * WINDOW:(g + 1) * WINDOW] for g in range(GROUP)])
        o_ref[...] = jnp.concatenate(outs, axis=1)

    tbl = pl.BlockSpec((WINDOW, kvw), lambda n: (n, 0))
    tbl_prev = pl.BlockSpec((WINDOW, kvw), lambda n: (jnp.maximum(n - 1, 0), 0))
    return pl.pallas_call(
        body, name="attention_fwd", grid=(nb,),
        in_specs=[pl.BlockSpec((WINDOW, qw + 2 * kvw), lambda n: (n, 0)),
                  pl.BlockSpec((WINDOW, 2 * kvw), lambda n: (jnp.maximum(n - 1, 0), (qw // (2 * kvw)))),
                  tbl, tbl, tbl_prev, tbl_prev, _VMEM],
        out_specs=pl.BlockSpec((WINDOW, qw), lambda n: (n, 0)),
        out_shape=jax.ShapeDtypeStruct((s, qw), F32),
        compiler_params=_params(("parallel",)),
    )(proj, proj, cos_t, sin_t, cos_t, sin_t, sinks)


def _conv_taps(cw_ref):
    return [jnp.concatenate([cw_ref[s, k:k + 1, :] for s in range(N_CHIPS)], axis=1) for k in range(3)]


def _shift_down(z, halo, steps):
    rows = z.shape[0]
    row = lax.broadcasted_iota(jnp.int32, z.shape, 0)
    out = pltpu.roll(z, steps, 0)
    for r in range(steps):
        out = jnp.where(row == r, halo[8 - steps + r:8 - steps + r + 1, :], out)
    return out


def _shift_up(z, halo, steps):
    rows = z.shape[0]
    row = lax.broadcasted_iota(jnp.int32, z.shape, 0)
    out = pltpu.roll(z, rows - steps, 0)
    for r in range(steps):
        out = jnp.where(row == rows - steps + r, halo[r:r + 1, :], out)
    return out


def _split_cbu(lo, hi, cw):
    c_gate = lo[:, :cw]
    b_gate = jnp.concatenate([lo[:, cw:], hi[:, :2 * cw - lo.shape[1]]], axis=1)
    u = hi[:, 2 * cw - lo.shape[1]:]
    return c_gate, b_gate, u


def _conv_norm(proj, attn, cw_full, g_ac):
    s, in_w = proj.shape
    cw = attn.shape[1]
    blk_w = in_w // 3
    tb = min(TB_CONV, s)

    def body(lo_ref, hi_ref, lo_h_ref, hi_h_ref, attn_ref, cw_ref, g_ref, mixed_ref, ac_ref, rstd_ref):
        i = pl.program_id(0)
        c_gate, b_gate, u = _split_cbu(lo_ref[...], hi_ref[...], cw)
        c_h, _, u_h = _split_cbu(lo_h_ref[...], hi_h_ref[...], cw)
        z = c_gate * u
        z_h = jnp.where(i == 0, 0.0, c_h * u_h)
        w0, w1, w2 = _conv_taps(cw_ref)
        y = w0 * _shift_down(z, z_h, 2) + w1 * _shift_down(z, z_h, 1) + w2 * z
        conv = b_gate * y
        a = attn_ref[...]
        r_a = lax.rsqrt(jnp.mean(a * a, axis=-1, keepdims=True) + RMS_EPS)
        r_c = lax.rsqrt(jnp.mean(conv * conv, axis=-1, keepdims=True) + RMS_EPS)
        g = g_ref[...]
        mixed_ref[...] = jnp.concatenate([a * r_a * g[:, :cw], conv * r_c * g[:, cw:]], axis=1).astype(BF16)
        ac_ref[...] = jnp.concatenate([a, conv], axis=1)
        rstd_ref[0] = r_a
        rstd_ref[1] = r_c

    halo_idx = lambda i: jnp.maximum(i * (tb // 8) - 1, 0)
    return pl.pallas_call(
        body, name="conv_norm", grid=(s // tb,),
        in_specs=[pl.BlockSpec((tb, blk_w), lambda i: (i, 1)),
                  pl.BlockSpec((tb, blk_w), lambda i: (i, 2)),
                  pl.BlockSpec((8, blk_w), lambda i: (halo_idx(i), 1)),
                  pl.BlockSpec((8, blk_w), lambda i: (halo_idx(i), 2)),
                  pl.BlockSpec((tb, cw), lambda i: (i, 0)),
                  _VMEM, _VMEM],
        out_specs=[pl.BlockSpec((tb, 2 * cw), lambda i: (i, 0)),
                   pl.BlockSpec((tb, 2 * cw), lambda i: (i, 0)),
                   pl.BlockSpec((2, tb, 1), lambda i: (0, i, 0))],
        out_shape=[jax.ShapeDtypeStruct((s, 2 * cw), BF16), jax.ShapeDtypeStruct((s, 2 * cw), F32),
                   jax.ShapeDtypeStruct((2, s, 1), F32)],
        compiler_params=_params(("parallel",)),
    )(proj, proj, proj, proj, attn, cw_full, g_ac)


def _out_proj_ln(mixed, w_out_g, x, ln_g, ln_b):
    s, d = mixed.shape
    tm = min(TM, s)
    tk = min(512, d)
    nk = d // tk

    def body(a_ref, w_ref, x_ref, g_ref, b_ref, xhat_ref, h_ref, rstd_ref, acc):
        k = pl.program_id(1)

        @pl.when(k == 0)
        def _():
            acc[...] = jnp.zeros_like(acc)

        acc[...] += _dot(a_ref[...], w_ref[...])

        @pl.when(k == nk - 1)
        def _():
            def rows_fn(rows):
                xhat, rstd = _ln_fwd(ALPHA * x_ref[rows, :] + acc[rows, :])
                xhat_ref[rows, :] = xhat
                h_ref[rows, :] = (xhat * g_ref[...] + b_ref[...]).astype(BF16)
                rstd_ref[rows, :] = rstd

            _for_row_chunks(tm, rows_fn)

    row = pl.BlockSpec((tm, d), lambda i, k: (i, 0))
    return pl.pallas_call(
        body, name="out_proj_ln", grid=(s // tm, nk),
        in_specs=[pl.BlockSpec((tm, tk), lambda i, k: (i, k)),
                  pl.BlockSpec((tk, d), lambda i, k: (k, 0)),
                  pl.BlockSpec((None, tm, d), lambda i, k: (0, i, 0)),
                  _VMEM, _VMEM],
        out_specs=[row, row, pl.BlockSpec((tm, 1), lambda i, k: (i, 0))],
        out_shape=[jax.ShapeDtypeStruct((s, d), F32), jax.ShapeDtypeStruct((s, d), BF16),
                   jax.ShapeDtypeStruct((s, 1), F32)],
        scratch_shapes=[pltpu.VMEM((tm, d), F32)],
        compiler_params=_params(("parallel", "arbitrary")),
    )(mixed, w_out_g, x, ln_g, ln_b)


def _gate_up(h1, w_gate_g, w_up_g):
    s, d = h1.shape
    ns, _, fs = w_gate_g.shape
    tm = min(TM, s)
    tk = min(1024, d)
    nk = d // tk

    def body(h_ref, wg_ref, wu_ref, act_ref, g_ref, u_ref, acc_g, acc_u):
        k = pl.program_id(2)

        @pl.when(k == 0)
        def _():
            acc_g[...] = jnp.zeros_like(acc_g)
            acc_u[...] = jnp.zeros_like(acc_u)

        h = h_ref[...]
        acc_g[...] += _dot(h, wg_ref[...])
        acc_u[...] += _dot(h, wu_ref[...])

        @pl.when(k == nk - 1)
        def _():
            def rows_fn(rows):
                g, u = acc_g[rows, :], acc_u[rows, :]
                act_ref[rows, :] = (g * _sigmoid(g) * u).astype(BF16)
                g_ref[rows, :] = g.astype(BF16)
                u_ref[rows, :] = u.astype(BF16)

            _for_row_chunks(tm, rows_fn)

    wspec = pl.BlockSpec((None, tk, fs), lambda i, j, k: (j, k, 0))
    ospec = pl.BlockSpec((tm, fs), lambda i, j, k: (i, j))
    return pl.pallas_call(
        body, name="gate_up", grid=(s // tm, ns, nk),
        in_specs=[pl.BlockSpec((tm, tk), lambda i, j, k: (i, k)), wspec, wspec],
        out_specs=[ospec] * 3,
        out_shape=[jax.ShapeDtypeStruct((s, ns * fs), BF16)] * 3,
        scratch_shapes=[pltpu.VMEM((tm, fs), F32)] * 2,
        compiler_params=_params(("parallel", "arbitrary", "arbitrary")),
    )(h1, w_gate_g, w_up_g)


def _down_ln_loss(act, w_down_g, xhat1, ln1_g, ln1_b, ln2_g, ln2_b, target):
    s, f = act.shape
    d = xhat1.shape[1]
    tm = min(TM, s)
    tk = 512
    nk = f // tk

    def body(a_ref, w_ref, xh_ref, g1_ref, b1_ref, g2_ref, b2_ref, t_ref, dpre_ref, loss_ref, gg_ref, gb_ref, acc):
        i, k = pl.program_id(0), pl.program_id(1)

        @pl.when(k == 0)
        def _():
            acc[...] = jnp.zeros_like(acc)

        acc[...] += _dot(a_ref[...], w_ref[...])

        @pl.when(k == nk - 1)
        def _():
            @pl.when(i == 0)
            def _():
                loss_ref[...] = jnp.zeros_like(loss_ref)
                gg_ref[...] = jnp.zeros_like(gg_ref)
                gb_ref[...] = jnp.zeros_like(gb_ref)

            def rows_fn(rows):
                h1 = xh_ref[rows, :] * g1_ref[...] + b1_ref[...]
                xhat, rstd = _ln_fwd(ALPHA * h1 + acc[rows, :])
                g2 = g2_ref[...]
                diff = xhat * g2 + b2_ref[...] - t_ref[rows, :]
                dy = diff * (1.0 / d)
                dpre_ref[rows, :] = _ln_bwd(dy, xhat, rstd, g2)
                sq = jnp.sum(jnp.sum(diff * diff, axis=1, keepdims=True), axis=0, keepdims=True)
                loss_ref[...] += jnp.broadcast_to(sq * (0.5 / d), (1, 128))
                gg_ref[...] += jnp.sum(dy * xhat, axis=0, keepdims=True)
                gb_ref[...] += jnp.sum(dy, axis=0, keepdims=True)

            _for_row_chunks(tm, rows_fn)

    row = pl.BlockSpec((tm, d), lambda i, k: (i, 0))
    vec = pl.BlockSpec((1, d), lambda i, k: (0, 0))
    return pl.pallas_call(
        body, name="down_ln_loss", grid=(s // tm, nk),
        in_specs=[pl.BlockSpec((tm, tk), lambda i, k: (i, k)),
                  pl.BlockSpec((tk, d), lambda i, k: (k, 0)),
                  row, _VMEM, _VMEM, _VMEM, _VMEM,
                  pl.BlockSpec((None, tm, d), lambda i, k: (0, i, 0))],
        out_specs=[row, pl.BlockSpec((1, 128), lambda i, k: (0, 0)), vec, vec],
        out_shape=[jax.ShapeDtypeStruct((s, d), F32), jax.ShapeDtypeStruct((1, 128), F32),
                   jax.ShapeDtypeStruct((1, d), F32), jax.ShapeDtypeStruct((1, d), F32)],
        scratch_shapes=[pltpu.VMEM((tm, d), F32)],
        compiler_params=_params(("arbitrary", "arbitrary")),
    )(act, w_down_g, xhat1, ln1_g, ln1_b, ln2_g, ln2_b, target)


def _dact_silu_bwd(dpre2, w_down_g, gate, up):
    s, d = dpre2.shape
    f = gate.shape[1]
    fs = f // N_CHIPS
    tm = min(TM, s)

    def body(dp_ref, w_ref, g_ref, u_ref, dg_ref, du_ref):
        d_act = _dot_nt(dp_ref[...].astype(BF16), w_ref[...])
        g = g_ref[...].astype(F32)
        u = u_ref[...].astype(F32)
        sg = _sigmoid(g)
        dg_ref[...] = (d_act * u * (sg * (1.0 + g * (1.0 - sg)))).astype(BF16)
        du_ref[...] = (d_act * (g * sg)).astype(BF16)

    blk = pl.BlockSpec((tm, fs), lambda i, j: (i, j))
    return pl.pallas_call(
        body, name="dact_silu_bwd", grid=(s // tm, N_CHIPS),
        in_specs=[pl.BlockSpec((tm, d), lambda i, j: (i, 0)),
                  pl.BlockSpec((fs, d), lambda i, j: (j, 0)), blk, blk],
        out_specs=[blk, blk],
        out_shape=[jax.ShapeDtypeStruct((s, f), BF16)] * 2,
        compiler_params=_params(("parallel", "arbitrary")),
    )(dpre2, w_down_g, gate, up)


def _grad_rows(a, b, name, row_blocks=1):
    s, m = a.shape
    n = b.shape[1]
    ms = m // N_CHIPS
    tmw = ms // row_blocks
    tk = min(TK_TOK, s)
    nk = s // tk

    def body(a_ref, b_ref, o_ref, acc):
        k = pl.program_id(2)

        @pl.when(k == 0)
        def _():
            acc[...] = jnp.zeros_like(acc)

        acc[...] += _dot_tn(a_ref[...].astype(BF16), b_ref[...].astype(BF16))

        @pl.when(k == nk - 1)
        def _():
            o_ref[...] = acc[...].astype(BF16)

    return pl.pallas_call(
        body, name=name, grid=(N_CHIPS, row_blocks, nk),
        in_specs=[pl.BlockSpec((tk, tmw), lambda j, r, k: (k, j * row_blocks + r)),
                  pl.BlockSpec((tk, n), lambda j, r, k: (k, 0))],
        out_specs=pl.BlockSpec((None, tmw, n), lambda j, r, k: (j, r, 0)),
        out_shape=jax.ShapeDtypeStruct((N_CHIPS, ms, n), BF16),
        scratch_shapes=[pltpu.VMEM((tmw, n), F32)],
        compiler_params=_params(("parallel", "parallel", "arbitrary")),
    )(a, b)


def _grad_cols(a, bs, name, a_3d=False, row_blocks=2):
    s, m = a.shape[-2:]
    n = bs[0].shape[1]
    ns = n // N_CHIPS
    nb = len(bs)
    tmw = m // row_blocks
    tk = min(TK_TOK, s)
    nk = s // tk

    def body(*refs):
        a_ref, b_refs, o_refs, accs = refs[0], refs[1:1 + nb], refs[1 + nb:1 + 2 * nb], refs[1 + 2 * nb:]
        k = pl.program_id(2)

        @pl.when(k == 0)
        def _():
            for acc in accs:
                acc[...] = jnp.zeros_like(acc)

        at = a_ref[...].astype(BF16)
        for b_ref, acc in zip(b_refs, accs):
            acc[...] += _dot_tn(at, b_ref[...].astype(BF16))

        @pl.when(k == nk - 1)
        def _():
            for o_ref, acc in zip(o_refs, accs):
                o_ref[...] = acc[...].astype(BF16)

    if a_3d:
        a_spec = pl.BlockSpec((None, tk, tmw), lambda j, r, k: (0, k, r))
    else:
        a_spec = pl.BlockSpec((tk, tmw), lambda j, r, k: (k, r))
    return pl.pallas_call(
        body, name=name, grid=(N_CHIPS, row_blocks, nk),
        in_specs=[a_spec] + [pl.BlockSpec((tk, ns), lambda j, r, k: (k, j))] * nb,
        out_specs=[pl.BlockSpec((None, tmw, ns), lambda j, r, k: (j, r, 0))] * nb,
        out_shape=[jax.ShapeDtypeStruct((N_CHIPS, m, ns), BF16)] * nb,
        scratch_shapes=[pltpu.VMEM((tmw, ns), F32)] * nb,
        compiler_params=_params(("parallel", "parallel", "arbitrary")),
    )(a, *bs)


def _dh1_ln_bwd(d_gate, d_up, w_gate_g, w_up_g, dpre2, xhat1, rstd1, ln1_g, tm_max=256):
    s, f = d_gate.shape
    d = dpre2.shape[1]
    fs = f // N_CHIPS
    tm = min(tm_max, s)

    def body(dg_ref, du_ref, wg_ref, wu_ref, dp2_ref, xh_ref, rs_ref, g_ref, dpre_ref, gg_ref, gb_ref, acc):
        i, j = pl.program_id(0), pl.program_id(1)

        @pl.when(j == 0)
        def _():
            acc[...] = jnp.zeros_like(acc)

        acc[...] += _dot_nt(dg_ref[...], wg_ref[...]) + _dot_nt(du_ref[...], wu_ref[...])

        @pl.when(j == N_CHIPS - 1)
        def _():
            @pl.when(i == 0)
            def _():
                gg_ref[...] = jnp.zeros_like(gg_ref)
                gb_ref[...] = jnp.zeros_like(gb_ref)

            def rows_fn(rows):
                dh = acc[rows, :] + ALPHA * dp2_ref[rows, :]
                xhat = xh_ref[rows, :]
                dpre_ref[rows, :] = _ln_bwd(dh, xhat, rs_ref[rows, :], g_ref[...])
                gg_ref[...] += jnp.sum(dh * xhat, axis=0, keepdims=True)
                gb_ref[...] += jnp.sum(dh, axis=0, keepdims=True)

            _for_row_chunks(tm, rows_fn)

    row = pl.BlockSpec((tm, d), lambda i, j: (i, 0))
    vec = pl.BlockSpec((1, d), lambda i, j: (0, 0))
    act_blk = pl.BlockSpec((tm, fs), lambda i, j: (i, j))
    w_blk = pl.BlockSpec((None, d, fs), lambda i, j: (j, 0, 0))
    return pl.pallas_call(
        body, name="dh1_ln_bwd", grid=(s // tm, N_CHIPS),
        in_specs=[act_blk, act_blk, w_blk, w_blk, row, row, pl.BlockSpec((tm, 1), lambda i, j: (i, 0)), _VMEM],
        out_specs=[row, vec, vec],
        out_shape=[jax.ShapeDtypeStruct((s, d), F32), jax.ShapeDtypeStruct((1, d), F32),
                   jax.ShapeDtypeStruct((1, d), F32)],
        scratch_shapes=[pltpu.VMEM((tm, d), F32)],
        compiler_params=_params(("arbitrary", "arbitrary")),
    )(d_gate, d_up, w_gate_g, w_up_g, dpre2, xhat1, rstd1, ln1_g)


def _dmixed_rms_bwd(dpre1, w_out_g, ac, rstd, g_ac):
    s, d = dpre1.shape
    hd = d // 2
    tm = min(TM, s)

    def body(dp_ref, w_ref, ac_ref, rs_ref, g_ref, dac_ref, gg_ref):
        i = pl.program_id(1)
        dm = _dot_nt(dp_ref[...].astype(BF16), w_ref[...])
        pre = ac_ref[...]
        r = rs_ref[...]
        gdm = dm * g_ref[...]
        dac_ref[...] = r * gdm - pre * (r * r * r) * jnp.mean(gdm * pre, axis=-1, keepdims=True)
        gg = jnp.sum(dm * pre * r, axis=0, keepdims=True)

        @pl.when(i == 0)
        def _():
            gg_ref[...] = gg

        @pl.when(i > 0)
        def _():
            gg_ref[...] += gg

    return pl.pallas_call(
        body, name="dmixed_rms_bwd", grid=(2, s // tm),
        in_specs=[pl.BlockSpec((tm, d), lambda h, i: (i, 0)),
                  pl.BlockSpec((hd, d), lambda h, i: (h, 0)),
                  pl.BlockSpec((tm, hd), lambda h, i: (i, h)),
                  pl.BlockSpec((None, tm, 1), lambda h, i: (h, i, 0)),
                  pl.BlockSpec((1, hd), lambda h, i: (0, h))],
        out_specs=[pl.BlockSpec((tm, hd), lambda h, i: (i, h)),
                   pl.BlockSpec((1, hd), lambda h, i: (0, h))],
        out_shape=[jax.ShapeDtypeStruct((s, d), F32), jax.ShapeDtypeStruct((1, d), F32)],
        compiler_params=_params(("arbitrary", "arbitrary")),
    )(dpre1, w_out_g, ac, rstd, g_ac)


def _attention_bwd(proj, d_ac, cos_t, sin_t, sinks):
    s = proj.shape[0]
    qw = GROUP * N_KV_HEADS * HEAD_DIM
    kvw = N_KV_HEADS * HEAD_DIM
    nb = s // WINDOW
    nq = GROUP * N_KV_HEADS

    def body(cur_ref, prev_ref, do_ref, cos_ref, sin_ref, cosp_ref, sinp_ref, sinks_ref,
             dq_ref, dcur_ref, dprev_ref, dsink_ref):
        n = pl.program_id(0)
        first = n == 0
        cur = cur_ref[...]
        cos, sin = cos_ref[...], sin_ref[...]
        cos_q, sin_q = _tile_lanes(cos, GROUP), _tile_lanes(sin, GROUP)
        q = _rope(cur[:, :qw], cos_q, sin_q, 1.0)
        k_cur = _rope(cur[:, qw:qw + kvw], cos, sin, 1.0)
        v_cur = cur[:, qw + kvw:]
        prev = prev_ref[...]
        k_prev = _rope(prev[:, :kvw], cosp_ref[...], sinp_ref[...], 1.0)
        v_prev = prev[:, kvw:]
        d_out = do_ref[...]
        valid = _band_mask(first)
        dq_parts, dk_parts, dv_parts, dsink_parts = [], [], [], []
        for h in range(N_KV_HEADS):
            q4, kk, probs, p_sink = _attention_scores(q, k_prev, k_cur, sinks_ref, h, first, valid)
            vv = jnp.concatenate([v_prev[:, h * HEAD_DIM:(h + 1) * HEAD_DIM], v_cur[:, h * HEAD_DIM:(h + 1) * HEAD_DIM]],
                                 axis=0).astype(BF16)
            do4 = jnp.concatenate(
                [d_out[:, (GROUP * h + g) * HEAD_DIM:(GROUP * h + g + 1) * HEAD_DIM] for g in range(GROUP)],
                axis=0).astype(BF16)
            d_probs = _dot_nt(do4, vv)
            delta = jnp.sum(probs * d_probs, axis=1, keepdims=True)
            d_s = (probs * (d_probs - delta) * ATTN_SCALE).astype(BF16)
            dq4 = _dot(d_s, kk)
            dq_parts.extend([dq4[g * WINDOW:(g + 1) * WINDOW] for g in range(GROUP)])
            dk_parts.append(_dot_tn(d_s, q4))
            dv_parts.append(_dot_tn(probs.astype(BF16), do4))
            ds_sink = -p_sink * delta
            dsink_parts.extend([jnp.sum(ds_sink[g * WINDOW:(g + 1) * WINDOW], axis=0, keepdims=True)
                                for g in range(GROUP)])
        dq_ref[...] = _rope(jnp.concatenate(dq_parts, axis=1), cos_q, sin_q, -1.0)
        dk = jnp.concatenate(dk_parts, axis=1)
        dv = jnp.concatenate(dv_parts, axis=1)
        dprev_ref[...] = jnp.concatenate([dk[:WINDOW], dv[:WINDOW]], axis=1)
        dcur_ref[...] = jnp.concatenate([dk[WINDOW:], dv[WINDOW:]], axis=1)
        dsink = jnp.concatenate(dsink_parts, axis=1)

        @pl.when(first)
        def _():
            dsink_ref[...] = dsink

        @pl.when(n > 0)
        def _():
            dsink_ref[...] += dsink

    tbl = pl.BlockSpec((WINDOW, kvw), lambda n: (n, 0))
    tbl_prev = pl.BlockSpec((WINDOW, kvw), lambda n: (jnp.maximum(n - 1, 0), 0))
    kv_blk = pl.BlockSpec((WINDOW, 2 * kvw), lambda n: (n, 0))
    return pl.pallas_call(
        body, name="attention_bwd", grid=(nb,),
        in_specs=[pl.BlockSpec((WINDOW, qw + 2 * kvw), lambda n: (n, 0)),
                  pl.BlockSpec((WINDOW, 2 * kvw), lambda n: (jnp.maximum(n - 1, 0), (qw // (2 * kvw)))),
                  pl.BlockSpec((WINDOW, qw), lambda n: (n, 0)),
                  tbl, tbl, tbl_prev, tbl_prev, _VMEM],
        out_specs=[pl.BlockSpec((WINDOW, qw), lambda n: (n, 0)), kv_blk, kv_blk,
                   pl.BlockSpec((1, nq), lambda n: (0, 0))],
        out_shape=[jax.ShapeDtypeStruct((s, qw), F32), jax.ShapeDtypeStruct((s, 2 * kvw), F32),
                   jax.ShapeDtypeStruct((s, 2 * kvw), F32), jax.ShapeDtypeStruct((1, nq), F32)],
        compiler_params=_params(("arbitrary",)),
    )(proj, proj, d_ac, cos_t, sin_t, cos_t, sin_t, sinks)


def _dproj_assemble(proj, d_ac, dq, dkv_cur, dkv_prev, cos_t, sin_t, cw_full):
    s, in_w = proj.shape
    cw = dq.shape[1]
    kvw = N_KV_HEADS * HEAD_DIM
    blk_w = in_w // 3
    tb = WINDOW
    nb = s // tb

    def body(lo_ref, hi_ref, lo_p_ref, hi_p_ref, lo_n_ref, hi_n_ref, dconv_ref, dconv_n_ref,
             dq_ref, dcur_ref, dprev_n_ref, cos_ref, sin_ref, cw_ref, dproj_ref, gcw_ref):
        i = pl.program_id(0)
        last = i == nb - 1
        c_gate, b_gate, u = _split_cbu(lo_ref[...], hi_ref[...], cw)
        c_p, _, u_p = _split_cbu(lo_p_ref[...], hi_p_ref[...], cw)
        _, b_n, _ = _split_cbu(lo_n_ref[...], hi_n_ref[...], cw)
        z = c_gate * u
        z_p = jnp.where(i == 0, 0.0, c_p * u_p)
        z1 = _shift_down(z, z_p, 1)
        z2 = _shift_down(z, z_p, 2)
        w0, w1, w2 = _conv_taps(cw_ref)
        y = w0 * z2 + w1 * z1 + w2 * z
        d_conv = dconv_ref[...]
        d_b = d_conv * y
        d_y = d_conv * b_gate
        d_y_n = jnp.where(last, 0.0, dconv_n_ref[...] * b_n)
        d_z = w2 * d_y + w1 * _shift_up(d_y, d_y_n, 1) + w0 * _shift_up(d_y, d_y_n, 2)
        d_c = d_z * u
        d_u = d_z * c_gate
        gcw = jnp.concatenate([jnp.sum(d_y * z2, axis=0, keepdims=True), jnp.sum(d_y * z1, axis=0, keepdims=True),
                               jnp.sum(d_y * z, axis=0, keepdims=True)], axis=0)

        @pl.when(i == 0)
        def _():
            gcw_ref[...] = gcw

        @pl.when(i > 0)
        def _():
            gcw_ref[...] += gcw

        dkv = dcur_ref[...] + jnp.where(last, 0.0, dprev_n_ref[...])
        dk = _rope(dkv[:, :kvw], cos_ref[...], sin_ref[...], -1.0)
        dproj_ref[...] = jnp.concatenate([dq_ref[...], dk, dkv[:, kvw:], d_c, d_b, d_u], axis=1).astype(BF16)

    prev8 = lambda i: jnp.maximum(i * (tb // 8) - 1, 0)
    next8 = lambda i: jnp.minimum((i + 1) * (tb // 8), s // 8 - 1)
    nxt = lambda i: jnp.minimum(i + 1, nb - 1)
    return pl.pallas_call(
        body, name="dproj_assemble", grid=(nb,),
        in_specs=[pl.BlockSpec((tb, blk_w), lambda i: (i, 1)),
                  pl.BlockSpec((tb, blk_w), lambda i: (i, 2)),
                  pl.BlockSpec((8, blk_w), lambda i: (prev8(i), 1)),
                  pl.BlockSpec((8, blk_w), lambda i: (prev8(i), 2)),
                  pl.BlockSpec((8, blk_w), lambda i: (next8(i), 1)),
                  pl.BlockSpec((8, blk_w), lambda i: (next8(i), 2)),
                  pl.BlockSpec((tb, cw), lambda i: (i, 1)),
                  pl.BlockSpec((8, cw), lambda i: (next8(i), 1)),
                  pl.BlockSpec((tb, cw), lambda i: (i, 0)),
                  pl.BlockSpec((tb, 2 * kvw), lambda i: (i, 0)),
                  pl.BlockSpec((tb, 2 * kvw), lambda i: (nxt(i), 0)),
                  pl.BlockSpec((tb, kvw), lambda i: (i, 0)),
                  pl.BlockSpec((tb, kvw), lambda i: (i, 0)),
                  _VMEM],
        out_specs=[pl.BlockSpec((tb, in_w), lambda i: (i, 0)),
                   pl.BlockSpec((3, cw), lambda i: (0, 0))],
        out_shape=[jax.ShapeDtypeStruct((s, in_w), BF16), jax.ShapeDtypeStruct((3, cw), F32)],
        compiler_params=_params(("arbitrary",)),
    )(proj, proj, proj, proj, proj, proj, d_ac, d_ac, dq, dkv_cur, dkv_prev, cos_t, sin_t, cw_full)


def _dx(d_proj, w_in_g, dpre1):
    s, in_w = d_proj.shape
    ns, d, ncol = w_in_g.shape
    tm = min(TM, s)

    def body(dp_ref, w_ref, r_ref, o_ref, acc):
        j = pl.program_id(1)

        @pl.when(j == 0)
        def _():
            acc[...] = jnp.zeros_like(acc)

        acc[...] += _dot_nt(dp_ref[...], w_ref[...])

        @pl.when(j == ns - 1)
        def _():
            o_ref[...] = acc[...] + ALPHA * r_ref[...]

    return pl.pallas_call(
        body, name="dx", grid=(s // tm, ns),
        in_specs=[pl.BlockSpec((tm, ncol), lambda i, j: (i, j)),
                  pl.BlockSpec((None, d, ncol), lambda i, j: (j, 0, 0)),
                  pl.BlockSpec((tm, d), lambda i, j: (i, 0))],
        out_specs=pl.BlockSpec((None, tm, d), lambda i, j: (0, i, 0)),
        out_shape=jax.ShapeDtypeStruct((1, s, d), F32),
        scratch_shapes=[pltpu.VMEM((tm, d), F32)],
        compiler_params=_params(("parallel", "arbitrary")),
    )(d_proj, w_in_g, dpre1)


def kernel(x, positions, w_in, conv_w, sinks, g_attn, g_conv, w_out, ln1_g, ln1_b, w_gate, w_up, w_down, ln2_g, ln2_b, loss_target, m_w_in, m_conv_w, m_sinks, m_g_attn, m_g_conv, m_w_out, m_ln1_g, m_ln1_b, m_w_gate, m_w_up, m_w_down, m_ln2_g, m_ln2_b, v_w_in, v_conv_w, v_sinks, v_g_attn, v_g_conv, v_w_out, v_ln1_g, v_ln1_b, v_w_gate, v_w_up, v_w_down, v_ln2_g, v_ln2_b):
    s = x.shape[1]
    d = x.shape[2]

    chip_vec = _chip_id(lax.axis_index("x"), lax.axis_index("y")).astype(jnp.int32).reshape(1)
    wnames = ["w_in", "w_out", "w_gate", "w_up", "w_down"]
    bufs = [_cast_weight(w, chip_vec, "cast_" + nme) for w, nme in zip([w_in, w_out, w_gate, w_up, w_down], wnames)]
    cw_full = _allgather_conv_w(conv_w)
    flights, token = _gather_start(bufs, cw_full)

    def gathered(i, after):
        send_sems, recv_sems, buf = flights[i]
        buf = _gather_wait(send_sems, recv_sems, buf, after, "gather_wait_" + wnames[i])
        return _sibling_fill(buf, "sibling_fill_" + wnames[i])

    g_ac = jnp.concatenate([g_attn, g_conv], axis=1)

    cos_t, sin_t = _rope_tables(positions.reshape(s, 1) + token[0:1, 0:1].astype(jnp.int32))
    w_in_g = gathered(0, cos_t)
    proj = _in_proj(x, w_in_g)
    w_out_full = gathered(1, proj).reshape(d, d)
    attn = _attention_fwd(proj, cos_t, sin_t, sinks)
    mixed, ac, rstd_ac = _conv_norm(proj, attn, cw_full, g_ac)
    w_gate_g = gathered(2, mixed)
    w_up_g = gathered(3, mixed)
    xhat1, h1, rstd1 = _out_proj_ln(mixed, w_out_full, x, ln1_g, ln1_b)
    w_down_full = gathered(4, h1).reshape(-1, d)
    act, gate, up = _gate_up(h1, w_gate_g, w_up_g)
    dpre2, loss_part, g_ln2_g, g_ln2_b = _down_ln_loss(act, w_down_full, xhat1, ln1_g, ln1_b, ln2_g, ln2_b, loss_target)

    cvec = lax.axis_index("c").astype(jnp.int32).reshape(1)

    def reduce_begin(part, nme):
        (got,) = _exchange_halves([part], "exchange_halves_" + nme)
        chip_sum = _add_halves(part, got, cvec, "add_halves_" + nme)
        return _scatter_start(chip_sum, "scatter_start_" + nme)

    d_gate, d_up = _dact_silu_bwd(dpre2, w_down_full, gate, up)
    f_down = reduce_begin(_grad_rows(act, dpre2, "grad_w_down"), "w_down")
    p_gate, p_up = _grad_cols(h1, [d_gate, d_up], "grad_w_gate_up")
    f_gate = reduce_begin(p_gate, "w_gate")
    f_up = reduce_begin(p_up, "w_up")
    dpre1, g_ln1_g, g_ln1_b = _dh1_ln_bwd(d_gate, d_up, w_gate_g, w_up_g, dpre2, xhat1, rstd1, ln1_g)
    d_ac, g_g_ac = _dmixed_rms_bwd(dpre1, w_out_full, ac, rstd_ac, g_ac)
    f_out = reduce_begin(_grad_rows(mixed, dpre1, "grad_w_out"), "w_out")
    dq, dkv_cur, dkv_prev, g_sinks = _attention_bwd(proj, d_ac, cos_t, sin_t, sinks)
    d_proj, g_conv_w = _dproj_assemble(proj, d_ac, dq, dkv_cur, dkv_prev, cos_t, sin_t, cw_full)
    (p_in,) = _grad_cols(x, [d_proj], "grad_w_in", a_3d=True)
    f_in = reduce_begin(p_in, "w_in")
    grad_x = _dx(d_proj, w_in_g, dpre1)
    red = _allreduce_small(g_ln2_g, g_ln2_b, g_ln1_g, g_ln1_b, g_g_ac, g_conv_w, g_sinks, loss_part)

    names = ["w_in", "w_out", "w_gate", "w_up", "w_down"]
    landed = [_scatter_wait(*f, red, "scatter_wait_" + nme)
              for f, nme in zip([f_in, f_out, f_gate, f_up, f_down], names)]
    sums = _complete_chip_sums([sm for sm, _ in landed], [land for _, land in landed])

    big = {}
    for nme, w, m, v, sm in zip(names, [w_in, w_out, w_gate, w_up, w_down],
                                [m_w_in, m_w_out, m_w_gate, m_w_up, m_w_down],
                                [v_w_in, v_w_out, v_w_gate, v_w_up, v_w_down], sums):
        big[nme] = _adamw_shard(w, m, v, sm, "adamw_" + nme)
    small = _adamw_small(red, {
        "sinks": (sinks, m_sinks, v_sinks), "g_attn": (g_attn, m_g_attn, v_g_attn),
        "g_conv": (g_conv, m_g_conv, v_g_conv), "ln1_g": (ln1_g, m_ln1_g, v_ln1_g),
        "ln1_b": (ln1_b, m_ln1_b, v_ln1_b), "ln2_g": (ln2_g, m_ln2_g, v_ln2_g),
        "ln2_b": (ln2_b, m_ln2_b, v_ln2_b), "conv_w": (conv_w, m_conv_w, v_conv_w)})
    res = {**big, **small}
    order = ["w_in", "conv_w", "sinks", "g_attn", "g_conv", "w_out", "ln1_g", "ln1_b", "w_gate", "w_up", "w_down",
             "ln2_g", "ln2_b"]
    loss = red[6, d // 2 + 128]
    return (loss, grad_x, *[res[n][0] for n in order], *[res[n][1] for n in order],
            *[res[n][2] for n in order], *[res[n][3] for n in order])
```

```python
import functools

import numpy as np
import jax
import jax.numpy as jnp
from jax import lax
from jax.experimental import pallas as pl
from jax.experimental.pallas import tpu as pltpu

F32 = jnp.float32
BF16 = jnp.bfloat16
MESH = pl.DeviceIdType.MESH

HEAD_DIM = 64
N_KV_HEADS = 4
GROUP = 4
WINDOW = 128
ROT_DIM = 16
ROPE_THETA = 500000.0
ATTN_SCALE = HEAD_DIM ** -0.5
ALPHA = 2.0 ** 0.25
LN_EPS = 1e-5
RMS_EPS = 1e-6
ADAM_LR = 0.001
ADAM_B1 = 0.9
ADAM_B2 = 0.999
ADAM_EPS = 1e-08
ADAM_WD = 0.01
ADAM_STEP = 10
N_CHIPS = 4
NEG_BIG = -1e30

V7X_VMEM_BYTES = 64 * 1024 * 1024
VMEM_LIMIT = V7X_VMEM_BYTES - 6 * 1024 * 1024

TM = 512
TK_TOK = 512
TB_CONV = 256
TR_ELT = 256
ROW_CHUNK = 128


def _params(sem):
    return pltpu.CompilerParams(dimension_semantics=sem, vmem_limit_bytes=VMEM_LIMIT)


def _row_tile(rows, target):
    best = None
    for t in range(16, min(rows, target) + 1, 16):
        if rows % t == 0:
            best = t
    assert best is not None, (rows, target)
    return best


def _dot(a, b):
    return jnp.dot(a, b, preferred_element_type=F32)


def _dot_nt(a, b):
    return lax.dot_general(a, b, (((1,), (1,)), ((), ())), preferred_element_type=F32)


def _dot_tn(a, b):
    return lax.dot_general(a, b, (((0,), (0,)), ((), ())), preferred_element_type=F32)


def _mesh_pos():
    x, y, c = lax.axis_index("x"), lax.axis_index("y"), lax.axis_index("c")
    chips = [(1 - x, y), (x, 1 - y), (1 - x, 1 - y)]
    return x, y, c, chips


def _chip_id(px, py):
    return 2 * px + py


def _rope(t, cos, sgn_sin, sign):
    w = t.shape[1]
    lane = lax.broadcasted_iota(jnp.int32, t.shape, 1) & (HEAD_DIM - 1)
    partner = jnp.where(lane < ROT_DIM // 2, pltpu.roll(t, w - ROT_DIM // 2, 1), pltpu.roll(t, ROT_DIM // 2, 1))
    return t * cos + sign * (partner * sgn_sin)


def _tile_lanes(t, n):
    return jnp.concatenate([t] * n, axis=1)


def _sigmoid(g):
    return 1.0 / (1.0 + jnp.exp(-g))


def _for_row_chunks(n_rows, fn):
    def step(r, carry):
        fn(pl.ds(pl.multiple_of(r * ROW_CHUNK, ROW_CHUNK), ROW_CHUNK))
        return carry

    lax.fori_loop(0, n_rows // ROW_CHUNK, step, 0)


def _ln_fwd(pre):
    mu = jnp.mean(pre, axis=-1, keepdims=True)
    cen = pre - mu
    var = jnp.mean(cen * cen, axis=-1, keepdims=True)
    rstd = lax.rsqrt(var + LN_EPS)
    return cen * rstd, rstd


def _ln_bwd(dy, xhat, rstd, g):
    dxhat = dy * g
    m1 = jnp.mean(dxhat, axis=-1, keepdims=True)
    m2 = jnp.mean(dxhat * xhat, axis=-1, keepdims=True)
    return rstd * (dxhat - m1 - xhat * m2)


def _cast_weight(w, chip_vec, name):
    _, r, c = w.shape
    tr = _row_tile(r, TR_ELT)

    def body(chip_ref, w_ref, o_ref):
        o_ref[...] = w_ref[...].astype(BF16)

    grid_spec = pltpu.PrefetchScalarGridSpec(
        num_scalar_prefetch=1, grid=(r // tr,),
        in_specs=[pl.BlockSpec((None, tr, c), lambda i, chip_ref: (0, i, 0))],
        out_specs=pl.BlockSpec((None, tr, c), lambda i, chip_ref: (chip_ref[0], i, 0)))
    return pl.pallas_call(
        body, name=name, grid_spec=grid_spec,
        out_shape=jax.ShapeDtypeStruct((N_CHIPS, r, c), BF16),
        compiler_params=_params(("parallel",)),
    )(chip_vec, w)


_HBM = pl.BlockSpec(memory_space=pltpu.HBM)
_VMEM = pl.BlockSpec(memory_space=pltpu.VMEM)


_SEM = pl.BlockSpec(memory_space=pltpu.SEMAPHORE)
_ANY = pl.BlockSpec(memory_space=pl.ANY)
_EFFECT = pltpu.SideEffectType.DATAFLOW_SIDE_EFFECTING


def _chip_copy(buf, k, chip_of_src, half_rows, send_sems, recv_sems, to):
    part = buf.at[chip_of_src, half_rows]
    return pltpu.make_async_remote_copy(
        src_ref=part, dst_ref=part, send_sem=send_sems.at[k], recv_sem=recv_sems.at[k], device_id=to, device_id_type=MESH)


def _half_rows(buf, which):
    hr = buf.shape[1] // 2
    return pl.ds(which * hr, hr)


def _gather_start(bufs, after):
    n = len(bufs)

    def body(*refs):
        ins = refs[:n]
        sends, recvs = refs[n + 1:2 * n + 1], refs[2 * n + 1:3 * n + 1]
        token = refs[4 * n + 1]
        x, y, c, chips = _mesh_pos()
        me = _chip_id(x, y)
        for w in range(n):
            for k, chip in enumerate(chips):
                _chip_copy(ins[w], k, me, _half_rows(ins[w], c), sends[w], recvs[w], (*chip, c)).start()
        token[...] = jnp.zeros_like(token)

    outs = pl.pallas_call(
        body, name="gather_start",
        in_specs=[_HBM] * n + [_ANY],
        out_specs=[_SEM] * (2 * n) + [_HBM] * n + [_VMEM],
        out_shape=[pltpu.SemaphoreType.DMA((3,))] * (2 * n) + [pltpu.HBM(b.shape, b.dtype) for b in bufs]
        + [jax.ShapeDtypeStruct((8, 128), F32)],
        input_output_aliases={w: 2 * n + w for w in range(n)},
        compiler_params=pltpu.CompilerParams(has_side_effects=_EFFECT),
    )(*[pltpu.with_memory_space_constraint(b, pltpu.HBM) for b in bufs], after)
    return [(outs[w], outs[n + w], outs[2 * n + w]) for w in range(n)], outs[3 * n]


def _gather_wait(send_sems, recv_sems, buf, after, name):
    def body(buf_ref, send_ref, recv_ref, after_ref, out_ref):
        x, y, c, chips = _mesh_pos()
        me = _chip_id(x, y)
        for k, chip in enumerate(chips):
            _chip_copy(buf_ref, k, me, _half_rows(buf_ref, c), send_ref, recv_ref, (*chip, c)).wait_send()
        for k, chip in enumerate(chips):
            _chip_copy(buf_ref, k, _chip_id(*chip), _half_rows(buf_ref, c), send_ref, recv_ref, (*chip, c)).wait_recv()

    return pl.pallas_call(
        body, name=name,
        in_specs=[_HBM, _SEM, _SEM, _ANY], out_specs=_HBM,
        out_shape=pltpu.HBM(buf.shape, buf.dtype),
        input_output_aliases={0: 0},
        compiler_params=pltpu.CompilerParams(has_side_effects=_EFFECT),
    )(buf, send_sems, recv_sems, after)


def _sibling_fill(buf, name, own_too=False):
    n_copies = 4 if own_too else 3

    def body(buf_ref, out_ref, send_sems, recv_sems):
        x, y, c, chips = _mesh_pos()
        sibling = (x, y, 1 - c)
        slots = [_chip_id(*chip) for chip in chips] + ([_chip_id(x, y)] if own_too else [])
        copies = []
        for k, slot in enumerate(slots):
            cp = _chip_copy(out_ref, k, slot, _half_rows(out_ref, c), send_sems, recv_sems, sibling)
            cp.start()
            copies.append(cp)
        for k, slot in enumerate(slots):
            _chip_copy(out_ref, k, slot, _half_rows(out_ref, 1 - c), send_sems, recv_sems, sibling).wait_recv()
        for cp in copies:
            cp.wait_send()

    return pl.pallas_call(
        body, name=name,
        in_specs=[_HBM], out_specs=_HBM,
        out_shape=jax.ShapeDtypeStruct(buf.shape, buf.dtype),
        input_output_aliases={0: 0},
        scratch_shapes=[pltpu.SemaphoreType.DMA((n_copies,)), pltpu.SemaphoreType.DMA((n_copies,))],
    )(buf)


def _allgather_conv_w(cw):
    _, kw, cs = cw.shape

    def body(cw_ref, out_ref, send_sems, recv_sems):
        x, y, c, chips = _mesh_pos()
        me = _chip_id(x, y)
        out_ref[pl.ds(me, 1)] = cw_ref[...]
        copies = []
        for k, chip in enumerate(chips):
            cp = pltpu.make_async_remote_copy(
                src_ref=cw_ref.at[0], dst_ref=out_ref.at[me], send_sem=send_sems.at[k], recv_sem=recv_sems.at[k],
                device_id=(*chip, c), device_id_type=MESH)
            cp.start()
            copies.append(cp)
        for k, chip in enumerate(chips):
            pltpu.make_async_remote_copy(
                src_ref=cw_ref.at[0], dst_ref=out_ref.at[_chip_id(*chip)], send_sem=send_sems.at[k],
                recv_sem=recv_sems.at[k], device_id=(*chip, c), device_id_type=MESH).wait_recv()
        for cp in copies:
            cp.wait_send()

    return pl.pallas_call(
        body, name="allgather_conv_w",
        in_specs=[_VMEM], out_specs=_VMEM,
        out_shape=jax.ShapeDtypeStruct((N_CHIPS, kw, cs), F32),
        scratch_shapes=[pltpu.SemaphoreType.DMA((3,)), pltpu.SemaphoreType.DMA((3,))],
    )(cw)


def _exchange_halves(parts, name):
    n = len(parts)
    shapes = [p.shape for p in parts]

    def body(*refs):
        ins, outs = refs[:n], refs[n:2 * n]
        send_sems, recv_sems = refs[2 * n:]
        x, y, c, _ = _mesh_pos()
        copies = []
        for w in range(n):
            hr = shapes[w][1] // 2
            cp = pltpu.make_async_remote_copy(
                src_ref=ins[w].at[:, pl.ds((1 - c) * hr, hr)], dst_ref=outs[w],
                send_sem=send_sems.at[w], recv_sem=recv_sems.at[w],
                device_id=(x, y, 1 - c), device_id_type=MESH)
            cp.start()
            copies.append(cp)
        for cp in copies:
            cp.wait()

    return pl.pallas_call(
        body, name=name,
        in_specs=[_HBM] * n, out_specs=[_HBM] * n,
        out_shape=[jax.ShapeDtypeStruct((s[0], s[1] // 2, s[2]), BF16) for s in shapes],
        scratch_shapes=[pltpu.SemaphoreType.DMA((n,)), pltpu.SemaphoreType.DMA((n,))],
    )(*parts)


def _add_halves(part, got, cvec, name):
    ns, r, cdim = part.shape
    hr = r // 2
    tr = _row_tile(hr, TR_ELT)
    nblk = hr // tr

    def body(c_ref, a_ref, b_ref, o_ref):
        o_ref[...] = (a_ref[...].astype(F32) + b_ref[...].astype(F32)).astype(BF16)

    grid_spec = pltpu.PrefetchScalarGridSpec(
        num_scalar_prefetch=1, grid=(ns, nblk),
        in_specs=[pl.BlockSpec((None, tr, cdim), lambda s, i, c_ref: (s, c_ref[0] * nblk + i, 0)),
                  pl.BlockSpec((None, tr, cdim), lambda s, i, c_ref: (s, i, 0))],
        out_specs=pl.BlockSpec((None, tr, cdim), lambda s, i, c_ref: (s, i, 0)))
    return pl.pallas_call(
        body, name=name, grid_spec=grid_spec,
        out_shape=jax.ShapeDtypeStruct((ns, hr, cdim), BF16),
        compiler_params=_params(("parallel", "parallel")),
    )(cvec, part, got)


def _scatter_copy(sums_ref, land_ref, k, src_slot, dst_slot, c, send_sems, recv_sems, to):
    return pltpu.make_async_remote_copy(
        src_ref=sums_ref.at[src_slot], dst_ref=land_ref.at[dst_slot, _half_rows(land_ref, c)],
        send_sem=send_sems.at[k], recv_sem=recv_sems.at[k], device_id=to, device_id_type=MESH)


def _scatter_start(sums, name):
    ns, hr, cdim = sums.shape
    land = lax.empty((ns, 2 * hr, cdim), sums.dtype)

    def body(sums_ref, land_ref, send_sems, recv_sems, sums_thru, land_thru):
        x, y, c, chips = _mesh_pos()
        me = _chip_id(x, y)
        for k, chip in enumerate(chips):
            _scatter_copy(sums_ref, land_ref, k, _chip_id(*chip), me, c, send_sems, recv_sems, (*chip, c)).start()

    return pl.pallas_call(
        body, name=name,
        in_specs=[_HBM, _HBM], out_specs=[_SEM, _SEM, _HBM, _HBM],
        out_shape=[pltpu.SemaphoreType.DMA((3,)), pltpu.SemaphoreType.DMA((3,)),
                   pltpu.HBM(sums.shape, sums.dtype), pltpu.HBM(land.shape, land.dtype)],
        input_output_aliases={0: 2, 1: 3},
        compiler_params=pltpu.CompilerParams(has_side_effects=_EFFECT),
    )(pltpu.with_memory_space_constraint(sums, pltpu.HBM), pltpu.with_memory_space_constraint(land, pltpu.HBM))


def _scatter_wait(send_sems, recv_sems, sums, land, after, name):
    def body(sums_ref, land_ref, send_ref, recv_ref, after_ref, sums_out, land_out):
        x, y, c, chips = _mesh_pos()
        me = _chip_id(x, y)
        for k, chip in enumerate(chips):
            _scatter_copy(sums_ref, land_ref, k, _chip_id(*chip), me, c, send_ref, recv_ref, (*chip, c)).wait_send()
        for k, chip in enumerate(chips):
            _scatter_copy(sums_ref, land_ref, k, me, _chip_id(*chip), c, send_ref, recv_ref, (*chip, c)).wait_recv()

    return pl.pallas_call(
        body, name=name,
        in_specs=[_HBM, _HBM, _SEM, _SEM, _ANY], out_specs=[_HBM, _HBM],
        out_shape=[pltpu.HBM(sums.shape, sums.dtype), pltpu.HBM(land.shape, land.dtype)],
        input_output_aliases={0: 0, 1: 1},
        compiler_params=pltpu.CompilerParams(has_side_effects=_EFFECT),
    )(sums, land, send_sems, recv_sems, after)


def _complete_chip_sums(sums, lands):
    n = len(sums)

    def body(*refs):
        sums_refs, outs = refs[:n], refs[2 * n:3 * n]
        send_sems, recv_sems = refs[3 * n:]
        x, y, c, chips = _mesh_pos()
        me = _chip_id(x, y)
        sibling = (x, y, 1 - c)
        slots = [_chip_id(*chip) for chip in chips]
        sent = []
        for w in range(n):
            out = outs[w]
            cp = _scatter_copy(sums_refs[w], out, 3, me, me, c, send_sems.at[w], recv_sems.at[w], sibling)
            cp.start()
            sent.append(cp)
            for k, slot in enumerate(slots):
                cp = _chip_copy(out, k, slot, _half_rows(out, c), send_sems.at[w], recv_sems.at[w], sibling)
                cp.start()
                sent.append(cp)
        for w in range(n):
            out = outs[w]
            _scatter_copy(sums_refs[w], out, 3, me, me, 1 - c, send_sems.at[w], recv_sems.at[w], sibling).wait_recv()
            for k, slot in enumerate(slots):
                _chip_copy(out, k, slot, _half_rows(out, 1 - c), send_sems.at[w], recv_sems.at[w], sibling).wait_recv()
        for cp in sent:
            cp.wait_send()

    return pl.pallas_call(
        body, name="complete_chip_sums",
        in_specs=[_HBM] * (2 * n), out_specs=[_HBM] * n,
        out_shape=[jax.ShapeDtypeStruct(b.shape, b.dtype) for b in lands],
        input_output_aliases={n + w: w for w in range(n)},
        scratch_shapes=[pltpu.SemaphoreType.DMA((n, 4)), pltpu.SemaphoreType.DMA((n, 4))],
    )(*sums, *lands)


SMALL_ROWS = 8


def _allreduce_small(gl2g, gl2b, gl1g, gl1b, g_ac, gcw, gsink, loss, after):
    d = gl2g.shape[1]
    hd = d // 2
    nq = gsink.shape[1]

    def body(a_ref, b_ref, c_ref, d_ref, e_ref, cw_ref, sk_ref, ls_ref, after_ref, out_ref, mine, gath, send_sems,
             recv_sems):
        x, y, c, _ = _mesh_pos()
        me = 4 * x + 2 * y + c
        mine[...] = jnp.zeros_like(mine)
        mine[0:1, :] = a_ref[...]
        mine[1:2, :] = b_ref[...]
        mine[2:3, :] = c_ref[...]
        mine[3:4, :] = d_ref[...]
        mine[4:5, :] = e_ref[...]
        mine[5:6, 0:hd] = cw_ref[0:1, :]
        mine[5:6, hd:d] = cw_ref[1:2, :]
        mine[6:7, 0:hd] = cw_ref[2:3, :]
        mine[6:7, hd:hd + nq] = sk_ref[...]
        mine[6:7, hd + 128:hd + 256] = ls_ref[...]
        gath[pl.ds(me, 1)] = mine[...][None]
        copies = []
        for r in range(1, 8):
            peer = ((1 - x) if r & 4 else x, (1 - y) if r & 2 else y, (1 - c) if r & 1 else c)
            cp = pltpu.make_async_remote_copy(
                src_ref=mine, dst_ref=gath.at[me], send_sem=send_sems.at[r - 1], recv_sem=recv_sems.at[r - 1],
                device_id=peer, device_id_type=MESH)
            cp.start()
            copies.append(cp)
        for r in range(1, 8):
            peer = ((1 - x) if r & 4 else x, (1 - y) if r & 2 else y, (1 - c) if r & 1 else c)
            peer_id = 4 * peer[0] + 2 * peer[1] + peer[2]
            pltpu.make_async_remote_copy(
                src_ref=mine, dst_ref=gath.at[peer_id], send_sem=send_sems.at[r - 1], recv_sem=recv_sems.at[r - 1],
                device_id=peer, device_id_type=MESH).wait_recv()
        for cp in copies:
            cp.wait_send()
        total = gath[0]
        for dev in range(1, 8):
            total = total + gath[dev]
        out_ref[...] = total

    return pl.pallas_call(
        body, name="allreduce_small",
        in_specs=[_VMEM] * 8 + [_ANY], out_specs=_VMEM,
        out_shape=jax.ShapeDtypeStruct((SMALL_ROWS, d), F32),
        scratch_shapes=[pltpu.VMEM((SMALL_ROWS, d), F32), pltpu.VMEM((8, SMALL_ROWS, d), F32),
                        pltpu.SemaphoreType.DMA((7,)), pltpu.SemaphoreType.DMA((7,))],
    )(gl2g, gl2b, gl1g, gl1b, g_ac, gcw, gsink, loss, after)


def _adamw(w, g, m, v):
    m = ADAM_B1 * m + (1.0 - ADAM_B1) * g
    v = ADAM_B2 * v + (1.0 - ADAM_B2) * (g * g)
    m_hat = m / (1.0 - ADAM_B1 ** ADAM_STEP)
    v_hat = v / (1.0 - ADAM_B2 ** ADAM_STEP)
    delta = -ADAM_LR * (m_hat / (jnp.sqrt(v_hat) + ADAM_EPS) + ADAM_WD * w)
    return delta, m, v


def _adamw_shard(w, m, v, land, own, pos_vec, name):
    _, r, c = w.shape
    hr = r // 2
    tr = _row_tile(hr, TR_ELT)
    nh = hr // tr

    def body(pos_ref, w_ref, m_ref, v_ref, l0, l1, l2, l3, own_ref, g_out, d_out, m_out, v_out):
        i = pl.program_id(0)
        mine = (i // nh) == pos_ref[1]
        own_blk = own_ref[...].astype(F32)
        g = None
        for s, l_ref in enumerate([l0, l1, l2, l3]):
            term = jnp.where(mine & (pos_ref[0] == s), own_blk, l_ref[...].astype(F32))
            g = term if g is None else g + term
        delta, nm, nv = _adamw(w_ref[...], g, m_ref[...], v_ref[...])
        g_out[...] = g
        d_out[...] = delta
        m_out[...] = nm
        v_out[...] = nv

    def land_spec(s):
        def index(i, pos_ref):
            skip = (pos_ref[0] == s) & ((i // nh) == pos_ref[1])
            return (s, jnp.where(skip, (i + nh) % (2 * nh), i), 0)
        return pl.BlockSpec((None, tr, c), index)

    blk = pl.BlockSpec((None, tr, c), lambda i, pos_ref: (0, i, 0))
    grid_spec = pltpu.PrefetchScalarGridSpec(
        num_scalar_prefetch=1, grid=(2 * nh,),
        in_specs=[blk, blk, blk] + [land_spec(s) for s in range(N_CHIPS)]
        + [pl.BlockSpec((None, tr, c), lambda i, pos_ref: (pos_ref[0], i % nh, 0))],
        out_specs=[blk] * 4)
    return pl.pallas_call(
        body, name=name, grid_spec=grid_spec,
        out_shape=[jax.ShapeDtypeStruct((1, r, c), F32)] * 4,
        compiler_params=_params(("parallel",)),
    )(pos_vec, w, m, v, land, land, land, land, own)


def _adamw_small(red, params):
    names = ["sinks", "g_attn", "g_conv", "ln1_g", "ln1_b", "ln2_g", "ln2_b", "conv_w"]
    d = red.shape[1]
    hd = d // 2
    flat = []
    for nme in names:
        flat.extend(params[nme])
    nq = params["sinks"][0].shape[1]
    cs = params["conv_w"][0].shape[2]

    def body(*refs):
        red_ref = refs[0]
        ins = refs[1:1 + 3 * len(names)]
        outs = refs[1 + 3 * len(names):]
        x, y, _, _ = _mesh_pos()
        me = _chip_id(x, y)

        def conv_tap(row, base):
            picked = red_ref[row:row + 1, base:base + cs]
            for s in range(1, N_CHIPS):
                picked = jnp.where(me == s, red_ref[row:row + 1, base + s * cs:base + (s + 1) * cs], picked)
            return picked

        grads = {
            "sinks": red_ref[6:7, hd:hd + nq],
            "g_attn": red_ref[4:5, 0:hd],
            "g_conv": red_ref[4:5, hd:d],
            "ln1_g": red_ref[2:3, :],
            "ln1_b": red_ref[3:4, :],
            "ln2_g": red_ref[0:1, :],
            "ln2_b": red_ref[1:2, :],
        }
        for i, nme in enumerate(names):
            w_ref, m_ref, v_ref = ins[3 * i:3 * i + 3]
            g_out, d_out, m_out, v_out = outs[4 * i:4 * i + 4]
            if nme == "conv_w":
                for tap, (row, base) in enumerate([(5, 0), (5, hd), (6, 0)]):
                    g = conv_tap(row, base)
                    delta, nm, nv = _adamw(w_ref[0, tap:tap + 1, :], g, m_ref[0, tap:tap + 1, :], v_ref[0, tap:tap + 1, :])
                    g_out[0, tap:tap + 1, :] = g
                    d_out[0, tap:tap + 1, :] = delta
                    m_out[0, tap:tap + 1, :] = nm
                    v_out[0, tap:tap + 1, :] = nv
            else:
                g = grads[nme]
                delta, nm, nv = _adamw(w_ref[...], g, m_ref[...], v_ref[...])
                g_out[...] = g
                d_out[...] = delta
                m_out[...] = nm
                v_out[...] = nv

    out_shape = []
    for nme in names:
        out_shape.extend([jax.ShapeDtypeStruct(params[nme][0].shape, F32)] * 4)
    outs = pl.pallas_call(
        body, name="adamw_small",
        in_specs=[_VMEM] * (1 + len(flat)), out_specs=[_VMEM] * len(out_shape),
        out_shape=out_shape,
    )(red, *flat)
    return {nme: tuple(outs[4 * i:4 * i + 4]) for i, nme in enumerate(names)}


def _rope_tables(pos_col):
    s = pos_col.shape[0]
    w = N_KV_HEADS * HEAD_DIM
    tb = min(512, s)
    inv_freq = (ROPE_THETA ** (-np.arange(0, ROT_DIM, 2, dtype=np.float32) / ROT_DIM)).astype(np.float32)

    def body(pos_ref, cos_ref, sin_ref):
        pos = pos_ref[...].astype(F32)
        lane = lax.broadcasted_iota(jnp.int32, (tb, w), 1) & (HEAD_DIM - 1)
        fidx = lane & (ROT_DIM // 2 - 1)
        inv = jnp.zeros((tb, w), F32)
        for k in range(ROT_DIM // 2):
            inv = jnp.where(fidx == k, float(inv_freq[k]), inv)
        ang = pos * inv
        rot = lane < ROT_DIM
        cos_ref[...] = jnp.where(rot, jnp.cos(ang), 1.0)
        sin_v = jnp.sin(ang)
        sin_ref[...] = jnp.where(lane < ROT_DIM // 2, -sin_v, jnp.where(rot, sin_v, 0.0))

    return pl.pallas_call(
        body, name="rope_tables", grid=(s // tb,),
        in_specs=[pl.BlockSpec((tb, 1), lambda i: (i, 0))],
        out_specs=[pl.BlockSpec((tb, w), lambda i: (i, 0))] * 2,
        out_shape=[jax.ShapeDtypeStruct((s, w), F32)] * 2,
        compiler_params=_params(("parallel",)),
    )(pos_col)


def _in_proj(x, w_in_g):
    _, s, d = x.shape
    ns, _, ncol = w_in_g.shape
    tm = min(TM, s)

    def body(x_ref, w_ref, o_ref):
        o_ref[...] = _dot(x_ref[...].astype(BF16), w_ref[...])

    return pl.pallas_call(
        body, name="in_proj", grid=(s // tm, ns),
        in_specs=[pl.BlockSpec((None, tm, d), lambda i, j: (0, i, 0)),
                  pl.BlockSpec((None, d, ncol), lambda i, j: (j, 0, 0))],
        out_specs=pl.BlockSpec((tm, ncol), lambda i, j: (i, j)),
        out_shape=jax.ShapeDtypeStruct((s, ns * ncol), F32),
        compiler_params=_params(("parallel", "arbitrary")),
    )(x, w_in_g)


def _attention_scores(q, k_prev, k_cur, sinks_ref, h, first, valid):
    heads = [q[:, (GROUP * h + g) * HEAD_DIM:(GROUP * h + g + 1) * HEAD_DIM] for g in range(GROUP)]
    q4 = jnp.concatenate(heads, axis=0).astype(BF16)
    kk = jnp.concatenate([k_prev[:, h * HEAD_DIM:(h + 1) * HEAD_DIM], k_cur[:, h * HEAD_DIM:(h + 1) * HEAD_DIM]],
                         axis=0).astype(BF16)
    s = _dot_nt(q4, kk) * ATTN_SCALE
    s = jnp.where(valid, s, NEG_BIG)
    sink = jnp.concatenate(
        [jnp.broadcast_to(sinks_ref[0:1, GROUP * h + g:GROUP * h + g + 1], (WINDOW, 1)) for g in range(GROUP)], axis=0)
    m = jnp.maximum(jnp.max(s, axis=1, keepdims=True), sink)
    p = jnp.exp(s - m)
    p_sink = jnp.exp(sink - m)
    inv_l = 1.0 / (jnp.sum(p, axis=1, keepdims=True) + p_sink)
    return q4, kk, p * inv_l, p_sink * inv_l


def _band_mask(first):
    rows = GROUP * WINDOW
    qi = lax.broadcasted_iota(jnp.int32, (rows, 2 * WINDOW), 0) & (WINDOW - 1)
    kj = lax.broadcasted_iota(jnp.int32, (rows, 2 * WINDOW), 1)
    rel = qi + WINDOW - kj
    band = (rel >= 0) & (rel < WINDOW)
    return band & jnp.logical_not(first & (kj < WINDOW))


def _attention_fwd(proj, cos_t, sin_t, sinks):
    s = proj.shape[0]
    qw = GROUP * N_KV_HEADS * HEAD_DIM
    kvw = N_KV_HEADS * HEAD_DIM
    nb = s // WINDOW

    def body(cur_ref, prev_ref, cos_ref, sin_ref, cosp_ref, sinp_ref, sinks_ref, o_ref):
        n = pl.program_id(0)
        first = n == 0
        cur = cur_ref[...]
        cos, sin = cos_ref[...], sin_ref[...]
        q = _rope(cur[:, :qw], _tile_lanes(cos, GROUP), _tile_lanes(sin, GROUP), 1.0)
        k_cur = _rope(cur[:, qw:qw + kvw], cos, sin, 1.0)
        v_cur = cur[:, qw + kvw:]
        prev = prev_ref[...]
        k_prev = _rope(prev[:, :kvw], cosp_ref[...], sinp_ref[...], 1.0)
        v_prev = prev[:, kvw:]
        valid = _band_mask(first)
        outs = []
        for h in range(N_KV_HEADS):
            _, _, probs, _ = _attention_scores(q, k_prev, k_cur, sinks_ref, h, first, valid)
            vv = jnp.concatenate([v_prev[:, h * HEAD_DIM:(h + 1) * HEAD_DIM], v_cur[:, h * HEAD_DIM:(h + 1) * HEAD_DIM]],
                                 axis=0).astype(BF16)
            o = _dot(probs.astype(BF16), vv)
            outs.extend([o[g * WINDOW:(g + 1) * WINDOW] for g in range(GROUP)])
        o_ref[...] = jnp.concatenate(outs, axis=1)

    tbl = pl.BlockSpec((WINDOW, kvw), lambda n: (n, 0))
    tbl_prev = pl.BlockSpec((WINDOW, kvw), lambda n: (jnp.maximum(n - 1, 0), 0))
    return pl.pallas_call(
        body, name="attention_fwd", grid=(nb,),
        in_specs=[pl.BlockSpec((WINDOW, qw + 2 * kvw), lambda n: (n, 0)),
                  pl.BlockSpec((WINDOW, 2 * kvw), lambda n: (jnp.maximum(n - 1, 0), (qw // (2 * kvw)))),
                  tbl, tbl, tbl_prev, tbl_prev, _VMEM],
        out_specs=pl.BlockSpec((WINDOW, qw), lambda n: (n, 0)),
        out_shape=jax.ShapeDtypeStruct((s, qw), F32),
        compiler_params=_params(("parallel",)),
    )(proj, proj, cos_t, sin_t, cos_t, sin_t, sinks)


def _conv_taps(cw_ref):
    return [jnp.concatenate([cw_ref[s, k:k + 1, :] for s in range(N_CHIPS)], axis=1) for k in range(3)]


def _shift_down(z, halo, steps):
    rows = z.shape[0]
    row = lax.broadcasted_iota(jnp.int32, z.shape, 0)
    out = pltpu.roll(z, steps, 0)
    for r in range(steps):
        out = jnp.where(row == r, halo[8 - steps + r:8 - steps + r + 1, :], out)
    return out


def _shift_up(z, halo, steps):
    rows = z.shape[0]
    row = lax.broadcasted_iota(jnp.int32, z.shape, 0)
    out = pltpu.roll(z, rows - steps, 0)
    for r in range(steps):
        out = jnp.where(row == rows - steps + r, halo[r:r + 1, :], out)
    return out


def _split_cbu(lo, hi, cw):
    c_gate = lo[:, :cw]
    b_gate = jnp.concatenate([lo[:, cw:], hi[:, :2 * cw - lo.shape[1]]], axis=1)
    u = hi[:, 2 * cw - lo.shape[1]:]
    return c_gate, b_gate, u


def _conv_norm(proj, attn, cw_full, g_ac):
    s, in_w = proj.shape
    cw = attn.shape[1]
    blk_w = in_w // 3
    tb = min(TB_CONV, s)

    def body(lo_ref, hi_ref, lo_h_ref, hi_h_ref, attn_ref, cw_ref, g_ref, mixed_ref, ac_ref, rstd_ref):
        i = pl.program_id(0)
        c_gate, b_gate, u = _split_cbu(lo_ref[...], hi_ref[...], cw)
        c_h, _, u_h = _split_cbu(lo_h_ref[...], hi_h_ref[...], cw)
        z = c_gate * u
        z_h = jnp.where(i == 0, 0.0, c_h * u_h)
        w0, w1, w2 = _conv_taps(cw_ref)
        y = w0 * _shift_down(z, z_h, 2) + w1 * _shift_down(z, z_h, 1) + w2 * z
        conv = b_gate * y
        a = attn_ref[...]
        r_a = lax.rsqrt(jnp.mean(a * a, axis=-1, keepdims=True) + RMS_EPS)
        r_c = lax.rsqrt(jnp.mean(conv * conv, axis=-1, keepdims=True) + RMS_EPS)
        g = g_ref[...]
        mixed_ref[...] = jnp.concatenate([a * r_a * g[:, :cw], conv * r_c * g[:, cw:]], axis=1).astype(BF16)
        ac_ref[...] = jnp.concatenate([a, conv], axis=1)
        rstd_ref[0] = r_a
        rstd_ref[1] = r_c

    halo_idx = lambda i: jnp.maximum(i * (tb // 8) - 1, 0)
    return pl.pallas_call(
        body, name="conv_norm", grid=(s // tb,),
        in_specs=[pl.BlockSpec((tb, blk_w), lambda i: (i, 1)),
                  pl.BlockSpec((tb, blk_w), lambda i: (i, 2)),
                  pl.BlockSpec((8, blk_w), lambda i: (halo_idx(i), 1)),
                  pl.BlockSpec((8, blk_w), lambda i: (halo_idx(i), 2)),
                  pl.BlockSpec((tb, cw), lambda i: (i, 0)),
                  _VMEM, _VMEM],
        out_specs=[pl.BlockSpec((tb, 2 * cw), lambda i: (i, 0)),
                   pl.BlockSpec((tb, 2 * cw), lambda i: (i, 0)),
                   pl.BlockSpec((2, tb, 1), lambda i: (0, i, 0))],
        out_shape=[jax.ShapeDtypeStruct((s, 2 * cw), BF16), jax.ShapeDtypeStruct((s, 2 * cw), F32),
                   jax.ShapeDtypeStruct((2, s, 1), F32)],
        compiler_params=_params(("parallel",)),
    )(proj, proj, proj, proj, attn, cw_full, g_ac)


def _out_proj_ln(mixed, w_out_g, x, ln_g, ln_b):
    s, d = mixed.shape
    tm = min(TM, s)
    tk = min(512, d)
    nk = d // tk

    def body(a_ref, w_ref, x_ref, g_ref, b_ref, xhat_ref, h_ref, rstd_ref, acc):
        k = pl.program_id(1)

        @pl.when(k == 0)
        def _():
            acc[...] = jnp.zeros_like(acc)

        acc[...] += _dot(a_ref[...], w_ref[...])

        @pl.when(k == nk - 1)
        def _():
            def rows_fn(rows):
                xhat, rstd = _ln_fwd(ALPHA * x_ref[rows, :] + acc[rows, :])
                xhat_ref[rows, :] = xhat
                h_ref[rows, :] = (xhat * g_ref[...] + b_ref[...]).astype(BF16)
                rstd_ref[rows, :] = rstd

            _for_row_chunks(tm, rows_fn)

    row = pl.BlockSpec((tm, d), lambda i, k: (i, 0))
    return pl.pallas_call(
        body, name="out_proj_ln", grid=(s // tm, nk),
        in_specs=[pl.BlockSpec((tm, tk), lambda i, k: (i, k)),
                  pl.BlockSpec((tk, d), lambda i, k: (k, 0)),
                  pl.BlockSpec((None, tm, d), lambda i, k: (0, i, 0)),
                  _VMEM, _VMEM],
        out_specs=[row, row, pl.BlockSpec((tm, 1), lambda i, k: (i, 0))],
        out_shape=[jax.ShapeDtypeStruct((s, d), F32), jax.ShapeDtypeStruct((s, d), BF16),
                   jax.ShapeDtypeStruct((s, 1), F32)],
        scratch_shapes=[pltpu.VMEM((tm, d), F32)],
        compiler_params=_params(("parallel", "arbitrary")),
    )(mixed, w_out_g, x, ln_g, ln_b)


def _gate_up(h1, w_gate_g, w_up_g):
    s, d = h1.shape
    ns, _, fs = w_gate_g.shape
    tm = min(TM, s)
    tk = min(1024, d)
    nk = d // tk

    def body(h_ref, wg_ref, wu_ref, act_ref, g_ref, u_ref, acc_g, acc_u):
        k = pl.program_id(2)

        @pl.when(k == 0)
        def _():
            acc_g[...] = jnp.zeros_like(acc_g)
            acc_u[...] = jnp.zeros_like(acc_u)

        h = h_ref[...]
        acc_g[...] += _dot(h, wg_ref[...])
        acc_u[...] += _dot(h, wu_ref[...])

        @pl.when(k == nk - 1)
        def _():
            def rows_fn(rows):
                g, u = acc_g[rows, :], acc_u[rows, :]
                act_ref[rows, :] = (g * _sigmoid(g) * u).astype(BF16)
                g_ref[rows, :] = g.astype(BF16)
                u_ref[rows, :] = u.astype(BF16)

            _for_row_chunks(tm, rows_fn)

    wspec = pl.BlockSpec((None, tk, fs), lambda i, j, k: (j, k, 0))
    ospec = pl.BlockSpec((tm, fs), lambda i, j, k: (i, j))
    return pl.pallas_call(
        body, name="gate_up", grid=(s // tm, ns, nk),
        in_specs=[pl.BlockSpec((tm, tk), lambda i, j, k: (i, k)), wspec, wspec],
        out_specs=[ospec] * 3,
        out_shape=[jax.ShapeDtypeStruct((s, ns * fs), BF16)] * 3,
        scratch_shapes=[pltpu.VMEM((tm, fs), F32)] * 2,
        compiler_params=_params(("parallel", "arbitrary", "arbitrary")),
    )(h1, w_gate_g, w_up_g)


def _down_ln_loss(act, w_down_g, xhat1, ln1_g, ln1_b, ln2_g, ln2_b, target):
    s, f = act.shape
    d = xhat1.shape[1]
    tm = min(TM, s)
    tk = 512
    nk = f // tk

    def body(a_ref, w_ref, xh_ref, g1_ref, b1_ref, g2_ref, b2_ref, t_ref, dpre_ref, loss_ref, gg_ref, gb_ref, acc):
        i, k = pl.program_id(0), pl.program_id(1)

        @pl.when(k == 0)
        def _():
            acc[...] = jnp.zeros_like(acc)

        acc[...] += _dot(a_ref[...], w_ref[...])

        @pl.when(k == nk - 1)
        def _():
            @pl.when(i == 0)
            def _():
                loss_ref[...] = jnp.zeros_like(loss_ref)
                gg_ref[...] = jnp.zeros_like(gg_ref)
                gb_ref[...] = jnp.zeros_like(gb_ref)

            def rows_fn(rows):
                h1 = xh_ref[rows, :] * g1_ref[...] + b1_ref[...]
                xhat, rstd = _ln_fwd(ALPHA * h1 + acc[rows, :])
                g2 = g2_ref[...]
                diff = xhat * g2 + b2_ref[...] - t_ref[rows, :]
                dy = diff * (1.0 / d)
                dpre_ref[rows, :] = _ln_bwd(dy, xhat, rstd, g2)
                sq = jnp.sum(jnp.sum(diff * diff, axis=1, keepdims=True), axis=0, keepdims=True)
                loss_ref[...] += jnp.broadcast_to(sq * (0.5 / d), (1, 128))
                gg_ref[...] += jnp.sum(dy * xhat, axis=0, keepdims=True)
                gb_ref[...] += jnp.sum(dy, axis=0, keepdims=True)

            _for_row_chunks(tm, rows_fn)

    row = pl.BlockSpec((tm, d), lambda i, k: (i, 0))
    vec = pl.BlockSpec((1, d), lambda i, k: (0, 0))
    return pl.pallas_call(
        body, name="down_ln_loss", grid=(s // tm, nk),
        in_specs=[pl.BlockSpec((tm, tk), lambda i, k: (i, k)),
                  pl.BlockSpec((tk, d), lambda i, k: (k, 0)),
                  row, _VMEM, _VMEM, _VMEM, _VMEM,
                  pl.BlockSpec((None, tm, d), lambda i, k: (0, i, 0))],
        out_specs=[row, pl.BlockSpec((1, 128), lambda i, k: (0, 0)), vec, vec],
        out_shape=[jax.ShapeDtypeStruct((s, d), F32), jax.ShapeDtypeStruct((1, 128), F32),
                   jax.ShapeDtypeStruct((1, d), F32), jax.ShapeDtypeStruct((1, d), F32)],
        scratch_shapes=[pltpu.VMEM((tm, d), F32)],
        compiler_params=_params(("arbitrary", "arbitrary")),
    )(act, w_down_g, xhat1, ln1_g, ln1_b, ln2_g, ln2_b, target)


def _dact_silu_bwd(dpre2, w_down_g, gate, up):
    s, d = dpre2.shape
    f = gate.shape[1]
    fs = f // N_CHIPS
    tm = min(TM, s)

    def body(dp_ref, w_ref, g_ref, u_ref, dg_ref, du_ref):
        d_act = _dot_nt(dp_ref[...].astype(BF16), w_ref[...])
        g = g_ref[...].astype(F32)
        u = u_ref[...].astype(F32)
        sg = _sigmoid(g)
        dg_ref[...] = (d_act * u * (sg * (1.0 + g * (1.0 - sg)))).astype(BF16)
        du_ref[...] = (d_act * (g * sg)).astype(BF16)

    blk = pl.BlockSpec((tm, fs), lambda i, j: (i, j))
    return pl.pallas_call(
        body, name="dact_silu_bwd", grid=(s // tm, N_CHIPS),
        in_specs=[pl.BlockSpec((tm, d), lambda i, j: (i, 0)),
                  pl.BlockSpec((fs, d), lambda i, j: (j, 0)), blk, blk],
        out_specs=[blk, blk],
        out_shape=[jax.ShapeDtypeStruct((s, f), BF16)] * 2,
        compiler_params=_params(("parallel", "arbitrary")),
    )(dpre2, w_down_g, gate, up)


def _grad_rows(a, b, after, name, row_blocks=1):
    s, m = a.shape
    n = b.shape[1]
    ms = m // N_CHIPS
    tmw = ms // row_blocks
    tk = min(TK_TOK, s)
    nk = s // tk

    def body(a_ref, b_ref, after_ref, o_ref, acc):
        k = pl.program_id(2)

        @pl.when(k == 0)
        def _():
            acc[...] = jnp.zeros_like(acc)

        acc[...] += _dot_tn(a_ref[...].astype(BF16), b_ref[...].astype(BF16))

        @pl.when(k == nk - 1)
        def _():
            o_ref[...] = acc[...].astype(BF16)

    return pl.pallas_call(
        body, name=name, grid=(N_CHIPS, row_blocks, nk),
        in_specs=[pl.BlockSpec((tk, tmw), lambda j, r, k: (k, j * row_blocks + r)),
                  pl.BlockSpec((tk, n), lambda j, r, k: (k, 0)), _ANY],
        out_specs=pl.BlockSpec((None, tmw, n), lambda j, r, k: (j, r, 0)),
        out_shape=jax.ShapeDtypeStruct((N_CHIPS, ms, n), BF16),
        scratch_shapes=[pltpu.VMEM((tmw, n), F32)],
        compiler_params=_params(("parallel", "parallel", "arbitrary")),
    )(a, b, after)


def _grad_cols(a, bs, after, name, a_3d=False, row_blocks=2):
    s, m = a.shape[-2:]
    n = bs[0].shape[1]
    ns = n // N_CHIPS
    nb = len(bs)
    tmw = m // row_blocks
    tk = min(TK_TOK, s)
    nk = s // tk

    def body(*refs):
        a_ref, b_refs, o_refs, accs = refs[0], refs[1:1 + nb], refs[2 + nb:2 + 2 * nb], refs[2 + 2 * nb:]
        k = pl.program_id(2)

        @pl.when(k == 0)
        def _():
            for acc in accs:
                acc[...] = jnp.zeros_like(acc)

        at = a_ref[...].astype(BF16)
        for b_ref, acc in zip(b_refs, accs):
            acc[...] += _dot_tn(at, b_ref[...].astype(BF16))

        @pl.when(k == nk - 1)
        def _():
            for o_ref, acc in zip(o_refs, accs):
                o_ref[...] = acc[...].astype(BF16)

    if a_3d:
        a_spec = pl.BlockSpec((None, tk, tmw), lambda j, r, k: (0, k, r))
    else:
        a_spec = pl.BlockSpec((tk, tmw), lambda j, r, k: (k, r))
    return pl.pallas_call(
        body, name=name, grid=(N_CHIPS, row_blocks, nk),
        in_specs=[a_spec] + [pl.BlockSpec((tk, ns), lambda j, r, k: (k, j))] * nb + [_ANY],
        out_specs=[pl.BlockSpec((None, tmw, ns), lambda j, r, k: (j, r, 0))] * nb,
        out_shape=[jax.ShapeDtypeStruct((N_CHIPS, m, ns), BF16)] * nb,
        scratch_shapes=[pltpu.VMEM((tmw, ns), F32)] * nb,
        compiler_params=_params(("parallel", "parallel", "arbitrary")),
    )(a, *bs, after)


def _dh1_ln_bwd(d_gate, d_up, w_gate_g, w_up_g, dpre2, xhat1, rstd1, ln1_g, after, tm_max=256):
    s, f = d_gate.shape
    d = dpre2.shape[1]
    fs = f // N_CHIPS
    tm = min(tm_max, s)

    def body(dg_ref, du_ref, wg_ref, wu_ref, dp2_ref, xh_ref, rs_ref, g_ref, after_ref, dpre_ref, gg_ref, gb_ref, acc):
        i, j = pl.program_id(0), pl.program_id(1)

        @pl.when(j == 0)
        def _():
            acc[...] = jnp.zeros_like(acc)

        acc[...] += _dot_nt(dg_ref[...], wg_ref[...]) + _dot_nt(du_ref[...], wu_ref[...])

        @pl.when(j == N_CHIPS - 1)
        def _():
            @pl.when(i == 0)
            def _():
                gg_ref[...] = jnp.zeros_like(gg_ref)
                gb_ref[...] = jnp.zeros_like(gb_ref)

            def rows_fn(rows):
                dh = acc[rows, :] + ALPHA * dp2_ref[rows, :]
                xhat = xh_ref[rows, :]
                dpre_ref[rows, :] = _ln_bwd(dh, xhat, rs_ref[rows, :], g_ref[...])
                gg_ref[...] += jnp.sum(dh * xhat, axis=0, keepdims=True)
                gb_ref[...] += jnp.sum(dh, axis=0, keepdims=True)

            _for_row_chunks(tm, rows_fn)

    row = pl.BlockSpec((tm, d), lambda i, j: (i, 0))
    vec = pl.BlockSpec((1, d), lambda i, j: (0, 0))
    act_blk = pl.BlockSpec((tm, fs), lambda i, j: (i, j))
    w_blk = pl.BlockSpec((None, d, fs), lambda i, j: (j, 0, 0))
    return pl.pallas_call(
        body, name="dh1_ln_bwd", grid=(s // tm, N_CHIPS),
        in_specs=[act_blk, act_blk, w_blk, w_blk, row, row, pl.BlockSpec((tm, 1), lambda i, j: (i, 0)), _VMEM, _ANY],
        out_specs=[row, vec, vec],
        out_shape=[jax.ShapeDtypeStruct((s, d), F32), jax.ShapeDtypeStruct((1, d), F32),
                   jax.ShapeDtypeStruct((1, d), F32)],
        scratch_shapes=[pltpu.VMEM((tm, d), F32)],
        compiler_params=_params(("arbitrary", "arbitrary")),
    )(d_gate, d_up, w_gate_g, w_up_g, dpre2, xhat1, rstd1, ln1_g, after)


def _dmixed_rms_bwd(dpre1, w_out_g, ac, rstd, g_ac):
    s, d = dpre1.shape
    hd = d // 2
    tm = min(TM, s)

    def body(dp_ref, w_ref, ac_ref, rs_ref, g_ref, dac_ref, gg_ref):
        i = pl.program_id(1)
        dm = _dot_nt(dp_ref[...].astype(BF16), w_ref[...])
        pre = ac_ref[...]
        r = rs_ref[...]
        gdm = dm * g_ref[...]
        dac_ref[...] = r * gdm - pre * (r * r * r) * jnp.mean(gdm * pre, axis=-1, keepdims=True)
        gg = jnp.sum(dm * pre * r, axis=0, keepdims=True)

        @pl.when(i == 0)
        def _():
            gg_ref[...] = gg

        @pl.when(i > 0)
        def _():
            gg_ref[...] += gg

    return pl.pallas_call(
        body, name="dmixed_rms_bwd", grid=(2, s // tm),
        in_specs=[pl.BlockSpec((tm, d), lambda h, i: (i, 0)),
                  pl.BlockSpec((hd, d), lambda h, i: (h, 0)),
                  pl.BlockSpec((tm, hd), lambda h, i: (i, h)),
                  pl.BlockSpec((None, tm, 1), lambda h, i: (h, i, 0)),
                  pl.BlockSpec((1, hd), lambda h, i: (0, h))],
        out_specs=[pl.BlockSpec((tm, hd), lambda h, i: (i, h)),
                   pl.BlockSpec((1, hd), lambda h, i: (0, h))],
        out_shape=[jax.ShapeDtypeStruct((s, d), F32), jax.ShapeDtypeStruct((1, d), F32)],
        compiler_params=_params(("arbitrary", "arbitrary")),
    )(dpre1, w_out_g, ac, rstd, g_ac)


def _attention_bwd(proj, d_ac, cos_t, sin_t, sinks, after):
    s = proj.shape[0]
    qw = GROUP * N_KV_HEADS * HEAD_DIM
    kvw = N_KV_HEADS * HEAD_DIM
    nb = s // WINDOW
    nq = GROUP * N_KV_HEADS

    def body(cur_ref, prev_ref, do_ref, cos_ref, sin_ref, cosp_ref, sinp_ref, sinks_ref, after_ref,
             dq_ref, dcur_ref, dprev_ref, dsink_ref):
        n = pl.program_id(0)
        first = n == 0
        cur = cur_ref[...]
        cos, sin = cos_ref[...], sin_ref[...]
        cos_q, sin_q = _tile_lanes(cos, GROUP), _tile_lanes(sin, GROUP)
        q = _rope(cur[:, :qw], cos_q, sin_q, 1.0)
        k_cur = _rope(cur[:, qw:qw + kvw], cos, sin, 1.0)
        v_cur = cur[:, qw + kvw:]
        prev = prev_ref[...]
        k_prev = _rope(prev[:, :kvw], cosp_ref[...], sinp_ref[...], 1.0)
        v_prev = prev[:, kvw:]
        d_out = do_ref[...]
        valid = _band_mask(first)
        dq_parts, dk_parts, dv_parts, dsink_parts = [], [], [], []
        for h in range(N_KV_HEADS):
            q4, kk, probs, p_sink = _attention_scores(q, k_prev, k_cur, sinks_ref, h, first, valid)
            vv = jnp.concatenate([v_prev[:, h * HEAD_DIM:(h + 1) * HEAD_DIM], v_cur[:, h * HEAD_DIM:(h + 1) * HEAD_DIM]],
                                 axis=0).astype(BF16)
            do4 = jnp.concatenate(
                [d_out[:, (GROUP * h + g) * HEAD_DIM:(GROUP * h + g + 1) * HEAD_DIM] for g in range(GROUP)],
                axis=0).astype(BF16)
            d_probs = _dot_nt(do4, vv)
            delta = jnp.sum(probs * d_probs, axis=1, keepdims=True)
            d_s = (probs * (d_probs - delta) * ATTN_SCALE).astype(BF16)
            dq4 = _dot(d_s, kk)
            dq_parts.extend([dq4[g * WINDOW:(g + 1) * WINDOW] for g in range(GROUP)])
            dk_parts.append(_dot_tn(d_s, q4))
            dv_parts.append(_dot_tn(probs.astype(BF16), do4))
            ds_sink = -p_sink * delta
            dsink_parts.extend([jnp.sum(ds_sink[g * WINDOW:(g + 1) * WINDOW], axis=0, keepdims=True)
                                for g in range(GROUP)])
        dq_ref[...] = _rope(jnp.concatenate(dq_parts, axis=1), cos_q, sin_q, -1.0)
        dk = jnp.concatenate(dk_parts, axis=1)
        dv = jnp.concatenate(dv_parts, axis=1)
        dprev_ref[...] = jnp.concatenate([dk[:WINDOW], dv[:WINDOW]], axis=1)
        dcur_ref[...] = jnp.concatenate([dk[WINDOW:], dv[WINDOW:]], axis=1)
        dsink = jnp.concatenate(dsink_parts, axis=1)

        @pl.when(first)
        def _():
            dsink_ref[...] = dsink

        @pl.when(n > 0)
        def _():
            dsink_ref[...] += dsink

    tbl = pl.BlockSpec((WINDOW, kvw), lambda n: (n, 0))
    tbl_prev = pl.BlockSpec((WINDOW, kvw), lambda n: (jnp.maximum(n - 1, 0), 0))
    kv_blk = pl.BlockSpec((WINDOW, 2 * kvw), lambda n: (n, 0))
    return pl.pallas_call(
        body, name="attention_bwd", grid=(nb,),
        in_specs=[pl.BlockSpec((WINDOW, qw + 2 * kvw), lambda n: (n, 0)),
                  pl.BlockSpec((WINDOW, 2 * kvw), lambda n: (jnp.maximum(n - 1, 0), (qw // (2 * kvw)))),
                  pl.BlockSpec((WINDOW, qw), lambda n: (n, 0)),
                  tbl, tbl, tbl_prev, tbl_prev, _VMEM, _ANY],
        out_specs=[pl.BlockSpec((WINDOW, qw), lambda n: (n, 0)), kv_blk, kv_blk,
                   pl.BlockSpec((1, nq), lambda n: (0, 0))],
        out_shape=[jax.ShapeDtypeStruct((s, qw), F32), jax.ShapeDtypeStruct((s, 2 * kvw), F32),
                   jax.ShapeDtypeStruct((s, 2 * kvw), F32), jax.ShapeDtypeStruct((1, nq), F32)],
        compiler_params=_params(("arbitrary",)),
    )(proj, proj, d_ac, cos_t, sin_t, cos_t, sin_t, sinks, after)


def _dproj_assemble(proj, d_ac, dq, dkv_cur, dkv_prev, cos_t, sin_t, cw_full):
    s, in_w = proj.shape
    cw = dq.shape[1]
    kvw = N_KV_HEADS * HEAD_DIM
    blk_w = in_w // 3
    tb = WINDOW
    nb = s // tb

    def body(lo_ref, hi_ref, lo_p_ref, hi_p_ref, lo_n_ref, hi_n_ref, dconv_ref, dconv_n_ref,
             dq_ref, dcur_ref, dprev_n_ref, cos_ref, sin_ref, cw_ref, dproj_ref, gcw_ref):
        i = pl.program_id(0)
        last = i == nb - 1
        c_gate, b_gate, u = _split_cbu(lo_ref[...], hi_ref[...], cw)
        c_p, _, u_p = _split_cbu(lo_p_ref[...], hi_p_ref[...], cw)
        _, b_n, _ = _split_cbu(lo_n_ref[...], hi_n_ref[...], cw)
        z = c_gate * u
        z_p = jnp.where(i == 0, 0.0, c_p * u_p)
        z1 = _shift_down(z, z_p, 1)
        z2 = _shift_down(z, z_p, 2)
        w0, w1, w2 = _conv_taps(cw_ref)
        y = w0 * z2 + w1 * z1 + w2 * z
        d_conv = dconv_ref[...]
        d_b = d_conv * y
        d_y = d_conv * b_gate
        d_y_n = jnp.where(last, 0.0, dconv_n_ref[...] * b_n)
        d_z = w2 * d_y + w1 * _shift_up(d_y, d_y_n, 1) + w0 * _shift_up(d_y, d_y_n, 2)
        d_c = d_z * u
        d_u = d_z * c_gate
        gcw = jnp.concatenate([jnp.sum(d_y * z2, axis=0, keepdims=True), jnp.sum(d_y * z1, axis=0, keepdims=True),
                               jnp.sum(d_y * z, axis=0, keepdims=True)], axis=0)

        @pl.when(i == 0)
        def _():
            gcw_ref[...] = gcw

        @pl.when(i > 0)
        def _():
            gcw_ref[...] += gcw

        dkv = dcur_ref[...] + jnp.where(last, 0.0, dprev_n_ref[...])
        dk = _rope(dkv[:, :kvw], cos_ref[...], sin_ref[...], -1.0)
        dproj_ref[...] = jnp.concatenate([dq_ref[...], dk, dkv[:, kvw:], d_c, d_b, d_u], axis=1).astype(BF16)

    prev8 = lambda i: jnp.maximum(i * (tb // 8) - 1, 0)
    next8 = lambda i: jnp.minimum((i + 1) * (tb // 8), s // 8 - 1)
    nxt = lambda i: jnp.minimum(i + 1, nb - 1)
    return pl.pallas_call(
        body, name="dproj_assemble", grid=(nb,),
        in_specs=[pl.BlockSpec((tb, blk_w), lambda i: (i, 1)),
                  pl.BlockSpec((tb, blk_w), lambda i: (i, 2)),
                  pl.BlockSpec((8, blk_w), lambda i: (prev8(i), 1)),
                  pl.BlockSpec((8, blk_w), lambda i: (prev8(i), 2)),
                  pl.BlockSpec((8, blk_w), lambda i: (next8(i), 1)),
                  pl.BlockSpec((8, blk_w), lambda i: (next8(i), 2)),
                  pl.BlockSpec((tb, cw), lambda i: (i, 1)),
                  pl.BlockSpec((8, cw), lambda i: (next8(i), 1)),
                  pl.BlockSpec((tb, cw), lambda i: (i, 0)),
                  pl.BlockSpec((tb, 2 * kvw), lambda i: (i, 0)),
                  pl.BlockSpec((tb, 2 * kvw), lambda i: (nxt(i), 0)),
                  pl.BlockSpec((tb, kvw), lambda i: (i, 0)),
                  pl.BlockSpec((tb, kvw), lambda i: (i, 0)),
                  _VMEM],
        out_specs=[pl.BlockSpec((tb, in_w), lambda i: (i, 0)),
                   pl.BlockSpec((3, cw), lambda i: (0, 0))],
        out_shape=[jax.ShapeDtypeStruct((s, in_w), BF16), jax.ShapeDtypeStruct((3, cw), F32)],
        compiler_params=_params(("arbitrary",)),
    )(proj, proj, proj, proj, proj, proj, d_ac, d_ac, dq, dkv_cur, dkv_prev, cos_t, sin_t, cw_full)


def _dx(d_proj, w_in_g, dpre1, after):
    s, in_w = d_proj.shape
    ns, d, ncol = w_in_g.shape
    tm = min(TM, s)

    def body(dp_ref, w_ref, r_ref, after_ref, o_ref, acc):
        j = pl.program_id(1)

        @pl.when(j == 0)
        def _():
            acc[...] = jnp.zeros_like(acc)

        acc[...] += _dot_nt(dp_ref[...], w_ref[...])

        @pl.when(j == ns - 1)
        def _():
            o_ref[...] = acc[...] + ALPHA * r_ref[...]

    return pl.pallas_call(
        body, name="dx", grid=(s // tm, ns),
        in_specs=[pl.BlockSpec((tm, ncol), lambda i, j: (i, j)),
                  pl.BlockSpec((None, d, ncol), lambda i, j: (j, 0, 0)),
                  pl.BlockSpec((tm, d), lambda i, j: (i, 0)), _ANY],
        out_specs=pl.BlockSpec((None, tm, d), lambda i, j: (0, i, 0)),
        out_shape=jax.ShapeDtypeStruct((1, s, d), F32),
        scratch_shapes=[pltpu.VMEM((tm, d), F32)],
        compiler_params=_params(("parallel", "arbitrary")),
    )(d_proj, w_in_g, dpre1, after)


def kernel(x, positions, w_in, conv_w, sinks, g_attn, g_conv, w_out, ln1_g, ln1_b, w_gate, w_up, w_down, ln2_g, ln2_b, loss_target, m_w_in, m_conv_w, m_sinks, m_g_attn, m_g_conv, m_w_out, m_ln1_g, m_ln1_b, m_w_gate, m_w_up, m_w_down, m_ln2_g, m_ln2_b, v_w_in, v_conv_w, v_sinks, v_g_attn, v_g_conv, v_w_out, v_ln1_g, v_ln1_b, v_w_gate, v_w_up, v_w_down, v_ln2_g, v_ln2_b):
    s = x.shape[1]
    d = x.shape[2]

    chip_vec = _chip_id(lax.axis_index("x"), lax.axis_index("y")).astype(jnp.int32).reshape(1)
    wnames = ["w_in", "w_out", "w_gate", "w_up", "w_down"]
    bufs = [_cast_weight(w, chip_vec, "cast_" + nme) for w, nme in zip([w_in, w_out, w_gate, w_up, w_down], wnames)]
    cw_full = _allgather_conv_w(conv_w)
    flights, token = _gather_start(bufs, cw_full)

    def gathered(i, after):
        send_sems, recv_sems, buf = flights[i]
        buf = _gather_wait(send_sems, recv_sems, buf, after, "gather_wait_" + wnames[i])
        return _sibling_fill(buf, "sibling_fill_" + wnames[i])

    g_ac = jnp.concatenate([g_attn, g_conv], axis=1)

    cos_t, sin_t = _rope_tables(positions.reshape(s, 1) + token[0:1, 0:1].astype(jnp.int32))
    w_in_g = gathered(0, cos_t)
    proj = _in_proj(x, w_in_g)
    w_out_full = gathered(1, proj).reshape(d, d)
    attn = _attention_fwd(proj, cos_t, sin_t, sinks)
    mixed, ac, rstd_ac = _conv_norm(proj, attn, cw_full, g_ac)
    w_gate_g = gathered(2, mixed)
    w_up_g = gathered(3, mixed)
    xhat1, h1, rstd1 = _out_proj_ln(mixed, w_out_full, x, ln1_g, ln1_b)
    w_down_full = gathered(4, h1).reshape(-1, d)
    act, gate, up = _gate_up(h1, w_gate_g, w_up_g)
    dpre2, loss_part, g_ln2_g, g_ln2_b = _down_ln_loss(act, w_down_full, xhat1, ln1_g, ln1_b, ln2_g, ln2_b, loss_target)

    cvec = lax.axis_index("c").astype(jnp.int32).reshape(1)

    def reduce_begin(part, nme):
        (got,) = _exchange_halves([part], "exchange_halves_" + nme)
        chip_sum = _add_halves(part, got, cvec, "add_halves_" + nme)
        return _scatter_start(chip_sum, "scatter_start_" + nme)

    d_gate, d_up = _dact_silu_bwd(dpre2, w_down_full, gate, up)
    f_down = reduce_begin(_grad_rows(act, dpre2, d_gate, "grad_w_down"), "w_down")
    p_gate, p_up = _grad_cols(h1, [d_gate, d_up], f_down[2], "grad_w_gate_up")
    f_gate = reduce_begin(p_gate, "w_gate")
    f_up = reduce_begin(p_up, "w_up")
    dpre1, g_ln1_g, g_ln1_b = _dh1_ln_bwd(d_gate, d_up, w_gate_g, w_up_g, dpre2, xhat1, rstd1, ln1_g, f_up[2])
    d_ac, g_g_ac = _dmixed_rms_bwd(dpre1, w_out_full, ac, rstd_ac, g_ac)
    f_out = reduce_begin(_grad_rows(mixed, dpre1, d_ac, "grad_w_out"), "w_out")
    dq, dkv_cur, dkv_prev, g_sinks = _attention_bwd(proj, d_ac, cos_t, sin_t, sinks, f_out[2])
    d_proj, g_conv_w = _dproj_assemble(proj, d_ac, dq, dkv_cur, dkv_prev, cos_t, sin_t, cw_full)
    (p_in,) = _grad_cols(x, [d_proj], d_proj, "grad_w_in", a_3d=True)
    f_in = reduce_begin(p_in, "w_in")
    grad_x = _dx(d_proj, w_in_g, dpre1, f_in[2])
    red = _allreduce_small(g_ln2_g, g_ln2_b, g_ln1_g, g_ln1_b, g_g_ac, g_conv_w, g_sinks, loss_part, grad_x)

    names = ["w_in", "w_out", "w_gate", "w_up", "w_down"]
    landed = [_scatter_wait(*f, red, "scatter_wait_" + nme)
              for f, nme in zip([f_in, f_out, f_gate, f_up, f_down], names)]
    own_sums = [sm for sm, _ in landed]
    lands = _complete_chip_sums(own_sums, [land for _, land in landed])

    pos_vec = jnp.concatenate([chip_vec, cvec])
    big = {}
    for nme, w, m, v, land, own in zip(names, [w_in, w_out, w_gate, w_up, w_down],
                                       [m_w_in, m_w_out, m_w_gate, m_w_up, m_w_down],
                                       [v_w_in, v_w_out, v_w_gate, v_w_up, v_w_down], lands, own_sums):
        big[nme] = _adamw_shard(w, m, v, land, own, pos_vec, "adamw_" + nme)
    small = _adamw_small(red, {
        "sinks": (sinks, m_sinks, v_sinks), "g_attn": (g_attn, m_g_attn, v_g_attn),
        "g_conv": (g_conv, m_g_conv, v_g_conv), "ln1_g": (ln1_g, m_ln1_g, v_ln1_g),
        "ln1_b": (ln1_b, m_ln1_b, v_ln1_b), "ln2_g": (ln2_g, m_ln2_g, v_ln2_g),
        "ln2_b": (ln2_b, m_ln2_b, v_ln2_b), "conv_w": (conv_w, m_conv_w, v_conv_w)})
    res = {**big, **small}
    order = ["w_in", "conv_w", "sinks", "g_attn", "g_conv", "w_out", "ln1_g", "ln1_b", "w_gate", "w_up", "w_down",
             "ln2_g", "ln2_b"]
    loss = red[6, d // 2 + 128]
    return (loss, grad_x, *[res[n][0] for n in order], *[res[n][1] for n in order],
            *[res[n][2] for n in order], *[res[n][3] for n in order])
```

```python
import functools

import numpy as np
import jax
import jax.numpy as jnp
from jax import lax
from jax.experimental import pallas as pl
from jax.experimental.pallas import tpu as pltpu

F32 = jnp.float32
BF16 = jnp.bfloat16
MESH = pl.DeviceIdType.MESH

HEAD_DIM = 64
N_KV_HEADS = 4
GROUP = 4
WINDOW = 128
ROT_DIM = 16
ROPE_THETA = 500000.0
ATTN_SCALE = HEAD_DIM ** -0.5
ALPHA = 2.0 ** 0.25
LN_EPS = 1e-5
RMS_EPS = 1e-6
ADAM_LR = 0.001
ADAM_B1 = 0.9
ADAM_B2 = 0.999
ADAM_EPS = 1e-08
ADAM_WD = 0.01
ADAM_STEP = 10
N_CHIPS = 4
NEG_BIG = -1e30

V7X_VMEM_BYTES = 64 * 1024 * 1024
VMEM_LIMIT = V7X_VMEM_BYTES - 6 * 1024 * 1024

TM = 512
TK_TOK = 1024
TB_CONV = 256
TR_ELT = 256
ROW_CHUNK = 128


def _params(sem):
    return pltpu.CompilerParams(dimension_semantics=sem, vmem_limit_bytes=VMEM_LIMIT)


def _row_tile(rows, target):
    best = None
    for t in range(16, min(rows, target) + 1, 16):
        if rows % t == 0:
            best = t
    assert best is not None, (rows, target)
    return best


def _dot(a, b):
    return jnp.dot(a, b, preferred_element_type=F32)


def _dot_nt(a, b):
    return lax.dot_general(a, b, (((1,), (1,)), ((), ())), preferred_element_type=F32)


def _dot_tn(a, b):
    return lax.dot_general(a, b, (((0,), (0,)), ((), ())), preferred_element_type=F32)


def _mesh_pos():
    x, y, c = lax.axis_index("x"), lax.axis_index("y"), lax.axis_index("c")
    chips = [(1 - x, y), (x, 1 - y), (1 - x, 1 - y)]
    return x, y, c, chips


def _chip_id(px, py):
    return 2 * px + py


def _rope(t, cos, sgn_sin, sign):
    w = t.shape[1]
    lane = lax.broadcasted_iota(jnp.int32, t.shape, 1) & (HEAD_DIM - 1)
    partner = jnp.where(lane < ROT_DIM // 2, pltpu.roll(t, w - ROT_DIM // 2, 1), pltpu.roll(t, ROT_DIM // 2, 1))
    return t * cos + sign * (partner * sgn_sin)


def _tile_lanes(t, n):
    return jnp.concatenate([t] * n, axis=1)


def _sigmoid(g):
    return 1.0 / (1.0 + jnp.exp(-g))


def _for_row_chunks(n_rows, fn):
    def step(r, carry):
        fn(pl.ds(pl.multiple_of(r * ROW_CHUNK, ROW_CHUNK), ROW_CHUNK))
        return carry

    lax.fori_loop(0, n_rows // ROW_CHUNK, step, 0)


def _accumulate(acc, val, k, nk):
    if nk == 1:
        acc[...] = val
        return

    @pl.when(k == 0)
    def _():
        acc[...] = val

    @pl.when(k > 0)
    def _():
        acc[...] += val


def _ln_fwd(pre):
    mu = jnp.mean(pre, axis=-1, keepdims=True)
    cen = pre - mu
    var = jnp.mean(cen * cen, axis=-1, keepdims=True)
    rstd = lax.rsqrt(var + LN_EPS)
    return cen * rstd, rstd


def _ln_bwd(dy, xhat, rstd, g):
    dxhat = dy * g
    m1 = jnp.mean(dxhat, axis=-1, keepdims=True)
    m2 = jnp.mean(dxhat * xhat, axis=-1, keepdims=True)
    return rstd * (dxhat - m1 - xhat * m2)


def _cast_weight(w, chip_vec, after, name):
    _, r, c = w.shape
    tr = _row_tile(r, TR_ELT)

    def body(chip_ref, w_ref, after_ref, o_ref):
        o_ref[...] = w_ref[...].astype(BF16)

    grid_spec = pltpu.PrefetchScalarGridSpec(
        num_scalar_prefetch=1, grid=(r // tr,),
        in_specs=[pl.BlockSpec((None, tr, c), lambda i, chip_ref: (0, i, 0)), _ANY],
        out_specs=pl.BlockSpec((None, tr, c), lambda i, chip_ref: (chip_ref[0], i, 0)))
    return pl.pallas_call(
        body, name=name, grid_spec=grid_spec,
        out_shape=jax.ShapeDtypeStruct((N_CHIPS, r, c), BF16),
        compiler_params=_params(("parallel",)),
    )(chip_vec, w, after)


_HBM = pl.BlockSpec(memory_space=pltpu.HBM)
_VMEM = pl.BlockSpec(memory_space=pltpu.VMEM)


_SEM = pl.BlockSpec(memory_space=pltpu.SEMAPHORE)
_ANY = pl.BlockSpec(memory_space=pl.ANY)
_EFFECT = pltpu.SideEffectType.DATAFLOW_SIDE_EFFECTING


def _chip_copy(buf, k, chip_of_src, half_rows, send_sems, recv_sems, to):
    part = buf.at[chip_of_src, half_rows]
    return pltpu.make_async_remote_copy(
        src_ref=part, dst_ref=part, send_sem=send_sems.at[k], recv_sem=recv_sems.at[k], device_id=to, device_id_type=MESH)


def _half_rows(buf, which):
    hr = buf.shape[1] // 2
    return pl.ds(which * hr, hr)


def _gather_start(bufs, after, name):
    n = len(bufs)

    def body(*refs):
        ins = refs[:n]
        sends, recvs = refs[n + 1:2 * n + 1], refs[2 * n + 1:3 * n + 1]
        token = refs[4 * n + 1]
        x, y, c, chips = _mesh_pos()
        me = _chip_id(x, y)
        for w in range(n):
            for k, chip in enumerate(chips):
                _chip_copy(ins[w], k, me, _half_rows(ins[w], c), sends[w], recvs[w], (*chip, c)).start()
        token[...] = jnp.zeros_like(token)

    outs = pl.pallas_call(
        body, name=name,
        in_specs=[_HBM] * n + [_ANY],
        out_specs=[_SEM] * (2 * n) + [_HBM] * n + [_VMEM],
        out_shape=[pltpu.SemaphoreType.DMA((3,))] * (2 * n) + [pltpu.HBM(b.shape, b.dtype) for b in bufs]
        + [jax.ShapeDtypeStruct((8, 128), F32)],
        input_output_aliases={w: 2 * n + w for w in range(n)},
        compiler_params=pltpu.CompilerParams(has_side_effects=_EFFECT),
    )(*[pltpu.with_memory_space_constraint(b, pltpu.HBM) for b in bufs], after)
    return [(outs[w], outs[n + w], outs[2 * n + w]) for w in range(n)], outs[3 * n]


def _gather_wait(send_sems, recv_sems, buf, after, name):
    def body(buf_ref, send_ref, recv_ref, after_ref, out_ref):
        x, y, c, chips = _mesh_pos()
        me = _chip_id(x, y)
        for k, chip in enumerate(chips):
            _chip_copy(buf_ref, k, me, _half_rows(buf_ref, c), send_ref, recv_ref, (*chip, c)).wait_send()
        for k, chip in enumerate(chips):
            _chip_copy(buf_ref, k, _chip_id(*chip), _half_rows(buf_ref, c), send_ref, recv_ref, (*chip, c)).wait_recv()

    return pl.pallas_call(
        body, name=name,
        in_specs=[_HBM, _SEM, _SEM, _ANY], out_specs=_HBM,
        out_shape=pltpu.HBM(buf.shape, buf.dtype),
        input_output_aliases={0: 0},
        compiler_params=pltpu.CompilerParams(has_side_effects=_EFFECT),
    )(buf, send_sems, recv_sems, after)


def _sibling_fill(buf, name, own_too=False):
    n_copies = 4 if own_too else 3

    def body(buf_ref, out_ref, send_sems, recv_sems):
        x, y, c, chips = _mesh_pos()
        sibling = (x, y, 1 - c)
        slots = [_chip_id(*chip) for chip in chips] + ([_chip_id(x, y)] if own_too else [])
        copies = []
        for k, slot in enumerate(slots):
            cp = _chip_copy(out_ref, k, slot, _half_rows(out_ref, c), send_sems, recv_sems, sibling)
            cp.start()
            copies.append(cp)
        for k, slot in enumerate(slots):
            _chip_copy(out_ref, k, slot, _half_rows(out_ref, 1 - c), send_sems, recv_sems, sibling).wait_recv()
        for cp in copies:
            cp.wait_send()

    return pl.pallas_call(
        body, name=name,
        in_specs=[_HBM], out_specs=_HBM,
        out_shape=jax.ShapeDtypeStruct(buf.shape, buf.dtype),
        input_output_aliases={0: 0},
        scratch_shapes=[pltpu.SemaphoreType.DMA((n_copies,)), pltpu.SemaphoreType.DMA((n_copies,))],
    )(buf)


def _allgather_conv_w(cw):
    _, kw, cs = cw.shape

    def body(cw_ref, out_ref, send_sems, recv_sems):
        x, y, c, chips = _mesh_pos()
        me = _chip_id(x, y)
        out_ref[pl.ds(me, 1)] = cw_ref[...]
        copies = []
        for k, chip in enumerate(chips):
            cp = pltpu.make_async_remote_copy(
                src_ref=cw_ref.at[0], dst_ref=out_ref.at[me], send_sem=send_sems.at[k], recv_sem=recv_sems.at[k],
                device_id=(*chip, c), device_id_type=MESH)
            cp.start()
            copies.append(cp)
        for k, chip in enumerate(chips):
            pltpu.make_async_remote_copy(
                src_ref=cw_ref.at[0], dst_ref=out_ref.at[_chip_id(*chip)], send_sem=send_sems.at[k],
                recv_sem=recv_sems.at[k], device_id=(*chip, c), device_id_type=MESH).wait_recv()
        for cp in copies:
            cp.wait_send()

    return pl.pallas_call(
        body, name="allgather_conv_w",
        in_specs=[_VMEM], out_specs=_VMEM,
        out_shape=jax.ShapeDtypeStruct((N_CHIPS, kw, cs), F32),
        scratch_shapes=[pltpu.SemaphoreType.DMA((3,)), pltpu.SemaphoreType.DMA((3,))],
    )(cw)


def _exchange_halves(parts, after, name):
    n = len(parts)
    shapes = [p.shape for p in parts]

    def body(*refs):
        ins, outs = refs[:n], refs[n + 1:2 * n + 1]
        send_sems, recv_sems = refs[2 * n + 1:]
        x, y, c, _ = _mesh_pos()
        copies = []
        for w in range(n):
            hr = shapes[w][1] // 2
            cp = pltpu.make_async_remote_copy(
                src_ref=ins[w].at[:, pl.ds((1 - c) * hr, hr)], dst_ref=outs[w],
                send_sem=send_sems.at[w], recv_sem=recv_sems.at[w],
                device_id=(x, y, 1 - c), device_id_type=MESH)
            cp.start()
            copies.append(cp)
        for cp in copies:
            cp.wait()

    return pl.pallas_call(
        body, name=name,
        in_specs=[_HBM] * n + [_ANY], out_specs=[_HBM] * n,
        out_shape=[jax.ShapeDtypeStruct((s[0], s[1] // 2, s[2]), BF16) for s in shapes],
        scratch_shapes=[pltpu.SemaphoreType.DMA((n,)), pltpu.SemaphoreType.DMA((n,))],
    )(*parts, after)


def _add_halves(part, got, cvec, name):
    ns, r, cdim = part.shape
    hr = r // 2
    tr = _row_tile(hr, TR_ELT)
    nblk = hr // tr

    def body(c_ref, a_ref, b_ref, o_ref):
        o_ref[...] = (a_ref[...].astype(F32) + b_ref[...].astype(F32)).astype(BF16)

    grid_spec = pltpu.PrefetchScalarGridSpec(
        num_scalar_prefetch=1, grid=(ns, nblk),
        in_specs=[pl.BlockSpec((None, tr, cdim), lambda s, i, c_ref: (s, c_ref[0] * nblk + i, 0)),
                  pl.BlockSpec((None, tr, cdim), lambda s, i, c_ref: (s, i, 0))],
        out_specs=pl.BlockSpec((None, tr, cdim), lambda s, i, c_ref: (s, i, 0)))
    return pl.pallas_call(
        body, name=name, grid_spec=grid_spec,
        out_shape=jax.ShapeDtypeStruct((ns, hr, cdim), BF16),
        compiler_params=_params(("parallel", "parallel")),
    )(cvec, part, got)


def _scatter_copy(sums_ref, land_ref, k, src_slot, dst_slot, c, send_sems, recv_sems, to):
    return pltpu.make_async_remote_copy(
        src_ref=sums_ref.at[src_slot], dst_ref=land_ref.at[dst_slot, _half_rows(land_ref, c)],
        send_sem=send_sems.at[k], recv_sem=recv_sems.at[k], device_id=to, device_id_type=MESH)


def _scatter_start(sums, name):
    ns, hr, cdim = sums.shape
    land = lax.empty((ns, 2 * hr, cdim), sums.dtype)

    def body(sums_ref, land_ref, send_sems, recv_sems, sums_thru, land_thru):
        x, y, c, chips = _mesh_pos()
        me = _chip_id(x, y)
        for k, chip in enumerate(chips):
            _scatter_copy(sums_ref, land_ref, k, _chip_id(*chip), me, c, send_sems, recv_sems, (*chip, c)).start()

    return pl.pallas_call(
        body, name=name,
        in_specs=[_HBM, _HBM], out_specs=[_SEM, _SEM, _HBM, _HBM],
        out_shape=[pltpu.SemaphoreType.DMA((3,)), pltpu.SemaphoreType.DMA((3,)),
                   pltpu.HBM(sums.shape, sums.dtype), pltpu.HBM(land.shape, land.dtype)],
        input_output_aliases={0: 2, 1: 3},
        compiler_params=pltpu.CompilerParams(has_side_effects=_EFFECT),
    )(pltpu.with_memory_space_constraint(sums, pltpu.HBM), pltpu.with_memory_space_constraint(land, pltpu.HBM))


def _scatter_wait(send_sems, recv_sems, sums, land, after, name):
    def body(sums_ref, land_ref, send_ref, recv_ref, after_ref, sums_out, land_out):
        x, y, c, chips = _mesh_pos()
        me = _chip_id(x, y)
        for k, chip in enumerate(chips):
            _scatter_copy(sums_ref, land_ref, k, _chip_id(*chip), me, c, send_ref, recv_ref, (*chip, c)).wait_send()
        for k, chip in enumerate(chips):
            _scatter_copy(sums_ref, land_ref, k, me, _chip_id(*chip), c, send_ref, recv_ref, (*chip, c)).wait_recv()

    return pl.pallas_call(
        body, name=name,
        in_specs=[_HBM, _HBM, _SEM, _SEM, _ANY], out_specs=[_HBM, _HBM],
        out_shape=[pltpu.HBM(sums.shape, sums.dtype), pltpu.HBM(land.shape, land.dtype)],
        input_output_aliases={0: 0, 1: 1},
        compiler_params=pltpu.CompilerParams(has_side_effects=_EFFECT),
    )(sums, land, send_sems, recv_sems, after)


def _complete_chip_sums(sums, lands):
    n = len(sums)

    def body(*refs):
        sums_refs, outs = refs[:n], refs[2 * n:3 * n]
        send_sems, recv_sems = refs[3 * n:]
        x, y, c, chips = _mesh_pos()
        me = _chip_id(x, y)
        sibling = (x, y, 1 - c)
        slots = [_chip_id(*chip) for chip in chips]
        sent = []
        for w in range(n):
            out = outs[w]
            cp = _scatter_copy(sums_refs[w], out, 3, me, me, c, send_sems.at[w], recv_sems.at[w], sibling)
            cp.start()
            sent.append(cp)
            for k, slot in enumerate(slots):
                cp = _chip_copy(out, k, slot, _half_rows(out, c), send_sems.at[w], recv_sems.at[w], sibling)
                cp.start()
                sent.append(cp)
        for w in range(n):
            out = outs[w]
            _scatter_copy(sums_refs[w], out, 3, me, me, 1 - c, send_sems.at[w], recv_sems.at[w], sibling).wait_recv()
            for k, slot in enumerate(slots):
                _chip_copy(out, k, slot, _half_rows(out, 1 - c), send_sems.at[w], recv_sems.at[w], sibling).wait_recv()
        for cp in sent:
            cp.wait_send()

    return pl.pallas_call(
        body, name="complete_chip_sums",
        in_specs=[_HBM] * (2 * n), out_specs=[_HBM] * n,
        out_shape=[jax.ShapeDtypeStruct(b.shape, b.dtype) for b in lands],
        input_output_aliases={n + w: w for w in range(n)},
        scratch_shapes=[pltpu.SemaphoreType.DMA((n, 4)), pltpu.SemaphoreType.DMA((n, 4))],
    )(*sums, *lands)


SMALL_ROWS = 8


def _allreduce_small(gl2g, gl2b, gl1g, gl1b, g_ac, gcw, gsink, loss, after):
    d = gl2g.shape[1]
    hd = d // 2
    nq = gsink.shape[1]

    def body(a_ref, b_ref, c_ref, d_ref, e_ref, cw_ref, sk_ref, ls_ref, after_ref, out_ref, mine, gath, send_sems,
             recv_sems):
        x, y, c, _ = _mesh_pos()
        me = 4 * x + 2 * y + c
        mine[...] = jnp.zeros_like(mine)
        mine[0:1, :] = a_ref[...]
        mine[1:2, :] = b_ref[...]
        mine[2:3, :] = c_ref[...]
        mine[3:4, :] = d_ref[...]
        mine[4:5, :] = e_ref[...]
        mine[5:6, 0:hd] = cw_ref[0:1, :]
        mine[5:6, hd:d] = cw_ref[1:2, :]
        mine[6:7, 0:hd] = cw_ref[2:3, :]
        mine[6:7, hd:hd + nq] = sk_ref[...]
        mine[6:7, hd + 128:hd + 256] = ls_ref[...]
        gath[pl.ds(me, 1)] = mine[...][None]
        copies = []
        for r in range(1, 8):
            peer = ((1 - x) if r & 4 else x, (1 - y) if r & 2 else y, (1 - c) if r & 1 else c)
            cp = pltpu.make_async_remote_copy(
                src_ref=mine, dst_ref=gath.at[me], send_sem=send_sems.at[r - 1], recv_sem=recv_sems.at[r - 1],
                device_id=peer, device_id_type=MESH)
            cp.start()
            copies.append(cp)
        for r in range(1, 8):
            peer = ((1 - x) if r & 4 else x, (1 - y) if r & 2 else y, (1 - c) if r & 1 else c)
            peer_id = 4 * peer[0] + 2 * peer[1] + peer[2]
            pltpu.make_async_remote_copy(
                src_ref=mine, dst_ref=gath.at[peer_id], send_sem=send_sems.at[r - 1], recv_sem=recv_sems.at[r - 1],
                device_id=peer, device_id_type=MESH).wait_recv()
        for cp in copies:
            cp.wait_send()
        total = gath[0]
        for dev in range(1, 8):
            total = total + gath[dev]
        out_ref[...] = total

    return pl.pallas_call(
        body, name="allreduce_small",
        in_specs=[_VMEM] * 8 + [_ANY], out_specs=_VMEM,
        out_shape=jax.ShapeDtypeStruct((SMALL_ROWS, d), F32),
        scratch_shapes=[pltpu.VMEM((SMALL_ROWS, d), F32), pltpu.VMEM((8, SMALL_ROWS, d), F32),
                        pltpu.SemaphoreType.DMA((7,)), pltpu.SemaphoreType.DMA((7,))],
    )(gl2g, gl2b, gl1g, gl1b, g_ac, gcw, gsink, loss, after)


def _adamw(w, g, m, v):
    m = ADAM_B1 * m + (1.0 - ADAM_B1) * g
    v = ADAM_B2 * v + (1.0 - ADAM_B2) * (g * g)
    m_hat = m / (1.0 - ADAM_B1 ** ADAM_STEP)
    v_hat = v / (1.0 - ADAM_B2 ** ADAM_STEP)
    delta = -ADAM_LR * (m_hat / (jnp.sqrt(v_hat) + ADAM_EPS) + ADAM_WD * w)
    return delta, m, v


def _adamw_shard(w, m, v, land, own, pos_vec, name):
    _, r, c = w.shape
    hr = r // 2
    tr = _row_tile(hr, TR_ELT)
    nh = hr // tr

    def body(pos_ref, w_ref, m_ref, v_ref, l0, l1, l2, l3, own_ref, g_out, d_out, m_out, v_out):
        i = pl.program_id(0)
        mine = (i // nh) == pos_ref[1]
        own_blk = own_ref[...].astype(F32)
        g = None
        for s, l_ref in enumerate([l0, l1, l2, l3]):
            term = jnp.where(mine & (pos_ref[0] == s), own_blk, l_ref[...].astype(F32))
            g = term if g is None else g + term
        delta, nm, nv = _adamw(w_ref[...], g, m_ref[...], v_ref[...])
        g_out[...] = g
        d_out[...] = delta
        m_out[...] = nm
        v_out[...] = nv

    def land_spec(s):
        def index(i, pos_ref):
            skip = (pos_ref[0] == s) & ((i // nh) == pos_ref[1])
            return (s, jnp.where(skip, (i + nh) % (2 * nh), i), 0)
        return pl.BlockSpec((None, tr, c), index)

    blk = pl.BlockSpec((None, tr, c), lambda i, pos_ref: (0, i, 0))
    grid_spec = pltpu.PrefetchScalarGridSpec(
        num_scalar_prefetch=1, grid=(2 * nh,),
        in_specs=[blk, blk, blk] + [land_spec(s) for s in range(N_CHIPS)]
        + [pl.BlockSpec((None, tr, c), lambda i, pos_ref: (pos_ref[0], i % nh, 0))],
        out_specs=[blk] * 4)
    return pl.pallas_call(
        body, name=name, grid_spec=grid_spec,
        out_shape=[jax.ShapeDtypeStruct((1, r, c), F32)] * 4,
        compiler_params=_params(("parallel",)),
    )(pos_vec, w, m, v, land, land, land, land, own)


def _adamw_small(red, params):
    names = ["sinks", "g_attn", "g_conv", "ln1_g", "ln1_b", "ln2_g", "ln2_b", "conv_w"]
    d = red.shape[1]
    hd = d // 2
    flat = []
    for nme in names:
        flat.extend(params[nme])
    nq = params["sinks"][0].shape[1]
    cs = params["conv_w"][0].shape[2]

    def body(*refs):
        red_ref = refs[0]
        ins = refs[1:1 + 3 * len(names)]
        outs = refs[1 + 3 * len(names):]
        x, y, _, _ = _mesh_pos()
        me = _chip_id(x, y)

        def conv_tap(row, base):
            picked = red_ref[row:row + 1, base:base + cs]
            for s in range(1, N_CHIPS):
                picked = jnp.where(me == s, red_ref[row:row + 1, base + s * cs:base + (s + 1) * cs], picked)
            return picked

        grads = {
            "sinks": red_ref[6:7, hd:hd + nq],
            "g_attn": red_ref[4:5, 0:hd],
            "g_conv": red_ref[4:5, hd:d],
            "ln1_g": red_ref[2:3, :],
            "ln1_b": red_ref[3:4, :],
            "ln2_g": red_ref[0:1, :],
            "ln2_b": red_ref[1:2, :],
        }
        for i, nme in enumerate(names):
            w_ref, m_ref, v_ref = ins[3 * i:3 * i + 3]
            g_out, d_out, m_out, v_out = outs[4 * i:4 * i + 4]
            if nme == "conv_w":
                for tap, (row, base) in enumerate([(5, 0), (5, hd), (6, 0)]):
                    g = conv_tap(row, base)
                    delta, nm, nv = _adamw(w_ref[0, tap:tap + 1, :], g, m_ref[0, tap:tap + 1, :], v_ref[0, tap:tap + 1, :])
                    g_out[0, tap:tap + 1, :] = g
                    d_out[0, tap:tap + 1, :] = delta
                    m_out[0, tap:tap + 1, :] = nm
                    v_out[0, tap:tap + 1, :] = nv
            else:
                g = grads[nme]
                delta, nm, nv = _adamw(w_ref[...], g, m_ref[...], v_ref[...])
                g_out[...] = g
                d_out[...] = delta
                m_out[...] = nm
                v_out[...] = nv

    out_shape = []
    for nme in names:
        out_shape.extend([jax.ShapeDtypeStruct(params[nme][0].shape, F32)] * 4)
    outs = pl.pallas_call(
        body, name="adamw_small",
        in_specs=[_VMEM] * (1 + len(flat)), out_specs=[_VMEM] * len(out_shape),
        out_shape=out_shape,
    )(red, *flat)
    return {nme: tuple(outs[4 * i:4 * i + 4]) for i, nme in enumerate(names)}


def _rope_tables(pos_col):
    s = pos_col.shape[0]
    w = N_KV_HEADS * HEAD_DIM
    tb = min(512, s)
    inv_freq = (ROPE_THETA ** (-np.arange(0, ROT_DIM, 2, dtype=np.float32) / ROT_DIM)).astype(np.float32)

    def body(pos_ref, cos_ref, sin_ref):
        pos = pos_ref[...].astype(F32)
        lane = lax.broadcasted_iota(jnp.int32, (tb, w), 1) & (HEAD_DIM - 1)
        fidx = lane & (ROT_DIM // 2 - 1)
        inv = jnp.zeros((tb, w), F32)
        for k in range(ROT_DIM // 2):
            inv = jnp.where(fidx == k, float(inv_freq[k]), inv)
        ang = pos * inv
        rot = lane < ROT_DIM
        cos_ref[...] = jnp.where(rot, jnp.cos(ang), 1.0)
        sin_v = jnp.sin(ang)
        sin_ref[...] = jnp.where(lane < ROT_DIM // 2, -sin_v, jnp.where(rot, sin_v, 0.0))

    return pl.pallas_call(
        body, name="rope_tables", grid=(s // tb,),
        in_specs=[pl.BlockSpec((tb, 1), lambda i: (i, 0))],
        out_specs=[pl.BlockSpec((tb, w), lambda i: (i, 0))] * 2,
        out_shape=[jax.ShapeDtypeStruct((s, w), F32)] * 2,
        compiler_params=_params(("parallel",)),
    )(pos_col)


def _in_proj(x, w_in_g):
    _, s, d = x.shape
    ns, _, ncol = w_in_g.shape
    tm = min(2 * TM, s)

    def body(x_ref, w_ref, o_ref):
        o_ref[...] = _dot(x_ref[...].astype(BF16), w_ref[...])

    return pl.pallas_call(
        body, name="in_proj", grid=(s // tm, ns),
        in_specs=[pl.BlockSpec((None, tm, d), lambda i, j: (0, i, 0)),
                  pl.BlockSpec((None, d, ncol), lambda i, j: (j, 0, 0))],
        out_specs=pl.BlockSpec((tm, ncol), lambda i, j: (i, j)),
        out_shape=jax.ShapeDtypeStruct((s, ns * ncol), F32),
        compiler_params=_params(("parallel", "arbitrary")),
    )(x, w_in_g)


def _attention_scores(q, k_prev, k_cur, sinks_ref, h, first, valid):
    heads = [q[:, (GROUP * h + g) * HEAD_DIM:(GROUP * h + g + 1) * HEAD_DIM] for g in range(GROUP)]
    q4 = jnp.concatenate(heads, axis=0).astype(BF16)
    kk = jnp.concatenate([k_prev[:, h * HEAD_DIM:(h + 1) * HEAD_DIM], k_cur[:, h * HEAD_DIM:(h + 1) * HEAD_DIM]],
                         axis=0).astype(BF16)
    s = _dot_nt(q4, kk) * ATTN_SCALE
    s = jnp.where(valid, s, NEG_BIG)
    sink = jnp.concatenate(
        [jnp.broadcast_to(sinks_ref[0:1, GROUP * h + g:GROUP * h + g + 1], (WINDOW, 1)) for g in range(GROUP)], axis=0)
    m = jnp.maximum(jnp.max(s, axis=1, keepdims=True), sink)
    p = jnp.exp(s - m)
    p_sink = jnp.exp(sink - m)
    inv_l = 1.0 / (jnp.sum(p, axis=1, keepdims=True) + p_sink)
    return q4, kk, p * inv_l, p_sink * inv_l


def _band_mask(first):
    rows = GROUP * WINDOW
    qi = lax.broadcasted_iota(jnp.int32, (rows, 2 * WINDOW), 0) & (WINDOW - 1)
    kj = lax.broadcasted_iota(jnp.int32, (rows, 2 * WINDOW), 1)
    rel = qi + WINDOW - kj
    band = (rel >= 0) & (rel < WINDOW)
    return band & jnp.logical_not(first & (kj < WINDOW))


def _attention_fwd(proj, cos_t, sin_t, sinks):
    s = proj.shape[0]
    qw = GROUP * N_KV_HEADS * HEAD_DIM
    kvw = N_KV_HEADS * HEAD_DIM
    nb = s // WINDOW

    def body(cur_ref, prev_ref, cos_ref, sin_ref, cosp_ref, sinp_ref, sinks_ref, o_ref):
        n = pl.program_id(0)
        first = n == 0
        cur = cur_ref[...]
        cos, sin = cos_ref[...], sin_ref[...]
        q = _rope(cur[:, :qw], _tile_lanes(cos, GROUP), _tile_lanes(sin, GROUP), 1.0)
        k_cur = _rope(cur[:, qw:qw + kvw], cos, sin, 1.0)
        v_cur = cur[:, qw + kvw:]
        prev = prev_ref[...]
        k_prev = _rope(prev[:, :kvw], cosp_ref[...], sinp_ref[...], 1.0)
        v_prev = prev[:, kvw:]
        valid = _band_mask(first)
        outs = []
        for h in range(N_KV_HEADS):
            _, _, probs, _ = _attention_scores(q, k_prev, k_cur, sinks_ref, h, first, valid)
            vv = jnp.concatenate([v_prev[:, h * HEAD_DIM:(h + 1) * HEAD_DIM], v_cur[:, h * HEAD_DIM:(h + 1) * HEAD_DIM]],
                                 axis=0).astype(BF16)
            o = _dot(probs.astype(BF16), vv)
            outs.extend([o[g * WINDOW:(g + 1) * WINDOW] for g in range(GROUP)])
        o_ref[...] = jnp.concatenate(outs, axis=1)

    tbl = pl.BlockSpec((WINDOW, kvw), lambda n: (n, 0))
    tbl_prev = pl.BlockSpec((WINDOW, kvw), lambda n: (jnp.maximum(n - 1, 0), 0))
    return pl.pallas_call(
        body, name="attention_fwd", grid=(nb,),
        in_specs=[pl.BlockSpec((WINDOW, qw + 2 * kvw), lambda n: (n, 0)),
                  pl.BlockSpec((WINDOW, 2 * kvw), lambda n: (jnp.maximum(n - 1, 0), (qw // (2 * kvw)))),
                  tbl, tbl, tbl_prev, tbl_prev, _VMEM],
        out_specs=pl.BlockSpec((WINDOW, qw), lambda n: (n, 0)),
        out_shape=jax.ShapeDtypeStruct((s, qw), F32),
        compiler_params=_params(("parallel",)),
    )(proj, proj, cos_t, sin_t, cos_t, sin_t, sinks)


def _conv_taps(cw_ref):
    return [jnp.concatenate([cw_ref[s, k:k + 1, :] for s in range(N_CHIPS)], axis=1) for k in range(3)]


def _shift_down(z, halo, steps):
    rows = z.shape[0]
    row = lax.broadcasted_iota(jnp.int32, z.shape, 0)
    out = pltpu.roll(z, steps, 0)
    for r in range(steps):
        out = jnp.where(row == r, halo[8 - steps + r:8 - steps + r + 1, :], out)
    return out


def _shift_up(z, halo, steps):
    rows = z.shape[0]
    row = lax.broadcasted_iota(jnp.int32, z.shape, 0)
    out = pltpu.roll(z, rows - steps, 0)
    for r in range(steps):
        out = jnp.where(row == rows - steps + r, halo[r:r + 1, :], out)
    return out


def _split_cbu(lo, hi, cw):
    c_gate = lo[:, :cw]
    b_gate = jnp.concatenate([lo[:, cw:], hi[:, :2 * cw - lo.shape[1]]], axis=1)
    u = hi[:, 2 * cw - lo.shape[1]:]
    return c_gate, b_gate, u


def _conv_norm(proj, attn, cw_full, g_ac):
    s, in_w = proj.shape
    cw = attn.shape[1]
    blk_w = in_w // 3
    tb = min(TB_CONV, s)

    def body(lo_ref, hi_ref, lo_h_ref, hi_h_ref, attn_ref, cw_ref, g_ref, mixed_ref, ac_ref, rstd_ref):
        i = pl.program_id(0)
        c_gate, b_gate, u = _split_cbu(lo_ref[...], hi_ref[...], cw)
        c_h, _, u_h = _split_cbu(lo_h_ref[...], hi_h_ref[...], cw)
        z = c_gate * u
        z_h = jnp.where(i == 0, 0.0, c_h * u_h)
        w0, w1, w2 = _conv_taps(cw_ref)
        y = w0 * _shift_down(z, z_h, 2) + w1 * _shift_down(z, z_h, 1) + w2 * z
        conv = b_gate * y
        a = attn_ref[...]
        r_a = lax.rsqrt(jnp.mean(a * a, axis=-1, keepdims=True) + RMS_EPS)
        r_c = lax.rsqrt(jnp.mean(conv * conv, axis=-1, keepdims=True) + RMS_EPS)
        g = g_ref[...]
        mixed_ref[...] = jnp.concatenate([a * r_a * g[:, :cw], conv * r_c * g[:, cw:]], axis=1).astype(BF16)
        ac_ref[...] = jnp.concatenate([a, conv], axis=1)
        rstd_ref[0] = r_a
        rstd_ref[1] = r_c

    halo_idx = lambda i: jnp.maximum(i * (tb // 8) - 1, 0)
    return pl.pallas_call(
        body, name="conv_norm", grid=(s // tb,),
        in_specs=[pl.BlockSpec((tb, blk_w), lambda i: (i, 1)),
                  pl.BlockSpec((tb, blk_w), lambda i: (i, 2)),
                  pl.BlockSpec((8, blk_w), lambda i: (halo_idx(i), 1)),
                  pl.BlockSpec((8, blk_w), lambda i: (halo_idx(i), 2)),
                  pl.BlockSpec((tb, cw), lambda i: (i, 0)),
                  _VMEM, _VMEM],
        out_specs=[pl.BlockSpec((tb, 2 * cw), lambda i: (i, 0)),
                   pl.BlockSpec((tb, 2 * cw), lambda i: (i, 0)),
                   pl.BlockSpec((2, tb, 1), lambda i: (0, i, 0))],
        out_shape=[jax.ShapeDtypeStruct((s, 2 * cw), BF16), jax.ShapeDtypeStruct((s, 2 * cw), F32),
                   jax.ShapeDtypeStruct((2, s, 1), F32)],
        compiler_params=_params(("parallel",)),
    )(proj, proj, proj, proj, attn, cw_full, g_ac)


def _out_proj_ln(mixed, w_out_g, x, ln_g, ln_b):
    s, d = mixed.shape
    tm = min(TM, s)
    tk = d
    nk = d // tk

    def body(a_ref, w_ref, x_ref, g_ref, b_ref, xhat_ref, h_ref, rstd_ref, acc):
        k = pl.program_id(1)
        _accumulate(acc, _dot(a_ref[...], w_ref[...]), k, nk)

        @pl.when(k == nk - 1)
        def _():
            def rows_fn(rows):
                xhat, rstd = _ln_fwd(ALPHA * x_ref[rows, :] + acc[rows, :])
                xhat_ref[rows, :] = xhat
                h_ref[rows, :] = (xhat * g_ref[...] + b_ref[...]).astype(BF16)
                rstd_ref[rows, :] = rstd

            _for_row_chunks(tm, rows_fn)

    row = pl.BlockSpec((tm, d), lambda i, k: (i, 0))
    return pl.pallas_call(
        body, name="out_proj_ln", grid=(s // tm, nk),
        in_specs=[pl.BlockSpec((tm, tk), lambda i, k: (i, k)),
                  pl.BlockSpec((tk, d), lambda i, k: (k, 0)),
                  pl.BlockSpec((None, tm, d), lambda i, k: (0, i, 0)),
                  _VMEM, _VMEM],
        out_specs=[row, row, pl.BlockSpec((tm, 1), lambda i, k: (i, 0))],
        out_shape=[jax.ShapeDtypeStruct((s, d), F32), jax.ShapeDtypeStruct((s, d), BF16),
                   jax.ShapeDtypeStruct((s, 1), F32)],
        scratch_shapes=[pltpu.VMEM((tm, d), F32)],
        compiler_params=_params(("parallel", "arbitrary")),
    )(mixed, w_out_g, x, ln_g, ln_b)


def _gate_up(h1, w_gate_g, w_up_g):
    s, d = h1.shape
    ns, _, fs = w_gate_g.shape
    tm = min(TM, s)
    tk = d
    nk = d // tk

    def body(h_ref, wg_ref, wu_ref, act_ref, g_ref, u_ref, acc_g, acc_u):
        k = pl.program_id(2)
        h = h_ref[...]
        _accumulate(acc_g, _dot(h, wg_ref[...]), k, nk)
        _accumulate(acc_u, _dot(h, wu_ref[...]), k, nk)

        @pl.when(k == nk - 1)
        def _():
            def rows_fn(rows):
                g, u = acc_g[rows, :], acc_u[rows, :]
                act_ref[rows, :] = (g * _sigmoid(g) * u).astype(BF16)
                g_ref[rows, :] = g.astype(BF16)
                u_ref[rows, :] = u.astype(BF16)

            _for_row_chunks(tm, rows_fn)

    wspec = pl.BlockSpec((None, tk, fs), lambda i, j, k: (j, k, 0))
    ospec = pl.BlockSpec((tm, fs), lambda i, j, k: (i, j))
    return pl.pallas_call(
        body, name="gate_up", grid=(s // tm, ns, nk),
        in_specs=[pl.BlockSpec((tm, tk), lambda i, j, k: (i, k)), wspec, wspec],
        out_specs=[ospec] * 3,
        out_shape=[jax.ShapeDtypeStruct((s, ns * fs), BF16)] * 3,
        scratch_shapes=[pltpu.VMEM((tm, fs), F32)] * 2,
        compiler_params=_params(("parallel", "arbitrary", "arbitrary")),
    )(h1, w_gate_g, w_up_g)


def _down_ln_loss(act, w_down_g, xhat1, ln1_g, ln1_b, ln2_g, ln2_b, target):
    s, f = act.shape
    d = xhat1.shape[1]
    tm = min(TM, s)
    tk = f // N_CHIPS
    nk = f // tk

    def body(a_ref, w_ref, xh_ref, g1_ref, b1_ref, g2_ref, b2_ref, t_ref, dpre_ref, loss_ref, gg_ref, gb_ref, acc):
        i, k = pl.program_id(0), pl.program_id(1)
        _accumulate(acc, _dot(a_ref[...], w_ref[...]), k, nk)

        @pl.when(k == nk - 1)
        def _():
            @pl.when(i == 0)
            def _():
                loss_ref[...] = jnp.zeros_like(loss_ref)
                gg_ref[...] = jnp.zeros_like(gg_ref)
                gb_ref[...] = jnp.zeros_like(gb_ref)

            def rows_fn(rows):
                h1 = xh_ref[rows, :] * g1_ref[...] + b1_ref[...]
                xhat, rstd = _ln_fwd(ALPHA * h1 + acc[rows, :])
                g2 = g2_ref[...]
                diff = xhat * g2 + b2_ref[...] - t_ref[rows, :]
                dy = diff * (1.0 / d)
                dpre_ref[rows, :] = _ln_bwd(dy, xhat, rstd, g2)
                sq = jnp.sum(jnp.sum(diff * diff, axis=1, keepdims=True), axis=0, keepdims=True)
                loss_ref[...] += jnp.broadcast_to(sq * (0.5 / d), (1, 128))
                gg_ref[...] += jnp.sum(dy * xhat, axis=0, keepdims=True)
                gb_ref[...] += jnp.sum(dy, axis=0, keepdims=True)

            _for_row_chunks(tm, rows_fn)

    row = pl.BlockSpec((tm, d), lambda i, k: (i, 0))
    vec = pl.BlockSpec((1, d), lambda i, k: (0, 0))
    return pl.pallas_call(
        body, name="down_ln_loss", grid=(s // tm, nk),
        in_specs=[pl.BlockSpec((tm, tk), lambda i, k: (i, k)),
                  pl.BlockSpec((tk, d), lambda i, k: (k, 0)),
                  row, _VMEM, _VMEM, _VMEM, _VMEM,
                  pl.BlockSpec((None, tm, d), lambda i, k: (0, i, 0))],
        out_specs=[row, pl.BlockSpec((1, 128), lambda i, k: (0, 0)), vec, vec],
        out_shape=[jax.ShapeDtypeStruct((s, d), F32), jax.ShapeDtypeStruct((1, 128), F32),
                   jax.ShapeDtypeStruct((1, d), F32), jax.ShapeDtypeStruct((1, d), F32)],
        scratch_shapes=[pltpu.VMEM((tm, d), F32)],
        compiler_params=_params(("arbitrary", "arbitrary")),
    )(act, w_down_g, xhat1, ln1_g, ln1_b, ln2_g, ln2_b, target)


def _dact_silu_bwd(dpre2, w_down_g, gate, up):
    s, d = dpre2.shape
    f = gate.shape[1]
    fs = f // N_CHIPS
    tm = min(TM, s)

    def body(dp_ref, w_ref, g_ref, u_ref, dg_ref, du_ref):
        d_act = _dot_nt(dp_ref[...].astype(BF16), w_ref[...])
        g = g_ref[...].astype(F32)
        u = u_ref[...].astype(F32)
        sg = _sigmoid(g)
        dg_ref[...] = (d_act * u * (sg * (1.0 + g * (1.0 - sg)))).astype(BF16)
        du_ref[...] = (d_act * (g * sg)).astype(BF16)

    blk = pl.BlockSpec((tm, fs), lambda i, j: (i, j))
    return pl.pallas_call(
        body, name="dact_silu_bwd", grid=(s // tm, N_CHIPS),
        in_specs=[pl.BlockSpec((tm, d), lambda i, j: (i, 0)),
                  pl.BlockSpec((fs, d), lambda i, j: (j, 0)), blk, blk],
        out_specs=[blk, blk],
        out_shape=[jax.ShapeDtypeStruct((s, f), BF16)] * 2,
        compiler_params=_params(("parallel", "arbitrary")),
    )(dpre2, w_down_g, gate, up)


def _grad_rows(a, b, after, name, row_blocks=1):
    s, m = a.shape
    n = b.shape[1]
    ms = m // N_CHIPS
    tmw = ms // row_blocks
    tk = min(TK_TOK, s)
    nk = s // tk

    def body(a_ref, b_ref, after_ref, o_ref, acc):
        k = pl.program_id(2)
        _accumulate(acc, _dot_tn(a_ref[...].astype(BF16), b_ref[...].astype(BF16)), k, nk)

        @pl.when(k == nk - 1)
        def _():
            o_ref[...] = acc[...].astype(BF16)

    return pl.pallas_call(
        body, name=name, grid=(N_CHIPS, row_blocks, nk),
        in_specs=[pl.BlockSpec((tk, tmw), lambda j, r, k: (k, j * row_blocks + r)),
                  pl.BlockSpec((tk, n), lambda j, r, k: (k, 0)), _ANY],
        out_specs=pl.BlockSpec((None, tmw, n), lambda j, r, k: (j, r, 0)),
        out_shape=jax.ShapeDtypeStruct((N_CHIPS, ms, n), BF16),
        scratch_shapes=[pltpu.VMEM((tmw, n), F32)],
        compiler_params=_params(("parallel", "parallel", "arbitrary")),
    )(a, b, after)


def _grad_cols(a, bs, after, name, a_3d=False, row_blocks=2):
    s, m = a.shape[-2:]
    n = bs[0].shape[1]
    ns = n // N_CHIPS
    nb = len(bs)
    tmw = m // row_blocks
    tk = min(TK_TOK, s)
    nk = s // tk

    def body(*refs):
        a_ref, b_refs, o_refs, accs = refs[0], refs[1:1 + nb], refs[2 + nb:2 + 2 * nb], refs[2 + 2 * nb:]
        k = pl.program_id(2)
        at = a_ref[...].astype(BF16)
        for b_ref, acc in zip(b_refs, accs):
            _accumulate(acc, _dot_tn(at, b_ref[...].astype(BF16)), k, nk)

        @pl.when(k == nk - 1)
        def _():
            for o_ref, acc in zip(o_refs, accs):
                o_ref[...] = acc[...].astype(BF16)

    if a_3d:
        a_spec = pl.BlockSpec((None, tk, tmw), lambda j, r, k: (0, k, r))
    else:
        a_spec = pl.BlockSpec((tk, tmw), lambda j, r, k: (k, r))
    return pl.pallas_call(
        body, name=name, grid=(N_CHIPS, row_blocks, nk),
        in_specs=[a_spec] + [pl.BlockSpec((tk, ns), lambda j, r, k: (k, j))] * nb + [_ANY],
        out_specs=[pl.BlockSpec((None, tmw, ns), lambda j, r, k: (j, r, 0))] * nb,
        out_shape=[jax.ShapeDtypeStruct((N_CHIPS, m, ns), BF16)] * nb,
        scratch_shapes=[pltpu.VMEM((tmw, ns), F32)] * nb,
        compiler_params=_params(("parallel", "parallel", "arbitrary")),
    )(a, *bs, after)


def _dh1_ln_bwd(d_gate, d_up, w_gate_g, w_up_g, dpre2, xhat1, rstd1, ln1_g, after):
    s, f = d_gate.shape
    d = dpre2.shape[1]
    hd = d // 2
    fs = f // N_CHIPS
    tm = min(TM, s)

    def body(dg_ref, du_ref, wg_ref, wu_ref, dp2_ref, xh_ref, rs_ref, g_ref, after_ref, dpre_ref, gg_ref, gb_ref,
             acc_lo, acc_hi):
        i, j, half = pl.program_id(0), pl.program_id(1), pl.program_id(2)
        val = _dot_nt(dg_ref[...], wg_ref[...]) + _dot_nt(du_ref[...], wu_ref[...])

        @pl.when(half == 0)
        def _():
            _accumulate(acc_lo, val, j, N_CHIPS)

        @pl.when(half == 1)
        def _():
            _accumulate(acc_hi, val, j, N_CHIPS)

        @pl.when((j == N_CHIPS - 1) & (half == 1))
        def _():
            @pl.when(i == 0)
            def _():
                gg_ref[...] = jnp.zeros_like(gg_ref)
                gb_ref[...] = jnp.zeros_like(gb_ref)

            def rows_fn(rows):
                dh = jnp.concatenate([acc_lo[rows, :], acc_hi[rows, :]], axis=1) + ALPHA * dp2_ref[rows, :]
                xhat = xh_ref[rows, :]
                dpre_ref[rows, :] = _ln_bwd(dh, xhat, rs_ref[rows, :], g_ref[...])
                gg_ref[...] += jnp.sum(dh * xhat, axis=0, keepdims=True)
                gb_ref[...] += jnp.sum(dh, axis=0, keepdims=True)

            _for_row_chunks(tm, rows_fn)

    row = pl.BlockSpec((tm, d), lambda i, j, h: (i, 0))
    vec = pl.BlockSpec((1, d), lambda i, j, h: (0, 0))
    act_blk = pl.BlockSpec((tm, fs), lambda i, j, h: (i, j))
    w_blk = pl.BlockSpec((None, hd, fs), lambda i, j, h: (j, h, 0))
    return pl.pallas_call(
        body, name="dh1_ln_bwd", grid=(s // tm, N_CHIPS, 2),
        in_specs=[act_blk, act_blk, w_blk, w_blk, row, row, pl.BlockSpec((tm, 1), lambda i, j, h: (i, 0)), _VMEM,
                  _ANY],
        out_specs=[row, vec, vec],
        out_shape=[jax.ShapeDtypeStruct((s, d), F32), jax.ShapeDtypeStruct((1, d), F32),
                   jax.ShapeDtypeStruct((1, d), F32)],
        scratch_shapes=[pltpu.VMEM((tm, hd), F32)] * 2,
        compiler_params=_params(("arbitrary", "arbitrary", "arbitrary")),
    )(d_gate, d_up, w_gate_g, w_up_g, dpre2, xhat1, rstd1, ln1_g, after)


def _dmixed_rms_bwd(dpre1, w_out_g, ac, rstd, g_ac):
    s, d = dpre1.shape
    hd = d // 2
    tm = min(TM, s)

    def body(dp_ref, w_ref, ac_ref, rs_ref, g_ref, dac_ref, gg_ref):
        i = pl.program_id(1)
        dm = _dot_nt(dp_ref[...].astype(BF16), w_ref[...])
        pre = ac_ref[...]
        r = rs_ref[...]
        gdm = dm * g_ref[...]
        dac_ref[...] = r * gdm - pre * (r * r * r) * jnp.mean(gdm * pre, axis=-1, keepdims=True)
        gg = jnp.sum(dm * pre * r, axis=0, keepdims=True)

        @pl.when(i == 0)
        def _():
            gg_ref[...] = gg

        @pl.when(i > 0)
        def _():
            gg_ref[...] += gg

    return pl.pallas_call(
        body, name="dmixed_rms_bwd", grid=(2, s // tm),
        in_specs=[pl.BlockSpec((tm, d), lambda h, i: (i, 0)),
                  pl.BlockSpec((hd, d), lambda h, i: (h, 0)),
                  pl.BlockSpec((tm, hd), lambda h, i: (i, h)),
                  pl.BlockSpec((None, tm, 1), lambda h, i: (h, i, 0)),
                  pl.BlockSpec((1, hd), lambda h, i: (0, h))],
        out_specs=[pl.BlockSpec((tm, hd), lambda h, i: (i, h)),
                   pl.BlockSpec((1, hd), lambda h, i: (0, h))],
        out_shape=[jax.ShapeDtypeStruct((s, d), F32), jax.ShapeDtypeStruct((1, d), F32)],
        compiler_params=_params(("arbitrary", "arbitrary")),
    )(dpre1, w_out_g, ac, rstd, g_ac)


def _attention_bwd(proj, d_ac, cos_t, sin_t, sinks, after):
    s = proj.shape[0]
    qw = GROUP * N_KV_HEADS * HEAD_DIM
    kvw = N_KV_HEADS * HEAD_DIM
    nb = s // WINDOW
    nq = GROUP * N_KV_HEADS

    def body(cur_ref, prev_ref, do_ref, cos_ref, sin_ref, cosp_ref, sinp_ref, sinks_ref, after_ref,
             dq_ref, dcur_ref, dprev_ref, dsink_ref):
        n = pl.program_id(0)
        first = n == 0
        cur = cur_ref[...]
        cos, sin = cos_ref[...], sin_ref[...]
        cos_q, sin_q = _tile_lanes(cos, GROUP), _tile_lanes(sin, GROUP)
        q = _rope(cur[:, :qw], cos_q, sin_q, 1.0)
        k_cur = _rope(cur[:, qw:qw + kvw], cos, sin, 1.0)
        v_cur = cur[:, qw + kvw:]
        prev = prev_ref[...]
        k_prev = _rope(prev[:, :kvw], cosp_ref[...], sinp_ref[...], 1.0)
        v_prev = prev[:, kvw:]
        d_out = do_ref[...]
        valid = _band_mask(first)
        dq_parts, dk_parts, dv_parts, dsink_parts = [], [], [], []
        for h in range(N_KV_HEADS):
            q4, kk, probs, p_sink = _attention_scores(q, k_prev, k_cur, sinks_ref, h, first, valid)
            vv = jnp.concatenate([v_prev[:, h * HEAD_DIM:(h + 1) * HEAD_DIM], v_cur[:, h * HEAD_DIM:(h + 1) * HEAD_DIM]],
                                 axis=0).astype(BF16)
            do4 = jnp.concatenate(
                [d_out[:, (GROUP * h + g) * HEAD_DIM:(GROUP * h + g + 1) * HEAD_DIM] for g in range(GROUP)],
                axis=0).astype(BF16)
            d_probs = _dot_nt(do4, vv)
            delta = jnp.sum(probs * d_probs, axis=1, keepdims=True)
            d_s = (probs * (d_probs - delta) * ATTN_SCALE).astype(BF16)
            dq4 = _dot(d_s, kk)
            dq_parts.extend([dq4[g * WINDOW:(g + 1) * WINDOW] for g in range(GROUP)])
            dk_parts.append(_dot_tn(d_s, q4))
            dv_parts.append(_dot_tn(probs.astype(BF16), do4))
            ds_sink = -p_sink * delta
            dsink_parts.extend([jnp.sum(ds_sink[g * WINDOW:(g + 1) * WINDOW], axis=0, keepdims=True)
                                for g in range(GROUP)])
        dq_ref[...] = _rope(jnp.concatenate(dq_parts, axis=1), cos_q, sin_q, -1.0)
        dk = jnp.concatenate(dk_parts, axis=1)
        dv = jnp.concatenate(dv_parts, axis=1)
        dprev_ref[...] = jnp.concatenate([dk[:WINDOW], dv[:WINDOW]], axis=1)
        dcur_ref[...] = jnp.concatenate([dk[WINDOW:], dv[WINDOW:]], axis=1)
        dsink = jnp.concatenate(dsink_parts, axis=1)

        @pl.when(first)
        def _():
            dsink_ref[...] = dsink

        @pl.when(n > 0)
        def _():
            dsink_ref[...] += dsink

    tbl = pl.BlockSpec((WINDOW, kvw), lambda n: (n, 0))
    tbl_prev = pl.BlockSpec((WINDOW, kvw), lambda n: (jnp.maximum(n - 1, 0), 0))
    kv_blk = pl.BlockSpec((WINDOW, 2 * kvw), lambda n: (n, 0))
    return pl.pallas_call(
        body, name="attention_bwd", grid=(nb,),
        in_specs=[pl.BlockSpec((WINDOW, qw + 2 * kvw), lambda n: (n, 0)),
                  pl.BlockSpec((WINDOW, 2 * kvw), lambda n: (jnp.maximum(n - 1, 0), (qw // (2 * kvw)))),
                  pl.BlockSpec((WINDOW, qw), lambda n: (n, 0)),
                  tbl, tbl, tbl_prev, tbl_prev, _VMEM, _ANY],
        out_specs=[pl.BlockSpec((WINDOW, qw), lambda n: (n, 0)), kv_blk, kv_blk,
                   pl.BlockSpec((1, nq), lambda n: (0, 0))],
        out_shape=[jax.ShapeDtypeStruct((s, qw), F32), jax.ShapeDtypeStruct((s, 2 * kvw), F32),
                   jax.ShapeDtypeStruct((s, 2 * kvw), F32), jax.ShapeDtypeStruct((1, nq), F32)],
        compiler_params=_params(("arbitrary",)),
    )(proj, proj, d_ac, cos_t, sin_t, cos_t, sin_t, sinks, after)


def _dproj_assemble(proj, d_ac, dq, dkv_cur, dkv_prev, cos_t, sin_t, cw_full):
    s, in_w = proj.shape
    cw = dq.shape[1]
    kvw = N_KV_HEADS * HEAD_DIM
    blk_w = in_w // 3
    tb = WINDOW
    nb = s // tb

    def body(lo_ref, hi_ref, lo_p_ref, hi_p_ref, lo_n_ref, hi_n_ref, dconv_ref, dconv_n_ref,
             dq_ref, dcur_ref, dprev_n_ref, cos_ref, sin_ref, cw_ref, dproj_ref, gcw_ref):
        i = pl.program_id(0)
        last = i == nb - 1
        c_gate, b_gate, u = _split_cbu(lo_ref[...], hi_ref[...], cw)
        c_p, _, u_p = _split_cbu(lo_p_ref[...], hi_p_ref[...], cw)
        _, b_n, _ = _split_cbu(lo_n_ref[...], hi_n_ref[...], cw)
        z = c_gate * u
        z_p = jnp.where(i == 0, 0.0, c_p * u_p)
        z1 = _shift_down(z, z_p, 1)
        z2 = _shift_down(z, z_p, 2)
        w0, w1, w2 = _conv_taps(cw_ref)
        y = w0 * z2 + w1 * z1 + w2 * z
        d_conv = dconv_ref[...]
        d_b = d_conv * y
        d_y = d_conv * b_gate
        d_y_n = jnp.where(last, 0.0, dconv_n_ref[...] * b_n)
        d_z = w2 * d_y + w1 * _shift_up(d_y, d_y_n, 1) + w0 * _shift_up(d_y, d_y_n, 2)
        d_c = d_z * u
        d_u = d_z * c_gate
        gcw = jnp.concatenate([jnp.sum(d_y * z2, axis=0, keepdims=True), jnp.sum(d_y * z1, axis=0, keepdims=True),
                               jnp.sum(d_y * z, axis=0, keepdims=True)], axis=0)

        @pl.when(i == 0)
        def _():
            gcw_ref[...] = gcw

        @pl.when(i > 0)
        def _():
            gcw_ref[...] += gcw

        dkv = dcur_ref[...] + jnp.where(last, 0.0, dprev_n_ref[...])
        dk = _rope(dkv[:, :kvw], cos_ref[...], sin_ref[...], -1.0)
        dproj_ref[...] = jnp.concatenate([dq_ref[...], dk, dkv[:, kvw:], d_c, d_b, d_u], axis=1).astype(BF16)

    prev8 = lambda i: jnp.maximum(i * (tb // 8) - 1, 0)
    next8 = lambda i: jnp.minimum((i + 1) * (tb // 8), s // 8 - 1)
    nxt = lambda i: jnp.minimum(i + 1, nb - 1)
    return pl.pallas_call(
        body, name="dproj_assemble", grid=(nb,),
        in_specs=[pl.BlockSpec((tb, blk_w), lambda i: (i, 1)),
                  pl.BlockSpec((tb, blk_w), lambda i: (i, 2)),
                  pl.BlockSpec((8, blk_w), lambda i: (prev8(i), 1)),
                  pl.BlockSpec((8, blk_w), lambda i: (prev8(i), 2)),
                  pl.BlockSpec((8, blk_w), lambda i: (next8(i), 1)),
                  pl.BlockSpec((8, blk_w), lambda i: (next8(i), 2)),
                  pl.BlockSpec((tb, cw), lambda i: (i, 1)),
                  pl.BlockSpec((8, cw), lambda i: (next8(i), 1)),
                  pl.BlockSpec((tb, cw), lambda i: (i, 0)),
                  pl.BlockSpec((tb, 2 * kvw), lambda i: (i, 0)),
                  pl.BlockSpec((tb, 2 * kvw), lambda i: (nxt(i), 0)),
                  pl.BlockSpec((tb, kvw), lambda i: (i, 0)),
                  pl.BlockSpec((tb, kvw), lambda i: (i, 0)),
                  _VMEM],
        out_specs=[pl.BlockSpec((tb, in_w), lambda i: (i, 0)),
                   pl.BlockSpec((3, cw), lambda i: (0, 0))],
        out_shape=[jax.ShapeDtypeStruct((s, in_w), BF16), jax.ShapeDtypeStruct((3, cw), F32)],
        compiler_params=_params(("arbitrary",)),
    )(proj, proj, proj, proj, proj, proj, d_ac, d_ac, dq, dkv_cur, dkv_prev, cos_t, sin_t, cw_full)


def _dx(d_proj, w_in_g, dpre1, after):
    s, in_w = d_proj.shape
    ns, d, ncol = w_in_g.shape
    tm = min(TM, s)

    def body(dp_ref, w_ref, r_ref, after_ref, o_ref, acc):
        j = pl.program_id(1)
        _accumulate(acc, _dot_nt(dp_ref[...], w_ref[...]), j, ns)

        @pl.when(j == ns - 1)
        def _():
            o_ref[...] = acc[...] + ALPHA * r_ref[...]

    return pl.pallas_call(
        body, name="dx", grid=(s // tm, ns),
        in_specs=[pl.BlockSpec((tm, ncol), lambda i, j: (i, j)),
                  pl.BlockSpec((None, d, ncol), lambda i, j: (j, 0, 0)),
                  pl.BlockSpec((tm, d), lambda i, j: (i, 0)), _ANY],
        out_specs=pl.BlockSpec((None, tm, d), lambda i, j: (0, i, 0)),
        out_shape=jax.ShapeDtypeStruct((1, s, d), F32),
        scratch_shapes=[pltpu.VMEM((tm, d), F32)],
        compiler_params=_params(("parallel", "arbitrary")),
    )(d_proj, w_in_g, dpre1, after)


def kernel(x, positions, w_in, conv_w, sinks, g_attn, g_conv, w_out, ln1_g, ln1_b, w_gate, w_up, w_down, ln2_g, ln2_b, loss_target, m_w_in, m_conv_w, m_sinks, m_g_attn, m_g_conv, m_w_out, m_ln1_g, m_ln1_b, m_w_gate, m_w_up, m_w_down, m_ln2_g, m_ln2_b, v_w_in, v_conv_w, v_sinks, v_g_attn, v_g_conv, v_w_out, v_ln1_g, v_ln1_b, v_w_gate, v_w_up, v_w_down, v_ln2_g, v_ln2_b):
    s = x.shape[1]
    d = x.shape[2]

    chip_vec = _chip_id(lax.axis_index("x"), lax.axis_index("y")).astype(jnp.int32).reshape(1)
    wnames = ["w_in", "w_out", "w_gate", "w_up", "w_down"]
    cw_full = _allgather_conv_w(conv_w)
    buf_in = _cast_weight(w_in, chip_vec, cw_full, "cast_w_in")
    flight_in, token_in = _gather_start([buf_in], cw_full, "gather_start_w_in")
    bufs = [_cast_weight(w, chip_vec, token_in, "cast_" + nme)
            for w, nme in zip([w_out, w_gate, w_up, w_down], wnames[1:])]
    flights_rest, token = _gather_start(bufs, token_in, "gather_start_rest")
    flights = flight_in + flights_rest

    def gathered(i, after):
        send_sems, recv_sems, buf = flights[i]
        buf = _gather_wait(send_sems, recv_sems, buf, after, "gather_wait_" + wnames[i])
        return _sibling_fill(buf, "sibling_fill_" + wnames[i])

    g_ac = jnp.concatenate([g_attn, g_conv], axis=1)

    cos_t, sin_t = _rope_tables(positions.reshape(s, 1) + token[0:1, 0:1].astype(jnp.int32))
    w_in_g = gathered(0, cos_t)
    proj = _in_proj(x, w_in_g)
    w_out_full = gathered(1, proj).reshape(d, d)
    attn = _attention_fwd(proj, cos_t, sin_t, sinks)
    mixed, ac, rstd_ac = _conv_norm(proj, attn, cw_full, g_ac)
    w_gate_g = gathered(2, mixed)
    w_up_g = gathered(3, mixed)
    xhat1, h1, rstd1 = _out_proj_ln(mixed, w_out_full, x, ln1_g, ln1_b)
    w_down_full = gathered(4, h1).reshape(-1, d)
    act, gate, up = _gate_up(h1, w_gate_g, w_up_g)
    dpre2, loss_part, g_ln2_g, g_ln2_b = _down_ln_loss(act, w_down_full, xhat1, ln1_g, ln1_b, ln2_g, ln2_b, loss_target)

    cvec = lax.axis_index("c").astype(jnp.int32).reshape(1)

    def reduce_begin(part, after, nme):
        (got,) = _exchange_halves([part], after, "exchange_halves_" + nme)
        chip_sum = _add_halves(part, got, cvec, "add_halves_" + nme)
        return _scatter_start(chip_sum, "scatter_start_" + nme)

    d_gate, d_up = _dact_silu_bwd(dpre2, w_down_full, gate, up)
    p_down = _grad_rows(act, dpre2, d_gate, "grad_w_down")
    f_down = reduce_begin(p_down, p_down, "w_down")
    p_gate, p_up = _grad_cols(h1, [d_gate, d_up], f_down[2], "grad_w_gate_up")
    f_gate = reduce_begin(p_gate, p_gate, "w_gate")
    f_up = reduce_begin(p_up, f_gate[2], "w_up")
    dpre1, g_ln1_g, g_ln1_b = _dh1_ln_bwd(d_gate, d_up, w_gate_g, w_up_g, dpre2, xhat1, rstd1, ln1_g, f_up[2])
    d_ac, g_g_ac = _dmixed_rms_bwd(dpre1, w_out_full, ac, rstd_ac, g_ac)
    p_out = _grad_rows(mixed, dpre1, d_ac, "grad_w_out")
    f_out = reduce_begin(p_out, p_out, "w_out")
    dq, dkv_cur, dkv_prev, g_sinks = _attention_bwd(proj, d_ac, cos_t, sin_t, sinks, f_out[2])
    d_proj, g_conv_w = _dproj_assemble(proj, d_ac, dq, dkv_cur, dkv_prev, cos_t, sin_t, cw_full)
    (p_in,) = _grad_cols(x, [d_proj], d_proj, "grad_w_in", a_3d=True)
    f_in = reduce_begin(p_in, p_in, "w_in")
    grad_x = _dx(d_proj, w_in_g, dpre1, f_in[2])
    red = _allreduce_small(g_ln2_g, g_ln2_b, g_ln1_g, g_ln1_b, g_g_ac, g_conv_w, g_sinks, loss_part, grad_x)

    names = ["w_in", "w_out", "w_gate", "w_up", "w_down"]
    landed = [_scatter_wait(*f, red, "scatter_wait_" + nme)
              for f, nme in zip([f_in, f_out, f_gate, f_up, f_down], names)]
    own_sums = [sm for sm, _ in landed]
    lands = _complete_chip_sums(own_sums, [land for _, land in landed])

    pos_vec = jnp.concatenate([chip_vec, cvec])
    big = {}
    for nme, w, m, v, land, own in zip(names, [w_in, w_out, w_gate, w_up, w_down],
                                       [m_w_in, m_w_out, m_w_gate, m_w_up, m_w_down],
                                       [v_w_in, v_w_out, v_w_gate, v_w_up, v_w_down], lands, own_sums):
        big[nme] = _adamw_shard(w, m, v, land, own, pos_vec, "adamw_" + nme)
    small = _adamw_small(red, {
        "sinks": (sinks, m_sinks, v_sinks), "g_attn": (g_attn, m_g_attn, v_g_attn),
        "g_conv": (g_conv, m_g_conv, v_g_conv), "ln1_g": (ln1_g, m_ln1_g, v_ln1_g),
        "ln1_b": (ln1_b, m_ln1_b, v_ln1_b), "ln2_g": (ln2_g, m_ln2_g, v_ln2_g),
        "ln2_b": (ln2_b, m_ln2_b, v_ln2_b), "conv_w": (conv_w, m_conv_w, v_conv_w)})
    res = {**big, **small}
    order = ["w_in", "conv_w", "sinks", "g_attn", "g_conv", "w_out", "ln1_g", "ln1_b", "w_gate", "w_up", "w_down",
             "ln2_g", "ln2_b"]
    loss = red[6, d // 2 + 128]
    return (loss, grad_x, *[res[n][0] for n in order], *[res[n][1] for n in order],
            *[res[n][2] for n in order], *[res[n][3] for n in order])
```

```python
import functools

import numpy as np
import jax
import jax.numpy as jnp
from jax import lax
from jax.experimental import pallas as pl
from jax.experimental.pallas import tpu as pltpu

F32 = jnp.float32
BF16 = jnp.bfloat16
MESH = pl.DeviceIdType.MESH

HEAD_DIM = 64
N_KV_HEADS = 4
GROUP = 4
WINDOW = 128
ROT_DIM = 16
ROPE_THETA = 500000.0
ATTN_SCALE = HEAD_DIM ** -0.5
ALPHA = 2.0 ** 0.25
LN_EPS = 1e-5
RMS_EPS = 1e-6
ADAM_LR = 0.001
ADAM_B1 = 0.9
ADAM_B2 = 0.999
ADAM_EPS = 1e-08
ADAM_WD = 0.01
ADAM_STEP = 10
N_CHIPS = 4
NEG_BIG = -1e30

V7X_VMEM_BYTES = 64 * 1024 * 1024
VMEM_LIMIT = V7X_VMEM_BYTES - 6 * 1024 * 1024

TM = 512
TK_TOK = 1024
TB_CONV = 256
TR_ELT = 256
ROW_CHUNK = 128


def _params(sem):
    return pltpu.CompilerParams(dimension_semantics=sem, vmem_limit_bytes=VMEM_LIMIT)


def _row_tile(rows, target):
    best = None
    for t in range(16, min(rows, target) + 1, 16):
        if rows % t == 0:
            best = t
    assert best is not None, (rows, target)
    return best


def _dot(a, b):
    return jnp.dot(a, b, preferred_element_type=F32)


def _dot_nt(a, b):
    return lax.dot_general(a, b, (((1,), (1,)), ((), ())), preferred_element_type=F32)


def _dot_tn(a, b):
    return lax.dot_general(a, b, (((0,), (0,)), ((), ())), preferred_element_type=F32)


def _mesh_pos():
    x, y, c = lax.axis_index("x"), lax.axis_index("y"), lax.axis_index("c")
    chips = [(1 - x, y), (x, 1 - y), (1 - x, 1 - y)]
    return x, y, c, chips


def _chip_id(px, py):
    return 2 * px + py


def _rope(t, cos, sgn_sin, sign):
    w = t.shape[1]
    lane = lax.broadcasted_iota(jnp.int32, t.shape, 1) & (HEAD_DIM - 1)
    partner = jnp.where(lane < ROT_DIM // 2, pltpu.roll(t, w - ROT_DIM // 2, 1), pltpu.roll(t, ROT_DIM // 2, 1))
    return t * cos + sign * (partner * sgn_sin)


def _tile_lanes(t, n):
    return jnp.concatenate([t] * n, axis=1)


def _sigmoid(g):
    return 1.0 / (1.0 + jnp.exp(-g))


def _for_row_chunks(n_rows, fn):
    def step(r, carry):
        fn(pl.ds(pl.multiple_of(r * ROW_CHUNK, ROW_CHUNK), ROW_CHUNK))
        return carry

    lax.fori_loop(0, n_rows // ROW_CHUNK, step, 0)


def _accumulate(acc, val, k, nk):
    if nk == 1:
        acc[...] = val
        return

    @pl.when(k == 0)
    def _():
        acc[...] = jnp.zeros_like(acc)

    acc[...] += val


def _ln_fwd(pre):
    mu = jnp.mean(pre, axis=-1, keepdims=True)
    cen = pre - mu
    var = jnp.mean(cen * cen, axis=-1, keepdims=True)
    rstd = lax.rsqrt(var + LN_EPS)
    return cen * rstd, rstd


def _ln_bwd(dy, xhat, rstd, g):
    dxhat = dy * g
    m1 = jnp.mean(dxhat, axis=-1, keepdims=True)
    m2 = jnp.mean(dxhat * xhat, axis=-1, keepdims=True)
    return rstd * (dxhat - m1 - xhat * m2)


def _cast_weight(w, chip_vec, after, name):
    _, r, c = w.shape
    tr = _row_tile(r, TR_ELT)

    def body(chip_ref, w_ref, after_ref, o_ref):
        o_ref[...] = w_ref[...].astype(BF16)

    grid_spec = pltpu.PrefetchScalarGridSpec(
        num_scalar_prefetch=1, grid=(r // tr,),
        in_specs=[pl.BlockSpec((None, tr, c), lambda i, chip_ref: (0, i, 0)), _ANY],
        out_specs=pl.BlockSpec((None, tr, c), lambda i, chip_ref: (chip_ref[0], i, 0)))
    return pl.pallas_call(
        body, name=name, grid_spec=grid_spec,
        out_shape=jax.ShapeDtypeStruct((N_CHIPS, r, c), BF16),
        compiler_params=_params(("parallel",)),
    )(chip_vec, w, after)


_HBM = pl.BlockSpec(memory_space=pltpu.HBM)
_VMEM = pl.BlockSpec(memory_space=pltpu.VMEM)


_SEM = pl.BlockSpec(memory_space=pltpu.SEMAPHORE)
_ANY = pl.BlockSpec(memory_space=pl.ANY)
_EFFECT = pltpu.SideEffectType.DATAFLOW_SIDE_EFFECTING


def _chip_copy(buf, k, chip_of_src, half_rows, send_sems, recv_sems, to):
    part = buf.at[chip_of_src, half_rows]
    return pltpu.make_async_remote_copy(
        src_ref=part, dst_ref=part, send_sem=send_sems.at[k], recv_sem=recv_sems.at[k], device_id=to, device_id_type=MESH)


def _half_rows(buf, which):
    hr = buf.shape[1] // 2
    return pl.ds(which * hr, hr)


def _gather_start(bufs, after, name):
    n = len(bufs)

    def body(*refs):
        ins = refs[:n]
        sends, recvs = refs[n + 1:2 * n + 1], refs[2 * n + 1:3 * n + 1]
        token = refs[4 * n + 1]
        x, y, c, chips = _mesh_pos()
        me = _chip_id(x, y)
        for w in range(n):
            for k, chip in enumerate(chips):
                _chip_copy(ins[w], k, me, _half_rows(ins[w], c), sends[w], recvs[w], (*chip, c)).start()
        token[...] = jnp.zeros_like(token)

    outs = pl.pallas_call(
        body, name=name,
        in_specs=[_HBM] * n + [_ANY],
        out_specs=[_SEM] * (2 * n) + [_HBM] * n + [_VMEM],
        out_shape=[pltpu.SemaphoreType.DMA((3,))] * (2 * n) + [pltpu.HBM(b.shape, b.dtype) for b in bufs]
        + [jax.ShapeDtypeStruct((8, 128), F32)],
        input_output_aliases={w: 2 * n + w for w in range(n)},
        compiler_params=pltpu.CompilerParams(has_side_effects=_EFFECT),
    )(*[pltpu.with_memory_space_constraint(b, pltpu.HBM) for b in bufs], after)
    return [(outs[w], outs[n + w], outs[2 * n + w]) for w in range(n)], outs[3 * n]


def _gather_wait(send_sems, recv_sems, buf, after, name):
    def body(buf_ref, send_ref, recv_ref, after_ref, out_ref):
        x, y, c, chips = _mesh_pos()
        me = _chip_id(x, y)
        for k, chip in enumerate(chips):
            _chip_copy(buf_ref, k, me, _half_rows(buf_ref, c), send_ref, recv_ref, (*chip, c)).wait_send()
        for k, chip in enumerate(chips):
            _chip_copy(buf_ref, k, _chip_id(*chip), _half_rows(buf_ref, c), send_ref, recv_ref, (*chip, c)).wait_recv()

    return pl.pallas_call(
        body, name=name,
        in_specs=[_HBM, _SEM, _SEM, _ANY], out_specs=_HBM,
        out_shape=pltpu.HBM(buf.shape, buf.dtype),
        input_output_aliases={0: 0},
        compiler_params=pltpu.CompilerParams(has_side_effects=_EFFECT),
    )(buf, send_sems, recv_sems, after)


def _sibling_fill(buf, name, own_too=False):
    n_copies = 4 if own_too else 3

    def body(buf_ref, out_ref, send_sems, recv_sems):
        x, y, c, chips = _mesh_pos()
        sibling = (x, y, 1 - c)
        slots = [_chip_id(*chip) for chip in chips] + ([_chip_id(x, y)] if own_too else [])
        copies = []
        for k, slot in enumerate(slots):
            cp = _chip_copy(out_ref, k, slot, _half_rows(out_ref, c), send_sems, recv_sems, sibling)
            cp.start()
            copies.append(cp)
        for k, slot in enumerate(slots):
            _chip_copy(out_ref, k, slot, _half_rows(out_ref, 1 - c), send_sems, recv_sems, sibling).wait_recv()
        for cp in copies:
            cp.wait_send()

    return pl.pallas_call(
        body, name=name,
        in_specs=[_HBM], out_specs=_HBM,
        out_shape=jax.ShapeDtypeStruct(buf.shape, buf.dtype),
        input_output_aliases={0: 0},
        scratch_shapes=[pltpu.SemaphoreType.DMA((n_copies,)), pltpu.SemaphoreType.DMA((n_copies,))],
    )(buf)


def _allgather_conv_w(cw):
    _, kw, cs = cw.shape

    def body(cw_ref, out_ref, send_sems, recv_sems):
        x, y, c, chips = _mesh_pos()
        me = _chip_id(x, y)
        out_ref[pl.ds(me, 1)] = cw_ref[...]
        copies = []
        for k, chip in enumerate(chips):
            cp = pltpu.make_async_remote_copy(
                src_ref=cw_ref.at[0], dst_ref=out_ref.at[me], send_sem=send_sems.at[k], recv_sem=recv_sems.at[k],
                device_id=(*chip, c), device_id_type=MESH)
            cp.start()
            copies.append(cp)
        for k, chip in enumerate(chips):
            pltpu.make_async_remote_copy(
                src_ref=cw_ref.at[0], dst_ref=out_ref.at[_chip_id(*chip)], send_sem=send_sems.at[k],
                recv_sem=recv_sems.at[k], device_id=(*chip, c), device_id_type=MESH).wait_recv()
        for cp in copies:
            cp.wait_send()

    return pl.pallas_call(
        body, name="allgather_conv_w",
        in_specs=[_VMEM], out_specs=_VMEM,
        out_shape=jax.ShapeDtypeStruct((N_CHIPS, kw, cs), F32),
        scratch_shapes=[pltpu.SemaphoreType.DMA((3,)), pltpu.SemaphoreType.DMA((3,))],
    )(cw)


def _exchange_halves(parts, after, name):
    n = len(parts)
    shapes = [p.shape for p in parts]

    def body(*refs):
        ins, outs = refs[:n], refs[n + 1:2 * n + 1]
        send_sems, recv_sems = refs[2 * n + 1:]
        x, y, c, _ = _mesh_pos()
        copies = []
        for w in range(n):
            hr = shapes[w][1] // 2
            cp = pltpu.make_async_remote_copy(
                src_ref=ins[w].at[:, pl.ds((1 - c) * hr, hr)], dst_ref=outs[w],
                send_sem=send_sems.at[w], recv_sem=recv_sems.at[w],
                device_id=(x, y, 1 - c), device_id_type=MESH)
            cp.start()
            copies.append(cp)
        for cp in copies:
            cp.wait()

    return pl.pallas_call(
        body, name=name,
        in_specs=[_HBM] * n + [_ANY], out_specs=[_HBM] * n,
        out_shape=[jax.ShapeDtypeStruct((s[0], s[1] // 2, s[2]), BF16) for s in shapes],
        scratch_shapes=[pltpu.SemaphoreType.DMA((n,)), pltpu.SemaphoreType.DMA((n,))],
    )(*parts, after)


def _add_halves(part, got, cvec, name):
    ns, r, cdim = part.shape
    hr = r // 2
    tr = _row_tile(hr, TR_ELT)
    nblk = hr // tr

    def body(c_ref, a_ref, b_ref, o_ref):
        o_ref[...] = (a_ref[...].astype(F32) + b_ref[...].astype(F32)).astype(BF16)

    grid_spec = pltpu.PrefetchScalarGridSpec(
        num_scalar_prefetch=1, grid=(ns, nblk),
        in_specs=[pl.BlockSpec((None, tr, cdim), lambda s, i, c_ref: (s, c_ref[0] * nblk + i, 0)),
                  pl.BlockSpec((None, tr, cdim), lambda s, i, c_ref: (s, i, 0))],
        out_specs=pl.BlockSpec((None, tr, cdim), lambda s, i, c_ref: (s, i, 0)))
    return pl.pallas_call(
        body, name=name, grid_spec=grid_spec,
        out_shape=jax.ShapeDtypeStruct((ns, hr, cdim), BF16),
        compiler_params=_params(("parallel", "parallel")),
    )(cvec, part, got)


def _scatter_copy(sums_ref, land_ref, k, src_slot, dst_slot, c, send_sems, recv_sems, to):
    return pltpu.make_async_remote_copy(
        src_ref=sums_ref.at[src_slot], dst_ref=land_ref.at[dst_slot, _half_rows(land_ref, c)],
        send_sem=send_sems.at[k], recv_sem=recv_sems.at[k], device_id=to, device_id_type=MESH)


def _scatter_start(sums, name):
    ns, hr, cdim = sums.shape
    land = lax.empty((ns, 2 * hr, cdim), sums.dtype)

    def body(sums_ref, land_ref, send_sems, recv_sems, sums_thru, land_thru):
        x, y, c, chips = _mesh_pos()
        me = _chip_id(x, y)
        for k, chip in enumerate(chips):
            _scatter_copy(sums_ref, land_ref, k, _chip_id(*chip), me, c, send_sems, recv_sems, (*chip, c)).start()

    return pl.pallas_call(
        body, name=name,
        in_specs=[_HBM, _HBM], out_specs=[_SEM, _SEM, _HBM, _HBM],
        out_shape=[pltpu.SemaphoreType.DMA((3,)), pltpu.SemaphoreType.DMA((3,)),
                   pltpu.HBM(sums.shape, sums.dtype), pltpu.HBM(land.shape, land.dtype)],
        input_output_aliases={0: 2, 1: 3},
        compiler_params=pltpu.CompilerParams(has_side_effects=_EFFECT),
    )(pltpu.with_memory_space_constraint(sums, pltpu.HBM), pltpu.with_memory_space_constraint(land, pltpu.HBM))


def _scatter_wait(send_sems, recv_sems, sums, land, after, name):
    def body(sums_ref, land_ref, send_ref, recv_ref, after_ref, sums_out, land_out):
        x, y, c, chips = _mesh_pos()
        me = _chip_id(x, y)
        for k, chip in enumerate(chips):
            _scatter_copy(sums_ref, land_ref, k, _chip_id(*chip), me, c, send_ref, recv_ref, (*chip, c)).wait_send()
        for k, chip in enumerate(chips):
            _scatter_copy(sums_ref, land_ref, k, me, _chip_id(*chip), c, send_ref, recv_ref, (*chip, c)).wait_recv()

    return pl.pallas_call(
        body, name=name,
        in_specs=[_HBM, _HBM, _SEM, _SEM, _ANY], out_specs=[_HBM, _HBM],
        out_shape=[pltpu.HBM(sums.shape, sums.dtype), pltpu.HBM(land.shape, land.dtype)],
        input_output_aliases={0: 0, 1: 1},
        compiler_params=pltpu.CompilerParams(has_side_effects=_EFFECT),
    )(sums, land, send_sems, recv_sems, after)


def _complete_chip_sums(sums, lands):
    n = len(sums)

    def body(*refs):
        sums_refs, outs = refs[:n], refs[2 * n:3 * n]
        send_sems, recv_sems = refs[3 * n:]
        x, y, c, chips = _mesh_pos()
        me = _chip_id(x, y)
        sibling = (x, y, 1 - c)
        slots = [_chip_id(*chip) for chip in chips]
        sent = []
        for w in range(n):
            out = outs[w]
            cp = _scatter_copy(sums_refs[w], out, 3, me, me, c, send_sems.at[w], recv_sems.at[w], sibling)
            cp.start()
            sent.append(cp)
            for k, slot in enumerate(slots):
                cp = _chip_copy(out, k, slot, _half_rows(out, c), send_sems.at[w], recv_sems.at[w], sibling)
                cp.start()
                sent.append(cp)
        for w in range(n):
            out = outs[w]
            _scatter_copy(sums_refs[w], out, 3, me, me, 1 - c, send_sems.at[w], recv_sems.at[w], sibling).wait_recv()
            for k, slot in enumerate(slots):
                _chip_copy(out, k, slot, _half_rows(out, 1 - c), send_sems.at[w], recv_sems.at[w], sibling).wait_recv()
        for cp in sent:
            cp.wait_send()

    return pl.pallas_call(
        body, name="complete_chip_sums",
        in_specs=[_HBM] * (2 * n), out_specs=[_HBM] * n,
        out_shape=[jax.ShapeDtypeStruct(b.shape, b.dtype) for b in lands],
        input_output_aliases={n + w: w for w in range(n)},
        scratch_shapes=[pltpu.SemaphoreType.DMA((n, 4)), pltpu.SemaphoreType.DMA((n, 4))],
    )(*sums, *lands)


SMALL_ROWS = 8


def _allreduce_small(gl2g, gl2b, gl1g, gl1b, g_ac, gcw, gsink, loss, after):
    d = gl2g.shape[1]
    hd = d // 2
    nq = gsink.shape[1]

    def body(a_ref, b_ref, c_ref, d_ref, e_ref, cw_ref, sk_ref, ls_ref, after_ref, out_ref, mine, gath, send_sems,
             recv_sems):
        x, y, c, _ = _mesh_pos()
        me = 4 * x + 2 * y + c
        mine[...] = jnp.zeros_like(mine)
        mine[0:1, :] = a_ref[...]
        mine[1:2, :] = b_ref[...]
        mine[2:3, :] = c_ref[...]
        mine[3:4, :] = d_ref[...]
        mine[4:5, :] = e_ref[...]
        mine[5:6, 0:hd] = cw_ref[0:1, :]
        mine[5:6, hd:d] = cw_ref[1:2, :]
        mine[6:7, 0:hd] = cw_ref[2:3, :]
        mine[6:7, hd:hd + nq] = sk_ref[...]
        mine[6:7, hd + 128:hd + 256] = ls_ref[...]
        gath[pl.ds(me, 1)] = mine[...][None]
        copies = []
        for r in range(1, 8):
            peer = ((1 - x) if r & 4 else x, (1 - y) if r & 2 else y, (1 - c) if r & 1 else c)
            cp = pltpu.make_async_remote_copy(
                src_ref=mine, dst_ref=gath.at[me], send_sem=send_sems.at[r - 1], recv_sem=recv_sems.at[r - 1],
                device_id=peer, device_id_type=MESH)
            cp.start()
            copies.append(cp)
        for r in range(1, 8):
            peer = ((1 - x) if r & 4 else x, (1 - y) if r & 2 else y, (1 - c) if r & 1 else c)
            peer_id = 4 * peer[0] + 2 * peer[1] + peer[2]
            pltpu.make_async_remote_copy(
                src_ref=mine, dst_ref=gath.at[peer_id], send_sem=send_sems.at[r - 1], recv_sem=recv_sems.at[r - 1],
                device_id=peer, device_id_type=MESH).wait_recv()
        for cp in copies:
            cp.wait_send()
        total = gath[0]
        for dev in range(1, 8):
            total = total + gath[dev]
        out_ref[...] = total

    return pl.pallas_call(
        body, name="allreduce_small",
        in_specs=[_VMEM] * 8 + [_ANY], out_specs=_VMEM,
        out_shape=jax.ShapeDtypeStruct((SMALL_ROWS, d), F32),
        scratch_shapes=[pltpu.VMEM((SMALL_ROWS, d), F32), pltpu.VMEM((8, SMALL_ROWS, d), F32),
                        pltpu.SemaphoreType.DMA((7,)), pltpu.SemaphoreType.DMA((7,))],
    )(gl2g, gl2b, gl1g, gl1b, g_ac, gcw, gsink, loss, after)


def _adamw(w, g, m, v):
    m = ADAM_B1 * m + (1.0 - ADAM_B1) * g
    v = ADAM_B2 * v + (1.0 - ADAM_B2) * (g * g)
    m_hat = m / (1.0 - ADAM_B1 ** ADAM_STEP)
    v_hat = v / (1.0 - ADAM_B2 ** ADAM_STEP)
    delta = -ADAM_LR * (m_hat / (jnp.sqrt(v_hat) + ADAM_EPS) + ADAM_WD * w)
    return delta, m, v


def _adamw_shard(w, m, v, land, own, pos_vec, name):
    _, r, c = w.shape
    hr = r // 2
    tr = _row_tile(hr, TR_ELT)
    nh = hr // tr

    def body(pos_ref, w_ref, m_ref, v_ref, l0, l1, l2, l3, own_ref, g_out, d_out, m_out, v_out):
        i = pl.program_id(0)
        mine = (i // nh) == pos_ref[1]
        own_blk = own_ref[...].astype(F32)
        g = None
        for s, l_ref in enumerate([l0, l1, l2, l3]):
            term = jnp.where(mine & (pos_ref[0] == s), own_blk, l_ref[...].astype(F32))
            g = term if g is None else g + term
        delta, nm, nv = _adamw(w_ref[...], g, m_ref[...], v_ref[...])
        g_out[...] = g
        d_out[...] = delta
        m_out[...] = nm
        v_out[...] = nv

    def land_spec(s):
        def index(i, pos_ref):
            skip = (pos_ref[0] == s) & ((i // nh) == pos_ref[1])
            return (s, jnp.where(skip, (i + nh) % (2 * nh), i), 0)
        return pl.BlockSpec((None, tr, c), index)

    blk = pl.BlockSpec((None, tr, c), lambda i, pos_ref: (0, i, 0))
    grid_spec = pltpu.PrefetchScalarGridSpec(
        num_scalar_prefetch=1, grid=(2 * nh,),
        in_specs=[blk, blk, blk] + [land_spec(s) for s in range(N_CHIPS)]
        + [pl.BlockSpec((None, tr, c), lambda i, pos_ref: (pos_ref[0], i % nh, 0))],
        out_specs=[blk] * 4)
    return pl.pallas_call(
        body, name=name, grid_spec=grid_spec,
        out_shape=[jax.ShapeDtypeStruct((1, r, c), F32)] * 4,
        compiler_params=_params(("parallel",)),
    )(pos_vec, w, m, v, land, land, land, land, own)


def _adamw_small(red, params):
    names = ["sinks", "g_attn", "g_conv", "ln1_g", "ln1_b", "ln2_g", "ln2_b", "conv_w"]
    d = red.shape[1]
    hd = d // 2
    flat = []
    for nme in names:
        flat.extend(params[nme])
    nq = params["sinks"][0].shape[1]
    cs = params["conv_w"][0].shape[2]

    def body(*refs):
        red_ref = refs[0]
        ins = refs[1:1 + 3 * len(names)]
        outs = refs[1 + 3 * len(names):]
        x, y, _, _ = _mesh_pos()
        me = _chip_id(x, y)

        def conv_tap(row, base):
            picked = red_ref[row:row + 1, base:base + cs]
            for s in range(1, N_CHIPS):
                picked = jnp.where(me == s, red_ref[row:row + 1, base + s * cs:base + (s + 1) * cs], picked)
            return picked

        grads = {
            "sinks": red_ref[6:7, hd:hd + nq],
            "g_attn": red_ref[4:5, 0:hd],
            "g_conv": red_ref[4:5, hd:d],
            "ln1_g": red_ref[2:3, :],
            "ln1_b": red_ref[3:4, :],
            "ln2_g": red_ref[0:1, :],
            "ln2_b": red_ref[1:2, :],
        }
        for i, nme in enumerate(names):
            w_ref, m_ref, v_ref = ins[3 * i:3 * i + 3]
            g_out, d_out, m_out, v_out = outs[4 * i:4 * i + 4]
            if nme == "conv_w":
                for tap, (row, base) in enumerate([(5, 0), (5, hd), (6, 0)]):
                    g = conv_tap(row, base)
                    delta, nm, nv = _adamw(w_ref[0, tap:tap + 1, :], g, m_ref[0, tap:tap + 1, :], v_ref[0, tap:tap + 1, :])
                    g_out[0, tap:tap + 1, :] = g
                    d_out[0, tap:tap + 1, :] = delta
                    m_out[0, tap:tap + 1, :] = nm
                    v_out[0, tap:tap + 1, :] = nv
            else:
                g = grads[nme]
                delta, nm, nv = _adamw(w_ref[...], g, m_ref[...], v_ref[...])
                g_out[...] = g
                d_out[...] = delta
                m_out[...] = nm
                v_out[...] = nv

    out_shape = []
    for nme in names:
        out_shape.extend([jax.ShapeDtypeStruct(params[nme][0].shape, F32)] * 4)
    outs = pl.pallas_call(
        body, name="adamw_small",
        in_specs=[_VMEM] * (1 + len(flat)), out_specs=[_VMEM] * len(out_shape),
        out_shape=out_shape,
    )(red, *flat)
    return {nme: tuple(outs[4 * i:4 * i + 4]) for i, nme in enumerate(names)}


def _rope_tables(pos_col):
    s = pos_col.shape[0]
    w = N_KV_HEADS * HEAD_DIM
    tb = min(512, s)
    inv_freq = (ROPE_THETA ** (-np.arange(0, ROT_DIM, 2, dtype=np.float32) / ROT_DIM)).astype(np.float32)

    def body(pos_ref, cos_ref, sin_ref):
        pos = pos_ref[...].astype(F32)
        lane = lax.broadcasted_iota(jnp.int32, (tb, w), 1) & (HEAD_DIM - 1)
        fidx = lane & (ROT_DIM // 2 - 1)
        inv = jnp.zeros((tb, w), F32)
        for k in range(ROT_DIM // 2):
            inv = jnp.where(fidx == k, float(inv_freq[k]), inv)
        ang = pos * inv
        rot = lane < ROT_DIM
        cos_ref[...] = jnp.where(rot, jnp.cos(ang), 1.0)
        sin_v = jnp.sin(ang)
        sin_ref[...] = jnp.where(lane < ROT_DIM // 2, -sin_v, jnp.where(rot, sin_v, 0.0))

    return pl.pallas_call(
        body, name="rope_tables", grid=(s // tb,),
        in_specs=[pl.BlockSpec((tb, 1), lambda i: (i, 0))],
        out_specs=[pl.BlockSpec((tb, w), lambda i: (i, 0))] * 2,
        out_shape=[jax.ShapeDtypeStruct((s, w), F32)] * 2,
        compiler_params=_params(("parallel",)),
    )(pos_col)


def _in_proj(x, w_in_g):
    _, s, d = x.shape
    ns, _, ncol = w_in_g.shape
    tm = min(2 * TM, s)

    def body(x_ref, w_ref, o_ref):
        o_ref[...] = _dot(x_ref[...].astype(BF16), w_ref[...])

    return pl.pallas_call(
        body, name="in_proj", grid=(s // tm, ns),
        in_specs=[pl.BlockSpec((None, tm, d), lambda i, j: (0, i, 0)),
                  pl.BlockSpec((None, d, ncol), lambda i, j: (j, 0, 0))],
        out_specs=pl.BlockSpec((tm, ncol), lambda i, j: (i, j)),
        out_shape=jax.ShapeDtypeStruct((s, ns * ncol), F32),
        compiler_params=_params(("parallel", "arbitrary")),
    )(x, w_in_g)


def _attention_scores(q, k_prev, k_cur, sinks_ref, h, first, valid):
    heads = [q[:, (GROUP * h + g) * HEAD_DIM:(GROUP * h + g + 1) * HEAD_DIM] for g in range(GROUP)]
    q4 = jnp.concatenate(heads, axis=0).astype(BF16)
    kk = jnp.concatenate([k_prev[:, h * HEAD_DIM:(h + 1) * HEAD_DIM], k_cur[:, h * HEAD_DIM:(h + 1) * HEAD_DIM]],
                         axis=0).astype(BF16)
    s = _dot_nt(q4, kk) * ATTN_SCALE
    s = jnp.where(valid, s, NEG_BIG)
    sink = jnp.concatenate(
        [jnp.broadcast_to(sinks_ref[0:1, GROUP * h + g:GROUP * h + g + 1], (WINDOW, 1)) for g in range(GROUP)], axis=0)
    m = jnp.maximum(jnp.max(s, axis=1, keepdims=True), sink)
    p = jnp.exp(s - m)
    p_sink = jnp.exp(sink - m)
    inv_l = 1.0 / (jnp.sum(p, axis=1, keepdims=True) + p_sink)
    return q4, kk, p * inv_l, p_sink * inv_l


def _band_mask(first):
    rows = GROUP * WINDOW
    qi = lax.broadcasted_iota(jnp.int32, (rows, 2 * WINDOW), 0) & (WINDOW - 1)
    kj = lax.broadcasted_iota(jnp.int32, (rows, 2 * WINDOW), 1)
    rel = qi + WINDOW - kj
    band = (rel >= 0) & (rel < WINDOW)
    return band & jnp.logical_not(first & (kj < WINDOW))


def _attention_fwd(proj, cos_t, sin_t, sinks):
    s = proj.shape[0]
    qw = GROUP * N_KV_HEADS * HEAD_DIM
    kvw = N_KV_HEADS * HEAD_DIM
    nb = s // WINDOW

    def body(cur_ref, prev_ref, cos_ref, sin_ref, cosp_ref, sinp_ref, sinks_ref, o_ref):
        n = pl.program_id(0)
        first = n == 0
        cur = cur_ref[...]
        cos, sin = cos_ref[...], sin_ref[...]
        q = _rope(cur[:, :qw], _tile_lanes(cos, GROUP), _tile_lanes(sin, GROUP), 1.0)
        k_cur = _rope(cur[:, qw:qw + kvw], cos, sin, 1.0)
        v_cur = cur[:, qw + kvw:]
        prev = prev_ref[...]
        k_prev = _rope(prev[:, :kvw], cosp_ref[...], sinp_ref[...], 1.0)
        v_prev = prev[:, kvw:]
        valid = _band_mask(first)
        outs = []
        for h in range(N_KV_HEADS):
            _, _, probs, _ = _attention_scores(q, k_prev, k_cur, sinks_ref, h, first, valid)
            vv = jnp.concatenate([v_prev[:, h * HEAD_DIM:(h + 1) * HEAD_DIM], v_cur[:, h * HEAD_DIM:(h + 1) * HEAD_DIM]],
                                 axis=0).astype(BF16)
            o = _dot(probs.astype(BF16), vv)
            outs.extend([o[g * WINDOW:(g + 1) * WINDOW] for g in range(GROUP)])
        o_ref[...] = jnp.concatenate(outs, axis=1)

    tbl = pl.BlockSpec((WINDOW, kvw), lambda n: (n, 0))
    tbl_prev = pl.BlockSpec((WINDOW, kvw), lambda n: (jnp.maximum(n - 1, 0), 0))
    return pl.pallas_call(
        body, name="attention_fwd", grid=(nb,),
        in_specs=[pl.BlockSpec((WINDOW, qw + 2 * kvw), lambda n: (n, 0)),
                  pl.BlockSpec((WINDOW, 2 * kvw), lambda n: (jnp.maximum(n - 1, 0), (qw // (2 * kvw)))),
                  tbl, tbl, tbl_prev, tbl_prev, _VMEM],
        out_specs=pl.BlockSpec((WINDOW, qw), lambda n: (n, 0)),
        out_shape=jax.ShapeDtypeStruct((s, qw), F32),
        compiler_params=_params(("parallel",)),
    )(proj, proj, cos_t, sin_t, cos_t, sin_t, sinks)


def _conv_taps(cw_ref):
    return [jnp.concatenate([cw_ref[s, k:k + 1, :] for s in range(N_CHIPS)], axis=1) for k in range(3)]


def _shift_down(z, halo, steps):
    rows = z.shape[0]
    row = lax.broadcasted_iota(jnp.int32, z.shape, 0)
    out = pltpu.roll(z, steps, 0)
    for r in range(steps):
        out = jnp.where(row == r, halo[8 - steps + r:8 - steps + r + 1, :], out)
    return out


def _shift_up(z, halo, steps):
    rows = z.shape[0]
    row = lax.broadcasted_iota(jnp.int32, z.shape, 0)
    out = pltpu.roll(z, rows - steps, 0)
    for r in range(steps):
        out = jnp.where(row == rows - steps + r, halo[r:r + 1, :], out)
    return out


def _split_cbu(lo, hi, cw):
    c_gate = lo[:, :cw]
    b_gate = jnp.concatenate([lo[:, cw:], hi[:, :2 * cw - lo.shape[1]]], axis=1)
    u = hi[:, 2 * cw - lo.shape[1]:]
    return c_gate, b_gate, u


def _conv_norm(proj, attn, cw_full, g_ac):
    s, in_w = proj.shape
    cw = attn.shape[1]
    blk_w = in_w // 3
    tb = min(TB_CONV, s)

    def body(lo_ref, hi_ref, lo_h_ref, hi_h_ref, attn_ref, cw_ref, g_ref, mixed_ref, ac_ref, rstd_ref):
        i = pl.program_id(0)
        c_gate, b_gate, u = _split_cbu(lo_ref[...], hi_ref[...], cw)
        c_h, _, u_h = _split_cbu(lo_h_ref[...], hi_h_ref[...], cw)
        z = c_gate * u
        z_h = jnp.where(i == 0, 0.0, c_h * u_h)
        w0, w1, w2 = _conv_taps(cw_ref)
        y = w0 * _shift_down(z, z_h, 2) + w1 * _shift_down(z, z_h, 1) + w2 * z
        conv = b_gate * y
        a = attn_ref[...]
        r_a = lax.rsqrt(jnp.mean(a * a, axis=-1, keepdims=True) + RMS_EPS)
        r_c = lax.rsqrt(jnp.mean(conv * conv, axis=-1, keepdims=True) + RMS_EPS)
        g = g_ref[...]
        mixed_ref[...] = jnp.concatenate([a * r_a * g[:, :cw], conv * r_c * g[:, cw:]], axis=1).astype(BF16)
        ac_ref[...] = jnp.concatenate([a, conv], axis=1)
        rstd_ref[0] = r_a
        rstd_ref[1] = r_c

    halo_idx = lambda i: jnp.maximum(i * (tb // 8) - 1, 0)
    return pl.pallas_call(
        body, name="conv_norm", grid=(s // tb,),
        in_specs=[pl.BlockSpec((tb, blk_w), lambda i: (i, 1)),
                  pl.BlockSpec((tb, blk_w), lambda i: (i, 2)),
                  pl.BlockSpec((8, blk_w), lambda i: (halo_idx(i), 1)),
                  pl.BlockSpec((8, blk_w), lambda i: (halo_idx(i), 2)),
                  pl.BlockSpec((tb, cw), lambda i: (i, 0)),
                  _VMEM, _VMEM],
        out_specs=[pl.BlockSpec((tb, 2 * cw), lambda i: (i, 0)),
                   pl.BlockSpec((tb, 2 * cw), lambda i: (i, 0)),
                   pl.BlockSpec((2, tb, 1), lambda i: (0, i, 0))],
        out_shape=[jax.ShapeDtypeStruct((s, 2 * cw), BF16), jax.ShapeDtypeStruct((s, 2 * cw), F32),
                   jax.ShapeDtypeStruct((2, s, 1), F32)],
        compiler_params=_params(("parallel",)),
    )(proj, proj, proj, proj, attn, cw_full, g_ac)


def _out_proj_ln(mixed, w_out_g, x, ln_g, ln_b):
    s, d = mixed.shape
    tm = min(TM, s)
    tk = d
    nk = d // tk

    def body(a_ref, w_ref, x_ref, g_ref, b_ref, xhat_ref, h_ref, rstd_ref, acc):
        k = pl.program_id(1)
        _accumulate(acc, _dot(a_ref[...], w_ref[...]), k, nk)

        @pl.when(k == nk - 1)
        def _():
            def rows_fn(rows):
                xhat, rstd = _ln_fwd(ALPHA * x_ref[rows, :] + acc[rows, :])
                xhat_ref[rows, :] = xhat
                h_ref[rows, :] = (xhat * g_ref[...] + b_ref[...]).astype(BF16)
                rstd_ref[rows, :] = rstd

            _for_row_chunks(tm, rows_fn)

    row = pl.BlockSpec((tm, d), lambda i, k: (i, 0))
    return pl.pallas_call(
        body, name="out_proj_ln", grid=(s // tm, nk),
        in_specs=[pl.BlockSpec((tm, tk), lambda i, k: (i, k)),
                  pl.BlockSpec((tk, d), lambda i, k: (k, 0)),
                  pl.BlockSpec((None, tm, d), lambda i, k: (0, i, 0)),
                  _VMEM, _VMEM],
        out_specs=[row, row, pl.BlockSpec((tm, 1), lambda i, k: (i, 0))],
        out_shape=[jax.ShapeDtypeStruct((s, d), F32), jax.ShapeDtypeStruct((s, d), BF16),
                   jax.ShapeDtypeStruct((s, 1), F32)],
        scratch_shapes=[pltpu.VMEM((tm, d), F32)],
        compiler_params=_params(("parallel", "arbitrary")),
    )(mixed, w_out_g, x, ln_g, ln_b)


def _gate_up(h1, w_gate_g, w_up_g):
    s, d = h1.shape
    ns, _, fs = w_gate_g.shape
    tm = min(TM, s)
    tk = d
    nk = d // tk

    def body(h_ref, wg_ref, wu_ref, act_ref, g_ref, u_ref, acc_g, acc_u):
        k = pl.program_id(2)
        h = h_ref[...]
        _accumulate(acc_g, _dot(h, wg_ref[...]), k, nk)
        _accumulate(acc_u, _dot(h, wu_ref[...]), k, nk)

        @pl.when(k == nk - 1)
        def _():
            def rows_fn(rows):
                g, u = acc_g[rows, :], acc_u[rows, :]
                act_ref[rows, :] = (g * _sigmoid(g) * u).astype(BF16)
                g_ref[rows, :] = g.astype(BF16)
                u_ref[rows, :] = u.astype(BF16)

            _for_row_chunks(tm, rows_fn)

    wspec = pl.BlockSpec((None, tk, fs), lambda i, j, k: (j, k, 0))
    ospec = pl.BlockSpec((tm, fs), lambda i, j, k: (i, j))
    return pl.pallas_call(
        body, name="gate_up", grid=(s // tm, ns, nk),
        in_specs=[pl.BlockSpec((tm, tk), lambda i, j, k: (i, k)), wspec, wspec],
        out_specs=[ospec] * 3,
        out_shape=[jax.ShapeDtypeStruct((s, ns * fs), BF16)] * 3,
        scratch_shapes=[pltpu.VMEM((tm, fs), F32)] * 2,
        compiler_params=_params(("parallel", "arbitrary", "arbitrary")),
    )(h1, w_gate_g, w_up_g)


def _down_ln_loss(act, w_down_g, xhat1, ln1_g, ln1_b, ln2_g, ln2_b, target):
    s, f = act.shape
    d = xhat1.shape[1]
    tm = min(TM, s)
    tk = f // N_CHIPS
    nk = f // tk

    def body(a_ref, w_ref, xh_ref, g1_ref, b1_ref, g2_ref, b2_ref, t_ref, dpre_ref, loss_ref, gg_ref, gb_ref, acc):
        i, k = pl.program_id(0), pl.program_id(1)
        _accumulate(acc, _dot(a_ref[...], w_ref[...]), k, nk)

        @pl.when(k == nk - 1)
        def _():
            @pl.when(i == 0)
            def _():
                loss_ref[...] = jnp.zeros_like(loss_ref)
                gg_ref[...] = jnp.zeros_like(gg_ref)
                gb_ref[...] = jnp.zeros_like(gb_ref)

            def rows_fn(rows):
                h1 = xh_ref[rows, :] * g1_ref[...] + b1_ref[...]
                xhat, rstd = _ln_fwd(ALPHA * h1 + acc[rows, :])
                g2 = g2_ref[...]
                diff = xhat * g2 + b2_ref[...] - t_ref[rows, :]
                dy = diff * (1.0 / d)
                dpre_ref[rows, :] = _ln_bwd(dy, xhat, rstd, g2)
                sq = jnp.sum(jnp.sum(diff * diff, axis=1, keepdims=True), axis=0, keepdims=True)
                loss_ref[...] += jnp.broadcast_to(sq * (0.5 / d), (1, 128))
                gg_ref[...] += jnp.sum(dy * xhat, axis=0, keepdims=True)
                gb_ref[...] += jnp.sum(dy, axis=0, keepdims=True)

            _for_row_chunks(tm, rows_fn)

    row = pl.BlockSpec((tm, d), lambda i, k: (i, 0))
    vec = pl.BlockSpec((1, d), lambda i, k: (0, 0))
    return pl.pallas_call(
        body, name="down_ln_loss", grid=(s // tm, nk),
        in_specs=[pl.BlockSpec((tm, tk), lambda i, k: (i, k)),
                  pl.BlockSpec((tk, d), lambda i, k: (k, 0)),
                  row, _VMEM, _VMEM, _VMEM, _VMEM,
                  pl.BlockSpec((None, tm, d), lambda i, k: (0, i, 0))],
        out_specs=[row, pl.BlockSpec((1, 128), lambda i, k: (0, 0)), vec, vec],
        out_shape=[jax.ShapeDtypeStruct((s, d), F32), jax.ShapeDtypeStruct((1, 128), F32),
                   jax.ShapeDtypeStruct((1, d), F32), jax.ShapeDtypeStruct((1, d), F32)],
        scratch_shapes=[pltpu.VMEM((tm, d), F32)],
        compiler_params=_params(("arbitrary", "arbitrary")),
    )(act, w_down_g, xhat1, ln1_g, ln1_b, ln2_g, ln2_b, target)


def _dact_silu_bwd(dpre2, w_down_g, gate, up):
    s, d = dpre2.shape
    f = gate.shape[1]
    fs = f // N_CHIPS
    tm = min(TM, s)

    def body(dp_ref, w_ref, g_ref, u_ref, dg_ref, du_ref):
        d_act = _dot_nt(dp_ref[...].astype(BF16), w_ref[...])
        g = g_ref[...].astype(F32)
        u = u_ref[...].astype(F32)
        sg = _sigmoid(g)
        dg_ref[...] = (d_act * u * (sg * (1.0 + g * (1.0 - sg)))).astype(BF16)
        du_ref[...] = (d_act * (g * sg)).astype(BF16)

    blk = pl.BlockSpec((tm, fs), lambda i, j: (i, j))
    return pl.pallas_call(
        body, name="dact_silu_bwd", grid=(s // tm, N_CHIPS),
        in_specs=[pl.BlockSpec((tm, d), lambda i, j: (i, 0)),
                  pl.BlockSpec((fs, d), lambda i, j: (j, 0)), blk, blk],
        out_specs=[blk, blk],
        out_shape=[jax.ShapeDtypeStruct((s, f), BF16)] * 2,
        compiler_params=_params(("parallel", "arbitrary")),
    )(dpre2, w_down_g, gate, up)


def _grad_rows(a, b, after, name, row_blocks=1):
    s, m = a.shape
    n = b.shape[1]
    ms = m // N_CHIPS
    tmw = ms // row_blocks
    tk = min(TK_TOK, s)
    nk = s // tk

    def body(a_ref, b_ref, after_ref, o_ref, acc):
        k = pl.program_id(2)
        _accumulate(acc, _dot_tn(a_ref[...].astype(BF16), b_ref[...].astype(BF16)), k, nk)

        @pl.when(k == nk - 1)
        def _():
            o_ref[...] = acc[...].astype(BF16)

    return pl.pallas_call(
        body, name=name, grid=(N_CHIPS, row_blocks, nk),
        in_specs=[pl.BlockSpec((tk, tmw), lambda j, r, k: (k, j * row_blocks + r)),
                  pl.BlockSpec((tk, n), lambda j, r, k: (k, 0)), _ANY],
        out_specs=pl.BlockSpec((None, tmw, n), lambda j, r, k: (j, r, 0)),
        out_shape=jax.ShapeDtypeStruct((N_CHIPS, ms, n), BF16),
        scratch_shapes=[pltpu.VMEM((tmw, n), F32)],
        compiler_params=_params(("parallel", "parallel", "arbitrary")),
    )(a, b, after)


def _grad_cols(a, bs, after, name, a_3d=False, row_blocks=2):
    s, m = a.shape[-2:]
    n = bs[0].shape[1]
    ns = n // N_CHIPS
    nb = len(bs)
    tmw = m // row_blocks
    tk = min(TK_TOK, s)
    nk = s // tk

    def body(*refs):
        a_ref, b_refs, o_refs, accs = refs[0], refs[1:1 + nb], refs[2 + nb:2 + 2 * nb], refs[2 + 2 * nb:]
        k = pl.program_id(2)
        at = a_ref[...].astype(BF16)
        for b_ref, acc in zip(b_refs, accs):
            _accumulate(acc, _dot_tn(at, b_ref[...].astype(BF16)), k, nk)

        @pl.when(k == nk - 1)
        def _():
            for o_ref, acc in zip(o_refs, accs):
                o_ref[...] = acc[...].astype(BF16)

    if a_3d:
        a_spec = pl.BlockSpec((None, tk, tmw), lambda j, r, k: (0, k, r))
    else:
        a_spec = pl.BlockSpec((tk, tmw), lambda j, r, k: (k, r))
    return pl.pallas_call(
        body, name=name, grid=(N_CHIPS, row_blocks, nk),
        in_specs=[a_spec] + [pl.BlockSpec((tk, ns), lambda j, r, k: (k, j))] * nb + [_ANY],
        out_specs=[pl.BlockSpec((None, tmw, ns), lambda j, r, k: (j, r, 0))] * nb,
        out_shape=[jax.ShapeDtypeStruct((N_CHIPS, m, ns), BF16)] * nb,
        scratch_shapes=[pltpu.VMEM((tmw, ns), F32)] * nb,
        compiler_params=_params(("parallel", "parallel", "arbitrary")),
    )(a, *bs, after)


def _dh1_ln_bwd(d_gate, d_up, w_gate_g, w_up_g, dpre2, xhat1, rstd1, ln1_g, after):
    s, f = d_gate.shape
    d = dpre2.shape[1]
    hd = d // 2
    fs = f // N_CHIPS
    tm = min(TM, s)

    def body(dg_ref, du_ref, wg_ref, wu_ref, dp2_ref, xh_ref, rs_ref, g_ref, after_ref, dpre_ref, gg_ref, gb_ref,
             acc_lo, acc_hi):
        i, j, half = pl.program_id(0), pl.program_id(1), pl.program_id(2)
        val = _dot_nt(dg_ref[...], wg_ref[...]) + _dot_nt(du_ref[...], wu_ref[...])

        @pl.when(half == 0)
        def _():
            _accumulate(acc_lo, val, j, N_CHIPS)

        @pl.when(half == 1)
        def _():
            _accumulate(acc_hi, val, j, N_CHIPS)

        @pl.when((j == N_CHIPS - 1) & (half == 1))
        def _():
            @pl.when(i == 0)
            def _():
                gg_ref[...] = jnp.zeros_like(gg_ref)
                gb_ref[...] = jnp.zeros_like(gb_ref)

            def rows_fn(rows):
                dh = jnp.concatenate([acc_lo[rows, :], acc_hi[rows, :]], axis=1) + ALPHA * dp2_ref[rows, :]
                xhat = xh_ref[rows, :]
                dpre_ref[rows, :] = _ln_bwd(dh, xhat, rs_ref[rows, :], g_ref[...])
                gg_ref[...] += jnp.sum(dh * xhat, axis=0, keepdims=True)
                gb_ref[...] += jnp.sum(dh, axis=0, keepdims=True)

            _for_row_chunks(tm, rows_fn)

    row = pl.BlockSpec((tm, d), lambda i, j, h: (i, 0))
    vec = pl.BlockSpec((1, d), lambda i, j, h: (0, 0))
    act_blk = pl.BlockSpec((tm, fs), lambda i, j, h: (i, j))
    w_blk = pl.BlockSpec((None, hd, fs), lambda i, j, h: (j, h, 0))
    return pl.pallas_call(
        body, name="dh1_ln_bwd", grid=(s // tm, N_CHIPS, 2),
        in_specs=[act_blk, act_blk, w_blk, w_blk, row, row, pl.BlockSpec((tm, 1), lambda i, j, h: (i, 0)), _VMEM,
                  _ANY],
        out_specs=[row, vec, vec],
        out_shape=[jax.ShapeDtypeStruct((s, d), F32), jax.ShapeDtypeStruct((1, d), F32),
                   jax.ShapeDtypeStruct((1, d), F32)],
        scratch_shapes=[pltpu.VMEM((tm, hd), F32)] * 2,
        compiler_params=_params(("arbitrary", "arbitrary", "arbitrary")),
    )(d_gate, d_up, w_gate_g, w_up_g, dpre2, xhat1, rstd1, ln1_g, after)


def _dmixed_rms_bwd(dpre1, w_out_g, ac, rstd, g_ac):
    s, d = dpre1.shape
    hd = d // 2
    tm = min(TM, s)

    def body(dp_ref, w_ref, ac_ref, rs_ref, g_ref, dac_ref, gg_ref):
        i = pl.program_id(1)
        dm = _dot_nt(dp_ref[...].astype(BF16), w_ref[...])
        pre = ac_ref[...]
        r = rs_ref[...]
        gdm = dm * g_ref[...]
        dac_ref[...] = r * gdm - pre * (r * r * r) * jnp.mean(gdm * pre, axis=-1, keepdims=True)
        gg = jnp.sum(dm * pre * r, axis=0, keepdims=True)

        @pl.when(i == 0)
        def _():
            gg_ref[...] = gg

        @pl.when(i > 0)
        def _():
            gg_ref[...] += gg

    return pl.pallas_call(
        body, name="dmixed_rms_bwd", grid=(2, s // tm),
        in_specs=[pl.BlockSpec((tm, d), lambda h, i: (i, 0)),
                  pl.BlockSpec((hd, d), lambda h, i: (h, 0)),
                  pl.BlockSpec((tm, hd), lambda h, i: (i, h)),
                  pl.BlockSpec((None, tm, 1), lambda h, i: (h, i, 0)),
                  pl.BlockSpec((1, hd), lambda h, i: (0, h))],
        out_specs=[pl.BlockSpec((tm, hd), lambda h, i: (i, h)),
                   pl.BlockSpec((1, hd), lambda h, i: (0, h))],
        out_shape=[jax.ShapeDtypeStruct((s, d), F32), jax.ShapeDtypeStruct((1, d), F32)],
        compiler_params=_params(("arbitrary", "arbitrary")),
    )(dpre1, w_out_g, ac, rstd, g_ac)


def _attention_bwd(proj, d_ac, cos_t, sin_t, sinks, after):
    s = proj.shape[0]
    qw = GROUP * N_KV_HEADS * HEAD_DIM
    kvw = N_KV_HEADS * HEAD_DIM
    nb = s // WINDOW
    nq = GROUP * N_KV_HEADS

    def body(cur_ref, prev_ref, do_ref, cos_ref, sin_ref, cosp_ref, sinp_ref, sinks_ref, after_ref,
             dq_ref, dcur_ref, dprev_ref, dsink_ref):
        n = pl.program_id(0)
        first = n == 0
        cur = cur_ref[...]
        cos, sin = cos_ref[...], sin_ref[...]
        cos_q, sin_q = _tile_lanes(cos, GROUP), _tile_lanes(sin, GROUP)
        q = _rope(cur[:, :qw], cos_q, sin_q, 1.0)
        k_cur = _rope(cur[:, qw:qw + kvw], cos, sin, 1.0)
        v_cur = cur[:, qw + kvw:]
        prev = prev_ref[...]
        k_prev = _rope(prev[:, :kvw], cosp_ref[...], sinp_ref[...], 1.0)
        v_prev = prev[:, kvw:]
        d_out = do_ref[...]
        valid = _band_mask(first)
        dq_parts, dk_parts, dv_parts, dsink_parts = [], [], [], []
        for h in range(N_KV_HEADS):
            q4, kk, probs, p_sink = _attention_scores(q, k_prev, k_cur, sinks_ref, h, first, valid)
            vv = jnp.concatenate([v_prev[:, h * HEAD_DIM:(h + 1) * HEAD_DIM], v_cur[:, h * HEAD_DIM:(h + 1) * HEAD_DIM]],
                                 axis=0).astype(BF16)
            do4 = jnp.concatenate(
                [d_out[:, (GROUP * h + g) * HEAD_DIM:(GROUP * h + g + 1) * HEAD_DIM] for g in range(GROUP)],
                axis=0).astype(BF16)
            d_probs = _dot_nt(do4, vv)
            delta = jnp.sum(probs * d_probs, axis=1, keepdims=True)
            d_s = (probs * (d_probs - delta) * ATTN_SCALE).astype(BF16)
            dq4 = _dot(d_s, kk)
            dq_parts.extend([dq4[g * WINDOW:(g + 1) * WINDOW] for g in range(GROUP)])
            dk_parts.append(_dot_tn(d_s, q4))
            dv_parts.append(_dot_tn(probs.astype(BF16), do4))
            ds_sink = -p_sink * delta
            dsink_parts.extend([jnp.sum(ds_sink[g * WINDOW:(g + 1) * WINDOW], axis=0, keepdims=True)
                                for g in range(GROUP)])
        dq_ref[...] = _rope(jnp.concatenate(dq_parts, axis=1), cos_q, sin_q, -1.0)
        dk = jnp.concatenate(dk_parts, axis=1)
        dv = jnp.concatenate(dv_parts, axis=1)
        dprev_ref[...] = jnp.concatenate([dk[:WINDOW], dv[:WINDOW]], axis=1)
        dcur_ref[...] = jnp.concatenate([dk[WINDOW:], dv[WINDOW:]], axis=1)
        dsink = jnp.concatenate(dsink_parts, axis=1)

        @pl.when(first)
        def _():
            dsink_ref[...] = dsink

        @pl.when(n > 0)
        def _():
            dsink_ref[...] += dsink

    tbl = pl.BlockSpec((WINDOW, kvw), lambda n: (n, 0))
    tbl_prev = pl.BlockSpec((WINDOW, kvw), lambda n: (jnp.maximum(n - 1, 0), 0))
    kv_blk = pl.BlockSpec((WINDOW, 2 * kvw), lambda n: (n, 0))
    return pl.pallas_call(
        body, name="attention_bwd", grid=(nb,),
        in_specs=[pl.BlockSpec((WINDOW, qw + 2 * kvw), lambda n: (n, 0)),
                  pl.BlockSpec((WINDOW, 2 * kvw), lambda n: (jnp.maximum(n - 1, 0), (qw // (2 * kvw)))),
                  pl.BlockSpec((WINDOW, qw), lambda n: (n, 0)),
                  tbl, tbl, tbl_prev, tbl_prev, _VMEM, _ANY],
        out_specs=[pl.BlockSpec((WINDOW, qw), lambda n: (n, 0)), kv_blk, kv_blk,
                   pl.BlockSpec((1, nq), lambda n: (0, 0))],
        out_shape=[jax.ShapeDtypeStruct((s, qw), F32), jax.ShapeDtypeStruct((s, 2 * kvw), F32),
                   jax.ShapeDtypeStruct((s, 2 * kvw), F32), jax.ShapeDtypeStruct((1, nq), F32)],
        compiler_params=_params(("arbitrary",)),
    )(proj, proj, d_ac, cos_t, sin_t, cos_t, sin_t, sinks, after)


def _dproj_assemble(proj, d_ac, dq, dkv_cur, dkv_prev, cos_t, sin_t, cw_full):
    s, in_w = proj.shape
    cw = dq.shape[1]
    kvw = N_KV_HEADS * HEAD_DIM
    blk_w = in_w // 3
    tb = WINDOW
    nb = s // tb

    def body(lo_ref, hi_ref, lo_p_ref, hi_p_ref, lo_n_ref, hi_n_ref, dconv_ref, dconv_n_ref,
             dq_ref, dcur_ref, dprev_n_ref, cos_ref, sin_ref, cw_ref, dproj_ref, gcw_ref):
        i = pl.program_id(0)
        last = i == nb - 1
        c_gate, b_gate, u = _split_cbu(lo_ref[...], hi_ref[...], cw)
        c_p, _, u_p = _split_cbu(lo_p_ref[...], hi_p_ref[...], cw)
        _, b_n, _ = _split_cbu(lo_n_ref[...], hi_n_ref[...], cw)
        z = c_gate * u
        z_p = jnp.where(i == 0, 0.0, c_p * u_p)
        z1 = _shift_down(z, z_p, 1)
        z2 = _shift_down(z, z_p, 2)
        w0, w1, w2 = _conv_taps(cw_ref)
        y = w0 * z2 + w1 * z1 + w2 * z
        d_conv = dconv_ref[...]
        d_b = d_conv * y
        d_y = d_conv * b_gate
        d_y_n = jnp.where(last, 0.0, dconv_n_ref[...] * b_n)
        d_z = w2 * d_y + w1 * _shift_up(d_y, d_y_n, 1) + w0 * _shift_up(d_y, d_y_n, 2)
        d_c = d_z * u
        d_u = d_z * c_gate
        gcw = jnp.concatenate([jnp.sum(d_y * z2, axis=0, keepdims=True), jnp.sum(d_y * z1, axis=0, keepdims=True),
                               jnp.sum(d_y * z, axis=0, keepdims=True)], axis=0)

        @pl.when(i == 0)
        def _():
            gcw_ref[...] = gcw

        @pl.when(i > 0)
        def _():
            gcw_ref[...] += gcw

        dkv = dcur_ref[...] + jnp.where(last, 0.0, dprev_n_ref[...])
        dk = _rope(dkv[:, :kvw], cos_ref[...], sin_ref[...], -1.0)
        dproj_ref[...] = jnp.concatenate([dq_ref[...], dk, dkv[:, kvw:], d_c, d_b, d_u], axis=1).astype(BF16)

    prev8 = lambda i: jnp.maximum(i * (tb // 8) - 1, 0)
    next8 = lambda i: jnp.minimum((i + 1) * (tb // 8), s // 8 - 1)
    nxt = lambda i: jnp.minimum(i + 1, nb - 1)
    return pl.pallas_call(
        body, name="dproj_assemble", grid=(nb,),
        in_specs=[pl.BlockSpec((tb, blk_w), lambda i: (i, 1)),
                  pl.BlockSpec((tb, blk_w), lambda i: (i, 2)),
                  pl.BlockSpec((8, blk_w), lambda i: (prev8(i), 1)),
                  pl.BlockSpec((8, blk_w), lambda i: (prev8(i), 2)),
                  pl.BlockSpec((8, blk_w), lambda i: (next8(i), 1)),
                  pl.BlockSpec((8, blk_w), lambda i: (next8(i), 2)),
                  pl.BlockSpec((tb, cw), lambda i: (i, 1)),
                  pl.BlockSpec((8, cw), lambda i: (next8(i), 1)),
                  pl.BlockSpec((tb, cw), lambda i: (i, 0)),
                  pl.BlockSpec((tb, 2 * kvw), lambda i: (i, 0)),
                  pl.BlockSpec((tb, 2 * kvw), lambda i: (nxt(i), 0)),
                  pl.BlockSpec((tb, kvw), lambda i: (i, 0)),
                  pl.BlockSpec((tb, kvw), lambda i: (i, 0)),
                  _VMEM],
        out_specs=[pl.BlockSpec((tb, in_w), lambda i: (i, 0)),
                   pl.BlockSpec((3, cw), lambda i: (0, 0))],
        out_shape=[jax.ShapeDtypeStruct((s, in_w), BF16), jax.ShapeDtypeStruct((3, cw), F32)],
        compiler_params=_params(("arbitrary",)),
    )(proj, proj, proj, proj, proj, proj, d_ac, d_ac, dq, dkv_cur, dkv_prev, cos_t, sin_t, cw_full)


def _dx(d_proj, w_in_g, dpre1, after):
    s, in_w = d_proj.shape
    ns, d, ncol = w_in_g.shape
    tm = min(TM, s)

    def body(dp_ref, w_ref, r_ref, after_ref, o_ref, acc):
        j = pl.program_id(1)
        _accumulate(acc, _dot_nt(dp_ref[...], w_ref[...]), j, ns)

        @pl.when(j == ns - 1)
        def _():
            o_ref[...] = acc[...] + ALPHA * r_ref[...]

    return pl.pallas_call(
        body, name="dx", grid=(s // tm, ns),
        in_specs=[pl.BlockSpec((tm, ncol), lambda i, j: (i, j)),
                  pl.BlockSpec((None, d, ncol), lambda i, j: (j, 0, 0)),
                  pl.BlockSpec((tm, d), lambda i, j: (i, 0)), _ANY],
        out_specs=pl.BlockSpec((None, tm, d), lambda i, j: (0, i, 0)),
        out_shape=jax.ShapeDtypeStruct((1, s, d), F32),
        scratch_shapes=[pltpu.VMEM((tm, d), F32)],
        compiler_params=_params(("parallel", "arbitrary")),
    )(d_proj, w_in_g, dpre1, after)


def kernel(x, positions, w_in, conv_w, sinks, g_attn, g_conv, w_out, ln1_g, ln1_b, w_gate, w_up, w_down, ln2_g, ln2_b, loss_target, m_w_in, m_conv_w, m_sinks, m_g_attn, m_g_conv, m_w_out, m_ln1_g, m_ln1_b, m_w_gate, m_w_up, m_w_down, m_ln2_g, m_ln2_b, v_w_in, v_conv_w, v_sinks, v_g_attn, v_g_conv, v_w_out, v_ln1_g, v_ln1_b, v_w_gate, v_w_up, v_w_down, v_ln2_g, v_ln2_b):
    s = x.shape[1]
    d = x.shape[2]

    chip_vec = _chip_id(lax.axis_index("x"), lax.axis_index("y")).astype(jnp.int32).reshape(1)
    wnames = ["w_in", "w_out", "w_gate", "w_up", "w_down"]
    cw_full = _allgather_conv_w(conv_w)
    buf_in = _cast_weight(w_in, chip_vec, cw_full, "cast_w_in")
    flight_in, token_in = _gather_start([buf_in], cw_full, "gather_start_w_in")
    bufs = [_cast_weight(w, chip_vec, token_in, "cast_" + nme)
            for w, nme in zip([w_out, w_gate, w_up, w_down], wnames[1:])]
    flights_rest, token = _gather_start(bufs, token_in, "gather_start_rest")
    flights = flight_in + flights_rest

    def gathered(i, after):
        send_sems, recv_sems, buf = flights[i]
        buf = _gather_wait(send_sems, recv_sems, buf, after, "gather_wait_" + wnames[i])
        return _sibling_fill(buf, "sibling_fill_" + wnames[i])

    g_ac = jnp.concatenate([g_attn, g_conv], axis=1)

    cos_t, sin_t = _rope_tables(positions.reshape(s, 1) + token[0:1, 0:1].astype(jnp.int32))
    w_in_g = gathered(0, cos_t)
    proj = _in_proj(x, w_in_g)
    w_out_full = gathered(1, proj).reshape(d, d)
    attn = _attention_fwd(proj, cos_t, sin_t, sinks)
    mixed, ac, rstd_ac = _conv_norm(proj, attn, cw_full, g_ac)
    w_gate_g = gathered(2, mixed)
    w_up_g = gathered(3, w_gate_g)
    xhat1, h1, rstd1 = _out_proj_ln(mixed, w_out_full, x, ln1_g, ln1_b)
    act, gate, up = _gate_up(h1, w_gate_g, w_up_g)
    w_down_full = gathered(4, act).reshape(-1, d)
    dpre2, loss_part, g_ln2_g, g_ln2_b = _down_ln_loss(act, w_down_full, xhat1, ln1_g, ln1_b, ln2_g, ln2_b, loss_target)

    cvec = lax.axis_index("c").astype(jnp.int32).reshape(1)

    def reduce_begin(part, after, nme):
        (got,) = _exchange_halves([part], after, "exchange_halves_" + nme)
        chip_sum = _add_halves(part, got, cvec, "add_halves_" + nme)
        return _scatter_start(chip_sum, "scatter_start_" + nme)

    d_gate, d_up = _dact_silu_bwd(dpre2, w_down_full, gate, up)
    p_down = _grad_rows(act, dpre2, d_gate, "grad_w_down")
    f_down = reduce_begin(p_down, p_down, "w_down")
    p_gate, p_up = _grad_cols(h1, [d_gate, d_up], f_down[2], "grad_w_gate_up")
    f_gate = reduce_begin(p_gate, p_gate, "w_gate")
    f_up = reduce_begin(p_up, f_gate[2], "w_up")
    dpre1, g_ln1_g, g_ln1_b = _dh1_ln_bwd(d_gate, d_up, w_gate_g, w_up_g, dpre2, xhat1, rstd1, ln1_g, f_up[2])
    d_ac, g_g_ac = _dmixed_rms_bwd(dpre1, w_out_full, ac, rstd_ac, g_ac)
    p_out = _grad_rows(mixed, dpre1, d_ac, "grad_w_out")
    f_out = reduce_begin(p_out, p_out, "w_out")
    dq, dkv_cur, dkv_prev, g_sinks = _attention_bwd(proj, d_ac, cos_t, sin_t, sinks, f_out[2])
    d_proj, g_conv_w = _dproj_assemble(proj, d_ac, dq, dkv_cur, dkv_prev, cos_t, sin_t, cw_full)
    (p_in,) = _grad_cols(x, [d_proj], d_proj, "grad_w_in", a_3d=True)
    f_in = reduce_begin(p_in, p_in, "w_in")
    grad_x = _dx(d_proj, w_in_g, dpre1, f_in[2])
    red = _allreduce_small(g_ln2_g, g_ln2_b, g_ln1_g, g_ln1_b, g_g_ac, g_conv_w, g_sinks, loss_part, grad_x)

    names = ["w_in", "w_out", "w_gate", "w_up", "w_down"]
    landed = [_scatter_wait(*f, red, "scatter_wait_" + nme)
              for f, nme in zip([f_in, f_out, f_gate, f_up, f_down], names)]
    own_sums = [sm for sm, _ in landed]
    lands = _complete_chip_sums(own_sums, [land for _, land in landed])

    pos_vec = jnp.concatenate([chip_vec, cvec])
    big = {}
    for nme, w, m, v, land, own in zip(names, [w_in, w_out, w_gate, w_up, w_down],
                                       [m_w_in, m_w_out, m_w_gate, m_w_up, m_w_down],
                                       [v_w_in, v_w_out, v_w_gate, v_w_up, v_w_down], lands, own_sums):
        big[nme] = _adamw_shard(w, m, v, land, own, pos_vec, "adamw_" + nme)
    small = _adamw_small(red, {
        "sinks": (sinks, m_sinks, v_sinks), "g_attn": (g_attn, m_g_attn, v_g_attn),
        "g_conv": (g_conv, m_g_conv, v_g_conv), "ln1_g": (ln1_g, m_ln1_g, v_ln1_g),
        "ln1_b": (ln1_b, m_ln1_b, v_ln1_b), "ln2_g": (ln2_g, m_ln2_g, v_ln2_g),
        "ln2_b": (ln2_b, m_ln2_b, v_ln2_b), "conv_w": (conv_w, m_conv_w, v_conv_w)})
    res = {**big, **small}
    order = ["w_in", "conv_w", "sinks", "g_attn", "g_conv", "w_out", "ln1_g", "ln1_b", "w_gate", "w_up", "w_down",
             "ln2_g", "ln2_b"]
    loss = red[6, d // 2 + 128]
    return (loss, grad_x, *[res[n][0] for n in order], *[res[n][1] for n in order],
            *[res[n][2] for n in order], *[res[n][3] for n in order])
```

```python
import functools

import numpy as np
import jax
import jax.numpy as jnp
from jax import lax
from jax.experimental import pallas as pl
from jax.experimental.pallas import tpu as pltpu

F32 = jnp.float32
BF16 = jnp.bfloat16
MESH = pl.DeviceIdType.MESH

HEAD_DIM = 64
N_KV_HEADS = 4
GROUP = 4
WINDOW = 128
ROT_DIM = 16
ROPE_THETA = 500000.0
ATTN_SCALE = HEAD_DIM ** -0.5
ALPHA = 2.0 ** 0.25
LN_EPS = 1e-5
RMS_EPS = 1e-6
ADAM_LR = 0.001
ADAM_B1 = 0.9
ADAM_B2 = 0.999
ADAM_EPS = 1e-08
ADAM_WD = 0.01
ADAM_STEP = 10
N_CHIPS = 4
NEG_BIG = -1e30

V7X_VMEM_BYTES = 64 * 1024 * 1024
VMEM_LIMIT = V7X_VMEM_BYTES - 6 * 1024 * 1024

TM = 512
TK_TOK = 1024
TB_CONV = 256
TR_ELT = 256
ROW_CHUNK = 128


def _params(sem):
    return pltpu.CompilerParams(dimension_semantics=sem, vmem_limit_bytes=VMEM_LIMIT)


def _row_tile(rows, target):
    best = None
    for t in range(16, min(rows, target) + 1, 16):
        if rows % t == 0:
            best = t
    assert best is not None, (rows, target)
    return best


def _dot(a, b):
    return jnp.dot(a, b, preferred_element_type=F32)


def _dot_nt(a, b):
    return lax.dot_general(a, b, (((1,), (1,)), ((), ())), preferred_element_type=F32)


def _dot_tn(a, b):
    return lax.dot_general(a, b, (((0,), (0,)), ((), ())), preferred_element_type=F32)


def _mesh_pos():
    x, y, c = lax.axis_index("x"), lax.axis_index("y"), lax.axis_index("c")
    chips = [(1 - x, y), (x, 1 - y), (1 - x, 1 - y)]
    return x, y, c, chips


def _chip_id(px, py):
    return 2 * px + py


def _rope(t, cos, sgn_sin, sign):
    w = t.shape[1]
    lane = lax.broadcasted_iota(jnp.int32, t.shape, 1) & (HEAD_DIM - 1)
    partner = jnp.where(lane < ROT_DIM // 2, pltpu.roll(t, w - ROT_DIM // 2, 1), pltpu.roll(t, ROT_DIM // 2, 1))
    return t * cos + sign * (partner * sgn_sin)


def _tile_lanes(t, n):
    return jnp.concatenate([t] * n, axis=1)


def _sigmoid(g):
    return 1.0 / (1.0 + jnp.exp(-g))


def _for_row_chunks(n_rows, fn):
    def step(r, carry):
        fn(pl.ds(pl.multiple_of(r * ROW_CHUNK, ROW_CHUNK), ROW_CHUNK))
        return carry

    lax.fori_loop(0, n_rows // ROW_CHUNK, step, 0)


def _accumulate(acc, make_val, k, nk):
    if nk == 1:
        acc[...] = make_val()
        return

    @pl.when(k == 0)
    def _():
        acc[...] = jnp.zeros_like(acc)

    acc[...] += make_val()


def _ln_fwd(pre):
    mu = jnp.mean(pre, axis=-1, keepdims=True)
    cen = pre - mu
    var = jnp.mean(cen * cen, axis=-1, keepdims=True)
    rstd = lax.rsqrt(var + LN_EPS)
    return cen * rstd, rstd


def _ln_bwd(dy, xhat, rstd, g):
    dxhat = dy * g
    m1 = jnp.mean(dxhat, axis=-1, keepdims=True)
    m2 = jnp.mean(dxhat * xhat, axis=-1, keepdims=True)
    return rstd * (dxhat - m1 - xhat * m2)


def _cast_weight(w, chip_vec, after, name):
    _, r, c = w.shape
    tr = _row_tile(r, TR_ELT)

    def body(chip_ref, w_ref, after_ref, o_ref):
        o_ref[...] = w_ref[...].astype(BF16)

    grid_spec = pltpu.PrefetchScalarGridSpec(
        num_scalar_prefetch=1, grid=(r // tr,),
        in_specs=[pl.BlockSpec((None, tr, c), lambda i, chip_ref: (0, i, 0)), _ANY],
        out_specs=pl.BlockSpec((None, tr, c), lambda i, chip_ref: (chip_ref[0], i, 0)))
    return pl.pallas_call(
        body, name=name, grid_spec=grid_spec,
        out_shape=jax.ShapeDtypeStruct((N_CHIPS, r, c), BF16),
        compiler_params=_params(("parallel",)),
    )(chip_vec, w, after)


_HBM = pl.BlockSpec(memory_space=pltpu.HBM)
_VMEM = pl.BlockSpec(memory_space=pltpu.VMEM)


_SEM = pl.BlockSpec(memory_space=pltpu.SEMAPHORE)
_ANY = pl.BlockSpec(memory_space=pl.ANY)
_EFFECT = pltpu.SideEffectType.DATAFLOW_SIDE_EFFECTING


def _chip_copy(buf, k, chip_of_src, half_rows, send_sems, recv_sems, to):
    part = buf.at[chip_of_src, half_rows]
    return pltpu.make_async_remote_copy(
        src_ref=part, dst_ref=part, send_sem=send_sems.at[k], recv_sem=recv_sems.at[k], device_id=to, device_id_type=MESH)


def _half_rows(buf, which):
    hr = buf.shape[1] // 2
    return pl.ds(which * hr, hr)


def _gather_start(bufs, after, name):
    n = len(bufs)

    def body(*refs):
        ins = refs[:n]
        sends, recvs = refs[n + 1:2 * n + 1], refs[2 * n + 1:3 * n + 1]
        token = refs[4 * n + 1]
        x, y, c, chips = _mesh_pos()
        me = _chip_id(x, y)
        for w in range(n):
            for k, chip in enumerate(chips):
                _chip_copy(ins[w], k, me, _half_rows(ins[w], c), sends[w], recvs[w], (*chip, c)).start()
        token[...] = jnp.zeros_like(token)

    outs = pl.pallas_call(
        body, name=name,
        in_specs=[_HBM] * n + [_ANY],
        out_specs=[_SEM] * (2 * n) + [_HBM] * n + [_VMEM],
        out_shape=[pltpu.SemaphoreType.DMA((3,))] * (2 * n) + [pltpu.HBM(b.shape, b.dtype) for b in bufs]
        + [jax.ShapeDtypeStruct((8, 128), F32)],
        input_output_aliases={w: 2 * n + w for w in range(n)},
        compiler_params=pltpu.CompilerParams(has_side_effects=_EFFECT),
    )(*[pltpu.with_memory_space_constraint(b, pltpu.HBM) for b in bufs], after)
    return [(outs[w], outs[n + w], outs[2 * n + w]) for w in range(n)], outs[3 * n]


def _gather_wait(send_sems, recv_sems, buf, after, name):
    def body(buf_ref, send_ref, recv_ref, after_ref, out_ref):
        x, y, c, chips = _mesh_pos()
        me = _chip_id(x, y)
        for k, chip in enumerate(chips):
            _chip_copy(buf_ref, k, me, _half_rows(buf_ref, c), send_ref, recv_ref, (*chip, c)).wait_send()
        for k, chip in enumerate(chips):
            _chip_copy(buf_ref, k, _chip_id(*chip), _half_rows(buf_ref, c), send_ref, recv_ref, (*chip, c)).wait_recv()

    return pl.pallas_call(
        body, name=name,
        in_specs=[_HBM, _SEM, _SEM, _ANY], out_specs=_HBM,
        out_shape=pltpu.HBM(buf.shape, buf.dtype),
        input_output_aliases={0: 0},
        compiler_params=pltpu.CompilerParams(has_side_effects=_EFFECT),
    )(buf, send_sems, recv_sems, after)


def _sibling_fill(buf, name, own_too=False):
    n_copies = 4 if own_too else 3

    def body(buf_ref, out_ref, send_sems, recv_sems):
        x, y, c, chips = _mesh_pos()
        sibling = (x, y, 1 - c)
        slots = [_chip_id(*chip) for chip in chips] + ([_chip_id(x, y)] if own_too else [])
        copies = []
        for k, slot in enumerate(slots):
            cp = _chip_copy(out_ref, k, slot, _half_rows(out_ref, c), send_sems, recv_sems, sibling)
            cp.start()
            copies.append(cp)
        for k, slot in enumerate(slots):
            _chip_copy(out_ref, k, slot, _half_rows(out_ref, 1 - c), send_sems, recv_sems, sibling).wait_recv()
        for cp in copies:
            cp.wait_send()

    return pl.pallas_call(
        body, name=name,
        in_specs=[_HBM], out_specs=_HBM,
        out_shape=jax.ShapeDtypeStruct(buf.shape, buf.dtype),
        input_output_aliases={0: 0},
        scratch_shapes=[pltpu.SemaphoreType.DMA((n_copies,)), pltpu.SemaphoreType.DMA((n_copies,))],
    )(buf)


def _allgather_conv_w(cw):
    _, kw, cs = cw.shape

    def body(cw_ref, out_ref, send_sems, recv_sems):
        x, y, c, chips = _mesh_pos()
        me = _chip_id(x, y)
        out_ref[pl.ds(me, 1)] = cw_ref[...]
        copies = []
        for k, chip in enumerate(chips):
            cp = pltpu.make_async_remote_copy(
                src_ref=cw_ref.at[0], dst_ref=out_ref.at[me], send_sem=send_sems.at[k], recv_sem=recv_sems.at[k],
                device_id=(*chip, c), device_id_type=MESH)
            cp.start()
            copies.append(cp)
        for k, chip in enumerate(chips):
            pltpu.make_async_remote_copy(
                src_ref=cw_ref.at[0], dst_ref=out_ref.at[_chip_id(*chip)], send_sem=send_sems.at[k],
                recv_sem=recv_sems.at[k], device_id=(*chip, c), device_id_type=MESH).wait_recv()
        for cp in copies:
            cp.wait_send()

    return pl.pallas_call(
        body, name="allgather_conv_w",
        in_specs=[_VMEM], out_specs=_VMEM,
        out_shape=jax.ShapeDtypeStruct((N_CHIPS, kw, cs), F32),
        scratch_shapes=[pltpu.SemaphoreType.DMA((3,)), pltpu.SemaphoreType.DMA((3,))],
    )(cw)


def _exchange_halves(parts, after, name):
    n = len(parts)
    shapes = [p.shape for p in parts]

    def body(*refs):
        ins, outs = refs[:n], refs[n + 1:2 * n + 1]
        send_sems, recv_sems = refs[2 * n + 1:]
        x, y, c, _ = _mesh_pos()
        copies = []
        for w in range(n):
            hr = shapes[w][1] // 2
            cp = pltpu.make_async_remote_copy(
                src_ref=ins[w].at[:, pl.ds((1 - c) * hr, hr)], dst_ref=outs[w],
                send_sem=send_sems.at[w], recv_sem=recv_sems.at[w],
                device_id=(x, y, 1 - c), device_id_type=MESH)
            cp.start()
            copies.append(cp)
        for cp in copies:
            cp.wait()

    return pl.pallas_call(
        body, name=name,
        in_specs=[_HBM] * n + [_ANY], out_specs=[_HBM] * n,
        out_shape=[jax.ShapeDtypeStruct((s[0], s[1] // 2, s[2]), BF16) for s in shapes],
        scratch_shapes=[pltpu.SemaphoreType.DMA((n,)), pltpu.SemaphoreType.DMA((n,))],
    )(*parts, after)


def _add_halves(part, got, cvec, name):
    ns, r, cdim = part.shape
    hr = r // 2
    tr = _row_tile(hr, TR_ELT)
    nblk = hr // tr

    def body(c_ref, a_ref, b_ref, o_ref):
        o_ref[...] = (a_ref[...].astype(F32) + b_ref[...].astype(F32)).astype(BF16)

    grid_spec = pltpu.PrefetchScalarGridSpec(
        num_scalar_prefetch=1, grid=(ns, nblk),
        in_specs=[pl.BlockSpec((None, tr, cdim), lambda s, i, c_ref: (s, c_ref[0] * nblk + i, 0)),
                  pl.BlockSpec((None, tr, cdim), lambda s, i, c_ref: (s, i, 0))],
        out_specs=pl.BlockSpec((None, tr, cdim), lambda s, i, c_ref: (s, i, 0)))
    return pl.pallas_call(
        body, name=name, grid_spec=grid_spec,
        out_shape=jax.ShapeDtypeStruct((ns, hr, cdim), BF16),
        compiler_params=_params(("parallel", "parallel")),
    )(cvec, part, got)


def _scatter_copy(sums_ref, land_ref, k, src_slot, dst_slot, c, send_sems, recv_sems, to):
    return pltpu.make_async_remote_copy(
        src_ref=sums_ref.at[src_slot], dst_ref=land_ref.at[dst_slot, _half_rows(land_ref, c)],
        send_sem=send_sems.at[k], recv_sem=recv_sems.at[k], device_id=to, device_id_type=MESH)


def _scatter_start(sums, name):
    ns, hr, cdim = sums.shape
    land = lax.empty((ns, 2 * hr, cdim), sums.dtype)

    def body(sums_ref, land_ref, send_sems, recv_sems, sums_thru, land_thru):
        x, y, c, chips = _mesh_pos()
        me = _chip_id(x, y)
        for k, chip in enumerate(chips):
            _scatter_copy(sums_ref, land_ref, k, _chip_id(*chip), me, c, send_sems, recv_sems, (*chip, c)).start()

    return pl.pallas_call(
        body, name=name,
        in_specs=[_HBM, _HBM], out_specs=[_SEM, _SEM, _HBM, _HBM],
        out_shape=[pltpu.SemaphoreType.DMA((3,)), pltpu.SemaphoreType.DMA((3,)),
                   pltpu.HBM(sums.shape, sums.dtype), pltpu.HBM(land.shape, land.dtype)],
        input_output_aliases={0: 2, 1: 3},
        compiler_params=pltpu.CompilerParams(has_side_effects=_EFFECT),
    )(pltpu.with_memory_space_constraint(sums, pltpu.HBM), pltpu.with_memory_space_constraint(land, pltpu.HBM))


def _scatter_wait(send_sems, recv_sems, sums, land, after, name):
    def body(sums_ref, land_ref, send_ref, recv_ref, after_ref, sums_out, land_out):
        x, y, c, chips = _mesh_pos()
        me = _chip_id(x, y)
        for k, chip in enumerate(chips):
            _scatter_copy(sums_ref, land_ref, k, _chip_id(*chip), me, c, send_ref, recv_ref, (*chip, c)).wait_send()
        for k, chip in enumerate(chips):
            _scatter_copy(sums_ref, land_ref, k, me, _chip_id(*chip), c, send_ref, recv_ref, (*chip, c)).wait_recv()

    return pl.pallas_call(
        body, name=name,
        in_specs=[_HBM, _HBM, _SEM, _SEM, _ANY], out_specs=[_HBM, _HBM],
        out_shape=[pltpu.HBM(sums.shape, sums.dtype), pltpu.HBM(land.shape, land.dtype)],
        input_output_aliases={0: 0, 1: 1},
        compiler_params=pltpu.CompilerParams(has_side_effects=_EFFECT),
    )(sums, land, send_sems, recv_sems, after)


def _complete_chip_sums(sums, lands):
    n = len(sums)

    def body(*refs):
        sums_refs, outs = refs[:n], refs[2 * n:3 * n]
        send_sems, recv_sems = refs[3 * n:]
        x, y, c, chips = _mesh_pos()
        me = _chip_id(x, y)
        sibling = (x, y, 1 - c)
        slots = [_chip_id(*chip) for chip in chips]
        sent = []
        for w in range(n):
            out = outs[w]
            cp = _scatter_copy(sums_refs[w], out, 3, me, me, c, send_sems.at[w], recv_sems.at[w], sibling)
            cp.start()
            sent.append(cp)
            for k, slot in enumerate(slots):
                cp = _chip_copy(out, k, slot, _half_rows(out, c), send_sems.at[w], recv_sems.at[w], sibling)
                cp.start()
                sent.append(cp)
        for w in range(n):
            out = outs[w]
            _scatter_copy(sums_refs[w], out, 3, me, me, 1 - c, send_sems.at[w], recv_sems.at[w], sibling).wait_recv()
            for k, slot in enumerate(slots):
                _chip_copy(out, k, slot, _half_rows(out, 1 - c), send_sems.at[w], recv_sems.at[w], sibling).wait_recv()
        for cp in sent:
            cp.wait_send()

    return pl.pallas_call(
        body, name="complete_chip_sums",
        in_specs=[_HBM] * (2 * n), out_specs=[_HBM] * n,
        out_shape=[jax.ShapeDtypeStruct(b.shape, b.dtype) for b in lands],
        input_output_aliases={n + w: w for w in range(n)},
        scratch_shapes=[pltpu.SemaphoreType.DMA((n, 4)), pltpu.SemaphoreType.DMA((n, 4))],
    )(*sums, *lands)


SMALL_ROWS = 8


def _allreduce_small(gl2g, gl2b, gl1g, gl1b, g_ac, gcw, gsink, loss, after):
    d = gl2g.shape[1]
    hd = d // 2
    nq = gsink.shape[1]

    def body(a_ref, b_ref, c_ref, d_ref, e_ref, cw_ref, sk_ref, ls_ref, after_ref, out_ref, mine, gath, send_sems,
             recv_sems):
        x, y, c, _ = _mesh_pos()
        me = 4 * x + 2 * y + c
        mine[...] = jnp.zeros_like(mine)
        mine[0:1, :] = a_ref[...]
        mine[1:2, :] = b_ref[...]
        mine[2:3, :] = c_ref[...]
        mine[3:4, :] = d_ref[...]
        mine[4:5, :] = e_ref[...]
        mine[5:6, 0:hd] = cw_ref[0:1, :]
        mine[5:6, hd:d] = cw_ref[1:2, :]
        mine[6:7, 0:hd] = cw_ref[2:3, :]
        mine[6:7, hd:hd + nq] = sk_ref[...]
        mine[6:7, hd + 128:hd + 256] = ls_ref[...]
        gath[pl.ds(me, 1)] = mine[...][None]
        copies = []
        for r in range(1, 8):
            peer = ((1 - x) if r & 4 else x, (1 - y) if r & 2 else y, (1 - c) if r & 1 else c)
            cp = pltpu.make_async_remote_copy(
                src_ref=mine, dst_ref=gath.at[me], send_sem=send_sems.at[r - 1], recv_sem=recv_sems.at[r - 1],
                device_id=peer, device_id_type=MESH)
            cp.start()
            copies.append(cp)
        for r in range(1, 8):
            peer = ((1 - x) if r & 4 else x, (1 - y) if r & 2 else y, (1 - c) if r & 1 else c)
            peer_id = 4 * peer[0] + 2 * peer[1] + peer[2]
            pltpu.make_async_remote_copy(
                src_ref=mine, dst_ref=gath.at[peer_id], send_sem=send_sems.at[r - 1], recv_sem=recv_sems.at[r - 1],
                device_id=peer, device_id_type=MESH).wait_recv()
        for cp in copies:
            cp.wait_send()
        total = gath[0]
        for dev in range(1, 8):
            total = total + gath[dev]
        out_ref[...] = total

    return pl.pallas_call(
        body, name="allreduce_small",
        in_specs=[_VMEM] * 8 + [_ANY], out_specs=_VMEM,
        out_shape=jax.ShapeDtypeStruct((SMALL_ROWS, d), F32),
        scratch_shapes=[pltpu.VMEM((SMALL_ROWS, d), F32), pltpu.VMEM((8, SMALL_ROWS, d), F32),
                        pltpu.SemaphoreType.DMA((7,)), pltpu.SemaphoreType.DMA((7,))],
    )(gl2g, gl2b, gl1g, gl1b, g_ac, gcw, gsink, loss, after)


def _adamw(w, g, m, v):
    m = ADAM_B1 * m + (1.0 - ADAM_B1) * g
    v = ADAM_B2 * v + (1.0 - ADAM_B2) * (g * g)
    m_hat = m / (1.0 - ADAM_B1 ** ADAM_STEP)
    v_hat = v / (1.0 - ADAM_B2 ** ADAM_STEP)
    delta = -ADAM_LR * (m_hat / (jnp.sqrt(v_hat) + ADAM_EPS) + ADAM_WD * w)
    return delta, m, v


def _adamw_shard(w, m, v, land, own, pos_vec, name):
    _, r, c = w.shape
    hr = r // 2
    tr = _row_tile(hr, TR_ELT)
    nh = hr // tr

    def body(pos_ref, w_ref, m_ref, v_ref, l0, l1, l2, l3, own_ref, g_out, d_out, m_out, v_out):
        i = pl.program_id(0)
        mine = (i // nh) == pos_ref[1]
        own_blk = own_ref[...].astype(F32)
        g = None
        for s, l_ref in enumerate([l0, l1, l2, l3]):
            term = jnp.where(mine & (pos_ref[0] == s), own_blk, l_ref[...].astype(F32))
            g = term if g is None else g + term
        delta, nm, nv = _adamw(w_ref[...], g, m_ref[...], v_ref[...])
        g_out[...] = g
        d_out[...] = delta
        m_out[...] = nm
        v_out[...] = nv

    def land_spec(s):
        def index(i, pos_ref):
            skip = (pos_ref[0] == s) & ((i // nh) == pos_ref[1])
            return (s, jnp.where(skip, (i + nh) % (2 * nh), i), 0)
        return pl.BlockSpec((None, tr, c), index)

    blk = pl.BlockSpec((None, tr, c), lambda i, pos_ref: (0, i, 0))
    grid_spec = pltpu.PrefetchScalarGridSpec(
        num_scalar_prefetch=1, grid=(2 * nh,),
        in_specs=[blk, blk, blk] + [land_spec(s) for s in range(N_CHIPS)]
        + [pl.BlockSpec((None, tr, c), lambda i, pos_ref: (pos_ref[0], i % nh, 0))],
        out_specs=[blk] * 4)
    return pl.pallas_call(
        body, name=name, grid_spec=grid_spec,
        out_shape=[jax.ShapeDtypeStruct((1, r, c), F32)] * 4,
        compiler_params=_params(("parallel",)),
    )(pos_vec, w, m, v, land, land, land, land, own)


def _adamw_small(red, params):
    names = ["sinks", "g_attn", "g_conv", "ln1_g", "ln1_b", "ln2_g", "ln2_b", "conv_w"]
    d = red.shape[1]
    hd = d // 2
    flat = []
    for nme in names:
        flat.extend(params[nme])
    nq = params["sinks"][0].shape[1]
    cs = params["conv_w"][0].shape[2]

    def body(*refs):
        red_ref = refs[0]
        ins = refs[1:1 + 3 * len(names)]
        outs = refs[1 + 3 * len(names):]
        x, y, _, _ = _mesh_pos()
        me = _chip_id(x, y)

        def conv_tap(row, base):
            picked = red_ref[row:row + 1, base:base + cs]
            for s in range(1, N_CHIPS):
                picked = jnp.where(me == s, red_ref[row:row + 1, base + s * cs:base + (s + 1) * cs], picked)
            return picked

        grads = {
            "sinks": red_ref[6:7, hd:hd + nq],
            "g_attn": red_ref[4:5, 0:hd],
            "g_conv": red_ref[4:5, hd:d],
            "ln1_g": red_ref[2:3, :],
            "ln1_b": red_ref[3:4, :],
            "ln2_g": red_ref[0:1, :],
            "ln2_b": red_ref[1:2, :],
        }
        for i, nme in enumerate(names):
            w_ref, m_ref, v_ref = ins[3 * i:3 * i + 3]
            g_out, d_out, m_out, v_out = outs[4 * i:4 * i + 4]
            if nme == "conv_w":
                for tap, (row, base) in enumerate([(5, 0), (5, hd), (6, 0)]):
                    g = conv_tap(row, base)
                    delta, nm, nv = _adamw(w_ref[0, tap:tap + 1, :], g, m_ref[0, tap:tap + 1, :], v_ref[0, tap:tap + 1, :])
                    g_out[0, tap:tap + 1, :] = g
                    d_out[0, tap:tap + 1, :] = delta
                    m_out[0, tap:tap + 1, :] = nm
                    v_out[0, tap:tap + 1, :] = nv
            else:
                g = grads[nme]
                delta, nm, nv = _adamw(w_ref[...], g, m_ref[...], v_ref[...])
                g_out[...] = g
                d_out[...] = delta
                m_out[...] = nm
                v_out[...] = nv

    out_shape = []
    for nme in names:
        out_shape.extend([jax.ShapeDtypeStruct(params[nme][0].shape, F32)] * 4)
    outs = pl.pallas_call(
        body, name="adamw_small",
        in_specs=[_VMEM] * (1 + len(flat)), out_specs=[_VMEM] * len(out_shape),
        out_shape=out_shape,
    )(red, *flat)
    return {nme: tuple(outs[4 * i:4 * i + 4]) for i, nme in enumerate(names)}


def _rope_tables(pos_col):
    s = pos_col.shape[0]
    w = N_KV_HEADS * HEAD_DIM
    tb = min(512, s)
    inv_freq = (ROPE_THETA ** (-np.arange(0, ROT_DIM, 2, dtype=np.float32) / ROT_DIM)).astype(np.float32)

    def body(pos_ref, cos_ref, sin_ref):
        pos = pos_ref[...].astype(F32)
        lane = lax.broadcasted_iota(jnp.int32, (tb, w), 1) & (HEAD_DIM - 1)
        fidx = lane & (ROT_DIM // 2 - 1)
        inv = jnp.zeros((tb, w), F32)
        for k in range(ROT_DIM // 2):
            inv = jnp.where(fidx == k, float(inv_freq[k]), inv)
        ang = pos * inv
        rot = lane < ROT_DIM
        cos_ref[...] = jnp.where(rot, jnp.cos(ang), 1.0)
        sin_v = jnp.sin(ang)
        sin_ref[...] = jnp.where(lane < ROT_DIM // 2, -sin_v, jnp.where(rot, sin_v, 0.0))

    return pl.pallas_call(
        body, name="rope_tables", grid=(s // tb,),
        in_specs=[pl.BlockSpec((tb, 1), lambda i: (i, 0))],
        out_specs=[pl.BlockSpec((tb, w), lambda i: (i, 0))] * 2,
        out_shape=[jax.ShapeDtypeStruct((s, w), F32)] * 2,
        compiler_params=_params(("parallel",)),
    )(pos_col)


def _in_proj(x, w_in_g):
    _, s, d = x.shape
    ns, _, ncol = w_in_g.shape
    tm = min(2 * TM, s)

    def body(x_ref, w_ref, o_ref):
        o_ref[...] = _dot(x_ref[...].astype(BF16), w_ref[...])

    return pl.pallas_call(
        body, name="in_proj", grid=(s // tm, ns),
        in_specs=[pl.BlockSpec((None, tm, d), lambda i, j: (0, i, 0)),
                  pl.BlockSpec((None, d, ncol), lambda i, j: (j, 0, 0))],
        out_specs=pl.BlockSpec((tm, ncol), lambda i, j: (i, j)),
        out_shape=jax.ShapeDtypeStruct((s, ns * ncol), F32),
        compiler_params=_params(("parallel", "arbitrary")),
    )(x, w_in_g)


PAIR = 2 * HEAD_DIM
KEYS = 2 * WINDOW


def _pair_operand(t_all, h):
    col = (h // 2) * PAIR
    lane = lax.broadcasted_iota(jnp.int32, (KEYS, PAIR), 1)
    own_low = h % 2 == 0
    mine = jnp.where((lane < HEAD_DIM) if own_low else (lane >= HEAD_DIM), t_all[:, col:col + PAIR], 0.0)
    other = pltpu.roll(mine, HEAD_DIM, 1)
    low, high = (mine, other) if own_low else (other, mine)
    return jnp.concatenate([low, high], axis=0).astype(BF16)


def _pair_grad(acc, h):
    lane = lax.broadcasted_iota(jnp.int32, (KEYS, PAIR), 1)
    low = jnp.where(lane < HEAD_DIM, acc[:KEYS], 0.0)
    high = jnp.where(lane >= HEAD_DIM, acc[KEYS:], 0.0)
    if h % 2 == 0:
        return low + pltpu.roll(high, HEAD_DIM, 1)
    return high + pltpu.roll(low, HEAD_DIM, 1)


N_PAIRS = N_KV_HEADS * GROUP // 2


def _all_probs(q, kk2s, first, sinks_ref):
    assert ATTN_SCALE == 0.125
    q = q * ATTN_SCALE
    qps, scores = [], []
    for pair in range(N_PAIRS):
        qp = q[:, pair * PAIR:(pair + 1) * PAIR].astype(BF16)
        qps.append(qp)
        scores.append(_dot_nt(qp, kk2s[pair // (GROUP // 2)]))
    qi = lax.broadcasted_iota(jnp.int32, (WINDOW, 2 * KEYS), 0)
    kj = lax.broadcasted_iota(jnp.int32, (WINDOW, 2 * KEYS), 1) & (KEYS - 1)
    rel = qi + WINDOW - kj
    valid = (rel >= 0) & (rel < WINDOW) & jnp.logical_not(first & (kj < WINDOW))
    bias = jnp.where(valid, 0.0, NEG_BIG)
    s = (jnp.stack(scores, axis=0) + bias[None]).reshape(N_PAIRS * WINDOW, 2 * KEYS)
    probs, p_sinks = [], []
    for t in range(2):
        st = s[:, t * KEYS:(t + 1) * KEYS]
        sink = jnp.concatenate([jnp.broadcast_to(sinks_ref[0:1, 2 * pair + t:2 * pair + t + 1], (WINDOW, 1))
                                for pair in range(N_PAIRS)], axis=0)
        m = jnp.maximum(jnp.max(st, axis=1, keepdims=True), sink)
        e = jnp.exp(st - m)
        e_sink = jnp.exp(sink - m)
        inv_l = 1.0 / (jnp.sum(e, axis=1, keepdims=True) + e_sink)
        probs.append(e * inv_l)
        p_sinks.append(e_sink * inv_l)
    return qps, jnp.concatenate(probs, axis=1), p_sinks


def _roped_qkv(cur_ref, prev_ref, cos_ref, sin_ref, cosp_ref, sinp_ref, qw, kvw):
    cur = cur_ref[...]
    cos, sin = cos_ref[...], sin_ref[...]
    cos_q, sin_q = _tile_lanes(cos, GROUP), _tile_lanes(sin, GROUP)
    q = _rope(cur[:, :qw], cos_q, sin_q, 1.0)
    prev = prev_ref[...]
    k_all = jnp.concatenate([_rope(prev[:, :kvw], cosp_ref[...], sinp_ref[...], 1.0),
                             _rope(cur[:, qw:qw + kvw], cos, sin, 1.0)], axis=0)
    v_all = jnp.concatenate([prev[:, kvw:], cur[:, qw + kvw:]], axis=0)
    return q, k_all, v_all, cos_q, sin_q


def _attention_fwd(proj, cos_t, sin_t, sinks):
    s = proj.shape[0]
    qw = GROUP * N_KV_HEADS * HEAD_DIM
    kvw = N_KV_HEADS * HEAD_DIM
    nb = s // WINDOW

    def body(cur_ref, prev_ref, cos_ref, sin_ref, cosp_ref, sinp_ref, sinks_ref, o_ref):
        first = pl.program_id(0) == 0
        q, k_all, v_all, _, _ = _roped_qkv(cur_ref, prev_ref, cos_ref, sin_ref, cosp_ref, sinp_ref, qw, kvw)
        kk2s = [_pair_operand(k_all, h) for h in range(N_KV_HEADS)]
        vv2s = [_pair_operand(v_all, h) for h in range(N_KV_HEADS)]
        _, probs, _ = _all_probs(q, kk2s, first, sinks_ref)
        probs = probs.astype(BF16)
        outs = [_dot(probs[pair * WINDOW:(pair + 1) * WINDOW], vv2s[pair // (GROUP // 2)]) for pair in range(N_PAIRS)]
        o_ref[...] = jnp.concatenate(outs, axis=1)

    tbl = pl.BlockSpec((WINDOW, kvw), lambda n: (n, 0))
    tbl_prev = pl.BlockSpec((WINDOW, kvw), lambda n: (jnp.maximum(n - 1, 0), 0))
    return pl.pallas_call(
        body, name="attention_fwd", grid=(nb,),
        in_specs=[pl.BlockSpec((WINDOW, qw + 2 * kvw), lambda n: (n, 0)),
                  pl.BlockSpec((WINDOW, 2 * kvw), lambda n: (jnp.maximum(n - 1, 0), (qw // (2 * kvw)))),
                  tbl, tbl, tbl_prev, tbl_prev, _VMEM],
        out_specs=pl.BlockSpec((WINDOW, qw), lambda n: (n, 0)),
        out_shape=jax.ShapeDtypeStruct((s, qw), F32),
        compiler_params=_params(("parallel",)),
    )(proj, proj, cos_t, sin_t, cos_t, sin_t, sinks)


def _conv_taps(cw_ref):
    return [jnp.concatenate([cw_ref[s, k:k + 1, :] for s in range(N_CHIPS)], axis=1) for k in range(3)]


def _shift_down(z, halo, steps):
    rows = z.shape[0]
    row = lax.broadcasted_iota(jnp.int32, z.shape, 0)
    out = pltpu.roll(z, steps, 0)
    for r in range(steps):
        out = jnp.where(row == r, halo[8 - steps + r:8 - steps + r + 1, :], out)
    return out


def _shift_up(z, halo, steps):
    rows = z.shape[0]
    row = lax.broadcasted_iota(jnp.int32, z.shape, 0)
    out = pltpu.roll(z, rows - steps, 0)
    for r in range(steps):
        out = jnp.where(row == rows - steps + r, halo[r:r + 1, :], out)
    return out


def _split_cbu(lo, hi, cw):
    c_gate = lo[:, :cw]
    b_gate = jnp.concatenate([lo[:, cw:], hi[:, :2 * cw - lo.shape[1]]], axis=1)
    u = hi[:, 2 * cw - lo.shape[1]:]
    return c_gate, b_gate, u


def _conv_norm(proj, attn, cw_full, g_ac):
    s, in_w = proj.shape
    cw = attn.shape[1]
    blk_w = in_w // 3
    tb = min(TB_CONV, s)

    def body(lo_ref, hi_ref, lo_h_ref, hi_h_ref, attn_ref, cw_ref, g_ref, mixed_ref, ac_ref, rstd_ref):
        i = pl.program_id(0)
        c_gate, b_gate, u = _split_cbu(lo_ref[...], hi_ref[...], cw)
        c_h, _, u_h = _split_cbu(lo_h_ref[...], hi_h_ref[...], cw)
        z = c_gate * u
        z_h = jnp.where(i == 0, 0.0, c_h * u_h)
        w0, w1, w2 = _conv_taps(cw_ref)
        y = w0 * _shift_down(z, z_h, 2) + w1 * _shift_down(z, z_h, 1) + w2 * z
        conv = b_gate * y
        a = attn_ref[...]
        r_a = lax.rsqrt(jnp.mean(a * a, axis=-1, keepdims=True) + RMS_EPS)
        r_c = lax.rsqrt(jnp.mean(conv * conv, axis=-1, keepdims=True) + RMS_EPS)
        g = g_ref[...]
        mixed_ref[...] = jnp.concatenate([a * r_a * g[:, :cw], conv * r_c * g[:, cw:]], axis=1).astype(BF16)
        ac_ref[...] = jnp.concatenate([a, conv], axis=1)
        rstd_ref[0] = r_a
        rstd_ref[1] = r_c

    halo_idx = lambda i: jnp.maximum(i * (tb // 8) - 1, 0)
    return pl.pallas_call(
        body, name="conv_norm", grid=(s // tb,),
        in_specs=[pl.BlockSpec((tb, blk_w), lambda i: (i, 1)),
                  pl.BlockSpec((tb, blk_w), lambda i: (i, 2)),
                  pl.BlockSpec((8, blk_w), lambda i: (halo_idx(i), 1)),
                  pl.BlockSpec((8, blk_w), lambda i: (halo_idx(i), 2)),
                  pl.BlockSpec((tb, cw), lambda i: (i, 0)),
                  _VMEM, _VMEM],
        out_specs=[pl.BlockSpec((tb, 2 * cw), lambda i: (i, 0)),
                   pl.BlockSpec((tb, 2 * cw), lambda i: (i, 0)),
                   pl.BlockSpec((2, tb, 1), lambda i: (0, i, 0))],
        out_shape=[jax.ShapeDtypeStruct((s, 2 * cw), BF16), jax.ShapeDtypeStruct((s, 2 * cw), F32),
                   jax.ShapeDtypeStruct((2, s, 1), F32)],
        compiler_params=_params(("parallel",)),
    )(proj, proj, proj, proj, attn, cw_full, g_ac)


def _out_proj_ln(mixed, w_out_g, x, ln_g, ln_b):
    s, d = mixed.shape
    tm = min(TM, s)
    tk = d
    nk = d // tk

    def body(a_ref, w_ref, x_ref, g_ref, b_ref, xhat_ref, h_ref, rstd_ref, acc):
        k = pl.program_id(1)
        _accumulate(acc, lambda: _dot(a_ref[...], w_ref[...]), k, nk)

        @pl.when(k == nk - 1)
        def _():
            def rows_fn(rows):
                xhat, rstd = _ln_fwd(ALPHA * x_ref[rows, :] + acc[rows, :])
                xhat_ref[rows, :] = xhat
                h_ref[rows, :] = (xhat * g_ref[...] + b_ref[...]).astype(BF16)
                rstd_ref[rows, :] = rstd

            _for_row_chunks(tm, rows_fn)

    row = pl.BlockSpec((tm, d), lambda i, k: (i, 0))
    return pl.pallas_call(
        body, name="out_proj_ln", grid=(s // tm, nk),
        in_specs=[pl.BlockSpec((tm, tk), lambda i, k: (i, k)),
                  pl.BlockSpec((tk, d), lambda i, k: (k, 0)),
                  pl.BlockSpec((None, tm, d), lambda i, k: (0, i, 0)),
                  _VMEM, _VMEM],
        out_specs=[row, row, pl.BlockSpec((tm, 1), lambda i, k: (i, 0))],
        out_shape=[jax.ShapeDtypeStruct((s, d), F32), jax.ShapeDtypeStruct((s, d), BF16),
                   jax.ShapeDtypeStruct((s, 1), F32)],
        scratch_shapes=[pltpu.VMEM((tm, d), F32)],
        compiler_params=_params(("parallel", "arbitrary")),
    )(mixed, w_out_g, x, ln_g, ln_b)


def _gate_up(h1, w_gate_g, w_up_g):
    s, d = h1.shape
    ns, _, fs = w_gate_g.shape
    tm = min(TM, s)
    tk = d
    nk = d // tk

    def body(h_ref, wg_ref, wu_ref, act_ref, g_ref, u_ref, acc_g, acc_u):
        k = pl.program_id(2)
        _accumulate(acc_g, lambda: _dot(h_ref[...], wg_ref[...]), k, nk)
        _accumulate(acc_u, lambda: _dot(h_ref[...], wu_ref[...]), k, nk)

        @pl.when(k == nk - 1)
        def _():
            def rows_fn(rows):
                g, u = acc_g[rows, :], acc_u[rows, :]
                act_ref[rows, :] = (g * _sigmoid(g) * u).astype(BF16)
                g_ref[rows, :] = g.astype(BF16)
                u_ref[rows, :] = u.astype(BF16)

            _for_row_chunks(tm, rows_fn)

    wspec = pl.BlockSpec((None, tk, fs), lambda i, j, k: (j, k, 0))
    ospec = pl.BlockSpec((tm, fs), lambda i, j, k: (i, j))
    return pl.pallas_call(
        body, name="gate_up", grid=(s // tm, ns, nk),
        in_specs=[pl.BlockSpec((tm, tk), lambda i, j, k: (i, k)), wspec, wspec],
        out_specs=[ospec] * 3,
        out_shape=[jax.ShapeDtypeStruct((s, ns * fs), BF16)] * 3,
        scratch_shapes=[pltpu.VMEM((tm, fs), F32)] * 2,
        compiler_params=_params(("parallel", "arbitrary", "arbitrary")),
    )(h1, w_gate_g, w_up_g)


def _down_ln_loss(act, w_down_g, xhat1, ln1_g, ln1_b, ln2_g, ln2_b, target):
    s, f = act.shape
    d = xhat1.shape[1]
    tm = min(TM, s)
    tk = f // N_CHIPS
    nk = f // tk

    def body(a_ref, w_ref, xh_ref, g1_ref, b1_ref, g2_ref, b2_ref, t_ref, dpre_ref, loss_ref, gg_ref, gb_ref, acc):
        i, k = pl.program_id(0), pl.program_id(1)
        _accumulate(acc, lambda: _dot(a_ref[...], w_ref[...]), k, nk)

        @pl.when(k == nk - 1)
        def _():
            @pl.when(i == 0)
            def _():
                loss_ref[...] = jnp.zeros_like(loss_ref)
                gg_ref[...] = jnp.zeros_like(gg_ref)
                gb_ref[...] = jnp.zeros_like(gb_ref)

            def rows_fn(rows):
                h1 = xh_ref[rows, :] * g1_ref[...] + b1_ref[...]
                xhat, rstd = _ln_fwd(ALPHA * h1 + acc[rows, :])
                g2 = g2_ref[...]
                diff = xhat * g2 + b2_ref[...] - t_ref[rows, :]
                dy = diff * (1.0 / d)
                dpre_ref[rows, :] = _ln_bwd(dy, xhat, rstd, g2)
                sq = jnp.sum(jnp.sum(diff * diff, axis=1, keepdims=True), axis=0, keepdims=True)
                loss_ref[...] += jnp.broadcast_to(sq * (0.5 / d), (1, 128))
                gg_ref[...] += jnp.sum(dy * xhat, axis=0, keepdims=True)
                gb_ref[...] += jnp.sum(dy, axis=0, keepdims=True)

            _for_row_chunks(tm, rows_fn)

    row = pl.BlockSpec((tm, d), lambda i, k: (i, 0))
    vec = pl.BlockSpec((1, d), lambda i, k: (0, 0))
    return pl.pallas_call(
        body, name="down_ln_loss", grid=(s // tm, nk),
        in_specs=[pl.BlockSpec((tm, tk), lambda i, k: (i, k)),
                  pl.BlockSpec((tk, d), lambda i, k: (k, 0)),
                  row, _VMEM, _VMEM, _VMEM, _VMEM,
                  pl.BlockSpec((None, tm, d), lambda i, k: (0, i, 0))],
        out_specs=[row, pl.BlockSpec((1, 128), lambda i, k: (0, 0)), vec, vec],
        out_shape=[jax.ShapeDtypeStruct((s, d), F32), jax.ShapeDtypeStruct((1, 128), F32),
                   jax.ShapeDtypeStruct((1, d), F32), jax.ShapeDtypeStruct((1, d), F32)],
        scratch_shapes=[pltpu.VMEM((tm, d), F32)],
        compiler_params=_params(("arbitrary", "arbitrary")),
    )(act, w_down_g, xhat1, ln1_g, ln1_b, ln2_g, ln2_b, target)


def _dact_silu_bwd(dpre2, w_down_g, gate, up):
    s, d = dpre2.shape
    f = gate.shape[1]
    fs = f // N_CHIPS
    tm = min(TM, s)

    def body(dp_ref, w_ref, g_ref, u_ref, dg_ref, du_ref):
        d_act = _dot_nt(dp_ref[...].astype(BF16), w_ref[...])
        g = g_ref[...].astype(F32)
        u = u_ref[...].astype(F32)
        sg = _sigmoid(g)
        dg_ref[...] = (d_act * u * (sg * (1.0 + g * (1.0 - sg)))).astype(BF16)
        du_ref[...] = (d_act * (g * sg)).astype(BF16)

    blk = pl.BlockSpec((tm, fs), lambda i, j: (i, j))
    return pl.pallas_call(
        body, name="dact_silu_bwd", grid=(s // tm, N_CHIPS),
        in_specs=[pl.BlockSpec((tm, d), lambda i, j: (i, 0)),
                  pl.BlockSpec((fs, d), lambda i, j: (j, 0)), blk, blk],
        out_specs=[blk, blk],
        out_shape=[jax.ShapeDtypeStruct((s, f), BF16)] * 2,
        compiler_params=_params(("parallel", "arbitrary")),
    )(dpre2, w_down_g, gate, up)


def _grad_rows(a, b, after, name, row_blocks=1):
    s, m = a.shape
    n = b.shape[1]
    ms = m // N_CHIPS
    tmw = ms // row_blocks
    tk = min(TK_TOK, s)
    nk = s // tk

    def body(a_ref, b_ref, after_ref, o_ref, acc):
        k = pl.program_id(2)
        _accumulate(acc, lambda: _dot_tn(a_ref[...].astype(BF16), b_ref[...].astype(BF16)), k, nk)

        @pl.when(k == nk - 1)
        def _():
            o_ref[...] = acc[...].astype(BF16)

    return pl.pallas_call(
        body, name=name, grid=(N_CHIPS, row_blocks, nk),
        in_specs=[pl.BlockSpec((tk, tmw), lambda j, r, k: (k, j * row_blocks + r)),
                  pl.BlockSpec((tk, n), lambda j, r, k: (k, 0)), _ANY],
        out_specs=pl.BlockSpec((None, tmw, n), lambda j, r, k: (j, r, 0)),
        out_shape=jax.ShapeDtypeStruct((N_CHIPS, ms, n), BF16),
        scratch_shapes=[pltpu.VMEM((tmw, n), F32)],
        compiler_params=_params(("parallel", "parallel", "arbitrary")),
    )(a, b, after)


def _grad_cols(a, bs, after, name, a_3d=False, row_blocks=2):
    s, m = a.shape[-2:]
    n = bs[0].shape[1]
    ns = n // N_CHIPS
    nb = len(bs)
    tmw = m // row_blocks
    tk = min(TK_TOK, s)
    nk = s // tk

    def body(*refs):
        a_ref, b_refs, o_refs, accs = refs[0], refs[1:1 + nb], refs[2 + nb:2 + 2 * nb], refs[2 + 2 * nb:]
        k = pl.program_id(2)
        for b_ref, acc in zip(b_refs, accs):
            _accumulate(acc, lambda b_ref=b_ref: _dot_tn(a_ref[...].astype(BF16), b_ref[...].astype(BF16)), k, nk)

        @pl.when(k == nk - 1)
        def _():
            for o_ref, acc in zip(o_refs, accs):
                o_ref[...] = acc[...].astype(BF16)

    if a_3d:
        a_spec = pl.BlockSpec((None, tk, tmw), lambda j, r, k: (0, k, r))
    else:
        a_spec = pl.BlockSpec((tk, tmw), lambda j, r, k: (k, r))
    return pl.pallas_call(
        body, name=name, grid=(N_CHIPS, row_blocks, nk),
        in_specs=[a_spec] + [pl.BlockSpec((tk, ns), lambda j, r, k: (k, j))] * nb + [_ANY],
        out_specs=[pl.BlockSpec((None, tmw, ns), lambda j, r, k: (j, r, 0))] * nb,
        out_shape=[jax.ShapeDtypeStruct((N_CHIPS, m, ns), BF16)] * nb,
        scratch_shapes=[pltpu.VMEM((tmw, ns), F32)] * nb,
        compiler_params=_params(("parallel", "parallel", "arbitrary")),
    )(a, *bs, after)


def _dh1_ln_bwd(d_gate, d_up, w_gate_g, w_up_g, dpre2, xhat1, rstd1, ln1_g, after):
    s, f = d_gate.shape
    d = dpre2.shape[1]
    hd = d // 2
    fs = f // N_CHIPS
    tm = min(TM, s)

    def body(dg_ref, du_ref, wg_ref, wu_ref, dp2_ref, xh_ref, rs_ref, g_ref, after_ref, dpre_ref, gg_ref, gb_ref,
             acc_lo, acc_hi):
        i, j, half = pl.program_id(0), pl.program_id(1), pl.program_id(2)
        def product():
            return _dot_nt(dg_ref[...], wg_ref[...]) + _dot_nt(du_ref[...], wu_ref[...])

        @pl.when(half == 0)
        def _():
            _accumulate(acc_lo, product, j, N_CHIPS)

        @pl.when(half == 1)
        def _():
            _accumulate(acc_hi, product, j, N_CHIPS)

        @pl.when((j == N_CHIPS - 1) & (half == 1))
        def _():
            @pl.when(i == 0)
            def _():
                gg_ref[...] = jnp.zeros_like(gg_ref)
                gb_ref[...] = jnp.zeros_like(gb_ref)

            def rows_fn(rows):
                dh = jnp.concatenate([acc_lo[rows, :], acc_hi[rows, :]], axis=1) + ALPHA * dp2_ref[rows, :]
                xhat = xh_ref[rows, :]
                dpre_ref[rows, :] = _ln_bwd(dh, xhat, rs_ref[rows, :], g_ref[...])
                gg_ref[...] += jnp.sum(dh * xhat, axis=0, keepdims=True)
                gb_ref[...] += jnp.sum(dh, axis=0, keepdims=True)

            _for_row_chunks(tm, rows_fn)

    row = pl.BlockSpec((tm, d), lambda i, j, h: (i, 0))
    vec = pl.BlockSpec((1, d), lambda i, j, h: (0, 0))
    act_blk = pl.BlockSpec((tm, fs), lambda i, j, h: (i, j))
    w_blk = pl.BlockSpec((None, hd, fs), lambda i, j, h: (j, h, 0))
    return pl.pallas_call(
        body, name="dh1_ln_bwd", grid=(s // tm, N_CHIPS, 2),
        in_specs=[act_blk, act_blk, w_blk, w_blk, row, row, pl.BlockSpec((tm, 1), lambda i, j, h: (i, 0)), _VMEM,
                  _ANY],
        out_specs=[row, vec, vec],
        out_shape=[jax.ShapeDtypeStruct((s, d), F32), jax.ShapeDtypeStruct((1, d), F32),
                   jax.ShapeDtypeStruct((1, d), F32)],
        scratch_shapes=[pltpu.VMEM((tm, hd), F32)] * 2,
        compiler_params=_params(("arbitrary", "arbitrary", "arbitrary")),
    )(d_gate, d_up, w_gate_g, w_up_g, dpre2, xhat1, rstd1, ln1_g, after)


def _dmixed_rms_bwd(dpre1, w_out_g, ac, rstd, g_ac):
    s, d = dpre1.shape
    hd = d // 2
    tm = min(TM, s)

    def body(dp_ref, w_ref, ac_ref, rs_ref, g_ref, dac_ref, gg_ref):
        i = pl.program_id(1)
        dm = _dot_nt(dp_ref[...].astype(BF16), w_ref[...])
        pre = ac_ref[...]
        r = rs_ref[...]
        gdm = dm * g_ref[...]
        dac_ref[...] = r * gdm - pre * (r * r * r) * jnp.mean(gdm * pre, axis=-1, keepdims=True)
        gg = jnp.sum(dm * pre * r, axis=0, keepdims=True)

        @pl.when(i == 0)
        def _():
            gg_ref[...] = gg

        @pl.when(i > 0)
        def _():
            gg_ref[...] += gg

    return pl.pallas_call(
        body, name="dmixed_rms_bwd", grid=(2, s // tm),
        in_specs=[pl.BlockSpec((tm, d), lambda h, i: (i, 0)),
                  pl.BlockSpec((hd, d), lambda h, i: (h, 0)),
                  pl.BlockSpec((tm, hd), lambda h, i: (i, h)),
                  pl.BlockSpec((None, tm, 1), lambda h, i: (h, i, 0)),
                  pl.BlockSpec((1, hd), lambda h, i: (0, h))],
        out_specs=[pl.BlockSpec((tm, hd), lambda h, i: (i, h)),
                   pl.BlockSpec((1, hd), lambda h, i: (0, h))],
        out_shape=[jax.ShapeDtypeStruct((s, d), F32), jax.ShapeDtypeStruct((1, d), F32)],
        compiler_params=_params(("arbitrary", "arbitrary")),
    )(dpre1, w_out_g, ac, rstd, g_ac)


def _attention_bwd(proj, d_ac, cos_t, sin_t, sinks, after):
    s = proj.shape[0]
    qw = GROUP * N_KV_HEADS * HEAD_DIM
    kvw = N_KV_HEADS * HEAD_DIM
    nb = s // WINDOW
    nq = GROUP * N_KV_HEADS

    def body(cur_ref, prev_ref, do_ref, cos_ref, sin_ref, cosp_ref, sinp_ref, sinks_ref, after_ref,
             dq_ref, dcur_ref, dprev_ref, dsink_ref):
        n = pl.program_id(0)
        first = n == 0
        q, k_all, v_all, cos_q, sin_q = _roped_qkv(cur_ref, prev_ref, cos_ref, sin_ref, cosp_ref, sinp_ref, qw, kvw)
        kk2s = [_pair_operand(k_all, h) for h in range(N_KV_HEADS)]
        vv2s = [_pair_operand(v_all, h) for h in range(N_KV_HEADS)]
        qps, probs, p_sinks = _all_probs(q, kk2s, first, sinks_ref)
        dops = [do_ref[:, pair * PAIR:(pair + 1) * PAIR].astype(BF16) for pair in range(N_PAIRS)]
        d_probs = jnp.concatenate([_dot_nt(dops[pair], vv2s[pair // (GROUP // 2)]) for pair in range(N_PAIRS)], axis=0)
        d_s, ds_sinks = [], []
        for t in range(2):
            cols = slice(t * KEYS, (t + 1) * KEYS)
            delta = jnp.sum(probs[:, cols] * d_probs[:, cols], axis=1, keepdims=True)
            d_s.append(probs[:, cols] * (d_probs[:, cols] - delta))
            ds_sinks.append(-p_sinks[t] * delta)
        d_s = jnp.concatenate(d_s, axis=1).astype(BF16)
        probs = probs.astype(BF16)
        dq_parts, dk_tiles, dv_tiles, dsink_parts = [], [], [], []
        for h in range(N_KV_HEADS):
            dkk2, dvv2 = None, None
            for p in range(GROUP // 2):
                pair = (GROUP // 2) * h + p
                rows = slice(pair * WINDOW, (pair + 1) * WINDOW)
                dq_parts.append(_dot(d_s[rows], kk2s[h]) * ATTN_SCALE)
                dk_term = _dot_tn(d_s[rows], qps[pair])
                dv_term = _dot_tn(probs[rows], dops[pair])
                dkk2 = dk_term if dkk2 is None else dkk2 + dk_term
                dvv2 = dv_term if dvv2 is None else dvv2 + dv_term
                dsink_parts.extend([jnp.sum(ds_sinks[t][rows], axis=0, keepdims=True) for t in range(2)])
            dk_tiles.append(_pair_grad(dkk2, h))
            dv_tiles.append(_pair_grad(dvv2, h))
        dq_ref[...] = _rope(jnp.concatenate(dq_parts, axis=1), cos_q, sin_q, -1.0)
        dk = jnp.concatenate([dk_tiles[0] + dk_tiles[1], dk_tiles[2] + dk_tiles[3]], axis=1)
        dv = jnp.concatenate([dv_tiles[0] + dv_tiles[1], dv_tiles[2] + dv_tiles[3]], axis=1)
        dprev_ref[...] = jnp.concatenate([dk[:WINDOW], dv[:WINDOW]], axis=1)
        dcur_ref[...] = jnp.concatenate([dk[WINDOW:], dv[WINDOW:]], axis=1)
        dsink = jnp.concatenate(dsink_parts, axis=1)

        @pl.when(first)
        def _():
            dsink_ref[...] = dsink

        @pl.when(n > 0)
        def _():
            dsink_ref[...] += dsink

    tbl = pl.BlockSpec((WINDOW, kvw), lambda n: (n, 0))
    tbl_prev = pl.BlockSpec((WINDOW, kvw), lambda n: (jnp.maximum(n - 1, 0), 0))
    kv_blk = pl.BlockSpec((WINDOW, 2 * kvw), lambda n: (n, 0))
    return pl.pallas_call(
        body, name="attention_bwd", grid=(nb,),
        in_specs=[pl.BlockSpec((WINDOW, qw + 2 * kvw), lambda n: (n, 0)),
                  pl.BlockSpec((WINDOW, 2 * kvw), lambda n: (jnp.maximum(n - 1, 0), (qw // (2 * kvw)))),
                  pl.BlockSpec((WINDOW, qw), lambda n: (n, 0)),
                  tbl, tbl, tbl_prev, tbl_prev, _VMEM, _ANY],
        out_specs=[pl.BlockSpec((WINDOW, qw), lambda n: (n, 0)), kv_blk, kv_blk,
                   pl.BlockSpec((1, nq), lambda n: (0, 0))],
        out_shape=[jax.ShapeDtypeStruct((s, qw), F32), jax.ShapeDtypeStruct((s, 2 * kvw), F32),
                   jax.ShapeDtypeStruct((s, 2 * kvw), F32), jax.ShapeDtypeStruct((1, nq), F32)],
        compiler_params=_params(("arbitrary",)),
    )(proj, proj, d_ac, cos_t, sin_t, cos_t, sin_t, sinks, after)


def _dproj_assemble(proj, d_ac, dq, dkv_cur, dkv_prev, cos_t, sin_t, cw_full):
    s, in_w = proj.shape
    cw = dq.shape[1]
    kvw = N_KV_HEADS * HEAD_DIM
    blk_w = in_w // 3
    tb = WINDOW
    nb = s // tb

    def body(lo_ref, hi_ref, lo_p_ref, hi_p_ref, lo_n_ref, hi_n_ref, dconv_ref, dconv_n_ref,
             dq_ref, dcur_ref, dprev_n_ref, cos_ref, sin_ref, cw_ref, dproj_ref, gcw_ref):
        i = pl.program_id(0)
        last = i == nb - 1
        c_gate, b_gate, u = _split_cbu(lo_ref[...], hi_ref[...], cw)
        c_p, _, u_p = _split_cbu(lo_p_ref[...], hi_p_ref[...], cw)
        _, b_n, _ = _split_cbu(lo_n_ref[...], hi_n_ref[...], cw)
        z = c_gate * u
        z_p = jnp.where(i == 0, 0.0, c_p * u_p)
        z1 = _shift_down(z, z_p, 1)
        z2 = _shift_down(z, z_p, 2)
        w0, w1, w2 = _conv_taps(cw_ref)
        y = w0 * z2 + w1 * z1 + w2 * z
        d_conv = dconv_ref[...]
        d_b = d_conv * y
        d_y = d_conv * b_gate
        d_y_n = jnp.where(last, 0.0, dconv_n_ref[...] * b_n)
        d_z = w2 * d_y + w1 * _shift_up(d_y, d_y_n, 1) + w0 * _shift_up(d_y, d_y_n, 2)
        d_c = d_z * u
        d_u = d_z * c_gate
        gcw = jnp.concatenate([jnp.sum(d_y * z2, axis=0, keepdims=True), jnp.sum(d_y * z1, axis=0, keepdims=True),
                               jnp.sum(d_y * z, axis=0, keepdims=True)], axis=0)

        @pl.when(i == 0)
        def _():
            gcw_ref[...] = gcw

        @pl.when(i > 0)
        def _():
            gcw_ref[...] += gcw

        dkv = dcur_ref[...] + jnp.where(last, 0.0, dprev_n_ref[...])
        dk = _rope(dkv[:, :kvw], cos_ref[...], sin_ref[...], -1.0)
        dproj_ref[...] = jnp.concatenate([dq_ref[...], dk, dkv[:, kvw:], d_c, d_b, d_u], axis=1).astype(BF16)

    prev8 = lambda i: jnp.maximum(i * (tb // 8) - 1, 0)
    next8 = lambda i: jnp.minimum((i + 1) * (tb // 8), s // 8 - 1)
    nxt = lambda i: jnp.minimum(i + 1, nb - 1)
    return pl.pallas_call(
        body, name="dproj_assemble", grid=(nb,),
        in_specs=[pl.BlockSpec((tb, blk_w), lambda i: (i, 1)),
                  pl.BlockSpec((tb, blk_w), lambda i: (i, 2)),
                  pl.BlockSpec((8, blk_w), lambda i: (prev8(i), 1)),
                  pl.BlockSpec((8, blk_w), lambda i: (prev8(i), 2)),
                  pl.BlockSpec((8, blk_w), lambda i: (next8(i), 1)),
                  pl.BlockSpec((8, blk_w), lambda i: (next8(i), 2)),
                  pl.BlockSpec((tb, cw), lambda i: (i, 1)),
                  pl.BlockSpec((8, cw), lambda i: (next8(i), 1)),
                  pl.BlockSpec((tb, cw), lambda i: (i, 0)),
                  pl.BlockSpec((tb, 2 * kvw), lambda i: (i, 0)),
                  pl.BlockSpec((tb, 2 * kvw), lambda i: (nxt(i), 0)),
                  pl.BlockSpec((tb, kvw), lambda i: (i, 0)),
                  pl.BlockSpec((tb, kvw), lambda i: (i, 0)),
                  _VMEM],
        out_specs=[pl.BlockSpec((tb, in_w), lambda i: (i, 0)),
                   pl.BlockSpec((3, cw), lambda i: (0, 0))],
        out_shape=[jax.ShapeDtypeStruct((s, in_w), BF16), jax.ShapeDtypeStruct((3, cw), F32)],
        compiler_params=_params(("arbitrary",)),
    )(proj, proj, proj, proj, proj, proj, d_ac, d_ac, dq, dkv_cur, dkv_prev, cos_t, sin_t, cw_full)


def _dx(d_proj, w_in_g, dpre1, after):
    s, in_w = d_proj.shape
    ns, d, ncol = w_in_g.shape
    tm = min(TM, s)

    def body(dp_ref, w_ref, r_ref, after_ref, o_ref, acc):
        j = pl.program_id(1)
        _accumulate(acc, lambda: _dot_nt(dp_ref[...], w_ref[...]), j, ns)

        @pl.when(j == ns - 1)
        def _():
            o_ref[...] = acc[...] + ALPHA * r_ref[...]

    return pl.pallas_call(
        body, name="dx", grid=(s // tm, ns),
        in_specs=[pl.BlockSpec((tm, ncol), lambda i, j: (i, j)),
                  pl.BlockSpec((None, d, ncol), lambda i, j: (j, 0, 0)),
                  pl.BlockSpec((tm, d), lambda i, j: (i, 0)), _ANY],
        out_specs=pl.BlockSpec((None, tm, d), lambda i, j: (0, i, 0)),
        out_shape=jax.ShapeDtypeStruct((1, s, d), F32),
        scratch_shapes=[pltpu.VMEM((tm, d), F32)],
        compiler_params=_params(("parallel", "arbitrary")),
    )(d_proj, w_in_g, dpre1, after)


def kernel(x, positions, w_in, conv_w, sinks, g_attn, g_conv, w_out, ln1_g, ln1_b, w_gate, w_up, w_down, ln2_g, ln2_b, loss_target, m_w_in, m_conv_w, m_sinks, m_g_attn, m_g_conv, m_w_out, m_ln1_g, m_ln1_b, m_w_gate, m_w_up, m_w_down, m_ln2_g, m_ln2_b, v_w_in, v_conv_w, v_sinks, v_g_attn, v_g_conv, v_w_out, v_ln1_g, v_ln1_b, v_w_gate, v_w_up, v_w_down, v_ln2_g, v_ln2_b):
    s = x.shape[1]
    d = x.shape[2]

    chip_vec = _chip_id(lax.axis_index("x"), lax.axis_index("y")).astype(jnp.int32).reshape(1)
    wnames = ["w_in", "w_out", "w_gate", "w_up", "w_down"]
    cw_full = _allgather_conv_w(conv_w)
    buf_in = _cast_weight(w_in, chip_vec, cw_full, "cast_w_in")
    flight_in, token_in = _gather_start([buf_in], cw_full, "gather_start_w_in")
    bufs = [_cast_weight(w, chip_vec, token_in, "cast_" + nme)
            for w, nme in zip([w_out, w_gate, w_up, w_down], wnames[1:])]
    flights_rest, token = _gather_start(bufs, token_in, "gather_start_rest")
    flights = flight_in + flights_rest

    def gathered(i, after):
        send_sems, recv_sems, buf = flights[i]
        buf = _gather_wait(send_sems, recv_sems, buf, after, "gather_wait_" + wnames[i])
        return _sibling_fill(buf, "sibling_fill_" + wnames[i])

    g_ac = jnp.concatenate([g_attn, g_conv], axis=1)

    cos_t, sin_t = _rope_tables(positions.reshape(s, 1) + token[0:1, 0:1].astype(jnp.int32))
    w_in_g = gathered(0, cos_t)
    proj = _in_proj(x, w_in_g)
    w_out_full = gathered(1, proj).reshape(d, d)
    attn = _attention_fwd(proj, cos_t, sin_t, sinks)
    mixed, ac, rstd_ac = _conv_norm(proj, attn, cw_full, g_ac)
    w_gate_g = gathered(2, mixed)
    w_up_g = gathered(3, w_gate_g)
    xhat1, h1, rstd1 = _out_proj_ln(mixed, w_out_full, x, ln1_g, ln1_b)
    act, gate, up = _gate_up(h1, w_gate_g, w_up_g)
    w_down_full = gathered(4, act).reshape(-1, d)
    dpre2, loss_part, g_ln2_g, g_ln2_b = _down_ln_loss(act, w_down_full, xhat1, ln1_g, ln1_b, ln2_g, ln2_b, loss_target)

    cvec = lax.axis_index("c").astype(jnp.int32).reshape(1)

    def reduce_begin(part, after, nme):
        (got,) = _exchange_halves([part], after, "exchange_halves_" + nme)
        chip_sum = _add_halves(part, got, cvec, "add_halves_" + nme)
        return _scatter_start(chip_sum, "scatter_start_" + nme)

    d_gate, d_up = _dact_silu_bwd(dpre2, w_down_full, gate, up)
    p_down = _grad_rows(act, dpre2, d_gate, "grad_w_down")
    f_down = reduce_begin(p_down, p_down, "w_down")
    p_gate, p_up = _grad_cols(h1, [d_gate, d_up], f_down[2], "grad_w_gate_up")
    f_gate = reduce_begin(p_gate, p_gate, "w_gate")
    f_up = reduce_begin(p_up, f_gate[2], "w_up")
    dpre1, g_ln1_g, g_ln1_b = _dh1_ln_bwd(d_gate, d_up, w_gate_g, w_up_g, dpre2, xhat1, rstd1, ln1_g, f_up[2])
    d_ac, g_g_ac = _dmixed_rms_bwd(dpre1, w_out_full, ac, rstd_ac, g_ac)
    p_out = _grad_rows(mixed, dpre1, d_ac, "grad_w_out")
    f_out = reduce_begin(p_out, p_out, "w_out")
    dq, dkv_cur, dkv_prev, g_sinks = _attention_bwd(proj, d_ac, cos_t, sin_t, sinks, f_out[2])
    d_proj, g_conv_w = _dproj_assemble(proj, d_ac, dq, dkv_cur, dkv_prev, cos_t, sin_t, cw_full)
    (p_in,) = _grad_cols(x, [d_proj], d_proj, "grad_w_in", a_3d=True)
    f_in = reduce_begin(p_in, p_in, "w_in")
    grad_x = _dx(d_proj, w_in_g, dpre1, f_in[2])
    red = _allreduce_small(g_ln2_g, g_ln2_b, g_ln1_g, g_ln1_b, g_g_ac, g_conv_w, g_sinks, loss_part, grad_x)

    names = ["w_in", "w_out", "w_gate", "w_up", "w_down"]
    landed = [_scatter_wait(*f, red, "scatter_wait_" + nme)
              for f, nme in zip([f_in, f_out, f_gate, f_up, f_down], names)]
    own_sums = [sm for sm, _ in landed]
    lands = _complete_chip_sums(own_sums, [land for _, land in landed])

    pos_vec = jnp.concatenate([chip_vec, cvec])
    big = {}
    for nme, w, m, v, land, own in zip(names, [w_in, w_out, w_gate, w_up, w_down],
                                       [m_w_in, m_w_out, m_w_gate, m_w_up, m_w_down],
                                       [v_w_in, v_w_out, v_w_gate, v_w_up, v_w_down], lands, own_sums):
        big[nme] = _adamw_shard(w, m, v, land, own, pos_vec, "adamw_" + nme)
    small = _adamw_small(red, {
        "sinks": (sinks, m_sinks, v_sinks), "g_attn": (g_attn, m_g_attn, v_g_attn),
        "g_conv": (g_conv, m_g_conv, v_g_conv), "ln1_g": (ln1_g, m_ln1_g, v_ln1_g),
        "ln1_b": (ln1_b, m_ln1_b, v_ln1_b), "ln2_g": (ln2_g, m_ln2_g, v_ln2_g),
        "ln2_b": (ln2_b, m_ln2_b, v_ln2_b), "conv_w": (conv_w, m_conv_w, v_conv_w)})
    res = {**big, **small}
    order = ["w_in", "conv_w", "sinks", "g_attn", "g_conv", "w_out", "ln1_g", "ln1_b", "w_gate", "w_up", "w_down",
             "ln2_g", "ln2_b"]
    loss = red[6, d // 2 + 128]
    return (loss, grad_x, *[res[n][0] for n in order], *[res[n][1] for n in order],
            *[res[n][2] for n in order], *[res[n][3] for n in order])
```

```python
import functools

import numpy as np
import jax
import jax.numpy as jnp
from jax import lax
from jax.experimental import pallas as pl
from jax.experimental.pallas import tpu as pltpu

F32 = jnp.float32
BF16 = jnp.bfloat16
MESH = pl.DeviceIdType.MESH

HEAD_DIM = 64
N_KV_HEADS = 4
GROUP = 4
WINDOW = 128
ROT_DIM = 16
ROPE_THETA = 500000.0
ATTN_SCALE = HEAD_DIM ** -0.5
ALPHA = 2.0 ** 0.25
LN_EPS = 1e-5
RMS_EPS = 1e-6
ADAM_LR = 0.001
ADAM_B1 = 0.9
ADAM_B2 = 0.999
ADAM_EPS = 1e-08
ADAM_WD = 0.01
ADAM_STEP = 10
N_CHIPS = 4
NEG_BIG = -1e30

V7X_VMEM_BYTES = 64 * 1024 * 1024
VMEM_LIMIT = V7X_VMEM_BYTES - 6 * 1024 * 1024

TM = 512
TK_TOK = 1024
TB_CONV = 256
TR_ELT = 256
ROW_CHUNK = 128


def _params(sem):
    return pltpu.CompilerParams(dimension_semantics=sem, vmem_limit_bytes=VMEM_LIMIT)


def _row_tile(rows, target):
    best = None
    for t in range(16, min(rows, target) + 1, 16):
        if rows % t == 0:
            best = t
    assert best is not None, (rows, target)
    return best


def _dot(a, b):
    return jnp.dot(a, b, preferred_element_type=F32)


def _dot_nt(a, b):
    return lax.dot_general(a, b, (((1,), (1,)), ((), ())), preferred_element_type=F32)


def _dot_tn(a, b):
    return lax.dot_general(a, b, (((0,), (0,)), ((), ())), preferred_element_type=F32)


def _mesh_pos():
    x, y, c = lax.axis_index("x"), lax.axis_index("y"), lax.axis_index("c")
    chips = [(1 - x, y), (x, 1 - y), (1 - x, 1 - y)]
    return x, y, c, chips


def _chip_id(px, py):
    return 2 * px + py


def _rope(t, cos, sgn_sin, sign):
    w = t.shape[1]
    lane = lax.broadcasted_iota(jnp.int32, t.shape, 1) & (HEAD_DIM - 1)
    partner = jnp.where(lane < ROT_DIM // 2, pltpu.roll(t, w - ROT_DIM // 2, 1), pltpu.roll(t, ROT_DIM // 2, 1))
    return t * cos + sign * (partner * sgn_sin)


def _tile_lanes(t, n):
    return jnp.concatenate([t] * n, axis=1)


def _sigmoid(g):
    return 1.0 / (1.0 + jnp.exp(-g))


def _for_row_chunks(n_rows, fn):
    def step(r, carry):
        fn(pl.ds(pl.multiple_of(r * ROW_CHUNK, ROW_CHUNK), ROW_CHUNK))
        return carry

    lax.fori_loop(0, n_rows // ROW_CHUNK, step, 0)


def _accumulate(acc, make_val, k, nk):
    if nk == 1:
        acc[...] = make_val()
        return

    @pl.when(k == 0)
    def _():
        acc[...] = jnp.zeros_like(acc)

    acc[...] += make_val()


def _ln_fwd(pre):
    mu = jnp.mean(pre, axis=-1, keepdims=True)
    cen = pre - mu
    var = jnp.mean(cen * cen, axis=-1, keepdims=True)
    rstd = lax.rsqrt(var + LN_EPS)
    return cen * rstd, rstd


def _ln_bwd(dy, xhat, rstd, g):
    dxhat = dy * g
    m1 = jnp.mean(dxhat, axis=-1, keepdims=True)
    m2 = jnp.mean(dxhat * xhat, axis=-1, keepdims=True)
    return rstd * (dxhat - m1 - xhat * m2)


def _cast_weight(w, chip_vec, after, name):
    _, r, c = w.shape
    tr = _row_tile(r, TR_ELT)

    def body(chip_ref, w_ref, after_ref, o_ref):
        o_ref[...] = w_ref[...].astype(BF16)

    grid_spec = pltpu.PrefetchScalarGridSpec(
        num_scalar_prefetch=1, grid=(r // tr,),
        in_specs=[pl.BlockSpec((None, tr, c), lambda i, chip_ref: (0, i, 0)), _ANY],
        out_specs=pl.BlockSpec((None, tr, c), lambda i, chip_ref: (chip_ref[0], i, 0)))
    return pl.pallas_call(
        body, name=name, grid_spec=grid_spec,
        out_shape=jax.ShapeDtypeStruct((N_CHIPS, r, c), BF16),
        compiler_params=_params(("parallel",)),
    )(chip_vec, w, after)


_HBM = pl.BlockSpec(memory_space=pltpu.HBM)
_VMEM = pl.BlockSpec(memory_space=pltpu.VMEM)


_SEM = pl.BlockSpec(memory_space=pltpu.SEMAPHORE)
_ANY = pl.BlockSpec(memory_space=pl.ANY)
_EFFECT = pltpu.SideEffectType.DATAFLOW_SIDE_EFFECTING


def _chip_copy(buf, k, chip_of_src, half_rows, send_sems, recv_sems, to):
    part = buf.at[chip_of_src, half_rows]
    return pltpu.make_async_remote_copy(
        src_ref=part, dst_ref=part, send_sem=send_sems.at[k], recv_sem=recv_sems.at[k], device_id=to, device_id_type=MESH)


def _half_rows(buf, which):
    hr = buf.shape[1] // 2
    return pl.ds(which * hr, hr)


def _after(value, dep):
    return lax.optimization_barrier((value, dep))[0]


def _flight_start(name, bufs, plan, n_sems, after):
    n = len(bufs)

    def body(*refs):
        sends, _ = plan(refs[:n], refs[n + 1], refs[n + 2])
        for cp in sends:
            cp.start()

    outs = pl.pallas_call(
        body, name=name,
        in_specs=[_HBM] * n + [_ANY], out_specs=[_SEM, _SEM] + [_HBM] * n,
        out_shape=[pltpu.SemaphoreType.DMA((n_sems,))] * 2 + [pltpu.HBM(b.shape, b.dtype) for b in bufs],
        input_output_aliases={i: 2 + i for i in range(n)},
        compiler_params=pltpu.CompilerParams(has_side_effects=_EFFECT),
    )(*[pltpu.with_memory_space_constraint(b, pltpu.HBM) for b in bufs], after)
    return outs[0], outs[1], list(outs[2:])


def _flight_wait(name, flight, plan, after):
    send_sems, recv_sems, bufs = flight
    n = len(bufs)

    def body(*refs):
        sends, recvs = plan(refs[:n], refs[n], refs[n + 1])
        for cp in sends:
            cp.wait_send()
        for cp in recvs:
            cp.wait_recv()

    outs = pl.pallas_call(
        body, name=name,
        in_specs=[_HBM] * n + [_SEM, _SEM, _ANY], out_specs=[_HBM] * n,
        out_shape=[pltpu.HBM(b.shape, b.dtype) for b in bufs],
        input_output_aliases={i: i for i in range(n)},
        compiler_params=pltpu.CompilerParams(has_side_effects=_EFFECT),
    )(*bufs, send_sems, recv_sems, after)
    return list(outs)


def _fill_plan(n_bufs):
    def plan(refs, send_sems, recv_sems):
        x, y, c, chips = _mesh_pos()
        sibling = (x, y, 1 - c)
        sends, recvs = [], []
        for w in range(n_bufs):
            for k, chip in enumerate(chips):
                slot = _chip_id(*chip)
                sends.append(_chip_copy(refs[w], 3 * w + k, slot, _half_rows(refs[w], c), send_sems, recv_sems, sibling))
                recvs.append(_chip_copy(refs[w], 3 * w + k, slot, _half_rows(refs[w], 1 - c), send_sems, recv_sems,
                                        sibling))
        return sends, recvs
    return plan


def _exchange_plan(n_parts):
    def plan(refs, send_sems, recv_sems):
        x, y, c, _ = _mesh_pos()
        copies = []
        for w in range(n_parts):
            part, got = refs[2 * w], refs[2 * w + 1]
            hr = got.shape[1]
            copies.append(pltpu.make_async_remote_copy(
                src_ref=part.at[:, pl.ds((1 - c) * hr, hr)], dst_ref=got, send_sem=send_sems.at[w],
                recv_sem=recv_sems.at[w], device_id=(x, y, 1 - c), device_id_type=MESH))
        return copies, copies
    return plan


def _gather_start(bufs, after, name):
    n = len(bufs)

    def body(*refs):
        ins = refs[:n]
        sends, recvs = refs[n + 1:2 * n + 1], refs[2 * n + 1:3 * n + 1]
        token = refs[4 * n + 1]
        x, y, c, chips = _mesh_pos()
        me = _chip_id(x, y)
        for w in range(n):
            for k, chip in enumerate(chips):
                _chip_copy(ins[w], k, me, _half_rows(ins[w], c), sends[w], recvs[w], (*chip, c)).start()
        token[...] = jnp.zeros_like(token)

    outs = pl.pallas_call(
        body, name=name,
        in_specs=[_HBM] * n + [_ANY],
        out_specs=[_SEM] * (2 * n) + [_HBM] * n + [_VMEM],
        out_shape=[pltpu.SemaphoreType.DMA((3,))] * (2 * n) + [pltpu.HBM(b.shape, b.dtype) for b in bufs]
        + [jax.ShapeDtypeStruct((8, 128), F32)],
        input_output_aliases={w: 2 * n + w for w in range(n)},
        compiler_params=pltpu.CompilerParams(has_side_effects=_EFFECT),
    )(*[pltpu.with_memory_space_constraint(b, pltpu.HBM) for b in bufs], after)
    return [(outs[w], outs[n + w], outs[2 * n + w]) for w in range(n)], outs[3 * n]


def _gather_wait(send_sems, recv_sems, buf, after, name):
    def body(buf_ref, send_ref, recv_ref, after_ref, out_ref):
        x, y, c, chips = _mesh_pos()
        me = _chip_id(x, y)
        for k, chip in enumerate(chips):
            _chip_copy(buf_ref, k, me, _half_rows(buf_ref, c), send_ref, recv_ref, (*chip, c)).wait_send()
        for k, chip in enumerate(chips):
            _chip_copy(buf_ref, k, _chip_id(*chip), _half_rows(buf_ref, c), send_ref, recv_ref, (*chip, c)).wait_recv()

    return pl.pallas_call(
        body, name=name,
        in_specs=[_HBM, _SEM, _SEM, _ANY], out_specs=_HBM,
        out_shape=pltpu.HBM(buf.shape, buf.dtype),
        input_output_aliases={0: 0},
        compiler_params=pltpu.CompilerParams(has_side_effects=_EFFECT),
    )(buf, send_sems, recv_sems, after)


def _sibling_fill(buf, name, own_too=False):
    n_copies = 4 if own_too else 3

    def body(buf_ref, out_ref, send_sems, recv_sems):
        x, y, c, chips = _mesh_pos()
        sibling = (x, y, 1 - c)
        slots = [_chip_id(*chip) for chip in chips] + ([_chip_id(x, y)] if own_too else [])
        copies = []
        for k, slot in enumerate(slots):
            cp = _chip_copy(out_ref, k, slot, _half_rows(out_ref, c), send_sems, recv_sems, sibling)
            cp.start()
            copies.append(cp)
        for k, slot in enumerate(slots):
            _chip_copy(out_ref, k, slot, _half_rows(out_ref, 1 - c), send_sems, recv_sems, sibling).wait_recv()
        for cp in copies:
            cp.wait_send()

    return pl.pallas_call(
        body, name=name,
        in_specs=[_HBM], out_specs=_HBM,
        out_shape=jax.ShapeDtypeStruct(buf.shape, buf.dtype),
        input_output_aliases={0: 0},
        scratch_shapes=[pltpu.SemaphoreType.DMA((n_copies,)), pltpu.SemaphoreType.DMA((n_copies,))],
    )(buf)


def _allgather_conv_w(cw):
    _, kw, cs = cw.shape

    def body(cw_ref, out_ref, send_sems, recv_sems):
        x, y, c, chips = _mesh_pos()
        me = _chip_id(x, y)
        out_ref[pl.ds(me, 1)] = cw_ref[...]
        copies = []
        for k, chip in enumerate(chips):
            cp = pltpu.make_async_remote_copy(
                src_ref=cw_ref.at[0], dst_ref=out_ref.at[me], send_sem=send_sems.at[k], recv_sem=recv_sems.at[k],
                device_id=(*chip, c), device_id_type=MESH)
            cp.start()
            copies.append(cp)
        for k, chip in enumerate(chips):
            pltpu.make_async_remote_copy(
                src_ref=cw_ref.at[0], dst_ref=out_ref.at[_chip_id(*chip)], send_sem=send_sems.at[k],
                recv_sem=recv_sems.at[k], device_id=(*chip, c), device_id_type=MESH).wait_recv()
        for cp in copies:
            cp.wait_send()

    return pl.pallas_call(
        body, name="allgather_conv_w",
        in_specs=[_VMEM], out_specs=_VMEM,
        out_shape=jax.ShapeDtypeStruct((N_CHIPS, kw, cs), F32),
        scratch_shapes=[pltpu.SemaphoreType.DMA((3,)), pltpu.SemaphoreType.DMA((3,))],
    )(cw)


def _exchange_halves(parts, after, name):
    n = len(parts)
    shapes = [p.shape for p in parts]

    def body(*refs):
        ins, outs = refs[:n], refs[n + 1:2 * n + 1]
        send_sems, recv_sems = refs[2 * n + 1:]
        x, y, c, _ = _mesh_pos()
        copies = []
        for w in range(n):
            hr = shapes[w][1] // 2
            cp = pltpu.make_async_remote_copy(
                src_ref=ins[w].at[:, pl.ds((1 - c) * hr, hr)], dst_ref=outs[w],
                send_sem=send_sems.at[w], recv_sem=recv_sems.at[w],
                device_id=(x, y, 1 - c), device_id_type=MESH)
            cp.start()
            copies.append(cp)
        for cp in copies:
            cp.wait()

    return pl.pallas_call(
        body, name=name,
        in_specs=[_HBM] * n + [_ANY], out_specs=[_HBM] * n,
        out_shape=[jax.ShapeDtypeStruct((s[0], s[1] // 2, s[2]), BF16) for s in shapes],
        scratch_shapes=[pltpu.SemaphoreType.DMA((n,)), pltpu.SemaphoreType.DMA((n,))],
    )(*parts, after)


def _add_halves(part, got, cvec, name):
    ns, r, cdim = part.shape
    hr = r // 2
    tr = _row_tile(hr, TR_ELT)
    nblk = hr // tr

    def body(c_ref, a_ref, b_ref, o_ref):
        o_ref[...] = (a_ref[...].astype(F32) + b_ref[...].astype(F32)).astype(BF16)

    grid_spec = pltpu.PrefetchScalarGridSpec(
        num_scalar_prefetch=1, grid=(ns, nblk),
        in_specs=[pl.BlockSpec((None, tr, cdim), lambda s, i, c_ref: (s, c_ref[0] * nblk + i, 0)),
                  pl.BlockSpec((None, tr, cdim), lambda s, i, c_ref: (s, i, 0))],
        out_specs=pl.BlockSpec((None, tr, cdim), lambda s, i, c_ref: (s, i, 0)))
    return pl.pallas_call(
        body, name=name, grid_spec=grid_spec,
        out_shape=jax.ShapeDtypeStruct((ns, hr, cdim), BF16),
        compiler_params=_params(("parallel", "parallel")),
    )(cvec, part, got)


def _scatter_copy(sums_ref, land_ref, k, src_slot, dst_slot, c, send_sems, recv_sems, to):
    return pltpu.make_async_remote_copy(
        src_ref=sums_ref.at[src_slot], dst_ref=land_ref.at[dst_slot, _half_rows(land_ref, c)],
        send_sem=send_sems.at[k], recv_sem=recv_sems.at[k], device_id=to, device_id_type=MESH)


def _scatter_start(sums, name):
    ns, hr, cdim = sums.shape
    land = lax.empty((ns, 2 * hr, cdim), sums.dtype)

    def body(sums_ref, land_ref, send_sems, recv_sems, sums_thru, land_thru):
        x, y, c, chips = _mesh_pos()
        me = _chip_id(x, y)
        for k, chip in enumerate(chips):
            _scatter_copy(sums_ref, land_ref, k, _chip_id(*chip), me, c, send_sems, recv_sems, (*chip, c)).start()

    return pl.pallas_call(
        body, name=name,
        in_specs=[_HBM, _HBM], out_specs=[_SEM, _SEM, _HBM, _HBM],
        out_shape=[pltpu.SemaphoreType.DMA((3,)), pltpu.SemaphoreType.DMA((3,)),
                   pltpu.HBM(sums.shape, sums.dtype), pltpu.HBM(land.shape, land.dtype)],
        input_output_aliases={0: 2, 1: 3},
        compiler_params=pltpu.CompilerParams(has_side_effects=_EFFECT),
    )(pltpu.with_memory_space_constraint(sums, pltpu.HBM), pltpu.with_memory_space_constraint(land, pltpu.HBM))


def _scatter_wait(send_sems, recv_sems, sums, land, after, name):
    def body(sums_ref, land_ref, send_ref, recv_ref, after_ref, sums_out, land_out):
        x, y, c, chips = _mesh_pos()
        me = _chip_id(x, y)
        for k, chip in enumerate(chips):
            _scatter_copy(sums_ref, land_ref, k, _chip_id(*chip), me, c, send_ref, recv_ref, (*chip, c)).wait_send()
        for k, chip in enumerate(chips):
            _scatter_copy(sums_ref, land_ref, k, me, _chip_id(*chip), c, send_ref, recv_ref, (*chip, c)).wait_recv()

    return pl.pallas_call(
        body, name=name,
        in_specs=[_HBM, _HBM, _SEM, _SEM, _ANY], out_specs=[_HBM, _HBM],
        out_shape=[pltpu.HBM(sums.shape, sums.dtype), pltpu.HBM(land.shape, land.dtype)],
        input_output_aliases={0: 0, 1: 1},
        compiler_params=pltpu.CompilerParams(has_side_effects=_EFFECT),
    )(sums, land, send_sems, recv_sems, after)


def _complete_plan(n_weights):
    def plan(refs, send_sems, recv_sems):
        x, y, c, chips = _mesh_pos()
        me = _chip_id(x, y)
        sibling = (x, y, 1 - c)
        sends, recvs = [], []
        for w in range(n_weights):
            sums, land = refs[2 * w], refs[2 * w + 1]
            sends.append(_scatter_copy(sums, land, 4 * w + 3, me, me, c, send_sems, recv_sems, sibling))
            recvs.append(_scatter_copy(sums, land, 4 * w + 3, me, me, 1 - c, send_sems, recv_sems, sibling))
            for k, chip in enumerate(chips):
                slot = _chip_id(*chip)
                sends.append(_chip_copy(land, 4 * w + k, slot, _half_rows(land, c), send_sems, recv_sems, sibling))
                recvs.append(_chip_copy(land, 4 * w + k, slot, _half_rows(land, 1 - c), send_sems, recv_sems, sibling))
        return sends, recvs
    return plan


def _complete_chip_sums(sums, lands):
    n = len(sums)

    def body(*refs):
        sums_refs, outs = refs[:n], refs[2 * n:3 * n]
        send_sems, recv_sems = refs[3 * n:]
        x, y, c, chips = _mesh_pos()
        me = _chip_id(x, y)
        sibling = (x, y, 1 - c)
        slots = [_chip_id(*chip) for chip in chips]
        sent = []
        for w in range(n):
            out = outs[w]
            cp = _scatter_copy(sums_refs[w], out, 3, me, me, c, send_sems.at[w], recv_sems.at[w], sibling)
            cp.start()
            sent.append(cp)
            for k, slot in enumerate(slots):
                cp = _chip_copy(out, k, slot, _half_rows(out, c), send_sems.at[w], recv_sems.at[w], sibling)
                cp.start()
                sent.append(cp)
        for w in range(n):
            out = outs[w]
            _scatter_copy(sums_refs[w], out, 3, me, me, 1 - c, send_sems.at[w], recv_sems.at[w], sibling).wait_recv()
            for k, slot in enumerate(slots):
                _chip_copy(out, k, slot, _half_rows(out, 1 - c), send_sems.at[w], recv_sems.at[w], sibling).wait_recv()
        for cp in sent:
            cp.wait_send()

    return pl.pallas_call(
        body, name="complete_chip_sums",
        in_specs=[_HBM] * (2 * n), out_specs=[_HBM] * n,
        out_shape=[jax.ShapeDtypeStruct(b.shape, b.dtype) for b in lands],
        input_output_aliases={n + w: w for w in range(n)},
        scratch_shapes=[pltpu.SemaphoreType.DMA((n, 4)), pltpu.SemaphoreType.DMA((n, 4))],
    )(*sums, *lands)


SMALL_ROWS = 8


def _allreduce_small(gl2g, gl2b, gl1g, gl1b, g_ac, gcw, gsink, loss, after):
    d = gl2g.shape[1]
    hd = d // 2
    nq = gsink.shape[1]

    def body(a_ref, b_ref, c_ref, d_ref, e_ref, cw_ref, sk_ref, ls_ref, after_ref, out_ref, mine, gath, send_sems,
             recv_sems):
        x, y, c, _ = _mesh_pos()
        me = 4 * x + 2 * y + c
        mine[...] = jnp.zeros_like(mine)
        mine[0:1, :] = a_ref[...]
        mine[1:2, :] = b_ref[...]
        mine[2:3, :] = c_ref[...]
        mine[3:4, :] = d_ref[...]
        mine[4:5, :] = e_ref[...]
        mine[5:6, 0:hd] = cw_ref[0:1, :]
        mine[5:6, hd:d] = cw_ref[1:2, :]
        mine[6:7, 0:hd] = cw_ref[2:3, :]
        mine[6:7, hd:hd + nq] = sk_ref[...]
        mine[6:7, hd + 128:hd + 256] = ls_ref[...]
        gath[pl.ds(me, 1)] = mine[...][None]
        copies = []
        for r in range(1, 8):
            peer = ((1 - x) if r & 4 else x, (1 - y) if r & 2 else y, (1 - c) if r & 1 else c)
            cp = pltpu.make_async_remote_copy(
                src_ref=mine, dst_ref=gath.at[me], send_sem=send_sems.at[r - 1], recv_sem=recv_sems.at[r - 1],
                device_id=peer, device_id_type=MESH)
            cp.start()
            copies.append(cp)
        for r in range(1, 8):
            peer = ((1 - x) if r & 4 else x, (1 - y) if r & 2 else y, (1 - c) if r & 1 else c)
            peer_id = 4 * peer[0] + 2 * peer[1] + peer[2]
            pltpu.make_async_remote_copy(
                src_ref=mine, dst_ref=gath.at[peer_id], send_sem=send_sems.at[r - 1], recv_sem=recv_sems.at[r - 1],
                device_id=peer, device_id_type=MESH).wait_recv()
        for cp in copies:
            cp.wait_send()
        total = gath[0]
        for dev in range(1, 8):
            total = total + gath[dev]
        out_ref[...] = total

    return pl.pallas_call(
        body, name="allreduce_small",
        in_specs=[_VMEM] * 8 + [_ANY], out_specs=_VMEM,
        out_shape=jax.ShapeDtypeStruct((SMALL_ROWS, d), F32),
        scratch_shapes=[pltpu.VMEM((SMALL_ROWS, d), F32), pltpu.VMEM((8, SMALL_ROWS, d), F32),
                        pltpu.SemaphoreType.DMA((7,)), pltpu.SemaphoreType.DMA((7,))],
    )(gl2g, gl2b, gl1g, gl1b, g_ac, gcw, gsink, loss, after)


def _adamw(w, g, m, v):
    m = ADAM_B1 * m + (1.0 - ADAM_B1) * g
    v = ADAM_B2 * v + (1.0 - ADAM_B2) * (g * g)
    m_hat = m / (1.0 - ADAM_B1 ** ADAM_STEP)
    v_hat = v / (1.0 - ADAM_B2 ** ADAM_STEP)
    delta = -ADAM_LR * (m_hat / (jnp.sqrt(v_hat) + ADAM_EPS) + ADAM_WD * w)
    return delta, m, v


def _adamw_shard(w, m, v, land, own, pos_vec, name):
    _, r, c = w.shape
    hr = r // 2
    tr = _row_tile(hr, TR_ELT)
    nh = hr // tr

    def body(pos_ref, w_ref, m_ref, v_ref, l0, l1, l2, l3, own_ref, g_out, d_out, m_out, v_out):
        i = pl.program_id(0)
        mine = (i // nh) == pos_ref[1]
        own_blk = own_ref[...].astype(F32)
        g = None
        for s, l_ref in enumerate([l0, l1, l2, l3]):
            term = jnp.where(mine & (pos_ref[0] == s), own_blk, l_ref[...].astype(F32))
            g = term if g is None else g + term
        delta, nm, nv = _adamw(w_ref[...], g, m_ref[...], v_ref[...])
        g_out[...] = g
        d_out[...] = delta
        m_out[...] = nm
        v_out[...] = nv

    def land_spec(s):
        def index(i, pos_ref):
            skip = (pos_ref[0] == s) & ((i // nh) == pos_ref[1])
            return (s, jnp.where(skip, (i + nh) % (2 * nh), i), 0)
        return pl.BlockSpec((None, tr, c), index)

    blk = pl.BlockSpec((None, tr, c), lambda i, pos_ref: (0, i, 0))
    grid_spec = pltpu.PrefetchScalarGridSpec(
        num_scalar_prefetch=1, grid=(2 * nh,),
        in_specs=[blk, blk, blk] + [land_spec(s) for s in range(N_CHIPS)]
        + [pl.BlockSpec((None, tr, c), lambda i, pos_ref: (pos_ref[0], i % nh, 0))],
        out_specs=[blk] * 4)
    return pl.pallas_call(
        body, name=name, grid_spec=grid_spec,
        out_shape=[jax.ShapeDtypeStruct((1, r, c), F32)] * 4,
        compiler_params=_params(("parallel",)),
    )(pos_vec, w, m, v, land, land, land, land, own)


def _adamw_small(red, params):
    names = ["sinks", "g_attn", "g_conv", "ln1_g", "ln1_b", "ln2_g", "ln2_b", "conv_w"]
    d = red.shape[1]
    hd = d // 2
    flat = []
    for nme in names:
        flat.extend(params[nme])
    nq = params["sinks"][0].shape[1]
    cs = params["conv_w"][0].shape[2]

    def body(*refs):
        red_ref = refs[0]
        ins = refs[1:1 + 3 * len(names)]
        outs = refs[1 + 3 * len(names):]
        x, y, _, _ = _mesh_pos()
        me = _chip_id(x, y)

        def conv_tap(row, base):
            picked = red_ref[row:row + 1, base:base + cs]
            for s in range(1, N_CHIPS):
                picked = jnp.where(me == s, red_ref[row:row + 1, base + s * cs:base + (s + 1) * cs], picked)
            return picked

        grads = {
            "sinks": red_ref[6:7, hd:hd + nq],
            "g_attn": red_ref[4:5, 0:hd],
            "g_conv": red_ref[4:5, hd:d],
            "ln1_g": red_ref[2:3, :],
            "ln1_b": red_ref[3:4, :],
            "ln2_g": red_ref[0:1, :],
            "ln2_b": red_ref[1:2, :],
        }
        for i, nme in enumerate(names):
            w_ref, m_ref, v_ref = ins[3 * i:3 * i + 3]
            g_out, d_out, m_out, v_out = outs[4 * i:4 * i + 4]
            if nme == "conv_w":
                for tap, (row, base) in enumerate([(5, 0), (5, hd), (6, 0)]):
                    g = conv_tap(row, base)
                    delta, nm, nv = _adamw(w_ref[0, tap:tap + 1, :], g, m_ref[0, tap:tap + 1, :], v_ref[0, tap:tap + 1, :])
                    g_out[0, tap:tap + 1, :] = g
                    d_out[0, tap:tap + 1, :] = delta
                    m_out[0, tap:tap + 1, :] = nm
                    v_out[0, tap:tap + 1, :] = nv
            else:
                g = grads[nme]
                delta, nm, nv = _adamw(w_ref[...], g, m_ref[...], v_ref[...])
                g_out[...] = g
                d_out[...] = delta
                m_out[...] = nm
                v_out[...] = nv

    out_shape = []
    for nme in names:
        out_shape.extend([jax.ShapeDtypeStruct(params[nme][0].shape, F32)] * 4)
    outs = pl.pallas_call(
        body, name="adamw_small",
        in_specs=[_VMEM] * (1 + len(flat)), out_specs=[_VMEM] * len(out_shape),
        out_shape=out_shape,
    )(red, *flat)
    return {nme: tuple(outs[4 * i:4 * i + 4]) for i, nme in enumerate(names)}


def _rope_tables(pos_col):
    s = pos_col.shape[0]
    w = N_KV_HEADS * HEAD_DIM
    tb = min(512, s)
    inv_freq = (ROPE_THETA ** (-np.arange(0, ROT_DIM, 2, dtype=np.float32) / ROT_DIM)).astype(np.float32)

    def body(pos_ref, cos_ref, sin_ref):
        pos = pos_ref[...].astype(F32)
        lane = lax.broadcasted_iota(jnp.int32, (tb, w), 1) & (HEAD_DIM - 1)
        fidx = lane & (ROT_DIM // 2 - 1)
        inv = jnp.zeros((tb, w), F32)
        for k in range(ROT_DIM // 2):
            inv = jnp.where(fidx == k, float(inv_freq[k]), inv)
        ang = pos * inv
        rot = lane < ROT_DIM
        cos_ref[...] = jnp.where(rot, jnp.cos(ang), 1.0)
        sin_v = jnp.sin(ang)
        sin_ref[...] = jnp.where(lane < ROT_DIM // 2, -sin_v, jnp.where(rot, sin_v, 0.0))

    return pl.pallas_call(
        body, name="rope_tables", grid=(s // tb,),
        in_specs=[pl.BlockSpec((tb, 1), lambda i: (i, 0))],
        out_specs=[pl.BlockSpec((tb, w), lambda i: (i, 0))] * 2,
        out_shape=[jax.ShapeDtypeStruct((s, w), F32)] * 2,
        compiler_params=_params(("parallel",)),
    )(pos_col)


def _in_proj(x, w_in_g):
    _, s, d = x.shape
    ns, _, ncol = w_in_g.shape
    tm = min(2 * TM, s)

    def body(x_ref, w_ref, o_ref):
        o_ref[...] = _dot(x_ref[...].astype(BF16), w_ref[...])

    return pl.pallas_call(
        body, name="in_proj", grid=(s // tm, ns),
        in_specs=[pl.BlockSpec((None, tm, d), lambda i, j: (0, i, 0)),
                  pl.BlockSpec((None, d, ncol), lambda i, j: (j, 0, 0))],
        out_specs=pl.BlockSpec((tm, ncol), lambda i, j: (i, j)),
        out_shape=jax.ShapeDtypeStruct((s, ns * ncol), F32),
        compiler_params=_params(("parallel", "arbitrary")),
    )(x, w_in_g)


PAIR = 2 * HEAD_DIM
KEYS = 2 * WINDOW


def _pair_operand(t_all, h):
    col = (h // 2) * PAIR
    lane = lax.broadcasted_iota(jnp.int32, (KEYS, PAIR), 1)
    own_low = h % 2 == 0
    mine = jnp.where((lane < HEAD_DIM) if own_low else (lane >= HEAD_DIM), t_all[:, col:col + PAIR], 0.0)
    other = pltpu.roll(mine, HEAD_DIM, 1)
    low, high = (mine, other) if own_low else (other, mine)
    return jnp.concatenate([low, high], axis=0).astype(BF16)


def _pair_grad(acc, h):
    lane = lax.broadcasted_iota(jnp.int32, (KEYS, PAIR), 1)
    low = jnp.where(lane < HEAD_DIM, acc[:KEYS], 0.0)
    high = jnp.where(lane >= HEAD_DIM, acc[KEYS:], 0.0)
    if h % 2 == 0:
        return low + pltpu.roll(high, HEAD_DIM, 1)
    return high + pltpu.roll(low, HEAD_DIM, 1)


N_PAIRS = N_KV_HEADS * GROUP // 2


def _all_probs(q, kk2s, first, sinks_ref):
    assert ATTN_SCALE == 0.125
    q = q * ATTN_SCALE
    qps, scores = [], []
    for pair in range(N_PAIRS):
        qp = q[:, pair * PAIR:(pair + 1) * PAIR].astype(BF16)
        qps.append(qp)
        scores.append(_dot_nt(qp, kk2s[pair // (GROUP // 2)]))
    qi = lax.broadcasted_iota(jnp.int32, (WINDOW, 2 * KEYS), 0)
    kj = lax.broadcasted_iota(jnp.int32, (WINDOW, 2 * KEYS), 1) & (KEYS - 1)
    rel = qi + WINDOW - kj
    valid = (rel >= 0) & (rel < WINDOW) & jnp.logical_not(first & (kj < WINDOW))
    bias = jnp.where(valid, 0.0, NEG_BIG)
    s = (jnp.stack(scores, axis=0) + bias[None]).reshape(N_PAIRS * WINDOW, 2 * KEYS)
    probs, p_sinks = [], []
    for t in range(2):
        st = s[:, t * KEYS:(t + 1) * KEYS]
        sink = jnp.concatenate([jnp.broadcast_to(sinks_ref[0:1, 2 * pair + t:2 * pair + t + 1], (WINDOW, 1))
                                for pair in range(N_PAIRS)], axis=0)
        m = jnp.maximum(jnp.max(st, axis=1, keepdims=True), sink)
        e = jnp.exp(st - m)
        e_sink = jnp.exp(sink - m)
        inv_l = 1.0 / (jnp.sum(e, axis=1, keepdims=True) + e_sink)
        probs.append(e * inv_l)
        p_sinks.append(e_sink * inv_l)
    return qps, jnp.concatenate(probs, axis=1), p_sinks


def _roped_qkv(cur_ref, prev_ref, cos_ref, sin_ref, cosp_ref, sinp_ref, qw, kvw):
    cur = cur_ref[...]
    cos, sin = cos_ref[...], sin_ref[...]
    cos_q, sin_q = _tile_lanes(cos, GROUP), _tile_lanes(sin, GROUP)
    q = _rope(cur[:, :qw], cos_q, sin_q, 1.0)
    prev = prev_ref[...]
    k_all = jnp.concatenate([_rope(prev[:, :kvw], cosp_ref[...], sinp_ref[...], 1.0),
                             _rope(cur[:, qw:qw + kvw], cos, sin, 1.0)], axis=0)
    v_all = jnp.concatenate([prev[:, kvw:], cur[:, qw + kvw:]], axis=0)
    return q, k_all, v_all, cos_q, sin_q


def _attention_fwd(proj, cos_t, sin_t, sinks):
    s = proj.shape[0]
    qw = GROUP * N_KV_HEADS * HEAD_DIM
    kvw = N_KV_HEADS * HEAD_DIM
    nb = s // WINDOW

    def body(cur_ref, prev_ref, cos_ref, sin_ref, cosp_ref, sinp_ref, sinks_ref, o_ref):
        first = pl.program_id(0) == 0
        q, k_all, v_all, _, _ = _roped_qkv(cur_ref, prev_ref, cos_ref, sin_ref, cosp_ref, sinp_ref, qw, kvw)
        kk2s = [_pair_operand(k_all, h) for h in range(N_KV_HEADS)]
        vv2s = [_pair_operand(v_all, h) for h in range(N_KV_HEADS)]
        _, probs, _ = _all_probs(q, kk2s, first, sinks_ref)
        probs = probs.astype(BF16)
        outs = [_dot(probs[pair * WINDOW:(pair + 1) * WINDOW], vv2s[pair // (GROUP // 2)]) for pair in range(N_PAIRS)]
        o_ref[...] = jnp.concatenate(outs, axis=1)

    tbl = pl.BlockSpec((WINDOW, kvw), lambda n: (n, 0))
    tbl_prev = pl.BlockSpec((WINDOW, kvw), lambda n: (jnp.maximum(n - 1, 0), 0))
    return pl.pallas_call(
        body, name="attention_fwd", grid=(nb,),
        in_specs=[pl.BlockSpec((WINDOW, qw + 2 * kvw), lambda n: (n, 0)),
                  pl.BlockSpec((WINDOW, 2 * kvw), lambda n: (jnp.maximum(n - 1, 0), (qw // (2 * kvw)))),
                  tbl, tbl, tbl_prev, tbl_prev, _VMEM],
        out_specs=pl.BlockSpec((WINDOW, qw), lambda n: (n, 0)),
        out_shape=jax.ShapeDtypeStruct((s, qw), F32),
        compiler_params=_params(("parallel",)),
    )(proj, proj, cos_t, sin_t, cos_t, sin_t, sinks)


def _conv_taps(cw_ref):
    return [jnp.concatenate([cw_ref[s, k:k + 1, :] for s in range(N_CHIPS)], axis=1) for k in range(3)]


def _shift_down(z, halo, steps):
    rows = z.shape[0]
    row = lax.broadcasted_iota(jnp.int32, z.shape, 0)
    out = pltpu.roll(z, steps, 0)
    for r in range(steps):
        out = jnp.where(row == r, halo[8 - steps + r:8 - steps + r + 1, :], out)
    return out


def _shift_up(z, halo, steps):
    rows = z.shape[0]
    row = lax.broadcasted_iota(jnp.int32, z.shape, 0)
    out = pltpu.roll(z, rows - steps, 0)
    for r in range(steps):
        out = jnp.where(row == rows - steps + r, halo[r:r + 1, :], out)
    return out


def _split_cbu(lo, hi, cw):
    c_gate = lo[:, :cw]
    b_gate = jnp.concatenate([lo[:, cw:], hi[:, :2 * cw - lo.shape[1]]], axis=1)
    u = hi[:, 2 * cw - lo.shape[1]:]
    return c_gate, b_gate, u


def _conv_norm(proj, attn, cw_full, g_ac):
    s, in_w = proj.shape
    cw = attn.shape[1]
    blk_w = in_w // 3
    tb = min(TB_CONV, s)

    def body(lo_ref, hi_ref, lo_h_ref, hi_h_ref, attn_ref, cw_ref, g_ref, mixed_ref, ac_ref, rstd_ref):
        i = pl.program_id(0)
        c_gate, b_gate, u = _split_cbu(lo_ref[...], hi_ref[...], cw)
        c_h, _, u_h = _split_cbu(lo_h_ref[...], hi_h_ref[...], cw)
        z = c_gate * u
        z_h = jnp.where(i == 0, 0.0, c_h * u_h)
        w0, w1, w2 = _conv_taps(cw_ref)
        y = w0 * _shift_down(z, z_h, 2) + w1 * _shift_down(z, z_h, 1) + w2 * z
        conv = b_gate * y
        a = attn_ref[...]
        r_a = lax.rsqrt(jnp.mean(a * a, axis=-1, keepdims=True) + RMS_EPS)
        r_c = lax.rsqrt(jnp.mean(conv * conv, axis=-1, keepdims=True) + RMS_EPS)
        g = g_ref[...]
        mixed_ref[...] = jnp.concatenate([a * r_a * g[:, :cw], conv * r_c * g[:, cw:]], axis=1).astype(BF16)
        ac_ref[...] = jnp.concatenate([a, conv], axis=1)
        rstd_ref[0] = r_a
        rstd_ref[1] = r_c

    halo_idx = lambda i: jnp.maximum(i * (tb // 8) - 1, 0)
    return pl.pallas_call(
        body, name="conv_norm", grid=(s // tb,),
        in_specs=[pl.BlockSpec((tb, blk_w), lambda i: (i, 1)),
                  pl.BlockSpec((tb, blk_w), lambda i: (i, 2)),
                  pl.BlockSpec((8, blk_w), lambda i: (halo_idx(i), 1)),
                  pl.BlockSpec((8, blk_w), lambda i: (halo_idx(i), 2)),
                  pl.BlockSpec((tb, cw), lambda i: (i, 0)),
                  _VMEM, _VMEM],
        out_specs=[pl.BlockSpec((tb, 2 * cw), lambda i: (i, 0)),
                   pl.BlockSpec((tb, 2 * cw), lambda i: (i, 0)),
                   pl.BlockSpec((2, tb, 1), lambda i: (0, i, 0))],
        out_shape=[jax.ShapeDtypeStruct((s, 2 * cw), BF16), jax.ShapeDtypeStruct((s, 2 * cw), F32),
                   jax.ShapeDtypeStruct((2, s, 1), F32)],
        compiler_params=_params(("parallel",)),
    )(proj, proj, proj, proj, attn, cw_full, g_ac)


def _out_proj_ln(mixed, w_out_g, x, ln_g, ln_b):
    s, d = mixed.shape
    tm = min(TM, s)
    tk = d
    nk = d // tk

    def body(a_ref, w_ref, x_ref, g_ref, b_ref, xhat_ref, h_ref, rstd_ref, acc):
        k = pl.program_id(1)
        _accumulate(acc, lambda: _dot(a_ref[...], w_ref[...]), k, nk)

        @pl.when(k == nk - 1)
        def _():
            def rows_fn(rows):
                xhat, rstd = _ln_fwd(ALPHA * x_ref[rows, :] + acc[rows, :])
                xhat_ref[rows, :] = xhat
                h_ref[rows, :] = (xhat * g_ref[...] + b_ref[...]).astype(BF16)
                rstd_ref[rows, :] = rstd

            _for_row_chunks(tm, rows_fn)

    row = pl.BlockSpec((tm, d), lambda i, k: (i, 0))
    return pl.pallas_call(
        body, name="out_proj_ln", grid=(s // tm, nk),
        in_specs=[pl.BlockSpec((tm, tk), lambda i, k: (i, k)),
                  pl.BlockSpec((tk, d), lambda i, k: (k, 0)),
                  pl.BlockSpec((None, tm, d), lambda i, k: (0, i, 0)),
                  _VMEM, _VMEM],
        out_specs=[row, row, pl.BlockSpec((tm, 1), lambda i, k: (i, 0))],
        out_shape=[jax.ShapeDtypeStruct((s, d), F32), jax.ShapeDtypeStruct((s, d), BF16),
                   jax.ShapeDtypeStruct((s, 1), F32)],
        scratch_shapes=[pltpu.VMEM((tm, d), F32)],
        compiler_params=_params(("parallel", "arbitrary")),
    )(mixed, w_out_g, x, ln_g, ln_b)


def _gate_up(h1, w_gate_g, w_up_g):
    s, d = h1.shape
    ns, _, fs = w_gate_g.shape
    tm = min(TM, s)
    tk = d
    nk = d // tk

    def body(h_ref, wg_ref, wu_ref, act_ref, g_ref, u_ref, acc_g, acc_u):
        k = pl.program_id(2)
        _accumulate(acc_g, lambda: _dot(h_ref[...], wg_ref[...]), k, nk)
        _accumulate(acc_u, lambda: _dot(h_ref[...], wu_ref[...]), k, nk)

        @pl.when(k == nk - 1)
        def _():
            def rows_fn(rows):
                g, u = acc_g[rows, :], acc_u[rows, :]
                act_ref[rows, :] = (g * _sigmoid(g) * u).astype(BF16)
                g_ref[rows, :] = g.astype(BF16)
                u_ref[rows, :] = u.astype(BF16)

            _for_row_chunks(tm, rows_fn)

    wspec = pl.BlockSpec((None, tk, fs), lambda i, j, k: (j, k, 0))
    ospec = pl.BlockSpec((tm, fs), lambda i, j, k: (i, j))
    return pl.pallas_call(
        body, name="gate_up", grid=(s // tm, ns, nk),
        in_specs=[pl.BlockSpec((tm, tk), lambda i, j, k: (i, k)), wspec, wspec],
        out_specs=[ospec] * 3,
        out_shape=[jax.ShapeDtypeStruct((s, ns * fs), BF16)] * 3,
        scratch_shapes=[pltpu.VMEM((tm, fs), F32)] * 2,
        compiler_params=_params(("parallel", "arbitrary", "arbitrary")),
    )(h1, w_gate_g, w_up_g)


def _down_ln_loss(act, w_down_g, xhat1, ln1_g, ln1_b, ln2_g, ln2_b, target):
    s, f = act.shape
    d = xhat1.shape[1]
    tm = min(TM, s)
    tk = f // N_CHIPS
    nk = f // tk

    def body(a_ref, w_ref, xh_ref, g1_ref, b1_ref, g2_ref, b2_ref, t_ref, dpre_ref, loss_ref, gg_ref, gb_ref, acc):
        i, k = pl.program_id(0), pl.program_id(1)
        _accumulate(acc, lambda: _dot(a_ref[...], w_ref[...]), k, nk)

        @pl.when(k == nk - 1)
        def _():
            @pl.when(i == 0)
            def _():
                loss_ref[...] = jnp.zeros_like(loss_ref)
                gg_ref[...] = jnp.zeros_like(gg_ref)
                gb_ref[...] = jnp.zeros_like(gb_ref)

            def rows_fn(rows):
                h1 = xh_ref[rows, :] * g1_ref[...] + b1_ref[...]
                xhat, rstd = _ln_fwd(ALPHA * h1 + acc[rows, :])
                g2 = g2_ref[...]
                diff = xhat * g2 + b2_ref[...] - t_ref[rows, :]
                dy = diff * (1.0 / d)
                dpre_ref[rows, :] = _ln_bwd(dy, xhat, rstd, g2)
                sq = jnp.sum(jnp.sum(diff * diff, axis=1, keepdims=True), axis=0, keepdims=True)
                loss_ref[...] += jnp.broadcast_to(sq * (0.5 / d), (1, 128))
                gg_ref[...] += jnp.sum(dy * xhat, axis=0, keepdims=True)
                gb_ref[...] += jnp.sum(dy, axis=0, keepdims=True)

            _for_row_chunks(tm, rows_fn)

    row = pl.BlockSpec((tm, d), lambda i, k: (i, 0))
    vec = pl.BlockSpec((1, d), lambda i, k: (0, 0))
    return pl.pallas_call(
        body, name="down_ln_loss", grid=(s // tm, nk),
        in_specs=[pl.BlockSpec((tm, tk), lambda i, k: (i, k)),
                  pl.BlockSpec((tk, d), lambda i, k: (k, 0)),
                  row, _VMEM, _VMEM, _VMEM, _VMEM,
                  pl.BlockSpec((None, tm, d), lambda i, k: (0, i, 0))],
        out_specs=[row, pl.BlockSpec((1, 128), lambda i, k: (0, 0)), vec, vec],
        out_shape=[jax.ShapeDtypeStruct((s, d), F32), jax.ShapeDtypeStruct((1, 128), F32),
                   jax.ShapeDtypeStruct((1, d), F32), jax.ShapeDtypeStruct((1, d), F32)],
        scratch_shapes=[pltpu.VMEM((tm, d), F32)],
        compiler_params=_params(("arbitrary", "arbitrary")),
    )(act, w_down_g, xhat1, ln1_g, ln1_b, ln2_g, ln2_b, target)


def _dact_silu_bwd(dpre2, w_down_g, gate, up):
    s, d = dpre2.shape
    f = gate.shape[1]
    fs = f // N_CHIPS
    tm = min(TM, s)

    def body(dp_ref, w_ref, g_ref, u_ref, dg_ref, du_ref):
        d_act = _dot_nt(dp_ref[...].astype(BF16), w_ref[...])
        g = g_ref[...].astype(F32)
        u = u_ref[...].astype(F32)
        sg = _sigmoid(g)
        dg_ref[...] = (d_act * u * (sg * (1.0 + g * (1.0 - sg)))).astype(BF16)
        du_ref[...] = (d_act * (g * sg)).astype(BF16)

    blk = pl.BlockSpec((tm, fs), lambda i, j: (i, j))
    return pl.pallas_call(
        body, name="dact_silu_bwd", grid=(s // tm, N_CHIPS),
        in_specs=[pl.BlockSpec((tm, d), lambda i, j: (i, 0)),
                  pl.BlockSpec((fs, d), lambda i, j: (j, 0)), blk, blk],
        out_specs=[blk, blk],
        out_shape=[jax.ShapeDtypeStruct((s, f), BF16)] * 2,
        compiler_params=_params(("parallel", "arbitrary")),
    )(dpre2, w_down_g, gate, up)


def _grad_rows(a, b, after, name, row_blocks=1):
    s, m = a.shape
    n = b.shape[1]
    ms = m // N_CHIPS
    tmw = ms // row_blocks
    tk = min(TK_TOK, s)
    nk = s // tk

    def body(a_ref, b_ref, after_ref, o_ref, acc):
        k = pl.program_id(2)
        _accumulate(acc, lambda: _dot_tn(a_ref[...].astype(BF16), b_ref[...].astype(BF16)), k, nk)

        @pl.when(k == nk - 1)
        def _():
            o_ref[...] = acc[...].astype(BF16)

    return pl.pallas_call(
        body, name=name, grid=(N_CHIPS, row_blocks, nk),
        in_specs=[pl.BlockSpec((tk, tmw), lambda j, r, k: (k, j * row_blocks + r)),
                  pl.BlockSpec((tk, n), lambda j, r, k: (k, 0)), _ANY],
        out_specs=pl.BlockSpec((None, tmw, n), lambda j, r, k: (j, r, 0)),
        out_shape=jax.ShapeDtypeStruct((N_CHIPS, ms, n), BF16),
        scratch_shapes=[pltpu.VMEM((tmw, n), F32)],
        compiler_params=_params(("parallel", "parallel", "arbitrary")),
    )(a, b, after)


def _grad_cols(a, bs, after, name, a_3d=False, row_blocks=2):
    s, m = a.shape[-2:]
    n = bs[0].shape[1]
    ns = n // N_CHIPS
    nb = len(bs)
    tmw = m // row_blocks
    tk = min(TK_TOK, s)
    nk = s // tk

    def body(*refs):
        a_ref, b_refs, o_refs, accs = refs[0], refs[1:1 + nb], refs[2 + nb:2 + 2 * nb], refs[2 + 2 * nb:]
        k = pl.program_id(2)
        for b_ref, acc in zip(b_refs, accs):
            _accumulate(acc, lambda b_ref=b_ref: _dot_tn(a_ref[...].astype(BF16), b_ref[...].astype(BF16)), k, nk)

        @pl.when(k == nk - 1)
        def _():
            for o_ref, acc in zip(o_refs, accs):
                o_ref[...] = acc[...].astype(BF16)

    if a_3d:
        a_spec = pl.BlockSpec((None, tk, tmw), lambda j, r, k: (0, k, r))
    else:
        a_spec = pl.BlockSpec((tk, tmw), lambda j, r, k: (k, r))
    return pl.pallas_call(
        body, name=name, grid=(N_CHIPS, row_blocks, nk),
        in_specs=[a_spec] + [pl.BlockSpec((tk, ns), lambda j, r, k: (k, j))] * nb + [_ANY],
        out_specs=[pl.BlockSpec((None, tmw, ns), lambda j, r, k: (j, r, 0))] * nb,
        out_shape=[jax.ShapeDtypeStruct((N_CHIPS, m, ns), BF16)] * nb,
        scratch_shapes=[pltpu.VMEM((tmw, ns), F32)] * nb,
        compiler_params=_params(("parallel", "parallel", "arbitrary")),
    )(a, *bs, after)


def _dh1_ln_bwd(d_gate, d_up, w_gate_g, w_up_g, dpre2, xhat1, rstd1, ln1_g, after):
    s, f = d_gate.shape
    d = dpre2.shape[1]
    hd = d // 2
    fs = f // N_CHIPS
    tm = min(TM, s)

    def body(dg_ref, du_ref, wg_ref, wu_ref, dp2_ref, xh_ref, rs_ref, g_ref, after_ref, dpre_ref, gg_ref, gb_ref,
             acc_lo, acc_hi):
        i, j, half = pl.program_id(0), pl.program_id(1), pl.program_id(2)
        def product():
            return _dot_nt(dg_ref[...], wg_ref[...]) + _dot_nt(du_ref[...], wu_ref[...])

        @pl.when(half == 0)
        def _():
            _accumulate(acc_lo, product, j, N_CHIPS)

        @pl.when(half == 1)
        def _():
            _accumulate(acc_hi, product, j, N_CHIPS)

        @pl.when((j == N_CHIPS - 1) & (half == 1))
        def _():
            @pl.when(i == 0)
            def _():
                gg_ref[...] = jnp.zeros_like(gg_ref)
                gb_ref[...] = jnp.zeros_like(gb_ref)

            def rows_fn(rows):
                dh = jnp.concatenate([acc_lo[rows, :], acc_hi[rows, :]], axis=1) + ALPHA * dp2_ref[rows, :]
                xhat = xh_ref[rows, :]
                dpre_ref[rows, :] = _ln_bwd(dh, xhat, rs_ref[rows, :], g_ref[...])
                gg_ref[...] += jnp.sum(dh * xhat, axis=0, keepdims=True)
                gb_ref[...] += jnp.sum(dh, axis=0, keepdims=True)

            _for_row_chunks(tm, rows_fn)

    row = pl.BlockSpec((tm, d), lambda i, j, h: (i, 0))
    vec = pl.BlockSpec((1, d), lambda i, j, h: (0, 0))
    act_blk = pl.BlockSpec((tm, fs), lambda i, j, h: (i, j))
    w_blk = pl.BlockSpec((None, hd, fs), lambda i, j, h: (j, h, 0))
    return pl.pallas_call(
        body, name="dh1_ln_bwd", grid=(s // tm, N_CHIPS, 2),
        in_specs=[act_blk, act_blk, w_blk, w_blk, row, row, pl.BlockSpec((tm, 1), lambda i, j, h: (i, 0)), _VMEM,
                  _ANY],
        out_specs=[row, vec, vec],
        out_shape=[jax.ShapeDtypeStruct((s, d), F32), jax.ShapeDtypeStruct((1, d), F32),
                   jax.ShapeDtypeStruct((1, d), F32)],
        scratch_shapes=[pltpu.VMEM((tm, hd), F32)] * 2,
        compiler_params=_params(("arbitrary", "arbitrary", "arbitrary")),
    )(d_gate, d_up, w_gate_g, w_up_g, dpre2, xhat1, rstd1, ln1_g, after)


def _dmixed_rms_bwd(dpre1, w_out_g, ac, rstd, g_ac):
    s, d = dpre1.shape
    hd = d // 2
    tm = min(TM, s)

    def body(dp_ref, w_ref, ac_ref, rs_ref, g_ref, dac_ref, gg_ref):
        i = pl.program_id(1)
        dm = _dot_nt(dp_ref[...].astype(BF16), w_ref[...])
        pre = ac_ref[...]
        r = rs_ref[...]
        gdm = dm * g_ref[...]
        dac_ref[...] = r * gdm - pre * (r * r * r) * jnp.mean(gdm * pre, axis=-1, keepdims=True)
        gg = jnp.sum(dm * pre * r, axis=0, keepdims=True)

        @pl.when(i == 0)
        def _():
            gg_ref[...] = gg

        @pl.when(i > 0)
        def _():
            gg_ref[...] += gg

    return pl.pallas_call(
        body, name="dmixed_rms_bwd", grid=(2, s // tm),
        in_specs=[pl.BlockSpec((tm, d), lambda h, i: (i, 0)),
                  pl.BlockSpec((hd, d), lambda h, i: (h, 0)),
                  pl.BlockSpec((tm, hd), lambda h, i: (i, h)),
                  pl.BlockSpec((None, tm, 1), lambda h, i: (h, i, 0)),
                  pl.BlockSpec((1, hd), lambda h, i: (0, h))],
        out_specs=[pl.BlockSpec((tm, hd), lambda h, i: (i, h)),
                   pl.BlockSpec((1, hd), lambda h, i: (0, h))],
        out_shape=[jax.ShapeDtypeStruct((s, d), F32), jax.ShapeDtypeStruct((1, d), F32)],
        compiler_params=_params(("arbitrary", "arbitrary")),
    )(dpre1, w_out_g, ac, rstd, g_ac)


def _attention_bwd(proj, d_ac, cos_t, sin_t, sinks, after):
    s = proj.shape[0]
    qw = GROUP * N_KV_HEADS * HEAD_DIM
    kvw = N_KV_HEADS * HEAD_DIM
    nb = s // WINDOW
    nq = GROUP * N_KV_HEADS

    def body(cur_ref, prev_ref, do_ref, cos_ref, sin_ref, cosp_ref, sinp_ref, sinks_ref, after_ref,
             dq_ref, dcur_ref, dprev_ref, dsink_ref):
        n = pl.program_id(0)
        first = n == 0
        q, k_all, v_all, cos_q, sin_q = _roped_qkv(cur_ref, prev_ref, cos_ref, sin_ref, cosp_ref, sinp_ref, qw, kvw)
        kk2s = [_pair_operand(k_all, h) for h in range(N_KV_HEADS)]
        vv2s = [_pair_operand(v_all, h) for h in range(N_KV_HEADS)]
        qps, probs, p_sinks = _all_probs(q, kk2s, first, sinks_ref)
        dops = [do_ref[:, pair * PAIR:(pair + 1) * PAIR].astype(BF16) for pair in range(N_PAIRS)]
        d_probs = jnp.concatenate([_dot_nt(dops[pair], vv2s[pair // (GROUP // 2)]) for pair in range(N_PAIRS)], axis=0)
        d_s, ds_sinks = [], []
        for t in range(2):
            cols = slice(t * KEYS, (t + 1) * KEYS)
            delta = jnp.sum(probs[:, cols] * d_probs[:, cols], axis=1, keepdims=True)
            d_s.append(probs[:, cols] * (d_probs[:, cols] - delta))
            ds_sinks.append(-p_sinks[t] * delta)
        d_s = jnp.concatenate(d_s, axis=1).astype(BF16)
        probs = probs.astype(BF16)
        dq_parts, dk_tiles, dv_tiles, dsink_parts = [], [], [], []
        for h in range(N_KV_HEADS):
            dkk2, dvv2 = None, None
            for p in range(GROUP // 2):
                pair = (GROUP // 2) * h + p
                rows = slice(pair * WINDOW, (pair + 1) * WINDOW)
                dq_parts.append(_dot(d_s[rows], kk2s[h]) * ATTN_SCALE)
                dk_term = _dot_tn(d_s[rows], qps[pair])
                dv_term = _dot_tn(probs[rows], dops[pair])
                dkk2 = dk_term if dkk2 is None else dkk2 + dk_term
                dvv2 = dv_term if dvv2 is None else dvv2 + dv_term
                dsink_parts.extend([jnp.sum(ds_sinks[t][rows], axis=0, keepdims=True) for t in range(2)])
            dk_tiles.append(_pair_grad(dkk2, h))
            dv_tiles.append(_pair_grad(dvv2, h))
        dq_ref[...] = _rope(jnp.concatenate(dq_parts, axis=1), cos_q, sin_q, -1.0)
        dk = jnp.concatenate([dk_tiles[0] + dk_tiles[1], dk_tiles[2] + dk_tiles[3]], axis=1)
        dv = jnp.concatenate([dv_tiles[0] + dv_tiles[1], dv_tiles[2] + dv_tiles[3]], axis=1)
        dprev_ref[...] = jnp.concatenate([dk[:WINDOW], dv[:WINDOW]], axis=1)
        dcur_ref[...] = jnp.concatenate([dk[WINDOW:], dv[WINDOW:]], axis=1)
        dsink = jnp.concatenate(dsink_parts, axis=1)

        @pl.when(first)
        def _():
            dsink_ref[...] = dsink

        @pl.when(n > 0)
        def _():
            dsink_ref[...] += dsink

    tbl = pl.BlockSpec((WINDOW, kvw), lambda n: (n, 0))
    tbl_prev = pl.BlockSpec((WINDOW, kvw), lambda n: (jnp.maximum(n - 1, 0), 0))
    kv_blk = pl.BlockSpec((WINDOW, 2 * kvw), lambda n: (n, 0))
    return pl.pallas_call(
        body, name="attention_bwd", grid=(nb,),
        in_specs=[pl.BlockSpec((WINDOW, qw + 2 * kvw), lambda n: (n, 0)),
                  pl.BlockSpec((WINDOW, 2 * kvw), lambda n: (jnp.maximum(n - 1, 0), (qw // (2 * kvw)))),
                  pl.BlockSpec((WINDOW, qw), lambda n: (n, 0)),
                  tbl, tbl, tbl_prev, tbl_prev, _VMEM, _ANY],
        out_specs=[pl.BlockSpec((WINDOW, qw), lambda n: (n, 0)), kv_blk, kv_blk,
                   pl.BlockSpec((1, nq), lambda n: (0, 0))],
        out_shape=[jax.ShapeDtypeStruct((s, qw), F32), jax.ShapeDtypeStruct((s, 2 * kvw), F32),
                   jax.ShapeDtypeStruct((s, 2 * kvw), F32), jax.ShapeDtypeStruct((1, nq), F32)],
        compiler_params=_params(("arbitrary",)),
    )(proj, proj, d_ac, cos_t, sin_t, cos_t, sin_t, sinks, after)


def _dproj_assemble(proj, d_ac, dq, dkv_cur, dkv_prev, cos_t, sin_t, cw_full):
    s, in_w = proj.shape
    cw = dq.shape[1]
    kvw = N_KV_HEADS * HEAD_DIM
    blk_w = in_w // 3
    tb = WINDOW
    nb = s // tb

    def body(lo_ref, hi_ref, lo_p_ref, hi_p_ref, lo_n_ref, hi_n_ref, dconv_ref, dconv_n_ref,
             dq_ref, dcur_ref, dprev_n_ref, cos_ref, sin_ref, cw_ref, dproj_ref, gcw_ref):
        i = pl.program_id(0)
        last = i == nb - 1
        c_gate, b_gate, u = _split_cbu(lo_ref[...], hi_ref[...], cw)
        c_p, _, u_p = _split_cbu(lo_p_ref[...], hi_p_ref[...], cw)
        _, b_n, _ = _split_cbu(lo_n_ref[...], hi_n_ref[...], cw)
        z = c_gate * u
        z_p = jnp.where(i == 0, 0.0, c_p * u_p)
        z1 = _shift_down(z, z_p, 1)
        z2 = _shift_down(z, z_p, 2)
        w0, w1, w2 = _conv_taps(cw_ref)
        y = w0 * z2 + w1 * z1 + w2 * z
        d_conv = dconv_ref[...]
        d_b = d_conv * y
        d_y = d_conv * b_gate
        d_y_n = jnp.where(last, 0.0, dconv_n_ref[...] * b_n)
        d_z = w2 * d_y + w1 * _shift_up(d_y, d_y_n, 1) + w0 * _shift_up(d_y, d_y_n, 2)
        d_c = d_z * u
        d_u = d_z * c_gate
        gcw = jnp.concatenate([jnp.sum(d_y * z2, axis=0, keepdims=True), jnp.sum(d_y * z1, axis=0, keepdims=True),
                               jnp.sum(d_y * z, axis=0, keepdims=True)], axis=0)

        @pl.when(i == 0)
        def _():
            gcw_ref[...] = gcw

        @pl.when(i > 0)
        def _():
            gcw_ref[...] += gcw

        dkv = dcur_ref[...] + jnp.where(last, 0.0, dprev_n_ref[...])
        dk = _rope(dkv[:, :kvw], cos_ref[...], sin_ref[...], -1.0)
        dproj_ref[...] = jnp.concatenate([dq_ref[...], dk, dkv[:, kvw:], d_c, d_b, d_u], axis=1).astype(BF16)

    prev8 = lambda i: jnp.maximum(i * (tb // 8) - 1, 0)
    next8 = lambda i: jnp.minimum((i + 1) * (tb // 8), s // 8 - 1)
    nxt = lambda i: jnp.minimum(i + 1, nb - 1)
    return pl.pallas_call(
        body, name="dproj_assemble", grid=(nb,),
        in_specs=[pl.BlockSpec((tb, blk_w), lambda i: (i, 1)),
                  pl.BlockSpec((tb, blk_w), lambda i: (i, 2)),
                  pl.BlockSpec((8, blk_w), lambda i: (prev8(i), 1)),
                  pl.BlockSpec((8, blk_w), lambda i: (prev8(i), 2)),
                  pl.BlockSpec((8, blk_w), lambda i: (next8(i), 1)),
                  pl.BlockSpec((8, blk_w), lambda i: (next8(i), 2)),
                  pl.BlockSpec((tb, cw), lambda i: (i, 1)),
                  pl.BlockSpec((8, cw), lambda i: (next8(i), 1)),
                  pl.BlockSpec((tb, cw), lambda i: (i, 0)),
                  pl.BlockSpec((tb, 2 * kvw), lambda i: (i, 0)),
                  pl.BlockSpec((tb, 2 * kvw), lambda i: (nxt(i), 0)),
                  pl.BlockSpec((tb, kvw), lambda i: (i, 0)),
                  pl.BlockSpec((tb, kvw), lambda i: (i, 0)),
                  _VMEM],
        out_specs=[pl.BlockSpec((tb, in_w), lambda i: (i, 0)),
                   pl.BlockSpec((3, cw), lambda i: (0, 0))],
        out_shape=[jax.ShapeDtypeStruct((s, in_w), BF16), jax.ShapeDtypeStruct((3, cw), F32)],
        compiler_params=_params(("arbitrary",)),
    )(proj, proj, proj, proj, proj, proj, d_ac, d_ac, dq, dkv_cur, dkv_prev, cos_t, sin_t, cw_full)


def _dx(d_proj, w_in_g, dpre1, after):
    s, in_w = d_proj.shape
    ns, d, ncol = w_in_g.shape
    tm = min(TM, s)

    def body(dp_ref, w_ref, r_ref, after_ref, o_ref, acc):
        j = pl.program_id(1)
        _accumulate(acc, lambda: _dot_nt(dp_ref[...], w_ref[...]), j, ns)

        @pl.when(j == ns - 1)
        def _():
            o_ref[...] = acc[...] + ALPHA * r_ref[...]

    return pl.pallas_call(
        body, name="dx", grid=(s // tm, ns),
        in_specs=[pl.BlockSpec((tm, ncol), lambda i, j: (i, j)),
                  pl.BlockSpec((None, d, ncol), lambda i, j: (j, 0, 0)),
                  pl.BlockSpec((tm, d), lambda i, j: (i, 0)), _ANY],
        out_specs=pl.BlockSpec((None, tm, d), lambda i, j: (0, i, 0)),
        out_shape=jax.ShapeDtypeStruct((1, s, d), F32),
        scratch_shapes=[pltpu.VMEM((tm, d), F32)],
        compiler_params=_params(("parallel", "arbitrary")),
    )(d_proj, w_in_g, dpre1, after)


def kernel(x, positions, w_in, conv_w, sinks, g_attn, g_conv, w_out, ln1_g, ln1_b, w_gate, w_up, w_down, ln2_g, ln2_b, loss_target, m_w_in, m_conv_w, m_sinks, m_g_attn, m_g_conv, m_w_out, m_ln1_g, m_ln1_b, m_w_gate, m_w_up, m_w_down, m_ln2_g, m_ln2_b, v_w_in, v_conv_w, v_sinks, v_g_attn, v_g_conv, v_w_out, v_ln1_g, v_ln1_b, v_w_gate, v_w_up, v_w_down, v_ln2_g, v_ln2_b):
    s = x.shape[1]
    d = x.shape[2]

    chip_vec = _chip_id(lax.axis_index("x"), lax.axis_index("y")).astype(jnp.int32).reshape(1)
    wnames = ["w_in", "w_out", "w_gate", "w_up", "w_down"]
    cw_full = _allgather_conv_w(conv_w)
    buf_in = _cast_weight(w_in, chip_vec, cw_full, "cast_w_in")
    flight_in, token_in = _gather_start([buf_in], cw_full, "gather_start_w_in")
    bufs = [_cast_weight(w, chip_vec, token_in, "cast_" + nme)
            for w, nme in zip([w_out, w_gate, w_up, w_down], wnames[1:])]
    flights_rest, token = _gather_start(bufs, token_in, "gather_start_rest")
    flights = flight_in + flights_rest

    def gathered(i, after):
        send_sems, recv_sems, buf = flights[i]
        buf = _gather_wait(send_sems, recv_sems, buf, after, "gather_wait_" + wnames[i])
        return _sibling_fill(buf, "sibling_fill_" + wnames[i])

    g_ac = jnp.concatenate([g_attn, g_conv], axis=1)

    cos_t, sin_t = _rope_tables(positions.reshape(s, 1) + token[0:1, 0:1].astype(jnp.int32))
    w_in_g = gathered(0, cos_t)
    proj = _in_proj(x, w_in_g)
    send_sems, recv_sems, buf_out = flights[1]
    buf_out = _gather_wait(send_sems, recv_sems, buf_out, proj, "gather_wait_w_out")
    fill_out = _flight_start("fill_start_w_out", [buf_out], _fill_plan(1), 3, chip_vec)
    attn = _attention_fwd(_after(proj, fill_out[2][0]), cos_t, sin_t, sinks)
    mixed, ac, rstd_ac = _conv_norm(proj, attn, cw_full, g_ac)
    send_sems, recv_sems, buf_gate = flights[2]
    buf_gate = _gather_wait(send_sems, recv_sems, buf_gate, mixed, "gather_wait_w_gate")
    send_sems, recv_sems, buf_up = flights[3]
    buf_up = _gather_wait(send_sems, recv_sems, buf_up, buf_gate, "gather_wait_w_up")
    fill_gu = _flight_start("fill_start_w_gate_up", [buf_gate, buf_up], _fill_plan(2), 6, chip_vec)
    (w_out_g,) = _flight_wait("fill_wait_w_out", fill_out, _fill_plan(1), fill_gu[2][0])
    w_out_full = w_out_g.reshape(d, d)
    xhat1, h1, rstd1 = _out_proj_ln(mixed, w_out_full, x, ln1_g, ln1_b)
    w_gate_g, w_up_g = _flight_wait("fill_wait_w_gate_up", fill_gu, _fill_plan(2), h1)
    act, gate, up = _gate_up(h1, w_gate_g, w_up_g)
    w_down_full = gathered(4, act).reshape(-1, d)
    dpre2, loss_part, g_ln2_g, g_ln2_b = _down_ln_loss(act, w_down_full, xhat1, ln1_g, ln1_b, ln2_g, ln2_b, loss_target)

    cvec = lax.axis_index("c").astype(jnp.int32).reshape(1)

    def exchange_begin(parts, nme):
        bufs = []
        for part in parts:
            ns, r, cdim = part.shape
            bufs.extend([part, lax.empty((ns, r // 2, cdim), part.dtype)])
        return _flight_start("exchange_start_" + nme, bufs, _exchange_plan(len(parts)), len(parts), cvec)

    def exchange_end(flight, n_parts, after, nme):
        bufs = _flight_wait("exchange_wait_" + nme, flight, _exchange_plan(n_parts), after)
        return [(bufs[2 * w], bufs[2 * w + 1]) for w in range(n_parts)]

    def scatter_begin(part, got, nme):
        return _scatter_start(_add_halves(part, got, cvec, "add_halves_" + nme), "scatter_start_" + nme)

    d_gate, d_up = _dact_silu_bwd(dpre2, w_down_full, gate, up)
    p_down = _grad_rows(act, dpre2, d_gate, "grad_w_down")
    x_down = exchange_begin([p_down], "w_down")
    p_gate, p_up = _grad_cols(h1, [d_gate, d_up], x_down[2][0], "grad_w_gate_up")
    ((p_down, got),) = exchange_end(x_down, 1, p_gate, "w_down")
    f_down = scatter_begin(p_down, got, "w_down")
    x_gu = exchange_begin([_after(p_gate, f_down[2]), p_up], "w_gate_up")
    dpre1, g_ln1_g, g_ln1_b = _dh1_ln_bwd(d_gate, d_up, w_gate_g, w_up_g, dpre2, xhat1, rstd1, ln1_g, x_gu[2][0])
    (p_gate, got_gate), (p_up, got_up) = exchange_end(x_gu, 2, dpre1, "w_gate_up")
    f_gate = scatter_begin(p_gate, got_gate, "w_gate")
    f_up = scatter_begin(_after(p_up, f_gate[2]), got_up, "w_up")
    d_ac, g_g_ac = _dmixed_rms_bwd(_after(dpre1, f_up[2]), w_out_full, ac, rstd_ac, g_ac)
    p_out = _grad_rows(mixed, dpre1, d_ac, "grad_w_out")
    x_out = exchange_begin([p_out], "w_out")
    dq, dkv_cur, dkv_prev, g_sinks = _attention_bwd(proj, d_ac, cos_t, sin_t, sinks, x_out[2][0])
    ((p_out, got),) = exchange_end(x_out, 1, dq, "w_out")
    f_out = scatter_begin(p_out, got, "w_out")
    d_proj, g_conv_w = _dproj_assemble(proj, _after(d_ac, f_out[2]), dq, dkv_cur, dkv_prev, cos_t, sin_t, cw_full)
    (p_in,) = _grad_cols(x, [d_proj], d_proj, "grad_w_in", a_3d=True)
    x_in = exchange_begin([p_in], "w_in")
    grad_x = _dx(d_proj, w_in_g, dpre1, x_in[2][0])
    ((p_in, got),) = exchange_end(x_in, 1, grad_x, "w_in")
    f_in = scatter_begin(p_in, got, "w_in")
    red = _allreduce_small(g_ln2_g, g_ln2_b, g_ln1_g, g_ln1_b, g_g_ac, g_conv_w, g_sinks, loss_part, f_in[2])

    pos_vec = jnp.concatenate([chip_vec, cvec])
    shards = {"w_in": (w_in, m_w_in, v_w_in), "w_out": (w_out, m_w_out, v_w_out), "w_gate": (w_gate, m_w_gate, v_w_gate),
              "w_up": (w_up, m_w_up, v_w_up), "w_down": (w_down, m_w_down, v_w_down)}
    early = ["w_down", "w_gate", "w_up", "w_out"]
    after = red
    completing = {}
    for nme, f in zip(early, [f_down, f_gate, f_up, f_out]):
        sums, land = _scatter_wait(*f, after, "scatter_wait_" + nme)
        completing[nme] = _flight_start("complete_start_" + nme, [sums, land], _complete_plan(1), 4, cvec)
        after = completing[nme][2][1]
    big = {}
    for nme in early:
        sums, land = _flight_wait("complete_wait_" + nme, completing[nme], _complete_plan(1), after)
        big[nme] = _adamw_shard(*shards[nme], land, sums, pos_vec, "adamw_" + nme)
        after = big[nme][0]
    sums, land = _scatter_wait(*f_in, after, "scatter_wait_w_in")
    (land,) = _complete_chip_sums([sums], [land])
    big["w_in"] = _adamw_shard(*shards["w_in"], land, sums, pos_vec, "adamw_w_in")
    small = _adamw_small(red, {
        "sinks": (sinks, m_sinks, v_sinks), "g_attn": (g_attn, m_g_attn, v_g_attn),
        "g_conv": (g_conv, m_g_conv, v_g_conv), "ln1_g": (ln1_g, m_ln1_g, v_ln1_g),
        "ln1_b": (ln1_b, m_ln1_b, v_ln1_b), "ln2_g": (ln2_g, m_ln2_g, v_ln2_g),
        "ln2_b": (ln2_b, m_ln2_b, v_ln2_b), "conv_w": (conv_w, m_conv_w, v_conv_w)})
    res = {**big, **small}
    order = ["w_in", "conv_w", "sinks", "g_attn", "g_conv", "w_out", "ln1_g", "ln1_b", "w_gate", "w_up", "w_down",
             "ln2_g", "ln2_b"]
    loss = red[6, d // 2 + 128]
    return (loss, grad_x, *[res[n][0] for n in order], *[res[n][1] for n in order],
            *[res[n][2] for n in order], *[res[n][3] for n in order])
```

```python
import functools

import numpy as np
import jax
import jax.numpy as jnp
from jax import lax
from jax.experimental import pallas as pl
from jax.experimental.pallas import tpu as pltpu

F32 = jnp.float32
BF16 = jnp.bfloat16
MESH = pl.DeviceIdType.MESH

HEAD_DIM = 64
N_KV_HEADS = 4
GROUP = 4
WINDOW = 128
ROT_DIM = 16
ROPE_THETA = 500000.0
ATTN_SCALE = HEAD_DIM ** -0.5
ALPHA = 2.0 ** 0.25
LN_EPS = 1e-5
RMS_EPS = 1e-6
ADAM_LR = 0.001
ADAM_B1 = 0.9
ADAM_B2 = 0.999
ADAM_EPS = 1e-08
ADAM_WD = 0.01
ADAM_STEP = 10
N_CHIPS = 4
NEG_BIG = -1e30

V7X_VMEM_BYTES = 64 * 1024 * 1024
VMEM_LIMIT = V7X_VMEM_BYTES - 6 * 1024 * 1024

TM = 512
TK_TOK = 1024
TB_CONV = 256
TR_ELT = 256
ROW_CHUNK = 128


def _params(sem):
    return pltpu.CompilerParams(dimension_semantics=sem, vmem_limit_bytes=VMEM_LIMIT)


def _row_tile(rows, target):
    best = None
    for t in range(16, min(rows, target) + 1, 16):
        if rows % t == 0:
            best = t
    assert best is not None, (rows, target)
    return best


def _dot(a, b):
    return jnp.dot(a, b, preferred_element_type=F32)


def _dot_nt(a, b):
    return lax.dot_general(a, b, (((1,), (1,)), ((), ())), preferred_element_type=F32)


def _dot_tn(a, b):
    return lax.dot_general(a, b, (((0,), (0,)), ((), ())), preferred_element_type=F32)


def _mesh_pos():
    x, y, c = lax.axis_index("x"), lax.axis_index("y"), lax.axis_index("c")
    chips = [(1 - x, y), (x, 1 - y), (1 - x, 1 - y)]
    return x, y, c, chips


def _chip_id(px, py):
    return 2 * px + py


def _rope(t, cos, sgn_sin, sign):
    w = t.shape[1]
    lane = lax.broadcasted_iota(jnp.int32, t.shape, 1) & (HEAD_DIM - 1)
    partner = jnp.where(lane < ROT_DIM // 2, pltpu.roll(t, w - ROT_DIM // 2, 1), pltpu.roll(t, ROT_DIM // 2, 1))
    return t * cos + sign * (partner * sgn_sin)


def _tile_lanes(t, n):
    return jnp.concatenate([t] * n, axis=1)


def _sigmoid(g):
    return 1.0 / (1.0 + jnp.exp(-g))


def _for_row_chunks(n_rows, fn):
    def step(r, carry):
        fn(pl.ds(pl.multiple_of(r * ROW_CHUNK, ROW_CHUNK), ROW_CHUNK))
        return carry

    lax.fori_loop(0, n_rows // ROW_CHUNK, step, 0)


def _accumulate(acc, make_val, k, nk):
    if nk == 1:
        acc[...] = make_val()
        return

    @pl.when(k == 0)
    def _():
        acc[...] = jnp.zeros_like(acc)

    acc[...] += make_val()


def _ln_fwd(pre):
    mu = jnp.mean(pre, axis=-1, keepdims=True)
    cen = pre - mu
    var = jnp.mean(cen * cen, axis=-1, keepdims=True)
    rstd = lax.rsqrt(var + LN_EPS)
    return cen * rstd, rstd


def _ln_bwd(dy, xhat, rstd, g):
    dxhat = dy * g
    m1 = jnp.mean(dxhat, axis=-1, keepdims=True)
    m2 = jnp.mean(dxhat * xhat, axis=-1, keepdims=True)
    return rstd * (dxhat - m1 - xhat * m2)


def _cast_weight(w, chip_vec, after, name):
    _, r, c = w.shape
    tr = _row_tile(r, TR_ELT)

    def body(chip_ref, w_ref, after_ref, o_ref):
        o_ref[...] = w_ref[...].astype(BF16)

    grid_spec = pltpu.PrefetchScalarGridSpec(
        num_scalar_prefetch=1, grid=(r // tr,),
        in_specs=[pl.BlockSpec((None, tr, c), lambda i, chip_ref: (0, i, 0)), _ANY],
        out_specs=pl.BlockSpec((None, tr, c), lambda i, chip_ref: (chip_ref[0], i, 0)))
    return pl.pallas_call(
        body, name=name, grid_spec=grid_spec,
        out_shape=jax.ShapeDtypeStruct((N_CHIPS, r, c), BF16),
        compiler_params=_params(("parallel",)),
    )(chip_vec, w, after)


_HBM = pl.BlockSpec(memory_space=pltpu.HBM)
_VMEM = pl.BlockSpec(memory_space=pltpu.VMEM)


_SEM = pl.BlockSpec(memory_space=pltpu.SEMAPHORE)
_ANY = pl.BlockSpec(memory_space=pl.ANY)
_EFFECT = pltpu.SideEffectType.DATAFLOW_SIDE_EFFECTING


def _chip_copy(buf, k, chip_of_src, half_rows, send_sems, recv_sems, to):
    part = buf.at[chip_of_src, half_rows]
    return pltpu.make_async_remote_copy(
        src_ref=part, dst_ref=part, send_sem=send_sems.at[k], recv_sem=recv_sems.at[k], device_id=to, device_id_type=MESH)


def _half_rows(buf, which):
    hr = buf.shape[1] // 2
    return pl.ds(which * hr, hr)


def _after(value, dep):
    return lax.optimization_barrier((value, dep))[0]


def _flight_start(name, bufs, plan, n_sems, after):
    n = len(bufs)

    def body(*refs):
        sends, _ = plan(refs[:n], refs[n + 1], refs[n + 2])
        for cp in sends:
            cp.start()

    outs = pl.pallas_call(
        body, name=name,
        in_specs=[_HBM] * n + [_ANY], out_specs=[_SEM, _SEM] + [_HBM] * n,
        out_shape=[pltpu.SemaphoreType.DMA((n_sems,))] * 2 + [pltpu.HBM(b.shape, b.dtype) for b in bufs],
        input_output_aliases={i: 2 + i for i in range(n)},
        compiler_params=pltpu.CompilerParams(has_side_effects=_EFFECT),
    )(*[pltpu.with_memory_space_constraint(b, pltpu.HBM) for b in bufs], after)
    return outs[0], outs[1], list(outs[2:])


def _flight_wait(name, flight, plan, after):
    send_sems, recv_sems, bufs = flight
    n = len(bufs)

    def body(*refs):
        sends, recvs = plan(refs[:n], refs[n], refs[n + 1])
        for cp in sends:
            cp.wait_send()
        for cp in recvs:
            cp.wait_recv()

    outs = pl.pallas_call(
        body, name=name,
        in_specs=[_HBM] * n + [_SEM, _SEM, _ANY], out_specs=[_HBM] * n,
        out_shape=[pltpu.HBM(b.shape, b.dtype) for b in bufs],
        input_output_aliases={i: i for i in range(n)},
        compiler_params=pltpu.CompilerParams(has_side_effects=_EFFECT),
    )(*bufs, send_sems, recv_sems, after)
    return list(outs)


def _fill_plan(n_bufs):
    def plan(refs, send_sems, recv_sems):
        x, y, c, chips = _mesh_pos()
        sibling = (x, y, 1 - c)
        sends, recvs = [], []
        for w in range(n_bufs):
            for k, chip in enumerate(chips):
                slot = _chip_id(*chip)
                sends.append(_chip_copy(refs[w], 3 * w + k, slot, _half_rows(refs[w], c), send_sems, recv_sems, sibling))
                recvs.append(_chip_copy(refs[w], 3 * w + k, slot, _half_rows(refs[w], 1 - c), send_sems, recv_sems,
                                        sibling))
        return sends, recvs
    return plan


def _exchange_plan(n_parts):
    def plan(refs, send_sems, recv_sems):
        x, y, c, _ = _mesh_pos()
        copies = []
        for w in range(n_parts):
            part, got = refs[2 * w], refs[2 * w + 1]
            hr = got.shape[1]
            copies.append(pltpu.make_async_remote_copy(
                src_ref=part.at[:, pl.ds((1 - c) * hr, hr)], dst_ref=got, send_sem=send_sems.at[w],
                recv_sem=recv_sems.at[w], device_id=(x, y, 1 - c), device_id_type=MESH))
        return copies, copies
    return plan


def _gather_start(bufs, after, name):
    n = len(bufs)

    def body(*refs):
        ins = refs[:n]
        sends, recvs = refs[n + 1:2 * n + 1], refs[2 * n + 1:3 * n + 1]
        token = refs[4 * n + 1]
        x, y, c, chips = _mesh_pos()
        me = _chip_id(x, y)
        for w in range(n):
            for k, chip in enumerate(chips):
                _chip_copy(ins[w], k, me, _half_rows(ins[w], c), sends[w], recvs[w], (*chip, c)).start()
        token[...] = jnp.zeros_like(token)

    outs = pl.pallas_call(
        body, name=name,
        in_specs=[_HBM] * n + [_ANY],
        out_specs=[_SEM] * (2 * n) + [_HBM] * n + [_VMEM],
        out_shape=[pltpu.SemaphoreType.DMA((3,))] * (2 * n) + [pltpu.HBM(b.shape, b.dtype) for b in bufs]
        + [jax.ShapeDtypeStruct((8, 128), F32)],
        input_output_aliases={w: 2 * n + w for w in range(n)},
        compiler_params=pltpu.CompilerParams(has_side_effects=_EFFECT),
    )(*[pltpu.with_memory_space_constraint(b, pltpu.HBM) for b in bufs], after)
    return [(outs[w], outs[n + w], outs[2 * n + w]) for w in range(n)], outs[3 * n]


def _gather_wait(send_sems, recv_sems, buf, after, name):
    def body(buf_ref, send_ref, recv_ref, after_ref, out_ref):
        x, y, c, chips = _mesh_pos()
        me = _chip_id(x, y)
        for k, chip in enumerate(chips):
            _chip_copy(buf_ref, k, me, _half_rows(buf_ref, c), send_ref, recv_ref, (*chip, c)).wait_send()
        for k, chip in enumerate(chips):
            _chip_copy(buf_ref, k, _chip_id(*chip), _half_rows(buf_ref, c), send_ref, recv_ref, (*chip, c)).wait_recv()

    return pl.pallas_call(
        body, name=name,
        in_specs=[_HBM, _SEM, _SEM, _ANY], out_specs=_HBM,
        out_shape=pltpu.HBM(buf.shape, buf.dtype),
        input_output_aliases={0: 0},
        compiler_params=pltpu.CompilerParams(has_side_effects=_EFFECT),
    )(buf, send_sems, recv_sems, after)


def _sibling_fill(buf, name, own_too=False):
    n_copies = 4 if own_too else 3

    def body(buf_ref, out_ref, send_sems, recv_sems):
        x, y, c, chips = _mesh_pos()
        sibling = (x, y, 1 - c)
        slots = [_chip_id(*chip) for chip in chips] + ([_chip_id(x, y)] if own_too else [])
        copies = []
        for k, slot in enumerate(slots):
            cp = _chip_copy(out_ref, k, slot, _half_rows(out_ref, c), send_sems, recv_sems, sibling)
            cp.start()
            copies.append(cp)
        for k, slot in enumerate(slots):
            _chip_copy(out_ref, k, slot, _half_rows(out_ref, 1 - c), send_sems, recv_sems, sibling).wait_recv()
        for cp in copies:
            cp.wait_send()

    return pl.pallas_call(
        body, name=name,
        in_specs=[_HBM], out_specs=_HBM,
        out_shape=jax.ShapeDtypeStruct(buf.shape, buf.dtype),
        input_output_aliases={0: 0},
        scratch_shapes=[pltpu.SemaphoreType.DMA((n_copies,)), pltpu.SemaphoreType.DMA((n_copies,))],
    )(buf)


def _allgather_conv_w(cw):
    _, kw, cs = cw.shape

    def body(cw_ref, out_ref, send_sems, recv_sems):
        x, y, c, chips = _mesh_pos()
        me = _chip_id(x, y)
        out_ref[pl.ds(me, 1)] = cw_ref[...]
        copies = []
        for k, chip in enumerate(chips):
            cp = pltpu.make_async_remote_copy(
                src_ref=cw_ref.at[0], dst_ref=out_ref.at[me], send_sem=send_sems.at[k], recv_sem=recv_sems.at[k],
                device_id=(*chip, c), device_id_type=MESH)
            cp.start()
            copies.append(cp)
        for k, chip in enumerate(chips):
            pltpu.make_async_remote_copy(
                src_ref=cw_ref.at[0], dst_ref=out_ref.at[_chip_id(*chip)], send_sem=send_sems.at[k],
                recv_sem=recv_sems.at[k], device_id=(*chip, c), device_id_type=MESH).wait_recv()
        for cp in copies:
            cp.wait_send()

    return pl.pallas_call(
        body, name="allgather_conv_w",
        in_specs=[_VMEM], out_specs=_VMEM,
        out_shape=jax.ShapeDtypeStruct((N_CHIPS, kw, cs), F32),
        scratch_shapes=[pltpu.SemaphoreType.DMA((3,)), pltpu.SemaphoreType.DMA((3,))],
    )(cw)


def _exchange_halves(parts, after, name):
    n = len(parts)
    shapes = [p.shape for p in parts]

    def body(*refs):
        ins, outs = refs[:n], refs[n + 1:2 * n + 1]
        send_sems, recv_sems = refs[2 * n + 1:]
        x, y, c, _ = _mesh_pos()
        copies = []
        for w in range(n):
            hr = shapes[w][1] // 2
            cp = pltpu.make_async_remote_copy(
                src_ref=ins[w].at[:, pl.ds((1 - c) * hr, hr)], dst_ref=outs[w],
                send_sem=send_sems.at[w], recv_sem=recv_sems.at[w],
                device_id=(x, y, 1 - c), device_id_type=MESH)
            cp.start()
            copies.append(cp)
        for cp in copies:
            cp.wait()

    return pl.pallas_call(
        body, name=name,
        in_specs=[_HBM] * n + [_ANY], out_specs=[_HBM] * n,
        out_shape=[jax.ShapeDtypeStruct((s[0], s[1] // 2, s[2]), BF16) for s in shapes],
        scratch_shapes=[pltpu.SemaphoreType.DMA((n,)), pltpu.SemaphoreType.DMA((n,))],
    )(*parts, after)


def _add_halves(part, got, cvec, name):
    ns, r, cdim = part.shape
    hr = r // 2
    tr = _row_tile(hr, TR_ELT)
    nblk = hr // tr

    def body(c_ref, a_ref, b_ref, o_ref):
        o_ref[...] = (a_ref[...].astype(F32) + b_ref[...].astype(F32)).astype(BF16)

    grid_spec = pltpu.PrefetchScalarGridSpec(
        num_scalar_prefetch=1, grid=(ns, nblk),
        in_specs=[pl.BlockSpec((None, tr, cdim), lambda s, i, c_ref: (s, c_ref[0] * nblk + i, 0)),
                  pl.BlockSpec((None, tr, cdim), lambda s, i, c_ref: (s, i, 0))],
        out_specs=pl.BlockSpec((None, tr, cdim), lambda s, i, c_ref: (s, i, 0)))
    return pl.pallas_call(
        body, name=name, grid_spec=grid_spec,
        out_shape=jax.ShapeDtypeStruct((ns, hr, cdim), BF16),
        compiler_params=_params(("parallel", "parallel")),
    )(cvec, part, got)


def _scatter_copy(sums_ref, land_ref, k, src_slot, dst_slot, c, send_sems, recv_sems, to):
    return pltpu.make_async_remote_copy(
        src_ref=sums_ref.at[src_slot], dst_ref=land_ref.at[dst_slot, _half_rows(land_ref, c)],
        send_sem=send_sems.at[k], recv_sem=recv_sems.at[k], device_id=to, device_id_type=MESH)


def _scatter_start(sums, name):
    ns, hr, cdim = sums.shape
    land = lax.empty((ns, 2 * hr, cdim), sums.dtype)

    def body(sums_ref, land_ref, send_sems, recv_sems, sums_thru, land_thru):
        x, y, c, chips = _mesh_pos()
        me = _chip_id(x, y)
        for k, chip in enumerate(chips):
            _scatter_copy(sums_ref, land_ref, k, _chip_id(*chip), me, c, send_sems, recv_sems, (*chip, c)).start()

    return pl.pallas_call(
        body, name=name,
        in_specs=[_HBM, _HBM], out_specs=[_SEM, _SEM, _HBM, _HBM],
        out_shape=[pltpu.SemaphoreType.DMA((3,)), pltpu.SemaphoreType.DMA((3,)),
                   pltpu.HBM(sums.shape, sums.dtype), pltpu.HBM(land.shape, land.dtype)],
        input_output_aliases={0: 2, 1: 3},
        compiler_params=pltpu.CompilerParams(has_side_effects=_EFFECT),
    )(pltpu.with_memory_space_constraint(sums, pltpu.HBM), pltpu.with_memory_space_constraint(land, pltpu.HBM))


def _scatter_wait(send_sems, recv_sems, sums, land, after, name):
    def body(sums_ref, land_ref, send_ref, recv_ref, after_ref, sums_out, land_out):
        x, y, c, chips = _mesh_pos()
        me = _chip_id(x, y)
        for k, chip in enumerate(chips):
            _scatter_copy(sums_ref, land_ref, k, _chip_id(*chip), me, c, send_ref, recv_ref, (*chip, c)).wait_send()
        for k, chip in enumerate(chips):
            _scatter_copy(sums_ref, land_ref, k, me, _chip_id(*chip), c, send_ref, recv_ref, (*chip, c)).wait_recv()

    return pl.pallas_call(
        body, name=name,
        in_specs=[_HBM, _HBM, _SEM, _SEM, _ANY], out_specs=[_HBM, _HBM],
        out_shape=[pltpu.HBM(sums.shape, sums.dtype), pltpu.HBM(land.shape, land.dtype)],
        input_output_aliases={0: 0, 1: 1},
        compiler_params=pltpu.CompilerParams(has_side_effects=_EFFECT),
    )(sums, land, send_sems, recv_sems, after)


def _complete_plan(n_weights):
    def plan(refs, send_sems, recv_sems):
        x, y, c, chips = _mesh_pos()
        me = _chip_id(x, y)
        sibling = (x, y, 1 - c)
        sends, recvs = [], []
        for w in range(n_weights):
            sums, land = refs[2 * w], refs[2 * w + 1]
            sends.append(_scatter_copy(sums, land, 4 * w + 3, me, me, c, send_sems, recv_sems, sibling))
            recvs.append(_scatter_copy(sums, land, 4 * w + 3, me, me, 1 - c, send_sems, recv_sems, sibling))
            for k, chip in enumerate(chips):
                slot = _chip_id(*chip)
                sends.append(_chip_copy(land, 4 * w + k, slot, _half_rows(land, c), send_sems, recv_sems, sibling))
                recvs.append(_chip_copy(land, 4 * w + k, slot, _half_rows(land, 1 - c), send_sems, recv_sems, sibling))
        return sends, recvs
    return plan


def _complete_chip_sums(sums, lands):
    n = len(sums)

    def body(*refs):
        sums_refs, outs = refs[:n], refs[2 * n:3 * n]
        send_sems, recv_sems = refs[3 * n:]
        x, y, c, chips = _mesh_pos()
        me = _chip_id(x, y)
        sibling = (x, y, 1 - c)
        slots = [_chip_id(*chip) for chip in chips]
        sent = []
        for w in range(n):
            out = outs[w]
            cp = _scatter_copy(sums_refs[w], out, 3, me, me, c, send_sems.at[w], recv_sems.at[w], sibling)
            cp.start()
            sent.append(cp)
            for k, slot in enumerate(slots):
                cp = _chip_copy(out, k, slot, _half_rows(out, c), send_sems.at[w], recv_sems.at[w], sibling)
                cp.start()
                sent.append(cp)
        for w in range(n):
            out = outs[w]
            _scatter_copy(sums_refs[w], out, 3, me, me, 1 - c, send_sems.at[w], recv_sems.at[w], sibling).wait_recv()
            for k, slot in enumerate(slots):
                _chip_copy(out, k, slot, _half_rows(out, 1 - c), send_sems.at[w], recv_sems.at[w], sibling).wait_recv()
        for cp in sent:
            cp.wait_send()

    return pl.pallas_call(
        body, name="complete_chip_sums",
        in_specs=[_HBM] * (2 * n), out_specs=[_HBM] * n,
        out_shape=[jax.ShapeDtypeStruct(b.shape, b.dtype) for b in lands],
        input_output_aliases={n + w: w for w in range(n)},
        scratch_shapes=[pltpu.SemaphoreType.DMA((n, 4)), pltpu.SemaphoreType.DMA((n, 4))],
    )(*sums, *lands)


SMALL_ROWS = 8


def _allreduce_small(gl2g, gl2b, gl1g, gl1b, g_ac, gcw, gsink, loss, after):
    d = gl2g.shape[1]
    hd = d // 2
    nq = gsink.shape[1]

    def body(a_ref, b_ref, c_ref, d_ref, e_ref, cw_ref, sk_ref, ls_ref, after_ref, out_ref, mine, gath, send_sems,
             recv_sems):
        x, y, c, _ = _mesh_pos()
        me = 4 * x + 2 * y + c
        mine[...] = jnp.zeros_like(mine)
        mine[0:1, :] = a_ref[...]
        mine[1:2, :] = b_ref[...]
        mine[2:3, :] = c_ref[...]
        mine[3:4, :] = d_ref[...]
        mine[4:5, :] = e_ref[...]
        mine[5:6, 0:hd] = cw_ref[0:1, :]
        mine[5:6, hd:d] = cw_ref[1:2, :]
        mine[6:7, 0:hd] = cw_ref[2:3, :]
        mine[6:7, hd:hd + nq] = sk_ref[...]
        mine[6:7, hd + 128:hd + 256] = ls_ref[...]
        gath[pl.ds(me, 1)] = mine[...][None]
        copies = []
        for r in range(1, 8):
            peer = ((1 - x) if r & 4 else x, (1 - y) if r & 2 else y, (1 - c) if r & 1 else c)
            cp = pltpu.make_async_remote_copy(
                src_ref=mine, dst_ref=gath.at[me], send_sem=send_sems.at[r - 1], recv_sem=recv_sems.at[r - 1],
                device_id=peer, device_id_type=MESH)
            cp.start()
            copies.append(cp)
        for r in range(1, 8):
            peer = ((1 - x) if r & 4 else x, (1 - y) if r & 2 else y, (1 - c) if r & 1 else c)
            peer_id = 4 * peer[0] + 2 * peer[1] + peer[2]
            pltpu.make_async_remote_copy(
                src_ref=mine, dst_ref=gath.at[peer_id], send_sem=send_sems.at[r - 1], recv_sem=recv_sems.at[r - 1],
                device_id=peer, device_id_type=MESH).wait_recv()
        for cp in copies:
            cp.wait_send()
        total = gath[0]
        for dev in range(1, 8):
            total = total + gath[dev]
        out_ref[...] = total

    return pl.pallas_call(
        body, name="allreduce_small",
        in_specs=[_VMEM] * 8 + [_ANY], out_specs=_VMEM,
        out_shape=jax.ShapeDtypeStruct((SMALL_ROWS, d), F32),
        scratch_shapes=[pltpu.VMEM((SMALL_ROWS, d), F32), pltpu.VMEM((8, SMALL_ROWS, d), F32),
                        pltpu.SemaphoreType.DMA((7,)), pltpu.SemaphoreType.DMA((7,))],
    )(gl2g, gl2b, gl1g, gl1b, g_ac, gcw, gsink, loss, after)


def _adamw(w, g, m, v):
    m = ADAM_B1 * m + (1.0 - ADAM_B1) * g
    v = ADAM_B2 * v + (1.0 - ADAM_B2) * (g * g)
    m_hat = m / (1.0 - ADAM_B1 ** ADAM_STEP)
    v_hat = v / (1.0 - ADAM_B2 ** ADAM_STEP)
    delta = -ADAM_LR * (m_hat / (jnp.sqrt(v_hat) + ADAM_EPS) + ADAM_WD * w)
    return delta, m, v


def _adamw_shard(w, m, v, land, own, pos_vec, name):
    _, r, c = w.shape
    hr = r // 2
    tr = _row_tile(hr, TR_ELT)
    nh = hr // tr

    def body(pos_ref, w_ref, m_ref, v_ref, l0, l1, l2, l3, own_ref, g_out, d_out, m_out, v_out):
        i = pl.program_id(0)
        mine = (i // nh) == pos_ref[1]
        own_blk = own_ref[...].astype(F32)
        g = None
        for s, l_ref in enumerate([l0, l1, l2, l3]):
            term = jnp.where(mine & (pos_ref[0] == s), own_blk, l_ref[...].astype(F32))
            g = term if g is None else g + term
        delta, nm, nv = _adamw(w_ref[...], g, m_ref[...], v_ref[...])
        g_out[...] = g
        d_out[...] = delta
        m_out[...] = nm
        v_out[...] = nv

    def land_spec(s):
        def index(i, pos_ref):
            skip = (pos_ref[0] == s) & ((i // nh) == pos_ref[1])
            return (s, jnp.where(skip, (i + nh) % (2 * nh), i), 0)
        return pl.BlockSpec((None, tr, c), index)

    blk = pl.BlockSpec((None, tr, c), lambda i, pos_ref: (0, i, 0))
    grid_spec = pltpu.PrefetchScalarGridSpec(
        num_scalar_prefetch=1, grid=(2 * nh,),
        in_specs=[blk, blk, blk] + [land_spec(s) for s in range(N_CHIPS)]
        + [pl.BlockSpec((None, tr, c), lambda i, pos_ref: (pos_ref[0], i % nh, 0))],
        out_specs=[blk] * 4)
    return pl.pallas_call(
        body, name=name, grid_spec=grid_spec,
        out_shape=[jax.ShapeDtypeStruct((1, r, c), F32)] * 4,
        compiler_params=_params(("parallel",)),
    )(pos_vec, w, m, v, land, land, land, land, own)


def _adamw_small(red, params):
    names = ["sinks", "g_attn", "g_conv", "ln1_g", "ln1_b", "ln2_g", "ln2_b", "conv_w"]
    d = red.shape[1]
    hd = d // 2
    flat = []
    for nme in names:
        flat.extend(params[nme])
    nq = params["sinks"][0].shape[1]
    cs = params["conv_w"][0].shape[2]

    def body(*refs):
        red_ref = refs[0]
        ins = refs[1:1 + 3 * len(names)]
        outs = refs[1 + 3 * len(names):]
        x, y, _, _ = _mesh_pos()
        me = _chip_id(x, y)

        def conv_tap(row, base):
            picked = red_ref[row:row + 1, base:base + cs]
            for s in range(1, N_CHIPS):
                picked = jnp.where(me == s, red_ref[row:row + 1, base + s * cs:base + (s + 1) * cs], picked)
            return picked

        grads = {
            "sinks": red_ref[6:7, hd:hd + nq],
            "g_attn": red_ref[4:5, 0:hd],
            "g_conv": red_ref[4:5, hd:d],
            "ln1_g": red_ref[2:3, :],
            "ln1_b": red_ref[3:4, :],
            "ln2_g": red_ref[0:1, :],
            "ln2_b": red_ref[1:2, :],
        }
        for i, nme in enumerate(names):
            w_ref, m_ref, v_ref = ins[3 * i:3 * i + 3]
            g_out, d_out, m_out, v_out = outs[4 * i:4 * i + 4]
            if nme == "conv_w":
                for tap, (row, base) in enumerate([(5, 0), (5, hd), (6, 0)]):
                    g = conv_tap(row, base)
                    delta, nm, nv = _adamw(w_ref[0, tap:tap + 1, :], g, m_ref[0, tap:tap + 1, :], v_ref[0, tap:tap + 1, :])
                    g_out[0, tap:tap + 1, :] = g
                    d_out[0, tap:tap + 1, :] = delta
                    m_out[0, tap:tap + 1, :] = nm
                    v_out[0, tap:tap + 1, :] = nv
            else:
                g = grads[nme]
                delta, nm, nv = _adamw(w_ref[...], g, m_ref[...], v_ref[...])
                g_out[...] = g
                d_out[...] = delta
                m_out[...] = nm
                v_out[...] = nv

    out_shape = []
    for nme in names:
        out_shape.extend([jax.ShapeDtypeStruct(params[nme][0].shape, F32)] * 4)
    outs = pl.pallas_call(
        body, name="adamw_small",
        in_specs=[_VMEM] * (1 + len(flat)), out_specs=[_VMEM] * len(out_shape),
        out_shape=out_shape,
    )(red, *flat)
    return {nme: tuple(outs[4 * i:4 * i + 4]) for i, nme in enumerate(names)}


def _rope_tables(pos_col):
    s = pos_col.shape[0]
    w = N_KV_HEADS * HEAD_DIM
    tb = min(512, s)
    inv_freq = (ROPE_THETA ** (-np.arange(0, ROT_DIM, 2, dtype=np.float32) / ROT_DIM)).astype(np.float32)

    def body(pos_ref, cos_ref, sin_ref):
        pos = pos_ref[...].astype(F32)
        lane = lax.broadcasted_iota(jnp.int32, (tb, w), 1) & (HEAD_DIM - 1)
        fidx = lane & (ROT_DIM // 2 - 1)
        inv = jnp.zeros((tb, w), F32)
        for k in range(ROT_DIM // 2):
            inv = jnp.where(fidx == k, float(inv_freq[k]), inv)
        ang = pos * inv
        rot = lane < ROT_DIM
        cos_ref[...] = jnp.where(rot, jnp.cos(ang), 1.0)
        sin_v = jnp.sin(ang)
        sin_ref[...] = jnp.where(lane < ROT_DIM // 2, -sin_v, jnp.where(rot, sin_v, 0.0))

    return pl.pallas_call(
        body, name="rope_tables", grid=(s // tb,),
        in_specs=[pl.BlockSpec((tb, 1), lambda i: (i, 0))],
        out_specs=[pl.BlockSpec((tb, w), lambda i: (i, 0))] * 2,
        out_shape=[jax.ShapeDtypeStruct((s, w), F32)] * 2,
        compiler_params=_params(("parallel",)),
    )(pos_col)


def _in_proj(x, w_in_g):
    _, s, d = x.shape
    ns, _, ncol = w_in_g.shape
    tm = min(2 * TM, s)

    def body(x_ref, w_ref, o_ref):
        o_ref[...] = _dot(x_ref[...].astype(BF16), w_ref[...])

    return pl.pallas_call(
        body, name="in_proj", grid=(s // tm, ns),
        in_specs=[pl.BlockSpec((None, tm, d), lambda i, j: (0, i, 0)),
                  pl.BlockSpec((None, d, ncol), lambda i, j: (j, 0, 0))],
        out_specs=pl.BlockSpec((tm, ncol), lambda i, j: (i, j)),
        out_shape=jax.ShapeDtypeStruct((s, ns * ncol), F32),
        compiler_params=_params(("parallel", "arbitrary")),
    )(x, w_in_g)


PAIR = 2 * HEAD_DIM
KEYS = 2 * WINDOW


def _pair_operand(t_all, h):
    col = (h // 2) * PAIR
    lane = lax.broadcasted_iota(jnp.int32, (KEYS, PAIR), 1)
    own_low = h % 2 == 0
    mine = jnp.where((lane < HEAD_DIM) if own_low else (lane >= HEAD_DIM), t_all[:, col:col + PAIR], 0.0)
    other = pltpu.roll(mine, HEAD_DIM, 1)
    low, high = (mine, other) if own_low else (other, mine)
    return jnp.concatenate([low, high], axis=0).astype(BF16)


def _pair_grad(acc, h):
    lane = lax.broadcasted_iota(jnp.int32, (KEYS, PAIR), 1)
    low = jnp.where(lane < HEAD_DIM, acc[:KEYS], 0.0)
    high = jnp.where(lane >= HEAD_DIM, acc[KEYS:], 0.0)
    if h % 2 == 0:
        return low + pltpu.roll(high, HEAD_DIM, 1)
    return high + pltpu.roll(low, HEAD_DIM, 1)


N_PAIRS = N_KV_HEADS * GROUP // 2


def _all_probs(q, kk2s, first, sinks_ref):
    assert ATTN_SCALE == 0.125
    q = q * ATTN_SCALE
    qps, scores = [], []
    for pair in range(N_PAIRS):
        qp = q[:, pair * PAIR:(pair + 1) * PAIR].astype(BF16)
        qps.append(qp)
        scores.append(_dot_nt(qp, kk2s[pair // (GROUP // 2)]))
    qi = lax.broadcasted_iota(jnp.int32, (WINDOW, 2 * KEYS), 0)
    kj = lax.broadcasted_iota(jnp.int32, (WINDOW, 2 * KEYS), 1) & (KEYS - 1)
    rel = qi + WINDOW - kj
    valid = (rel >= 0) & (rel < WINDOW) & jnp.logical_not(first & (kj < WINDOW))
    bias = jnp.where(valid, 0.0, NEG_BIG)
    s = (jnp.stack(scores, axis=0) + bias[None]).reshape(N_PAIRS * WINDOW, 2 * KEYS)
    probs, p_sinks = [], []
    for t in range(2):
        st = s[:, t * KEYS:(t + 1) * KEYS]
        sink = jnp.concatenate([jnp.broadcast_to(sinks_ref[0:1, 2 * pair + t:2 * pair + t + 1], (WINDOW, 1))
                                for pair in range(N_PAIRS)], axis=0)
        m = jnp.maximum(jnp.max(st, axis=1, keepdims=True), sink)
        e = jnp.exp(st - m)
        e_sink = jnp.exp(sink - m)
        inv_l = 1.0 / (jnp.sum(e, axis=1, keepdims=True) + e_sink)
        probs.append(e * inv_l)
        p_sinks.append(e_sink * inv_l)
    return qps, jnp.concatenate(probs, axis=1), p_sinks


def _roped_qkv(cur_ref, prev_ref, cos_ref, sin_ref, cosp_ref, sinp_ref, qw, kvw):
    cur = cur_ref[...]
    cos, sin = cos_ref[...], sin_ref[...]
    cos_q, sin_q = _tile_lanes(cos, GROUP), _tile_lanes(sin, GROUP)
    q = _rope(cur[:, :qw], cos_q, sin_q, 1.0)
    prev = prev_ref[...]
    k_all = jnp.concatenate([_rope(prev[:, :kvw], cosp_ref[...], sinp_ref[...], 1.0),
                             _rope(cur[:, qw:qw + kvw], cos, sin, 1.0)], axis=0)
    v_all = jnp.concatenate([prev[:, kvw:], cur[:, qw + kvw:]], axis=0)
    return q, k_all, v_all, cos_q, sin_q


def _attention_fwd(proj, cos_t, sin_t, sinks):
    s = proj.shape[0]
    qw = GROUP * N_KV_HEADS * HEAD_DIM
    kvw = N_KV_HEADS * HEAD_DIM
    nb = s // WINDOW

    def body(cur_ref, prev_ref, cos_ref, sin_ref, cosp_ref, sinp_ref, sinks_ref, o_ref):
        first = pl.program_id(0) == 0
        q, k_all, v_all, _, _ = _roped_qkv(cur_ref, prev_ref, cos_ref, sin_ref, cosp_ref, sinp_ref, qw, kvw)
        kk2s = [_pair_operand(k_all, h) for h in range(N_KV_HEADS)]
        vv2s = [_pair_operand(v_all, h) for h in range(N_KV_HEADS)]
        _, probs, _ = _all_probs(q, kk2s, first, sinks_ref)
        probs = probs.astype(BF16)
        outs = [_dot(probs[pair * WINDOW:(pair + 1) * WINDOW], vv2s[pair // (GROUP // 2)]) for pair in range(N_PAIRS)]
        o_ref[...] = jnp.concatenate(outs, axis=1)

    tbl = pl.BlockSpec((WINDOW, kvw), lambda n: (n, 0))
    tbl_prev = pl.BlockSpec((WINDOW, kvw), lambda n: (jnp.maximum(n - 1, 0), 0))
    return pl.pallas_call(
        body, name="attention_fwd", grid=(nb,),
        in_specs=[pl.BlockSpec((WINDOW, qw + 2 * kvw), lambda n: (n, 0)),
                  pl.BlockSpec((WINDOW, 2 * kvw), lambda n: (jnp.maximum(n - 1, 0), (qw // (2 * kvw)))),
                  tbl, tbl, tbl_prev, tbl_prev, _VMEM],
        out_specs=pl.BlockSpec((WINDOW, qw), lambda n: (n, 0)),
        out_shape=jax.ShapeDtypeStruct((s, qw), F32),
        compiler_params=_params(("parallel",)),
    )(proj, proj, cos_t, sin_t, cos_t, sin_t, sinks)


def _conv_taps(cw_ref):
    return [jnp.concatenate([cw_ref[s, k:k + 1, :] for s in range(N_CHIPS)], axis=1) for k in range(3)]


def _shift_down(z, halo, steps):
    rows = z.shape[0]
    row = lax.broadcasted_iota(jnp.int32, z.shape, 0)
    out = pltpu.roll(z, steps, 0)
    for r in range(steps):
        out = jnp.where(row == r, halo[8 - steps + r:8 - steps + r + 1, :], out)
    return out


def _shift_up(z, halo, steps):
    rows = z.shape[0]
    row = lax.broadcasted_iota(jnp.int32, z.shape, 0)
    out = pltpu.roll(z, rows - steps, 0)
    for r in range(steps):
        out = jnp.where(row == rows - steps + r, halo[r:r + 1, :], out)
    return out


def _split_cbu(lo, hi, cw):
    c_gate = lo[:, :cw]
    b_gate = jnp.concatenate([lo[:, cw:], hi[:, :2 * cw - lo.shape[1]]], axis=1)
    u = hi[:, 2 * cw - lo.shape[1]:]
    return c_gate, b_gate, u


def _conv_norm(proj, attn, cw_full, g_ac):
    s, in_w = proj.shape
    cw = attn.shape[1]
    blk_w = in_w // 3
    tb = min(TB_CONV, s)

    def body(lo_ref, hi_ref, lo_h_ref, hi_h_ref, attn_ref, cw_ref, g_ref, mixed_ref, ac_ref, rstd_ref):
        i = pl.program_id(0)
        c_gate, b_gate, u = _split_cbu(lo_ref[...], hi_ref[...], cw)
        c_h, _, u_h = _split_cbu(lo_h_ref[...], hi_h_ref[...], cw)
        z = c_gate * u
        z_h = jnp.where(i == 0, 0.0, c_h * u_h)
        w0, w1, w2 = _conv_taps(cw_ref)
        y = w0 * _shift_down(z, z_h, 2) + w1 * _shift_down(z, z_h, 1) + w2 * z
        conv = b_gate * y
        a = attn_ref[...]
        r_a = lax.rsqrt(jnp.mean(a * a, axis=-1, keepdims=True) + RMS_EPS)
        r_c = lax.rsqrt(jnp.mean(conv * conv, axis=-1, keepdims=True) + RMS_EPS)
        g = g_ref[...]
        mixed_ref[...] = jnp.concatenate([a * r_a * g[:, :cw], conv * r_c * g[:, cw:]], axis=1).astype(BF16)
        ac_ref[...] = jnp.concatenate([a, conv], axis=1)
        rstd_ref[0] = r_a
        rstd_ref[1] = r_c

    halo_idx = lambda i: jnp.maximum(i * (tb // 8) - 1, 0)
    return pl.pallas_call(
        body, name="conv_norm", grid=(s // tb,),
        in_specs=[pl.BlockSpec((tb, blk_w), lambda i: (i, 1)),
                  pl.BlockSpec((tb, blk_w), lambda i: (i, 2)),
                  pl.BlockSpec((8, blk_w), lambda i: (halo_idx(i), 1)),
                  pl.BlockSpec((8, blk_w), lambda i: (halo_idx(i), 2)),
                  pl.BlockSpec((tb, cw), lambda i: (i, 0)),
                  _VMEM, _VMEM],
        out_specs=[pl.BlockSpec((tb, 2 * cw), lambda i: (i, 0)),
                   pl.BlockSpec((tb, 2 * cw), lambda i: (i, 0)),
                   pl.BlockSpec((2, tb, 1), lambda i: (0, i, 0))],
        out_shape=[jax.ShapeDtypeStruct((s, 2 * cw), BF16), jax.ShapeDtypeStruct((s, 2 * cw), F32),
                   jax.ShapeDtypeStruct((2, s, 1), F32)],
        compiler_params=_params(("parallel",)),
    )(proj, proj, proj, proj, attn, cw_full, g_ac)


def _out_proj_ln(mixed, w_out_g, x, ln_g, ln_b):
    s, d = mixed.shape
    tm = min(TM, s)
    tk = d
    nk = d // tk

    def body(a_ref, w_ref, x_ref, g_ref, b_ref, xhat_ref, h_ref, rstd_ref, acc):
        k = pl.program_id(1)
        _accumulate(acc, lambda: _dot(a_ref[...], w_ref[...]), k, nk)

        @pl.when(k == nk - 1)
        def _():
            def rows_fn(rows):
                xhat, rstd = _ln_fwd(ALPHA * x_ref[rows, :] + acc[rows, :])
                xhat_ref[rows, :] = xhat
                h_ref[rows, :] = (xhat * g_ref[...] + b_ref[...]).astype(BF16)
                rstd_ref[rows, :] = rstd

            _for_row_chunks(tm, rows_fn)

    row = pl.BlockSpec((tm, d), lambda i, k: (i, 0))
    return pl.pallas_call(
        body, name="out_proj_ln", grid=(s // tm, nk),
        in_specs=[pl.BlockSpec((tm, tk), lambda i, k: (i, k)),
                  pl.BlockSpec((tk, d), lambda i, k: (k, 0)),
                  pl.BlockSpec((None, tm, d), lambda i, k: (0, i, 0)),
                  _VMEM, _VMEM],
        out_specs=[row, row, pl.BlockSpec((tm, 1), lambda i, k: (i, 0))],
        out_shape=[jax.ShapeDtypeStruct((s, d), F32), jax.ShapeDtypeStruct((s, d), BF16),
                   jax.ShapeDtypeStruct((s, 1), F32)],
        scratch_shapes=[pltpu.VMEM((tm, d), F32)],
        compiler_params=_params(("parallel", "arbitrary")),
    )(mixed, w_out_g, x, ln_g, ln_b)


def _gate_up(h1, w_gate_g, w_up_g):
    s, d = h1.shape
    ns, _, fs = w_gate_g.shape
    tm = min(TM, s)

    def body(h_ref, wg_ref, wu_ref, act_ref, g_ref, u_ref):
        h = h_ref[...]
        g = _dot(h, wg_ref[...])
        u = _dot(h, wu_ref[...])
        act_ref[...] = (g * _sigmoid(g) * u).astype(BF16)
        g_ref[...] = g.astype(BF16)
        u_ref[...] = u.astype(BF16)

    wspec = pl.BlockSpec((None, d, fs), lambda i, j: (j, 0, 0))
    ospec = pl.BlockSpec((tm, fs), lambda i, j: (i, j))
    return pl.pallas_call(
        body, name="gate_up", grid=(s // tm, ns),
        in_specs=[pl.BlockSpec((tm, d), lambda i, j: (i, 0)), wspec, wspec],
        out_specs=[ospec] * 3,
        out_shape=[jax.ShapeDtypeStruct((s, ns * fs), BF16)] * 3,
        compiler_params=_params(("parallel", "arbitrary")),
    )(h1, w_gate_g, w_up_g)


def _down_ln_loss(act, w_down_g, xhat1, ln1_g, ln1_b, ln2_g, ln2_b, target):
    s, f = act.shape
    d = xhat1.shape[1]
    tm = min(TM, s)
    tk = f // N_CHIPS
    nk = f // tk

    def body(a_ref, w_ref, xh_ref, g1_ref, b1_ref, g2_ref, b2_ref, t_ref, dpre_ref, loss_ref, gg_ref, gb_ref, acc):
        i, k = pl.program_id(0), pl.program_id(1)
        _accumulate(acc, lambda: _dot(a_ref[...], w_ref[...]), k, nk)

        @pl.when(k == nk - 1)
        def _():
            @pl.when(i == 0)
            def _():
                loss_ref[...] = jnp.zeros_like(loss_ref)
                gg_ref[...] = jnp.zeros_like(gg_ref)
                gb_ref[...] = jnp.zeros_like(gb_ref)

            def rows_fn(rows):
                h1 = xh_ref[rows, :] * g1_ref[...] + b1_ref[...]
                xhat, rstd = _ln_fwd(ALPHA * h1 + acc[rows, :])
                g2 = g2_ref[...]
                diff = xhat * g2 + b2_ref[...] - t_ref[rows, :]
                dy = diff * (1.0 / d)
                dpre_ref[rows, :] = _ln_bwd(dy, xhat, rstd, g2)
                sq = jnp.sum(jnp.sum(diff * diff, axis=1, keepdims=True), axis=0, keepdims=True)
                loss_ref[...] += jnp.broadcast_to(sq * (0.5 / d), (1, 128))
                gg_ref[...] += jnp.sum(dy * xhat, axis=0, keepdims=True)
                gb_ref[...] += jnp.sum(dy, axis=0, keepdims=True)

            _for_row_chunks(tm, rows_fn)

    row = pl.BlockSpec((tm, d), lambda i, k: (i, 0))
    vec = pl.BlockSpec((1, d), lambda i, k: (0, 0))
    return pl.pallas_call(
        body, name="down_ln_loss", grid=(s // tm, nk),
        in_specs=[pl.BlockSpec((tm, tk), lambda i, k: (i, k)),
                  pl.BlockSpec((tk, d), lambda i, k: (k, 0)),
                  row, _VMEM, _VMEM, _VMEM, _VMEM,
                  pl.BlockSpec((None, tm, d), lambda i, k: (0, i, 0))],
        out_specs=[row, pl.BlockSpec((1, 128), lambda i, k: (0, 0)), vec, vec],
        out_shape=[jax.ShapeDtypeStruct((s, d), F32), jax.ShapeDtypeStruct((1, 128), F32),
                   jax.ShapeDtypeStruct((1, d), F32), jax.ShapeDtypeStruct((1, d), F32)],
        scratch_shapes=[pltpu.VMEM((tm, d), F32)],
        compiler_params=_params(("arbitrary", "arbitrary")),
    )(act, w_down_g, xhat1, ln1_g, ln1_b, ln2_g, ln2_b, target)


def _dact_silu_bwd(dpre2, w_down_g, gate, up):
    s, d = dpre2.shape
    f = gate.shape[1]
    fs = f // N_CHIPS
    tm = min(TM, s)

    def body(dp_ref, w_ref, g_ref, u_ref, dg_ref, du_ref):
        d_act = _dot_nt(dp_ref[...].astype(BF16), w_ref[...])
        g = g_ref[...].astype(F32)
        u = u_ref[...].astype(F32)
        sg = _sigmoid(g)
        dg_ref[...] = (d_act * u * (sg * (1.0 + g * (1.0 - sg)))).astype(BF16)
        du_ref[...] = (d_act * (g * sg)).astype(BF16)

    blk = pl.BlockSpec((tm, fs), lambda i, j: (i, j))
    return pl.pallas_call(
        body, name="dact_silu_bwd", grid=(s // tm, N_CHIPS),
        in_specs=[pl.BlockSpec((tm, d), lambda i, j: (i, 0)),
                  pl.BlockSpec((fs, d), lambda i, j: (j, 0)), blk, blk],
        out_specs=[blk, blk],
        out_shape=[jax.ShapeDtypeStruct((s, f), BF16)] * 2,
        compiler_params=_params(("parallel", "arbitrary")),
    )(dpre2, w_down_g, gate, up)


def _grad_rows(a, b, after, name, row_blocks=1):
    s, m = a.shape
    n = b.shape[1]
    ms = m // N_CHIPS
    tmw = ms // row_blocks
    tk = min(TK_TOK, s)
    nk = s // tk

    def body(a_ref, b_ref, after_ref, o_ref, acc):
        k = pl.program_id(2)
        _accumulate(acc, lambda: _dot_tn(a_ref[...].astype(BF16), b_ref[...].astype(BF16)), k, nk)

        @pl.when(k == nk - 1)
        def _():
            o_ref[...] = acc[...].astype(BF16)

    return pl.pallas_call(
        body, name=name, grid=(N_CHIPS, row_blocks, nk),
        in_specs=[pl.BlockSpec((tk, tmw), lambda j, r, k: (k, j * row_blocks + r)),
                  pl.BlockSpec((tk, n), lambda j, r, k: (k, 0)), _ANY],
        out_specs=pl.BlockSpec((None, tmw, n), lambda j, r, k: (j, r, 0)),
        out_shape=jax.ShapeDtypeStruct((N_CHIPS, ms, n), BF16),
        scratch_shapes=[pltpu.VMEM((tmw, n), F32)],
        compiler_params=_params(("parallel", "parallel", "arbitrary")),
    )(a, b, after)


def _grad_cols(a, bs, after, name, a_3d=False, row_blocks=2):
    s, m = a.shape[-2:]
    n = bs[0].shape[1]
    ns = n // N_CHIPS
    nb = len(bs)
    tmw = m // row_blocks
    tk = min(TK_TOK, s)
    nk = s // tk

    def body(*refs):
        a_ref, b_refs, o_refs, accs = refs[0], refs[1:1 + nb], refs[2 + nb:2 + 2 * nb], refs[2 + 2 * nb:]
        k = pl.program_id(2)
        for b_ref, acc in zip(b_refs, accs):
            _accumulate(acc, lambda b_ref=b_ref: _dot_tn(a_ref[...].astype(BF16), b_ref[...].astype(BF16)), k, nk)

        @pl.when(k == nk - 1)
        def _():
            for o_ref, acc in zip(o_refs, accs):
                o_ref[...] = acc[...].astype(BF16)

    if a_3d:
        a_spec = pl.BlockSpec((None, tk, tmw), lambda j, r, k: (0, k, r))
    else:
        a_spec = pl.BlockSpec((tk, tmw), lambda j, r, k: (k, r))
    return pl.pallas_call(
        body, name=name, grid=(N_CHIPS, row_blocks, nk),
        in_specs=[a_spec] + [pl.BlockSpec((tk, ns), lambda j, r, k: (k, j))] * nb + [_ANY],
        out_specs=[pl.BlockSpec((None, tmw, ns), lambda j, r, k: (j, r, 0))] * nb,
        out_shape=[jax.ShapeDtypeStruct((N_CHIPS, m, ns), BF16)] * nb,
        scratch_shapes=[pltpu.VMEM((tmw, ns), F32)] * nb,
        compiler_params=_params(("parallel", "parallel", "arbitrary")),
    )(a, *bs, after)


def _dh1_ln_bwd(d_gate, d_up, w_gate_g, w_up_g, dpre2, xhat1, rstd1, ln1_g, after):
    s, f = d_gate.shape
    d = dpre2.shape[1]
    hd = d // 2
    fs = f // N_CHIPS
    tm = min(TM, s)

    def body(dg_ref, du_ref, wg_ref, wu_ref, dp2_ref, xh_ref, rs_ref, g_ref, after_ref, dpre_ref, gg_ref, gb_ref,
             acc_lo, acc_hi):
        i, j, half = pl.program_id(0), pl.program_id(1), pl.program_id(2)
        def product():
            return _dot_nt(dg_ref[...], wg_ref[...]) + _dot_nt(du_ref[...], wu_ref[...])

        @pl.when(half == 0)
        def _():
            _accumulate(acc_lo, product, j, N_CHIPS)

        @pl.when(half == 1)
        def _():
            _accumulate(acc_hi, product, j, N_CHIPS)

        @pl.when((j == N_CHIPS - 1) & (half == 1))
        def _():
            @pl.when(i == 0)
            def _():
                gg_ref[...] = jnp.zeros_like(gg_ref)
                gb_ref[...] = jnp.zeros_like(gb_ref)

            def rows_fn(rows):
                dh = jnp.concatenate([acc_lo[rows, :], acc_hi[rows, :]], axis=1) + ALPHA * dp2_ref[rows, :]
                xhat = xh_ref[rows, :]
                dpre_ref[rows, :] = _ln_bwd(dh, xhat, rs_ref[rows, :], g_ref[...])
                gg_ref[...] += jnp.sum(dh * xhat, axis=0, keepdims=True)
                gb_ref[...] += jnp.sum(dh, axis=0, keepdims=True)

            _for_row_chunks(tm, rows_fn)

    row = pl.BlockSpec((tm, d), lambda i, j, h: (i, 0))
    vec = pl.BlockSpec((1, d), lambda i, j, h: (0, 0))
    act_blk = pl.BlockSpec((tm, fs), lambda i, j, h: (i, j))
    w_blk = pl.BlockSpec((None, hd, fs), lambda i, j, h: (j, h, 0))
    return pl.pallas_call(
        body, name="dh1_ln_bwd", grid=(s // tm, N_CHIPS, 2),
        in_specs=[act_blk, act_blk, w_blk, w_blk, row, row, pl.BlockSpec((tm, 1), lambda i, j, h: (i, 0)), _VMEM,
                  _ANY],
        out_specs=[row, vec, vec],
        out_shape=[jax.ShapeDtypeStruct((s, d), F32), jax.ShapeDtypeStruct((1, d), F32),
                   jax.ShapeDtypeStruct((1, d), F32)],
        scratch_shapes=[pltpu.VMEM((tm, hd), F32)] * 2,
        compiler_params=_params(("arbitrary", "arbitrary", "arbitrary")),
    )(d_gate, d_up, w_gate_g, w_up_g, dpre2, xhat1, rstd1, ln1_g, after)


def _dmixed_rms_bwd(dpre1, w_out_g, ac, rstd, g_ac):
    s, d = dpre1.shape
    hd = d // 2
    tm = min(TM, s)

    def body(dp_ref, w_ref, ac_ref, rs_ref, g_ref, dac_ref, gg_ref):
        i = pl.program_id(1)
        dm = _dot_nt(dp_ref[...].astype(BF16), w_ref[...])
        pre = ac_ref[...]
        r = rs_ref[...]
        gdm = dm * g_ref[...]
        dac_ref[...] = r * gdm - pre * (r * r * r) * jnp.mean(gdm * pre, axis=-1, keepdims=True)
        gg = jnp.sum(dm * pre * r, axis=0, keepdims=True)

        @pl.when(i == 0)
        def _():
            gg_ref[...] = gg

        @pl.when(i > 0)
        def _():
            gg_ref[...] += gg

    return pl.pallas_call(
        body, name="dmixed_rms_bwd", grid=(2, s // tm),
        in_specs=[pl.BlockSpec((tm, d), lambda h, i: (i, 0)),
                  pl.BlockSpec((hd, d), lambda h, i: (h, 0)),
                  pl.BlockSpec((tm, hd), lambda h, i: (i, h)),
                  pl.BlockSpec((None, tm, 1), lambda h, i: (h, i, 0)),
                  pl.BlockSpec((1, hd), lambda h, i: (0, h))],
        out_specs=[pl.BlockSpec((tm, hd), lambda h, i: (i, h)),
                   pl.BlockSpec((1, hd), lambda h, i: (0, h))],
        out_shape=[jax.ShapeDtypeStruct((s, d), F32), jax.ShapeDtypeStruct((1, d), F32)],
        compiler_params=_params(("arbitrary", "arbitrary")),
    )(dpre1, w_out_g, ac, rstd, g_ac)


def _attention_bwd(proj, d_ac, cos_t, sin_t, sinks, after):
    s = proj.shape[0]
    qw = GROUP * N_KV_HEADS * HEAD_DIM
    kvw = N_KV_HEADS * HEAD_DIM
    nb = s // WINDOW
    nq = GROUP * N_KV_HEADS

    def body(cur_ref, prev_ref, do_ref, cos_ref, sin_ref, cosp_ref, sinp_ref, sinks_ref, after_ref,
             dq_ref, dcur_ref, dprev_ref, dsink_ref):
        n = pl.program_id(0)
        first = n == 0
        q, k_all, v_all, cos_q, sin_q = _roped_qkv(cur_ref, prev_ref, cos_ref, sin_ref, cosp_ref, sinp_ref, qw, kvw)
        kk2s = [_pair_operand(k_all, h) for h in range(N_KV_HEADS)]
        vv2s = [_pair_operand(v_all, h) for h in range(N_KV_HEADS)]
        qps, probs, p_sinks = _all_probs(q, kk2s, first, sinks_ref)
        dops = [do_ref[:, pair * PAIR:(pair + 1) * PAIR].astype(BF16) for pair in range(N_PAIRS)]
        d_probs = jnp.concatenate([_dot_nt(dops[pair], vv2s[pair // (GROUP // 2)]) for pair in range(N_PAIRS)], axis=0)
        d_s, ds_sinks = [], []
        for t in range(2):
            cols = slice(t * KEYS, (t + 1) * KEYS)
            delta = jnp.sum(probs[:, cols] * d_probs[:, cols], axis=1, keepdims=True)
            d_s.append(probs[:, cols] * (d_probs[:, cols] - delta))
            ds_sinks.append(-p_sinks[t] * delta)
        d_s = jnp.concatenate(d_s, axis=1).astype(BF16)
        probs = probs.astype(BF16)
        dq_parts, dk_tiles, dv_tiles, dsink_parts = [], [], [], []
        for h in range(N_KV_HEADS):
            dkk2, dvv2 = None, None
            for p in range(GROUP // 2):
                pair = (GROUP // 2) * h + p
                rows = slice(pair * WINDOW, (pair + 1) * WINDOW)
                dq_parts.append(_dot(d_s[rows], kk2s[h]) * ATTN_SCALE)
                dk_term = _dot_tn(d_s[rows], qps[pair])
                dv_term = _dot_tn(probs[rows], dops[pair])
                dkk2 = dk_term if dkk2 is None else dkk2 + dk_term
                dvv2 = dv_term if dvv2 is None else dvv2 + dv_term
                dsink_parts.extend([jnp.sum(ds_sinks[t][rows], axis=0, keepdims=True) for t in range(2)])
            dk_tiles.append(_pair_grad(dkk2, h))
            dv_tiles.append(_pair_grad(dvv2, h))
        dq_ref[...] = _rope(jnp.concatenate(dq_parts, axis=1), cos_q, sin_q, -1.0)
        dk = jnp.concatenate([dk_tiles[0] + dk_tiles[1], dk_tiles[2] + dk_tiles[3]], axis=1)
        dv = jnp.concatenate([dv_tiles[0] + dv_tiles[1], dv_tiles[2] + dv_tiles[3]], axis=1)
        dprev_ref[...] = jnp.concatenate([dk[:WINDOW], dv[:WINDOW]], axis=1)
        dcur_ref[...] = jnp.concatenate([dk[WINDOW:], dv[WINDOW:]], axis=1)
        dsink = jnp.concatenate(dsink_parts, axis=1)

        @pl.when(first)
        def _():
            dsink_ref[...] = dsink

        @pl.when(n > 0)
        def _():
            dsink_ref[...] += dsink

    tbl = pl.BlockSpec((WINDOW, kvw), lambda n: (n, 0))
    tbl_prev = pl.BlockSpec((WINDOW, kvw), lambda n: (jnp.maximum(n - 1, 0), 0))
    kv_blk = pl.BlockSpec((WINDOW, 2 * kvw), lambda n: (n, 0))
    return pl.pallas_call(
        body, name="attention_bwd", grid=(nb,),
        in_specs=[pl.BlockSpec((WINDOW, qw + 2 * kvw), lambda n: (n, 0)),
                  pl.BlockSpec((WINDOW, 2 * kvw), lambda n: (jnp.maximum(n - 1, 0), (qw // (2 * kvw)))),
                  pl.BlockSpec((WINDOW, qw), lambda n: (n, 0)),
                  tbl, tbl, tbl_prev, tbl_prev, _VMEM, _ANY],
        out_specs=[pl.BlockSpec((WINDOW, qw), lambda n: (n, 0)), kv_blk, kv_blk,
                   pl.BlockSpec((1, nq), lambda n: (0, 0))],
        out_shape=[jax.ShapeDtypeStruct((s, qw), F32), jax.ShapeDtypeStruct((s, 2 * kvw), F32),
                   jax.ShapeDtypeStruct((s, 2 * kvw), F32), jax.ShapeDtypeStruct((1, nq), F32)],
        compiler_params=_params(("arbitrary",)),
    )(proj, proj, d_ac, cos_t, sin_t, cos_t, sin_t, sinks, after)


def _dproj_assemble(proj, d_ac, dq, dkv_cur, dkv_prev, cos_t, sin_t, cw_full):
    s, in_w = proj.shape
    cw = dq.shape[1]
    kvw = N_KV_HEADS * HEAD_DIM
    blk_w = in_w // 3
    tb = WINDOW
    nb = s // tb

    def body(lo_ref, hi_ref, lo_p_ref, hi_p_ref, lo_n_ref, hi_n_ref, dconv_ref, dconv_n_ref,
             dq_ref, dcur_ref, dprev_n_ref, cos_ref, sin_ref, cw_ref, dproj_ref, gcw_ref):
        i = pl.program_id(0)
        last = i == nb - 1
        c_gate, b_gate, u = _split_cbu(lo_ref[...], hi_ref[...], cw)
        c_p, _, u_p = _split_cbu(lo_p_ref[...], hi_p_ref[...], cw)
        _, b_n, _ = _split_cbu(lo_n_ref[...], hi_n_ref[...], cw)
        z = c_gate * u
        z_p = jnp.where(i == 0, 0.0, c_p * u_p)
        z1 = _shift_down(z, z_p, 1)
        z2 = _shift_down(z, z_p, 2)
        w0, w1, w2 = _conv_taps(cw_ref)
        y = w0 * z2 + w1 * z1 + w2 * z
        d_conv = dconv_ref[...]
        d_b = d_conv * y
        d_y = d_conv * b_gate
        d_y_n = jnp.where(last, 0.0, dconv_n_ref[...] * b_n)
        d_z = w2 * d_y + w1 * _shift_up(d_y, d_y_n, 1) + w0 * _shift_up(d_y, d_y_n, 2)
        d_c = d_z * u
        d_u = d_z * c_gate
        gcw = jnp.concatenate([jnp.sum(d_y * z2, axis=0, keepdims=True), jnp.sum(d_y * z1, axis=0, keepdims=True),
                               jnp.sum(d_y * z, axis=0, keepdims=True)], axis=0)

        @pl.when(i == 0)
        def _():
            gcw_ref[...] = gcw

        @pl.when(i > 0)
        def _():
            gcw_ref[...] += gcw

        dkv = dcur_ref[...] + jnp.where(last, 0.0, dprev_n_ref[...])
        dk = _rope(dkv[:, :kvw], cos_ref[...], sin_ref[...], -1.0)
        dproj_ref[...] = jnp.concatenate([dq_ref[...], dk, dkv[:, kvw:], d_c, d_b, d_u], axis=1).astype(BF16)

    prev8 = lambda i: jnp.maximum(i * (tb // 8) - 1, 0)
    next8 = lambda i: jnp.minimum((i + 1) * (tb // 8), s // 8 - 1)
    nxt = lambda i: jnp.minimum(i + 1, nb - 1)
    return pl.pallas_call(
        body, name="dproj_assemble", grid=(nb,),
        in_specs=[pl.BlockSpec((tb, blk_w), lambda i: (i, 1)),
                  pl.BlockSpec((tb, blk_w), lambda i: (i, 2)),
                  pl.BlockSpec((8, blk_w), lambda i: (prev8(i), 1)),
                  pl.BlockSpec((8, blk_w), lambda i: (prev8(i), 2)),
                  pl.BlockSpec((8, blk_w), lambda i: (next8(i), 1)),
                  pl.BlockSpec((8, blk_w), lambda i: (next8(i), 2)),
                  pl.BlockSpec((tb, cw), lambda i: (i, 1)),
                  pl.BlockSpec((8, cw), lambda i: (next8(i), 1)),
                  pl.BlockSpec((tb, cw), lambda i: (i, 0)),
                  pl.BlockSpec((tb, 2 * kvw), lambda i: (i, 0)),
                  pl.BlockSpec((tb, 2 * kvw), lambda i: (nxt(i), 0)),
                  pl.BlockSpec((tb, kvw), lambda i: (i, 0)),
                  pl.BlockSpec((tb, kvw), lambda i: (i, 0)),
                  _VMEM],
        out_specs=[pl.BlockSpec((tb, in_w), lambda i: (i, 0)),
                   pl.BlockSpec((3, cw), lambda i: (0, 0))],
        out_shape=[jax.ShapeDtypeStruct((s, in_w), BF16), jax.ShapeDtypeStruct((3, cw), F32)],
        compiler_params=_params(("arbitrary",)),
    )(proj, proj, proj, proj, proj, proj, d_ac, d_ac, dq, dkv_cur, dkv_prev, cos_t, sin_t, cw_full)


def _dx(d_proj, w_in_g, dpre1, after):
    s, in_w = d_proj.shape
    ns, d, ncol = w_in_g.shape
    tm = min(TM, s)

    def body(dp_ref, w_ref, r_ref, after_ref, o_ref, acc):
        j = pl.program_id(1)
        _accumulate(acc, lambda: _dot_nt(dp_ref[...], w_ref[...]), j, ns)

        @pl.when(j == ns - 1)
        def _():
            o_ref[...] = acc[...] + ALPHA * r_ref[...]

    return pl.pallas_call(
        body, name="dx", grid=(s // tm, ns),
        in_specs=[pl.BlockSpec((tm, ncol), lambda i, j: (i, j)),
                  pl.BlockSpec((None, d, ncol), lambda i, j: (j, 0, 0)),
                  pl.BlockSpec((tm, d), lambda i, j: (i, 0)), _ANY],
        out_specs=pl.BlockSpec((None, tm, d), lambda i, j: (0, i, 0)),
        out_shape=jax.ShapeDtypeStruct((1, s, d), F32),
        scratch_shapes=[pltpu.VMEM((tm, d), F32)],
        compiler_params=_params(("parallel", "arbitrary")),
    )(d_proj, w_in_g, dpre1, after)


def kernel(x, positions, w_in, conv_w, sinks, g_attn, g_conv, w_out, ln1_g, ln1_b, w_gate, w_up, w_down, ln2_g, ln2_b, loss_target, m_w_in, m_conv_w, m_sinks, m_g_attn, m_g_conv, m_w_out, m_ln1_g, m_ln1_b, m_w_gate, m_w_up, m_w_down, m_ln2_g, m_ln2_b, v_w_in, v_conv_w, v_sinks, v_g_attn, v_g_conv, v_w_out, v_ln1_g, v_ln1_b, v_w_gate, v_w_up, v_w_down, v_ln2_g, v_ln2_b):
    s = x.shape[1]
    d = x.shape[2]

    chip_vec = _chip_id(lax.axis_index("x"), lax.axis_index("y")).astype(jnp.int32).reshape(1)
    wnames = ["w_in", "w_out", "w_gate", "w_up", "w_down"]
    cw_full = _allgather_conv_w(conv_w)
    buf_in = _cast_weight(w_in, chip_vec, cw_full, "cast_w_in")
    flight_in, token_in = _gather_start([buf_in], cw_full, "gather_start_w_in")
    bufs = [_cast_weight(w, chip_vec, token_in, "cast_" + nme)
            for w, nme in zip([w_out, w_gate, w_up, w_down], wnames[1:])]
    flights_rest, token = _gather_start(bufs, token_in, "gather_start_rest")
    flights = flight_in + flights_rest

    def gathered(i, after):
        send_sems, recv_sems, buf = flights[i]
        buf = _gather_wait(send_sems, recv_sems, buf, after, "gather_wait_" + wnames[i])
        return _sibling_fill(buf, "sibling_fill_" + wnames[i])

    g_ac = jnp.concatenate([g_attn, g_conv], axis=1)

    cos_t, sin_t = _rope_tables(positions.reshape(s, 1) + token[0:1, 0:1].astype(jnp.int32))
    w_in_g = gathered(0, cos_t)
    proj = _in_proj(x, w_in_g)
    send_sems, recv_sems, buf_out = flights[1]
    buf_out = _gather_wait(send_sems, recv_sems, buf_out, proj, "gather_wait_w_out")
    fill_out = _flight_start("fill_start_w_out", [buf_out], _fill_plan(1), 3, chip_vec)
    attn = _attention_fwd(_after(proj, fill_out[2][0]), cos_t, sin_t, sinks)
    mixed, ac, rstd_ac = _conv_norm(proj, attn, cw_full, g_ac)
    send_sems, recv_sems, buf_gate = flights[2]
    buf_gate = _gather_wait(send_sems, recv_sems, buf_gate, mixed, "gather_wait_w_gate")
    fill_gate = _flight_start("fill_start_w_gate", [buf_gate], _fill_plan(1), 3, chip_vec)
    (w_out_g,) = _flight_wait("fill_wait_w_out", fill_out, _fill_plan(1), fill_gate[2][0])
    w_out_full = w_out_g.reshape(d, d)
    xhat1, h1, rstd1 = _out_proj_ln(mixed, w_out_full, x, ln1_g, ln1_b)
    w_up_g = gathered(3, h1)
    (w_gate_g,) = _flight_wait("fill_wait_w_gate", fill_gate, _fill_plan(1), w_up_g)
    act, gate, up = _gate_up(h1, w_gate_g, w_up_g)
    w_down_full = gathered(4, act).reshape(-1, d)
    dpre2, loss_part, g_ln2_g, g_ln2_b = _down_ln_loss(act, w_down_full, xhat1, ln1_g, ln1_b, ln2_g, ln2_b, loss_target)

    cvec = lax.axis_index("c").astype(jnp.int32).reshape(1)

    def exchange_begin(parts, nme):
        bufs = []
        for part in parts:
            ns, r, cdim = part.shape
            bufs.extend([part, lax.empty((ns, r // 2, cdim), part.dtype)])
        return _flight_start("exchange_start_" + nme, bufs, _exchange_plan(len(parts)), len(parts), cvec)

    def exchange_end(flight, n_parts, after, nme):
        bufs = _flight_wait("exchange_wait_" + nme, flight, _exchange_plan(n_parts), after)
        return [(bufs[2 * w], bufs[2 * w + 1]) for w in range(n_parts)]

    def scatter_begin(part, got, nme):
        return _scatter_start(_add_halves(part, got, cvec, "add_halves_" + nme), "scatter_start_" + nme)

    d_gate, d_up = _dact_silu_bwd(dpre2, w_down_full, gate, up)
    p_down = _grad_rows(act, dpre2, d_gate, "grad_w_down")
    x_down = exchange_begin([p_down], "w_down")
    p_gate, p_up = _grad_cols(h1, [d_gate, d_up], x_down[2][0], "grad_w_gate_up")
    ((p_down, got),) = exchange_end(x_down, 1, p_gate, "w_down")
    f_down = scatter_begin(p_down, got, "w_down")
    x_gu = exchange_begin([_after(p_gate, f_down[2]), p_up], "w_gate_up")
    dpre1, g_ln1_g, g_ln1_b = _dh1_ln_bwd(d_gate, d_up, w_gate_g, w_up_g, dpre2, xhat1, rstd1, ln1_g, x_gu[2][0])
    (p_gate, got_gate), (p_up, got_up) = exchange_end(x_gu, 2, dpre1, "w_gate_up")
    f_gate = scatter_begin(p_gate, got_gate, "w_gate")
    f_up = scatter_begin(_after(p_up, f_gate[2]), got_up, "w_up")
    d_ac, g_g_ac = _dmixed_rms_bwd(_after(dpre1, f_up[2]), w_out_full, ac, rstd_ac, g_ac)
    p_out = _grad_rows(mixed, dpre1, d_ac, "grad_w_out")
    x_out = exchange_begin([p_out], "w_out")
    dq, dkv_cur, dkv_prev, g_sinks = _attention_bwd(proj, d_ac, cos_t, sin_t, sinks, x_out[2][0])
    ((p_out, got),) = exchange_end(x_out, 1, dq, "w_out")
    f_out = scatter_begin(p_out, got, "w_out")
    d_proj, g_conv_w = _dproj_assemble(proj, _after(d_ac, f_out[2]), dq, dkv_cur, dkv_prev, cos_t, sin_t, cw_full)
    (p_in,) = _grad_cols(x, [d_proj], d_proj, "grad_w_in", a_3d=True)
    x_in = exchange_begin([p_in], "w_in")
    grad_x = _dx(d_proj, w_in_g, dpre1, x_in[2][0])
    red = _allreduce_small(g_ln2_g, g_ln2_b, g_ln1_g, g_ln1_b, g_g_ac, g_conv_w, g_sinks, loss_part, grad_x)
    ((p_in, got),) = exchange_end(x_in, 1, red, "w_in")
    f_in = scatter_begin(p_in, got, "w_in")

    pos_vec = jnp.concatenate([chip_vec, cvec])
    shards = {"w_in": (w_in, m_w_in, v_w_in), "w_out": (w_out, m_w_out, v_w_out), "w_gate": (w_gate, m_w_gate, v_w_gate),
              "w_up": (w_up, m_w_up, v_w_up), "w_down": (w_down, m_w_down, v_w_down)}
    early = ["w_down", "w_gate", "w_up", "w_out"]
    after = f_in[2]
    completing = {}
    for nme, f in zip(early, [f_down, f_gate, f_up, f_out]):
        sums, land = _scatter_wait(*f, after, "scatter_wait_" + nme)
        completing[nme] = _flight_start("complete_start_" + nme, [sums, land], _complete_plan(1), 4, cvec)
        after = completing[nme][2][1]
    big = {}
    for nme in early:
        sums, land = _flight_wait("complete_wait_" + nme, completing[nme], _complete_plan(1), after)
        big[nme] = _adamw_shard(*shards[nme], land, sums, pos_vec, "adamw_" + nme)
        after = big[nme][0]
    sums, land = _scatter_wait(*f_in, after, "scatter_wait_w_in")
    (land,) = _complete_chip_sums([sums], [land])
    big["w_in"] = _adamw_shard(*shards["w_in"], land, sums, pos_vec, "adamw_w_in")
    small = _adamw_small(red, {
        "sinks": (sinks, m_sinks, v_sinks), "g_attn": (g_attn, m_g_attn, v_g_attn),
        "g_conv": (g_conv, m_g_conv, v_g_conv), "ln1_g": (ln1_g, m_ln1_g, v_ln1_g),
        "ln1_b": (ln1_b, m_ln1_b, v_ln1_b), "ln2_g": (ln2_g, m_ln2_g, v_ln2_g),
        "ln2_b": (ln2_b, m_ln2_b, v_ln2_b), "conv_w": (conv_w, m_conv_w, v_conv_w)})
    res = {**big, **small}
    order = ["w_in", "conv_w", "sinks", "g_attn", "g_conv", "w_out", "ln1_g", "ln1_b", "w_gate", "w_up", "w_down",
             "ln2_g", "ln2_b"]
    loss = red[6, d // 2 + 128]
    return (loss, grad_x, *[res[n][0] for n in order], *[res[n][1] for n in order],
            *[res[n][2] for n in order], *[res[n][3] for n in order])
```

```python
import functools

import numpy as np
import jax
import jax.numpy as jnp
from jax import lax
from jax.experimental import pallas as pl
from jax.experimental.pallas import tpu as pltpu

F32 = jnp.float32
BF16 = jnp.bfloat16
MESH = pl.DeviceIdType.MESH

HEAD_DIM = 64
N_KV_HEADS = 4
GROUP = 4
WINDOW = 128
ROT_DIM = 16
ROPE_THETA = 500000.0
ATTN_SCALE = HEAD_DIM ** -0.5
ALPHA = 2.0 ** 0.25
LN_EPS = 1e-5
RMS_EPS = 1e-6
ADAM_LR = 0.001
ADAM_B1 = 0.9
ADAM_B2 = 0.999
ADAM_EPS = 1e-08
ADAM_WD = 0.01
ADAM_STEP = 10
N_CHIPS = 4
NEG_BIG = -1e30

V7X_VMEM_BYTES = 64 * 1024 * 1024
VMEM_LIMIT = V7X_VMEM_BYTES - 6 * 1024 * 1024

TM = 512
TK_TOK = 1024
TB_CONV = 256
TR_ELT = 256
ROW_CHUNK = 128


def _params(sem):
    return pltpu.CompilerParams(dimension_semantics=sem, vmem_limit_bytes=VMEM_LIMIT)


def _row_tile(rows, target):
    best = None
    for t in range(16, min(rows, target) + 1, 16):
        if rows % t == 0:
            best = t
    assert best is not None, (rows, target)
    return best


def _dot(a, b):
    return jnp.dot(a, b, preferred_element_type=F32)


def _dot_nt(a, b):
    return lax.dot_general(a, b, (((1,), (1,)), ((), ())), preferred_element_type=F32)


def _dot_tn(a, b):
    return lax.dot_general(a, b, (((0,), (0,)), ((), ())), preferred_element_type=F32)


def _mesh_pos():
    x, y, c = lax.axis_index("x"), lax.axis_index("y"), lax.axis_index("c")
    chips = [(1 - x, y), (x, 1 - y), (1 - x, 1 - y)]
    return x, y, c, chips


def _chip_id(px, py):
    return 2 * px + py


def _rope(t, cos, sgn_sin, sign):
    w = t.shape[1]
    lane = lax.broadcasted_iota(jnp.int32, t.shape, 1) & (HEAD_DIM - 1)
    partner = jnp.where(lane < ROT_DIM // 2, pltpu.roll(t, w - ROT_DIM // 2, 1), pltpu.roll(t, ROT_DIM // 2, 1))
    return t * cos + sign * (partner * sgn_sin)


def _tile_lanes(t, n):
    return jnp.concatenate([t] * n, axis=1)


def _sigmoid(g):
    return 1.0 / (1.0 + jnp.exp(-g))


def _for_row_chunks(n_rows, fn):
    def step(r, carry):
        fn(pl.ds(pl.multiple_of(r * ROW_CHUNK, ROW_CHUNK), ROW_CHUNK))
        return carry

    lax.fori_loop(0, n_rows // ROW_CHUNK, step, 0)


def _accumulate(acc, make_val, k, nk):
    if nk == 1:
        acc[...] = make_val()
        return

    @pl.when(k == 0)
    def _():
        acc[...] = jnp.zeros_like(acc)

    acc[...] += make_val()


def _ln_fwd(pre):
    mu = jnp.mean(pre, axis=-1, keepdims=True)
    cen = pre - mu
    var = jnp.mean(cen * cen, axis=-1, keepdims=True)
    rstd = lax.rsqrt(var + LN_EPS)
    return cen * rstd, rstd


def _ln_bwd(dy, xhat, rstd, g):
    dxhat = dy * g
    m1 = jnp.mean(dxhat, axis=-1, keepdims=True)
    m2 = jnp.mean(dxhat * xhat, axis=-1, keepdims=True)
    return rstd * (dxhat - m1 - xhat * m2)


def _cast_weight(w, chip_vec, after, name, col_block=0, n_col_blocks=1):
    _, r, c = w.shape
    tr = _row_tile(r, TR_ELT)

    def body(chip_ref, w_ref, after_ref, o_ref):
        o_ref[...] = w_ref[...].astype(BF16)

    grid_spec = pltpu.PrefetchScalarGridSpec(
        num_scalar_prefetch=1, grid=(r // tr,),
        in_specs=[pl.BlockSpec((None, tr, c), lambda i, chip_ref: (0, i, 0)), _ANY],
        out_specs=pl.BlockSpec((None, tr, c), lambda i, chip_ref: (chip_ref[0], i, col_block)))
    return pl.pallas_call(
        body, name=name, grid_spec=grid_spec,
        out_shape=jax.ShapeDtypeStruct((N_CHIPS, r, n_col_blocks * c), BF16),
        input_output_aliases={2: 0} if col_block else {},
        compiler_params=_params(("parallel",)),
    )(chip_vec, w, after)


_HBM = pl.BlockSpec(memory_space=pltpu.HBM)
_VMEM = pl.BlockSpec(memory_space=pltpu.VMEM)


_SEM = pl.BlockSpec(memory_space=pltpu.SEMAPHORE)
_ANY = pl.BlockSpec(memory_space=pl.ANY)
_EFFECT = pltpu.SideEffectType.DATAFLOW_SIDE_EFFECTING


def _chip_copy(buf, k, chip_of_src, half_rows, send_sems, recv_sems, to):
    part = buf.at[chip_of_src, half_rows]
    return pltpu.make_async_remote_copy(
        src_ref=part, dst_ref=part, send_sem=send_sems.at[k], recv_sem=recv_sems.at[k], device_id=to, device_id_type=MESH)


def _half_rows(buf, which):
    hr = buf.shape[1] // 2
    return pl.ds(which * hr, hr)


def _after(value, dep):
    return lax.optimization_barrier((value, dep))[0]


def _flight_start(name, bufs, plan, n_sems, after):
    n = len(bufs)

    def body(*refs):
        sends, _ = plan(refs[:n], refs[n + 1], refs[n + 2])
        for cp in sends:
            cp.start()

    outs = pl.pallas_call(
        body, name=name,
        in_specs=[_HBM] * n + [_ANY], out_specs=[_SEM, _SEM] + [_HBM] * n,
        out_shape=[pltpu.SemaphoreType.DMA((n_sems,))] * 2 + [pltpu.HBM(b.shape, b.dtype) for b in bufs],
        input_output_aliases={i: 2 + i for i in range(n)},
        compiler_params=pltpu.CompilerParams(has_side_effects=_EFFECT),
    )(*[pltpu.with_memory_space_constraint(b, pltpu.HBM) for b in bufs], after)
    return outs[0], outs[1], list(outs[2:])


def _flight_wait(name, flight, plan, after):
    send_sems, recv_sems, bufs = flight
    n = len(bufs)

    def body(*refs):
        sends, recvs = plan(refs[:n], refs[n], refs[n + 1])
        for cp in sends:
            cp.wait_send()
        for cp in recvs:
            cp.wait_recv()

    outs = pl.pallas_call(
        body, name=name,
        in_specs=[_HBM] * n + [_SEM, _SEM, _ANY], out_specs=[_HBM] * n,
        out_shape=[pltpu.HBM(b.shape, b.dtype) for b in bufs],
        input_output_aliases={i: i for i in range(n)},
        compiler_params=pltpu.CompilerParams(has_side_effects=_EFFECT),
    )(*bufs, send_sems, recv_sems, after)
    return list(outs)


def _fill_plan(n_bufs):
    def plan(refs, send_sems, recv_sems):
        x, y, c, chips = _mesh_pos()
        sibling = (x, y, 1 - c)
        sends, recvs = [], []
        for w in range(n_bufs):
            for k, chip in enumerate(chips):
                slot = _chip_id(*chip)
                sends.append(_chip_copy(refs[w], 3 * w + k, slot, _half_rows(refs[w], c), send_sems, recv_sems, sibling))
                recvs.append(_chip_copy(refs[w], 3 * w + k, slot, _half_rows(refs[w], 1 - c), send_sems, recv_sems,
                                        sibling))
        return sends, recvs
    return plan


def _exchange_plan(n_parts):
    def plan(refs, send_sems, recv_sems):
        x, y, c, _ = _mesh_pos()
        copies = []
        for w in range(n_parts):
            part, got = refs[2 * w], refs[2 * w + 1]
            hr = got.shape[1]
            copies.append(pltpu.make_async_remote_copy(
                src_ref=part.at[:, pl.ds((1 - c) * hr, hr)], dst_ref=got, send_sem=send_sems.at[w],
                recv_sem=recv_sems.at[w], device_id=(x, y, 1 - c), device_id_type=MESH))
        return copies, copies
    return plan


def _gather_start(bufs, after, name):
    n = len(bufs)

    def body(*refs):
        ins = refs[:n]
        sends, recvs = refs[n + 1:2 * n + 1], refs[2 * n + 1:3 * n + 1]
        token = refs[4 * n + 1]
        x, y, c, chips = _mesh_pos()
        me = _chip_id(x, y)
        for w in range(n):
            for k, chip in enumerate(chips):
                _chip_copy(ins[w], k, me, _half_rows(ins[w], c), sends[w], recvs[w], (*chip, c)).start()
        token[...] = jnp.zeros_like(token)

    outs = pl.pallas_call(
        body, name=name,
        in_specs=[_HBM] * n + [_ANY],
        out_specs=[_SEM] * (2 * n) + [_HBM] * n + [_VMEM],
        out_shape=[pltpu.SemaphoreType.DMA((3,))] * (2 * n) + [pltpu.HBM(b.shape, b.dtype) for b in bufs]
        + [jax.ShapeDtypeStruct((8, 128), F32)],
        input_output_aliases={w: 2 * n + w for w in range(n)},
        compiler_params=pltpu.CompilerParams(has_side_effects=_EFFECT),
    )(*[pltpu.with_memory_space_constraint(b, pltpu.HBM) for b in bufs], after)
    return [(outs[w], outs[n + w], outs[2 * n + w]) for w in range(n)], outs[3 * n]


def _gather_wait(send_sems, recv_sems, buf, after, name):
    def body(buf_ref, send_ref, recv_ref, after_ref, out_ref):
        x, y, c, chips = _mesh_pos()
        me = _chip_id(x, y)
        for k, chip in enumerate(chips):
            _chip_copy(buf_ref, k, me, _half_rows(buf_ref, c), send_ref, recv_ref, (*chip, c)).wait_send()
        for k, chip in enumerate(chips):
            _chip_copy(buf_ref, k, _chip_id(*chip), _half_rows(buf_ref, c), send_ref, recv_ref, (*chip, c)).wait_recv()

    return pl.pallas_call(
        body, name=name,
        in_specs=[_HBM, _SEM, _SEM, _ANY], out_specs=_HBM,
        out_shape=pltpu.HBM(buf.shape, buf.dtype),
        input_output_aliases={0: 0},
        compiler_params=pltpu.CompilerParams(has_side_effects=_EFFECT),
    )(buf, send_sems, recv_sems, after)


def _sibling_fill(buf, name, own_too=False):
    n_copies = 4 if own_too else 3

    def body(buf_ref, out_ref, send_sems, recv_sems):
        x, y, c, chips = _mesh_pos()
        sibling = (x, y, 1 - c)
        slots = [_chip_id(*chip) for chip in chips] + ([_chip_id(x, y)] if own_too else [])
        copies = []
        for k, slot in enumerate(slots):
            cp = _chip_copy(out_ref, k, slot, _half_rows(out_ref, c), send_sems, recv_sems, sibling)
            cp.start()
            copies.append(cp)
        for k, slot in enumerate(slots):
            _chip_copy(out_ref, k, slot, _half_rows(out_ref, 1 - c), send_sems, recv_sems, sibling).wait_recv()
        for cp in copies:
            cp.wait_send()

    return pl.pallas_call(
        body, name=name,
        in_specs=[_HBM], out_specs=_HBM,
        out_shape=jax.ShapeDtypeStruct(buf.shape, buf.dtype),
        input_output_aliases={0: 0},
        scratch_shapes=[pltpu.SemaphoreType.DMA((n_copies,)), pltpu.SemaphoreType.DMA((n_copies,))],
    )(buf)


def _allgather_conv_w(cw):
    _, kw, cs = cw.shape

    def body(cw_ref, out_ref, send_sems, recv_sems):
        x, y, c, chips = _mesh_pos()
        me = _chip_id(x, y)
        out_ref[pl.ds(me, 1)] = cw_ref[...]
        copies = []
        for k, chip in enumerate(chips):
            cp = pltpu.make_async_remote_copy(
                src_ref=cw_ref.at[0], dst_ref=out_ref.at[me], send_sem=send_sems.at[k], recv_sem=recv_sems.at[k],
                device_id=(*chip, c), device_id_type=MESH)
            cp.start()
            copies.append(cp)
        for k, chip in enumerate(chips):
            pltpu.make_async_remote_copy(
                src_ref=cw_ref.at[0], dst_ref=out_ref.at[_chip_id(*chip)], send_sem=send_sems.at[k],
                recv_sem=recv_sems.at[k], device_id=(*chip, c), device_id_type=MESH).wait_recv()
        for cp in copies:
            cp.wait_send()

    return pl.pallas_call(
        body, name="allgather_conv_w",
        in_specs=[_VMEM], out_specs=_VMEM,
        out_shape=jax.ShapeDtypeStruct((N_CHIPS, kw, cs), F32),
        scratch_shapes=[pltpu.SemaphoreType.DMA((3,)), pltpu.SemaphoreType.DMA((3,))],
    )(cw)


def _exchange_halves(parts, after, name):
    n = len(parts)
    shapes = [p.shape for p in parts]

    def body(*refs):
        ins, outs = refs[:n], refs[n + 1:2 * n + 1]
        send_sems, recv_sems = refs[2 * n + 1:]
        x, y, c, _ = _mesh_pos()
        copies = []
        for w in range(n):
            hr = shapes[w][1] // 2
            cp = pltpu.make_async_remote_copy(
                src_ref=ins[w].at[:, pl.ds((1 - c) * hr, hr)], dst_ref=outs[w],
                send_sem=send_sems.at[w], recv_sem=recv_sems.at[w],
                device_id=(x, y, 1 - c), device_id_type=MESH)
            cp.start()
            copies.append(cp)
        for cp in copies:
            cp.wait()

    return pl.pallas_call(
        body, name=name,
        in_specs=[_HBM] * n + [_ANY], out_specs=[_HBM] * n,
        out_shape=[jax.ShapeDtypeStruct((s[0], s[1] // 2, s[2]), BF16) for s in shapes],
        scratch_shapes=[pltpu.SemaphoreType.DMA((n,)), pltpu.SemaphoreType.DMA((n,))],
    )(*parts, after)


def _add_halves(part, got, cvec, name):
    ns, r, cdim = part.shape
    hr = r // 2
    tr = _row_tile(hr, TR_ELT)
    nblk = hr // tr

    def body(c_ref, a_ref, b_ref, o_ref):
        o_ref[...] = (a_ref[...].astype(F32) + b_ref[...].astype(F32)).astype(BF16)

    grid_spec = pltpu.PrefetchScalarGridSpec(
        num_scalar_prefetch=1, grid=(ns, nblk),
        in_specs=[pl.BlockSpec((None, tr, cdim), lambda s, i, c_ref: (s, c_ref[0] * nblk + i, 0)),
                  pl.BlockSpec((None, tr, cdim), lambda s, i, c_ref: (s, i, 0))],
        out_specs=pl.BlockSpec((None, tr, cdim), lambda s, i, c_ref: (s, i, 0)))
    return pl.pallas_call(
        body, name=name, grid_spec=grid_spec,
        out_shape=jax.ShapeDtypeStruct((ns, hr, cdim), BF16),
        compiler_params=_params(("parallel", "parallel")),
    )(cvec, part, got)


def _scatter_copy(sums_ref, land_ref, k, src_slot, dst_slot, c, send_sems, recv_sems, to):
    return pltpu.make_async_remote_copy(
        src_ref=sums_ref.at[src_slot], dst_ref=land_ref.at[dst_slot, _half_rows(land_ref, c)],
        send_sem=send_sems.at[k], recv_sem=recv_sems.at[k], device_id=to, device_id_type=MESH)


def _scatter_start(sums, name):
    ns, hr, cdim = sums.shape
    land = lax.empty((ns, 2 * hr, cdim), sums.dtype)

    def body(sums_ref, land_ref, send_sems, recv_sems, sums_thru, land_thru):
        x, y, c, chips = _mesh_pos()
        me = _chip_id(x, y)
        for k, chip in enumerate(chips):
            _scatter_copy(sums_ref, land_ref, k, _chip_id(*chip), me, c, send_sems, recv_sems, (*chip, c)).start()

    return pl.pallas_call(
        body, name=name,
        in_specs=[_HBM, _HBM], out_specs=[_SEM, _SEM, _HBM, _HBM],
        out_shape=[pltpu.SemaphoreType.DMA((3,)), pltpu.SemaphoreType.DMA((3,)),
                   pltpu.HBM(sums.shape, sums.dtype), pltpu.HBM(land.shape, land.dtype)],
        input_output_aliases={0: 2, 1: 3},
        compiler_params=pltpu.CompilerParams(has_side_effects=_EFFECT),
    )(pltpu.with_memory_space_constraint(sums, pltpu.HBM), pltpu.with_memory_space_constraint(land, pltpu.HBM))


def _scatter_wait(send_sems, recv_sems, sums, land, after, name):
    def body(sums_ref, land_ref, send_ref, recv_ref, after_ref, sums_out, land_out):
        x, y, c, chips = _mesh_pos()
        me = _chip_id(x, y)
        for k, chip in enumerate(chips):
            _scatter_copy(sums_ref, land_ref, k, _chip_id(*chip), me, c, send_ref, recv_ref, (*chip, c)).wait_send()
        for k, chip in enumerate(chips):
            _scatter_copy(sums_ref, land_ref, k, me, _chip_id(*chip), c, send_ref, recv_ref, (*chip, c)).wait_recv()

    return pl.pallas_call(
        body, name=name,
        in_specs=[_HBM, _HBM, _SEM, _SEM, _ANY], out_specs=[_HBM, _HBM],
        out_shape=[pltpu.HBM(sums.shape, sums.dtype), pltpu.HBM(land.shape, land.dtype)],
        input_output_aliases={0: 0, 1: 1},
        compiler_params=pltpu.CompilerParams(has_side_effects=_EFFECT),
    )(sums, land, send_sems, recv_sems, after)


def _complete_plan(n_weights):
    def plan(refs, send_sems, recv_sems):
        x, y, c, chips = _mesh_pos()
        me = _chip_id(x, y)
        sibling = (x, y, 1 - c)
        sends, recvs = [], []
        for w in range(n_weights):
            sums, land = refs[2 * w], refs[2 * w + 1]
            sends.append(_scatter_copy(sums, land, 4 * w + 3, me, me, c, send_sems, recv_sems, sibling))
            recvs.append(_scatter_copy(sums, land, 4 * w + 3, me, me, 1 - c, send_sems, recv_sems, sibling))
            for k, chip in enumerate(chips):
                slot = _chip_id(*chip)
                sends.append(_chip_copy(land, 4 * w + k, slot, _half_rows(land, c), send_sems, recv_sems, sibling))
                recvs.append(_chip_copy(land, 4 * w + k, slot, _half_rows(land, 1 - c), send_sems, recv_sems, sibling))
        return sends, recvs
    return plan


def _complete_chip_sums(sums, lands):
    n = len(sums)

    def body(*refs):
        sums_refs, outs = refs[:n], refs[2 * n:3 * n]
        send_sems, recv_sems = refs[3 * n:]
        x, y, c, chips = _mesh_pos()
        me = _chip_id(x, y)
        sibling = (x, y, 1 - c)
        slots = [_chip_id(*chip) for chip in chips]
        sent = []
        for w in range(n):
            out = outs[w]
            cp = _scatter_copy(sums_refs[w], out, 3, me, me, c, send_sems.at[w], recv_sems.at[w], sibling)
            cp.start()
            sent.append(cp)
            for k, slot in enumerate(slots):
                cp = _chip_copy(out, k, slot, _half_rows(out, c), send_sems.at[w], recv_sems.at[w], sibling)
                cp.start()
                sent.append(cp)
        for w in range(n):
            out = outs[w]
            _scatter_copy(sums_refs[w], out, 3, me, me, 1 - c, send_sems.at[w], recv_sems.at[w], sibling).wait_recv()
            for k, slot in enumerate(slots):
                _chip_copy(out, k, slot, _half_rows(out, 1 - c), send_sems.at[w], recv_sems.at[w], sibling).wait_recv()
        for cp in sent:
            cp.wait_send()

    return pl.pallas_call(
        body, name="complete_chip_sums",
        in_specs=[_HBM] * (2 * n), out_specs=[_HBM] * n,
        out_shape=[jax.ShapeDtypeStruct(b.shape, b.dtype) for b in lands],
        input_output_aliases={n + w: w for w in range(n)},
        scratch_shapes=[pltpu.SemaphoreType.DMA((n, 4)), pltpu.SemaphoreType.DMA((n, 4))],
    )(*sums, *lands)


SMALL_ROWS = 8


def _allreduce_small(gl2g, gl2b, gl1g, gl1b, g_ac, gcw, gsink, loss, after):
    d = gl2g.shape[1]
    hd = d // 2
    nq = gsink.shape[1]

    def body(a_ref, b_ref, c_ref, d_ref, e_ref, cw_ref, sk_ref, ls_ref, after_ref, out_ref, mine, gath, send_sems,
             recv_sems):
        x, y, c, _ = _mesh_pos()
        me = 4 * x + 2 * y + c
        mine[...] = jnp.zeros_like(mine)
        mine[0:1, :] = a_ref[...]
        mine[1:2, :] = b_ref[...]
        mine[2:3, :] = c_ref[...]
        mine[3:4, :] = d_ref[...]
        mine[4:5, :] = e_ref[...]
        mine[5:6, 0:hd] = cw_ref[0:1, :]
        mine[5:6, hd:d] = cw_ref[1:2, :]
        mine[6:7, 0:hd] = cw_ref[2:3, :]
        mine[6:7, hd:hd + nq] = sk_ref[...]
        mine[6:7, hd + 128:hd + 256] = ls_ref[...]
        gath[pl.ds(me, 1)] = mine[...][None]
        copies = []
        for r in range(1, 8):
            peer = ((1 - x) if r & 4 else x, (1 - y) if r & 2 else y, (1 - c) if r & 1 else c)
            cp = pltpu.make_async_remote_copy(
                src_ref=mine, dst_ref=gath.at[me], send_sem=send_sems.at[r - 1], recv_sem=recv_sems.at[r - 1],
                device_id=peer, device_id_type=MESH)
            cp.start()
            copies.append(cp)
        for r in range(1, 8):
            peer = ((1 - x) if r & 4 else x, (1 - y) if r & 2 else y, (1 - c) if r & 1 else c)
            peer_id = 4 * peer[0] + 2 * peer[1] + peer[2]
            pltpu.make_async_remote_copy(
                src_ref=mine, dst_ref=gath.at[peer_id], send_sem=send_sems.at[r - 1], recv_sem=recv_sems.at[r - 1],
                device_id=peer, device_id_type=MESH).wait_recv()
        for cp in copies:
            cp.wait_send()
        total = gath[0]
        for dev in range(1, 8):
            total = total + gath[dev]
        out_ref[...] = total

    return pl.pallas_call(
        body, name="allreduce_small",
        in_specs=[_VMEM] * 8 + [_ANY], out_specs=_VMEM,
        out_shape=jax.ShapeDtypeStruct((SMALL_ROWS, d), F32),
        scratch_shapes=[pltpu.VMEM((SMALL_ROWS, d), F32), pltpu.VMEM((8, SMALL_ROWS, d), F32),
                        pltpu.SemaphoreType.DMA((7,)), pltpu.SemaphoreType.DMA((7,))],
    )(gl2g, gl2b, gl1g, gl1b, g_ac, gcw, gsink, loss, after)


def _adamw(w, g, m, v):
    m = ADAM_B1 * m + (1.0 - ADAM_B1) * g
    v = ADAM_B2 * v + (1.0 - ADAM_B2) * (g * g)
    m_hat = m / (1.0 - ADAM_B1 ** ADAM_STEP)
    v_hat = v / (1.0 - ADAM_B2 ** ADAM_STEP)
    delta = -ADAM_LR * (m_hat / (jnp.sqrt(v_hat) + ADAM_EPS) + ADAM_WD * w)
    return delta, m, v


def _adamw_shard(w, m, v, land, own, pos_vec, name, col_block=0):
    _, r, c = w.shape
    hr = r // 2
    tr = _row_tile(hr, TR_ELT)
    nh = hr // tr

    def body(pos_ref, w_ref, m_ref, v_ref, l0, l1, l2, l3, own_ref, g_out, d_out, m_out, v_out):
        i = pl.program_id(0)
        mine = (i // nh) == pos_ref[1]
        own_blk = own_ref[...].astype(F32)
        g = None
        for s, l_ref in enumerate([l0, l1, l2, l3]):
            term = jnp.where(mine & (pos_ref[0] == s), own_blk, l_ref[...].astype(F32))
            g = term if g is None else g + term
        delta, nm, nv = _adamw(w_ref[...], g, m_ref[...], v_ref[...])
        g_out[...] = g
        d_out[...] = delta
        m_out[...] = nm
        v_out[...] = nv

    def land_spec(s):
        def index(i, pos_ref):
            skip = (pos_ref[0] == s) & ((i // nh) == pos_ref[1])
            return (s, jnp.where(skip, (i + nh) % (2 * nh), i), col_block)
        return pl.BlockSpec((None, tr, c), index)

    blk = pl.BlockSpec((None, tr, c), lambda i, pos_ref: (0, i, 0))
    grid_spec = pltpu.PrefetchScalarGridSpec(
        num_scalar_prefetch=1, grid=(2 * nh,),
        in_specs=[blk, blk, blk] + [land_spec(s) for s in range(N_CHIPS)]
        + [pl.BlockSpec((None, tr, c), lambda i, pos_ref: (pos_ref[0], i % nh, col_block))],
        out_specs=[blk] * 4)
    return pl.pallas_call(
        body, name=name, grid_spec=grid_spec,
        out_shape=[jax.ShapeDtypeStruct((1, r, c), F32)] * 4,
        compiler_params=_params(("parallel",)),
    )(pos_vec, w, m, v, land, land, land, land, own)


def _adamw_small(red, params):
    names = ["sinks", "g_attn", "g_conv", "ln1_g", "ln1_b", "ln2_g", "ln2_b", "conv_w"]
    d = red.shape[1]
    hd = d // 2
    flat = []
    for nme in names:
        flat.extend(params[nme])
    nq = params["sinks"][0].shape[1]
    cs = params["conv_w"][0].shape[2]

    def body(*refs):
        red_ref = refs[0]
        ins = refs[1:1 + 3 * len(names)]
        outs = refs[1 + 3 * len(names):]
        x, y, _, _ = _mesh_pos()
        me = _chip_id(x, y)

        def conv_tap(row, base):
            picked = red_ref[row:row + 1, base:base + cs]
            for s in range(1, N_CHIPS):
                picked = jnp.where(me == s, red_ref[row:row + 1, base + s * cs:base + (s + 1) * cs], picked)
            return picked

        grads = {
            "sinks": red_ref[6:7, hd:hd + nq],
            "g_attn": red_ref[4:5, 0:hd],
            "g_conv": red_ref[4:5, hd:d],
            "ln1_g": red_ref[2:3, :],
            "ln1_b": red_ref[3:4, :],
            "ln2_g": red_ref[0:1, :],
            "ln2_b": red_ref[1:2, :],
        }
        for i, nme in enumerate(names):
            w_ref, m_ref, v_ref = ins[3 * i:3 * i + 3]
            g_out, d_out, m_out, v_out = outs[4 * i:4 * i + 4]
            if nme == "conv_w":
                for tap, (row, base) in enumerate([(5, 0), (5, hd), (6, 0)]):
                    g = conv_tap(row, base)
                    delta, nm, nv = _adamw(w_ref[0, tap:tap + 1, :], g, m_ref[0, tap:tap + 1, :], v_ref[0, tap:tap + 1, :])
                    g_out[0, tap:tap + 1, :] = g
                    d_out[0, tap:tap + 1, :] = delta
                    m_out[0, tap:tap + 1, :] = nm
                    v_out[0, tap:tap + 1, :] = nv
            else:
                g = grads[nme]
                delta, nm, nv = _adamw(w_ref[...], g, m_ref[...], v_ref[...])
                g_out[...] = g
                d_out[...] = delta
                m_out[...] = nm
                v_out[...] = nv

    out_shape = []
    for nme in names:
        out_shape.extend([jax.ShapeDtypeStruct(params[nme][0].shape, F32)] * 4)
    outs = pl.pallas_call(
        body, name="adamw_small",
        in_specs=[_VMEM] * (1 + len(flat)), out_specs=[_VMEM] * len(out_shape),
        out_shape=out_shape,
    )(red, *flat)
    return {nme: tuple(outs[4 * i:4 * i + 4]) for i, nme in enumerate(names)}


def _rope_tables(pos_col):
    s = pos_col.shape[0]
    w = N_KV_HEADS * HEAD_DIM
    tb = min(512, s)
    inv_freq = (ROPE_THETA ** (-np.arange(0, ROT_DIM, 2, dtype=np.float32) / ROT_DIM)).astype(np.float32)

    def body(pos_ref, cos_ref, sin_ref):
        pos = pos_ref[...].astype(F32)
        lane = lax.broadcasted_iota(jnp.int32, (tb, PAIR), 1) & (HEAD_DIM - 1)
        fidx = lane & (ROT_DIM // 2 - 1)
        inv = jnp.zeros((tb, PAIR), F32)
        for k in range(ROT_DIM // 2):
            inv = jnp.where(fidx == k, float(inv_freq[k]), inv)
        ang = pos * inv
        rot = lane < ROT_DIM
        sin_v = jnp.sin(ang)
        cos_ref[...] = _tile_lanes(jnp.where(rot, jnp.cos(ang), 1.0), w // PAIR)
        sin_ref[...] = _tile_lanes(jnp.where(lane < ROT_DIM // 2, -sin_v, jnp.where(rot, sin_v, 0.0)), w // PAIR)

    return pl.pallas_call(
        body, name="rope_tables", grid=(s // tb,),
        in_specs=[pl.BlockSpec((tb, 1), lambda i: (i, 0))],
        out_specs=[pl.BlockSpec((tb, w), lambda i: (i, 0))] * 2,
        out_shape=[jax.ShapeDtypeStruct((s, w), F32)] * 2,
        compiler_params=_params(("parallel",)),
    )(pos_col)


def _in_proj(x, w_in_g):
    _, s, d = x.shape
    ns, _, ncol = w_in_g.shape
    tm = min(2 * TM, s)

    def body(x_ref, w_ref, o_ref):
        o_ref[...] = _dot(x_ref[...].astype(BF16), w_ref[...])

    return pl.pallas_call(
        body, name="in_proj", grid=(s // tm, ns),
        in_specs=[pl.BlockSpec((None, tm, d), lambda i, j: (0, i, 0)),
                  pl.BlockSpec((None, d, ncol), lambda i, j: (j, 0, 0))],
        out_specs=pl.BlockSpec((tm, ncol), lambda i, j: (i, j)),
        out_shape=jax.ShapeDtypeStruct((s, ns * ncol), F32),
        compiler_params=_params(("parallel", "arbitrary")),
    )(x, w_in_g)


PAIR = 2 * HEAD_DIM
KEYS = 2 * WINDOW


def _pair_operand(t_all, h):
    col = (h // 2) * PAIR
    lane = lax.broadcasted_iota(jnp.int32, (KEYS, PAIR), 1)
    own_low = h % 2 == 0
    mine = jnp.where((lane < HEAD_DIM) if own_low else (lane >= HEAD_DIM), t_all[:, col:col + PAIR], 0.0)
    other = pltpu.roll(mine, HEAD_DIM, 1)
    low, high = (mine, other) if own_low else (other, mine)
    return jnp.concatenate([low, high], axis=0).astype(BF16)


def _pair_grad(acc, h):
    lane = lax.broadcasted_iota(jnp.int32, (KEYS, PAIR), 1)
    low = jnp.where(lane < HEAD_DIM, acc[:KEYS], 0.0)
    high = jnp.where(lane >= HEAD_DIM, acc[KEYS:], 0.0)
    if h % 2 == 0:
        return low + pltpu.roll(high, HEAD_DIM, 1)
    return high + pltpu.roll(low, HEAD_DIM, 1)


N_PAIRS = N_KV_HEADS * GROUP // 2


def _all_probs(q, kk2s, first, sinks_ref):
    assert ATTN_SCALE == 0.125
    q = q * ATTN_SCALE
    qps, scores = [], []
    for pair in range(N_PAIRS):
        qp = q[:, pair * PAIR:(pair + 1) * PAIR].astype(BF16)
        qps.append(qp)
        scores.append(_dot_nt(qp, kk2s[pair // (GROUP // 2)]))
    qi = lax.broadcasted_iota(jnp.int32, (WINDOW, 2 * KEYS), 0)
    kj = lax.broadcasted_iota(jnp.int32, (WINDOW, 2 * KEYS), 1) & (KEYS - 1)
    rel = qi + WINDOW - kj
    valid = (rel >= 0) & (rel < WINDOW) & jnp.logical_not(first & (kj < WINDOW))
    bias = jnp.where(valid, 0.0, NEG_BIG)
    s = (jnp.stack(scores, axis=0) + bias[None]).reshape(N_PAIRS * WINDOW, 2 * KEYS)
    probs, p_sinks = [], []
    for t in range(2):
        st = s[:, t * KEYS:(t + 1) * KEYS]
        sink = jnp.concatenate([jnp.broadcast_to(sinks_ref[0:1, 2 * pair + t:2 * pair + t + 1], (WINDOW, 1))
                                for pair in range(N_PAIRS)], axis=0)
        m = jnp.maximum(jnp.max(st, axis=1, keepdims=True), sink)
        e = jnp.exp(st - m)
        e_sink = jnp.exp(sink - m)
        inv_l = 1.0 / (jnp.sum(e, axis=1, keepdims=True) + e_sink)
        probs.append(e * inv_l)
        p_sinks.append(e_sink * inv_l)
    return qps, jnp.concatenate(probs, axis=1), p_sinks


def _roped_qkv(cur_ref, prev_ref, cos_ref, sin_ref, cosp_ref, sinp_ref, qw, kvw):
    cur = cur_ref[...]
    cos, sin = cos_ref[...], sin_ref[...]
    cos_q, sin_q = _tile_lanes(cos, GROUP), _tile_lanes(sin, GROUP)
    q = _rope(cur[:, :qw], cos_q, sin_q, 1.0)
    prev = prev_ref[...]
    k_all = jnp.concatenate([_rope(prev[:, :kvw], cosp_ref[...], sinp_ref[...], 1.0),
                             _rope(cur[:, qw:qw + kvw], cos, sin, 1.0)], axis=0)
    v_all = jnp.concatenate([prev[:, kvw:], cur[:, qw + kvw:]], axis=0)
    return q, k_all, v_all, cos_q, sin_q


def _attention_fwd(proj, cos_t, sin_t, sinks):
    s = proj.shape[0]
    qw = GROUP * N_KV_HEADS * HEAD_DIM
    kvw = N_KV_HEADS * HEAD_DIM
    nb = s // WINDOW

    def body(cur_ref, prev_ref, cos_ref, sin_ref, cosp_ref, sinp_ref, sinks_ref, o_ref):
        first = pl.program_id(0) == 0
        q, k_all, v_all, _, _ = _roped_qkv(cur_ref, prev_ref, cos_ref, sin_ref, cosp_ref, sinp_ref, qw, kvw)
        kk2s = [_pair_operand(k_all, h) for h in range(N_KV_HEADS)]
        vv2s = [_pair_operand(v_all, h) for h in range(N_KV_HEADS)]
        _, probs, _ = _all_probs(q, kk2s, first, sinks_ref)
        probs = probs.astype(BF16)
        outs = [_dot(probs[pair * WINDOW:(pair + 1) * WINDOW], vv2s[pair // (GROUP // 2)]) for pair in range(N_PAIRS)]
        o_ref[...] = jnp.concatenate(outs, axis=1)

    tbl = pl.BlockSpec((WINDOW, kvw), lambda n: (n, 0))
    tbl_prev = pl.BlockSpec((WINDOW, kvw), lambda n: (jnp.maximum(n - 1, 0), 0))
    return pl.pallas_call(
        body, name="attention_fwd", grid=(nb,),
        in_specs=[pl.BlockSpec((WINDOW, qw + 2 * kvw), lambda n: (n, 0)),
                  pl.BlockSpec((WINDOW, 2 * kvw), lambda n: (jnp.maximum(n - 1, 0), (qw // (2 * kvw)))),
                  tbl, tbl, tbl_prev, tbl_prev, _VMEM],
        out_specs=pl.BlockSpec((WINDOW, qw), lambda n: (n, 0)),
        out_shape=jax.ShapeDtypeStruct((s, qw), F32),
        compiler_params=_params(("parallel",)),
    )(proj, proj, cos_t, sin_t, cos_t, sin_t, sinks)


def _conv_taps(cw_ref):
    return [jnp.concatenate([cw_ref[s, k:k + 1, :] for s in range(N_CHIPS)], axis=1) for k in range(3)]


def _shift_down(z, halo, steps):
    rows = z.shape[0]
    row = lax.broadcasted_iota(jnp.int32, z.shape, 0)
    out = pltpu.roll(z, steps, 0)
    for r in range(steps):
        out = jnp.where(row == r, halo[8 - steps + r:8 - steps + r + 1, :], out)
    return out


def _shift_up(z, halo, steps):
    rows = z.shape[0]
    row = lax.broadcasted_iota(jnp.int32, z.shape, 0)
    out = pltpu.roll(z, rows - steps, 0)
    for r in range(steps):
        out = jnp.where(row == rows - steps + r, halo[r:r + 1, :], out)
    return out


def _split_cbu(lo, hi, cw):
    c_gate = lo[:, :cw]
    b_gate = jnp.concatenate([lo[:, cw:], hi[:, :2 * cw - lo.shape[1]]], axis=1)
    u = hi[:, 2 * cw - lo.shape[1]:]
    return c_gate, b_gate, u


def _conv_norm(proj, attn, cw_full, g_ac):
    s, in_w = proj.shape
    cw = attn.shape[1]
    blk_w = in_w // 3
    tb = min(TB_CONV, s)

    def body(lo_ref, hi_ref, lo_h_ref, hi_h_ref, attn_ref, cw_ref, g_ref, mixed_ref, ac_ref, rstd_ref):
        i = pl.program_id(0)
        c_gate, b_gate, u = _split_cbu(lo_ref[...], hi_ref[...], cw)
        c_h, _, u_h = _split_cbu(lo_h_ref[...], hi_h_ref[...], cw)
        z = c_gate * u
        z_h = jnp.where(i == 0, 0.0, c_h * u_h)
        w0, w1, w2 = _conv_taps(cw_ref)
        y = w0 * _shift_down(z, z_h, 2) + w1 * _shift_down(z, z_h, 1) + w2 * z
        conv = b_gate * y
        a = attn_ref[...]
        r_a = lax.rsqrt(jnp.mean(a * a, axis=-1, keepdims=True) + RMS_EPS)
        r_c = lax.rsqrt(jnp.mean(conv * conv, axis=-1, keepdims=True) + RMS_EPS)
        g = g_ref[...]
        mixed_ref[...] = jnp.concatenate([a * r_a * g[:, :cw], conv * r_c * g[:, cw:]], axis=1).astype(BF16)
        ac_ref[...] = jnp.concatenate([a, conv], axis=1)
        rstd_ref[0] = r_a
        rstd_ref[1] = r_c

    halo_idx = lambda i: jnp.maximum(i * (tb // 8) - 1, 0)
    return pl.pallas_call(
        body, name="conv_norm", grid=(s // tb,),
        in_specs=[pl.BlockSpec((tb, blk_w), lambda i: (i, 1)),
                  pl.BlockSpec((tb, blk_w), lambda i: (i, 2)),
                  pl.BlockSpec((8, blk_w), lambda i: (halo_idx(i), 1)),
                  pl.BlockSpec((8, blk_w), lambda i: (halo_idx(i), 2)),
                  pl.BlockSpec((tb, cw), lambda i: (i, 0)),
                  _VMEM, _VMEM],
        out_specs=[pl.BlockSpec((tb, 2 * cw), lambda i: (i, 0)),
                   pl.BlockSpec((tb, 2 * cw), lambda i: (i, 0)),
                   pl.BlockSpec((2, tb, 1), lambda i: (0, i, 0))],
        out_shape=[jax.ShapeDtypeStruct((s, 2 * cw), BF16), jax.ShapeDtypeStruct((s, 2 * cw), F32),
                   jax.ShapeDtypeStruct((2, s, 1), F32)],
        compiler_params=_params(("parallel",)),
    )(proj, proj, proj, proj, attn, cw_full, g_ac)


def _out_proj_ln(mixed, w_out_g, x, ln_g, ln_b):
    s, d = mixed.shape
    tm = min(TM, s)
    tk = d
    nk = d // tk

    def body(a_ref, w_ref, x_ref, g_ref, b_ref, xhat_ref, h_ref, rstd_ref, acc):
        k = pl.program_id(1)
        _accumulate(acc, lambda: _dot(a_ref[...], w_ref[...]), k, nk)

        @pl.when(k == nk - 1)
        def _():
            def rows_fn(rows):
                xhat, rstd = _ln_fwd(ALPHA * x_ref[rows, :] + acc[rows, :])
                xhat_ref[rows, :] = xhat
                h_ref[rows, :] = (xhat * g_ref[...] + b_ref[...]).astype(BF16)
                rstd_ref[rows, :] = rstd

            _for_row_chunks(tm, rows_fn)

    row = pl.BlockSpec((tm, d), lambda i, k: (i, 0))
    return pl.pallas_call(
        body, name="out_proj_ln", grid=(s // tm, nk),
        in_specs=[pl.BlockSpec((tm, tk), lambda i, k: (i, k)),
                  pl.BlockSpec((tk, d), lambda i, k: (k, 0)),
                  pl.BlockSpec((None, tm, d), lambda i, k: (0, i, 0)),
                  _VMEM, _VMEM],
        out_specs=[row, row, pl.BlockSpec((tm, 1), lambda i, k: (i, 0))],
        out_shape=[jax.ShapeDtypeStruct((s, d), F32), jax.ShapeDtypeStruct((s, d), BF16),
                   jax.ShapeDtypeStruct((s, 1), F32)],
        scratch_shapes=[pltpu.VMEM((tm, d), F32)],
        compiler_params=_params(("parallel", "arbitrary")),
    )(mixed, w_out_g, x, ln_g, ln_b)


def _gate_up(h1, w_gu_g):
    s, d = h1.shape
    ns, _, fs2 = w_gu_g.shape
    fs = fs2 // 2
    tm = min(TM, s)

    def body(h_ref, w_ref, act_ref, ab_ref):
        gu = _dot(h_ref[...], w_ref[...])
        g, u = gu[:, :fs], gu[:, fs:]
        sg = _sigmoid(g)
        silu = g * sg
        act_ref[...] = (silu * u).astype(BF16)
        ab_ref[:, :fs] = (u * (sg * (1.0 + g * (1.0 - sg)))).astype(BF16)
        ab_ref[:, fs:] = silu.astype(BF16)

    return pl.pallas_call(
        body, name="gate_up", grid=(s // tm, ns),
        in_specs=[pl.BlockSpec((tm, d), lambda i, j: (i, 0)), pl.BlockSpec((None, d, fs2), lambda i, j: (j, 0, 0))],
        out_specs=[pl.BlockSpec((tm, fs), lambda i, j: (i, j)), pl.BlockSpec((tm, fs2), lambda i, j: (i, j))],
        out_shape=[jax.ShapeDtypeStruct((s, ns * fs), BF16), jax.ShapeDtypeStruct((s, ns * fs2), BF16)],
        compiler_params=_params(("parallel", "arbitrary")),
    )(h1, w_gu_g)


def _down_ln_loss(act, w_down_g, xhat1, ln1_g, ln1_b, ln2_g, ln2_b, target):
    s, f = act.shape
    d = xhat1.shape[1]
    tm = min(TM, s)
    tk = f // N_CHIPS
    nk = f // tk

    def body(a_ref, w_ref, xh_ref, g1_ref, b1_ref, g2_ref, b2_ref, t_ref, dpre_ref, loss_ref, gg_ref, gb_ref, acc):
        i, k = pl.program_id(0), pl.program_id(1)
        _accumulate(acc, lambda: _dot(a_ref[...], w_ref[...]), k, nk)

        @pl.when(k == nk - 1)
        def _():
            @pl.when(i == 0)
            def _():
                loss_ref[...] = jnp.zeros_like(loss_ref)
                gg_ref[...] = jnp.zeros_like(gg_ref)
                gb_ref[...] = jnp.zeros_like(gb_ref)

            def rows_fn(rows):
                h1 = xh_ref[rows, :] * g1_ref[...] + b1_ref[...]
                xhat, rstd = _ln_fwd(ALPHA * h1 + acc[rows, :])
                g2 = g2_ref[...]
                diff = xhat * g2 + b2_ref[...] - t_ref[rows, :]
                dy = diff * (1.0 / d)
                dpre_ref[rows, :] = _ln_bwd(dy, xhat, rstd, g2)
                sq = jnp.sum(jnp.sum(diff * diff, axis=1, keepdims=True), axis=0, keepdims=True)
                loss_ref[...] += jnp.broadcast_to(sq * (0.5 / d), (1, 128))
                gg_ref[...] += jnp.sum(dy * xhat, axis=0, keepdims=True)
                gb_ref[...] += jnp.sum(dy, axis=0, keepdims=True)

            _for_row_chunks(tm, rows_fn)

    row = pl.BlockSpec((tm, d), lambda i, k: (i, 0))
    vec = pl.BlockSpec((1, d), lambda i, k: (0, 0))
    return pl.pallas_call(
        body, name="down_ln_loss", grid=(s // tm, nk),
        in_specs=[pl.BlockSpec((tm, tk), lambda i, k: (i, k)),
                  pl.BlockSpec((tk, d), lambda i, k: (k, 0)),
                  row, _VMEM, _VMEM, _VMEM, _VMEM,
                  pl.BlockSpec((None, tm, d), lambda i, k: (0, i, 0))],
        out_specs=[row, pl.BlockSpec((1, 128), lambda i, k: (0, 0)), vec, vec],
        out_shape=[jax.ShapeDtypeStruct((s, d), F32), jax.ShapeDtypeStruct((1, 128), F32),
                   jax.ShapeDtypeStruct((1, d), F32), jax.ShapeDtypeStruct((1, d), F32)],
        scratch_shapes=[pltpu.VMEM((tm, d), F32)],
        compiler_params=_params(("arbitrary", "arbitrary")),
    )(act, w_down_g, xhat1, ln1_g, ln1_b, ln2_g, ln2_b, target)


def _dact_silu_bwd(dpre2, w_down_g, ab):
    s, d = dpre2.shape
    fs2 = ab.shape[1] // N_CHIPS
    fs = fs2 // 2
    tm = min(TM, s)

    def body(dp_ref, w_ref, ab_ref, dgu_ref):
        d_act = _dot_nt(dp_ref[...].astype(BF16), w_ref[...])
        dgu_ref[:, :fs] = (d_act * ab_ref[:, :fs].astype(F32)).astype(BF16)
        dgu_ref[:, fs:] = (d_act * ab_ref[:, fs:].astype(F32)).astype(BF16)

    blk = pl.BlockSpec((tm, fs2), lambda i, j: (i, j))
    return pl.pallas_call(
        body, name="dact_silu_bwd", grid=(s // tm, N_CHIPS),
        in_specs=[pl.BlockSpec((tm, d), lambda i, j: (i, 0)),
                  pl.BlockSpec((fs, d), lambda i, j: (j, 0)), blk],
        out_specs=blk,
        out_shape=jax.ShapeDtypeStruct(ab.shape, BF16),
        compiler_params=_params(("parallel", "arbitrary")),
    )(dpre2, w_down_g, ab)


def _grad_rows(a, b, after, name, row_blocks=1):
    s, m = a.shape
    n = b.shape[1]
    ms = m // N_CHIPS
    tmw = ms // row_blocks
    tk = min(TK_TOK, s)
    nk = s // tk

    def body(a_ref, b_ref, after_ref, o_ref, acc):
        k = pl.program_id(2)
        _accumulate(acc, lambda: _dot_tn(a_ref[...].astype(BF16), b_ref[...].astype(BF16)), k, nk)

        @pl.when(k == nk - 1)
        def _():
            o_ref[...] = acc[...].astype(BF16)

    return pl.pallas_call(
        body, name=name, grid=(N_CHIPS, row_blocks, nk),
        in_specs=[pl.BlockSpec((tk, tmw), lambda j, r, k: (k, j * row_blocks + r)),
                  pl.BlockSpec((tk, n), lambda j, r, k: (k, 0)), _ANY],
        out_specs=pl.BlockSpec((None, tmw, n), lambda j, r, k: (j, r, 0)),
        out_shape=jax.ShapeDtypeStruct((N_CHIPS, ms, n), BF16),
        scratch_shapes=[pltpu.VMEM((tmw, n), F32)],
        compiler_params=_params(("parallel", "parallel", "arbitrary")),
    )(a, b, after)


def _grad_cols(a, bs, after, name, a_3d=False, row_blocks=2):
    s, m = a.shape[-2:]
    n = bs[0].shape[1]
    ns = n // N_CHIPS
    nb = len(bs)
    tmw = m // row_blocks
    tk = min(TK_TOK, s)
    nk = s // tk

    def body(*refs):
        a_ref, b_refs, o_refs, accs = refs[0], refs[1:1 + nb], refs[2 + nb:2 + 2 * nb], refs[2 + 2 * nb:]
        k = pl.program_id(2)
        for b_ref, acc in zip(b_refs, accs):
            _accumulate(acc, lambda b_ref=b_ref: _dot_tn(a_ref[...].astype(BF16), b_ref[...].astype(BF16)), k, nk)

        @pl.when(k == nk - 1)
        def _():
            for o_ref, acc in zip(o_refs, accs):
                o_ref[...] = acc[...].astype(BF16)

    if a_3d:
        a_spec = pl.BlockSpec((None, tk, tmw), lambda j, r, k: (0, k, r))
    else:
        a_spec = pl.BlockSpec((tk, tmw), lambda j, r, k: (k, r))
    return pl.pallas_call(
        body, name=name, grid=(N_CHIPS, row_blocks, nk),
        in_specs=[a_spec] + [pl.BlockSpec((tk, ns), lambda j, r, k: (k, j))] * nb + [_ANY],
        out_specs=[pl.BlockSpec((None, tmw, ns), lambda j, r, k: (j, r, 0))] * nb,
        out_shape=[jax.ShapeDtypeStruct((N_CHIPS, m, ns), BF16)] * nb,
        scratch_shapes=[pltpu.VMEM((tmw, ns), F32)] * nb,
        compiler_params=_params(("parallel", "parallel", "arbitrary")),
    )(a, *bs, after)


def _dh1_ln_bwd(d_gu, w_gu_g, dpre2, xhat1, rstd1, ln1_g, after):
    s = d_gu.shape[0]
    d = dpre2.shape[1]
    hd = d // 2
    fs = w_gu_g.shape[2]
    tm = min(TM, s)

    def body(dgu_ref, w_ref, dp2_ref, xh_ref, rs_ref, g_ref, after_ref, dpre_ref, gg_ref, gb_ref, acc_lo, acc_hi):
        i, j, half = pl.program_id(0), pl.program_id(1), pl.program_id(2)

        def product():
            return _dot_nt(dgu_ref[...], w_ref[...])

        @pl.when(half == 0)
        def _():
            _accumulate(acc_lo, product, j, N_CHIPS)

        @pl.when(half == 1)
        def _():
            _accumulate(acc_hi, product, j, N_CHIPS)

        @pl.when((j == N_CHIPS - 1) & (half == 1))
        def _():
            @pl.when(i == 0)
            def _():
                gg_ref[...] = jnp.zeros_like(gg_ref)
                gb_ref[...] = jnp.zeros_like(gb_ref)

            def rows_fn(rows):
                dh = jnp.concatenate([acc_lo[rows, :], acc_hi[rows, :]], axis=1) + ALPHA * dp2_ref[rows, :]
                xhat = xh_ref[rows, :]
                dpre_ref[rows, :] = _ln_bwd(dh, xhat, rs_ref[rows, :], g_ref[...])
                gg_ref[...] += jnp.sum(dh * xhat, axis=0, keepdims=True)
                gb_ref[...] += jnp.sum(dh, axis=0, keepdims=True)

            _for_row_chunks(tm, rows_fn)

    row = pl.BlockSpec((tm, d), lambda i, j, h: (i, 0))
    vec = pl.BlockSpec((1, d), lambda i, j, h: (0, 0))
    act_blk = pl.BlockSpec((tm, fs), lambda i, j, h: (i, j))
    w_blk = pl.BlockSpec((None, hd, fs), lambda i, j, h: (j, h, 0))
    return pl.pallas_call(
        body, name="dh1_ln_bwd", grid=(s // tm, N_CHIPS, 2),
        in_specs=[act_blk, w_blk, row, row, pl.BlockSpec((tm, 1), lambda i, j, h: (i, 0)), _VMEM, _ANY],
        out_specs=[row, vec, vec],
        out_shape=[jax.ShapeDtypeStruct((s, d), F32), jax.ShapeDtypeStruct((1, d), F32),
                   jax.ShapeDtypeStruct((1, d), F32)],
        scratch_shapes=[pltpu.VMEM((tm, hd), F32)] * 2,
        compiler_params=_params(("arbitrary", "arbitrary", "arbitrary")),
    )(d_gu, w_gu_g, dpre2, xhat1, rstd1, ln1_g, after)


def _dmixed_rms_bwd(dpre1, w_out_g, ac, rstd, g_ac):
    s, d = dpre1.shape
    hd = d // 2
    tm = min(TM, s)

    def body(dp_ref, w_ref, ac_ref, rs_ref, g_ref, dac_ref, gg_ref):
        i = pl.program_id(1)
        dm = _dot_nt(dp_ref[...].astype(BF16), w_ref[...])
        pre = ac_ref[...]
        r = rs_ref[...]
        gdm = dm * g_ref[...]
        dac_ref[...] = r * gdm - pre * (r * r * r) * jnp.mean(gdm * pre, axis=-1, keepdims=True)
        gg = jnp.sum(dm * pre * r, axis=0, keepdims=True)

        @pl.when(i == 0)
        def _():
            gg_ref[...] = gg

        @pl.when(i > 0)
        def _():
            gg_ref[...] += gg

    return pl.pallas_call(
        body, name="dmixed_rms_bwd", grid=(2, s // tm),
        in_specs=[pl.BlockSpec((tm, d), lambda h, i: (i, 0)),
                  pl.BlockSpec((hd, d), lambda h, i: (h, 0)),
                  pl.BlockSpec((tm, hd), lambda h, i: (i, h)),
                  pl.BlockSpec((None, tm, 1), lambda h, i: (h, i, 0)),
                  pl.BlockSpec((1, hd), lambda h, i: (0, h))],
        out_specs=[pl.BlockSpec((tm, hd), lambda h, i: (i, h)),
                   pl.BlockSpec((1, hd), lambda h, i: (0, h))],
        out_shape=[jax.ShapeDtypeStruct((s, d), F32), jax.ShapeDtypeStruct((1, d), F32)],
        compiler_params=_params(("arbitrary", "arbitrary")),
    )(dpre1, w_out_g, ac, rstd, g_ac)


def _attention_bwd(proj, d_ac, cos_t, sin_t, sinks, after):
    s = proj.shape[0]
    qw = GROUP * N_KV_HEADS * HEAD_DIM
    kvw = N_KV_HEADS * HEAD_DIM
    nb = s // WINDOW
    nq = GROUP * N_KV_HEADS

    def body(cur_ref, prev_ref, do_ref, cos_ref, sin_ref, cosp_ref, sinp_ref, sinks_ref, after_ref,
             dq_ref, dcur_ref, dprev_ref, dsink_ref):
        n = pl.program_id(0)
        first = n == 0
        q, k_all, v_all, cos_q, sin_q = _roped_qkv(cur_ref, prev_ref, cos_ref, sin_ref, cosp_ref, sinp_ref, qw, kvw)
        kk2s = [_pair_operand(k_all, h) for h in range(N_KV_HEADS)]
        vv2s = [_pair_operand(v_all, h) for h in range(N_KV_HEADS)]
        qps, probs, p_sinks = _all_probs(q, kk2s, first, sinks_ref)
        dops = [do_ref[:, pair * PAIR:(pair + 1) * PAIR].astype(BF16) for pair in range(N_PAIRS)]
        d_probs = jnp.concatenate([_dot_nt(dops[pair], vv2s[pair // (GROUP // 2)]) for pair in range(N_PAIRS)], axis=0)
        d_s, ds_sinks = [], []
        for t in range(2):
            cols = slice(t * KEYS, (t + 1) * KEYS)
            delta = jnp.sum(probs[:, cols] * d_probs[:, cols], axis=1, keepdims=True)
            d_s.append(probs[:, cols] * (d_probs[:, cols] - delta))
            ds_sinks.append(-p_sinks[t] * delta)
        d_s = jnp.concatenate(d_s, axis=1).astype(BF16)
        probs = probs.astype(BF16)
        dq_parts, dk_tiles, dv_tiles, dsink_parts = [], [], [], []
        for h in range(N_KV_HEADS):
            dkk2, dvv2 = None, None
            for p in range(GROUP // 2):
                pair = (GROUP // 2) * h + p
                rows = slice(pair * WINDOW, (pair + 1) * WINDOW)
                dq_parts.append(_dot(d_s[rows], kk2s[h]) * ATTN_SCALE)
                dk_term = _dot_tn(d_s[rows], qps[pair])
                dv_term = _dot_tn(probs[rows], dops[pair])
                dkk2 = dk_term if dkk2 is None else dkk2 + dk_term
                dvv2 = dv_term if dvv2 is None else dvv2 + dv_term
                dsink_parts.extend([jnp.sum(ds_sinks[t][rows], axis=0, keepdims=True) for t in range(2)])
            dk_tiles.append(_pair_grad(dkk2, h))
            dv_tiles.append(_pair_grad(dvv2, h))
        dq_ref[...] = _rope(jnp.concatenate(dq_parts, axis=1), cos_q, sin_q, -1.0)
        dk = jnp.concatenate([dk_tiles[0] + dk_tiles[1], dk_tiles[2] + dk_tiles[3]], axis=1)
        dv = jnp.concatenate([dv_tiles[0] + dv_tiles[1], dv_tiles[2] + dv_tiles[3]], axis=1)
        dprev_ref[...] = jnp.concatenate([dk[:WINDOW], dv[:WINDOW]], axis=1)
        dcur_ref[...] = jnp.concatenate([dk[WINDOW:], dv[WINDOW:]], axis=1)
        dsink = jnp.concatenate(dsink_parts, axis=1)

        @pl.when(first)
        def _():
            dsink_ref[...] = dsink

        @pl.when(n > 0)
        def _():
            dsink_ref[...] += dsink

    tbl = pl.BlockSpec((WINDOW, kvw), lambda n: (n, 0))
    tbl_prev = pl.BlockSpec((WINDOW, kvw), lambda n: (jnp.maximum(n - 1, 0), 0))
    kv_blk = pl.BlockSpec((WINDOW, 2 * kvw), lambda n: (n, 0))
    return pl.pallas_call(
        body, name="attention_bwd", grid=(nb,),
        in_specs=[pl.BlockSpec((WINDOW, qw + 2 * kvw), lambda n: (n, 0)),
                  pl.BlockSpec((WINDOW, 2 * kvw), lambda n: (jnp.maximum(n - 1, 0), (qw // (2 * kvw)))),
                  pl.BlockSpec((WINDOW, qw), lambda n: (n, 0)),
                  tbl, tbl, tbl_prev, tbl_prev, _VMEM, _ANY],
        out_specs=[pl.BlockSpec((WINDOW, qw), lambda n: (n, 0)), kv_blk, kv_blk,
                   pl.BlockSpec((1, nq), lambda n: (0, 0))],
        out_shape=[jax.ShapeDtypeStruct((s, qw), F32), jax.ShapeDtypeStruct((s, 2 * kvw), F32),
                   jax.ShapeDtypeStruct((s, 2 * kvw), F32), jax.ShapeDtypeStruct((1, nq), F32)],
        compiler_params=_params(("arbitrary",)),
    )(proj, proj, d_ac, cos_t, sin_t, cos_t, sin_t, sinks, after)


def _dproj_assemble(proj, d_ac, dq, dkv_cur, dkv_prev, cos_t, sin_t, cw_full):
    s, in_w = proj.shape
    cw = dq.shape[1]
    kvw = N_KV_HEADS * HEAD_DIM
    blk_w = in_w // 3
    tb = WINDOW
    nb = s // tb

    def body(lo_ref, hi_ref, lo_p_ref, hi_p_ref, lo_n_ref, hi_n_ref, dconv_ref, dconv_n_ref,
             dq_ref, dcur_ref, dprev_n_ref, cos_ref, sin_ref, cw_ref, dproj_ref, gcw_ref):
        i = pl.program_id(0)
        last = i == nb - 1
        c_gate, b_gate, u = _split_cbu(lo_ref[...], hi_ref[...], cw)
        c_p, _, u_p = _split_cbu(lo_p_ref[...], hi_p_ref[...], cw)
        _, b_n, _ = _split_cbu(lo_n_ref[...], hi_n_ref[...], cw)
        z = c_gate * u
        z_p = jnp.where(i == 0, 0.0, c_p * u_p)
        z1 = _shift_down(z, z_p, 1)
        z2 = _shift_down(z, z_p, 2)
        w0, w1, w2 = _conv_taps(cw_ref)
        y = w0 * z2 + w1 * z1 + w2 * z
        d_conv = dconv_ref[...]
        d_b = d_conv * y
        d_y = d_conv * b_gate
        d_y_n = jnp.where(last, 0.0, dconv_n_ref[...] * b_n)
        d_z = w2 * d_y + w1 * _shift_up(d_y, d_y_n, 1) + w0 * _shift_up(d_y, d_y_n, 2)
        d_c = d_z * u
        d_u = d_z * c_gate
        gcw = jnp.concatenate([jnp.sum(d_y * z2, axis=0, keepdims=True), jnp.sum(d_y * z1, axis=0, keepdims=True),
                               jnp.sum(d_y * z, axis=0, keepdims=True)], axis=0)

        @pl.when(i == 0)
        def _():
            gcw_ref[...] = gcw

        @pl.when(i > 0)
        def _():
            gcw_ref[...] += gcw

        dkv = dcur_ref[...] + jnp.where(last, 0.0, dprev_n_ref[...])
        dk = _rope(dkv[:, :kvw], cos_ref[...], sin_ref[...], -1.0)
        dproj_ref[...] = jnp.concatenate([dq_ref[...], dk, dkv[:, kvw:], d_c, d_b, d_u], axis=1).astype(BF16)

    prev8 = lambda i: jnp.maximum(i * (tb // 8) - 1, 0)
    next8 = lambda i: jnp.minimum((i + 1) * (tb // 8), s // 8 - 1)
    nxt = lambda i: jnp.minimum(i + 1, nb - 1)
    return pl.pallas_call(
        body, name="dproj_assemble", grid=(nb,),
        in_specs=[pl.BlockSpec((tb, blk_w), lambda i: (i, 1)),
                  pl.BlockSpec((tb, blk_w), lambda i: (i, 2)),
                  pl.BlockSpec((8, blk_w), lambda i: (prev8(i), 1)),
                  pl.BlockSpec((8, blk_w), lambda i: (prev8(i), 2)),
                  pl.BlockSpec((8, blk_w), lambda i: (next8(i), 1)),
                  pl.BlockSpec((8, blk_w), lambda i: (next8(i), 2)),
                  pl.BlockSpec((tb, cw), lambda i: (i, 1)),
                  pl.BlockSpec((8, cw), lambda i: (next8(i), 1)),
                  pl.BlockSpec((tb, cw), lambda i: (i, 0)),
                  pl.BlockSpec((tb, 2 * kvw), lambda i: (i, 0)),
                  pl.BlockSpec((tb, 2 * kvw), lambda i: (nxt(i), 0)),
                  pl.BlockSpec((tb, kvw), lambda i: (i, 0)),
                  pl.BlockSpec((tb, kvw), lambda i: (i, 0)),
                  _VMEM],
        out_specs=[pl.BlockSpec((tb, in_w), lambda i: (i, 0)),
                   pl.BlockSpec((3, cw), lambda i: (0, 0))],
        out_shape=[jax.ShapeDtypeStruct((s, in_w), BF16), jax.ShapeDtypeStruct((3, cw), F32)],
        compiler_params=_params(("arbitrary",)),
    )(proj, proj, proj, proj, proj, proj, d_ac, d_ac, dq, dkv_cur, dkv_prev, cos_t, sin_t, cw_full)


def _dx(d_proj, w_in_g, dpre1, after):
    s, in_w = d_proj.shape
    ns, d, ncol = w_in_g.shape
    tm = min(TM, s)

    def body(dp_ref, w_ref, r_ref, after_ref, o_ref, acc):
        j = pl.program_id(1)
        _accumulate(acc, lambda: _dot_nt(dp_ref[...], w_ref[...]), j, ns)

        @pl.when(j == ns - 1)
        def _():
            o_ref[...] = acc[...] + ALPHA * r_ref[...]

    return pl.pallas_call(
        body, name="dx", grid=(s // tm, ns),
        in_specs=[pl.BlockSpec((tm, ncol), lambda i, j: (i, j)),
                  pl.BlockSpec((None, d, ncol), lambda i, j: (j, 0, 0)),
                  pl.BlockSpec((tm, d), lambda i, j: (i, 0)), _ANY],
        out_specs=pl.BlockSpec((None, tm, d), lambda i, j: (0, i, 0)),
        out_shape=jax.ShapeDtypeStruct((1, s, d), F32),
        scratch_shapes=[pltpu.VMEM((tm, d), F32)],
        compiler_params=_params(("parallel", "arbitrary")),
    )(d_proj, w_in_g, dpre1, after)


def kernel(x, positions, w_in, conv_w, sinks, g_attn, g_conv, w_out, ln1_g, ln1_b, w_gate, w_up, w_down, ln2_g, ln2_b, loss_target, m_w_in, m_conv_w, m_sinks, m_g_attn, m_g_conv, m_w_out, m_ln1_g, m_ln1_b, m_w_gate, m_w_up, m_w_down, m_ln2_g, m_ln2_b, v_w_in, v_conv_w, v_sinks, v_g_attn, v_g_conv, v_w_out, v_ln1_g, v_ln1_b, v_w_gate, v_w_up, v_w_down, v_ln2_g, v_ln2_b):
    s = x.shape[1]
    d = x.shape[2]

    chip_vec = _chip_id(lax.axis_index("x"), lax.axis_index("y")).astype(jnp.int32).reshape(1)
    wnames = ["w_in", "w_out", "w_gu", "w_down"]
    cw_full = _allgather_conv_w(conv_w)
    buf_in = _cast_weight(w_in, chip_vec, cw_full, "cast_w_in")
    flight_in, token_in = _gather_start([buf_in], cw_full, "gather_start_w_in")
    buf_gu = _cast_weight(w_gate, chip_vec, token_in, "cast_w_gate", 0, 2)
    buf_gu = _cast_weight(w_up, chip_vec, buf_gu, "cast_w_up", 1, 2)
    bufs = [_cast_weight(w_out, chip_vec, token_in, "cast_w_out"), buf_gu,
            _cast_weight(w_down, chip_vec, token_in, "cast_w_down")]
    flights_rest, token = _gather_start(bufs, token_in, "gather_start_rest")
    flights = flight_in + flights_rest

    def gathered(i, after):
        send_sems, recv_sems, buf = flights[i]
        buf = _gather_wait(send_sems, recv_sems, buf, after, "gather_wait_" + wnames[i])
        return _sibling_fill(buf, "sibling_fill_" + wnames[i])

    g_ac = jnp.concatenate([g_attn, g_conv], axis=1)

    cos_t, sin_t = _rope_tables(positions.reshape(s, 1) + token[0:1, 0:1].astype(jnp.int32))
    w_in_g = gathered(0, cos_t)
    proj = _in_proj(x, w_in_g)
    send_sems, recv_sems, buf_out = flights[1]
    buf_out = _gather_wait(send_sems, recv_sems, buf_out, proj, "gather_wait_w_out")
    fill_out = _flight_start("fill_start_w_out", [buf_out], _fill_plan(1), 3, chip_vec)
    attn = _attention_fwd(_after(proj, fill_out[2][0]), cos_t, sin_t, sinks)
    mixed, ac, rstd_ac = _conv_norm(proj, attn, cw_full, g_ac)
    (w_out_g,) = _flight_wait("fill_wait_w_out", fill_out, _fill_plan(1), mixed)
    w_out_full = w_out_g.reshape(d, d)
    xhat1, h1, rstd1 = _out_proj_ln(mixed, w_out_full, x, ln1_g, ln1_b)
    w_gu_g = gathered(2, h1)
    act, ab = _gate_up(h1, w_gu_g)
    w_down_full = gathered(3, act).reshape(-1, d)
    dpre2, loss_part, g_ln2_g, g_ln2_b = _down_ln_loss(act, w_down_full, xhat1, ln1_g, ln1_b, ln2_g, ln2_b, loss_target)

    cvec = lax.axis_index("c").astype(jnp.int32).reshape(1)

    def exchange_begin(parts, nme):
        bufs = []
        for part in parts:
            ns, r, cdim = part.shape
            bufs.extend([part, lax.empty((ns, r // 2, cdim), part.dtype)])
        return _flight_start("exchange_start_" + nme, bufs, _exchange_plan(len(parts)), len(parts), cvec)

    def exchange_end(flight, n_parts, after, nme):
        bufs = _flight_wait("exchange_wait_" + nme, flight, _exchange_plan(n_parts), after)
        return [(bufs[2 * w], bufs[2 * w + 1]) for w in range(n_parts)]

    def scatter_begin(part, got, nme):
        return _scatter_start(_add_halves(part, got, cvec, "add_halves_" + nme), "scatter_start_" + nme)

    d_gu = _dact_silu_bwd(dpre2, w_down_full, ab)
    p_down = _grad_rows(act, dpre2, d_gu, "grad_w_down")
    x_down = exchange_begin([p_down], "w_down")
    (p_gu,) = _grad_cols(h1, [d_gu], x_down[2][0], "grad_w_gate_up")
    ((p_down, got),) = exchange_end(x_down, 1, p_gu, "w_down")
    f_down = scatter_begin(p_down, got, "w_down")
    x_gu = exchange_begin([_after(p_gu, f_down[2])], "w_gu")
    dpre1, g_ln1_g, g_ln1_b = _dh1_ln_bwd(d_gu, w_gu_g, dpre2, xhat1, rstd1, ln1_g, x_gu[2][0])
    ((p_gu, got),) = exchange_end(x_gu, 1, dpre1, "w_gu")
    f_gu = scatter_begin(p_gu, got, "w_gu")
    d_ac, g_g_ac = _dmixed_rms_bwd(_after(dpre1, f_gu[2]), w_out_full, ac, rstd_ac, g_ac)
    p_out = _grad_rows(mixed, dpre1, d_ac, "grad_w_out")
    x_out = exchange_begin([p_out], "w_out")
    dq, dkv_cur, dkv_prev, g_sinks = _attention_bwd(proj, d_ac, cos_t, sin_t, sinks, x_out[2][0])
    ((p_out, got),) = exchange_end(x_out, 1, dq, "w_out")
    f_out = scatter_begin(p_out, got, "w_out")
    d_proj, g_conv_w = _dproj_assemble(proj, _after(d_ac, f_out[2]), dq, dkv_cur, dkv_prev, cos_t, sin_t, cw_full)
    (p_in,) = _grad_cols(x, [d_proj], d_proj, "grad_w_in", a_3d=True)
    x_in = exchange_begin([p_in], "w_in")
    grad_x = _dx(d_proj, w_in_g, dpre1, x_in[2][0])
    red = _allreduce_small(g_ln2_g, g_ln2_b, g_ln1_g, g_ln1_b, g_g_ac, g_conv_w, g_sinks, loss_part, grad_x)
    ((p_in, got),) = exchange_end(x_in, 1, red, "w_in")
    f_in = scatter_begin(p_in, got, "w_in")

    pos_vec = jnp.concatenate([chip_vec, cvec])
    shards = {"w_in": (w_in, m_w_in, v_w_in), "w_out": (w_out, m_w_out, v_w_out), "w_gate": (w_gate, m_w_gate, v_w_gate),
              "w_up": (w_up, m_w_up, v_w_up), "w_down": (w_down, m_w_down, v_w_down)}
    early = [("w_down", ["w_down"]), ("w_gu", ["w_gate", "w_up"]), ("w_out", ["w_out"])]
    after = f_in[2]
    completing = {}
    for (nme, _), f in zip(early, [f_down, f_gu, f_out]):
        sums, land = _scatter_wait(*f, after, "scatter_wait_" + nme)
        completing[nme] = _flight_start("complete_start_" + nme, [sums, land], _complete_plan(1), 4, cvec)
        after = completing[nme][2][1]
    big = {}
    for nme, members in early:
        sums, land = _flight_wait("complete_wait_" + nme, completing[nme], _complete_plan(1), after)
        for col_block, member in enumerate(members):
            big[member] = _adamw_shard(*shards[member], land, sums, pos_vec, "adamw_" + member, col_block)
            after = big[member][0]
    sums, land = _scatter_wait(*f_in, after, "scatter_wait_w_in")
    (land,) = _complete_chip_sums([sums], [land])
    big["w_in"] = _adamw_shard(*shards["w_in"], land, sums, pos_vec, "adamw_w_in")
    small = _adamw_small(red, {
        "sinks": (sinks, m_sinks, v_sinks), "g_attn": (g_attn, m_g_attn, v_g_attn),
        "g_conv": (g_conv, m_g_conv, v_g_conv), "ln1_g": (ln1_g, m_ln1_g, v_ln1_g),
        "ln1_b": (ln1_b, m_ln1_b, v_ln1_b), "ln2_g": (ln2_g, m_ln2_g, v_ln2_g),
        "ln2_b": (ln2_b, m_ln2_b, v_ln2_b), "conv_w": (conv_w, m_conv_w, v_conv_w)})
    res = {**big, **small}
    order = ["w_in", "conv_w", "sinks", "g_attn", "g_conv", "w_out", "ln1_g", "ln1_b", "w_gate", "w_up", "w_down",
             "ln2_g", "ln2_b"]
    loss = red[6, d // 2 + 128]
    return (loss, grad_x, *[res[n][0] for n in order], *[res[n][1] for n in order],
            *[res[n][2] for n in order], *[res[n][3] for n in order])
```

```python
import functools

import numpy as np
import jax
import jax.numpy as jnp
from jax import lax
from jax.experimental import pallas as pl
from jax.experimental.pallas import tpu as pltpu

F32 = jnp.float32
BF16 = jnp.bfloat16
MESH = pl.DeviceIdType.MESH

HEAD_DIM = 64
N_KV_HEADS = 4
GROUP = 4
WINDOW = 128
ROT_DIM = 16
ROPE_THETA = 500000.0
ATTN_SCALE = HEAD_DIM ** -0.5
ALPHA = 2.0 ** 0.25
LN_EPS = 1e-5
RMS_EPS = 1e-6
ADAM_LR = 0.001
ADAM_B1 = 0.9
ADAM_B2 = 0.999
ADAM_EPS = 1e-08
ADAM_WD = 0.01
ADAM_STEP = 10
N_CHIPS = 4
NEG_BIG = -1e30

V7X_VMEM_BYTES = 64 * 1024 * 1024
VMEM_LIMIT = V7X_VMEM_BYTES - 6 * 1024 * 1024

TM = 512
TK_TOK = 1024
TB_CONV = 256
TR_ELT = 256
ROW_CHUNK = 128
HALO_ROWS = 16


def _params(sem):
    return pltpu.CompilerParams(dimension_semantics=sem, vmem_limit_bytes=VMEM_LIMIT)


def _row_tile(rows, target):
    best = None
    for t in range(16, min(rows, target) + 1, 16):
        if rows % t == 0:
            best = t
    assert best is not None, (rows, target)
    return best


def _dot(a, b):
    return jnp.dot(a, b, preferred_element_type=F32)


def _dot_nt(a, b):
    return lax.dot_general(a, b, (((1,), (1,)), ((), ())), preferred_element_type=F32)


def _dot_tn(a, b):
    return lax.dot_general(a, b, (((0,), (0,)), ((), ())), preferred_element_type=F32)


def _mesh_pos():
    x, y, c = lax.axis_index("x"), lax.axis_index("y"), lax.axis_index("c")
    chips = [(1 - x, y), (x, 1 - y), (1 - x, 1 - y)]
    return x, y, c, chips


def _chip_id(px, py):
    return 2 * px + py


def _rope(t, cos, sgn_sin, sign):
    w = t.shape[1]
    lane = lax.broadcasted_iota(jnp.int32, t.shape, 1) & (HEAD_DIM - 1)
    partner = jnp.where(lane < ROT_DIM // 2, pltpu.roll(t, w - ROT_DIM // 2, 1), pltpu.roll(t, ROT_DIM // 2, 1))
    return t * cos + sign * (partner * sgn_sin)


def _tile_lanes(t, n):
    return jnp.concatenate([t] * n, axis=1)


def _sigmoid(g):
    return 1.0 / (1.0 + jnp.exp(-g))


def _for_row_chunks(n_rows, fn):
    def step(r, carry):
        fn(pl.ds(pl.multiple_of(r * ROW_CHUNK, ROW_CHUNK), ROW_CHUNK))
        return carry

    lax.fori_loop(0, n_rows // ROW_CHUNK, step, 0)


def _accumulate(acc, make_val, k, nk):
    if nk == 1:
        acc[...] = make_val()
        return

    @pl.when(k == 0)
    def _():
        acc[...] = jnp.zeros_like(acc)

    acc[...] += make_val()


def _ln_fwd(pre):
    mu = jnp.mean(pre, axis=-1, keepdims=True)
    cen = pre - mu
    var = jnp.mean(cen * cen, axis=-1, keepdims=True)
    rstd = lax.rsqrt(var + LN_EPS)
    return cen * rstd, rstd


def _ln_bwd(dy, xhat, rstd, g):
    dxhat = dy * g
    m1 = jnp.mean(dxhat, axis=-1, keepdims=True)
    m2 = jnp.mean(dxhat * xhat, axis=-1, keepdims=True)
    return rstd * (dxhat - m1 - xhat * m2)


def _cast_weight(w, chip_vec, after, name, col_block=0, n_col_blocks=1):
    _, r, c = w.shape
    tr = _row_tile(r, TR_ELT)

    def body(chip_ref, w_ref, after_ref, o_ref):
        o_ref[...] = w_ref[...].astype(BF16)

    grid_spec = pltpu.PrefetchScalarGridSpec(
        num_scalar_prefetch=1, grid=(r // tr,),
        in_specs=[pl.BlockSpec((None, tr, c), lambda i, chip_ref: (0, i, 0)), _ANY],
        out_specs=pl.BlockSpec((None, tr, c), lambda i, chip_ref: (chip_ref[0], i, col_block)))
    return pl.pallas_call(
        body, name=name, grid_spec=grid_spec,
        out_shape=jax.ShapeDtypeStruct((N_CHIPS, r, n_col_blocks * c), BF16),
        input_output_aliases={2: 0} if col_block else {},
        compiler_params=_params(("parallel",)),
    )(chip_vec, w, after)


_HBM = pl.BlockSpec(memory_space=pltpu.HBM)
_VMEM = pl.BlockSpec(memory_space=pltpu.VMEM)


_SEM = pl.BlockSpec(memory_space=pltpu.SEMAPHORE)
_ANY = pl.BlockSpec(memory_space=pl.ANY)
_EFFECT = pltpu.SideEffectType.DATAFLOW_SIDE_EFFECTING


def _chip_copy(buf, k, chip_of_src, half_rows, send_sems, recv_sems, to):
    part = buf.at[chip_of_src, half_rows]
    return pltpu.make_async_remote_copy(
        src_ref=part, dst_ref=part, send_sem=send_sems.at[k], recv_sem=recv_sems.at[k], device_id=to, device_id_type=MESH)


def _half_rows(buf, which):
    hr = buf.shape[1] // 2
    return pl.ds(which * hr, hr)


def _after(value, dep):
    return lax.optimization_barrier((value, dep))[0]


def _flight_start(name, bufs, plan, n_sems, after):
    n = len(bufs)

    def body(*refs):
        sends, _ = plan(refs[:n], refs[n + 1], refs[n + 2])
        for cp in sends:
            cp.start()

    outs = pl.pallas_call(
        body, name=name,
        in_specs=[_HBM] * n + [_ANY], out_specs=[_SEM, _SEM] + [_HBM] * n,
        out_shape=[pltpu.SemaphoreType.DMA((n_sems,))] * 2 + [pltpu.HBM(b.shape, b.dtype) for b in bufs],
        input_output_aliases={i: 2 + i for i in range(n)},
        compiler_params=pltpu.CompilerParams(has_side_effects=_EFFECT),
    )(*[pltpu.with_memory_space_constraint(b, pltpu.HBM) for b in bufs], after)
    return outs[0], outs[1], list(outs[2:])


def _flight_wait(name, flight, plan, after):
    send_sems, recv_sems, bufs = flight
    n = len(bufs)

    def body(*refs):
        sends, recvs = plan(refs[:n], refs[n], refs[n + 1])
        for cp in sends:
            cp.wait_send()
        for cp in recvs:
            cp.wait_recv()

    outs = pl.pallas_call(
        body, name=name,
        in_specs=[_HBM] * n + [_SEM, _SEM, _ANY], out_specs=[_HBM] * n,
        out_shape=[pltpu.HBM(b.shape, b.dtype) for b in bufs],
        input_output_aliases={i: i for i in range(n)},
        compiler_params=pltpu.CompilerParams(has_side_effects=_EFFECT),
    )(*bufs, send_sems, recv_sems, after)
    return list(outs)


def _fill_plan(n_bufs):
    def plan(refs, send_sems, recv_sems):
        x, y, c, chips = _mesh_pos()
        sibling = (x, y, 1 - c)
        sends, recvs = [], []
        for w in range(n_bufs):
            for k, chip in enumerate(chips):
                slot = _chip_id(*chip)
                sends.append(_chip_copy(refs[w], 3 * w + k, slot, _half_rows(refs[w], c), send_sems, recv_sems, sibling))
                recvs.append(_chip_copy(refs[w], 3 * w + k, slot, _half_rows(refs[w], 1 - c), send_sems, recv_sems,
                                        sibling))
        return sends, recvs
    return plan


def _exchange_plan(n_parts):
    def plan(refs, send_sems, recv_sems):
        x, y, c, _ = _mesh_pos()
        copies = []
        for w in range(n_parts):
            part, got = refs[2 * w], refs[2 * w + 1]
            hr = got.shape[1]
            copies.append(pltpu.make_async_remote_copy(
                src_ref=part.at[:, pl.ds((1 - c) * hr, hr)], dst_ref=got, send_sem=send_sems.at[w],
                recv_sem=recv_sems.at[w], device_id=(x, y, 1 - c), device_id_type=MESH))
        return copies, copies
    return plan


def _gather_start(bufs, after, name):
    n = len(bufs)

    def body(*refs):
        ins = refs[:n]
        sends, recvs = refs[n + 1:2 * n + 1], refs[2 * n + 1:3 * n + 1]
        token = refs[4 * n + 1]
        x, y, c, chips = _mesh_pos()
        me = _chip_id(x, y)
        for w in range(n):
            for k, chip in enumerate(chips):
                _chip_copy(ins[w], k, me, _half_rows(ins[w], c), sends[w], recvs[w], (*chip, c)).start()
        token[...] = jnp.zeros_like(token)

    outs = pl.pallas_call(
        body, name=name,
        in_specs=[_HBM] * n + [_ANY],
        out_specs=[_SEM] * (2 * n) + [_HBM] * n + [_VMEM],
        out_shape=[pltpu.SemaphoreType.DMA((3,))] * (2 * n) + [pltpu.HBM(b.shape, b.dtype) for b in bufs]
        + [jax.ShapeDtypeStruct((8, 128), F32)],
        input_output_aliases={w: 2 * n + w for w in range(n)},
        compiler_params=pltpu.CompilerParams(has_side_effects=_EFFECT),
    )(*[pltpu.with_memory_space_constraint(b, pltpu.HBM) for b in bufs], after)
    return [(outs[w], outs[n + w], outs[2 * n + w]) for w in range(n)], outs[3 * n]


def _gather_wait(send_sems, recv_sems, buf, after, name):
    def body(buf_ref, send_ref, recv_ref, after_ref, out_ref):
        x, y, c, chips = _mesh_pos()
        me = _chip_id(x, y)
        for k, chip in enumerate(chips):
            _chip_copy(buf_ref, k, me, _half_rows(buf_ref, c), send_ref, recv_ref, (*chip, c)).wait_send()
        for k, chip in enumerate(chips):
            _chip_copy(buf_ref, k, _chip_id(*chip), _half_rows(buf_ref, c), send_ref, recv_ref, (*chip, c)).wait_recv()

    return pl.pallas_call(
        body, name=name,
        in_specs=[_HBM, _SEM, _SEM, _ANY], out_specs=_HBM,
        out_shape=pltpu.HBM(buf.shape, buf.dtype),
        input_output_aliases={0: 0},
        compiler_params=pltpu.CompilerParams(has_side_effects=_EFFECT),
    )(buf, send_sems, recv_sems, after)


def _sibling_fill(buf, name, own_too=False):
    n_copies = 4 if own_too else 3

    def body(buf_ref, out_ref, send_sems, recv_sems):
        x, y, c, chips = _mesh_pos()
        sibling = (x, y, 1 - c)
        slots = [_chip_id(*chip) for chip in chips] + ([_chip_id(x, y)] if own_too else [])
        copies = []
        for k, slot in enumerate(slots):
            cp = _chip_copy(out_ref, k, slot, _half_rows(out_ref, c), send_sems, recv_sems, sibling)
            cp.start()
            copies.append(cp)
        for k, slot in enumerate(slots):
            _chip_copy(out_ref, k, slot, _half_rows(out_ref, 1 - c), send_sems, recv_sems, sibling).wait_recv()
        for cp in copies:
            cp.wait_send()

    return pl.pallas_call(
        body, name=name,
        in_specs=[_HBM], out_specs=_HBM,
        out_shape=jax.ShapeDtypeStruct(buf.shape, buf.dtype),
        input_output_aliases={0: 0},
        scratch_shapes=[pltpu.SemaphoreType.DMA((n_copies,)), pltpu.SemaphoreType.DMA((n_copies,))],
    )(buf)


def _allgather_conv_w(cw):
    _, kw, cs = cw.shape

    def body(cw_ref, out_ref, send_sems, recv_sems):
        x, y, c, chips = _mesh_pos()
        me = _chip_id(x, y)
        out_ref[pl.ds(me, 1)] = cw_ref[...]
        copies = []
        for k, chip in enumerate(chips):
            cp = pltpu.make_async_remote_copy(
                src_ref=cw_ref.at[0], dst_ref=out_ref.at[me], send_sem=send_sems.at[k], recv_sem=recv_sems.at[k],
                device_id=(*chip, c), device_id_type=MESH)
            cp.start()
            copies.append(cp)
        for k, chip in enumerate(chips):
            pltpu.make_async_remote_copy(
                src_ref=cw_ref.at[0], dst_ref=out_ref.at[_chip_id(*chip)], send_sem=send_sems.at[k],
                recv_sem=recv_sems.at[k], device_id=(*chip, c), device_id_type=MESH).wait_recv()
        for cp in copies:
            cp.wait_send()

    return pl.pallas_call(
        body, name="allgather_conv_w",
        in_specs=[_VMEM], out_specs=_VMEM,
        out_shape=jax.ShapeDtypeStruct((N_CHIPS, kw, cs), F32),
        scratch_shapes=[pltpu.SemaphoreType.DMA((3,)), pltpu.SemaphoreType.DMA((3,))],
    )(cw)


def _exchange_halves(parts, after, name):
    n = len(parts)
    shapes = [p.shape for p in parts]

    def body(*refs):
        ins, outs = refs[:n], refs[n + 1:2 * n + 1]
        send_sems, recv_sems = refs[2 * n + 1:]
        x, y, c, _ = _mesh_pos()
        copies = []
        for w in range(n):
            hr = shapes[w][1] // 2
            cp = pltpu.make_async_remote_copy(
                src_ref=ins[w].at[:, pl.ds((1 - c) * hr, hr)], dst_ref=outs[w],
                send_sem=send_sems.at[w], recv_sem=recv_sems.at[w],
                device_id=(x, y, 1 - c), device_id_type=MESH)
            cp.start()
            copies.append(cp)
        for cp in copies:
            cp.wait()

    return pl.pallas_call(
        body, name=name,
        in_specs=[_HBM] * n + [_ANY], out_specs=[_HBM] * n,
        out_shape=[jax.ShapeDtypeStruct((s[0], s[1] // 2, s[2]), BF16) for s in shapes],
        scratch_shapes=[pltpu.SemaphoreType.DMA((n,)), pltpu.SemaphoreType.DMA((n,))],
    )(*parts, after)


def _add_halves(part, got, cvec, name):
    ns, r, cdim = part.shape
    hr = r // 2
    tr = _row_tile(hr, TR_ELT)
    nblk = hr // tr

    def body(c_ref, a_ref, b_ref, o_ref):
        o_ref[...] = (a_ref[...].astype(F32) + b_ref[...].astype(F32)).astype(BF16)

    grid_spec = pltpu.PrefetchScalarGridSpec(
        num_scalar_prefetch=1, grid=(ns, nblk),
        in_specs=[pl.BlockSpec((None, tr, cdim), lambda s, i, c_ref: (s, c_ref[0] * nblk + i, 0)),
                  pl.BlockSpec((None, tr, cdim), lambda s, i, c_ref: (s, i, 0))],
        out_specs=pl.BlockSpec((None, tr, cdim), lambda s, i, c_ref: (s, i, 0)))
    return pl.pallas_call(
        body, name=name, grid_spec=grid_spec,
        out_shape=jax.ShapeDtypeStruct((ns, hr, cdim), BF16),
        compiler_params=_params(("parallel", "parallel")),
    )(cvec, part, got)


def _scatter_copy(sums_ref, land_ref, k, src_slot, dst_slot, c, send_sems, recv_sems, to):
    return pltpu.make_async_remote_copy(
        src_ref=sums_ref.at[src_slot], dst_ref=land_ref.at[dst_slot, _half_rows(land_ref, c)],
        send_sem=send_sems.at[k], recv_sem=recv_sems.at[k], device_id=to, device_id_type=MESH)


def _scatter_start(sums, name):
    ns, hr, cdim = sums.shape
    land = lax.empty((ns, 2 * hr, cdim), sums.dtype)

    def body(sums_ref, land_ref, send_sems, recv_sems, sums_thru, land_thru):
        x, y, c, chips = _mesh_pos()
        me = _chip_id(x, y)
        for k, chip in enumerate(chips):
            _scatter_copy(sums_ref, land_ref, k, _chip_id(*chip), me, c, send_sems, recv_sems, (*chip, c)).start()

    return pl.pallas_call(
        body, name=name,
        in_specs=[_HBM, _HBM], out_specs=[_SEM, _SEM, _HBM, _HBM],
        out_shape=[pltpu.SemaphoreType.DMA((3,)), pltpu.SemaphoreType.DMA((3,)),
                   pltpu.HBM(sums.shape, sums.dtype), pltpu.HBM(land.shape, land.dtype)],
        input_output_aliases={0: 2, 1: 3},
        compiler_params=pltpu.CompilerParams(has_side_effects=_EFFECT),
    )(pltpu.with_memory_space_constraint(sums, pltpu.HBM), pltpu.with_memory_space_constraint(land, pltpu.HBM))


def _scatter_wait(send_sems, recv_sems, sums, land, after, name):
    def body(sums_ref, land_ref, send_ref, recv_ref, after_ref, sums_out, land_out):
        x, y, c, chips = _mesh_pos()
        me = _chip_id(x, y)
        for k, chip in enumerate(chips):
            _scatter_copy(sums_ref, land_ref, k, _chip_id(*chip), me, c, send_ref, recv_ref, (*chip, c)).wait_send()
        for k, chip in enumerate(chips):
            _scatter_copy(sums_ref, land_ref, k, me, _chip_id(*chip), c, send_ref, recv_ref, (*chip, c)).wait_recv()

    return pl.pallas_call(
        body, name=name,
        in_specs=[_HBM, _HBM, _SEM, _SEM, _ANY], out_specs=[_HBM, _HBM],
        out_shape=[pltpu.HBM(sums.shape, sums.dtype), pltpu.HBM(land.shape, land.dtype)],
        input_output_aliases={0: 0, 1: 1},
        compiler_params=pltpu.CompilerParams(has_side_effects=_EFFECT),
    )(sums, land, send_sems, recv_sems, after)


def _complete_plan(n_weights):
    def plan(refs, send_sems, recv_sems):
        x, y, c, chips = _mesh_pos()
        me = _chip_id(x, y)
        sibling = (x, y, 1 - c)
        sends, recvs = [], []
        for w in range(n_weights):
            sums, land = refs[2 * w], refs[2 * w + 1]
            sends.append(_scatter_copy(sums, land, 4 * w + 3, me, me, c, send_sems, recv_sems, sibling))
            recvs.append(_scatter_copy(sums, land, 4 * w + 3, me, me, 1 - c, send_sems, recv_sems, sibling))
            for k, chip in enumerate(chips):
                slot = _chip_id(*chip)
                sends.append(_chip_copy(land, 4 * w + k, slot, _half_rows(land, c), send_sems, recv_sems, sibling))
                recvs.append(_chip_copy(land, 4 * w + k, slot, _half_rows(land, 1 - c), send_sems, recv_sems, sibling))
        return sends, recvs
    return plan


def _complete_chip_sums(sums, lands):
    n = len(sums)

    def body(*refs):
        sums_refs, outs = refs[:n], refs[2 * n:3 * n]
        send_sems, recv_sems = refs[3 * n:]
        x, y, c, chips = _mesh_pos()
        me = _chip_id(x, y)
        sibling = (x, y, 1 - c)
        slots = [_chip_id(*chip) for chip in chips]
        sent = []
        for w in range(n):
            out = outs[w]
            cp = _scatter_copy(sums_refs[w], out, 3, me, me, c, send_sems.at[w], recv_sems.at[w], sibling)
            cp.start()
            sent.append(cp)
            for k, slot in enumerate(slots):
                cp = _chip_copy(out, k, slot, _half_rows(out, c), send_sems.at[w], recv_sems.at[w], sibling)
                cp.start()
                sent.append(cp)
        for w in range(n):
            out = outs[w]
            _scatter_copy(sums_refs[w], out, 3, me, me, 1 - c, send_sems.at[w], recv_sems.at[w], sibling).wait_recv()
            for k, slot in enumerate(slots):
                _chip_copy(out, k, slot, _half_rows(out, 1 - c), send_sems.at[w], recv_sems.at[w], sibling).wait_recv()
        for cp in sent:
            cp.wait_send()

    return pl.pallas_call(
        body, name="complete_chip_sums",
        in_specs=[_HBM] * (2 * n), out_specs=[_HBM] * n,
        out_shape=[jax.ShapeDtypeStruct(b.shape, b.dtype) for b in lands],
        input_output_aliases={n + w: w for w in range(n)},
        scratch_shapes=[pltpu.SemaphoreType.DMA((n, 4)), pltpu.SemaphoreType.DMA((n, 4))],
    )(*sums, *lands)


SMALL_ROWS = 8


def _allreduce_small(gl2g, gl2b, gl1g, gl1b, g_ac, gcw, gsink, loss, after):
    d = gl2g.shape[1]
    hd = d // 2
    nq = gsink.shape[1]

    def body(a_ref, b_ref, c_ref, d_ref, e_ref, cw_ref, sk_ref, ls_ref, after_ref, out_ref, mine, gath, send_sems,
             recv_sems):
        x, y, c, _ = _mesh_pos()
        me = 4 * x + 2 * y + c
        mine[...] = jnp.zeros_like(mine)
        mine[0:1, :] = a_ref[...]
        mine[1:2, :] = b_ref[...]
        mine[2:3, :] = c_ref[...]
        mine[3:4, :] = d_ref[...]
        mine[4:5, :] = e_ref[...]
        mine[5:6, 0:hd] = cw_ref[0:1, :]
        mine[5:6, hd:d] = cw_ref[1:2, :]
        mine[6:7, 0:hd] = cw_ref[2:3, :]
        mine[6:7, hd:hd + nq] = sk_ref[...]
        mine[6:7, hd + 128:hd + 256] = ls_ref[...]
        gath[pl.ds(me, 1)] = mine[...][None]
        copies = []
        for r in range(1, 8):
            peer = ((1 - x) if r & 4 else x, (1 - y) if r & 2 else y, (1 - c) if r & 1 else c)
            cp = pltpu.make_async_remote_copy(
                src_ref=mine, dst_ref=gath.at[me], send_sem=send_sems.at[r - 1], recv_sem=recv_sems.at[r - 1],
                device_id=peer, device_id_type=MESH)
            cp.start()
            copies.append(cp)
        for r in range(1, 8):
            peer = ((1 - x) if r & 4 else x, (1 - y) if r & 2 else y, (1 - c) if r & 1 else c)
            peer_id = 4 * peer[0] + 2 * peer[1] + peer[2]
            pltpu.make_async_remote_copy(
                src_ref=mine, dst_ref=gath.at[peer_id], send_sem=send_sems.at[r - 1], recv_sem=recv_sems.at[r - 1],
                device_id=peer, device_id_type=MESH).wait_recv()
        for cp in copies:
            cp.wait_send()
        total = gath[0]
        for dev in range(1, 8):
            total = total + gath[dev]
        out_ref[...] = total

    return pl.pallas_call(
        body, name="allreduce_small",
        in_specs=[_VMEM] * 8 + [_ANY], out_specs=_VMEM,
        out_shape=jax.ShapeDtypeStruct((SMALL_ROWS, d), F32),
        scratch_shapes=[pltpu.VMEM((SMALL_ROWS, d), F32), pltpu.VMEM((8, SMALL_ROWS, d), F32),
                        pltpu.SemaphoreType.DMA((7,)), pltpu.SemaphoreType.DMA((7,))],
    )(gl2g, gl2b, gl1g, gl1b, g_ac, gcw, gsink, loss, after)


def _adamw(w, g, m, v):
    m = ADAM_B1 * m + (1.0 - ADAM_B1) * g
    v = ADAM_B2 * v + (1.0 - ADAM_B2) * (g * g)
    m_hat = m / (1.0 - ADAM_B1 ** ADAM_STEP)
    v_hat = v / (1.0 - ADAM_B2 ** ADAM_STEP)
    delta = -ADAM_LR * (m_hat / (jnp.sqrt(v_hat) + ADAM_EPS) + ADAM_WD * w)
    return delta, m, v


def _adamw_shard(w, m, v, land, own, pos_vec, name, col_block=0):
    _, r, c = w.shape
    hr = r // 2
    tr = _row_tile(hr, TR_ELT)
    nh = hr // tr

    def body(pos_ref, w_ref, m_ref, v_ref, l0, l1, l2, l3, own_ref, g_out, d_out, m_out, v_out):
        i = pl.program_id(0)
        mine = (i // nh) == pos_ref[1]
        own_blk = own_ref[...].astype(F32)
        g = None
        for s, l_ref in enumerate([l0, l1, l2, l3]):
            term = jnp.where(mine & (pos_ref[0] == s), own_blk, l_ref[...].astype(F32))
            g = term if g is None else g + term
        delta, nm, nv = _adamw(w_ref[...], g, m_ref[...], v_ref[...])
        g_out[...] = g
        d_out[...] = delta
        m_out[...] = nm
        v_out[...] = nv

    def land_spec(s):
        def index(i, pos_ref):
            skip = (pos_ref[0] == s) & ((i // nh) == pos_ref[1])
            return (s, jnp.where(skip, (i + nh) % (2 * nh), i), col_block)
        return pl.BlockSpec((None, tr, c), index)

    blk = pl.BlockSpec((None, tr, c), lambda i, pos_ref: (0, i, 0))
    grid_spec = pltpu.PrefetchScalarGridSpec(
        num_scalar_prefetch=1, grid=(2 * nh,),
        in_specs=[blk, blk, blk] + [land_spec(s) for s in range(N_CHIPS)]
        + [pl.BlockSpec((None, tr, c), lambda i, pos_ref: (pos_ref[0], i % nh, col_block))],
        out_specs=[blk] * 4)
    return pl.pallas_call(
        body, name=name, grid_spec=grid_spec,
        out_shape=[jax.ShapeDtypeStruct((1, r, c), F32)] * 4,
        compiler_params=_params(("parallel",)),
    )(pos_vec, w, m, v, land, land, land, land, own)


def _adamw_small(red, params):
    names = ["sinks", "g_attn", "g_conv", "ln1_g", "ln1_b", "ln2_g", "ln2_b", "conv_w"]
    d = red.shape[1]
    hd = d // 2
    flat = []
    for nme in names:
        flat.extend(params[nme])
    nq = params["sinks"][0].shape[1]
    cs = params["conv_w"][0].shape[2]

    def body(*refs):
        red_ref = refs[0]
        ins = refs[1:1 + 3 * len(names)]
        outs = refs[1 + 3 * len(names):]
        x, y, _, _ = _mesh_pos()
        me = _chip_id(x, y)

        def conv_tap(row, base):
            picked = red_ref[row:row + 1, base:base + cs]
            for s in range(1, N_CHIPS):
                picked = jnp.where(me == s, red_ref[row:row + 1, base + s * cs:base + (s + 1) * cs], picked)
            return picked

        grads = {
            "sinks": red_ref[6:7, hd:hd + nq],
            "g_attn": red_ref[4:5, 0:hd],
            "g_conv": red_ref[4:5, hd:d],
            "ln1_g": red_ref[2:3, :],
            "ln1_b": red_ref[3:4, :],
            "ln2_g": red_ref[0:1, :],
            "ln2_b": red_ref[1:2, :],
        }
        for i, nme in enumerate(names):
            w_ref, m_ref, v_ref = ins[3 * i:3 * i + 3]
            g_out, d_out, m_out, v_out = outs[4 * i:4 * i + 4]
            if nme == "conv_w":
                for tap, (row, base) in enumerate([(5, 0), (5, hd), (6, 0)]):
                    g = conv_tap(row, base)
                    delta, nm, nv = _adamw(w_ref[0, tap:tap + 1, :], g, m_ref[0, tap:tap + 1, :], v_ref[0, tap:tap + 1, :])
                    g_out[0, tap:tap + 1, :] = g
                    d_out[0, tap:tap + 1, :] = delta
                    m_out[0, tap:tap + 1, :] = nm
                    v_out[0, tap:tap + 1, :] = nv
            else:
                g = grads[nme]
                delta, nm, nv = _adamw(w_ref[...], g, m_ref[...], v_ref[...])
                g_out[...] = g
                d_out[...] = delta
                m_out[...] = nm
                v_out[...] = nv

    out_shape = []
    for nme in names:
        out_shape.extend([jax.ShapeDtypeStruct(params[nme][0].shape, F32)] * 4)
    outs = pl.pallas_call(
        body, name="adamw_small",
        in_specs=[_VMEM] * (1 + len(flat)), out_specs=[_VMEM] * len(out_shape),
        out_shape=out_shape,
    )(red, *flat)
    return {nme: tuple(outs[4 * i:4 * i + 4]) for i, nme in enumerate(names)}


def _rope_tables(pos_col):
    s = pos_col.shape[0]
    w = N_KV_HEADS * HEAD_DIM
    tb = min(512, s)
    inv_freq = (ROPE_THETA ** (-np.arange(0, ROT_DIM, 2, dtype=np.float32) / ROT_DIM)).astype(np.float32)

    def body(pos_ref, cos_ref, sin_ref):
        pos = pos_ref[...].astype(F32)
        lane = lax.broadcasted_iota(jnp.int32, (tb, PAIR), 1) & (HEAD_DIM - 1)
        fidx = lane & (ROT_DIM // 2 - 1)
        inv = jnp.zeros((tb, PAIR), F32)
        for k in range(ROT_DIM // 2):
            inv = jnp.where(fidx == k, float(inv_freq[k]), inv)
        ang = pos * inv
        rot = lane < ROT_DIM
        sin_v = jnp.sin(ang)
        cos_ref[...] = _tile_lanes(jnp.where(rot, jnp.cos(ang), 1.0), w // PAIR)
        sin_ref[...] = _tile_lanes(jnp.where(lane < ROT_DIM // 2, -sin_v, jnp.where(rot, sin_v, 0.0)), w // PAIR)

    return pl.pallas_call(
        body, name="rope_tables", grid=(s // tb,),
        in_specs=[pl.BlockSpec((tb, 1), lambda i: (i, 0))],
        out_specs=[pl.BlockSpec((tb, w), lambda i: (i, 0))] * 2,
        out_shape=[jax.ShapeDtypeStruct((s, w), F32)] * 2,
        compiler_params=_params(("parallel",)),
    )(pos_col)


def _in_proj(x, w_in_g):
    _, s, d = x.shape
    ns, _, ncol = w_in_g.shape
    tm = min(2 * TM, s)

    def body(x_ref, w_ref, o_ref):
        o_ref[...] = _dot(x_ref[...].astype(BF16), w_ref[...]).astype(BF16)

    return pl.pallas_call(
        body, name="in_proj", grid=(s // tm, ns),
        in_specs=[pl.BlockSpec((None, tm, d), lambda i, j: (0, i, 0)),
                  pl.BlockSpec((None, d, ncol), lambda i, j: (j, 0, 0))],
        out_specs=pl.BlockSpec((tm, ncol), lambda i, j: (i, j)),
        out_shape=jax.ShapeDtypeStruct((s, ns * ncol), BF16),
        compiler_params=_params(("parallel", "arbitrary")),
    )(x, w_in_g)


PAIR = 2 * HEAD_DIM
KEYS = 2 * WINDOW


def _pair_operand(t_all, h):
    col = (h // 2) * PAIR
    lane = lax.broadcasted_iota(jnp.int32, (KEYS, PAIR), 1)
    own_low = h % 2 == 0
    mine = jnp.where((lane < HEAD_DIM) if own_low else (lane >= HEAD_DIM), t_all[:, col:col + PAIR], 0.0)
    other = pltpu.roll(mine, HEAD_DIM, 1)
    low, high = (mine, other) if own_low else (other, mine)
    return jnp.concatenate([low, high], axis=0).astype(BF16)


def _pair_grad(acc, h):
    lane = lax.broadcasted_iota(jnp.int32, (KEYS, PAIR), 1)
    low = jnp.where(lane < HEAD_DIM, acc[:KEYS], 0.0)
    high = jnp.where(lane >= HEAD_DIM, acc[KEYS:], 0.0)
    if h % 2 == 0:
        return low + pltpu.roll(high, HEAD_DIM, 1)
    return high + pltpu.roll(low, HEAD_DIM, 1)


N_PAIRS = N_KV_HEADS * GROUP // 2


def _all_probs(q, kk2s, first, sinks_ref):
    assert ATTN_SCALE == 0.125
    q = q * ATTN_SCALE
    qps, scores = [], []
    for pair in range(N_PAIRS):
        qp = q[:, pair * PAIR:(pair + 1) * PAIR].astype(BF16)
        qps.append(qp)
        scores.append(_dot_nt(qp, kk2s[pair // (GROUP // 2)]))
    qi = lax.broadcasted_iota(jnp.int32, (WINDOW, 2 * KEYS), 0)
    kj = lax.broadcasted_iota(jnp.int32, (WINDOW, 2 * KEYS), 1) & (KEYS - 1)
    rel = qi + WINDOW - kj
    valid = (rel >= 0) & (rel < WINDOW) & jnp.logical_not(first & (kj < WINDOW))
    bias = jnp.where(valid, 0.0, NEG_BIG)
    s = (jnp.stack(scores, axis=0) + bias[None]).reshape(N_PAIRS * WINDOW, 2 * KEYS)
    probs, p_sinks = [], []
    for t in range(2):
        st = s[:, t * KEYS:(t + 1) * KEYS]
        sink = jnp.concatenate([jnp.broadcast_to(sinks_ref[0:1, 2 * pair + t:2 * pair + t + 1], (WINDOW, 1))
                                for pair in range(N_PAIRS)], axis=0)
        m = jnp.maximum(jnp.max(st, axis=1, keepdims=True), sink)
        e = jnp.exp(st - m)
        e_sink = jnp.exp(sink - m)
        inv_l = 1.0 / (jnp.sum(e, axis=1, keepdims=True) + e_sink)
        probs.append(e * inv_l)
        p_sinks.append(e_sink * inv_l)
    return qps, jnp.concatenate(probs, axis=1), p_sinks


def _roped_qkv(cur_ref, prev_ref, cos_ref, sin_ref, cosp_ref, sinp_ref, qw, kvw):
    cur = cur_ref[...].astype(F32)
    cos, sin = cos_ref[...], sin_ref[...]
    cos_q, sin_q = _tile_lanes(cos, GROUP), _tile_lanes(sin, GROUP)
    q = _rope(cur[:, :qw], cos_q, sin_q, 1.0)
    prev = prev_ref[...].astype(F32)
    k_all = jnp.concatenate([_rope(prev[:, :kvw], cosp_ref[...], sinp_ref[...], 1.0),
                             _rope(cur[:, qw:qw + kvw], cos, sin, 1.0)], axis=0)
    v_all = jnp.concatenate([prev[:, kvw:], cur[:, qw + kvw:]], axis=0)
    return q, k_all, v_all, cos_q, sin_q


def _attention_fwd(proj, cos_t, sin_t, sinks):
    s = proj.shape[0]
    qw = GROUP * N_KV_HEADS * HEAD_DIM
    kvw = N_KV_HEADS * HEAD_DIM
    nb = s // WINDOW

    def body(cur_ref, prev_ref, cos_ref, sin_ref, cosp_ref, sinp_ref, sinks_ref, o_ref):
        first = pl.program_id(0) == 0
        q, k_all, v_all, _, _ = _roped_qkv(cur_ref, prev_ref, cos_ref, sin_ref, cosp_ref, sinp_ref, qw, kvw)
        kk2s = [_pair_operand(k_all, h) for h in range(N_KV_HEADS)]
        vv2s = [_pair_operand(v_all, h) for h in range(N_KV_HEADS)]
        _, probs, _ = _all_probs(q, kk2s, first, sinks_ref)
        probs = probs.astype(BF16)
        outs = [_dot(probs[pair * WINDOW:(pair + 1) * WINDOW], vv2s[pair // (GROUP // 2)]) for pair in range(N_PAIRS)]
        o_ref[...] = jnp.concatenate(outs, axis=1)

    tbl = pl.BlockSpec((WINDOW, kvw), lambda n: (n, 0))
    tbl_prev = pl.BlockSpec((WINDOW, kvw), lambda n: (jnp.maximum(n - 1, 0), 0))
    return pl.pallas_call(
        body, name="attention_fwd", grid=(nb,),
        in_specs=[pl.BlockSpec((WINDOW, qw + 2 * kvw), lambda n: (n, 0)),
                  pl.BlockSpec((WINDOW, 2 * kvw), lambda n: (jnp.maximum(n - 1, 0), (qw // (2 * kvw)))),
                  tbl, tbl, tbl_prev, tbl_prev, _VMEM],
        out_specs=pl.BlockSpec((WINDOW, qw), lambda n: (n, 0)),
        out_shape=jax.ShapeDtypeStruct((s, qw), F32),
        compiler_params=_params(("parallel",)),
    )(proj, proj, cos_t, sin_t, cos_t, sin_t, sinks)


def _conv_taps(cw_ref):
    return [jnp.concatenate([cw_ref[s, k:k + 1, :] for s in range(N_CHIPS)], axis=1) for k in range(3)]


def _shift_down(z, halo, steps):
    last = halo.shape[0]
    row = lax.broadcasted_iota(jnp.int32, z.shape, 0)
    out = pltpu.roll(z, steps, 0)
    for r in range(steps):
        out = jnp.where(row == r, halo[last - steps + r:last - steps + r + 1, :], out)
    return out


def _shift_up(z, halo, steps):
    rows = z.shape[0]
    row = lax.broadcasted_iota(jnp.int32, z.shape, 0)
    out = pltpu.roll(z, rows - steps, 0)
    for r in range(steps):
        out = jnp.where(row == rows - steps + r, halo[r:r + 1, :], out)
    return out


def _split_cbu(lo, hi, cw):
    lo, hi = lo.astype(F32), hi.astype(F32)
    c_gate = lo[:, :cw]
    b_gate = jnp.concatenate([lo[:, cw:], hi[:, :2 * cw - lo.shape[1]]], axis=1)
    u = hi[:, 2 * cw - lo.shape[1]:]
    return c_gate, b_gate, u


def _conv_norm(proj, attn, cw_full, g_ac):
    s, in_w = proj.shape
    cw = attn.shape[1]
    blk_w = in_w // 3
    tb = min(TB_CONV, s)

    def body(lo_ref, hi_ref, lo_h_ref, hi_h_ref, attn_ref, cw_ref, g_ref, mixed_ref, ac_ref, rstd_ref):
        i = pl.program_id(0)
        c_gate, b_gate, u = _split_cbu(lo_ref[...], hi_ref[...], cw)
        c_h, _, u_h = _split_cbu(lo_h_ref[...], hi_h_ref[...], cw)
        z = c_gate * u
        z_h = jnp.where(i == 0, 0.0, c_h * u_h)
        w0, w1, w2 = _conv_taps(cw_ref)
        y = w0 * _shift_down(z, z_h, 2) + w1 * _shift_down(z, z_h, 1) + w2 * z
        conv = b_gate * y
        a = attn_ref[...]
        r_a = lax.rsqrt(jnp.mean(a * a, axis=-1, keepdims=True) + RMS_EPS)
        r_c = lax.rsqrt(jnp.mean(conv * conv, axis=-1, keepdims=True) + RMS_EPS)
        g = g_ref[...]
        mixed_ref[...] = jnp.concatenate([a * r_a * g[:, :cw], conv * r_c * g[:, cw:]], axis=1).astype(BF16)
        ac_ref[...] = jnp.concatenate([a, conv], axis=1)
        rstd_ref[0] = r_a
        rstd_ref[1] = r_c

    halo_idx = lambda i: jnp.maximum(i * (tb // HALO_ROWS) - 1, 0)
    return pl.pallas_call(
        body, name="conv_norm", grid=(s // tb,),
        in_specs=[pl.BlockSpec((tb, blk_w), lambda i: (i, 1)),
                  pl.BlockSpec((tb, blk_w), lambda i: (i, 2)),
                  pl.BlockSpec((HALO_ROWS, blk_w), lambda i: (halo_idx(i), 1)),
                  pl.BlockSpec((HALO_ROWS, blk_w), lambda i: (halo_idx(i), 2)),
                  pl.BlockSpec((tb, cw), lambda i: (i, 0)),
                  _VMEM, _VMEM],
        out_specs=[pl.BlockSpec((tb, 2 * cw), lambda i: (i, 0)),
                   pl.BlockSpec((tb, 2 * cw), lambda i: (i, 0)),
                   pl.BlockSpec((2, tb, 1), lambda i: (0, i, 0))],
        out_shape=[jax.ShapeDtypeStruct((s, 2 * cw), BF16), jax.ShapeDtypeStruct((s, 2 * cw), F32),
                   jax.ShapeDtypeStruct((2, s, 1), F32)],
        compiler_params=_params(("parallel",)),
    )(proj, proj, proj, proj, attn, cw_full, g_ac)


def _out_proj_ln(mixed, w_out_g, x, ln_g, ln_b):
    s, d = mixed.shape
    tm = min(TM, s)
    tk = d
    nk = d // tk

    def body(a_ref, w_ref, x_ref, g_ref, b_ref, xhat_ref, h_ref, rstd_ref, acc):
        k = pl.program_id(1)
        _accumulate(acc, lambda: _dot(a_ref[...], w_ref[...]), k, nk)

        @pl.when(k == nk - 1)
        def _():
            def rows_fn(rows):
                xhat, rstd = _ln_fwd(ALPHA * x_ref[rows, :] + acc[rows, :])
                xhat_ref[rows, :] = xhat
                h_ref[rows, :] = (xhat * g_ref[...] + b_ref[...]).astype(BF16)
                rstd_ref[rows, :] = rstd

            _for_row_chunks(tm, rows_fn)

    row = pl.BlockSpec((tm, d), lambda i, k: (i, 0))
    return pl.pallas_call(
        body, name="out_proj_ln", grid=(s // tm, nk),
        in_specs=[pl.BlockSpec((tm, tk), lambda i, k: (i, k)),
                  pl.BlockSpec((tk, d), lambda i, k: (k, 0)),
                  pl.BlockSpec((None, tm, d), lambda i, k: (0, i, 0)),
                  _VMEM, _VMEM],
        out_specs=[row, row, pl.BlockSpec((tm, 1), lambda i, k: (i, 0))],
        out_shape=[jax.ShapeDtypeStruct((s, d), F32), jax.ShapeDtypeStruct((s, d), BF16),
                   jax.ShapeDtypeStruct((s, 1), F32)],
        scratch_shapes=[pltpu.VMEM((tm, d), F32)],
        compiler_params=_params(("parallel", "arbitrary")),
    )(mixed, w_out_g, x, ln_g, ln_b)


def _gate_up(h1, w_gu_g):
    s, d = h1.shape
    ns, _, fs2 = w_gu_g.shape
    fs = fs2 // 2
    tm = min(TM, s)

    def body(h_ref, w_ref, act_ref, ab_ref):
        gu = _dot(h_ref[...], w_ref[...])
        g, u = gu[:, :fs], gu[:, fs:]
        sg = _sigmoid(g)
        silu = g * sg
        act_ref[...] = (silu * u).astype(BF16)
        ab_ref[:, :fs] = (u * (sg * (1.0 + g * (1.0 - sg)))).astype(BF16)
        ab_ref[:, fs:] = silu.astype(BF16)

    return pl.pallas_call(
        body, name="gate_up", grid=(s // tm, ns),
        in_specs=[pl.BlockSpec((tm, d), lambda i, j: (i, 0)), pl.BlockSpec((None, d, fs2), lambda i, j: (j, 0, 0))],
        out_specs=[pl.BlockSpec((tm, fs), lambda i, j: (i, j)), pl.BlockSpec((tm, fs2), lambda i, j: (i, j))],
        out_shape=[jax.ShapeDtypeStruct((s, ns * fs), BF16), jax.ShapeDtypeStruct((s, ns * fs2), BF16)],
        compiler_params=_params(("parallel", "arbitrary")),
    )(h1, w_gu_g)


def _down_ln_loss(act, w_down_g, xhat1, ln1_g, ln1_b, ln2_g, ln2_b, target):
    s, f = act.shape
    d = xhat1.shape[1]
    tm = min(TM, s)
    tk = f // N_CHIPS
    nk = f // tk

    def body(a_ref, w_ref, xh_ref, g1_ref, b1_ref, g2_ref, b2_ref, t_ref, dpre_ref, dpre16_ref, loss_ref, gg_ref, gb_ref,
             acc):
        i, k = pl.program_id(0), pl.program_id(1)
        _accumulate(acc, lambda: _dot(a_ref[...], w_ref[...]), k, nk)

        @pl.when(k == nk - 1)
        def _():
            @pl.when(i == 0)
            def _():
                loss_ref[...] = jnp.zeros_like(loss_ref)
                gg_ref[...] = jnp.zeros_like(gg_ref)
                gb_ref[...] = jnp.zeros_like(gb_ref)

            def rows_fn(rows):
                h1 = xh_ref[rows, :] * g1_ref[...] + b1_ref[...]
                xhat, rstd = _ln_fwd(ALPHA * h1 + acc[rows, :])
                g2 = g2_ref[...]
                diff = xhat * g2 + b2_ref[...] - t_ref[rows, :]
                dy = diff * (1.0 / d)
                dpre = _ln_bwd(dy, xhat, rstd, g2)
                dpre_ref[rows, :] = dpre
                dpre16_ref[rows, :] = dpre.astype(BF16)
                sq = jnp.sum(jnp.sum(diff * diff, axis=1, keepdims=True), axis=0, keepdims=True)
                loss_ref[...] += jnp.broadcast_to(sq * (0.5 / d), (1, 128))
                gg_ref[...] += jnp.sum(dy * xhat, axis=0, keepdims=True)
                gb_ref[...] += jnp.sum(dy, axis=0, keepdims=True)

            _for_row_chunks(tm, rows_fn)

    row = pl.BlockSpec((tm, d), lambda i, k: (i, 0))
    vec = pl.BlockSpec((1, d), lambda i, k: (0, 0))
    return pl.pallas_call(
        body, name="down_ln_loss", grid=(s // tm, nk),
        in_specs=[pl.BlockSpec((tm, tk), lambda i, k: (i, k)),
                  pl.BlockSpec((tk, d), lambda i, k: (k, 0)),
                  row, _VMEM, _VMEM, _VMEM, _VMEM,
                  pl.BlockSpec((None, tm, d), lambda i, k: (0, i, 0))],
        out_specs=[row, row, pl.BlockSpec((1, 128), lambda i, k: (0, 0)), vec, vec],
        out_shape=[jax.ShapeDtypeStruct((s, d), F32), jax.ShapeDtypeStruct((s, d), BF16),
                   jax.ShapeDtypeStruct((1, 128), F32), jax.ShapeDtypeStruct((1, d), F32),
                   jax.ShapeDtypeStruct((1, d), F32)],
        scratch_shapes=[pltpu.VMEM((tm, d), F32)],
        compiler_params=_params(("arbitrary", "arbitrary")),
    )(act, w_down_g, xhat1, ln1_g, ln1_b, ln2_g, ln2_b, target)


def _dact_silu_bwd(dpre2, w_down_g, ab):
    s, d = dpre2.shape
    fs2 = ab.shape[1] // N_CHIPS
    fs = fs2 // 2
    tm = min(TM, s)

    def body(dp_ref, w_ref, ab_ref, dgu_ref):
        d_act = _dot_nt(dp_ref[...], w_ref[...])
        dgu_ref[:, :fs] = (d_act * ab_ref[:, :fs].astype(F32)).astype(BF16)
        dgu_ref[:, fs:] = (d_act * ab_ref[:, fs:].astype(F32)).astype(BF16)

    blk = pl.BlockSpec((tm, fs2), lambda j, i: (i, j))
    return pl.pallas_call(
        body, name="dact_silu_bwd", grid=(N_CHIPS, s // tm),
        in_specs=[pl.BlockSpec((tm, d), lambda j, i: (i, 0)),
                  pl.BlockSpec((fs, d), lambda j, i: (j, 0)), blk],
        out_specs=blk,
        out_shape=jax.ShapeDtypeStruct(ab.shape, BF16),
        compiler_params=_params(("parallel", "parallel")),
    )(dpre2, w_down_g, ab)


def _grad_rows(a, b, after, name, row_blocks=1):
    s, m = a.shape
    n = b.shape[1]
    ms = m // N_CHIPS
    tmw = ms // row_blocks
    tk = min(TK_TOK, s)
    nk = s // tk

    def body(a_ref, b_ref, after_ref, o_ref, acc):
        k = pl.program_id(2)
        _accumulate(acc, lambda: _dot_tn(a_ref[...].astype(BF16), b_ref[...].astype(BF16)), k, nk)

        @pl.when(k == nk - 1)
        def _():
            o_ref[...] = acc[...].astype(BF16)

    return pl.pallas_call(
        body, name=name, grid=(N_CHIPS, row_blocks, nk),
        in_specs=[pl.BlockSpec((tk, tmw), lambda j, r, k: (k, j * row_blocks + r)),
                  pl.BlockSpec((tk, n), lambda j, r, k: (k, 0)), _ANY],
        out_specs=pl.BlockSpec((None, tmw, n), lambda j, r, k: (j, r, 0)),
        out_shape=jax.ShapeDtypeStruct((N_CHIPS, ms, n), BF16),
        scratch_shapes=[pltpu.VMEM((tmw, n), F32)],
        compiler_params=_params(("parallel", "parallel", "arbitrary")),
    )(a, b, after)


def _grad_cols(a, bs, after, name, a_3d=False, row_blocks=2):
    s, m = a.shape[-2:]
    n = bs[0].shape[1]
    ns = n // N_CHIPS
    nb = len(bs)
    tmw = m // row_blocks
    tk = min(TK_TOK, s)
    nk = s // tk

    def body(*refs):
        a_ref, b_refs, o_refs, accs = refs[0], refs[1:1 + nb], refs[2 + nb:2 + 2 * nb], refs[2 + 2 * nb:]
        k = pl.program_id(2)
        for b_ref, acc in zip(b_refs, accs):
            _accumulate(acc, lambda b_ref=b_ref: _dot_tn(a_ref[...].astype(BF16), b_ref[...].astype(BF16)), k, nk)

        @pl.when(k == nk - 1)
        def _():
            for o_ref, acc in zip(o_refs, accs):
                o_ref[...] = acc[...].astype(BF16)

    if a_3d:
        a_spec = pl.BlockSpec((None, tk, tmw), lambda j, r, k: (0, k, r))
    else:
        a_spec = pl.BlockSpec((tk, tmw), lambda j, r, k: (k, r))
    return pl.pallas_call(
        body, name=name, grid=(N_CHIPS, row_blocks, nk),
        in_specs=[a_spec] + [pl.BlockSpec((tk, ns), lambda j, r, k: (k, j))] * nb + [_ANY],
        out_specs=[pl.BlockSpec((None, tmw, ns), lambda j, r, k: (j, r, 0))] * nb,
        out_shape=[jax.ShapeDtypeStruct((N_CHIPS, m, ns), BF16)] * nb,
        scratch_shapes=[pltpu.VMEM((tmw, ns), F32)] * nb,
        compiler_params=_params(("parallel", "parallel", "arbitrary")),
    )(a, *bs, after)


def _dh1_ln_bwd(d_gu, w_gu_g, dpre2, xhat1, rstd1, ln1_g, after):
    s = d_gu.shape[0]
    d = dpre2.shape[1]
    hd = d // 2
    fs = w_gu_g.shape[2]
    tm = min(TM, s)

    def body(dgu_ref, w_ref, dp2_ref, xh_ref, rs_ref, g_ref, after_ref, dpre_ref, gg_ref, gb_ref, acc_lo, acc_hi):
        i, j, half = pl.program_id(0), pl.program_id(1), pl.program_id(2)

        def product():
            return _dot_nt(dgu_ref[...], w_ref[...])

        @pl.when(half == 0)
        def _():
            _accumulate(acc_lo, product, j, N_CHIPS)

        @pl.when(half == 1)
        def _():
            _accumulate(acc_hi, product, j, N_CHIPS)

        @pl.when((j == N_CHIPS - 1) & (half == 1))
        def _():
            @pl.when(i == 0)
            def _():
                gg_ref[...] = jnp.zeros_like(gg_ref)
                gb_ref[...] = jnp.zeros_like(gb_ref)

            def rows_fn(rows):
                dh = jnp.concatenate([acc_lo[rows, :], acc_hi[rows, :]], axis=1) + ALPHA * dp2_ref[rows, :]
                xhat = xh_ref[rows, :]
                dpre_ref[rows, :] = _ln_bwd(dh, xhat, rs_ref[rows, :], g_ref[...])
                gg_ref[...] += jnp.sum(dh * xhat, axis=0, keepdims=True)
                gb_ref[...] += jnp.sum(dh, axis=0, keepdims=True)

            _for_row_chunks(tm, rows_fn)

    row = pl.BlockSpec((tm, d), lambda i, j, h: (i, 0))
    vec = pl.BlockSpec((1, d), lambda i, j, h: (0, 0))
    act_blk = pl.BlockSpec((tm, fs), lambda i, j, h: (i, j))
    w_blk = pl.BlockSpec((None, hd, fs), lambda i, j, h: (j, h, 0))
    return pl.pallas_call(
        body, name="dh1_ln_bwd", grid=(s // tm, N_CHIPS, 2),
        in_specs=[act_blk, w_blk, row, row, pl.BlockSpec((tm, 1), lambda i, j, h: (i, 0)), _VMEM, _ANY],
        out_specs=[row, vec, vec],
        out_shape=[jax.ShapeDtypeStruct((s, d), F32), jax.ShapeDtypeStruct((1, d), F32),
                   jax.ShapeDtypeStruct((1, d), F32)],
        scratch_shapes=[pltpu.VMEM((tm, hd), F32)] * 2,
        compiler_params=_params(("arbitrary", "arbitrary", "arbitrary")),
    )(d_gu, w_gu_g, dpre2, xhat1, rstd1, ln1_g, after)


def _dmixed_rms_bwd(dpre1, w_out_g, ac, rstd, g_ac):
    s, d = dpre1.shape
    hd = d // 2
    tm = min(TM, s)

    def body(dp_ref, w_ref, ac_ref, rs_ref, g_ref, dac_ref, gg_ref):
        i = pl.program_id(1)
        dm = _dot_nt(dp_ref[...].astype(BF16), w_ref[...])
        pre = ac_ref[...]
        r = rs_ref[...]
        gdm = dm * g_ref[...]
        dac_ref[...] = r * gdm - pre * (r * r * r) * jnp.mean(gdm * pre, axis=-1, keepdims=True)
        gg = jnp.sum(dm * pre * r, axis=0, keepdims=True)

        @pl.when(i == 0)
        def _():
            gg_ref[...] = gg

        @pl.when(i > 0)
        def _():
            gg_ref[...] += gg

    return pl.pallas_call(
        body, name="dmixed_rms_bwd", grid=(2, s // tm),
        in_specs=[pl.BlockSpec((tm, d), lambda h, i: (i, 0)),
                  pl.BlockSpec((hd, d), lambda h, i: (h, 0)),
                  pl.BlockSpec((tm, hd), lambda h, i: (i, h)),
                  pl.BlockSpec((None, tm, 1), lambda h, i: (h, i, 0)),
                  pl.BlockSpec((1, hd), lambda h, i: (0, h))],
        out_specs=[pl.BlockSpec((tm, hd), lambda h, i: (i, h)),
                   pl.BlockSpec((1, hd), lambda h, i: (0, h))],
        out_shape=[jax.ShapeDtypeStruct((s, d), F32), jax.ShapeDtypeStruct((1, d), F32)],
        compiler_params=_params(("arbitrary", "arbitrary")),
    )(dpre1, w_out_g, ac, rstd, g_ac)


def _attention_bwd(proj, d_ac, cos_t, sin_t, sinks, after):
    s = proj.shape[0]
    qw = GROUP * N_KV_HEADS * HEAD_DIM
    kvw = N_KV_HEADS * HEAD_DIM
    nb = s // WINDOW
    nq = GROUP * N_KV_HEADS

    def body(cur_ref, prev_ref, do_ref, cos_ref, sin_ref, cosp_ref, sinp_ref, sinks_ref, after_ref,
             dq_ref, dcur_ref, dprev_ref, dsink_ref):
        n = pl.program_id(0)
        first = n == 0
        q, k_all, v_all, cos_q, sin_q = _roped_qkv(cur_ref, prev_ref, cos_ref, sin_ref, cosp_ref, sinp_ref, qw, kvw)
        kk2s = [_pair_operand(k_all, h) for h in range(N_KV_HEADS)]
        vv2s = [_pair_operand(v_all, h) for h in range(N_KV_HEADS)]
        qps, probs, p_sinks = _all_probs(q, kk2s, first, sinks_ref)
        dops = [do_ref[:, pair * PAIR:(pair + 1) * PAIR].astype(BF16) for pair in range(N_PAIRS)]
        d_probs = jnp.concatenate([_dot_nt(dops[pair], vv2s[pair // (GROUP // 2)]) for pair in range(N_PAIRS)], axis=0)
        d_s, ds_sinks = [], []
        for t in range(2):
            cols = slice(t * KEYS, (t + 1) * KEYS)
            delta = jnp.sum(probs[:, cols] * d_probs[:, cols], axis=1, keepdims=True)
            d_s.append(probs[:, cols] * (d_probs[:, cols] - delta))
            ds_sinks.append(-p_sinks[t] * delta)
        d_s = jnp.concatenate(d_s, axis=1).astype(BF16)
        probs = probs.astype(BF16)
        dq_parts, dk_tiles, dv_tiles, dsink_parts = [], [], [], []
        for h in range(N_KV_HEADS):
            dkk2, dvv2 = None, None
            for p in range(GROUP // 2):
                pair = (GROUP // 2) * h + p
                rows = slice(pair * WINDOW, (pair + 1) * WINDOW)
                dq_parts.append(_dot(d_s[rows], kk2s[h]) * ATTN_SCALE)
                dk_term = _dot_tn(d_s[rows], qps[pair])
                dv_term = _dot_tn(probs[rows], dops[pair])
                dkk2 = dk_term if dkk2 is None else dkk2 + dk_term
                dvv2 = dv_term if dvv2 is None else dvv2 + dv_term
                dsink_parts.extend([jnp.sum(ds_sinks[t][rows], axis=0, keepdims=True) for t in range(2)])
            dk_tiles.append(_pair_grad(dkk2, h))
            dv_tiles.append(_pair_grad(dvv2, h))
        dq_ref[...] = _rope(jnp.concatenate(dq_parts, axis=1), cos_q, sin_q, -1.0)
        dk = jnp.concatenate([dk_tiles[0] + dk_tiles[1], dk_tiles[2] + dk_tiles[3]], axis=1)
        dv = jnp.concatenate([dv_tiles[0] + dv_tiles[1], dv_tiles[2] + dv_tiles[3]], axis=1)
        dprev_ref[...] = jnp.concatenate([dk[:WINDOW], dv[:WINDOW]], axis=1)
        dcur_ref[...] = jnp.concatenate([dk[WINDOW:], dv[WINDOW:]], axis=1)
        dsink = jnp.concatenate(dsink_parts, axis=1)

        @pl.when(first)
        def _():
            dsink_ref[...] = dsink

        @pl.when(n > 0)
        def _():
            dsink_ref[...] += dsink

    tbl = pl.BlockSpec((WINDOW, kvw), lambda n: (n, 0))
    tbl_prev = pl.BlockSpec((WINDOW, kvw), lambda n: (jnp.maximum(n - 1, 0), 0))
    kv_blk = pl.BlockSpec((WINDOW, 2 * kvw), lambda n: (n, 0))
    return pl.pallas_call(
        body, name="attention_bwd", grid=(nb,),
        in_specs=[pl.BlockSpec((WINDOW, qw + 2 * kvw), lambda n: (n, 0)),
                  pl.BlockSpec((WINDOW, 2 * kvw), lambda n: (jnp.maximum(n - 1, 0), (qw // (2 * kvw)))),
                  pl.BlockSpec((WINDOW, qw), lambda n: (n, 0)),
                  tbl, tbl, tbl_prev, tbl_prev, _VMEM, _ANY],
        out_specs=[pl.BlockSpec((WINDOW, qw), lambda n: (n, 0)), kv_blk, kv_blk,
                   pl.BlockSpec((1, nq), lambda n: (0, 0))],
        out_shape=[jax.ShapeDtypeStruct((s, qw), F32), jax.ShapeDtypeStruct((s, 2 * kvw), F32),
                   jax.ShapeDtypeStruct((s, 2 * kvw), F32), jax.ShapeDtypeStruct((1, nq), F32)],
        compiler_params=_params(("arbitrary",)),
    )(proj, proj, d_ac, cos_t, sin_t, cos_t, sin_t, sinks, after)


def _dproj_assemble(proj, d_ac, dq, dkv_cur, dkv_prev, cos_t, sin_t, cw_full):
    s, in_w = proj.shape
    cw = dq.shape[1]
    kvw = N_KV_HEADS * HEAD_DIM
    blk_w = in_w // 3
    tb = WINDOW
    nb = s // tb

    def body(lo_ref, hi_ref, lo_p_ref, hi_p_ref, lo_n_ref, hi_n_ref, dconv_ref, dconv_n_ref,
             dq_ref, dcur_ref, dprev_n_ref, cos_ref, sin_ref, cw_ref, dproj_ref, gcw_ref):
        i = pl.program_id(0)
        last = i == nb - 1
        c_gate, b_gate, u = _split_cbu(lo_ref[...], hi_ref[...], cw)
        c_p, _, u_p = _split_cbu(lo_p_ref[...], hi_p_ref[...], cw)
        _, b_n, _ = _split_cbu(lo_n_ref[...], hi_n_ref[...], cw)
        z = c_gate * u
        z_p = jnp.where(i == 0, 0.0, c_p * u_p)
        z1 = _shift_down(z, z_p, 1)
        z2 = _shift_down(z, z_p, 2)
        w0, w1, w2 = _conv_taps(cw_ref)
        y = w0 * z2 + w1 * z1 + w2 * z
        d_conv = dconv_ref[...]
        d_b = d_conv * y
        d_y = d_conv * b_gate
        d_y_n = jnp.where(last, 0.0, dconv_n_ref[...] * b_n[:dconv_n_ref.shape[0]])
        d_z = w2 * d_y + w1 * _shift_up(d_y, d_y_n, 1) + w0 * _shift_up(d_y, d_y_n, 2)
        d_c = d_z * u
        d_u = d_z * c_gate
        gcw = jnp.concatenate([jnp.sum(d_y * z2, axis=0, keepdims=True), jnp.sum(d_y * z1, axis=0, keepdims=True),
                               jnp.sum(d_y * z, axis=0, keepdims=True)], axis=0)

        @pl.when(i == 0)
        def _():
            gcw_ref[...] = gcw

        @pl.when(i > 0)
        def _():
            gcw_ref[...] += gcw

        dkv = dcur_ref[...] + jnp.where(last, 0.0, dprev_n_ref[...])
        dk = _rope(dkv[:, :kvw], cos_ref[...], sin_ref[...], -1.0)
        dproj_ref[...] = jnp.concatenate([dq_ref[...], dk, dkv[:, kvw:], d_c, d_b, d_u], axis=1).astype(BF16)

    prev_halo = lambda i: jnp.maximum(i * (tb // HALO_ROWS) - 1, 0)
    next_halo = lambda i: jnp.minimum((i + 1) * (tb // HALO_ROWS), s // HALO_ROWS - 1)
    next8 = lambda i: jnp.minimum((i + 1) * (tb // 8), s // 8 - 1)
    nxt = lambda i: jnp.minimum(i + 1, nb - 1)
    return pl.pallas_call(
        body, name="dproj_assemble", grid=(nb,),
        in_specs=[pl.BlockSpec((tb, blk_w), lambda i: (i, 1)),
                  pl.BlockSpec((tb, blk_w), lambda i: (i, 2)),
                  pl.BlockSpec((HALO_ROWS, blk_w), lambda i: (prev_halo(i), 1)),
                  pl.BlockSpec((HALO_ROWS, blk_w), lambda i: (prev_halo(i), 2)),
                  pl.BlockSpec((HALO_ROWS, blk_w), lambda i: (next_halo(i), 1)),
                  pl.BlockSpec((HALO_ROWS, blk_w), lambda i: (next_halo(i), 2)),
                  pl.BlockSpec((tb, cw), lambda i: (i, 1)),
                  pl.BlockSpec((8, cw), lambda i: (next8(i), 1)),
                  pl.BlockSpec((tb, cw), lambda i: (i, 0)),
                  pl.BlockSpec((tb, 2 * kvw), lambda i: (i, 0)),
                  pl.BlockSpec((tb, 2 * kvw), lambda i: (nxt(i), 0)),
                  pl.BlockSpec((tb, kvw), lambda i: (i, 0)),
                  pl.BlockSpec((tb, kvw), lambda i: (i, 0)),
                  _VMEM],
        out_specs=[pl.BlockSpec((tb, in_w), lambda i: (i, 0)),
                   pl.BlockSpec((3, cw), lambda i: (0, 0))],
        out_shape=[jax.ShapeDtypeStruct((s, in_w), BF16), jax.ShapeDtypeStruct((3, cw), F32)],
        compiler_params=_params(("arbitrary",)),
    )(proj, proj, proj, proj, proj, proj, d_ac, d_ac, dq, dkv_cur, dkv_prev, cos_t, sin_t, cw_full)


def _dx(d_proj, w_in_g, dpre1, after):
    s, in_w = d_proj.shape
    ns, d, ncol = w_in_g.shape
    tm = min(TM, s)

    def body(dp_ref, w_ref, r_ref, after_ref, o_ref, acc):
        j = pl.program_id(1)
        _accumulate(acc, lambda: _dot_nt(dp_ref[...], w_ref[...]), j, ns)

        @pl.when(j == ns - 1)
        def _():
            o_ref[...] = acc[...] + ALPHA * r_ref[...]

    return pl.pallas_call(
        body, name="dx", grid=(s // tm, ns),
        in_specs=[pl.BlockSpec((tm, ncol), lambda i, j: (i, j)),
                  pl.BlockSpec((None, d, ncol), lambda i, j: (j, 0, 0)),
                  pl.BlockSpec((tm, d), lambda i, j: (i, 0)), _ANY],
        out_specs=pl.BlockSpec((None, tm, d), lambda i, j: (0, i, 0)),
        out_shape=jax.ShapeDtypeStruct((1, s, d), F32),
        scratch_shapes=[pltpu.VMEM((tm, d), F32)],
        compiler_params=_params(("parallel", "arbitrary")),
    )(d_proj, w_in_g, dpre1, after)


def kernel(x, positions, w_in, conv_w, sinks, g_attn, g_conv, w_out, ln1_g, ln1_b, w_gate, w_up, w_down, ln2_g, ln2_b, loss_target, m_w_in, m_conv_w, m_sinks, m_g_attn, m_g_conv, m_w_out, m_ln1_g, m_ln1_b, m_w_gate, m_w_up, m_w_down, m_ln2_g, m_ln2_b, v_w_in, v_conv_w, v_sinks, v_g_attn, v_g_conv, v_w_out, v_ln1_g, v_ln1_b, v_w_gate, v_w_up, v_w_down, v_ln2_g, v_ln2_b):
    s = x.shape[1]
    d = x.shape[2]

    chip_vec = _chip_id(lax.axis_index("x"), lax.axis_index("y")).astype(jnp.int32).reshape(1)
    wnames = ["w_in", "w_out", "w_gu", "w_down"]
    cw_full = _allgather_conv_w(conv_w)
    buf_in = _cast_weight(w_in, chip_vec, cw_full, "cast_w_in")
    flight_in, token_in = _gather_start([buf_in], cw_full, "gather_start_w_in")
    buf_gu = _cast_weight(w_gate, chip_vec, token_in, "cast_w_gate", 0, 2)
    buf_gu = _cast_weight(w_up, chip_vec, buf_gu, "cast_w_up", 1, 2)
    bufs = [_cast_weight(w_out, chip_vec, token_in, "cast_w_out"), buf_gu,
            _cast_weight(w_down, chip_vec, token_in, "cast_w_down")]
    flights_rest, token = _gather_start(bufs, token_in, "gather_start_rest")
    flights = flight_in + flights_rest

    def gathered(i, after):
        send_sems, recv_sems, buf = flights[i]
        buf = _gather_wait(send_sems, recv_sems, buf, after, "gather_wait_" + wnames[i])
        return _sibling_fill(buf, "sibling_fill_" + wnames[i])

    g_ac = jnp.concatenate([g_attn, g_conv], axis=1)

    cos_t, sin_t = _rope_tables(positions.reshape(s, 1) + token[0:1, 0:1].astype(jnp.int32))
    w_in_g = gathered(0, cos_t)
    proj = _in_proj(x, w_in_g)
    send_sems, recv_sems, buf_out = flights[1]
    buf_out = _gather_wait(send_sems, recv_sems, buf_out, proj, "gather_wait_w_out")
    fill_out = _flight_start("fill_start_w_out", [buf_out], _fill_plan(1), 3, chip_vec)
    attn = _attention_fwd(_after(proj, fill_out[2][0]), cos_t, sin_t, sinks)
    mixed, ac, rstd_ac = _conv_norm(proj, attn, cw_full, g_ac)
    (w_out_g,) = _flight_wait("fill_wait_w_out", fill_out, _fill_plan(1), mixed)
    w_out_full = w_out_g.reshape(d, d)
    xhat1, h1, rstd1 = _out_proj_ln(mixed, w_out_full, x, ln1_g, ln1_b)
    w_gu_g = gathered(2, h1)
    act, ab = _gate_up(h1, w_gu_g)
    w_down_full = gathered(3, act).reshape(-1, d)
    dpre2, dpre2_16, loss_part, g_ln2_g, g_ln2_b = _down_ln_loss(act, w_down_full, xhat1, ln1_g, ln1_b, ln2_g, ln2_b,
                                                                 loss_target)

    cvec = lax.axis_index("c").astype(jnp.int32).reshape(1)

    def exchange_begin(parts, nme):
        bufs = []
        for part in parts:
            ns, r, cdim = part.shape
            bufs.extend([part, lax.empty((ns, r // 2, cdim), part.dtype)])
        return _flight_start("exchange_start_" + nme, bufs, _exchange_plan(len(parts)), len(parts), cvec)

    def exchange_end(flight, n_parts, after, nme):
        bufs = _flight_wait("exchange_wait_" + nme, flight, _exchange_plan(n_parts), after)
        return [(bufs[2 * w], bufs[2 * w + 1]) for w in range(n_parts)]

    def scatter_begin(part, got, nme):
        return _scatter_start(_add_halves(part, got, cvec, "add_halves_" + nme), "scatter_start_" + nme)

    d_gu = _dact_silu_bwd(dpre2_16, w_down_full, ab)
    p_down = _grad_rows(act, dpre2_16, d_gu, "grad_w_down")
    x_down = exchange_begin([p_down], "w_down")
    (p_gu,) = _grad_cols(h1, [d_gu], x_down[2][0], "grad_w_gate_up")
    ((p_down, got),) = exchange_end(x_down, 1, p_gu, "w_down")
    f_down = scatter_begin(p_down, got, "w_down")
    x_gu = exchange_begin([_after(p_gu, f_down[2])], "w_gu")
    dpre1, g_ln1_g, g_ln1_b = _dh1_ln_bwd(d_gu, w_gu_g, dpre2, xhat1, rstd1, ln1_g, x_gu[2][0])
    ((p_gu, got),) = exchange_end(x_gu, 1, dpre1, "w_gu")
    f_gu = scatter_begin(p_gu, got, "w_gu")
    d_ac, g_g_ac = _dmixed_rms_bwd(_after(dpre1, f_gu[2]), w_out_full, ac, rstd_ac, g_ac)
    p_out = _grad_rows(mixed, dpre1, d_ac, "grad_w_out")
    x_out = exchange_begin([p_out], "w_out")
    dq, dkv_cur, dkv_prev, g_sinks = _attention_bwd(proj, d_ac, cos_t, sin_t, sinks, x_out[2][0])
    ((p_out, got),) = exchange_end(x_out, 1, dq, "w_out")
    f_out = scatter_begin(p_out, got, "w_out")
    d_proj, g_conv_w = _dproj_assemble(proj, _after(d_ac, f_out[2]), dq, dkv_cur, dkv_prev, cos_t, sin_t, cw_full)
    (p_in,) = _grad_cols(x, [d_proj], d_proj, "grad_w_in", a_3d=True)
    x_in = exchange_begin([p_in], "w_in")
    grad_x = _dx(d_proj, w_in_g, dpre1, x_in[2][0])
    red = _allreduce_small(g_ln2_g, g_ln2_b, g_ln1_g, g_ln1_b, g_g_ac, g_conv_w, g_sinks, loss_part, grad_x)
    ((p_in, got),) = exchange_end(x_in, 1, red, "w_in")
    f_in = scatter_begin(p_in, got, "w_in")

    pos_vec = jnp.concatenate([chip_vec, cvec])
    shards = {"w_in": (w_in, m_w_in, v_w_in), "w_out": (w_out, m_w_out, v_w_out), "w_gate": (w_gate, m_w_gate, v_w_gate),
              "w_up": (w_up, m_w_up, v_w_up), "w_down": (w_down, m_w_down, v_w_down)}
    early = [("w_down", ["w_down"]), ("w_gu", ["w_gate", "w_up"]), ("w_out", ["w_out"])]
    after = f_in[2]
    completing = {}
    for (nme, _), f in zip(early, [f_down, f_gu, f_out]):
        sums, land = _scatter_wait(*f, after, "scatter_wait_" + nme)
        completing[nme] = _flight_start("complete_start_" + nme, [sums, land], _complete_plan(1), 4, cvec)
        after = completing[nme][2][1]
    big = {}
    for nme, members in early:
        sums, land = _flight_wait("complete_wait_" + nme, completing[nme], _complete_plan(1), after)
        for col_block, member in enumerate(members):
            big[member] = _adamw_shard(*shards[member], land, sums, pos_vec, "adamw_" + member, col_block)
            after = big[member][0]
    sums, land = _scatter_wait(*f_in, after, "scatter_wait_w_in")
    (land,) = _complete_chip_sums([sums], [land])
    big["w_in"] = _adamw_shard(*shards["w_in"], land, sums, pos_vec, "adamw_w_in")
    small = _adamw_small(red, {
        "sinks": (sinks, m_sinks, v_sinks), "g_attn": (g_attn, m_g_attn, v_g_attn),
        "g_conv": (g_conv, m_g_conv, v_g_conv), "ln1_g": (ln1_g, m_ln1_g, v_ln1_g),
        "ln1_b": (ln1_b, m_ln1_b, v_ln1_b), "ln2_g": (ln2_g, m_ln2_g, v_ln2_g),
        "ln2_b": (ln2_b, m_ln2_b, v_ln2_b), "conv_w": (conv_w, m_conv_w, v_conv_w)})
    res = {**big, **small}
    order = ["w_in", "conv_w", "sinks", "g_attn", "g_conv", "w_out", "ln1_g", "ln1_b", "w_gate", "w_up", "w_down",
             "ln2_g", "ln2_b"]
    loss = red[6, d // 2 + 128]
    return (loss, grad_x, *[res[n][0] for n in order], *[res[n][1] for n in order],
            *[res[n][2] for n in order], *[res[n][3] for n in order])
```

```python
import functools

import numpy as np
import jax
import jax.numpy as jnp
from jax import lax
from jax.experimental import pallas as pl
from jax.experimental.pallas import tpu as pltpu

F32 = jnp.float32
BF16 = jnp.bfloat16
MESH = pl.DeviceIdType.MESH

HEAD_DIM = 64
N_KV_HEADS = 4
GROUP = 4
WINDOW = 128
ROT_DIM = 16
ROPE_THETA = 500000.0
ATTN_SCALE = HEAD_DIM ** -0.5
ALPHA = 2.0 ** 0.25
LN_EPS = 1e-5
RMS_EPS = 1e-6
ADAM_LR = 0.001
ADAM_B1 = 0.9
ADAM_B2 = 0.999
ADAM_EPS = 1e-08
ADAM_WD = 0.01
ADAM_STEP = 10
N_CHIPS = 4
NEG_BIG = -1e30

V7X_VMEM_BYTES = 64 * 1024 * 1024
VMEM_LIMIT = V7X_VMEM_BYTES - 6 * 1024 * 1024

TM = 512
TK_TOK = 1024
TB_CONV = 256
TR_ELT = 256
ROW_CHUNK = 128
HALO_ROWS = 16


def _params(sem):
    return pltpu.CompilerParams(dimension_semantics=sem, vmem_limit_bytes=VMEM_LIMIT)


def _row_tile(rows, target):
    best = None
    for t in range(16, min(rows, target) + 1, 16):
        if rows % t == 0:
            best = t
    assert best is not None, (rows, target)
    return best


def _dot(a, b):
    return jnp.dot(a, b, preferred_element_type=F32)


def _dot_nt(a, b):
    return lax.dot_general(a, b, (((1,), (1,)), ((), ())), preferred_element_type=F32)


def _dot_tn(a, b):
    return lax.dot_general(a, b, (((0,), (0,)), ((), ())), preferred_element_type=F32)


def _mesh_pos():
    x, y, c = lax.axis_index("x"), lax.axis_index("y"), lax.axis_index("c")
    chips = [(1 - x, y), (x, 1 - y), (1 - x, 1 - y)]
    return x, y, c, chips


def _chip_id(px, py):
    return 2 * px + py


def _rope(t, cos, sgn_sin, sign):
    w = t.shape[1]
    lane = lax.broadcasted_iota(jnp.int32, t.shape, 1) & (HEAD_DIM - 1)
    partner = jnp.where(lane < ROT_DIM // 2, pltpu.roll(t, w - ROT_DIM // 2, 1), pltpu.roll(t, ROT_DIM // 2, 1))
    return t * cos + sign * (partner * sgn_sin)


def _tile_lanes(t, n):
    return jnp.concatenate([t] * n, axis=1)


def _sigmoid(g):
    return 1.0 / (1.0 + jnp.exp(-g))


def _for_row_chunks(n_rows, fn):
    def step(r, carry):
        fn(pl.ds(pl.multiple_of(r * ROW_CHUNK, ROW_CHUNK), ROW_CHUNK))
        return carry

    lax.fori_loop(0, n_rows // ROW_CHUNK, step, 0)


def _accumulate(acc, make_val, k, nk):
    if nk == 1:
        acc[...] = make_val()
        return

    @pl.when(k == 0)
    def _():
        acc[...] = jnp.zeros_like(acc)

    acc[...] += make_val()


def _ln_fwd(pre):
    mu = jnp.mean(pre, axis=-1, keepdims=True)
    cen = pre - mu
    var = jnp.mean(cen * cen, axis=-1, keepdims=True)
    rstd = lax.rsqrt(var + LN_EPS)
    return cen * rstd, rstd


def _ln_bwd(dy, xhat, rstd, g):
    dxhat = dy * g
    m1 = jnp.mean(dxhat, axis=-1, keepdims=True)
    m2 = jnp.mean(dxhat * xhat, axis=-1, keepdims=True)
    return rstd * (dxhat - m1 - xhat * m2)


def _cast_weight(w, chip_vec, after, name, col_block=0, n_col_blocks=1):
    _, r, c = w.shape
    tr = _row_tile(r, TR_ELT)

    def body(chip_ref, w_ref, after_ref, o_ref):
        o_ref[...] = w_ref[...].astype(BF16)

    grid_spec = pltpu.PrefetchScalarGridSpec(
        num_scalar_prefetch=1, grid=(r // tr,),
        in_specs=[pl.BlockSpec((None, tr, c), lambda i, chip_ref: (0, i, 0)), _ANY],
        out_specs=pl.BlockSpec((None, tr, c), lambda i, chip_ref: (chip_ref[0], i, col_block)))
    return pl.pallas_call(
        body, name=name, grid_spec=grid_spec,
        out_shape=jax.ShapeDtypeStruct((N_CHIPS, r, n_col_blocks * c), BF16),
        input_output_aliases={2: 0} if col_block else {},
        compiler_params=_params(("parallel",)),
    )(chip_vec, w, after)


_HBM = pl.BlockSpec(memory_space=pltpu.HBM)
_VMEM = pl.BlockSpec(memory_space=pltpu.VMEM)


_SEM = pl.BlockSpec(memory_space=pltpu.SEMAPHORE)
_ANY = pl.BlockSpec(memory_space=pl.ANY)
_EFFECT = pltpu.SideEffectType.DATAFLOW_SIDE_EFFECTING


def _chip_copy(buf, k, chip_of_src, half_rows, send_sems, recv_sems, to):
    part = buf.at[chip_of_src, half_rows]
    return pltpu.make_async_remote_copy(
        src_ref=part, dst_ref=part, send_sem=send_sems.at[k], recv_sem=recv_sems.at[k], device_id=to, device_id_type=MESH)


def _half_rows(buf, which):
    hr = buf.shape[1] // 2
    return pl.ds(which * hr, hr)


def _after(value, dep):
    return lax.optimization_barrier((value, dep))[0]


def _flight_start(name, bufs, plan, n_sems, after):
    n = len(bufs)

    def body(*refs):
        sends, _ = plan(refs[:n], refs[n + 1], refs[n + 2])
        for cp in sends:
            cp.start()

    outs = pl.pallas_call(
        body, name=name,
        in_specs=[_HBM] * n + [_ANY], out_specs=[_SEM, _SEM] + [_HBM] * n,
        out_shape=[pltpu.SemaphoreType.DMA((n_sems,))] * 2 + [pltpu.HBM(b.shape, b.dtype) for b in bufs],
        input_output_aliases={i: 2 + i for i in range(n)},
        compiler_params=pltpu.CompilerParams(has_side_effects=_EFFECT),
    )(*[pltpu.with_memory_space_constraint(b, pltpu.HBM) for b in bufs], after)
    return outs[0], outs[1], list(outs[2:])


def _flight_wait(name, flight, plan, after):
    send_sems, recv_sems, bufs = flight
    n = len(bufs)

    def body(*refs):
        sends, recvs = plan(refs[:n], refs[n], refs[n + 1])
        for cp in sends:
            cp.wait_send()
        for cp in recvs:
            cp.wait_recv()

    outs = pl.pallas_call(
        body, name=name,
        in_specs=[_HBM] * n + [_SEM, _SEM, _ANY], out_specs=[_HBM] * n,
        out_shape=[pltpu.HBM(b.shape, b.dtype) for b in bufs],
        input_output_aliases={i: i for i in range(n)},
        compiler_params=pltpu.CompilerParams(has_side_effects=_EFFECT),
    )(*bufs, send_sems, recv_sems, after)
    return list(outs)


def _fill_plan(n_bufs):
    def plan(refs, send_sems, recv_sems):
        x, y, c, chips = _mesh_pos()
        sibling = (x, y, 1 - c)
        sends, recvs = [], []
        for w in range(n_bufs):
            for k, chip in enumerate(chips):
                slot = _chip_id(*chip)
                sends.append(_chip_copy(refs[w], 3 * w + k, slot, _half_rows(refs[w], c), send_sems, recv_sems, sibling))
                recvs.append(_chip_copy(refs[w], 3 * w + k, slot, _half_rows(refs[w], 1 - c), send_sems, recv_sems,
                                        sibling))
        return sends, recvs
    return plan


def _conv_w_plan():
    def plan(refs, send_sems, recv_sems):
        x, y, c, chips = _mesh_pos()
        me = _chip_id(x, y)
        (buf,) = refs
        sends, recvs = [], []
        for k, chip in enumerate(chips):
            for slot, into in ((me, sends), (_chip_id(*chip), recvs)):
                into.append(pltpu.make_async_remote_copy(
                    src_ref=buf.at[slot], dst_ref=buf.at[slot], send_sem=send_sems.at[k], recv_sem=recv_sems.at[k],
                    device_id=(*chip, c), device_id_type=MESH))
        return sends, recvs
    return plan


def _exchange_plan(n_parts):
    def plan(refs, send_sems, recv_sems):
        x, y, c, _ = _mesh_pos()
        copies = []
        for w in range(n_parts):
            part, got = refs[2 * w], refs[2 * w + 1]
            hr = got.shape[1]
            copies.append(pltpu.make_async_remote_copy(
                src_ref=part.at[:, pl.ds((1 - c) * hr, hr)], dst_ref=got, send_sem=send_sems.at[w],
                recv_sem=recv_sems.at[w], device_id=(x, y, 1 - c), device_id_type=MESH))
        return copies, copies
    return plan


def _gather_start(bufs, after, name):
    n = len(bufs)

    def body(*refs):
        ins = refs[:n]
        sends, recvs = refs[n + 1:2 * n + 1], refs[2 * n + 1:3 * n + 1]
        token = refs[4 * n + 1]
        x, y, c, chips = _mesh_pos()
        me = _chip_id(x, y)
        for w in range(n):
            for k, chip in enumerate(chips):
                _chip_copy(ins[w], k, me, _half_rows(ins[w], c), sends[w], recvs[w], (*chip, c)).start()
        token[...] = jnp.zeros_like(token)

    outs = pl.pallas_call(
        body, name=name,
        in_specs=[_HBM] * n + [_ANY],
        out_specs=[_SEM] * (2 * n) + [_HBM] * n + [_VMEM],
        out_shape=[pltpu.SemaphoreType.DMA((3,))] * (2 * n) + [pltpu.HBM(b.shape, b.dtype) for b in bufs]
        + [jax.ShapeDtypeStruct((8, 128), F32)],
        input_output_aliases={w: 2 * n + w for w in range(n)},
        compiler_params=pltpu.CompilerParams(has_side_effects=_EFFECT),
    )(*[pltpu.with_memory_space_constraint(b, pltpu.HBM) for b in bufs], after)
    return [(outs[w], outs[n + w], outs[2 * n + w]) for w in range(n)], outs[3 * n]


def _gather_wait(send_sems, recv_sems, buf, after, name):
    def body(buf_ref, send_ref, recv_ref, after_ref, out_ref):
        x, y, c, chips = _mesh_pos()
        me = _chip_id(x, y)
        for k, chip in enumerate(chips):
            _chip_copy(buf_ref, k, me, _half_rows(buf_ref, c), send_ref, recv_ref, (*chip, c)).wait_send()
        for k, chip in enumerate(chips):
            _chip_copy(buf_ref, k, _chip_id(*chip), _half_rows(buf_ref, c), send_ref, recv_ref, (*chip, c)).wait_recv()

    return pl.pallas_call(
        body, name=name,
        in_specs=[_HBM, _SEM, _SEM, _ANY], out_specs=_HBM,
        out_shape=pltpu.HBM(buf.shape, buf.dtype),
        input_output_aliases={0: 0},
        compiler_params=pltpu.CompilerParams(has_side_effects=_EFFECT),
    )(buf, send_sems, recv_sems, after)


def _sibling_fill(buf, name, own_too=False):
    n_copies = 4 if own_too else 3

    def body(buf_ref, out_ref, send_sems, recv_sems):
        x, y, c, chips = _mesh_pos()
        sibling = (x, y, 1 - c)
        slots = [_chip_id(*chip) for chip in chips] + ([_chip_id(x, y)] if own_too else [])
        copies = []
        for k, slot in enumerate(slots):
            cp = _chip_copy(out_ref, k, slot, _half_rows(out_ref, c), send_sems, recv_sems, sibling)
            cp.start()
            copies.append(cp)
        for k, slot in enumerate(slots):
            _chip_copy(out_ref, k, slot, _half_rows(out_ref, 1 - c), send_sems, recv_sems, sibling).wait_recv()
        for cp in copies:
            cp.wait_send()

    return pl.pallas_call(
        body, name=name,
        in_specs=[_HBM], out_specs=_HBM,
        out_shape=jax.ShapeDtypeStruct(buf.shape, buf.dtype),
        input_output_aliases={0: 0},
        scratch_shapes=[pltpu.SemaphoreType.DMA((n_copies,)), pltpu.SemaphoreType.DMA((n_copies,))],
    )(buf)


def _allgather_conv_w(cw):
    _, kw, cs = cw.shape

    def body(cw_ref, out_ref, send_sems, recv_sems):
        x, y, c, chips = _mesh_pos()
        me = _chip_id(x, y)
        out_ref[pl.ds(me, 1)] = cw_ref[...]
        copies = []
        for k, chip in enumerate(chips):
            cp = pltpu.make_async_remote_copy(
                src_ref=cw_ref.at[0], dst_ref=out_ref.at[me], send_sem=send_sems.at[k], recv_sem=recv_sems.at[k],
                device_id=(*chip, c), device_id_type=MESH)
            cp.start()
            copies.append(cp)
        for k, chip in enumerate(chips):
            pltpu.make_async_remote_copy(
                src_ref=cw_ref.at[0], dst_ref=out_ref.at[_chip_id(*chip)], send_sem=send_sems.at[k],
                recv_sem=recv_sems.at[k], device_id=(*chip, c), device_id_type=MESH).wait_recv()
        for cp in copies:
            cp.wait_send()

    return pl.pallas_call(
        body, name="allgather_conv_w",
        in_specs=[_VMEM], out_specs=_VMEM,
        out_shape=jax.ShapeDtypeStruct((N_CHIPS, kw, cs), F32),
        scratch_shapes=[pltpu.SemaphoreType.DMA((3,)), pltpu.SemaphoreType.DMA((3,))],
    )(cw)


def _exchange_halves(parts, after, name):
    n = len(parts)
    shapes = [p.shape for p in parts]

    def body(*refs):
        ins, outs = refs[:n], refs[n + 1:2 * n + 1]
        send_sems, recv_sems = refs[2 * n + 1:]
        x, y, c, _ = _mesh_pos()
        copies = []
        for w in range(n):
            hr = shapes[w][1] // 2
            cp = pltpu.make_async_remote_copy(
                src_ref=ins[w].at[:, pl.ds((1 - c) * hr, hr)], dst_ref=outs[w],
                send_sem=send_sems.at[w], recv_sem=recv_sems.at[w],
                device_id=(x, y, 1 - c), device_id_type=MESH)
            cp.start()
            copies.append(cp)
        for cp in copies:
            cp.wait()

    return pl.pallas_call(
        body, name=name,
        in_specs=[_HBM] * n + [_ANY], out_specs=[_HBM] * n,
        out_shape=[jax.ShapeDtypeStruct((s[0], s[1] // 2, s[2]), BF16) for s in shapes],
        scratch_shapes=[pltpu.SemaphoreType.DMA((n,)), pltpu.SemaphoreType.DMA((n,))],
    )(*parts, after)


def _add_halves(part, got, cvec, name):
    ns, r, cdim = part.shape
    hr = r // 2
    tr = _row_tile(hr, TR_ELT)
    nblk = hr // tr

    def body(c_ref, a_ref, b_ref, o_ref):
        o_ref[...] = (a_ref[...].astype(F32) + b_ref[...].astype(F32)).astype(BF16)

    grid_spec = pltpu.PrefetchScalarGridSpec(
        num_scalar_prefetch=1, grid=(ns, nblk),
        in_specs=[pl.BlockSpec((None, tr, cdim), lambda s, i, c_ref: (s, c_ref[0] * nblk + i, 0)),
                  pl.BlockSpec((None, tr, cdim), lambda s, i, c_ref: (s, i, 0))],
        out_specs=pl.BlockSpec((None, tr, cdim), lambda s, i, c_ref: (s, i, 0)))
    return pl.pallas_call(
        body, name=name, grid_spec=grid_spec,
        out_shape=jax.ShapeDtypeStruct((ns, hr, cdim), BF16),
        compiler_params=_params(("parallel", "parallel")),
    )(cvec, part, got)


def _scatter_copy(sums_ref, land_ref, k, src_slot, dst_slot, c, send_sems, recv_sems, to):
    return pltpu.make_async_remote_copy(
        src_ref=sums_ref.at[src_slot], dst_ref=land_ref.at[dst_slot, _half_rows(land_ref, c)],
        send_sem=send_sems.at[k], recv_sem=recv_sems.at[k], device_id=to, device_id_type=MESH)


def _scatter_start(sums, name):
    ns, hr, cdim = sums.shape
    land = lax.empty((ns, 2 * hr, cdim), sums.dtype)

    def body(sums_ref, land_ref, send_sems, recv_sems, sums_thru, land_thru):
        x, y, c, chips = _mesh_pos()
        me = _chip_id(x, y)
        for k, chip in enumerate(chips):
            _scatter_copy(sums_ref, land_ref, k, _chip_id(*chip), me, c, send_sems, recv_sems, (*chip, c)).start()

    return pl.pallas_call(
        body, name=name,
        in_specs=[_HBM, _HBM], out_specs=[_SEM, _SEM, _HBM, _HBM],
        out_shape=[pltpu.SemaphoreType.DMA((3,)), pltpu.SemaphoreType.DMA((3,)),
                   pltpu.HBM(sums.shape, sums.dtype), pltpu.HBM(land.shape, land.dtype)],
        input_output_aliases={0: 2, 1: 3},
        compiler_params=pltpu.CompilerParams(has_side_effects=_EFFECT),
    )(pltpu.with_memory_space_constraint(sums, pltpu.HBM), pltpu.with_memory_space_constraint(land, pltpu.HBM))


def _scatter_wait(send_sems, recv_sems, sums, land, after, name):
    def body(sums_ref, land_ref, send_ref, recv_ref, after_ref, sums_out, land_out):
        x, y, c, chips = _mesh_pos()
        me = _chip_id(x, y)
        for k, chip in enumerate(chips):
            _scatter_copy(sums_ref, land_ref, k, _chip_id(*chip), me, c, send_ref, recv_ref, (*chip, c)).wait_send()
        for k, chip in enumerate(chips):
            _scatter_copy(sums_ref, land_ref, k, me, _chip_id(*chip), c, send_ref, recv_ref, (*chip, c)).wait_recv()

    return pl.pallas_call(
        body, name=name,
        in_specs=[_HBM, _HBM, _SEM, _SEM, _ANY], out_specs=[_HBM, _HBM],
        out_shape=[pltpu.HBM(sums.shape, sums.dtype), pltpu.HBM(land.shape, land.dtype)],
        input_output_aliases={0: 0, 1: 1},
        compiler_params=pltpu.CompilerParams(has_side_effects=_EFFECT),
    )(sums, land, send_sems, recv_sems, after)


def _complete_plan(n_weights):
    def plan(refs, send_sems, recv_sems):
        x, y, c, chips = _mesh_pos()
        me = _chip_id(x, y)
        sibling = (x, y, 1 - c)
        sends, recvs = [], []
        for w in range(n_weights):
            sums, land = refs[2 * w], refs[2 * w + 1]
            sends.append(_scatter_copy(sums, land, 4 * w + 3, me, me, c, send_sems, recv_sems, sibling))
            recvs.append(_scatter_copy(sums, land, 4 * w + 3, me, me, 1 - c, send_sems, recv_sems, sibling))
            for k, chip in enumerate(chips):
                slot = _chip_id(*chip)
                sends.append(_chip_copy(land, 4 * w + k, slot, _half_rows(land, c), send_sems, recv_sems, sibling))
                recvs.append(_chip_copy(land, 4 * w + k, slot, _half_rows(land, 1 - c), send_sems, recv_sems, sibling))
        return sends, recvs
    return plan


def _complete_chip_sums(sums, lands):
    n = len(sums)

    def body(*refs):
        sums_refs, outs = refs[:n], refs[2 * n:3 * n]
        send_sems, recv_sems = refs[3 * n:]
        x, y, c, chips = _mesh_pos()
        me = _chip_id(x, y)
        sibling = (x, y, 1 - c)
        slots = [_chip_id(*chip) for chip in chips]
        sent = []
        for w in range(n):
            out = outs[w]
            cp = _scatter_copy(sums_refs[w], out, 3, me, me, c, send_sems.at[w], recv_sems.at[w], sibling)
            cp.start()
            sent.append(cp)
            for k, slot in enumerate(slots):
                cp = _chip_copy(out, k, slot, _half_rows(out, c), send_sems.at[w], recv_sems.at[w], sibling)
                cp.start()
                sent.append(cp)
        for w in range(n):
            out = outs[w]
            _scatter_copy(sums_refs[w], out, 3, me, me, 1 - c, send_sems.at[w], recv_sems.at[w], sibling).wait_recv()
            for k, slot in enumerate(slots):
                _chip_copy(out, k, slot, _half_rows(out, 1 - c), send_sems.at[w], recv_sems.at[w], sibling).wait_recv()
        for cp in sent:
            cp.wait_send()

    return pl.pallas_call(
        body, name="complete_chip_sums",
        in_specs=[_HBM] * (2 * n), out_specs=[_HBM] * n,
        out_shape=[jax.ShapeDtypeStruct(b.shape, b.dtype) for b in lands],
        input_output_aliases={n + w: w for w in range(n)},
        scratch_shapes=[pltpu.SemaphoreType.DMA((n, 4)), pltpu.SemaphoreType.DMA((n, 4))],
    )(*sums, *lands)


SMALL_ROWS = 8


def _allreduce_small(gl2g, gl2b, gl1g, gl1b, g_ac, gcw, gsink, loss, after):
    d = gl2g.shape[1]
    hd = d // 2
    nq = gsink.shape[1]

    def body(a_ref, b_ref, c_ref, d_ref, e_ref, cw_ref, sk_ref, ls_ref, after_ref, out_ref, mine, gath, send_sems,
             recv_sems):
        x, y, c, _ = _mesh_pos()
        me = 4 * x + 2 * y + c
        mine[...] = jnp.zeros_like(mine)
        mine[0:1, :] = a_ref[...]
        mine[1:2, :] = b_ref[...]
        mine[2:3, :] = c_ref[...]
        mine[3:4, :] = d_ref[...]
        mine[4:5, :] = e_ref[...]
        mine[5:6, 0:hd] = cw_ref[0:1, :]
        mine[5:6, hd:d] = cw_ref[1:2, :]
        mine[6:7, 0:hd] = cw_ref[2:3, :]
        mine[6:7, hd:hd + nq] = sk_ref[...]
        mine[6:7, hd + 128:hd + 256] = ls_ref[...]
        gath[pl.ds(me, 1)] = mine[...][None]
        copies = []
        for r in range(1, 8):
            peer = ((1 - x) if r & 4 else x, (1 - y) if r & 2 else y, (1 - c) if r & 1 else c)
            cp = pltpu.make_async_remote_copy(
                src_ref=mine, dst_ref=gath.at[me], send_sem=send_sems.at[r - 1], recv_sem=recv_sems.at[r - 1],
                device_id=peer, device_id_type=MESH)
            cp.start()
            copies.append(cp)
        for r in range(1, 8):
            peer = ((1 - x) if r & 4 else x, (1 - y) if r & 2 else y, (1 - c) if r & 1 else c)
            peer_id = 4 * peer[0] + 2 * peer[1] + peer[2]
            pltpu.make_async_remote_copy(
                src_ref=mine, dst_ref=gath.at[peer_id], send_sem=send_sems.at[r - 1], recv_sem=recv_sems.at[r - 1],
                device_id=peer, device_id_type=MESH).wait_recv()
        for cp in copies:
            cp.wait_send()
        total = gath[0]
        for dev in range(1, 8):
            total = total + gath[dev]
        out_ref[...] = total

    return pl.pallas_call(
        body, name="allreduce_small",
        in_specs=[_VMEM] * 8 + [_ANY], out_specs=_VMEM,
        out_shape=jax.ShapeDtypeStruct((SMALL_ROWS, d), F32),
        scratch_shapes=[pltpu.VMEM((SMALL_ROWS, d), F32), pltpu.VMEM((8, SMALL_ROWS, d), F32),
                        pltpu.SemaphoreType.DMA((7,)), pltpu.SemaphoreType.DMA((7,))],
    )(gl2g, gl2b, gl1g, gl1b, g_ac, gcw, gsink, loss, after)


def _adamw(w, g, m, v):
    m = ADAM_B1 * m + (1.0 - ADAM_B1) * g
    v = ADAM_B2 * v + (1.0 - ADAM_B2) * (g * g)
    m_hat = m / (1.0 - ADAM_B1 ** ADAM_STEP)
    v_hat = v / (1.0 - ADAM_B2 ** ADAM_STEP)
    delta = -ADAM_LR * (m_hat / (jnp.sqrt(v_hat) + ADAM_EPS) + ADAM_WD * w)
    return delta, m, v


def _adamw_shard(w, m, v, land, own, pos_vec, name, col_block=0):
    _, r, c = w.shape
    hr = r // 2
    tr = _row_tile(hr, TR_ELT)
    nh = hr // tr

    def body(pos_ref, w_ref, m_ref, v_ref, l0, l1, l2, l3, own_ref, g_out, d_out, m_out, v_out):
        i = pl.program_id(0)
        mine = (i // nh) == pos_ref[1]
        own_blk = own_ref[...].astype(F32)
        g = None
        for s, l_ref in enumerate([l0, l1, l2, l3]):
            term = jnp.where(mine & (pos_ref[0] == s), own_blk, l_ref[...].astype(F32))
            g = term if g is None else g + term
        delta, nm, nv = _adamw(w_ref[...], g, m_ref[...], v_ref[...])
        g_out[...] = g
        d_out[...] = delta
        m_out[...] = nm
        v_out[...] = nv

    def land_spec(s):
        def index(i, pos_ref):
            skip = (pos_ref[0] == s) & ((i // nh) == pos_ref[1])
            return (s, jnp.where(skip, (i + nh) % (2 * nh), i), col_block)
        return pl.BlockSpec((None, tr, c), index)

    blk = pl.BlockSpec((None, tr, c), lambda i, pos_ref: (0, i, 0))
    grid_spec = pltpu.PrefetchScalarGridSpec(
        num_scalar_prefetch=1, grid=(2 * nh,),
        in_specs=[blk, blk, blk] + [land_spec(s) for s in range(N_CHIPS)]
        + [pl.BlockSpec((None, tr, c), lambda i, pos_ref: (pos_ref[0], i % nh, col_block))],
        out_specs=[blk] * 4)
    return pl.pallas_call(
        body, name=name, grid_spec=grid_spec,
        out_shape=[jax.ShapeDtypeStruct((1, r, c), F32)] * 4,
        compiler_params=_params(("parallel",)),
    )(pos_vec, w, m, v, land, land, land, land, own)


def _adamw_small(red, params):
    names = ["sinks", "g_attn", "g_conv", "ln1_g", "ln1_b", "ln2_g", "ln2_b", "conv_w"]
    d = red.shape[1]
    hd = d // 2
    flat = []
    for nme in names:
        flat.extend(params[nme])
    nq = params["sinks"][0].shape[1]
    cs = params["conv_w"][0].shape[2]

    def body(*refs):
        red_ref = refs[0]
        ins = refs[1:1 + 3 * len(names)]
        outs = refs[1 + 3 * len(names):]
        x, y, _, _ = _mesh_pos()
        me = _chip_id(x, y)

        def conv_tap(row, base):
            picked = red_ref[row:row + 1, base:base + cs]
            for s in range(1, N_CHIPS):
                picked = jnp.where(me == s, red_ref[row:row + 1, base + s * cs:base + (s + 1) * cs], picked)
            return picked

        grads = {
            "sinks": red_ref[6:7, hd:hd + nq],
            "g_attn": red_ref[4:5, 0:hd],
            "g_conv": red_ref[4:5, hd:d],
            "ln1_g": red_ref[2:3, :],
            "ln1_b": red_ref[3:4, :],
            "ln2_g": red_ref[0:1, :],
            "ln2_b": red_ref[1:2, :],
        }
        for i, nme in enumerate(names):
            w_ref, m_ref, v_ref = ins[3 * i:3 * i + 3]
            g_out, d_out, m_out, v_out = outs[4 * i:4 * i + 4]
            if nme == "conv_w":
                for tap, (row, base) in enumerate([(5, 0), (5, hd), (6, 0)]):
                    g = conv_tap(row, base)
                    delta, nm, nv = _adamw(w_ref[0, tap:tap + 1, :], g, m_ref[0, tap:tap + 1, :], v_ref[0, tap:tap + 1, :])
                    g_out[0, tap:tap + 1, :] = g
                    d_out[0, tap:tap + 1, :] = delta
                    m_out[0, tap:tap + 1, :] = nm
                    v_out[0, tap:tap + 1, :] = nv
            else:
                g = grads[nme]
                delta, nm, nv = _adamw(w_ref[...], g, m_ref[...], v_ref[...])
                g_out[...] = g
                d_out[...] = delta
                m_out[...] = nm
                v_out[...] = nv

    out_shape = []
    for nme in names:
        out_shape.extend([jax.ShapeDtypeStruct(params[nme][0].shape, F32)] * 4)
    outs = pl.pallas_call(
        body, name="adamw_small",
        in_specs=[_VMEM] * (1 + len(flat)), out_specs=[_VMEM] * len(out_shape),
        out_shape=out_shape,
    )(red, *flat)
    return {nme: tuple(outs[4 * i:4 * i + 4]) for i, nme in enumerate(names)}


def _rope_tables(pos_col):
    s = pos_col.shape[0]
    w = N_KV_HEADS * HEAD_DIM
    tb = min(512, s)
    inv_freq = (ROPE_THETA ** (-np.arange(0, ROT_DIM, 2, dtype=np.float32) / ROT_DIM)).astype(np.float32)

    def body(pos_ref, cos_ref, sin_ref):
        pos = pos_ref[...].astype(F32)
        lane = lax.broadcasted_iota(jnp.int32, (tb, PAIR), 1) & (HEAD_DIM - 1)
        fidx = lane & (ROT_DIM // 2 - 1)
        inv = jnp.zeros((tb, PAIR), F32)
        for k in range(ROT_DIM // 2):
            inv = jnp.where(fidx == k, float(inv_freq[k]), inv)
        ang = pos * inv
        rot = lane < ROT_DIM
        sin_v = jnp.sin(ang)
        cos_ref[...] = _tile_lanes(jnp.where(rot, jnp.cos(ang), 1.0), w // PAIR)
        sin_ref[...] = _tile_lanes(jnp.where(lane < ROT_DIM // 2, -sin_v, jnp.where(rot, sin_v, 0.0)), w // PAIR)

    return pl.pallas_call(
        body, name="rope_tables", grid=(s // tb,),
        in_specs=[pl.BlockSpec((tb, 1), lambda i: (i, 0))],
        out_specs=[pl.BlockSpec((tb, w), lambda i: (i, 0))] * 2,
        out_shape=[jax.ShapeDtypeStruct((s, w), F32)] * 2,
        compiler_params=_params(("parallel",)),
    )(pos_col)


def _in_proj(x, w_in_g, first_vec, n_shards, into, name):
    _, s, d = x.shape
    ns, _, ncol = w_in_g.shape
    tm = min(2 * TM, s)

    def body(first_ref, x_ref, w_ref, into_ref, o_ref):
        o_ref[...] = _dot(x_ref[...].astype(BF16), w_ref[...]).astype(BF16)

    shard = lambda j, first_ref: lax.rem(first_ref[0] + j, ns)
    grid_spec = pltpu.PrefetchScalarGridSpec(
        num_scalar_prefetch=1, grid=(s // tm, n_shards),
        in_specs=[pl.BlockSpec((None, tm, d), lambda i, j, first_ref: (0, i, 0)),
                  pl.BlockSpec((None, d, ncol), lambda i, j, first_ref: (shard(j, first_ref), 0, 0)), _ANY],
        out_specs=pl.BlockSpec((tm, ncol), lambda i, j, first_ref: (i, shard(j, first_ref))))
    return pl.pallas_call(
        body, name=name, grid_spec=grid_spec,
        out_shape=jax.ShapeDtypeStruct((s, ns * ncol), BF16),
        input_output_aliases={} if into is None else {3: 0},
        compiler_params=_params(("parallel", "arbitrary")),
    )(first_vec, x, w_in_g, first_vec if into is None else into)


PAIR = 2 * HEAD_DIM
KEYS = 2 * WINDOW


def _pair_operand(t_all, h):
    col = (h // 2) * PAIR
    lane = lax.broadcasted_iota(jnp.int32, (KEYS, PAIR), 1)
    own_low = h % 2 == 0
    mine = jnp.where((lane < HEAD_DIM) if own_low else (lane >= HEAD_DIM), t_all[:, col:col + PAIR], 0.0)
    other = pltpu.roll(mine, HEAD_DIM, 1)
    low, high = (mine, other) if own_low else (other, mine)
    return jnp.concatenate([low, high], axis=0).astype(BF16)


def _pair_grad(acc, h):
    lane = lax.broadcasted_iota(jnp.int32, (KEYS, PAIR), 1)
    low = jnp.where(lane < HEAD_DIM, acc[:KEYS], 0.0)
    high = jnp.where(lane >= HEAD_DIM, acc[KEYS:], 0.0)
    if h % 2 == 0:
        return low + pltpu.roll(high, HEAD_DIM, 1)
    return high + pltpu.roll(low, HEAD_DIM, 1)


N_PAIRS = N_KV_HEADS * GROUP // 2


def _all_probs(q, kk2s, first, sinks_ref):
    assert ATTN_SCALE == 0.125
    q = q * ATTN_SCALE
    qps, scores = [], []
    for pair in range(N_PAIRS):
        qp = q[:, pair * PAIR:(pair + 1) * PAIR].astype(BF16)
        qps.append(qp)
        scores.append(_dot_nt(qp, kk2s[pair // (GROUP // 2)]))
    qi = lax.broadcasted_iota(jnp.int32, (WINDOW, 2 * KEYS), 0)
    kj = lax.broadcasted_iota(jnp.int32, (WINDOW, 2 * KEYS), 1) & (KEYS - 1)
    rel = qi + WINDOW - kj
    valid = (rel >= 0) & (rel < WINDOW) & jnp.logical_not(first & (kj < WINDOW))
    bias = jnp.where(valid, 0.0, NEG_BIG)
    s = (jnp.stack(scores, axis=0) + bias[None]).reshape(N_PAIRS * WINDOW, 2 * KEYS)
    probs, p_sinks = [], []
    for t in range(2):
        st = s[:, t * KEYS:(t + 1) * KEYS]
        sink = jnp.concatenate([jnp.broadcast_to(sinks_ref[0:1, 2 * pair + t:2 * pair + t + 1], (WINDOW, 1))
                                for pair in range(N_PAIRS)], axis=0)
        m = jnp.maximum(jnp.max(st, axis=1, keepdims=True), sink)
        e = jnp.exp(st - m)
        e_sink = jnp.exp(sink - m)
        inv_l = 1.0 / (jnp.sum(e, axis=1, keepdims=True) + e_sink)
        probs.append(e * inv_l)
        p_sinks.append(e_sink * inv_l)
    return qps, jnp.concatenate(probs, axis=1), p_sinks


def _roped_qkv(cur_ref, prev_ref, cos_ref, sin_ref, cosp_ref, sinp_ref, qw, kvw):
    cur = cur_ref[...].astype(F32)
    cos, sin = cos_ref[...], sin_ref[...]
    cos_q, sin_q = _tile_lanes(cos, GROUP), _tile_lanes(sin, GROUP)
    q = _rope(cur[:, :qw], cos_q, sin_q, 1.0)
    prev = prev_ref[...].astype(F32)
    k_all = jnp.concatenate([_rope(prev[:, :kvw], cosp_ref[...], sinp_ref[...], 1.0),
                             _rope(cur[:, qw:qw + kvw], cos, sin, 1.0)], axis=0)
    v_all = jnp.concatenate([prev[:, kvw:], cur[:, qw + kvw:]], axis=0)
    return q, k_all, v_all, cos_q, sin_q


def _attention_fwd(proj, cos_t, sin_t, sinks):
    s = proj.shape[0]
    qw = GROUP * N_KV_HEADS * HEAD_DIM
    kvw = N_KV_HEADS * HEAD_DIM
    nb = s // WINDOW

    def body(cur_ref, prev_ref, cos_ref, sin_ref, cosp_ref, sinp_ref, sinks_ref, o_ref):
        first = pl.program_id(0) == 0
        q, k_all, v_all, _, _ = _roped_qkv(cur_ref, prev_ref, cos_ref, sin_ref, cosp_ref, sinp_ref, qw, kvw)
        kk2s = [_pair_operand(k_all, h) for h in range(N_KV_HEADS)]
        vv2s = [_pair_operand(v_all, h) for h in range(N_KV_HEADS)]
        _, probs, _ = _all_probs(q, kk2s, first, sinks_ref)
        probs = probs.astype(BF16)
        outs = [_dot(probs[pair * WINDOW:(pair + 1) * WINDOW], vv2s[pair // (GROUP // 2)]) for pair in range(N_PAIRS)]
        o_ref[...] = jnp.concatenate(outs, axis=1)

    tbl = pl.BlockSpec((WINDOW, kvw), lambda n: (n, 0))
    tbl_prev = pl.BlockSpec((WINDOW, kvw), lambda n: (jnp.maximum(n - 1, 0), 0))
    return pl.pallas_call(
        body, name="attention_fwd", grid=(nb,),
        in_specs=[pl.BlockSpec((WINDOW, qw + 2 * kvw), lambda n: (n, 0)),
                  pl.BlockSpec((WINDOW, 2 * kvw), lambda n: (jnp.maximum(n - 1, 0), (qw // (2 * kvw)))),
                  tbl, tbl, tbl_prev, tbl_prev, _VMEM],
        out_specs=pl.BlockSpec((WINDOW, qw), lambda n: (n, 0)),
        out_shape=jax.ShapeDtypeStruct((s, qw), F32),
        compiler_params=_params(("parallel",)),
    )(proj, proj, cos_t, sin_t, cos_t, sin_t, sinks)


def _conv_taps(cw_ref):
    return [jnp.concatenate([cw_ref[s, k:k + 1, :] for s in range(N_CHIPS)], axis=1) for k in range(3)]


def _shift_down(z, halo, steps):
    last = halo.shape[0]
    row = lax.broadcasted_iota(jnp.int32, z.shape, 0)
    out = pltpu.roll(z, steps, 0)
    for r in range(steps):
        out = jnp.where(row == r, halo[last - steps + r:last - steps + r + 1, :], out)
    return out


def _shift_up(z, halo, steps):
    rows = z.shape[0]
    row = lax.broadcasted_iota(jnp.int32, z.shape, 0)
    out = pltpu.roll(z, rows - steps, 0)
    for r in range(steps):
        out = jnp.where(row == rows - steps + r, halo[r:r + 1, :], out)
    return out


def _split_cbu(lo, hi, cw):
    lo, hi = lo.astype(F32), hi.astype(F32)
    c_gate = lo[:, :cw]
    b_gate = jnp.concatenate([lo[:, cw:], hi[:, :2 * cw - lo.shape[1]]], axis=1)
    u = hi[:, 2 * cw - lo.shape[1]:]
    return c_gate, b_gate, u


def _conv_norm(proj, attn, cw_full, g_ac):
    s, in_w = proj.shape
    cw = attn.shape[1]
    blk_w = in_w // 3
    tb = min(TB_CONV, s)

    def body(lo_ref, hi_ref, lo_h_ref, hi_h_ref, attn_ref, cw_ref, g_ref, mixed_ref, ac_ref, rstd_ref):
        i = pl.program_id(0)
        c_gate, b_gate, u = _split_cbu(lo_ref[...], hi_ref[...], cw)
        c_h, _, u_h = _split_cbu(lo_h_ref[...], hi_h_ref[...], cw)
        z = c_gate * u
        z_h = jnp.where(i == 0, 0.0, c_h * u_h)
        w0, w1, w2 = _conv_taps(cw_ref)
        y = w0 * _shift_down(z, z_h, 2) + w1 * _shift_down(z, z_h, 1) + w2 * z
        conv = b_gate * y
        a = attn_ref[...]
        r_a = lax.rsqrt(jnp.mean(a * a, axis=-1, keepdims=True) + RMS_EPS)
        r_c = lax.rsqrt(jnp.mean(conv * conv, axis=-1, keepdims=True) + RMS_EPS)
        g = g_ref[...]
        mixed_ref[...] = jnp.concatenate([a * r_a * g[:, :cw], conv * r_c * g[:, cw:]], axis=1).astype(BF16)
        ac_ref[...] = jnp.concatenate([a, conv], axis=1)
        rstd_ref[0] = r_a
        rstd_ref[1] = r_c

    halo_idx = lambda i: jnp.maximum(i * (tb // HALO_ROWS) - 1, 0)
    return pl.pallas_call(
        body, name="conv_norm", grid=(s // tb,),
        in_specs=[pl.BlockSpec((tb, blk_w), lambda i: (i, 1)),
                  pl.BlockSpec((tb, blk_w), lambda i: (i, 2)),
                  pl.BlockSpec((HALO_ROWS, blk_w), lambda i: (halo_idx(i), 1)),
                  pl.BlockSpec((HALO_ROWS, blk_w), lambda i: (halo_idx(i), 2)),
                  pl.BlockSpec((tb, cw), lambda i: (i, 0)),
                  _VMEM, _VMEM],
        out_specs=[pl.BlockSpec((tb, 2 * cw), lambda i: (i, 0)),
                   pl.BlockSpec((tb, 2 * cw), lambda i: (i, 0)),
                   pl.BlockSpec((2, tb, 1), lambda i: (0, i, 0))],
        out_shape=[jax.ShapeDtypeStruct((s, 2 * cw), BF16), jax.ShapeDtypeStruct((s, 2 * cw), F32),
                   jax.ShapeDtypeStruct((2, s, 1), F32)],
        compiler_params=_params(("parallel",)),
    )(proj, proj, proj, proj, attn, cw_full, g_ac)


def _out_proj_ln(mixed, w_out_g, x, ln_g, ln_b):
    s, d = mixed.shape
    tm = min(TM, s)
    tk = d
    nk = d // tk

    def body(a_ref, w_ref, x_ref, g_ref, b_ref, xhat_ref, h_ref, rstd_ref, acc):
        k = pl.program_id(1)
        _accumulate(acc, lambda: _dot(a_ref[...], w_ref[...]), k, nk)

        @pl.when(k == nk - 1)
        def _():
            def rows_fn(rows):
                xhat, rstd = _ln_fwd(ALPHA * x_ref[rows, :] + acc[rows, :])
                xhat_ref[rows, :] = xhat
                h_ref[rows, :] = (xhat * g_ref[...] + b_ref[...]).astype(BF16)
                rstd_ref[rows, :] = rstd

            _for_row_chunks(tm, rows_fn)

    row = pl.BlockSpec((tm, d), lambda i, k: (i, 0))
    return pl.pallas_call(
        body, name="out_proj_ln", grid=(s // tm, nk),
        in_specs=[pl.BlockSpec((tm, tk), lambda i, k: (i, k)),
                  pl.BlockSpec((tk, d), lambda i, k: (k, 0)),
                  pl.BlockSpec((None, tm, d), lambda i, k: (0, i, 0)),
                  _VMEM, _VMEM],
        out_specs=[row, row, pl.BlockSpec((tm, 1), lambda i, k: (i, 0))],
        out_shape=[jax.ShapeDtypeStruct((s, d), F32), jax.ShapeDtypeStruct((s, d), BF16),
                   jax.ShapeDtypeStruct((s, 1), F32)],
        scratch_shapes=[pltpu.VMEM((tm, d), F32)],
        compiler_params=_params(("parallel", "arbitrary")),
    )(mixed, w_out_g, x, ln_g, ln_b)


def _gate_up(h1, w_gu_g):
    s, d = h1.shape
    ns, _, fs2 = w_gu_g.shape
    fs = fs2 // 2
    tm = min(TM, s)

    def body(h_ref, w_ref, act_ref, ab_ref):
        gu = _dot(h_ref[...], w_ref[...])
        g, u = gu[:, :fs], gu[:, fs:]
        sg = _sigmoid(g)
        silu = g * sg
        act_ref[...] = (silu * u).astype(BF16)
        ab_ref[:, :fs] = (u * (sg * (1.0 + g * (1.0 - sg)))).astype(BF16)
        ab_ref[:, fs:] = silu.astype(BF16)

    return pl.pallas_call(
        body, name="gate_up", grid=(s // tm, ns),
        in_specs=[pl.BlockSpec((tm, d), lambda i, j: (i, 0)), pl.BlockSpec((None, d, fs2), lambda i, j: (j, 0, 0))],
        out_specs=[pl.BlockSpec((tm, fs), lambda i, j: (i, j)), pl.BlockSpec((tm, fs2), lambda i, j: (i, j))],
        out_shape=[jax.ShapeDtypeStruct((s, ns * fs), BF16), jax.ShapeDtypeStruct((s, ns * fs2), BF16)],
        compiler_params=_params(("parallel", "arbitrary")),
    )(h1, w_gu_g)


def _down_ln_loss(act, w_down_g, xhat1, ln1_g, ln1_b, ln2_g, ln2_b, target):
    s, f = act.shape
    d = xhat1.shape[1]
    tm = min(TM, s)
    tk = f // N_CHIPS
    nk = f // tk

    def body(a_ref, w_ref, xh_ref, g1_ref, b1_ref, g2_ref, b2_ref, t_ref, dpre_ref, dpre16_ref, loss_ref, gg_ref, gb_ref,
             acc):
        i, k = pl.program_id(0), pl.program_id(1)
        _accumulate(acc, lambda: _dot(a_ref[...], w_ref[...]), k, nk)

        @pl.when(k == nk - 1)
        def _():
            @pl.when(i == 0)
            def _():
                loss_ref[...] = jnp.zeros_like(loss_ref)
                gg_ref[...] = jnp.zeros_like(gg_ref)
                gb_ref[...] = jnp.zeros_like(gb_ref)

            def rows_fn(rows):
                h1 = xh_ref[rows, :] * g1_ref[...] + b1_ref[...]
                xhat, rstd = _ln_fwd(ALPHA * h1 + acc[rows, :])
                g2 = g2_ref[...]
                diff = xhat * g2 + b2_ref[...] - t_ref[rows, :]
                dy = diff * (1.0 / d)
                dpre = _ln_bwd(dy, xhat, rstd, g2)
                dpre_ref[rows, :] = dpre
                dpre16_ref[rows, :] = dpre.astype(BF16)
                sq = jnp.sum(jnp.sum(diff * diff, axis=1, keepdims=True), axis=0, keepdims=True)
                loss_ref[...] += jnp.broadcast_to(sq * (0.5 / d), (1, 128))
                gg_ref[...] += jnp.sum(dy * xhat, axis=0, keepdims=True)
                gb_ref[...] += jnp.sum(dy, axis=0, keepdims=True)

            _for_row_chunks(tm, rows_fn)

    row = pl.BlockSpec((tm, d), lambda i, k: (i, 0))
    vec = pl.BlockSpec((1, d), lambda i, k: (0, 0))
    return pl.pallas_call(
        body, name="down_ln_loss", grid=(s // tm, nk),
        in_specs=[pl.BlockSpec((tm, tk), lambda i, k: (i, k)),
                  pl.BlockSpec((tk, d), lambda i, k: (k, 0)),
                  row, _VMEM, _VMEM, _VMEM, _VMEM,
                  pl.BlockSpec((None, tm, d), lambda i, k: (0, i, 0))],
        out_specs=[row, row, pl.BlockSpec((1, 128), lambda i, k: (0, 0)), vec, vec],
        out_shape=[jax.ShapeDtypeStruct((s, d), F32), jax.ShapeDtypeStruct((s, d), BF16),
                   jax.ShapeDtypeStruct((1, 128), F32), jax.ShapeDtypeStruct((1, d), F32),
                   jax.ShapeDtypeStruct((1, d), F32)],
        scratch_shapes=[pltpu.VMEM((tm, d), F32)],
        compiler_params=_params(("arbitrary", "arbitrary")),
    )(act, w_down_g, xhat1, ln1_g, ln1_b, ln2_g, ln2_b, target)


def _dact_silu_bwd(dpre2, w_down_g, ab):
    s, d = dpre2.shape
    fs2 = ab.shape[1] // N_CHIPS
    fs = fs2 // 2
    tm = min(TM, s)

    def body(dp_ref, w_ref, ab_ref, dgu_ref):
        d_act = _dot_nt(dp_ref[...], w_ref[...])
        dgu_ref[:, :fs] = (d_act * ab_ref[:, :fs].astype(F32)).astype(BF16)
        dgu_ref[:, fs:] = (d_act * ab_ref[:, fs:].astype(F32)).astype(BF16)

    blk = pl.BlockSpec((tm, fs2), lambda j, i: (i, j))
    return pl.pallas_call(
        body, name="dact_silu_bwd", grid=(N_CHIPS, s // tm),
        in_specs=[pl.BlockSpec((tm, d), lambda j, i: (i, 0)),
                  pl.BlockSpec((fs, d), lambda j, i: (j, 0)), blk],
        out_specs=blk,
        out_shape=jax.ShapeDtypeStruct(ab.shape, BF16),
        compiler_params=_params(("parallel", "parallel")),
    )(dpre2, w_down_g, ab)


def _grad_rows(a, b, after, name, row_blocks=1):
    s, m = a.shape
    n = b.shape[1]
    ms = m // N_CHIPS
    tmw = ms // row_blocks
    tk = min(TK_TOK, s)
    nk = s // tk

    def body(a_ref, b_ref, after_ref, o_ref, acc):
        k = pl.program_id(2)
        _accumulate(acc, lambda: _dot_tn(a_ref[...].astype(BF16), b_ref[...].astype(BF16)), k, nk)

        @pl.when(k == nk - 1)
        def _():
            o_ref[...] = acc[...].astype(BF16)

    return pl.pallas_call(
        body, name=name, grid=(N_CHIPS, row_blocks, nk),
        in_specs=[pl.BlockSpec((tk, tmw), lambda j, r, k: (k, j * row_blocks + r)),
                  pl.BlockSpec((tk, n), lambda j, r, k: (k, 0)), _ANY],
        out_specs=pl.BlockSpec((None, tmw, n), lambda j, r, k: (j, r, 0)),
        out_shape=jax.ShapeDtypeStruct((N_CHIPS, ms, n), BF16),
        scratch_shapes=[pltpu.VMEM((tmw, n), F32)],
        compiler_params=_params(("parallel", "parallel", "arbitrary")),
    )(a, b, after)


def _grad_cols(a, bs, after, name, a_3d=False, row_blocks=2):
    s, m = a.shape[-2:]
    n = bs[0].shape[1]
    ns = n // N_CHIPS
    nb = len(bs)
    tmw = m // row_blocks
    tk = min(TK_TOK, s)
    nk = s // tk

    def body(*refs):
        a_ref, b_refs, o_refs, accs = refs[0], refs[1:1 + nb], refs[2 + nb:2 + 2 * nb], refs[2 + 2 * nb:]
        k = pl.program_id(2)
        for b_ref, acc in zip(b_refs, accs):
            _accumulate(acc, lambda b_ref=b_ref: _dot_tn(a_ref[...].astype(BF16), b_ref[...].astype(BF16)), k, nk)

        @pl.when(k == nk - 1)
        def _():
            for o_ref, acc in zip(o_refs, accs):
                o_ref[...] = acc[...].astype(BF16)

    if a_3d:
        a_spec = pl.BlockSpec((None, tk, tmw), lambda j, r, k: (0, k, r))
    else:
        a_spec = pl.BlockSpec((tk, tmw), lambda j, r, k: (k, r))
    return pl.pallas_call(
        body, name=name, grid=(N_CHIPS, row_blocks, nk),
        in_specs=[a_spec] + [pl.BlockSpec((tk, ns), lambda j, r, k: (k, j))] * nb + [_ANY],
        out_specs=[pl.BlockSpec((None, tmw, ns), lambda j, r, k: (j, r, 0))] * nb,
        out_shape=[jax.ShapeDtypeStruct((N_CHIPS, m, ns), BF16)] * nb,
        scratch_shapes=[pltpu.VMEM((tmw, ns), F32)] * nb,
        compiler_params=_params(("parallel", "parallel", "arbitrary")),
    )(a, *bs, after)


def _dh1_ln_bwd(d_gu, w_gu_g, dpre2, xhat1, rstd1, ln1_g, after):
    s = d_gu.shape[0]
    d = dpre2.shape[1]
    hd = d // 2
    fs = w_gu_g.shape[2]
    tm = min(TM, s)

    def body(dgu_ref, w_ref, dp2_ref, xh_ref, rs_ref, g_ref, after_ref, dpre_ref, gg_ref, gb_ref, acc_lo, acc_hi):
        i, j, half = pl.program_id(0), pl.program_id(1), pl.program_id(2)

        def product():
            return _dot_nt(dgu_ref[...], w_ref[...])

        @pl.when(half == 0)
        def _():
            _accumulate(acc_lo, product, j, N_CHIPS)

        @pl.when(half == 1)
        def _():
            _accumulate(acc_hi, product, j, N_CHIPS)

        @pl.when((j == N_CHIPS - 1) & (half == 1))
        def _():
            @pl.when(i == 0)
            def _():
                gg_ref[...] = jnp.zeros_like(gg_ref)
                gb_ref[...] = jnp.zeros_like(gb_ref)

            def rows_fn(rows):
                dh = jnp.concatenate([acc_lo[rows, :], acc_hi[rows, :]], axis=1) + ALPHA * dp2_ref[rows, :]
                xhat = xh_ref[rows, :]
                dpre_ref[rows, :] = _ln_bwd(dh, xhat, rs_ref[rows, :], g_ref[...])
                gg_ref[...] += jnp.sum(dh * xhat, axis=0, keepdims=True)
                gb_ref[...] += jnp.sum(dh, axis=0, keepdims=True)

            _for_row_chunks(tm, rows_fn)

    row = pl.BlockSpec((tm, d), lambda i, j, h: (i, 0))
    vec = pl.BlockSpec((1, d), lambda i, j, h: (0, 0))
    act_blk = pl.BlockSpec((tm, fs), lambda i, j, h: (i, j))
    w_blk = pl.BlockSpec((None, hd, fs), lambda i, j, h: (j, h, 0))
    return pl.pallas_call(
        body, name="dh1_ln_bwd", grid=(s // tm, N_CHIPS, 2),
        in_specs=[act_blk, w_blk, row, row, pl.BlockSpec((tm, 1), lambda i, j, h: (i, 0)), _VMEM, _ANY],
        out_specs=[row, vec, vec],
        out_shape=[jax.ShapeDtypeStruct((s, d), F32), jax.ShapeDtypeStruct((1, d), F32),
                   jax.ShapeDtypeStruct((1, d), F32)],
        scratch_shapes=[pltpu.VMEM((tm, hd), F32)] * 2,
        compiler_params=_params(("arbitrary", "arbitrary", "arbitrary")),
    )(d_gu, w_gu_g, dpre2, xhat1, rstd1, ln1_g, after)


def _dmixed_rms_bwd(dpre1, w_out_g, ac, rstd, g_ac):
    s, d = dpre1.shape
    hd = d // 2
    tm = min(TM, s)

    def body(dp_ref, w_ref, ac_ref, rs_ref, g_ref, dac_ref, gg_ref):
        i = pl.program_id(1)
        dm = _dot_nt(dp_ref[...].astype(BF16), w_ref[...])
        pre = ac_ref[...]
        r = rs_ref[...]
        gdm = dm * g_ref[...]
        dac_ref[...] = r * gdm - pre * (r * r * r) * jnp.mean(gdm * pre, axis=-1, keepdims=True)
        gg = jnp.sum(dm * pre * r, axis=0, keepdims=True)

        @pl.when(i == 0)
        def _():
            gg_ref[...] = gg

        @pl.when(i > 0)
        def _():
            gg_ref[...] += gg

    return pl.pallas_call(
        body, name="dmixed_rms_bwd", grid=(2, s // tm),
        in_specs=[pl.BlockSpec((tm, d), lambda h, i: (i, 0)),
                  pl.BlockSpec((hd, d), lambda h, i: (h, 0)),
                  pl.BlockSpec((tm, hd), lambda h, i: (i, h)),
                  pl.BlockSpec((None, tm, 1), lambda h, i: (h, i, 0)),
                  pl.BlockSpec((1, hd), lambda h, i: (0, h))],
        out_specs=[pl.BlockSpec((tm, hd), lambda h, i: (i, h)),
                   pl.BlockSpec((1, hd), lambda h, i: (0, h))],
        out_shape=[jax.ShapeDtypeStruct((s, d), F32), jax.ShapeDtypeStruct((1, d), F32)],
        compiler_params=_params(("arbitrary", "arbitrary")),
    )(dpre1, w_out_g, ac, rstd, g_ac)


def _attention_bwd(proj, d_ac, cos_t, sin_t, sinks, after):
    s = proj.shape[0]
    qw = GROUP * N_KV_HEADS * HEAD_DIM
    kvw = N_KV_HEADS * HEAD_DIM
    nb = s // WINDOW
    nq = GROUP * N_KV_HEADS

    def body(cur_ref, prev_ref, do_ref, cos_ref, sin_ref, cosp_ref, sinp_ref, sinks_ref, after_ref,
             dq_ref, dcur_ref, dprev_ref, dsink_ref):
        n = pl.program_id(0)
        first = n == 0
        q, k_all, v_all, cos_q, sin_q = _roped_qkv(cur_ref, prev_ref, cos_ref, sin_ref, cosp_ref, sinp_ref, qw, kvw)
        kk2s = [_pair_operand(k_all, h) for h in range(N_KV_HEADS)]
        vv2s = [_pair_operand(v_all, h) for h in range(N_KV_HEADS)]
        qps, probs, p_sinks = _all_probs(q, kk2s, first, sinks_ref)
        dops = [do_ref[:, pair * PAIR:(pair + 1) * PAIR].astype(BF16) for pair in range(N_PAIRS)]
        d_probs = jnp.concatenate([_dot_nt(dops[pair], vv2s[pair // (GROUP // 2)]) for pair in range(N_PAIRS)], axis=0)
        d_s, ds_sinks = [], []
        for t in range(2):
            cols = slice(t * KEYS, (t + 1) * KEYS)
            delta = jnp.sum(probs[:, cols] * d_probs[:, cols], axis=1, keepdims=True)
            d_s.append(probs[:, cols] * (d_probs[:, cols] - delta))
            ds_sinks.append(-p_sinks[t] * delta)
        d_s = jnp.concatenate(d_s, axis=1).astype(BF16)
        probs = probs.astype(BF16)
        dq_parts, dk_tiles, dv_tiles, dsink_parts = [], [], [], []
        for h in range(N_KV_HEADS):
            dkk2, dvv2 = None, None
            for p in range(GROUP // 2):
                pair = (GROUP // 2) * h + p
                rows = slice(pair * WINDOW, (pair + 1) * WINDOW)
                dq_parts.append(_dot(d_s[rows], kk2s[h]) * ATTN_SCALE)
                dk_term = _dot_tn(d_s[rows], qps[pair])
                dv_term = _dot_tn(probs[rows], dops[pair])
                dkk2 = dk_term if dkk2 is None else dkk2 + dk_term
                dvv2 = dv_term if dvv2 is None else dvv2 + dv_term
                dsink_parts.extend([jnp.sum(ds_sinks[t][rows], axis=0, keepdims=True) for t in range(2)])
            dk_tiles.append(_pair_grad(dkk2, h))
            dv_tiles.append(_pair_grad(dvv2, h))
        dq_ref[...] = _rope(jnp.concatenate(dq_parts, axis=1), cos_q, sin_q, -1.0)
        dk = jnp.concatenate([dk_tiles[0] + dk_tiles[1], dk_tiles[2] + dk_tiles[3]], axis=1)
        dv = jnp.concatenate([dv_tiles[0] + dv_tiles[1], dv_tiles[2] + dv_tiles[3]], axis=1)
        dprev_ref[...] = jnp.concatenate([dk[:WINDOW], dv[:WINDOW]], axis=1)
        dcur_ref[...] = jnp.concatenate([dk[WINDOW:], dv[WINDOW:]], axis=1)
        dsink = jnp.concatenate(dsink_parts, axis=1)

        @pl.when(first)
        def _():
            dsink_ref[...] = dsink

        @pl.when(n > 0)
        def _():
            dsink_ref[...] += dsink

    tbl = pl.BlockSpec((WINDOW, kvw), lambda n: (n, 0))
    tbl_prev = pl.BlockSpec((WINDOW, kvw), lambda n: (jnp.maximum(n - 1, 0), 0))
    kv_blk = pl.BlockSpec((WINDOW, 2 * kvw), lambda n: (n, 0))
    return pl.pallas_call(
        body, name="attention_bwd", grid=(nb,),
        in_specs=[pl.BlockSpec((WINDOW, qw + 2 * kvw), lambda n: (n, 0)),
                  pl.BlockSpec((WINDOW, 2 * kvw), lambda n: (jnp.maximum(n - 1, 0), (qw // (2 * kvw)))),
                  pl.BlockSpec((WINDOW, qw), lambda n: (n, 0)),
                  tbl, tbl, tbl_prev, tbl_prev, _VMEM, _ANY],
        out_specs=[pl.BlockSpec((WINDOW, qw), lambda n: (n, 0)), kv_blk, kv_blk,
                   pl.BlockSpec((1, nq), lambda n: (0, 0))],
        out_shape=[jax.ShapeDtypeStruct((s, qw), F32), jax.ShapeDtypeStruct((s, 2 * kvw), F32),
                   jax.ShapeDtypeStruct((s, 2 * kvw), F32), jax.ShapeDtypeStruct((1, nq), F32)],
        compiler_params=_params(("arbitrary",)),
    )(proj, proj, d_ac, cos_t, sin_t, cos_t, sin_t, sinks, after)


def _dproj_assemble(proj, d_ac, dq, dkv_cur, dkv_prev, cos_t, sin_t, cw_full):
    s, in_w = proj.shape
    cw = dq.shape[1]
    kvw = N_KV_HEADS * HEAD_DIM
    blk_w = in_w // 3
    tb = WINDOW
    nb = s // tb

    def body(lo_ref, hi_ref, lo_p_ref, hi_p_ref, lo_n_ref, hi_n_ref, dconv_ref, dconv_n_ref,
             dq_ref, dcur_ref, dprev_n_ref, cos_ref, sin_ref, cw_ref, dproj_ref, gcw_ref):
        i = pl.program_id(0)
        last = i == nb - 1
        c_gate, b_gate, u = _split_cbu(lo_ref[...], hi_ref[...], cw)
        c_p, _, u_p = _split_cbu(lo_p_ref[...], hi_p_ref[...], cw)
        _, b_n, _ = _split_cbu(lo_n_ref[...], hi_n_ref[...], cw)
        z = c_gate * u
        z_p = jnp.where(i == 0, 0.0, c_p * u_p)
        z1 = _shift_down(z, z_p, 1)
        z2 = _shift_down(z, z_p, 2)
        w0, w1, w2 = _conv_taps(cw_ref)
        y = w0 * z2 + w1 * z1 + w2 * z
        d_conv = dconv_ref[...]
        d_b = d_conv * y
        d_y = d_conv * b_gate
        d_y_n = jnp.where(last, 0.0, dconv_n_ref[...] * b_n[:dconv_n_ref.shape[0]])
        d_z = w2 * d_y + w1 * _shift_up(d_y, d_y_n, 1) + w0 * _shift_up(d_y, d_y_n, 2)
        d_c = d_z * u
        d_u = d_z * c_gate
        gcw = jnp.concatenate([jnp.sum(d_y * z2, axis=0, keepdims=True), jnp.sum(d_y * z1, axis=0, keepdims=True),
                               jnp.sum(d_y * z, axis=0, keepdims=True)], axis=0)

        @pl.when(i == 0)
        def _():
            gcw_ref[...] = gcw

        @pl.when(i > 0)
        def _():
            gcw_ref[...] += gcw

        dkv = dcur_ref[...] + jnp.where(last, 0.0, dprev_n_ref[...])
        dk = _rope(dkv[:, :kvw], cos_ref[...], sin_ref[...], -1.0)
        dproj_ref[...] = jnp.concatenate([dq_ref[...], dk, dkv[:, kvw:], d_c, d_b, d_u], axis=1).astype(BF16)

    prev_halo = lambda i: jnp.maximum(i * (tb // HALO_ROWS) - 1, 0)
    next_halo = lambda i: jnp.minimum((i + 1) * (tb // HALO_ROWS), s // HALO_ROWS - 1)
    next8 = lambda i: jnp.minimum((i + 1) * (tb // 8), s // 8 - 1)
    nxt = lambda i: jnp.minimum(i + 1, nb - 1)
    return pl.pallas_call(
        body, name="dproj_assemble", grid=(nb,),
        in_specs=[pl.BlockSpec((tb, blk_w), lambda i: (i, 1)),
                  pl.BlockSpec((tb, blk_w), lambda i: (i, 2)),
                  pl.BlockSpec((HALO_ROWS, blk_w), lambda i: (prev_halo(i), 1)),
                  pl.BlockSpec((HALO_ROWS, blk_w), lambda i: (prev_halo(i), 2)),
                  pl.BlockSpec((HALO_ROWS, blk_w), lambda i: (next_halo(i), 1)),
                  pl.BlockSpec((HALO_ROWS, blk_w), lambda i: (next_halo(i), 2)),
                  pl.BlockSpec((tb, cw), lambda i: (i, 1)),
                  pl.BlockSpec((8, cw), lambda i: (next8(i), 1)),
                  pl.BlockSpec((tb, cw), lambda i: (i, 0)),
                  pl.BlockSpec((tb, 2 * kvw), lambda i: (i, 0)),
                  pl.BlockSpec((tb, 2 * kvw), lambda i: (nxt(i), 0)),
                  pl.BlockSpec((tb, kvw), lambda i: (i, 0)),
                  pl.BlockSpec((tb, kvw), lambda i: (i, 0)),
                  _VMEM],
        out_specs=[pl.BlockSpec((tb, in_w), lambda i: (i, 0)),
                   pl.BlockSpec((3, cw), lambda i: (0, 0))],
        out_shape=[jax.ShapeDtypeStruct((s, in_w), BF16), jax.ShapeDtypeStruct((3, cw), F32)],
        compiler_params=_params(("arbitrary",)),
    )(proj, proj, proj, proj, proj, proj, d_ac, d_ac, dq, dkv_cur, dkv_prev, cos_t, sin_t, cw_full)


def _dx(d_proj, w_in_g, dpre1, after):
    s, in_w = d_proj.shape
    ns, d, ncol = w_in_g.shape
    tm = min(TM, s)

    def body(dp_ref, w_ref, r_ref, after_ref, o_ref, acc):
        j = pl.program_id(1)
        _accumulate(acc, lambda: _dot_nt(dp_ref[...], w_ref[...]), j, ns)

        @pl.when(j == ns - 1)
        def _():
            o_ref[...] = acc[...] + ALPHA * r_ref[...]

    return pl.pallas_call(
        body, name="dx", grid=(s // tm, ns),
        in_specs=[pl.BlockSpec((tm, ncol), lambda i, j: (i, j)),
                  pl.BlockSpec((None, d, ncol), lambda i, j: (j, 0, 0)),
                  pl.BlockSpec((tm, d), lambda i, j: (i, 0)), _ANY],
        out_specs=pl.BlockSpec((None, tm, d), lambda i, j: (0, i, 0)),
        out_shape=jax.ShapeDtypeStruct((1, s, d), F32),
        scratch_shapes=[pltpu.VMEM((tm, d), F32)],
        compiler_params=_params(("parallel", "arbitrary")),
    )(d_proj, w_in_g, dpre1, after)


def kernel(x, positions, w_in, conv_w, sinks, g_attn, g_conv, w_out, ln1_g, ln1_b, w_gate, w_up, w_down, ln2_g, ln2_b, loss_target, m_w_in, m_conv_w, m_sinks, m_g_attn, m_g_conv, m_w_out, m_ln1_g, m_ln1_b, m_w_gate, m_w_up, m_w_down, m_ln2_g, m_ln2_b, v_w_in, v_conv_w, v_sinks, v_g_attn, v_g_conv, v_w_out, v_ln1_g, v_ln1_b, v_w_gate, v_w_up, v_w_down, v_ln2_g, v_ln2_b):
    s = x.shape[1]
    d = x.shape[2]

    chip_vec = _chip_id(lax.axis_index("x"), lax.axis_index("y")).astype(jnp.int32).reshape(1)
    wnames = ["w_in", "w_out", "w_gu", "w_down"]
    buf_in = _cast_weight(w_in, chip_vec, chip_vec, "cast_w_in")
    flight_in, token_in = _gather_start([buf_in], chip_vec, "gather_start_w_in")
    cw_buf = lax.dynamic_update_slice(jnp.zeros((N_CHIPS,) + conv_w.shape[1:], F32), conv_w, (chip_vec[0], 0, 0))
    cw_flight = _flight_start("conv_w_start", [cw_buf], _conv_w_plan(), 3, token_in)
    buf_gu = _cast_weight(w_gate, chip_vec, token_in, "cast_w_gate", 0, 2)
    buf_gu = _cast_weight(w_up, chip_vec, buf_gu, "cast_w_up", 1, 2)
    bufs = [_cast_weight(w_out, chip_vec, token_in, "cast_w_out"), buf_gu,
            _cast_weight(w_down, chip_vec, token_in, "cast_w_down")]
    flights_rest, token = _gather_start(bufs, token_in, "gather_start_rest")
    flights = flight_in + flights_rest

    def gathered(i, after):
        send_sems, recv_sems, buf = flights[i]
        buf = _gather_wait(send_sems, recv_sems, buf, after, "gather_wait_" + wnames[i])
        return _sibling_fill(buf, "sibling_fill_" + wnames[i])

    g_ac = jnp.concatenate([g_attn, g_conv], axis=1)

    proj_own = _in_proj(x, _after(flights[0][2], token), chip_vec, 1, None, "in_proj_own")
    cos_t, sin_t = _rope_tables(positions.reshape(s, 1) + token[0:1, 0:1].astype(jnp.int32))
    w_in_g = gathered(0, _after(cos_t, proj_own))
    proj = _in_proj(x, w_in_g, chip_vec + 1, N_CHIPS - 1, proj_own, "in_proj_rest")
    send_sems, recv_sems, buf_out = flights[1]
    buf_out = _gather_wait(send_sems, recv_sems, buf_out, proj, "gather_wait_w_out")
    fill_out = _flight_start("fill_start_w_out", [buf_out], _fill_plan(1), 3, chip_vec)
    attn = _attention_fwd(_after(proj, fill_out[2][0]), cos_t, sin_t, sinks)
    (cw_full,) = _flight_wait("conv_w_wait", cw_flight, _conv_w_plan(), attn)
    mixed, ac, rstd_ac = _conv_norm(proj, attn, cw_full, g_ac)
    (w_out_g,) = _flight_wait("fill_wait_w_out", fill_out, _fill_plan(1), mixed)
    w_out_full = w_out_g.reshape(d, d)
    xhat1, h1, rstd1 = _out_proj_ln(mixed, w_out_full, x, ln1_g, ln1_b)
    w_gu_g = gathered(2, h1)
    act, ab = _gate_up(h1, w_gu_g)
    w_down_full = gathered(3, act).reshape(-1, d)
    dpre2, dpre2_16, loss_part, g_ln2_g, g_ln2_b = _down_ln_loss(act, w_down_full, xhat1, ln1_g, ln1_b, ln2_g, ln2_b,
                                                                 loss_target)

    cvec = lax.axis_index("c").astype(jnp.int32).reshape(1)

    def exchange_begin(parts, nme):
        bufs = []
        for part in parts:
            ns, r, cdim = part.shape
            bufs.extend([part, lax.empty((ns, r // 2, cdim), part.dtype)])
        return _flight_start("exchange_start_" + nme, bufs, _exchange_plan(len(parts)), len(parts), cvec)

    def exchange_end(flight, n_parts, after, nme):
        bufs = _flight_wait("exchange_wait_" + nme, flight, _exchange_plan(n_parts), after)
        return [(bufs[2 * w], bufs[2 * w + 1]) for w in range(n_parts)]

    def scatter_begin(part, got, nme):
        return _scatter_start(_add_halves(part, got, cvec, "add_halves_" + nme), "scatter_start_" + nme)

    d_gu = _dact_silu_bwd(dpre2_16, w_down_full, ab)
    p_down = _grad_rows(act, dpre2_16, d_gu, "grad_w_down")
    x_down = exchange_begin([p_down], "w_down")
    (p_gu,) = _grad_cols(h1, [d_gu], x_down[2][0], "grad_w_gate_up")
    ((p_down, got),) = exchange_end(x_down, 1, p_gu, "w_down")
    f_down = scatter_begin(p_down, got, "w_down")
    x_gu = exchange_begin([_after(p_gu, f_down[2])], "w_gu")
    dpre1, g_ln1_g, g_ln1_b = _dh1_ln_bwd(d_gu, w_gu_g, dpre2, xhat1, rstd1, ln1_g, x_gu[2][0])
    ((p_gu, got),) = exchange_end(x_gu, 1, dpre1, "w_gu")
    f_gu = scatter_begin(p_gu, got, "w_gu")
    d_ac, g_g_ac = _dmixed_rms_bwd(_after(dpre1, f_gu[2]), w_out_full, ac, rstd_ac, g_ac)
    p_out = _grad_rows(mixed, dpre1, d_ac, "grad_w_out")
    x_out = exchange_begin([p_out], "w_out")
    dq, dkv_cur, dkv_prev, g_sinks = _attention_bwd(proj, d_ac, cos_t, sin_t, sinks, x_out[2][0])
    ((p_out, got),) = exchange_end(x_out, 1, dq, "w_out")
    f_out = scatter_begin(p_out, got, "w_out")
    d_proj, g_conv_w = _dproj_assemble(proj, _after(d_ac, f_out[2]), dq, dkv_cur, dkv_prev, cos_t, sin_t, cw_full)
    (p_in,) = _grad_cols(x, [d_proj], d_proj, "grad_w_in", a_3d=True)
    x_in = exchange_begin([p_in], "w_in")
    grad_x = _dx(d_proj, w_in_g, dpre1, x_in[2][0])
    red = _allreduce_small(g_ln2_g, g_ln2_b, g_ln1_g, g_ln1_b, g_g_ac, g_conv_w, g_sinks, loss_part, grad_x)
    ((p_in, got),) = exchange_end(x_in, 1, red, "w_in")
    f_in = scatter_begin(p_in, got, "w_in")

    pos_vec = jnp.concatenate([chip_vec, cvec])
    shards = {"w_in": (w_in, m_w_in, v_w_in), "w_out": (w_out, m_w_out, v_w_out), "w_gate": (w_gate, m_w_gate, v_w_gate),
              "w_up": (w_up, m_w_up, v_w_up), "w_down": (w_down, m_w_down, v_w_down)}
    early = [("w_down", ["w_down"]), ("w_gu", ["w_gate", "w_up"]), ("w_out", ["w_out"])]
    after = f_in[2]
    completing = {}
    for (nme, _), f in zip(early, [f_down, f_gu, f_out]):
        sums, land = _scatter_wait(*f, after, "scatter_wait_" + nme)
        completing[nme] = _flight_start("complete_start_" + nme, [sums, land], _complete_plan(1), 4, cvec)
        after = completing[nme][2][1]
    big = {}
    for nme, members in early:
        sums, land = _flight_wait("complete_wait_" + nme, completing[nme], _complete_plan(1), after)
        for col_block, member in enumerate(members):
            big[member] = _adamw_shard(*shards[member], land, sums, pos_vec, "adamw_" + member, col_block)
            after = big[member][0]
    sums, land = _scatter_wait(*f_in, after, "scatter_wait_w_in")
    (land,) = _complete_chip_sums([sums], [land])
    big["w_in"] = _adamw_shard(*shards["w_in"], land, sums, pos_vec, "adamw_w_in")
    small = _adamw_small(red, {
        "sinks": (sinks, m_sinks, v_sinks), "g_attn": (g_attn, m_g_attn, v_g_attn),
        "g_conv": (g_conv, m_g_conv, v_g_conv), "ln1_g": (ln1_g, m_ln1_g, v_ln1_g),
        "ln1_b": (ln1_b, m_ln1_b, v_ln1_b), "ln2_g": (ln2_g, m_ln2_g, v_ln2_g),
        "ln2_b": (ln2_b, m_ln2_b, v_ln2_b), "conv_w": (conv_w, m_conv_w, v_conv_w)})
    res = {**big, **small}
    order = ["w_in", "conv_w", "sinks", "g_attn", "g_conv", "w_out", "ln1_g", "ln1_b", "w_gate", "w_up", "w_down",
             "ln2_g", "ln2_b"]
    loss = red[6, d // 2 + 128]
    return (loss, grad_x, *[res[n][0] for n in order], *[res[n][1] for n in order],
            *[res[n][2] for n in order], *[res[n][3] for n in order])
```

```python
import functools

import numpy as np
import jax
import jax.numpy as jnp
from jax import lax
from jax.experimental import pallas as pl
from jax.experimental.pallas import tpu as pltpu

F32 = jnp.float32
BF16 = jnp.bfloat16
MESH = pl.DeviceIdType.MESH

HEAD_DIM = 64
N_KV_HEADS = 4
GROUP = 4
WINDOW = 128
ROT_DIM = 16
ROPE_THETA = 500000.0
ATTN_SCALE = HEAD_DIM ** -0.5
ALPHA = 2.0 ** 0.25
LN_EPS = 1e-5
RMS_EPS = 1e-6
ADAM_LR = 0.001
ADAM_B1 = 0.9
ADAM_B2 = 0.999
ADAM_EPS = 1e-08
ADAM_WD = 0.01
ADAM_STEP = 10
N_CHIPS = 4
NEG_BIG = -1e30

V7X_VMEM_BYTES = 64 * 1024 * 1024
VMEM_LIMIT = V7X_VMEM_BYTES - 6 * 1024 * 1024

TM = 512
TK_TOK = 1024
TB_CONV = 256
TR_ELT = 256
ROW_CHUNK = 128
HALO_ROWS = 16


def _params(sem):
    return pltpu.CompilerParams(dimension_semantics=sem, vmem_limit_bytes=VMEM_LIMIT)


def _row_tile(rows, target):
    best = None
    for t in range(16, min(rows, target) + 1, 16):
        if rows % t == 0:
            best = t
    assert best is not None, (rows, target)
    return best


def _dot(a, b):
    return jnp.dot(a, b, preferred_element_type=F32)


def _dot_nt(a, b):
    return lax.dot_general(a, b, (((1,), (1,)), ((), ())), preferred_element_type=F32)


def _dot_tn(a, b):
    return lax.dot_general(a, b, (((0,), (0,)), ((), ())), preferred_element_type=F32)


def _mesh_pos():
    x, y, c = lax.axis_index("x"), lax.axis_index("y"), lax.axis_index("c")
    chips = [(1 - x, y), (x, 1 - y), (1 - x, 1 - y)]
    return x, y, c, chips


def _chip_id(px, py):
    return 2 * px + py


def _rope(t, cos, sgn_sin, sign):
    w = t.shape[1]
    lane = lax.broadcasted_iota(jnp.int32, t.shape, 1) & (HEAD_DIM - 1)
    partner = jnp.where(lane < ROT_DIM // 2, pltpu.roll(t, w - ROT_DIM // 2, 1), pltpu.roll(t, ROT_DIM // 2, 1))
    return t * cos + sign * (partner * sgn_sin)


def _tile_lanes(t, n):
    return jnp.concatenate([t] * n, axis=1)


def _sigmoid(g):
    return 1.0 / (1.0 + jnp.exp(-g))


def _for_row_chunks(n_rows, fn):
    def step(r, carry):
        fn(pl.ds(pl.multiple_of(r * ROW_CHUNK, ROW_CHUNK), ROW_CHUNK))
        return carry

    lax.fori_loop(0, n_rows // ROW_CHUNK, step, 0)


def _accumulate(acc, make_val, k, nk):
    if nk == 1:
        acc[...] = make_val()
        return

    @pl.when(k == 0)
    def _():
        acc[...] = jnp.zeros_like(acc)

    acc[...] += make_val()


def _ln_fwd(pre):
    mu = jnp.mean(pre, axis=-1, keepdims=True)
    cen = pre - mu
    var = jnp.mean(cen * cen, axis=-1, keepdims=True)
    rstd = lax.rsqrt(var + LN_EPS)
    return cen * rstd, rstd


def _ln_bwd(dy, xhat, rstd, g):
    dxhat = dy * g
    m1 = jnp.mean(dxhat, axis=-1, keepdims=True)
    m2 = jnp.mean(dxhat * xhat, axis=-1, keepdims=True)
    return rstd * (dxhat - m1 - xhat * m2)


def _cast_weight(w, chip_vec, after, name, col_block=0, n_col_blocks=1):
    _, r, c = w.shape
    tr = _row_tile(r, TR_ELT)

    def body(chip_ref, w_ref, after_ref, o_ref):
        o_ref[...] = w_ref[...].astype(BF16)

    grid_spec = pltpu.PrefetchScalarGridSpec(
        num_scalar_prefetch=1, grid=(r // tr,),
        in_specs=[pl.BlockSpec((None, tr, c), lambda i, chip_ref: (0, i, 0)), _ANY],
        out_specs=pl.BlockSpec((None, tr, c), lambda i, chip_ref: (chip_ref[0], i, col_block)))
    return pl.pallas_call(
        body, name=name, grid_spec=grid_spec,
        out_shape=jax.ShapeDtypeStruct((N_CHIPS, r, n_col_blocks * c), BF16),
        input_output_aliases={2: 0} if col_block else {},
        compiler_params=_params(("parallel",)),
    )(chip_vec, w, after)


_HBM = pl.BlockSpec(memory_space=pltpu.HBM)
_VMEM = pl.BlockSpec(memory_space=pltpu.VMEM)


_SEM = pl.BlockSpec(memory_space=pltpu.SEMAPHORE)
_ANY = pl.BlockSpec(memory_space=pl.ANY)
_EFFECT = pltpu.SideEffectType.DATAFLOW_SIDE_EFFECTING


def _chip_copy(buf, k, chip_of_src, half_rows, send_sems, recv_sems, to):
    part = buf.at[chip_of_src, half_rows]
    return pltpu.make_async_remote_copy(
        src_ref=part, dst_ref=part, send_sem=send_sems.at[k], recv_sem=recv_sems.at[k], device_id=to, device_id_type=MESH)


def _half_rows(buf, which):
    hr = buf.shape[1] // 2
    return pl.ds(which * hr, hr)


def _after(value, dep):
    return lax.optimization_barrier((value, dep))[0]


def _flight_start(name, bufs, plan, n_sems, after):
    n = len(bufs)

    def body(*refs):
        sends, _ = plan(refs[:n], refs[n + 1], refs[n + 2])
        for cp in sends:
            cp.start()

    outs = pl.pallas_call(
        body, name=name,
        in_specs=[_HBM] * n + [_ANY], out_specs=[_SEM, _SEM] + [_HBM] * n,
        out_shape=[pltpu.SemaphoreType.DMA((n_sems,))] * 2 + [pltpu.HBM(b.shape, b.dtype) for b in bufs],
        input_output_aliases={i: 2 + i for i in range(n)},
        compiler_params=pltpu.CompilerParams(has_side_effects=_EFFECT),
    )(*[pltpu.with_memory_space_constraint(b, pltpu.HBM) for b in bufs], after)
    return outs[0], outs[1], list(outs[2:])


def _flight_wait(name, flight, plan, after):
    send_sems, recv_sems, bufs = flight
    n = len(bufs)

    def body(*refs):
        sends, recvs = plan(refs[:n], refs[n], refs[n + 1])
        for cp in sends:
            cp.wait_send()
        for cp in recvs:
            cp.wait_recv()

    outs = pl.pallas_call(
        body, name=name,
        in_specs=[_HBM] * n + [_SEM, _SEM, _ANY], out_specs=[_HBM] * n,
        out_shape=[pltpu.HBM(b.shape, b.dtype) for b in bufs],
        input_output_aliases={i: i for i in range(n)},
        compiler_params=pltpu.CompilerParams(has_side_effects=_EFFECT),
    )(*bufs, send_sems, recv_sems, after)
    return list(outs)


def _fill_plan(n_bufs):
    def plan(refs, send_sems, recv_sems):
        x, y, c, chips = _mesh_pos()
        sibling = (x, y, 1 - c)
        sends, recvs = [], []
        for w in range(n_bufs):
            for k, chip in enumerate(chips):
                slot = _chip_id(*chip)
                sends.append(_chip_copy(refs[w], 3 * w + k, slot, _half_rows(refs[w], c), send_sems, recv_sems, sibling))
                recvs.append(_chip_copy(refs[w], 3 * w + k, slot, _half_rows(refs[w], 1 - c), send_sems, recv_sems,
                                        sibling))
        return sends, recvs
    return plan


def _conv_w_plan():
    def plan(refs, send_sems, recv_sems):
        x, y, c, chips = _mesh_pos()
        me = _chip_id(x, y)
        (buf,) = refs
        sends, recvs = [], []
        for k, chip in enumerate(chips):
            for slot, into in ((me, sends), (_chip_id(*chip), recvs)):
                into.append(pltpu.make_async_remote_copy(
                    src_ref=buf.at[slot], dst_ref=buf.at[slot], send_sem=send_sems.at[k], recv_sem=recv_sems.at[k],
                    device_id=(*chip, c), device_id_type=MESH))
        return sends, recvs
    return plan


def _exchange_plan(n_parts):
    def plan(refs, send_sems, recv_sems):
        x, y, c, _ = _mesh_pos()
        copies = []
        for w in range(n_parts):
            part, got = refs[2 * w], refs[2 * w + 1]
            hr = got.shape[1]
            copies.append(pltpu.make_async_remote_copy(
                src_ref=part.at[:, pl.ds((1 - c) * hr, hr)], dst_ref=got, send_sem=send_sems.at[w],
                recv_sem=recv_sems.at[w], device_id=(x, y, 1 - c), device_id_type=MESH))
        return copies, copies
    return plan


def _gather_start(bufs, after, name):
    n = len(bufs)

    def body(*refs):
        ins = refs[:n]
        sends, recvs = refs[n + 1:2 * n + 1], refs[2 * n + 1:3 * n + 1]
        token = refs[4 * n + 1]
        x, y, c, chips = _mesh_pos()
        me = _chip_id(x, y)
        for w in range(n):
            for k, chip in enumerate(chips):
                _chip_copy(ins[w], k, me, _half_rows(ins[w], c), sends[w], recvs[w], (*chip, c)).start()
        token[...] = jnp.zeros_like(token)

    outs = pl.pallas_call(
        body, name=name,
        in_specs=[_HBM] * n + [_ANY],
        out_specs=[_SEM] * (2 * n) + [_HBM] * n + [_VMEM],
        out_shape=[pltpu.SemaphoreType.DMA((3,))] * (2 * n) + [pltpu.HBM(b.shape, b.dtype) for b in bufs]
        + [jax.ShapeDtypeStruct((8, 128), F32)],
        input_output_aliases={w: 2 * n + w for w in range(n)},
        compiler_params=pltpu.CompilerParams(has_side_effects=_EFFECT),
    )(*[pltpu.with_memory_space_constraint(b, pltpu.HBM) for b in bufs], after)
    return [(outs[w], outs[n + w], outs[2 * n + w]) for w in range(n)], outs[3 * n]


def _gather_wait(send_sems, recv_sems, buf, after, name):
    def body(buf_ref, send_ref, recv_ref, after_ref, out_ref):
        x, y, c, chips = _mesh_pos()
        me = _chip_id(x, y)
        for k, chip in enumerate(chips):
            _chip_copy(buf_ref, k, me, _half_rows(buf_ref, c), send_ref, recv_ref, (*chip, c)).wait_send()
        for k, chip in enumerate(chips):
            _chip_copy(buf_ref, k, _chip_id(*chip), _half_rows(buf_ref, c), send_ref, recv_ref, (*chip, c)).wait_recv()

    return pl.pallas_call(
        body, name=name,
        in_specs=[_HBM, _SEM, _SEM, _ANY], out_specs=_HBM,
        out_shape=pltpu.HBM(buf.shape, buf.dtype),
        input_output_aliases={0: 0},
        compiler_params=pltpu.CompilerParams(has_side_effects=_EFFECT),
    )(buf, send_sems, recv_sems, after)


def _sibling_fill(buf, name, own_too=False):
    n_copies = 4 if own_too else 3

    def body(buf_ref, out_ref, send_sems, recv_sems):
        x, y, c, chips = _mesh_pos()
        sibling = (x, y, 1 - c)
        slots = [_chip_id(*chip) for chip in chips] + ([_chip_id(x, y)] if own_too else [])
        copies = []
        for k, slot in enumerate(slots):
            cp = _chip_copy(out_ref, k, slot, _half_rows(out_ref, c), send_sems, recv_sems, sibling)
            cp.start()
            copies.append(cp)
        for k, slot in enumerate(slots):
            _chip_copy(out_ref, k, slot, _half_rows(out_ref, 1 - c), send_sems, recv_sems, sibling).wait_recv()
        for cp in copies:
            cp.wait_send()

    return pl.pallas_call(
        body, name=name,
        in_specs=[_HBM], out_specs=_HBM,
        out_shape=jax.ShapeDtypeStruct(buf.shape, buf.dtype),
        input_output_aliases={0: 0},
        scratch_shapes=[pltpu.SemaphoreType.DMA((n_copies,)), pltpu.SemaphoreType.DMA((n_copies,))],
    )(buf)


def _allgather_conv_w(cw):
    _, kw, cs = cw.shape

    def body(cw_ref, out_ref, send_sems, recv_sems):
        x, y, c, chips = _mesh_pos()
        me = _chip_id(x, y)
        out_ref[pl.ds(me, 1)] = cw_ref[...]
        copies = []
        for k, chip in enumerate(chips):
            cp = pltpu.make_async_remote_copy(
                src_ref=cw_ref.at[0], dst_ref=out_ref.at[me], send_sem=send_sems.at[k], recv_sem=recv_sems.at[k],
                device_id=(*chip, c), device_id_type=MESH)
            cp.start()
            copies.append(cp)
        for k, chip in enumerate(chips):
            pltpu.make_async_remote_copy(
                src_ref=cw_ref.at[0], dst_ref=out_ref.at[_chip_id(*chip)], send_sem=send_sems.at[k],
                recv_sem=recv_sems.at[k], device_id=(*chip, c), device_id_type=MESH).wait_recv()
        for cp in copies:
            cp.wait_send()

    return pl.pallas_call(
        body, name="allgather_conv_w",
        in_specs=[_VMEM], out_specs=_VMEM,
        out_shape=jax.ShapeDtypeStruct((N_CHIPS, kw, cs), F32),
        scratch_shapes=[pltpu.SemaphoreType.DMA((3,)), pltpu.SemaphoreType.DMA((3,))],
    )(cw)


def _exchange_halves(parts, after, name):
    n = len(parts)
    shapes = [p.shape for p in parts]

    def body(*refs):
        ins, outs = refs[:n], refs[n + 1:2 * n + 1]
        send_sems, recv_sems = refs[2 * n + 1:]
        x, y, c, _ = _mesh_pos()
        copies = []
        for w in range(n):
            hr = shapes[w][1] // 2
            cp = pltpu.make_async_remote_copy(
                src_ref=ins[w].at[:, pl.ds((1 - c) * hr, hr)], dst_ref=outs[w],
                send_sem=send_sems.at[w], recv_sem=recv_sems.at[w],
                device_id=(x, y, 1 - c), device_id_type=MESH)
            cp.start()
            copies.append(cp)
        for cp in copies:
            cp.wait()

    return pl.pallas_call(
        body, name=name,
        in_specs=[_HBM] * n + [_ANY], out_specs=[_HBM] * n,
        out_shape=[jax.ShapeDtypeStruct((s[0], s[1] // 2, s[2]), BF16) for s in shapes],
        scratch_shapes=[pltpu.SemaphoreType.DMA((n,)), pltpu.SemaphoreType.DMA((n,))],
    )(*parts, after)


def _add_halves(part, got, cvec, name):
    ns, r, cdim = part.shape
    hr = r // 2
    tr = _row_tile(hr, TR_ELT)
    nblk = hr // tr

    def body(c_ref, a_ref, b_ref, o_ref):
        o_ref[...] = (a_ref[...].astype(F32) + b_ref[...].astype(F32)).astype(BF16)

    grid_spec = pltpu.PrefetchScalarGridSpec(
        num_scalar_prefetch=1, grid=(ns, nblk),
        in_specs=[pl.BlockSpec((None, tr, cdim), lambda s, i, c_ref: (s, c_ref[0] * nblk + i, 0)),
                  pl.BlockSpec((None, tr, cdim), lambda s, i, c_ref: (s, i, 0))],
        out_specs=pl.BlockSpec((None, tr, cdim), lambda s, i, c_ref: (s, i, 0)))
    return pl.pallas_call(
        body, name=name, grid_spec=grid_spec,
        out_shape=jax.ShapeDtypeStruct((ns, hr, cdim), BF16),
        compiler_params=_params(("parallel", "parallel")),
    )(cvec, part, got)


def _scatter_copy(sums_ref, land_ref, k, src_slot, dst_slot, c, send_sems, recv_sems, to):
    return pltpu.make_async_remote_copy(
        src_ref=sums_ref.at[src_slot], dst_ref=land_ref.at[dst_slot, _half_rows(land_ref, c)],
        send_sem=send_sems.at[k], recv_sem=recv_sems.at[k], device_id=to, device_id_type=MESH)


def _scatter_start(sums, name):
    ns, hr, cdim = sums.shape
    land = lax.empty((ns, 2 * hr, cdim), sums.dtype)

    def body(sums_ref, land_ref, send_sems, recv_sems, sums_thru, land_thru):
        x, y, c, chips = _mesh_pos()
        me = _chip_id(x, y)
        for k, chip in enumerate(chips):
            _scatter_copy(sums_ref, land_ref, k, _chip_id(*chip), me, c, send_sems, recv_sems, (*chip, c)).start()

    return pl.pallas_call(
        body, name=name,
        in_specs=[_HBM, _HBM], out_specs=[_SEM, _SEM, _HBM, _HBM],
        out_shape=[pltpu.SemaphoreType.DMA((3,)), pltpu.SemaphoreType.DMA((3,)),
                   pltpu.HBM(sums.shape, sums.dtype), pltpu.HBM(land.shape, land.dtype)],
        input_output_aliases={0: 2, 1: 3},
        compiler_params=pltpu.CompilerParams(has_side_effects=_EFFECT),
    )(pltpu.with_memory_space_constraint(sums, pltpu.HBM), pltpu.with_memory_space_constraint(land, pltpu.HBM))


def _scatter_wait(send_sems, recv_sems, sums, land, after, name):
    def body(sums_ref, land_ref, send_ref, recv_ref, after_ref, sums_out, land_out):
        x, y, c, chips = _mesh_pos()
        me = _chip_id(x, y)
        for k, chip in enumerate(chips):
            _scatter_copy(sums_ref, land_ref, k, _chip_id(*chip), me, c, send_ref, recv_ref, (*chip, c)).wait_send()
        for k, chip in enumerate(chips):
            _scatter_copy(sums_ref, land_ref, k, me, _chip_id(*chip), c, send_ref, recv_ref, (*chip, c)).wait_recv()

    return pl.pallas_call(
        body, name=name,
        in_specs=[_HBM, _HBM, _SEM, _SEM, _ANY], out_specs=[_HBM, _HBM],
        out_shape=[pltpu.HBM(sums.shape, sums.dtype), pltpu.HBM(land.shape, land.dtype)],
        input_output_aliases={0: 0, 1: 1},
        compiler_params=pltpu.CompilerParams(has_side_effects=_EFFECT),
    )(sums, land, send_sems, recv_sems, after)


def _complete_plan(n_weights):
    def plan(refs, send_sems, recv_sems):
        x, y, c, chips = _mesh_pos()
        me = _chip_id(x, y)
        sibling = (x, y, 1 - c)
        sends, recvs = [], []
        for w in range(n_weights):
            sums, land = refs[2 * w], refs[2 * w + 1]
            sends.append(_scatter_copy(sums, land, 4 * w + 3, me, me, c, send_sems, recv_sems, sibling))
            recvs.append(_scatter_copy(sums, land, 4 * w + 3, me, me, 1 - c, send_sems, recv_sems, sibling))
            for k, chip in enumerate(chips):
                slot = _chip_id(*chip)
                sends.append(_chip_copy(land, 4 * w + k, slot, _half_rows(land, c), send_sems, recv_sems, sibling))
                recvs.append(_chip_copy(land, 4 * w + k, slot, _half_rows(land, 1 - c), send_sems, recv_sems, sibling))
        return sends, recvs
    return plan


def _complete_chip_sums(sums, lands):
    n = len(sums)

    def body(*refs):
        sums_refs, outs = refs[:n], refs[2 * n:3 * n]
        send_sems, recv_sems = refs[3 * n:]
        x, y, c, chips = _mesh_pos()
        me = _chip_id(x, y)
        sibling = (x, y, 1 - c)
        slots = [_chip_id(*chip) for chip in chips]
        sent = []
        for w in range(n):
            out = outs[w]
            cp = _scatter_copy(sums_refs[w], out, 3, me, me, c, send_sems.at[w], recv_sems.at[w], sibling)
            cp.start()
            sent.append(cp)
            for k, slot in enumerate(slots):
                cp = _chip_copy(out, k, slot, _half_rows(out, c), send_sems.at[w], recv_sems.at[w], sibling)
                cp.start()
                sent.append(cp)
        for w in range(n):
            out = outs[w]
            _scatter_copy(sums_refs[w], out, 3, me, me, 1 - c, send_sems.at[w], recv_sems.at[w], sibling).wait_recv()
            for k, slot in enumerate(slots):
                _chip_copy(out, k, slot, _half_rows(out, 1 - c), send_sems.at[w], recv_sems.at[w], sibling).wait_recv()
        for cp in sent:
            cp.wait_send()

    return pl.pallas_call(
        body, name="complete_chip_sums",
        in_specs=[_HBM] * (2 * n), out_specs=[_HBM] * n,
        out_shape=[jax.ShapeDtypeStruct(b.shape, b.dtype) for b in lands],
        input_output_aliases={n + w: w for w in range(n)},
        scratch_shapes=[pltpu.SemaphoreType.DMA((n, 4)), pltpu.SemaphoreType.DMA((n, 4))],
    )(*sums, *lands)


SMALL_ROWS = 8


def _allreduce_small(gl2g, gl2b, gl1g, gl1b, g_ac, gcw, gsink, loss, after):
    d = gl2g.shape[1]
    hd = d // 2
    nq = gsink.shape[1]

    def body(a_ref, b_ref, c_ref, d_ref, e_ref, cw_ref, sk_ref, ls_ref, after_ref, out_ref, mine, gath, send_sems,
             recv_sems):
        x, y, c, _ = _mesh_pos()
        me = 4 * x + 2 * y + c
        mine[...] = jnp.zeros_like(mine)
        mine[0:1, :] = a_ref[...]
        mine[1:2, :] = b_ref[...]
        mine[2:3, :] = c_ref[...]
        mine[3:4, :] = d_ref[...]
        mine[4:5, :] = e_ref[...]
        mine[5:6, 0:hd] = cw_ref[0:1, :]
        mine[5:6, hd:d] = cw_ref[1:2, :]
        mine[6:7, 0:hd] = cw_ref[2:3, :]
        mine[6:7, hd:hd + nq] = sk_ref[...]
        mine[6:7, hd + 128:hd + 256] = ls_ref[...]
        gath[pl.ds(me, 1)] = mine[...][None]
        copies = []
        for r in range(1, 8):
            peer = ((1 - x) if r & 4 else x, (1 - y) if r & 2 else y, (1 - c) if r & 1 else c)
            cp = pltpu.make_async_remote_copy(
                src_ref=mine, dst_ref=gath.at[me], send_sem=send_sems.at[r - 1], recv_sem=recv_sems.at[r - 1],
                device_id=peer, device_id_type=MESH)
            cp.start()
            copies.append(cp)
        for r in range(1, 8):
            peer = ((1 - x) if r & 4 else x, (1 - y) if r & 2 else y, (1 - c) if r & 1 else c)
            peer_id = 4 * peer[0] + 2 * peer[1] + peer[2]
            pltpu.make_async_remote_copy(
                src_ref=mine, dst_ref=gath.at[peer_id], send_sem=send_sems.at[r - 1], recv_sem=recv_sems.at[r - 1],
                device_id=peer, device_id_type=MESH).wait_recv()
        for cp in copies:
            cp.wait_send()
        total = gath[0]
        for dev in range(1, 8):
            total = total + gath[dev]
        out_ref[...] = total

    return pl.pallas_call(
        body, name="allreduce_small",
        in_specs=[_VMEM] * 8 + [_ANY], out_specs=_VMEM,
        out_shape=jax.ShapeDtypeStruct((SMALL_ROWS, d), F32),
        scratch_shapes=[pltpu.VMEM((SMALL_ROWS, d), F32), pltpu.VMEM((8, SMALL_ROWS, d), F32),
                        pltpu.SemaphoreType.DMA((7,)), pltpu.SemaphoreType.DMA((7,))],
    )(gl2g, gl2b, gl1g, gl1b, g_ac, gcw, gsink, loss, after)


def _adamw(w, g, m, v):
    m = ADAM_B1 * m + (1.0 - ADAM_B1) * g
    v = ADAM_B2 * v + (1.0 - ADAM_B2) * (g * g)
    m_hat = m / (1.0 - ADAM_B1 ** ADAM_STEP)
    v_hat = v / (1.0 - ADAM_B2 ** ADAM_STEP)
    delta = -ADAM_LR * (m_hat / (jnp.sqrt(v_hat) + ADAM_EPS) + ADAM_WD * w)
    return delta, m, v


def _adamw_shard(w, m, v, land, own, pos_vec, name, col_block=0):
    _, r, c = w.shape
    hr = r // 2
    tr = _row_tile(hr, TR_ELT)
    nh = hr // tr

    def body(pos_ref, w_ref, m_ref, v_ref, l0, l1, l2, l3, own_ref, g_out, d_out, m_out, v_out):
        i = pl.program_id(0)
        mine = (i // nh) == pos_ref[1]
        own_blk = own_ref[...].astype(F32)
        g = None
        for s, l_ref in enumerate([l0, l1, l2, l3]):
            term = jnp.where(mine & (pos_ref[0] == s), own_blk, l_ref[...].astype(F32))
            g = term if g is None else g + term
        delta, nm, nv = _adamw(w_ref[...], g, m_ref[...], v_ref[...])
        g_out[...] = g
        d_out[...] = delta
        m_out[...] = nm
        v_out[...] = nv

    def land_spec(s):
        def index(i, pos_ref):
            skip = (pos_ref[0] == s) & ((i // nh) == pos_ref[1])
            return (s, jnp.where(skip, (i + nh) % (2 * nh), i), col_block)
        return pl.BlockSpec((None, tr, c), index)

    blk = pl.BlockSpec((None, tr, c), lambda i, pos_ref: (0, i, 0))
    grid_spec = pltpu.PrefetchScalarGridSpec(
        num_scalar_prefetch=1, grid=(2 * nh,),
        in_specs=[blk, blk, blk] + [land_spec(s) for s in range(N_CHIPS)]
        + [pl.BlockSpec((None, tr, c), lambda i, pos_ref: (pos_ref[0], i % nh, col_block))],
        out_specs=[blk] * 4)
    return pl.pallas_call(
        body, name=name, grid_spec=grid_spec,
        out_shape=[jax.ShapeDtypeStruct((1, r, c), F32)] * 4,
        compiler_params=_params(("parallel",)),
    )(pos_vec, w, m, v, land, land, land, land, own)


def _adamw_small(red, params):
    names = ["sinks", "g_attn", "g_conv", "ln1_g", "ln1_b", "ln2_g", "ln2_b", "conv_w"]
    d = red.shape[1]
    hd = d // 2
    flat = []
    for nme in names:
        flat.extend(params[nme])
    nq = params["sinks"][0].shape[1]
    cs = params["conv_w"][0].shape[2]

    def body(*refs):
        red_ref = refs[0]
        ins = refs[1:1 + 3 * len(names)]
        outs = refs[1 + 3 * len(names):]
        x, y, _, _ = _mesh_pos()
        me = _chip_id(x, y)

        def conv_tap(row, base):
            picked = red_ref[row:row + 1, base:base + cs]
            for s in range(1, N_CHIPS):
                picked = jnp.where(me == s, red_ref[row:row + 1, base + s * cs:base + (s + 1) * cs], picked)
            return picked

        grads = {
            "sinks": red_ref[6:7, hd:hd + nq],
            "g_attn": red_ref[4:5, 0:hd],
            "g_conv": red_ref[4:5, hd:d],
            "ln1_g": red_ref[2:3, :],
            "ln1_b": red_ref[3:4, :],
            "ln2_g": red_ref[0:1, :],
            "ln2_b": red_ref[1:2, :],
        }
        for i, nme in enumerate(names):
            w_ref, m_ref, v_ref = ins[3 * i:3 * i + 3]
            g_out, d_out, m_out, v_out = outs[4 * i:4 * i + 4]
            if nme == "conv_w":
                for tap, (row, base) in enumerate([(5, 0), (5, hd), (6, 0)]):
                    g = conv_tap(row, base)
                    delta, nm, nv = _adamw(w_ref[0, tap:tap + 1, :], g, m_ref[0, tap:tap + 1, :], v_ref[0, tap:tap + 1, :])
                    g_out[0, tap:tap + 1, :] = g
                    d_out[0, tap:tap + 1, :] = delta
                    m_out[0, tap:tap + 1, :] = nm
                    v_out[0, tap:tap + 1, :] = nv
            else:
                g = grads[nme]
                delta, nm, nv = _adamw(w_ref[...], g, m_ref[...], v_ref[...])
                g_out[...] = g
                d_out[...] = delta
                m_out[...] = nm
                v_out[...] = nv

    out_shape = []
    for nme in names:
        out_shape.extend([jax.ShapeDtypeStruct(params[nme][0].shape, F32)] * 4)
    outs = pl.pallas_call(
        body, name="adamw_small",
        in_specs=[_VMEM] * (1 + len(flat)), out_specs=[_VMEM] * len(out_shape),
        out_shape=out_shape,
    )(red, *flat)
    return {nme: tuple(outs[4 * i:4 * i + 4]) for i, nme in enumerate(names)}


def _rope_tables(pos_col):
    s = pos_col.shape[0]
    w = N_KV_HEADS * HEAD_DIM
    tb = min(512, s)
    inv_freq = (ROPE_THETA ** (-np.arange(0, ROT_DIM, 2, dtype=np.float32) / ROT_DIM)).astype(np.float32)

    def body(pos_ref, cos_ref, sin_ref):
        pos = pos_ref[...].astype(F32)
        lane = lax.broadcasted_iota(jnp.int32, (tb, PAIR), 1) & (HEAD_DIM - 1)
        fidx = lane & (ROT_DIM // 2 - 1)
        inv = jnp.zeros((tb, PAIR), F32)
        for k in range(ROT_DIM // 2):
            inv = jnp.where(fidx == k, float(inv_freq[k]), inv)
        ang = pos * inv
        rot = lane < ROT_DIM
        sin_v = jnp.sin(ang)
        cos_ref[...] = _tile_lanes(jnp.where(rot, jnp.cos(ang), 1.0), w // PAIR)
        sin_ref[...] = _tile_lanes(jnp.where(lane < ROT_DIM // 2, -sin_v, jnp.where(rot, sin_v, 0.0)), w // PAIR)

    return pl.pallas_call(
        body, name="rope_tables", grid=(s // tb,),
        in_specs=[pl.BlockSpec((tb, 1), lambda i: (i, 0))],
        out_specs=[pl.BlockSpec((tb, w), lambda i: (i, 0))] * 2,
        out_shape=[jax.ShapeDtypeStruct((s, w), F32)] * 2,
        compiler_params=_params(("parallel",)),
    )(pos_col)


def _in_proj(x, w_in_g, first_vec, n_shards, into, name):
    _, s, d = x.shape
    ns, _, ncol = w_in_g.shape
    tm = min(2 * TM, s)

    def body(first_ref, x_ref, w_ref, into_ref, o_ref):
        o_ref[...] = _dot(x_ref[...].astype(BF16), w_ref[...]).astype(BF16)

    shard = lambda j, first_ref: lax.rem(first_ref[0] + j, ns)
    grid_spec = pltpu.PrefetchScalarGridSpec(
        num_scalar_prefetch=1, grid=(s // tm, n_shards),
        in_specs=[pl.BlockSpec((None, tm, d), lambda i, j, first_ref: (0, i, 0)),
                  pl.BlockSpec((None, d, ncol), lambda i, j, first_ref: (shard(j, first_ref), 0, 0)), _ANY],
        out_specs=pl.BlockSpec((tm, ncol), lambda i, j, first_ref: (i, shard(j, first_ref))))
    return pl.pallas_call(
        body, name=name, grid_spec=grid_spec,
        out_shape=jax.ShapeDtypeStruct((s, ns * ncol), BF16),
        input_output_aliases={} if into is None else {3: 0},
        compiler_params=_params(("parallel", "arbitrary")),
    )(first_vec, x, w_in_g, first_vec if into is None else into)


PAIR = 2 * HEAD_DIM
KEYS = 2 * WINDOW


def _pair_operand(t_all, h):
    col = (h // 2) * PAIR
    lane = lax.broadcasted_iota(jnp.int32, (KEYS, PAIR), 1)
    own_low = h % 2 == 0
    mine = jnp.where((lane < HEAD_DIM) if own_low else (lane >= HEAD_DIM), t_all[:, col:col + PAIR], 0.0)
    other = pltpu.roll(mine, HEAD_DIM, 1)
    low, high = (mine, other) if own_low else (other, mine)
    return jnp.concatenate([low, high], axis=0).astype(BF16)


def _pair_grad(acc, h):
    lane = lax.broadcasted_iota(jnp.int32, (KEYS, PAIR), 1)
    low = jnp.where(lane < HEAD_DIM, acc[:KEYS], 0.0)
    high = jnp.where(lane >= HEAD_DIM, acc[KEYS:], 0.0)
    if h % 2 == 0:
        return low + pltpu.roll(high, HEAD_DIM, 1)
    return high + pltpu.roll(low, HEAD_DIM, 1)


N_PAIRS = N_KV_HEADS * GROUP // 2


def _all_probs(q, kk2s, first, sinks_ref):
    assert ATTN_SCALE == 0.125
    q = q * ATTN_SCALE
    qps, scores = [], []
    for pair in range(N_PAIRS):
        qp = q[:, pair * PAIR:(pair + 1) * PAIR].astype(BF16)
        qps.append(qp)
        scores.append(_dot_nt(qp, kk2s[pair // (GROUP // 2)]))
    qi = lax.broadcasted_iota(jnp.int32, (WINDOW, 2 * KEYS), 0)
    kj = lax.broadcasted_iota(jnp.int32, (WINDOW, 2 * KEYS), 1) & (KEYS - 1)
    rel = qi + WINDOW - kj
    valid = (rel >= 0) & (rel < WINDOW) & jnp.logical_not(first & (kj < WINDOW))
    bias = jnp.where(valid, 0.0, NEG_BIG)
    s = (jnp.stack(scores, axis=0) + bias[None]).reshape(N_PAIRS * WINDOW, 2 * KEYS)
    probs, p_sinks = [], []
    for t in range(2):
        st = s[:, t * KEYS:(t + 1) * KEYS]
        sink = jnp.concatenate([jnp.broadcast_to(sinks_ref[0:1, 2 * pair + t:2 * pair + t + 1], (WINDOW, 1))
                                for pair in range(N_PAIRS)], axis=0)
        m = jnp.maximum(jnp.max(st, axis=1, keepdims=True), sink)
        e = jnp.exp(st - m)
        e_sink = jnp.exp(sink - m)
        inv_l = 1.0 / (jnp.sum(e, axis=1, keepdims=True) + e_sink)
        probs.append(e * inv_l)
        p_sinks.append(e_sink * inv_l)
    return qps, jnp.concatenate(probs, axis=1), p_sinks


def _roped_qkv(cur_ref, prev_ref, cos_ref, sin_ref, cosp_ref, sinp_ref, qw, kvw):
    cur = cur_ref[...].astype(F32)
    cos, sin = cos_ref[...], sin_ref[...]
    cos_q, sin_q = _tile_lanes(cos, GROUP), _tile_lanes(sin, GROUP)
    q = _rope(cur[:, :qw], cos_q, sin_q, 1.0)
    prev = prev_ref[...].astype(F32)
    k_all = jnp.concatenate([_rope(prev[:, :kvw], cosp_ref[...], sinp_ref[...], 1.0),
                             _rope(cur[:, qw:qw + kvw], cos, sin, 1.0)], axis=0)
    v_all = jnp.concatenate([prev[:, kvw:], cur[:, qw + kvw:]], axis=0)
    return q, k_all, v_all, cos_q, sin_q


def _attention_fwd(proj, cos_t, sin_t, sinks):
    s = proj.shape[0]
    qw = GROUP * N_KV_HEADS * HEAD_DIM
    kvw = N_KV_HEADS * HEAD_DIM
    nb = s // WINDOW

    def body(cur_ref, prev_ref, cos_ref, sin_ref, cosp_ref, sinp_ref, sinks_ref, o_ref):
        first = pl.program_id(0) == 0
        q, k_all, v_all, _, _ = _roped_qkv(cur_ref, prev_ref, cos_ref, sin_ref, cosp_ref, sinp_ref, qw, kvw)
        kk2s = [_pair_operand(k_all, h) for h in range(N_KV_HEADS)]
        vv2s = [_pair_operand(v_all, h) for h in range(N_KV_HEADS)]
        _, probs, _ = _all_probs(q, kk2s, first, sinks_ref)
        probs = probs.astype(BF16)
        outs = [_dot(probs[pair * WINDOW:(pair + 1) * WINDOW], vv2s[pair // (GROUP // 2)]) for pair in range(N_PAIRS)]
        o_ref[...] = jnp.concatenate(outs, axis=1)

    tbl = pl.BlockSpec((WINDOW, kvw), lambda n: (n, 0))
    tbl_prev = pl.BlockSpec((WINDOW, kvw), lambda n: (jnp.maximum(n - 1, 0), 0))
    return pl.pallas_call(
        body, name="attention_fwd", grid=(nb,),
        in_specs=[pl.BlockSpec((WINDOW, qw + 2 * kvw), lambda n: (n, 0)),
                  pl.BlockSpec((WINDOW, 2 * kvw), lambda n: (jnp.maximum(n - 1, 0), (qw // (2 * kvw)))),
                  tbl, tbl, tbl_prev, tbl_prev, _VMEM],
        out_specs=pl.BlockSpec((WINDOW, qw), lambda n: (n, 0)),
        out_shape=jax.ShapeDtypeStruct((s, qw), F32),
        compiler_params=_params(("parallel",)),
    )(proj, proj, cos_t, sin_t, cos_t, sin_t, sinks)


def _conv_taps(cw_ref):
    return [jnp.concatenate([cw_ref[s, k:k + 1, :] for s in range(N_CHIPS)], axis=1) for k in range(3)]


def _shift_down(z, halo, steps):
    last = halo.shape[0]
    row = lax.broadcasted_iota(jnp.int32, z.shape, 0)
    out = pltpu.roll(z, steps, 0)
    for r in range(steps):
        out = jnp.where(row == r, halo[last - steps + r:last - steps + r + 1, :], out)
    return out


def _shift_up(z, halo, steps):
    rows = z.shape[0]
    row = lax.broadcasted_iota(jnp.int32, z.shape, 0)
    out = pltpu.roll(z, rows - steps, 0)
    for r in range(steps):
        out = jnp.where(row == rows - steps + r, halo[r:r + 1, :], out)
    return out


def _split_cbu(lo, hi, cw):
    lo, hi = lo.astype(F32), hi.astype(F32)
    c_gate = lo[:, :cw]
    b_gate = jnp.concatenate([lo[:, cw:], hi[:, :2 * cw - lo.shape[1]]], axis=1)
    u = hi[:, 2 * cw - lo.shape[1]:]
    return c_gate, b_gate, u


def _conv_norm(proj, attn, cw_full, g_ac):
    s, in_w = proj.shape
    cw = attn.shape[1]
    blk_w = in_w // 3
    tb = min(TB_CONV, s)

    def body(lo_ref, hi_ref, lo_h_ref, hi_h_ref, attn_ref, cw_ref, g_ref, mixed_ref, ac_ref, rstd_ref):
        i = pl.program_id(0)
        c_gate, b_gate, u = _split_cbu(lo_ref[...], hi_ref[...], cw)
        c_h, _, u_h = _split_cbu(lo_h_ref[...], hi_h_ref[...], cw)
        z = c_gate * u
        z_h = jnp.where(i == 0, 0.0, c_h * u_h)
        w0, w1, w2 = _conv_taps(cw_ref)
        y = w0 * _shift_down(z, z_h, 2) + w1 * _shift_down(z, z_h, 1) + w2 * z
        conv = b_gate * y
        a = attn_ref[...]
        r_a = lax.rsqrt(jnp.mean(a * a, axis=-1, keepdims=True) + RMS_EPS)
        r_c = lax.rsqrt(jnp.mean(conv * conv, axis=-1, keepdims=True) + RMS_EPS)
        g = g_ref[...]
        mixed_ref[...] = jnp.concatenate([a * r_a * g[:, :cw], conv * r_c * g[:, cw:]], axis=1).astype(BF16)
        ac_ref[...] = jnp.concatenate([a, conv], axis=1)
        rstd_ref[0] = r_a
        rstd_ref[1] = r_c

    halo_idx = lambda i: jnp.maximum(i * (tb // HALO_ROWS) - 1, 0)
    return pl.pallas_call(
        body, name="conv_norm", grid=(s // tb,),
        in_specs=[pl.BlockSpec((tb, blk_w), lambda i: (i, 1)),
                  pl.BlockSpec((tb, blk_w), lambda i: (i, 2)),
                  pl.BlockSpec((HALO_ROWS, blk_w), lambda i: (halo_idx(i), 1)),
                  pl.BlockSpec((HALO_ROWS, blk_w), lambda i: (halo_idx(i), 2)),
                  pl.BlockSpec((tb, cw), lambda i: (i, 0)),
                  _VMEM, _VMEM],
        out_specs=[pl.BlockSpec((tb, 2 * cw), lambda i: (i, 0)),
                   pl.BlockSpec((tb, 2 * cw), lambda i: (i, 0)),
                   pl.BlockSpec((2, tb, 1), lambda i: (0, i, 0))],
        out_shape=[jax.ShapeDtypeStruct((s, 2 * cw), BF16), jax.ShapeDtypeStruct((s, 2 * cw), F32),
                   jax.ShapeDtypeStruct((2, s, 1), F32)],
        compiler_params=_params(("parallel",)),
    )(proj, proj, proj, proj, attn, cw_full, g_ac)


def _out_proj_ln(mixed, w_out_g, x, ln_g, ln_b):
    s, d = mixed.shape
    tm = min(TM, s)
    tk = d
    nk = d // tk

    def body(a_ref, w_ref, x_ref, g_ref, b_ref, xhat_ref, h_ref, rstd_ref, acc):
        k = pl.program_id(1)
        _accumulate(acc, lambda: _dot(a_ref[...], w_ref[...]), k, nk)

        @pl.when(k == nk - 1)
        def _():
            def rows_fn(rows):
                xhat, rstd = _ln_fwd(ALPHA * x_ref[rows, :] + acc[rows, :])
                xhat_ref[rows, :] = xhat
                h_ref[rows, :] = (xhat * g_ref[...] + b_ref[...]).astype(BF16)
                rstd_ref[rows, :] = rstd

            _for_row_chunks(tm, rows_fn)

    row = pl.BlockSpec((tm, d), lambda i, k: (i, 0))
    return pl.pallas_call(
        body, name="out_proj_ln", grid=(s // tm, nk),
        in_specs=[pl.BlockSpec((tm, tk), lambda i, k: (i, k)),
                  pl.BlockSpec((tk, d), lambda i, k: (k, 0)),
                  pl.BlockSpec((None, tm, d), lambda i, k: (0, i, 0)),
                  _VMEM, _VMEM],
        out_specs=[row, row, pl.BlockSpec((tm, 1), lambda i, k: (i, 0))],
        out_shape=[jax.ShapeDtypeStruct((s, d), F32), jax.ShapeDtypeStruct((s, d), BF16),
                   jax.ShapeDtypeStruct((s, 1), F32)],
        scratch_shapes=[pltpu.VMEM((tm, d), F32)],
        compiler_params=_params(("parallel", "arbitrary")),
    )(mixed, w_out_g, x, ln_g, ln_b)


def _gate_up(h1, w_gu_g):
    s, d = h1.shape
    ns, _, fs2 = w_gu_g.shape
    fs = fs2 // 2
    tm = min(TM, s)

    def body(h_ref, w_ref, act_ref, ab_ref):
        gu = _dot(h_ref[...], w_ref[...])
        g, u = gu[:, :fs], gu[:, fs:]
        sg = _sigmoid(g)
        silu = g * sg
        act_ref[...] = (silu * u).astype(BF16)
        ab_ref[:, :fs] = (u * (sg * (1.0 + g * (1.0 - sg)))).astype(BF16)
        ab_ref[:, fs:] = silu.astype(BF16)

    return pl.pallas_call(
        body, name="gate_up", grid=(s // tm, ns),
        in_specs=[pl.BlockSpec((tm, d), lambda i, j: (i, 0)), pl.BlockSpec((None, d, fs2), lambda i, j: (j, 0, 0))],
        out_specs=[pl.BlockSpec((tm, fs), lambda i, j: (i, j)), pl.BlockSpec((tm, fs2), lambda i, j: (i, j))],
        out_shape=[jax.ShapeDtypeStruct((s, ns * fs), BF16), jax.ShapeDtypeStruct((s, ns * fs2), BF16)],
        compiler_params=_params(("parallel", "arbitrary")),
    )(h1, w_gu_g)


def _down_ln_loss(act, w_down_g, xhat1, ln1_g, ln1_b, ln2_g, ln2_b, target):
    s, f = act.shape
    d = xhat1.shape[1]
    tm = min(TM, s)
    tk = f // N_CHIPS
    nk = f // tk

    def body(a_ref, w_ref, xh_ref, g1_ref, b1_ref, g2_ref, b2_ref, t_ref, dpre_ref, dpre16_ref, loss_ref, gg_ref, gb_ref,
             acc):
        i, k = pl.program_id(0), pl.program_id(1)
        _accumulate(acc, lambda: _dot(a_ref[...], w_ref[...]), k, nk)

        @pl.when(k == nk - 1)
        def _():
            @pl.when(i == 0)
            def _():
                loss_ref[...] = jnp.zeros_like(loss_ref)
                gg_ref[...] = jnp.zeros_like(gg_ref)
                gb_ref[...] = jnp.zeros_like(gb_ref)

            def rows_fn(rows):
                h1 = xh_ref[rows, :] * g1_ref[...] + b1_ref[...]
                xhat, rstd = _ln_fwd(ALPHA * h1 + acc[rows, :])
                g2 = g2_ref[...]
                diff = xhat * g2 + b2_ref[...] - t_ref[rows, :]
                dy = diff * (1.0 / d)
                dpre = _ln_bwd(dy, xhat, rstd, g2)
                dpre_ref[rows, :] = dpre
                dpre16_ref[rows, :] = dpre.astype(BF16)
                sq = jnp.sum(jnp.sum(diff * diff, axis=1, keepdims=True), axis=0, keepdims=True)
                loss_ref[...] += jnp.broadcast_to(sq * (0.5 / d), (1, 128))
                gg_ref[...] += jnp.sum(dy * xhat, axis=0, keepdims=True)
                gb_ref[...] += jnp.sum(dy, axis=0, keepdims=True)

            _for_row_chunks(tm, rows_fn)

    row = pl.BlockSpec((tm, d), lambda i, k: (i, 0))
    vec = pl.BlockSpec((1, d), lambda i, k: (0, 0))
    return pl.pallas_call(
        body, name="down_ln_loss", grid=(s // tm, nk),
        in_specs=[pl.BlockSpec((tm, tk), lambda i, k: (i, k)),
                  pl.BlockSpec((tk, d), lambda i, k: (k, 0)),
                  row, _VMEM, _VMEM, _VMEM, _VMEM,
                  pl.BlockSpec((None, tm, d), lambda i, k: (0, i, 0))],
        out_specs=[row, row, pl.BlockSpec((1, 128), lambda i, k: (0, 0)), vec, vec],
        out_shape=[jax.ShapeDtypeStruct((s, d), F32), jax.ShapeDtypeStruct((s, d), BF16),
                   jax.ShapeDtypeStruct((1, 128), F32), jax.ShapeDtypeStruct((1, d), F32),
                   jax.ShapeDtypeStruct((1, d), F32)],
        scratch_shapes=[pltpu.VMEM((tm, d), F32)],
        compiler_params=_params(("arbitrary", "arbitrary")),
    )(act, w_down_g, xhat1, ln1_g, ln1_b, ln2_g, ln2_b, target)


def _dact_silu_bwd(dpre2, w_down_g, ab):
    s, d = dpre2.shape
    fs2 = ab.shape[1] // N_CHIPS
    fs = fs2 // 2
    tm = min(TM, s)

    def body(dp_ref, w_ref, ab_ref, dgu_ref):
        d_act = _dot_nt(dp_ref[...], w_ref[...])
        dgu_ref[:, :fs] = (d_act * ab_ref[:, :fs].astype(F32)).astype(BF16)
        dgu_ref[:, fs:] = (d_act * ab_ref[:, fs:].astype(F32)).astype(BF16)

    blk = pl.BlockSpec((tm, fs2), lambda j, i: (i, j))
    return pl.pallas_call(
        body, name="dact_silu_bwd", grid=(N_CHIPS, s // tm),
        in_specs=[pl.BlockSpec((tm, d), lambda j, i: (i, 0)),
                  pl.BlockSpec((fs, d), lambda j, i: (j, 0)), blk],
        out_specs=blk,
        out_shape=jax.ShapeDtypeStruct(ab.shape, BF16),
        compiler_params=_params(("parallel", "parallel")),
    )(dpre2, w_down_g, ab)


def _grad_rows(a, b, after, name, row_blocks=1):
    s, m = a.shape
    n = b.shape[1]
    ms = m // N_CHIPS
    tmw = ms // row_blocks
    tk = min(TK_TOK, s)
    nk = s // tk

    def body(a_ref, b_ref, after_ref, o_ref, acc):
        k = pl.program_id(2)
        _accumulate(acc, lambda: _dot_tn(a_ref[...].astype(BF16), b_ref[...].astype(BF16)), k, nk)

        @pl.when(k == nk - 1)
        def _():
            o_ref[...] = acc[...].astype(BF16)

    return pl.pallas_call(
        body, name=name, grid=(N_CHIPS, row_blocks, nk),
        in_specs=[pl.BlockSpec((tk, tmw), lambda j, r, k: (k, j * row_blocks + r)),
                  pl.BlockSpec((tk, n), lambda j, r, k: (k, 0)), _ANY],
        out_specs=pl.BlockSpec((None, tmw, n), lambda j, r, k: (j, r, 0)),
        out_shape=jax.ShapeDtypeStruct((N_CHIPS, ms, n), BF16),
        scratch_shapes=[pltpu.VMEM((tmw, n), F32)],
        compiler_params=_params(("parallel", "parallel", "arbitrary")),
    )(a, b, after)


def _grad_cols(a, bs, after, name, a_3d=False, row_blocks=2):
    s, m = a.shape[-2:]
    n = bs[0].shape[1]
    ns = n // N_CHIPS
    nb = len(bs)
    tmw = m // row_blocks
    tk = min(TK_TOK, s)
    nk = s // tk

    def body(*refs):
        a_ref, b_refs, o_refs, accs = refs[0], refs[1:1 + nb], refs[2 + nb:2 + 2 * nb], refs[2 + 2 * nb:]
        k = pl.program_id(2)
        for b_ref, acc in zip(b_refs, accs):
            _accumulate(acc, lambda b_ref=b_ref: _dot_tn(a_ref[...].astype(BF16), b_ref[...].astype(BF16)), k, nk)

        @pl.when(k == nk - 1)
        def _():
            for o_ref, acc in zip(o_refs, accs):
                o_ref[...] = acc[...].astype(BF16)

    if a_3d:
        a_spec = pl.BlockSpec((None, tk, tmw), lambda j, r, k: (0, k, r))
    else:
        a_spec = pl.BlockSpec((tk, tmw), lambda j, r, k: (k, r))
    return pl.pallas_call(
        body, name=name, grid=(N_CHIPS, row_blocks, nk),
        in_specs=[a_spec] + [pl.BlockSpec((tk, ns), lambda j, r, k: (k, j))] * nb + [_ANY],
        out_specs=[pl.BlockSpec((None, tmw, ns), lambda j, r, k: (j, r, 0))] * nb,
        out_shape=[jax.ShapeDtypeStruct((N_CHIPS, m, ns), BF16)] * nb,
        scratch_shapes=[pltpu.VMEM((tmw, ns), F32)] * nb,
        compiler_params=_params(("parallel", "parallel", "arbitrary")),
    )(a, *bs, after)


def _dh1_ln_bwd(d_gu, w_gu_g, dpre2, xhat1, rstd1, ln1_g, after):
    s = d_gu.shape[0]
    d = dpre2.shape[1]
    hd = d // 2
    fs = w_gu_g.shape[2]
    tm = min(TM, s)

    def body(dgu_ref, w_ref, dp2_ref, xh_ref, rs_ref, g_ref, after_ref, dpre_ref, gg_ref, gb_ref, acc_lo, acc_hi):
        i, j, half = pl.program_id(0), pl.program_id(1), pl.program_id(2)

        def product():
            return _dot_nt(dgu_ref[...], w_ref[...])

        @pl.when(half == 0)
        def _():
            _accumulate(acc_lo, product, j, N_CHIPS)

        @pl.when(half == 1)
        def _():
            _accumulate(acc_hi, product, j, N_CHIPS)

        @pl.when((j == N_CHIPS - 1) & (half == 1))
        def _():
            @pl.when(i == 0)
            def _():
                gg_ref[...] = jnp.zeros_like(gg_ref)
                gb_ref[...] = jnp.zeros_like(gb_ref)

            def rows_fn(rows):
                dh = jnp.concatenate([acc_lo[rows, :], acc_hi[rows, :]], axis=1) + ALPHA * dp2_ref[rows, :]
                xhat = xh_ref[rows, :]
                dpre_ref[rows, :] = _ln_bwd(dh, xhat, rs_ref[rows, :], g_ref[...])
                gg_ref[...] += jnp.sum(dh * xhat, axis=0, keepdims=True)
                gb_ref[...] += jnp.sum(dh, axis=0, keepdims=True)

            _for_row_chunks(tm, rows_fn)

    row = pl.BlockSpec((tm, d), lambda i, j, h: (i, 0))
    vec = pl.BlockSpec((1, d), lambda i, j, h: (0, 0))
    act_blk = pl.BlockSpec((tm, fs), lambda i, j, h: (i, j))
    w_blk = pl.BlockSpec((None, hd, fs), lambda i, j, h: (j, h, 0))
    return pl.pallas_call(
        body, name="dh1_ln_bwd", grid=(s // tm, N_CHIPS, 2),
        in_specs=[act_blk, w_blk, row, row, pl.BlockSpec((tm, 1), lambda i, j, h: (i, 0)), _VMEM, _ANY],
        out_specs=[row, vec, vec],
        out_shape=[jax.ShapeDtypeStruct((s, d), F32), jax.ShapeDtypeStruct((1, d), F32),
                   jax.ShapeDtypeStruct((1, d), F32)],
        scratch_shapes=[pltpu.VMEM((tm, hd), F32)] * 2,
        compiler_params=_params(("arbitrary", "arbitrary", "arbitrary")),
    )(d_gu, w_gu_g, dpre2, xhat1, rstd1, ln1_g, after)


def _dmixed_rms_bwd(dpre1, w_out_g, ac, rstd, g_ac):
    s, d = dpre1.shape
    hd = d // 2
    tm = min(TM, s)

    def body(dp_ref, w_ref, ac_ref, rs_ref, g_ref, dac_ref, gg_ref):
        i = pl.program_id(1)
        dm = _dot_nt(dp_ref[...].astype(BF16), w_ref[...])
        pre = ac_ref[...]
        r = rs_ref[...]
        gdm = dm * g_ref[...]
        dac_ref[...] = r * gdm - pre * (r * r * r) * jnp.mean(gdm * pre, axis=-1, keepdims=True)
        gg = jnp.sum(dm * pre * r, axis=0, keepdims=True)

        @pl.when(i == 0)
        def _():
            gg_ref[...] = gg

        @pl.when(i > 0)
        def _():
            gg_ref[...] += gg

    return pl.pallas_call(
        body, name="dmixed_rms_bwd", grid=(2, s // tm),
        in_specs=[pl.BlockSpec((tm, d), lambda h, i: (i, 0)),
                  pl.BlockSpec((hd, d), lambda h, i: (h, 0)),
                  pl.BlockSpec((tm, hd), lambda h, i: (i, h)),
                  pl.BlockSpec((None, tm, 1), lambda h, i: (h, i, 0)),
                  pl.BlockSpec((1, hd), lambda h, i: (0, h))],
        out_specs=[pl.BlockSpec((tm, hd), lambda h, i: (i, h)),
                   pl.BlockSpec((1, hd), lambda h, i: (0, h))],
        out_shape=[jax.ShapeDtypeStruct((s, d), F32), jax.ShapeDtypeStruct((1, d), F32)],
        compiler_params=_params(("arbitrary", "arbitrary")),
    )(dpre1, w_out_g, ac, rstd, g_ac)


def _attention_bwd(proj, d_ac, cos_t, sin_t, sinks, after):
    s = proj.shape[0]
    qw = GROUP * N_KV_HEADS * HEAD_DIM
    kvw = N_KV_HEADS * HEAD_DIM
    nb = s // WINDOW
    nq = GROUP * N_KV_HEADS

    def body(cur_ref, prev_ref, do_ref, cos_ref, sin_ref, cosp_ref, sinp_ref, sinks_ref, after_ref,
             dq_ref, dcur_ref, dprev_ref, dsink_ref):
        n = pl.program_id(0)
        first = n == 0
        q, k_all, v_all, cos_q, sin_q = _roped_qkv(cur_ref, prev_ref, cos_ref, sin_ref, cosp_ref, sinp_ref, qw, kvw)
        kk2s = [_pair_operand(k_all, h) for h in range(N_KV_HEADS)]
        vv2s = [_pair_operand(v_all, h) for h in range(N_KV_HEADS)]
        qps, probs, p_sinks = _all_probs(q, kk2s, first, sinks_ref)
        dops = [do_ref[:, pair * PAIR:(pair + 1) * PAIR].astype(BF16) for pair in range(N_PAIRS)]
        d_probs = jnp.concatenate([_dot_nt(dops[pair], vv2s[pair // (GROUP // 2)]) for pair in range(N_PAIRS)], axis=0)
        d_s, ds_sinks = [], []
        for t in range(2):
            cols = slice(t * KEYS, (t + 1) * KEYS)
            delta = jnp.sum(probs[:, cols] * d_probs[:, cols], axis=1, keepdims=True)
            d_s.append(probs[:, cols] * (d_probs[:, cols] - delta))
            ds_sinks.append(-p_sinks[t] * delta)
        d_s = jnp.concatenate(d_s, axis=1).astype(BF16)
        probs = probs.astype(BF16)
        dq_parts, dk_tiles, dv_tiles, dsink_parts = [], [], [], []
        for h in range(N_KV_HEADS):
            dkk2, dvv2 = None, None
            for p in range(GROUP // 2):
                pair = (GROUP // 2) * h + p
                rows = slice(pair * WINDOW, (pair + 1) * WINDOW)
                dq_parts.append(_dot(d_s[rows], kk2s[h]) * ATTN_SCALE)
                dk_term = _dot_tn(d_s[rows], qps[pair])
                dv_term = _dot_tn(probs[rows], dops[pair])
                dkk2 = dk_term if dkk2 is None else dkk2 + dk_term
                dvv2 = dv_term if dvv2 is None else dvv2 + dv_term
                dsink_parts.extend([jnp.sum(ds_sinks[t][rows], axis=0, keepdims=True) for t in range(2)])
            dk_tiles.append(_pair_grad(dkk2, h))
            dv_tiles.append(_pair_grad(dvv2, h))
        dq_ref[...] = _rope(jnp.concatenate(dq_parts, axis=1), cos_q, sin_q, -1.0)
        dk = jnp.concatenate([dk_tiles[0] + dk_tiles[1], dk_tiles[2] + dk_tiles[3]], axis=1)
        dv = jnp.concatenate([dv_tiles[0] + dv_tiles[1], dv_tiles[2] + dv_tiles[3]], axis=1)
        dprev_ref[...] = jnp.concatenate([dk[:WINDOW], dv[:WINDOW]], axis=1)
        dcur_ref[...] = jnp.concatenate([dk[WINDOW:], dv[WINDOW:]], axis=1)
        dsink = jnp.concatenate(dsink_parts, axis=1)

        @pl.when(first)
        def _():
            dsink_ref[...] = dsink

        @pl.when(n > 0)
        def _():
            dsink_ref[...] += dsink

    tbl = pl.BlockSpec((WINDOW, kvw), lambda n: (n, 0))
    tbl_prev = pl.BlockSpec((WINDOW, kvw), lambda n: (jnp.maximum(n - 1, 0), 0))
    kv_blk = pl.BlockSpec((WINDOW, 2 * kvw), lambda n: (n, 0))
    return pl.pallas_call(
        body, name="attention_bwd", grid=(nb,),
        in_specs=[pl.BlockSpec((WINDOW, qw + 2 * kvw), lambda n: (n, 0)),
                  pl.BlockSpec((WINDOW, 2 * kvw), lambda n: (jnp.maximum(n - 1, 0), (qw // (2 * kvw)))),
                  pl.BlockSpec((WINDOW, qw), lambda n: (n, 0)),
                  tbl, tbl, tbl_prev, tbl_prev, _VMEM, _ANY],
        out_specs=[pl.BlockSpec((WINDOW, qw), lambda n: (n, 0)), kv_blk, kv_blk,
                   pl.BlockSpec((1, nq), lambda n: (0, 0))],
        out_shape=[jax.ShapeDtypeStruct((s, qw), F32), jax.ShapeDtypeStruct((s, 2 * kvw), F32),
                   jax.ShapeDtypeStruct((s, 2 * kvw), F32), jax.ShapeDtypeStruct((1, nq), F32)],
        compiler_params=_params(("arbitrary",)),
    )(proj, proj, d_ac, cos_t, sin_t, cos_t, sin_t, sinks, after)


def _dproj_assemble(proj, d_ac, dq, dkv_cur, dkv_prev, cos_t, sin_t, cw_full):
    s, in_w = proj.shape
    cw = dq.shape[1]
    kvw = N_KV_HEADS * HEAD_DIM
    blk_w = in_w // 3
    tb = WINDOW
    nb = s // tb

    def body(lo_ref, hi_ref, lo_p_ref, hi_p_ref, lo_n_ref, hi_n_ref, dconv_ref, dconv_n_ref,
             dq_ref, dcur_ref, dprev_n_ref, cos_ref, sin_ref, cw_ref, dproj_ref, gcw_ref):
        i = pl.program_id(0)
        last = i == nb - 1
        c_gate, b_gate, u = _split_cbu(lo_ref[...], hi_ref[...], cw)
        c_p, _, u_p = _split_cbu(lo_p_ref[...], hi_p_ref[...], cw)
        _, b_n, _ = _split_cbu(lo_n_ref[...], hi_n_ref[...], cw)
        z = c_gate * u
        z_p = jnp.where(i == 0, 0.0, c_p * u_p)
        z1 = _shift_down(z, z_p, 1)
        z2 = _shift_down(z, z_p, 2)
        w0, w1, w2 = _conv_taps(cw_ref)
        y = w0 * z2 + w1 * z1 + w2 * z
        d_conv = dconv_ref[...]
        d_b = d_conv * y
        d_y = d_conv * b_gate
        d_y_n = jnp.where(last, 0.0, dconv_n_ref[...] * b_n[:dconv_n_ref.shape[0]])
        d_z = w2 * d_y + w1 * _shift_up(d_y, d_y_n, 1) + w0 * _shift_up(d_y, d_y_n, 2)
        d_c = d_z * u
        d_u = d_z * c_gate
        gcw = jnp.concatenate([jnp.sum(d_y * z2, axis=0, keepdims=True), jnp.sum(d_y * z1, axis=0, keepdims=True),
                               jnp.sum(d_y * z, axis=0, keepdims=True)], axis=0)

        @pl.when(i == 0)
        def _():
            gcw_ref[...] = gcw

        @pl.when(i > 0)
        def _():
            gcw_ref[...] += gcw

        dkv = dcur_ref[...] + jnp.where(last, 0.0, dprev_n_ref[...])
        dk = _rope(dkv[:, :kvw], cos_ref[...], sin_ref[...], -1.0)
        dproj_ref[...] = jnp.concatenate([dq_ref[...], dk, dkv[:, kvw:], d_c, d_b, d_u], axis=1).astype(BF16)

    prev_halo = lambda i: jnp.maximum(i * (tb // HALO_ROWS) - 1, 0)
    next_halo = lambda i: jnp.minimum((i + 1) * (tb // HALO_ROWS), s // HALO_ROWS - 1)
    next8 = lambda i: jnp.minimum((i + 1) * (tb // 8), s // 8 - 1)
    nxt = lambda i: jnp.minimum(i + 1, nb - 1)
    return pl.pallas_call(
        body, name="dproj_assemble", grid=(nb,),
        in_specs=[pl.BlockSpec((tb, blk_w), lambda i: (i, 1)),
                  pl.BlockSpec((tb, blk_w), lambda i: (i, 2)),
                  pl.BlockSpec((HALO_ROWS, blk_w), lambda i: (prev_halo(i), 1)),
                  pl.BlockSpec((HALO_ROWS, blk_w), lambda i: (prev_halo(i), 2)),
                  pl.BlockSpec((HALO_ROWS, blk_w), lambda i: (next_halo(i), 1)),
                  pl.BlockSpec((HALO_ROWS, blk_w), lambda i: (next_halo(i), 2)),
                  pl.BlockSpec((tb, cw), lambda i: (i, 1)),
                  pl.BlockSpec((8, cw), lambda i: (next8(i), 1)),
                  pl.BlockSpec((tb, cw), lambda i: (i, 0)),
                  pl.BlockSpec((tb, 2 * kvw), lambda i: (i, 0)),
                  pl.BlockSpec((tb, 2 * kvw), lambda i: (nxt(i), 0)),
                  pl.BlockSpec((tb, kvw), lambda i: (i, 0)),
                  pl.BlockSpec((tb, kvw), lambda i: (i, 0)),
                  _VMEM],
        out_specs=[pl.BlockSpec((tb, in_w), lambda i: (i, 0)),
                   pl.BlockSpec((3, cw), lambda i: (0, 0))],
        out_shape=[jax.ShapeDtypeStruct((s, in_w), BF16), jax.ShapeDtypeStruct((3, cw), F32)],
        compiler_params=_params(("arbitrary",)),
    )(proj, proj, proj, proj, proj, proj, d_ac, d_ac, dq, dkv_cur, dkv_prev, cos_t, sin_t, cw_full)


def _dx(d_proj, w_in_g, dpre1, after):
    s, in_w = d_proj.shape
    ns, d, ncol = w_in_g.shape
    tm = min(TM, s)

    def body(dp_ref, w_ref, r_ref, after_ref, o_ref, acc):
        j = pl.program_id(1)
        _accumulate(acc, lambda: _dot_nt(dp_ref[...], w_ref[...]), j, ns)

        @pl.when(j == ns - 1)
        def _():
            o_ref[...] = acc[...] + ALPHA * r_ref[...]

    return pl.pallas_call(
        body, name="dx", grid=(s // tm, ns),
        in_specs=[pl.BlockSpec((tm, ncol), lambda i, j: (i, j)),
                  pl.BlockSpec((None, d, ncol), lambda i, j: (j, 0, 0)),
                  pl.BlockSpec((tm, d), lambda i, j: (i, 0)), _ANY],
        out_specs=pl.BlockSpec((None, tm, d), lambda i, j: (0, i, 0)),
        out_shape=jax.ShapeDtypeStruct((1, s, d), F32),
        scratch_shapes=[pltpu.VMEM((tm, d), F32)],
        compiler_params=_params(("parallel", "arbitrary")),
    )(d_proj, w_in_g, dpre1, after)


def kernel(x, positions, w_in, conv_w, sinks, g_attn, g_conv, w_out, ln1_g, ln1_b, w_gate, w_up, w_down, ln2_g, ln2_b, loss_target, m_w_in, m_conv_w, m_sinks, m_g_attn, m_g_conv, m_w_out, m_ln1_g, m_ln1_b, m_w_gate, m_w_up, m_w_down, m_ln2_g, m_ln2_b, v_w_in, v_conv_w, v_sinks, v_g_attn, v_g_conv, v_w_out, v_ln1_g, v_ln1_b, v_w_gate, v_w_up, v_w_down, v_ln2_g, v_ln2_b):
    s = x.shape[1]
    d = x.shape[2]

    chip_vec = _chip_id(lax.axis_index("x"), lax.axis_index("y")).astype(jnp.int32).reshape(1)
    wnames = ["w_in", "w_out", "w_gu", "w_down"]
    buf_in = _cast_weight(w_in, chip_vec, chip_vec, "cast_w_in")
    flight_in, token_in = _gather_start([buf_in], chip_vec, "gather_start_w_in")
    cw_buf = lax.dynamic_update_slice(jnp.zeros((N_CHIPS,) + conv_w.shape[1:], F32), conv_w, (chip_vec[0], 0, 0))
    cw_flight = _flight_start("conv_w_start", [cw_buf], _conv_w_plan(), 3, token_in)
    started = cw_flight[2][0]
    buf_gu = _cast_weight(w_gate, chip_vec, started, "cast_w_gate", 0, 2)
    buf_gu = _cast_weight(w_up, chip_vec, buf_gu, "cast_w_up", 1, 2)
    bufs = [_cast_weight(w_out, chip_vec, started, "cast_w_out"), buf_gu,
            _cast_weight(w_down, chip_vec, started, "cast_w_down")]
    flights_rest, token = _gather_start(bufs, token_in, "gather_start_rest")
    flights = flight_in + flights_rest

    def gathered(i, after):
        send_sems, recv_sems, buf = flights[i]
        buf = _gather_wait(send_sems, recv_sems, buf, after, "gather_wait_" + wnames[i])
        return _sibling_fill(buf, "sibling_fill_" + wnames[i])

    g_ac = jnp.concatenate([g_attn, g_conv], axis=1)

    proj_own = _in_proj(x, _after(flights[0][2], token), chip_vec, 1, None, "in_proj_own")
    cos_t, sin_t = _rope_tables(positions.reshape(s, 1) + token[0:1, 0:1].astype(jnp.int32))
    w_in_g = gathered(0, _after(cos_t, proj_own))
    proj = _in_proj(x, w_in_g, chip_vec + 1, N_CHIPS - 1, proj_own, "in_proj_rest")
    send_sems, recv_sems, buf_out = flights[1]
    buf_out = _gather_wait(send_sems, recv_sems, buf_out, proj, "gather_wait_w_out")
    fill_out = _flight_start("fill_start_w_out", [buf_out], _fill_plan(1), 3, chip_vec)
    attn = _attention_fwd(_after(proj, fill_out[2][0]), cos_t, sin_t, sinks)
    (cw_full,) = _flight_wait("conv_w_wait", cw_flight, _conv_w_plan(), attn)
    mixed, ac, rstd_ac = _conv_norm(proj, attn, cw_full, g_ac)
    (w_out_g,) = _flight_wait("fill_wait_w_out", fill_out, _fill_plan(1), mixed)
    w_out_full = w_out_g.reshape(d, d)
    xhat1, h1, rstd1 = _out_proj_ln(mixed, w_out_full, x, ln1_g, ln1_b)
    w_gu_g = gathered(2, h1)
    act, ab = _gate_up(h1, w_gu_g)
    w_down_full = gathered(3, act).reshape(-1, d)
    dpre2, dpre2_16, loss_part, g_ln2_g, g_ln2_b = _down_ln_loss(act, w_down_full, xhat1, ln1_g, ln1_b, ln2_g, ln2_b,
                                                                 loss_target)

    cvec = lax.axis_index("c").astype(jnp.int32).reshape(1)

    def exchange_begin(parts, nme):
        bufs = []
        for part in parts:
            ns, r, cdim = part.shape
            bufs.extend([part, lax.empty((ns, r // 2, cdim), part.dtype)])
        return _flight_start("exchange_start_" + nme, bufs, _exchange_plan(len(parts)), len(parts), cvec)

    def exchange_end(flight, n_parts, after, nme):
        bufs = _flight_wait("exchange_wait_" + nme, flight, _exchange_plan(n_parts), after)
        return [(bufs[2 * w], bufs[2 * w + 1]) for w in range(n_parts)]

    def scatter_begin(part, got, nme):
        return _scatter_start(_add_halves(part, got, cvec, "add_halves_" + nme), "scatter_start_" + nme)

    d_gu = _dact_silu_bwd(dpre2_16, w_down_full, ab)
    p_down = _grad_rows(act, dpre2_16, d_gu, "grad_w_down")
    x_down = exchange_begin([p_down], "w_down")
    (p_gu,) = _grad_cols(h1, [d_gu], x_down[2][0], "grad_w_gate_up")
    ((p_down, got),) = exchange_end(x_down, 1, p_gu, "w_down")
    f_down = scatter_begin(p_down, got, "w_down")
    x_gu = exchange_begin([_after(p_gu, f_down[2])], "w_gu")
    dpre1, g_ln1_g, g_ln1_b = _dh1_ln_bwd(d_gu, w_gu_g, dpre2, xhat1, rstd1, ln1_g, x_gu[2][0])
    ((p_gu, got),) = exchange_end(x_gu, 1, dpre1, "w_gu")
    f_gu = scatter_begin(p_gu, got, "w_gu")
    d_ac, g_g_ac = _dmixed_rms_bwd(_after(dpre1, f_gu[2]), w_out_full, ac, rstd_ac, g_ac)
    p_out = _grad_rows(mixed, dpre1, d_ac, "grad_w_out")
    x_out = exchange_begin([p_out], "w_out")
    dq, dkv_cur, dkv_prev, g_sinks = _attention_bwd(proj, d_ac, cos_t, sin_t, sinks, x_out[2][0])
    ((p_out, got),) = exchange_end(x_out, 1, dq, "w_out")
    f_out = scatter_begin(p_out, got, "w_out")
    d_proj, g_conv_w = _dproj_assemble(proj, _after(d_ac, f_out[2]), dq, dkv_cur, dkv_prev, cos_t, sin_t, cw_full)
    (p_in,) = _grad_cols(x, [d_proj], d_proj, "grad_w_in", a_3d=True)
    x_in = exchange_begin([p_in], "w_in")
    grad_x = _dx(d_proj, w_in_g, dpre1, x_in[2][0])
    red = _allreduce_small(g_ln2_g, g_ln2_b, g_ln1_g, g_ln1_b, g_g_ac, g_conv_w, g_sinks, loss_part, grad_x)
    ((p_in, got),) = exchange_end(x_in, 1, red, "w_in")
    f_in = scatter_begin(p_in, got, "w_in")

    pos_vec = jnp.concatenate([chip_vec, cvec])
    shards = {"w_in": (w_in, m_w_in, v_w_in), "w_out": (w_out, m_w_out, v_w_out), "w_gate": (w_gate, m_w_gate, v_w_gate),
              "w_up": (w_up, m_w_up, v_w_up), "w_down": (w_down, m_w_down, v_w_down)}
    early = [("w_down", ["w_down"]), ("w_gu", ["w_gate", "w_up"]), ("w_out", ["w_out"])]
    after = f_in[2]
    completing = {}
    for (nme, _), f in zip(early, [f_down, f_gu, f_out]):
        sums, land = _scatter_wait(*f, after, "scatter_wait_" + nme)
        completing[nme] = _flight_start("complete_start_" + nme, [sums, land], _complete_plan(1), 4, cvec)
        after = completing[nme][2][1]
    big = {}
    for nme, members in early:
        sums, land = _flight_wait("complete_wait_" + nme, completing[nme], _complete_plan(1), after)
        for col_block, member in enumerate(members):
            big[member] = _adamw_shard(*shards[member], land, sums, pos_vec, "adamw_" + member, col_block)
            after = big[member][0]
    sums, land = _scatter_wait(*f_in, after, "scatter_wait_w_in")
    (land,) = _complete_chip_sums([sums], [land])
    big["w_in"] = _adamw_shard(*shards["w_in"], land, sums, pos_vec, "adamw_w_in")
    small = _adamw_small(red, {
        "sinks": (sinks, m_sinks, v_sinks), "g_attn": (g_attn, m_g_attn, v_g_attn),
        "g_conv": (g_conv, m_g_conv, v_g_conv), "ln1_g": (ln1_g, m_ln1_g, v_ln1_g),
        "ln1_b": (ln1_b, m_ln1_b, v_ln1_b), "ln2_g": (ln2_g, m_ln2_g, v_ln2_g),
        "ln2_b": (ln2_b, m_ln2_b, v_ln2_b), "conv_w": (conv_w, m_conv_w, v_conv_w)})
    res = {**big, **small}
    order = ["w_in", "conv_w", "sinks", "g_attn", "g_conv", "w_out", "ln1_g", "ln1_b", "w_gate", "w_up", "w_down",
             "ln2_g", "ln2_b"]
    loss = red[6, d // 2 + 128]
    return (loss, grad_x, *[res[n][0] for n in order], *[res[n][1] for n in order],
            *[res[n][2] for n in order], *[res[n][3] for n in order])
```

```python
import functools

import numpy as np
import jax
import jax.numpy as jnp
from jax import lax
from jax.experimental import pallas as pl
from jax.experimental.pallas import tpu as pltpu

F32 = jnp.float32
BF16 = jnp.bfloat16
MESH = pl.DeviceIdType.MESH

HEAD_DIM = 64
N_KV_HEADS = 4
GROUP = 4
WINDOW = 128
ROT_DIM = 16
ROPE_THETA = 500000.0
ATTN_SCALE = HEAD_DIM ** -0.5
ALPHA = 2.0 ** 0.25
LN_EPS = 1e-5
RMS_EPS = 1e-6
ADAM_LR = 0.001
ADAM_B1 = 0.9
ADAM_B2 = 0.999
ADAM_EPS = 1e-08
ADAM_WD = 0.01
ADAM_STEP = 10
N_CHIPS = 4
NEG_BIG = -1e30

V7X_VMEM_BYTES = 64 * 1024 * 1024
VMEM_LIMIT = V7X_VMEM_BYTES - 6 * 1024 * 1024

TM = 512
TK_TOK = 1024
TB_CONV = 256
TR_ELT = 256
ROW_CHUNK = 128
HALO_ROWS = 16


def _params(sem):
    return pltpu.CompilerParams(dimension_semantics=sem, vmem_limit_bytes=VMEM_LIMIT)


def _row_tile(rows, target):
    best = None
    for t in range(16, min(rows, target) + 1, 16):
        if rows % t == 0:
            best = t
    assert best is not None, (rows, target)
    return best


def _dot(a, b):
    return jnp.dot(a, b, preferred_element_type=F32)


def _dot_nt(a, b):
    return lax.dot_general(a, b, (((1,), (1,)), ((), ())), preferred_element_type=F32)


def _dot_tn(a, b):
    return lax.dot_general(a, b, (((0,), (0,)), ((), ())), preferred_element_type=F32)


def _mesh_pos():
    x, y, c = lax.axis_index("x"), lax.axis_index("y"), lax.axis_index("c")
    chips = [(1 - x, y), (x, 1 - y), (1 - x, 1 - y)]
    return x, y, c, chips


def _chip_id(px, py):
    return 2 * px + py


def _rope(t, cos, sgn_sin, sign):
    w = t.shape[1]
    lane = lax.broadcasted_iota(jnp.int32, t.shape, 1) & (HEAD_DIM - 1)
    partner = jnp.where(lane < ROT_DIM // 2, pltpu.roll(t, w - ROT_DIM // 2, 1), pltpu.roll(t, ROT_DIM // 2, 1))
    return t * cos + sign * (partner * sgn_sin)


def _tile_lanes(t, n):
    return jnp.concatenate([t] * n, axis=1)


def _sigmoid(g):
    return 1.0 / (1.0 + jnp.exp(-g))


def _for_row_chunks(n_rows, fn):
    def step(r, carry):
        fn(pl.ds(pl.multiple_of(r * ROW_CHUNK, ROW_CHUNK), ROW_CHUNK))
        return carry

    lax.fori_loop(0, n_rows // ROW_CHUNK, step, 0)


def _accumulate(acc, make_val, k, nk):
    if nk == 1:
        acc[...] = make_val()
        return

    @pl.when(k == 0)
    def _():
        acc[...] = jnp.zeros_like(acc)

    acc[...] += make_val()


def _ln_fwd(pre):
    mu = jnp.mean(pre, axis=-1, keepdims=True)
    cen = pre - mu
    var = jnp.mean(cen * cen, axis=-1, keepdims=True)
    rstd = lax.rsqrt(var + LN_EPS)
    return cen * rstd, rstd


def _ln_bwd(dy, xhat, rstd, g):
    dxhat = dy * g
    m1 = jnp.mean(dxhat, axis=-1, keepdims=True)
    m2 = jnp.mean(dxhat * xhat, axis=-1, keepdims=True)
    return rstd * (dxhat - m1 - xhat * m2)


def _cast_weight(w, chip_vec, after, name, col_block=0, n_col_blocks=1):
    _, r, c = w.shape
    tr = _row_tile(r, TR_ELT)

    def body(chip_ref, w_ref, after_ref, o_ref):
        o_ref[...] = w_ref[...].astype(BF16)

    grid_spec = pltpu.PrefetchScalarGridSpec(
        num_scalar_prefetch=1, grid=(r // tr,),
        in_specs=[pl.BlockSpec((None, tr, c), lambda i, chip_ref: (0, i, 0)), _ANY],
        out_specs=pl.BlockSpec((None, tr, c), lambda i, chip_ref: (chip_ref[0], i, col_block)))
    return pl.pallas_call(
        body, name=name, grid_spec=grid_spec,
        out_shape=jax.ShapeDtypeStruct((N_CHIPS, r, n_col_blocks * c), BF16),
        input_output_aliases={2: 0} if col_block else {},
        compiler_params=_params(("parallel",)),
    )(chip_vec, w, after)


_HBM = pl.BlockSpec(memory_space=pltpu.HBM)
_VMEM = pl.BlockSpec(memory_space=pltpu.VMEM)


_SEM = pl.BlockSpec(memory_space=pltpu.SEMAPHORE)
_ANY = pl.BlockSpec(memory_space=pl.ANY)
_EFFECT = pltpu.SideEffectType.DATAFLOW_SIDE_EFFECTING


def _chip_copy(buf, k, chip_of_src, half_rows, send_sems, recv_sems, to):
    part = buf.at[chip_of_src, half_rows]
    return pltpu.make_async_remote_copy(
        src_ref=part, dst_ref=part, send_sem=send_sems.at[k], recv_sem=recv_sems.at[k], device_id=to, device_id_type=MESH)


def _half_rows(buf, which):
    hr = buf.shape[1] // 2
    return pl.ds(which * hr, hr)


def _after(value, dep):
    return lax.optimization_barrier((value, dep))[0]


def _flight_start(name, bufs, plan, n_sems, after):
    n = len(bufs)

    def body(*refs):
        sends, _ = plan(refs[:n], refs[n + 1], refs[n + 2])
        for cp in sends:
            cp.start()

    outs = pl.pallas_call(
        body, name=name,
        in_specs=[_HBM] * n + [_ANY], out_specs=[_SEM, _SEM] + [_HBM] * n,
        out_shape=[pltpu.SemaphoreType.DMA((n_sems,))] * 2 + [pltpu.HBM(b.shape, b.dtype) for b in bufs],
        input_output_aliases={i: 2 + i for i in range(n)},
        compiler_params=pltpu.CompilerParams(has_side_effects=_EFFECT),
    )(*[pltpu.with_memory_space_constraint(b, pltpu.HBM) for b in bufs], after)
    return outs[0], outs[1], list(outs[2:])


def _flight_wait(name, flight, plan, after):
    send_sems, recv_sems, bufs = flight
    n = len(bufs)

    def body(*refs):
        sends, recvs = plan(refs[:n], refs[n], refs[n + 1])
        for cp in sends:
            cp.wait_send()
        for cp in recvs:
            cp.wait_recv()

    outs = pl.pallas_call(
        body, name=name,
        in_specs=[_HBM] * n + [_SEM, _SEM, _ANY], out_specs=[_HBM] * n,
        out_shape=[pltpu.HBM(b.shape, b.dtype) for b in bufs],
        input_output_aliases={i: i for i in range(n)},
        compiler_params=pltpu.CompilerParams(has_side_effects=_EFFECT),
    )(*bufs, send_sems, recv_sems, after)
    return list(outs)


def _fill_plan(n_bufs):
    def plan(refs, send_sems, recv_sems):
        x, y, c, chips = _mesh_pos()
        sibling = (x, y, 1 - c)
        sends, recvs = [], []
        for w in range(n_bufs):
            for k, chip in enumerate(chips):
                slot = _chip_id(*chip)
                sends.append(_chip_copy(refs[w], 3 * w + k, slot, _half_rows(refs[w], c), send_sems, recv_sems, sibling))
                recvs.append(_chip_copy(refs[w], 3 * w + k, slot, _half_rows(refs[w], 1 - c), send_sems, recv_sems,
                                        sibling))
        return sends, recvs
    return plan


def _conv_w_plan():
    def plan(refs, send_sems, recv_sems):
        x, y, c, chips = _mesh_pos()
        me = _chip_id(x, y)
        (buf,) = refs
        sends, recvs = [], []
        for k, chip in enumerate(chips):
            for slot, into in ((me, sends), (_chip_id(*chip), recvs)):
                into.append(pltpu.make_async_remote_copy(
                    src_ref=buf.at[slot], dst_ref=buf.at[slot], send_sem=send_sems.at[k], recv_sem=recv_sems.at[k],
                    device_id=(*chip, c), device_id_type=MESH))
        return sends, recvs
    return plan


def _exchange_plan(n_parts):
    def plan(refs, send_sems, recv_sems):
        x, y, c, _ = _mesh_pos()
        copies = []
        for w in range(n_parts):
            part, got = refs[2 * w], refs[2 * w + 1]
            hr = got.shape[1]
            copies.append(pltpu.make_async_remote_copy(
                src_ref=part.at[:, pl.ds((1 - c) * hr, hr)], dst_ref=got, send_sem=send_sems.at[w],
                recv_sem=recv_sems.at[w], device_id=(x, y, 1 - c), device_id_type=MESH))
        return copies, copies
    return plan


def _gather_start(bufs, after, name):
    n = len(bufs)

    def body(*refs):
        ins = refs[:n]
        sends, recvs = refs[n + 1:2 * n + 1], refs[2 * n + 1:3 * n + 1]
        token = refs[4 * n + 1]
        x, y, c, chips = _mesh_pos()
        me = _chip_id(x, y)
        for w in range(n):
            for k, chip in enumerate(chips):
                _chip_copy(ins[w], k, me, _half_rows(ins[w], c), sends[w], recvs[w], (*chip, c)).start()
        token[...] = jnp.zeros_like(token)

    outs = pl.pallas_call(
        body, name=name,
        in_specs=[_HBM] * n + [_ANY],
        out_specs=[_SEM] * (2 * n) + [_HBM] * n + [_VMEM],
        out_shape=[pltpu.SemaphoreType.DMA((3,))] * (2 * n) + [pltpu.HBM(b.shape, b.dtype) for b in bufs]
        + [jax.ShapeDtypeStruct((8, 128), F32)],
        input_output_aliases={w: 2 * n + w for w in range(n)},
        compiler_params=pltpu.CompilerParams(has_side_effects=_EFFECT),
    )(*[pltpu.with_memory_space_constraint(b, pltpu.HBM) for b in bufs], after)
    return [(outs[w], outs[n + w], outs[2 * n + w]) for w in range(n)], outs[3 * n]


def _gather_wait(send_sems, recv_sems, buf, after, name):
    def body(buf_ref, send_ref, recv_ref, after_ref, out_ref):
        x, y, c, chips = _mesh_pos()
        me = _chip_id(x, y)
        for k, chip in enumerate(chips):
            _chip_copy(buf_ref, k, me, _half_rows(buf_ref, c), send_ref, recv_ref, (*chip, c)).wait_send()
        for k, chip in enumerate(chips):
            _chip_copy(buf_ref, k, _chip_id(*chip), _half_rows(buf_ref, c), send_ref, recv_ref, (*chip, c)).wait_recv()

    return pl.pallas_call(
        body, name=name,
        in_specs=[_HBM, _SEM, _SEM, _ANY], out_specs=_HBM,
        out_shape=pltpu.HBM(buf.shape, buf.dtype),
        input_output_aliases={0: 0},
        compiler_params=pltpu.CompilerParams(has_side_effects=_EFFECT),
    )(buf, send_sems, recv_sems, after)


def _sibling_fill(buf, name, own_too=False):
    n_copies = 4 if own_too else 3

    def body(buf_ref, out_ref, send_sems, recv_sems):
        x, y, c, chips = _mesh_pos()
        sibling = (x, y, 1 - c)
        slots = [_chip_id(*chip) for chip in chips] + ([_chip_id(x, y)] if own_too else [])
        copies = []
        for k, slot in enumerate(slots):
            cp = _chip_copy(out_ref, k, slot, _half_rows(out_ref, c), send_sems, recv_sems, sibling)
            cp.start()
            copies.append(cp)
        for k, slot in enumerate(slots):
            _chip_copy(out_ref, k, slot, _half_rows(out_ref, 1 - c), send_sems, recv_sems, sibling).wait_recv()
        for cp in copies:
            cp.wait_send()

    return pl.pallas_call(
        body, name=name,
        in_specs=[_HBM], out_specs=_HBM,
        out_shape=jax.ShapeDtypeStruct(buf.shape, buf.dtype),
        input_output_aliases={0: 0},
        scratch_shapes=[pltpu.SemaphoreType.DMA((n_copies,)), pltpu.SemaphoreType.DMA((n_copies,))],
    )(buf)


def _allgather_conv_w(cw):
    _, kw, cs = cw.shape

    def body(cw_ref, out_ref, send_sems, recv_sems):
        x, y, c, chips = _mesh_pos()
        me = _chip_id(x, y)
        out_ref[pl.ds(me, 1)] = cw_ref[...]
        copies = []
        for k, chip in enumerate(chips):
            cp = pltpu.make_async_remote_copy(
                src_ref=cw_ref.at[0], dst_ref=out_ref.at[me], send_sem=send_sems.at[k], recv_sem=recv_sems.at[k],
                device_id=(*chip, c), device_id_type=MESH)
            cp.start()
            copies.append(cp)
        for k, chip in enumerate(chips):
            pltpu.make_async_remote_copy(
                src_ref=cw_ref.at[0], dst_ref=out_ref.at[_chip_id(*chip)], send_sem=send_sems.at[k],
                recv_sem=recv_sems.at[k], device_id=(*chip, c), device_id_type=MESH).wait_recv()
        for cp in copies:
            cp.wait_send()

    return pl.pallas_call(
        body, name="allgather_conv_w",
        in_specs=[_VMEM], out_specs=_VMEM,
        out_shape=jax.ShapeDtypeStruct((N_CHIPS, kw, cs), F32),
        scratch_shapes=[pltpu.SemaphoreType.DMA((3,)), pltpu.SemaphoreType.DMA((3,))],
    )(cw)


def _exchange_halves(parts, after, name):
    n = len(parts)
    shapes = [p.shape for p in parts]

    def body(*refs):
        ins, outs = refs[:n], refs[n + 1:2 * n + 1]
        send_sems, recv_sems = refs[2 * n + 1:]
        x, y, c, _ = _mesh_pos()
        copies = []
        for w in range(n):
            hr = shapes[w][1] // 2
            cp = pltpu.make_async_remote_copy(
                src_ref=ins[w].at[:, pl.ds((1 - c) * hr, hr)], dst_ref=outs[w],
                send_sem=send_sems.at[w], recv_sem=recv_sems.at[w],
                device_id=(x, y, 1 - c), device_id_type=MESH)
            cp.start()
            copies.append(cp)
        for cp in copies:
            cp.wait()

    return pl.pallas_call(
        body, name=name,
        in_specs=[_HBM] * n + [_ANY], out_specs=[_HBM] * n,
        out_shape=[jax.ShapeDtypeStruct((s[0], s[1] // 2, s[2]), BF16) for s in shapes],
        scratch_shapes=[pltpu.SemaphoreType.DMA((n,)), pltpu.SemaphoreType.DMA((n,))],
    )(*parts, after)


def _add_halves(part, got, cvec, name):
    ns, r, cdim = part.shape
    hr = r // 2
    tr = _row_tile(hr, TR_ELT)
    nblk = hr // tr

    def body(c_ref, a_ref, b_ref, o_ref):
        o_ref[...] = (a_ref[...].astype(F32) + b_ref[...].astype(F32)).astype(BF16)

    grid_spec = pltpu.PrefetchScalarGridSpec(
        num_scalar_prefetch=1, grid=(ns, nblk),
        in_specs=[pl.BlockSpec((None, tr, cdim), lambda s, i, c_ref: (s, c_ref[0] * nblk + i, 0)),
                  pl.BlockSpec((None, tr, cdim), lambda s, i, c_ref: (s, i, 0))],
        out_specs=pl.BlockSpec((None, tr, cdim), lambda s, i, c_ref: (s, i, 0)))
    return pl.pallas_call(
        body, name=name, grid_spec=grid_spec,
        out_shape=jax.ShapeDtypeStruct((ns, hr, cdim), BF16),
        compiler_params=_params(("parallel", "parallel")),
    )(cvec, part, got)


def _scatter_copy(sums_ref, land_ref, k, src_slot, dst_slot, c, send_sems, recv_sems, to):
    return pltpu.make_async_remote_copy(
        src_ref=sums_ref.at[src_slot], dst_ref=land_ref.at[dst_slot, _half_rows(land_ref, c)],
        send_sem=send_sems.at[k], recv_sem=recv_sems.at[k], device_id=to, device_id_type=MESH)


def _scatter_start(sums, name):
    ns, hr, cdim = sums.shape
    land = lax.empty((ns, 2 * hr, cdim), sums.dtype)

    def body(sums_ref, land_ref, send_sems, recv_sems, sums_thru, land_thru):
        x, y, c, chips = _mesh_pos()
        me = _chip_id(x, y)
        for k, chip in enumerate(chips):
            _scatter_copy(sums_ref, land_ref, k, _chip_id(*chip), me, c, send_sems, recv_sems, (*chip, c)).start()

    return pl.pallas_call(
        body, name=name,
        in_specs=[_HBM, _HBM], out_specs=[_SEM, _SEM, _HBM, _HBM],
        out_shape=[pltpu.SemaphoreType.DMA((3,)), pltpu.SemaphoreType.DMA((3,)),
                   pltpu.HBM(sums.shape, sums.dtype), pltpu.HBM(land.shape, land.dtype)],
        input_output_aliases={0: 2, 1: 3},
        compiler_params=pltpu.CompilerParams(has_side_effects=_EFFECT),
    )(pltpu.with_memory_space_constraint(sums, pltpu.HBM), pltpu.with_memory_space_constraint(land, pltpu.HBM))


def _scatter_wait(send_sems, recv_sems, sums, land, after, name):
    def body(sums_ref, land_ref, send_ref, recv_ref, after_ref, sums_out, land_out):
        x, y, c, chips = _mesh_pos()
        me = _chip_id(x, y)
        for k, chip in enumerate(chips):
            _scatter_copy(sums_ref, land_ref, k, _chip_id(*chip), me, c, send_ref, recv_ref, (*chip, c)).wait_send()
        for k, chip in enumerate(chips):
            _scatter_copy(sums_ref, land_ref, k, me, _chip_id(*chip), c, send_ref, recv_ref, (*chip, c)).wait_recv()

    return pl.pallas_call(
        body, name=name,
        in_specs=[_HBM, _HBM, _SEM, _SEM, _ANY], out_specs=[_HBM, _HBM],
        out_shape=[pltpu.HBM(sums.shape, sums.dtype), pltpu.HBM(land.shape, land.dtype)],
        input_output_aliases={0: 0, 1: 1},
        compiler_params=pltpu.CompilerParams(has_side_effects=_EFFECT),
    )(sums, land, send_sems, recv_sems, after)


def _complete_plan(n_weights):
    def plan(refs, send_sems, recv_sems):
        x, y, c, chips = _mesh_pos()
        me = _chip_id(x, y)
        sibling = (x, y, 1 - c)
        sends, recvs = [], []
        for w in range(n_weights):
            sums, land = refs[2 * w], refs[2 * w + 1]
            sends.append(_scatter_copy(sums, land, 4 * w + 3, me, me, c, send_sems, recv_sems, sibling))
            recvs.append(_scatter_copy(sums, land, 4 * w + 3, me, me, 1 - c, send_sems, recv_sems, sibling))
            for k, chip in enumerate(chips):
                slot = _chip_id(*chip)
                sends.append(_chip_copy(land, 4 * w + k, slot, _half_rows(land, c), send_sems, recv_sems, sibling))
                recvs.append(_chip_copy(land, 4 * w + k, slot, _half_rows(land, 1 - c), send_sems, recv_sems, sibling))
        return sends, recvs
    return plan


def _complete_chip_sums(sums, lands):
    n = len(sums)

    def body(*refs):
        sums_refs, outs = refs[:n], refs[2 * n:3 * n]
        send_sems, recv_sems = refs[3 * n:]
        x, y, c, chips = _mesh_pos()
        me = _chip_id(x, y)
        sibling = (x, y, 1 - c)
        slots = [_chip_id(*chip) for chip in chips]
        sent = []
        for w in range(n):
            out = outs[w]
            cp = _scatter_copy(sums_refs[w], out, 3, me, me, c, send_sems.at[w], recv_sems.at[w], sibling)
            cp.start()
            sent.append(cp)
            for k, slot in enumerate(slots):
                cp = _chip_copy(out, k, slot, _half_rows(out, c), send_sems.at[w], recv_sems.at[w], sibling)
                cp.start()
                sent.append(cp)
        for w in range(n):
            out = outs[w]
            _scatter_copy(sums_refs[w], out, 3, me, me, 1 - c, send_sems.at[w], recv_sems.at[w], sibling).wait_recv()
            for k, slot in enumerate(slots):
                _chip_copy(out, k, slot, _half_rows(out, 1 - c), send_sems.at[w], recv_sems.at[w], sibling).wait_recv()
        for cp in sent:
            cp.wait_send()

    return pl.pallas_call(
        body, name="complete_chip_sums",
        in_specs=[_HBM] * (2 * n), out_specs=[_HBM] * n,
        out_shape=[jax.ShapeDtypeStruct(b.shape, b.dtype) for b in lands],
        input_output_aliases={n + w: w for w in range(n)},
        scratch_shapes=[pltpu.SemaphoreType.DMA((n, 4)), pltpu.SemaphoreType.DMA((n, 4))],
    )(*sums, *lands)


SMALL_ROWS = 8


def _allreduce_small(gl2g, gl2b, gl1g, gl1b, g_ac, gcw, gsink, loss, after):
    d = gl2g.shape[1]
    hd = d // 2
    nq = gsink.shape[1]

    def body(a_ref, b_ref, c_ref, d_ref, e_ref, cw_ref, sk_ref, ls_ref, after_ref, out_ref, mine, gath, send_sems,
             recv_sems):
        x, y, c, _ = _mesh_pos()
        me = 4 * x + 2 * y + c
        mine[...] = jnp.zeros_like(mine)
        mine[0:1, :] = a_ref[...]
        mine[1:2, :] = b_ref[...]
        mine[2:3, :] = c_ref[...]
        mine[3:4, :] = d_ref[...]
        mine[4:5, :] = e_ref[...]
        mine[5:6, 0:hd] = cw_ref[0:1, :]
        mine[5:6, hd:d] = cw_ref[1:2, :]
        mine[6:7, 0:hd] = cw_ref[2:3, :]
        mine[6:7, hd:hd + nq] = sk_ref[...]
        mine[6:7, hd + 128:hd + 256] = ls_ref[...]
        gath[pl.ds(me, 1)] = mine[...][None]
        copies = []
        for r in range(1, 8):
            peer = ((1 - x) if r & 4 else x, (1 - y) if r & 2 else y, (1 - c) if r & 1 else c)
            cp = pltpu.make_async_remote_copy(
                src_ref=mine, dst_ref=gath.at[me], send_sem=send_sems.at[r - 1], recv_sem=recv_sems.at[r - 1],
                device_id=peer, device_id_type=MESH)
            cp.start()
            copies.append(cp)
        for r in range(1, 8):
            peer = ((1 - x) if r & 4 else x, (1 - y) if r & 2 else y, (1 - c) if r & 1 else c)
            peer_id = 4 * peer[0] + 2 * peer[1] + peer[2]
            pltpu.make_async_remote_copy(
                src_ref=mine, dst_ref=gath.at[peer_id], send_sem=send_sems.at[r - 1], recv_sem=recv_sems.at[r - 1],
                device_id=peer, device_id_type=MESH).wait_recv()
        for cp in copies:
            cp.wait_send()
        total = gath[0]
        for dev in range(1, 8):
            total = total + gath[dev]
        out_ref[...] = total

    return pl.pallas_call(
        body, name="allreduce_small",
        in_specs=[_VMEM] * 8 + [_ANY], out_specs=_VMEM,
        out_shape=jax.ShapeDtypeStruct((SMALL_ROWS, d), F32),
        scratch_shapes=[pltpu.VMEM((SMALL_ROWS, d), F32), pltpu.VMEM((8, SMALL_ROWS, d), F32),
                        pltpu.SemaphoreType.DMA((7,)), pltpu.SemaphoreType.DMA((7,))],
    )(gl2g, gl2b, gl1g, gl1b, g_ac, gcw, gsink, loss, after)


def _adamw(w, g, m, v):
    m = ADAM_B1 * m + (1.0 - ADAM_B1) * g
    v = ADAM_B2 * v + (1.0 - ADAM_B2) * (g * g)
    m_hat = m / (1.0 - ADAM_B1 ** ADAM_STEP)
    v_hat = v / (1.0 - ADAM_B2 ** ADAM_STEP)
    delta = -ADAM_LR * (m_hat / (jnp.sqrt(v_hat) + ADAM_EPS) + ADAM_WD * w)
    return delta, m, v


def _adamw_shard(w, m, v, land, own, pos_vec, name, col_block=0):
    _, r, c = w.shape
    hr = r // 2
    tr = _row_tile(hr, TR_ELT)
    nh = hr // tr

    def body(pos_ref, w_ref, m_ref, v_ref, l0, l1, l2, l3, own_ref, g_out, d_out, m_out, v_out):
        i = pl.program_id(0)
        mine = (i // nh) == pos_ref[1]
        own_blk = own_ref[...].astype(F32)
        g = None
        for s, l_ref in enumerate([l0, l1, l2, l3]):
            term = jnp.where(mine & (pos_ref[0] == s), own_blk, l_ref[...].astype(F32))
            g = term if g is None else g + term
        delta, nm, nv = _adamw(w_ref[...], g, m_ref[...], v_ref[...])
        g_out[...] = g
        d_out[...] = delta
        m_out[...] = nm
        v_out[...] = nv

    def land_spec(s):
        def index(i, pos_ref):
            skip = (pos_ref[0] == s) & ((i // nh) == pos_ref[1])
            return (s, jnp.where(skip, (i + nh) % (2 * nh), i), col_block)
        return pl.BlockSpec((None, tr, c), index)

    blk = pl.BlockSpec((None, tr, c), lambda i, pos_ref: (0, i, 0))
    grid_spec = pltpu.PrefetchScalarGridSpec(
        num_scalar_prefetch=1, grid=(2 * nh,),
        in_specs=[blk, blk, blk] + [land_spec(s) for s in range(N_CHIPS)]
        + [pl.BlockSpec((None, tr, c), lambda i, pos_ref: (pos_ref[0], i % nh, col_block))],
        out_specs=[blk] * 4)
    return pl.pallas_call(
        body, name=name, grid_spec=grid_spec,
        out_shape=[jax.ShapeDtypeStruct((1, r, c), F32)] * 4,
        compiler_params=_params(("parallel",)),
    )(pos_vec, w, m, v, land, land, land, land, own)


def _adamw_small(red, params):
    names = ["sinks", "g_attn", "g_conv", "ln1_g", "ln1_b", "ln2_g", "ln2_b", "conv_w"]
    d = red.shape[1]
    hd = d // 2
    flat = []
    for nme in names:
        flat.extend(params[nme])
    nq = params["sinks"][0].shape[1]
    cs = params["conv_w"][0].shape[2]

    def body(*refs):
        red_ref = refs[0]
        ins = refs[1:1 + 3 * len(names)]
        outs = refs[1 + 3 * len(names):]
        x, y, _, _ = _mesh_pos()
        me = _chip_id(x, y)

        def conv_tap(row, base):
            picked = red_ref[row:row + 1, base:base + cs]
            for s in range(1, N_CHIPS):
                picked = jnp.where(me == s, red_ref[row:row + 1, base + s * cs:base + (s + 1) * cs], picked)
            return picked

        grads = {
            "sinks": red_ref[6:7, hd:hd + nq],
            "g_attn": red_ref[4:5, 0:hd],
            "g_conv": red_ref[4:5, hd:d],
            "ln1_g": red_ref[2:3, :],
            "ln1_b": red_ref[3:4, :],
            "ln2_g": red_ref[0:1, :],
            "ln2_b": red_ref[1:2, :],
        }
        for i, nme in enumerate(names):
            w_ref, m_ref, v_ref = ins[3 * i:3 * i + 3]
            g_out, d_out, m_out, v_out = outs[4 * i:4 * i + 4]
            if nme == "conv_w":
                for tap, (row, base) in enumerate([(5, 0), (5, hd), (6, 0)]):
                    g = conv_tap(row, base)
                    delta, nm, nv = _adamw(w_ref[0, tap:tap + 1, :], g, m_ref[0, tap:tap + 1, :], v_ref[0, tap:tap + 1, :])
                    g_out[0, tap:tap + 1, :] = g
                    d_out[0, tap:tap + 1, :] = delta
                    m_out[0, tap:tap + 1, :] = nm
                    v_out[0, tap:tap + 1, :] = nv
            else:
                g = grads[nme]
                delta, nm, nv = _adamw(w_ref[...], g, m_ref[...], v_ref[...])
                g_out[...] = g
                d_out[...] = delta
                m_out[...] = nm
                v_out[...] = nv

    out_shape = []
    for nme in names:
        out_shape.extend([jax.ShapeDtypeStruct(params[nme][0].shape, F32)] * 4)
    outs = pl.pallas_call(
        body, name="adamw_small",
        in_specs=[_VMEM] * (1 + len(flat)), out_specs=[_VMEM] * len(out_shape),
        out_shape=out_shape,
    )(red, *flat)
    return {nme: tuple(outs[4 * i:4 * i + 4]) for i, nme in enumerate(names)}


def _rope_tables(pos_col):
    s = pos_col.shape[0]
    w = N_KV_HEADS * HEAD_DIM
    tb = min(512, s)
    inv_freq = (ROPE_THETA ** (-np.arange(0, ROT_DIM, 2, dtype=np.float32) / ROT_DIM)).astype(np.float32)

    def body(pos_ref, cos_ref, sin_ref):
        pos = pos_ref[...].astype(F32)
        lane = lax.broadcasted_iota(jnp.int32, (tb, PAIR), 1) & (HEAD_DIM - 1)
        fidx = lane & (ROT_DIM // 2 - 1)
        inv = jnp.zeros((tb, PAIR), F32)
        for k in range(ROT_DIM // 2):
            inv = jnp.where(fidx == k, float(inv_freq[k]), inv)
        ang = pos * inv
        rot = lane < ROT_DIM
        sin_v = jnp.sin(ang)
        cos_ref[...] = _tile_lanes(jnp.where(rot, jnp.cos(ang), 1.0), w // PAIR)
        sin_ref[...] = _tile_lanes(jnp.where(lane < ROT_DIM // 2, -sin_v, jnp.where(rot, sin_v, 0.0)), w // PAIR)

    return pl.pallas_call(
        body, name="rope_tables", grid=(s // tb,),
        in_specs=[pl.BlockSpec((tb, 1), lambda i: (i, 0))],
        out_specs=[pl.BlockSpec((tb, w), lambda i: (i, 0))] * 2,
        out_shape=[jax.ShapeDtypeStruct((s, w), F32)] * 2,
        compiler_params=_params(("parallel",)),
    )(pos_col)


def _in_proj(x, w_in_g, first_vec, n_shards, into, name):
    _, s, d = x.shape
    ns, _, ncol = w_in_g.shape
    tm = min(2 * TM, s)

    def body(first_ref, x_ref, w_ref, into_ref, o_ref):
        o_ref[...] = _dot(x_ref[...].astype(BF16), w_ref[...]).astype(BF16)

    shard = lambda j, first_ref: lax.rem(first_ref[0] + j, ns)
    grid_spec = pltpu.PrefetchScalarGridSpec(
        num_scalar_prefetch=1, grid=(s // tm, n_shards),
        in_specs=[pl.BlockSpec((None, tm, d), lambda i, j, first_ref: (0, i, 0)),
                  pl.BlockSpec((None, d, ncol), lambda i, j, first_ref: (shard(j, first_ref), 0, 0)), _ANY],
        out_specs=pl.BlockSpec((tm, ncol), lambda i, j, first_ref: (i, shard(j, first_ref))))
    return pl.pallas_call(
        body, name=name, grid_spec=grid_spec,
        out_shape=jax.ShapeDtypeStruct((s, ns * ncol), BF16),
        input_output_aliases={} if into is None else {3: 0},
        compiler_params=_params(("parallel", "arbitrary")),
    )(first_vec, x, w_in_g, first_vec if into is None else into)


PAIR = 2 * HEAD_DIM
KEYS = 2 * WINDOW


def _pair_operand(t_all, h):
    col = (h // 2) * PAIR
    lane = lax.broadcasted_iota(jnp.int32, (KEYS, PAIR), 1)
    own_low = h % 2 == 0
    mine = jnp.where((lane < HEAD_DIM) if own_low else (lane >= HEAD_DIM), t_all[:, col:col + PAIR], 0.0)
    other = pltpu.roll(mine, HEAD_DIM, 1)
    low, high = (mine, other) if own_low else (other, mine)
    return jnp.concatenate([low, high], axis=0).astype(BF16)


def _pair_grad(acc, h):
    lane = lax.broadcasted_iota(jnp.int32, (KEYS, PAIR), 1)
    low = jnp.where(lane < HEAD_DIM, acc[:KEYS], 0.0)
    high = jnp.where(lane >= HEAD_DIM, acc[KEYS:], 0.0)
    if h % 2 == 0:
        return low + pltpu.roll(high, HEAD_DIM, 1)
    return high + pltpu.roll(low, HEAD_DIM, 1)


N_PAIRS = N_KV_HEADS * GROUP // 2


def _all_probs(q, kk2s, first, sinks_ref):
    assert ATTN_SCALE == 0.125
    q = q * ATTN_SCALE
    qps, scores = [], []
    for pair in range(N_PAIRS):
        qp = q[:, pair * PAIR:(pair + 1) * PAIR].astype(BF16)
        qps.append(qp)
        scores.append(_dot_nt(qp, kk2s[pair // (GROUP // 2)]))
    qi = lax.broadcasted_iota(jnp.int32, (WINDOW, 2 * KEYS), 0)
    kj = lax.broadcasted_iota(jnp.int32, (WINDOW, 2 * KEYS), 1) & (KEYS - 1)
    rel = qi + WINDOW - kj
    valid = (rel >= 0) & (rel < WINDOW) & jnp.logical_not(first & (kj < WINDOW))
    bias = jnp.where(valid, 0.0, NEG_BIG)
    s = (jnp.stack(scores, axis=0) + bias[None]).reshape(N_PAIRS * WINDOW, 2 * KEYS)
    probs, p_sinks = [], []
    for t in range(2):
        st = s[:, t * KEYS:(t + 1) * KEYS]
        sink = jnp.concatenate([jnp.broadcast_to(sinks_ref[0:1, 2 * pair + t:2 * pair + t + 1], (WINDOW, 1))
                                for pair in range(N_PAIRS)], axis=0)
        m = jnp.maximum(jnp.max(st, axis=1, keepdims=True), sink)
        e = jnp.exp(st - m)
        e_sink = jnp.exp(sink - m)
        inv_l = 1.0 / (jnp.sum(e, axis=1, keepdims=True) + e_sink)
        probs.append(e * inv_l)
        p_sinks.append(e_sink * inv_l)
    return qps, jnp.concatenate(probs, axis=1), p_sinks


def _roped_qkv(cur_ref, prev_ref, cos_ref, sin_ref, cosp_ref, sinp_ref, qw, kvw):
    cur = cur_ref[...].astype(F32)
    cos, sin = cos_ref[...], sin_ref[...]
    cos_q, sin_q = _tile_lanes(cos, GROUP), _tile_lanes(sin, GROUP)
    q = _rope(cur[:, :qw], cos_q, sin_q, 1.0)
    prev = prev_ref[...].astype(F32)
    k_all = jnp.concatenate([_rope(prev[:, :kvw], cosp_ref[...], sinp_ref[...], 1.0),
                             _rope(cur[:, qw:qw + kvw], cos, sin, 1.0)], axis=0)
    v_all = jnp.concatenate([prev[:, kvw:], cur[:, qw + kvw:]], axis=0)
    return q, k_all, v_all, cos_q, sin_q


def _attention_fwd(proj, cos_t, sin_t, sinks):
    s = proj.shape[0]
    qw = GROUP * N_KV_HEADS * HEAD_DIM
    kvw = N_KV_HEADS * HEAD_DIM
    nb = s // WINDOW

    def body(cur_ref, prev_ref, cos_ref, sin_ref, cosp_ref, sinp_ref, sinks_ref, o_ref):
        first = pl.program_id(0) == 0
        q, k_all, v_all, _, _ = _roped_qkv(cur_ref, prev_ref, cos_ref, sin_ref, cosp_ref, sinp_ref, qw, kvw)
        kk2s = [_pair_operand(k_all, h) for h in range(N_KV_HEADS)]
        vv2s = [_pair_operand(v_all, h) for h in range(N_KV_HEADS)]
        _, probs, _ = _all_probs(q, kk2s, first, sinks_ref)
        probs = probs.astype(BF16)
        outs = [_dot(probs[pair * WINDOW:(pair + 1) * WINDOW], vv2s[pair // (GROUP // 2)]) for pair in range(N_PAIRS)]
        o_ref[...] = jnp.concatenate(outs, axis=1)

    tbl = pl.BlockSpec((WINDOW, kvw), lambda n: (n, 0))
    tbl_prev = pl.BlockSpec((WINDOW, kvw), lambda n: (jnp.maximum(n - 1, 0), 0))
    return pl.pallas_call(
        body, name="attention_fwd", grid=(nb,),
        in_specs=[pl.BlockSpec((WINDOW, qw + 2 * kvw), lambda n: (n, 0)),
                  pl.BlockSpec((WINDOW, 2 * kvw), lambda n: (jnp.maximum(n - 1, 0), (qw // (2 * kvw)))),
                  tbl, tbl, tbl_prev, tbl_prev, _VMEM],
        out_specs=pl.BlockSpec((WINDOW, qw), lambda n: (n, 0)),
        out_shape=jax.ShapeDtypeStruct((s, qw), F32),
        compiler_params=_params(("parallel",)),
    )(proj, proj, cos_t, sin_t, cos_t, sin_t, sinks)


def _conv_taps(cw_ref):
    return [jnp.concatenate([cw_ref[s, k:k + 1, :] for s in range(N_CHIPS)], axis=1) for k in range(3)]


def _shift_down(z, halo, steps):
    last = halo.shape[0]
    row = lax.broadcasted_iota(jnp.int32, z.shape, 0)
    out = pltpu.roll(z, steps, 0)
    for r in range(steps):
        out = jnp.where(row == r, halo[last - steps + r:last - steps + r + 1, :], out)
    return out


def _shift_up(z, halo, steps):
    rows = z.shape[0]
    row = lax.broadcasted_iota(jnp.int32, z.shape, 0)
    out = pltpu.roll(z, rows - steps, 0)
    for r in range(steps):
        out = jnp.where(row == rows - steps + r, halo[r:r + 1, :], out)
    return out


def _split_cbu(lo, hi, cw):
    lo, hi = lo.astype(F32), hi.astype(F32)
    c_gate = lo[:, :cw]
    b_gate = jnp.concatenate([lo[:, cw:], hi[:, :2 * cw - lo.shape[1]]], axis=1)
    u = hi[:, 2 * cw - lo.shape[1]:]
    return c_gate, b_gate, u


def _conv_norm(proj, attn, cw_full, g_ac):
    s, in_w = proj.shape
    cw = attn.shape[1]
    blk_w = in_w // 3
    tb = min(TB_CONV, s)

    def body(lo_ref, hi_ref, lo_h_ref, hi_h_ref, attn_ref, cw_ref, g_ref, mixed_ref, ac_ref, rstd_ref):
        i = pl.program_id(0)
        c_gate, b_gate, u = _split_cbu(lo_ref[...], hi_ref[...], cw)
        c_h, _, u_h = _split_cbu(lo_h_ref[...], hi_h_ref[...], cw)
        z = c_gate * u
        z_h = jnp.where(i == 0, 0.0, c_h * u_h)
        w0, w1, w2 = _conv_taps(cw_ref)
        y = w0 * _shift_down(z, z_h, 2) + w1 * _shift_down(z, z_h, 1) + w2 * z
        conv = b_gate * y
        a = attn_ref[...]
        r_a = lax.rsqrt(jnp.mean(a * a, axis=-1, keepdims=True) + RMS_EPS)
        r_c = lax.rsqrt(jnp.mean(conv * conv, axis=-1, keepdims=True) + RMS_EPS)
        g = g_ref[...]
        mixed_ref[...] = jnp.concatenate([a * r_a * g[:, :cw], conv * r_c * g[:, cw:]], axis=1).astype(BF16)
        ac_ref[...] = jnp.concatenate([a, conv], axis=1)
        rstd_ref[0] = r_a
        rstd_ref[1] = r_c

    halo_idx = lambda i: jnp.maximum(i * (tb // HALO_ROWS) - 1, 0)
    return pl.pallas_call(
        body, name="conv_norm", grid=(s // tb,),
        in_specs=[pl.BlockSpec((tb, blk_w), lambda i: (i, 1)),
                  pl.BlockSpec((tb, blk_w), lambda i: (i, 2)),
                  pl.BlockSpec((HALO_ROWS, blk_w), lambda i: (halo_idx(i), 1)),
                  pl.BlockSpec((HALO_ROWS, blk_w), lambda i: (halo_idx(i), 2)),
                  pl.BlockSpec((tb, cw), lambda i: (i, 0)),
                  _VMEM, _VMEM],
        out_specs=[pl.BlockSpec((tb, 2 * cw), lambda i: (i, 0)),
                   pl.BlockSpec((tb, 2 * cw), lambda i: (i, 0)),
                   pl.BlockSpec((2, tb, 1), lambda i: (0, i, 0))],
        out_shape=[jax.ShapeDtypeStruct((s, 2 * cw), BF16), jax.ShapeDtypeStruct((s, 2 * cw), F32),
                   jax.ShapeDtypeStruct((2, s, 1), F32)],
        compiler_params=_params(("parallel",)),
    )(proj, proj, proj, proj, attn, cw_full, g_ac)


def _out_proj_ln(mixed, w_out_g, x, ln_g, ln_b):
    s, d = mixed.shape
    tm = min(TM, s)
    tk = d
    nk = d // tk

    def body(a_ref, w_ref, x_ref, g_ref, b_ref, xhat_ref, h_ref, rstd_ref, acc):
        k = pl.program_id(1)
        _accumulate(acc, lambda: _dot(a_ref[...], w_ref[...]), k, nk)

        @pl.when(k == nk - 1)
        def _():
            def rows_fn(rows):
                xhat, rstd = _ln_fwd(ALPHA * x_ref[rows, :] + acc[rows, :])
                xhat_ref[rows, :] = xhat
                h_ref[rows, :] = (xhat * g_ref[...] + b_ref[...]).astype(BF16)
                rstd_ref[rows, :] = rstd

            _for_row_chunks(tm, rows_fn)

    row = pl.BlockSpec((tm, d), lambda i, k: (i, 0))
    return pl.pallas_call(
        body, name="out_proj_ln", grid=(s // tm, nk),
        in_specs=[pl.BlockSpec((tm, tk), lambda i, k: (i, k)),
                  pl.BlockSpec((tk, d), lambda i, k: (k, 0)),
                  pl.BlockSpec((None, tm, d), lambda i, k: (0, i, 0)),
                  _VMEM, _VMEM],
        out_specs=[row, row, pl.BlockSpec((tm, 1), lambda i, k: (i, 0))],
        out_shape=[jax.ShapeDtypeStruct((s, d), F32), jax.ShapeDtypeStruct((s, d), BF16),
                   jax.ShapeDtypeStruct((s, 1), F32)],
        scratch_shapes=[pltpu.VMEM((tm, d), F32)],
        compiler_params=_params(("parallel", "arbitrary")),
    )(mixed, w_out_g, x, ln_g, ln_b)


def _gate_up(h1, w_gu_g, first_vec, n_shards, into, name):
    s, d = h1.shape
    ns, _, fs2 = w_gu_g.shape
    fs = fs2 // 2
    tm = min(TM, s)

    def body(first_ref, h_ref, w_ref, act_in, ab_in, act_ref, ab_ref):
        gu = _dot(h_ref[...], w_ref[...])
        g, u = gu[:, :fs], gu[:, fs:]
        sg = _sigmoid(g)
        silu = g * sg
        act_ref[...] = (silu * u).astype(BF16)
        ab_ref[:, :fs] = (u * (sg * (1.0 + g * (1.0 - sg)))).astype(BF16)
        ab_ref[:, fs:] = silu.astype(BF16)

    shard = lambda j, first_ref: lax.rem(first_ref[0] + j, ns)
    grid_spec = pltpu.PrefetchScalarGridSpec(
        num_scalar_prefetch=1, grid=(s // tm, n_shards),
        in_specs=[pl.BlockSpec((tm, d), lambda i, j, first_ref: (i, 0)),
                  pl.BlockSpec((None, d, fs2), lambda i, j, first_ref: (shard(j, first_ref), 0, 0)), _ANY, _ANY],
        out_specs=[pl.BlockSpec((tm, fs), lambda i, j, first_ref: (i, shard(j, first_ref))),
                   pl.BlockSpec((tm, fs2), lambda i, j, first_ref: (i, shard(j, first_ref)))])
    return pl.pallas_call(
        body, name=name, grid_spec=grid_spec,
        out_shape=[jax.ShapeDtypeStruct((s, ns * fs), BF16), jax.ShapeDtypeStruct((s, ns * fs2), BF16)],
        input_output_aliases={} if into is None else {3: 0, 4: 1},
        compiler_params=_params(("parallel", "arbitrary")),
    )(first_vec, h1, w_gu_g, *((first_vec, first_vec) if into is None else into))


def _down_ln_loss(act, w_down_g, xhat1, ln1_g, ln1_b, ln2_g, ln2_b, target):
    s, f = act.shape
    d = xhat1.shape[1]
    tm = min(TM, s)
    tk = f // N_CHIPS
    nk = f // tk

    def body(a_ref, w_ref, xh_ref, g1_ref, b1_ref, g2_ref, b2_ref, t_ref, dpre_ref, dpre16_ref, loss_ref, gg_ref, gb_ref,
             acc):
        i, k = pl.program_id(0), pl.program_id(1)
        _accumulate(acc, lambda: _dot(a_ref[...], w_ref[...]), k, nk)

        @pl.when(k == nk - 1)
        def _():
            @pl.when(i == 0)
            def _():
                loss_ref[...] = jnp.zeros_like(loss_ref)
                gg_ref[...] = jnp.zeros_like(gg_ref)
                gb_ref[...] = jnp.zeros_like(gb_ref)

            def rows_fn(rows):
                h1 = xh_ref[rows, :] * g1_ref[...] + b1_ref[...]
                xhat, rstd = _ln_fwd(ALPHA * h1 + acc[rows, :])
                g2 = g2_ref[...]
                diff = xhat * g2 + b2_ref[...] - t_ref[rows, :]
                dy = diff * (1.0 / d)
                dpre = _ln_bwd(dy, xhat, rstd, g2)
                dpre_ref[rows, :] = dpre
                dpre16_ref[rows, :] = dpre.astype(BF16)
                sq = jnp.sum(jnp.sum(diff * diff, axis=1, keepdims=True), axis=0, keepdims=True)
                loss_ref[...] += jnp.broadcast_to(sq * (0.5 / d), (1, 128))
                gg_ref[...] += jnp.sum(dy * xhat, axis=0, keepdims=True)
                gb_ref[...] += jnp.sum(dy, axis=0, keepdims=True)

            _for_row_chunks(tm, rows_fn)

    row = pl.BlockSpec((tm, d), lambda i, k: (i, 0))
    vec = pl.BlockSpec((1, d), lambda i, k: (0, 0))
    return pl.pallas_call(
        body, name="down_ln_loss", grid=(s // tm, nk),
        in_specs=[pl.BlockSpec((tm, tk), lambda i, k: (i, k)),
                  pl.BlockSpec((tk, d), lambda i, k: (k, 0)),
                  row, _VMEM, _VMEM, _VMEM, _VMEM,
                  pl.BlockSpec((None, tm, d), lambda i, k: (0, i, 0))],
        out_specs=[row, row, pl.BlockSpec((1, 128), lambda i, k: (0, 0)), vec, vec],
        out_shape=[jax.ShapeDtypeStruct((s, d), F32), jax.ShapeDtypeStruct((s, d), BF16),
                   jax.ShapeDtypeStruct((1, 128), F32), jax.ShapeDtypeStruct((1, d), F32),
                   jax.ShapeDtypeStruct((1, d), F32)],
        scratch_shapes=[pltpu.VMEM((tm, d), F32)],
        compiler_params=_params(("arbitrary", "arbitrary")),
    )(act, w_down_g, xhat1, ln1_g, ln1_b, ln2_g, ln2_b, target)


def _dact_silu_bwd(dpre2, w_down_g, ab):
    s, d = dpre2.shape
    fs2 = ab.shape[1] // N_CHIPS
    fs = fs2 // 2
    tm = min(TM, s)

    def body(dp_ref, w_ref, ab_ref, dgu_ref):
        d_act = _dot_nt(dp_ref[...], w_ref[...])
        dgu_ref[:, :fs] = (d_act * ab_ref[:, :fs].astype(F32)).astype(BF16)
        dgu_ref[:, fs:] = (d_act * ab_ref[:, fs:].astype(F32)).astype(BF16)

    blk = pl.BlockSpec((tm, fs2), lambda j, i: (i, j))
    return pl.pallas_call(
        body, name="dact_silu_bwd", grid=(N_CHIPS, s // tm),
        in_specs=[pl.BlockSpec((tm, d), lambda j, i: (i, 0)),
                  pl.BlockSpec((fs, d), lambda j, i: (j, 0)), blk],
        out_specs=blk,
        out_shape=jax.ShapeDtypeStruct(ab.shape, BF16),
        compiler_params=_params(("parallel", "parallel")),
    )(dpre2, w_down_g, ab)


def _grad_rows(a, b, after, name, row_blocks=1):
    s, m = a.shape
    n = b.shape[1]
    ms = m // N_CHIPS
    tmw = ms // row_blocks
    tk = min(TK_TOK, s)
    nk = s // tk

    def body(a_ref, b_ref, after_ref, o_ref, acc):
        k = pl.program_id(2)
        _accumulate(acc, lambda: _dot_tn(a_ref[...].astype(BF16), b_ref[...].astype(BF16)), k, nk)

        @pl.when(k == nk - 1)
        def _():
            o_ref[...] = acc[...].astype(BF16)

    return pl.pallas_call(
        body, name=name, grid=(N_CHIPS, row_blocks, nk),
        in_specs=[pl.BlockSpec((tk, tmw), lambda j, r, k: (k, j * row_blocks + r)),
                  pl.BlockSpec((tk, n), lambda j, r, k: (k, 0)), _ANY],
        out_specs=pl.BlockSpec((None, tmw, n), lambda j, r, k: (j, r, 0)),
        out_shape=jax.ShapeDtypeStruct((N_CHIPS, ms, n), BF16),
        scratch_shapes=[pltpu.VMEM((tmw, n), F32)],
        compiler_params=_params(("parallel", "parallel", "arbitrary")),
    )(a, b, after)


def _grad_cols(a, bs, after, name, a_3d=False, row_blocks=2):
    s, m = a.shape[-2:]
    n = bs[0].shape[1]
    ns = n // N_CHIPS
    nb = len(bs)
    tmw = m // row_blocks
    tk = min(TK_TOK, s)
    nk = s // tk

    def body(*refs):
        a_ref, b_refs, o_refs, accs = refs[0], refs[1:1 + nb], refs[2 + nb:2 + 2 * nb], refs[2 + 2 * nb:]
        k = pl.program_id(2)
        for b_ref, acc in zip(b_refs, accs):
            _accumulate(acc, lambda b_ref=b_ref: _dot_tn(a_ref[...].astype(BF16), b_ref[...].astype(BF16)), k, nk)

        @pl.when(k == nk - 1)
        def _():
            for o_ref, acc in zip(o_refs, accs):
                o_ref[...] = acc[...].astype(BF16)

    if a_3d:
        a_spec = pl.BlockSpec((None, tk, tmw), lambda j, r, k: (0, k, r))
    else:
        a_spec = pl.BlockSpec((tk, tmw), lambda j, r, k: (k, r))
    return pl.pallas_call(
        body, name=name, grid=(N_CHIPS, row_blocks, nk),
        in_specs=[a_spec] + [pl.BlockSpec((tk, ns), lambda j, r, k: (k, j))] * nb + [_ANY],
        out_specs=[pl.BlockSpec((None, tmw, ns), lambda j, r, k: (j, r, 0))] * nb,
        out_shape=[jax.ShapeDtypeStruct((N_CHIPS, m, ns), BF16)] * nb,
        scratch_shapes=[pltpu.VMEM((tmw, ns), F32)] * nb,
        compiler_params=_params(("parallel", "parallel", "arbitrary")),
    )(a, *bs, after)


def _dh1_ln_bwd(d_gu, w_gu_g, dpre2, xhat1, rstd1, ln1_g, after):
    s = d_gu.shape[0]
    d = dpre2.shape[1]
    hd = d // 2
    fs = w_gu_g.shape[2]
    tm = min(TM, s)

    def body(dgu_ref, w_ref, dp2_ref, xh_ref, rs_ref, g_ref, after_ref, dpre_ref, gg_ref, gb_ref, acc_lo, acc_hi):
        i, j, half = pl.program_id(0), pl.program_id(1), pl.program_id(2)

        def product():
            return _dot_nt(dgu_ref[...], w_ref[...])

        @pl.when(half == 0)
        def _():
            _accumulate(acc_lo, product, j, N_CHIPS)

        @pl.when(half == 1)
        def _():
            _accumulate(acc_hi, product, j, N_CHIPS)

        @pl.when((j == N_CHIPS - 1) & (half == 1))
        def _():
            @pl.when(i == 0)
            def _():
                gg_ref[...] = jnp.zeros_like(gg_ref)
                gb_ref[...] = jnp.zeros_like(gb_ref)

            def rows_fn(rows):
                dh = jnp.concatenate([acc_lo[rows, :], acc_hi[rows, :]], axis=1) + ALPHA * dp2_ref[rows, :]
                xhat = xh_ref[rows, :]
                dpre_ref[rows, :] = _ln_bwd(dh, xhat, rs_ref[rows, :], g_ref[...])
                gg_ref[...] += jnp.sum(dh * xhat, axis=0, keepdims=True)
                gb_ref[...] += jnp.sum(dh, axis=0, keepdims=True)

            _for_row_chunks(tm, rows_fn)

    row = pl.BlockSpec((tm, d), lambda i, j, h: (i, 0))
    vec = pl.BlockSpec((1, d), lambda i, j, h: (0, 0))
    act_blk = pl.BlockSpec((tm, fs), lambda i, j, h: (i, j))
    w_blk = pl.BlockSpec((None, hd, fs), lambda i, j, h: (j, h, 0))
    return pl.pallas_call(
        body, name="dh1_ln_bwd", grid=(s // tm, N_CHIPS, 2),
        in_specs=[act_blk, w_blk, row, row, pl.BlockSpec((tm, 1), lambda i, j, h: (i, 0)), _VMEM, _ANY],
        out_specs=[row, vec, vec],
        out_shape=[jax.ShapeDtypeStruct((s, d), F32), jax.ShapeDtypeStruct((1, d), F32),
                   jax.ShapeDtypeStruct((1, d), F32)],
        scratch_shapes=[pltpu.VMEM((tm, hd), F32)] * 2,
        compiler_params=_params(("arbitrary", "arbitrary", "arbitrary")),
    )(d_gu, w_gu_g, dpre2, xhat1, rstd1, ln1_g, after)


def _dmixed_rms_bwd(dpre1, w_out_g, ac, rstd, g_ac):
    s, d = dpre1.shape
    hd = d // 2
    tm = min(TM, s)

    def body(dp_ref, w_ref, ac_ref, rs_ref, g_ref, dac_ref, gg_ref):
        i = pl.program_id(1)
        dm = _dot_nt(dp_ref[...].astype(BF16), w_ref[...])
        pre = ac_ref[...]
        r = rs_ref[...]
        gdm = dm * g_ref[...]
        dac_ref[...] = r * gdm - pre * (r * r * r) * jnp.mean(gdm * pre, axis=-1, keepdims=True)
        gg = jnp.sum(dm * pre * r, axis=0, keepdims=True)

        @pl.when(i == 0)
        def _():
            gg_ref[...] = gg

        @pl.when(i > 0)
        def _():
            gg_ref[...] += gg

    return pl.pallas_call(
        body, name="dmixed_rms_bwd", grid=(2, s // tm),
        in_specs=[pl.BlockSpec((tm, d), lambda h, i: (i, 0)),
                  pl.BlockSpec((hd, d), lambda h, i: (h, 0)),
                  pl.BlockSpec((tm, hd), lambda h, i: (i, h)),
                  pl.BlockSpec((None, tm, 1), lambda h, i: (h, i, 0)),
                  pl.BlockSpec((1, hd), lambda h, i: (0, h))],
        out_specs=[pl.BlockSpec((tm, hd), lambda h, i: (i, h)),
                   pl.BlockSpec((1, hd), lambda h, i: (0, h))],
        out_shape=[jax.ShapeDtypeStruct((s, d), F32), jax.ShapeDtypeStruct((1, d), F32)],
        compiler_params=_params(("arbitrary", "arbitrary")),
    )(dpre1, w_out_g, ac, rstd, g_ac)


def _attention_bwd(proj, d_ac, cos_t, sin_t, sinks, after):
    s = proj.shape[0]
    qw = GROUP * N_KV_HEADS * HEAD_DIM
    kvw = N_KV_HEADS * HEAD_DIM
    nb = s // WINDOW
    nq = GROUP * N_KV_HEADS

    def body(cur_ref, prev_ref, do_ref, cos_ref, sin_ref, cosp_ref, sinp_ref, sinks_ref, after_ref,
             dq_ref, dcur_ref, dprev_ref, dsink_ref):
        n = pl.program_id(0)
        first = n == 0
        q, k_all, v_all, cos_q, sin_q = _roped_qkv(cur_ref, prev_ref, cos_ref, sin_ref, cosp_ref, sinp_ref, qw, kvw)
        kk2s = [_pair_operand(k_all, h) for h in range(N_KV_HEADS)]
        vv2s = [_pair_operand(v_all, h) for h in range(N_KV_HEADS)]
        qps, probs, p_sinks = _all_probs(q, kk2s, first, sinks_ref)
        dops = [do_ref[:, pair * PAIR:(pair + 1) * PAIR].astype(BF16) for pair in range(N_PAIRS)]
        d_probs = jnp.concatenate([_dot_nt(dops[pair], vv2s[pair // (GROUP // 2)]) for pair in range(N_PAIRS)], axis=0)
        d_s, ds_sinks = [], []
        for t in range(2):
            cols = slice(t * KEYS, (t + 1) * KEYS)
            delta = jnp.sum(probs[:, cols] * d_probs[:, cols], axis=1, keepdims=True)
            d_s.append(probs[:, cols] * (d_probs[:, cols] - delta))
            ds_sinks.append(-p_sinks[t] * delta)
        d_s = jnp.concatenate(d_s, axis=1).astype(BF16)
        probs = probs.astype(BF16)
        dq_parts, dk_tiles, dv_tiles, dsink_parts = [], [], [], []
        for h in range(N_KV_HEADS):
            dkk2, dvv2 = None, None
            for p in range(GROUP // 2):
                pair = (GROUP // 2) * h + p
                rows = slice(pair * WINDOW, (pair + 1) * WINDOW)
                dq_parts.append(_dot(d_s[rows], kk2s[h]) * ATTN_SCALE)
                dk_term = _dot_tn(d_s[rows], qps[pair])
                dv_term = _dot_tn(probs[rows], dops[pair])
                dkk2 = dk_term if dkk2 is None else dkk2 + dk_term
                dvv2 = dv_term if dvv2 is None else dvv2 + dv_term
                dsink_parts.extend([jnp.sum(ds_sinks[t][rows], axis=0, keepdims=True) for t in range(2)])
            dk_tiles.append(_pair_grad(dkk2, h))
            dv_tiles.append(_pair_grad(dvv2, h))
        dq_ref[...] = _rope(jnp.concatenate(dq_parts, axis=1), cos_q, sin_q, -1.0)
        dk = jnp.concatenate([dk_tiles[0] + dk_tiles[1], dk_tiles[2] + dk_tiles[3]], axis=1)
        dv = jnp.concatenate([dv_tiles[0] + dv_tiles[1], dv_tiles[2] + dv_tiles[3]], axis=1)
        dprev_ref[...] = jnp.concatenate([dk[:WINDOW], dv[:WINDOW]], axis=1)
        dcur_ref[...] = jnp.concatenate([dk[WINDOW:], dv[WINDOW:]], axis=1)
        dsink = jnp.concatenate(dsink_parts, axis=1)

        @pl.when(first)
        def _():
            dsink_ref[...] = dsink

        @pl.when(n > 0)
        def _():
            dsink_ref[...] += dsink

    tbl = pl.BlockSpec((WINDOW, kvw), lambda n: (n, 0))
    tbl_prev = pl.BlockSpec((WINDOW, kvw), lambda n: (jnp.maximum(n - 1, 0), 0))
    kv_blk = pl.BlockSpec((WINDOW, 2 * kvw), lambda n: (n, 0))
    return pl.pallas_call(
        body, name="attention_bwd", grid=(nb,),
        in_specs=[pl.BlockSpec((WINDOW, qw + 2 * kvw), lambda n: (n, 0)),
                  pl.BlockSpec((WINDOW, 2 * kvw), lambda n: (jnp.maximum(n - 1, 0), (qw // (2 * kvw)))),
                  pl.BlockSpec((WINDOW, qw), lambda n: (n, 0)),
                  tbl, tbl, tbl_prev, tbl_prev, _VMEM, _ANY],
        out_specs=[pl.BlockSpec((WINDOW, qw), lambda n: (n, 0)), kv_blk, kv_blk,
                   pl.BlockSpec((1, nq), lambda n: (0, 0))],
        out_shape=[jax.ShapeDtypeStruct((s, qw), F32), jax.ShapeDtypeStruct((s, 2 * kvw), F32),
                   jax.ShapeDtypeStruct((s, 2 * kvw), F32), jax.ShapeDtypeStruct((1, nq), F32)],
        compiler_params=_params(("arbitrary",)),
    )(proj, proj, d_ac, cos_t, sin_t, cos_t, sin_t, sinks, after)


def _dproj_assemble(proj, d_ac, dq, dkv_cur, dkv_prev, cos_t, sin_t, cw_full):
    s, in_w = proj.shape
    cw = dq.shape[1]
    kvw = N_KV_HEADS * HEAD_DIM
    blk_w = in_w // 3
    tb = WINDOW
    nb = s // tb

    def body(lo_ref, hi_ref, lo_p_ref, hi_p_ref, lo_n_ref, hi_n_ref, dconv_ref, dconv_n_ref,
             dq_ref, dcur_ref, dprev_n_ref, cos_ref, sin_ref, cw_ref, dproj_ref, gcw_ref):
        i = pl.program_id(0)
        last = i == nb - 1
        c_gate, b_gate, u = _split_cbu(lo_ref[...], hi_ref[...], cw)
        c_p, _, u_p = _split_cbu(lo_p_ref[...], hi_p_ref[...], cw)
        _, b_n, _ = _split_cbu(lo_n_ref[...], hi_n_ref[...], cw)
        z = c_gate * u
        z_p = jnp.where(i == 0, 0.0, c_p * u_p)
        z1 = _shift_down(z, z_p, 1)
        z2 = _shift_down(z, z_p, 2)
        w0, w1, w2 = _conv_taps(cw_ref)
        y = w0 * z2 + w1 * z1 + w2 * z
        d_conv = dconv_ref[...]
        d_b = d_conv * y
        d_y = d_conv * b_gate
        d_y_n = jnp.where(last, 0.0, dconv_n_ref[...] * b_n[:dconv_n_ref.shape[0]])
        d_z = w2 * d_y + w1 * _shift_up(d_y, d_y_n, 1) + w0 * _shift_up(d_y, d_y_n, 2)
        d_c = d_z * u
        d_u = d_z * c_gate
        gcw = jnp.concatenate([jnp.sum(d_y * z2, axis=0, keepdims=True), jnp.sum(d_y * z1, axis=0, keepdims=True),
                               jnp.sum(d_y * z, axis=0, keepdims=True)], axis=0)

        @pl.when(i == 0)
        def _():
            gcw_ref[...] = gcw

        @pl.when(i > 0)
        def _():
            gcw_ref[...] += gcw

        dkv = dcur_ref[...] + jnp.where(last, 0.0, dprev_n_ref[...])
        dk = _rope(dkv[:, :kvw], cos_ref[...], sin_ref[...], -1.0)
        dproj_ref[...] = jnp.concatenate([dq_ref[...], dk, dkv[:, kvw:], d_c, d_b, d_u], axis=1).astype(BF16)

    prev_halo = lambda i: jnp.maximum(i * (tb // HALO_ROWS) - 1, 0)
    next_halo = lambda i: jnp.minimum((i + 1) * (tb // HALO_ROWS), s // HALO_ROWS - 1)
    next8 = lambda i: jnp.minimum((i + 1) * (tb // 8), s // 8 - 1)
    nxt = lambda i: jnp.minimum(i + 1, nb - 1)
    return pl.pallas_call(
        body, name="dproj_assemble", grid=(nb,),
        in_specs=[pl.BlockSpec((tb, blk_w), lambda i: (i, 1)),
                  pl.BlockSpec((tb, blk_w), lambda i: (i, 2)),
                  pl.BlockSpec((HALO_ROWS, blk_w), lambda i: (prev_halo(i), 1)),
                  pl.BlockSpec((HALO_ROWS, blk_w), lambda i: (prev_halo(i), 2)),
                  pl.BlockSpec((HALO_ROWS, blk_w), lambda i: (next_halo(i), 1)),
                  pl.BlockSpec((HALO_ROWS, blk_w), lambda i: (next_halo(i), 2)),
                  pl.BlockSpec((tb, cw), lambda i: (i, 1)),
                  pl.BlockSpec((8, cw), lambda i: (next8(i), 1)),
                  pl.BlockSpec((tb, cw), lambda i: (i, 0)),
                  pl.BlockSpec((tb, 2 * kvw), lambda i: (i, 0)),
                  pl.BlockSpec((tb, 2 * kvw), lambda i: (nxt(i), 0)),
                  pl.BlockSpec((tb, kvw), lambda i: (i, 0)),
                  pl.BlockSpec((tb, kvw), lambda i: (i, 0)),
                  _VMEM],
        out_specs=[pl.BlockSpec((tb, in_w), lambda i: (i, 0)),
                   pl.BlockSpec((3, cw), lambda i: (0, 0))],
        out_shape=[jax.ShapeDtypeStruct((s, in_w), BF16), jax.ShapeDtypeStruct((3, cw), F32)],
        compiler_params=_params(("arbitrary",)),
    )(proj, proj, proj, proj, proj, proj, d_ac, d_ac, dq, dkv_cur, dkv_prev, cos_t, sin_t, cw_full)


def _dx(d_proj, w_in_g, dpre1, after):
    s, in_w = d_proj.shape
    ns, d, ncol = w_in_g.shape
    tm = min(TM, s)

    def body(dp_ref, w_ref, r_ref, after_ref, o_ref, acc):
        j = pl.program_id(1)
        _accumulate(acc, lambda: _dot_nt(dp_ref[...], w_ref[...]), j, ns)

        @pl.when(j == ns - 1)
        def _():
            o_ref[...] = acc[...] + ALPHA * r_ref[...]

    return pl.pallas_call(
        body, name="dx", grid=(s // tm, ns),
        in_specs=[pl.BlockSpec((tm, ncol), lambda i, j: (i, j)),
                  pl.BlockSpec((None, d, ncol), lambda i, j: (j, 0, 0)),
                  pl.BlockSpec((tm, d), lambda i, j: (i, 0)), _ANY],
        out_specs=pl.BlockSpec((None, tm, d), lambda i, j: (0, i, 0)),
        out_shape=jax.ShapeDtypeStruct((1, s, d), F32),
        scratch_shapes=[pltpu.VMEM((tm, d), F32)],
        compiler_params=_params(("parallel", "arbitrary")),
    )(d_proj, w_in_g, dpre1, after)


def kernel(x, positions, w_in, conv_w, sinks, g_attn, g_conv, w_out, ln1_g, ln1_b, w_gate, w_up, w_down, ln2_g, ln2_b, loss_target, m_w_in, m_conv_w, m_sinks, m_g_attn, m_g_conv, m_w_out, m_ln1_g, m_ln1_b, m_w_gate, m_w_up, m_w_down, m_ln2_g, m_ln2_b, v_w_in, v_conv_w, v_sinks, v_g_attn, v_g_conv, v_w_out, v_ln1_g, v_ln1_b, v_w_gate, v_w_up, v_w_down, v_ln2_g, v_ln2_b):
    s = x.shape[1]
    d = x.shape[2]

    chip_vec = _chip_id(lax.axis_index("x"), lax.axis_index("y")).astype(jnp.int32).reshape(1)
    wnames = ["w_in", "w_out", "w_gu", "w_down"]
    buf_in = _cast_weight(w_in, chip_vec, chip_vec, "cast_w_in")
    flight_in, token_in = _gather_start([buf_in], chip_vec, "gather_start_w_in")
    cw_buf = lax.dynamic_update_slice(jnp.zeros((N_CHIPS,) + conv_w.shape[1:], F32), conv_w, (chip_vec[0], 0, 0))
    cw_flight = _flight_start("conv_w_start", [cw_buf], _conv_w_plan(), 3, token_in)
    started = cw_flight[2][0]
    buf_gu = _cast_weight(w_gate, chip_vec, started, "cast_w_gate", 0, 2)
    buf_gu = _cast_weight(w_up, chip_vec, buf_gu, "cast_w_up", 1, 2)
    bufs = [_cast_weight(w_out, chip_vec, started, "cast_w_out"), buf_gu,
            _cast_weight(w_down, chip_vec, started, "cast_w_down")]
    flights_rest, token = _gather_start(bufs, token_in, "gather_start_rest")
    flights = flight_in + flights_rest

    def gathered(i, after):
        send_sems, recv_sems, buf = flights[i]
        buf = _gather_wait(send_sems, recv_sems, buf, after, "gather_wait_" + wnames[i])
        return _sibling_fill(buf, "sibling_fill_" + wnames[i])

    g_ac = jnp.concatenate([g_attn, g_conv], axis=1)

    proj_own = _in_proj(x, _after(flights[0][2], token), chip_vec, 1, None, "in_proj_own")
    cos_t, sin_t = _rope_tables(positions.reshape(s, 1) + token[0:1, 0:1].astype(jnp.int32))
    w_in_g = gathered(0, _after(cos_t, proj_own))
    proj = _in_proj(x, w_in_g, chip_vec + 1, N_CHIPS - 1, proj_own, "in_proj_rest")
    send_sems, recv_sems, buf_out = flights[1]
    buf_out = _gather_wait(send_sems, recv_sems, buf_out, proj, "gather_wait_w_out")
    fill_out = _flight_start("fill_start_w_out", [buf_out], _fill_plan(1), 3, chip_vec)
    attn = _attention_fwd(_after(proj, fill_out[2][0]), cos_t, sin_t, sinks)
    (cw_full,) = _flight_wait("conv_w_wait", cw_flight, _conv_w_plan(), attn)
    mixed, ac, rstd_ac = _conv_norm(proj, attn, cw_full, g_ac)
    (w_out_g,) = _flight_wait("fill_wait_w_out", fill_out, _fill_plan(1), mixed)
    w_out_full = w_out_g.reshape(d, d)
    xhat1, h1, rstd1 = _out_proj_ln(mixed, w_out_full, x, ln1_g, ln1_b)
    own = _gate_up(h1, flights[2][2], chip_vec, 1, None, "gate_up_own")
    w_gu_g = gathered(2, own[0])
    act, ab = _gate_up(h1, w_gu_g, chip_vec + 1, N_CHIPS - 1, own, "gate_up_rest")
    w_down_full = gathered(3, act).reshape(-1, d)
    dpre2, dpre2_16, loss_part, g_ln2_g, g_ln2_b = _down_ln_loss(act, w_down_full, xhat1, ln1_g, ln1_b, ln2_g, ln2_b,
                                                                 loss_target)

    cvec = lax.axis_index("c").astype(jnp.int32).reshape(1)

    def exchange_begin(parts, nme):
        bufs = []
        for part in parts:
            ns, r, cdim = part.shape
            bufs.extend([part, lax.empty((ns, r // 2, cdim), part.dtype)])
        return _flight_start("exchange_start_" + nme, bufs, _exchange_plan(len(parts)), len(parts), cvec)

    def exchange_end(flight, n_parts, after, nme):
        bufs = _flight_wait("exchange_wait_" + nme, flight, _exchange_plan(n_parts), after)
        return [(bufs[2 * w], bufs[2 * w + 1]) for w in range(n_parts)]

    def scatter_begin(part, got, nme):
        return _scatter_start(_add_halves(part, got, cvec, "add_halves_" + nme), "scatter_start_" + nme)

    d_gu = _dact_silu_bwd(dpre2_16, w_down_full, ab)
    p_down = _grad_rows(act, dpre2_16, d_gu, "grad_w_down")
    x_down = exchange_begin([p_down], "w_down")
    (p_gu,) = _grad_cols(h1, [d_gu], x_down[2][0], "grad_w_gate_up")
    ((p_down, got),) = exchange_end(x_down, 1, p_gu, "w_down")
    f_down = scatter_begin(p_down, got, "w_down")
    x_gu = exchange_begin([_after(p_gu, f_down[2])], "w_gu")
    dpre1, g_ln1_g, g_ln1_b = _dh1_ln_bwd(d_gu, w_gu_g, dpre2, xhat1, rstd1, ln1_g, x_gu[2][0])
    ((p_gu, got),) = exchange_end(x_gu, 1, dpre1, "w_gu")
    f_gu = scatter_begin(p_gu, got, "w_gu")
    d_ac, g_g_ac = _dmixed_rms_bwd(_after(dpre1, f_gu[2]), w_out_full, ac, rstd_ac, g_ac)
    p_out = _grad_rows(mixed, dpre1, d_ac, "grad_w_out")
    x_out = exchange_begin([p_out], "w_out")
    dq, dkv_cur, dkv_prev, g_sinks = _attention_bwd(proj, d_ac, cos_t, sin_t, sinks, x_out[2][0])
    ((p_out, got),) = exchange_end(x_out, 1, dq, "w_out")
    f_out = scatter_begin(p_out, got, "w_out")
    d_proj, g_conv_w = _dproj_assemble(proj, _after(d_ac, f_out[2]), dq, dkv_cur, dkv_prev, cos_t, sin_t, cw_full)
    (p_in,) = _grad_cols(x, [d_proj], d_proj, "grad_w_in", a_3d=True)
    x_in = exchange_begin([p_in], "w_in")
    grad_x = _dx(d_proj, w_in_g, dpre1, x_in[2][0])
    red = _allreduce_small(g_ln2_g, g_ln2_b, g_ln1_g, g_ln1_b, g_g_ac, g_conv_w, g_sinks, loss_part, grad_x)
    ((p_in, got),) = exchange_end(x_in, 1, red, "w_in")
    f_in = scatter_begin(p_in, got, "w_in")

    pos_vec = jnp.concatenate([chip_vec, cvec])
    shards = {"w_in": (w_in, m_w_in, v_w_in), "w_out": (w_out, m_w_out, v_w_out), "w_gate": (w_gate, m_w_gate, v_w_gate),
              "w_up": (w_up, m_w_up, v_w_up), "w_down": (w_down, m_w_down, v_w_down)}
    early = [("w_down", ["w_down"]), ("w_gu", ["w_gate", "w_up"]), ("w_out", ["w_out"])]
    after = f_in[2]
    completing = {}
    for (nme, _), f in zip(early, [f_down, f_gu, f_out]):
        sums, land = _scatter_wait(*f, after, "scatter_wait_" + nme)
        completing[nme] = _flight_start("complete_start_" + nme, [sums, land], _complete_plan(1), 4, cvec)
        after = completing[nme][2][1]
    big = {}
    for nme, members in early:
        sums, land = _flight_wait("complete_wait_" + nme, completing[nme], _complete_plan(1), after)
        for col_block, member in enumerate(members):
            big[member] = _adamw_shard(*shards[member], land, sums, pos_vec, "adamw_" + member, col_block)
            after = big[member][0]
    sums, land = _scatter_wait(*f_in, after, "scatter_wait_w_in")
    (land,) = _complete_chip_sums([sums], [land])
    big["w_in"] = _adamw_shard(*shards["w_in"], land, sums, pos_vec, "adamw_w_in")
    small = _adamw_small(red, {
        "sinks": (sinks, m_sinks, v_sinks), "g_attn": (g_attn, m_g_attn, v_g_attn),
        "g_conv": (g_conv, m_g_conv, v_g_conv), "ln1_g": (ln1_g, m_ln1_g, v_ln1_g),
        "ln1_b": (ln1_b, m_ln1_b, v_ln1_b), "ln2_g": (ln2_g, m_ln2_g, v_ln2_g),
        "ln2_b": (ln2_b, m_ln2_b, v_ln2_b), "conv_w": (conv_w, m_conv_w, v_conv_w)})
    res = {**big, **small}
    order = ["w_in", "conv_w", "sinks", "g_attn", "g_conv", "w_out", "ln1_g", "ln1_b", "w_gate", "w_up", "w_down",
             "ln2_g", "ln2_b"]
    loss = red[6, d // 2 + 128]
    return (loss, grad_x, *[res[n][0] for n in order], *[res[n][1] for n in order],
            *[res[n][2] for n in order], *[res[n][3] for n in order])
```

```python
import functools

import numpy as np
import jax
import jax.numpy as jnp
from jax import lax
from jax.experimental import pallas as pl
from jax.experimental.pallas import tpu as pltpu

F32 = jnp.float32
BF16 = jnp.bfloat16
MESH = pl.DeviceIdType.MESH

HEAD_DIM = 64
N_KV_HEADS = 4
GROUP = 4
WINDOW = 128
ROT_DIM = 16
ROPE_THETA = 500000.0
ATTN_SCALE = HEAD_DIM ** -0.5
ALPHA = 2.0 ** 0.25
LN_EPS = 1e-5
RMS_EPS = 1e-6
ADAM_LR = 0.001
ADAM_B1 = 0.9
ADAM_B2 = 0.999
ADAM_EPS = 1e-08
ADAM_WD = 0.01
ADAM_STEP = 10
N_CHIPS = 4
NEG_BIG = -1e30

V7X_VMEM_BYTES = 64 * 1024 * 1024
VMEM_LIMIT = V7X_VMEM_BYTES - 6 * 1024 * 1024

TM = 512
TK_TOK = 1024
TB_CONV = 256
TR_ELT = 256
ROW_CHUNK = 128
HALO_ROWS = 16


def _params(sem):
    return pltpu.CompilerParams(dimension_semantics=sem, vmem_limit_bytes=VMEM_LIMIT)


def _row_tile(rows, target):
    best = None
    for t in range(16, min(rows, target) + 1, 16):
        if rows % t == 0:
            best = t
    assert best is not None, (rows, target)
    return best


def _dot(a, b):
    return jnp.dot(a, b, preferred_element_type=F32)


def _dot_nt(a, b):
    return lax.dot_general(a, b, (((1,), (1,)), ((), ())), preferred_element_type=F32)


def _dot_tn(a, b):
    return lax.dot_general(a, b, (((0,), (0,)), ((), ())), preferred_element_type=F32)


def _mesh_pos():
    x, y, c = lax.axis_index("x"), lax.axis_index("y"), lax.axis_index("c")
    chips = [(1 - x, y), (x, 1 - y), (1 - x, 1 - y)]
    return x, y, c, chips


def _chip_id(px, py):
    return 2 * px + py


def _rope(t, cos, sgn_sin, sign):
    w = t.shape[1]
    lane = lax.broadcasted_iota(jnp.int32, t.shape, 1) & (HEAD_DIM - 1)
    partner = jnp.where(lane < ROT_DIM // 2, pltpu.roll(t, w - ROT_DIM // 2, 1), pltpu.roll(t, ROT_DIM // 2, 1))
    return t * cos + sign * (partner * sgn_sin)


def _tile_lanes(t, n):
    return jnp.concatenate([t] * n, axis=1)


def _sigmoid(g):
    return 1.0 / (1.0 + jnp.exp(-g))


def _for_row_chunks(n_rows, fn):
    def step(r, carry):
        fn(pl.ds(pl.multiple_of(r * ROW_CHUNK, ROW_CHUNK), ROW_CHUNK))
        return carry

    lax.fori_loop(0, n_rows // ROW_CHUNK, step, 0)


def _accumulate(acc, make_val, k, nk):
    if nk == 1:
        acc[...] = make_val()
        return

    @pl.when(k == 0)
    def _():
        acc[...] = jnp.zeros_like(acc)

    acc[...] += make_val()


def _ln_fwd(pre):
    mu = jnp.mean(pre, axis=-1, keepdims=True)
    cen = pre - mu
    var = jnp.mean(cen * cen, axis=-1, keepdims=True)
    rstd = lax.rsqrt(var + LN_EPS)
    return cen * rstd, rstd


def _ln_bwd(dy, xhat, rstd, g):
    dxhat = dy * g
    m1 = jnp.mean(dxhat, axis=-1, keepdims=True)
    m2 = jnp.mean(dxhat * xhat, axis=-1, keepdims=True)
    return rstd * (dxhat - m1 - xhat * m2)


def _cast_weight(w, chip_vec, after, name, col_block=0, n_col_blocks=1):
    _, r, c = w.shape
    tr = _row_tile(r, TR_ELT)

    def body(chip_ref, w_ref, after_ref, o_ref):
        o_ref[...] = w_ref[...].astype(BF16)

    grid_spec = pltpu.PrefetchScalarGridSpec(
        num_scalar_prefetch=1, grid=(r // tr,),
        in_specs=[pl.BlockSpec((None, tr, c), lambda i, chip_ref: (0, i, 0)), _ANY],
        out_specs=pl.BlockSpec((None, tr, c), lambda i, chip_ref: (chip_ref[0], i, col_block)))
    return pl.pallas_call(
        body, name=name, grid_spec=grid_spec,
        out_shape=jax.ShapeDtypeStruct((N_CHIPS, r, n_col_blocks * c), BF16),
        input_output_aliases={2: 0} if col_block else {},
        compiler_params=_params(("parallel",)),
    )(chip_vec, w, after)


_HBM = pl.BlockSpec(memory_space=pltpu.HBM)
_VMEM = pl.BlockSpec(memory_space=pltpu.VMEM)


_SEM = pl.BlockSpec(memory_space=pltpu.SEMAPHORE)
_ANY = pl.BlockSpec(memory_space=pl.ANY)
_EFFECT = pltpu.SideEffectType.DATAFLOW_SIDE_EFFECTING


def _chip_copy(buf, k, chip_of_src, half_rows, send_sems, recv_sems, to):
    part = buf.at[chip_of_src, half_rows]
    return pltpu.make_async_remote_copy(
        src_ref=part, dst_ref=part, send_sem=send_sems.at[k], recv_sem=recv_sems.at[k], device_id=to, device_id_type=MESH)


def _half_rows(buf, which):
    hr = buf.shape[1] // 2
    return pl.ds(which * hr, hr)


def _after(value, dep):
    return lax.optimization_barrier((value, dep))[0]


def _flight_start(name, bufs, plan, n_sems, after):
    n = len(bufs)

    def body(*refs):
        sends, _ = plan(refs[:n], refs[n + 1], refs[n + 2])
        for cp in sends:
            cp.start()

    outs = pl.pallas_call(
        body, name=name,
        in_specs=[_HBM] * n + [_ANY], out_specs=[_SEM, _SEM] + [_HBM] * n,
        out_shape=[pltpu.SemaphoreType.DMA((n_sems,))] * 2 + [pltpu.HBM(b.shape, b.dtype) for b in bufs],
        input_output_aliases={i: 2 + i for i in range(n)},
        compiler_params=pltpu.CompilerParams(has_side_effects=_EFFECT),
    )(*[pltpu.with_memory_space_constraint(b, pltpu.HBM) for b in bufs], after)
    return outs[0], outs[1], list(outs[2:])


def _flight_wait(name, flight, plan, after):
    send_sems, recv_sems, bufs = flight
    n = len(bufs)

    def body(*refs):
        sends, recvs = plan(refs[:n], refs[n], refs[n + 1])
        for cp in sends:
            cp.wait_send()
        for cp in recvs:
            cp.wait_recv()

    outs = pl.pallas_call(
        body, name=name,
        in_specs=[_HBM] * n + [_SEM, _SEM, _ANY], out_specs=[_HBM] * n,
        out_shape=[pltpu.HBM(b.shape, b.dtype) for b in bufs],
        input_output_aliases={i: i for i in range(n)},
        compiler_params=pltpu.CompilerParams(has_side_effects=_EFFECT),
    )(*bufs, send_sems, recv_sems, after)
    return list(outs)


def _fill_plan(n_bufs):
    def plan(refs, send_sems, recv_sems):
        x, y, c, chips = _mesh_pos()
        sibling = (x, y, 1 - c)
        sends, recvs = [], []
        for w in range(n_bufs):
            for k, chip in enumerate(chips):
                slot = _chip_id(*chip)
                sends.append(_chip_copy(refs[w], 3 * w + k, slot, _half_rows(refs[w], c), send_sems, recv_sems, sibling))
                recvs.append(_chip_copy(refs[w], 3 * w + k, slot, _half_rows(refs[w], 1 - c), send_sems, recv_sems,
                                        sibling))
        return sends, recvs
    return plan


def _conv_w_plan():
    def plan(refs, send_sems, recv_sems):
        x, y, c, chips = _mesh_pos()
        me = _chip_id(x, y)
        (buf,) = refs
        sends, recvs = [], []
        for k, chip in enumerate(chips):
            for slot, into in ((me, sends), (_chip_id(*chip), recvs)):
                into.append(pltpu.make_async_remote_copy(
                    src_ref=buf.at[slot], dst_ref=buf.at[slot], send_sem=send_sems.at[k], recv_sem=recv_sems.at[k],
                    device_id=(*chip, c), device_id_type=MESH))
        return sends, recvs
    return plan


def _exchange_plan(n_parts):
    def plan(refs, send_sems, recv_sems):
        x, y, c, _ = _mesh_pos()
        copies = []
        for w in range(n_parts):
            part, got = refs[2 * w], refs[2 * w + 1]
            hr = got.shape[1]
            copies.append(pltpu.make_async_remote_copy(
                src_ref=part.at[:, pl.ds((1 - c) * hr, hr)], dst_ref=got, send_sem=send_sems.at[w],
                recv_sem=recv_sems.at[w], device_id=(x, y, 1 - c), device_id_type=MESH))
        return copies, copies
    return plan


def _gather_start(bufs, after, name):
    n = len(bufs)

    def body(*refs):
        ins = refs[:n]
        sends, recvs = refs[n + 1:2 * n + 1], refs[2 * n + 1:3 * n + 1]
        token = refs[4 * n + 1]
        x, y, c, chips = _mesh_pos()
        me = _chip_id(x, y)
        for w in range(n):
            for k, chip in enumerate(chips):
                _chip_copy(ins[w], k, me, _half_rows(ins[w], c), sends[w], recvs[w], (*chip, c)).start()
        token[...] = jnp.zeros_like(token)

    outs = pl.pallas_call(
        body, name=name,
        in_specs=[_HBM] * n + [_ANY],
        out_specs=[_SEM] * (2 * n) + [_HBM] * n + [_VMEM],
        out_shape=[pltpu.SemaphoreType.DMA((3,))] * (2 * n) + [pltpu.HBM(b.shape, b.dtype) for b in bufs]
        + [jax.ShapeDtypeStruct((8, 128), F32)],
        input_output_aliases={w: 2 * n + w for w in range(n)},
        compiler_params=pltpu.CompilerParams(has_side_effects=_EFFECT),
    )(*[pltpu.with_memory_space_constraint(b, pltpu.HBM) for b in bufs], after)
    return [(outs[w], outs[n + w], outs[2 * n + w]) for w in range(n)], outs[3 * n]


def _gather_wait(send_sems, recv_sems, buf, after, name):
    def body(buf_ref, send_ref, recv_ref, after_ref, out_ref):
        x, y, c, chips = _mesh_pos()
        me = _chip_id(x, y)
        for k, chip in enumerate(chips):
            _chip_copy(buf_ref, k, me, _half_rows(buf_ref, c), send_ref, recv_ref, (*chip, c)).wait_send()
        for k, chip in enumerate(chips):
            _chip_copy(buf_ref, k, _chip_id(*chip), _half_rows(buf_ref, c), send_ref, recv_ref, (*chip, c)).wait_recv()

    return pl.pallas_call(
        body, name=name,
        in_specs=[_HBM, _SEM, _SEM, _ANY], out_specs=_HBM,
        out_shape=pltpu.HBM(buf.shape, buf.dtype),
        input_output_aliases={0: 0},
        compiler_params=pltpu.CompilerParams(has_side_effects=_EFFECT),
    )(buf, send_sems, recv_sems, after)


def _sibling_fill(buf, name, own_too=False):
    n_copies = 4 if own_too else 3

    def body(buf_ref, out_ref, send_sems, recv_sems):
        x, y, c, chips = _mesh_pos()
        sibling = (x, y, 1 - c)
        slots = [_chip_id(*chip) for chip in chips] + ([_chip_id(x, y)] if own_too else [])
        copies = []
        for k, slot in enumerate(slots):
            cp = _chip_copy(out_ref, k, slot, _half_rows(out_ref, c), send_sems, recv_sems, sibling)
            cp.start()
            copies.append(cp)
        for k, slot in enumerate(slots):
            _chip_copy(out_ref, k, slot, _half_rows(out_ref, 1 - c), send_sems, recv_sems, sibling).wait_recv()
        for cp in copies:
            cp.wait_send()

    return pl.pallas_call(
        body, name=name,
        in_specs=[_HBM], out_specs=_HBM,
        out_shape=jax.ShapeDtypeStruct(buf.shape, buf.dtype),
        input_output_aliases={0: 0},
        scratch_shapes=[pltpu.SemaphoreType.DMA((n_copies,)), pltpu.SemaphoreType.DMA((n_copies,))],
    )(buf)


def _allgather_conv_w(cw):
    _, kw, cs = cw.shape

    def body(cw_ref, out_ref, send_sems, recv_sems):
        x, y, c, chips = _mesh_pos()
        me = _chip_id(x, y)
        out_ref[pl.ds(me, 1)] = cw_ref[...]
        copies = []
        for k, chip in enumerate(chips):
            cp = pltpu.make_async_remote_copy(
                src_ref=cw_ref.at[0], dst_ref=out_ref.at[me], send_sem=send_sems.at[k], recv_sem=recv_sems.at[k],
                device_id=(*chip, c), device_id_type=MESH)
            cp.start()
            copies.append(cp)
        for k, chip in enumerate(chips):
            pltpu.make_async_remote_copy(
                src_ref=cw_ref.at[0], dst_ref=out_ref.at[_chip_id(*chip)], send_sem=send_sems.at[k],
                recv_sem=recv_sems.at[k], device_id=(*chip, c), device_id_type=MESH).wait_recv()
        for cp in copies:
            cp.wait_send()

    return pl.pallas_call(
        body, name="allgather_conv_w",
        in_specs=[_VMEM], out_specs=_VMEM,
        out_shape=jax.ShapeDtypeStruct((N_CHIPS, kw, cs), F32),
        scratch_shapes=[pltpu.SemaphoreType.DMA((3,)), pltpu.SemaphoreType.DMA((3,))],
    )(cw)


def _exchange_halves(parts, after, name):
    n = len(parts)
    shapes = [p.shape for p in parts]

    def body(*refs):
        ins, outs = refs[:n], refs[n + 1:2 * n + 1]
        send_sems, recv_sems = refs[2 * n + 1:]
        x, y, c, _ = _mesh_pos()
        copies = []
        for w in range(n):
            hr = shapes[w][1] // 2
            cp = pltpu.make_async_remote_copy(
                src_ref=ins[w].at[:, pl.ds((1 - c) * hr, hr)], dst_ref=outs[w],
                send_sem=send_sems.at[w], recv_sem=recv_sems.at[w],
                device_id=(x, y, 1 - c), device_id_type=MESH)
            cp.start()
            copies.append(cp)
        for cp in copies:
            cp.wait()

    return pl.pallas_call(
        body, name=name,
        in_specs=[_HBM] * n + [_ANY], out_specs=[_HBM] * n,
        out_shape=[jax.ShapeDtypeStruct((s[0], s[1] // 2, s[2]), BF16) for s in shapes],
        scratch_shapes=[pltpu.SemaphoreType.DMA((n,)), pltpu.SemaphoreType.DMA((n,))],
    )(*parts, after)


def _add_halves(part, got, cvec, name):
    ns, r, cdim = part.shape
    hr = r // 2
    tr = _row_tile(hr, TR_ELT)
    nblk = hr // tr

    def body(c_ref, a_ref, b_ref, o_ref):
        o_ref[...] = (a_ref[...].astype(F32) + b_ref[...].astype(F32)).astype(BF16)

    grid_spec = pltpu.PrefetchScalarGridSpec(
        num_scalar_prefetch=1, grid=(ns, nblk),
        in_specs=[pl.BlockSpec((None, tr, cdim), lambda s, i, c_ref: (s, c_ref[0] * nblk + i, 0)),
                  pl.BlockSpec((None, tr, cdim), lambda s, i, c_ref: (s, i, 0))],
        out_specs=pl.BlockSpec((None, tr, cdim), lambda s, i, c_ref: (s, i, 0)))
    return pl.pallas_call(
        body, name=name, grid_spec=grid_spec,
        out_shape=jax.ShapeDtypeStruct((ns, hr, cdim), BF16),
        compiler_params=_params(("parallel", "parallel")),
    )(cvec, part, got)


def _scatter_copy(sums_ref, land_ref, k, src_slot, dst_slot, c, send_sems, recv_sems, to):
    return pltpu.make_async_remote_copy(
        src_ref=sums_ref.at[src_slot], dst_ref=land_ref.at[dst_slot, _half_rows(land_ref, c)],
        send_sem=send_sems.at[k], recv_sem=recv_sems.at[k], device_id=to, device_id_type=MESH)


def _scatter_start(sums, name):
    ns, hr, cdim = sums.shape
    land = lax.empty((ns, 2 * hr, cdim), sums.dtype)

    def body(sums_ref, land_ref, send_sems, recv_sems, sums_thru, land_thru):
        x, y, c, chips = _mesh_pos()
        me = _chip_id(x, y)
        for k, chip in enumerate(chips):
            _scatter_copy(sums_ref, land_ref, k, _chip_id(*chip), me, c, send_sems, recv_sems, (*chip, c)).start()

    return pl.pallas_call(
        body, name=name,
        in_specs=[_HBM, _HBM], out_specs=[_SEM, _SEM, _HBM, _HBM],
        out_shape=[pltpu.SemaphoreType.DMA((3,)), pltpu.SemaphoreType.DMA((3,)),
                   pltpu.HBM(sums.shape, sums.dtype), pltpu.HBM(land.shape, land.dtype)],
        input_output_aliases={0: 2, 1: 3},
        compiler_params=pltpu.CompilerParams(has_side_effects=_EFFECT),
    )(pltpu.with_memory_space_constraint(sums, pltpu.HBM), pltpu.with_memory_space_constraint(land, pltpu.HBM))


def _scatter_wait(send_sems, recv_sems, sums, land, after, name):
    def body(sums_ref, land_ref, send_ref, recv_ref, after_ref, sums_out, land_out):
        x, y, c, chips = _mesh_pos()
        me = _chip_id(x, y)
        for k, chip in enumerate(chips):
            _scatter_copy(sums_ref, land_ref, k, _chip_id(*chip), me, c, send_ref, recv_ref, (*chip, c)).wait_send()
        for k, chip in enumerate(chips):
            _scatter_copy(sums_ref, land_ref, k, me, _chip_id(*chip), c, send_ref, recv_ref, (*chip, c)).wait_recv()

    return pl.pallas_call(
        body, name=name,
        in_specs=[_HBM, _HBM, _SEM, _SEM, _ANY], out_specs=[_HBM, _HBM],
        out_shape=[pltpu.HBM(sums.shape, sums.dtype), pltpu.HBM(land.shape, land.dtype)],
        input_output_aliases={0: 0, 1: 1},
        compiler_params=pltpu.CompilerParams(has_side_effects=_EFFECT),
    )(sums, land, send_sems, recv_sems, after)


def _complete_plan(n_weights):
    def plan(refs, send_sems, recv_sems):
        x, y, c, chips = _mesh_pos()
        me = _chip_id(x, y)
        sibling = (x, y, 1 - c)
        sends, recvs = [], []
        for w in range(n_weights):
            sums, land = refs[2 * w], refs[2 * w + 1]
            sends.append(_scatter_copy(sums, land, 4 * w + 3, me, me, c, send_sems, recv_sems, sibling))
            recvs.append(_scatter_copy(sums, land, 4 * w + 3, me, me, 1 - c, send_sems, recv_sems, sibling))
            for k, chip in enumerate(chips):
                slot = _chip_id(*chip)
                sends.append(_chip_copy(land, 4 * w + k, slot, _half_rows(land, c), send_sems, recv_sems, sibling))
                recvs.append(_chip_copy(land, 4 * w + k, slot, _half_rows(land, 1 - c), send_sems, recv_sems, sibling))
        return sends, recvs
    return plan


def _complete_chip_sums(sums, lands):
    n = len(sums)

    def body(*refs):
        sums_refs, outs = refs[:n], refs[2 * n:3 * n]
        send_sems, recv_sems = refs[3 * n:]
        x, y, c, chips = _mesh_pos()
        me = _chip_id(x, y)
        sibling = (x, y, 1 - c)
        slots = [_chip_id(*chip) for chip in chips]
        sent = []
        for w in range(n):
            out = outs[w]
            cp = _scatter_copy(sums_refs[w], out, 3, me, me, c, send_sems.at[w], recv_sems.at[w], sibling)
            cp.start()
            sent.append(cp)
            for k, slot in enumerate(slots):
                cp = _chip_copy(out, k, slot, _half_rows(out, c), send_sems.at[w], recv_sems.at[w], sibling)
                cp.start()
                sent.append(cp)
        for w in range(n):
            out = outs[w]
            _scatter_copy(sums_refs[w], out, 3, me, me, 1 - c, send_sems.at[w], recv_sems.at[w], sibling).wait_recv()
            for k, slot in enumerate(slots):
                _chip_copy(out, k, slot, _half_rows(out, 1 - c), send_sems.at[w], recv_sems.at[w], sibling).wait_recv()
        for cp in sent:
            cp.wait_send()

    return pl.pallas_call(
        body, name="complete_chip_sums",
        in_specs=[_HBM] * (2 * n), out_specs=[_HBM] * n,
        out_shape=[jax.ShapeDtypeStruct(b.shape, b.dtype) for b in lands],
        input_output_aliases={n + w: w for w in range(n)},
        scratch_shapes=[pltpu.SemaphoreType.DMA((n, 4)), pltpu.SemaphoreType.DMA((n, 4))],
    )(*sums, *lands)


SMALL_ROWS = 8


def _allreduce_small(gl2g, gl2b, gl1g, gl1b, g_ac, gcw, gsink, loss, after):
    d = gl2g.shape[1]
    hd = d // 2
    nq = gsink.shape[1]

    def body(a_ref, b_ref, c_ref, d_ref, e_ref, cw_ref, sk_ref, ls_ref, after_ref, out_ref, mine, gath, send_sems,
             recv_sems):
        x, y, c, _ = _mesh_pos()
        me = 4 * x + 2 * y + c
        mine[...] = jnp.zeros_like(mine)
        mine[0:1, :] = a_ref[...]
        mine[1:2, :] = b_ref[...]
        mine[2:3, :] = c_ref[...]
        mine[3:4, :] = d_ref[...]
        mine[4:5, :] = e_ref[...]
        mine[5:6, 0:hd] = cw_ref[0:1, :]
        mine[5:6, hd:d] = cw_ref[1:2, :]
        mine[6:7, 0:hd] = cw_ref[2:3, :]
        mine[6:7, hd:hd + nq] = sk_ref[...]
        mine[6:7, hd + 128:hd + 256] = ls_ref[...]
        gath[pl.ds(me, 1)] = mine[...][None]
        copies = []
        for r in range(1, 8):
            peer = ((1 - x) if r & 4 else x, (1 - y) if r & 2 else y, (1 - c) if r & 1 else c)
            cp = pltpu.make_async_remote_copy(
                src_ref=mine, dst_ref=gath.at[me], send_sem=send_sems.at[r - 1], recv_sem=recv_sems.at[r - 1],
                device_id=peer, device_id_type=MESH)
            cp.start()
            copies.append(cp)
        for r in range(1, 8):
            peer = ((1 - x) if r & 4 else x, (1 - y) if r & 2 else y, (1 - c) if r & 1 else c)
            peer_id = 4 * peer[0] + 2 * peer[1] + peer[2]
            pltpu.make_async_remote_copy(
                src_ref=mine, dst_ref=gath.at[peer_id], send_sem=send_sems.at[r - 1], recv_sem=recv_sems.at[r - 1],
                device_id=peer, device_id_type=MESH).wait_recv()
        for cp in copies:
            cp.wait_send()
        total = gath[0]
        for dev in range(1, 8):
            total = total + gath[dev]
        out_ref[...] = total

    return pl.pallas_call(
        body, name="allreduce_small",
        in_specs=[_VMEM] * 8 + [_ANY], out_specs=_VMEM,
        out_shape=jax.ShapeDtypeStruct((SMALL_ROWS, d), F32),
        scratch_shapes=[pltpu.VMEM((SMALL_ROWS, d), F32), pltpu.VMEM((8, SMALL_ROWS, d), F32),
                        pltpu.SemaphoreType.DMA((7,)), pltpu.SemaphoreType.DMA((7,))],
    )(gl2g, gl2b, gl1g, gl1b, g_ac, gcw, gsink, loss, after)


def _adamw(w, g, m, v):
    m = ADAM_B1 * m + (1.0 - ADAM_B1) * g
    v = ADAM_B2 * v + (1.0 - ADAM_B2) * (g * g)
    m_hat = m / (1.0 - ADAM_B1 ** ADAM_STEP)
    v_hat = v / (1.0 - ADAM_B2 ** ADAM_STEP)
    delta = -ADAM_LR * (m_hat / (jnp.sqrt(v_hat) + ADAM_EPS) + ADAM_WD * w)
    return delta, m, v


def _adamw_shard(w, m, v, land, own, pos_vec, name, col_block=0):
    _, r, c = w.shape
    hr = r // 2
    tr = _row_tile(hr, TR_ELT)
    nh = hr // tr

    def body(pos_ref, w_ref, m_ref, v_ref, l0, l1, l2, l3, own_ref, g_out, d_out, m_out, v_out):
        i = pl.program_id(0)
        mine = (i // nh) == pos_ref[1]
        own_blk = own_ref[...].astype(F32)
        g = None
        for s, l_ref in enumerate([l0, l1, l2, l3]):
            term = jnp.where(mine & (pos_ref[0] == s), own_blk, l_ref[...].astype(F32))
            g = term if g is None else g + term
        delta, nm, nv = _adamw(w_ref[...], g, m_ref[...], v_ref[...])
        g_out[...] = g
        d_out[...] = delta
        m_out[...] = nm
        v_out[...] = nv

    def land_spec(s):
        def index(i, pos_ref):
            skip = (pos_ref[0] == s) & ((i // nh) == pos_ref[1])
            return (s, jnp.where(skip, (i + nh) % (2 * nh), i), col_block)
        return pl.BlockSpec((None, tr, c), index)

    blk = pl.BlockSpec((None, tr, c), lambda i, pos_ref: (0, i, 0))
    grid_spec = pltpu.PrefetchScalarGridSpec(
        num_scalar_prefetch=1, grid=(2 * nh,),
        in_specs=[blk, blk, blk] + [land_spec(s) for s in range(N_CHIPS)]
        + [pl.BlockSpec((None, tr, c), lambda i, pos_ref: (pos_ref[0], i % nh, col_block))],
        out_specs=[blk] * 4)
    return pl.pallas_call(
        body, name=name, grid_spec=grid_spec,
        out_shape=[jax.ShapeDtypeStruct((1, r, c), F32)] * 4,
        compiler_params=_params(("parallel",)),
    )(pos_vec, w, m, v, land, land, land, land, own)


def _adamw_small(red, params):
    names = ["sinks", "g_attn", "g_conv", "ln1_g", "ln1_b", "ln2_g", "ln2_b", "conv_w"]
    d = red.shape[1]
    hd = d // 2
    flat = []
    for nme in names:
        flat.extend(params[nme])
    nq = params["sinks"][0].shape[1]
    cs = params["conv_w"][0].shape[2]

    def body(*refs):
        red_ref = refs[0]
        ins = refs[1:1 + 3 * len(names)]
        outs = refs[1 + 3 * len(names):]
        x, y, _, _ = _mesh_pos()
        me = _chip_id(x, y)

        def conv_tap(row, base):
            picked = red_ref[row:row + 1, base:base + cs]
            for s in range(1, N_CHIPS):
                picked = jnp.where(me == s, red_ref[row:row + 1, base + s * cs:base + (s + 1) * cs], picked)
            return picked

        grads = {
            "sinks": red_ref[6:7, hd:hd + nq],
            "g_attn": red_ref[4:5, 0:hd],
            "g_conv": red_ref[4:5, hd:d],
            "ln1_g": red_ref[2:3, :],
            "ln1_b": red_ref[3:4, :],
            "ln2_g": red_ref[0:1, :],
            "ln2_b": red_ref[1:2, :],
        }
        for i, nme in enumerate(names):
            w_ref, m_ref, v_ref = ins[3 * i:3 * i + 3]
            g_out, d_out, m_out, v_out = outs[4 * i:4 * i + 4]
            if nme == "conv_w":
                for tap, (row, base) in enumerate([(5, 0), (5, hd), (6, 0)]):
                    g = conv_tap(row, base)
                    delta, nm, nv = _adamw(w_ref[0, tap:tap + 1, :], g, m_ref[0, tap:tap + 1, :], v_ref[0, tap:tap + 1, :])
                    g_out[0, tap:tap + 1, :] = g
                    d_out[0, tap:tap + 1, :] = delta
                    m_out[0, tap:tap + 1, :] = nm
                    v_out[0, tap:tap + 1, :] = nv
            else:
                g = grads[nme]
                delta, nm, nv = _adamw(w_ref[...], g, m_ref[...], v_ref[...])
                g_out[...] = g
                d_out[...] = delta
                m_out[...] = nm
                v_out[...] = nv

    out_shape = []
    for nme in names:
        out_shape.extend([jax.ShapeDtypeStruct(params[nme][0].shape, F32)] * 4)
    outs = pl.pallas_call(
        body, name="adamw_small",
        in_specs=[_VMEM] * (1 + len(flat)), out_specs=[_VMEM] * len(out_shape),
        out_shape=out_shape,
    )(red, *flat)
    return {nme: tuple(outs[4 * i:4 * i + 4]) for i, nme in enumerate(names)}


def _rope_tables(pos_col):
    s = pos_col.shape[0]
    w = N_KV_HEADS * HEAD_DIM
    tb = min(512, s)
    inv_freq = (ROPE_THETA ** (-np.arange(0, ROT_DIM, 2, dtype=np.float32) / ROT_DIM)).astype(np.float32)

    def body(pos_ref, cos_ref, sin_ref):
        pos = pos_ref[...].astype(F32)
        lane = lax.broadcasted_iota(jnp.int32, (tb, PAIR), 1) & (HEAD_DIM - 1)
        fidx = lane & (ROT_DIM // 2 - 1)
        inv = jnp.zeros((tb, PAIR), F32)
        for k in range(ROT_DIM // 2):
            inv = jnp.where(fidx == k, float(inv_freq[k]), inv)
        ang = pos * inv
        rot = lane < ROT_DIM
        sin_v = jnp.sin(ang)
        cos_ref[...] = _tile_lanes(jnp.where(rot, jnp.cos(ang), 1.0), w // PAIR)
        sin_ref[...] = _tile_lanes(jnp.where(lane < ROT_DIM // 2, -sin_v, jnp.where(rot, sin_v, 0.0)), w // PAIR)

    return pl.pallas_call(
        body, name="rope_tables", grid=(s // tb,),
        in_specs=[pl.BlockSpec((tb, 1), lambda i: (i, 0))],
        out_specs=[pl.BlockSpec((tb, w), lambda i: (i, 0))] * 2,
        out_shape=[jax.ShapeDtypeStruct((s, w), F32)] * 2,
        compiler_params=_params(("parallel",)),
    )(pos_col)


def _in_proj(x, w_in_g, first_vec, n_shards, into, name):
    _, s, d = x.shape
    ns, _, ncol = w_in_g.shape
    tm = min(2 * TM, s)

    def body(first_ref, x_ref, w_ref, into_ref, o_ref):
        o_ref[...] = _dot(x_ref[...].astype(BF16), w_ref[...]).astype(BF16)

    shard = lambda j, first_ref: lax.rem(first_ref[0] + j, ns)
    grid_spec = pltpu.PrefetchScalarGridSpec(
        num_scalar_prefetch=1, grid=(s // tm, n_shards),
        in_specs=[pl.BlockSpec((None, tm, d), lambda i, j, first_ref: (0, i, 0)),
                  pl.BlockSpec((None, d, ncol), lambda i, j, first_ref: (shard(j, first_ref), 0, 0)), _ANY],
        out_specs=pl.BlockSpec((tm, ncol), lambda i, j, first_ref: (i, shard(j, first_ref))))
    return pl.pallas_call(
        body, name=name, grid_spec=grid_spec,
        out_shape=jax.ShapeDtypeStruct((s, ns * ncol), BF16),
        input_output_aliases={} if into is None else {3: 0},
        compiler_params=_params(("parallel", "arbitrary")),
    )(first_vec, x, w_in_g, first_vec if into is None else into)


PAIR = 2 * HEAD_DIM
KEYS = 2 * WINDOW


def _pair_operand(t_all, h):
    col = (h // 2) * PAIR
    lane = lax.broadcasted_iota(jnp.int32, (KEYS, PAIR), 1)
    own_low = h % 2 == 0
    mine = jnp.where((lane < HEAD_DIM) if own_low else (lane >= HEAD_DIM), t_all[:, col:col + PAIR], 0.0)
    other = pltpu.roll(mine, HEAD_DIM, 1)
    low, high = (mine, other) if own_low else (other, mine)
    return jnp.concatenate([low, high], axis=0).astype(BF16)


def _pair_grad(acc, h):
    lane = lax.broadcasted_iota(jnp.int32, (KEYS, PAIR), 1)
    low = jnp.where(lane < HEAD_DIM, acc[:KEYS], 0.0)
    high = jnp.where(lane >= HEAD_DIM, acc[KEYS:], 0.0)
    if h % 2 == 0:
        return low + pltpu.roll(high, HEAD_DIM, 1)
    return high + pltpu.roll(low, HEAD_DIM, 1)


N_PAIRS = N_KV_HEADS * GROUP // 2


def _all_probs(q, kk2s, first, sinks_ref):
    assert ATTN_SCALE == 0.125
    q = q * ATTN_SCALE
    qps, scores = [], []
    for pair in range(N_PAIRS):
        qp = q[:, pair * PAIR:(pair + 1) * PAIR].astype(BF16)
        qps.append(qp)
        scores.append(_dot_nt(qp, kk2s[pair // (GROUP // 2)]))
    qi = lax.broadcasted_iota(jnp.int32, (WINDOW, 2 * KEYS), 0)
    kj = lax.broadcasted_iota(jnp.int32, (WINDOW, 2 * KEYS), 1) & (KEYS - 1)
    rel = qi + WINDOW - kj
    valid = (rel >= 0) & (rel < WINDOW) & jnp.logical_not(first & (kj < WINDOW))
    bias = jnp.where(valid, 0.0, NEG_BIG)
    s = (jnp.stack(scores, axis=0) + bias[None]).reshape(N_PAIRS * WINDOW, 2 * KEYS)
    probs, p_sinks = [], []
    for t in range(2):
        st = s[:, t * KEYS:(t + 1) * KEYS]
        sink = jnp.concatenate([jnp.broadcast_to(sinks_ref[0:1, 2 * pair + t:2 * pair + t + 1], (WINDOW, 1))
                                for pair in range(N_PAIRS)], axis=0)
        m = jnp.maximum(jnp.max(st, axis=1, keepdims=True), sink)
        e = jnp.exp(st - m)
        e_sink = jnp.exp(sink - m)
        inv_l = 1.0 / (jnp.sum(e, axis=1, keepdims=True) + e_sink)
        probs.append(e * inv_l)
        p_sinks.append(e_sink * inv_l)
    return qps, jnp.concatenate(probs, axis=1), p_sinks


def _roped_qkv(cur_ref, prev_ref, cos_ref, sin_ref, cosp_ref, sinp_ref, qw, kvw):
    cur = cur_ref[...].astype(F32)
    cos, sin = cos_ref[...], sin_ref[...]
    cos_q, sin_q = _tile_lanes(cos, GROUP), _tile_lanes(sin, GROUP)
    q = _rope(cur[:, :qw], cos_q, sin_q, 1.0)
    prev = prev_ref[...].astype(F32)
    k_all = jnp.concatenate([_rope(prev[:, :kvw], cosp_ref[...], sinp_ref[...], 1.0),
                             _rope(cur[:, qw:qw + kvw], cos, sin, 1.0)], axis=0)
    v_all = jnp.concatenate([prev[:, kvw:], cur[:, qw + kvw:]], axis=0)
    return q, k_all, v_all, cos_q, sin_q


def _attention_fwd(proj, cos_t, sin_t, sinks):
    s = proj.shape[0]
    qw = GROUP * N_KV_HEADS * HEAD_DIM
    kvw = N_KV_HEADS * HEAD_DIM
    nb = s // WINDOW

    def body(cur_ref, prev_ref, cos_ref, sin_ref, cosp_ref, sinp_ref, sinks_ref, o_ref):
        first = pl.program_id(0) == 0
        q, k_all, v_all, _, _ = _roped_qkv(cur_ref, prev_ref, cos_ref, sin_ref, cosp_ref, sinp_ref, qw, kvw)
        kk2s = [_pair_operand(k_all, h) for h in range(N_KV_HEADS)]
        vv2s = [_pair_operand(v_all, h) for h in range(N_KV_HEADS)]
        _, probs, _ = _all_probs(q, kk2s, first, sinks_ref)
        probs = probs.astype(BF16)
        outs = [_dot(probs[pair * WINDOW:(pair + 1) * WINDOW], vv2s[pair // (GROUP // 2)]) for pair in range(N_PAIRS)]
        o_ref[...] = jnp.concatenate(outs, axis=1)

    tbl = pl.BlockSpec((WINDOW, kvw), lambda n: (n, 0))
    tbl_prev = pl.BlockSpec((WINDOW, kvw), lambda n: (jnp.maximum(n - 1, 0), 0))
    return pl.pallas_call(
        body, name="attention_fwd", grid=(nb,),
        in_specs=[pl.BlockSpec((WINDOW, qw + 2 * kvw), lambda n: (n, 0)),
                  pl.BlockSpec((WINDOW, 2 * kvw), lambda n: (jnp.maximum(n - 1, 0), (qw // (2 * kvw)))),
                  tbl, tbl, tbl_prev, tbl_prev, _VMEM],
        out_specs=pl.BlockSpec((WINDOW, qw), lambda n: (n, 0)),
        out_shape=jax.ShapeDtypeStruct((s, qw), F32),
        compiler_params=_params(("parallel",)),
    )(proj, proj, cos_t, sin_t, cos_t, sin_t, sinks)


def _conv_taps(cw_ref):
    return [jnp.concatenate([cw_ref[s, k:k + 1, :] for s in range(N_CHIPS)], axis=1) for k in range(3)]


def _shift_down(z, halo, steps):
    last = halo.shape[0]
    row = lax.broadcasted_iota(jnp.int32, z.shape, 0)
    out = pltpu.roll(z, steps, 0)
    for r in range(steps):
        out = jnp.where(row == r, halo[last - steps + r:last - steps + r + 1, :], out)
    return out


def _shift_up(z, halo, steps):
    rows = z.shape[0]
    row = lax.broadcasted_iota(jnp.int32, z.shape, 0)
    out = pltpu.roll(z, rows - steps, 0)
    for r in range(steps):
        out = jnp.where(row == rows - steps + r, halo[r:r + 1, :], out)
    return out


def _split_cbu(lo, hi, cw):
    lo, hi = lo.astype(F32), hi.astype(F32)
    c_gate = lo[:, :cw]
    b_gate = jnp.concatenate([lo[:, cw:], hi[:, :2 * cw - lo.shape[1]]], axis=1)
    u = hi[:, 2 * cw - lo.shape[1]:]
    return c_gate, b_gate, u


def _conv_norm(proj, attn, cw_full, g_ac):
    s, in_w = proj.shape
    cw = attn.shape[1]
    blk_w = in_w // 3
    tb = min(TB_CONV, s)

    def body(lo_ref, hi_ref, lo_h_ref, hi_h_ref, attn_ref, cw_ref, g_ref, mixed_ref, ac_ref, rstd_ref):
        i = pl.program_id(0)
        c_gate, b_gate, u = _split_cbu(lo_ref[...], hi_ref[...], cw)
        c_h, _, u_h = _split_cbu(lo_h_ref[...], hi_h_ref[...], cw)
        z = c_gate * u
        z_h = jnp.where(i == 0, 0.0, c_h * u_h)
        w0, w1, w2 = _conv_taps(cw_ref)
        y = w0 * _shift_down(z, z_h, 2) + w1 * _shift_down(z, z_h, 1) + w2 * z
        conv = b_gate * y
        a = attn_ref[...]
        r_a = lax.rsqrt(jnp.mean(a * a, axis=-1, keepdims=True) + RMS_EPS)
        r_c = lax.rsqrt(jnp.mean(conv * conv, axis=-1, keepdims=True) + RMS_EPS)
        g = g_ref[...]
        mixed_ref[...] = jnp.concatenate([a * r_a * g[:, :cw], conv * r_c * g[:, cw:]], axis=1).astype(BF16)
        ac_ref[...] = jnp.concatenate([a, conv], axis=1)
        rstd_ref[0] = r_a
        rstd_ref[1] = r_c

    halo_idx = lambda i: jnp.maximum(i * (tb // HALO_ROWS) - 1, 0)
    return pl.pallas_call(
        body, name="conv_norm", grid=(s // tb,),
        in_specs=[pl.BlockSpec((tb, blk_w), lambda i: (i, 1)),
                  pl.BlockSpec((tb, blk_w), lambda i: (i, 2)),
                  pl.BlockSpec((HALO_ROWS, blk_w), lambda i: (halo_idx(i), 1)),
                  pl.BlockSpec((HALO_ROWS, blk_w), lambda i: (halo_idx(i), 2)),
                  pl.BlockSpec((tb, cw), lambda i: (i, 0)),
                  _VMEM, _VMEM],
        out_specs=[pl.BlockSpec((tb, 2 * cw), lambda i: (i, 0)),
                   pl.BlockSpec((tb, 2 * cw), lambda i: (i, 0)),
                   pl.BlockSpec((2, tb, 1), lambda i: (0, i, 0))],
        out_shape=[jax.ShapeDtypeStruct((s, 2 * cw), BF16), jax.ShapeDtypeStruct((s, 2 * cw), F32),
                   jax.ShapeDtypeStruct((2, s, 1), F32)],
        compiler_params=_params(("parallel",)),
    )(proj, proj, proj, proj, attn, cw_full, g_ac)


def _out_proj_ln(mixed, w_out_g, x, ln_g, ln_b):
    s, d = mixed.shape
    tm = min(TM, s)
    tk = d
    nk = d // tk

    def body(a_ref, w_ref, x_ref, g_ref, b_ref, xhat_ref, h_ref, rstd_ref, acc):
        k = pl.program_id(1)
        _accumulate(acc, lambda: _dot(a_ref[...], w_ref[...]), k, nk)

        @pl.when(k == nk - 1)
        def _():
            def rows_fn(rows):
                xhat, rstd = _ln_fwd(ALPHA * x_ref[rows, :] + acc[rows, :])
                xhat_ref[rows, :] = xhat
                h_ref[rows, :] = (xhat * g_ref[...] + b_ref[...]).astype(BF16)
                rstd_ref[rows, :] = rstd

            _for_row_chunks(tm, rows_fn)

    row = pl.BlockSpec((tm, d), lambda i, k: (i, 0))
    return pl.pallas_call(
        body, name="out_proj_ln", grid=(s // tm, nk),
        in_specs=[pl.BlockSpec((tm, tk), lambda i, k: (i, k)),
                  pl.BlockSpec((tk, d), lambda i, k: (k, 0)),
                  pl.BlockSpec((None, tm, d), lambda i, k: (0, i, 0)),
                  _VMEM, _VMEM],
        out_specs=[row, row, pl.BlockSpec((tm, 1), lambda i, k: (i, 0))],
        out_shape=[jax.ShapeDtypeStruct((s, d), F32), jax.ShapeDtypeStruct((s, d), BF16),
                   jax.ShapeDtypeStruct((s, 1), F32)],
        scratch_shapes=[pltpu.VMEM((tm, d), F32)],
        compiler_params=_params(("parallel", "arbitrary")),
    )(mixed, w_out_g, x, ln_g, ln_b)


def _gate_up(h1, w_gu_g, first_vec, n_shards, into, name):
    s, d = h1.shape
    ns, _, fs2 = w_gu_g.shape
    fs = fs2 // 2
    tm = min(TM, s)

    def body(first_ref, h_ref, w_ref, act_in, ab_in, act_ref, ab_ref):
        gu = _dot(h_ref[...], w_ref[...])
        g, u = gu[:, :fs], gu[:, fs:]
        sg = _sigmoid(g)
        silu = g * sg
        act_ref[...] = (silu * u).astype(BF16)
        ab_ref[:, :fs] = (u * (sg * (1.0 + g * (1.0 - sg)))).astype(BF16)
        ab_ref[:, fs:] = silu.astype(BF16)

    shard = lambda j, first_ref: lax.rem(first_ref[0] + j, ns)
    grid_spec = pltpu.PrefetchScalarGridSpec(
        num_scalar_prefetch=1, grid=(s // tm, n_shards),
        in_specs=[pl.BlockSpec((tm, d), lambda i, j, first_ref: (i, 0)),
                  pl.BlockSpec((None, d, fs2), lambda i, j, first_ref: (shard(j, first_ref), 0, 0)), _ANY, _ANY],
        out_specs=[pl.BlockSpec((tm, fs), lambda i, j, first_ref: (i, shard(j, first_ref))),
                   pl.BlockSpec((tm, fs2), lambda i, j, first_ref: (i, shard(j, first_ref)))])
    return pl.pallas_call(
        body, name=name, grid_spec=grid_spec,
        out_shape=[jax.ShapeDtypeStruct((s, ns * fs), BF16), jax.ShapeDtypeStruct((s, ns * fs2), BF16)],
        input_output_aliases={} if into is None else {3: 0, 4: 1},
        compiler_params=_params(("parallel", "arbitrary")),
    )(first_vec, h1, w_gu_g, *((first_vec, first_vec) if into is None else into))


def _down_ln_loss(act, w_down_g, xhat1, ln1_g, ln1_b, ln2_g, ln2_b, target):
    s, f = act.shape
    d = xhat1.shape[1]
    tm = min(TM, s)
    tk = f // N_CHIPS
    nk = f // tk

    def body(a_ref, w_ref, xh_ref, g1_ref, b1_ref, g2_ref, b2_ref, t_ref, dpre_ref, dpre16_ref, loss_ref, gg_ref, gb_ref,
             acc):
        i, k = pl.program_id(0), pl.program_id(1)
        _accumulate(acc, lambda: _dot(a_ref[...], w_ref[...]), k, nk)

        @pl.when(k == nk - 1)
        def _():
            @pl.when(i == 0)
            def _():
                loss_ref[...] = jnp.zeros_like(loss_ref)
                gg_ref[...] = jnp.zeros_like(gg_ref)
                gb_ref[...] = jnp.zeros_like(gb_ref)

            def rows_fn(rows):
                h1 = xh_ref[rows, :] * g1_ref[...] + b1_ref[...]
                xhat, rstd = _ln_fwd(ALPHA * h1 + acc[rows, :])
                g2 = g2_ref[...]
                diff = xhat * g2 + b2_ref[...] - t_ref[rows, :]
                dy = diff * (1.0 / d)
                dpre = _ln_bwd(dy, xhat, rstd, g2)
                dpre_ref[rows, :] = dpre
                dpre16_ref[rows, :] = dpre.astype(BF16)
                sq = jnp.sum(jnp.sum(diff * diff, axis=1, keepdims=True), axis=0, keepdims=True)
                loss_ref[...] += jnp.broadcast_to(sq * (0.5 / d), (1, 128))
                gg_ref[...] += jnp.sum(dy * xhat, axis=0, keepdims=True)
                gb_ref[...] += jnp.sum(dy, axis=0, keepdims=True)

            _for_row_chunks(tm, rows_fn)

    row = pl.BlockSpec((tm, d), lambda i, k: (i, 0))
    vec = pl.BlockSpec((1, d), lambda i, k: (0, 0))
    return pl.pallas_call(
        body, name="down_ln_loss", grid=(s // tm, nk),
        in_specs=[pl.BlockSpec((tm, tk), lambda i, k: (i, k)),
                  pl.BlockSpec((tk, d), lambda i, k: (k, 0)),
                  row, _VMEM, _VMEM, _VMEM, _VMEM,
                  pl.BlockSpec((None, tm, d), lambda i, k: (0, i, 0))],
        out_specs=[row, row, pl.BlockSpec((1, 128), lambda i, k: (0, 0)), vec, vec],
        out_shape=[jax.ShapeDtypeStruct((s, d), F32), jax.ShapeDtypeStruct((s, d), BF16),
                   jax.ShapeDtypeStruct((1, 128), F32), jax.ShapeDtypeStruct((1, d), F32),
                   jax.ShapeDtypeStruct((1, d), F32)],
        scratch_shapes=[pltpu.VMEM((tm, d), F32)],
        compiler_params=_params(("arbitrary", "arbitrary")),
    )(act, w_down_g, xhat1, ln1_g, ln1_b, ln2_g, ln2_b, target)


def _dact_silu_bwd(dpre2, w_down_g, ab):
    s, d = dpre2.shape
    fs2 = ab.shape[1] // N_CHIPS
    fs = fs2 // 2
    tm = min(TM, s)

    def body(dp_ref, w_ref, ab_ref, dgu_ref):
        d_act = _dot_nt(dp_ref[...], w_ref[...])
        dgu_ref[:, :fs] = (d_act * ab_ref[:, :fs].astype(F32)).astype(BF16)
        dgu_ref[:, fs:] = (d_act * ab_ref[:, fs:].astype(F32)).astype(BF16)

    blk = pl.BlockSpec((tm, fs2), lambda j, i: (i, j))
    return pl.pallas_call(
        body, name="dact_silu_bwd", grid=(N_CHIPS, s // tm),
        in_specs=[pl.BlockSpec((tm, d), lambda j, i: (i, 0)),
                  pl.BlockSpec((fs, d), lambda j, i: (j, 0)), blk],
        out_specs=blk,
        out_shape=jax.ShapeDtypeStruct(ab.shape, BF16),
        compiler_params=_params(("parallel", "parallel")),
    )(dpre2, w_down_g, ab)


def _grad_rows(a, b, after, name, row_blocks=1):
    s, m = a.shape
    n = b.shape[1]
    ms = m // N_CHIPS
    tmw = ms // row_blocks
    tk = min(TK_TOK, s)
    nk = s // tk

    def body(a_ref, b_ref, after_ref, o_ref, acc):
        k = pl.program_id(2)
        _accumulate(acc, lambda: _dot_tn(a_ref[...].astype(BF16), b_ref[...].astype(BF16)), k, nk)

        @pl.when(k == nk - 1)
        def _():
            o_ref[...] = acc[...].astype(BF16)

    return pl.pallas_call(
        body, name=name, grid=(N_CHIPS, row_blocks, nk),
        in_specs=[pl.BlockSpec((tk, tmw), lambda j, r, k: (k, j * row_blocks + r)),
                  pl.BlockSpec((tk, n), lambda j, r, k: (k, 0)), _ANY],
        out_specs=pl.BlockSpec((None, tmw, n), lambda j, r, k: (j, r, 0)),
        out_shape=jax.ShapeDtypeStruct((N_CHIPS, ms, n), BF16),
        scratch_shapes=[pltpu.VMEM((tmw, n), F32)],
        compiler_params=_params(("parallel", "parallel", "arbitrary")),
    )(a, b, after)


def _grad_cols(a, bs, after, name, a_3d=False, row_blocks=2):
    s, m = a.shape[-2:]
    n = bs[0].shape[1]
    ns = n // N_CHIPS
    nb = len(bs)
    tmw = m // row_blocks
    tk = min(TK_TOK, s)
    nk = s // tk

    def body(*refs):
        a_ref, b_refs, o_refs, accs = refs[0], refs[1:1 + nb], refs[2 + nb:2 + 2 * nb], refs[2 + 2 * nb:]
        k = pl.program_id(2)
        for b_ref, acc in zip(b_refs, accs):
            _accumulate(acc, lambda b_ref=b_ref: _dot_tn(a_ref[...].astype(BF16), b_ref[...].astype(BF16)), k, nk)

        @pl.when(k == nk - 1)
        def _():
            for o_ref, acc in zip(o_refs, accs):
                o_ref[...] = acc[...].astype(BF16)

    if a_3d:
        a_spec = pl.BlockSpec((None, tk, tmw), lambda j, r, k: (0, k, r))
    else:
        a_spec = pl.BlockSpec((tk, tmw), lambda j, r, k: (k, r))
    return pl.pallas_call(
        body, name=name, grid=(N_CHIPS, row_blocks, nk),
        in_specs=[a_spec] + [pl.BlockSpec((tk, ns), lambda j, r, k: (k, j))] * nb + [_ANY],
        out_specs=[pl.BlockSpec((None, tmw, ns), lambda j, r, k: (j, r, 0))] * nb,
        out_shape=[jax.ShapeDtypeStruct((N_CHIPS, m, ns), BF16)] * nb,
        scratch_shapes=[pltpu.VMEM((tmw, ns), F32)] * nb,
        compiler_params=_params(("parallel", "parallel", "arbitrary")),
    )(a, *bs, after)


def _dh1_ln_bwd(d_gu, w_gu_g, dpre2, xhat1, rstd1, ln1_g, after):
    s = d_gu.shape[0]
    d = dpre2.shape[1]
    hd = d // 2
    fs = w_gu_g.shape[2]
    tm = min(TM, s)

    def body(dgu_ref, w_ref, dp2_ref, xh_ref, rs_ref, g_ref, after_ref, dpre_ref, gg_ref, gb_ref, acc_lo, acc_hi):
        i, j, half = pl.program_id(0), pl.program_id(1), pl.program_id(2)

        def product():
            return _dot_nt(dgu_ref[...], w_ref[...])

        @pl.when(half == 0)
        def _():
            _accumulate(acc_lo, product, j, N_CHIPS)

        @pl.when(half == 1)
        def _():
            _accumulate(acc_hi, product, j, N_CHIPS)

        @pl.when((j == N_CHIPS - 1) & (half == 1))
        def _():
            @pl.when(i == 0)
            def _():
                gg_ref[...] = jnp.zeros_like(gg_ref)
                gb_ref[...] = jnp.zeros_like(gb_ref)

            def rows_fn(rows):
                dh = jnp.concatenate([acc_lo[rows, :], acc_hi[rows, :]], axis=1) + ALPHA * dp2_ref[rows, :]
                xhat = xh_ref[rows, :]
                dpre_ref[rows, :] = _ln_bwd(dh, xhat, rs_ref[rows, :], g_ref[...])
                gg_ref[...] += jnp.sum(dh * xhat, axis=0, keepdims=True)
                gb_ref[...] += jnp.sum(dh, axis=0, keepdims=True)

            _for_row_chunks(tm, rows_fn)

    row = pl.BlockSpec((tm, d), lambda i, j, h: (i, 0))
    vec = pl.BlockSpec((1, d), lambda i, j, h: (0, 0))
    act_blk = pl.BlockSpec((tm, fs), lambda i, j, h: (i, j))
    w_blk = pl.BlockSpec((None, hd, fs), lambda i, j, h: (j, h, 0))
    return pl.pallas_call(
        body, name="dh1_ln_bwd", grid=(s // tm, N_CHIPS, 2),
        in_specs=[act_blk, w_blk, row, row, pl.BlockSpec((tm, 1), lambda i, j, h: (i, 0)), _VMEM, _ANY],
        out_specs=[row, vec, vec],
        out_shape=[jax.ShapeDtypeStruct((s, d), F32), jax.ShapeDtypeStruct((1, d), F32),
                   jax.ShapeDtypeStruct((1, d), F32)],
        scratch_shapes=[pltpu.VMEM((tm, hd), F32)] * 2,
        compiler_params=_params(("arbitrary", "arbitrary", "arbitrary")),
    )(d_gu, w_gu_g, dpre2, xhat1, rstd1, ln1_g, after)


def _dmixed_rms_bwd(dpre1, w_out_g, ac, rstd, g_ac):
    s, d = dpre1.shape
    hd = d // 2
    tm = min(TM, s)

    def body(dp_ref, w_ref, ac_ref, rs_ref, g_ref, dac_ref, gg_ref):
        i = pl.program_id(1)
        dm = _dot_nt(dp_ref[...].astype(BF16), w_ref[...])
        pre = ac_ref[...]
        r = rs_ref[...]
        gdm = dm * g_ref[...]
        dac_ref[...] = r * gdm - pre * (r * r * r) * jnp.mean(gdm * pre, axis=-1, keepdims=True)
        gg = jnp.sum(dm * pre * r, axis=0, keepdims=True)

        @pl.when(i == 0)
        def _():
            gg_ref[...] = gg

        @pl.when(i > 0)
        def _():
            gg_ref[...] += gg

    return pl.pallas_call(
        body, name="dmixed_rms_bwd", grid=(2, s // tm),
        in_specs=[pl.BlockSpec((tm, d), lambda h, i: (i, 0)),
                  pl.BlockSpec((hd, d), lambda h, i: (h, 0)),
                  pl.BlockSpec((tm, hd), lambda h, i: (i, h)),
                  pl.BlockSpec((None, tm, 1), lambda h, i: (h, i, 0)),
                  pl.BlockSpec((1, hd), lambda h, i: (0, h))],
        out_specs=[pl.BlockSpec((tm, hd), lambda h, i: (i, h)),
                   pl.BlockSpec((1, hd), lambda h, i: (0, h))],
        out_shape=[jax.ShapeDtypeStruct((s, d), F32), jax.ShapeDtypeStruct((1, d), F32)],
        compiler_params=_params(("arbitrary", "arbitrary")),
    )(dpre1, w_out_g, ac, rstd, g_ac)


def _attention_bwd(proj, d_ac, cos_t, sin_t, sinks, after):
    s = proj.shape[0]
    qw = GROUP * N_KV_HEADS * HEAD_DIM
    kvw = N_KV_HEADS * HEAD_DIM
    nb = s // WINDOW
    nq = GROUP * N_KV_HEADS

    def body(cur_ref, prev_ref, do_ref, cos_ref, sin_ref, cosp_ref, sinp_ref, sinks_ref, after_ref,
             dq_ref, dcur_ref, dprev_ref, dsink_ref):
        n = pl.program_id(0)
        first = n == 0
        q, k_all, v_all, cos_q, sin_q = _roped_qkv(cur_ref, prev_ref, cos_ref, sin_ref, cosp_ref, sinp_ref, qw, kvw)
        kk2s = [_pair_operand(k_all, h) for h in range(N_KV_HEADS)]
        vv2s = [_pair_operand(v_all, h) for h in range(N_KV_HEADS)]
        qps, probs, p_sinks = _all_probs(q, kk2s, first, sinks_ref)
        dops = [do_ref[:, pair * PAIR:(pair + 1) * PAIR].astype(BF16) for pair in range(N_PAIRS)]
        d_probs = jnp.concatenate([_dot_nt(dops[pair], vv2s[pair // (GROUP // 2)]) for pair in range(N_PAIRS)], axis=0)
        d_s, ds_sinks = [], []
        for t in range(2):
            cols = slice(t * KEYS, (t + 1) * KEYS)
            delta = jnp.sum(probs[:, cols] * d_probs[:, cols], axis=1, keepdims=True)
            d_s.append(probs[:, cols] * (d_probs[:, cols] - delta))
            ds_sinks.append(-p_sinks[t] * delta)
        d_s = jnp.concatenate(d_s, axis=1).astype(BF16)
        probs = probs.astype(BF16)
        dq_parts, dk_tiles, dv_tiles, dsink_parts = [], [], [], []
        for h in range(N_KV_HEADS):
            dkk2, dvv2 = None, None
            for p in range(GROUP // 2):
                pair = (GROUP // 2) * h + p
                rows = slice(pair * WINDOW, (pair + 1) * WINDOW)
                dq_parts.append(_dot(d_s[rows], kk2s[h]) * ATTN_SCALE)
                dk_term = _dot_tn(d_s[rows], qps[pair])
                dv_term = _dot_tn(probs[rows], dops[pair])
                dkk2 = dk_term if dkk2 is None else dkk2 + dk_term
                dvv2 = dv_term if dvv2 is None else dvv2 + dv_term
                dsink_parts.extend([jnp.sum(ds_sinks[t][rows], axis=0, keepdims=True) for t in range(2)])
            dk_tiles.append(_pair_grad(dkk2, h))
            dv_tiles.append(_pair_grad(dvv2, h))
        dq_ref[...] = _rope(jnp.concatenate(dq_parts, axis=1), cos_q, sin_q, -1.0)
        dk = jnp.concatenate([dk_tiles[0] + dk_tiles[1], dk_tiles[2] + dk_tiles[3]], axis=1)
        dv = jnp.concatenate([dv_tiles[0] + dv_tiles[1], dv_tiles[2] + dv_tiles[3]], axis=1)
        dprev_ref[...] = jnp.concatenate([dk[:WINDOW], dv[:WINDOW]], axis=1)
        dcur_ref[...] = jnp.concatenate([dk[WINDOW:], dv[WINDOW:]], axis=1)
        dsink = jnp.concatenate(dsink_parts, axis=1)

        @pl.when(first)
        def _():
            dsink_ref[...] = dsink

        @pl.when(n > 0)
        def _():
            dsink_ref[...] += dsink

    tbl = pl.BlockSpec((WINDOW, kvw), lambda n: (n, 0))
    tbl_prev = pl.BlockSpec((WINDOW, kvw), lambda n: (jnp.maximum(n - 1, 0), 0))
    kv_blk = pl.BlockSpec((WINDOW, 2 * kvw), lambda n: (n, 0))
    return pl.pallas_call(
        body, name="attention_bwd", grid=(nb,),
        in_specs=[pl.BlockSpec((WINDOW, qw + 2 * kvw), lambda n: (n, 0)),
                  pl.BlockSpec((WINDOW, 2 * kvw), lambda n: (jnp.maximum(n - 1, 0), (qw // (2 * kvw)))),
                  pl.BlockSpec((WINDOW, qw), lambda n: (n, 0)),
                  tbl, tbl, tbl_prev, tbl_prev, _VMEM, _ANY],
        out_specs=[pl.BlockSpec((WINDOW, qw), lambda n: (n, 0)), kv_blk, kv_blk,
                   pl.BlockSpec((1, nq), lambda n: (0, 0))],
        out_shape=[jax.ShapeDtypeStruct((s, qw), F32), jax.ShapeDtypeStruct((s, 2 * kvw), F32),
                   jax.ShapeDtypeStruct((s, 2 * kvw), F32), jax.ShapeDtypeStruct((1, nq), F32)],
        compiler_params=_params(("arbitrary",)),
    )(proj, proj, d_ac, cos_t, sin_t, cos_t, sin_t, sinks, after)


def _dproj_assemble(proj, d_ac, dq, dkv_cur, dkv_prev, cos_t, sin_t, cw_full):
    s, in_w = proj.shape
    cw = dq.shape[1]
    kvw = N_KV_HEADS * HEAD_DIM
    blk_w = in_w // 3
    tb = WINDOW
    nb = s // tb

    def body(lo_ref, hi_ref, lo_p_ref, hi_p_ref, lo_n_ref, hi_n_ref, dconv_ref, dconv_n_ref,
             dq_ref, dcur_ref, dprev_n_ref, cos_ref, sin_ref, cw_ref, dproj_ref, gcw_ref):
        i = pl.program_id(0)
        last = i == nb - 1
        c_gate, b_gate, u = _split_cbu(lo_ref[...], hi_ref[...], cw)
        c_p, _, u_p = _split_cbu(lo_p_ref[...], hi_p_ref[...], cw)
        _, b_n, _ = _split_cbu(lo_n_ref[...], hi_n_ref[...], cw)
        z = c_gate * u
        z_p = jnp.where(i == 0, 0.0, c_p * u_p)
        z1 = _shift_down(z, z_p, 1)
        z2 = _shift_down(z, z_p, 2)
        w0, w1, w2 = _conv_taps(cw_ref)
        y = w0 * z2 + w1 * z1 + w2 * z
        d_conv = dconv_ref[...]
        d_b = d_conv * y
        d_y = d_conv * b_gate
        d_y_n = jnp.where(last, 0.0, dconv_n_ref[...] * b_n[:dconv_n_ref.shape[0]])
        d_z = w2 * d_y + w1 * _shift_up(d_y, d_y_n, 1) + w0 * _shift_up(d_y, d_y_n, 2)
        d_c = d_z * u
        d_u = d_z * c_gate
        gcw = jnp.concatenate([jnp.sum(d_y * z2, axis=0, keepdims=True), jnp.sum(d_y * z1, axis=0, keepdims=True),
                               jnp.sum(d_y * z, axis=0, keepdims=True)], axis=0)

        @pl.when(i == 0)
        def _():
            gcw_ref[...] = gcw

        @pl.when(i > 0)
        def _():
            gcw_ref[...] += gcw

        dkv = dcur_ref[...] + jnp.where(last, 0.0, dprev_n_ref[...])
        dk = _rope(dkv[:, :kvw], cos_ref[...], sin_ref[...], -1.0)
        dproj_ref[...] = jnp.concatenate([dq_ref[...], dk, dkv[:, kvw:], d_c, d_b, d_u], axis=1).astype(BF16)

    prev_halo = lambda i: jnp.maximum(i * (tb // HALO_ROWS) - 1, 0)
    next_halo = lambda i: jnp.minimum((i + 1) * (tb // HALO_ROWS), s // HALO_ROWS - 1)
    next8 = lambda i: jnp.minimum((i + 1) * (tb // 8), s // 8 - 1)
    nxt = lambda i: jnp.minimum(i + 1, nb - 1)
    return pl.pallas_call(
        body, name="dproj_assemble", grid=(nb,),
        in_specs=[pl.BlockSpec((tb, blk_w), lambda i: (i, 1)),
                  pl.BlockSpec((tb, blk_w), lambda i: (i, 2)),
                  pl.BlockSpec((HALO_ROWS, blk_w), lambda i: (prev_halo(i), 1)),
                  pl.BlockSpec((HALO_ROWS, blk_w), lambda i: (prev_halo(i), 2)),
                  pl.BlockSpec((HALO_ROWS, blk_w), lambda i: (next_halo(i), 1)),
                  pl.BlockSpec((HALO_ROWS, blk_w), lambda i: (next_halo(i), 2)),
                  pl.BlockSpec((tb, cw), lambda i: (i, 1)),
                  pl.BlockSpec((8, cw), lambda i: (next8(i), 1)),
                  pl.BlockSpec((tb, cw), lambda i: (i, 0)),
                  pl.BlockSpec((tb, 2 * kvw), lambda i: (i, 0)),
                  pl.BlockSpec((tb, 2 * kvw), lambda i: (nxt(i), 0)),
                  pl.BlockSpec((tb, kvw), lambda i: (i, 0)),
                  pl.BlockSpec((tb, kvw), lambda i: (i, 0)),
                  _VMEM],
        out_specs=[pl.BlockSpec((tb, in_w), lambda i: (i, 0)),
                   pl.BlockSpec((3, cw), lambda i: (0, 0))],
        out_shape=[jax.ShapeDtypeStruct((s, in_w), BF16), jax.ShapeDtypeStruct((3, cw), F32)],
        compiler_params=_params(("arbitrary",)),
    )(proj, proj, proj, proj, proj, proj, d_ac, d_ac, dq, dkv_cur, dkv_prev, cos_t, sin_t, cw_full)


def _dx(d_proj, w_in_g, dpre1, after):
    s, in_w = d_proj.shape
    ns, d, ncol = w_in_g.shape
    tm = min(TM, s)

    def body(dp_ref, w_ref, r_ref, after_ref, o_ref, acc):
        j = pl.program_id(1)
        _accumulate(acc, lambda: _dot_nt(dp_ref[...], w_ref[...]), j, ns)

        @pl.when(j == ns - 1)
        def _():
            o_ref[...] = acc[...] + ALPHA * r_ref[...]

    return pl.pallas_call(
        body, name="dx", grid=(s // tm, ns),
        in_specs=[pl.BlockSpec((tm, ncol), lambda i, j: (i, j)),
                  pl.BlockSpec((None, d, ncol), lambda i, j: (j, 0, 0)),
                  pl.BlockSpec((tm, d), lambda i, j: (i, 0)), _ANY],
        out_specs=pl.BlockSpec((None, tm, d), lambda i, j: (0, i, 0)),
        out_shape=jax.ShapeDtypeStruct((1, s, d), F32),
        scratch_shapes=[pltpu.VMEM((tm, d), F32)],
        compiler_params=_params(("parallel", "arbitrary")),
    )(d_proj, w_in_g, dpre1, after)


def kernel(x, positions, w_in, conv_w, sinks, g_attn, g_conv, w_out, ln1_g, ln1_b, w_gate, w_up, w_down, ln2_g, ln2_b, loss_target, m_w_in, m_conv_w, m_sinks, m_g_attn, m_g_conv, m_w_out, m_ln1_g, m_ln1_b, m_w_gate, m_w_up, m_w_down, m_ln2_g, m_ln2_b, v_w_in, v_conv_w, v_sinks, v_g_attn, v_g_conv, v_w_out, v_ln1_g, v_ln1_b, v_w_gate, v_w_up, v_w_down, v_ln2_g, v_ln2_b):
    s = x.shape[1]
    d = x.shape[2]

    chip_vec = _chip_id(lax.axis_index("x"), lax.axis_index("y")).astype(jnp.int32).reshape(1)
    wnames = ["w_in", "w_out", "w_gu", "w_down"]
    buf_in = _cast_weight(w_in, chip_vec, chip_vec, "cast_w_in")
    flight_in, token_in = _gather_start([buf_in], chip_vec, "gather_start_w_in")
    cw_buf = lax.dynamic_update_slice(jnp.zeros((N_CHIPS,) + conv_w.shape[1:], F32), conv_w, (chip_vec[0], 0, 0))
    cw_flight = _flight_start("conv_w_start", [cw_buf], _conv_w_plan(), 3, token_in)
    started = cw_flight[2][0]
    buf_gu = _cast_weight(w_gate, chip_vec, started, "cast_w_gate", 0, 2)
    buf_gu = _cast_weight(w_up, chip_vec, buf_gu, "cast_w_up", 1, 2)
    bufs = [_cast_weight(w_out, chip_vec, started, "cast_w_out"), buf_gu,
            _cast_weight(w_down, chip_vec, started, "cast_w_down")]
    flights_rest, token = _gather_start(bufs, token_in, "gather_start_rest")
    flights = flight_in + flights_rest

    def gathered(i, after):
        send_sems, recv_sems, buf = flights[i]
        buf = _gather_wait(send_sems, recv_sems, buf, after, "gather_wait_" + wnames[i])
        return _sibling_fill(buf, "sibling_fill_" + wnames[i])

    g_ac = jnp.concatenate([g_attn, g_conv], axis=1)

    proj_own = _in_proj(x, _after(flights[0][2], token), chip_vec, 1, None, "in_proj_own")
    cos_t, sin_t = _rope_tables(positions.reshape(s, 1) + token[0:1, 0:1].astype(jnp.int32))
    w_in_g = gathered(0, _after(cos_t, proj_own))
    proj = _in_proj(x, w_in_g, chip_vec + 1, N_CHIPS - 1, proj_own, "in_proj_rest")
    send_sems, recv_sems, buf_out = flights[1]
    buf_out = _gather_wait(send_sems, recv_sems, buf_out, proj, "gather_wait_w_out")
    fill_out = _flight_start("fill_start_w_out", [buf_out], _fill_plan(1), 3, chip_vec)
    attn = _attention_fwd(_after(proj, fill_out[2][0]), cos_t, sin_t, sinks)
    (cw_full,) = _flight_wait("conv_w_wait", cw_flight, _conv_w_plan(), attn)
    mixed, ac, rstd_ac = _conv_norm(proj, attn, cw_full, g_ac)
    (w_out_g,) = _flight_wait("fill_wait_w_out", fill_out, _fill_plan(1), mixed)
    w_out_full = w_out_g.reshape(d, d)
    xhat1, h1, rstd1 = _out_proj_ln(mixed, w_out_full, x, ln1_g, ln1_b)
    send_sems, recv_sems, buf_gu = flights[2]
    buf_gu = _gather_wait(send_sems, recv_sems, buf_gu, h1, "gather_wait_w_gu")
    fill_gu = _flight_start("fill_start_w_gu", [buf_gu], _fill_plan(1), 3, chip_vec)
    own = _gate_up(h1, fill_gu[2][0], chip_vec, 1, None, "gate_up_own")
    (w_gu_g,) = _flight_wait("fill_wait_w_gu", fill_gu, _fill_plan(1), own[0])
    act, ab = _gate_up(h1, w_gu_g, chip_vec + 1, N_CHIPS - 1, own, "gate_up_rest")
    w_down_full = gathered(3, act).reshape(-1, d)
    dpre2, dpre2_16, loss_part, g_ln2_g, g_ln2_b = _down_ln_loss(act, w_down_full, xhat1, ln1_g, ln1_b, ln2_g, ln2_b,
                                                                 loss_target)

    cvec = lax.axis_index("c").astype(jnp.int32).reshape(1)

    def exchange_begin(parts, nme):
        bufs = []
        for part in parts:
            ns, r, cdim = part.shape
            bufs.extend([part, lax.empty((ns, r // 2, cdim), part.dtype)])
        return _flight_start("exchange_start_" + nme, bufs, _exchange_plan(len(parts)), len(parts), cvec)

    def exchange_end(flight, n_parts, after, nme):
        bufs = _flight_wait("exchange_wait_" + nme, flight, _exchange_plan(n_parts), after)
        return [(bufs[2 * w], bufs[2 * w + 1]) for w in range(n_parts)]

    def scatter_begin(part, got, nme):
        return _scatter_start(_add_halves(part, got, cvec, "add_halves_" + nme), "scatter_start_" + nme)

    d_gu = _dact_silu_bwd(dpre2_16, w_down_full, ab)
    p_down = _grad_rows(act, dpre2_16, d_gu, "grad_w_down")
    x_down = exchange_begin([p_down], "w_down")
    (p_gu,) = _grad_cols(h1, [d_gu], x_down[2][0], "grad_w_gate_up")
    ((p_down, got),) = exchange_end(x_down, 1, p_gu, "w_down")
    f_down = scatter_begin(p_down, got, "w_down")
    x_gu = exchange_begin([_after(p_gu, f_down[2])], "w_gu")
    dpre1, g_ln1_g, g_ln1_b = _dh1_ln_bwd(d_gu, w_gu_g, dpre2, xhat1, rstd1, ln1_g, x_gu[2][0])
    ((p_gu, got),) = exchange_end(x_gu, 1, dpre1, "w_gu")
    f_gu = scatter_begin(p_gu, got, "w_gu")
    d_ac, g_g_ac = _dmixed_rms_bwd(_after(dpre1, f_gu[2]), w_out_full, ac, rstd_ac, g_ac)
    p_out = _grad_rows(mixed, dpre1, d_ac, "grad_w_out")
    x_out = exchange_begin([p_out], "w_out")
    dq, dkv_cur, dkv_prev, g_sinks = _attention_bwd(proj, d_ac, cos_t, sin_t, sinks, x_out[2][0])
    ((p_out, got),) = exchange_end(x_out, 1, dq, "w_out")
    f_out = scatter_begin(p_out, got, "w_out")
    d_proj, g_conv_w = _dproj_assemble(proj, _after(d_ac, f_out[2]), dq, dkv_cur, dkv_prev, cos_t, sin_t, cw_full)
    (p_in,) = _grad_cols(x, [d_proj], d_proj, "grad_w_in", a_3d=True)
    x_in = exchange_begin([p_in], "w_in")
    grad_x = _dx(d_proj, w_in_g, dpre1, x_in[2][0])
    red = _allreduce_small(g_ln2_g, g_ln2_b, g_ln1_g, g_ln1_b, g_g_ac, g_conv_w, g_sinks, loss_part, grad_x)
    ((p_in, got),) = exchange_end(x_in, 1, red, "w_in")
    f_in = scatter_begin(p_in, got, "w_in")

    pos_vec = jnp.concatenate([chip_vec, cvec])
    shards = {"w_in": (w_in, m_w_in, v_w_in), "w_out": (w_out, m_w_out, v_w_out), "w_gate": (w_gate, m_w_gate, v_w_gate),
              "w_up": (w_up, m_w_up, v_w_up), "w_down": (w_down, m_w_down, v_w_down)}
    early = [("w_down", ["w_down"]), ("w_gu", ["w_gate", "w_up"]), ("w_out", ["w_out"])]
    after = f_in[2]
    completing = {}
    for (nme, _), f in zip(early, [f_down, f_gu, f_out]):
        sums, land = _scatter_wait(*f, after, "scatter_wait_" + nme)
        completing[nme] = _flight_start("complete_start_" + nme, [sums, land], _complete_plan(1), 4, cvec)
        after = completing[nme][2][1]
    big = {}
    for nme, members in early:
        sums, land = _flight_wait("complete_wait_" + nme, completing[nme], _complete_plan(1), after)
        for col_block, member in enumerate(members):
            big[member] = _adamw_shard(*shards[member], land, sums, pos_vec, "adamw_" + member, col_block)
            after = big[member][0]
    sums, land = _scatter_wait(*f_in, after, "scatter_wait_w_in")
    (land,) = _complete_chip_sums([sums], [land])
    big["w_in"] = _adamw_shard(*shards["w_in"], land, sums, pos_vec, "adamw_w_in")
    small = _adamw_small(red, {
        "sinks": (sinks, m_sinks, v_sinks), "g_attn": (g_attn, m_g_attn, v_g_attn),
        "g_conv": (g_conv, m_g_conv, v_g_conv), "ln1_g": (ln1_g, m_ln1_g, v_ln1_g),
        "ln1_b": (ln1_b, m_ln1_b, v_ln1_b), "ln2_g": (ln2_g, m_ln2_g, v_ln2_g),
        "ln2_b": (ln2_b, m_ln2_b, v_ln2_b), "conv_w": (conv_w, m_conv_w, v_conv_w)})
    res = {**big, **small}
    order = ["w_in", "conv_w", "sinks", "g_attn", "g_conv", "w_out", "ln1_g", "ln1_b", "w_gate", "w_up", "w_down",
             "ln2_g", "ln2_b"]
    loss = red[6, d // 2 + 128]
    return (loss, grad_x, *[res[n][0] for n in order], *[res[n][1] for n in order],
            *[res[n][2] for n in order], *[res[n][3] for n in order])
```

```python
import functools

import numpy as np
import jax
import jax.numpy as jnp
from jax import lax
from jax.experimental import pallas as pl
from jax.experimental.pallas import tpu as pltpu

F32 = jnp.float32
BF16 = jnp.bfloat16
MESH = pl.DeviceIdType.MESH

HEAD_DIM = 64
N_KV_HEADS = 4
GROUP = 4
WINDOW = 128
ROT_DIM = 16
ROPE_THETA = 500000.0
ATTN_SCALE = HEAD_DIM ** -0.5
ALPHA = 2.0 ** 0.25
LN_EPS = 1e-5
RMS_EPS = 1e-6
ADAM_LR = 0.001
ADAM_B1 = 0.9
ADAM_B2 = 0.999
ADAM_EPS = 1e-08
ADAM_WD = 0.01
ADAM_STEP = 10
N_CHIPS = 4
NEG_BIG = -1e30

V7X_VMEM_BYTES = 64 * 1024 * 1024
VMEM_LIMIT = V7X_VMEM_BYTES - 6 * 1024 * 1024

TM = 512
TK_TOK = 1024
TB_CONV = 256
TR_ELT = 256
ROW_CHUNK = 128
HALO_ROWS = 16


def _params(sem):
    return pltpu.CompilerParams(dimension_semantics=sem, vmem_limit_bytes=VMEM_LIMIT)


def _row_tile(rows, target):
    best = None
    for t in range(16, min(rows, target) + 1, 16):
        if rows % t == 0:
            best = t
    assert best is not None, (rows, target)
    return best


def _dot(a, b):
    return jnp.dot(a, b, preferred_element_type=F32)


def _dot_nt(a, b):
    return lax.dot_general(a, b, (((1,), (1,)), ((), ())), preferred_element_type=F32)


def _dot_tn(a, b):
    return lax.dot_general(a, b, (((0,), (0,)), ((), ())), preferred_element_type=F32)


def _mesh_pos():
    x, y, c = lax.axis_index("x"), lax.axis_index("y"), lax.axis_index("c")
    chips = [(1 - x, y), (x, 1 - y), (1 - x, 1 - y)]
    return x, y, c, chips


def _chip_id(px, py):
    return 2 * px + py


def _rope(t, cos, sgn_sin, sign):
    w = t.shape[1]
    lane = lax.broadcasted_iota(jnp.int32, t.shape, 1) & (HEAD_DIM - 1)
    partner = jnp.where(lane < ROT_DIM // 2, pltpu.roll(t, w - ROT_DIM // 2, 1), pltpu.roll(t, ROT_DIM // 2, 1))
    return t * cos + sign * (partner * sgn_sin)


def _tile_lanes(t, n):
    return jnp.concatenate([t] * n, axis=1)


def _sigmoid(g):
    return 1.0 / (1.0 + jnp.exp(-g))


def _for_row_chunks(n_rows, fn):
    def step(r, carry):
        fn(pl.ds(pl.multiple_of(r * ROW_CHUNK, ROW_CHUNK), ROW_CHUNK))
        return carry

    lax.fori_loop(0, n_rows // ROW_CHUNK, step, 0)


def _accumulate(acc, make_val, k, nk):
    if nk == 1:
        acc[...] = make_val()
        return

    @pl.when(k == 0)
    def _():
        acc[...] = jnp.zeros_like(acc)

    acc[...] += make_val()


def _ln_fwd(pre):
    mu = jnp.mean(pre, axis=-1, keepdims=True)
    cen = pre - mu
    var = jnp.mean(cen * cen, axis=-1, keepdims=True)
    rstd = lax.rsqrt(var + LN_EPS)
    return cen * rstd, rstd


def _ln_bwd(dy, xhat, rstd, g):
    dxhat = dy * g
    m1 = jnp.mean(dxhat, axis=-1, keepdims=True)
    m2 = jnp.mean(dxhat * xhat, axis=-1, keepdims=True)
    return rstd * (dxhat - m1 - xhat * m2)


def _cast_weight(w, chip_vec, after, name, col_block=0, n_col_blocks=1):
    _, r, c = w.shape
    tr = _row_tile(r, TR_ELT)

    def body(chip_ref, w_ref, after_ref, o_ref):
        o_ref[...] = w_ref[...].astype(BF16)

    grid_spec = pltpu.PrefetchScalarGridSpec(
        num_scalar_prefetch=1, grid=(r // tr,),
        in_specs=[pl.BlockSpec((None, tr, c), lambda i, chip_ref: (0, i, 0)), _ANY],
        out_specs=pl.BlockSpec((None, tr, c), lambda i, chip_ref: (chip_ref[0], i, col_block)))
    return pl.pallas_call(
        body, name=name, grid_spec=grid_spec,
        out_shape=jax.ShapeDtypeStruct((N_CHIPS, r, n_col_blocks * c), BF16),
        input_output_aliases={2: 0} if col_block else {},
        compiler_params=_params(("parallel",)),
    )(chip_vec, w, after)


_HBM = pl.BlockSpec(memory_space=pltpu.HBM)
_VMEM = pl.BlockSpec(memory_space=pltpu.VMEM)


_SEM = pl.BlockSpec(memory_space=pltpu.SEMAPHORE)
_ANY = pl.BlockSpec(memory_space=pl.ANY)
_EFFECT = pltpu.SideEffectType.DATAFLOW_SIDE_EFFECTING


def _chip_copy(buf, k, chip_of_src, half_rows, send_sems, recv_sems, to):
    part = buf.at[chip_of_src, half_rows]
    return pltpu.make_async_remote_copy(
        src_ref=part, dst_ref=part, send_sem=send_sems.at[k], recv_sem=recv_sems.at[k], device_id=to, device_id_type=MESH)


def _half_rows(buf, which):
    hr = buf.shape[1] // 2
    return pl.ds(which * hr, hr)


def _after(value, dep):
    return lax.optimization_barrier((value, dep))[0]


def _flight_start(name, bufs, plan, n_sems, after):
    n = len(bufs)

    def body(*refs):
        sends, _ = plan(refs[:n], refs[n + 1], refs[n + 2])
        for cp in sends:
            cp.start()

    outs = pl.pallas_call(
        body, name=name,
        in_specs=[_HBM] * n + [_ANY], out_specs=[_SEM, _SEM] + [_HBM] * n,
        out_shape=[pltpu.SemaphoreType.DMA((n_sems,))] * 2 + [pltpu.HBM(b.shape, b.dtype) for b in bufs],
        input_output_aliases={i: 2 + i for i in range(n)},
        compiler_params=pltpu.CompilerParams(has_side_effects=_EFFECT),
    )(*[pltpu.with_memory_space_constraint(b, pltpu.HBM) for b in bufs], after)
    return outs[0], outs[1], list(outs[2:])


def _flight_wait(name, flight, plan, after):
    send_sems, recv_sems, bufs = flight
    n = len(bufs)

    def body(*refs):
        sends, recvs = plan(refs[:n], refs[n], refs[n + 1])
        for cp in sends:
            cp.wait_send()
        for cp in recvs:
            cp.wait_recv()

    outs = pl.pallas_call(
        body, name=name,
        in_specs=[_HBM] * n + [_SEM, _SEM, _ANY], out_specs=[_HBM] * n,
        out_shape=[pltpu.HBM(b.shape, b.dtype) for b in bufs],
        input_output_aliases={i: i for i in range(n)},
        compiler_params=pltpu.CompilerParams(has_side_effects=_EFFECT),
    )(*bufs, send_sems, recv_sems, after)
    return list(outs)


def _fill_plan(n_bufs):
    def plan(refs, send_sems, recv_sems):
        x, y, c, chips = _mesh_pos()
        sibling = (x, y, 1 - c)
        sends, recvs = [], []
        for w in range(n_bufs):
            for k, chip in enumerate(chips):
                slot = _chip_id(*chip)
                sends.append(_chip_copy(refs[w], 3 * w + k, slot, _half_rows(refs[w], c), send_sems, recv_sems, sibling))
                recvs.append(_chip_copy(refs[w], 3 * w + k, slot, _half_rows(refs[w], 1 - c), send_sems, recv_sems,
                                        sibling))
        return sends, recvs
    return plan


def _conv_w_plan():
    def plan(refs, send_sems, recv_sems):
        x, y, c, chips = _mesh_pos()
        me = _chip_id(x, y)
        (buf,) = refs
        sends, recvs = [], []
        for k, chip in enumerate(chips):
            for slot, into in ((me, sends), (_chip_id(*chip), recvs)):
                into.append(pltpu.make_async_remote_copy(
                    src_ref=buf.at[slot], dst_ref=buf.at[slot], send_sem=send_sems.at[k], recv_sem=recv_sems.at[k],
                    device_id=(*chip, c), device_id_type=MESH))
        return sends, recvs
    return plan


def _exchange_plan(n_parts):
    def plan(refs, send_sems, recv_sems):
        x, y, c, _ = _mesh_pos()
        copies = []
        for w in range(n_parts):
            part, got = refs[2 * w], refs[2 * w + 1]
            hr = got.shape[1]
            copies.append(pltpu.make_async_remote_copy(
                src_ref=part.at[:, pl.ds((1 - c) * hr, hr)], dst_ref=got, send_sem=send_sems.at[w],
                recv_sem=recv_sems.at[w], device_id=(x, y, 1 - c), device_id_type=MESH))
        return copies, copies
    return plan


def _gather_start(bufs, after, name):
    n = len(bufs)

    def body(*refs):
        ins = refs[:n]
        sends, recvs = refs[n + 1:2 * n + 1], refs[2 * n + 1:3 * n + 1]
        token = refs[4 * n + 1]
        x, y, c, chips = _mesh_pos()
        me = _chip_id(x, y)
        for w in range(n):
            for k, chip in enumerate(chips):
                _chip_copy(ins[w], k, me, _half_rows(ins[w], c), sends[w], recvs[w], (*chip, c)).start()
        token[...] = jnp.zeros_like(token)

    outs = pl.pallas_call(
        body, name=name,
        in_specs=[_HBM] * n + [_ANY],
        out_specs=[_SEM] * (2 * n) + [_HBM] * n + [_VMEM],
        out_shape=[pltpu.SemaphoreType.DMA((3,))] * (2 * n) + [pltpu.HBM(b.shape, b.dtype) for b in bufs]
        + [jax.ShapeDtypeStruct((8, 128), F32)],
        input_output_aliases={w: 2 * n + w for w in range(n)},
        compiler_params=pltpu.CompilerParams(has_side_effects=_EFFECT),
    )(*[pltpu.with_memory_space_constraint(b, pltpu.HBM) for b in bufs], after)
    return [(outs[w], outs[n + w], outs[2 * n + w]) for w in range(n)], outs[3 * n]


def _gather_wait(send_sems, recv_sems, buf, after, name):
    def body(buf_ref, send_ref, recv_ref, after_ref, out_ref):
        x, y, c, chips = _mesh_pos()
        me = _chip_id(x, y)
        for k, chip in enumerate(chips):
            _chip_copy(buf_ref, k, me, _half_rows(buf_ref, c), send_ref, recv_ref, (*chip, c)).wait_send()
        for k, chip in enumerate(chips):
            _chip_copy(buf_ref, k, _chip_id(*chip), _half_rows(buf_ref, c), send_ref, recv_ref, (*chip, c)).wait_recv()

    return pl.pallas_call(
        body, name=name,
        in_specs=[_HBM, _SEM, _SEM, _ANY], out_specs=_HBM,
        out_shape=pltpu.HBM(buf.shape, buf.dtype),
        input_output_aliases={0: 0},
        compiler_params=pltpu.CompilerParams(has_side_effects=_EFFECT),
    )(buf, send_sems, recv_sems, after)


def _sibling_fill(buf, name, own_too=False):
    n_copies = 4 if own_too else 3

    def body(buf_ref, out_ref, send_sems, recv_sems):
        x, y, c, chips = _mesh_pos()
        sibling = (x, y, 1 - c)
        slots = [_chip_id(*chip) for chip in chips] + ([_chip_id(x, y)] if own_too else [])
        copies = []
        for k, slot in enumerate(slots):
            cp = _chip_copy(out_ref, k, slot, _half_rows(out_ref, c), send_sems, recv_sems, sibling)
            cp.start()
            copies.append(cp)
        for k, slot in enumerate(slots):
            _chip_copy(out_ref, k, slot, _half_rows(out_ref, 1 - c), send_sems, recv_sems, sibling).wait_recv()
        for cp in copies:
            cp.wait_send()

    return pl.pallas_call(
        body, name=name,
        in_specs=[_HBM], out_specs=_HBM,
        out_shape=jax.ShapeDtypeStruct(buf.shape, buf.dtype),
        input_output_aliases={0: 0},
        scratch_shapes=[pltpu.SemaphoreType.DMA((n_copies,)), pltpu.SemaphoreType.DMA((n_copies,))],
    )(buf)


def _allgather_conv_w(cw):
    _, kw, cs = cw.shape

    def body(cw_ref, out_ref, send_sems, recv_sems):
        x, y, c, chips = _mesh_pos()
        me = _chip_id(x, y)
        out_ref[pl.ds(me, 1)] = cw_ref[...]
        copies = []
        for k, chip in enumerate(chips):
            cp = pltpu.make_async_remote_copy(
                src_ref=cw_ref.at[0], dst_ref=out_ref.at[me], send_sem=send_sems.at[k], recv_sem=recv_sems.at[k],
                device_id=(*chip, c), device_id_type=MESH)
            cp.start()
            copies.append(cp)
        for k, chip in enumerate(chips):
            pltpu.make_async_remote_copy(
                src_ref=cw_ref.at[0], dst_ref=out_ref.at[_chip_id(*chip)], send_sem=send_sems.at[k],
                recv_sem=recv_sems.at[k], device_id=(*chip, c), device_id_type=MESH).wait_recv()
        for cp in copies:
            cp.wait_send()

    return pl.pallas_call(
        body, name="allgather_conv_w",
        in_specs=[_VMEM], out_specs=_VMEM,
        out_shape=jax.ShapeDtypeStruct((N_CHIPS, kw, cs), F32),
        scratch_shapes=[pltpu.SemaphoreType.DMA((3,)), pltpu.SemaphoreType.DMA((3,))],
    )(cw)


def _exchange_halves(parts, after, name):
    n = len(parts)
    shapes = [p.shape for p in parts]

    def body(*refs):
        ins, outs = refs[:n], refs[n + 1:2 * n + 1]
        send_sems, recv_sems = refs[2 * n + 1:]
        x, y, c, _ = _mesh_pos()
        copies = []
        for w in range(n):
            hr = shapes[w][1] // 2
            cp = pltpu.make_async_remote_copy(
                src_ref=ins[w].at[:, pl.ds((1 - c) * hr, hr)], dst_ref=outs[w],
                send_sem=send_sems.at[w], recv_sem=recv_sems.at[w],
                device_id=(x, y, 1 - c), device_id_type=MESH)
            cp.start()
            copies.append(cp)
        for cp in copies:
            cp.wait()

    return pl.pallas_call(
        body, name=name,
        in_specs=[_HBM] * n + [_ANY], out_specs=[_HBM] * n,
        out_shape=[jax.ShapeDtypeStruct((s[0], s[1] // 2, s[2]), BF16) for s in shapes],
        scratch_shapes=[pltpu.SemaphoreType.DMA((n,)), pltpu.SemaphoreType.DMA((n,))],
    )(*parts, after)


def _add_halves(part, got, cvec, name):
    ns, r, cdim = part.shape
    hr = r // 2
    tr = _row_tile(hr, TR_ELT)
    nblk = hr // tr

    def body(c_ref, a_ref, b_ref, o_ref):
        o_ref[...] = (a_ref[...].astype(F32) + b_ref[...].astype(F32)).astype(BF16)

    grid_spec = pltpu.PrefetchScalarGridSpec(
        num_scalar_prefetch=1, grid=(ns, nblk),
        in_specs=[pl.BlockSpec((None, tr, cdim), lambda s, i, c_ref: (s, c_ref[0] * nblk + i, 0)),
                  pl.BlockSpec((None, tr, cdim), lambda s, i, c_ref: (s, i, 0))],
        out_specs=pl.BlockSpec((None, tr, cdim), lambda s, i, c_ref: (s, i, 0)))
    return pl.pallas_call(
        body, name=name, grid_spec=grid_spec,
        out_shape=jax.ShapeDtypeStruct((ns, hr, cdim), BF16),
        compiler_params=_params(("parallel", "parallel")),
    )(cvec, part, got)


def _scatter_copy(sums_ref, land_ref, k, src_slot, dst_slot, c, send_sems, recv_sems, to):
    return pltpu.make_async_remote_copy(
        src_ref=sums_ref.at[src_slot], dst_ref=land_ref.at[dst_slot, _half_rows(land_ref, c)],
        send_sem=send_sems.at[k], recv_sem=recv_sems.at[k], device_id=to, device_id_type=MESH)


def _scatter_start(sums, name):
    ns, hr, cdim = sums.shape
    land = lax.empty((ns, 2 * hr, cdim), sums.dtype)

    def body(sums_ref, land_ref, send_sems, recv_sems, sums_thru, land_thru):
        x, y, c, chips = _mesh_pos()
        me = _chip_id(x, y)
        for k, chip in enumerate(chips):
            _scatter_copy(sums_ref, land_ref, k, _chip_id(*chip), me, c, send_sems, recv_sems, (*chip, c)).start()

    return pl.pallas_call(
        body, name=name,
        in_specs=[_HBM, _HBM], out_specs=[_SEM, _SEM, _HBM, _HBM],
        out_shape=[pltpu.SemaphoreType.DMA((3,)), pltpu.SemaphoreType.DMA((3,)),
                   pltpu.HBM(sums.shape, sums.dtype), pltpu.HBM(land.shape, land.dtype)],
        input_output_aliases={0: 2, 1: 3},
        compiler_params=pltpu.CompilerParams(has_side_effects=_EFFECT),
    )(pltpu.with_memory_space_constraint(sums, pltpu.HBM), pltpu.with_memory_space_constraint(land, pltpu.HBM))


def _scatter_wait(send_sems, recv_sems, sums, land, after, name):
    def body(sums_ref, land_ref, send_ref, recv_ref, after_ref, sums_out, land_out):
        x, y, c, chips = _mesh_pos()
        me = _chip_id(x, y)
        for k, chip in enumerate(chips):
            _scatter_copy(sums_ref, land_ref, k, _chip_id(*chip), me, c, send_ref, recv_ref, (*chip, c)).wait_send()
        for k, chip in enumerate(chips):
            _scatter_copy(sums_ref, land_ref, k, me, _chip_id(*chip), c, send_ref, recv_ref, (*chip, c)).wait_recv()

    return pl.pallas_call(
        body, name=name,
        in_specs=[_HBM, _HBM, _SEM, _SEM, _ANY], out_specs=[_HBM, _HBM],
        out_shape=[pltpu.HBM(sums.shape, sums.dtype), pltpu.HBM(land.shape, land.dtype)],
        input_output_aliases={0: 0, 1: 1},
        compiler_params=pltpu.CompilerParams(has_side_effects=_EFFECT),
    )(sums, land, send_sems, recv_sems, after)


def _complete_plan(n_weights):
    def plan(refs, send_sems, recv_sems):
        x, y, c, chips = _mesh_pos()
        me = _chip_id(x, y)
        sibling = (x, y, 1 - c)
        sends, recvs = [], []
        for w in range(n_weights):
            sums, land = refs[2 * w], refs[2 * w + 1]
            sends.append(_scatter_copy(sums, land, 4 * w + 3, me, me, c, send_sems, recv_sems, sibling))
            recvs.append(_scatter_copy(sums, land, 4 * w + 3, me, me, 1 - c, send_sems, recv_sems, sibling))
            for k, chip in enumerate(chips):
                slot = _chip_id(*chip)
                sends.append(_chip_copy(land, 4 * w + k, slot, _half_rows(land, c), send_sems, recv_sems, sibling))
                recvs.append(_chip_copy(land, 4 * w + k, slot, _half_rows(land, 1 - c), send_sems, recv_sems, sibling))
        return sends, recvs
    return plan


def _complete_chip_sums(sums, lands):
    n = len(sums)

    def body(*refs):
        sums_refs, outs = refs[:n], refs[2 * n:3 * n]
        send_sems, recv_sems = refs[3 * n:]
        x, y, c, chips = _mesh_pos()
        me = _chip_id(x, y)
        sibling = (x, y, 1 - c)
        slots = [_chip_id(*chip) for chip in chips]
        sent = []
        for w in range(n):
            out = outs[w]
            cp = _scatter_copy(sums_refs[w], out, 3, me, me, c, send_sems.at[w], recv_sems.at[w], sibling)
            cp.start()
            sent.append(cp)
            for k, slot in enumerate(slots):
                cp = _chip_copy(out, k, slot, _half_rows(out, c), send_sems.at[w], recv_sems.at[w], sibling)
                cp.start()
                sent.append(cp)
        for w in range(n):
            out = outs[w]
            _scatter_copy(sums_refs[w], out, 3, me, me, 1 - c, send_sems.at[w], recv_sems.at[w], sibling).wait_recv()
            for k, slot in enumerate(slots):
                _chip_copy(out, k, slot, _half_rows(out, 1 - c), send_sems.at[w], recv_sems.at[w], sibling).wait_recv()
        for cp in sent:
            cp.wait_send()

    return pl.pallas_call(
        body, name="complete_chip_sums",
        in_specs=[_HBM] * (2 * n), out_specs=[_HBM] * n,
        out_shape=[jax.ShapeDtypeStruct(b.shape, b.dtype) for b in lands],
        input_output_aliases={n + w: w for w in range(n)},
        scratch_shapes=[pltpu.SemaphoreType.DMA((n, 4)), pltpu.SemaphoreType.DMA((n, 4))],
    )(*sums, *lands)


SMALL_ROWS = 8


def _allreduce_small(gl2g, gl2b, gl1g, gl1b, g_ac, gcw, gsink, loss, after):
    d = gl2g.shape[1]
    hd = d // 2
    nq = gsink.shape[1]

    def body(a_ref, b_ref, c_ref, d_ref, e_ref, cw_ref, sk_ref, ls_ref, after_ref, out_ref, mine, gath, send_sems,
             recv_sems):
        x, y, c, _ = _mesh_pos()
        me = 4 * x + 2 * y + c
        mine[...] = jnp.zeros_like(mine)
        mine[0:1, :] = a_ref[...]
        mine[1:2, :] = b_ref[...]
        mine[2:3, :] = c_ref[...]
        mine[3:4, :] = d_ref[...]
        mine[4:5, :] = e_ref[...]
        mine[5:6, 0:hd] = cw_ref[0:1, :]
        mine[5:6, hd:d] = cw_ref[1:2, :]
        mine[6:7, 0:hd] = cw_ref[2:3, :]
        mine[6:7, hd:hd + nq] = sk_ref[...]
        mine[6:7, hd + 128:hd + 256] = ls_ref[...]
        gath[pl.ds(me, 1)] = mine[...][None]
        copies = []
        for r in range(1, 8):
            peer = ((1 - x) if r & 4 else x, (1 - y) if r & 2 else y, (1 - c) if r & 1 else c)
            cp = pltpu.make_async_remote_copy(
                src_ref=mine, dst_ref=gath.at[me], send_sem=send_sems.at[r - 1], recv_sem=recv_sems.at[r - 1],
                device_id=peer, device_id_type=MESH)
            cp.start()
            copies.append(cp)
        for r in range(1, 8):
            peer = ((1 - x) if r & 4 else x, (1 - y) if r & 2 else y, (1 - c) if r & 1 else c)
            peer_id = 4 * peer[0] + 2 * peer[1] + peer[2]
            pltpu.make_async_remote_copy(
                src_ref=mine, dst_ref=gath.at[peer_id], send_sem=send_sems.at[r - 1], recv_sem=recv_sems.at[r - 1],
                device_id=peer, device_id_type=MESH).wait_recv()
        for cp in copies:
            cp.wait_send()
        total = gath[0]
        for dev in range(1, 8):
            total = total + gath[dev]
        out_ref[...] = total

    return pl.pallas_call(
        body, name="allreduce_small",
        in_specs=[_VMEM] * 8 + [_ANY], out_specs=_VMEM,
        out_shape=jax.ShapeDtypeStruct((SMALL_ROWS, d), F32),
        scratch_shapes=[pltpu.VMEM((SMALL_ROWS, d), F32), pltpu.VMEM((8, SMALL_ROWS, d), F32),
                        pltpu.SemaphoreType.DMA((7,)), pltpu.SemaphoreType.DMA((7,))],
    )(gl2g, gl2b, gl1g, gl1b, g_ac, gcw, gsink, loss, after)


def _adamw(w, g, m, v):
    m = ADAM_B1 * m + (1.0 - ADAM_B1) * g
    v = ADAM_B2 * v + (1.0 - ADAM_B2) * (g * g)
    m_hat = m / (1.0 - ADAM_B1 ** ADAM_STEP)
    v_hat = v / (1.0 - ADAM_B2 ** ADAM_STEP)
    delta = -ADAM_LR * (m_hat / (jnp.sqrt(v_hat) + ADAM_EPS) + ADAM_WD * w)
    return delta, m, v


def _adamw_shard(w, m, v, land, own, pos_vec, name, col_block=0):
    tr = _row_tile(w.shape[1] // 2, TR_ELT)
    grid = (w.shape[1] // tr,)
    body, in_specs, out_specs, out_shape = _adamw_passenger(w.shape, tr, grid, col_block)
    grid_spec = pltpu.PrefetchScalarGridSpec(num_scalar_prefetch=1, grid=grid, in_specs=in_specs, out_specs=out_specs)
    return pl.pallas_call(
        body, name=name, grid_spec=grid_spec, out_shape=out_shape,
        compiler_params=_params(("parallel",)),
    )(pos_vec, w, m, v, land, land, land, land, own)


def _adamw_passenger(shape, tr, grid, col_block):
    _, r, c = shape
    nh = r // 2 // tr
    n_blocks = 2 * nh
    n_steps = int(np.prod(grid))
    assert nh * tr * 2 == r and n_blocks <= n_steps

    def step_of(ids):
        step = ids[0]
        for n, i in zip(grid[1:], ids[1:]):
            step = step * n + i
        return step

    def block_of(ids):
        return jnp.minimum(step_of(ids), n_blocks - 1)

    def update(pos_ref, w_ref, m_ref, v_ref, l0, l1, l2, l3, own_ref, g_out, d_out, m_out, v_out):
        i = block_of([pl.program_id(a) for a in range(len(grid))])
        mine = (i // nh) == pos_ref[1]
        own_blk = own_ref[...].astype(F32)
        g = None
        for s, l_ref in enumerate([l0, l1, l2, l3]):
            term = jnp.where(mine & (pos_ref[0] == s), own_blk, l_ref[...].astype(F32))
            g = term if g is None else g + term
        delta, nm, nv = _adamw(w_ref[...], g, m_ref[...], v_ref[...])
        g_out[...] = g
        d_out[...] = delta
        m_out[...] = nm
        v_out[...] = nv

    def body(*refs):
        if n_blocks == n_steps:
            update(*refs)
        else:
            pl.when(step_of([pl.program_id(a) for a in range(len(grid))]) < n_blocks)(lambda: update(*refs))

    def land_spec(s):
        def index(*args):
            i, pos_ref = block_of(args[:-1]), args[-1]
            skip = (pos_ref[0] == s) & ((i // nh) == pos_ref[1])
            return (s, jnp.where(skip, (i + nh) % n_blocks, i), col_block)
        return pl.BlockSpec((None, tr, c), index)

    blk = pl.BlockSpec((None, tr, c), lambda *args: (0, block_of(args[:-1]), 0))
    in_specs = ([blk, blk, blk] + [land_spec(s) for s in range(N_CHIPS)]
                + [pl.BlockSpec((None, tr, c), lambda *args: (args[-1][0], block_of(args[:-1]) % nh, col_block))])
    return body, in_specs, [blk] * 4, [jax.ShapeDtypeStruct((1, r, c), F32)] * 4


def _call_with_adamw(body, name, grid, in_specs, out_specs, out_shape, scratch_shapes, semantics, operands, shard):
    w, m, v, land, own, pos_vec, col_block = shard
    n_steps = int(np.prod(grid))
    hr = w.shape[1] // 2
    tr = min(t for t in range(16, hr + 1, 16) if hr % t == 0 and 2 * (hr // t) <= n_steps)
    adam_body, adam_in, adam_out, adam_shape = _adamw_passenger(w.shape, tr, grid, col_block)
    n_in, n_out = len(in_specs), len(out_specs)

    def with_pos(spec):
        if spec.index_map is None:
            return spec
        return pl.BlockSpec(spec.block_shape, lambda *args: spec.index_map(*args[:-1]))

    def both(pos_ref, *refs):
        ins, adam_ins = refs[:n_in], refs[n_in:n_in + len(adam_in)]
        refs = refs[n_in + len(adam_in):]
        outs, adam_outs, scratch = refs[:n_out], refs[n_out:n_out + len(adam_out)], refs[n_out + len(adam_out):]
        body(*ins, *outs, *scratch)
        adam_body(pos_ref, *adam_ins, *adam_outs)

    grid_spec = pltpu.PrefetchScalarGridSpec(
        num_scalar_prefetch=1, grid=grid, in_specs=[with_pos(sp) for sp in in_specs] + adam_in,
        out_specs=[with_pos(sp) for sp in out_specs] + adam_out, scratch_shapes=scratch_shapes)
    return pl.pallas_call(
        both, name=name, grid_spec=grid_spec, out_shape=list(out_shape) + adam_shape,
        compiler_params=_params(semantics),
    )(pos_vec, *operands, w, m, v, land, land, land, land, own)


def _adamw_small(red, params):
    names = ["sinks", "g_attn", "g_conv", "ln1_g", "ln1_b", "ln2_g", "ln2_b", "conv_w"]
    d = red.shape[1]
    hd = d // 2
    flat = []
    for nme in names:
        flat.extend(params[nme])
    nq = params["sinks"][0].shape[1]
    cs = params["conv_w"][0].shape[2]

    def body(*refs):
        red_ref = refs[0]
        ins = refs[1:1 + 3 * len(names)]
        outs = refs[1 + 3 * len(names):]
        x, y, _, _ = _mesh_pos()
        me = _chip_id(x, y)

        def conv_tap(row, base):
            picked = red_ref[row:row + 1, base:base + cs]
            for s in range(1, N_CHIPS):
                picked = jnp.where(me == s, red_ref[row:row + 1, base + s * cs:base + (s + 1) * cs], picked)
            return picked

        grads = {
            "sinks": red_ref[6:7, hd:hd + nq],
            "g_attn": red_ref[4:5, 0:hd],
            "g_conv": red_ref[4:5, hd:d],
            "ln1_g": red_ref[2:3, :],
            "ln1_b": red_ref[3:4, :],
            "ln2_g": red_ref[0:1, :],
            "ln2_b": red_ref[1:2, :],
        }
        for i, nme in enumerate(names):
            w_ref, m_ref, v_ref = ins[3 * i:3 * i + 3]
            g_out, d_out, m_out, v_out = outs[4 * i:4 * i + 4]
            if nme == "conv_w":
                for tap, (row, base) in enumerate([(5, 0), (5, hd), (6, 0)]):
                    g = conv_tap(row, base)
                    delta, nm, nv = _adamw(w_ref[0, tap:tap + 1, :], g, m_ref[0, tap:tap + 1, :], v_ref[0, tap:tap + 1, :])
                    g_out[0, tap:tap + 1, :] = g
                    d_out[0, tap:tap + 1, :] = delta
                    m_out[0, tap:tap + 1, :] = nm
                    v_out[0, tap:tap + 1, :] = nv
            else:
                g = grads[nme]
                delta, nm, nv = _adamw(w_ref[...], g, m_ref[...], v_ref[...])
                g_out[...] = g
                d_out[...] = delta
                m_out[...] = nm
                v_out[...] = nv

    out_shape = []
    for nme in names:
        out_shape.extend([jax.ShapeDtypeStruct(params[nme][0].shape, F32)] * 4)
    outs = pl.pallas_call(
        body, name="adamw_small",
        in_specs=[_VMEM] * (1 + len(flat)), out_specs=[_VMEM] * len(out_shape),
        out_shape=out_shape,
    )(red, *flat)
    return {nme: tuple(outs[4 * i:4 * i + 4]) for i, nme in enumerate(names)}


def _rope_tables(pos_col):
    s = pos_col.shape[0]
    w = N_KV_HEADS * HEAD_DIM
    tb = min(512, s)
    inv_freq = (ROPE_THETA ** (-np.arange(0, ROT_DIM, 2, dtype=np.float32) / ROT_DIM)).astype(np.float32)

    def body(pos_ref, cos_ref, sin_ref):
        pos = pos_ref[...].astype(F32)
        lane = lax.broadcasted_iota(jnp.int32, (tb, PAIR), 1) & (HEAD_DIM - 1)
        fidx = lane & (ROT_DIM // 2 - 1)
        inv = jnp.zeros((tb, PAIR), F32)
        for k in range(ROT_DIM // 2):
            inv = jnp.where(fidx == k, float(inv_freq[k]), inv)
        ang = pos * inv
        rot = lane < ROT_DIM
        sin_v = jnp.sin(ang)
        cos_ref[...] = _tile_lanes(jnp.where(rot, jnp.cos(ang), 1.0), w // PAIR)
        sin_ref[...] = _tile_lanes(jnp.where(lane < ROT_DIM // 2, -sin_v, jnp.where(rot, sin_v, 0.0)), w // PAIR)

    return pl.pallas_call(
        body, name="rope_tables", grid=(s // tb,),
        in_specs=[pl.BlockSpec((tb, 1), lambda i: (i, 0))],
        out_specs=[pl.BlockSpec((tb, w), lambda i: (i, 0))] * 2,
        out_shape=[jax.ShapeDtypeStruct((s, w), F32)] * 2,
        compiler_params=_params(("parallel",)),
    )(pos_col)


def _in_proj(x, w_in_g, first_vec, n_shards, into, name):
    _, s, d = x.shape
    ns, _, ncol = w_in_g.shape
    tm = min(2 * TM, s)

    def body(first_ref, x_ref, w_ref, into_ref, o_ref):
        o_ref[...] = _dot(x_ref[...].astype(BF16), w_ref[...]).astype(BF16)

    shard = lambda j, first_ref: lax.rem(first_ref[0] + j, ns)
    grid_spec = pltpu.PrefetchScalarGridSpec(
        num_scalar_prefetch=1, grid=(s // tm, n_shards),
        in_specs=[pl.BlockSpec((None, tm, d), lambda i, j, first_ref: (0, i, 0)),
                  pl.BlockSpec((None, d, ncol), lambda i, j, first_ref: (shard(j, first_ref), 0, 0)), _ANY],
        out_specs=pl.BlockSpec((tm, ncol), lambda i, j, first_ref: (i, shard(j, first_ref))))
    return pl.pallas_call(
        body, name=name, grid_spec=grid_spec,
        out_shape=jax.ShapeDtypeStruct((s, ns * ncol), BF16),
        input_output_aliases={} if into is None else {3: 0},
        compiler_params=_params(("parallel", "arbitrary")),
    )(first_vec, x, w_in_g, first_vec if into is None else into)


PAIR = 2 * HEAD_DIM
KEYS = 2 * WINDOW


def _pair_operand(t_all, h):
    col = (h // 2) * PAIR
    lane = lax.broadcasted_iota(jnp.int32, (KEYS, PAIR), 1)
    own_low = h % 2 == 0
    mine = jnp.where((lane < HEAD_DIM) if own_low else (lane >= HEAD_DIM), t_all[:, col:col + PAIR], 0.0)
    other = pltpu.roll(mine, HEAD_DIM, 1)
    low, high = (mine, other) if own_low else (other, mine)
    return jnp.concatenate([low, high], axis=0).astype(BF16)


def _pair_grad(acc, h):
    lane = lax.broadcasted_iota(jnp.int32, (KEYS, PAIR), 1)
    low = jnp.where(lane < HEAD_DIM, acc[:KEYS], 0.0)
    high = jnp.where(lane >= HEAD_DIM, acc[KEYS:], 0.0)
    if h % 2 == 0:
        return low + pltpu.roll(high, HEAD_DIM, 1)
    return high + pltpu.roll(low, HEAD_DIM, 1)


N_PAIRS = N_KV_HEADS * GROUP // 2


def _all_probs(q, kk2s, first, sinks_ref):
    assert ATTN_SCALE == 0.125
    q = q * ATTN_SCALE
    qps, scores = [], []
    for pair in range(N_PAIRS):
        qp = q[:, pair * PAIR:(pair + 1) * PAIR].astype(BF16)
        qps.append(qp)
        scores.append(_dot_nt(qp, kk2s[pair // (GROUP // 2)]))
    qi = lax.broadcasted_iota(jnp.int32, (WINDOW, 2 * KEYS), 0)
    kj = lax.broadcasted_iota(jnp.int32, (WINDOW, 2 * KEYS), 1) & (KEYS - 1)
    rel = qi + WINDOW - kj
    valid = (rel >= 0) & (rel < WINDOW) & jnp.logical_not(first & (kj < WINDOW))
    bias = jnp.where(valid, 0.0, NEG_BIG)
    s = (jnp.stack(scores, axis=0) + bias[None]).reshape(N_PAIRS * WINDOW, 2 * KEYS)
    probs, p_sinks = [], []
    for t in range(2):
        st = s[:, t * KEYS:(t + 1) * KEYS]
        sink = jnp.concatenate([jnp.broadcast_to(sinks_ref[0:1, 2 * pair + t:2 * pair + t + 1], (WINDOW, 1))
                                for pair in range(N_PAIRS)], axis=0)
        m = jnp.maximum(jnp.max(st, axis=1, keepdims=True), sink)
        e = jnp.exp(st - m)
        e_sink = jnp.exp(sink - m)
        inv_l = 1.0 / (jnp.sum(e, axis=1, keepdims=True) + e_sink)
        probs.append(e * inv_l)
        p_sinks.append(e_sink * inv_l)
    return qps, jnp.concatenate(probs, axis=1), p_sinks


def _roped_qkv(cur_ref, prev_ref, cos_ref, sin_ref, cosp_ref, sinp_ref, qw, kvw):
    cur = cur_ref[...].astype(F32)
    cos, sin = cos_ref[...], sin_ref[...]
    cos_q, sin_q = _tile_lanes(cos, GROUP), _tile_lanes(sin, GROUP)
    q = _rope(cur[:, :qw], cos_q, sin_q, 1.0)
    prev = prev_ref[...].astype(F32)
    k_all = jnp.concatenate([_rope(prev[:, :kvw], cosp_ref[...], sinp_ref[...], 1.0),
                             _rope(cur[:, qw:qw + kvw], cos, sin, 1.0)], axis=0)
    v_all = jnp.concatenate([prev[:, kvw:], cur[:, qw + kvw:]], axis=0)
    return q, k_all, v_all, cos_q, sin_q


def _attention_fwd(proj, cos_t, sin_t, sinks):
    s = proj.shape[0]
    qw = GROUP * N_KV_HEADS * HEAD_DIM
    kvw = N_KV_HEADS * HEAD_DIM
    nb = s // WINDOW

    def body(cur_ref, prev_ref, cos_ref, sin_ref, cosp_ref, sinp_ref, sinks_ref, o_ref):
        first = pl.program_id(0) == 0
        q, k_all, v_all, _, _ = _roped_qkv(cur_ref, prev_ref, cos_ref, sin_ref, cosp_ref, sinp_ref, qw, kvw)
        kk2s = [_pair_operand(k_all, h) for h in range(N_KV_HEADS)]
        vv2s = [_pair_operand(v_all, h) for h in range(N_KV_HEADS)]
        _, probs, _ = _all_probs(q, kk2s, first, sinks_ref)
        probs = probs.astype(BF16)
        outs = [_dot(probs[pair * WINDOW:(pair + 1) * WINDOW], vv2s[pair // (GROUP // 2)]) for pair in range(N_PAIRS)]
        o_ref[...] = jnp.concatenate(outs, axis=1)

    tbl = pl.BlockSpec((WINDOW, kvw), lambda n: (n, 0))
    tbl_prev = pl.BlockSpec((WINDOW, kvw), lambda n: (jnp.maximum(n - 1, 0), 0))
    return pl.pallas_call(
        body, name="attention_fwd", grid=(nb,),
        in_specs=[pl.BlockSpec((WINDOW, qw + 2 * kvw), lambda n: (n, 0)),
                  pl.BlockSpec((WINDOW, 2 * kvw), lambda n: (jnp.maximum(n - 1, 0), (qw // (2 * kvw)))),
                  tbl, tbl, tbl_prev, tbl_prev, _VMEM],
        out_specs=pl.BlockSpec((WINDOW, qw), lambda n: (n, 0)),
        out_shape=jax.ShapeDtypeStruct((s, qw), F32),
        compiler_params=_params(("parallel",)),
    )(proj, proj, cos_t, sin_t, cos_t, sin_t, sinks)


def _conv_taps(cw_ref):
    return [jnp.concatenate([cw_ref[s, k:k + 1, :] for s in range(N_CHIPS)], axis=1) for k in range(3)]


def _shift_down(z, halo, steps):
    last = halo.shape[0]
    row = lax.broadcasted_iota(jnp.int32, z.shape, 0)
    out = pltpu.roll(z, steps, 0)
    for r in range(steps):
        out = jnp.where(row == r, halo[last - steps + r:last - steps + r + 1, :], out)
    return out


def _shift_up(z, halo, steps):
    rows = z.shape[0]
    row = lax.broadcasted_iota(jnp.int32, z.shape, 0)
    out = pltpu.roll(z, rows - steps, 0)
    for r in range(steps):
        out = jnp.where(row == rows - steps + r, halo[r:r + 1, :], out)
    return out


def _split_cbu(lo, hi, cw):
    lo, hi = lo.astype(F32), hi.astype(F32)
    c_gate = lo[:, :cw]
    b_gate = jnp.concatenate([lo[:, cw:], hi[:, :2 * cw - lo.shape[1]]], axis=1)
    u = hi[:, 2 * cw - lo.shape[1]:]
    return c_gate, b_gate, u


def _conv_norm(proj, attn, cw_full, g_ac):
    s, in_w = proj.shape
    cw = attn.shape[1]
    blk_w = in_w // 3
    tb = min(TB_CONV, s)

    def body(lo_ref, hi_ref, lo_h_ref, hi_h_ref, attn_ref, cw_ref, g_ref, mixed_ref, ac_ref, rstd_ref):
        i = pl.program_id(0)
        c_gate, b_gate, u = _split_cbu(lo_ref[...], hi_ref[...], cw)
        c_h, _, u_h = _split_cbu(lo_h_ref[...], hi_h_ref[...], cw)
        z = c_gate * u
        z_h = jnp.where(i == 0, 0.0, c_h * u_h)
        w0, w1, w2 = _conv_taps(cw_ref)
        y = w0 * _shift_down(z, z_h, 2) + w1 * _shift_down(z, z_h, 1) + w2 * z
        conv = b_gate * y
        a = attn_ref[...]
        r_a = lax.rsqrt(jnp.mean(a * a, axis=-1, keepdims=True) + RMS_EPS)
        r_c = lax.rsqrt(jnp.mean(conv * conv, axis=-1, keepdims=True) + RMS_EPS)
        g = g_ref[...]
        mixed_ref[...] = jnp.concatenate([a * r_a * g[:, :cw], conv * r_c * g[:, cw:]], axis=1).astype(BF16)
        ac_ref[...] = jnp.concatenate([a, conv], axis=1)
        rstd_ref[0] = r_a
        rstd_ref[1] = r_c

    halo_idx = lambda i: jnp.maximum(i * (tb // HALO_ROWS) - 1, 0)
    return pl.pallas_call(
        body, name="conv_norm", grid=(s // tb,),
        in_specs=[pl.BlockSpec((tb, blk_w), lambda i: (i, 1)),
                  pl.BlockSpec((tb, blk_w), lambda i: (i, 2)),
                  pl.BlockSpec((HALO_ROWS, blk_w), lambda i: (halo_idx(i), 1)),
                  pl.BlockSpec((HALO_ROWS, blk_w), lambda i: (halo_idx(i), 2)),
                  pl.BlockSpec((tb, cw), lambda i: (i, 0)),
                  _VMEM, _VMEM],
        out_specs=[pl.BlockSpec((tb, 2 * cw), lambda i: (i, 0)),
                   pl.BlockSpec((tb, 2 * cw), lambda i: (i, 0)),
                   pl.BlockSpec((2, tb, 1), lambda i: (0, i, 0))],
        out_shape=[jax.ShapeDtypeStruct((s, 2 * cw), BF16), jax.ShapeDtypeStruct((s, 2 * cw), F32),
                   jax.ShapeDtypeStruct((2, s, 1), F32)],
        compiler_params=_params(("parallel",)),
    )(proj, proj, proj, proj, attn, cw_full, g_ac)


def _out_proj_ln(mixed, w_out_g, x, ln_g, ln_b):
    s, d = mixed.shape
    tm = min(TM, s)
    tk = d
    nk = d // tk

    def body(a_ref, w_ref, x_ref, g_ref, b_ref, xhat_ref, h_ref, rstd_ref, acc):
        k = pl.program_id(1)
        _accumulate(acc, lambda: _dot(a_ref[...], w_ref[...]), k, nk)

        @pl.when(k == nk - 1)
        def _():
            def rows_fn(rows):
                xhat, rstd = _ln_fwd(ALPHA * x_ref[rows, :] + acc[rows, :])
                xhat_ref[rows, :] = xhat
                h_ref[rows, :] = (xhat * g_ref[...] + b_ref[...]).astype(BF16)
                rstd_ref[rows, :] = rstd

            _for_row_chunks(tm, rows_fn)

    row = pl.BlockSpec((tm, d), lambda i, k: (i, 0))
    return pl.pallas_call(
        body, name="out_proj_ln", grid=(s // tm, nk),
        in_specs=[pl.BlockSpec((tm, tk), lambda i, k: (i, k)),
                  pl.BlockSpec((tk, d), lambda i, k: (k, 0)),
                  pl.BlockSpec((None, tm, d), lambda i, k: (0, i, 0)),
                  _VMEM, _VMEM],
        out_specs=[row, row, pl.BlockSpec((tm, 1), lambda i, k: (i, 0))],
        out_shape=[jax.ShapeDtypeStruct((s, d), F32), jax.ShapeDtypeStruct((s, d), BF16),
                   jax.ShapeDtypeStruct((s, 1), F32)],
        scratch_shapes=[pltpu.VMEM((tm, d), F32)],
        compiler_params=_params(("parallel", "arbitrary")),
    )(mixed, w_out_g, x, ln_g, ln_b)


def _gate_up(h1, w_gu_g, first_vec, n_shards, into, name):
    s, d = h1.shape
    ns, _, fs2 = w_gu_g.shape
    fs = fs2 // 2
    tm = min(TM, s)

    def body(first_ref, h_ref, w_ref, act_in, ab_in, act_ref, ab_ref):
        gu = _dot(h_ref[...], w_ref[...])
        g, u = gu[:, :fs], gu[:, fs:]
        sg = _sigmoid(g)
        silu = g * sg
        act_ref[...] = (silu * u).astype(BF16)
        ab_ref[:, :fs] = (u * (sg * (1.0 + g * (1.0 - sg)))).astype(BF16)
        ab_ref[:, fs:] = silu.astype(BF16)

    shard = lambda j, first_ref: lax.rem(first_ref[0] + j, ns)
    grid_spec = pltpu.PrefetchScalarGridSpec(
        num_scalar_prefetch=1, grid=(s // tm, n_shards),
        in_specs=[pl.BlockSpec((tm, d), lambda i, j, first_ref: (i, 0)),
                  pl.BlockSpec((None, d, fs2), lambda i, j, first_ref: (shard(j, first_ref), 0, 0)), _ANY, _ANY],
        out_specs=[pl.BlockSpec((tm, fs), lambda i, j, first_ref: (i, shard(j, first_ref))),
                   pl.BlockSpec((tm, fs2), lambda i, j, first_ref: (i, shard(j, first_ref)))])
    return pl.pallas_call(
        body, name=name, grid_spec=grid_spec,
        out_shape=[jax.ShapeDtypeStruct((s, ns * fs), BF16), jax.ShapeDtypeStruct((s, ns * fs2), BF16)],
        input_output_aliases={} if into is None else {3: 0, 4: 1},
        compiler_params=_params(("parallel", "arbitrary")),
    )(first_vec, h1, w_gu_g, *((first_vec, first_vec) if into is None else into))


def _down_ln_loss(act, w_down_g, xhat1, ln1_g, ln1_b, ln2_g, ln2_b, target):
    s, f = act.shape
    d = xhat1.shape[1]
    tm = min(TM, s)
    tk = f // N_CHIPS
    nk = f // tk

    def body(a_ref, w_ref, xh_ref, g1_ref, b1_ref, g2_ref, b2_ref, t_ref, dpre_ref, dpre16_ref, loss_ref, gg_ref, gb_ref,
             acc):
        i, k = pl.program_id(0), pl.program_id(1)
        _accumulate(acc, lambda: _dot(a_ref[...], w_ref[...]), k, nk)

        @pl.when(k == nk - 1)
        def _():
            @pl.when(i == 0)
            def _():
                loss_ref[...] = jnp.zeros_like(loss_ref)
                gg_ref[...] = jnp.zeros_like(gg_ref)
                gb_ref[...] = jnp.zeros_like(gb_ref)

            def rows_fn(rows):
                h1 = xh_ref[rows, :] * g1_ref[...] + b1_ref[...]
                xhat, rstd = _ln_fwd(ALPHA * h1 + acc[rows, :])
                g2 = g2_ref[...]
                diff = xhat * g2 + b2_ref[...] - t_ref[rows, :]
                dy = diff * (1.0 / d)
                dpre = _ln_bwd(dy, xhat, rstd, g2)
                dpre_ref[rows, :] = dpre
                dpre16_ref[rows, :] = dpre.astype(BF16)
                sq = jnp.sum(jnp.sum(diff * diff, axis=1, keepdims=True), axis=0, keepdims=True)
                loss_ref[...] += jnp.broadcast_to(sq * (0.5 / d), (1, 128))
                gg_ref[...] += jnp.sum(dy * xhat, axis=0, keepdims=True)
                gb_ref[...] += jnp.sum(dy, axis=0, keepdims=True)

            _for_row_chunks(tm, rows_fn)

    row = pl.BlockSpec((tm, d), lambda i, k: (i, 0))
    vec = pl.BlockSpec((1, d), lambda i, k: (0, 0))
    return pl.pallas_call(
        body, name="down_ln_loss", grid=(s // tm, nk),
        in_specs=[pl.BlockSpec((tm, tk), lambda i, k: (i, k)),
                  pl.BlockSpec((tk, d), lambda i, k: (k, 0)),
                  row, _VMEM, _VMEM, _VMEM, _VMEM,
                  pl.BlockSpec((None, tm, d), lambda i, k: (0, i, 0))],
        out_specs=[row, row, pl.BlockSpec((1, 128), lambda i, k: (0, 0)), vec, vec],
        out_shape=[jax.ShapeDtypeStruct((s, d), F32), jax.ShapeDtypeStruct((s, d), BF16),
                   jax.ShapeDtypeStruct((1, 128), F32), jax.ShapeDtypeStruct((1, d), F32),
                   jax.ShapeDtypeStruct((1, d), F32)],
        scratch_shapes=[pltpu.VMEM((tm, d), F32)],
        compiler_params=_params(("arbitrary", "arbitrary")),
    )(act, w_down_g, xhat1, ln1_g, ln1_b, ln2_g, ln2_b, target)


def _dact_silu_bwd(dpre2, w_down_g, ab):
    s, d = dpre2.shape
    fs2 = ab.shape[1] // N_CHIPS
    fs = fs2 // 2
    tm = min(TM, s)

    def body(dp_ref, w_ref, ab_ref, dgu_ref):
        d_act = _dot_nt(dp_ref[...], w_ref[...])
        dgu_ref[:, :fs] = (d_act * ab_ref[:, :fs].astype(F32)).astype(BF16)
        dgu_ref[:, fs:] = (d_act * ab_ref[:, fs:].astype(F32)).astype(BF16)

    blk = pl.BlockSpec((tm, fs2), lambda j, i: (i, j))
    return pl.pallas_call(
        body, name="dact_silu_bwd", grid=(N_CHIPS, s // tm),
        in_specs=[pl.BlockSpec((tm, d), lambda j, i: (i, 0)),
                  pl.BlockSpec((fs, d), lambda j, i: (j, 0)), blk],
        out_specs=blk,
        out_shape=jax.ShapeDtypeStruct(ab.shape, BF16),
        compiler_params=_params(("parallel", "parallel")),
    )(dpre2, w_down_g, ab)


def _grad_rows(a, b, after, name, row_blocks=1):
    s, m = a.shape
    n = b.shape[1]
    ms = m // N_CHIPS
    tmw = ms // row_blocks
    tk = min(TK_TOK, s)
    nk = s // tk

    def body(a_ref, b_ref, after_ref, o_ref, acc):
        k = pl.program_id(2)
        _accumulate(acc, lambda: _dot_tn(a_ref[...].astype(BF16), b_ref[...].astype(BF16)), k, nk)

        @pl.when(k == nk - 1)
        def _():
            o_ref[...] = acc[...].astype(BF16)

    return pl.pallas_call(
        body, name=name, grid=(N_CHIPS, row_blocks, nk),
        in_specs=[pl.BlockSpec((tk, tmw), lambda j, r, k: (k, j * row_blocks + r)),
                  pl.BlockSpec((tk, n), lambda j, r, k: (k, 0)), _ANY],
        out_specs=pl.BlockSpec((None, tmw, n), lambda j, r, k: (j, r, 0)),
        out_shape=jax.ShapeDtypeStruct((N_CHIPS, ms, n), BF16),
        scratch_shapes=[pltpu.VMEM((tmw, n), F32)],
        compiler_params=_params(("parallel", "parallel", "arbitrary")),
    )(a, b, after)


def _grad_cols(a, bs, after, name, a_3d=False, row_blocks=2):
    s, m = a.shape[-2:]
    n = bs[0].shape[1]
    ns = n // N_CHIPS
    nb = len(bs)
    tmw = m // row_blocks
    tk = min(TK_TOK, s)
    nk = s // tk

    def body(*refs):
        a_ref, b_refs, o_refs, accs = refs[0], refs[1:1 + nb], refs[2 + nb:2 + 2 * nb], refs[2 + 2 * nb:]
        k = pl.program_id(2)
        for b_ref, acc in zip(b_refs, accs):
            _accumulate(acc, lambda b_ref=b_ref: _dot_tn(a_ref[...].astype(BF16), b_ref[...].astype(BF16)), k, nk)

        @pl.when(k == nk - 1)
        def _():
            for o_ref, acc in zip(o_refs, accs):
                o_ref[...] = acc[...].astype(BF16)

    if a_3d:
        a_spec = pl.BlockSpec((None, tk, tmw), lambda j, r, k: (0, k, r))
    else:
        a_spec = pl.BlockSpec((tk, tmw), lambda j, r, k: (k, r))
    return pl.pallas_call(
        body, name=name, grid=(N_CHIPS, row_blocks, nk),
        in_specs=[a_spec] + [pl.BlockSpec((tk, ns), lambda j, r, k: (k, j))] * nb + [_ANY],
        out_specs=[pl.BlockSpec((None, tmw, ns), lambda j, r, k: (j, r, 0))] * nb,
        out_shape=[jax.ShapeDtypeStruct((N_CHIPS, m, ns), BF16)] * nb,
        scratch_shapes=[pltpu.VMEM((tmw, ns), F32)] * nb,
        compiler_params=_params(("parallel", "parallel", "arbitrary")),
    )(a, *bs, after)


def _dh1_ln_bwd(d_gu, w_gu_g, dpre2, xhat1, rstd1, ln1_g, after):
    s = d_gu.shape[0]
    d = dpre2.shape[1]
    hd = d // 2
    fs = w_gu_g.shape[2]
    tm = min(TM, s)

    def body(dgu_ref, w_ref, dp2_ref, xh_ref, rs_ref, g_ref, after_ref, dpre_ref, gg_ref, gb_ref, acc_lo, acc_hi):
        i, j, half = pl.program_id(0), pl.program_id(1), pl.program_id(2)

        def product():
            return _dot_nt(dgu_ref[...], w_ref[...])

        @pl.when(half == 0)
        def _():
            _accumulate(acc_lo, product, j, N_CHIPS)

        @pl.when(half == 1)
        def _():
            _accumulate(acc_hi, product, j, N_CHIPS)

        @pl.when((j == N_CHIPS - 1) & (half == 1))
        def _():
            @pl.when(i == 0)
            def _():
                gg_ref[...] = jnp.zeros_like(gg_ref)
                gb_ref[...] = jnp.zeros_like(gb_ref)

            def rows_fn(rows):
                dh = jnp.concatenate([acc_lo[rows, :], acc_hi[rows, :]], axis=1) + ALPHA * dp2_ref[rows, :]
                xhat = xh_ref[rows, :]
                dpre_ref[rows, :] = _ln_bwd(dh, xhat, rs_ref[rows, :], g_ref[...])
                gg_ref[...] += jnp.sum(dh * xhat, axis=0, keepdims=True)
                gb_ref[...] += jnp.sum(dh, axis=0, keepdims=True)

            _for_row_chunks(tm, rows_fn)

    row = pl.BlockSpec((tm, d), lambda i, j, h: (i, 0))
    vec = pl.BlockSpec((1, d), lambda i, j, h: (0, 0))
    act_blk = pl.BlockSpec((tm, fs), lambda i, j, h: (i, j))
    w_blk = pl.BlockSpec((None, hd, fs), lambda i, j, h: (j, h, 0))
    return pl.pallas_call(
        body, name="dh1_ln_bwd", grid=(s // tm, N_CHIPS, 2),
        in_specs=[act_blk, w_blk, row, row, pl.BlockSpec((tm, 1), lambda i, j, h: (i, 0)), _VMEM, _ANY],
        out_specs=[row, vec, vec],
        out_shape=[jax.ShapeDtypeStruct((s, d), F32), jax.ShapeDtypeStruct((1, d), F32),
                   jax.ShapeDtypeStruct((1, d), F32)],
        scratch_shapes=[pltpu.VMEM((tm, hd), F32)] * 2,
        compiler_params=_params(("arbitrary", "arbitrary", "arbitrary")),
    )(d_gu, w_gu_g, dpre2, xhat1, rstd1, ln1_g, after)


def _dmixed_rms_bwd(dpre1, w_out_g, ac, rstd, g_ac):
    s, d = dpre1.shape
    hd = d // 2
    tm = min(TM, s)

    def body(dp_ref, w_ref, ac_ref, rs_ref, g_ref, dac_ref, gg_ref):
        i = pl.program_id(1)
        dm = _dot_nt(dp_ref[...].astype(BF16), w_ref[...])
        pre = ac_ref[...]
        r = rs_ref[...]
        gdm = dm * g_ref[...]
        dac_ref[...] = r * gdm - pre * (r * r * r) * jnp.mean(gdm * pre, axis=-1, keepdims=True)
        gg = jnp.sum(dm * pre * r, axis=0, keepdims=True)

        @pl.when(i == 0)
        def _():
            gg_ref[...] = gg

        @pl.when(i > 0)
        def _():
            gg_ref[...] += gg

    return pl.pallas_call(
        body, name="dmixed_rms_bwd", grid=(2, s // tm),
        in_specs=[pl.BlockSpec((tm, d), lambda h, i: (i, 0)),
                  pl.BlockSpec((hd, d), lambda h, i: (h, 0)),
                  pl.BlockSpec((tm, hd), lambda h, i: (i, h)),
                  pl.BlockSpec((None, tm, 1), lambda h, i: (h, i, 0)),
                  pl.BlockSpec((1, hd), lambda h, i: (0, h))],
        out_specs=[pl.BlockSpec((tm, hd), lambda h, i: (i, h)),
                   pl.BlockSpec((1, hd), lambda h, i: (0, h))],
        out_shape=[jax.ShapeDtypeStruct((s, d), F32), jax.ShapeDtypeStruct((1, d), F32)],
        compiler_params=_params(("arbitrary", "arbitrary")),
    )(dpre1, w_out_g, ac, rstd, g_ac)


def _attention_bwd(proj, d_ac, cos_t, sin_t, sinks, after, shard):
    s = proj.shape[0]
    qw = GROUP * N_KV_HEADS * HEAD_DIM
    kvw = N_KV_HEADS * HEAD_DIM
    nb = s // WINDOW
    nq = GROUP * N_KV_HEADS

    def body(cur_ref, prev_ref, do_ref, cos_ref, sin_ref, cosp_ref, sinp_ref, sinks_ref, after_ref,
             dq_ref, dcur_ref, dprev_ref, dsink_ref):
        n = pl.program_id(0)
        first = n == 0
        q, k_all, v_all, cos_q, sin_q = _roped_qkv(cur_ref, prev_ref, cos_ref, sin_ref, cosp_ref, sinp_ref, qw, kvw)
        kk2s = [_pair_operand(k_all, h) for h in range(N_KV_HEADS)]
        vv2s = [_pair_operand(v_all, h) for h in range(N_KV_HEADS)]
        qps, probs, p_sinks = _all_probs(q, kk2s, first, sinks_ref)
        dops = [do_ref[:, pair * PAIR:(pair + 1) * PAIR].astype(BF16) for pair in range(N_PAIRS)]
        d_probs = jnp.concatenate([_dot_nt(dops[pair], vv2s[pair // (GROUP // 2)]) for pair in range(N_PAIRS)], axis=0)
        d_s, ds_sinks = [], []
        for t in range(2):
            cols = slice(t * KEYS, (t + 1) * KEYS)
            delta = jnp.sum(probs[:, cols] * d_probs[:, cols], axis=1, keepdims=True)
            d_s.append(probs[:, cols] * (d_probs[:, cols] - delta))
            ds_sinks.append(-p_sinks[t] * delta)
        d_s = jnp.concatenate(d_s, axis=1).astype(BF16)
        probs = probs.astype(BF16)
        dq_parts, dk_tiles, dv_tiles, dsink_parts = [], [], [], []
        for h in range(N_KV_HEADS):
            dkk2, dvv2 = None, None
            for p in range(GROUP // 2):
                pair = (GROUP // 2) * h + p
                rows = slice(pair * WINDOW, (pair + 1) * WINDOW)
                dq_parts.append(_dot(d_s[rows], kk2s[h]) * ATTN_SCALE)
                dk_term = _dot_tn(d_s[rows], qps[pair])
                dv_term = _dot_tn(probs[rows], dops[pair])
                dkk2 = dk_term if dkk2 is None else dkk2 + dk_term
                dvv2 = dv_term if dvv2 is None else dvv2 + dv_term
                dsink_parts.extend([jnp.sum(ds_sinks[t][rows], axis=0, keepdims=True) for t in range(2)])
            dk_tiles.append(_pair_grad(dkk2, h))
            dv_tiles.append(_pair_grad(dvv2, h))
        dq_ref[...] = _rope(jnp.concatenate(dq_parts, axis=1), cos_q, sin_q, -1.0)
        dk = jnp.concatenate([dk_tiles[0] + dk_tiles[1], dk_tiles[2] + dk_tiles[3]], axis=1)
        dv = jnp.concatenate([dv_tiles[0] + dv_tiles[1], dv_tiles[2] + dv_tiles[3]], axis=1)
        dprev_ref[...] = jnp.concatenate([dk[:WINDOW], dv[:WINDOW]], axis=1)
        dcur_ref[...] = jnp.concatenate([dk[WINDOW:], dv[WINDOW:]], axis=1)
        dsink = jnp.concatenate(dsink_parts, axis=1)

        @pl.when(first)
        def _():
            dsink_ref[...] = dsink

        @pl.when(n > 0)
        def _():
            dsink_ref[...] += dsink

    tbl = pl.BlockSpec((WINDOW, kvw), lambda n: (n, 0))
    tbl_prev = pl.BlockSpec((WINDOW, kvw), lambda n: (jnp.maximum(n - 1, 0), 0))
    kv_blk = pl.BlockSpec((WINDOW, 2 * kvw), lambda n: (n, 0))
    return _call_with_adamw(
        body, "attention_bwd", (nb,),
        [pl.BlockSpec((WINDOW, qw + 2 * kvw), lambda n: (n, 0)),
         pl.BlockSpec((WINDOW, 2 * kvw), lambda n: (jnp.maximum(n - 1, 0), (qw // (2 * kvw)))),
         pl.BlockSpec((WINDOW, qw), lambda n: (n, 0)),
         tbl, tbl, tbl_prev, tbl_prev, _VMEM, _ANY],
        [pl.BlockSpec((WINDOW, qw), lambda n: (n, 0)), kv_blk, kv_blk, pl.BlockSpec((1, nq), lambda n: (0, 0))],
        [jax.ShapeDtypeStruct((s, qw), F32), jax.ShapeDtypeStruct((s, 2 * kvw), F32),
         jax.ShapeDtypeStruct((s, 2 * kvw), F32), jax.ShapeDtypeStruct((1, nq), F32)],
        [], ("arbitrary",), (proj, proj, d_ac, cos_t, sin_t, cos_t, sin_t, sinks, after), shard)


def _dproj_assemble(proj, d_ac, dq, dkv_cur, dkv_prev, cos_t, sin_t, cw_full):
    s, in_w = proj.shape
    cw = dq.shape[1]
    kvw = N_KV_HEADS * HEAD_DIM
    blk_w = in_w // 3
    tb = WINDOW
    nb = s // tb

    def body(lo_ref, hi_ref, lo_p_ref, hi_p_ref, lo_n_ref, hi_n_ref, dconv_ref, dconv_n_ref,
             dq_ref, dcur_ref, dprev_n_ref, cos_ref, sin_ref, cw_ref, dproj_ref, gcw_ref):
        i = pl.program_id(0)
        last = i == nb - 1
        c_gate, b_gate, u = _split_cbu(lo_ref[...], hi_ref[...], cw)
        c_p, _, u_p = _split_cbu(lo_p_ref[...], hi_p_ref[...], cw)
        _, b_n, _ = _split_cbu(lo_n_ref[...], hi_n_ref[...], cw)
        z = c_gate * u
        z_p = jnp.where(i == 0, 0.0, c_p * u_p)
        z1 = _shift_down(z, z_p, 1)
        z2 = _shift_down(z, z_p, 2)
        w0, w1, w2 = _conv_taps(cw_ref)
        y = w0 * z2 + w1 * z1 + w2 * z
        d_conv = dconv_ref[...]
        d_b = d_conv * y
        d_y = d_conv * b_gate
        d_y_n = jnp.where(last, 0.0, dconv_n_ref[...] * b_n[:dconv_n_ref.shape[0]])
        d_z = w2 * d_y + w1 * _shift_up(d_y, d_y_n, 1) + w0 * _shift_up(d_y, d_y_n, 2)
        d_c = d_z * u
        d_u = d_z * c_gate
        gcw = jnp.concatenate([jnp.sum(d_y * z2, axis=0, keepdims=True), jnp.sum(d_y * z1, axis=0, keepdims=True),
                               jnp.sum(d_y * z, axis=0, keepdims=True)], axis=0)

        @pl.when(i == 0)
        def _():
            gcw_ref[...] = gcw

        @pl.when(i > 0)
        def _():
            gcw_ref[...] += gcw

        dkv = dcur_ref[...] + jnp.where(last, 0.0, dprev_n_ref[...])
        dk = _rope(dkv[:, :kvw], cos_ref[...], sin_ref[...], -1.0)
        dproj_ref[...] = jnp.concatenate([dq_ref[...], dk, dkv[:, kvw:], d_c, d_b, d_u], axis=1).astype(BF16)

    prev_halo = lambda i: jnp.maximum(i * (tb // HALO_ROWS) - 1, 0)
    next_halo = lambda i: jnp.minimum((i + 1) * (tb // HALO_ROWS), s // HALO_ROWS - 1)
    next8 = lambda i: jnp.minimum((i + 1) * (tb // 8), s // 8 - 1)
    nxt = lambda i: jnp.minimum(i + 1, nb - 1)
    return pl.pallas_call(
        body, name="dproj_assemble", grid=(nb,),
        in_specs=[pl.BlockSpec((tb, blk_w), lambda i: (i, 1)),
                  pl.BlockSpec((tb, blk_w), lambda i: (i, 2)),
                  pl.BlockSpec((HALO_ROWS, blk_w), lambda i: (prev_halo(i), 1)),
                  pl.BlockSpec((HALO_ROWS, blk_w), lambda i: (prev_halo(i), 2)),
                  pl.BlockSpec((HALO_ROWS, blk_w), lambda i: (next_halo(i), 1)),
                  pl.BlockSpec((HALO_ROWS, blk_w), lambda i: (next_halo(i), 2)),
                  pl.BlockSpec((tb, cw), lambda i: (i, 1)),
                  pl.BlockSpec((8, cw), lambda i: (next8(i), 1)),
                  pl.BlockSpec((tb, cw), lambda i: (i, 0)),
                  pl.BlockSpec((tb, 2 * kvw), lambda i: (i, 0)),
                  pl.BlockSpec((tb, 2 * kvw), lambda i: (nxt(i), 0)),
                  pl.BlockSpec((tb, kvw), lambda i: (i, 0)),
                  pl.BlockSpec((tb, kvw), lambda i: (i, 0)),
                  _VMEM],
        out_specs=[pl.BlockSpec((tb, in_w), lambda i: (i, 0)),
                   pl.BlockSpec((3, cw), lambda i: (0, 0))],
        out_shape=[jax.ShapeDtypeStruct((s, in_w), BF16), jax.ShapeDtypeStruct((3, cw), F32)],
        compiler_params=_params(("arbitrary",)),
    )(proj, proj, proj, proj, proj, proj, d_ac, d_ac, dq, dkv_cur, dkv_prev, cos_t, sin_t, cw_full)


def _dx(d_proj, w_in_g, dpre1, after):
    s, in_w = d_proj.shape
    ns, d, ncol = w_in_g.shape
    tm = min(TM, s)

    def body(dp_ref, w_ref, r_ref, after_ref, o_ref, acc):
        j = pl.program_id(1)
        _accumulate(acc, lambda: _dot_nt(dp_ref[...], w_ref[...]), j, ns)

        @pl.when(j == ns - 1)
        def _():
            o_ref[...] = acc[...] + ALPHA * r_ref[...]

    return pl.pallas_call(
        body, name="dx", grid=(s // tm, ns),
        in_specs=[pl.BlockSpec((tm, ncol), lambda i, j: (i, j)),
                  pl.BlockSpec((None, d, ncol), lambda i, j: (j, 0, 0)),
                  pl.BlockSpec((tm, d), lambda i, j: (i, 0)), _ANY],
        out_specs=pl.BlockSpec((None, tm, d), lambda i, j: (0, i, 0)),
        out_shape=jax.ShapeDtypeStruct((1, s, d), F32),
        scratch_shapes=[pltpu.VMEM((tm, d), F32)],
        compiler_params=_params(("parallel", "arbitrary")),
    )(d_proj, w_in_g, dpre1, after)


def kernel(x, positions, w_in, conv_w, sinks, g_attn, g_conv, w_out, ln1_g, ln1_b, w_gate, w_up, w_down, ln2_g, ln2_b, loss_target, m_w_in, m_conv_w, m_sinks, m_g_attn, m_g_conv, m_w_out, m_ln1_g, m_ln1_b, m_w_gate, m_w_up, m_w_down, m_ln2_g, m_ln2_b, v_w_in, v_conv_w, v_sinks, v_g_attn, v_g_conv, v_w_out, v_ln1_g, v_ln1_b, v_w_gate, v_w_up, v_w_down, v_ln2_g, v_ln2_b):
    s = x.shape[1]
    d = x.shape[2]

    chip_vec = _chip_id(lax.axis_index("x"), lax.axis_index("y")).astype(jnp.int32).reshape(1)
    wnames = ["w_in", "w_out", "w_gu", "w_down"]
    buf_in = _cast_weight(w_in, chip_vec, chip_vec, "cast_w_in")
    flight_in, token_in = _gather_start([buf_in], chip_vec, "gather_start_w_in")
    cw_buf = lax.dynamic_update_slice(jnp.zeros((N_CHIPS,) + conv_w.shape[1:], F32), conv_w, (chip_vec[0], 0, 0))
    cw_flight = _flight_start("conv_w_start", [cw_buf], _conv_w_plan(), 3, token_in)
    started = cw_flight[2][0]
    buf_gu = _cast_weight(w_gate, chip_vec, started, "cast_w_gate", 0, 2)
    buf_gu = _cast_weight(w_up, chip_vec, buf_gu, "cast_w_up", 1, 2)
    bufs = [_cast_weight(w_out, chip_vec, started, "cast_w_out"), buf_gu,
            _cast_weight(w_down, chip_vec, started, "cast_w_down")]
    flights_rest, token = _gather_start(bufs, token_in, "gather_start_rest")
    flights = flight_in + flights_rest

    def gathered(i, after):
        send_sems, recv_sems, buf = flights[i]
        buf = _gather_wait(send_sems, recv_sems, buf, after, "gather_wait_" + wnames[i])
        return _sibling_fill(buf, "sibling_fill_" + wnames[i])

    g_ac = jnp.concatenate([g_attn, g_conv], axis=1)

    proj_own = _in_proj(x, _after(flights[0][2], token), chip_vec, 1, None, "in_proj_own")
    cos_t, sin_t = _rope_tables(positions.reshape(s, 1) + token[0:1, 0:1].astype(jnp.int32))
    w_in_g = gathered(0, _after(cos_t, proj_own))
    proj = _in_proj(x, w_in_g, chip_vec + 1, N_CHIPS - 1, proj_own, "in_proj_rest")
    send_sems, recv_sems, buf_out = flights[1]
    buf_out = _gather_wait(send_sems, recv_sems, buf_out, proj, "gather_wait_w_out")
    fill_out = _flight_start("fill_start_w_out", [buf_out], _fill_plan(1), 3, chip_vec)
    attn = _attention_fwd(_after(proj, fill_out[2][0]), cos_t, sin_t, sinks)
    (cw_full,) = _flight_wait("conv_w_wait", cw_flight, _conv_w_plan(), attn)
    mixed, ac, rstd_ac = _conv_norm(proj, attn, cw_full, g_ac)
    (w_out_g,) = _flight_wait("fill_wait_w_out", fill_out, _fill_plan(1), mixed)
    w_out_full = w_out_g.reshape(d, d)
    xhat1, h1, rstd1 = _out_proj_ln(mixed, w_out_full, x, ln1_g, ln1_b)
    send_sems, recv_sems, buf_gu = flights[2]
    buf_gu = _gather_wait(send_sems, recv_sems, buf_gu, h1, "gather_wait_w_gu")
    fill_gu = _flight_start("fill_start_w_gu", [buf_gu], _fill_plan(1), 3, chip_vec)
    own = _gate_up(h1, fill_gu[2][0], chip_vec, 1, None, "gate_up_own")
    (w_gu_g,) = _flight_wait("fill_wait_w_gu", fill_gu, _fill_plan(1), own[0])
    act, ab = _gate_up(h1, w_gu_g, chip_vec + 1, N_CHIPS - 1, own, "gate_up_rest")
    w_down_full = gathered(3, act).reshape(-1, d)
    dpre2, dpre2_16, loss_part, g_ln2_g, g_ln2_b = _down_ln_loss(act, w_down_full, xhat1, ln1_g, ln1_b, ln2_g, ln2_b,
                                                                 loss_target)

    cvec = lax.axis_index("c").astype(jnp.int32).reshape(1)

    def exchange_begin(parts, nme):
        bufs = []
        for part in parts:
            ns, r, cdim = part.shape
            bufs.extend([part, lax.empty((ns, r // 2, cdim), part.dtype)])
        return _flight_start("exchange_start_" + nme, bufs, _exchange_plan(len(parts)), len(parts), cvec)

    def exchange_end(flight, n_parts, after, nme):
        bufs = _flight_wait("exchange_wait_" + nme, flight, _exchange_plan(n_parts), after)
        return [(bufs[2 * w], bufs[2 * w + 1]) for w in range(n_parts)]

    def scatter_begin(part, got, nme):
        return _scatter_start(_add_halves(part, got, cvec, "add_halves_" + nme), "scatter_start_" + nme)

    d_gu = _dact_silu_bwd(dpre2_16, w_down_full, ab)
    p_down = _grad_rows(act, dpre2_16, d_gu, "grad_w_down")
    x_down = exchange_begin([p_down], "w_down")
    (p_gu,) = _grad_cols(h1, [d_gu], x_down[2][0], "grad_w_gate_up")
    ((p_down, got),) = exchange_end(x_down, 1, p_gu, "w_down")
    f_down = scatter_begin(p_down, got, "w_down")
    x_gu = exchange_begin([_after(p_gu, f_down[2])], "w_gu")
    dpre1, g_ln1_g, g_ln1_b = _dh1_ln_bwd(d_gu, w_gu_g, dpre2, xhat1, rstd1, ln1_g, x_gu[2][0])
    ((p_gu, got),) = exchange_end(x_gu, 1, dpre1, "w_gu")
    f_gu = scatter_begin(p_gu, got, "w_gu")
    d_ac, g_g_ac = _dmixed_rms_bwd(_after(dpre1, f_gu[2]), w_out_full, ac, rstd_ac, g_ac)
    pos_vec = jnp.concatenate([chip_vec, cvec])
    sums, land = _scatter_wait(*f_down, d_ac, "scatter_wait_w_down")
    c_down = _flight_start("complete_start_w_down", [sums, land], _complete_plan(1), 4, cvec)
    p_out = _grad_rows(mixed, dpre1, c_down[2][1], "grad_w_out")
    x_out = exchange_begin([p_out], "w_out")
    sums, land = _flight_wait("complete_wait_w_down", c_down, _complete_plan(1), x_out[2][0])
    dq, dkv_cur, dkv_prev, g_sinks, *new_w_down = _attention_bwd(
        proj, d_ac, cos_t, sin_t, sinks, x_out[2][0], (w_down, m_w_down, v_w_down, land, sums, pos_vec, 0))
    ((p_out, got),) = exchange_end(x_out, 1, dq, "w_out")
    f_out = scatter_begin(p_out, got, "w_out")
    d_proj, g_conv_w = _dproj_assemble(proj, _after(d_ac, f_out[2]), dq, dkv_cur, dkv_prev, cos_t, sin_t, cw_full)
    (p_in,) = _grad_cols(x, [d_proj], d_proj, "grad_w_in", a_3d=True)
    x_in = exchange_begin([p_in], "w_in")
    grad_x = _dx(d_proj, w_in_g, dpre1, x_in[2][0])
    red = _allreduce_small(g_ln2_g, g_ln2_b, g_ln1_g, g_ln1_b, g_g_ac, g_conv_w, g_sinks, loss_part, grad_x)
    ((p_in, got),) = exchange_end(x_in, 1, red, "w_in")
    f_in = scatter_begin(p_in, got, "w_in")

    shards = {"w_in": (w_in, m_w_in, v_w_in), "w_out": (w_out, m_w_out, v_w_out), "w_gate": (w_gate, m_w_gate, v_w_gate),
              "w_up": (w_up, m_w_up, v_w_up)}
    early = [("w_gu", ["w_gate", "w_up"]), ("w_out", ["w_out"])]
    after = f_in[2]
    completing = {}
    for (nme, _), f in zip(early, [f_gu, f_out]):
        sums, land = _scatter_wait(*f, after, "scatter_wait_" + nme)
        completing[nme] = _flight_start("complete_start_" + nme, [sums, land], _complete_plan(1), 4, cvec)
        after = completing[nme][2][1]
    big = {"w_down": new_w_down}
    for nme, members in early:
        sums, land = _flight_wait("complete_wait_" + nme, completing[nme], _complete_plan(1), after)
        for col_block, member in enumerate(members):
            big[member] = _adamw_shard(*shards[member], land, sums, pos_vec, "adamw_" + member, col_block)
            after = big[member][0]
    sums, land = _scatter_wait(*f_in, after, "scatter_wait_w_in")
    (land,) = _complete_chip_sums([sums], [land])
    big["w_in"] = _adamw_shard(*shards["w_in"], land, sums, pos_vec, "adamw_w_in")
    small = _adamw_small(red, {
        "sinks": (sinks, m_sinks, v_sinks), "g_attn": (g_attn, m_g_attn, v_g_attn),
        "g_conv": (g_conv, m_g_conv, v_g_conv), "ln1_g": (ln1_g, m_ln1_g, v_ln1_g),
        "ln1_b": (ln1_b, m_ln1_b, v_ln1_b), "ln2_g": (ln2_g, m_ln2_g, v_ln2_g),
        "ln2_b": (ln2_b, m_ln2_b, v_ln2_b), "conv_w": (conv_w, m_conv_w, v_conv_w)})
    res = {**big, **small}
    order = ["w_in", "conv_w", "sinks", "g_attn", "g_conv", "w_out", "ln1_g", "ln1_b", "w_gate", "w_up", "w_down",
             "ln2_g", "ln2_b"]
    loss = red[6, d // 2 + 128]
    return (loss, grad_x, *[res[n][0] for n in order], *[res[n][1] for n in order],
            *[res[n][2] for n in order], *[res[n][3] for n in order])
```

```python
import functools

import numpy as np
import jax
import jax.numpy as jnp
from jax import lax
from jax.experimental import pallas as pl
from jax.experimental.pallas import tpu as pltpu

F32 = jnp.float32
BF16 = jnp.bfloat16
MESH = pl.DeviceIdType.MESH

HEAD_DIM = 64
N_KV_HEADS = 4
GROUP = 4
WINDOW = 128
ROT_DIM = 16
ROPE_THETA = 500000.0
ATTN_SCALE = HEAD_DIM ** -0.5
ALPHA = 2.0 ** 0.25
LN_EPS = 1e-5
RMS_EPS = 1e-6
ADAM_LR = 0.001
ADAM_B1 = 0.9
ADAM_B2 = 0.999
ADAM_EPS = 1e-08
ADAM_WD = 0.01
ADAM_STEP = 10
N_CHIPS = 4
NEG_BIG = -1e30

V7X_VMEM_BYTES = 64 * 1024 * 1024
VMEM_LIMIT = V7X_VMEM_BYTES - 6 * 1024 * 1024

TM = 512
TK_TOK = 1024
TB_CONV = 256
TR_ELT = 256
ROW_CHUNK = 128
HALO_ROWS = 16


def _params(sem):
    return pltpu.CompilerParams(dimension_semantics=sem, vmem_limit_bytes=VMEM_LIMIT)


def _row_tile(rows, target):
    best = None
    for t in range(16, min(rows, target) + 1, 16):
        if rows % t == 0:
            best = t
    assert best is not None, (rows, target)
    return best


def _dot(a, b):
    return jnp.dot(a, b, preferred_element_type=F32)


def _dot_nt(a, b):
    return lax.dot_general(a, b, (((1,), (1,)), ((), ())), preferred_element_type=F32)


def _dot_tn(a, b):
    return lax.dot_general(a, b, (((0,), (0,)), ((), ())), preferred_element_type=F32)


def _mesh_pos():
    x, y, c = lax.axis_index("x"), lax.axis_index("y"), lax.axis_index("c")
    chips = [(1 - x, y), (x, 1 - y), (1 - x, 1 - y)]
    return x, y, c, chips


def _chip_id(px, py):
    return 2 * px + py


def _rope(t, cos, sgn_sin, sign):
    w = t.shape[1]
    lane = lax.broadcasted_iota(jnp.int32, t.shape, 1) & (HEAD_DIM - 1)
    partner = jnp.where(lane < ROT_DIM // 2, pltpu.roll(t, w - ROT_DIM // 2, 1), pltpu.roll(t, ROT_DIM // 2, 1))
    return t * cos + sign * (partner * sgn_sin)


def _tile_lanes(t, n):
    return jnp.concatenate([t] * n, axis=1)


def _sigmoid(g):
    return 1.0 / (1.0 + jnp.exp(-g))


def _for_row_chunks(n_rows, fn):
    def step(r, carry):
        fn(pl.ds(pl.multiple_of(r * ROW_CHUNK, ROW_CHUNK), ROW_CHUNK))
        return carry

    lax.fori_loop(0, n_rows // ROW_CHUNK, step, 0)


def _accumulate(acc, make_val, k, nk):
    if nk == 1:
        acc[...] = make_val()
        return

    @pl.when(k == 0)
    def _():
        acc[...] = jnp.zeros_like(acc)

    acc[...] += make_val()


def _ln_fwd(pre):
    mu = jnp.mean(pre, axis=-1, keepdims=True)
    cen = pre - mu
    var = jnp.mean(cen * cen, axis=-1, keepdims=True)
    rstd = lax.rsqrt(var + LN_EPS)
    return cen * rstd, rstd


def _ln_bwd(dy, xhat, rstd, g):
    dxhat = dy * g
    m1 = jnp.mean(dxhat, axis=-1, keepdims=True)
    m2 = jnp.mean(dxhat * xhat, axis=-1, keepdims=True)
    return rstd * (dxhat - m1 - xhat * m2)


def _cast_weight(w, chip_vec, after, name, col_block=0, n_col_blocks=1):
    _, r, c = w.shape
    tr = _row_tile(r, TR_ELT)

    def body(chip_ref, w_ref, after_ref, o_ref):
        o_ref[...] = w_ref[...].astype(BF16)

    grid_spec = pltpu.PrefetchScalarGridSpec(
        num_scalar_prefetch=1, grid=(r // tr,),
        in_specs=[pl.BlockSpec((None, tr, c), lambda i, chip_ref: (0, i, 0)), _ANY],
        out_specs=pl.BlockSpec((None, tr, c), lambda i, chip_ref: (chip_ref[0], i, col_block)))
    return pl.pallas_call(
        body, name=name, grid_spec=grid_spec,
        out_shape=jax.ShapeDtypeStruct((N_CHIPS, r, n_col_blocks * c), BF16),
        input_output_aliases={2: 0} if col_block else {},
        compiler_params=_params(("parallel",)),
    )(chip_vec, w, after)


_HBM = pl.BlockSpec(memory_space=pltpu.HBM)
_VMEM = pl.BlockSpec(memory_space=pltpu.VMEM)


_SEM = pl.BlockSpec(memory_space=pltpu.SEMAPHORE)
_ANY = pl.BlockSpec(memory_space=pl.ANY)
_EFFECT = pltpu.SideEffectType.DATAFLOW_SIDE_EFFECTING


def _chip_copy(buf, k, chip_of_src, half_rows, send_sems, recv_sems, to):
    part = buf.at[chip_of_src, half_rows]
    return pltpu.make_async_remote_copy(
        src_ref=part, dst_ref=part, send_sem=send_sems.at[k], recv_sem=recv_sems.at[k], device_id=to, device_id_type=MESH)


def _half_rows(buf, which):
    hr = buf.shape[1] // 2
    return pl.ds(which * hr, hr)


def _after(value, dep):
    return lax.optimization_barrier((value, dep))[0]


def _flight_start(name, bufs, plan, n_sems, after):
    n = len(bufs)

    def body(*refs):
        sends, _ = plan(refs[:n], refs[n + 1], refs[n + 2])
        for cp in sends:
            cp.start()

    outs = pl.pallas_call(
        body, name=name,
        in_specs=[_HBM] * n + [_ANY], out_specs=[_SEM, _SEM] + [_HBM] * n,
        out_shape=[pltpu.SemaphoreType.DMA((n_sems,))] * 2 + [pltpu.HBM(b.shape, b.dtype) for b in bufs],
        input_output_aliases={i: 2 + i for i in range(n)},
        compiler_params=pltpu.CompilerParams(has_side_effects=_EFFECT),
    )(*[pltpu.with_memory_space_constraint(b, pltpu.HBM) for b in bufs], after)
    return outs[0], outs[1], list(outs[2:])


def _flight_wait(name, flight, plan, after):
    send_sems, recv_sems, bufs = flight
    n = len(bufs)

    def body(*refs):
        sends, recvs = plan(refs[:n], refs[n], refs[n + 1])
        for cp in sends:
            cp.wait_send()
        for cp in recvs:
            cp.wait_recv()

    outs = pl.pallas_call(
        body, name=name,
        in_specs=[_HBM] * n + [_SEM, _SEM, _ANY], out_specs=[_HBM] * n,
        out_shape=[pltpu.HBM(b.shape, b.dtype) for b in bufs],
        input_output_aliases={i: i for i in range(n)},
        compiler_params=pltpu.CompilerParams(has_side_effects=_EFFECT),
    )(*bufs, send_sems, recv_sems, after)
    return list(outs)


def _fill_plan(n_bufs):
    def plan(refs, send_sems, recv_sems):
        x, y, c, chips = _mesh_pos()
        sibling = (x, y, 1 - c)
        sends, recvs = [], []
        for w in range(n_bufs):
            for k, chip in enumerate(chips):
                slot = _chip_id(*chip)
                sends.append(_chip_copy(refs[w], 3 * w + k, slot, _half_rows(refs[w], c), send_sems, recv_sems, sibling))
                recvs.append(_chip_copy(refs[w], 3 * w + k, slot, _half_rows(refs[w], 1 - c), send_sems, recv_sems,
                                        sibling))
        return sends, recvs
    return plan


def _conv_w_plan():
    def plan(refs, send_sems, recv_sems):
        x, y, c, chips = _mesh_pos()
        me = _chip_id(x, y)
        (buf,) = refs
        sends, recvs = [], []
        for k, chip in enumerate(chips):
            for slot, into in ((me, sends), (_chip_id(*chip), recvs)):
                into.append(pltpu.make_async_remote_copy(
                    src_ref=buf.at[slot], dst_ref=buf.at[slot], send_sem=send_sems.at[k], recv_sem=recv_sems.at[k],
                    device_id=(*chip, c), device_id_type=MESH))
        return sends, recvs
    return plan


def _exchange_plan(n_parts):
    def plan(refs, send_sems, recv_sems):
        x, y, c, _ = _mesh_pos()
        copies = []
        for w in range(n_parts):
            part, got = refs[2 * w], refs[2 * w + 1]
            hr = got.shape[1]
            copies.append(pltpu.make_async_remote_copy(
                src_ref=part.at[:, pl.ds((1 - c) * hr, hr)], dst_ref=got, send_sem=send_sems.at[w],
                recv_sem=recv_sems.at[w], device_id=(x, y, 1 - c), device_id_type=MESH))
        return copies, copies
    return plan


def _gather_start(bufs, after, name):
    n = len(bufs)

    def body(*refs):
        ins = refs[:n]
        sends, recvs = refs[n + 1:2 * n + 1], refs[2 * n + 1:3 * n + 1]
        token = refs[4 * n + 1]
        x, y, c, chips = _mesh_pos()
        me = _chip_id(x, y)
        for w in range(n):
            for k, chip in enumerate(chips):
                _chip_copy(ins[w], k, me, _half_rows(ins[w], c), sends[w], recvs[w], (*chip, c)).start()
        token[...] = jnp.zeros_like(token)

    outs = pl.pallas_call(
        body, name=name,
        in_specs=[_HBM] * n + [_ANY],
        out_specs=[_SEM] * (2 * n) + [_HBM] * n + [_VMEM],
        out_shape=[pltpu.SemaphoreType.DMA((3,))] * (2 * n) + [pltpu.HBM(b.shape, b.dtype) for b in bufs]
        + [jax.ShapeDtypeStruct((8, 128), F32)],
        input_output_aliases={w: 2 * n + w for w in range(n)},
        compiler_params=pltpu.CompilerParams(has_side_effects=_EFFECT),
    )(*[pltpu.with_memory_space_constraint(b, pltpu.HBM) for b in bufs], after)
    return [(outs[w], outs[n + w], outs[2 * n + w]) for w in range(n)], outs[3 * n]


def _gather_wait(send_sems, recv_sems, buf, after, name):
    def body(buf_ref, send_ref, recv_ref, after_ref, out_ref):
        x, y, c, chips = _mesh_pos()
        me = _chip_id(x, y)
        for k, chip in enumerate(chips):
            _chip_copy(buf_ref, k, me, _half_rows(buf_ref, c), send_ref, recv_ref, (*chip, c)).wait_send()
        for k, chip in enumerate(chips):
            _chip_copy(buf_ref, k, _chip_id(*chip), _half_rows(buf_ref, c), send_ref, recv_ref, (*chip, c)).wait_recv()

    return pl.pallas_call(
        body, name=name,
        in_specs=[_HBM, _SEM, _SEM, _ANY], out_specs=_HBM,
        out_shape=pltpu.HBM(buf.shape, buf.dtype),
        input_output_aliases={0: 0},
        compiler_params=pltpu.CompilerParams(has_side_effects=_EFFECT),
    )(buf, send_sems, recv_sems, after)


def _sibling_fill(buf, name, own_too=False):
    n_copies = 4 if own_too else 3

    def body(buf_ref, out_ref, send_sems, recv_sems):
        x, y, c, chips = _mesh_pos()
        sibling = (x, y, 1 - c)
        slots = [_chip_id(*chip) for chip in chips] + ([_chip_id(x, y)] if own_too else [])
        copies = []
        for k, slot in enumerate(slots):
            cp = _chip_copy(out_ref, k, slot, _half_rows(out_ref, c), send_sems, recv_sems, sibling)
            cp.start()
            copies.append(cp)
        for k, slot in enumerate(slots):
            _chip_copy(out_ref, k, slot, _half_rows(out_ref, 1 - c), send_sems, recv_sems, sibling).wait_recv()
        for cp in copies:
            cp.wait_send()

    return pl.pallas_call(
        body, name=name,
        in_specs=[_HBM], out_specs=_HBM,
        out_shape=jax.ShapeDtypeStruct(buf.shape, buf.dtype),
        input_output_aliases={0: 0},
        scratch_shapes=[pltpu.SemaphoreType.DMA((n_copies,)), pltpu.SemaphoreType.DMA((n_copies,))],
    )(buf)


def _allgather_conv_w(cw):
    _, kw, cs = cw.shape

    def body(cw_ref, out_ref, send_sems, recv_sems):
        x, y, c, chips = _mesh_pos()
        me = _chip_id(x, y)
        out_ref[pl.ds(me, 1)] = cw_ref[...]
        copies = []
        for k, chip in enumerate(chips):
            cp = pltpu.make_async_remote_copy(
                src_ref=cw_ref.at[0], dst_ref=out_ref.at[me], send_sem=send_sems.at[k], recv_sem=recv_sems.at[k],
                device_id=(*chip, c), device_id_type=MESH)
            cp.start()
            copies.append(cp)
        for k, chip in enumerate(chips):
            pltpu.make_async_remote_copy(
                src_ref=cw_ref.at[0], dst_ref=out_ref.at[_chip_id(*chip)], send_sem=send_sems.at[k],
                recv_sem=recv_sems.at[k], device_id=(*chip, c), device_id_type=MESH).wait_recv()
        for cp in copies:
            cp.wait_send()

    return pl.pallas_call(
        body, name="allgather_conv_w",
        in_specs=[_VMEM], out_specs=_VMEM,
        out_shape=jax.ShapeDtypeStruct((N_CHIPS, kw, cs), F32),
        scratch_shapes=[pltpu.SemaphoreType.DMA((3,)), pltpu.SemaphoreType.DMA((3,))],
    )(cw)


def _exchange_halves(parts, after, name):
    n = len(parts)
    shapes = [p.shape for p in parts]

    def body(*refs):
        ins, outs = refs[:n], refs[n + 1:2 * n + 1]
        send_sems, recv_sems = refs[2 * n + 1:]
        x, y, c, _ = _mesh_pos()
        copies = []
        for w in range(n):
            hr = shapes[w][1] // 2
            cp = pltpu.make_async_remote_copy(
                src_ref=ins[w].at[:, pl.ds((1 - c) * hr, hr)], dst_ref=outs[w],
                send_sem=send_sems.at[w], recv_sem=recv_sems.at[w],
                device_id=(x, y, 1 - c), device_id_type=MESH)
            cp.start()
            copies.append(cp)
        for cp in copies:
            cp.wait()

    return pl.pallas_call(
        body, name=name,
        in_specs=[_HBM] * n + [_ANY], out_specs=[_HBM] * n,
        out_shape=[jax.ShapeDtypeStruct((s[0], s[1] // 2, s[2]), BF16) for s in shapes],
        scratch_shapes=[pltpu.SemaphoreType.DMA((n,)), pltpu.SemaphoreType.DMA((n,))],
    )(*parts, after)


def _add_halves(part, got, cvec, name):
    ns, r, cdim = part.shape
    hr = r // 2
    tr = _row_tile(hr, TR_ELT)
    nblk = hr // tr

    def body(c_ref, a_ref, b_ref, o_ref):
        o_ref[...] = (a_ref[...].astype(F32) + b_ref[...].astype(F32)).astype(BF16)

    grid_spec = pltpu.PrefetchScalarGridSpec(
        num_scalar_prefetch=1, grid=(ns, nblk),
        in_specs=[pl.BlockSpec((None, tr, cdim), lambda s, i, c_ref: (s, c_ref[0] * nblk + i, 0)),
                  pl.BlockSpec((None, tr, cdim), lambda s, i, c_ref: (s, i, 0))],
        out_specs=pl.BlockSpec((None, tr, cdim), lambda s, i, c_ref: (s, i, 0)))
    return pl.pallas_call(
        body, name=name, grid_spec=grid_spec,
        out_shape=jax.ShapeDtypeStruct((ns, hr, cdim), BF16),
        compiler_params=_params(("parallel", "parallel")),
    )(cvec, part, got)


def _scatter_copy(sums_ref, land_ref, k, src_slot, dst_slot, c, send_sems, recv_sems, to):
    return pltpu.make_async_remote_copy(
        src_ref=sums_ref.at[src_slot], dst_ref=land_ref.at[dst_slot, _half_rows(land_ref, c)],
        send_sem=send_sems.at[k], recv_sem=recv_sems.at[k], device_id=to, device_id_type=MESH)


def _scatter_start(sums, name):
    ns, hr, cdim = sums.shape
    land = lax.empty((ns, 2 * hr, cdim), sums.dtype)

    def body(sums_ref, land_ref, send_sems, recv_sems, sums_thru, land_thru):
        x, y, c, chips = _mesh_pos()
        me = _chip_id(x, y)
        for k, chip in enumerate(chips):
            _scatter_copy(sums_ref, land_ref, k, _chip_id(*chip), me, c, send_sems, recv_sems, (*chip, c)).start()

    return pl.pallas_call(
        body, name=name,
        in_specs=[_HBM, _HBM], out_specs=[_SEM, _SEM, _HBM, _HBM],
        out_shape=[pltpu.SemaphoreType.DMA((3,)), pltpu.SemaphoreType.DMA((3,)),
                   pltpu.HBM(sums.shape, sums.dtype), pltpu.HBM(land.shape, land.dtype)],
        input_output_aliases={0: 2, 1: 3},
        compiler_params=pltpu.CompilerParams(has_side_effects=_EFFECT),
    )(pltpu.with_memory_space_constraint(sums, pltpu.HBM), pltpu.with_memory_space_constraint(land, pltpu.HBM))


def _scatter_wait(send_sems, recv_sems, sums, land, after, name):
    def body(sums_ref, land_ref, send_ref, recv_ref, after_ref, sums_out, land_out):
        x, y, c, chips = _mesh_pos()
        me = _chip_id(x, y)
        for k, chip in enumerate(chips):
            _scatter_copy(sums_ref, land_ref, k, _chip_id(*chip), me, c, send_ref, recv_ref, (*chip, c)).wait_send()
        for k, chip in enumerate(chips):
            _scatter_copy(sums_ref, land_ref, k, me, _chip_id(*chip), c, send_ref, recv_ref, (*chip, c)).wait_recv()

    return pl.pallas_call(
        body, name=name,
        in_specs=[_HBM, _HBM, _SEM, _SEM, _ANY], out_specs=[_HBM, _HBM],
        out_shape=[pltpu.HBM(sums.shape, sums.dtype), pltpu.HBM(land.shape, land.dtype)],
        input_output_aliases={0: 0, 1: 1},
        compiler_params=pltpu.CompilerParams(has_side_effects=_EFFECT),
    )(sums, land, send_sems, recv_sems, after)


def _complete_plan(n_weights):
    def plan(refs, send_sems, recv_sems):
        x, y, c, chips = _mesh_pos()
        me = _chip_id(x, y)
        sibling = (x, y, 1 - c)
        sends, recvs = [], []
        for w in range(n_weights):
            sums, land = refs[2 * w], refs[2 * w + 1]
            sends.append(_scatter_copy(sums, land, 4 * w + 3, me, me, c, send_sems, recv_sems, sibling))
            recvs.append(_scatter_copy(sums, land, 4 * w + 3, me, me, 1 - c, send_sems, recv_sems, sibling))
            for k, chip in enumerate(chips):
                slot = _chip_id(*chip)
                sends.append(_chip_copy(land, 4 * w + k, slot, _half_rows(land, c), send_sems, recv_sems, sibling))
                recvs.append(_chip_copy(land, 4 * w + k, slot, _half_rows(land, 1 - c), send_sems, recv_sems, sibling))
        return sends, recvs
    return plan


def _complete_chip_sums(sums, lands):
    n = len(sums)

    def body(*refs):
        sums_refs, outs = refs[:n], refs[2 * n:3 * n]
        send_sems, recv_sems = refs[3 * n:]
        x, y, c, chips = _mesh_pos()
        me = _chip_id(x, y)
        sibling = (x, y, 1 - c)
        slots = [_chip_id(*chip) for chip in chips]
        sent = []
        for w in range(n):
            out = outs[w]
            cp = _scatter_copy(sums_refs[w], out, 3, me, me, c, send_sems.at[w], recv_sems.at[w], sibling)
            cp.start()
            sent.append(cp)
            for k, slot in enumerate(slots):
                cp = _chip_copy(out, k, slot, _half_rows(out, c), send_sems.at[w], recv_sems.at[w], sibling)
                cp.start()
                sent.append(cp)
        for w in range(n):
            out = outs[w]
            _scatter_copy(sums_refs[w], out, 3, me, me, 1 - c, send_sems.at[w], recv_sems.at[w], sibling).wait_recv()
            for k, slot in enumerate(slots):
                _chip_copy(out, k, slot, _half_rows(out, 1 - c), send_sems.at[w], recv_sems.at[w], sibling).wait_recv()
        for cp in sent:
            cp.wait_send()

    return pl.pallas_call(
        body, name="complete_chip_sums",
        in_specs=[_HBM] * (2 * n), out_specs=[_HBM] * n,
        out_shape=[jax.ShapeDtypeStruct(b.shape, b.dtype) for b in lands],
        input_output_aliases={n + w: w for w in range(n)},
        scratch_shapes=[pltpu.SemaphoreType.DMA((n, 4)), pltpu.SemaphoreType.DMA((n, 4))],
    )(*sums, *lands)


SMALL_ROWS = 8


def _allreduce_small(gl2g, gl2b, gl1g, gl1b, g_ac, gcw, gsink, loss, after):
    d = gl2g.shape[1]
    hd = d // 2
    nq = gsink.shape[1]

    def body(a_ref, b_ref, c_ref, d_ref, e_ref, cw_ref, sk_ref, ls_ref, after_ref, out_ref, mine, gath, send_sems,
             recv_sems):
        x, y, c, _ = _mesh_pos()
        me = 4 * x + 2 * y + c
        mine[...] = jnp.zeros_like(mine)
        mine[0:1, :] = a_ref[...]
        mine[1:2, :] = b_ref[...]
        mine[2:3, :] = c_ref[...]
        mine[3:4, :] = d_ref[...]
        mine[4:5, :] = e_ref[...]
        mine[5:6, 0:hd] = cw_ref[0:1, :]
        mine[5:6, hd:d] = cw_ref[1:2, :]
        mine[6:7, 0:hd] = cw_ref[2:3, :]
        mine[6:7, hd:hd + nq] = sk_ref[...]
        mine[6:7, hd + 128:hd + 256] = ls_ref[...]
        gath[pl.ds(me, 1)] = mine[...][None]
        copies = []
        for r in range(1, 8):
            peer = ((1 - x) if r & 4 else x, (1 - y) if r & 2 else y, (1 - c) if r & 1 else c)
            cp = pltpu.make_async_remote_copy(
                src_ref=mine, dst_ref=gath.at[me], send_sem=send_sems.at[r - 1], recv_sem=recv_sems.at[r - 1],
                device_id=peer, device_id_type=MESH)
            cp.start()
            copies.append(cp)
        for r in range(1, 8):
            peer = ((1 - x) if r & 4 else x, (1 - y) if r & 2 else y, (1 - c) if r & 1 else c)
            peer_id = 4 * peer[0] + 2 * peer[1] + peer[2]
            pltpu.make_async_remote_copy(
                src_ref=mine, dst_ref=gath.at[peer_id], send_sem=send_sems.at[r - 1], recv_sem=recv_sems.at[r - 1],
                device_id=peer, device_id_type=MESH).wait_recv()
        for cp in copies:
            cp.wait_send()
        total = gath[0]
        for dev in range(1, 8):
            total = total + gath[dev]
        out_ref[...] = total

    return pl.pallas_call(
        body, name="allreduce_small",
        in_specs=[_VMEM] * 8 + [_ANY], out_specs=_VMEM,
        out_shape=jax.ShapeDtypeStruct((SMALL_ROWS, d), F32),
        scratch_shapes=[pltpu.VMEM((SMALL_ROWS, d), F32), pltpu.VMEM((8, SMALL_ROWS, d), F32),
                        pltpu.SemaphoreType.DMA((7,)), pltpu.SemaphoreType.DMA((7,))],
    )(gl2g, gl2b, gl1g, gl1b, g_ac, gcw, gsink, loss, after)


def _adamw(w, g, m, v):
    m = ADAM_B1 * m + (1.0 - ADAM_B1) * g
    v = ADAM_B2 * v + (1.0 - ADAM_B2) * (g * g)
    m_hat = m / (1.0 - ADAM_B1 ** ADAM_STEP)
    v_hat = v / (1.0 - ADAM_B2 ** ADAM_STEP)
    delta = -ADAM_LR * (m_hat / (jnp.sqrt(v_hat) + ADAM_EPS) + ADAM_WD * w)
    return delta, m, v


def _adamw_shard(w, m, v, land, own, pos_vec, name, col_block=0):
    tr = _row_tile(w.shape[1] // 2, TR_ELT)
    grid = (w.shape[1] // tr,)
    body, in_specs, out_specs, out_shape = _adamw_passenger(w.shape, tr, grid, col_block)
    grid_spec = pltpu.PrefetchScalarGridSpec(num_scalar_prefetch=1, grid=grid, in_specs=in_specs, out_specs=out_specs)
    return pl.pallas_call(
        body, name=name, grid_spec=grid_spec, out_shape=out_shape,
        compiler_params=_params(("parallel",)),
    )(pos_vec, w, m, v, land, land, land, land, own)


def _adamw_passenger(shape, tr, grid, col_block):
    _, r, c = shape
    nh = r // 2 // tr
    n_blocks = 2 * nh
    n_steps = int(np.prod(grid))
    assert nh * tr * 2 == r and n_blocks <= n_steps

    def step_of(ids):
        step = ids[0]
        for n, i in zip(grid[1:], ids[1:]):
            step = step * n + i
        return step

    def block_of(ids):
        return jnp.minimum(step_of(ids), n_blocks - 1)

    def update(pos_ref, w_ref, m_ref, v_ref, l0, l1, l2, l3, own_ref, g_out, d_out, m_out, v_out):
        i = block_of([pl.program_id(a) for a in range(len(grid))])
        mine = (i // nh) == pos_ref[1]
        own_blk = own_ref[...].astype(F32)
        g = None
        for s, l_ref in enumerate([l0, l1, l2, l3]):
            term = jnp.where(mine & (pos_ref[0] == s), own_blk, l_ref[...].astype(F32))
            g = term if g is None else g + term
        delta, nm, nv = _adamw(w_ref[...], g, m_ref[...], v_ref[...])
        g_out[...] = g
        d_out[...] = delta
        m_out[...] = nm
        v_out[...] = nv

    def body(*refs):
        if n_blocks == n_steps:
            update(*refs)
        else:
            pl.when(step_of([pl.program_id(a) for a in range(len(grid))]) < n_blocks)(lambda: update(*refs))

    def land_spec(s):
        def index(*args):
            i, pos_ref = block_of(args[:-1]), args[-1]
            skip = (pos_ref[0] == s) & ((i // nh) == pos_ref[1])
            return (s, jnp.where(skip, (i + nh) % n_blocks, i), col_block)
        return pl.BlockSpec((None, tr, c), index)

    blk = pl.BlockSpec((None, tr, c), lambda *args: (0, block_of(args[:-1]), 0))
    in_specs = ([blk, blk, blk] + [land_spec(s) for s in range(N_CHIPS)]
                + [pl.BlockSpec((None, tr, c), lambda *args: (args[-1][0], block_of(args[:-1]) % nh, col_block))])
    return body, in_specs, [blk] * 4, [jax.ShapeDtypeStruct((1, r, c), F32)] * 4


def _call_with_adamw(body, name, grid, in_specs, out_specs, out_shape, scratch_shapes, semantics, operands, shard):
    if shard is None:
        return pl.pallas_call(
            body, name=name, grid=grid, in_specs=in_specs, out_specs=out_specs, out_shape=out_shape,
            scratch_shapes=scratch_shapes, compiler_params=_params(semantics))(*operands)
    w, m, v, land, own, pos_vec, col_block = shard
    n_steps = int(np.prod(grid))
    hr = w.shape[1] // 2
    tr = min(t for t in range(16, hr + 1, 16) if hr % t == 0 and 2 * (hr // t) <= n_steps)
    adam_body, adam_in, adam_out, adam_shape = _adamw_passenger(w.shape, tr, grid, col_block)
    n_in, n_out = len(in_specs), len(out_specs)

    def with_pos(spec):
        if spec.index_map is None:
            return spec
        return pl.BlockSpec(spec.block_shape, lambda *args: spec.index_map(*args[:-1]))

    def both(pos_ref, *refs):
        ins, adam_ins = refs[:n_in], refs[n_in:n_in + len(adam_in)]
        refs = refs[n_in + len(adam_in):]
        outs, adam_outs, scratch = refs[:n_out], refs[n_out:n_out + len(adam_out)], refs[n_out + len(adam_out):]
        body(*ins, *outs, *scratch)
        adam_body(pos_ref, *adam_ins, *adam_outs)

    grid_spec = pltpu.PrefetchScalarGridSpec(
        num_scalar_prefetch=1, grid=grid, in_specs=[with_pos(sp) for sp in in_specs] + adam_in,
        out_specs=[with_pos(sp) for sp in out_specs] + adam_out, scratch_shapes=scratch_shapes)
    return pl.pallas_call(
        both, name=name, grid_spec=grid_spec, out_shape=list(out_shape) + adam_shape,
        compiler_params=_params(semantics),
    )(pos_vec, *operands, w, m, v, land, land, land, land, own)


def _adamw_small(red, params):
    names = ["sinks", "g_attn", "g_conv", "ln1_g", "ln1_b", "ln2_g", "ln2_b", "conv_w"]
    d = red.shape[1]
    hd = d // 2
    flat = []
    for nme in names:
        flat.extend(params[nme])
    nq = params["sinks"][0].shape[1]
    cs = params["conv_w"][0].shape[2]

    def body(*refs):
        red_ref = refs[0]
        ins = refs[1:1 + 3 * len(names)]
        outs = refs[1 + 3 * len(names):]
        x, y, _, _ = _mesh_pos()
        me = _chip_id(x, y)

        def conv_tap(row, base):
            picked = red_ref[row:row + 1, base:base + cs]
            for s in range(1, N_CHIPS):
                picked = jnp.where(me == s, red_ref[row:row + 1, base + s * cs:base + (s + 1) * cs], picked)
            return picked

        grads = {
            "sinks": red_ref[6:7, hd:hd + nq],
            "g_attn": red_ref[4:5, 0:hd],
            "g_conv": red_ref[4:5, hd:d],
            "ln1_g": red_ref[2:3, :],
            "ln1_b": red_ref[3:4, :],
            "ln2_g": red_ref[0:1, :],
            "ln2_b": red_ref[1:2, :],
        }
        for i, nme in enumerate(names):
            w_ref, m_ref, v_ref = ins[3 * i:3 * i + 3]
            g_out, d_out, m_out, v_out = outs[4 * i:4 * i + 4]
            if nme == "conv_w":
                for tap, (row, base) in enumerate([(5, 0), (5, hd), (6, 0)]):
                    g = conv_tap(row, base)
                    delta, nm, nv = _adamw(w_ref[0, tap:tap + 1, :], g, m_ref[0, tap:tap + 1, :], v_ref[0, tap:tap + 1, :])
                    g_out[0, tap:tap + 1, :] = g
                    d_out[0, tap:tap + 1, :] = delta
                    m_out[0, tap:tap + 1, :] = nm
                    v_out[0, tap:tap + 1, :] = nv
            else:
                g = grads[nme]
                delta, nm, nv = _adamw(w_ref[...], g, m_ref[...], v_ref[...])
                g_out[...] = g
                d_out[...] = delta
                m_out[...] = nm
                v_out[...] = nv

    out_shape = []
    for nme in names:
        out_shape.extend([jax.ShapeDtypeStruct(params[nme][0].shape, F32)] * 4)
    outs = pl.pallas_call(
        body, name="adamw_small",
        in_specs=[_VMEM] * (1 + len(flat)), out_specs=[_VMEM] * len(out_shape),
        out_shape=out_shape,
    )(red, *flat)
    return {nme: tuple(outs[4 * i:4 * i + 4]) for i, nme in enumerate(names)}


def _rope_tables(pos_col):
    s = pos_col.shape[0]
    w = N_KV_HEADS * HEAD_DIM
    tb = min(512, s)
    inv_freq = (ROPE_THETA ** (-np.arange(0, ROT_DIM, 2, dtype=np.float32) / ROT_DIM)).astype(np.float32)

    def body(pos_ref, cos_ref, sin_ref):
        pos = pos_ref[...].astype(F32)
        lane = lax.broadcasted_iota(jnp.int32, (tb, PAIR), 1) & (HEAD_DIM - 1)
        fidx = lane & (ROT_DIM // 2 - 1)
        inv = jnp.zeros((tb, PAIR), F32)
        for k in range(ROT_DIM // 2):
            inv = jnp.where(fidx == k, float(inv_freq[k]), inv)
        ang = pos * inv
        rot = lane < ROT_DIM
        sin_v = jnp.sin(ang)
        cos_ref[...] = _tile_lanes(jnp.where(rot, jnp.cos(ang), 1.0), w // PAIR)
        sin_ref[...] = _tile_lanes(jnp.where(lane < ROT_DIM // 2, -sin_v, jnp.where(rot, sin_v, 0.0)), w // PAIR)

    return pl.pallas_call(
        body, name="rope_tables", grid=(s // tb,),
        in_specs=[pl.BlockSpec((tb, 1), lambda i: (i, 0))],
        out_specs=[pl.BlockSpec((tb, w), lambda i: (i, 0))] * 2,
        out_shape=[jax.ShapeDtypeStruct((s, w), F32)] * 2,
        compiler_params=_params(("parallel",)),
    )(pos_col)


def _in_proj(x, w_in_g, first_vec, n_shards, into, name):
    _, s, d = x.shape
    ns, _, ncol = w_in_g.shape
    tm = min(2 * TM, s)

    def body(first_ref, x_ref, w_ref, into_ref, o_ref):
        o_ref[...] = _dot(x_ref[...].astype(BF16), w_ref[...]).astype(BF16)

    shard = lambda j, first_ref: lax.rem(first_ref[0] + j, ns)
    grid_spec = pltpu.PrefetchScalarGridSpec(
        num_scalar_prefetch=1, grid=(s // tm, n_shards),
        in_specs=[pl.BlockSpec((None, tm, d), lambda i, j, first_ref: (0, i, 0)),
                  pl.BlockSpec((None, d, ncol), lambda i, j, first_ref: (shard(j, first_ref), 0, 0)), _ANY],
        out_specs=pl.BlockSpec((tm, ncol), lambda i, j, first_ref: (i, shard(j, first_ref))))
    return pl.pallas_call(
        body, name=name, grid_spec=grid_spec,
        out_shape=jax.ShapeDtypeStruct((s, ns * ncol), BF16),
        input_output_aliases={} if into is None else {3: 0},
        compiler_params=_params(("parallel", "arbitrary")),
    )(first_vec, x, w_in_g, first_vec if into is None else into)


PAIR = 2 * HEAD_DIM
KEYS = 2 * WINDOW


def _pair_operand(t_all, h):
    col = (h // 2) * PAIR
    lane = lax.broadcasted_iota(jnp.int32, (KEYS, PAIR), 1)
    own_low = h % 2 == 0
    mine = jnp.where((lane < HEAD_DIM) if own_low else (lane >= HEAD_DIM), t_all[:, col:col + PAIR], 0.0)
    other = pltpu.roll(mine, HEAD_DIM, 1)
    low, high = (mine, other) if own_low else (other, mine)
    return jnp.concatenate([low, high], axis=0).astype(BF16)


def _pair_grad(acc, h):
    lane = lax.broadcasted_iota(jnp.int32, (KEYS, PAIR), 1)
    low = jnp.where(lane < HEAD_DIM, acc[:KEYS], 0.0)
    high = jnp.where(lane >= HEAD_DIM, acc[KEYS:], 0.0)
    if h % 2 == 0:
        return low + pltpu.roll(high, HEAD_DIM, 1)
    return high + pltpu.roll(low, HEAD_DIM, 1)


N_PAIRS = N_KV_HEADS * GROUP // 2


def _all_probs(q, kk2s, first, sinks_ref):
    assert ATTN_SCALE == 0.125
    q = q * ATTN_SCALE
    qps, scores = [], []
    for pair in range(N_PAIRS):
        qp = q[:, pair * PAIR:(pair + 1) * PAIR].astype(BF16)
        qps.append(qp)
        scores.append(_dot_nt(qp, kk2s[pair // (GROUP // 2)]))
    qi = lax.broadcasted_iota(jnp.int32, (WINDOW, 2 * KEYS), 0)
    kj = lax.broadcasted_iota(jnp.int32, (WINDOW, 2 * KEYS), 1) & (KEYS - 1)
    rel = qi + WINDOW - kj
    valid = (rel >= 0) & (rel < WINDOW) & jnp.logical_not(first & (kj < WINDOW))
    bias = jnp.where(valid, 0.0, NEG_BIG)
    s = (jnp.stack(scores, axis=0) + bias[None]).reshape(N_PAIRS * WINDOW, 2 * KEYS)
    probs, p_sinks = [], []
    for t in range(2):
        st = s[:, t * KEYS:(t + 1) * KEYS]
        sink = jnp.concatenate([jnp.broadcast_to(sinks_ref[0:1, 2 * pair + t:2 * pair + t + 1], (WINDOW, 1))
                                for pair in range(N_PAIRS)], axis=0)
        m = jnp.maximum(jnp.max(st, axis=1, keepdims=True), sink)
        e = jnp.exp(st - m)
        e_sink = jnp.exp(sink - m)
        inv_l = 1.0 / (jnp.sum(e, axis=1, keepdims=True) + e_sink)
        probs.append(e * inv_l)
        p_sinks.append(e_sink * inv_l)
    return qps, jnp.concatenate(probs, axis=1), p_sinks


def _roped_qkv(cur_ref, prev_ref, cos_ref, sin_ref, cosp_ref, sinp_ref, qw, kvw):
    cur = cur_ref[...].astype(F32)
    cos, sin = cos_ref[...], sin_ref[...]
    cos_q, sin_q = _tile_lanes(cos, GROUP), _tile_lanes(sin, GROUP)
    q = _rope(cur[:, :qw], cos_q, sin_q, 1.0)
    prev = prev_ref[...].astype(F32)
    k_all = jnp.concatenate([_rope(prev[:, :kvw], cosp_ref[...], sinp_ref[...], 1.0),
                             _rope(cur[:, qw:qw + kvw], cos, sin, 1.0)], axis=0)
    v_all = jnp.concatenate([prev[:, kvw:], cur[:, qw + kvw:]], axis=0)
    return q, k_all, v_all, cos_q, sin_q


def _attention_fwd(proj, cos_t, sin_t, sinks):
    s = proj.shape[0]
    qw = GROUP * N_KV_HEADS * HEAD_DIM
    kvw = N_KV_HEADS * HEAD_DIM
    nb = s // WINDOW

    def body(cur_ref, prev_ref, cos_ref, sin_ref, cosp_ref, sinp_ref, sinks_ref, o_ref):
        first = pl.program_id(0) == 0
        q, k_all, v_all, _, _ = _roped_qkv(cur_ref, prev_ref, cos_ref, sin_ref, cosp_ref, sinp_ref, qw, kvw)
        kk2s = [_pair_operand(k_all, h) for h in range(N_KV_HEADS)]
        vv2s = [_pair_operand(v_all, h) for h in range(N_KV_HEADS)]
        _, probs, _ = _all_probs(q, kk2s, first, sinks_ref)
        probs = probs.astype(BF16)
        outs = [_dot(probs[pair * WINDOW:(pair + 1) * WINDOW], vv2s[pair // (GROUP // 2)]) for pair in range(N_PAIRS)]
        o_ref[...] = jnp.concatenate(outs, axis=1)

    tbl = pl.BlockSpec((WINDOW, kvw), lambda n: (n, 0))
    tbl_prev = pl.BlockSpec((WINDOW, kvw), lambda n: (jnp.maximum(n - 1, 0), 0))
    return pl.pallas_call(
        body, name="attention_fwd", grid=(nb,),
        in_specs=[pl.BlockSpec((WINDOW, qw + 2 * kvw), lambda n: (n, 0)),
                  pl.BlockSpec((WINDOW, 2 * kvw), lambda n: (jnp.maximum(n - 1, 0), (qw // (2 * kvw)))),
                  tbl, tbl, tbl_prev, tbl_prev, _VMEM],
        out_specs=pl.BlockSpec((WINDOW, qw), lambda n: (n, 0)),
        out_shape=jax.ShapeDtypeStruct((s, qw), F32),
        compiler_params=_params(("parallel",)),
    )(proj, proj, cos_t, sin_t, cos_t, sin_t, sinks)


def _conv_taps(cw_ref):
    return [jnp.concatenate([cw_ref[s, k:k + 1, :] for s in range(N_CHIPS)], axis=1) for k in range(3)]


def _shift_down(z, halo, steps):
    last = halo.shape[0]
    row = lax.broadcasted_iota(jnp.int32, z.shape, 0)
    out = pltpu.roll(z, steps, 0)
    for r in range(steps):
        out = jnp.where(row == r, halo[last - steps + r:last - steps + r + 1, :], out)
    return out


def _shift_up(z, halo, steps):
    rows = z.shape[0]
    row = lax.broadcasted_iota(jnp.int32, z.shape, 0)
    out = pltpu.roll(z, rows - steps, 0)
    for r in range(steps):
        out = jnp.where(row == rows - steps + r, halo[r:r + 1, :], out)
    return out


def _split_cbu(lo, hi, cw):
    lo, hi = lo.astype(F32), hi.astype(F32)
    c_gate = lo[:, :cw]
    b_gate = jnp.concatenate([lo[:, cw:], hi[:, :2 * cw - lo.shape[1]]], axis=1)
    u = hi[:, 2 * cw - lo.shape[1]:]
    return c_gate, b_gate, u


def _conv_norm(proj, attn, cw_full, g_ac):
    s, in_w = proj.shape
    cw = attn.shape[1]
    blk_w = in_w // 3
    tb = min(TB_CONV, s)

    def body(lo_ref, hi_ref, lo_h_ref, hi_h_ref, attn_ref, cw_ref, g_ref, mixed_ref, ac_ref, rstd_ref):
        i = pl.program_id(0)
        c_gate, b_gate, u = _split_cbu(lo_ref[...], hi_ref[...], cw)
        c_h, _, u_h = _split_cbu(lo_h_ref[...], hi_h_ref[...], cw)
        z = c_gate * u
        z_h = jnp.where(i == 0, 0.0, c_h * u_h)
        w0, w1, w2 = _conv_taps(cw_ref)
        y = w0 * _shift_down(z, z_h, 2) + w1 * _shift_down(z, z_h, 1) + w2 * z
        conv = b_gate * y
        a = attn_ref[...]
        r_a = lax.rsqrt(jnp.mean(a * a, axis=-1, keepdims=True) + RMS_EPS)
        r_c = lax.rsqrt(jnp.mean(conv * conv, axis=-1, keepdims=True) + RMS_EPS)
        g = g_ref[...]
        mixed_ref[...] = jnp.concatenate([a * r_a * g[:, :cw], conv * r_c * g[:, cw:]], axis=1).astype(BF16)
        ac_ref[...] = jnp.concatenate([a, conv], axis=1)
        rstd_ref[0] = r_a
        rstd_ref[1] = r_c

    halo_idx = lambda i: jnp.maximum(i * (tb // HALO_ROWS) - 1, 0)
    return pl.pallas_call(
        body, name="conv_norm", grid=(s // tb,),
        in_specs=[pl.BlockSpec((tb, blk_w), lambda i: (i, 1)),
                  pl.BlockSpec((tb, blk_w), lambda i: (i, 2)),
                  pl.BlockSpec((HALO_ROWS, blk_w), lambda i: (halo_idx(i), 1)),
                  pl.BlockSpec((HALO_ROWS, blk_w), lambda i: (halo_idx(i), 2)),
                  pl.BlockSpec((tb, cw), lambda i: (i, 0)),
                  _VMEM, _VMEM],
        out_specs=[pl.BlockSpec((tb, 2 * cw), lambda i: (i, 0)),
                   pl.BlockSpec((tb, 2 * cw), lambda i: (i, 0)),
                   pl.BlockSpec((2, tb, 1), lambda i: (0, i, 0))],
        out_shape=[jax.ShapeDtypeStruct((s, 2 * cw), BF16), jax.ShapeDtypeStruct((s, 2 * cw), F32),
                   jax.ShapeDtypeStruct((2, s, 1), F32)],
        compiler_params=_params(("parallel",)),
    )(proj, proj, proj, proj, attn, cw_full, g_ac)


def _out_proj_ln(mixed, w_out_g, x, ln_g, ln_b):
    s, d = mixed.shape
    tm = min(TM, s)
    tk = d
    nk = d // tk

    def body(a_ref, w_ref, x_ref, g_ref, b_ref, xhat_ref, h_ref, rstd_ref, acc):
        k = pl.program_id(1)
        _accumulate(acc, lambda: _dot(a_ref[...], w_ref[...]), k, nk)

        @pl.when(k == nk - 1)
        def _():
            def rows_fn(rows):
                xhat, rstd = _ln_fwd(ALPHA * x_ref[rows, :] + acc[rows, :])
                xhat_ref[rows, :] = xhat
                h_ref[rows, :] = (xhat * g_ref[...] + b_ref[...]).astype(BF16)
                rstd_ref[rows, :] = rstd

            _for_row_chunks(tm, rows_fn)

    row = pl.BlockSpec((tm, d), lambda i, k: (i, 0))
    return pl.pallas_call(
        body, name="out_proj_ln", grid=(s // tm, nk),
        in_specs=[pl.BlockSpec((tm, tk), lambda i, k: (i, k)),
                  pl.BlockSpec((tk, d), lambda i, k: (k, 0)),
                  pl.BlockSpec((None, tm, d), lambda i, k: (0, i, 0)),
                  _VMEM, _VMEM],
        out_specs=[row, row, pl.BlockSpec((tm, 1), lambda i, k: (i, 0))],
        out_shape=[jax.ShapeDtypeStruct((s, d), F32), jax.ShapeDtypeStruct((s, d), BF16),
                   jax.ShapeDtypeStruct((s, 1), F32)],
        scratch_shapes=[pltpu.VMEM((tm, d), F32)],
        compiler_params=_params(("parallel", "arbitrary")),
    )(mixed, w_out_g, x, ln_g, ln_b)


def _gate_up(h1, w_gu_g, first_vec, n_shards, into, name):
    s, d = h1.shape
    ns, _, fs2 = w_gu_g.shape
    fs = fs2 // 2
    tm = min(TM, s)

    def body(first_ref, h_ref, w_ref, act_in, ab_in, act_ref, ab_ref):
        gu = _dot(h_ref[...], w_ref[...])
        g, u = gu[:, :fs], gu[:, fs:]
        sg = _sigmoid(g)
        silu = g * sg
        act_ref[...] = (silu * u).astype(BF16)
        ab_ref[:, :fs] = (u * (sg * (1.0 + g * (1.0 - sg)))).astype(BF16)
        ab_ref[:, fs:] = silu.astype(BF16)

    shard = lambda j, first_ref: lax.rem(first_ref[0] + j, ns)
    grid_spec = pltpu.PrefetchScalarGridSpec(
        num_scalar_prefetch=1, grid=(s // tm, n_shards),
        in_specs=[pl.BlockSpec((tm, d), lambda i, j, first_ref: (i, 0)),
                  pl.BlockSpec((None, d, fs2), lambda i, j, first_ref: (shard(j, first_ref), 0, 0)), _ANY, _ANY],
        out_specs=[pl.BlockSpec((tm, fs), lambda i, j, first_ref: (i, shard(j, first_ref))),
                   pl.BlockSpec((tm, fs2), lambda i, j, first_ref: (i, shard(j, first_ref)))])
    return pl.pallas_call(
        body, name=name, grid_spec=grid_spec,
        out_shape=[jax.ShapeDtypeStruct((s, ns * fs), BF16), jax.ShapeDtypeStruct((s, ns * fs2), BF16)],
        input_output_aliases={} if into is None else {3: 0, 4: 1},
        compiler_params=_params(("parallel", "arbitrary")),
    )(first_vec, h1, w_gu_g, *((first_vec, first_vec) if into is None else into))


def _down_ln_loss(act, w_down_g, xhat1, ln1_g, ln1_b, ln2_g, ln2_b, target):
    s, f = act.shape
    d = xhat1.shape[1]
    tm = min(TM, s)
    tk = f // N_CHIPS
    nk = f // tk

    def body(a_ref, w_ref, xh_ref, g1_ref, b1_ref, g2_ref, b2_ref, t_ref, dpre_ref, dpre16_ref, loss_ref, gg_ref, gb_ref,
             acc):
        i, k = pl.program_id(0), pl.program_id(1)
        _accumulate(acc, lambda: _dot(a_ref[...], w_ref[...]), k, nk)

        @pl.when(k == nk - 1)
        def _():
            @pl.when(i == 0)
            def _():
                loss_ref[...] = jnp.zeros_like(loss_ref)
                gg_ref[...] = jnp.zeros_like(gg_ref)
                gb_ref[...] = jnp.zeros_like(gb_ref)

            def rows_fn(rows):
                h1 = xh_ref[rows, :] * g1_ref[...] + b1_ref[...]
                xhat, rstd = _ln_fwd(ALPHA * h1 + acc[rows, :])
                g2 = g2_ref[...]
                diff = xhat * g2 + b2_ref[...] - t_ref[rows, :]
                dy = diff * (1.0 / d)
                dpre = _ln_bwd(dy, xhat, rstd, g2)
                dpre_ref[rows, :] = dpre
                dpre16_ref[rows, :] = dpre.astype(BF16)
                sq = jnp.sum(jnp.sum(diff * diff, axis=1, keepdims=True), axis=0, keepdims=True)
                loss_ref[...] += jnp.broadcast_to(sq * (0.5 / d), (1, 128))
                gg_ref[...] += jnp.sum(dy * xhat, axis=0, keepdims=True)
                gb_ref[...] += jnp.sum(dy, axis=0, keepdims=True)

            _for_row_chunks(tm, rows_fn)

    row = pl.BlockSpec((tm, d), lambda i, k: (i, 0))
    vec = pl.BlockSpec((1, d), lambda i, k: (0, 0))
    return pl.pallas_call(
        body, name="down_ln_loss", grid=(s // tm, nk),
        in_specs=[pl.BlockSpec((tm, tk), lambda i, k: (i, k)),
                  pl.BlockSpec((tk, d), lambda i, k: (k, 0)),
                  row, _VMEM, _VMEM, _VMEM, _VMEM,
                  pl.BlockSpec((None, tm, d), lambda i, k: (0, i, 0))],
        out_specs=[row, row, pl.BlockSpec((1, 128), lambda i, k: (0, 0)), vec, vec],
        out_shape=[jax.ShapeDtypeStruct((s, d), F32), jax.ShapeDtypeStruct((s, d), BF16),
                   jax.ShapeDtypeStruct((1, 128), F32), jax.ShapeDtypeStruct((1, d), F32),
                   jax.ShapeDtypeStruct((1, d), F32)],
        scratch_shapes=[pltpu.VMEM((tm, d), F32)],
        compiler_params=_params(("arbitrary", "arbitrary")),
    )(act, w_down_g, xhat1, ln1_g, ln1_b, ln2_g, ln2_b, target)


def _dact_silu_bwd(dpre2, w_down_g, ab):
    s, d = dpre2.shape
    fs2 = ab.shape[1] // N_CHIPS
    fs = fs2 // 2
    tm = min(TM, s)

    def body(dp_ref, w_ref, ab_ref, dgu_ref):
        d_act = _dot_nt(dp_ref[...], w_ref[...])
        dgu_ref[:, :fs] = (d_act * ab_ref[:, :fs].astype(F32)).astype(BF16)
        dgu_ref[:, fs:] = (d_act * ab_ref[:, fs:].astype(F32)).astype(BF16)

    blk = pl.BlockSpec((tm, fs2), lambda j, i: (i, j))
    return pl.pallas_call(
        body, name="dact_silu_bwd", grid=(N_CHIPS, s // tm),
        in_specs=[pl.BlockSpec((tm, d), lambda j, i: (i, 0)),
                  pl.BlockSpec((fs, d), lambda j, i: (j, 0)), blk],
        out_specs=blk,
        out_shape=jax.ShapeDtypeStruct(ab.shape, BF16),
        compiler_params=_params(("parallel", "parallel")),
    )(dpre2, w_down_g, ab)


def _grad_rows(a, b, after, name, row_blocks=1):
    s, m = a.shape
    n = b.shape[1]
    ms = m // N_CHIPS
    tmw = ms // row_blocks
    tk = min(TK_TOK, s)
    nk = s // tk

    def body(a_ref, b_ref, after_ref, o_ref, acc):
        k = pl.program_id(2)
        _accumulate(acc, lambda: _dot_tn(a_ref[...].astype(BF16), b_ref[...].astype(BF16)), k, nk)

        @pl.when(k == nk - 1)
        def _():
            o_ref[...] = acc[...].astype(BF16)

    return pl.pallas_call(
        body, name=name, grid=(N_CHIPS, row_blocks, nk),
        in_specs=[pl.BlockSpec((tk, tmw), lambda j, r, k: (k, j * row_blocks + r)),
                  pl.BlockSpec((tk, n), lambda j, r, k: (k, 0)), _ANY],
        out_specs=pl.BlockSpec((None, tmw, n), lambda j, r, k: (j, r, 0)),
        out_shape=jax.ShapeDtypeStruct((N_CHIPS, ms, n), BF16),
        scratch_shapes=[pltpu.VMEM((tmw, n), F32)],
        compiler_params=_params(("parallel", "parallel", "arbitrary")),
    )(a, b, after)


def _grad_cols(a, bs, after, name, a_3d=False, row_blocks=2, shard=None):
    s, m = a.shape[-2:]
    n = bs[0].shape[1]
    ns = n // N_CHIPS
    nb = len(bs)
    tmw = m // row_blocks
    tk = min(TK_TOK, s)
    nk = s // tk

    def body(*refs):
        a_ref, b_refs, o_refs, accs = refs[0], refs[1:1 + nb], refs[2 + nb:2 + 2 * nb], refs[2 + 2 * nb:]
        k = pl.program_id(2)
        for b_ref, acc in zip(b_refs, accs):
            _accumulate(acc, lambda b_ref=b_ref: _dot_tn(a_ref[...].astype(BF16), b_ref[...].astype(BF16)), k, nk)

        @pl.when(k == nk - 1)
        def _():
            for o_ref, acc in zip(o_refs, accs):
                o_ref[...] = acc[...].astype(BF16)

    if a_3d:
        a_spec = pl.BlockSpec((None, tk, tmw), lambda j, r, k: (0, k, r))
    else:
        a_spec = pl.BlockSpec((tk, tmw), lambda j, r, k: (k, r))
    return _call_with_adamw(
        body, name, (N_CHIPS, row_blocks, nk),
        [a_spec] + [pl.BlockSpec((tk, ns), lambda j, r, k: (k, j))] * nb + [_ANY],
        [pl.BlockSpec((None, tmw, ns), lambda j, r, k: (j, r, 0))] * nb,
        [jax.ShapeDtypeStruct((N_CHIPS, m, ns), BF16)] * nb,
        [pltpu.VMEM((tmw, ns), F32)] * nb, ("parallel", "parallel", "arbitrary"), (a, *bs, after), shard)


def _dh1_ln_bwd(d_gu, w_gu_g, dpre2, xhat1, rstd1, ln1_g, after):
    s = d_gu.shape[0]
    d = dpre2.shape[1]
    hd = d // 2
    fs = w_gu_g.shape[2]
    tm = min(TM, s)

    def body(dgu_ref, w_ref, dp2_ref, xh_ref, rs_ref, g_ref, after_ref, dpre_ref, gg_ref, gb_ref, acc_lo, acc_hi):
        i, j, half = pl.program_id(0), pl.program_id(1), pl.program_id(2)

        def product():
            return _dot_nt(dgu_ref[...], w_ref[...])

        @pl.when(half == 0)
        def _():
            _accumulate(acc_lo, product, j, N_CHIPS)

        @pl.when(half == 1)
        def _():
            _accumulate(acc_hi, product, j, N_CHIPS)

        @pl.when((j == N_CHIPS - 1) & (half == 1))
        def _():
            @pl.when(i == 0)
            def _():
                gg_ref[...] = jnp.zeros_like(gg_ref)
                gb_ref[...] = jnp.zeros_like(gb_ref)

            def rows_fn(rows):
                dh = jnp.concatenate([acc_lo[rows, :], acc_hi[rows, :]], axis=1) + ALPHA * dp2_ref[rows, :]
                xhat = xh_ref[rows, :]
                dpre_ref[rows, :] = _ln_bwd(dh, xhat, rs_ref[rows, :], g_ref[...])
                gg_ref[...] += jnp.sum(dh * xhat, axis=0, keepdims=True)
                gb_ref[...] += jnp.sum(dh, axis=0, keepdims=True)

            _for_row_chunks(tm, rows_fn)

    row = pl.BlockSpec((tm, d), lambda i, j, h: (i, 0))
    vec = pl.BlockSpec((1, d), lambda i, j, h: (0, 0))
    act_blk = pl.BlockSpec((tm, fs), lambda i, j, h: (i, j))
    w_blk = pl.BlockSpec((None, hd, fs), lambda i, j, h: (j, h, 0))
    return pl.pallas_call(
        body, name="dh1_ln_bwd", grid=(s // tm, N_CHIPS, 2),
        in_specs=[act_blk, w_blk, row, row, pl.BlockSpec((tm, 1), lambda i, j, h: (i, 0)), _VMEM, _ANY],
        out_specs=[row, vec, vec],
        out_shape=[jax.ShapeDtypeStruct((s, d), F32), jax.ShapeDtypeStruct((1, d), F32),
                   jax.ShapeDtypeStruct((1, d), F32)],
        scratch_shapes=[pltpu.VMEM((tm, hd), F32)] * 2,
        compiler_params=_params(("arbitrary", "arbitrary", "arbitrary")),
    )(d_gu, w_gu_g, dpre2, xhat1, rstd1, ln1_g, after)


def _dmixed_rms_bwd(dpre1, w_out_g, ac, rstd, g_ac):
    s, d = dpre1.shape
    hd = d // 2
    tm = min(TM, s)

    def body(dp_ref, w_ref, ac_ref, rs_ref, g_ref, dac_ref, gg_ref):
        i = pl.program_id(1)
        dm = _dot_nt(dp_ref[...].astype(BF16), w_ref[...])
        pre = ac_ref[...]
        r = rs_ref[...]
        gdm = dm * g_ref[...]
        dac_ref[...] = r * gdm - pre * (r * r * r) * jnp.mean(gdm * pre, axis=-1, keepdims=True)
        gg = jnp.sum(dm * pre * r, axis=0, keepdims=True)

        @pl.when(i == 0)
        def _():
            gg_ref[...] = gg

        @pl.when(i > 0)
        def _():
            gg_ref[...] += gg

    return pl.pallas_call(
        body, name="dmixed_rms_bwd", grid=(2, s // tm),
        in_specs=[pl.BlockSpec((tm, d), lambda h, i: (i, 0)),
                  pl.BlockSpec((hd, d), lambda h, i: (h, 0)),
                  pl.BlockSpec((tm, hd), lambda h, i: (i, h)),
                  pl.BlockSpec((None, tm, 1), lambda h, i: (h, i, 0)),
                  pl.BlockSpec((1, hd), lambda h, i: (0, h))],
        out_specs=[pl.BlockSpec((tm, hd), lambda h, i: (i, h)),
                   pl.BlockSpec((1, hd), lambda h, i: (0, h))],
        out_shape=[jax.ShapeDtypeStruct((s, d), F32), jax.ShapeDtypeStruct((1, d), F32)],
        compiler_params=_params(("arbitrary", "arbitrary")),
    )(dpre1, w_out_g, ac, rstd, g_ac)


def _attention_bwd(proj, d_ac, cos_t, sin_t, sinks, after, shard):
    s = proj.shape[0]
    qw = GROUP * N_KV_HEADS * HEAD_DIM
    kvw = N_KV_HEADS * HEAD_DIM
    nb = s // WINDOW
    nq = GROUP * N_KV_HEADS

    def body(cur_ref, prev_ref, do_ref, cos_ref, sin_ref, cosp_ref, sinp_ref, sinks_ref, after_ref,
             dq_ref, dcur_ref, dprev_ref, dsink_ref):
        n = pl.program_id(0)
        first = n == 0
        q, k_all, v_all, cos_q, sin_q = _roped_qkv(cur_ref, prev_ref, cos_ref, sin_ref, cosp_ref, sinp_ref, qw, kvw)
        kk2s = [_pair_operand(k_all, h) for h in range(N_KV_HEADS)]
        vv2s = [_pair_operand(v_all, h) for h in range(N_KV_HEADS)]
        qps, probs, p_sinks = _all_probs(q, kk2s, first, sinks_ref)
        dops = [do_ref[:, pair * PAIR:(pair + 1) * PAIR].astype(BF16) for pair in range(N_PAIRS)]
        d_probs = jnp.concatenate([_dot_nt(dops[pair], vv2s[pair // (GROUP // 2)]) for pair in range(N_PAIRS)], axis=0)
        d_s, ds_sinks = [], []
        for t in range(2):
            cols = slice(t * KEYS, (t + 1) * KEYS)
            delta = jnp.sum(probs[:, cols] * d_probs[:, cols], axis=1, keepdims=True)
            d_s.append(probs[:, cols] * (d_probs[:, cols] - delta))
            ds_sinks.append(-p_sinks[t] * delta)
        d_s = jnp.concatenate(d_s, axis=1).astype(BF16)
        probs = probs.astype(BF16)
        dq_parts, dk_tiles, dv_tiles, dsink_parts = [], [], [], []
        for h in range(N_KV_HEADS):
            dkk2, dvv2 = None, None
            for p in range(GROUP // 2):
                pair = (GROUP // 2) * h + p
                rows = slice(pair * WINDOW, (pair + 1) * WINDOW)
                dq_parts.append(_dot(d_s[rows], kk2s[h]) * ATTN_SCALE)
                dk_term = _dot_tn(d_s[rows], qps[pair])
                dv_term = _dot_tn(probs[rows], dops[pair])
                dkk2 = dk_term if dkk2 is None else dkk2 + dk_term
                dvv2 = dv_term if dvv2 is None else dvv2 + dv_term
                dsink_parts.extend([jnp.sum(ds_sinks[t][rows], axis=0, keepdims=True) for t in range(2)])
            dk_tiles.append(_pair_grad(dkk2, h))
            dv_tiles.append(_pair_grad(dvv2, h))
        dq_ref[...] = _rope(jnp.concatenate(dq_parts, axis=1), cos_q, sin_q, -1.0)
        dk = jnp.concatenate([dk_tiles[0] + dk_tiles[1], dk_tiles[2] + dk_tiles[3]], axis=1)
        dv = jnp.concatenate([dv_tiles[0] + dv_tiles[1], dv_tiles[2] + dv_tiles[3]], axis=1)
        dprev_ref[...] = jnp.concatenate([dk[:WINDOW], dv[:WINDOW]], axis=1)
        dcur_ref[...] = jnp.concatenate([dk[WINDOW:], dv[WINDOW:]], axis=1)
        dsink = jnp.concatenate(dsink_parts, axis=1)

        @pl.when(first)
        def _():
            dsink_ref[...] = dsink

        @pl.when(n > 0)
        def _():
            dsink_ref[...] += dsink

    tbl = pl.BlockSpec((WINDOW, kvw), lambda n: (n, 0))
    tbl_prev = pl.BlockSpec((WINDOW, kvw), lambda n: (jnp.maximum(n - 1, 0), 0))
    kv_blk = pl.BlockSpec((WINDOW, 2 * kvw), lambda n: (n, 0))
    return _call_with_adamw(
        body, "attention_bwd", (nb,),
        [pl.BlockSpec((WINDOW, qw + 2 * kvw), lambda n: (n, 0)),
         pl.BlockSpec((WINDOW, 2 * kvw), lambda n: (jnp.maximum(n - 1, 0), (qw // (2 * kvw)))),
         pl.BlockSpec((WINDOW, qw), lambda n: (n, 0)),
         tbl, tbl, tbl_prev, tbl_prev, _VMEM, _ANY],
        [pl.BlockSpec((WINDOW, qw), lambda n: (n, 0)), kv_blk, kv_blk, pl.BlockSpec((1, nq), lambda n: (0, 0))],
        [jax.ShapeDtypeStruct((s, qw), F32), jax.ShapeDtypeStruct((s, 2 * kvw), F32),
         jax.ShapeDtypeStruct((s, 2 * kvw), F32), jax.ShapeDtypeStruct((1, nq), F32)],
        [], ("arbitrary",), (proj, proj, d_ac, cos_t, sin_t, cos_t, sin_t, sinks, after), shard)


def _dproj_assemble(proj, d_ac, dq, dkv_cur, dkv_prev, cos_t, sin_t, cw_full):
    s, in_w = proj.shape
    cw = dq.shape[1]
    kvw = N_KV_HEADS * HEAD_DIM
    blk_w = in_w // 3
    tb = WINDOW
    nb = s // tb

    def body(lo_ref, hi_ref, lo_p_ref, hi_p_ref, lo_n_ref, hi_n_ref, dconv_ref, dconv_n_ref,
             dq_ref, dcur_ref, dprev_n_ref, cos_ref, sin_ref, cw_ref, dproj_ref, gcw_ref):
        i = pl.program_id(0)
        last = i == nb - 1
        c_gate, b_gate, u = _split_cbu(lo_ref[...], hi_ref[...], cw)
        c_p, _, u_p = _split_cbu(lo_p_ref[...], hi_p_ref[...], cw)
        _, b_n, _ = _split_cbu(lo_n_ref[...], hi_n_ref[...], cw)
        z = c_gate * u
        z_p = jnp.where(i == 0, 0.0, c_p * u_p)
        z1 = _shift_down(z, z_p, 1)
        z2 = _shift_down(z, z_p, 2)
        w0, w1, w2 = _conv_taps(cw_ref)
        y = w0 * z2 + w1 * z1 + w2 * z
        d_conv = dconv_ref[...]
        d_b = d_conv * y
        d_y = d_conv * b_gate
        d_y_n = jnp.where(last, 0.0, dconv_n_ref[...] * b_n[:dconv_n_ref.shape[0]])
        d_z = w2 * d_y + w1 * _shift_up(d_y, d_y_n, 1) + w0 * _shift_up(d_y, d_y_n, 2)
        d_c = d_z * u
        d_u = d_z * c_gate
        gcw = jnp.concatenate([jnp.sum(d_y * z2, axis=0, keepdims=True), jnp.sum(d_y * z1, axis=0, keepdims=True),
                               jnp.sum(d_y * z, axis=0, keepdims=True)], axis=0)

        @pl.when(i == 0)
        def _():
            gcw_ref[...] = gcw

        @pl.when(i > 0)
        def _():
            gcw_ref[...] += gcw

        dkv = dcur_ref[...] + jnp.where(last, 0.0, dprev_n_ref[...])
        dk = _rope(dkv[:, :kvw], cos_ref[...], sin_ref[...], -1.0)
        dproj_ref[...] = jnp.concatenate([dq_ref[...], dk, dkv[:, kvw:], d_c, d_b, d_u], axis=1).astype(BF16)

    prev_halo = lambda i: jnp.maximum(i * (tb // HALO_ROWS) - 1, 0)
    next_halo = lambda i: jnp.minimum((i + 1) * (tb // HALO_ROWS), s // HALO_ROWS - 1)
    next8 = lambda i: jnp.minimum((i + 1) * (tb // 8), s // 8 - 1)
    nxt = lambda i: jnp.minimum(i + 1, nb - 1)
    return pl.pallas_call(
        body, name="dproj_assemble", grid=(nb,),
        in_specs=[pl.BlockSpec((tb, blk_w), lambda i: (i, 1)),
                  pl.BlockSpec((tb, blk_w), lambda i: (i, 2)),
                  pl.BlockSpec((HALO_ROWS, blk_w), lambda i: (prev_halo(i), 1)),
                  pl.BlockSpec((HALO_ROWS, blk_w), lambda i: (prev_halo(i), 2)),
                  pl.BlockSpec((HALO_ROWS, blk_w), lambda i: (next_halo(i), 1)),
                  pl.BlockSpec((HALO_ROWS, blk_w), lambda i: (next_halo(i), 2)),
                  pl.BlockSpec((tb, cw), lambda i: (i, 1)),
                  pl.BlockSpec((8, cw), lambda i: (next8(i), 1)),
                  pl.BlockSpec((tb, cw), lambda i: (i, 0)),
                  pl.BlockSpec((tb, 2 * kvw), lambda i: (i, 0)),
                  pl.BlockSpec((tb, 2 * kvw), lambda i: (nxt(i), 0)),
                  pl.BlockSpec((tb, kvw), lambda i: (i, 0)),
                  pl.BlockSpec((tb, kvw), lambda i: (i, 0)),
                  _VMEM],
        out_specs=[pl.BlockSpec((tb, in_w), lambda i: (i, 0)),
                   pl.BlockSpec((3, cw), lambda i: (0, 0))],
        out_shape=[jax.ShapeDtypeStruct((s, in_w), BF16), jax.ShapeDtypeStruct((3, cw), F32)],
        compiler_params=_params(("arbitrary",)),
    )(proj, proj, proj, proj, proj, proj, d_ac, d_ac, dq, dkv_cur, dkv_prev, cos_t, sin_t, cw_full)


def _dx(d_proj, w_in_g, dpre1, after, shard):
    s, in_w = d_proj.shape
    ns, d, ncol = w_in_g.shape
    tm = min(TM, s)

    def body(dp_ref, w_ref, r_ref, after_ref, o_ref, acc):
        j = pl.program_id(1)
        _accumulate(acc, lambda: _dot_nt(dp_ref[...], w_ref[...]), j, ns)

        @pl.when(j == ns - 1)
        def _():
            o_ref[...] = acc[...] + ALPHA * r_ref[...]

    return _call_with_adamw(
        body, "dx", (s // tm, ns),
        [pl.BlockSpec((tm, ncol), lambda i, j: (i, j)),
         pl.BlockSpec((None, d, ncol), lambda i, j: (j, 0, 0)),
         pl.BlockSpec((tm, d), lambda i, j: (i, 0)), _ANY],
        [pl.BlockSpec((None, tm, d), lambda i, j: (0, i, 0))], [jax.ShapeDtypeStruct((1, s, d), F32)],
        [pltpu.VMEM((tm, d), F32)], ("parallel", "arbitrary"), (d_proj, w_in_g, dpre1, after), shard)


def kernel(x, positions, w_in, conv_w, sinks, g_attn, g_conv, w_out, ln1_g, ln1_b, w_gate, w_up, w_down, ln2_g, ln2_b, loss_target, m_w_in, m_conv_w, m_sinks, m_g_attn, m_g_conv, m_w_out, m_ln1_g, m_ln1_b, m_w_gate, m_w_up, m_w_down, m_ln2_g, m_ln2_b, v_w_in, v_conv_w, v_sinks, v_g_attn, v_g_conv, v_w_out, v_ln1_g, v_ln1_b, v_w_gate, v_w_up, v_w_down, v_ln2_g, v_ln2_b):
    s = x.shape[1]
    d = x.shape[2]

    chip_vec = _chip_id(lax.axis_index("x"), lax.axis_index("y")).astype(jnp.int32).reshape(1)
    wnames = ["w_in", "w_out", "w_gu", "w_down"]
    buf_in = _cast_weight(w_in, chip_vec, chip_vec, "cast_w_in")
    flight_in, token_in = _gather_start([buf_in], chip_vec, "gather_start_w_in")
    cw_buf = lax.dynamic_update_slice(jnp.zeros((N_CHIPS,) + conv_w.shape[1:], F32), conv_w, (chip_vec[0], 0, 0))
    cw_flight = _flight_start("conv_w_start", [cw_buf], _conv_w_plan(), 3, token_in)
    started = cw_flight[2][0]
    buf_gu = _cast_weight(w_gate, chip_vec, started, "cast_w_gate", 0, 2)
    buf_gu = _cast_weight(w_up, chip_vec, buf_gu, "cast_w_up", 1, 2)
    bufs = [_cast_weight(w_out, chip_vec, started, "cast_w_out"), buf_gu,
            _cast_weight(w_down, chip_vec, started, "cast_w_down")]
    flights_rest, token = _gather_start(bufs, token_in, "gather_start_rest")
    flights = flight_in + flights_rest

    def gathered(i, after):
        send_sems, recv_sems, buf = flights[i]
        buf = _gather_wait(send_sems, recv_sems, buf, after, "gather_wait_" + wnames[i])
        return _sibling_fill(buf, "sibling_fill_" + wnames[i])

    g_ac = jnp.concatenate([g_attn, g_conv], axis=1)

    proj_own = _in_proj(x, _after(flights[0][2], token), chip_vec, 1, None, "in_proj_own")
    cos_t, sin_t = _rope_tables(positions.reshape(s, 1) + token[0:1, 0:1].astype(jnp.int32))
    w_in_g = gathered(0, _after(cos_t, proj_own))
    proj = _in_proj(x, w_in_g, chip_vec + 1, N_CHIPS - 1, proj_own, "in_proj_rest")
    send_sems, recv_sems, buf_out = flights[1]
    buf_out = _gather_wait(send_sems, recv_sems, buf_out, proj, "gather_wait_w_out")
    fill_out = _flight_start("fill_start_w_out", [buf_out], _fill_plan(1), 3, chip_vec)
    attn = _attention_fwd(_after(proj, fill_out[2][0]), cos_t, sin_t, sinks)
    (cw_full,) = _flight_wait("conv_w_wait", cw_flight, _conv_w_plan(), attn)
    mixed, ac, rstd_ac = _conv_norm(proj, attn, cw_full, g_ac)
    (w_out_g,) = _flight_wait("fill_wait_w_out", fill_out, _fill_plan(1), mixed)
    w_out_full = w_out_g.reshape(d, d)
    xhat1, h1, rstd1 = _out_proj_ln(mixed, w_out_full, x, ln1_g, ln1_b)
    send_sems, recv_sems, buf_gu = flights[2]
    buf_gu = _gather_wait(send_sems, recv_sems, buf_gu, h1, "gather_wait_w_gu")
    fill_gu = _flight_start("fill_start_w_gu", [buf_gu], _fill_plan(1), 3, chip_vec)
    own = _gate_up(h1, fill_gu[2][0], chip_vec, 1, None, "gate_up_own")
    (w_gu_g,) = _flight_wait("fill_wait_w_gu", fill_gu, _fill_plan(1), own[0])
    act, ab = _gate_up(h1, w_gu_g, chip_vec + 1, N_CHIPS - 1, own, "gate_up_rest")
    w_down_full = gathered(3, act).reshape(-1, d)
    dpre2, dpre2_16, loss_part, g_ln2_g, g_ln2_b = _down_ln_loss(act, w_down_full, xhat1, ln1_g, ln1_b, ln2_g, ln2_b,
                                                                 loss_target)

    cvec = lax.axis_index("c").astype(jnp.int32).reshape(1)

    def exchange_begin(parts, nme):
        bufs = []
        for part in parts:
            ns, r, cdim = part.shape
            bufs.extend([part, lax.empty((ns, r // 2, cdim), part.dtype)])
        return _flight_start("exchange_start_" + nme, bufs, _exchange_plan(len(parts)), len(parts), cvec)

    def exchange_end(flight, n_parts, after, nme):
        bufs = _flight_wait("exchange_wait_" + nme, flight, _exchange_plan(n_parts), after)
        return [(bufs[2 * w], bufs[2 * w + 1]) for w in range(n_parts)]

    def scatter_begin(part, got, nme):
        return _scatter_start(_add_halves(part, got, cvec, "add_halves_" + nme), "scatter_start_" + nme)

    d_gu = _dact_silu_bwd(dpre2_16, w_down_full, ab)
    p_down = _grad_rows(act, dpre2_16, d_gu, "grad_w_down")
    x_down = exchange_begin([p_down], "w_down")
    (p_gu,) = _grad_cols(h1, [d_gu], x_down[2][0], "grad_w_gate_up")
    ((p_down, got),) = exchange_end(x_down, 1, p_gu, "w_down")
    f_down = scatter_begin(p_down, got, "w_down")
    x_gu = exchange_begin([_after(p_gu, f_down[2])], "w_gu")
    dpre1, g_ln1_g, g_ln1_b = _dh1_ln_bwd(d_gu, w_gu_g, dpre2, xhat1, rstd1, ln1_g, x_gu[2][0])
    ((p_gu, got),) = exchange_end(x_gu, 1, dpre1, "w_gu")
    f_gu = scatter_begin(p_gu, got, "w_gu")
    d_ac, g_g_ac = _dmixed_rms_bwd(_after(dpre1, f_gu[2]), w_out_full, ac, rstd_ac, g_ac)
    pos_vec = jnp.concatenate([chip_vec, cvec])
    sums, land = _scatter_wait(*f_down, d_ac, "scatter_wait_w_down")
    c_down = _flight_start("complete_start_w_down", [sums, land], _complete_plan(1), 4, cvec)
    p_out = _grad_rows(mixed, dpre1, c_down[2][1], "grad_w_out")
    x_out = exchange_begin([p_out], "w_out")
    sums, land = _flight_wait("complete_wait_w_down", c_down, _complete_plan(1), x_out[2][0])
    dq, dkv_cur, dkv_prev, g_sinks, *new_w_down = _attention_bwd(
        proj, d_ac, cos_t, sin_t, sinks, x_out[2][0], (w_down, m_w_down, v_w_down, land, sums, pos_vec, 0))
    ((p_out, got),) = exchange_end(x_out, 1, dq, "w_out")
    f_out = scatter_begin(p_out, got, "w_out")
    sums, land = _scatter_wait(*f_gu, f_out[2], "scatter_wait_w_gu")
    c_gu = _flight_start("complete_start_w_gu", [sums, land], _complete_plan(1), 4, cvec)
    d_proj, g_conv_w = _dproj_assemble(proj, _after(d_ac, c_gu[2][1]), dq, dkv_cur, dkv_prev, cos_t, sin_t, cw_full)
    sums_gu, land_gu = _flight_wait("complete_wait_w_gu", c_gu, _complete_plan(1), d_proj)
    p_in, *new_w_gate = _grad_cols(x, [d_proj], d_proj, "grad_w_in", a_3d=True,
                                   shard=(w_gate, m_w_gate, v_w_gate, land_gu, sums_gu, pos_vec, 0))
    x_in = exchange_begin([p_in], "w_in")
    sums, land = _scatter_wait(*f_out, x_in[2][0], "scatter_wait_w_out")
    c_out = _flight_start("complete_start_w_out", [sums, land], _complete_plan(1), 4, cvec)
    grad_x, *new_w_up = _dx(d_proj, w_in_g, dpre1, c_out[2][1], (w_up, m_w_up, v_w_up, land_gu, sums_gu, pos_vec, 1))
    red = _allreduce_small(g_ln2_g, g_ln2_b, g_ln1_g, g_ln1_b, g_g_ac, g_conv_w, g_sinks, loss_part, grad_x)
    ((p_in, got),) = exchange_end(x_in, 1, red, "w_in")
    f_in = scatter_begin(p_in, got, "w_in")

    big = {"w_down": new_w_down, "w_gate": new_w_gate, "w_up": new_w_up}
    sums, land = _flight_wait("complete_wait_w_out", c_out, _complete_plan(1), f_in[2])
    big["w_out"] = _adamw_shard(w_out, m_w_out, v_w_out, land, sums, pos_vec, "adamw_w_out")
    sums, land = _scatter_wait(*f_in, big["w_out"][0], "scatter_wait_w_in")
    (land,) = _complete_chip_sums([sums], [land])
    big["w_in"] = _adamw_shard(w_in, m_w_in, v_w_in, land, sums, pos_vec, "adamw_w_in")
    small = _adamw_small(red, {
        "sinks": (sinks, m_sinks, v_sinks), "g_attn": (g_attn, m_g_attn, v_g_attn),
        "g_conv": (g_conv, m_g_conv, v_g_conv), "ln1_g": (ln1_g, m_ln1_g, v_ln1_g),
        "ln1_b": (ln1_b, m_ln1_b, v_ln1_b), "ln2_g": (ln2_g, m_ln2_g, v_ln2_g),
        "ln2_b": (ln2_b, m_ln2_b, v_ln2_b), "conv_w": (conv_w, m_conv_w, v_conv_w)})
    res = {**big, **small}
    order = ["w_in", "conv_w", "sinks", "g_attn", "g_conv", "w_out", "ln1_g", "ln1_b", "w_gate", "w_up", "w_down",
             "ln2_g", "ln2_b"]
    loss = red[6, d // 2 + 128]
    return (loss, grad_x, *[res[n][0] for n in order], *[res[n][1] for n in order],
            *[res[n][2] for n in order], *[res[n][3] for n in order])
```

```python
import functools

import numpy as np
import jax
import jax.numpy as jnp
from jax import lax
from jax.experimental import pallas as pl
from jax.experimental.pallas import tpu as pltpu

F32 = jnp.float32
BF16 = jnp.bfloat16
MESH = pl.DeviceIdType.MESH

HEAD_DIM = 64
N_KV_HEADS = 4
GROUP = 4
WINDOW = 128
ROT_DIM = 16
ROPE_THETA = 500000.0
ATTN_SCALE = HEAD_DIM ** -0.5
ALPHA = 2.0 ** 0.25
LN_EPS = 1e-5
RMS_EPS = 1e-6
ADAM_LR = 0.001
ADAM_B1 = 0.9
ADAM_B2 = 0.999
ADAM_EPS = 1e-08
ADAM_WD = 0.01
ADAM_STEP = 10
N_CHIPS = 4
NEG_BIG = -1e30

V7X_VMEM_BYTES = 64 * 1024 * 1024
VMEM_LIMIT = V7X_VMEM_BYTES - 6 * 1024 * 1024

TM = 512
TK_TOK = 1024
TB_CONV = 256
TR_ELT = 256
ROW_CHUNK = 128
HALO_ROWS = 16


def _params(sem):
    return pltpu.CompilerParams(dimension_semantics=sem, vmem_limit_bytes=VMEM_LIMIT)


def _row_tile(rows, target):
    best = None
    for t in range(16, min(rows, target) + 1, 16):
        if rows % t == 0:
            best = t
    assert best is not None, (rows, target)
    return best


def _dot(a, b):
    return jnp.dot(a, b, preferred_element_type=F32)


def _dot_nt(a, b):
    return lax.dot_general(a, b, (((1,), (1,)), ((), ())), preferred_element_type=F32)


def _dot_tn(a, b):
    return lax.dot_general(a, b, (((0,), (0,)), ((), ())), preferred_element_type=F32)


def _mesh_pos():
    x, y, c = lax.axis_index("x"), lax.axis_index("y"), lax.axis_index("c")
    chips = [(1 - x, y), (x, 1 - y), (1 - x, 1 - y)]
    return x, y, c, chips


def _chip_id(px, py):
    return 2 * px + py


def _rope(t, cos, sgn_sin, sign):
    w = t.shape[1]
    lane = lax.broadcasted_iota(jnp.int32, t.shape, 1) & (HEAD_DIM - 1)
    partner = jnp.where(lane < ROT_DIM // 2, pltpu.roll(t, w - ROT_DIM // 2, 1), pltpu.roll(t, ROT_DIM // 2, 1))
    return t * cos + sign * (partner * sgn_sin)


def _tile_lanes(t, n):
    return jnp.concatenate([t] * n, axis=1)


def _sigmoid(g):
    return 1.0 / (1.0 + jnp.exp(-g))


def _for_row_chunks(n_rows, fn):
    def step(r, carry):
        fn(pl.ds(pl.multiple_of(r * ROW_CHUNK, ROW_CHUNK), ROW_CHUNK))
        return carry

    lax.fori_loop(0, n_rows // ROW_CHUNK, step, 0)


def _accumulate(acc, make_val, k, nk):
    if nk == 1:
        acc[...] = make_val()
        return

    @pl.when(k == 0)
    def _():
        acc[...] = jnp.zeros_like(acc)

    acc[...] += make_val()


def _ln_fwd(pre):
    mu = jnp.mean(pre, axis=-1, keepdims=True)
    cen = pre - mu
    var = jnp.mean(cen * cen, axis=-1, keepdims=True)
    rstd = lax.rsqrt(var + LN_EPS)
    return cen * rstd, rstd


def _ln_bwd(dy, xhat, rstd, g):
    dxhat = dy * g
    m1 = jnp.mean(dxhat, axis=-1, keepdims=True)
    m2 = jnp.mean(dxhat * xhat, axis=-1, keepdims=True)
    return rstd * (dxhat - m1 - xhat * m2)


def _cast_weight(w, chip_vec, after, name, col_block=0, n_col_blocks=1):
    _, r, c = w.shape
    tr = _row_tile(r, TR_ELT)

    def body(chip_ref, w_ref, after_ref, o_ref):
        o_ref[...] = w_ref[...].astype(BF16)

    grid_spec = pltpu.PrefetchScalarGridSpec(
        num_scalar_prefetch=1, grid=(r // tr,),
        in_specs=[pl.BlockSpec((None, tr, c), lambda i, chip_ref: (0, i, 0)), _ANY],
        out_specs=pl.BlockSpec((None, tr, c), lambda i, chip_ref: (chip_ref[0], i, col_block)))
    return pl.pallas_call(
        body, name=name, grid_spec=grid_spec,
        out_shape=jax.ShapeDtypeStruct((N_CHIPS, r, n_col_blocks * c), BF16),
        input_output_aliases={2: 0} if col_block else {},
        compiler_params=_params(("parallel",)),
    )(chip_vec, w, after)


_HBM = pl.BlockSpec(memory_space=pltpu.HBM)
_VMEM = pl.BlockSpec(memory_space=pltpu.VMEM)


_SEM = pl.BlockSpec(memory_space=pltpu.SEMAPHORE)
_ANY = pl.BlockSpec(memory_space=pl.ANY)
_EFFECT = pltpu.SideEffectType.DATAFLOW_SIDE_EFFECTING


def _chip_copy(buf, k, chip_of_src, half_rows, send_sems, recv_sems, to):
    part = buf.at[chip_of_src, half_rows]
    return pltpu.make_async_remote_copy(
        src_ref=part, dst_ref=part, send_sem=send_sems.at[k], recv_sem=recv_sems.at[k], device_id=to, device_id_type=MESH)


def _half_rows(buf, which):
    hr = buf.shape[1] // 2
    return pl.ds(which * hr, hr)


def _after(value, dep):
    return lax.optimization_barrier((value, dep))[0]


def _flight_start(name, bufs, plan, n_sems, after):
    n = len(bufs)

    def body(*refs):
        sends, _ = plan(refs[:n], refs[n + 1], refs[n + 2])
        for cp in sends:
            cp.start()

    outs = pl.pallas_call(
        body, name=name,
        in_specs=[_HBM] * n + [_ANY], out_specs=[_SEM, _SEM] + [_HBM] * n,
        out_shape=[pltpu.SemaphoreType.DMA((n_sems,))] * 2 + [pltpu.HBM(b.shape, b.dtype) for b in bufs],
        input_output_aliases={i: 2 + i for i in range(n)},
        compiler_params=pltpu.CompilerParams(has_side_effects=_EFFECT),
    )(*[pltpu.with_memory_space_constraint(b, pltpu.HBM) for b in bufs], after)
    return outs[0], outs[1], list(outs[2:])


def _flight_wait(name, flight, plan, after):
    send_sems, recv_sems, bufs = flight
    n = len(bufs)

    def body(*refs):
        sends, recvs = plan(refs[:n], refs[n], refs[n + 1])
        for cp in sends:
            cp.wait_send()
        for cp in recvs:
            cp.wait_recv()

    outs = pl.pallas_call(
        body, name=name,
        in_specs=[_HBM] * n + [_SEM, _SEM, _ANY], out_specs=[_HBM] * n,
        out_shape=[pltpu.HBM(b.shape, b.dtype) for b in bufs],
        input_output_aliases={i: i for i in range(n)},
        compiler_params=pltpu.CompilerParams(has_side_effects=_EFFECT),
    )(*bufs, send_sems, recv_sems, after)
    return list(outs)


def _fill_plan(n_bufs):
    def plan(refs, send_sems, recv_sems):
        x, y, c, chips = _mesh_pos()
        sibling = (x, y, 1 - c)
        sends, recvs = [], []
        for w in range(n_bufs):
            for k, chip in enumerate(chips):
                slot = _chip_id(*chip)
                sends.append(_chip_copy(refs[w], 3 * w + k, slot, _half_rows(refs[w], c), send_sems, recv_sems, sibling))
                recvs.append(_chip_copy(refs[w], 3 * w + k, slot, _half_rows(refs[w], 1 - c), send_sems, recv_sems,
                                        sibling))
        return sends, recvs
    return plan


def _conv_w_plan():
    def plan(refs, send_sems, recv_sems):
        x, y, c, chips = _mesh_pos()
        me = _chip_id(x, y)
        (buf,) = refs
        sends, recvs = [], []
        for k, chip in enumerate(chips):
            for slot, into in ((me, sends), (_chip_id(*chip), recvs)):
                into.append(pltpu.make_async_remote_copy(
                    src_ref=buf.at[slot], dst_ref=buf.at[slot], send_sem=send_sems.at[k], recv_sem=recv_sems.at[k],
                    device_id=(*chip, c), device_id_type=MESH))
        return sends, recvs
    return plan


def _exchange_plan(n_parts):
    def plan(refs, send_sems, recv_sems):
        x, y, c, _ = _mesh_pos()
        copies = []
        for w in range(n_parts):
            part, got = refs[2 * w], refs[2 * w + 1]
            hr = got.shape[1]
            copies.append(pltpu.make_async_remote_copy(
                src_ref=part.at[:, pl.ds((1 - c) * hr, hr)], dst_ref=got, send_sem=send_sems.at[w],
                recv_sem=recv_sems.at[w], device_id=(x, y, 1 - c), device_id_type=MESH))
        return copies, copies
    return plan


def _gather_start(bufs, after, name):
    n = len(bufs)

    def body(*refs):
        ins = refs[:n]
        sends, recvs = refs[n + 1:2 * n + 1], refs[2 * n + 1:3 * n + 1]
        token = refs[4 * n + 1]
        x, y, c, chips = _mesh_pos()
        me = _chip_id(x, y)
        for w in range(n):
            for k, chip in enumerate(chips):
                _chip_copy(ins[w], k, me, _half_rows(ins[w], c), sends[w], recvs[w], (*chip, c)).start()
        token[...] = jnp.zeros_like(token)

    outs = pl.pallas_call(
        body, name=name,
        in_specs=[_HBM] * n + [_ANY],
        out_specs=[_SEM] * (2 * n) + [_HBM] * n + [_VMEM],
        out_shape=[pltpu.SemaphoreType.DMA((3,))] * (2 * n) + [pltpu.HBM(b.shape, b.dtype) for b in bufs]
        + [jax.ShapeDtypeStruct((8, 128), F32)],
        input_output_aliases={w: 2 * n + w for w in range(n)},
        compiler_params=pltpu.CompilerParams(has_side_effects=_EFFECT),
    )(*[pltpu.with_memory_space_constraint(b, pltpu.HBM) for b in bufs], after)
    return [(outs[w], outs[n + w], outs[2 * n + w]) for w in range(n)], outs[3 * n]


def _gather_wait(send_sems, recv_sems, buf, after, name):
    def body(buf_ref, send_ref, recv_ref, after_ref, out_ref):
        x, y, c, chips = _mesh_pos()
        me = _chip_id(x, y)
        for k, chip in enumerate(chips):
            _chip_copy(buf_ref, k, me, _half_rows(buf_ref, c), send_ref, recv_ref, (*chip, c)).wait_send()
        for k, chip in enumerate(chips):
            _chip_copy(buf_ref, k, _chip_id(*chip), _half_rows(buf_ref, c), send_ref, recv_ref, (*chip, c)).wait_recv()

    return pl.pallas_call(
        body, name=name,
        in_specs=[_HBM, _SEM, _SEM, _ANY], out_specs=_HBM,
        out_shape=pltpu.HBM(buf.shape, buf.dtype),
        input_output_aliases={0: 0},
        compiler_params=pltpu.CompilerParams(has_side_effects=_EFFECT),
    )(buf, send_sems, recv_sems, after)


def _sibling_fill(buf, name, own_too=False):
    n_copies = 4 if own_too else 3

    def body(buf_ref, out_ref, send_sems, recv_sems):
        x, y, c, chips = _mesh_pos()
        sibling = (x, y, 1 - c)
        slots = [_chip_id(*chip) for chip in chips] + ([_chip_id(x, y)] if own_too else [])
        copies = []
        for k, slot in enumerate(slots):
            cp = _chip_copy(out_ref, k, slot, _half_rows(out_ref, c), send_sems, recv_sems, sibling)
            cp.start()
            copies.append(cp)
        for k, slot in enumerate(slots):
            _chip_copy(out_ref, k, slot, _half_rows(out_ref, 1 - c), send_sems, recv_sems, sibling).wait_recv()
        for cp in copies:
            cp.wait_send()

    return pl.pallas_call(
        body, name=name,
        in_specs=[_HBM], out_specs=_HBM,
        out_shape=jax.ShapeDtypeStruct(buf.shape, buf.dtype),
        input_output_aliases={0: 0},
        scratch_shapes=[pltpu.SemaphoreType.DMA((n_copies,)), pltpu.SemaphoreType.DMA((n_copies,))],
    )(buf)


def _allgather_conv_w(cw):
    _, kw, cs = cw.shape

    def body(cw_ref, out_ref, send_sems, recv_sems):
        x, y, c, chips = _mesh_pos()
        me = _chip_id(x, y)
        out_ref[pl.ds(me, 1)] = cw_ref[...]
        copies = []
        for k, chip in enumerate(chips):
            cp = pltpu.make_async_remote_copy(
                src_ref=cw_ref.at[0], dst_ref=out_ref.at[me], send_sem=send_sems.at[k], recv_sem=recv_sems.at[k],
                device_id=(*chip, c), device_id_type=MESH)
            cp.start()
            copies.append(cp)
        for k, chip in enumerate(chips):
            pltpu.make_async_remote_copy(
                src_ref=cw_ref.at[0], dst_ref=out_ref.at[_chip_id(*chip)], send_sem=send_sems.at[k],
                recv_sem=recv_sems.at[k], device_id=(*chip, c), device_id_type=MESH).wait_recv()
        for cp in copies:
            cp.wait_send()

    return pl.pallas_call(
        body, name="allgather_conv_w",
        in_specs=[_VMEM], out_specs=_VMEM,
        out_shape=jax.ShapeDtypeStruct((N_CHIPS, kw, cs), F32),
        scratch_shapes=[pltpu.SemaphoreType.DMA((3,)), pltpu.SemaphoreType.DMA((3,))],
    )(cw)


def _exchange_halves(parts, after, name):
    n = len(parts)
    shapes = [p.shape for p in parts]

    def body(*refs):
        ins, outs = refs[:n], refs[n + 1:2 * n + 1]
        send_sems, recv_sems = refs[2 * n + 1:]
        x, y, c, _ = _mesh_pos()
        copies = []
        for w in range(n):
            hr = shapes[w][1] // 2
            cp = pltpu.make_async_remote_copy(
                src_ref=ins[w].at[:, pl.ds((1 - c) * hr, hr)], dst_ref=outs[w],
                send_sem=send_sems.at[w], recv_sem=recv_sems.at[w],
                device_id=(x, y, 1 - c), device_id_type=MESH)
            cp.start()
            copies.append(cp)
        for cp in copies:
            cp.wait()

    return pl.pallas_call(
        body, name=name,
        in_specs=[_HBM] * n + [_ANY], out_specs=[_HBM] * n,
        out_shape=[jax.ShapeDtypeStruct((s[0], s[1] // 2, s[2]), BF16) for s in shapes],
        scratch_shapes=[pltpu.SemaphoreType.DMA((n,)), pltpu.SemaphoreType.DMA((n,))],
    )(*parts, after)


def _add_halves(part, got, cvec, name):
    ns, r, cdim = part.shape
    hr = r // 2
    tr = _row_tile(hr, TR_ELT)
    nblk = hr // tr

    def body(c_ref, a_ref, b_ref, o_ref):
        o_ref[...] = (a_ref[...].astype(F32) + b_ref[...].astype(F32)).astype(BF16)

    grid_spec = pltpu.PrefetchScalarGridSpec(
        num_scalar_prefetch=1, grid=(ns, nblk),
        in_specs=[pl.BlockSpec((None, tr, cdim), lambda s, i, c_ref: (s, c_ref[0] * nblk + i, 0)),
                  pl.BlockSpec((None, tr, cdim), lambda s, i, c_ref: (s, i, 0))],
        out_specs=pl.BlockSpec((None, tr, cdim), lambda s, i, c_ref: (s, i, 0)))
    return pl.pallas_call(
        body, name=name, grid_spec=grid_spec,
        out_shape=jax.ShapeDtypeStruct((ns, hr, cdim), BF16),
        compiler_params=_params(("parallel", "parallel")),
    )(cvec, part, got)


def _scatter_copy(sums_ref, land_ref, k, src_slot, dst_slot, c, send_sems, recv_sems, to):
    return pltpu.make_async_remote_copy(
        src_ref=sums_ref.at[src_slot], dst_ref=land_ref.at[dst_slot, _half_rows(land_ref, c)],
        send_sem=send_sems.at[k], recv_sem=recv_sems.at[k], device_id=to, device_id_type=MESH)


def _scatter_start(sums, name):
    ns, hr, cdim = sums.shape
    land = lax.empty((ns, 2 * hr, cdim), sums.dtype)

    def body(sums_ref, land_ref, send_sems, recv_sems, sums_thru, land_thru):
        x, y, c, chips = _mesh_pos()
        me = _chip_id(x, y)
        for k, chip in enumerate(chips):
            _scatter_copy(sums_ref, land_ref, k, _chip_id(*chip), me, c, send_sems, recv_sems, (*chip, c)).start()

    return pl.pallas_call(
        body, name=name,
        in_specs=[_HBM, _HBM], out_specs=[_SEM, _SEM, _HBM, _HBM],
        out_shape=[pltpu.SemaphoreType.DMA((3,)), pltpu.SemaphoreType.DMA((3,)),
                   pltpu.HBM(sums.shape, sums.dtype), pltpu.HBM(land.shape, land.dtype)],
        input_output_aliases={0: 2, 1: 3},
        compiler_params=pltpu.CompilerParams(has_side_effects=_EFFECT),
    )(pltpu.with_memory_space_constraint(sums, pltpu.HBM), pltpu.with_memory_space_constraint(land, pltpu.HBM))


def _scatter_wait(send_sems, recv_sems, sums, land, after, name):
    def body(sums_ref, land_ref, send_ref, recv_ref, after_ref, sums_out, land_out):
        x, y, c, chips = _mesh_pos()
        me = _chip_id(x, y)
        for k, chip in enumerate(chips):
            _scatter_copy(sums_ref, land_ref, k, _chip_id(*chip), me, c, send_ref, recv_ref, (*chip, c)).wait_send()
        for k, chip in enumerate(chips):
            _scatter_copy(sums_ref, land_ref, k, me, _chip_id(*chip), c, send_ref, recv_ref, (*chip, c)).wait_recv()

    return pl.pallas_call(
        body, name=name,
        in_specs=[_HBM, _HBM, _SEM, _SEM, _ANY], out_specs=[_HBM, _HBM],
        out_shape=[pltpu.HBM(sums.shape, sums.dtype), pltpu.HBM(land.shape, land.dtype)],
        input_output_aliases={0: 0, 1: 1},
        compiler_params=pltpu.CompilerParams(has_side_effects=_EFFECT),
    )(sums, land, send_sems, recv_sems, after)


def _complete_plan(n_weights):
    def plan(refs, send_sems, recv_sems):
        x, y, c, chips = _mesh_pos()
        me = _chip_id(x, y)
        sibling = (x, y, 1 - c)
        sends, recvs = [], []
        for w in range(n_weights):
            sums, land = refs[2 * w], refs[2 * w + 1]
            sends.append(_scatter_copy(sums, land, 4 * w + 3, me, me, c, send_sems, recv_sems, sibling))
            recvs.append(_scatter_copy(sums, land, 4 * w + 3, me, me, 1 - c, send_sems, recv_sems, sibling))
            for k, chip in enumerate(chips):
                slot = _chip_id(*chip)
                sends.append(_chip_copy(land, 4 * w + k, slot, _half_rows(land, c), send_sems, recv_sems, sibling))
                recvs.append(_chip_copy(land, 4 * w + k, slot, _half_rows(land, 1 - c), send_sems, recv_sems, sibling))
        return sends, recvs
    return plan


def _complete_chip_sums(sums, lands):
    n = len(sums)

    def body(*refs):
        sums_refs, outs = refs[:n], refs[2 * n:3 * n]
        send_sems, recv_sems = refs[3 * n:]
        x, y, c, chips = _mesh_pos()
        me = _chip_id(x, y)
        sibling = (x, y, 1 - c)
        slots = [_chip_id(*chip) for chip in chips]
        sent = []
        for w in range(n):
            out = outs[w]
            cp = _scatter_copy(sums_refs[w], out, 3, me, me, c, send_sems.at[w], recv_sems.at[w], sibling)
            cp.start()
            sent.append(cp)
            for k, slot in enumerate(slots):
                cp = _chip_copy(out, k, slot, _half_rows(out, c), send_sems.at[w], recv_sems.at[w], sibling)
                cp.start()
                sent.append(cp)
        for w in range(n):
            out = outs[w]
            _scatter_copy(sums_refs[w], out, 3, me, me, 1 - c, send_sems.at[w], recv_sems.at[w], sibling).wait_recv()
            for k, slot in enumerate(slots):
                _chip_copy(out, k, slot, _half_rows(out, 1 - c), send_sems.at[w], recv_sems.at[w], sibling).wait_recv()
        for cp in sent:
            cp.wait_send()

    return pl.pallas_call(
        body, name="complete_chip_sums",
        in_specs=[_HBM] * (2 * n), out_specs=[_HBM] * n,
        out_shape=[jax.ShapeDtypeStruct(b.shape, b.dtype) for b in lands],
        input_output_aliases={n + w: w for w in range(n)},
        scratch_shapes=[pltpu.SemaphoreType.DMA((n, 4)), pltpu.SemaphoreType.DMA((n, 4))],
    )(*sums, *lands)


SMALL_ROWS = 8


def _allreduce_small(gl2g, gl2b, gl1g, gl1b, g_ac, gcw, gsink, loss, after):
    d = gl2g.shape[1]
    hd = d // 2
    nq = gsink.shape[1]

    def body(a_ref, b_ref, c_ref, d_ref, e_ref, cw_ref, sk_ref, ls_ref, after_ref, out_ref, mine, gath, send_sems,
             recv_sems):
        x, y, c, _ = _mesh_pos()
        me = 4 * x + 2 * y + c
        mine[...] = jnp.zeros_like(mine)
        mine[0:1, :] = a_ref[...]
        mine[1:2, :] = b_ref[...]
        mine[2:3, :] = c_ref[...]
        mine[3:4, :] = d_ref[...]
        mine[4:5, :] = e_ref[...]
        mine[5:6, 0:hd] = cw_ref[0:1, :]
        mine[5:6, hd:d] = cw_ref[1:2, :]
        mine[6:7, 0:hd] = cw_ref[2:3, :]
        mine[6:7, hd:hd + nq] = sk_ref[...]
        mine[6:7, hd + 128:hd + 256] = ls_ref[...]
        gath[pl.ds(me, 1)] = mine[...][None]
        copies = []
        for r in range(1, 8):
            peer = ((1 - x) if r & 4 else x, (1 - y) if r & 2 else y, (1 - c) if r & 1 else c)
            cp = pltpu.make_async_remote_copy(
                src_ref=mine, dst_ref=gath.at[me], send_sem=send_sems.at[r - 1], recv_sem=recv_sems.at[r - 1],
                device_id=peer, device_id_type=MESH)
            cp.start()
            copies.append(cp)
        for r in range(1, 8):
            peer = ((1 - x) if r & 4 else x, (1 - y) if r & 2 else y, (1 - c) if r & 1 else c)
            peer_id = 4 * peer[0] + 2 * peer[1] + peer[2]
            pltpu.make_async_remote_copy(
                src_ref=mine, dst_ref=gath.at[peer_id], send_sem=send_sems.at[r - 1], recv_sem=recv_sems.at[r - 1],
                device_id=peer, device_id_type=MESH).wait_recv()
        for cp in copies:
            cp.wait_send()
        total = gath[0]
        for dev in range(1, 8):
            total = total + gath[dev]
        out_ref[...] = total

    return pl.pallas_call(
        body, name="allreduce_small",
        in_specs=[_VMEM] * 8 + [_ANY], out_specs=_VMEM,
        out_shape=jax.ShapeDtypeStruct((SMALL_ROWS, d), F32),
        scratch_shapes=[pltpu.VMEM((SMALL_ROWS, d), F32), pltpu.VMEM((8, SMALL_ROWS, d), F32),
                        pltpu.SemaphoreType.DMA((7,)), pltpu.SemaphoreType.DMA((7,))],
    )(gl2g, gl2b, gl1g, gl1b, g_ac, gcw, gsink, loss, after)


def _adamw(w, g, m, v):
    m = ADAM_B1 * m + (1.0 - ADAM_B1) * g
    v = ADAM_B2 * v + (1.0 - ADAM_B2) * (g * g)
    m_hat = m / (1.0 - ADAM_B1 ** ADAM_STEP)
    v_hat = v / (1.0 - ADAM_B2 ** ADAM_STEP)
    delta = -ADAM_LR * (m_hat / (jnp.sqrt(v_hat) + ADAM_EPS) + ADAM_WD * w)
    return delta, m, v


def _adamw_shard(w, m, v, land, own, pos_vec, name, col_block=0):
    tr = _row_tile(w.shape[1] // 2, TR_ELT)
    grid = (w.shape[1] // tr,)
    body, in_specs, out_specs, out_shape = _adamw_passenger(w.shape, tr, grid, col_block)
    grid_spec = pltpu.PrefetchScalarGridSpec(num_scalar_prefetch=1, grid=grid, in_specs=in_specs, out_specs=out_specs)
    return pl.pallas_call(
        body, name=name, grid_spec=grid_spec, out_shape=out_shape,
        compiler_params=_params(("parallel",)),
    )(pos_vec, w, m, v, land, land, land, land, own)


def _adamw_passenger(shape, tr, grid, col_block):
    _, r, c = shape
    nh = r // 2 // tr
    n_blocks = 2 * nh
    n_steps = int(np.prod(grid))
    assert nh * tr * 2 == r and n_blocks <= n_steps

    def step_of(ids):
        step = ids[0]
        for n, i in zip(grid[1:], ids[1:]):
            step = step * n + i
        return step

    def block_of(ids):
        return jnp.minimum(step_of(ids), n_blocks - 1)

    def update(pos_ref, w_ref, m_ref, v_ref, l0, l1, l2, l3, own_ref, g_out, d_out, m_out, v_out):
        i = block_of([pl.program_id(a) for a in range(len(grid))])
        mine = (i // nh) == pos_ref[1]
        own_blk = own_ref[...].astype(F32)
        g = None
        for s, l_ref in enumerate([l0, l1, l2, l3]):
            term = jnp.where(mine & (pos_ref[0] == s), own_blk, l_ref[...].astype(F32))
            g = term if g is None else g + term
        delta, nm, nv = _adamw(w_ref[...], g, m_ref[...], v_ref[...])
        g_out[...] = g
        d_out[...] = delta
        m_out[...] = nm
        v_out[...] = nv

    def body(*refs):
        if n_blocks == n_steps:
            update(*refs)
        else:
            pl.when(step_of([pl.program_id(a) for a in range(len(grid))]) < n_blocks)(lambda: update(*refs))

    def land_spec(s):
        def index(*args):
            i, pos_ref = block_of(args[:-1]), args[-1]
            skip = (pos_ref[0] == s) & ((i // nh) == pos_ref[1])
            return (s, jnp.where(skip, (i + nh) % n_blocks, i), col_block)
        return pl.BlockSpec((None, tr, c), index)

    blk = pl.BlockSpec((None, tr, c), lambda *args: (0, block_of(args[:-1]), 0))
    in_specs = ([blk, blk, blk] + [land_spec(s) for s in range(N_CHIPS)]
                + [pl.BlockSpec((None, tr, c), lambda *args: (args[-1][0], block_of(args[:-1]) % nh, col_block))])
    return body, in_specs, [blk] * 4, [jax.ShapeDtypeStruct((1, r, c), F32)] * 4


def _call_with_adamw(body, name, grid, in_specs, out_specs, out_shape, scratch_shapes, semantics, operands, shard):
    if shard is None:
        return pl.pallas_call(
            body, name=name, grid=grid, in_specs=in_specs, out_specs=out_specs, out_shape=out_shape,
            scratch_shapes=scratch_shapes, compiler_params=_params(semantics))(*operands)
    w, m, v, land, own, pos_vec, col_block = shard
    n_steps = int(np.prod(grid))
    hr = w.shape[1] // 2
    tr = min(t for t in range(16, hr + 1, 16) if hr % t == 0 and 2 * (hr // t) <= n_steps)
    adam_body, adam_in, adam_out, adam_shape = _adamw_passenger(w.shape, tr, grid, col_block)
    n_in, n_out = len(in_specs), len(out_specs)

    def with_pos(spec):
        if spec.index_map is None:
            return spec
        return pl.BlockSpec(spec.block_shape, lambda *args: spec.index_map(*args[:-1]))

    def both(pos_ref, *refs):
        ins, adam_ins = refs[:n_in], refs[n_in:n_in + len(adam_in)]
        refs = refs[n_in + len(adam_in):]
        outs, adam_outs, scratch = refs[:n_out], refs[n_out:n_out + len(adam_out)], refs[n_out + len(adam_out):]
        body(*ins, *outs, *scratch)
        adam_body(pos_ref, *adam_ins, *adam_outs)

    grid_spec = pltpu.PrefetchScalarGridSpec(
        num_scalar_prefetch=1, grid=grid, in_specs=[with_pos(sp) for sp in in_specs] + adam_in,
        out_specs=[with_pos(sp) for sp in out_specs] + adam_out, scratch_shapes=scratch_shapes)
    return pl.pallas_call(
        both, name=name, grid_spec=grid_spec, out_shape=list(out_shape) + adam_shape,
        compiler_params=_params(semantics),
    )(pos_vec, *operands, w, m, v, land, land, land, land, own)


def _adamw_small(red, params):
    names = ["sinks", "g_attn", "g_conv", "ln1_g", "ln1_b", "ln2_g", "ln2_b", "conv_w"]
    d = red.shape[1]
    hd = d // 2
    flat = []
    for nme in names:
        flat.extend(params[nme])
    nq = params["sinks"][0].shape[1]
    cs = params["conv_w"][0].shape[2]

    def body(*refs):
        red_ref = refs[0]
        ins = refs[1:1 + 3 * len(names)]
        outs = refs[1 + 3 * len(names):]
        x, y, _, _ = _mesh_pos()
        me = _chip_id(x, y)

        def conv_tap(row, base):
            picked = red_ref[row:row + 1, base:base + cs]
            for s in range(1, N_CHIPS):
                picked = jnp.where(me == s, red_ref[row:row + 1, base + s * cs:base + (s + 1) * cs], picked)
            return picked

        grads = {
            "sinks": red_ref[6:7, hd:hd + nq],
            "g_attn": red_ref[4:5, 0:hd],
            "g_conv": red_ref[4:5, hd:d],
            "ln1_g": red_ref[2:3, :],
            "ln1_b": red_ref[3:4, :],
            "ln2_g": red_ref[0:1, :],
            "ln2_b": red_ref[1:2, :],
        }
        for i, nme in enumerate(names):
            w_ref, m_ref, v_ref = ins[3 * i:3 * i + 3]
            g_out, d_out, m_out, v_out = outs[4 * i:4 * i + 4]
            if nme == "conv_w":
                for tap, (row, base) in enumerate([(5, 0), (5, hd), (6, 0)]):
                    g = conv_tap(row, base)
                    delta, nm, nv = _adamw(w_ref[0, tap:tap + 1, :], g, m_ref[0, tap:tap + 1, :], v_ref[0, tap:tap + 1, :])
                    g_out[0, tap:tap + 1, :] = g
                    d_out[0, tap:tap + 1, :] = delta
                    m_out[0, tap:tap + 1, :] = nm
                    v_out[0, tap:tap + 1, :] = nv
            else:
                g = grads[nme]
                delta, nm, nv = _adamw(w_ref[...], g, m_ref[...], v_ref[...])
                g_out[...] = g
                d_out[...] = delta
                m_out[...] = nm
                v_out[...] = nv

    out_shape = []
    for nme in names:
        out_shape.extend([jax.ShapeDtypeStruct(params[nme][0].shape, F32)] * 4)
    outs = pl.pallas_call(
        body, name="adamw_small",
        in_specs=[_VMEM] * (1 + len(flat)), out_specs=[_VMEM] * len(out_shape),
        out_shape=out_shape,
    )(red, *flat)
    return {nme: tuple(outs[4 * i:4 * i + 4]) for i, nme in enumerate(names)}


def _rope_tables(pos_col):
    s = pos_col.shape[0]
    w = N_KV_HEADS * HEAD_DIM
    tb = min(512, s)
    inv_freq = (ROPE_THETA ** (-np.arange(0, ROT_DIM, 2, dtype=np.float32) / ROT_DIM)).astype(np.float32)

    def body(pos_ref, cos_ref, sin_ref):
        pos = pos_ref[...].astype(F32)
        lane = lax.broadcasted_iota(jnp.int32, (tb, PAIR), 1) & (HEAD_DIM - 1)
        fidx = lane & (ROT_DIM // 2 - 1)
        inv = jnp.zeros((tb, PAIR), F32)
        for k in range(ROT_DIM // 2):
            inv = jnp.where(fidx == k, float(inv_freq[k]), inv)
        ang = pos * inv
        rot = lane < ROT_DIM
        sin_v = jnp.sin(ang)
        cos_ref[...] = _tile_lanes(jnp.where(rot, jnp.cos(ang), 1.0), w // PAIR)
        sin_ref[...] = _tile_lanes(jnp.where(lane < ROT_DIM // 2, -sin_v, jnp.where(rot, sin_v, 0.0)), w // PAIR)

    return pl.pallas_call(
        body, name="rope_tables", grid=(s // tb,),
        in_specs=[pl.BlockSpec((tb, 1), lambda i: (i, 0))],
        out_specs=[pl.BlockSpec((tb, w), lambda i: (i, 0))] * 2,
        out_shape=[jax.ShapeDtypeStruct((s, w), F32)] * 2,
        compiler_params=_params(("parallel",)),
    )(pos_col)


def _in_proj(x, w_in_g, first_vec, n_shards, into, name):
    _, s, d = x.shape
    ns, _, ncol = w_in_g.shape
    tm = min(2 * TM, s)

    def body(first_ref, x_ref, w_ref, into_ref, o_ref):
        o_ref[...] = _dot(x_ref[...].astype(BF16), w_ref[...]).astype(BF16)

    shard = lambda j, first_ref: lax.rem(first_ref[0] + j, ns)
    grid_spec = pltpu.PrefetchScalarGridSpec(
        num_scalar_prefetch=1, grid=(s // tm, n_shards),
        in_specs=[pl.BlockSpec((None, tm, d), lambda i, j, first_ref: (0, i, 0)),
                  pl.BlockSpec((None, d, ncol), lambda i, j, first_ref: (shard(j, first_ref), 0, 0)), _ANY],
        out_specs=pl.BlockSpec((tm, ncol), lambda i, j, first_ref: (i, shard(j, first_ref))))
    return pl.pallas_call(
        body, name=name, grid_spec=grid_spec,
        out_shape=jax.ShapeDtypeStruct((s, ns * ncol), BF16),
        input_output_aliases={} if into is None else {3: 0},
        compiler_params=_params(("parallel", "arbitrary")),
    )(first_vec, x, w_in_g, first_vec if into is None else into)


PAIR = 2 * HEAD_DIM
KEYS = 2 * WINDOW


def _pair_operand(t_all, h):
    col = (h // 2) * PAIR
    lane = lax.broadcasted_iota(jnp.int32, (KEYS, PAIR), 1)
    own_low = h % 2 == 0
    mine = jnp.where((lane < HEAD_DIM) if own_low else (lane >= HEAD_DIM), t_all[:, col:col + PAIR], 0.0)
    other = pltpu.roll(mine, HEAD_DIM, 1)
    low, high = (mine, other) if own_low else (other, mine)
    return jnp.concatenate([low, high], axis=0).astype(BF16)


def _pair_grad(acc, h):
    lane = lax.broadcasted_iota(jnp.int32, (KEYS, PAIR), 1)
    low = jnp.where(lane < HEAD_DIM, acc[:KEYS], 0.0)
    high = jnp.where(lane >= HEAD_DIM, acc[KEYS:], 0.0)
    if h % 2 == 0:
        return low + pltpu.roll(high, HEAD_DIM, 1)
    return high + pltpu.roll(low, HEAD_DIM, 1)


N_PAIRS = N_KV_HEADS * GROUP // 2


def _all_probs(q, kk2s, first, sinks_ref):
    assert ATTN_SCALE == 0.125
    q = q * ATTN_SCALE
    qps, scores = [], []
    for pair in range(N_PAIRS):
        qp = q[:, pair * PAIR:(pair + 1) * PAIR].astype(BF16)
        qps.append(qp)
        scores.append(_dot_nt(qp, kk2s[pair // (GROUP // 2)]))
    qi = lax.broadcasted_iota(jnp.int32, (WINDOW, 2 * KEYS), 0)
    kj = lax.broadcasted_iota(jnp.int32, (WINDOW, 2 * KEYS), 1) & (KEYS - 1)
    rel = qi + WINDOW - kj
    valid = (rel >= 0) & (rel < WINDOW) & jnp.logical_not(first & (kj < WINDOW))
    bias = jnp.where(valid, 0.0, NEG_BIG)
    s = (jnp.stack(scores, axis=0) + bias[None]).reshape(N_PAIRS * WINDOW, 2 * KEYS)
    probs, p_sinks = [], []
    for t in range(2):
        st = s[:, t * KEYS:(t + 1) * KEYS]
        sink = jnp.concatenate([jnp.broadcast_to(sinks_ref[0:1, 2 * pair + t:2 * pair + t + 1], (WINDOW, 1))
                                for pair in range(N_PAIRS)], axis=0)
        m = jnp.maximum(jnp.max(st, axis=1, keepdims=True), sink)
        e = jnp.exp(st - m)
        e_sink = jnp.exp(sink - m)
        inv_l = 1.0 / (jnp.sum(e, axis=1, keepdims=True) + e_sink)
        probs.append(e * inv_l)
        p_sinks.append(e_sink * inv_l)
    return qps, jnp.concatenate(probs, axis=1), p_sinks


def _roped_qkv(cur_ref, prev_ref, cos_ref, sin_ref, cosp_ref, sinp_ref, qw, kvw):
    cur = cur_ref[...].astype(F32)
    cos, sin = cos_ref[...], sin_ref[...]
    cos_q, sin_q = _tile_lanes(cos, GROUP), _tile_lanes(sin, GROUP)
    q = _rope(cur[:, :qw], cos_q, sin_q, 1.0)
    prev = prev_ref[...].astype(F32)
    k_all = jnp.concatenate([_rope(prev[:, :kvw], cosp_ref[...], sinp_ref[...], 1.0),
                             _rope(cur[:, qw:qw + kvw], cos, sin, 1.0)], axis=0)
    v_all = jnp.concatenate([prev[:, kvw:], cur[:, qw + kvw:]], axis=0)
    return q, k_all, v_all, cos_q, sin_q


def _attention_fwd(proj, cos_t, sin_t, sinks):
    s = proj.shape[0]
    qw = GROUP * N_KV_HEADS * HEAD_DIM
    kvw = N_KV_HEADS * HEAD_DIM
    nb = s // WINDOW

    def body(cur_ref, prev_ref, cos_ref, sin_ref, cosp_ref, sinp_ref, sinks_ref, o_ref):
        first = pl.program_id(0) == 0
        q, k_all, v_all, _, _ = _roped_qkv(cur_ref, prev_ref, cos_ref, sin_ref, cosp_ref, sinp_ref, qw, kvw)
        kk2s = [_pair_operand(k_all, h) for h in range(N_KV_HEADS)]
        vv2s = [_pair_operand(v_all, h) for h in range(N_KV_HEADS)]
        _, probs, _ = _all_probs(q, kk2s, first, sinks_ref)
        probs = probs.astype(BF16)
        outs = [_dot(probs[pair * WINDOW:(pair + 1) * WINDOW], vv2s[pair // (GROUP // 2)]) for pair in range(N_PAIRS)]
        o_ref[...] = jnp.concatenate(outs, axis=1)

    tbl = pl.BlockSpec((WINDOW, kvw), lambda n: (n, 0))
    tbl_prev = pl.BlockSpec((WINDOW, kvw), lambda n: (jnp.maximum(n - 1, 0), 0))
    return pl.pallas_call(
        body, name="attention_fwd", grid=(nb,),
        in_specs=[pl.BlockSpec((WINDOW, qw + 2 * kvw), lambda n: (n, 0)),
                  pl.BlockSpec((WINDOW, 2 * kvw), lambda n: (jnp.maximum(n - 1, 0), (qw // (2 * kvw)))),
                  tbl, tbl, tbl_prev, tbl_prev, _VMEM],
        out_specs=pl.BlockSpec((WINDOW, qw), lambda n: (n, 0)),
        out_shape=jax.ShapeDtypeStruct((s, qw), F32),
        compiler_params=_params(("parallel",)),
    )(proj, proj, cos_t, sin_t, cos_t, sin_t, sinks)


def _conv_taps(cw_ref):
    return [jnp.concatenate([cw_ref[s, k:k + 1, :] for s in range(N_CHIPS)], axis=1) for k in range(3)]


def _shift_down(z, halo, steps):
    last = halo.shape[0]
    row = lax.broadcasted_iota(jnp.int32, z.shape, 0)
    out = pltpu.roll(z, steps, 0)
    for r in range(steps):
        out = jnp.where(row == r, halo[last - steps + r:last - steps + r + 1, :], out)
    return out


def _shift_up(z, halo, steps):
    rows = z.shape[0]
    row = lax.broadcasted_iota(jnp.int32, z.shape, 0)
    out = pltpu.roll(z, rows - steps, 0)
    for r in range(steps):
        out = jnp.where(row == rows - steps + r, halo[r:r + 1, :], out)
    return out


def _split_cbu(lo, hi, cw):
    lo, hi = lo.astype(F32), hi.astype(F32)
    c_gate = lo[:, :cw]
    b_gate = jnp.concatenate([lo[:, cw:], hi[:, :2 * cw - lo.shape[1]]], axis=1)
    u = hi[:, 2 * cw - lo.shape[1]:]
    return c_gate, b_gate, u


def _conv_norm(proj, attn, cw_full, g_ac):
    s, in_w = proj.shape
    cw = attn.shape[1]
    blk_w = in_w // 3
    tb = min(TB_CONV, s)

    def body(lo_ref, hi_ref, lo_h_ref, hi_h_ref, attn_ref, cw_ref, g_ref, mixed_ref, ac_ref, rstd_ref):
        i = pl.program_id(0)
        c_gate, b_gate, u = _split_cbu(lo_ref[...], hi_ref[...], cw)
        c_h, _, u_h = _split_cbu(lo_h_ref[...], hi_h_ref[...], cw)
        z = c_gate * u
        z_h = jnp.where(i == 0, 0.0, c_h * u_h)
        w0, w1, w2 = _conv_taps(cw_ref)
        y = w0 * _shift_down(z, z_h, 2) + w1 * _shift_down(z, z_h, 1) + w2 * z
        conv = b_gate * y
        a = attn_ref[...]
        r_a = lax.rsqrt(jnp.mean(a * a, axis=-1, keepdims=True) + RMS_EPS)
        r_c = lax.rsqrt(jnp.mean(conv * conv, axis=-1, keepdims=True) + RMS_EPS)
        g = g_ref[...]
        mixed_ref[...] = jnp.concatenate([a * r_a * g[:, :cw], conv * r_c * g[:, cw:]], axis=1).astype(BF16)
        ac_ref[...] = jnp.concatenate([a, conv], axis=1)
        rstd_ref[0] = r_a
        rstd_ref[1] = r_c

    halo_idx = lambda i: jnp.maximum(i * (tb // HALO_ROWS) - 1, 0)
    return pl.pallas_call(
        body, name="conv_norm", grid=(s // tb,),
        in_specs=[pl.BlockSpec((tb, blk_w), lambda i: (i, 1)),
                  pl.BlockSpec((tb, blk_w), lambda i: (i, 2)),
                  pl.BlockSpec((HALO_ROWS, blk_w), lambda i: (halo_idx(i), 1)),
                  pl.BlockSpec((HALO_ROWS, blk_w), lambda i: (halo_idx(i), 2)),
                  pl.BlockSpec((tb, cw), lambda i: (i, 0)),
                  _VMEM, _VMEM],
        out_specs=[pl.BlockSpec((tb, 2 * cw), lambda i: (i, 0)),
                   pl.BlockSpec((tb, 2 * cw), lambda i: (i, 0)),
                   pl.BlockSpec((2, tb, 1), lambda i: (0, i, 0))],
        out_shape=[jax.ShapeDtypeStruct((s, 2 * cw), BF16), jax.ShapeDtypeStruct((s, 2 * cw), F32),
                   jax.ShapeDtypeStruct((2, s, 1), F32)],
        compiler_params=_params(("parallel",)),
    )(proj, proj, proj, proj, attn, cw_full, g_ac)


def _out_proj_ln(mixed, w_out_g, x, ln_g, ln_b):
    s, d = mixed.shape
    tm = min(TM, s)
    tk = d
    nk = d // tk

    def body(a_ref, w_ref, x_ref, g_ref, b_ref, xhat_ref, h_ref, rstd_ref, acc):
        k = pl.program_id(1)
        _accumulate(acc, lambda: _dot(a_ref[...], w_ref[...]), k, nk)

        @pl.when(k == nk - 1)
        def _():
            def rows_fn(rows):
                xhat, rstd = _ln_fwd(ALPHA * x_ref[rows, :] + acc[rows, :])
                xhat_ref[rows, :] = xhat
                h_ref[rows, :] = (xhat * g_ref[...] + b_ref[...]).astype(BF16)
                rstd_ref[rows, :] = rstd

            _for_row_chunks(tm, rows_fn)

    row = pl.BlockSpec((tm, d), lambda i, k: (i, 0))
    return pl.pallas_call(
        body, name="out_proj_ln", grid=(s // tm, nk),
        in_specs=[pl.BlockSpec((tm, tk), lambda i, k: (i, k)),
                  pl.BlockSpec((tk, d), lambda i, k: (k, 0)),
                  pl.BlockSpec((None, tm, d), lambda i, k: (0, i, 0)),
                  _VMEM, _VMEM],
        out_specs=[row, row, pl.BlockSpec((tm, 1), lambda i, k: (i, 0))],
        out_shape=[jax.ShapeDtypeStruct((s, d), F32), jax.ShapeDtypeStruct((s, d), BF16),
                   jax.ShapeDtypeStruct((s, 1), F32)],
        scratch_shapes=[pltpu.VMEM((tm, d), F32)],
        compiler_params=_params(("parallel", "arbitrary")),
    )(mixed, w_out_g, x, ln_g, ln_b)


def _gate_up(h1, w_gu_g, first_vec, n_shards, into, name):
    s, d = h1.shape
    ns, _, fs2 = w_gu_g.shape
    fs = fs2 // 2
    tm = min(TM, s)

    def body(first_ref, h_ref, w_ref, act_in, ab_in, act_ref, ab_ref):
        gu = _dot(h_ref[...], w_ref[...])
        g, u = gu[:, :fs], gu[:, fs:]
        sg = _sigmoid(g)
        silu = g * sg
        act_ref[...] = (silu * u).astype(BF16)
        ab_ref[:, :fs] = (u * (sg * (1.0 + g * (1.0 - sg)))).astype(BF16)
        ab_ref[:, fs:] = silu.astype(BF16)

    shard = lambda j, first_ref: lax.rem(first_ref[0] + j, ns)
    grid_spec = pltpu.PrefetchScalarGridSpec(
        num_scalar_prefetch=1, grid=(s // tm, n_shards),
        in_specs=[pl.BlockSpec((tm, d), lambda i, j, first_ref: (i, 0)),
                  pl.BlockSpec((None, d, fs2), lambda i, j, first_ref: (shard(j, first_ref), 0, 0)), _ANY, _ANY],
        out_specs=[pl.BlockSpec((tm, fs), lambda i, j, first_ref: (i, shard(j, first_ref))),
                   pl.BlockSpec((tm, fs2), lambda i, j, first_ref: (i, shard(j, first_ref)))])
    return pl.pallas_call(
        body, name=name, grid_spec=grid_spec,
        out_shape=[jax.ShapeDtypeStruct((s, ns * fs), BF16), jax.ShapeDtypeStruct((s, ns * fs2), BF16)],
        input_output_aliases={} if into is None else {3: 0, 4: 1},
        compiler_params=_params(("parallel", "arbitrary")),
    )(first_vec, h1, w_gu_g, *((first_vec, first_vec) if into is None else into))


def _down_ln_loss(act, w_down_g, xhat1, ln1_g, ln1_b, ln2_g, ln2_b, target):
    s, f = act.shape
    d = xhat1.shape[1]
    tm = min(TM, s)
    tk = f // N_CHIPS
    nk = f // tk

    def body(a_ref, w_ref, xh_ref, g1_ref, b1_ref, g2_ref, b2_ref, t_ref, dpre_ref, dpre16_ref, loss_ref, gg_ref, gb_ref,
             acc):
        i, k = pl.program_id(0), pl.program_id(1)
        _accumulate(acc, lambda: _dot(a_ref[...], w_ref[...]), k, nk)

        @pl.when(k == nk - 1)
        def _():
            @pl.when(i == 0)
            def _():
                loss_ref[...] = jnp.zeros_like(loss_ref)
                gg_ref[...] = jnp.zeros_like(gg_ref)
                gb_ref[...] = jnp.zeros_like(gb_ref)

            def rows_fn(rows):
                h1 = xh_ref[rows, :] * g1_ref[...] + b1_ref[...]
                xhat, rstd = _ln_fwd(ALPHA * h1 + acc[rows, :])
                g2 = g2_ref[...]
                diff = xhat * g2 + b2_ref[...] - t_ref[rows, :]
                dy = diff * (1.0 / d)
                dpre = _ln_bwd(dy, xhat, rstd, g2)
                dpre_ref[rows, :] = dpre
                dpre16_ref[rows, :] = dpre.astype(BF16)
                sq = jnp.sum(jnp.sum(diff * diff, axis=1, keepdims=True), axis=0, keepdims=True)
                loss_ref[...] += jnp.broadcast_to(sq * (0.5 / d), (1, 128))
                gg_ref[...] += jnp.sum(dy * xhat, axis=0, keepdims=True)
                gb_ref[...] += jnp.sum(dy, axis=0, keepdims=True)

            _for_row_chunks(tm, rows_fn)

    row = pl.BlockSpec((tm, d), lambda i, k: (i, 0))
    vec = pl.BlockSpec((1, d), lambda i, k: (0, 0))
    return pl.pallas_call(
        body, name="down_ln_loss", grid=(s // tm, nk),
        in_specs=[pl.BlockSpec((tm, tk), lambda i, k: (i, k)),
                  pl.BlockSpec((tk, d), lambda i, k: (k, 0)),
                  row, _VMEM, _VMEM, _VMEM, _VMEM,
                  pl.BlockSpec((None, tm, d), lambda i, k: (0, i, 0))],
        out_specs=[row, row, pl.BlockSpec((1, 128), lambda i, k: (0, 0)), vec, vec],
        out_shape=[jax.ShapeDtypeStruct((s, d), F32), jax.ShapeDtypeStruct((s, d), BF16),
                   jax.ShapeDtypeStruct((1, 128), F32), jax.ShapeDtypeStruct((1, d), F32),
                   jax.ShapeDtypeStruct((1, d), F32)],
        scratch_shapes=[pltpu.VMEM((tm, d), F32)],
        compiler_params=_params(("arbitrary", "arbitrary")),
    )(act, w_down_g, xhat1, ln1_g, ln1_b, ln2_g, ln2_b, target)


def _dact_silu_bwd(dpre2, w_down_g, ab):
    s, d = dpre2.shape
    fs2 = ab.shape[1] // N_CHIPS
    fs = fs2 // 2
    tm = min(TM, s)

    def body(dp_ref, w_ref, ab_ref, dgu_ref):
        d_act = _dot_nt(dp_ref[...], w_ref[...])
        dgu_ref[:, :fs] = (d_act * ab_ref[:, :fs].astype(F32)).astype(BF16)
        dgu_ref[:, fs:] = (d_act * ab_ref[:, fs:].astype(F32)).astype(BF16)

    blk = pl.BlockSpec((tm, fs2), lambda j, i: (i, j))
    return pl.pallas_call(
        body, name="dact_silu_bwd", grid=(N_CHIPS, s // tm),
        in_specs=[pl.BlockSpec((tm, d), lambda j, i: (i, 0)),
                  pl.BlockSpec((fs, d), lambda j, i: (j, 0)), blk],
        out_specs=blk,
        out_shape=jax.ShapeDtypeStruct(ab.shape, BF16),
        compiler_params=_params(("parallel", "parallel")),
    )(dpre2, w_down_g, ab)


def _grad_rows(a, b, after, name, row_blocks=1):
    s, m = a.shape
    n = b.shape[1]
    ms = m // N_CHIPS
    tmw = ms // row_blocks
    tk = min(TK_TOK, s)
    nk = s // tk

    def body(a_ref, b_ref, after_ref, o_ref, acc):
        k = pl.program_id(2)
        _accumulate(acc, lambda: _dot_tn(a_ref[...].astype(BF16), b_ref[...].astype(BF16)), k, nk)

        @pl.when(k == nk - 1)
        def _():
            o_ref[...] = acc[...].astype(BF16)

    return pl.pallas_call(
        body, name=name, grid=(N_CHIPS, row_blocks, nk),
        in_specs=[pl.BlockSpec((tk, tmw), lambda j, r, k: (k, j * row_blocks + r)),
                  pl.BlockSpec((tk, n), lambda j, r, k: (k, 0)), _ANY],
        out_specs=pl.BlockSpec((None, tmw, n), lambda j, r, k: (j, r, 0)),
        out_shape=jax.ShapeDtypeStruct((N_CHIPS, ms, n), BF16),
        scratch_shapes=[pltpu.VMEM((tmw, n), F32)],
        compiler_params=_params(("parallel", "parallel", "arbitrary")),
    )(a, b, after)


def _grad_cols(a, bs, after, name, a_3d=False, row_blocks=2, shard=None):
    s, m = a.shape[-2:]
    n = bs[0].shape[1]
    ns = n // N_CHIPS
    nb = len(bs)
    tmw = m // row_blocks
    tk = min(TK_TOK, s)
    nk = s // tk

    def body(*refs):
        a_ref, b_refs, o_refs, accs = refs[0], refs[1:1 + nb], refs[2 + nb:2 + 2 * nb], refs[2 + 2 * nb:]
        k = pl.program_id(2)
        for b_ref, acc in zip(b_refs, accs):
            _accumulate(acc, lambda b_ref=b_ref: _dot_tn(a_ref[...].astype(BF16), b_ref[...].astype(BF16)), k, nk)

        @pl.when(k == nk - 1)
        def _():
            for o_ref, acc in zip(o_refs, accs):
                o_ref[...] = acc[...].astype(BF16)

    if a_3d:
        a_spec = pl.BlockSpec((None, tk, tmw), lambda j, r, k: (0, k, r))
    else:
        a_spec = pl.BlockSpec((tk, tmw), lambda j, r, k: (k, r))
    return _call_with_adamw(
        body, name, (N_CHIPS, row_blocks, nk),
        [a_spec] + [pl.BlockSpec((tk, ns), lambda j, r, k: (k, j))] * nb + [_ANY],
        [pl.BlockSpec((None, tmw, ns), lambda j, r, k: (j, r, 0))] * nb,
        [jax.ShapeDtypeStruct((N_CHIPS, m, ns), BF16)] * nb,
        [pltpu.VMEM((tmw, ns), F32)] * nb, ("parallel", "parallel", "arbitrary"), (a, *bs, after), shard)


def _dh1_ln_bwd(d_gu, w_gu_g, dpre2, xhat1, rstd1, ln1_g, after):
    s = d_gu.shape[0]
    d = dpre2.shape[1]
    hd = d // 2
    fs = w_gu_g.shape[2]
    tm = min(TM, s)

    def body(dgu_ref, w_ref, dp2_ref, xh_ref, rs_ref, g_ref, after_ref, dpre_ref, gg_ref, gb_ref, acc_lo, acc_hi):
        i, j, half = pl.program_id(0), pl.program_id(1), pl.program_id(2)

        def product():
            return _dot_nt(dgu_ref[...], w_ref[...])

        @pl.when(half == 0)
        def _():
            _accumulate(acc_lo, product, j, N_CHIPS)

        @pl.when(half == 1)
        def _():
            _accumulate(acc_hi, product, j, N_CHIPS)

        @pl.when((j == N_CHIPS - 1) & (half == 1))
        def _():
            @pl.when(i == 0)
            def _():
                gg_ref[...] = jnp.zeros_like(gg_ref)
                gb_ref[...] = jnp.zeros_like(gb_ref)

            def rows_fn(rows):
                dh = jnp.concatenate([acc_lo[rows, :], acc_hi[rows, :]], axis=1) + ALPHA * dp2_ref[rows, :]
                xhat = xh_ref[rows, :]
                dpre_ref[rows, :] = _ln_bwd(dh, xhat, rs_ref[rows, :], g_ref[...])
                gg_ref[...] += jnp.sum(dh * xhat, axis=0, keepdims=True)
                gb_ref[...] += jnp.sum(dh, axis=0, keepdims=True)

            _for_row_chunks(tm, rows_fn)

    row = pl.BlockSpec((tm, d), lambda i, j, h: (i, 0))
    vec = pl.BlockSpec((1, d), lambda i, j, h: (0, 0))
    act_blk = pl.BlockSpec((tm, fs), lambda i, j, h: (i, j))
    w_blk = pl.BlockSpec((None, hd, fs), lambda i, j, h: (j, h, 0))
    return pl.pallas_call(
        body, name="dh1_ln_bwd", grid=(s // tm, N_CHIPS, 2),
        in_specs=[act_blk, w_blk, row, row, pl.BlockSpec((tm, 1), lambda i, j, h: (i, 0)), _VMEM, _ANY],
        out_specs=[row, vec, vec],
        out_shape=[jax.ShapeDtypeStruct((s, d), F32), jax.ShapeDtypeStruct((1, d), F32),
                   jax.ShapeDtypeStruct((1, d), F32)],
        scratch_shapes=[pltpu.VMEM((tm, hd), F32)] * 2,
        compiler_params=_params(("arbitrary", "arbitrary", "arbitrary")),
    )(d_gu, w_gu_g, dpre2, xhat1, rstd1, ln1_g, after)


def _dmixed_rms_bwd(dpre1, w_out_g, ac, rstd, g_ac):
    s, d = dpre1.shape
    hd = d // 2
    tm = min(TM, s)

    def body(dp_ref, w_ref, ac_ref, rs_ref, g_ref, dac_ref, gg_ref):
        i = pl.program_id(1)
        dm = _dot_nt(dp_ref[...].astype(BF16), w_ref[...])
        pre = ac_ref[...]
        r = rs_ref[...]
        gdm = dm * g_ref[...]
        dac_ref[...] = r * gdm - pre * (r * r * r) * jnp.mean(gdm * pre, axis=-1, keepdims=True)
        gg = jnp.sum(dm * pre * r, axis=0, keepdims=True)

        @pl.when(i == 0)
        def _():
            gg_ref[...] = gg

        @pl.when(i > 0)
        def _():
            gg_ref[...] += gg

    return pl.pallas_call(
        body, name="dmixed_rms_bwd", grid=(2, s // tm),
        in_specs=[pl.BlockSpec((tm, d), lambda h, i: (i, 0)),
                  pl.BlockSpec((hd, d), lambda h, i: (h, 0)),
                  pl.BlockSpec((tm, hd), lambda h, i: (i, h)),
                  pl.BlockSpec((None, tm, 1), lambda h, i: (h, i, 0)),
                  pl.BlockSpec((1, hd), lambda h, i: (0, h))],
        out_specs=[pl.BlockSpec((tm, hd), lambda h, i: (i, h)),
                   pl.BlockSpec((1, hd), lambda h, i: (0, h))],
        out_shape=[jax.ShapeDtypeStruct((s, d), F32), jax.ShapeDtypeStruct((1, d), F32)],
        compiler_params=_params(("arbitrary", "arbitrary")),
    )(dpre1, w_out_g, ac, rstd, g_ac)


def _attention_bwd(proj, d_ac, cos_t, sin_t, sinks, after, shard):
    s = proj.shape[0]
    qw = GROUP * N_KV_HEADS * HEAD_DIM
    kvw = N_KV_HEADS * HEAD_DIM
    nb = s // WINDOW
    nq = GROUP * N_KV_HEADS

    def body(cur_ref, prev_ref, do_ref, cos_ref, sin_ref, cosp_ref, sinp_ref, sinks_ref, after_ref,
             dq_ref, dcur_ref, dprev_ref, dsink_ref):
        n = pl.program_id(0)
        first = n == 0
        q, k_all, v_all, cos_q, sin_q = _roped_qkv(cur_ref, prev_ref, cos_ref, sin_ref, cosp_ref, sinp_ref, qw, kvw)
        kk2s = [_pair_operand(k_all, h) for h in range(N_KV_HEADS)]
        vv2s = [_pair_operand(v_all, h) for h in range(N_KV_HEADS)]
        qps, probs, p_sinks = _all_probs(q, kk2s, first, sinks_ref)
        dops = [do_ref[:, pair * PAIR:(pair + 1) * PAIR].astype(BF16) for pair in range(N_PAIRS)]
        d_probs = jnp.concatenate([_dot_nt(dops[pair], vv2s[pair // (GROUP // 2)]) for pair in range(N_PAIRS)], axis=0)
        d_s, ds_sinks = [], []
        for t in range(2):
            cols = slice(t * KEYS, (t + 1) * KEYS)
            delta = jnp.sum(probs[:, cols] * d_probs[:, cols], axis=1, keepdims=True)
            d_s.append(probs[:, cols] * (d_probs[:, cols] - delta))
            ds_sinks.append(-p_sinks[t] * delta)
        d_s = jnp.concatenate(d_s, axis=1).astype(BF16)
        probs = probs.astype(BF16)
        dq_parts, dk_tiles, dv_tiles, dsink_parts = [], [], [], []
        for h in range(N_KV_HEADS):
            dkk2, dvv2 = None, None
            for p in range(GROUP // 2):
                pair = (GROUP // 2) * h + p
                rows = slice(pair * WINDOW, (pair + 1) * WINDOW)
                dq_parts.append(_dot(d_s[rows], kk2s[h]) * ATTN_SCALE)
                dk_term = _dot_tn(d_s[rows], qps[pair])
                dv_term = _dot_tn(probs[rows], dops[pair])
                dkk2 = dk_term if dkk2 is None else dkk2 + dk_term
                dvv2 = dv_term if dvv2 is None else dvv2 + dv_term
                dsink_parts.extend([jnp.sum(ds_sinks[t][rows], axis=0, keepdims=True) for t in range(2)])
            dk_tiles.append(_pair_grad(dkk2, h))
            dv_tiles.append(_pair_grad(dvv2, h))
        dq_ref[...] = _rope(jnp.concatenate(dq_parts, axis=1), cos_q, sin_q, -1.0)
        dk = jnp.concatenate([dk_tiles[0] + dk_tiles[1], dk_tiles[2] + dk_tiles[3]], axis=1)
        dv = jnp.concatenate([dv_tiles[0] + dv_tiles[1], dv_tiles[2] + dv_tiles[3]], axis=1)
        dprev_ref[...] = jnp.concatenate([dk[:WINDOW], dv[:WINDOW]], axis=1)
        dcur_ref[...] = jnp.concatenate([dk[WINDOW:], dv[WINDOW:]], axis=1)
        dsink = jnp.concatenate(dsink_parts, axis=1)

        @pl.when(first)
        def _():
            dsink_ref[...] = dsink

        @pl.when(n > 0)
        def _():
            dsink_ref[...] += dsink

    tbl = pl.BlockSpec((WINDOW, kvw), lambda n: (n, 0))
    tbl_prev = pl.BlockSpec((WINDOW, kvw), lambda n: (jnp.maximum(n - 1, 0), 0))
    kv_blk = pl.BlockSpec((WINDOW, 2 * kvw), lambda n: (n, 0))
    return _call_with_adamw(
        body, "attention_bwd", (nb,),
        [pl.BlockSpec((WINDOW, qw + 2 * kvw), lambda n: (n, 0)),
         pl.BlockSpec((WINDOW, 2 * kvw), lambda n: (jnp.maximum(n - 1, 0), (qw // (2 * kvw)))),
         pl.BlockSpec((WINDOW, qw), lambda n: (n, 0)),
         tbl, tbl, tbl_prev, tbl_prev, _VMEM, _ANY],
        [pl.BlockSpec((WINDOW, qw), lambda n: (n, 0)), kv_blk, kv_blk, pl.BlockSpec((1, nq), lambda n: (0, 0))],
        [jax.ShapeDtypeStruct((s, qw), F32), jax.ShapeDtypeStruct((s, 2 * kvw), F32),
         jax.ShapeDtypeStruct((s, 2 * kvw), F32), jax.ShapeDtypeStruct((1, nq), F32)],
        [], ("arbitrary",), (proj, proj, d_ac, cos_t, sin_t, cos_t, sin_t, sinks, after), shard)


def _dproj_assemble(proj, d_ac, dq, dkv_cur, dkv_prev, cos_t, sin_t, cw_full):
    s, in_w = proj.shape
    cw = dq.shape[1]
    kvw = N_KV_HEADS * HEAD_DIM
    blk_w = in_w // 3
    tb = WINDOW
    nb = s // tb

    def body(lo_ref, hi_ref, lo_p_ref, hi_p_ref, lo_n_ref, hi_n_ref, dconv_ref, dconv_n_ref,
             dq_ref, dcur_ref, dprev_n_ref, cos_ref, sin_ref, cw_ref, dproj_ref, gcw_ref):
        i = pl.program_id(0)
        last = i == nb - 1
        c_gate, b_gate, u = _split_cbu(lo_ref[...], hi_ref[...], cw)
        c_p, _, u_p = _split_cbu(lo_p_ref[...], hi_p_ref[...], cw)
        _, b_n, _ = _split_cbu(lo_n_ref[...], hi_n_ref[...], cw)
        z = c_gate * u
        z_p = jnp.where(i == 0, 0.0, c_p * u_p)
        z1 = _shift_down(z, z_p, 1)
        z2 = _shift_down(z, z_p, 2)
        w0, w1, w2 = _conv_taps(cw_ref)
        y = w0 * z2 + w1 * z1 + w2 * z
        d_conv = dconv_ref[...]
        d_b = d_conv * y
        d_y = d_conv * b_gate
        d_y_n = jnp.where(last, 0.0, dconv_n_ref[...] * b_n[:dconv_n_ref.shape[0]])
        d_z = w2 * d_y + w1 * _shift_up(d_y, d_y_n, 1) + w0 * _shift_up(d_y, d_y_n, 2)
        d_c = d_z * u
        d_u = d_z * c_gate
        gcw = jnp.concatenate([jnp.sum(d_y * z2, axis=0, keepdims=True), jnp.sum(d_y * z1, axis=0, keepdims=True),
                               jnp.sum(d_y * z, axis=0, keepdims=True)], axis=0)

        @pl.when(i == 0)
        def _():
            gcw_ref[...] = gcw

        @pl.when(i > 0)
        def _():
            gcw_ref[...] += gcw

        dkv = dcur_ref[...] + jnp.where(last, 0.0, dprev_n_ref[...])
        dk = _rope(dkv[:, :kvw], cos_ref[...], sin_ref[...], -1.0)
        dproj_ref[...] = jnp.concatenate([dq_ref[...], dk, dkv[:, kvw:], d_c, d_b, d_u], axis=1).astype(BF16)

    prev_halo = lambda i: jnp.maximum(i * (tb // HALO_ROWS) - 1, 0)
    next_halo = lambda i: jnp.minimum((i + 1) * (tb // HALO_ROWS), s // HALO_ROWS - 1)
    next8 = lambda i: jnp.minimum((i + 1) * (tb // 8), s // 8 - 1)
    nxt = lambda i: jnp.minimum(i + 1, nb - 1)
    return pl.pallas_call(
        body, name="dproj_assemble", grid=(nb,),
        in_specs=[pl.BlockSpec((tb, blk_w), lambda i: (i, 1)),
                  pl.BlockSpec((tb, blk_w), lambda i: (i, 2)),
                  pl.BlockSpec((HALO_ROWS, blk_w), lambda i: (prev_halo(i), 1)),
                  pl.BlockSpec((HALO_ROWS, blk_w), lambda i: (prev_halo(i), 2)),
                  pl.BlockSpec((HALO_ROWS, blk_w), lambda i: (next_halo(i), 1)),
                  pl.BlockSpec((HALO_ROWS, blk_w), lambda i: (next_halo(i), 2)),
                  pl.BlockSpec((tb, cw), lambda i: (i, 1)),
                  pl.BlockSpec((8, cw), lambda i: (next8(i), 1)),
                  pl.BlockSpec((tb, cw), lambda i: (i, 0)),
                  pl.BlockSpec((tb, 2 * kvw), lambda i: (i, 0)),
                  pl.BlockSpec((tb, 2 * kvw), lambda i: (nxt(i), 0)),
                  pl.BlockSpec((tb, kvw), lambda i: (i, 0)),
                  pl.BlockSpec((tb, kvw), lambda i: (i, 0)),
                  _VMEM],
        out_specs=[pl.BlockSpec((tb, in_w), lambda i: (i, 0)),
                   pl.BlockSpec((3, cw), lambda i: (0, 0))],
        out_shape=[jax.ShapeDtypeStruct((s, in_w), BF16), jax.ShapeDtypeStruct((3, cw), F32)],
        compiler_params=_params(("arbitrary",)),
    )(proj, proj, proj, proj, proj, proj, d_ac, d_ac, dq, dkv_cur, dkv_prev, cos_t, sin_t, cw_full)


def _dx(d_proj, w_in_g, dpre1, after, shard):
    s, in_w = d_proj.shape
    ns, d, ncol = w_in_g.shape
    tm = min(TM, s)

    def body(dp_ref, w_ref, r_ref, after_ref, o_ref, acc):
        j = pl.program_id(1)
        _accumulate(acc, lambda: _dot_nt(dp_ref[...], w_ref[...]), j, ns)

        @pl.when(j == ns - 1)
        def _():
            o_ref[...] = acc[...] + ALPHA * r_ref[...]

    return _call_with_adamw(
        body, "dx", (s // tm, ns),
        [pl.BlockSpec((tm, ncol), lambda i, j: (i, j)),
         pl.BlockSpec((None, d, ncol), lambda i, j: (j, 0, 0)),
         pl.BlockSpec((tm, d), lambda i, j: (i, 0)), _ANY],
        [pl.BlockSpec((None, tm, d), lambda i, j: (0, i, 0))], [jax.ShapeDtypeStruct((1, s, d), F32)],
        [pltpu.VMEM((tm, d), F32)], ("parallel", "arbitrary"), (d_proj, w_in_g, dpre1, after), shard)


def kernel(x, positions, w_in, conv_w, sinks, g_attn, g_conv, w_out, ln1_g, ln1_b, w_gate, w_up, w_down, ln2_g, ln2_b, loss_target, m_w_in, m_conv_w, m_sinks, m_g_attn, m_g_conv, m_w_out, m_ln1_g, m_ln1_b, m_w_gate, m_w_up, m_w_down, m_ln2_g, m_ln2_b, v_w_in, v_conv_w, v_sinks, v_g_attn, v_g_conv, v_w_out, v_ln1_g, v_ln1_b, v_w_gate, v_w_up, v_w_down, v_ln2_g, v_ln2_b):
    s = x.shape[1]
    d = x.shape[2]

    chip_vec = _chip_id(lax.axis_index("x"), lax.axis_index("y")).astype(jnp.int32).reshape(1)
    wnames = ["w_in", "w_out", "w_gu", "w_down"]
    buf_in = _cast_weight(w_in, chip_vec, chip_vec, "cast_w_in")
    flight_in, token_in = _gather_start([buf_in], chip_vec, "gather_start_w_in")
    cw_buf = lax.dynamic_update_slice(jnp.zeros((N_CHIPS,) + conv_w.shape[1:], F32), conv_w, (chip_vec[0], 0, 0))
    cw_flight = _flight_start("conv_w_start", [cw_buf], _conv_w_plan(), 3, token_in)
    started = cw_flight[2][0]
    buf_gu = _cast_weight(w_gate, chip_vec, started, "cast_w_gate", 0, 2)
    buf_gu = _cast_weight(w_up, chip_vec, buf_gu, "cast_w_up", 1, 2)
    bufs = [_cast_weight(w_out, chip_vec, started, "cast_w_out"), buf_gu,
            _cast_weight(w_down, chip_vec, started, "cast_w_down")]
    flights_rest, token = _gather_start(bufs, token_in, "gather_start_rest")
    flights = flight_in + flights_rest

    def gathered(i, after):
        send_sems, recv_sems, buf = flights[i]
        buf = _gather_wait(send_sems, recv_sems, buf, after, "gather_wait_" + wnames[i])
        return _sibling_fill(buf, "sibling_fill_" + wnames[i])

    g_ac = jnp.concatenate([g_attn, g_conv], axis=1)

    proj_own = _in_proj(x, _after(flights[0][2], token), chip_vec, 1, None, "in_proj_own")
    cos_t, sin_t = _rope_tables(positions.reshape(s, 1) + token[0:1, 0:1].astype(jnp.int32))
    w_in_g = gathered(0, _after(cos_t, proj_own))
    proj = _in_proj(x, w_in_g, chip_vec + 1, N_CHIPS - 1, proj_own, "in_proj_rest")
    send_sems, recv_sems, buf_out = flights[1]
    buf_out = _gather_wait(send_sems, recv_sems, buf_out, proj, "gather_wait_w_out")
    fill_out = _flight_start("fill_start_w_out", [buf_out], _fill_plan(1), 3, chip_vec)
    attn = _attention_fwd(_after(proj, fill_out[2][0]), cos_t, sin_t, sinks)
    (cw_full,) = _flight_wait("conv_w_wait", cw_flight, _conv_w_plan(), attn)
    mixed, ac, rstd_ac = _conv_norm(proj, attn, cw_full, g_ac)
    (w_out_g,) = _flight_wait("fill_wait_w_out", fill_out, _fill_plan(1), mixed)
    w_out_full = w_out_g.reshape(d, d)
    xhat1, h1, rstd1 = _out_proj_ln(mixed, w_out_full, x, ln1_g, ln1_b)
    send_sems, recv_sems, buf_gu = flights[2]
    buf_gu = _gather_wait(send_sems, recv_sems, buf_gu, h1, "gather_wait_w_gu")
    fill_gu = _flight_start("fill_start_w_gu", [buf_gu], _fill_plan(1), 3, chip_vec)
    own = _gate_up(h1, fill_gu[2][0], chip_vec, 1, None, "gate_up_own")
    (w_gu_g,) = _flight_wait("fill_wait_w_gu", fill_gu, _fill_plan(1), own[0])
    act, ab = _gate_up(h1, w_gu_g, chip_vec + 1, N_CHIPS - 1, own, "gate_up_rest")
    w_down_full = gathered(3, act).reshape(-1, d)
    dpre2, dpre2_16, loss_part, g_ln2_g, g_ln2_b = _down_ln_loss(act, w_down_full, xhat1, ln1_g, ln1_b, ln2_g, ln2_b,
                                                                 loss_target)

    cvec = lax.axis_index("c").astype(jnp.int32).reshape(1)

    def exchange_begin(parts, nme):
        bufs = []
        for part in parts:
            ns, r, cdim = part.shape
            bufs.extend([part, lax.empty((ns, r // 2, cdim), part.dtype)])
        return _flight_start("exchange_start_" + nme, bufs, _exchange_plan(len(parts)), len(parts), cvec)

    def exchange_end(flight, n_parts, after, nme):
        bufs = _flight_wait("exchange_wait_" + nme, flight, _exchange_plan(n_parts), after)
        return [(bufs[2 * w], bufs[2 * w + 1]) for w in range(n_parts)]

    def scatter_begin(part, got, nme):
        return _scatter_start(_add_halves(part, got, cvec, "add_halves_" + nme), "scatter_start_" + nme)

    d_gu = _dact_silu_bwd(dpre2_16, w_down_full, ab)
    p_down = _grad_rows(act, dpre2_16, d_gu, "grad_w_down")
    x_down = exchange_begin([p_down], "w_down")
    (p_gu,) = _grad_cols(h1, [d_gu], x_down[2][0], "grad_w_gate_up")
    ((p_down, got),) = exchange_end(x_down, 1, p_gu, "w_down")
    f_down = scatter_begin(p_down, got, "w_down")
    x_gu = exchange_begin([_after(p_gu, f_down[2])], "w_gu")
    dpre1, g_ln1_g, g_ln1_b = _dh1_ln_bwd(d_gu, w_gu_g, dpre2, xhat1, rstd1, ln1_g, x_gu[2][0])
    ((p_gu, got),) = exchange_end(x_gu, 1, dpre1, "w_gu")
    f_gu = scatter_begin(p_gu, got, "w_gu")
    d_ac, g_g_ac = _dmixed_rms_bwd(_after(dpre1, f_gu[2]), w_out_full, ac, rstd_ac, g_ac)
    pos_vec = jnp.concatenate([chip_vec, cvec])
    sums, land = _scatter_wait(*f_down, d_ac, "scatter_wait_w_down")
    c_down = _flight_start("complete_start_w_down", [sums, land], _complete_plan(1), 4, cvec)
    p_out = _grad_rows(mixed, dpre1, c_down[2][1], "grad_w_out")
    x_out = exchange_begin([p_out], "w_out")
    sums, land = _flight_wait("complete_wait_w_down", c_down, _complete_plan(1), x_out[2][0])
    dq, dkv_cur, dkv_prev, g_sinks, *new_w_down = _attention_bwd(
        proj, d_ac, cos_t, sin_t, sinks, x_out[2][0], (w_down, m_w_down, v_w_down, land, sums, pos_vec, 0))
    ((p_out, got),) = exchange_end(x_out, 1, dq, "w_out")
    f_out = scatter_begin(p_out, got, "w_out")
    sums, land = _scatter_wait(*f_gu, f_out[2], "scatter_wait_w_gu")
    c_gu = _flight_start("complete_start_w_gu", [sums, land], _complete_plan(1), 4, cvec)
    d_proj, g_conv_w = _dproj_assemble(proj, _after(d_ac, c_gu[2][1]), dq, dkv_cur, dkv_prev, cos_t, sin_t, cw_full)
    red = _allreduce_small(g_ln2_g, g_ln2_b, g_ln1_g, g_ln1_b, g_g_ac, g_conv_w, g_sinks, loss_part, d_proj)
    sums_gu, land_gu = _flight_wait("complete_wait_w_gu", c_gu, _complete_plan(1), red)
    p_in, *new_w_gate = _grad_cols(x, [d_proj], red, "grad_w_in", a_3d=True,
                                   shard=(w_gate, m_w_gate, v_w_gate, land_gu, sums_gu, pos_vec, 0))
    x_in = exchange_begin([p_in], "w_in")
    sums, land = _scatter_wait(*f_out, x_in[2][0], "scatter_wait_w_out")
    c_out = _flight_start("complete_start_w_out", [sums, land], _complete_plan(1), 4, cvec)
    new_w_up = _adamw_shard(w_up, m_w_up, v_w_up, _after(land_gu, c_out[2][1]), sums_gu, pos_vec, "adamw_w_up", 1)
    ((p_in, got),) = exchange_end(x_in, 1, new_w_up[0], "w_in")
    f_in = scatter_begin(p_in, got, "w_in")
    (grad_x,) = _dx(d_proj, w_in_g, dpre1, f_in[2], None)

    big = {"w_down": new_w_down, "w_gate": new_w_gate, "w_up": new_w_up}
    sums, land = _flight_wait("complete_wait_w_out", c_out, _complete_plan(1), grad_x)
    big["w_out"] = _adamw_shard(w_out, m_w_out, v_w_out, land, sums, pos_vec, "adamw_w_out")
    sums, land = _scatter_wait(*f_in, big["w_out"][0], "scatter_wait_w_in")
    (land,) = _complete_chip_sums([sums], [land])
    big["w_in"] = _adamw_shard(w_in, m_w_in, v_w_in, land, sums, pos_vec, "adamw_w_in")
    small = _adamw_small(red, {
        "sinks": (sinks, m_sinks, v_sinks), "g_attn": (g_attn, m_g_attn, v_g_attn),
        "g_conv": (g_conv, m_g_conv, v_g_conv), "ln1_g": (ln1_g, m_ln1_g, v_ln1_g),
        "ln1_b": (ln1_b, m_ln1_b, v_ln1_b), "ln2_g": (ln2_g, m_ln2_g, v_ln2_g),
        "ln2_b": (ln2_b, m_ln2_b, v_ln2_b), "conv_w": (conv_w, m_conv_w, v_conv_w)})
    res = {**big, **small}
    order = ["w_in", "conv_w", "sinks", "g_attn", "g_conv", "w_out", "ln1_g", "ln1_b", "w_gate", "w_up", "w_down",
             "ln2_g", "ln2_b"]
    loss = red[6, d // 2 + 128]
    return (loss, grad_x, *[res[n][0] for n in order], *[res[n][1] for n in order],
            *[res[n][2] for n in order], *[res[n][3] for n in order])
```

```python
import functools

import numpy as np
import jax
import jax.numpy as jnp
from jax import lax
from jax.experimental import pallas as pl
from jax.experimental.pallas import tpu as pltpu

F32 = jnp.float32
BF16 = jnp.bfloat16
MESH = pl.DeviceIdType.MESH

HEAD_DIM = 64
N_KV_HEADS = 4
GROUP = 4
WINDOW = 128
ROT_DIM = 16
ROPE_THETA = 500000.0
ATTN_SCALE = HEAD_DIM ** -0.5
ALPHA = 2.0 ** 0.25
LN_EPS = 1e-5
RMS_EPS = 1e-6
ADAM_LR = 0.001
ADAM_B1 = 0.9
ADAM_B2 = 0.999
ADAM_EPS = 1e-08
ADAM_WD = 0.01
ADAM_STEP = 10
N_CHIPS = 4
NEG_BIG = -1e30

V7X_VMEM_BYTES = 64 * 1024 * 1024
VMEM_LIMIT = V7X_VMEM_BYTES - 6 * 1024 * 1024

TM = 512
TK_TOK = 1024
TB_CONV = 256
TR_ELT = 256
ROW_CHUNK = 128
HALO_ROWS = 16


def _params(sem):
    return pltpu.CompilerParams(dimension_semantics=sem, vmem_limit_bytes=VMEM_LIMIT)


def _row_tile(rows, target):
    best = None
    for t in range(16, min(rows, target) + 1, 16):
        if rows % t == 0:
            best = t
    assert best is not None, (rows, target)
    return best


def _dot(a, b):
    return jnp.dot(a, b, preferred_element_type=F32)


def _dot_nt(a, b):
    return lax.dot_general(a, b, (((1,), (1,)), ((), ())), preferred_element_type=F32)


def _dot_tn(a, b):
    return lax.dot_general(a, b, (((0,), (0,)), ((), ())), preferred_element_type=F32)


def _mesh_pos():
    x, y, c = lax.axis_index("x"), lax.axis_index("y"), lax.axis_index("c")
    chips = [(1 - x, y), (x, 1 - y), (1 - x, 1 - y)]
    return x, y, c, chips


def _chip_id(px, py):
    return 2 * px + py


def _rope(t, cos, sgn_sin, sign):
    w = t.shape[1]
    lane = lax.broadcasted_iota(jnp.int32, t.shape, 1) & (HEAD_DIM - 1)
    partner = jnp.where(lane < ROT_DIM // 2, pltpu.roll(t, w - ROT_DIM // 2, 1), pltpu.roll(t, ROT_DIM // 2, 1))
    return t * cos + sign * (partner * sgn_sin)


def _tile_lanes(t, n):
    return jnp.concatenate([t] * n, axis=1)


def _sigmoid(g):
    return 1.0 / (1.0 + jnp.exp(-g))


def _for_row_chunks(n_rows, fn):
    def step(r, carry):
        fn(pl.ds(pl.multiple_of(r * ROW_CHUNK, ROW_CHUNK), ROW_CHUNK))
        return carry

    lax.fori_loop(0, n_rows // ROW_CHUNK, step, 0)


def _accumulate(acc, make_val, k, nk):
    if nk == 1:
        acc[...] = make_val()
        return

    @pl.when(k == 0)
    def _():
        acc[...] = jnp.zeros_like(acc)

    acc[...] += make_val()


def _ln_fwd(pre):
    mu = jnp.mean(pre, axis=-1, keepdims=True)
    cen = pre - mu
    var = jnp.mean(cen * cen, axis=-1, keepdims=True)
    rstd = lax.rsqrt(var + LN_EPS)
    return cen * rstd, rstd


def _ln_bwd(dy, xhat, rstd, g):
    dxhat = dy * g
    m1 = jnp.mean(dxhat, axis=-1, keepdims=True)
    m2 = jnp.mean(dxhat * xhat, axis=-1, keepdims=True)
    return rstd * (dxhat - m1 - xhat * m2)


def _cast_weight(w, chip_vec, after, name, col_block=0, n_col_blocks=1):
    _, r, c = w.shape
    tr = _row_tile(r, TR_ELT)

    def body(chip_ref, w_ref, after_ref, o_ref):
        o_ref[...] = w_ref[...].astype(BF16)

    grid_spec = pltpu.PrefetchScalarGridSpec(
        num_scalar_prefetch=1, grid=(r // tr,),
        in_specs=[pl.BlockSpec((None, tr, c), lambda i, chip_ref: (0, i, 0)), _ANY],
        out_specs=pl.BlockSpec((None, tr, c), lambda i, chip_ref: (chip_ref[0], i, col_block)))
    return pl.pallas_call(
        body, name=name, grid_spec=grid_spec,
        out_shape=jax.ShapeDtypeStruct((N_CHIPS, r, n_col_blocks * c), BF16),
        input_output_aliases={2: 0} if col_block else {},
        compiler_params=_params(("parallel",)),
    )(chip_vec, w, after)


_HBM = pl.BlockSpec(memory_space=pltpu.HBM)
_VMEM = pl.BlockSpec(memory_space=pltpu.VMEM)


_SEM = pl.BlockSpec(memory_space=pltpu.SEMAPHORE)
_ANY = pl.BlockSpec(memory_space=pl.ANY)
_EFFECT = pltpu.SideEffectType.DATAFLOW_SIDE_EFFECTING


def _chip_copy(buf, k, chip_of_src, half_rows, send_sems, recv_sems, to):
    part = buf.at[chip_of_src, half_rows]
    return pltpu.make_async_remote_copy(
        src_ref=part, dst_ref=part, send_sem=send_sems.at[k], recv_sem=recv_sems.at[k], device_id=to, device_id_type=MESH)


def _half_rows(buf, which):
    hr = buf.shape[1] // 2
    return pl.ds(which * hr, hr)


def _after(value, dep):
    return lax.optimization_barrier((value, dep))[0]


def _flight_start(name, bufs, plan, n_sems, after):
    n = len(bufs)

    def body(*refs):
        sends, _ = plan(refs[:n], refs[n + 1], refs[n + 2])
        for cp in sends:
            cp.start()

    outs = pl.pallas_call(
        body, name=name,
        in_specs=[_HBM] * n + [_ANY], out_specs=[_SEM, _SEM] + [_HBM] * n,
        out_shape=[pltpu.SemaphoreType.DMA((n_sems,))] * 2 + [pltpu.HBM(b.shape, b.dtype) for b in bufs],
        input_output_aliases={i: 2 + i for i in range(n)},
        compiler_params=pltpu.CompilerParams(has_side_effects=_EFFECT),
    )(*[pltpu.with_memory_space_constraint(b, pltpu.HBM) for b in bufs], after)
    return outs[0], outs[1], list(outs[2:])


def _flight_wait(name, flight, plan, after):
    send_sems, recv_sems, bufs = flight
    n = len(bufs)

    def body(*refs):
        sends, recvs = plan(refs[:n], refs[n], refs[n + 1])
        for cp in sends:
            cp.wait_send()
        for cp in recvs:
            cp.wait_recv()

    outs = pl.pallas_call(
        body, name=name,
        in_specs=[_HBM] * n + [_SEM, _SEM, _ANY], out_specs=[_HBM] * n,
        out_shape=[pltpu.HBM(b.shape, b.dtype) for b in bufs],
        input_output_aliases={i: i for i in range(n)},
        compiler_params=pltpu.CompilerParams(has_side_effects=_EFFECT),
    )(*bufs, send_sems, recv_sems, after)
    return list(outs)


def _fill_plan(n_bufs):
    def plan(refs, send_sems, recv_sems):
        x, y, c, chips = _mesh_pos()
        sibling = (x, y, 1 - c)
        sends, recvs = [], []
        for w in range(n_bufs):
            for k, chip in enumerate(chips):
                slot = _chip_id(*chip)
                sends.append(_chip_copy(refs[w], 3 * w + k, slot, _half_rows(refs[w], c), send_sems, recv_sems, sibling))
                recvs.append(_chip_copy(refs[w], 3 * w + k, slot, _half_rows(refs[w], 1 - c), send_sems, recv_sems,
                                        sibling))
        return sends, recvs
    return plan


def _conv_w_plan():
    def plan(refs, send_sems, recv_sems):
        x, y, c, chips = _mesh_pos()
        me = _chip_id(x, y)
        (buf,) = refs
        sends, recvs = [], []
        for k, chip in enumerate(chips):
            for slot, into in ((me, sends), (_chip_id(*chip), recvs)):
                into.append(pltpu.make_async_remote_copy(
                    src_ref=buf.at[slot], dst_ref=buf.at[slot], send_sem=send_sems.at[k], recv_sem=recv_sems.at[k],
                    device_id=(*chip, c), device_id_type=MESH))
        return sends, recvs
    return plan


def _exchange_plan(n_parts):
    def plan(refs, send_sems, recv_sems):
        x, y, c, _ = _mesh_pos()
        copies = []
        for w in range(n_parts):
            part, got = refs[2 * w], refs[2 * w + 1]
            hr = got.shape[1]
            copies.append(pltpu.make_async_remote_copy(
                src_ref=part.at[:, pl.ds((1 - c) * hr, hr)], dst_ref=got, send_sem=send_sems.at[w],
                recv_sem=recv_sems.at[w], device_id=(x, y, 1 - c), device_id_type=MESH))
        return copies, copies
    return plan


def _gather_start(bufs, after, name):
    n = len(bufs)

    def body(*refs):
        ins = refs[:n]
        sends, recvs = refs[n + 1:2 * n + 1], refs[2 * n + 1:3 * n + 1]
        token = refs[4 * n + 1]
        x, y, c, chips = _mesh_pos()
        me = _chip_id(x, y)
        for w in range(n):
            for k, chip in enumerate(chips):
                _chip_copy(ins[w], k, me, _half_rows(ins[w], c), sends[w], recvs[w], (*chip, c)).start()
        token[...] = jnp.zeros_like(token)

    outs = pl.pallas_call(
        body, name=name,
        in_specs=[_HBM] * n + [_ANY],
        out_specs=[_SEM] * (2 * n) + [_HBM] * n + [_VMEM],
        out_shape=[pltpu.SemaphoreType.DMA((3,))] * (2 * n) + [pltpu.HBM(b.shape, b.dtype) for b in bufs]
        + [jax.ShapeDtypeStruct((8, 128), F32)],
        input_output_aliases={w: 2 * n + w for w in range(n)},
        compiler_params=pltpu.CompilerParams(has_side_effects=_EFFECT),
    )(*[pltpu.with_memory_space_constraint(b, pltpu.HBM) for b in bufs], after)
    return [(outs[w], outs[n + w], outs[2 * n + w]) for w in range(n)], outs[3 * n]


def _gather_wait(send_sems, recv_sems, buf, after, name):
    def body(buf_ref, send_ref, recv_ref, after_ref, out_ref):
        x, y, c, chips = _mesh_pos()
        me = _chip_id(x, y)
        for k, chip in enumerate(chips):
            _chip_copy(buf_ref, k, me, _half_rows(buf_ref, c), send_ref, recv_ref, (*chip, c)).wait_send()
        for k, chip in enumerate(chips):
            _chip_copy(buf_ref, k, _chip_id(*chip), _half_rows(buf_ref, c), send_ref, recv_ref, (*chip, c)).wait_recv()

    return pl.pallas_call(
        body, name=name,
        in_specs=[_HBM, _SEM, _SEM, _ANY], out_specs=_HBM,
        out_shape=pltpu.HBM(buf.shape, buf.dtype),
        input_output_aliases={0: 0},
        compiler_params=pltpu.CompilerParams(has_side_effects=_EFFECT),
    )(buf, send_sems, recv_sems, after)


def _sibling_fill(buf, name, own_too=False):
    n_copies = 4 if own_too else 3

    def body(buf_ref, out_ref, send_sems, recv_sems):
        x, y, c, chips = _mesh_pos()
        sibling = (x, y, 1 - c)
        slots = [_chip_id(*chip) for chip in chips] + ([_chip_id(x, y)] if own_too else [])
        copies = []
        for k, slot in enumerate(slots):
            cp = _chip_copy(out_ref, k, slot, _half_rows(out_ref, c), send_sems, recv_sems, sibling)
            cp.start()
            copies.append(cp)
        for k, slot in enumerate(slots):
            _chip_copy(out_ref, k, slot, _half_rows(out_ref, 1 - c), send_sems, recv_sems, sibling).wait_recv()
        for cp in copies:
            cp.wait_send()

    return pl.pallas_call(
        body, name=name,
        in_specs=[_HBM], out_specs=_HBM,
        out_shape=jax.ShapeDtypeStruct(buf.shape, buf.dtype),
        input_output_aliases={0: 0},
        scratch_shapes=[pltpu.SemaphoreType.DMA((n_copies,)), pltpu.SemaphoreType.DMA((n_copies,))],
    )(buf)


def _allgather_conv_w(cw):
    _, kw, cs = cw.shape

    def body(cw_ref, out_ref, send_sems, recv_sems):
        x, y, c, chips = _mesh_pos()
        me = _chip_id(x, y)
        out_ref[pl.ds(me, 1)] = cw_ref[...]
        copies = []
        for k, chip in enumerate(chips):
            cp = pltpu.make_async_remote_copy(
                src_ref=cw_ref.at[0], dst_ref=out_ref.at[me], send_sem=send_sems.at[k], recv_sem=recv_sems.at[k],
                device_id=(*chip, c), device_id_type=MESH)
            cp.start()
            copies.append(cp)
        for k, chip in enumerate(chips):
            pltpu.make_async_remote_copy(
                src_ref=cw_ref.at[0], dst_ref=out_ref.at[_chip_id(*chip)], send_sem=send_sems.at[k],
                recv_sem=recv_sems.at[k], device_id=(*chip, c), device_id_type=MESH).wait_recv()
        for cp in copies:
            cp.wait_send()

    return pl.pallas_call(
        body, name="allgather_conv_w",
        in_specs=[_VMEM], out_specs=_VMEM,
        out_shape=jax.ShapeDtypeStruct((N_CHIPS, kw, cs), F32),
        scratch_shapes=[pltpu.SemaphoreType.DMA((3,)), pltpu.SemaphoreType.DMA((3,))],
    )(cw)


def _exchange_halves(parts, after, name):
    n = len(parts)
    shapes = [p.shape for p in parts]

    def body(*refs):
        ins, outs = refs[:n], refs[n + 1:2 * n + 1]
        send_sems, recv_sems = refs[2 * n + 1:]
        x, y, c, _ = _mesh_pos()
        copies = []
        for w in range(n):
            hr = shapes[w][1] // 2
            cp = pltpu.make_async_remote_copy(
                src_ref=ins[w].at[:, pl.ds((1 - c) * hr, hr)], dst_ref=outs[w],
                send_sem=send_sems.at[w], recv_sem=recv_sems.at[w],
                device_id=(x, y, 1 - c), device_id_type=MESH)
            cp.start()
            copies.append(cp)
        for cp in copies:
            cp.wait()

    return pl.pallas_call(
        body, name=name,
        in_specs=[_HBM] * n + [_ANY], out_specs=[_HBM] * n,
        out_shape=[jax.ShapeDtypeStruct((s[0], s[1] // 2, s[2]), BF16) for s in shapes],
        scratch_shapes=[pltpu.SemaphoreType.DMA((n,)), pltpu.SemaphoreType.DMA((n,))],
    )(*parts, after)


def _add_halves(part, got, cvec, name):
    ns, r, cdim = part.shape
    hr = r // 2
    tr = _row_tile(hr, TR_ELT)
    nblk = hr // tr

    def body(c_ref, a_ref, b_ref, o_ref):
        o_ref[...] = (a_ref[...].astype(F32) + b_ref[...].astype(F32)).astype(BF16)

    grid_spec = pltpu.PrefetchScalarGridSpec(
        num_scalar_prefetch=1, grid=(ns, nblk),
        in_specs=[pl.BlockSpec((None, tr, cdim), lambda s, i, c_ref: (s, c_ref[0] * nblk + i, 0)),
                  pl.BlockSpec((None, tr, cdim), lambda s, i, c_ref: (s, i, 0))],
        out_specs=pl.BlockSpec((None, tr, cdim), lambda s, i, c_ref: (s, i, 0)))
    return pl.pallas_call(
        body, name=name, grid_spec=grid_spec,
        out_shape=jax.ShapeDtypeStruct((ns, hr, cdim), BF16),
        compiler_params=_params(("parallel", "parallel")),
    )(cvec, part, got)


def _scatter_copy(sums_ref, land_ref, k, src_slot, dst_slot, c, send_sems, recv_sems, to):
    return pltpu.make_async_remote_copy(
        src_ref=sums_ref.at[src_slot], dst_ref=land_ref.at[dst_slot, _half_rows(land_ref, c)],
        send_sem=send_sems.at[k], recv_sem=recv_sems.at[k], device_id=to, device_id_type=MESH)


def _scatter_start(sums, name):
    ns, hr, cdim = sums.shape
    land = lax.empty((ns, 2 * hr, cdim), sums.dtype)

    def body(sums_ref, land_ref, send_sems, recv_sems, sums_thru, land_thru):
        x, y, c, chips = _mesh_pos()
        me = _chip_id(x, y)
        for k, chip in enumerate(chips):
            _scatter_copy(sums_ref, land_ref, k, _chip_id(*chip), me, c, send_sems, recv_sems, (*chip, c)).start()

    return pl.pallas_call(
        body, name=name,
        in_specs=[_HBM, _HBM], out_specs=[_SEM, _SEM, _HBM, _HBM],
        out_shape=[pltpu.SemaphoreType.DMA((3,)), pltpu.SemaphoreType.DMA((3,)),
                   pltpu.HBM(sums.shape, sums.dtype), pltpu.HBM(land.shape, land.dtype)],
        input_output_aliases={0: 2, 1: 3},
        compiler_params=pltpu.CompilerParams(has_side_effects=_EFFECT),
    )(pltpu.with_memory_space_constraint(sums, pltpu.HBM), pltpu.with_memory_space_constraint(land, pltpu.HBM))


def _scatter_wait(send_sems, recv_sems, sums, land, after, name):
    def body(sums_ref, land_ref, send_ref, recv_ref, after_ref, sums_out, land_out):
        x, y, c, chips = _mesh_pos()
        me = _chip_id(x, y)
        for k, chip in enumerate(chips):
            _scatter_copy(sums_ref, land_ref, k, _chip_id(*chip), me, c, send_ref, recv_ref, (*chip, c)).wait_send()
        for k, chip in enumerate(chips):
            _scatter_copy(sums_ref, land_ref, k, me, _chip_id(*chip), c, send_ref, recv_ref, (*chip, c)).wait_recv()

    return pl.pallas_call(
        body, name=name,
        in_specs=[_HBM, _HBM, _SEM, _SEM, _ANY], out_specs=[_HBM, _HBM],
        out_shape=[pltpu.HBM(sums.shape, sums.dtype), pltpu.HBM(land.shape, land.dtype)],
        input_output_aliases={0: 0, 1: 1},
        compiler_params=pltpu.CompilerParams(has_side_effects=_EFFECT),
    )(sums, land, send_sems, recv_sems, after)


def _complete_plan(n_weights):
    def plan(refs, send_sems, recv_sems):
        x, y, c, chips = _mesh_pos()
        me = _chip_id(x, y)
        sibling = (x, y, 1 - c)
        sends, recvs = [], []
        for w in range(n_weights):
            sums, land = refs[2 * w], refs[2 * w + 1]
            sends.append(_scatter_copy(sums, land, 4 * w + 3, me, me, c, send_sems, recv_sems, sibling))
            recvs.append(_scatter_copy(sums, land, 4 * w + 3, me, me, 1 - c, send_sems, recv_sems, sibling))
            for k, chip in enumerate(chips):
                slot = _chip_id(*chip)
                sends.append(_chip_copy(land, 4 * w + k, slot, _half_rows(land, c), send_sems, recv_sems, sibling))
                recvs.append(_chip_copy(land, 4 * w + k, slot, _half_rows(land, 1 - c), send_sems, recv_sems, sibling))
        return sends, recvs
    return plan


def _complete_chip_sums(sums, lands):
    n = len(sums)

    def body(*refs):
        sums_refs, outs = refs[:n], refs[2 * n:3 * n]
        send_sems, recv_sems = refs[3 * n:]
        x, y, c, chips = _mesh_pos()
        me = _chip_id(x, y)
        sibling = (x, y, 1 - c)
        slots = [_chip_id(*chip) for chip in chips]
        sent = []
        for w in range(n):
            out = outs[w]
            cp = _scatter_copy(sums_refs[w], out, 3, me, me, c, send_sems.at[w], recv_sems.at[w], sibling)
            cp.start()
            sent.append(cp)
            for k, slot in enumerate(slots):
                cp = _chip_copy(out, k, slot, _half_rows(out, c), send_sems.at[w], recv_sems.at[w], sibling)
                cp.start()
                sent.append(cp)
        for w in range(n):
            out = outs[w]
            _scatter_copy(sums_refs[w], out, 3, me, me, 1 - c, send_sems.at[w], recv_sems.at[w], sibling).wait_recv()
            for k, slot in enumerate(slots):
                _chip_copy(out, k, slot, _half_rows(out, 1 - c), send_sems.at[w], recv_sems.at[w], sibling).wait_recv()
        for cp in sent:
            cp.wait_send()

    return pl.pallas_call(
        body, name="complete_chip_sums",
        in_specs=[_HBM] * (2 * n), out_specs=[_HBM] * n,
        out_shape=[jax.ShapeDtypeStruct(b.shape, b.dtype) for b in lands],
        input_output_aliases={n + w: w for w in range(n)},
        scratch_shapes=[pltpu.SemaphoreType.DMA((n, 4)), pltpu.SemaphoreType.DMA((n, 4))],
    )(*sums, *lands)


SMALL_ROWS = 8


def _allreduce_small(gl2g, gl2b, gl1g, gl1b, g_ac, gcw, gsink, loss, after):
    d = gl2g.shape[1]
    hd = d // 2
    nq = gsink.shape[1]

    def body(a_ref, b_ref, c_ref, d_ref, e_ref, cw_ref, sk_ref, ls_ref, after_ref, out_ref, mine, gath, send_sems,
             recv_sems):
        x, y, c, _ = _mesh_pos()
        me = 4 * x + 2 * y + c
        mine[...] = jnp.zeros_like(mine)
        mine[0:1, :] = a_ref[...]
        mine[1:2, :] = b_ref[...]
        mine[2:3, :] = c_ref[...]
        mine[3:4, :] = d_ref[...]
        mine[4:5, :] = e_ref[...]
        mine[5:6, 0:hd] = cw_ref[0:1, :]
        mine[5:6, hd:d] = cw_ref[1:2, :]
        mine[6:7, 0:hd] = cw_ref[2:3, :]
        mine[6:7, hd:hd + nq] = sk_ref[...]
        mine[6:7, hd + 128:hd + 256] = ls_ref[...]
        gath[pl.ds(me, 1)] = mine[...][None]
        copies = []
        for r in range(1, 8):
            peer = ((1 - x) if r & 4 else x, (1 - y) if r & 2 else y, (1 - c) if r & 1 else c)
            cp = pltpu.make_async_remote_copy(
                src_ref=mine, dst_ref=gath.at[me], send_sem=send_sems.at[r - 1], recv_sem=recv_sems.at[r - 1],
                device_id=peer, device_id_type=MESH)
            cp.start()
            copies.append(cp)
        for r in range(1, 8):
            peer = ((1 - x) if r & 4 else x, (1 - y) if r & 2 else y, (1 - c) if r & 1 else c)
            peer_id = 4 * peer[0] + 2 * peer[1] + peer[2]
            pltpu.make_async_remote_copy(
                src_ref=mine, dst_ref=gath.at[peer_id], send_sem=send_sems.at[r - 1], recv_sem=recv_sems.at[r - 1],
                device_id=peer, device_id_type=MESH).wait_recv()
        for cp in copies:
            cp.wait_send()
        total = gath[0]
        for dev in range(1, 8):
            total = total + gath[dev]
        out_ref[...] = total

    return pl.pallas_call(
        body, name="allreduce_small",
        in_specs=[_VMEM] * 8 + [_ANY], out_specs=_VMEM,
        out_shape=jax.ShapeDtypeStruct((SMALL_ROWS, d), F32),
        scratch_shapes=[pltpu.VMEM((SMALL_ROWS, d), F32), pltpu.VMEM((8, SMALL_ROWS, d), F32),
                        pltpu.SemaphoreType.DMA((7,)), pltpu.SemaphoreType.DMA((7,))],
    )(gl2g, gl2b, gl1g, gl1b, g_ac, gcw, gsink, loss, after)


def _adamw(w, g, m, v):
    m = ADAM_B1 * m + (1.0 - ADAM_B1) * g
    v = ADAM_B2 * v + (1.0 - ADAM_B2) * (g * g)
    m_hat = m / (1.0 - ADAM_B1 ** ADAM_STEP)
    v_hat = v / (1.0 - ADAM_B2 ** ADAM_STEP)
    delta = -ADAM_LR * (m_hat / (jnp.sqrt(v_hat) + ADAM_EPS) + ADAM_WD * w)
    return delta, m, v


def _adamw_shard(w, m, v, land, own, pos_vec, name, col_block=0):
    tr = _row_tile(w.shape[1] // 2, TR_ELT)
    grid = (w.shape[1] // tr,)
    body, in_specs, out_specs, out_shape = _adamw_passenger(w.shape, tr, grid, col_block)
    grid_spec = pltpu.PrefetchScalarGridSpec(num_scalar_prefetch=1, grid=grid, in_specs=in_specs, out_specs=out_specs)
    return pl.pallas_call(
        body, name=name, grid_spec=grid_spec, out_shape=out_shape,
        compiler_params=_params(("parallel",)),
    )(pos_vec, w, m, v, land, land, land, land, own)


def _adamw_passenger(shape, tr, grid, col_block):
    _, r, c = shape
    nh = r // 2 // tr
    n_blocks = 2 * nh
    n_steps = int(np.prod(grid))
    assert nh * tr * 2 == r and n_blocks <= n_steps

    def step_of(ids):
        step = ids[0]
        for n, i in zip(grid[1:], ids[1:]):
            step = step * n + i
        return step

    def block_of(ids):
        return jnp.minimum(step_of(ids), n_blocks - 1)

    def update(pos_ref, w_ref, m_ref, v_ref, l0, l1, l2, l3, own_ref, g_out, d_out, m_out, v_out):
        i = block_of([pl.program_id(a) for a in range(len(grid))])
        mine = (i // nh) == pos_ref[1]
        own_blk = own_ref[...].astype(F32)
        g = None
        for s, l_ref in enumerate([l0, l1, l2, l3]):
            term = jnp.where(mine & (pos_ref[0] == s), own_blk, l_ref[...].astype(F32))
            g = term if g is None else g + term
        delta, nm, nv = _adamw(w_ref[...], g, m_ref[...], v_ref[...])
        g_out[...] = g
        d_out[...] = delta
        m_out[...] = nm
        v_out[...] = nv

    def body(*refs):
        if n_blocks == n_steps:
            update(*refs)
        else:
            pl.when(step_of([pl.program_id(a) for a in range(len(grid))]) < n_blocks)(lambda: update(*refs))

    def land_spec(s):
        def index(*args):
            i, pos_ref = block_of(args[:-1]), args[-1]
            skip = (pos_ref[0] == s) & ((i // nh) == pos_ref[1])
            return (s, jnp.where(skip, (i + nh) % n_blocks, i), col_block)
        return pl.BlockSpec((None, tr, c), index)

    blk = pl.BlockSpec((None, tr, c), lambda *args: (0, block_of(args[:-1]), 0))
    in_specs = ([blk, blk, blk] + [land_spec(s) for s in range(N_CHIPS)]
                + [pl.BlockSpec((None, tr, c), lambda *args: (args[-1][0], block_of(args[:-1]) % nh, col_block))])
    return body, in_specs, [blk] * 4, [jax.ShapeDtypeStruct((1, r, c), F32)] * 4


def _call_with_adamw(body, name, grid, in_specs, out_specs, out_shape, scratch_shapes, semantics, operands, shard):
    if shard is None:
        return pl.pallas_call(
            body, name=name, grid=grid, in_specs=in_specs, out_specs=out_specs, out_shape=out_shape,
            scratch_shapes=scratch_shapes, compiler_params=_params(semantics))(*operands)
    w, m, v, land, own, pos_vec, col_block = shard
    n_steps = int(np.prod(grid))
    hr = w.shape[1] // 2
    tr = min(t for t in range(16, hr + 1, 16) if hr % t == 0 and 2 * (hr // t) <= n_steps)
    adam_body, adam_in, adam_out, adam_shape = _adamw_passenger(w.shape, tr, grid, col_block)
    n_in, n_out = len(in_specs), len(out_specs)

    def with_pos(spec):
        if spec.index_map is None:
            return spec
        return pl.BlockSpec(spec.block_shape, lambda *args: spec.index_map(*args[:-1]))

    def both(pos_ref, *refs):
        ins, adam_ins = refs[:n_in], refs[n_in:n_in + len(adam_in)]
        refs = refs[n_in + len(adam_in):]
        outs, adam_outs, scratch = refs[:n_out], refs[n_out:n_out + len(adam_out)], refs[n_out + len(adam_out):]
        body(*ins, *outs, *scratch)
        adam_body(pos_ref, *adam_ins, *adam_outs)

    grid_spec = pltpu.PrefetchScalarGridSpec(
        num_scalar_prefetch=1, grid=grid, in_specs=[with_pos(sp) for sp in in_specs] + adam_in,
        out_specs=[with_pos(sp) for sp in out_specs] + adam_out, scratch_shapes=scratch_shapes)
    return pl.pallas_call(
        both, name=name, grid_spec=grid_spec, out_shape=list(out_shape) + adam_shape,
        compiler_params=_params(semantics),
    )(pos_vec, *operands, w, m, v, land, land, land, land, own)


def _adamw_small(red, params):
    names = ["sinks", "g_attn", "g_conv", "ln1_g", "ln1_b", "ln2_g", "ln2_b", "conv_w"]
    d = red.shape[1]
    hd = d // 2
    flat = []
    for nme in names:
        flat.extend(params[nme])
    nq = params["sinks"][0].shape[1]
    cs = params["conv_w"][0].shape[2]

    def body(*refs):
        red_ref = refs[0]
        ins = refs[1:1 + 3 * len(names)]
        outs = refs[1 + 3 * len(names):]
        x, y, _, _ = _mesh_pos()
        me = _chip_id(x, y)

        def conv_tap(row, base):
            picked = red_ref[row:row + 1, base:base + cs]
            for s in range(1, N_CHIPS):
                picked = jnp.where(me == s, red_ref[row:row + 1, base + s * cs:base + (s + 1) * cs], picked)
            return picked

        grads = {
            "sinks": red_ref[6:7, hd:hd + nq],
            "g_attn": red_ref[4:5, 0:hd],
            "g_conv": red_ref[4:5, hd:d],
            "ln1_g": red_ref[2:3, :],
            "ln1_b": red_ref[3:4, :],
            "ln2_g": red_ref[0:1, :],
            "ln2_b": red_ref[1:2, :],
        }
        for i, nme in enumerate(names):
            w_ref, m_ref, v_ref = ins[3 * i:3 * i + 3]
            g_out, d_out, m_out, v_out = outs[4 * i:4 * i + 4]
            if nme == "conv_w":
                for tap, (row, base) in enumerate([(5, 0), (5, hd), (6, 0)]):
                    g = conv_tap(row, base)
                    delta, nm, nv = _adamw(w_ref[0, tap:tap + 1, :], g, m_ref[0, tap:tap + 1, :], v_ref[0, tap:tap + 1, :])
                    g_out[0, tap:tap + 1, :] = g
                    d_out[0, tap:tap + 1, :] = delta
                    m_out[0, tap:tap + 1, :] = nm
                    v_out[0, tap:tap + 1, :] = nv
            else:
                g = grads[nme]
                delta, nm, nv = _adamw(w_ref[...], g, m_ref[...], v_ref[...])
                g_out[...] = g
                d_out[...] = delta
                m_out[...] = nm
                v_out[...] = nv

    out_shape = []
    for nme in names:
        out_shape.extend([jax.ShapeDtypeStruct(params[nme][0].shape, F32)] * 4)
    outs = pl.pallas_call(
        body, name="adamw_small",
        in_specs=[_VMEM] * (1 + len(flat)), out_specs=[_VMEM] * len(out_shape),
        out_shape=out_shape,
    )(red, *flat)
    return {nme: tuple(outs[4 * i:4 * i + 4]) for i, nme in enumerate(names)}


def _rope_tables(pos_col):
    s = pos_col.shape[0]
    w = N_KV_HEADS * HEAD_DIM
    tb = min(512, s)
    inv_freq = (ROPE_THETA ** (-np.arange(0, ROT_DIM, 2, dtype=np.float32) / ROT_DIM)).astype(np.float32)

    def body(pos_ref, cos_ref, sin_ref):
        pos = pos_ref[...].astype(F32)
        lane = lax.broadcasted_iota(jnp.int32, (tb, PAIR), 1) & (HEAD_DIM - 1)
        fidx = lane & (ROT_DIM // 2 - 1)
        inv = jnp.zeros((tb, PAIR), F32)
        for k in range(ROT_DIM // 2):
            inv = jnp.where(fidx == k, float(inv_freq[k]), inv)
        ang = pos * inv
        rot = lane < ROT_DIM
        sin_v = jnp.sin(ang)
        cos_ref[...] = _tile_lanes(jnp.where(rot, jnp.cos(ang), 1.0), w // PAIR)
        sin_ref[...] = _tile_lanes(jnp.where(lane < ROT_DIM // 2, -sin_v, jnp.where(rot, sin_v, 0.0)), w // PAIR)

    return pl.pallas_call(
        body, name="rope_tables", grid=(s // tb,),
        in_specs=[pl.BlockSpec((tb, 1), lambda i: (i, 0))],
        out_specs=[pl.BlockSpec((tb, w), lambda i: (i, 0))] * 2,
        out_shape=[jax.ShapeDtypeStruct((s, w), F32)] * 2,
        compiler_params=_params(("parallel",)),
    )(pos_col)


def _in_proj(x, w_in_g, first_vec, n_shards, into, name):
    _, s, d = x.shape
    ns, _, ncol = w_in_g.shape
    tm = min(2 * TM, s)

    def body(first_ref, x_ref, w_ref, into_ref, o_ref):
        o_ref[...] = _dot(x_ref[...].astype(BF16), w_ref[...]).astype(BF16)

    shard = lambda j, first_ref: lax.rem(first_ref[0] + j, ns)
    grid_spec = pltpu.PrefetchScalarGridSpec(
        num_scalar_prefetch=1, grid=(s // tm, n_shards),
        in_specs=[pl.BlockSpec((None, tm, d), lambda i, j, first_ref: (0, i, 0)),
                  pl.BlockSpec((None, d, ncol), lambda i, j, first_ref: (shard(j, first_ref), 0, 0)), _ANY],
        out_specs=pl.BlockSpec((tm, ncol), lambda i, j, first_ref: (i, shard(j, first_ref))))
    return pl.pallas_call(
        body, name=name, grid_spec=grid_spec,
        out_shape=jax.ShapeDtypeStruct((s, ns * ncol), BF16),
        input_output_aliases={} if into is None else {3: 0},
        compiler_params=_params(("parallel", "arbitrary")),
    )(first_vec, x, w_in_g, first_vec if into is None else into)


PAIR = 2 * HEAD_DIM
KEYS = 2 * WINDOW


def _pair_operand(t_all, h):
    col = (h // 2) * PAIR
    lane = lax.broadcasted_iota(jnp.int32, (KEYS, PAIR), 1)
    own_low = h % 2 == 0
    mine = jnp.where((lane < HEAD_DIM) if own_low else (lane >= HEAD_DIM), t_all[:, col:col + PAIR], 0.0)
    other = pltpu.roll(mine, HEAD_DIM, 1)
    low, high = (mine, other) if own_low else (other, mine)
    return jnp.concatenate([low, high], axis=0).astype(BF16)


def _pair_grad(acc, h):
    lane = lax.broadcasted_iota(jnp.int32, (KEYS, PAIR), 1)
    low = jnp.where(lane < HEAD_DIM, acc[:KEYS], 0.0)
    high = jnp.where(lane >= HEAD_DIM, acc[KEYS:], 0.0)
    if h % 2 == 0:
        return low + pltpu.roll(high, HEAD_DIM, 1)
    return high + pltpu.roll(low, HEAD_DIM, 1)


N_PAIRS = N_KV_HEADS * GROUP // 2


def _all_probs(q, kk2s, first, sinks_ref):
    assert ATTN_SCALE == 0.125
    q = q * ATTN_SCALE
    qps, scores = [], []
    for pair in range(N_PAIRS):
        qp = q[:, pair * PAIR:(pair + 1) * PAIR].astype(BF16)
        qps.append(qp)
        scores.append(_dot_nt(qp, kk2s[pair // (GROUP // 2)]))
    qi = lax.broadcasted_iota(jnp.int32, (WINDOW, 2 * KEYS), 0)
    kj = lax.broadcasted_iota(jnp.int32, (WINDOW, 2 * KEYS), 1) & (KEYS - 1)
    rel = qi + WINDOW - kj
    valid = (rel >= 0) & (rel < WINDOW) & jnp.logical_not(first & (kj < WINDOW))
    bias = jnp.where(valid, 0.0, NEG_BIG)
    s = (jnp.stack(scores, axis=0) + bias[None]).reshape(N_PAIRS * WINDOW, 2 * KEYS)
    probs, p_sinks = [], []
    for t in range(2):
        st = s[:, t * KEYS:(t + 1) * KEYS]
        sink = jnp.concatenate([jnp.broadcast_to(sinks_ref[0:1, 2 * pair + t:2 * pair + t + 1], (WINDOW, 1))
                                for pair in range(N_PAIRS)], axis=0)
        m = jnp.maximum(jnp.max(st, axis=1, keepdims=True), sink)
        e = jnp.exp(st - m)
        e_sink = jnp.exp(sink - m)
        inv_l = 1.0 / (jnp.sum(e, axis=1, keepdims=True) + e_sink)
        probs.append(e * inv_l)
        p_sinks.append(e_sink * inv_l)
    return qps, jnp.concatenate(probs, axis=1), p_sinks


def _roped_qkv(cur_ref, prev_ref, cos_ref, sin_ref, cosp_ref, sinp_ref, qw, kvw):
    cur = cur_ref[...].astype(F32)
    cos, sin = cos_ref[...], sin_ref[...]
    cos_q, sin_q = _tile_lanes(cos, GROUP), _tile_lanes(sin, GROUP)
    q = _rope(cur[:, :qw], cos_q, sin_q, 1.0)
    prev = prev_ref[...].astype(F32)
    k_all = jnp.concatenate([_rope(prev[:, :kvw], cosp_ref[...], sinp_ref[...], 1.0),
                             _rope(cur[:, qw:qw + kvw], cos, sin, 1.0)], axis=0)
    v_all = jnp.concatenate([prev[:, kvw:], cur[:, qw + kvw:]], axis=0)
    return q, k_all, v_all, cos_q, sin_q


def _attention_fwd(proj, cos_t, sin_t, sinks):
    s = proj.shape[0]
    qw = GROUP * N_KV_HEADS * HEAD_DIM
    kvw = N_KV_HEADS * HEAD_DIM
    nb = s // WINDOW

    def body(cur_ref, prev_ref, cos_ref, sin_ref, cosp_ref, sinp_ref, sinks_ref, o_ref):
        first = pl.program_id(0) == 0
        q, k_all, v_all, _, _ = _roped_qkv(cur_ref, prev_ref, cos_ref, sin_ref, cosp_ref, sinp_ref, qw, kvw)
        kk2s = [_pair_operand(k_all, h) for h in range(N_KV_HEADS)]
        vv2s = [_pair_operand(v_all, h) for h in range(N_KV_HEADS)]
        _, probs, _ = _all_probs(q, kk2s, first, sinks_ref)
        probs = probs.astype(BF16)
        outs = [_dot(probs[pair * WINDOW:(pair + 1) * WINDOW], vv2s[pair // (GROUP // 2)]) for pair in range(N_PAIRS)]
        o_ref[...] = jnp.concatenate(outs, axis=1)

    tbl = pl.BlockSpec((WINDOW, kvw), lambda n: (n, 0))
    tbl_prev = pl.BlockSpec((WINDOW, kvw), lambda n: (jnp.maximum(n - 1, 0), 0))
    return pl.pallas_call(
        body, name="attention_fwd", grid=(nb,),
        in_specs=[pl.BlockSpec((WINDOW, qw + 2 * kvw), lambda n: (n, 0)),
                  pl.BlockSpec((WINDOW, 2 * kvw), lambda n: (jnp.maximum(n - 1, 0), (qw // (2 * kvw)))),
                  tbl, tbl, tbl_prev, tbl_prev, _VMEM],
        out_specs=pl.BlockSpec((WINDOW, qw), lambda n: (n, 0)),
        out_shape=jax.ShapeDtypeStruct((s, qw), F32),
        compiler_params=_params(("parallel",)),
    )(proj, proj, cos_t, sin_t, cos_t, sin_t, sinks)


def _conv_taps(cw_ref):
    return [jnp.concatenate([cw_ref[s, k:k + 1, :] for s in range(N_CHIPS)], axis=1) for k in range(3)]


def _shift_down(z, halo, steps):
    last = halo.shape[0]
    row = lax.broadcasted_iota(jnp.int32, z.shape, 0)
    out = pltpu.roll(z, steps, 0)
    for r in range(steps):
        out = jnp.where(row == r, halo[last - steps + r:last - steps + r + 1, :], out)
    return out


def _shift_up(z, halo, steps):
    rows = z.shape[0]
    row = lax.broadcasted_iota(jnp.int32, z.shape, 0)
    out = pltpu.roll(z, rows - steps, 0)
    for r in range(steps):
        out = jnp.where(row == rows - steps + r, halo[r:r + 1, :], out)
    return out


def _split_cbu(lo, hi, cw):
    lo, hi = lo.astype(F32), hi.astype(F32)
    c_gate = lo[:, :cw]
    b_gate = jnp.concatenate([lo[:, cw:], hi[:, :2 * cw - lo.shape[1]]], axis=1)
    u = hi[:, 2 * cw - lo.shape[1]:]
    return c_gate, b_gate, u


def _conv_norm(proj, attn, cw_full, g_ac):
    s, in_w = proj.shape
    cw = attn.shape[1]
    blk_w = in_w // 3
    tb = min(TB_CONV, s)

    def body(lo_ref, hi_ref, lo_h_ref, hi_h_ref, attn_ref, cw_ref, g_ref, mixed_ref, ac_ref, rstd_ref):
        i = pl.program_id(0)
        c_gate, b_gate, u = _split_cbu(lo_ref[...], hi_ref[...], cw)
        c_h, _, u_h = _split_cbu(lo_h_ref[...], hi_h_ref[...], cw)
        z = c_gate * u
        z_h = jnp.where(i == 0, 0.0, c_h * u_h)
        w0, w1, w2 = _conv_taps(cw_ref)
        y = w0 * _shift_down(z, z_h, 2) + w1 * _shift_down(z, z_h, 1) + w2 * z
        conv = b_gate * y
        a = attn_ref[...]
        r_a = lax.rsqrt(jnp.mean(a * a, axis=-1, keepdims=True) + RMS_EPS)
        r_c = lax.rsqrt(jnp.mean(conv * conv, axis=-1, keepdims=True) + RMS_EPS)
        g = g_ref[...]
        mixed_ref[...] = jnp.concatenate([a * r_a * g[:, :cw], conv * r_c * g[:, cw:]], axis=1).astype(BF16)
        ac_ref[...] = jnp.concatenate([a, conv], axis=1)
        rstd_ref[0] = r_a
        rstd_ref[1] = r_c

    halo_idx = lambda i: jnp.maximum(i * (tb // HALO_ROWS) - 1, 0)
    return pl.pallas_call(
        body, name="conv_norm", grid=(s // tb,),
        in_specs=[pl.BlockSpec((tb, blk_w), lambda i: (i, 1)),
                  pl.BlockSpec((tb, blk_w), lambda i: (i, 2)),
                  pl.BlockSpec((HALO_ROWS, blk_w), lambda i: (halo_idx(i), 1)),
                  pl.BlockSpec((HALO_ROWS, blk_w), lambda i: (halo_idx(i), 2)),
                  pl.BlockSpec((tb, cw), lambda i: (i, 0)),
                  _VMEM, _VMEM],
        out_specs=[pl.BlockSpec((tb, 2 * cw), lambda i: (i, 0)),
                   pl.BlockSpec((tb, 2 * cw), lambda i: (i, 0)),
                   pl.BlockSpec((2, tb, 1), lambda i: (0, i, 0))],
        out_shape=[jax.ShapeDtypeStruct((s, 2 * cw), BF16), jax.ShapeDtypeStruct((s, 2 * cw), F32),
                   jax.ShapeDtypeStruct((2, s, 1), F32)],
        compiler_params=_params(("parallel",)),
    )(proj, proj, proj, proj, attn, cw_full, g_ac)


def _out_proj_ln(mixed, w_out_g, x, ln_g, ln_b):
    s, d = mixed.shape
    tm = min(TM, s)
    tk = d
    nk = d // tk

    def body(a_ref, w_ref, x_ref, g_ref, b_ref, xhat_ref, h_ref, rstd_ref, acc):
        k = pl.program_id(1)
        _accumulate(acc, lambda: _dot(a_ref[...], w_ref[...]), k, nk)

        @pl.when(k == nk - 1)
        def _():
            def rows_fn(rows):
                xhat, rstd = _ln_fwd(ALPHA * x_ref[rows, :] + acc[rows, :])
                xhat_ref[rows, :] = xhat
                h_ref[rows, :] = (xhat * g_ref[...] + b_ref[...]).astype(BF16)
                rstd_ref[rows, :] = rstd

            _for_row_chunks(tm, rows_fn)

    row = pl.BlockSpec((tm, d), lambda i, k: (i, 0))
    return pl.pallas_call(
        body, name="out_proj_ln", grid=(s // tm, nk),
        in_specs=[pl.BlockSpec((tm, tk), lambda i, k: (i, k)),
                  pl.BlockSpec((tk, d), lambda i, k: (k, 0)),
                  pl.BlockSpec((None, tm, d), lambda i, k: (0, i, 0)),
                  _VMEM, _VMEM],
        out_specs=[row, row, pl.BlockSpec((tm, 1), lambda i, k: (i, 0))],
        out_shape=[jax.ShapeDtypeStruct((s, d), F32), jax.ShapeDtypeStruct((s, d), BF16),
                   jax.ShapeDtypeStruct((s, 1), F32)],
        scratch_shapes=[pltpu.VMEM((tm, d), F32)],
        compiler_params=_params(("parallel", "arbitrary")),
    )(mixed, w_out_g, x, ln_g, ln_b)


def _gate_up(h1, w_gu_g, first_vec, n_shards, into, name):
    s, d = h1.shape
    ns, _, fs2 = w_gu_g.shape
    fs = fs2 // 2
    tm = min(TM, s)

    def body(first_ref, h_ref, w_ref, act_in, ab_in, act_ref, ab_ref):
        gu = _dot(h_ref[...], w_ref[...])
        g, u = gu[:, :fs], gu[:, fs:]
        sg = _sigmoid(g)
        silu = g * sg
        act_ref[...] = (silu * u).astype(BF16)
        ab_ref[:, :fs] = (u * (sg * (1.0 + g * (1.0 - sg)))).astype(BF16)
        ab_ref[:, fs:] = silu.astype(BF16)

    shard = lambda j, first_ref: lax.rem(first_ref[0] + j, ns)
    grid_spec = pltpu.PrefetchScalarGridSpec(
        num_scalar_prefetch=1, grid=(s // tm, n_shards),
        in_specs=[pl.BlockSpec((tm, d), lambda i, j, first_ref: (i, 0)),
                  pl.BlockSpec((None, d, fs2), lambda i, j, first_ref: (shard(j, first_ref), 0, 0)), _ANY, _ANY],
        out_specs=[pl.BlockSpec((tm, fs), lambda i, j, first_ref: (i, shard(j, first_ref))),
                   pl.BlockSpec((tm, fs2), lambda i, j, first_ref: (i, shard(j, first_ref)))])
    return pl.pallas_call(
        body, name=name, grid_spec=grid_spec,
        out_shape=[jax.ShapeDtypeStruct((s, ns * fs), BF16), jax.ShapeDtypeStruct((s, ns * fs2), BF16)],
        input_output_aliases={} if into is None else {3: 0, 4: 1},
        compiler_params=_params(("parallel", "arbitrary")),
    )(first_vec, h1, w_gu_g, *((first_vec, first_vec) if into is None else into))


def _down_ln_loss(act, w_down_g, xhat1, ln1_g, ln1_b, ln2_g, ln2_b, target):
    s, f = act.shape
    d = xhat1.shape[1]
    tm = min(TM, s)
    tk = f // N_CHIPS
    nk = f // tk

    def body(a_ref, w_ref, xh_ref, g1_ref, b1_ref, g2_ref, b2_ref, t_ref, dpre_ref, dpre16_ref, loss_ref, gg_ref, gb_ref,
             acc):
        i, k = pl.program_id(0), pl.program_id(1)
        _accumulate(acc, lambda: _dot(a_ref[...], w_ref[...]), k, nk)

        @pl.when(k == nk - 1)
        def _():
            @pl.when(i == 0)
            def _():
                loss_ref[...] = jnp.zeros_like(loss_ref)
                gg_ref[...] = jnp.zeros_like(gg_ref)
                gb_ref[...] = jnp.zeros_like(gb_ref)

            def rows_fn(rows):
                h1 = xh_ref[rows, :] * g1_ref[...] + b1_ref[...]
                xhat, rstd = _ln_fwd(ALPHA * h1 + acc[rows, :])
                g2 = g2_ref[...]
                diff = xhat * g2 + b2_ref[...] - t_ref[rows, :]
                dy = diff * (1.0 / d)
                dpre = _ln_bwd(dy, xhat, rstd, g2)
                dpre_ref[rows, :] = dpre
                dpre16_ref[rows, :] = dpre.astype(BF16)
                sq = jnp.sum(jnp.sum(diff * diff, axis=1, keepdims=True), axis=0, keepdims=True)
                loss_ref[...] += jnp.broadcast_to(sq * (0.5 / d), (1, 128))
                gg_ref[...] += jnp.sum(dy * xhat, axis=0, keepdims=True)
                gb_ref[...] += jnp.sum(dy, axis=0, keepdims=True)

            _for_row_chunks(tm, rows_fn)

    row = pl.BlockSpec((tm, d), lambda i, k: (i, 0))
    vec = pl.BlockSpec((1, d), lambda i, k: (0, 0))
    return pl.pallas_call(
        body, name="down_ln_loss", grid=(s // tm, nk),
        in_specs=[pl.BlockSpec((tm, tk), lambda i, k: (i, k)),
                  pl.BlockSpec((tk, d), lambda i, k: (k, 0)),
                  row, _VMEM, _VMEM, _VMEM, _VMEM,
                  pl.BlockSpec((None, tm, d), lambda i, k: (0, i, 0))],
        out_specs=[row, row, pl.BlockSpec((1, 128), lambda i, k: (0, 0)), vec, vec],
        out_shape=[jax.ShapeDtypeStruct((s, d), F32), jax.ShapeDtypeStruct((s, d), BF16),
                   jax.ShapeDtypeStruct((1, 128), F32), jax.ShapeDtypeStruct((1, d), F32),
                   jax.ShapeDtypeStruct((1, d), F32)],
        scratch_shapes=[pltpu.VMEM((tm, d), F32)],
        compiler_params=_params(("arbitrary", "arbitrary")),
    )(act, w_down_g, xhat1, ln1_g, ln1_b, ln2_g, ln2_b, target)


def _dact_silu_bwd(dpre2, w_down_g, ab):
    s, d = dpre2.shape
    fs2 = ab.shape[1] // N_CHIPS
    fs = fs2 // 2
    tm = min(TM, s)

    def body(dp_ref, w_ref, ab_ref, dgu_ref):
        d_act = _dot_nt(dp_ref[...], w_ref[...])
        dgu_ref[:, :fs] = (d_act * ab_ref[:, :fs].astype(F32)).astype(BF16)
        dgu_ref[:, fs:] = (d_act * ab_ref[:, fs:].astype(F32)).astype(BF16)

    blk = pl.BlockSpec((tm, fs2), lambda j, i: (i, j))
    return pl.pallas_call(
        body, name="dact_silu_bwd", grid=(N_CHIPS, s // tm),
        in_specs=[pl.BlockSpec((tm, d), lambda j, i: (i, 0)),
                  pl.BlockSpec((fs, d), lambda j, i: (j, 0)), blk],
        out_specs=blk,
        out_shape=jax.ShapeDtypeStruct(ab.shape, BF16),
        compiler_params=_params(("parallel", "parallel")),
    )(dpre2, w_down_g, ab)


def _grad_rows(a, b, after, name, row_blocks=1):
    s, m = a.shape
    n = b.shape[1]
    ms = m // N_CHIPS
    tmw = ms // row_blocks
    tk = min(TK_TOK, s)
    nk = s // tk

    def body(a_ref, b_ref, after_ref, o_ref, acc):
        k = pl.program_id(2)
        _accumulate(acc, lambda: _dot_tn(a_ref[...].astype(BF16), b_ref[...].astype(BF16)), k, nk)

        @pl.when(k == nk - 1)
        def _():
            o_ref[...] = acc[...].astype(BF16)

    return pl.pallas_call(
        body, name=name, grid=(N_CHIPS, row_blocks, nk),
        in_specs=[pl.BlockSpec((tk, tmw), lambda j, r, k: (k, j * row_blocks + r)),
                  pl.BlockSpec((tk, n), lambda j, r, k: (k, 0)), _ANY],
        out_specs=pl.BlockSpec((None, tmw, n), lambda j, r, k: (j, r, 0)),
        out_shape=jax.ShapeDtypeStruct((N_CHIPS, ms, n), BF16),
        scratch_shapes=[pltpu.VMEM((tmw, n), F32)],
        compiler_params=_params(("parallel", "parallel", "arbitrary")),
    )(a, b, after)


def _grad_cols(a, bs, after, name, a_3d=False, row_blocks=2, shard=None):
    s, m = a.shape[-2:]
    n = bs[0].shape[1]
    ns = n // N_CHIPS
    nb = len(bs)
    tmw = m // row_blocks
    tk = min(TK_TOK, s)
    nk = s // tk

    def body(*refs):
        a_ref, b_refs, o_refs, accs = refs[0], refs[1:1 + nb], refs[2 + nb:2 + 2 * nb], refs[2 + 2 * nb:]
        k = pl.program_id(2)
        for b_ref, acc in zip(b_refs, accs):
            _accumulate(acc, lambda b_ref=b_ref: _dot_tn(a_ref[...].astype(BF16), b_ref[...].astype(BF16)), k, nk)

        @pl.when(k == nk - 1)
        def _():
            for o_ref, acc in zip(o_refs, accs):
                o_ref[...] = acc[...].astype(BF16)

    if a_3d:
        a_spec = pl.BlockSpec((None, tk, tmw), lambda j, r, k: (0, k, r))
    else:
        a_spec = pl.BlockSpec((tk, tmw), lambda j, r, k: (k, r))
    return _call_with_adamw(
        body, name, (N_CHIPS, row_blocks, nk),
        [a_spec] + [pl.BlockSpec((tk, ns), lambda j, r, k: (k, j))] * nb + [_ANY],
        [pl.BlockSpec((None, tmw, ns), lambda j, r, k: (j, r, 0))] * nb,
        [jax.ShapeDtypeStruct((N_CHIPS, m, ns), BF16)] * nb,
        [pltpu.VMEM((tmw, ns), F32)] * nb, ("parallel", "parallel", "arbitrary"), (a, *bs, after), shard)


def _dh1_ln_bwd(d_gu, w_gu_g, dpre2, xhat1, rstd1, ln1_g, after):
    s = d_gu.shape[0]
    d = dpre2.shape[1]
    hd = d // 2
    fs = w_gu_g.shape[2]
    tm = min(TM, s)

    def body(dgu_ref, w_ref, dp2_ref, xh_ref, rs_ref, g_ref, after_ref, dpre_ref, gg_ref, gb_ref, acc_lo, acc_hi):
        i, j, half = pl.program_id(0), pl.program_id(1), pl.program_id(2)

        def product():
            return _dot_nt(dgu_ref[...], w_ref[...])

        @pl.when(half == 0)
        def _():
            _accumulate(acc_lo, product, j, N_CHIPS)

        @pl.when(half == 1)
        def _():
            _accumulate(acc_hi, product, j, N_CHIPS)

        @pl.when((j == N_CHIPS - 1) & (half == 1))
        def _():
            @pl.when(i == 0)
            def _():
                gg_ref[...] = jnp.zeros_like(gg_ref)
                gb_ref[...] = jnp.zeros_like(gb_ref)

            def rows_fn(rows):
                dh = jnp.concatenate([acc_lo[rows, :], acc_hi[rows, :]], axis=1) + ALPHA * dp2_ref[rows, :]
                xhat = xh_ref[rows, :]
                dpre_ref[rows, :] = _ln_bwd(dh, xhat, rs_ref[rows, :], g_ref[...])
                gg_ref[...] += jnp.sum(dh * xhat, axis=0, keepdims=True)
                gb_ref[...] += jnp.sum(dh, axis=0, keepdims=True)

            _for_row_chunks(tm, rows_fn)

    row = pl.BlockSpec((tm, d), lambda i, j, h: (i, 0))
    vec = pl.BlockSpec((1, d), lambda i, j, h: (0, 0))
    act_blk = pl.BlockSpec((tm, fs), lambda i, j, h: (i, j))
    w_blk = pl.BlockSpec((None, hd, fs), lambda i, j, h: (j, h, 0))
    return pl.pallas_call(
        body, name="dh1_ln_bwd", grid=(s // tm, N_CHIPS, 2),
        in_specs=[act_blk, w_blk, row, row, pl.BlockSpec((tm, 1), lambda i, j, h: (i, 0)), _VMEM, _ANY],
        out_specs=[row, vec, vec],
        out_shape=[jax.ShapeDtypeStruct((s, d), F32), jax.ShapeDtypeStruct((1, d), F32),
                   jax.ShapeDtypeStruct((1, d), F32)],
        scratch_shapes=[pltpu.VMEM((tm, hd), F32)] * 2,
        compiler_params=_params(("arbitrary", "arbitrary", "arbitrary")),
    )(d_gu, w_gu_g, dpre2, xhat1, rstd1, ln1_g, after)


def _dmixed_rms_bwd(dpre1, w_out_g, ac, rstd, g_ac):
    s, d = dpre1.shape
    hd = d // 2
    tm = min(TM, s)

    def body(dp_ref, w_ref, ac_ref, rs_ref, g_ref, dac_ref, gg_ref):
        i = pl.program_id(1)
        dm = _dot_nt(dp_ref[...].astype(BF16), w_ref[...])
        pre = ac_ref[...]
        r = rs_ref[...]
        gdm = dm * g_ref[...]
        dac_ref[...] = r * gdm - pre * (r * r * r) * jnp.mean(gdm * pre, axis=-1, keepdims=True)
        gg = jnp.sum(dm * pre * r, axis=0, keepdims=True)

        @pl.when(i == 0)
        def _():
            gg_ref[...] = gg

        @pl.when(i > 0)
        def _():
            gg_ref[...] += gg

    return pl.pallas_call(
        body, name="dmixed_rms_bwd", grid=(2, s // tm),
        in_specs=[pl.BlockSpec((tm, d), lambda h, i: (i, 0)),
                  pl.BlockSpec((hd, d), lambda h, i: (h, 0)),
                  pl.BlockSpec((tm, hd), lambda h, i: (i, h)),
                  pl.BlockSpec((None, tm, 1), lambda h, i: (h, i, 0)),
                  pl.BlockSpec((1, hd), lambda h, i: (0, h))],
        out_specs=[pl.BlockSpec((tm, hd), lambda h, i: (i, h)),
                   pl.BlockSpec((1, hd), lambda h, i: (0, h))],
        out_shape=[jax.ShapeDtypeStruct((s, d), F32), jax.ShapeDtypeStruct((1, d), F32)],
        compiler_params=_params(("arbitrary", "arbitrary")),
    )(dpre1, w_out_g, ac, rstd, g_ac)


def _attention_bwd(proj, d_ac, cos_t, sin_t, sinks, after, shard):
    s = proj.shape[0]
    qw = GROUP * N_KV_HEADS * HEAD_DIM
    kvw = N_KV_HEADS * HEAD_DIM
    nb = s // WINDOW
    nq = GROUP * N_KV_HEADS

    def body(cur_ref, prev_ref, do_ref, cos_ref, sin_ref, cosp_ref, sinp_ref, sinks_ref, after_ref,
             dq_ref, dcur_ref, dprev_ref, dsink_ref):
        n = pl.program_id(0)
        first = n == 0
        q, k_all, v_all, cos_q, sin_q = _roped_qkv(cur_ref, prev_ref, cos_ref, sin_ref, cosp_ref, sinp_ref, qw, kvw)
        kk2s = [_pair_operand(k_all, h) for h in range(N_KV_HEADS)]
        vv2s = [_pair_operand(v_all, h) for h in range(N_KV_HEADS)]
        qps, probs, p_sinks = _all_probs(q, kk2s, first, sinks_ref)
        dops = [do_ref[:, pair * PAIR:(pair + 1) * PAIR].astype(BF16) for pair in range(N_PAIRS)]
        d_probs = jnp.concatenate([_dot_nt(dops[pair], vv2s[pair // (GROUP // 2)]) for pair in range(N_PAIRS)], axis=0)
        d_s, ds_sinks = [], []
        for t in range(2):
            cols = slice(t * KEYS, (t + 1) * KEYS)
            delta = jnp.sum(probs[:, cols] * d_probs[:, cols], axis=1, keepdims=True)
            d_s.append(probs[:, cols] * (d_probs[:, cols] - delta))
            ds_sinks.append(-p_sinks[t] * delta)
        d_s = jnp.concatenate(d_s, axis=1).astype(BF16)
        probs = probs.astype(BF16)
        dq_parts, dk_tiles, dv_tiles, dsink_parts = [], [], [], []
        for h in range(N_KV_HEADS):
            dkk2, dvv2 = None, None
            for p in range(GROUP // 2):
                pair = (GROUP // 2) * h + p
                rows = slice(pair * WINDOW, (pair + 1) * WINDOW)
                dq_parts.append(_dot(d_s[rows], kk2s[h]) * ATTN_SCALE)
                dk_term = _dot_tn(d_s[rows], qps[pair])
                dv_term = _dot_tn(probs[rows], dops[pair])
                dkk2 = dk_term if dkk2 is None else dkk2 + dk_term
                dvv2 = dv_term if dvv2 is None else dvv2 + dv_term
                dsink_parts.extend([jnp.sum(ds_sinks[t][rows], axis=0, keepdims=True) for t in range(2)])
            dk_tiles.append(_pair_grad(dkk2, h))
            dv_tiles.append(_pair_grad(dvv2, h))
        dq_ref[...] = _rope(jnp.concatenate(dq_parts, axis=1), cos_q, sin_q, -1.0)
        dk = jnp.concatenate([dk_tiles[0] + dk_tiles[1], dk_tiles[2] + dk_tiles[3]], axis=1)
        dv = jnp.concatenate([dv_tiles[0] + dv_tiles[1], dv_tiles[2] + dv_tiles[3]], axis=1)
        dprev_ref[...] = jnp.concatenate([dk[:WINDOW], dv[:WINDOW]], axis=1)
        dcur_ref[...] = jnp.concatenate([dk[WINDOW:], dv[WINDOW:]], axis=1)
        dsink = jnp.concatenate(dsink_parts, axis=1)

        @pl.when(first)
        def _():
            dsink_ref[...] = dsink

        @pl.when(n > 0)
        def _():
            dsink_ref[...] += dsink

    tbl = pl.BlockSpec((WINDOW, kvw), lambda n: (n, 0))
    tbl_prev = pl.BlockSpec((WINDOW, kvw), lambda n: (jnp.maximum(n - 1, 0), 0))
    kv_blk = pl.BlockSpec((WINDOW, 2 * kvw), lambda n: (n, 0))
    return _call_with_adamw(
        body, "attention_bwd", (nb,),
        [pl.BlockSpec((WINDOW, qw + 2 * kvw), lambda n: (n, 0)),
         pl.BlockSpec((WINDOW, 2 * kvw), lambda n: (jnp.maximum(n - 1, 0), (qw // (2 * kvw)))),
         pl.BlockSpec((WINDOW, qw), lambda n: (n, 0)),
         tbl, tbl, tbl_prev, tbl_prev, _VMEM, _ANY],
        [pl.BlockSpec((WINDOW, qw), lambda n: (n, 0)), kv_blk, kv_blk, pl.BlockSpec((1, nq), lambda n: (0, 0))],
        [jax.ShapeDtypeStruct((s, qw), F32), jax.ShapeDtypeStruct((s, 2 * kvw), F32),
         jax.ShapeDtypeStruct((s, 2 * kvw), F32), jax.ShapeDtypeStruct((1, nq), F32)],
        [], ("arbitrary",), (proj, proj, d_ac, cos_t, sin_t, cos_t, sin_t, sinks, after), shard)


def _dproj_assemble(proj, d_ac, dq, dkv_cur, dkv_prev, cos_t, sin_t, cw_full):
    s, in_w = proj.shape
    cw = dq.shape[1]
    kvw = N_KV_HEADS * HEAD_DIM
    blk_w = in_w // 3
    tb = WINDOW
    nb = s // tb

    def body(lo_ref, hi_ref, lo_p_ref, hi_p_ref, lo_n_ref, hi_n_ref, dconv_ref, dconv_n_ref,
             dq_ref, dcur_ref, dprev_n_ref, cos_ref, sin_ref, cw_ref, dproj_ref, gcw_ref):
        i = pl.program_id(0)
        last = i == nb - 1
        c_gate, b_gate, u = _split_cbu(lo_ref[...], hi_ref[...], cw)
        c_p, _, u_p = _split_cbu(lo_p_ref[...], hi_p_ref[...], cw)
        _, b_n, _ = _split_cbu(lo_n_ref[...], hi_n_ref[...], cw)
        z = c_gate * u
        z_p = jnp.where(i == 0, 0.0, c_p * u_p)
        z1 = _shift_down(z, z_p, 1)
        z2 = _shift_down(z, z_p, 2)
        w0, w1, w2 = _conv_taps(cw_ref)
        y = w0 * z2 + w1 * z1 + w2 * z
        d_conv = dconv_ref[...]
        d_b = d_conv * y
        d_y = d_conv * b_gate
        d_y_n = jnp.where(last, 0.0, dconv_n_ref[...] * b_n[:dconv_n_ref.shape[0]])
        d_z = w2 * d_y + w1 * _shift_up(d_y, d_y_n, 1) + w0 * _shift_up(d_y, d_y_n, 2)
        d_c = d_z * u
        d_u = d_z * c_gate
        gcw = jnp.concatenate([jnp.sum(d_y * z2, axis=0, keepdims=True), jnp.sum(d_y * z1, axis=0, keepdims=True),
                               jnp.sum(d_y * z, axis=0, keepdims=True)], axis=0)

        @pl.when(i == 0)
        def _():
            gcw_ref[...] = gcw

        @pl.when(i > 0)
        def _():
            gcw_ref[...] += gcw

        dkv = dcur_ref[...] + jnp.where(last, 0.0, dprev_n_ref[...])
        dk = _rope(dkv[:, :kvw], cos_ref[...], sin_ref[...], -1.0)
        dproj_ref[...] = jnp.concatenate([dq_ref[...], dk, dkv[:, kvw:], d_c, d_b, d_u], axis=1).astype(BF16)

    prev_halo = lambda i: jnp.maximum(i * (tb // HALO_ROWS) - 1, 0)
    next_halo = lambda i: jnp.minimum((i + 1) * (tb // HALO_ROWS), s // HALO_ROWS - 1)
    next8 = lambda i: jnp.minimum((i + 1) * (tb // 8), s // 8 - 1)
    nxt = lambda i: jnp.minimum(i + 1, nb - 1)
    return pl.pallas_call(
        body, name="dproj_assemble", grid=(nb,),
        in_specs=[pl.BlockSpec((tb, blk_w), lambda i: (i, 1)),
                  pl.BlockSpec((tb, blk_w), lambda i: (i, 2)),
                  pl.BlockSpec((HALO_ROWS, blk_w), lambda i: (prev_halo(i), 1)),
                  pl.BlockSpec((HALO_ROWS, blk_w), lambda i: (prev_halo(i), 2)),
                  pl.BlockSpec((HALO_ROWS, blk_w), lambda i: (next_halo(i), 1)),
                  pl.BlockSpec((HALO_ROWS, blk_w), lambda i: (next_halo(i), 2)),
                  pl.BlockSpec((tb, cw), lambda i: (i, 1)),
                  pl.BlockSpec((8, cw), lambda i: (next8(i), 1)),
                  pl.BlockSpec((tb, cw), lambda i: (i, 0)),
                  pl.BlockSpec((tb, 2 * kvw), lambda i: (i, 0)),
                  pl.BlockSpec((tb, 2 * kvw), lambda i: (nxt(i), 0)),
                  pl.BlockSpec((tb, kvw), lambda i: (i, 0)),
                  pl.BlockSpec((tb, kvw), lambda i: (i, 0)),
                  _VMEM],
        out_specs=[pl.BlockSpec((tb, in_w), lambda i: (i, 0)),
                   pl.BlockSpec((3, cw), lambda i: (0, 0))],
        out_shape=[jax.ShapeDtypeStruct((s, in_w), BF16), jax.ShapeDtypeStruct((3, cw), F32)],
        compiler_params=_params(("arbitrary",)),
    )(proj, proj, proj, proj, proj, proj, d_ac, d_ac, dq, dkv_cur, dkv_prev, cos_t, sin_t, cw_full)


def _dx(d_proj, w_in_g, dpre1, after, shard):
    s, in_w = d_proj.shape
    ns, d, ncol = w_in_g.shape
    tm = min(TM, s)

    def body(dp_ref, w_ref, r_ref, after_ref, o_ref, acc):
        j = pl.program_id(1)
        _accumulate(acc, lambda: _dot_nt(dp_ref[...], w_ref[...]), j, ns)

        @pl.when(j == ns - 1)
        def _():
            o_ref[...] = acc[...] + ALPHA * r_ref[...]

    return _call_with_adamw(
        body, "dx", (s // tm, ns),
        [pl.BlockSpec((tm, ncol), lambda i, j: (i, j)),
         pl.BlockSpec((None, d, ncol), lambda i, j: (j, 0, 0)),
         pl.BlockSpec((tm, d), lambda i, j: (i, 0)), _ANY],
        [pl.BlockSpec((None, tm, d), lambda i, j: (0, i, 0))], [jax.ShapeDtypeStruct((1, s, d), F32)],
        [pltpu.VMEM((tm, d), F32)], ("parallel", "arbitrary"), (d_proj, w_in_g, dpre1, after), shard)


def kernel(x, positions, w_in, conv_w, sinks, g_attn, g_conv, w_out, ln1_g, ln1_b, w_gate, w_up, w_down, ln2_g, ln2_b, loss_target, m_w_in, m_conv_w, m_sinks, m_g_attn, m_g_conv, m_w_out, m_ln1_g, m_ln1_b, m_w_gate, m_w_up, m_w_down, m_ln2_g, m_ln2_b, v_w_in, v_conv_w, v_sinks, v_g_attn, v_g_conv, v_w_out, v_ln1_g, v_ln1_b, v_w_gate, v_w_up, v_w_down, v_ln2_g, v_ln2_b):
    s = x.shape[1]
    d = x.shape[2]

    chip_vec = _chip_id(lax.axis_index("x"), lax.axis_index("y")).astype(jnp.int32).reshape(1)
    wnames = ["w_in", "w_out", "w_gu", "w_down"]
    buf_in = _cast_weight(w_in, chip_vec, chip_vec, "cast_w_in")
    flight_in, token_in = _gather_start([buf_in], chip_vec, "gather_start_w_in")
    cw_buf = lax.dynamic_update_slice(jnp.zeros((N_CHIPS,) + conv_w.shape[1:], F32), conv_w, (chip_vec[0], 0, 0))
    cw_flight = _flight_start("conv_w_start", [cw_buf], _conv_w_plan(), 3, token_in)
    started = cw_flight[2][0]
    buf_gu = _cast_weight(w_gate, chip_vec, started, "cast_w_gate", 0, 2)
    buf_gu = _cast_weight(w_up, chip_vec, buf_gu, "cast_w_up", 1, 2)
    bufs = [_cast_weight(w_out, chip_vec, started, "cast_w_out"), buf_gu,
            _cast_weight(w_down, chip_vec, started, "cast_w_down")]
    flights_rest, token = _gather_start(bufs, token_in, "gather_start_rest")
    flights = flight_in + flights_rest

    def gathered(i, after):
        send_sems, recv_sems, buf = flights[i]
        buf = _gather_wait(send_sems, recv_sems, buf, after, "gather_wait_" + wnames[i])
        return _sibling_fill(buf, "sibling_fill_" + wnames[i])

    g_ac = jnp.concatenate([g_attn, g_conv], axis=1)

    proj_own = _in_proj(x, _after(flights[0][2], token), chip_vec, 1, None, "in_proj_own")
    cos_t, sin_t = _rope_tables(positions.reshape(s, 1) + token[0:1, 0:1].astype(jnp.int32))
    w_in_g = gathered(0, _after(cos_t, proj_own))
    proj = _in_proj(x, w_in_g, chip_vec + 1, N_CHIPS - 1, proj_own, "in_proj_rest")
    send_sems, recv_sems, buf_out = flights[1]
    buf_out = _gather_wait(send_sems, recv_sems, buf_out, proj, "gather_wait_w_out")
    fill_out = _flight_start("fill_start_w_out", [buf_out], _fill_plan(1), 3, chip_vec)
    attn = _attention_fwd(_after(proj, fill_out[2][0]), cos_t, sin_t, sinks)
    (cw_full,) = _flight_wait("conv_w_wait", cw_flight, _conv_w_plan(), attn)
    mixed, ac, rstd_ac = _conv_norm(proj, attn, cw_full, g_ac)
    (w_out_g,) = _flight_wait("fill_wait_w_out", fill_out, _fill_plan(1), mixed)
    w_out_full = w_out_g.reshape(d, d)
    xhat1, h1, rstd1 = _out_proj_ln(mixed, w_out_full, x, ln1_g, ln1_b)
    send_sems, recv_sems, buf_gu = flights[2]
    buf_gu = _gather_wait(send_sems, recv_sems, buf_gu, h1, "gather_wait_w_gu")
    fill_gu = _flight_start("fill_start_w_gu", [buf_gu], _fill_plan(1), 3, chip_vec)
    own = _gate_up(h1, fill_gu[2][0], chip_vec, 1, None, "gate_up_own")
    (w_gu_g,) = _flight_wait("fill_wait_w_gu", fill_gu, _fill_plan(1), own[0])
    act, ab = _gate_up(h1, w_gu_g, chip_vec + 1, N_CHIPS - 1, own, "gate_up_rest")
    w_down_full = gathered(3, act).reshape(-1, d)
    dpre2, dpre2_16, loss_part, g_ln2_g, g_ln2_b = _down_ln_loss(act, w_down_full, xhat1, ln1_g, ln1_b, ln2_g, ln2_b,
                                                                 loss_target)

    cvec = lax.axis_index("c").astype(jnp.int32).reshape(1)

    def exchange_begin(parts, nme):
        bufs = []
        for part in parts:
            ns, r, cdim = part.shape
            bufs.extend([part, lax.empty((ns, r // 2, cdim), part.dtype)])
        return _flight_start("exchange_start_" + nme, bufs, _exchange_plan(len(parts)), len(parts), cvec)

    def exchange_end(flight, n_parts, after, nme):
        bufs = _flight_wait("exchange_wait_" + nme, flight, _exchange_plan(n_parts), after)
        return [(bufs[2 * w], bufs[2 * w + 1]) for w in range(n_parts)]

    def scatter_begin(part, got, nme):
        return _scatter_start(_add_halves(part, got, cvec, "add_halves_" + nme), "scatter_start_" + nme)

    d_gu = _dact_silu_bwd(dpre2_16, w_down_full, ab)
    p_down = _grad_rows(act, dpre2_16, d_gu, "grad_w_down")
    x_down = exchange_begin([p_down], "w_down")
    (p_gu,) = _grad_cols(h1, [d_gu], x_down[2][0], "grad_w_gate_up")
    ((p_down, got),) = exchange_end(x_down, 1, p_gu, "w_down")
    f_down = scatter_begin(p_down, got, "w_down")
    x_gu = exchange_begin([_after(p_gu, f_down[2])], "w_gu")
    dpre1, g_ln1_g, g_ln1_b = _dh1_ln_bwd(d_gu, w_gu_g, dpre2, xhat1, rstd1, ln1_g, x_gu[2][0])
    ((p_gu, got),) = exchange_end(x_gu, 1, dpre1, "w_gu")
    f_gu = scatter_begin(p_gu, got, "w_gu")
    d_ac, g_g_ac = _dmixed_rms_bwd(_after(dpre1, f_gu[2]), w_out_full, ac, rstd_ac, g_ac)
    pos_vec = jnp.concatenate([chip_vec, cvec])
    sums, land = _scatter_wait(*f_down, d_ac, "scatter_wait_w_down")
    c_down = _flight_start("complete_start_w_down", [sums, land], _complete_plan(1), 4, cvec)
    p_out = _grad_rows(mixed, dpre1, c_down[2][1], "grad_w_out")
    x_out = exchange_begin([p_out], "w_out")
    sums, land = _flight_wait("complete_wait_w_down", c_down, _complete_plan(1), x_out[2][0])
    dq, dkv_cur, dkv_prev, g_sinks, *new_w_down = _attention_bwd(
        proj, d_ac, cos_t, sin_t, sinks, x_out[2][0], (w_down, m_w_down, v_w_down, land, sums, pos_vec, 0))
    ((p_out, got),) = exchange_end(x_out, 1, dq, "w_out")
    f_out = scatter_begin(p_out, got, "w_out")
    sums, land = _scatter_wait(*f_gu, f_out[2], "scatter_wait_w_gu")
    c_gu = _flight_start("complete_start_w_gu", [sums, land], _complete_plan(1), 4, cvec)
    d_proj, g_conv_w = _dproj_assemble(proj, _after(d_ac, c_gu[2][1]), dq, dkv_cur, dkv_prev, cos_t, sin_t, cw_full)
    red = _allreduce_small(g_ln2_g, g_ln2_b, g_ln1_g, g_ln1_b, g_g_ac, g_conv_w, g_sinks, loss_part, d_proj)
    sums_gu, land_gu = _flight_wait("complete_wait_w_gu", c_gu, _complete_plan(1), red)
    p_in, *new_w_gate = _grad_cols(x, [d_proj], red, "grad_w_in", a_3d=True,
                                   shard=(w_gate, m_w_gate, v_w_gate, land_gu, sums_gu, pos_vec, 0))
    x_in = exchange_begin([p_in], "w_in")
    sums, land = _scatter_wait(*f_out, x_in[2][0], "scatter_wait_w_out")
    c_out = _flight_start("complete_start_w_out", [sums, land], _complete_plan(1), 4, cvec)
    new_w_up = _adamw_shard(w_up, m_w_up, v_w_up, _after(land_gu, c_out[2][1]), sums_gu, pos_vec, "adamw_w_up", 1)
    ((p_in, got),) = exchange_end(x_in, 1, new_w_up[0], "w_in")
    f_in = scatter_begin(p_in, got, "w_in")
    (grad_x,) = _dx(d_proj, w_in_g, dpre1, f_in[2], None)

    big = {"w_down": new_w_down, "w_gate": new_w_gate, "w_up": new_w_up}
    sums, land = _scatter_wait(*f_in, grad_x, "scatter_wait_w_in")
    c_in = _flight_start("complete_start_w_in", [sums, land], _complete_plan(1), 4, cvec)
    sums, land = _flight_wait("complete_wait_w_out", c_out, _complete_plan(1), c_in[2][1])
    big["w_out"] = _adamw_shard(w_out, m_w_out, v_w_out, land, sums, pos_vec, "adamw_w_out")
    sums, land = _flight_wait("complete_wait_w_in", c_in, _complete_plan(1), big["w_out"][0])
    big["w_in"] = _adamw_shard(w_in, m_w_in, v_w_in, land, sums, pos_vec, "adamw_w_in")
    small = _adamw_small(red, {
        "sinks": (sinks, m_sinks, v_sinks), "g_attn": (g_attn, m_g_attn, v_g_attn),
        "g_conv": (g_conv, m_g_conv, v_g_conv), "ln1_g": (ln1_g, m_ln1_g, v_ln1_g),
        "ln1_b": (ln1_b, m_ln1_b, v_ln1_b), "ln2_g": (ln2_g, m_ln2_g, v_ln2_g),
        "ln2_b": (ln2_b, m_ln2_b, v_ln2_b), "conv_w": (conv_w, m_conv_w, v_conv_w)})
    res = {**big, **small}
    order = ["w_in", "conv_w", "sinks", "g_attn", "g_conv", "w_out", "ln1_g", "ln1_b", "w_gate", "w_up", "w_down",
             "ln2_g", "ln2_b"]
    loss = red[6, d // 2 + 128]
    return (loss, grad_x, *[res[n][0] for n in order], *[res[n][1] for n in order],
            *[res[n][2] for n in order], *[res[n][3] for n in order])
```

```python
import functools

import numpy as np
import jax
import jax.numpy as jnp
from jax import lax
from jax.experimental import pallas as pl
from jax.experimental.pallas import tpu as pltpu

F32 = jnp.float32
BF16 = jnp.bfloat16
MESH = pl.DeviceIdType.MESH

HEAD_DIM = 64
N_KV_HEADS = 4
GROUP = 4
WINDOW = 128
ROT_DIM = 16
ROPE_THETA = 500000.0
ATTN_SCALE = HEAD_DIM ** -0.5
ALPHA = 2.0 ** 0.25
LN_EPS = 1e-5
RMS_EPS = 1e-6
ADAM_LR = 0.001
ADAM_B1 = 0.9
ADAM_B2 = 0.999
ADAM_EPS = 1e-08
ADAM_WD = 0.01
ADAM_STEP = 10
N_CHIPS = 4
NEG_BIG = -1e30

V7X_VMEM_BYTES = 64 * 1024 * 1024
VMEM_LIMIT = V7X_VMEM_BYTES - 6 * 1024 * 1024

TM = 512
TK_TOK = 1024
TB_CONV = 256
TR_ELT = 256
ROW_CHUNK = 128
HALO_ROWS = 16


def _params(sem):
    return pltpu.CompilerParams(dimension_semantics=sem, vmem_limit_bytes=VMEM_LIMIT)


def _row_tile(rows, target):
    best = None
    for t in range(16, min(rows, target) + 1, 16):
        if rows % t == 0:
            best = t
    assert best is not None, (rows, target)
    return best


def _dot(a, b):
    return jnp.dot(a, b, preferred_element_type=F32)


def _dot_nt(a, b):
    return lax.dot_general(a, b, (((1,), (1,)), ((), ())), preferred_element_type=F32)


def _dot_tn(a, b):
    return lax.dot_general(a, b, (((0,), (0,)), ((), ())), preferred_element_type=F32)


def _mesh_pos():
    x, y, c = lax.axis_index("x"), lax.axis_index("y"), lax.axis_index("c")
    chips = [(1 - x, y), (x, 1 - y), (1 - x, 1 - y)]
    return x, y, c, chips


def _chip_id(px, py):
    return 2 * px + py


def _rope(t, cos, sgn_sin, sign):
    w = t.shape[1]
    lane = lax.broadcasted_iota(jnp.int32, t.shape, 1) & (HEAD_DIM - 1)
    partner = jnp.where(lane < ROT_DIM // 2, pltpu.roll(t, w - ROT_DIM // 2, 1), pltpu.roll(t, ROT_DIM // 2, 1))
    return t * cos + sign * (partner * sgn_sin)


def _tile_lanes(t, n):
    return jnp.concatenate([t] * n, axis=1)


def _sigmoid(g):
    return 1.0 / (1.0 + jnp.exp(-g))


def _for_row_chunks(n_rows, fn):
    def step(r, carry):
        fn(pl.ds(pl.multiple_of(r * ROW_CHUNK, ROW_CHUNK), ROW_CHUNK))
        return carry

    lax.fori_loop(0, n_rows // ROW_CHUNK, step, 0)


def _accumulate(acc, make_val, k, nk):
    if nk == 1:
        acc[...] = make_val()
        return

    @pl.when(k == 0)
    def _():
        acc[...] = jnp.zeros_like(acc)

    acc[...] += make_val()


def _ln_fwd(pre):
    mu = jnp.mean(pre, axis=-1, keepdims=True)
    cen = pre - mu
    var = jnp.mean(cen * cen, axis=-1, keepdims=True)
    rstd = lax.rsqrt(var + LN_EPS)
    return cen * rstd, rstd


def _ln_bwd(dy, xhat, rstd, g):
    dxhat = dy * g
    m1 = jnp.mean(dxhat, axis=-1, keepdims=True)
    m2 = jnp.mean(dxhat * xhat, axis=-1, keepdims=True)
    return rstd * (dxhat - m1 - xhat * m2)


def _cast_weight(w, chip_vec, after, name, col_block=0, n_col_blocks=1):
    _, r, c = w.shape
    tr = _row_tile(r, TR_ELT)

    def body(chip_ref, w_ref, after_ref, o_ref):
        o_ref[...] = w_ref[...].astype(BF16)

    grid_spec = pltpu.PrefetchScalarGridSpec(
        num_scalar_prefetch=1, grid=(r // tr,),
        in_specs=[pl.BlockSpec((None, tr, c), lambda i, chip_ref: (0, i, 0)), _ANY],
        out_specs=pl.BlockSpec((None, tr, c), lambda i, chip_ref: (chip_ref[0], i, col_block)))
    return pl.pallas_call(
        body, name=name, grid_spec=grid_spec,
        out_shape=jax.ShapeDtypeStruct((N_CHIPS, r, n_col_blocks * c), BF16),
        input_output_aliases={2: 0} if col_block else {},
        compiler_params=_params(("parallel",)),
    )(chip_vec, w, after)


_HBM = pl.BlockSpec(memory_space=pltpu.HBM)
_VMEM = pl.BlockSpec(memory_space=pltpu.VMEM)


_SEM = pl.BlockSpec(memory_space=pltpu.SEMAPHORE)
_ANY = pl.BlockSpec(memory_space=pl.ANY)
_EFFECT = pltpu.SideEffectType.DATAFLOW_SIDE_EFFECTING


def _chip_copy(buf, k, chip_of_src, half_rows, send_sems, recv_sems, to):
    part = buf.at[chip_of_src, half_rows]
    return pltpu.make_async_remote_copy(
        src_ref=part, dst_ref=part, send_sem=send_sems.at[k], recv_sem=recv_sems.at[k], device_id=to, device_id_type=MESH)


def _half_rows(buf, which):
    hr = buf.shape[1] // 2
    return pl.ds(which * hr, hr)


def _after(value, dep):
    return lax.optimization_barrier((value, dep))[0]


def _flight_start(name, bufs, plan, n_sems, after):
    n = len(bufs)

    def body(*refs):
        sends, _ = plan(refs[:n], refs[n + 1], refs[n + 2])
        for cp in sends:
            cp.start()

    outs = pl.pallas_call(
        body, name=name,
        in_specs=[_HBM] * n + [_ANY], out_specs=[_SEM, _SEM] + [_HBM] * n,
        out_shape=[pltpu.SemaphoreType.DMA((n_sems,))] * 2 + [pltpu.HBM(b.shape, b.dtype) for b in bufs],
        input_output_aliases={i: 2 + i for i in range(n)},
        compiler_params=pltpu.CompilerParams(has_side_effects=_EFFECT),
    )(*[pltpu.with_memory_space_constraint(b, pltpu.HBM) for b in bufs], after)
    return outs[0], outs[1], list(outs[2:])


def _flight_wait(name, flight, plan, after):
    send_sems, recv_sems, bufs = flight
    n = len(bufs)

    def body(*refs):
        sends, recvs = plan(refs[:n], refs[n], refs[n + 1])
        for cp in sends:
            cp.wait_send()
        for cp in recvs:
            cp.wait_recv()

    outs = pl.pallas_call(
        body, name=name,
        in_specs=[_HBM] * n + [_SEM, _SEM, _ANY], out_specs=[_HBM] * n,
        out_shape=[pltpu.HBM(b.shape, b.dtype) for b in bufs],
        input_output_aliases={i: i for i in range(n)},
        compiler_params=pltpu.CompilerParams(has_side_effects=_EFFECT),
    )(*bufs, send_sems, recv_sems, after)
    return list(outs)


def _fill_plan(n_bufs):
    def plan(refs, send_sems, recv_sems):
        x, y, c, chips = _mesh_pos()
        sibling = (x, y, 1 - c)
        sends, recvs = [], []
        for w in range(n_bufs):
            for k, chip in enumerate(chips):
                slot = _chip_id(*chip)
                sends.append(_chip_copy(refs[w], 3 * w + k, slot, _half_rows(refs[w], c), send_sems, recv_sems, sibling))
                recvs.append(_chip_copy(refs[w], 3 * w + k, slot, _half_rows(refs[w], 1 - c), send_sems, recv_sems,
                                        sibling))
        return sends, recvs
    return plan


def _conv_w_plan():
    def plan(refs, send_sems, recv_sems):
        x, y, c, chips = _mesh_pos()
        me = _chip_id(x, y)
        (buf,) = refs
        sends, recvs = [], []
        for k, chip in enumerate(chips):
            for slot, into in ((me, sends), (_chip_id(*chip), recvs)):
                into.append(pltpu.make_async_remote_copy(
                    src_ref=buf.at[slot], dst_ref=buf.at[slot], send_sem=send_sems.at[k], recv_sem=recv_sems.at[k],
                    device_id=(*chip, c), device_id_type=MESH))
        return sends, recvs
    return plan


def _exchange_plan(n_parts):
    def plan(refs, send_sems, recv_sems):
        x, y, c, _ = _mesh_pos()
        copies = []
        for w in range(n_parts):
            part, got = refs[2 * w], refs[2 * w + 1]
            hr = got.shape[1]
            copies.append(pltpu.make_async_remote_copy(
                src_ref=part.at[:, pl.ds((1 - c) * hr, hr)], dst_ref=got, send_sem=send_sems.at[w],
                recv_sem=recv_sems.at[w], device_id=(x, y, 1 - c), device_id_type=MESH))
        return copies, copies
    return plan


def _gather_start(bufs, after, name):
    n = len(bufs)

    def body(*refs):
        ins = refs[:n]
        sends, recvs = refs[n + 1:2 * n + 1], refs[2 * n + 1:3 * n + 1]
        token = refs[4 * n + 1]
        x, y, c, chips = _mesh_pos()
        me = _chip_id(x, y)
        for w in range(n):
            for k, chip in enumerate(chips):
                _chip_copy(ins[w], k, me, _half_rows(ins[w], c), sends[w], recvs[w], (*chip, c)).start()
        token[...] = jnp.zeros_like(token)

    outs = pl.pallas_call(
        body, name=name,
        in_specs=[_HBM] * n + [_ANY],
        out_specs=[_SEM] * (2 * n) + [_HBM] * n + [_VMEM],
        out_shape=[pltpu.SemaphoreType.DMA((3,))] * (2 * n) + [pltpu.HBM(b.shape, b.dtype) for b in bufs]
        + [jax.ShapeDtypeStruct((8, 128), F32)],
        input_output_aliases={w: 2 * n + w for w in range(n)},
        compiler_params=pltpu.CompilerParams(has_side_effects=_EFFECT),
    )(*[pltpu.with_memory_space_constraint(b, pltpu.HBM) for b in bufs], after)
    return [(outs[w], outs[n + w], outs[2 * n + w]) for w in range(n)], outs[3 * n]


def _gather_wait(send_sems, recv_sems, buf, after, name):
    def body(buf_ref, send_ref, recv_ref, after_ref, out_ref):
        x, y, c, chips = _mesh_pos()
        me = _chip_id(x, y)
        for k, chip in enumerate(chips):
            _chip_copy(buf_ref, k, me, _half_rows(buf_ref, c), send_ref, recv_ref, (*chip, c)).wait_send()
        for k, chip in enumerate(chips):
            _chip_copy(buf_ref, k, _chip_id(*chip), _half_rows(buf_ref, c), send_ref, recv_ref, (*chip, c)).wait_recv()

    return pl.pallas_call(
        body, name=name,
        in_specs=[_HBM, _SEM, _SEM, _ANY], out_specs=_HBM,
        out_shape=pltpu.HBM(buf.shape, buf.dtype),
        input_output_aliases={0: 0},
        compiler_params=pltpu.CompilerParams(has_side_effects=_EFFECT),
    )(buf, send_sems, recv_sems, after)


def _sibling_fill(buf, name, own_too=False):
    n_copies = 4 if own_too else 3

    def body(buf_ref, out_ref, send_sems, recv_sems):
        x, y, c, chips = _mesh_pos()
        sibling = (x, y, 1 - c)
        slots = [_chip_id(*chip) for chip in chips] + ([_chip_id(x, y)] if own_too else [])
        copies = []
        for k, slot in enumerate(slots):
            cp = _chip_copy(out_ref, k, slot, _half_rows(out_ref, c), send_sems, recv_sems, sibling)
            cp.start()
            copies.append(cp)
        for k, slot in enumerate(slots):
            _chip_copy(out_ref, k, slot, _half_rows(out_ref, 1 - c), send_sems, recv_sems, sibling).wait_recv()
        for cp in copies:
            cp.wait_send()

    return pl.pallas_call(
        body, name=name,
        in_specs=[_HBM], out_specs=_HBM,
        out_shape=jax.ShapeDtypeStruct(buf.shape, buf.dtype),
        input_output_aliases={0: 0},
        scratch_shapes=[pltpu.SemaphoreType.DMA((n_copies,)), pltpu.SemaphoreType.DMA((n_copies,))],
    )(buf)


def _allgather_conv_w(cw):
    _, kw, cs = cw.shape

    def body(cw_ref, out_ref, send_sems, recv_sems):
        x, y, c, chips = _mesh_pos()
        me = _chip_id(x, y)
        out_ref[pl.ds(me, 1)] = cw_ref[...]
        copies = []
        for k, chip in enumerate(chips):
            cp = pltpu.make_async_remote_copy(
                src_ref=cw_ref.at[0], dst_ref=out_ref.at[me], send_sem=send_sems.at[k], recv_sem=recv_sems.at[k],
                device_id=(*chip, c), device_id_type=MESH)
            cp.start()
            copies.append(cp)
        for k, chip in enumerate(chips):
            pltpu.make_async_remote_copy(
                src_ref=cw_ref.at[0], dst_ref=out_ref.at[_chip_id(*chip)], send_sem=send_sems.at[k],
                recv_sem=recv_sems.at[k], device_id=(*chip, c), device_id_type=MESH).wait_recv()
        for cp in copies:
            cp.wait_send()

    return pl.pallas_call(
        body, name="allgather_conv_w",
        in_specs=[_VMEM], out_specs=_VMEM,
        out_shape=jax.ShapeDtypeStruct((N_CHIPS, kw, cs), F32),
        scratch_shapes=[pltpu.SemaphoreType.DMA((3,)), pltpu.SemaphoreType.DMA((3,))],
    )(cw)


def _exchange_halves(parts, after, name):
    n = len(parts)
    shapes = [p.shape for p in parts]

    def body(*refs):
        ins, outs = refs[:n], refs[n + 1:2 * n + 1]
        send_sems, recv_sems = refs[2 * n + 1:]
        x, y, c, _ = _mesh_pos()
        copies = []
        for w in range(n):
            hr = shapes[w][1] // 2
            cp = pltpu.make_async_remote_copy(
                src_ref=ins[w].at[:, pl.ds((1 - c) * hr, hr)], dst_ref=outs[w],
                send_sem=send_sems.at[w], recv_sem=recv_sems.at[w],
                device_id=(x, y, 1 - c), device_id_type=MESH)
            cp.start()
            copies.append(cp)
        for cp in copies:
            cp.wait()

    return pl.pallas_call(
        body, name=name,
        in_specs=[_HBM] * n + [_ANY], out_specs=[_HBM] * n,
        out_shape=[jax.ShapeDtypeStruct((s[0], s[1] // 2, s[2]), BF16) for s in shapes],
        scratch_shapes=[pltpu.SemaphoreType.DMA((n,)), pltpu.SemaphoreType.DMA((n,))],
    )(*parts, after)


def _add_halves(part, got, cvec, name):
    ns, r, cdim = part.shape
    hr = r // 2
    tr = _row_tile(hr, TR_ELT)
    nblk = hr // tr

    def body(c_ref, a_ref, b_ref, o_ref):
        o_ref[...] = (a_ref[...].astype(F32) + b_ref[...].astype(F32)).astype(BF16)

    grid_spec = pltpu.PrefetchScalarGridSpec(
        num_scalar_prefetch=1, grid=(ns, nblk),
        in_specs=[pl.BlockSpec((None, tr, cdim), lambda s, i, c_ref: (s, c_ref[0] * nblk + i, 0)),
                  pl.BlockSpec((None, tr, cdim), lambda s, i, c_ref: (s, i, 0))],
        out_specs=pl.BlockSpec((None, tr, cdim), lambda s, i, c_ref: (s, i, 0)))
    return pl.pallas_call(
        body, name=name, grid_spec=grid_spec,
        out_shape=jax.ShapeDtypeStruct((ns, hr, cdim), BF16),
        compiler_params=_params(("parallel", "parallel")),
    )(cvec, part, got)


def _scatter_copy(sums_ref, land_ref, k, src_slot, dst_slot, c, send_sems, recv_sems, to):
    return pltpu.make_async_remote_copy(
        src_ref=sums_ref.at[src_slot], dst_ref=land_ref.at[dst_slot, _half_rows(land_ref, c)],
        send_sem=send_sems.at[k], recv_sem=recv_sems.at[k], device_id=to, device_id_type=MESH)


def _scatter_start(sums, name):
    ns, hr, cdim = sums.shape
    land = lax.empty((ns, 2 * hr, cdim), sums.dtype)

    def body(sums_ref, land_ref, send_sems, recv_sems, sums_thru, land_thru):
        x, y, c, chips = _mesh_pos()
        me = _chip_id(x, y)
        for k, chip in enumerate(chips):
            _scatter_copy(sums_ref, land_ref, k, _chip_id(*chip), me, c, send_sems, recv_sems, (*chip, c)).start()

    return pl.pallas_call(
        body, name=name,
        in_specs=[_HBM, _HBM], out_specs=[_SEM, _SEM, _HBM, _HBM],
        out_shape=[pltpu.SemaphoreType.DMA((3,)), pltpu.SemaphoreType.DMA((3,)),
                   pltpu.HBM(sums.shape, sums.dtype), pltpu.HBM(land.shape, land.dtype)],
        input_output_aliases={0: 2, 1: 3},
        compiler_params=pltpu.CompilerParams(has_side_effects=_EFFECT),
    )(pltpu.with_memory_space_constraint(sums, pltpu.HBM), pltpu.with_memory_space_constraint(land, pltpu.HBM))


def _scatter_wait(send_sems, recv_sems, sums, land, after, name):
    def body(sums_ref, land_ref, send_ref, recv_ref, after_ref, sums_out, land_out):
        x, y, c, chips = _mesh_pos()
        me = _chip_id(x, y)
        for k, chip in enumerate(chips):
            _scatter_copy(sums_ref, land_ref, k, _chip_id(*chip), me, c, send_ref, recv_ref, (*chip, c)).wait_send()
        for k, chip in enumerate(chips):
            _scatter_copy(sums_ref, land_ref, k, me, _chip_id(*chip), c, send_ref, recv_ref, (*chip, c)).wait_recv()

    return pl.pallas_call(
        body, name=name,
        in_specs=[_HBM, _HBM, _SEM, _SEM, _ANY], out_specs=[_HBM, _HBM],
        out_shape=[pltpu.HBM(sums.shape, sums.dtype), pltpu.HBM(land.shape, land.dtype)],
        input_output_aliases={0: 0, 1: 1},
        compiler_params=pltpu.CompilerParams(has_side_effects=_EFFECT),
    )(sums, land, send_sems, recv_sems, after)


def _complete_plan(n_weights):
    def plan(refs, send_sems, recv_sems):
        x, y, c, chips = _mesh_pos()
        me = _chip_id(x, y)
        sibling = (x, y, 1 - c)
        sends, recvs = [], []
        for w in range(n_weights):
            sums, land = refs[2 * w], refs[2 * w + 1]
            sends.append(_scatter_copy(sums, land, 4 * w + 3, me, me, c, send_sems, recv_sems, sibling))
            recvs.append(_scatter_copy(sums, land, 4 * w + 3, me, me, 1 - c, send_sems, recv_sems, sibling))
            for k, chip in enumerate(chips):
                slot = _chip_id(*chip)
                sends.append(_chip_copy(land, 4 * w + k, slot, _half_rows(land, c), send_sems, recv_sems, sibling))
                recvs.append(_chip_copy(land, 4 * w + k, slot, _half_rows(land, 1 - c), send_sems, recv_sems, sibling))
        return sends, recvs
    return plan


def _complete_chip_sums(sums, lands):
    n = len(sums)

    def body(*refs):
        sums_refs, outs = refs[:n], refs[2 * n:3 * n]
        send_sems, recv_sems = refs[3 * n:]
        x, y, c, chips = _mesh_pos()
        me = _chip_id(x, y)
        sibling = (x, y, 1 - c)
        slots = [_chip_id(*chip) for chip in chips]
        sent = []
        for w in range(n):
            out = outs[w]
            cp = _scatter_copy(sums_refs[w], out, 3, me, me, c, send_sems.at[w], recv_sems.at[w], sibling)
            cp.start()
            sent.append(cp)
            for k, slot in enumerate(slots):
                cp = _chip_copy(out, k, slot, _half_rows(out, c), send_sems.at[w], recv_sems.at[w], sibling)
                cp.start()
                sent.append(cp)
        for w in range(n):
            out = outs[w]
            _scatter_copy(sums_refs[w], out, 3, me, me, 1 - c, send_sems.at[w], recv_sems.at[w], sibling).wait_recv()
            for k, slot in enumerate(slots):
                _chip_copy(out, k, slot, _half_rows(out, 1 - c), send_sems.at[w], recv_sems.at[w], sibling).wait_recv()
        for cp in sent:
            cp.wait_send()

    return pl.pallas_call(
        body, name="complete_chip_sums",
        in_specs=[_HBM] * (2 * n), out_specs=[_HBM] * n,
        out_shape=[jax.ShapeDtypeStruct(b.shape, b.dtype) for b in lands],
        input_output_aliases={n + w: w for w in range(n)},
        scratch_shapes=[pltpu.SemaphoreType.DMA((n, 4)), pltpu.SemaphoreType.DMA((n, 4))],
    )(*sums, *lands)


SMALL_ROWS = 8


def _allreduce_small(gl2g, gl2b, gl1g, gl1b, g_ac, gcw, gsink, loss, after):
    d = gl2g.shape[1]
    hd = d // 2
    nq = gsink.shape[1]

    def body(a_ref, b_ref, c_ref, d_ref, e_ref, cw_ref, sk_ref, ls_ref, after_ref, out_ref, mine, gath, send_sems,
             recv_sems):
        x, y, c, _ = _mesh_pos()
        me = 4 * x + 2 * y + c
        mine[...] = jnp.zeros_like(mine)
        mine[0:1, :] = a_ref[...]
        mine[1:2, :] = b_ref[...]
        mine[2:3, :] = c_ref[...]
        mine[3:4, :] = d_ref[...]
        mine[4:5, :] = e_ref[...]
        mine[5:6, 0:hd] = cw_ref[0:1, :]
        mine[5:6, hd:d] = cw_ref[1:2, :]
        mine[6:7, 0:hd] = cw_ref[2:3, :]
        mine[6:7, hd:hd + nq] = sk_ref[...]
        mine[6:7, hd + 128:hd + 256] = ls_ref[...]
        gath[pl.ds(me, 1)] = mine[...][None]
        copies = []
        for r in range(1, 8):
            peer = ((1 - x) if r & 4 else x, (1 - y) if r & 2 else y, (1 - c) if r & 1 else c)
            cp = pltpu.make_async_remote_copy(
                src_ref=mine, dst_ref=gath.at[me], send_sem=send_sems.at[r - 1], recv_sem=recv_sems.at[r - 1],
                device_id=peer, device_id_type=MESH)
            cp.start()
            copies.append(cp)
        for r in range(1, 8):
            peer = ((1 - x) if r & 4 else x, (1 - y) if r & 2 else y, (1 - c) if r & 1 else c)
            peer_id = 4 * peer[0] + 2 * peer[1] + peer[2]
            pltpu.make_async_remote_copy(
                src_ref=mine, dst_ref=gath.at[peer_id], send_sem=send_sems.at[r - 1], recv_sem=recv_sems.at[r - 1],
                device_id=peer, device_id_type=MESH).wait_recv()
        for cp in copies:
            cp.wait_send()
        total = gath[0]
        for dev in range(1, 8):
            total = total + gath[dev]
        out_ref[...] = total

    return pl.pallas_call(
        body, name="allreduce_small",
        in_specs=[_VMEM] * 8 + [_ANY], out_specs=_VMEM,
        out_shape=jax.ShapeDtypeStruct((SMALL_ROWS, d), F32),
        scratch_shapes=[pltpu.VMEM((SMALL_ROWS, d), F32), pltpu.VMEM((8, SMALL_ROWS, d), F32),
                        pltpu.SemaphoreType.DMA((7,)), pltpu.SemaphoreType.DMA((7,))],
    )(gl2g, gl2b, gl1g, gl1b, g_ac, gcw, gsink, loss, after)


def _adamw(w, g, m, v):
    m = ADAM_B1 * m + (1.0 - ADAM_B1) * g
    v = ADAM_B2 * v + (1.0 - ADAM_B2) * (g * g)
    m_hat = m / (1.0 - ADAM_B1 ** ADAM_STEP)
    v_hat = v / (1.0 - ADAM_B2 ** ADAM_STEP)
    delta = -ADAM_LR * (m_hat / (jnp.sqrt(v_hat) + ADAM_EPS) + ADAM_WD * w)
    return delta, m, v


def _adamw_shard(w, m, v, land, own, pos_vec, name, col_block=0):
    tr = _row_tile(w.shape[1] // 2, TR_ELT)
    grid = (w.shape[1] // tr,)
    body, in_specs, out_specs, out_shape = _adamw_passenger(w.shape, tr, grid, col_block)
    grid_spec = pltpu.PrefetchScalarGridSpec(num_scalar_prefetch=1, grid=grid, in_specs=in_specs, out_specs=out_specs)
    return pl.pallas_call(
        body, name=name, grid_spec=grid_spec, out_shape=out_shape,
        compiler_params=_params(("parallel",)),
    )(pos_vec, w, m, v, land, land, land, land, own)


def _adamw_passenger(shape, tr, grid, col_block):
    _, r, c = shape
    nh = r // 2 // tr
    n_blocks = 2 * nh
    n_steps = int(np.prod(grid))
    assert nh * tr * 2 == r and n_blocks <= n_steps

    def step_of(ids):
        step = ids[0]
        for n, i in zip(grid[1:], ids[1:]):
            step = step * n + i
        return step

    def block_of(ids):
        return jnp.minimum(step_of(ids), n_blocks - 1)

    def update(pos_ref, w_ref, m_ref, v_ref, l0, l1, l2, l3, own_ref, g_out, d_out, m_out, v_out):
        i = block_of([pl.program_id(a) for a in range(len(grid))])
        mine = (i // nh) == pos_ref[1]
        own_blk = own_ref[...].astype(F32)
        g = None
        for s, l_ref in enumerate([l0, l1, l2, l3]):
            term = jnp.where(mine & (pos_ref[0] == s), own_blk, l_ref[...].astype(F32))
            g = term if g is None else g + term
        delta, nm, nv = _adamw(w_ref[...], g, m_ref[...], v_ref[...])
        g_out[...] = g
        d_out[...] = delta
        m_out[...] = nm
        v_out[...] = nv

    def body(*refs):
        if n_blocks == n_steps:
            update(*refs)
        else:
            pl.when(step_of([pl.program_id(a) for a in range(len(grid))]) < n_blocks)(lambda: update(*refs))

    def land_spec(s):
        def index(*args):
            i, pos_ref = block_of(args[:-1]), args[-1]
            skip = (pos_ref[0] == s) & ((i // nh) == pos_ref[1])
            return (s, jnp.where(skip, (i + nh) % n_blocks, i), col_block)
        return pl.BlockSpec((None, tr, c), index)

    blk = pl.BlockSpec((None, tr, c), lambda *args: (0, block_of(args[:-1]), 0))
    in_specs = ([blk, blk, blk] + [land_spec(s) for s in range(N_CHIPS)]
                + [pl.BlockSpec((None, tr, c), lambda *args: (args[-1][0], block_of(args[:-1]) % nh, col_block))])
    return body, in_specs, [blk] * 4, [jax.ShapeDtypeStruct((1, r, c), F32)] * 4


def _call_with_adamw(body, name, grid, in_specs, out_specs, out_shape, scratch_shapes, semantics, operands, shard):
    if shard is None:
        return pl.pallas_call(
            body, name=name, grid=grid, in_specs=in_specs, out_specs=out_specs, out_shape=out_shape,
            scratch_shapes=scratch_shapes, compiler_params=_params(semantics))(*operands)
    w, m, v, land, own, pos_vec, col_block = shard
    n_steps = int(np.prod(grid))
    hr = w.shape[1] // 2
    tr = min(t for t in range(16, hr + 1, 16) if hr % t == 0 and 2 * (hr // t) <= n_steps)
    adam_body, adam_in, adam_out, adam_shape = _adamw_passenger(w.shape, tr, grid, col_block)
    n_in, n_out = len(in_specs), len(out_specs)

    def with_pos(spec):
        if spec.index_map is None:
            return spec
        return pl.BlockSpec(spec.block_shape, lambda *args: spec.index_map(*args[:-1]))

    def both(pos_ref, *refs):
        ins, adam_ins = refs[:n_in], refs[n_in:n_in + len(adam_in)]
        refs = refs[n_in + len(adam_in):]
        outs, adam_outs, scratch = refs[:n_out], refs[n_out:n_out + len(adam_out)], refs[n_out + len(adam_out):]
        body(*ins, *outs, *scratch)
        adam_body(pos_ref, *adam_ins, *adam_outs)

    grid_spec = pltpu.PrefetchScalarGridSpec(
        num_scalar_prefetch=1, grid=grid, in_specs=[with_pos(sp) for sp in in_specs] + adam_in,
        out_specs=[with_pos(sp) for sp in out_specs] + adam_out, scratch_shapes=scratch_shapes)
    return pl.pallas_call(
        both, name=name, grid_spec=grid_spec, out_shape=list(out_shape) + adam_shape,
        compiler_params=_params(semantics),
    )(pos_vec, *operands, w, m, v, land, land, land, land, own)


def _adamw_small(red, params):
    names = ["sinks", "g_attn", "g_conv", "ln1_g", "ln1_b", "ln2_g", "ln2_b", "conv_w"]
    d = red.shape[1]
    hd = d // 2
    flat = []
    for nme in names:
        flat.extend(params[nme])
    nq = params["sinks"][0].shape[1]
    cs = params["conv_w"][0].shape[2]

    def body(*refs):
        red_ref = refs[0]
        ins = refs[1:1 + 3 * len(names)]
        outs = refs[1 + 3 * len(names):]
        x, y, _, _ = _mesh_pos()
        me = _chip_id(x, y)

        def conv_tap(row, base):
            picked = red_ref[row:row + 1, base:base + cs]
            for s in range(1, N_CHIPS):
                picked = jnp.where(me == s, red_ref[row:row + 1, base + s * cs:base + (s + 1) * cs], picked)
            return picked

        grads = {
            "sinks": red_ref[6:7, hd:hd + nq],
            "g_attn": red_ref[4:5, 0:hd],
            "g_conv": red_ref[4:5, hd:d],
            "ln1_g": red_ref[2:3, :],
            "ln1_b": red_ref[3:4, :],
            "ln2_g": red_ref[0:1, :],
            "ln2_b": red_ref[1:2, :],
        }
        for i, nme in enumerate(names):
            w_ref, m_ref, v_ref = ins[3 * i:3 * i + 3]
            g_out, d_out, m_out, v_out = outs[4 * i:4 * i + 4]
            if nme == "conv_w":
                for tap, (row, base) in enumerate([(5, 0), (5, hd), (6, 0)]):
                    g = conv_tap(row, base)
                    delta, nm, nv = _adamw(w_ref[0, tap:tap + 1, :], g, m_ref[0, tap:tap + 1, :], v_ref[0, tap:tap + 1, :])
                    g_out[0, tap:tap + 1, :] = g
                    d_out[0, tap:tap + 1, :] = delta
                    m_out[0, tap:tap + 1, :] = nm
                    v_out[0, tap:tap + 1, :] = nv
            else:
                g = grads[nme]
                delta, nm, nv = _adamw(w_ref[...], g, m_ref[...], v_ref[...])
                g_out[...] = g
                d_out[...] = delta
                m_out[...] = nm
                v_out[...] = nv

    out_shape = []
    for nme in names:
        out_shape.extend([jax.ShapeDtypeStruct(params[nme][0].shape, F32)] * 4)
    outs = pl.pallas_call(
        body, name="adamw_small",
        in_specs=[_VMEM] * (1 + len(flat)), out_specs=[_VMEM] * len(out_shape),
        out_shape=out_shape,
    )(red, *flat)
    return {nme: tuple(outs[4 * i:4 * i + 4]) for i, nme in enumerate(names)}


def _rope_tables(pos_col):
    s = pos_col.shape[0]
    w = N_KV_HEADS * HEAD_DIM
    tb = min(512, s)
    inv_freq = (ROPE_THETA ** (-np.arange(0, ROT_DIM, 2, dtype=np.float32) / ROT_DIM)).astype(np.float32)

    def body(pos_ref, cos_ref, sin_ref):
        pos = pos_ref[...].astype(F32)
        lane = lax.broadcasted_iota(jnp.int32, (tb, PAIR), 1) & (HEAD_DIM - 1)
        fidx = lane & (ROT_DIM // 2 - 1)
        inv = jnp.zeros((tb, PAIR), F32)
        for k in range(ROT_DIM // 2):
            inv = jnp.where(fidx == k, float(inv_freq[k]), inv)
        ang = pos * inv
        rot = lane < ROT_DIM
        sin_v = jnp.sin(ang)
        cos_ref[...] = _tile_lanes(jnp.where(rot, jnp.cos(ang), 1.0), w // PAIR)
        sin_ref[...] = _tile_lanes(jnp.where(lane < ROT_DIM // 2, -sin_v, jnp.where(rot, sin_v, 0.0)), w // PAIR)

    return pl.pallas_call(
        body, name="rope_tables", grid=(s // tb,),
        in_specs=[pl.BlockSpec((tb, 1), lambda i: (i, 0))],
        out_specs=[pl.BlockSpec((tb, w), lambda i: (i, 0))] * 2,
        out_shape=[jax.ShapeDtypeStruct((s, w), F32)] * 2,
        compiler_params=_params(("parallel",)),
    )(pos_col)


def _in_proj(x, w_in_g, first_vec, n_shards, into, name):
    _, s, d = x.shape
    ns, _, ncol = w_in_g.shape
    tm = min(2 * TM, s)

    def body(first_ref, x_ref, w_ref, into_ref, o_ref):
        o_ref[...] = _dot(x_ref[...].astype(BF16), w_ref[...]).astype(BF16)

    shard = lambda j, first_ref: lax.rem(first_ref[0] + j, ns)
    grid_spec = pltpu.PrefetchScalarGridSpec(
        num_scalar_prefetch=1, grid=(s // tm, n_shards),
        in_specs=[pl.BlockSpec((None, tm, d), lambda i, j, first_ref: (0, i, 0)),
                  pl.BlockSpec((None, d, ncol), lambda i, j, first_ref: (shard(j, first_ref), 0, 0)), _ANY],
        out_specs=pl.BlockSpec((tm, ncol), lambda i, j, first_ref: (i, shard(j, first_ref))))
    return pl.pallas_call(
        body, name=name, grid_spec=grid_spec,
        out_shape=jax.ShapeDtypeStruct((s, ns * ncol), BF16),
        input_output_aliases={} if into is None else {3: 0},
        compiler_params=_params(("parallel", "arbitrary")),
    )(first_vec, x, w_in_g, first_vec if into is None else into)


PAIR = 2 * HEAD_DIM
KEYS = 2 * WINDOW


def _pair_operand(t_all, h):
    col = (h // 2) * PAIR
    lane = lax.broadcasted_iota(jnp.int32, (KEYS, PAIR), 1)
    own_low = h % 2 == 0
    mine = jnp.where((lane < HEAD_DIM) if own_low else (lane >= HEAD_DIM), t_all[:, col:col + PAIR], 0.0)
    other = pltpu.roll(mine, HEAD_DIM, 1)
    low, high = (mine, other) if own_low else (other, mine)
    return jnp.concatenate([low, high], axis=0).astype(BF16)


def _pair_grad(acc, h):
    lane = lax.broadcasted_iota(jnp.int32, (KEYS, PAIR), 1)
    low = jnp.where(lane < HEAD_DIM, acc[:KEYS], 0.0)
    high = jnp.where(lane >= HEAD_DIM, acc[KEYS:], 0.0)
    if h % 2 == 0:
        return low + pltpu.roll(high, HEAD_DIM, 1)
    return high + pltpu.roll(low, HEAD_DIM, 1)


N_PAIRS = N_KV_HEADS * GROUP // 2


def _all_probs(q, kk2s, first, sinks_ref):
    assert ATTN_SCALE == 0.125
    q = q * ATTN_SCALE
    qps, scores = [], []
    for pair in range(N_PAIRS):
        qp = q[:, pair * PAIR:(pair + 1) * PAIR].astype(BF16)
        qps.append(qp)
        scores.append(_dot_nt(qp, kk2s[pair // (GROUP // 2)]))
    qi = lax.broadcasted_iota(jnp.int32, (WINDOW, 2 * KEYS), 0)
    kj = lax.broadcasted_iota(jnp.int32, (WINDOW, 2 * KEYS), 1) & (KEYS - 1)
    rel = qi + WINDOW - kj
    valid = (rel >= 0) & (rel < WINDOW) & jnp.logical_not(first & (kj < WINDOW))
    bias = jnp.where(valid, 0.0, NEG_BIG)
    s = (jnp.stack(scores, axis=0) + bias[None]).reshape(N_PAIRS * WINDOW, 2 * KEYS)
    probs, p_sinks = [], []
    for t in range(2):
        st = s[:, t * KEYS:(t + 1) * KEYS]
        sink = jnp.concatenate([jnp.broadcast_to(sinks_ref[0:1, 2 * pair + t:2 * pair + t + 1], (WINDOW, 1))
                                for pair in range(N_PAIRS)], axis=0)
        m = jnp.maximum(jnp.max(st, axis=1, keepdims=True), sink)
        e = jnp.exp(st - m)
        e_sink = jnp.exp(sink - m)
        inv_l = 1.0 / (jnp.sum(e, axis=1, keepdims=True) + e_sink)
        probs.append(e * inv_l)
        p_sinks.append(e_sink * inv_l)
    return qps, jnp.concatenate(probs, axis=1), p_sinks


def _roped_qkv(cur_ref, prev_ref, cos_ref, sin_ref, cosp_ref, sinp_ref, qw, kvw):
    cur = cur_ref[...].astype(F32)
    cos, sin = cos_ref[...], sin_ref[...]
    cos_q, sin_q = _tile_lanes(cos, GROUP), _tile_lanes(sin, GROUP)
    q = _rope(cur[:, :qw], cos_q, sin_q, 1.0)
    prev = prev_ref[...].astype(F32)
    k_all = jnp.concatenate([_rope(prev[:, :kvw], cosp_ref[...], sinp_ref[...], 1.0),
                             _rope(cur[:, qw:qw + kvw], cos, sin, 1.0)], axis=0)
    v_all = jnp.concatenate([prev[:, kvw:], cur[:, qw + kvw:]], axis=0)
    return q, k_all, v_all, cos_q, sin_q


def _attention_fwd(proj, cos_t, sin_t, sinks):
    s = proj.shape[0]
    qw = GROUP * N_KV_HEADS * HEAD_DIM
    kvw = N_KV_HEADS * HEAD_DIM
    nb = s // WINDOW

    def body(cur_ref, prev_ref, cos_ref, sin_ref, cosp_ref, sinp_ref, sinks_ref, o_ref):
        first = pl.program_id(0) == 0
        q, k_all, v_all, _, _ = _roped_qkv(cur_ref, prev_ref, cos_ref, sin_ref, cosp_ref, sinp_ref, qw, kvw)
        kk2s = [_pair_operand(k_all, h) for h in range(N_KV_HEADS)]
        vv2s = [_pair_operand(v_all, h) for h in range(N_KV_HEADS)]
        _, probs, _ = _all_probs(q, kk2s, first, sinks_ref)
        probs = probs.astype(BF16)
        outs = [_dot(probs[pair * WINDOW:(pair + 1) * WINDOW], vv2s[pair // (GROUP // 2)]) for pair in range(N_PAIRS)]
        o_ref[...] = jnp.concatenate(outs, axis=1)

    tbl = pl.BlockSpec((WINDOW, kvw), lambda n: (n, 0))
    tbl_prev = pl.BlockSpec((WINDOW, kvw), lambda n: (jnp.maximum(n - 1, 0), 0))
    return pl.pallas_call(
        body, name="attention_fwd", grid=(nb,),
        in_specs=[pl.BlockSpec((WINDOW, qw + 2 * kvw), lambda n: (n, 0)),
                  pl.BlockSpec((WINDOW, 2 * kvw), lambda n: (jnp.maximum(n - 1, 0), (qw // (2 * kvw)))),
                  tbl, tbl, tbl_prev, tbl_prev, _VMEM],
        out_specs=pl.BlockSpec((WINDOW, qw), lambda n: (n, 0)),
        out_shape=jax.ShapeDtypeStruct((s, qw), F32),
        compiler_params=_params(("parallel",)),
    )(proj, proj, cos_t, sin_t, cos_t, sin_t, sinks)


def _conv_taps(cw_ref):
    return [jnp.concatenate([cw_ref[s, k:k + 1, :] for s in range(N_CHIPS)], axis=1) for k in range(3)]


def _shift_down(z, halo, steps):
    last = halo.shape[0]
    row = lax.broadcasted_iota(jnp.int32, z.shape, 0)
    out = pltpu.roll(z, steps, 0)
    for r in range(steps):
        out = jnp.where(row == r, halo[last - steps + r:last - steps + r + 1, :], out)
    return out


def _shift_up(z, halo, steps):
    rows = z.shape[0]
    row = lax.broadcasted_iota(jnp.int32, z.shape, 0)
    out = pltpu.roll(z, rows - steps, 0)
    for r in range(steps):
        out = jnp.where(row == rows - steps + r, halo[r:r + 1, :], out)
    return out


def _split_cbu(lo, hi, cw):
    lo, hi = lo.astype(F32), hi.astype(F32)
    c_gate = lo[:, :cw]
    b_gate = jnp.concatenate([lo[:, cw:], hi[:, :2 * cw - lo.shape[1]]], axis=1)
    u = hi[:, 2 * cw - lo.shape[1]:]
    return c_gate, b_gate, u


def _conv_norm(proj, attn, cw_full, g_ac):
    s, in_w = proj.shape
    cw = attn.shape[1]
    blk_w = in_w // 3
    tb = min(TB_CONV, s)

    def body(lo_ref, hi_ref, lo_h_ref, hi_h_ref, attn_ref, cw_ref, g_ref, mixed_ref, ac_ref, rstd_ref):
        i = pl.program_id(0)
        c_gate, b_gate, u = _split_cbu(lo_ref[...], hi_ref[...], cw)
        c_h, _, u_h = _split_cbu(lo_h_ref[...], hi_h_ref[...], cw)
        z = c_gate * u
        z_h = jnp.where(i == 0, 0.0, c_h * u_h)
        w0, w1, w2 = _conv_taps(cw_ref)
        y = w0 * _shift_down(z, z_h, 2) + w1 * _shift_down(z, z_h, 1) + w2 * z
        conv = b_gate * y
        a = attn_ref[...]
        r_a = lax.rsqrt(jnp.mean(a * a, axis=-1, keepdims=True) + RMS_EPS)
        r_c = lax.rsqrt(jnp.mean(conv * conv, axis=-1, keepdims=True) + RMS_EPS)
        g = g_ref[...]
        mixed_ref[...] = jnp.concatenate([a * r_a * g[:, :cw], conv * r_c * g[:, cw:]], axis=1).astype(BF16)
        ac_ref[...] = jnp.concatenate([a, conv], axis=1)
        rstd_ref[0] = r_a
        rstd_ref[1] = r_c

    halo_idx = lambda i: jnp.maximum(i * (tb // HALO_ROWS) - 1, 0)
    return pl.pallas_call(
        body, name="conv_norm", grid=(s // tb,),
        in_specs=[pl.BlockSpec((tb, blk_w), lambda i: (i, 1)),
                  pl.BlockSpec((tb, blk_w), lambda i: (i, 2)),
                  pl.BlockSpec((HALO_ROWS, blk_w), lambda i: (halo_idx(i), 1)),
                  pl.BlockSpec((HALO_ROWS, blk_w), lambda i: (halo_idx(i), 2)),
                  pl.BlockSpec((tb, cw), lambda i: (i, 0)),
                  _VMEM, _VMEM],
        out_specs=[pl.BlockSpec((tb, 2 * cw), lambda i: (i, 0)),
                   pl.BlockSpec((tb, 2 * cw), lambda i: (i, 0)),
                   pl.BlockSpec((2, tb, 1), lambda i: (0, i, 0))],
        out_shape=[jax.ShapeDtypeStruct((s, 2 * cw), BF16), jax.ShapeDtypeStruct((s, 2 * cw), F32),
                   jax.ShapeDtypeStruct((2, s, 1), F32)],
        compiler_params=_params(("parallel",)),
    )(proj, proj, proj, proj, attn, cw_full, g_ac)


def _out_proj_ln(mixed, w_out_g, x, ln_g, ln_b):
    s, d = mixed.shape
    tm = min(TM, s)
    tk = d
    nk = d // tk

    def body(a_ref, w_ref, x_ref, g_ref, b_ref, xhat_ref, h_ref, rstd_ref, acc):
        k = pl.program_id(1)
        _accumulate(acc, lambda: _dot(a_ref[...], w_ref[...]), k, nk)

        @pl.when(k == nk - 1)
        def _():
            def rows_fn(rows):
                xhat, rstd = _ln_fwd(ALPHA * x_ref[rows, :] + acc[rows, :])
                xhat_ref[rows, :] = xhat
                h_ref[rows, :] = (xhat * g_ref[...] + b_ref[...]).astype(BF16)
                rstd_ref[rows, :] = rstd

            _for_row_chunks(tm, rows_fn)

    row = pl.BlockSpec((tm, d), lambda i, k: (i, 0))
    return pl.pallas_call(
        body, name="out_proj_ln", grid=(s // tm, nk),
        in_specs=[pl.BlockSpec((tm, tk), lambda i, k: (i, k)),
                  pl.BlockSpec((tk, d), lambda i, k: (k, 0)),
                  pl.BlockSpec((None, tm, d), lambda i, k: (0, i, 0)),
                  _VMEM, _VMEM],
        out_specs=[row, row, pl.BlockSpec((tm, 1), lambda i, k: (i, 0))],
        out_shape=[jax.ShapeDtypeStruct((s, d), F32), jax.ShapeDtypeStruct((s, d), BF16),
                   jax.ShapeDtypeStruct((s, 1), F32)],
        scratch_shapes=[pltpu.VMEM((tm, d), F32)],
        compiler_params=_params(("parallel", "arbitrary")),
    )(mixed, w_out_g, x, ln_g, ln_b)


def _gate_up(h1, w_gu_g, first_vec, n_shards, into, name):
    s, d = h1.shape
    ns, _, fs2 = w_gu_g.shape
    fs = fs2 // 2
    tm = min(TM, s)

    def body(first_ref, h_ref, w_ref, act_in, ab_in, act_ref, ab_ref):
        gu = _dot(h_ref[...], w_ref[...])
        g, u = gu[:, :fs], gu[:, fs:]
        sg = _sigmoid(g)
        silu = g * sg
        act_ref[...] = (silu * u).astype(BF16)
        ab_ref[:, :fs] = (u * (sg * (1.0 + g * (1.0 - sg)))).astype(BF16)
        ab_ref[:, fs:] = silu.astype(BF16)

    shard = lambda j, first_ref: lax.rem(first_ref[0] + j, ns)
    grid_spec = pltpu.PrefetchScalarGridSpec(
        num_scalar_prefetch=1, grid=(s // tm, n_shards),
        in_specs=[pl.BlockSpec((tm, d), lambda i, j, first_ref: (i, 0)),
                  pl.BlockSpec((None, d, fs2), lambda i, j, first_ref: (shard(j, first_ref), 0, 0)), _ANY, _ANY],
        out_specs=[pl.BlockSpec((tm, fs), lambda i, j, first_ref: (i, shard(j, first_ref))),
                   pl.BlockSpec((tm, fs2), lambda i, j, first_ref: (i, shard(j, first_ref)))])
    return pl.pallas_call(
        body, name=name, grid_spec=grid_spec,
        out_shape=[jax.ShapeDtypeStruct((s, ns * fs), BF16), jax.ShapeDtypeStruct((s, ns * fs2), BF16)],
        input_output_aliases={} if into is None else {3: 0, 4: 1},
        compiler_params=_params(("parallel", "arbitrary")),
    )(first_vec, h1, w_gu_g, *((first_vec, first_vec) if into is None else into))


def _down_ln_loss(act, w_down_g, xhat1, ln1_g, ln1_b, ln2_g, ln2_b, target):
    s, f = act.shape
    d = xhat1.shape[1]
    tm = min(TM, s)
    tk = f // N_CHIPS
    nk = f // tk

    def body(a_ref, w_ref, xh_ref, g1_ref, b1_ref, g2_ref, b2_ref, t_ref, dpre_ref, dpre16_ref, loss_ref, gg_ref, gb_ref,
             acc):
        i, k = pl.program_id(0), pl.program_id(1)
        _accumulate(acc, lambda: _dot(a_ref[...], w_ref[...]), k, nk)

        @pl.when(k == nk - 1)
        def _():
            @pl.when(i == 0)
            def _():
                loss_ref[...] = jnp.zeros_like(loss_ref)
                gg_ref[...] = jnp.zeros_like(gg_ref)
                gb_ref[...] = jnp.zeros_like(gb_ref)

            def rows_fn(rows):
                h1 = xh_ref[rows, :] * g1_ref[...] + b1_ref[...]
                xhat, rstd = _ln_fwd(ALPHA * h1 + acc[rows, :])
                g2 = g2_ref[...]
                diff = xhat * g2 + b2_ref[...] - t_ref[rows, :]
                dy = diff * (1.0 / d)
                dpre = _ln_bwd(dy, xhat, rstd, g2)
                dpre_ref[rows, :] = dpre
                dpre16_ref[rows, :] = dpre.astype(BF16)
                sq = jnp.sum(jnp.sum(diff * diff, axis=1, keepdims=True), axis=0, keepdims=True)
                loss_ref[...] += jnp.broadcast_to(sq * (0.5 / d), (1, 128))
                gg_ref[...] += jnp.sum(dy * xhat, axis=0, keepdims=True)
                gb_ref[...] += jnp.sum(dy, axis=0, keepdims=True)

            _for_row_chunks(tm, rows_fn)

    row = pl.BlockSpec((tm, d), lambda i, k: (i, 0))
    vec = pl.BlockSpec((1, d), lambda i, k: (0, 0))
    return pl.pallas_call(
        body, name="down_ln_loss", grid=(s // tm, nk),
        in_specs=[pl.BlockSpec((tm, tk), lambda i, k: (i, k)),
                  pl.BlockSpec((tk, d), lambda i, k: (k, 0)),
                  row, _VMEM, _VMEM, _VMEM, _VMEM,
                  pl.BlockSpec((None, tm, d), lambda i, k: (0, i, 0))],
        out_specs=[row, row, pl.BlockSpec((1, 128), lambda i, k: (0, 0)), vec, vec],
        out_shape=[jax.ShapeDtypeStruct((s, d), F32), jax.ShapeDtypeStruct((s, d), BF16),
                   jax.ShapeDtypeStruct((1, 128), F32), jax.ShapeDtypeStruct((1, d), F32),
                   jax.ShapeDtypeStruct((1, d), F32)],
        scratch_shapes=[pltpu.VMEM((tm, d), F32)],
        compiler_params=_params(("arbitrary", "arbitrary")),
    )(act, w_down_g, xhat1, ln1_g, ln1_b, ln2_g, ln2_b, target)


def _dact_silu_bwd(dpre2, w_down_g, ab):
    s, d = dpre2.shape
    fs2 = ab.shape[1] // N_CHIPS
    fs = fs2 // 2
    tm = min(TM, s)

    def body(dp_ref, w_ref, ab_ref, dgu_ref):
        d_act = _dot_nt(dp_ref[...], w_ref[...])
        dgu_ref[:, :fs] = (d_act * ab_ref[:, :fs].astype(F32)).astype(BF16)
        dgu_ref[:, fs:] = (d_act * ab_ref[:, fs:].astype(F32)).astype(BF16)

    blk = pl.BlockSpec((tm, fs2), lambda j, i: (i, j))
    return pl.pallas_call(
        body, name="dact_silu_bwd", grid=(N_CHIPS, s // tm),
        in_specs=[pl.BlockSpec((tm, d), lambda j, i: (i, 0)),
                  pl.BlockSpec((fs, d), lambda j, i: (j, 0)), blk],
        out_specs=blk,
        out_shape=jax.ShapeDtypeStruct(ab.shape, BF16),
        compiler_params=_params(("parallel", "parallel")),
    )(dpre2, w_down_g, ab)


def _grad_rows(a, b, after, name, row_blocks=1):
    s, m = a.shape
    n = b.shape[1]
    ms = m // N_CHIPS
    tmw = ms // row_blocks
    tk = min(TK_TOK, s)
    nk = s // tk

    def body(a_ref, b_ref, after_ref, o_ref, acc):
        k = pl.program_id(2)
        _accumulate(acc, lambda: _dot_tn(a_ref[...].astype(BF16), b_ref[...].astype(BF16)), k, nk)

        @pl.when(k == nk - 1)
        def _():
            o_ref[...] = acc[...].astype(BF16)

    return pl.pallas_call(
        body, name=name, grid=(N_CHIPS, row_blocks, nk),
        in_specs=[pl.BlockSpec((tk, tmw), lambda j, r, k: (k, j * row_blocks + r)),
                  pl.BlockSpec((tk, n), lambda j, r, k: (k, 0)), _ANY],
        out_specs=pl.BlockSpec((None, tmw, n), lambda j, r, k: (j, r, 0)),
        out_shape=jax.ShapeDtypeStruct((N_CHIPS, ms, n), BF16),
        scratch_shapes=[pltpu.VMEM((tmw, n), F32)],
        compiler_params=_params(("parallel", "parallel", "arbitrary")),
    )(a, b, after)


def _grad_cols(a, bs, after, name, a_3d=False, row_blocks=2, shard=None):
    s, m = a.shape[-2:]
    n = bs[0].shape[1]
    ns = n // N_CHIPS
    nb = len(bs)
    tmw = m // row_blocks
    tk = min(TK_TOK, s)
    nk = s // tk

    def body(*refs):
        a_ref, b_refs, o_refs, accs = refs[0], refs[1:1 + nb], refs[2 + nb:2 + 2 * nb], refs[2 + 2 * nb:]
        k = pl.program_id(2)
        for b_ref, acc in zip(b_refs, accs):
            _accumulate(acc, lambda b_ref=b_ref: _dot_tn(a_ref[...].astype(BF16), b_ref[...].astype(BF16)), k, nk)

        @pl.when(k == nk - 1)
        def _():
            for o_ref, acc in zip(o_refs, accs):
                o_ref[...] = acc[...].astype(BF16)

    if a_3d:
        a_spec = pl.BlockSpec((None, tk, tmw), lambda j, r, k: (0, k, r))
    else:
        a_spec = pl.BlockSpec((tk, tmw), lambda j, r, k: (k, r))
    return _call_with_adamw(
        body, name, (N_CHIPS, row_blocks, nk),
        [a_spec] + [pl.BlockSpec((tk, ns), lambda j, r, k: (k, j))] * nb + [_ANY],
        [pl.BlockSpec((None, tmw, ns), lambda j, r, k: (j, r, 0))] * nb,
        [jax.ShapeDtypeStruct((N_CHIPS, m, ns), BF16)] * nb,
        [pltpu.VMEM((tmw, ns), F32)] * nb, ("parallel", "parallel", "arbitrary"), (a, *bs, after), shard)


def _dh1_ln_bwd(d_gu, w_gu_g, dpre2, xhat1, rstd1, ln1_g, after):
    s = d_gu.shape[0]
    d = dpre2.shape[1]
    hd = d // 2
    fs = w_gu_g.shape[2]
    tm = min(TM, s)

    def body(dgu_ref, w_ref, dp2_ref, xh_ref, rs_ref, g_ref, after_ref, dpre_ref, gg_ref, gb_ref, acc_lo, acc_hi):
        i, j, half = pl.program_id(0), pl.program_id(1), pl.program_id(2)

        def product():
            return _dot_nt(dgu_ref[...], w_ref[...])

        @pl.when(half == 0)
        def _():
            _accumulate(acc_lo, product, j, N_CHIPS)

        @pl.when(half == 1)
        def _():
            _accumulate(acc_hi, product, j, N_CHIPS)

        @pl.when((j == N_CHIPS - 1) & (half == 1))
        def _():
            @pl.when(i == 0)
            def _():
                gg_ref[...] = jnp.zeros_like(gg_ref)
                gb_ref[...] = jnp.zeros_like(gb_ref)

            def rows_fn(rows):
                dh = jnp.concatenate([acc_lo[rows, :], acc_hi[rows, :]], axis=1) + ALPHA * dp2_ref[rows, :]
                xhat = xh_ref[rows, :]
                dpre_ref[rows, :] = _ln_bwd(dh, xhat, rs_ref[rows, :], g_ref[...])
                gg_ref[...] += jnp.sum(dh * xhat, axis=0, keepdims=True)
                gb_ref[...] += jnp.sum(dh, axis=0, keepdims=True)

            _for_row_chunks(tm, rows_fn)

    row = pl.BlockSpec((tm, d), lambda i, j, h: (i, 0))
    vec = pl.BlockSpec((1, d), lambda i, j, h: (0, 0))
    act_blk = pl.BlockSpec((tm, fs), lambda i, j, h: (i, j))
    w_blk = pl.BlockSpec((None, hd, fs), lambda i, j, h: (j, h, 0))
    return pl.pallas_call(
        body, name="dh1_ln_bwd", grid=(s // tm, N_CHIPS, 2),
        in_specs=[act_blk, w_blk, row, row, pl.BlockSpec((tm, 1), lambda i, j, h: (i, 0)), _VMEM, _ANY],
        out_specs=[row, vec, vec],
        out_shape=[jax.ShapeDtypeStruct((s, d), F32), jax.ShapeDtypeStruct((1, d), F32),
                   jax.ShapeDtypeStruct((1, d), F32)],
        scratch_shapes=[pltpu.VMEM((tm, hd), F32)] * 2,
        compiler_params=_params(("arbitrary", "arbitrary", "arbitrary")),
    )(d_gu, w_gu_g, dpre2, xhat1, rstd1, ln1_g, after)


def _dmixed_rms_bwd(dpre1, w_out_g, ac, rstd, g_ac):
    s, d = dpre1.shape
    hd = d // 2
    tm = min(TM, s)

    def body(dp_ref, w_ref, ac_ref, rs_ref, g_ref, dac_ref, gg_ref):
        i = pl.program_id(1)
        dm = _dot_nt(dp_ref[...].astype(BF16), w_ref[...])
        pre = ac_ref[...]
        r = rs_ref[...]
        gdm = dm * g_ref[...]
        dac_ref[...] = r * gdm - pre * (r * r * r) * jnp.mean(gdm * pre, axis=-1, keepdims=True)
        gg = jnp.sum(dm * pre * r, axis=0, keepdims=True)

        @pl.when(i == 0)
        def _():
            gg_ref[...] = gg

        @pl.when(i > 0)
        def _():
            gg_ref[...] += gg

    return pl.pallas_call(
        body, name="dmixed_rms_bwd", grid=(2, s // tm),
        in_specs=[pl.BlockSpec((tm, d), lambda h, i: (i, 0)),
                  pl.BlockSpec((hd, d), lambda h, i: (h, 0)),
                  pl.BlockSpec((tm, hd), lambda h, i: (i, h)),
                  pl.BlockSpec((None, tm, 1), lambda h, i: (h, i, 0)),
                  pl.BlockSpec((1, hd), lambda h, i: (0, h))],
        out_specs=[pl.BlockSpec((tm, hd), lambda h, i: (i, h)),
                   pl.BlockSpec((1, hd), lambda h, i: (0, h))],
        out_shape=[jax.ShapeDtypeStruct((s, d), F32), jax.ShapeDtypeStruct((1, d), F32)],
        compiler_params=_params(("arbitrary", "arbitrary")),
    )(dpre1, w_out_g, ac, rstd, g_ac)


def _attention_bwd(proj, d_ac, cos_t, sin_t, sinks, after, shard):
    s = proj.shape[0]
    qw = GROUP * N_KV_HEADS * HEAD_DIM
    kvw = N_KV_HEADS * HEAD_DIM
    nb = s // WINDOW
    nq = GROUP * N_KV_HEADS

    def body(cur_ref, prev_ref, do_ref, cos_ref, sin_ref, cosp_ref, sinp_ref, sinks_ref, after_ref,
             dq_ref, dcur_ref, dprev_ref, dsink_ref):
        n = pl.program_id(0)
        first = n == 0
        q, k_all, v_all, cos_q, sin_q = _roped_qkv(cur_ref, prev_ref, cos_ref, sin_ref, cosp_ref, sinp_ref, qw, kvw)
        kk2s = [_pair_operand(k_all, h) for h in range(N_KV_HEADS)]
        vv2s = [_pair_operand(v_all, h) for h in range(N_KV_HEADS)]
        qps, probs, p_sinks = _all_probs(q, kk2s, first, sinks_ref)
        dops = [do_ref[:, pair * PAIR:(pair + 1) * PAIR].astype(BF16) for pair in range(N_PAIRS)]
        d_probs = jnp.concatenate([_dot_nt(dops[pair], vv2s[pair // (GROUP // 2)]) for pair in range(N_PAIRS)], axis=0)
        d_s, ds_sinks = [], []
        for t in range(2):
            cols = slice(t * KEYS, (t + 1) * KEYS)
            delta = jnp.sum(probs[:, cols] * d_probs[:, cols], axis=1, keepdims=True)
            d_s.append(probs[:, cols] * (d_probs[:, cols] - delta))
            ds_sinks.append(-p_sinks[t] * delta)
        d_s = jnp.concatenate(d_s, axis=1).astype(BF16)
        probs = probs.astype(BF16)
        dq_parts, dk_tiles, dv_tiles, dsink_parts = [], [], [], []
        for h in range(N_KV_HEADS):
            dkk2, dvv2 = None, None
            for p in range(GROUP // 2):
                pair = (GROUP // 2) * h + p
                rows = slice(pair * WINDOW, (pair + 1) * WINDOW)
                dq_parts.append(_dot(d_s[rows], kk2s[h]) * ATTN_SCALE)
                dk_term = _dot_tn(d_s[rows], qps[pair])
                dv_term = _dot_tn(probs[rows], dops[pair])
                dkk2 = dk_term if dkk2 is None else dkk2 + dk_term
                dvv2 = dv_term if dvv2 is None else dvv2 + dv_term
                dsink_parts.extend([jnp.sum(ds_sinks[t][rows], axis=0, keepdims=True) for t in range(2)])
            dk_tiles.append(_pair_grad(dkk2, h))
            dv_tiles.append(_pair_grad(dvv2, h))
        dq_ref[...] = _rope(jnp.concatenate(dq_parts, axis=1), cos_q, sin_q, -1.0)
        dk = jnp.concatenate([dk_tiles[0] + dk_tiles[1], dk_tiles[2] + dk_tiles[3]], axis=1)
        dv = jnp.concatenate([dv_tiles[0] + dv_tiles[1], dv_tiles[2] + dv_tiles[3]], axis=1)
        dprev_ref[...] = jnp.concatenate([dk[:WINDOW], dv[:WINDOW]], axis=1)
        dcur_ref[...] = jnp.concatenate([dk[WINDOW:], dv[WINDOW:]], axis=1)
        dsink = jnp.concatenate(dsink_parts, axis=1)

        @pl.when(first)
        def _():
            dsink_ref[...] = dsink

        @pl.when(n > 0)
        def _():
            dsink_ref[...] += dsink

    tbl = pl.BlockSpec((WINDOW, kvw), lambda n: (n, 0))
    tbl_prev = pl.BlockSpec((WINDOW, kvw), lambda n: (jnp.maximum(n - 1, 0), 0))
    kv_blk = pl.BlockSpec((WINDOW, 2 * kvw), lambda n: (n, 0))
    return _call_with_adamw(
        body, "attention_bwd", (nb,),
        [pl.BlockSpec((WINDOW, qw + 2 * kvw), lambda n: (n, 0)),
         pl.BlockSpec((WINDOW, 2 * kvw), lambda n: (jnp.maximum(n - 1, 0), (qw // (2 * kvw)))),
         pl.BlockSpec((WINDOW, qw), lambda n: (n, 0)),
         tbl, tbl, tbl_prev, tbl_prev, _VMEM, _ANY],
        [pl.BlockSpec((WINDOW, qw), lambda n: (n, 0)), kv_blk, kv_blk, pl.BlockSpec((1, nq), lambda n: (0, 0))],
        [jax.ShapeDtypeStruct((s, qw), F32), jax.ShapeDtypeStruct((s, 2 * kvw), F32),
         jax.ShapeDtypeStruct((s, 2 * kvw), F32), jax.ShapeDtypeStruct((1, nq), F32)],
        [], ("arbitrary",), (proj, proj, d_ac, cos_t, sin_t, cos_t, sin_t, sinks, after), shard)


def _dproj_assemble(proj, d_ac, dq, dkv_cur, dkv_prev, cos_t, sin_t, cw_full):
    s, in_w = proj.shape
    cw = dq.shape[1]
    kvw = N_KV_HEADS * HEAD_DIM
    blk_w = in_w // 3
    tb = WINDOW
    nb = s // tb

    def body(lo_ref, hi_ref, lo_p_ref, hi_p_ref, lo_n_ref, hi_n_ref, dconv_ref, dconv_n_ref,
             dq_ref, dcur_ref, dprev_n_ref, cos_ref, sin_ref, cw_ref, dproj_ref, gcw_ref):
        i = pl.program_id(0)
        last = i == nb - 1
        c_gate, b_gate, u = _split_cbu(lo_ref[...], hi_ref[...], cw)
        c_p, _, u_p = _split_cbu(lo_p_ref[...], hi_p_ref[...], cw)
        _, b_n, _ = _split_cbu(lo_n_ref[...], hi_n_ref[...], cw)
        z = c_gate * u
        z_p = jnp.where(i == 0, 0.0, c_p * u_p)
        z1 = _shift_down(z, z_p, 1)
        z2 = _shift_down(z, z_p, 2)
        w0, w1, w2 = _conv_taps(cw_ref)
        y = w0 * z2 + w1 * z1 + w2 * z
        d_conv = dconv_ref[...]
        d_b = d_conv * y
        d_y = d_conv * b_gate
        d_y_n = jnp.where(last, 0.0, dconv_n_ref[...] * b_n[:dconv_n_ref.shape[0]])
        d_z = w2 * d_y + w1 * _shift_up(d_y, d_y_n, 1) + w0 * _shift_up(d_y, d_y_n, 2)
        d_c = d_z * u
        d_u = d_z * c_gate
        gcw = jnp.concatenate([jnp.sum(d_y * z2, axis=0, keepdims=True), jnp.sum(d_y * z1, axis=0, keepdims=True),
                               jnp.sum(d_y * z, axis=0, keepdims=True)], axis=0)

        @pl.when(i == 0)
        def _():
            gcw_ref[...] = gcw

        @pl.when(i > 0)
        def _():
            gcw_ref[...] += gcw

        dkv = dcur_ref[...] + jnp.where(last, 0.0, dprev_n_ref[...])
        dk = _rope(dkv[:, :kvw], cos_ref[...], sin_ref[...], -1.0)
        dproj_ref[...] = jnp.concatenate([dq_ref[...], dk, dkv[:, kvw:], d_c, d_b, d_u], axis=1).astype(BF16)

    prev_halo = lambda i: jnp.maximum(i * (tb // HALO_ROWS) - 1, 0)
    next_halo = lambda i: jnp.minimum((i + 1) * (tb // HALO_ROWS), s // HALO_ROWS - 1)
    next8 = lambda i: jnp.minimum((i + 1) * (tb // 8), s // 8 - 1)
    nxt = lambda i: jnp.minimum(i + 1, nb - 1)
    return pl.pallas_call(
        body, name="dproj_assemble", grid=(nb,),
        in_specs=[pl.BlockSpec((tb, blk_w), lambda i: (i, 1)),
                  pl.BlockSpec((tb, blk_w), lambda i: (i, 2)),
                  pl.BlockSpec((HALO_ROWS, blk_w), lambda i: (prev_halo(i), 1)),
                  pl.BlockSpec((HALO_ROWS, blk_w), lambda i: (prev_halo(i), 2)),
                  pl.BlockSpec((HALO_ROWS, blk_w), lambda i: (next_halo(i), 1)),
                  pl.BlockSpec((HALO_ROWS, blk_w), lambda i: (next_halo(i), 2)),
                  pl.BlockSpec((tb, cw), lambda i: (i, 1)),
                  pl.BlockSpec((8, cw), lambda i: (next8(i), 1)),
                  pl.BlockSpec((tb, cw), lambda i: (i, 0)),
                  pl.BlockSpec((tb, 2 * kvw), lambda i: (i, 0)),
                  pl.BlockSpec((tb, 2 * kvw), lambda i: (nxt(i), 0)),
                  pl.BlockSpec((tb, kvw), lambda i: (i, 0)),
                  pl.BlockSpec((tb, kvw), lambda i: (i, 0)),
                  _VMEM],
        out_specs=[pl.BlockSpec((tb, in_w), lambda i: (i, 0)),
                   pl.BlockSpec((3, cw), lambda i: (0, 0))],
        out_shape=[jax.ShapeDtypeStruct((s, in_w), BF16), jax.ShapeDtypeStruct((3, cw), F32)],
        compiler_params=_params(("arbitrary",)),
    )(proj, proj, proj, proj, proj, proj, d_ac, d_ac, dq, dkv_cur, dkv_prev, cos_t, sin_t, cw_full)


def _dx(d_proj, w_in_g, dpre1, after, shard):
    s, in_w = d_proj.shape
    ns, d, ncol = w_in_g.shape
    tm = min(TM, s)

    def body(dp_ref, w_ref, r_ref, after_ref, o_ref, acc):
        j = pl.program_id(1)
        _accumulate(acc, lambda: _dot_nt(dp_ref[...], w_ref[...]), j, ns)

        @pl.when(j == ns - 1)
        def _():
            o_ref[...] = acc[...] + ALPHA * r_ref[...]

    return _call_with_adamw(
        body, "dx", (s // tm, ns),
        [pl.BlockSpec((tm, ncol), lambda i, j: (i, j)),
         pl.BlockSpec((None, d, ncol), lambda i, j: (j, 0, 0)),
         pl.BlockSpec((tm, d), lambda i, j: (i, 0)), _ANY],
        [pl.BlockSpec((None, tm, d), lambda i, j: (0, i, 0))], [jax.ShapeDtypeStruct((1, s, d), F32)],
        [pltpu.VMEM((tm, d), F32)], ("parallel", "arbitrary"), (d_proj, w_in_g, dpre1, after), shard)


def kernel(x, positions, w_in, conv_w, sinks, g_attn, g_conv, w_out, ln1_g, ln1_b, w_gate, w_up, w_down, ln2_g, ln2_b, loss_target, m_w_in, m_conv_w, m_sinks, m_g_attn, m_g_conv, m_w_out, m_ln1_g, m_ln1_b, m_w_gate, m_w_up, m_w_down, m_ln2_g, m_ln2_b, v_w_in, v_conv_w, v_sinks, v_g_attn, v_g_conv, v_w_out, v_ln1_g, v_ln1_b, v_w_gate, v_w_up, v_w_down, v_ln2_g, v_ln2_b):
    s = x.shape[1]
    d = x.shape[2]

    chip_vec = _chip_id(lax.axis_index("x"), lax.axis_index("y")).astype(jnp.int32).reshape(1)
    wnames = ["w_in", "w_out", "w_gu", "w_down"]
    buf_in = _cast_weight(w_in, chip_vec, chip_vec, "cast_w_in")
    flight_in, token_in = _gather_start([buf_in], chip_vec, "gather_start_w_in")
    cw_buf = lax.dynamic_update_slice(jnp.zeros((N_CHIPS,) + conv_w.shape[1:], F32), conv_w, (chip_vec[0], 0, 0))
    cw_flight = _flight_start("conv_w_start", [cw_buf], _conv_w_plan(), 3, token_in)
    started = cw_flight[2][0]
    buf_gu = _cast_weight(w_gate, chip_vec, started, "cast_w_gate", 0, 2)
    buf_gu = _cast_weight(w_up, chip_vec, buf_gu, "cast_w_up", 1, 2)
    bufs = [_cast_weight(w_out, chip_vec, started, "cast_w_out"), buf_gu,
            _cast_weight(w_down, chip_vec, started, "cast_w_down")]
    flights_rest, token = _gather_start(bufs, token_in, "gather_start_rest")
    flights = flight_in + flights_rest

    def gathered(i, after):
        send_sems, recv_sems, buf = flights[i]
        buf = _gather_wait(send_sems, recv_sems, buf, after, "gather_wait_" + wnames[i])
        return _sibling_fill(buf, "sibling_fill_" + wnames[i])

    g_ac = jnp.concatenate([g_attn, g_conv], axis=1)

    proj_own = _in_proj(x, _after(flights[0][2], token), chip_vec, 1, None, "in_proj_own")
    cos_t, sin_t = _rope_tables(positions.reshape(s, 1) + token[0:1, 0:1].astype(jnp.int32))
    w_in_g = gathered(0, _after(cos_t, proj_own))
    proj = _in_proj(x, w_in_g, chip_vec + 1, N_CHIPS - 1, proj_own, "in_proj_rest")
    send_sems, recv_sems, buf_out = flights[1]
    buf_out = _gather_wait(send_sems, recv_sems, buf_out, proj, "gather_wait_w_out")
    fill_out = _flight_start("fill_start_w_out", [buf_out], _fill_plan(1), 3, chip_vec)
    attn = _attention_fwd(_after(proj, fill_out[2][0]), cos_t, sin_t, sinks)
    (cw_full,) = _flight_wait("conv_w_wait", cw_flight, _conv_w_plan(), attn)
    mixed, ac, rstd_ac = _conv_norm(proj, attn, cw_full, g_ac)
    (w_out_g,) = _flight_wait("fill_wait_w_out", fill_out, _fill_plan(1), mixed)
    w_out_full = w_out_g.reshape(d, d)
    xhat1, h1, rstd1 = _out_proj_ln(mixed, w_out_full, x, ln1_g, ln1_b)
    send_sems, recv_sems, buf_gu = flights[2]
    buf_gu = _gather_wait(send_sems, recv_sems, buf_gu, h1, "gather_wait_w_gu")
    fill_gu = _flight_start("fill_start_w_gu", [buf_gu], _fill_plan(1), 3, chip_vec)
    own = _gate_up(h1, fill_gu[2][0], chip_vec, 1, None, "gate_up_own")
    (w_gu_g,) = _flight_wait("fill_wait_w_gu", fill_gu, _fill_plan(1), own[0])
    some = _gate_up(h1, w_gu_g, chip_vec + 1, N_CHIPS - 2, own, "gate_up_rest")
    send_sems, recv_sems, buf_down = flights[3]
    buf_down = _gather_wait(send_sems, recv_sems, buf_down, some[0], "gather_wait_w_down")
    fill_down = _flight_start("fill_start_w_down", [buf_down], _fill_plan(1), 3, chip_vec)
    act, ab = _gate_up(h1, _after(w_gu_g, fill_down[2][0]), chip_vec + N_CHIPS - 1, 1, some, "gate_up_last")
    (w_down_g,) = _flight_wait("fill_wait_w_down", fill_down, _fill_plan(1), act)
    w_down_full = w_down_g.reshape(-1, d)
    dpre2, dpre2_16, loss_part, g_ln2_g, g_ln2_b = _down_ln_loss(act, w_down_full, xhat1, ln1_g, ln1_b, ln2_g, ln2_b,
                                                                 loss_target)

    cvec = lax.axis_index("c").astype(jnp.int32).reshape(1)

    def exchange_begin(parts, nme):
        bufs = []
        for part in parts:
            ns, r, cdim = part.shape
            bufs.extend([part, lax.empty((ns, r // 2, cdim), part.dtype)])
        return _flight_start("exchange_start_" + nme, bufs, _exchange_plan(len(parts)), len(parts), cvec)

    def exchange_end(flight, n_parts, after, nme):
        bufs = _flight_wait("exchange_wait_" + nme, flight, _exchange_plan(n_parts), after)
        return [(bufs[2 * w], bufs[2 * w + 1]) for w in range(n_parts)]

    def scatter_begin(part, got, nme):
        return _scatter_start(_add_halves(part, got, cvec, "add_halves_" + nme), "scatter_start_" + nme)

    d_gu = _dact_silu_bwd(dpre2_16, w_down_full, ab)
    p_down = _grad_rows(act, dpre2_16, d_gu, "grad_w_down")
    x_down = exchange_begin([p_down], "w_down")
    (p_gu,) = _grad_cols(h1, [d_gu], x_down[2][0], "grad_w_gate_up")
    ((p_down, got),) = exchange_end(x_down, 1, p_gu, "w_down")
    f_down = scatter_begin(p_down, got, "w_down")
    x_gu = exchange_begin([_after(p_gu, f_down[2])], "w_gu")
    dpre1, g_ln1_g, g_ln1_b = _dh1_ln_bwd(d_gu, w_gu_g, dpre2, xhat1, rstd1, ln1_g, x_gu[2][0])
    ((p_gu, got),) = exchange_end(x_gu, 1, dpre1, "w_gu")
    f_gu = scatter_begin(p_gu, got, "w_gu")
    d_ac, g_g_ac = _dmixed_rms_bwd(_after(dpre1, f_gu[2]), w_out_full, ac, rstd_ac, g_ac)
    pos_vec = jnp.concatenate([chip_vec, cvec])
    sums, land = _scatter_wait(*f_down, d_ac, "scatter_wait_w_down")
    c_down = _flight_start("complete_start_w_down", [sums, land], _complete_plan(1), 4, cvec)
    p_out = _grad_rows(mixed, dpre1, c_down[2][1], "grad_w_out")
    x_out = exchange_begin([p_out], "w_out")
    sums, land = _flight_wait("complete_wait_w_down", c_down, _complete_plan(1), x_out[2][0])
    dq, dkv_cur, dkv_prev, g_sinks, *new_w_down = _attention_bwd(
        proj, d_ac, cos_t, sin_t, sinks, x_out[2][0], (w_down, m_w_down, v_w_down, land, sums, pos_vec, 0))
    ((p_out, got),) = exchange_end(x_out, 1, dq, "w_out")
    f_out = scatter_begin(p_out, got, "w_out")
    sums, land = _scatter_wait(*f_gu, f_out[2], "scatter_wait_w_gu")
    c_gu = _flight_start("complete_start_w_gu", [sums, land], _complete_plan(1), 4, cvec)
    d_proj, g_conv_w = _dproj_assemble(proj, _after(d_ac, c_gu[2][1]), dq, dkv_cur, dkv_prev, cos_t, sin_t, cw_full)
    red = _allreduce_small(g_ln2_g, g_ln2_b, g_ln1_g, g_ln1_b, g_g_ac, g_conv_w, g_sinks, loss_part, d_proj)
    sums_gu, land_gu = _flight_wait("complete_wait_w_gu", c_gu, _complete_plan(1), red)
    p_in, *new_w_gate = _grad_cols(x, [d_proj], red, "grad_w_in", a_3d=True,
                                   shard=(w_gate, m_w_gate, v_w_gate, land_gu, sums_gu, pos_vec, 0))
    x_in = exchange_begin([p_in], "w_in")
    sums, land = _scatter_wait(*f_out, x_in[2][0], "scatter_wait_w_out")
    c_out = _flight_start("complete_start_w_out", [sums, land], _complete_plan(1), 4, cvec)
    new_w_up = _adamw_shard(w_up, m_w_up, v_w_up, _after(land_gu, c_out[2][1]), sums_gu, pos_vec, "adamw_w_up", 1)
    ((p_in, got),) = exchange_end(x_in, 1, new_w_up[0], "w_in")
    f_in = scatter_begin(p_in, got, "w_in")
    (grad_x,) = _dx(d_proj, w_in_g, dpre1, f_in[2], None)

    big = {"w_down": new_w_down, "w_gate": new_w_gate, "w_up": new_w_up}
    sums, land = _scatter_wait(*f_in, grad_x, "scatter_wait_w_in")
    c_in = _flight_start("complete_start_w_in", [sums, land], _complete_plan(1), 4, cvec)
    sums, land = _flight_wait("complete_wait_w_out", c_out, _complete_plan(1), c_in[2][1])
    big["w_out"] = _adamw_shard(w_out, m_w_out, v_w_out, land, sums, pos_vec, "adamw_w_out")
    sums, land = _flight_wait("complete_wait_w_in", c_in, _complete_plan(1), big["w_out"][0])
    big["w_in"] = _adamw_shard(w_in, m_w_in, v_w_in, land, sums, pos_vec, "adamw_w_in")
    small = _adamw_small(red, {
        "sinks": (sinks, m_sinks, v_sinks), "g_attn": (g_attn, m_g_attn, v_g_attn),
        "g_conv": (g_conv, m_g_conv, v_g_conv), "ln1_g": (ln1_g, m_ln1_g, v_ln1_g),
        "ln1_b": (ln1_b, m_ln1_b, v_ln1_b), "ln2_g": (ln2_g, m_ln2_g, v_ln2_g),
        "ln2_b": (ln2_b, m_ln2_b, v_ln2_b), "conv_w": (conv_w, m_conv_w, v_conv_w)})
    res = {**big, **small}
    order = ["w_in", "conv_w", "sinks", "g_attn", "g_conv", "w_out", "ln1_g", "ln1_b", "w_gate", "w_up", "w_down",
             "ln2_g", "ln2_b"]
    loss = red[6, d // 2 + 128]
    return (loss, grad_x, *[res[n][0] for n in order], *[res[n][1] for n in order],
            *[res[n][2] for n in order], *[res[n][3] for n in order])
```

```python
import functools

import numpy as np
import jax
import jax.numpy as jnp
from jax import lax
from jax.experimental import pallas as pl
from jax.experimental.pallas import tpu as pltpu

F32 = jnp.float32
BF16 = jnp.bfloat16
MESH = pl.DeviceIdType.MESH

HEAD_DIM = 64
N_KV_HEADS = 4
GROUP = 4
WINDOW = 128
ROT_DIM = 16
ROPE_THETA = 500000.0
ATTN_SCALE = HEAD_DIM ** -0.5
ALPHA = 2.0 ** 0.25
LN_EPS = 1e-5
RMS_EPS = 1e-6
ADAM_LR = 0.001
ADAM_B1 = 0.9
ADAM_B2 = 0.999
ADAM_EPS = 1e-08
ADAM_WD = 0.01
ADAM_STEP = 10
N_CHIPS = 4
NEG_BIG = -1e30

V7X_VMEM_BYTES = 64 * 1024 * 1024
VMEM_LIMIT = V7X_VMEM_BYTES - 6 * 1024 * 1024

TM = 512
TK_TOK = 1024
TB_CONV = 256
TR_ELT = 256
ROW_CHUNK = 128
HALO_ROWS = 16


def _params(sem):
    return pltpu.CompilerParams(dimension_semantics=sem, vmem_limit_bytes=VMEM_LIMIT)


def _row_tile(rows, target):
    best = None
    for t in range(16, min(rows, target) + 1, 16):
        if rows % t == 0:
            best = t
    assert best is not None, (rows, target)
    return best


def _dot(a, b):
    return jnp.dot(a, b, preferred_element_type=F32)


def _dot_nt(a, b):
    return lax.dot_general(a, b, (((1,), (1,)), ((), ())), preferred_element_type=F32)


def _dot_tn(a, b):
    return lax.dot_general(a, b, (((0,), (0,)), ((), ())), preferred_element_type=F32)


def _mesh_pos():
    x, y, c = lax.axis_index("x"), lax.axis_index("y"), lax.axis_index("c")
    chips = [(1 - x, y), (x, 1 - y), (1 - x, 1 - y)]
    return x, y, c, chips


def _chip_id(px, py):
    return 2 * px + py


def _rope(t, cos, sgn_sin, sign):
    w = t.shape[1]
    lane = lax.broadcasted_iota(jnp.int32, t.shape, 1) & (HEAD_DIM - 1)
    partner = jnp.where(lane < ROT_DIM // 2, pltpu.roll(t, w - ROT_DIM // 2, 1), pltpu.roll(t, ROT_DIM // 2, 1))
    return t * cos + sign * (partner * sgn_sin)


def _tile_lanes(t, n):
    return jnp.concatenate([t] * n, axis=1)


def _sigmoid(g):
    return 1.0 / (1.0 + jnp.exp(-g))


def _for_row_chunks(n_rows, fn):
    def step(r, carry):
        fn(pl.ds(pl.multiple_of(r * ROW_CHUNK, ROW_CHUNK), ROW_CHUNK))
        return carry

    lax.fori_loop(0, n_rows // ROW_CHUNK, step, 0)


def _accumulate(acc, make_val, k, nk):
    if nk == 1:
        acc[...] = make_val()
        return

    @pl.when(k == 0)
    def _():
        acc[...] = jnp.zeros_like(acc)

    acc[...] += make_val()


def _ln_fwd(pre):
    mu = jnp.mean(pre, axis=-1, keepdims=True)
    cen = pre - mu
    var = jnp.mean(cen * cen, axis=-1, keepdims=True)
    rstd = lax.rsqrt(var + LN_EPS)
    return cen * rstd, rstd


def _ln_bwd(dy, xhat, rstd, g):
    dxhat = dy * g
    m1 = jnp.mean(dxhat, axis=-1, keepdims=True)
    m2 = jnp.mean(dxhat * xhat, axis=-1, keepdims=True)
    return rstd * (dxhat - m1 - xhat * m2)


def _cast_weight(w, chip_vec, after, name, col_block=0, n_col_blocks=1):
    _, r, c = w.shape
    tr = _row_tile(r, TR_ELT)

    def body(chip_ref, w_ref, after_ref, o_ref):
        o_ref[...] = w_ref[...].astype(BF16)

    grid_spec = pltpu.PrefetchScalarGridSpec(
        num_scalar_prefetch=1, grid=(r // tr,),
        in_specs=[pl.BlockSpec((None, tr, c), lambda i, chip_ref: (0, i, 0)), _ANY],
        out_specs=pl.BlockSpec((None, tr, c), lambda i, chip_ref: (chip_ref[0], i, col_block)))
    return pl.pallas_call(
        body, name=name, grid_spec=grid_spec,
        out_shape=jax.ShapeDtypeStruct((N_CHIPS, r, n_col_blocks * c), BF16),
        input_output_aliases={2: 0} if col_block else {},
        compiler_params=_params(("parallel",)),
    )(chip_vec, w, after)


_HBM = pl.BlockSpec(memory_space=pltpu.HBM)
_VMEM = pl.BlockSpec(memory_space=pltpu.VMEM)


_SEM = pl.BlockSpec(memory_space=pltpu.SEMAPHORE)
_ANY = pl.BlockSpec(memory_space=pl.ANY)
_EFFECT = pltpu.SideEffectType.DATAFLOW_SIDE_EFFECTING


def _chip_copy(buf, k, chip_of_src, half_rows, send_sems, recv_sems, to):
    part = buf.at[chip_of_src, half_rows]
    return pltpu.make_async_remote_copy(
        src_ref=part, dst_ref=part, send_sem=send_sems.at[k], recv_sem=recv_sems.at[k], device_id=to, device_id_type=MESH)


def _half_rows(buf, which):
    hr = buf.shape[1] // 2
    return pl.ds(which * hr, hr)


def _after(value, dep):
    return lax.optimization_barrier((value, dep))[0]


def _flight_start(name, bufs, plan, n_sems, after):
    n = len(bufs)

    def body(*refs):
        sends, _ = plan(refs[:n], refs[n + 1], refs[n + 2])
        for cp in sends:
            cp.start()

    outs = pl.pallas_call(
        body, name=name,
        in_specs=[_HBM] * n + [_ANY], out_specs=[_SEM, _SEM] + [_HBM] * n,
        out_shape=[pltpu.SemaphoreType.DMA((n_sems,))] * 2 + [pltpu.HBM(b.shape, b.dtype) for b in bufs],
        input_output_aliases={i: 2 + i for i in range(n)},
        compiler_params=pltpu.CompilerParams(has_side_effects=_EFFECT),
    )(*[pltpu.with_memory_space_constraint(b, pltpu.HBM) for b in bufs], after)
    return outs[0], outs[1], list(outs[2:])


def _flight_wait(name, flight, plan, after):
    send_sems, recv_sems, bufs = flight
    n = len(bufs)

    def body(*refs):
        sends, recvs = plan(refs[:n], refs[n], refs[n + 1])
        for cp in sends:
            cp.wait_send()
        for cp in recvs:
            cp.wait_recv()

    outs = pl.pallas_call(
        body, name=name,
        in_specs=[_HBM] * n + [_SEM, _SEM, _ANY], out_specs=[_HBM] * n,
        out_shape=[pltpu.HBM(b.shape, b.dtype) for b in bufs],
        input_output_aliases={i: i for i in range(n)},
        compiler_params=pltpu.CompilerParams(has_side_effects=_EFFECT),
    )(*bufs, send_sems, recv_sems, after)
    return list(outs)


def _fill_plan(n_bufs):
    def plan(refs, send_sems, recv_sems):
        x, y, c, chips = _mesh_pos()
        sibling = (x, y, 1 - c)
        sends, recvs = [], []
        for w in range(n_bufs):
            for k, chip in enumerate(chips):
                slot = _chip_id(*chip)
                sends.append(_chip_copy(refs[w], 3 * w + k, slot, _half_rows(refs[w], c), send_sems, recv_sems, sibling))
                recvs.append(_chip_copy(refs[w], 3 * w + k, slot, _half_rows(refs[w], 1 - c), send_sems, recv_sems,
                                        sibling))
        return sends, recvs
    return plan


def _conv_w_plan():
    def plan(refs, send_sems, recv_sems):
        x, y, c, chips = _mesh_pos()
        me = _chip_id(x, y)
        (buf,) = refs
        sends, recvs = [], []
        for k, chip in enumerate(chips):
            for slot, into in ((me, sends), (_chip_id(*chip), recvs)):
                into.append(pltpu.make_async_remote_copy(
                    src_ref=buf.at[slot], dst_ref=buf.at[slot], send_sem=send_sems.at[k], recv_sem=recv_sems.at[k],
                    device_id=(*chip, c), device_id_type=MESH))
        return sends, recvs
    return plan


def _exchange_plan(n_parts):
    def plan(refs, send_sems, recv_sems):
        x, y, c, _ = _mesh_pos()
        copies = []
        for w in range(n_parts):
            part, got = refs[2 * w], refs[2 * w + 1]
            hr = got.shape[1]
            copies.append(pltpu.make_async_remote_copy(
                src_ref=part.at[:, pl.ds((1 - c) * hr, hr)], dst_ref=got, send_sem=send_sems.at[w],
                recv_sem=recv_sems.at[w], device_id=(x, y, 1 - c), device_id_type=MESH))
        return copies, copies
    return plan


def _gather_start(bufs, after, name):
    n = len(bufs)

    def body(*refs):
        ins = refs[:n]
        sends, recvs = refs[n + 1:2 * n + 1], refs[2 * n + 1:3 * n + 1]
        token = refs[4 * n + 1]
        x, y, c, chips = _mesh_pos()
        me = _chip_id(x, y)
        for w in range(n):
            for k, chip in enumerate(chips):
                _chip_copy(ins[w], k, me, _half_rows(ins[w], c), sends[w], recvs[w], (*chip, c)).start()
        token[...] = jnp.zeros_like(token)

    outs = pl.pallas_call(
        body, name=name,
        in_specs=[_HBM] * n + [_ANY],
        out_specs=[_SEM] * (2 * n) + [_HBM] * n + [_VMEM],
        out_shape=[pltpu.SemaphoreType.DMA((3,))] * (2 * n) + [pltpu.HBM(b.shape, b.dtype) for b in bufs]
        + [jax.ShapeDtypeStruct((8, 128), F32)],
        input_output_aliases={w: 2 * n + w for w in range(n)},
        compiler_params=pltpu.CompilerParams(has_side_effects=_EFFECT),
    )(*[pltpu.with_memory_space_constraint(b, pltpu.HBM) for b in bufs], after)
    return [(outs[w], outs[n + w], outs[2 * n + w]) for w in range(n)], outs[3 * n]


def _gather_wait(send_sems, recv_sems, buf, after, name):
    def body(buf_ref, send_ref, recv_ref, after_ref, out_ref):
        x, y, c, chips = _mesh_pos()
        me = _chip_id(x, y)
        for k, chip in enumerate(chips):
            _chip_copy(buf_ref, k, me, _half_rows(buf_ref, c), send_ref, recv_ref, (*chip, c)).wait_send()
        for k, chip in enumerate(chips):
            _chip_copy(buf_ref, k, _chip_id(*chip), _half_rows(buf_ref, c), send_ref, recv_ref, (*chip, c)).wait_recv()

    return pl.pallas_call(
        body, name=name,
        in_specs=[_HBM, _SEM, _SEM, _ANY], out_specs=_HBM,
        out_shape=pltpu.HBM(buf.shape, buf.dtype),
        input_output_aliases={0: 0},
        compiler_params=pltpu.CompilerParams(has_side_effects=_EFFECT),
    )(buf, send_sems, recv_sems, after)


def _sibling_fill(buf, name, own_too=False):
    n_copies = 4 if own_too else 3

    def body(buf_ref, out_ref, send_sems, recv_sems):
        x, y, c, chips = _mesh_pos()
        sibling = (x, y, 1 - c)
        slots = [_chip_id(*chip) for chip in chips] + ([_chip_id(x, y)] if own_too else [])
        copies = []
        for k, slot in enumerate(slots):
            cp = _chip_copy(out_ref, k, slot, _half_rows(out_ref, c), send_sems, recv_sems, sibling)
            cp.start()
            copies.append(cp)
        for k, slot in enumerate(slots):
            _chip_copy(out_ref, k, slot, _half_rows(out_ref, 1 - c), send_sems, recv_sems, sibling).wait_recv()
        for cp in copies:
            cp.wait_send()

    return pl.pallas_call(
        body, name=name,
        in_specs=[_HBM], out_specs=_HBM,
        out_shape=jax.ShapeDtypeStruct(buf.shape, buf.dtype),
        input_output_aliases={0: 0},
        scratch_shapes=[pltpu.SemaphoreType.DMA((n_copies,)), pltpu.SemaphoreType.DMA((n_copies,))],
    )(buf)


def _allgather_conv_w(cw):
    _, kw, cs = cw.shape

    def body(cw_ref, out_ref, send_sems, recv_sems):
        x, y, c, chips = _mesh_pos()
        me = _chip_id(x, y)
        out_ref[pl.ds(me, 1)] = cw_ref[...]
        copies = []
        for k, chip in enumerate(chips):
            cp = pltpu.make_async_remote_copy(
                src_ref=cw_ref.at[0], dst_ref=out_ref.at[me], send_sem=send_sems.at[k], recv_sem=recv_sems.at[k],
                device_id=(*chip, c), device_id_type=MESH)
            cp.start()
            copies.append(cp)
        for k, chip in enumerate(chips):
            pltpu.make_async_remote_copy(
                src_ref=cw_ref.at[0], dst_ref=out_ref.at[_chip_id(*chip)], send_sem=send_sems.at[k],
                recv_sem=recv_sems.at[k], device_id=(*chip, c), device_id_type=MESH).wait_recv()
        for cp in copies:
            cp.wait_send()

    return pl.pallas_call(
        body, name="allgather_conv_w",
        in_specs=[_VMEM], out_specs=_VMEM,
        out_shape=jax.ShapeDtypeStruct((N_CHIPS, kw, cs), F32),
        scratch_shapes=[pltpu.SemaphoreType.DMA((3,)), pltpu.SemaphoreType.DMA((3,))],
    )(cw)


def _exchange_halves(parts, after, name):
    n = len(parts)
    shapes = [p.shape for p in parts]

    def body(*refs):
        ins, outs = refs[:n], refs[n + 1:2 * n + 1]
        send_sems, recv_sems = refs[2 * n + 1:]
        x, y, c, _ = _mesh_pos()
        copies = []
        for w in range(n):
            hr = shapes[w][1] // 2
            cp = pltpu.make_async_remote_copy(
                src_ref=ins[w].at[:, pl.ds((1 - c) * hr, hr)], dst_ref=outs[w],
                send_sem=send_sems.at[w], recv_sem=recv_sems.at[w],
                device_id=(x, y, 1 - c), device_id_type=MESH)
            cp.start()
            copies.append(cp)
        for cp in copies:
            cp.wait()

    return pl.pallas_call(
        body, name=name,
        in_specs=[_HBM] * n + [_ANY], out_specs=[_HBM] * n,
        out_shape=[jax.ShapeDtypeStruct((s[0], s[1] // 2, s[2]), BF16) for s in shapes],
        scratch_shapes=[pltpu.SemaphoreType.DMA((n,)), pltpu.SemaphoreType.DMA((n,))],
    )(*parts, after)


def _add_halves(part, got, cvec, name):
    ns, r, cdim = part.shape
    hr = r // 2
    tr = _row_tile(hr, TR_ELT)
    nblk = hr // tr

    def body(c_ref, a_ref, b_ref, o_ref):
        o_ref[...] = a_ref[...] + b_ref[...]

    grid_spec = pltpu.PrefetchScalarGridSpec(
        num_scalar_prefetch=1, grid=(ns, nblk),
        in_specs=[pl.BlockSpec((None, tr, cdim), lambda s, i, c_ref: (s, c_ref[0] * nblk + i, 0)),
                  pl.BlockSpec((None, tr, cdim), lambda s, i, c_ref: (s, i, 0))],
        out_specs=pl.BlockSpec((None, tr, cdim), lambda s, i, c_ref: (s, i, 0)))
    return pl.pallas_call(
        body, name=name, grid_spec=grid_spec,
        out_shape=jax.ShapeDtypeStruct((ns, hr, cdim), BF16),
        compiler_params=_params(("parallel", "parallel")),
    )(cvec, part, got)


def _scatter_copy(sums_ref, land_ref, k, src_slot, dst_slot, c, send_sems, recv_sems, to):
    return pltpu.make_async_remote_copy(
        src_ref=sums_ref.at[src_slot], dst_ref=land_ref.at[dst_slot, _half_rows(land_ref, c)],
        send_sem=send_sems.at[k], recv_sem=recv_sems.at[k], device_id=to, device_id_type=MESH)


def _scatter_start(sums, name):
    ns, hr, cdim = sums.shape
    land = lax.empty((ns, 2 * hr, cdim), sums.dtype)

    def body(sums_ref, land_ref, send_sems, recv_sems, sums_thru, land_thru):
        x, y, c, chips = _mesh_pos()
        me = _chip_id(x, y)
        for k, chip in enumerate(chips):
            _scatter_copy(sums_ref, land_ref, k, _chip_id(*chip), me, c, send_sems, recv_sems, (*chip, c)).start()

    return pl.pallas_call(
        body, name=name,
        in_specs=[_HBM, _HBM], out_specs=[_SEM, _SEM, _HBM, _HBM],
        out_shape=[pltpu.SemaphoreType.DMA((3,)), pltpu.SemaphoreType.DMA((3,)),
                   pltpu.HBM(sums.shape, sums.dtype), pltpu.HBM(land.shape, land.dtype)],
        input_output_aliases={0: 2, 1: 3},
        compiler_params=pltpu.CompilerParams(has_side_effects=_EFFECT),
    )(pltpu.with_memory_space_constraint(sums, pltpu.HBM), pltpu.with_memory_space_constraint(land, pltpu.HBM))


def _scatter_wait(send_sems, recv_sems, sums, land, after, name):
    def body(sums_ref, land_ref, send_ref, recv_ref, after_ref, sums_out, land_out):
        x, y, c, chips = _mesh_pos()
        me = _chip_id(x, y)
        for k, chip in enumerate(chips):
            _scatter_copy(sums_ref, land_ref, k, _chip_id(*chip), me, c, send_ref, recv_ref, (*chip, c)).wait_send()
        for k, chip in enumerate(chips):
            _scatter_copy(sums_ref, land_ref, k, me, _chip_id(*chip), c, send_ref, recv_ref, (*chip, c)).wait_recv()

    return pl.pallas_call(
        body, name=name,
        in_specs=[_HBM, _HBM, _SEM, _SEM, _ANY], out_specs=[_HBM, _HBM],
        out_shape=[pltpu.HBM(sums.shape, sums.dtype), pltpu.HBM(land.shape, land.dtype)],
        input_output_aliases={0: 0, 1: 1},
        compiler_params=pltpu.CompilerParams(has_side_effects=_EFFECT),
    )(sums, land, send_sems, recv_sems, after)


def _complete_plan(n_weights):
    def plan(refs, send_sems, recv_sems):
        x, y, c, chips = _mesh_pos()
        me = _chip_id(x, y)
        sibling = (x, y, 1 - c)
        sends, recvs = [], []
        for w in range(n_weights):
            sums, land = refs[2 * w], refs[2 * w + 1]
            sends.append(_scatter_copy(sums, land, 4 * w + 3, me, me, c, send_sems, recv_sems, sibling))
            recvs.append(_scatter_copy(sums, land, 4 * w + 3, me, me, 1 - c, send_sems, recv_sems, sibling))
            for k, chip in enumerate(chips):
                slot = _chip_id(*chip)
                sends.append(_chip_copy(land, 4 * w + k, slot, _half_rows(land, c), send_sems, recv_sems, sibling))
                recvs.append(_chip_copy(land, 4 * w + k, slot, _half_rows(land, 1 - c), send_sems, recv_sems, sibling))
        return sends, recvs
    return plan


def _complete_chip_sums(sums, lands):
    n = len(sums)

    def body(*refs):
        sums_refs, outs = refs[:n], refs[2 * n:3 * n]
        send_sems, recv_sems = refs[3 * n:]
        x, y, c, chips = _mesh_pos()
        me = _chip_id(x, y)
        sibling = (x, y, 1 - c)
        slots = [_chip_id(*chip) for chip in chips]
        sent = []
        for w in range(n):
            out = outs[w]
            cp = _scatter_copy(sums_refs[w], out, 3, me, me, c, send_sems.at[w], recv_sems.at[w], sibling)
            cp.start()
            sent.append(cp)
            for k, slot in enumerate(slots):
                cp = _chip_copy(out, k, slot, _half_rows(out, c), send_sems.at[w], recv_sems.at[w], sibling)
                cp.start()
                sent.append(cp)
        for w in range(n):
            out = outs[w]
            _scatter_copy(sums_refs[w], out, 3, me, me, 1 - c, send_sems.at[w], recv_sems.at[w], sibling).wait_recv()
            for k, slot in enumerate(slots):
                _chip_copy(out, k, slot, _half_rows(out, 1 - c), send_sems.at[w], recv_sems.at[w], sibling).wait_recv()
        for cp in sent:
            cp.wait_send()

    return pl.pallas_call(
        body, name="complete_chip_sums",
        in_specs=[_HBM] * (2 * n), out_specs=[_HBM] * n,
        out_shape=[jax.ShapeDtypeStruct(b.shape, b.dtype) for b in lands],
        input_output_aliases={n + w: w for w in range(n)},
        scratch_shapes=[pltpu.SemaphoreType.DMA((n, 4)), pltpu.SemaphoreType.DMA((n, 4))],
    )(*sums, *lands)


SMALL_ROWS = 8


def _allreduce_small(gl2g, gl2b, gl1g, gl1b, g_ac, gcw, gsink, loss, after):
    d = gl2g.shape[1]
    hd = d // 2
    nq = gsink.shape[1]

    def body(a_ref, b_ref, c_ref, d_ref, e_ref, cw_ref, sk_ref, ls_ref, after_ref, out_ref, mine, gath, send_sems,
             recv_sems):
        x, y, c, _ = _mesh_pos()
        me = 4 * x + 2 * y + c
        mine[...] = jnp.zeros_like(mine)
        mine[0:1, :] = a_ref[...]
        mine[1:2, :] = b_ref[...]
        mine[2:3, :] = c_ref[...]
        mine[3:4, :] = d_ref[...]
        mine[4:5, :] = e_ref[...]
        mine[5:6, 0:hd] = cw_ref[0:1, :]
        mine[5:6, hd:d] = cw_ref[1:2, :]
        mine[6:7, 0:hd] = cw_ref[2:3, :]
        mine[6:7, hd:hd + nq] = sk_ref[...]
        mine[6:7, hd + 128:hd + 256] = ls_ref[...]
        gath[pl.ds(me, 1)] = mine[...][None]
        copies = []
        for r in range(1, 8):
            peer = ((1 - x) if r & 4 else x, (1 - y) if r & 2 else y, (1 - c) if r & 1 else c)
            cp = pltpu.make_async_remote_copy(
                src_ref=mine, dst_ref=gath.at[me], send_sem=send_sems.at[r - 1], recv_sem=recv_sems.at[r - 1],
                device_id=peer, device_id_type=MESH)
            cp.start()
            copies.append(cp)
        for r in range(1, 8):
            peer = ((1 - x) if r & 4 else x, (1 - y) if r & 2 else y, (1 - c) if r & 1 else c)
            peer_id = 4 * peer[0] + 2 * peer[1] + peer[2]
            pltpu.make_async_remote_copy(
                src_ref=mine, dst_ref=gath.at[peer_id], send_sem=send_sems.at[r - 1], recv_sem=recv_sems.at[r - 1],
                device_id=peer, device_id_type=MESH).wait_recv()
        for cp in copies:
            cp.wait_send()
        total = gath[0]
        for dev in range(1, 8):
            total = total + gath[dev]
        out_ref[...] = total

    return pl.pallas_call(
        body, name="allreduce_small",
        in_specs=[_VMEM] * 8 + [_ANY], out_specs=_VMEM,
        out_shape=jax.ShapeDtypeStruct((SMALL_ROWS, d), F32),
        scratch_shapes=[pltpu.VMEM((SMALL_ROWS, d), F32), pltpu.VMEM((8, SMALL_ROWS, d), F32),
                        pltpu.SemaphoreType.DMA((7,)), pltpu.SemaphoreType.DMA((7,))],
    )(gl2g, gl2b, gl1g, gl1b, g_ac, gcw, gsink, loss, after)


def _adamw(w, g, m, v):
    m = ADAM_B1 * m + (1.0 - ADAM_B1) * g
    v = ADAM_B2 * v + (1.0 - ADAM_B2) * (g * g)
    m_hat = m / (1.0 - ADAM_B1 ** ADAM_STEP)
    v_hat = v / (1.0 - ADAM_B2 ** ADAM_STEP)
    delta = -ADAM_LR * (m_hat / (jnp.sqrt(v_hat) + ADAM_EPS) + ADAM_WD * w)
    return delta, m, v


def _adamw_shard(w, m, v, land, own, pos_vec, name, col_block=0):
    tr = _row_tile(w.shape[1] // 2, TR_ELT)
    grid = (w.shape[1] // tr,)
    body, in_specs, out_specs, out_shape = _adamw_passenger(w.shape, tr, grid, col_block)
    grid_spec = pltpu.PrefetchScalarGridSpec(num_scalar_prefetch=1, grid=grid, in_specs=in_specs, out_specs=out_specs)
    return pl.pallas_call(
        body, name=name, grid_spec=grid_spec, out_shape=out_shape,
        compiler_params=_params(("parallel",)),
    )(pos_vec, w, m, v, land, land, land, land, own)


def _adamw_passenger(shape, tr, grid, col_block):
    _, r, c = shape
    nh = r // 2 // tr
    n_blocks = 2 * nh
    n_steps = int(np.prod(grid))
    assert nh * tr * 2 == r and n_blocks <= n_steps

    def step_of(ids):
        step = ids[0]
        for n, i in zip(grid[1:], ids[1:]):
            step = step * n + i
        return step

    def block_of(ids):
        return jnp.minimum(step_of(ids), n_blocks - 1)

    def update(pos_ref, w_ref, m_ref, v_ref, l0, l1, l2, l3, own_ref, g_out, d_out, m_out, v_out):
        i = block_of([pl.program_id(a) for a in range(len(grid))])
        mine = (i // nh) == pos_ref[1]
        own_blk = own_ref[...].astype(F32)
        g = None
        for s, l_ref in enumerate([l0, l1, l2, l3]):
            term = jnp.where(mine & (pos_ref[0] == s), own_blk, l_ref[...].astype(F32))
            g = term if g is None else g + term
        delta, nm, nv = _adamw(w_ref[...], g, m_ref[...], v_ref[...])
        g_out[...] = g
        d_out[...] = delta
        m_out[...] = nm
        v_out[...] = nv

    def body(*refs):
        if n_blocks == n_steps:
            update(*refs)
        else:
            pl.when(step_of([pl.program_id(a) for a in range(len(grid))]) < n_blocks)(lambda: update(*refs))

    def land_spec(s):
        def index(*args):
            i, pos_ref = block_of(args[:-1]), args[-1]
            skip = (pos_ref[0] == s) & ((i // nh) == pos_ref[1])
            return (s, jnp.where(skip, (i + nh) % n_blocks, i), col_block)
        return pl.BlockSpec((None, tr, c), index)

    blk = pl.BlockSpec((None, tr, c), lambda *args: (0, block_of(args[:-1]), 0))
    in_specs = ([blk, blk, blk] + [land_spec(s) for s in range(N_CHIPS)]
                + [pl.BlockSpec((None, tr, c), lambda *args: (args[-1][0], block_of(args[:-1]) % nh, col_block))])
    return body, in_specs, [blk] * 4, [jax.ShapeDtypeStruct((1, r, c), F32)] * 4


def _call_with_adamw(body, name, grid, in_specs, out_specs, out_shape, scratch_shapes, semantics, operands, shard):
    if shard is None:
        return pl.pallas_call(
            body, name=name, grid=grid, in_specs=in_specs, out_specs=out_specs, out_shape=out_shape,
            scratch_shapes=scratch_shapes, compiler_params=_params(semantics))(*operands)
    w, m, v, land, own, pos_vec, col_block = shard
    n_steps = int(np.prod(grid))
    hr = w.shape[1] // 2
    tr = min(t for t in range(16, hr + 1, 16) if hr % t == 0 and 2 * (hr // t) <= n_steps)
    adam_body, adam_in, adam_out, adam_shape = _adamw_passenger(w.shape, tr, grid, col_block)
    n_in, n_out = len(in_specs), len(out_specs)

    def with_pos(spec):
        if spec.index_map is None:
            return spec
        return pl.BlockSpec(spec.block_shape, lambda *args: spec.index_map(*args[:-1]))

    def both(pos_ref, *refs):
        ins, adam_ins = refs[:n_in], refs[n_in:n_in + len(adam_in)]
        refs = refs[n_in + len(adam_in):]
        outs, adam_outs, scratch = refs[:n_out], refs[n_out:n_out + len(adam_out)], refs[n_out + len(adam_out):]
        body(*ins, *outs, *scratch)
        adam_body(pos_ref, *adam_ins, *adam_outs)

    grid_spec = pltpu.PrefetchScalarGridSpec(
        num_scalar_prefetch=1, grid=grid, in_specs=[with_pos(sp) for sp in in_specs] + adam_in,
        out_specs=[with_pos(sp) for sp in out_specs] + adam_out, scratch_shapes=scratch_shapes)
    return pl.pallas_call(
        both, name=name, grid_spec=grid_spec, out_shape=list(out_shape) + adam_shape,
        compiler_params=_params(semantics),
    )(pos_vec, *operands, w, m, v, land, land, land, land, own)


def _adamw_small(red, params):
    names = ["sinks", "g_attn", "g_conv", "ln1_g", "ln1_b", "ln2_g", "ln2_b", "conv_w"]
    d = red.shape[1]
    hd = d // 2
    flat = []
    for nme in names:
        flat.extend(params[nme])
    nq = params["sinks"][0].shape[1]
    cs = params["conv_w"][0].shape[2]

    def body(*refs):
        red_ref = refs[0]
        ins = refs[1:1 + 3 * len(names)]
        outs = refs[1 + 3 * len(names):]
        x, y, _, _ = _mesh_pos()
        me = _chip_id(x, y)

        def conv_tap(row, base):
            picked = red_ref[row:row + 1, base:base + cs]
            for s in range(1, N_CHIPS):
                picked = jnp.where(me == s, red_ref[row:row + 1, base + s * cs:base + (s + 1) * cs], picked)
            return picked

        grads = {
            "sinks": red_ref[6:7, hd:hd + nq],
            "g_attn": red_ref[4:5, 0:hd],
            "g_conv": red_ref[4:5, hd:d],
            "ln1_g": red_ref[2:3, :],
            "ln1_b": red_ref[3:4, :],
            "ln2_g": red_ref[0:1, :],
            "ln2_b": red_ref[1:2, :],
        }
        for i, nme in enumerate(names):
            w_ref, m_ref, v_ref = ins[3 * i:3 * i + 3]
            g_out, d_out, m_out, v_out = outs[4 * i:4 * i + 4]
            if nme == "conv_w":
                for tap, (row, base) in enumerate([(5, 0), (5, hd), (6, 0)]):
                    g = conv_tap(row, base)
                    delta, nm, nv = _adamw(w_ref[0, tap:tap + 1, :], g, m_ref[0, tap:tap + 1, :], v_ref[0, tap:tap + 1, :])
                    g_out[0, tap:tap + 1, :] = g
                    d_out[0, tap:tap + 1, :] = delta
                    m_out[0, tap:tap + 1, :] = nm
                    v_out[0, tap:tap + 1, :] = nv
            else:
                g = grads[nme]
                delta, nm, nv = _adamw(w_ref[...], g, m_ref[...], v_ref[...])
                g_out[...] = g
                d_out[...] = delta
                m_out[...] = nm
                v_out[...] = nv

    out_shape = []
    for nme in names:
        out_shape.extend([jax.ShapeDtypeStruct(params[nme][0].shape, F32)] * 4)
    outs = pl.pallas_call(
        body, name="adamw_small",
        in_specs=[_VMEM] * (1 + len(flat)), out_specs=[_VMEM] * len(out_shape),
        out_shape=out_shape,
    )(red, *flat)
    return {nme: tuple(outs[4 * i:4 * i + 4]) for i, nme in enumerate(names)}


def _rope_tables(pos_col):
    s = pos_col.shape[0]
    w = N_KV_HEADS * HEAD_DIM
    tb = min(512, s)
    inv_freq = (ROPE_THETA ** (-np.arange(0, ROT_DIM, 2, dtype=np.float32) / ROT_DIM)).astype(np.float32)

    def body(pos_ref, cos_ref, sin_ref):
        pos = pos_ref[...].astype(F32)
        lane = lax.broadcasted_iota(jnp.int32, (tb, PAIR), 1) & (HEAD_DIM - 1)
        fidx = lane & (ROT_DIM // 2 - 1)
        inv = jnp.zeros((tb, PAIR), F32)
        for k in range(ROT_DIM // 2):
            inv = jnp.where(fidx == k, float(inv_freq[k]), inv)
        ang = pos * inv
        rot = lane < ROT_DIM
        sin_v = jnp.sin(ang)
        cos_ref[...] = _tile_lanes(jnp.where(rot, jnp.cos(ang), 1.0), w // PAIR)
        sin_ref[...] = _tile_lanes(jnp.where(lane < ROT_DIM // 2, -sin_v, jnp.where(rot, sin_v, 0.0)), w // PAIR)

    return pl.pallas_call(
        body, name="rope_tables", grid=(s // tb,),
        in_specs=[pl.BlockSpec((tb, 1), lambda i: (i, 0))],
        out_specs=[pl.BlockSpec((tb, w), lambda i: (i, 0))] * 2,
        out_shape=[jax.ShapeDtypeStruct((s, w), F32)] * 2,
        compiler_params=_params(("parallel",)),
    )(pos_col)


def _in_proj(x, w_in_g, first_vec, n_shards, into, name):
    _, s, d = x.shape
    ns, _, ncol = w_in_g.shape
    tm = min(2 * TM, s)

    def body(first_ref, x_ref, w_ref, into_ref, o_ref):
        o_ref[...] = _dot(x_ref[...].astype(BF16), w_ref[...]).astype(BF16)

    shard = lambda j, first_ref: lax.rem(first_ref[0] + j, ns)
    grid_spec = pltpu.PrefetchScalarGridSpec(
        num_scalar_prefetch=1, grid=(s // tm, n_shards),
        in_specs=[pl.BlockSpec((None, tm, d), lambda i, j, first_ref: (0, i, 0)),
                  pl.BlockSpec((None, d, ncol), lambda i, j, first_ref: (shard(j, first_ref), 0, 0)), _ANY],
        out_specs=pl.BlockSpec((tm, ncol), lambda i, j, first_ref: (i, shard(j, first_ref))))
    return pl.pallas_call(
        body, name=name, grid_spec=grid_spec,
        out_shape=jax.ShapeDtypeStruct((s, ns * ncol), BF16),
        input_output_aliases={} if into is None else {3: 0},
        compiler_params=_params(("parallel", "arbitrary")),
    )(first_vec, x, w_in_g, first_vec if into is None else into)


PAIR = 2 * HEAD_DIM
KEYS = 2 * WINDOW


def _pair_operand(t_all, h):
    col = (h // 2) * PAIR
    lane = lax.broadcasted_iota(jnp.int32, (KEYS, PAIR), 1)
    own_low = h % 2 == 0
    mine = jnp.where((lane < HEAD_DIM) if own_low else (lane >= HEAD_DIM), t_all[:, col:col + PAIR], 0.0)
    other = pltpu.roll(mine, HEAD_DIM, 1)
    low, high = (mine, other) if own_low else (other, mine)
    return jnp.concatenate([low, high], axis=0).astype(BF16)


def _pair_grad(acc, h):
    lane = lax.broadcasted_iota(jnp.int32, (KEYS, PAIR), 1)
    low = jnp.where(lane < HEAD_DIM, acc[:KEYS], 0.0)
    high = jnp.where(lane >= HEAD_DIM, acc[KEYS:], 0.0)
    if h % 2 == 0:
        return low + pltpu.roll(high, HEAD_DIM, 1)
    return high + pltpu.roll(low, HEAD_DIM, 1)


N_PAIRS = N_KV_HEADS * GROUP // 2


def _all_probs(q, kk2s, first, sinks_ref):
    assert ATTN_SCALE == 0.125
    q = q * ATTN_SCALE
    qps, scores = [], []
    for pair in range(N_PAIRS):
        qp = q[:, pair * PAIR:(pair + 1) * PAIR].astype(BF16)
        qps.append(qp)
        scores.append(_dot_nt(qp, kk2s[pair // (GROUP // 2)]))
    qi = lax.broadcasted_iota(jnp.int32, (WINDOW, 2 * KEYS), 0)
    kj = lax.broadcasted_iota(jnp.int32, (WINDOW, 2 * KEYS), 1) & (KEYS - 1)
    rel = qi + WINDOW - kj
    valid = (rel >= 0) & (rel < WINDOW) & jnp.logical_not(first & (kj < WINDOW))
    bias = jnp.where(valid, 0.0, NEG_BIG)
    s = (jnp.stack(scores, axis=0) + bias[None]).reshape(N_PAIRS * WINDOW, 2 * KEYS)
    probs, p_sinks = [], []
    for t in range(2):
        st = s[:, t * KEYS:(t + 1) * KEYS]
        sink = jnp.concatenate([jnp.broadcast_to(sinks_ref[0:1, 2 * pair + t:2 * pair + t + 1], (WINDOW, 1))
                                for pair in range(N_PAIRS)], axis=0)
        m = jnp.maximum(jnp.max(st, axis=1, keepdims=True), sink)
        e = jnp.exp(st - m)
        e_sink = jnp.exp(sink - m)
        inv_l = 1.0 / (jnp.sum(e, axis=1, keepdims=True) + e_sink)
        probs.append(e * inv_l)
        p_sinks.append(e_sink * inv_l)
    return qps, jnp.concatenate(probs, axis=1), p_sinks


def _roped_qkv(cur_ref, prev_ref, cos_ref, sin_ref, cosp_ref, sinp_ref, qw, kvw):
    cur = cur_ref[...].astype(F32)
    cos, sin = cos_ref[...], sin_ref[...]
    cos_q, sin_q = _tile_lanes(cos, GROUP), _tile_lanes(sin, GROUP)
    q = _rope(cur[:, :qw], cos_q, sin_q, 1.0)
    prev = prev_ref[...].astype(F32)
    k_all = jnp.concatenate([_rope(prev[:, :kvw], cosp_ref[...], sinp_ref[...], 1.0),
                             _rope(cur[:, qw:qw + kvw], cos, sin, 1.0)], axis=0)
    v_all = jnp.concatenate([prev[:, kvw:], cur[:, qw + kvw:]], axis=0)
    return q, k_all, v_all, cos_q, sin_q


def _attention_fwd(proj, cos_t, sin_t, sinks):
    s = proj.shape[0]
    qw = GROUP * N_KV_HEADS * HEAD_DIM
    kvw = N_KV_HEADS * HEAD_DIM
    nb = s // WINDOW

    def body(cur_ref, prev_ref, cos_ref, sin_ref, cosp_ref, sinp_ref, sinks_ref, o_ref):
        first = pl.program_id(0) == 0
        q, k_all, v_all, _, _ = _roped_qkv(cur_ref, prev_ref, cos_ref, sin_ref, cosp_ref, sinp_ref, qw, kvw)
        kk2s = [_pair_operand(k_all, h) for h in range(N_KV_HEADS)]
        vv2s = [_pair_operand(v_all, h) for h in range(N_KV_HEADS)]
        _, probs, _ = _all_probs(q, kk2s, first, sinks_ref)
        probs = probs.astype(BF16)
        outs = [_dot(probs[pair * WINDOW:(pair + 1) * WINDOW], vv2s[pair // (GROUP // 2)]) for pair in range(N_PAIRS)]
        o_ref[...] = jnp.concatenate(outs, axis=1)

    tbl = pl.BlockSpec((WINDOW, kvw), lambda n: (n, 0))
    tbl_prev = pl.BlockSpec((WINDOW, kvw), lambda n: (jnp.maximum(n - 1, 0), 0))
    return pl.pallas_call(
        body, name="attention_fwd", grid=(nb,),
        in_specs=[pl.BlockSpec((WINDOW, qw + 2 * kvw), lambda n: (n, 0)),
                  pl.BlockSpec((WINDOW, 2 * kvw), lambda n: (jnp.maximum(n - 1, 0), (qw // (2 * kvw)))),
                  tbl, tbl, tbl_prev, tbl_prev, _VMEM],
        out_specs=pl.BlockSpec((WINDOW, qw), lambda n: (n, 0)),
        out_shape=jax.ShapeDtypeStruct((s, qw), F32),
        compiler_params=_params(("parallel",)),
    )(proj, proj, cos_t, sin_t, cos_t, sin_t, sinks)


def _conv_taps(cw_ref):
    return [jnp.concatenate([cw_ref[s, k:k + 1, :] for s in range(N_CHIPS)], axis=1) for k in range(3)]


def _shift_down(z, halo, steps):
    last = halo.shape[0]
    row = lax.broadcasted_iota(jnp.int32, z.shape, 0)
    out = pltpu.roll(z, steps, 0)
    for r in range(steps):
        out = jnp.where(row == r, halo[last - steps + r:last - steps + r + 1, :], out)
    return out


def _shift_up(z, halo, steps):
    rows = z.shape[0]
    row = lax.broadcasted_iota(jnp.int32, z.shape, 0)
    out = pltpu.roll(z, rows - steps, 0)
    for r in range(steps):
        out = jnp.where(row == rows - steps + r, halo[r:r + 1, :], out)
    return out


def _split_cbu(lo, hi, cw):
    lo, hi = lo.astype(F32), hi.astype(F32)
    c_gate = lo[:, :cw]
    b_gate = jnp.concatenate([lo[:, cw:], hi[:, :2 * cw - lo.shape[1]]], axis=1)
    u = hi[:, 2 * cw - lo.shape[1]:]
    return c_gate, b_gate, u


def _conv_norm(proj, attn, cw_full, g_ac):
    s, in_w = proj.shape
    cw = attn.shape[1]
    blk_w = in_w // 3
    tb = min(TB_CONV, s)

    def body(lo_ref, hi_ref, lo_h_ref, hi_h_ref, attn_ref, cw_ref, g_ref, mixed_ref, ac_ref, rstd_ref):
        i = pl.program_id(0)
        c_gate, b_gate, u = _split_cbu(lo_ref[...], hi_ref[...], cw)
        c_h, _, u_h = _split_cbu(lo_h_ref[...], hi_h_ref[...], cw)
        z = c_gate * u
        z_h = jnp.where(i == 0, 0.0, c_h * u_h)
        w0, w1, w2 = _conv_taps(cw_ref)
        y = w0 * _shift_down(z, z_h, 2) + w1 * _shift_down(z, z_h, 1) + w2 * z
        conv = b_gate * y
        a = attn_ref[...]
        r_a = lax.rsqrt(jnp.mean(a * a, axis=-1, keepdims=True) + RMS_EPS)
        r_c = lax.rsqrt(jnp.mean(conv * conv, axis=-1, keepdims=True) + RMS_EPS)
        g = g_ref[...]
        mixed_ref[...] = jnp.concatenate([a * r_a * g[:, :cw], conv * r_c * g[:, cw:]], axis=1).astype(BF16)
        ac_ref[...] = jnp.concatenate([a, conv], axis=1)
        rstd_ref[0] = r_a
        rstd_ref[1] = r_c

    halo_idx = lambda i: jnp.maximum(i * (tb // HALO_ROWS) - 1, 0)
    return pl.pallas_call(
        body, name="conv_norm", grid=(s // tb,),
        in_specs=[pl.BlockSpec((tb, blk_w), lambda i: (i, 1)),
                  pl.BlockSpec((tb, blk_w), lambda i: (i, 2)),
                  pl.BlockSpec((HALO_ROWS, blk_w), lambda i: (halo_idx(i), 1)),
                  pl.BlockSpec((HALO_ROWS, blk_w), lambda i: (halo_idx(i), 2)),
                  pl.BlockSpec((tb, cw), lambda i: (i, 0)),
                  _VMEM, _VMEM],
        out_specs=[pl.BlockSpec((tb, 2 * cw), lambda i: (i, 0)),
                   pl.BlockSpec((tb, 2 * cw), lambda i: (i, 0)),
                   pl.BlockSpec((2, tb, 1), lambda i: (0, i, 0))],
        out_shape=[jax.ShapeDtypeStruct((s, 2 * cw), BF16), jax.ShapeDtypeStruct((s, 2 * cw), F32),
                   jax.ShapeDtypeStruct((2, s, 1), F32)],
        compiler_params=_params(("parallel",)),
    )(proj, proj, proj, proj, attn, cw_full, g_ac)


def _out_proj_ln(mixed, w_out_g, x, ln_g, ln_b):
    s, d = mixed.shape
    tm = min(TM, s)
    tk = d
    nk = d // tk

    def body(a_ref, w_ref, x_ref, g_ref, b_ref, xhat_ref, h_ref, rstd_ref, acc):
        k = pl.program_id(1)
        _accumulate(acc, lambda: _dot(a_ref[...], w_ref[...]), k, nk)

        @pl.when(k == nk - 1)
        def _():
            def rows_fn(rows):
                xhat, rstd = _ln_fwd(ALPHA * x_ref[rows, :] + acc[rows, :])
                xhat_ref[rows, :] = xhat
                h_ref[rows, :] = (xhat * g_ref[...] + b_ref[...]).astype(BF16)
                rstd_ref[rows, :] = rstd

            _for_row_chunks(tm, rows_fn)

    row = pl.BlockSpec((tm, d), lambda i, k: (i, 0))
    return pl.pallas_call(
        body, name="out_proj_ln", grid=(s // tm, nk),
        in_specs=[pl.BlockSpec((tm, tk), lambda i, k: (i, k)),
                  pl.BlockSpec((tk, d), lambda i, k: (k, 0)),
                  pl.BlockSpec((None, tm, d), lambda i, k: (0, i, 0)),
                  _VMEM, _VMEM],
        out_specs=[row, row, pl.BlockSpec((tm, 1), lambda i, k: (i, 0))],
        out_shape=[jax.ShapeDtypeStruct((s, d), F32), jax.ShapeDtypeStruct((s, d), BF16),
                   jax.ShapeDtypeStruct((s, 1), F32)],
        scratch_shapes=[pltpu.VMEM((tm, d), F32)],
        compiler_params=_params(("parallel", "arbitrary")),
    )(mixed, w_out_g, x, ln_g, ln_b)


def _gate_up(h1, w_gu_g, first_vec, n_shards, into, name):
    s, d = h1.shape
    ns, _, fs2 = w_gu_g.shape
    fs = fs2 // 2
    tm = min(TM, s)

    def body(first_ref, h_ref, w_ref, act_in, ab_in, act_ref, ab_ref):
        gu = _dot(h_ref[...], w_ref[...])
        g, u = gu[:, :fs], gu[:, fs:]
        sg = _sigmoid(g)
        silu = g * sg
        act_ref[...] = (silu * u).astype(BF16)
        ab_ref[:, :fs] = (u * (sg * (1.0 + g * (1.0 - sg)))).astype(BF16)
        ab_ref[:, fs:] = silu.astype(BF16)

    shard = lambda j, first_ref: lax.rem(first_ref[0] + j, ns)
    grid_spec = pltpu.PrefetchScalarGridSpec(
        num_scalar_prefetch=1, grid=(s // tm, n_shards),
        in_specs=[pl.BlockSpec((tm, d), lambda i, j, first_ref: (i, 0)),
                  pl.BlockSpec((None, d, fs2), lambda i, j, first_ref: (shard(j, first_ref), 0, 0)), _ANY, _ANY],
        out_specs=[pl.BlockSpec((tm, fs), lambda i, j, first_ref: (i, shard(j, first_ref))),
                   pl.BlockSpec((tm, fs2), lambda i, j, first_ref: (i, shard(j, first_ref)))])
    return pl.pallas_call(
        body, name=name, grid_spec=grid_spec,
        out_shape=[jax.ShapeDtypeStruct((s, ns * fs), BF16), jax.ShapeDtypeStruct((s, ns * fs2), BF16)],
        input_output_aliases={} if into is None else {3: 0, 4: 1},
        compiler_params=_params(("parallel", "arbitrary")),
    )(first_vec, h1, w_gu_g, *((first_vec, first_vec) if into is None else into))


def _down_ln_loss(act, w_down_g, xhat1, ln1_g, ln1_b, ln2_g, ln2_b, target):
    s, f = act.shape
    d = xhat1.shape[1]
    tm = min(TM, s)
    tk = f // N_CHIPS
    nk = f // tk

    def body(a_ref, w_ref, xh_ref, g1_ref, b1_ref, g2_ref, b2_ref, t_ref, dpre_ref, dpre16_ref, loss_ref, gg_ref, gb_ref,
             acc):
        i, k = pl.program_id(0), pl.program_id(1)
        _accumulate(acc, lambda: _dot(a_ref[...], w_ref[...]), k, nk)

        @pl.when(k == nk - 1)
        def _():
            @pl.when(i == 0)
            def _():
                loss_ref[...] = jnp.zeros_like(loss_ref)
                gg_ref[...] = jnp.zeros_like(gg_ref)
                gb_ref[...] = jnp.zeros_like(gb_ref)

            def rows_fn(rows):
                h1 = xh_ref[rows, :] * g1_ref[...] + b1_ref[...]
                xhat, rstd = _ln_fwd(ALPHA * h1 + acc[rows, :])
                g2 = g2_ref[...]
                diff = xhat * g2 + b2_ref[...] - t_ref[rows, :]
                dy = diff * (1.0 / d)
                dpre = _ln_bwd(dy, xhat, rstd, g2)
                dpre_ref[rows, :] = dpre
                dpre16_ref[rows, :] = dpre.astype(BF16)
                sq = jnp.sum(jnp.sum(diff * diff, axis=1, keepdims=True), axis=0, keepdims=True)
                loss_ref[...] += jnp.broadcast_to(sq * (0.5 / d), (1, 128))
                gg_ref[...] += jnp.sum(dy * xhat, axis=0, keepdims=True)
                gb_ref[...] += jnp.sum(dy, axis=0, keepdims=True)

            _for_row_chunks(tm, rows_fn)

    row = pl.BlockSpec((tm, d), lambda i, k: (i, 0))
    vec = pl.BlockSpec((1, d), lambda i, k: (0, 0))
    return pl.pallas_call(
        body, name="down_ln_loss", grid=(s // tm, nk),
        in_specs=[pl.BlockSpec((tm, tk), lambda i, k: (i, k)),
                  pl.BlockSpec((tk, d), lambda i, k: (k, 0)),
                  row, _VMEM, _VMEM, _VMEM, _VMEM,
                  pl.BlockSpec((None, tm, d), lambda i, k: (0, i, 0))],
        out_specs=[row, row, pl.BlockSpec((1, 128), lambda i, k: (0, 0)), vec, vec],
        out_shape=[jax.ShapeDtypeStruct((s, d), F32), jax.ShapeDtypeStruct((s, d), BF16),
                   jax.ShapeDtypeStruct((1, 128), F32), jax.ShapeDtypeStruct((1, d), F32),
                   jax.ShapeDtypeStruct((1, d), F32)],
        scratch_shapes=[pltpu.VMEM((tm, d), F32)],
        compiler_params=_params(("arbitrary", "arbitrary")),
    )(act, w_down_g, xhat1, ln1_g, ln1_b, ln2_g, ln2_b, target)


def _dact_silu_bwd(dpre2, w_down_g, ab):
    s, d = dpre2.shape
    fs2 = ab.shape[1] // N_CHIPS
    fs = fs2 // 2
    tm = min(TM, s)

    def body(dp_ref, w_ref, ab_ref, dgu_ref):
        d_act = _dot_nt(dp_ref[...], w_ref[...])
        dgu_ref[:, :fs] = (d_act * ab_ref[:, :fs].astype(F32)).astype(BF16)
        dgu_ref[:, fs:] = (d_act * ab_ref[:, fs:].astype(F32)).astype(BF16)

    blk = pl.BlockSpec((tm, fs2), lambda j, i: (i, j))
    return pl.pallas_call(
        body, name="dact_silu_bwd", grid=(N_CHIPS, s // tm),
        in_specs=[pl.BlockSpec((tm, d), lambda j, i: (i, 0)),
                  pl.BlockSpec((fs, d), lambda j, i: (j, 0)), blk],
        out_specs=blk,
        out_shape=jax.ShapeDtypeStruct(ab.shape, BF16),
        compiler_params=_params(("parallel", "parallel")),
    )(dpre2, w_down_g, ab)


def _grad_rows(a, b, after, name, row_blocks=1):
    s, m = a.shape
    n = b.shape[1]
    ms = m // N_CHIPS
    tmw = ms // row_blocks
    tk = min(TK_TOK, s)
    nk = s // tk

    def body(a_ref, b_ref, after_ref, o_ref, acc):
        k = pl.program_id(2)
        _accumulate(acc, lambda: _dot_tn(a_ref[...].astype(BF16), b_ref[...].astype(BF16)), k, nk)

        @pl.when(k == nk - 1)
        def _():
            o_ref[...] = acc[...].astype(BF16)

    return pl.pallas_call(
        body, name=name, grid=(N_CHIPS, row_blocks, nk),
        in_specs=[pl.BlockSpec((tk, tmw), lambda j, r, k: (k, j * row_blocks + r)),
                  pl.BlockSpec((tk, n), lambda j, r, k: (k, 0)), _ANY],
        out_specs=pl.BlockSpec((None, tmw, n), lambda j, r, k: (j, r, 0)),
        out_shape=jax.ShapeDtypeStruct((N_CHIPS, ms, n), BF16),
        scratch_shapes=[pltpu.VMEM((tmw, n), F32)],
        compiler_params=_params(("parallel", "parallel", "arbitrary")),
    )(a, b, after)


def _grad_cols(a, bs, after, name, a_3d=False, row_blocks=2, shard=None):
    s, m = a.shape[-2:]
    n = bs[0].shape[1]
    ns = n // N_CHIPS
    nb = len(bs)
    tmw = m // row_blocks
    tk = min(TK_TOK, s)
    nk = s // tk

    def body(*refs):
        a_ref, b_refs, o_refs, accs = refs[0], refs[1:1 + nb], refs[2 + nb:2 + 2 * nb], refs[2 + 2 * nb:]
        k = pl.program_id(2)
        for b_ref, acc in zip(b_refs, accs):
            _accumulate(acc, lambda b_ref=b_ref: _dot_tn(a_ref[...].astype(BF16), b_ref[...].astype(BF16)), k, nk)

        @pl.when(k == nk - 1)
        def _():
            for o_ref, acc in zip(o_refs, accs):
                o_ref[...] = acc[...].astype(BF16)

    if a_3d:
        a_spec = pl.BlockSpec((None, tk, tmw), lambda j, r, k: (0, k, r))
    else:
        a_spec = pl.BlockSpec((tk, tmw), lambda j, r, k: (k, r))
    return _call_with_adamw(
        body, name, (N_CHIPS, row_blocks, nk),
        [a_spec] + [pl.BlockSpec((tk, ns), lambda j, r, k: (k, j))] * nb + [_ANY],
        [pl.BlockSpec((None, tmw, ns), lambda j, r, k: (j, r, 0))] * nb,
        [jax.ShapeDtypeStruct((N_CHIPS, m, ns), BF16)] * nb,
        [pltpu.VMEM((tmw, ns), F32)] * nb, ("parallel", "parallel", "arbitrary"), (a, *bs, after), shard)


def _dh1_ln_bwd(d_gu, w_gu_g, dpre2, xhat1, rstd1, ln1_g, after):
    s = d_gu.shape[0]
    d = dpre2.shape[1]
    hd = d // 2
    fs = w_gu_g.shape[2]
    tm = min(TM, s)

    def body(dgu_ref, w_ref, dp2_ref, xh_ref, rs_ref, g_ref, after_ref, dpre_ref, gg_ref, gb_ref, acc_lo, acc_hi):
        i, j, half = pl.program_id(0), pl.program_id(1), pl.program_id(2)

        def product():
            return _dot_nt(dgu_ref[...], w_ref[...])

        @pl.when(half == 0)
        def _():
            _accumulate(acc_lo, product, j, N_CHIPS)

        @pl.when(half == 1)
        def _():
            _accumulate(acc_hi, product, j, N_CHIPS)

        @pl.when((j == N_CHIPS - 1) & (half == 1))
        def _():
            @pl.when(i == 0)
            def _():
                gg_ref[...] = jnp.zeros_like(gg_ref)
                gb_ref[...] = jnp.zeros_like(gb_ref)

            def rows_fn(rows):
                dh = jnp.concatenate([acc_lo[rows, :], acc_hi[rows, :]], axis=1) + ALPHA * dp2_ref[rows, :]
                xhat = xh_ref[rows, :]
                dpre_ref[rows, :] = _ln_bwd(dh, xhat, rs_ref[rows, :], g_ref[...])
                gg_ref[...] += jnp.sum(dh * xhat, axis=0, keepdims=True)
                gb_ref[...] += jnp.sum(dh, axis=0, keepdims=True)

            _for_row_chunks(tm, rows_fn)

    row = pl.BlockSpec((tm, d), lambda i, j, h: (i, 0))
    vec = pl.BlockSpec((1, d), lambda i, j, h: (0, 0))
    act_blk = pl.BlockSpec((tm, fs), lambda i, j, h: (i, j))
    w_blk = pl.BlockSpec((None, hd, fs), lambda i, j, h: (j, h, 0))
    return pl.pallas_call(
        body, name="dh1_ln_bwd", grid=(s // tm, N_CHIPS, 2),
        in_specs=[act_blk, w_blk, row, row, pl.BlockSpec((tm, 1), lambda i, j, h: (i, 0)), _VMEM, _ANY],
        out_specs=[row, vec, vec],
        out_shape=[jax.ShapeDtypeStruct((s, d), F32), jax.ShapeDtypeStruct((1, d), F32),
                   jax.ShapeDtypeStruct((1, d), F32)],
        scratch_shapes=[pltpu.VMEM((tm, hd), F32)] * 2,
        compiler_params=_params(("arbitrary", "arbitrary", "arbitrary")),
    )(d_gu, w_gu_g, dpre2, xhat1, rstd1, ln1_g, after)


def _dmixed_rms_bwd(dpre1, w_out_g, ac, rstd, g_ac):
    s, d = dpre1.shape
    hd = d // 2
    tm = min(TM, s)

    def body(dp_ref, w_ref, ac_ref, rs_ref, g_ref, dac_ref, gg_ref):
        i = pl.program_id(1)
        dm = _dot_nt(dp_ref[...].astype(BF16), w_ref[...])
        pre = ac_ref[...]
        r = rs_ref[...]
        gdm = dm * g_ref[...]
        dac_ref[...] = r * gdm - pre * (r * r * r) * jnp.mean(gdm * pre, axis=-1, keepdims=True)
        gg = jnp.sum(dm * pre * r, axis=0, keepdims=True)

        @pl.when(i == 0)
        def _():
            gg_ref[...] = gg

        @pl.when(i > 0)
        def _():
            gg_ref[...] += gg

    return pl.pallas_call(
        body, name="dmixed_rms_bwd", grid=(2, s // tm),
        in_specs=[pl.BlockSpec((tm, d), lambda h, i: (i, 0)),
                  pl.BlockSpec((hd, d), lambda h, i: (h, 0)),
                  pl.BlockSpec((tm, hd), lambda h, i: (i, h)),
                  pl.BlockSpec((None, tm, 1), lambda h, i: (h, i, 0)),
                  pl.BlockSpec((1, hd), lambda h, i: (0, h))],
        out_specs=[pl.BlockSpec((tm, hd), lambda h, i: (i, h)),
                   pl.BlockSpec((1, hd), lambda h, i: (0, h))],
        out_shape=[jax.ShapeDtypeStruct((s, d), F32), jax.ShapeDtypeStruct((1, d), F32)],
        compiler_params=_params(("arbitrary", "arbitrary")),
    )(dpre1, w_out_g, ac, rstd, g_ac)


def _attention_bwd(proj, d_ac, cos_t, sin_t, sinks, after, shard):
    s = proj.shape[0]
    qw = GROUP * N_KV_HEADS * HEAD_DIM
    kvw = N_KV_HEADS * HEAD_DIM
    nb = s // WINDOW
    nq = GROUP * N_KV_HEADS

    def body(cur_ref, prev_ref, do_ref, cos_ref, sin_ref, cosp_ref, sinp_ref, sinks_ref, after_ref,
             dq_ref, dcur_ref, dprev_ref, dsink_ref):
        n = pl.program_id(0)
        first = n == 0
        q, k_all, v_all, cos_q, sin_q = _roped_qkv(cur_ref, prev_ref, cos_ref, sin_ref, cosp_ref, sinp_ref, qw, kvw)
        kk2s = [_pair_operand(k_all, h) for h in range(N_KV_HEADS)]
        vv2s = [_pair_operand(v_all, h) for h in range(N_KV_HEADS)]
        qps, probs, p_sinks = _all_probs(q, kk2s, first, sinks_ref)
        dops = [do_ref[:, pair * PAIR:(pair + 1) * PAIR].astype(BF16) for pair in range(N_PAIRS)]
        d_probs = jnp.concatenate([_dot_nt(dops[pair], vv2s[pair // (GROUP // 2)]) for pair in range(N_PAIRS)], axis=0)
        d_s, ds_sinks = [], []
        for t in range(2):
            cols = slice(t * KEYS, (t + 1) * KEYS)
            delta = jnp.sum(probs[:, cols] * d_probs[:, cols], axis=1, keepdims=True)
            d_s.append(probs[:, cols] * (d_probs[:, cols] - delta))
            ds_sinks.append(-p_sinks[t] * delta)
        d_s = jnp.concatenate(d_s, axis=1).astype(BF16)
        probs = probs.astype(BF16)
        dq_parts, dk_tiles, dv_tiles, dsink_parts = [], [], [], []
        for h in range(N_KV_HEADS):
            dkk2, dvv2 = None, None
            for p in range(GROUP // 2):
                pair = (GROUP // 2) * h + p
                rows = slice(pair * WINDOW, (pair + 1) * WINDOW)
                dq_parts.append(_dot(d_s[rows], kk2s[h]) * ATTN_SCALE)
                dk_term = _dot_tn(d_s[rows], qps[pair])
                dv_term = _dot_tn(probs[rows], dops[pair])
                dkk2 = dk_term if dkk2 is None else dkk2 + dk_term
                dvv2 = dv_term if dvv2 is None else dvv2 + dv_term
                dsink_parts.extend([jnp.sum(ds_sinks[t][rows], axis=0, keepdims=True) for t in range(2)])
            dk_tiles.append(_pair_grad(dkk2, h))
            dv_tiles.append(_pair_grad(dvv2, h))
        dq_ref[...] = _rope(jnp.concatenate(dq_parts, axis=1), cos_q, sin_q, -1.0)
        dk = jnp.concatenate([dk_tiles[0] + dk_tiles[1], dk_tiles[2] + dk_tiles[3]], axis=1)
        dv = jnp.concatenate([dv_tiles[0] + dv_tiles[1], dv_tiles[2] + dv_tiles[3]], axis=1)
        dprev_ref[...] = jnp.concatenate([dk[:WINDOW], dv[:WINDOW]], axis=1)
        dcur_ref[...] = jnp.concatenate([dk[WINDOW:], dv[WINDOW:]], axis=1)
        dsink = jnp.concatenate(dsink_parts, axis=1)

        @pl.when(first)
        def _():
            dsink_ref[...] = dsink

        @pl.when(n > 0)
        def _():
            dsink_ref[...] += dsink

    tbl = pl.BlockSpec((WINDOW, kvw), lambda n: (n, 0))
    tbl_prev = pl.BlockSpec((WINDOW, kvw), lambda n: (jnp.maximum(n - 1, 0), 0))
    kv_blk = pl.BlockSpec((WINDOW, 2 * kvw), lambda n: (n, 0))
    return _call_with_adamw(
        body, "attention_bwd", (nb,),
        [pl.BlockSpec((WINDOW, qw + 2 * kvw), lambda n: (n, 0)),
         pl.BlockSpec((WINDOW, 2 * kvw), lambda n: (jnp.maximum(n - 1, 0), (qw // (2 * kvw)))),
         pl.BlockSpec((WINDOW, qw), lambda n: (n, 0)),
         tbl, tbl, tbl_prev, tbl_prev, _VMEM, _ANY],
        [pl.BlockSpec((WINDOW, qw), lambda n: (n, 0)), kv_blk, kv_blk, pl.BlockSpec((1, nq), lambda n: (0, 0))],
        [jax.ShapeDtypeStruct((s, qw), F32), jax.ShapeDtypeStruct((s, 2 * kvw), F32),
         jax.ShapeDtypeStruct((s, 2 * kvw), F32), jax.ShapeDtypeStruct((1, nq), F32)],
        [], ("arbitrary",), (proj, proj, d_ac, cos_t, sin_t, cos_t, sin_t, sinks, after), shard)


def _dproj_assemble(proj, d_ac, dq, dkv_cur, dkv_prev, cos_t, sin_t, cw_full):
    s, in_w = proj.shape
    cw = dq.shape[1]
    kvw = N_KV_HEADS * HEAD_DIM
    blk_w = in_w // 3
    tb = WINDOW
    nb = s // tb

    def body(lo_ref, hi_ref, lo_p_ref, hi_p_ref, lo_n_ref, hi_n_ref, dconv_ref, dconv_n_ref,
             dq_ref, dcur_ref, dprev_n_ref, cos_ref, sin_ref, cw_ref, dproj_ref, gcw_ref):
        i = pl.program_id(0)
        last = i == nb - 1
        c_gate, b_gate, u = _split_cbu(lo_ref[...], hi_ref[...], cw)
        c_p, _, u_p = _split_cbu(lo_p_ref[...], hi_p_ref[...], cw)
        _, b_n, _ = _split_cbu(lo_n_ref[...], hi_n_ref[...], cw)
        z = c_gate * u
        z_p = jnp.where(i == 0, 0.0, c_p * u_p)
        z1 = _shift_down(z, z_p, 1)
        z2 = _shift_down(z, z_p, 2)
        w0, w1, w2 = _conv_taps(cw_ref)
        y = w0 * z2 + w1 * z1 + w2 * z
        d_conv = dconv_ref[...]
        d_b = d_conv * y
        d_y = d_conv * b_gate
        d_y_n = jnp.where(last, 0.0, dconv_n_ref[...] * b_n[:dconv_n_ref.shape[0]])
        d_z = w2 * d_y + w1 * _shift_up(d_y, d_y_n, 1) + w0 * _shift_up(d_y, d_y_n, 2)
        d_c = d_z * u
        d_u = d_z * c_gate
        gcw = jnp.concatenate([jnp.sum(d_y * z2, axis=0, keepdims=True), jnp.sum(d_y * z1, axis=0, keepdims=True),
                               jnp.sum(d_y * z, axis=0, keepdims=True)], axis=0)

        @pl.when(i == 0)
        def _():
            gcw_ref[...] = gcw

        @pl.when(i > 0)
        def _():
            gcw_ref[...] += gcw

        dkv = dcur_ref[...] + jnp.where(last, 0.0, dprev_n_ref[...])
        dk = _rope(dkv[:, :kvw], cos_ref[...], sin_ref[...], -1.0)
        dproj_ref[...] = jnp.concatenate([dq_ref[...], dk, dkv[:, kvw:], d_c, d_b, d_u], axis=1).astype(BF16)

    prev_halo = lambda i: jnp.maximum(i * (tb // HALO_ROWS) - 1, 0)
    next_halo = lambda i: jnp.minimum((i + 1) * (tb // HALO_ROWS), s // HALO_ROWS - 1)
    next8 = lambda i: jnp.minimum((i + 1) * (tb // 8), s // 8 - 1)
    nxt = lambda i: jnp.minimum(i + 1, nb - 1)
    return pl.pallas_call(
        body, name="dproj_assemble", grid=(nb,),
        in_specs=[pl.BlockSpec((tb, blk_w), lambda i: (i, 1)),
                  pl.BlockSpec((tb, blk_w), lambda i: (i, 2)),
                  pl.BlockSpec((HALO_ROWS, blk_w), lambda i: (prev_halo(i), 1)),
                  pl.BlockSpec((HALO_ROWS, blk_w), lambda i: (prev_halo(i), 2)),
                  pl.BlockSpec((HALO_ROWS, blk_w), lambda i: (next_halo(i), 1)),
                  pl.BlockSpec((HALO_ROWS, blk_w), lambda i: (next_halo(i), 2)),
                  pl.BlockSpec((tb, cw), lambda i: (i, 1)),
                  pl.BlockSpec((8, cw), lambda i: (next8(i), 1)),
                  pl.BlockSpec((tb, cw), lambda i: (i, 0)),
                  pl.BlockSpec((tb, 2 * kvw), lambda i: (i, 0)),
                  pl.BlockSpec((tb, 2 * kvw), lambda i: (nxt(i), 0)),
                  pl.BlockSpec((tb, kvw), lambda i: (i, 0)),
                  pl.BlockSpec((tb, kvw), lambda i: (i, 0)),
                  _VMEM],
        out_specs=[pl.BlockSpec((tb, in_w), lambda i: (i, 0)),
                   pl.BlockSpec((3, cw), lambda i: (0, 0))],
        out_shape=[jax.ShapeDtypeStruct((s, in_w), BF16), jax.ShapeDtypeStruct((3, cw), F32)],
        compiler_params=_params(("arbitrary",)),
    )(proj, proj, proj, proj, proj, proj, d_ac, d_ac, dq, dkv_cur, dkv_prev, cos_t, sin_t, cw_full)


def _dx(d_proj, w_in_g, dpre1, after, shard):
    s, in_w = d_proj.shape
    ns, d, ncol = w_in_g.shape
    tm = min(TM, s)

    def body(dp_ref, w_ref, r_ref, after_ref, o_ref, acc):
        j = pl.program_id(1)
        _accumulate(acc, lambda: _dot_nt(dp_ref[...], w_ref[...]), j, ns)

        @pl.when(j == ns - 1)
        def _():
            o_ref[...] = acc[...] + ALPHA * r_ref[...]

    return _call_with_adamw(
        body, "dx", (s // tm, ns),
        [pl.BlockSpec((tm, ncol), lambda i, j: (i, j)),
         pl.BlockSpec((None, d, ncol), lambda i, j: (j, 0, 0)),
         pl.BlockSpec((tm, d), lambda i, j: (i, 0)), _ANY],
        [pl.BlockSpec((None, tm, d), lambda i, j: (0, i, 0))], [jax.ShapeDtypeStruct((1, s, d), F32)],
        [pltpu.VMEM((tm, d), F32)], ("parallel", "arbitrary"), (d_proj, w_in_g, dpre1, after), shard)


def kernel(x, positions, w_in, conv_w, sinks, g_attn, g_conv, w_out, ln1_g, ln1_b, w_gate, w_up, w_down, ln2_g, ln2_b, loss_target, m_w_in, m_conv_w, m_sinks, m_g_attn, m_g_conv, m_w_out, m_ln1_g, m_ln1_b, m_w_gate, m_w_up, m_w_down, m_ln2_g, m_ln2_b, v_w_in, v_conv_w, v_sinks, v_g_attn, v_g_conv, v_w_out, v_ln1_g, v_ln1_b, v_w_gate, v_w_up, v_w_down, v_ln2_g, v_ln2_b):
    s = x.shape[1]
    d = x.shape[2]

    chip_vec = _chip_id(lax.axis_index("x"), lax.axis_index("y")).astype(jnp.int32).reshape(1)
    wnames = ["w_in", "w_out", "w_gu", "w_down"]
    buf_in = _cast_weight(w_in, chip_vec, chip_vec, "cast_w_in")
    flight_in, token_in = _gather_start([buf_in], chip_vec, "gather_start_w_in")
    cw_buf = lax.dynamic_update_slice(jnp.zeros((N_CHIPS,) + conv_w.shape[1:], F32), conv_w, (chip_vec[0], 0, 0))
    cw_flight = _flight_start("conv_w_start", [cw_buf], _conv_w_plan(), 3, token_in)
    started = cw_flight[2][0]
    buf_gu = _cast_weight(w_gate, chip_vec, started, "cast_w_gate", 0, 2)
    buf_gu = _cast_weight(w_up, chip_vec, buf_gu, "cast_w_up", 1, 2)
    bufs = [_cast_weight(w_out, chip_vec, started, "cast_w_out"), buf_gu,
            _cast_weight(w_down, chip_vec, started, "cast_w_down")]
    flights_rest, token = _gather_start(bufs, token_in, "gather_start_rest")
    flights = flight_in + flights_rest

    def gathered(i, after):
        send_sems, recv_sems, buf = flights[i]
        buf = _gather_wait(send_sems, recv_sems, buf, after, "gather_wait_" + wnames[i])
        return _sibling_fill(buf, "sibling_fill_" + wnames[i])

    g_ac = jnp.concatenate([g_attn, g_conv], axis=1)

    proj_own = _in_proj(x, _after(flights[0][2], token), chip_vec, 1, None, "in_proj_own")
    cos_t, sin_t = _rope_tables(positions.reshape(s, 1) + token[0:1, 0:1].astype(jnp.int32))
    w_in_g = gathered(0, _after(cos_t, proj_own))
    proj = _in_proj(x, w_in_g, chip_vec + 1, N_CHIPS - 1, proj_own, "in_proj_rest")
    send_sems, recv_sems, buf_out = flights[1]
    buf_out = _gather_wait(send_sems, recv_sems, buf_out, proj, "gather_wait_w_out")
    fill_out = _flight_start("fill_start_w_out", [buf_out], _fill_plan(1), 3, chip_vec)
    attn = _attention_fwd(_after(proj, fill_out[2][0]), cos_t, sin_t, sinks)
    (cw_full,) = _flight_wait("conv_w_wait", cw_flight, _conv_w_plan(), attn)
    mixed, ac, rstd_ac = _conv_norm(proj, attn, cw_full, g_ac)
    (w_out_g,) = _flight_wait("fill_wait_w_out", fill_out, _fill_plan(1), mixed)
    w_out_full = w_out_g.reshape(d, d)
    xhat1, h1, rstd1 = _out_proj_ln(mixed, w_out_full, x, ln1_g, ln1_b)
    send_sems, recv_sems, buf_gu = flights[2]
    buf_gu = _gather_wait(send_sems, recv_sems, buf_gu, h1, "gather_wait_w_gu")
    fill_gu = _flight_start("fill_start_w_gu", [buf_gu], _fill_plan(1), 3, chip_vec)
    own = _gate_up(h1, fill_gu[2][0], chip_vec, 1, None, "gate_up_own")
    (w_gu_g,) = _flight_wait("fill_wait_w_gu", fill_gu, _fill_plan(1), own[0])
    some = _gate_up(h1, w_gu_g, chip_vec + 1, N_CHIPS - 2, own, "gate_up_rest")
    send_sems, recv_sems, buf_down = flights[3]
    buf_down = _gather_wait(send_sems, recv_sems, buf_down, some[0], "gather_wait_w_down")
    fill_down = _flight_start("fill_start_w_down", [buf_down], _fill_plan(1), 3, chip_vec)
    act, ab = _gate_up(h1, _after(w_gu_g, fill_down[2][0]), chip_vec + N_CHIPS - 1, 1, some, "gate_up_last")
    (w_down_g,) = _flight_wait("fill_wait_w_down", fill_down, _fill_plan(1), act)
    w_down_full = w_down_g.reshape(-1, d)
    dpre2, dpre2_16, loss_part, g_ln2_g, g_ln2_b = _down_ln_loss(act, w_down_full, xhat1, ln1_g, ln1_b, ln2_g, ln2_b,
                                                                 loss_target)

    cvec = lax.axis_index("c").astype(jnp.int32).reshape(1)

    def exchange_begin(parts, nme):
        bufs = []
        for part in parts:
            ns, r, cdim = part.shape
            bufs.extend([part, lax.empty((ns, r // 2, cdim), part.dtype)])
        return _flight_start("exchange_start_" + nme, bufs, _exchange_plan(len(parts)), len(parts), cvec)

    def exchange_end(flight, n_parts, after, nme):
        bufs = _flight_wait("exchange_wait_" + nme, flight, _exchange_plan(n_parts), after)
        return [(bufs[2 * w], bufs[2 * w + 1]) for w in range(n_parts)]

    def scatter_begin(part, got, nme):
        return _scatter_start(_add_halves(part, got, cvec, "add_halves_" + nme), "scatter_start_" + nme)

    d_gu = _dact_silu_bwd(dpre2_16, w_down_full, ab)
    p_down = _grad_rows(act, dpre2_16, d_gu, "grad_w_down")
    x_down = exchange_begin([p_down], "w_down")
    (p_gu,) = _grad_cols(h1, [d_gu], x_down[2][0], "grad_w_gate_up")
    ((p_down, got),) = exchange_end(x_down, 1, p_gu, "w_down")
    f_down = scatter_begin(p_down, got, "w_down")
    x_gu = exchange_begin([_after(p_gu, f_down[2])], "w_gu")
    dpre1, g_ln1_g, g_ln1_b = _dh1_ln_bwd(d_gu, w_gu_g, dpre2, xhat1, rstd1, ln1_g, x_gu[2][0])
    ((p_gu, got),) = exchange_end(x_gu, 1, dpre1, "w_gu")
    f_gu = scatter_begin(p_gu, got, "w_gu")
    d_ac, g_g_ac = _dmixed_rms_bwd(_after(dpre1, f_gu[2]), w_out_full, ac, rstd_ac, g_ac)
    pos_vec = jnp.concatenate([chip_vec, cvec])
    sums, land = _scatter_wait(*f_down, d_ac, "scatter_wait_w_down")
    c_down = _flight_start("complete_start_w_down", [sums, land], _complete_plan(1), 4, cvec)
    p_out = _grad_rows(mixed, dpre1, c_down[2][1], "grad_w_out")
    x_out = exchange_begin([p_out], "w_out")
    sums, land = _flight_wait("complete_wait_w_down", c_down, _complete_plan(1), x_out[2][0])
    dq, dkv_cur, dkv_prev, g_sinks, *new_w_down = _attention_bwd(
        proj, d_ac, cos_t, sin_t, sinks, x_out[2][0], (w_down, m_w_down, v_w_down, land, sums, pos_vec, 0))
    ((p_out, got),) = exchange_end(x_out, 1, dq, "w_out")
    f_out = scatter_begin(p_out, got, "w_out")
    sums, land = _scatter_wait(*f_gu, f_out[2], "scatter_wait_w_gu")
    c_gu = _flight_start("complete_start_w_gu", [sums, land], _complete_plan(1), 4, cvec)
    d_proj, g_conv_w = _dproj_assemble(proj, _after(d_ac, c_gu[2][1]), dq, dkv_cur, dkv_prev, cos_t, sin_t, cw_full)
    red = _allreduce_small(g_ln2_g, g_ln2_b, g_ln1_g, g_ln1_b, g_g_ac, g_conv_w, g_sinks, loss_part, d_proj)
    sums_gu, land_gu = _flight_wait("complete_wait_w_gu", c_gu, _complete_plan(1), red)
    p_in, *new_w_gate = _grad_cols(x, [d_proj], red, "grad_w_in", a_3d=True,
                                   shard=(w_gate, m_w_gate, v_w_gate, land_gu, sums_gu, pos_vec, 0))
    x_in = exchange_begin([p_in], "w_in")
    sums, land = _scatter_wait(*f_out, x_in[2][0], "scatter_wait_w_out")
    c_out = _flight_start("complete_start_w_out", [sums, land], _complete_plan(1), 4, cvec)
    new_w_up = _adamw_shard(w_up, m_w_up, v_w_up, _after(land_gu, c_out[2][1]), sums_gu, pos_vec, "adamw_w_up", 1)
    ((p_in, got),) = exchange_end(x_in, 1, new_w_up[0], "w_in")
    f_in = scatter_begin(p_in, got, "w_in")
    (grad_x,) = _dx(d_proj, w_in_g, dpre1, f_in[2], None)

    big = {"w_down": new_w_down, "w_gate": new_w_gate, "w_up": new_w_up}
    sums, land = _scatter_wait(*f_in, grad_x, "scatter_wait_w_in")
    c_in = _flight_start("complete_start_w_in", [sums, land], _complete_plan(1), 4, cvec)
    sums, land = _flight_wait("complete_wait_w_out", c_out, _complete_plan(1), c_in[2][1])
    big["w_out"] = _adamw_shard(w_out, m_w_out, v_w_out, land, sums, pos_vec, "adamw_w_out")
    sums, land = _flight_wait("complete_wait_w_in", c_in, _complete_plan(1), big["w_out"][0])
    big["w_in"] = _adamw_shard(w_in, m_w_in, v_w_in, land, sums, pos_vec, "adamw_w_in")
    small = _adamw_small(red, {
        "sinks": (sinks, m_sinks, v_sinks), "g_attn": (g_attn, m_g_attn, v_g_attn),
        "g_conv": (g_conv, m_g_conv, v_g_conv), "ln1_g": (ln1_g, m_ln1_g, v_ln1_g),
        "ln1_b": (ln1_b, m_ln1_b, v_ln1_b), "ln2_g": (ln2_g, m_ln2_g, v_ln2_g),
        "ln2_b": (ln2_b, m_ln2_b, v_ln2_b), "conv_w": (conv_w, m_conv_w, v_conv_w)})
    res = {**big, **small}
    order = ["w_in", "conv_w", "sinks", "g_attn", "g_conv", "w_out", "ln1_g", "ln1_b", "w_gate", "w_up", "w_down",
             "ln2_g", "ln2_b"]
    loss = red[6, d // 2 + 128]
    return (loss, grad_x, *[res[n][0] for n in order], *[res[n][1] for n in order],
            *[res[n][2] for n in order], *[res[n][3] for n in order])
```

```python
import functools

import numpy as np
import jax
import jax.numpy as jnp
from jax import lax
from jax.experimental import pallas as pl
from jax.experimental.pallas import tpu as pltpu

F32 = jnp.float32
BF16 = jnp.bfloat16
MESH = pl.DeviceIdType.MESH

HEAD_DIM = 64
N_KV_HEADS = 4
GROUP = 4
WINDOW = 128
ROT_DIM = 16
ROPE_THETA = 500000.0
ATTN_SCALE = HEAD_DIM ** -0.5
ALPHA = 2.0 ** 0.25
LN_EPS = 1e-5
RMS_EPS = 1e-6
ADAM_LR = 0.001
ADAM_B1 = 0.9
ADAM_B2 = 0.999
ADAM_EPS = 1e-08
ADAM_WD = 0.01
ADAM_STEP = 10
N_CHIPS = 4
NEG_BIG = -1e30

V7X_VMEM_BYTES = 64 * 1024 * 1024
VMEM_LIMIT = V7X_VMEM_BYTES - 6 * 1024 * 1024

TM = 512
TK_TOK = 1024
TB_CONV = 256
TR_ELT = 256
ROW_CHUNK = 128
HALO_ROWS = 16


def _params(sem):
    return pltpu.CompilerParams(dimension_semantics=sem, vmem_limit_bytes=VMEM_LIMIT)


def _row_tile(rows, target):
    best = None
    for t in range(16, min(rows, target) + 1, 16):
        if rows % t == 0:
            best = t
    assert best is not None, (rows, target)
    return best


def _dot(a, b):
    return jnp.dot(a, b, preferred_element_type=F32)


def _dot_nt(a, b):
    return lax.dot_general(a, b, (((1,), (1,)), ((), ())), preferred_element_type=F32)


def _dot_tn(a, b):
    return lax.dot_general(a, b, (((0,), (0,)), ((), ())), preferred_element_type=F32)


def _mesh_pos():
    x, y, c = lax.axis_index("x"), lax.axis_index("y"), lax.axis_index("c")
    chips = [(1 - x, y), (x, 1 - y), (1 - x, 1 - y)]
    return x, y, c, chips


def _chip_id(px, py):
    return 2 * px + py


def _rope(t, cos, sgn_sin, sign):
    w = t.shape[1]
    lane = lax.broadcasted_iota(jnp.int32, t.shape, 1) & (HEAD_DIM - 1)
    partner = jnp.where(lane < ROT_DIM // 2, pltpu.roll(t, w - ROT_DIM // 2, 1), pltpu.roll(t, ROT_DIM // 2, 1))
    return t * cos + sign * (partner * sgn_sin)


def _tile_lanes(t, n):
    return jnp.concatenate([t] * n, axis=1)


def _sigmoid(g):
    return 1.0 / (1.0 + jnp.exp(-g))


def _for_row_chunks(n_rows, fn):
    def step(r, carry):
        fn(pl.ds(pl.multiple_of(r * ROW_CHUNK, ROW_CHUNK), ROW_CHUNK))
        return carry

    lax.fori_loop(0, n_rows // ROW_CHUNK, step, 0)


def _accumulate(acc, make_val, k, nk):
    if nk == 1:
        acc[...] = make_val()
        return

    @pl.when(k == 0)
    def _():
        acc[...] = jnp.zeros_like(acc)

    acc[...] += make_val()


def _ln_fwd(pre):
    mu = jnp.mean(pre, axis=-1, keepdims=True)
    cen = pre - mu
    var = jnp.mean(cen * cen, axis=-1, keepdims=True)
    rstd = lax.rsqrt(var + LN_EPS)
    return cen * rstd, rstd


def _ln_bwd(dy, xhat, rstd, g):
    dxhat = dy * g
    m1 = jnp.mean(dxhat, axis=-1, keepdims=True)
    m2 = jnp.mean(dxhat * xhat, axis=-1, keepdims=True)
    return rstd * (dxhat - m1 - xhat * m2)


def _cast_weight(w, chip_vec, after, name, col_block=0, n_col_blocks=1):
    _, r, c = w.shape
    tr = _row_tile(r, TR_ELT)

    def body(chip_ref, w_ref, after_ref, o_ref):
        o_ref[...] = w_ref[...].astype(BF16)

    grid_spec = pltpu.PrefetchScalarGridSpec(
        num_scalar_prefetch=1, grid=(r // tr,),
        in_specs=[pl.BlockSpec((None, tr, c), lambda i, chip_ref: (0, i, 0)), _ANY],
        out_specs=pl.BlockSpec((None, tr, c), lambda i, chip_ref: (chip_ref[0], i, col_block)))
    return pl.pallas_call(
        body, name=name, grid_spec=grid_spec,
        out_shape=jax.ShapeDtypeStruct((N_CHIPS, r, n_col_blocks * c), BF16),
        input_output_aliases={2: 0} if col_block else {},
        compiler_params=_params(("parallel",)),
    )(chip_vec, w, after)


_HBM = pl.BlockSpec(memory_space=pltpu.HBM)
_VMEM = pl.BlockSpec(memory_space=pltpu.VMEM)


_SEM = pl.BlockSpec(memory_space=pltpu.SEMAPHORE)
_ANY = pl.BlockSpec(memory_space=pl.ANY)
_EFFECT = pltpu.SideEffectType.DATAFLOW_SIDE_EFFECTING


def _chip_copy(buf, k, chip_of_src, half_rows, send_sems, recv_sems, to):
    part = buf.at[chip_of_src, half_rows]
    return pltpu.make_async_remote_copy(
        src_ref=part, dst_ref=part, send_sem=send_sems.at[k], recv_sem=recv_sems.at[k], device_id=to, device_id_type=MESH)


def _half_rows(buf, which):
    hr = buf.shape[1] // 2
    return pl.ds(which * hr, hr)


def _after(value, dep):
    return lax.optimization_barrier((value, dep))[0]


def _flight_start(name, bufs, plan, n_sems, after):
    n = len(bufs)

    def body(*refs):
        sends, _ = plan(refs[:n], refs[n + 1], refs[n + 2])
        for cp in sends:
            cp.start()

    outs = pl.pallas_call(
        body, name=name,
        in_specs=[_HBM] * n + [_ANY], out_specs=[_SEM, _SEM] + [_HBM] * n,
        out_shape=[pltpu.SemaphoreType.DMA((n_sems,))] * 2 + [pltpu.HBM(b.shape, b.dtype) for b in bufs],
        input_output_aliases={i: 2 + i for i in range(n)},
        compiler_params=pltpu.CompilerParams(has_side_effects=_EFFECT),
    )(*[pltpu.with_memory_space_constraint(b, pltpu.HBM) for b in bufs], after)
    return outs[0], outs[1], list(outs[2:])


def _flight_wait(name, flight, plan, after):
    send_sems, recv_sems, bufs = flight
    n = len(bufs)

    def body(*refs):
        sends, recvs = plan(refs[:n], refs[n], refs[n + 1])
        for cp in sends:
            cp.wait_send()
        for cp in recvs:
            cp.wait_recv()

    outs = pl.pallas_call(
        body, name=name,
        in_specs=[_HBM] * n + [_SEM, _SEM, _ANY], out_specs=[_HBM] * n,
        out_shape=[pltpu.HBM(b.shape, b.dtype) for b in bufs],
        input_output_aliases={i: i for i in range(n)},
        compiler_params=pltpu.CompilerParams(has_side_effects=_EFFECT),
    )(*bufs, send_sems, recv_sems, after)
    return list(outs)


def _fill_plan(n_bufs):
    def plan(refs, send_sems, recv_sems):
        x, y, c, chips = _mesh_pos()
        sibling = (x, y, 1 - c)
        sends, recvs = [], []
        for w in range(n_bufs):
            for k, chip in enumerate(chips):
                slot = _chip_id(*chip)
                sends.append(_chip_copy(refs[w], 3 * w + k, slot, _half_rows(refs[w], c), send_sems, recv_sems, sibling))
                recvs.append(_chip_copy(refs[w], 3 * w + k, slot, _half_rows(refs[w], 1 - c), send_sems, recv_sems,
                                        sibling))
        return sends, recvs
    return plan


def _conv_w_plan():
    def plan(refs, send_sems, recv_sems):
        x, y, c, chips = _mesh_pos()
        me = _chip_id(x, y)
        (buf,) = refs
        sends, recvs = [], []
        for k, chip in enumerate(chips):
            for slot, into in ((me, sends), (_chip_id(*chip), recvs)):
                into.append(pltpu.make_async_remote_copy(
                    src_ref=buf.at[slot], dst_ref=buf.at[slot], send_sem=send_sems.at[k], recv_sem=recv_sems.at[k],
                    device_id=(*chip, c), device_id_type=MESH))
        return sends, recvs
    return plan


def _exchange_plan(n_parts):
    def plan(refs, send_sems, recv_sems):
        x, y, c, _ = _mesh_pos()
        copies = []
        for w in range(n_parts):
            part, got = refs[2 * w], refs[2 * w + 1]
            hr = got.shape[1]
            copies.append(pltpu.make_async_remote_copy(
                src_ref=part.at[:, pl.ds((1 - c) * hr, hr)], dst_ref=got, send_sem=send_sems.at[w],
                recv_sem=recv_sems.at[w], device_id=(x, y, 1 - c), device_id_type=MESH))
        return copies, copies
    return plan


def _gather_start(bufs, after, name):
    n = len(bufs)

    def body(*refs):
        ins = refs[:n]
        sends, recvs = refs[n + 1:2 * n + 1], refs[2 * n + 1:3 * n + 1]
        token = refs[4 * n + 1]
        x, y, c, chips = _mesh_pos()
        me = _chip_id(x, y)
        for w in range(n):
            for k, chip in enumerate(chips):
                _chip_copy(ins[w], k, me, _half_rows(ins[w], c), sends[w], recvs[w], (*chip, c)).start()
        token[...] = jnp.zeros_like(token)

    outs = pl.pallas_call(
        body, name=name,
        in_specs=[_HBM] * n + [_ANY],
        out_specs=[_SEM] * (2 * n) + [_HBM] * n + [_VMEM],
        out_shape=[pltpu.SemaphoreType.DMA((3,))] * (2 * n) + [pltpu.HBM(b.shape, b.dtype) for b in bufs]
        + [jax.ShapeDtypeStruct((8, 128), F32)],
        input_output_aliases={w: 2 * n + w for w in range(n)},
        compiler_params=pltpu.CompilerParams(has_side_effects=_EFFECT),
    )(*[pltpu.with_memory_space_constraint(b, pltpu.HBM) for b in bufs], after)
    return [(outs[w], outs[n + w], outs[2 * n + w]) for w in range(n)], outs[3 * n]


def _gather_wait(send_sems, recv_sems, buf, after, name):
    def body(buf_ref, send_ref, recv_ref, after_ref, out_ref):
        x, y, c, chips = _mesh_pos()
        me = _chip_id(x, y)
        for k, chip in enumerate(chips):
            _chip_copy(buf_ref, k, me, _half_rows(buf_ref, c), send_ref, recv_ref, (*chip, c)).wait_send()
        for k, chip in enumerate(chips):
            _chip_copy(buf_ref, k, _chip_id(*chip), _half_rows(buf_ref, c), send_ref, recv_ref, (*chip, c)).wait_recv()

    return pl.pallas_call(
        body, name=name,
        in_specs=[_HBM, _SEM, _SEM, _ANY], out_specs=_HBM,
        out_shape=pltpu.HBM(buf.shape, buf.dtype),
        input_output_aliases={0: 0},
        compiler_params=pltpu.CompilerParams(has_side_effects=_EFFECT),
    )(buf, send_sems, recv_sems, after)


def _sibling_fill(buf, name, own_too=False):
    n_copies = 4 if own_too else 3

    def body(buf_ref, out_ref, send_sems, recv_sems):
        x, y, c, chips = _mesh_pos()
        sibling = (x, y, 1 - c)
        slots = [_chip_id(*chip) for chip in chips] + ([_chip_id(x, y)] if own_too else [])
        copies = []
        for k, slot in enumerate(slots):
            cp = _chip_copy(out_ref, k, slot, _half_rows(out_ref, c), send_sems, recv_sems, sibling)
            cp.start()
            copies.append(cp)
        for k, slot in enumerate(slots):
            _chip_copy(out_ref, k, slot, _half_rows(out_ref, 1 - c), send_sems, recv_sems, sibling).wait_recv()
        for cp in copies:
            cp.wait_send()

    return pl.pallas_call(
        body, name=name,
        in_specs=[_HBM], out_specs=_HBM,
        out_shape=jax.ShapeDtypeStruct(buf.shape, buf.dtype),
        input_output_aliases={0: 0},
        scratch_shapes=[pltpu.SemaphoreType.DMA((n_copies,)), pltpu.SemaphoreType.DMA((n_copies,))],
    )(buf)


def _allgather_conv_w(cw):
    _, kw, cs = cw.shape

    def body(cw_ref, out_ref, send_sems, recv_sems):
        x, y, c, chips = _mesh_pos()
        me = _chip_id(x, y)
        out_ref[pl.ds(me, 1)] = cw_ref[...]
        copies = []
        for k, chip in enumerate(chips):
            cp = pltpu.make_async_remote_copy(
                src_ref=cw_ref.at[0], dst_ref=out_ref.at[me], send_sem=send_sems.at[k], recv_sem=recv_sems.at[k],
                device_id=(*chip, c), device_id_type=MESH)
            cp.start()
            copies.append(cp)
        for k, chip in enumerate(chips):
            pltpu.make_async_remote_copy(
                src_ref=cw_ref.at[0], dst_ref=out_ref.at[_chip_id(*chip)], send_sem=send_sems.at[k],
                recv_sem=recv_sems.at[k], device_id=(*chip, c), device_id_type=MESH).wait_recv()
        for cp in copies:
            cp.wait_send()

    return pl.pallas_call(
        body, name="allgather_conv_w",
        in_specs=[_VMEM], out_specs=_VMEM,
        out_shape=jax.ShapeDtypeStruct((N_CHIPS, kw, cs), F32),
        scratch_shapes=[pltpu.SemaphoreType.DMA((3,)), pltpu.SemaphoreType.DMA((3,))],
    )(cw)


def _exchange_halves(parts, after, name):
    n = len(parts)
    shapes = [p.shape for p in parts]

    def body(*refs):
        ins, outs = refs[:n], refs[n + 1:2 * n + 1]
        send_sems, recv_sems = refs[2 * n + 1:]
        x, y, c, _ = _mesh_pos()
        copies = []
        for w in range(n):
            hr = shapes[w][1] // 2
            cp = pltpu.make_async_remote_copy(
                src_ref=ins[w].at[:, pl.ds((1 - c) * hr, hr)], dst_ref=outs[w],
                send_sem=send_sems.at[w], recv_sem=recv_sems.at[w],
                device_id=(x, y, 1 - c), device_id_type=MESH)
            cp.start()
            copies.append(cp)
        for cp in copies:
            cp.wait()

    return pl.pallas_call(
        body, name=name,
        in_specs=[_HBM] * n + [_ANY], out_specs=[_HBM] * n,
        out_shape=[jax.ShapeDtypeStruct((s[0], s[1] // 2, s[2]), BF16) for s in shapes],
        scratch_shapes=[pltpu.SemaphoreType.DMA((n,)), pltpu.SemaphoreType.DMA((n,))],
    )(*parts, after)


def _add_halves(part, got, cvec, name):
    ns, r, cdim = part.shape
    hr = r // 2
    tr = _row_tile(hr, TR_ELT)
    nblk = hr // tr

    def body(c_ref, a_ref, b_ref, o_ref):
        o_ref[...] = a_ref[...] + b_ref[...]

    grid_spec = pltpu.PrefetchScalarGridSpec(
        num_scalar_prefetch=1, grid=(ns, nblk),
        in_specs=[pl.BlockSpec((None, tr, cdim), lambda s, i, c_ref: (s, c_ref[0] * nblk + i, 0)),
                  pl.BlockSpec((None, tr, cdim), lambda s, i, c_ref: (s, i, 0))],
        out_specs=pl.BlockSpec((None, tr, cdim), lambda s, i, c_ref: (s, i, 0)))
    return pl.pallas_call(
        body, name=name, grid_spec=grid_spec,
        out_shape=jax.ShapeDtypeStruct((ns, hr, cdim), BF16),
        compiler_params=_params(("parallel", "parallel")),
    )(cvec, part, got)


def _scatter_copy(sums_ref, land_ref, k, src_slot, dst_slot, c, send_sems, recv_sems, to):
    return pltpu.make_async_remote_copy(
        src_ref=sums_ref.at[src_slot], dst_ref=land_ref.at[dst_slot, _half_rows(land_ref, c)],
        send_sem=send_sems.at[k], recv_sem=recv_sems.at[k], device_id=to, device_id_type=MESH)


def _scatter_start(sums, name):
    ns, hr, cdim = sums.shape
    land = lax.empty((ns, 2 * hr, cdim), sums.dtype)

    def body(sums_ref, land_ref, send_sems, recv_sems, sums_thru, land_thru):
        x, y, c, chips = _mesh_pos()
        me = _chip_id(x, y)
        for k, chip in enumerate(chips):
            _scatter_copy(sums_ref, land_ref, k, _chip_id(*chip), me, c, send_sems, recv_sems, (*chip, c)).start()

    return pl.pallas_call(
        body, name=name,
        in_specs=[_HBM, _HBM], out_specs=[_SEM, _SEM, _HBM, _HBM],
        out_shape=[pltpu.SemaphoreType.DMA((3,)), pltpu.SemaphoreType.DMA((3,)),
                   pltpu.HBM(sums.shape, sums.dtype), pltpu.HBM(land.shape, land.dtype)],
        input_output_aliases={0: 2, 1: 3},
        compiler_params=pltpu.CompilerParams(has_side_effects=_EFFECT),
    )(pltpu.with_memory_space_constraint(sums, pltpu.HBM), pltpu.with_memory_space_constraint(land, pltpu.HBM))


def _scatter_wait(send_sems, recv_sems, sums, land, after, name):
    def body(sums_ref, land_ref, send_ref, recv_ref, after_ref, sums_out, land_out):
        x, y, c, chips = _mesh_pos()
        me = _chip_id(x, y)
        for k, chip in enumerate(chips):
            _scatter_copy(sums_ref, land_ref, k, _chip_id(*chip), me, c, send_ref, recv_ref, (*chip, c)).wait_send()
        for k, chip in enumerate(chips):
            _scatter_copy(sums_ref, land_ref, k, me, _chip_id(*chip), c, send_ref, recv_ref, (*chip, c)).wait_recv()

    return pl.pallas_call(
        body, name=name,
        in_specs=[_HBM, _HBM, _SEM, _SEM, _ANY], out_specs=[_HBM, _HBM],
        out_shape=[pltpu.HBM(sums.shape, sums.dtype), pltpu.HBM(land.shape, land.dtype)],
        input_output_aliases={0: 0, 1: 1},
        compiler_params=pltpu.CompilerParams(has_side_effects=_EFFECT),
    )(sums, land, send_sems, recv_sems, after)


def _complete_plan(n_weights):
    def plan(refs, send_sems, recv_sems):
        x, y, c, chips = _mesh_pos()
        me = _chip_id(x, y)
        sibling = (x, y, 1 - c)
        sends, recvs = [], []
        for w in range(n_weights):
            sums, land = refs[2 * w], refs[2 * w + 1]
            sends.append(_scatter_copy(sums, land, 4 * w + 3, me, me, c, send_sems, recv_sems, sibling))
            recvs.append(_scatter_copy(sums, land, 4 * w + 3, me, me, 1 - c, send_sems, recv_sems, sibling))
            for k, chip in enumerate(chips):
                slot = _chip_id(*chip)
                sends.append(_chip_copy(land, 4 * w + k, slot, _half_rows(land, c), send_sems, recv_sems, sibling))
                recvs.append(_chip_copy(land, 4 * w + k, slot, _half_rows(land, 1 - c), send_sems, recv_sems, sibling))
        return sends, recvs
    return plan


def _complete_chip_sums(sums, lands):
    n = len(sums)

    def body(*refs):
        sums_refs, outs = refs[:n], refs[2 * n:3 * n]
        send_sems, recv_sems = refs[3 * n:]
        x, y, c, chips = _mesh_pos()
        me = _chip_id(x, y)
        sibling = (x, y, 1 - c)
        slots = [_chip_id(*chip) for chip in chips]
        sent = []
        for w in range(n):
            out = outs[w]
            cp = _scatter_copy(sums_refs[w], out, 3, me, me, c, send_sems.at[w], recv_sems.at[w], sibling)
            cp.start()
            sent.append(cp)
            for k, slot in enumerate(slots):
                cp = _chip_copy(out, k, slot, _half_rows(out, c), send_sems.at[w], recv_sems.at[w], sibling)
                cp.start()
                sent.append(cp)
        for w in range(n):
            out = outs[w]
            _scatter_copy(sums_refs[w], out, 3, me, me, 1 - c, send_sems.at[w], recv_sems.at[w], sibling).wait_recv()
            for k, slot in enumerate(slots):
                _chip_copy(out, k, slot, _half_rows(out, 1 - c), send_sems.at[w], recv_sems.at[w], sibling).wait_recv()
        for cp in sent:
            cp.wait_send()

    return pl.pallas_call(
        body, name="complete_chip_sums",
        in_specs=[_HBM] * (2 * n), out_specs=[_HBM] * n,
        out_shape=[jax.ShapeDtypeStruct(b.shape, b.dtype) for b in lands],
        input_output_aliases={n + w: w for w in range(n)},
        scratch_shapes=[pltpu.SemaphoreType.DMA((n, 4)), pltpu.SemaphoreType.DMA((n, 4))],
    )(*sums, *lands)


SMALL_ROWS = 8


N_DEVICES = 8


def _small_pack(gl2g, gl2b, gl1g, gl1b, g_ac, gcw, gsink, loss):
    d = gl2g.shape[1]
    hd = d // 2
    nq = gsink.shape[1]

    def body(a_ref, b_ref, c_ref, d_ref, e_ref, cw_ref, sk_ref, ls_ref, out_ref, mine):
        x, y, c, _ = _mesh_pos()
        me = 4 * x + 2 * y + c
        mine[...] = jnp.zeros_like(mine)
        mine[0:1, :] = a_ref[...]
        mine[1:2, :] = b_ref[...]
        mine[2:3, :] = c_ref[...]
        mine[3:4, :] = d_ref[...]
        mine[4:5, :] = e_ref[...]
        mine[5:6, 0:hd] = cw_ref[0:1, :]
        mine[5:6, hd:d] = cw_ref[1:2, :]
        mine[6:7, 0:hd] = cw_ref[2:3, :]
        mine[6:7, hd:hd + nq] = sk_ref[...]
        mine[6:7, hd + 128:hd + 256] = ls_ref[...]
        out_ref[...] = jnp.zeros_like(out_ref)
        out_ref[pl.ds(me, 1)] = mine[...][None]

    return pl.pallas_call(
        body, name="small_pack",
        in_specs=[_VMEM] * 8, out_specs=_VMEM,
        out_shape=jax.ShapeDtypeStruct((N_DEVICES, SMALL_ROWS, d), F32),
        scratch_shapes=[pltpu.VMEM((SMALL_ROWS, d), F32)],
    )(gl2g, gl2b, gl1g, gl1b, g_ac, gcw, gsink, loss)


def _small_plan():
    def plan(refs, send_sems, recv_sems):
        x, y, c, _ = _mesh_pos()
        me = 4 * x + 2 * y + c
        (gath,) = refs
        sends, recvs = [], []
        for r in range(1, N_DEVICES):
            peer = ((1 - x) if r & 4 else x, (1 - y) if r & 2 else y, (1 - c) if r & 1 else c)
            peer_id = 4 * peer[0] + 2 * peer[1] + peer[2]
            for slot, into in ((me, sends), (peer_id, recvs)):
                into.append(pltpu.make_async_remote_copy(
                    src_ref=gath.at[slot], dst_ref=gath.at[slot], send_sem=send_sems.at[r - 1],
                    recv_sem=recv_sems.at[r - 1], device_id=peer, device_id_type=MESH))
        return sends, recvs
    return plan


def _small_sum(gath):
    def body(gath_ref, out_ref):
        total = gath_ref[0]
        for dev in range(1, N_DEVICES):
            total = total + gath_ref[dev]
        out_ref[...] = total

    return pl.pallas_call(
        body, name="small_sum", in_specs=[_VMEM], out_specs=_VMEM,
        out_shape=jax.ShapeDtypeStruct(gath.shape[1:], F32),
    )(gath)


def _adamw(w, g, m, v):
    m = ADAM_B1 * m + (1.0 - ADAM_B1) * g
    v = ADAM_B2 * v + (1.0 - ADAM_B2) * (g * g)
    m_hat = m / (1.0 - ADAM_B1 ** ADAM_STEP)
    v_hat = v / (1.0 - ADAM_B2 ** ADAM_STEP)
    delta = -ADAM_LR * (m_hat / (jnp.sqrt(v_hat) + ADAM_EPS) + ADAM_WD * w)
    return delta, m, v


def _adamw_shard(w, m, v, land, own, pos_vec, name, col_block=0):
    tr = _row_tile(w.shape[1] // 2, TR_ELT)
    grid = (w.shape[1] // tr,)
    body, in_specs, out_specs, out_shape = _adamw_passenger(w.shape, tr, grid, col_block)
    grid_spec = pltpu.PrefetchScalarGridSpec(num_scalar_prefetch=1, grid=grid, in_specs=in_specs, out_specs=out_specs)
    return pl.pallas_call(
        body, name=name, grid_spec=grid_spec, out_shape=out_shape,
        compiler_params=_params(("parallel",)),
    )(pos_vec, w, m, v, land, land, land, land, own)


def _adamw_passenger(shape, tr, grid, col_block):
    _, r, c = shape
    nh = r // 2 // tr
    n_blocks = 2 * nh
    n_steps = int(np.prod(grid))
    assert nh * tr * 2 == r and n_blocks <= n_steps

    def step_of(ids):
        step = ids[0]
        for n, i in zip(grid[1:], ids[1:]):
            step = step * n + i
        return step

    def block_of(ids):
        return jnp.minimum(step_of(ids), n_blocks - 1)

    def update(pos_ref, w_ref, m_ref, v_ref, l0, l1, l2, l3, own_ref, g_out, d_out, m_out, v_out):
        i = block_of([pl.program_id(a) for a in range(len(grid))])
        mine = (i // nh) == pos_ref[1]
        own_blk = own_ref[...].astype(F32)
        g = None
        for s, l_ref in enumerate([l0, l1, l2, l3]):
            term = jnp.where(mine & (pos_ref[0] == s), own_blk, l_ref[...].astype(F32))
            g = term if g is None else g + term
        delta, nm, nv = _adamw(w_ref[...], g, m_ref[...], v_ref[...])
        g_out[...] = g
        d_out[...] = delta
        m_out[...] = nm
        v_out[...] = nv

    def body(*refs):
        if n_blocks == n_steps:
            update(*refs)
        else:
            pl.when(step_of([pl.program_id(a) for a in range(len(grid))]) < n_blocks)(lambda: update(*refs))

    def land_spec(s):
        def index(*args):
            i, pos_ref = block_of(args[:-1]), args[-1]
            skip = (pos_ref[0] == s) & ((i // nh) == pos_ref[1])
            return (s, jnp.where(skip, (i + nh) % n_blocks, i), col_block)
        return pl.BlockSpec((None, tr, c), index)

    blk = pl.BlockSpec((None, tr, c), lambda *args: (0, block_of(args[:-1]), 0))
    in_specs = ([blk, blk, blk] + [land_spec(s) for s in range(N_CHIPS)]
                + [pl.BlockSpec((None, tr, c), lambda *args: (args[-1][0], block_of(args[:-1]) % nh, col_block))])
    return body, in_specs, [blk] * 4, [jax.ShapeDtypeStruct((1, r, c), F32)] * 4


def _call_with_adamw(body, name, grid, in_specs, out_specs, out_shape, scratch_shapes, semantics, operands, shard):
    if shard is None:
        return pl.pallas_call(
            body, name=name, grid=grid, in_specs=in_specs, out_specs=out_specs, out_shape=out_shape,
            scratch_shapes=scratch_shapes, compiler_params=_params(semantics))(*operands)
    w, m, v, land, own, pos_vec, col_block = shard
    n_steps = int(np.prod(grid))
    hr = w.shape[1] // 2
    tr = min(t for t in range(16, hr + 1, 16) if hr % t == 0 and 2 * (hr // t) <= n_steps)
    adam_body, adam_in, adam_out, adam_shape = _adamw_passenger(w.shape, tr, grid, col_block)
    n_in, n_out = len(in_specs), len(out_specs)

    def with_pos(spec):
        if spec.index_map is None:
            return spec
        return pl.BlockSpec(spec.block_shape, lambda *args: spec.index_map(*args[:-1]))

    def both(pos_ref, *refs):
        ins, adam_ins = refs[:n_in], refs[n_in:n_in + len(adam_in)]
        refs = refs[n_in + len(adam_in):]
        outs, adam_outs, scratch = refs[:n_out], refs[n_out:n_out + len(adam_out)], refs[n_out + len(adam_out):]
        body(*ins, *outs, *scratch)
        adam_body(pos_ref, *adam_ins, *adam_outs)

    grid_spec = pltpu.PrefetchScalarGridSpec(
        num_scalar_prefetch=1, grid=grid, in_specs=[with_pos(sp) for sp in in_specs] + adam_in,
        out_specs=[with_pos(sp) for sp in out_specs] + adam_out, scratch_shapes=scratch_shapes)
    return pl.pallas_call(
        both, name=name, grid_spec=grid_spec, out_shape=list(out_shape) + adam_shape,
        compiler_params=_params(semantics),
    )(pos_vec, *operands, w, m, v, land, land, land, land, own)


def _adamw_small(red, params):
    names = ["sinks", "g_attn", "g_conv", "ln1_g", "ln1_b", "ln2_g", "ln2_b", "conv_w"]
    d = red.shape[1]
    hd = d // 2
    flat = []
    for nme in names:
        flat.extend(params[nme])
    nq = params["sinks"][0].shape[1]
    cs = params["conv_w"][0].shape[2]

    def body(*refs):
        red_ref = refs[0]
        ins = refs[1:1 + 3 * len(names)]
        outs = refs[1 + 3 * len(names):]
        x, y, _, _ = _mesh_pos()
        me = _chip_id(x, y)

        def conv_tap(row, base):
            picked = red_ref[row:row + 1, base:base + cs]
            for s in range(1, N_CHIPS):
                picked = jnp.where(me == s, red_ref[row:row + 1, base + s * cs:base + (s + 1) * cs], picked)
            return picked

        grads = {
            "sinks": red_ref[6:7, hd:hd + nq],
            "g_attn": red_ref[4:5, 0:hd],
            "g_conv": red_ref[4:5, hd:d],
            "ln1_g": red_ref[2:3, :],
            "ln1_b": red_ref[3:4, :],
            "ln2_g": red_ref[0:1, :],
            "ln2_b": red_ref[1:2, :],
        }
        for i, nme in enumerate(names):
            w_ref, m_ref, v_ref = ins[3 * i:3 * i + 3]
            g_out, d_out, m_out, v_out = outs[4 * i:4 * i + 4]
            if nme == "conv_w":
                for tap, (row, base) in enumerate([(5, 0), (5, hd), (6, 0)]):
                    g = conv_tap(row, base)
                    delta, nm, nv = _adamw(w_ref[0, tap:tap + 1, :], g, m_ref[0, tap:tap + 1, :], v_ref[0, tap:tap + 1, :])
                    g_out[0, tap:tap + 1, :] = g
                    d_out[0, tap:tap + 1, :] = delta
                    m_out[0, tap:tap + 1, :] = nm
                    v_out[0, tap:tap + 1, :] = nv
            else:
                g = grads[nme]
                delta, nm, nv = _adamw(w_ref[...], g, m_ref[...], v_ref[...])
                g_out[...] = g
                d_out[...] = delta
                m_out[...] = nm
                v_out[...] = nv

    out_shape = []
    for nme in names:
        out_shape.extend([jax.ShapeDtypeStruct(params[nme][0].shape, F32)] * 4)
    outs = pl.pallas_call(
        body, name="adamw_small",
        in_specs=[_VMEM] * (1 + len(flat)), out_specs=[_VMEM] * len(out_shape),
        out_shape=out_shape,
    )(red, *flat)
    return {nme: tuple(outs[4 * i:4 * i + 4]) for i, nme in enumerate(names)}


def _rope_tables(pos_col):
    s = pos_col.shape[0]
    w = N_KV_HEADS * HEAD_DIM
    tb = min(512, s)
    inv_freq = (ROPE_THETA ** (-np.arange(0, ROT_DIM, 2, dtype=np.float32) / ROT_DIM)).astype(np.float32)

    def body(pos_ref, cos_ref, sin_ref):
        pos = pos_ref[...].astype(F32)
        lane = lax.broadcasted_iota(jnp.int32, (tb, PAIR), 1) & (HEAD_DIM - 1)
        fidx = lane & (ROT_DIM // 2 - 1)
        inv = jnp.zeros((tb, PAIR), F32)
        for k in range(ROT_DIM // 2):
            inv = jnp.where(fidx == k, float(inv_freq[k]), inv)
        ang = pos * inv
        rot = lane < ROT_DIM
        sin_v = jnp.sin(ang)
        cos_ref[...] = _tile_lanes(jnp.where(rot, jnp.cos(ang), 1.0), w // PAIR)
        sin_ref[...] = _tile_lanes(jnp.where(lane < ROT_DIM // 2, -sin_v, jnp.where(rot, sin_v, 0.0)), w // PAIR)

    return pl.pallas_call(
        body, name="rope_tables", grid=(s // tb,),
        in_specs=[pl.BlockSpec((tb, 1), lambda i: (i, 0))],
        out_specs=[pl.BlockSpec((tb, w), lambda i: (i, 0))] * 2,
        out_shape=[jax.ShapeDtypeStruct((s, w), F32)] * 2,
        compiler_params=_params(("parallel",)),
    )(pos_col)


def _in_proj(x, w_in_g, first_vec, n_shards, into, name):
    _, s, d = x.shape
    ns, _, ncol = w_in_g.shape
    tm = min(2 * TM, s)

    def body(first_ref, x_ref, w_ref, into_ref, o_ref):
        o_ref[...] = _dot(x_ref[...].astype(BF16), w_ref[...]).astype(BF16)

    shard = lambda j, first_ref: lax.rem(first_ref[0] + j, ns)
    grid_spec = pltpu.PrefetchScalarGridSpec(
        num_scalar_prefetch=1, grid=(s // tm, n_shards),
        in_specs=[pl.BlockSpec((None, tm, d), lambda i, j, first_ref: (0, i, 0)),
                  pl.BlockSpec((None, d, ncol), lambda i, j, first_ref: (shard(j, first_ref), 0, 0)), _ANY],
        out_specs=pl.BlockSpec((tm, ncol), lambda i, j, first_ref: (i, shard(j, first_ref))))
    return pl.pallas_call(
        body, name=name, grid_spec=grid_spec,
        out_shape=jax.ShapeDtypeStruct((s, ns * ncol), BF16),
        input_output_aliases={} if into is None else {3: 0},
        compiler_params=_params(("parallel", "arbitrary")),
    )(first_vec, x, w_in_g, first_vec if into is None else into)


PAIR = 2 * HEAD_DIM
KEYS = 2 * WINDOW


def _pair_operand(t_all, h):
    col = (h // 2) * PAIR
    lane = lax.broadcasted_iota(jnp.int32, (KEYS, PAIR), 1)
    own_low = h % 2 == 0
    mine = jnp.where((lane < HEAD_DIM) if own_low else (lane >= HEAD_DIM), t_all[:, col:col + PAIR], 0.0)
    other = pltpu.roll(mine, HEAD_DIM, 1)
    low, high = (mine, other) if own_low else (other, mine)
    return jnp.concatenate([low, high], axis=0).astype(BF16)


def _pair_grad(acc, h):
    lane = lax.broadcasted_iota(jnp.int32, (KEYS, PAIR), 1)
    low = jnp.where(lane < HEAD_DIM, acc[:KEYS], 0.0)
    high = jnp.where(lane >= HEAD_DIM, acc[KEYS:], 0.0)
    if h % 2 == 0:
        return low + pltpu.roll(high, HEAD_DIM, 1)
    return high + pltpu.roll(low, HEAD_DIM, 1)


N_PAIRS = N_KV_HEADS * GROUP // 2


def _all_probs(q, kk2s, first, sinks_ref):
    assert ATTN_SCALE == 0.125
    q = q * ATTN_SCALE
    qps, scores = [], []
    for pair in range(N_PAIRS):
        qp = q[:, pair * PAIR:(pair + 1) * PAIR].astype(BF16)
        qps.append(qp)
        scores.append(_dot_nt(qp, kk2s[pair // (GROUP // 2)]))
    qi = lax.broadcasted_iota(jnp.int32, (WINDOW, 2 * KEYS), 0)
    kj = lax.broadcasted_iota(jnp.int32, (WINDOW, 2 * KEYS), 1) & (KEYS - 1)
    rel = qi + WINDOW - kj
    valid = (rel >= 0) & (rel < WINDOW) & jnp.logical_not(first & (kj < WINDOW))
    bias = jnp.where(valid, 0.0, NEG_BIG)
    s = (jnp.stack(scores, axis=0) + bias[None]).reshape(N_PAIRS * WINDOW, 2 * KEYS)
    probs, p_sinks = [], []
    for t in range(2):
        st = s[:, t * KEYS:(t + 1) * KEYS]
        sink = jnp.concatenate([jnp.broadcast_to(sinks_ref[0:1, 2 * pair + t:2 * pair + t + 1], (WINDOW, 1))
                                for pair in range(N_PAIRS)], axis=0)
        m = jnp.maximum(jnp.max(st, axis=1, keepdims=True), sink)
        e = jnp.exp(st - m)
        e_sink = jnp.exp(sink - m)
        inv_l = 1.0 / (jnp.sum(e, axis=1, keepdims=True) + e_sink)
        probs.append(e * inv_l)
        p_sinks.append(e_sink * inv_l)
    return qps, jnp.concatenate(probs, axis=1), p_sinks


def _roped_qkv(cur_ref, prev_ref, cos_ref, sin_ref, cosp_ref, sinp_ref, qw, kvw):
    cur = cur_ref[...].astype(F32)
    cos, sin = cos_ref[...], sin_ref[...]
    cos_q, sin_q = _tile_lanes(cos, GROUP), _tile_lanes(sin, GROUP)
    q = _rope(cur[:, :qw], cos_q, sin_q, 1.0)
    prev = prev_ref[...].astype(F32)
    k_all = jnp.concatenate([_rope(prev[:, :kvw], cosp_ref[...], sinp_ref[...], 1.0),
                             _rope(cur[:, qw:qw + kvw], cos, sin, 1.0)], axis=0)
    v_all = jnp.concatenate([prev[:, kvw:], cur[:, qw + kvw:]], axis=0)
    return q, k_all, v_all, cos_q, sin_q


def _attention_fwd(proj, cos_t, sin_t, sinks):
    s = proj.shape[0]
    qw = GROUP * N_KV_HEADS * HEAD_DIM
    kvw = N_KV_HEADS * HEAD_DIM
    nb = s // WINDOW

    def body(cur_ref, prev_ref, cos_ref, sin_ref, cosp_ref, sinp_ref, sinks_ref, o_ref):
        first = pl.program_id(0) == 0
        q, k_all, v_all, _, _ = _roped_qkv(cur_ref, prev_ref, cos_ref, sin_ref, cosp_ref, sinp_ref, qw, kvw)
        kk2s = [_pair_operand(k_all, h) for h in range(N_KV_HEADS)]
        vv2s = [_pair_operand(v_all, h) for h in range(N_KV_HEADS)]
        _, probs, _ = _all_probs(q, kk2s, first, sinks_ref)
        probs = probs.astype(BF16)
        outs = [_dot(probs[pair * WINDOW:(pair + 1) * WINDOW], vv2s[pair // (GROUP // 2)]) for pair in range(N_PAIRS)]
        o_ref[...] = jnp.concatenate(outs, axis=1)

    tbl = pl.BlockSpec((WINDOW, kvw), lambda n: (n, 0))
    tbl_prev = pl.BlockSpec((WINDOW, kvw), lambda n: (jnp.maximum(n - 1, 0), 0))
    return pl.pallas_call(
        body, name="attention_fwd", grid=(nb,),
        in_specs=[pl.BlockSpec((WINDOW, qw + 2 * kvw), lambda n: (n, 0)),
                  pl.BlockSpec((WINDOW, 2 * kvw), lambda n: (jnp.maximum(n - 1, 0), (qw // (2 * kvw)))),
                  tbl, tbl, tbl_prev, tbl_prev, _VMEM],
        out_specs=pl.BlockSpec((WINDOW, qw), lambda n: (n, 0)),
        out_shape=jax.ShapeDtypeStruct((s, qw), F32),
        compiler_params=_params(("parallel",)),
    )(proj, proj, cos_t, sin_t, cos_t, sin_t, sinks)


def _conv_taps(cw_ref):
    return [jnp.concatenate([cw_ref[s, k:k + 1, :] for s in range(N_CHIPS)], axis=1) for k in range(3)]


def _shift_down(z, halo, steps):
    last = halo.shape[0]
    row = lax.broadcasted_iota(jnp.int32, z.shape, 0)
    out = pltpu.roll(z, steps, 0)
    for r in range(steps):
        out = jnp.where(row == r, halo[last - steps + r:last - steps + r + 1, :], out)
    return out


def _shift_up(z, halo, steps):
    rows = z.shape[0]
    row = lax.broadcasted_iota(jnp.int32, z.shape, 0)
    out = pltpu.roll(z, rows - steps, 0)
    for r in range(steps):
        out = jnp.where(row == rows - steps + r, halo[r:r + 1, :], out)
    return out


def _split_cbu(lo, hi, cw):
    lo, hi = lo.astype(F32), hi.astype(F32)
    c_gate = lo[:, :cw]
    b_gate = jnp.concatenate([lo[:, cw:], hi[:, :2 * cw - lo.shape[1]]], axis=1)
    u = hi[:, 2 * cw - lo.shape[1]:]
    return c_gate, b_gate, u


def _conv_norm(proj, attn, cw_full, g_ac):
    s, in_w = proj.shape
    cw = attn.shape[1]
    blk_w = in_w // 3
    tb = min(TB_CONV, s)

    def body(lo_ref, hi_ref, lo_h_ref, hi_h_ref, attn_ref, cw_ref, g_ref, mixed_ref, ac_ref, rstd_ref):
        i = pl.program_id(0)
        c_gate, b_gate, u = _split_cbu(lo_ref[...], hi_ref[...], cw)
        c_h, _, u_h = _split_cbu(lo_h_ref[...], hi_h_ref[...], cw)
        z = c_gate * u
        z_h = jnp.where(i == 0, 0.0, c_h * u_h)
        w0, w1, w2 = _conv_taps(cw_ref)
        y = w0 * _shift_down(z, z_h, 2) + w1 * _shift_down(z, z_h, 1) + w2 * z
        conv = b_gate * y
        a = attn_ref[...]
        r_a = lax.rsqrt(jnp.mean(a * a, axis=-1, keepdims=True) + RMS_EPS)
        r_c = lax.rsqrt(jnp.mean(conv * conv, axis=-1, keepdims=True) + RMS_EPS)
        g = g_ref[...]
        mixed_ref[...] = jnp.concatenate([a * r_a * g[:, :cw], conv * r_c * g[:, cw:]], axis=1).astype(BF16)
        ac_ref[...] = jnp.concatenate([a, conv], axis=1)
        rstd_ref[0] = r_a
        rstd_ref[1] = r_c

    halo_idx = lambda i: jnp.maximum(i * (tb // HALO_ROWS) - 1, 0)
    return pl.pallas_call(
        body, name="conv_norm", grid=(s // tb,),
        in_specs=[pl.BlockSpec((tb, blk_w), lambda i: (i, 1)),
                  pl.BlockSpec((tb, blk_w), lambda i: (i, 2)),
                  pl.BlockSpec((HALO_ROWS, blk_w), lambda i: (halo_idx(i), 1)),
                  pl.BlockSpec((HALO_ROWS, blk_w), lambda i: (halo_idx(i), 2)),
                  pl.BlockSpec((tb, cw), lambda i: (i, 0)),
                  _VMEM, _VMEM],
        out_specs=[pl.BlockSpec((tb, 2 * cw), lambda i: (i, 0)),
                   pl.BlockSpec((tb, 2 * cw), lambda i: (i, 0)),
                   pl.BlockSpec((2, tb, 1), lambda i: (0, i, 0))],
        out_shape=[jax.ShapeDtypeStruct((s, 2 * cw), BF16), jax.ShapeDtypeStruct((s, 2 * cw), F32),
                   jax.ShapeDtypeStruct((2, s, 1), F32)],
        compiler_params=_params(("parallel",)),
    )(proj, proj, proj, proj, attn, cw_full, g_ac)


def _out_proj_ln(mixed, w_out_g, x, ln_g, ln_b):
    s, d = mixed.shape
    tm = min(TM, s)
    tk = d
    nk = d // tk

    def body(a_ref, w_ref, x_ref, g_ref, b_ref, xhat_ref, h_ref, rstd_ref, acc):
        k = pl.program_id(1)
        _accumulate(acc, lambda: _dot(a_ref[...], w_ref[...]), k, nk)

        @pl.when(k == nk - 1)
        def _():
            def rows_fn(rows):
                xhat, rstd = _ln_fwd(ALPHA * x_ref[rows, :] + acc[rows, :])
                xhat_ref[rows, :] = xhat
                h_ref[rows, :] = (xhat * g_ref[...] + b_ref[...]).astype(BF16)
                rstd_ref[rows, :] = rstd

            _for_row_chunks(tm, rows_fn)

    row = pl.BlockSpec((tm, d), lambda i, k: (i, 0))
    return pl.pallas_call(
        body, name="out_proj_ln", grid=(s // tm, nk),
        in_specs=[pl.BlockSpec((tm, tk), lambda i, k: (i, k)),
                  pl.BlockSpec((tk, d), lambda i, k: (k, 0)),
                  pl.BlockSpec((None, tm, d), lambda i, k: (0, i, 0)),
                  _VMEM, _VMEM],
        out_specs=[row, row, pl.BlockSpec((tm, 1), lambda i, k: (i, 0))],
        out_shape=[jax.ShapeDtypeStruct((s, d), F32), jax.ShapeDtypeStruct((s, d), BF16),
                   jax.ShapeDtypeStruct((s, 1), F32)],
        scratch_shapes=[pltpu.VMEM((tm, d), F32)],
        compiler_params=_params(("parallel", "arbitrary")),
    )(mixed, w_out_g, x, ln_g, ln_b)


def _gate_up(h1, w_gu_g, first_vec, n_shards, into, name):
    s, d = h1.shape
    ns, _, fs2 = w_gu_g.shape
    fs = fs2 // 2
    tm = min(TM, s)

    def body(first_ref, h_ref, w_ref, act_in, ab_in, act_ref, ab_ref):
        gu = _dot(h_ref[...], w_ref[...])
        g, u = gu[:, :fs], gu[:, fs:]
        sg = _sigmoid(g)
        silu = g * sg
        act_ref[...] = (silu * u).astype(BF16)
        ab_ref[:, :fs] = (u * (sg * (1.0 + g * (1.0 - sg)))).astype(BF16)
        ab_ref[:, fs:] = silu.astype(BF16)

    shard = lambda j, first_ref: lax.rem(first_ref[0] + j, ns)
    grid_spec = pltpu.PrefetchScalarGridSpec(
        num_scalar_prefetch=1, grid=(s // tm, n_shards),
        in_specs=[pl.BlockSpec((tm, d), lambda i, j, first_ref: (i, 0)),
                  pl.BlockSpec((None, d, fs2), lambda i, j, first_ref: (shard(j, first_ref), 0, 0)), _ANY, _ANY],
        out_specs=[pl.BlockSpec((tm, fs), lambda i, j, first_ref: (i, shard(j, first_ref))),
                   pl.BlockSpec((tm, fs2), lambda i, j, first_ref: (i, shard(j, first_ref)))])
    return pl.pallas_call(
        body, name=name, grid_spec=grid_spec,
        out_shape=[jax.ShapeDtypeStruct((s, ns * fs), BF16), jax.ShapeDtypeStruct((s, ns * fs2), BF16)],
        input_output_aliases={} if into is None else {3: 0, 4: 1},
        compiler_params=_params(("parallel", "arbitrary")),
    )(first_vec, h1, w_gu_g, *((first_vec, first_vec) if into is None else into))


def _down_ln_loss(act, w_down_g, xhat1, ln1_g, ln1_b, ln2_g, ln2_b, target):
    s, f = act.shape
    d = xhat1.shape[1]
    tm = min(TM, s)
    tk = f // N_CHIPS
    nk = f // tk

    def body(a_ref, w_ref, xh_ref, g1_ref, b1_ref, g2_ref, b2_ref, t_ref, dpre_ref, dpre16_ref, loss_ref, gg_ref, gb_ref,
             acc):
        i, k = pl.program_id(0), pl.program_id(1)
        _accumulate(acc, lambda: _dot(a_ref[...], w_ref[...]), k, nk)

        @pl.when(k == nk - 1)
        def _():
            @pl.when(i == 0)
            def _():
                loss_ref[...] = jnp.zeros_like(loss_ref)
                gg_ref[...] = jnp.zeros_like(gg_ref)
                gb_ref[...] = jnp.zeros_like(gb_ref)

            def rows_fn(rows):
                h1 = xh_ref[rows, :] * g1_ref[...] + b1_ref[...]
                xhat, rstd = _ln_fwd(ALPHA * h1 + acc[rows, :])
                g2 = g2_ref[...]
                diff = xhat * g2 + b2_ref[...] - t_ref[rows, :]
                dy = diff * (1.0 / d)
                dpre = _ln_bwd(dy, xhat, rstd, g2)
                dpre_ref[rows, :] = dpre
                dpre16_ref[rows, :] = dpre.astype(BF16)
                sq = jnp.sum(jnp.sum(diff * diff, axis=1, keepdims=True), axis=0, keepdims=True)
                loss_ref[...] += jnp.broadcast_to(sq * (0.5 / d), (1, 128))
                gg_ref[...] += jnp.sum(dy * xhat, axis=0, keepdims=True)
                gb_ref[...] += jnp.sum(dy, axis=0, keepdims=True)

            _for_row_chunks(tm, rows_fn)

    row = pl.BlockSpec((tm, d), lambda i, k: (i, 0))
    vec = pl.BlockSpec((1, d), lambda i, k: (0, 0))
    return pl.pallas_call(
        body, name="down_ln_loss", grid=(s // tm, nk),
        in_specs=[pl.BlockSpec((tm, tk), lambda i, k: (i, k)),
                  pl.BlockSpec((tk, d), lambda i, k: (k, 0)),
                  row, _VMEM, _VMEM, _VMEM, _VMEM,
                  pl.BlockSpec((None, tm, d), lambda i, k: (0, i, 0))],
        out_specs=[row, row, pl.BlockSpec((1, 128), lambda i, k: (0, 0)), vec, vec],
        out_shape=[jax.ShapeDtypeStruct((s, d), F32), jax.ShapeDtypeStruct((s, d), BF16),
                   jax.ShapeDtypeStruct((1, 128), F32), jax.ShapeDtypeStruct((1, d), F32),
                   jax.ShapeDtypeStruct((1, d), F32)],
        scratch_shapes=[pltpu.VMEM((tm, d), F32)],
        compiler_params=_params(("arbitrary", "arbitrary")),
    )(act, w_down_g, xhat1, ln1_g, ln1_b, ln2_g, ln2_b, target)


def _dact_silu_bwd(dpre2, w_down_g, ab):
    s, d = dpre2.shape
    fs2 = ab.shape[1] // N_CHIPS
    fs = fs2 // 2
    tm = min(TM, s)

    def body(dp_ref, w_ref, ab_ref, dgu_ref):
        d_act = _dot_nt(dp_ref[...], w_ref[...])
        dgu_ref[:, :fs] = (d_act * ab_ref[:, :fs].astype(F32)).astype(BF16)
        dgu_ref[:, fs:] = (d_act * ab_ref[:, fs:].astype(F32)).astype(BF16)

    blk = pl.BlockSpec((tm, fs2), lambda j, i: (i, j))
    return pl.pallas_call(
        body, name="dact_silu_bwd", grid=(N_CHIPS, s // tm),
        in_specs=[pl.BlockSpec((tm, d), lambda j, i: (i, 0)),
                  pl.BlockSpec((fs, d), lambda j, i: (j, 0)), blk],
        out_specs=blk,
        out_shape=jax.ShapeDtypeStruct(ab.shape, BF16),
        compiler_params=_params(("parallel", "parallel")),
    )(dpre2, w_down_g, ab)


def _grad_rows(a, b, after, name, row_blocks=1):
    s, m = a.shape
    n = b.shape[1]
    ms = m // N_CHIPS
    tmw = ms // row_blocks
    tk = min(TK_TOK, s)
    nk = s // tk

    def body(a_ref, b_ref, after_ref, o_ref, acc):
        k = pl.program_id(2)
        _accumulate(acc, lambda: _dot_tn(a_ref[...].astype(BF16), b_ref[...].astype(BF16)), k, nk)

        @pl.when(k == nk - 1)
        def _():
            o_ref[...] = acc[...].astype(BF16)

    return pl.pallas_call(
        body, name=name, grid=(N_CHIPS, row_blocks, nk),
        in_specs=[pl.BlockSpec((tk, tmw), lambda j, r, k: (k, j * row_blocks + r)),
                  pl.BlockSpec((tk, n), lambda j, r, k: (k, 0)), _ANY],
        out_specs=pl.BlockSpec((None, tmw, n), lambda j, r, k: (j, r, 0)),
        out_shape=jax.ShapeDtypeStruct((N_CHIPS, ms, n), BF16),
        scratch_shapes=[pltpu.VMEM((tmw, n), F32)],
        compiler_params=_params(("parallel", "parallel", "arbitrary")),
    )(a, b, after)


def _grad_cols(a, bs, after, name, a_3d=False, row_blocks=2, shard=None):
    s, m = a.shape[-2:]
    n = bs[0].shape[1]
    ns = n // N_CHIPS
    nb = len(bs)
    tmw = m // row_blocks
    tk = min(TK_TOK, s)
    nk = s // tk

    def body(*refs):
        a_ref, b_refs, o_refs, accs = refs[0], refs[1:1 + nb], refs[2 + nb:2 + 2 * nb], refs[2 + 2 * nb:]
        k = pl.program_id(2)
        for b_ref, acc in zip(b_refs, accs):
            _accumulate(acc, lambda b_ref=b_ref: _dot_tn(a_ref[...].astype(BF16), b_ref[...].astype(BF16)), k, nk)

        @pl.when(k == nk - 1)
        def _():
            for o_ref, acc in zip(o_refs, accs):
                o_ref[...] = acc[...].astype(BF16)

    if a_3d:
        a_spec = pl.BlockSpec((None, tk, tmw), lambda j, r, k: (0, k, r))
    else:
        a_spec = pl.BlockSpec((tk, tmw), lambda j, r, k: (k, r))
    return _call_with_adamw(
        body, name, (N_CHIPS, row_blocks, nk),
        [a_spec] + [pl.BlockSpec((tk, ns), lambda j, r, k: (k, j))] * nb + [_ANY],
        [pl.BlockSpec((None, tmw, ns), lambda j, r, k: (j, r, 0))] * nb,
        [jax.ShapeDtypeStruct((N_CHIPS, m, ns), BF16)] * nb,
        [pltpu.VMEM((tmw, ns), F32)] * nb, ("parallel", "parallel", "arbitrary"), (a, *bs, after), shard)


def _dh1_ln_bwd(d_gu, w_gu_g, dpre2, xhat1, rstd1, ln1_g, after):
    s = d_gu.shape[0]
    d = dpre2.shape[1]
    hd = d // 2
    fs = w_gu_g.shape[2]
    tm = min(TM, s)

    def body(dgu_ref, w_ref, dp2_ref, xh_ref, rs_ref, g_ref, after_ref, dpre_ref, gg_ref, gb_ref, acc_lo, acc_hi):
        i, j, half = pl.program_id(0), pl.program_id(1), pl.program_id(2)

        def product():
            return _dot_nt(dgu_ref[...], w_ref[...])

        @pl.when(half == 0)
        def _():
            _accumulate(acc_lo, product, j, N_CHIPS)

        @pl.when(half == 1)
        def _():
            _accumulate(acc_hi, product, j, N_CHIPS)

        @pl.when((j == N_CHIPS - 1) & (half == 1))
        def _():
            @pl.when(i == 0)
            def _():
                gg_ref[...] = jnp.zeros_like(gg_ref)
                gb_ref[...] = jnp.zeros_like(gb_ref)

            def rows_fn(rows):
                dh = jnp.concatenate([acc_lo[rows, :], acc_hi[rows, :]], axis=1) + ALPHA * dp2_ref[rows, :]
                xhat = xh_ref[rows, :]
                dpre_ref[rows, :] = _ln_bwd(dh, xhat, rs_ref[rows, :], g_ref[...])
                gg_ref[...] += jnp.sum(dh * xhat, axis=0, keepdims=True)
                gb_ref[...] += jnp.sum(dh, axis=0, keepdims=True)

            _for_row_chunks(tm, rows_fn)

    row = pl.BlockSpec((tm, d), lambda i, j, h: (i, 0))
    vec = pl.BlockSpec((1, d), lambda i, j, h: (0, 0))
    act_blk = pl.BlockSpec((tm, fs), lambda i, j, h: (i, j))
    w_blk = pl.BlockSpec((None, hd, fs), lambda i, j, h: (j, h, 0))
    return pl.pallas_call(
        body, name="dh1_ln_bwd", grid=(s // tm, N_CHIPS, 2),
        in_specs=[act_blk, w_blk, row, row, pl.BlockSpec((tm, 1), lambda i, j, h: (i, 0)), _VMEM, _ANY],
        out_specs=[row, vec, vec],
        out_shape=[jax.ShapeDtypeStruct((s, d), F32), jax.ShapeDtypeStruct((1, d), F32),
                   jax.ShapeDtypeStruct((1, d), F32)],
        scratch_shapes=[pltpu.VMEM((tm, hd), F32)] * 2,
        compiler_params=_params(("arbitrary", "arbitrary", "arbitrary")),
    )(d_gu, w_gu_g, dpre2, xhat1, rstd1, ln1_g, after)


def _dmixed_rms_bwd(dpre1, w_out_g, ac, rstd, g_ac):
    s, d = dpre1.shape
    hd = d // 2
    tm = min(TM, s)

    def body(dp_ref, w_ref, ac_ref, rs_ref, g_ref, dac_ref, gg_ref):
        i = pl.program_id(1)
        dm = _dot_nt(dp_ref[...].astype(BF16), w_ref[...])
        pre = ac_ref[...]
        r = rs_ref[...]
        gdm = dm * g_ref[...]
        dac_ref[...] = r * gdm - pre * (r * r * r) * jnp.mean(gdm * pre, axis=-1, keepdims=True)
        gg = jnp.sum(dm * pre * r, axis=0, keepdims=True)

        @pl.when(i == 0)
        def _():
            gg_ref[...] = gg

        @pl.when(i > 0)
        def _():
            gg_ref[...] += gg

    return pl.pallas_call(
        body, name="dmixed_rms_bwd", grid=(2, s // tm),
        in_specs=[pl.BlockSpec((tm, d), lambda h, i: (i, 0)),
                  pl.BlockSpec((hd, d), lambda h, i: (h, 0)),
                  pl.BlockSpec((tm, hd), lambda h, i: (i, h)),
                  pl.BlockSpec((None, tm, 1), lambda h, i: (h, i, 0)),
                  pl.BlockSpec((1, hd), lambda h, i: (0, h))],
        out_specs=[pl.BlockSpec((tm, hd), lambda h, i: (i, h)),
                   pl.BlockSpec((1, hd), lambda h, i: (0, h))],
        out_shape=[jax.ShapeDtypeStruct((s, d), F32), jax.ShapeDtypeStruct((1, d), F32)],
        compiler_params=_params(("arbitrary", "arbitrary")),
    )(dpre1, w_out_g, ac, rstd, g_ac)


def _attention_bwd(proj, d_ac, cos_t, sin_t, sinks, after, shard):
    s = proj.shape[0]
    qw = GROUP * N_KV_HEADS * HEAD_DIM
    kvw = N_KV_HEADS * HEAD_DIM
    nb = s // WINDOW
    nq = GROUP * N_KV_HEADS

    def body(cur_ref, prev_ref, do_ref, cos_ref, sin_ref, cosp_ref, sinp_ref, sinks_ref, after_ref,
             dq_ref, dcur_ref, dprev_ref, dsink_ref):
        n = pl.program_id(0)
        first = n == 0
        q, k_all, v_all, cos_q, sin_q = _roped_qkv(cur_ref, prev_ref, cos_ref, sin_ref, cosp_ref, sinp_ref, qw, kvw)
        kk2s = [_pair_operand(k_all, h) for h in range(N_KV_HEADS)]
        vv2s = [_pair_operand(v_all, h) for h in range(N_KV_HEADS)]
        qps, probs, p_sinks = _all_probs(q, kk2s, first, sinks_ref)
        dops = [do_ref[:, pair * PAIR:(pair + 1) * PAIR].astype(BF16) for pair in range(N_PAIRS)]
        d_probs = jnp.concatenate([_dot_nt(dops[pair], vv2s[pair // (GROUP // 2)]) for pair in range(N_PAIRS)], axis=0)
        d_s, ds_sinks = [], []
        for t in range(2):
            cols = slice(t * KEYS, (t + 1) * KEYS)
            delta = jnp.sum(probs[:, cols] * d_probs[:, cols], axis=1, keepdims=True)
            d_s.append(probs[:, cols] * (d_probs[:, cols] - delta))
            ds_sinks.append(-p_sinks[t] * delta)
        d_s = jnp.concatenate(d_s, axis=1).astype(BF16)
        probs = probs.astype(BF16)
        dq_parts, dk_tiles, dv_tiles, dsink_parts = [], [], [], []
        for h in range(N_KV_HEADS):
            dkk2, dvv2 = None, None
            for p in range(GROUP // 2):
                pair = (GROUP // 2) * h + p
                rows = slice(pair * WINDOW, (pair + 1) * WINDOW)
                dq_parts.append(_dot(d_s[rows], kk2s[h]) * ATTN_SCALE)
                dk_term = _dot_tn(d_s[rows], qps[pair])
                dv_term = _dot_tn(probs[rows], dops[pair])
                dkk2 = dk_term if dkk2 is None else dkk2 + dk_term
                dvv2 = dv_term if dvv2 is None else dvv2 + dv_term
                dsink_parts.extend([jnp.sum(ds_sinks[t][rows], axis=0, keepdims=True) for t in range(2)])
            dk_tiles.append(_pair_grad(dkk2, h))
            dv_tiles.append(_pair_grad(dvv2, h))
        dq_ref[...] = _rope(jnp.concatenate(dq_parts, axis=1), cos_q, sin_q, -1.0)
        dk = jnp.concatenate([dk_tiles[0] + dk_tiles[1], dk_tiles[2] + dk_tiles[3]], axis=1)
        dv = jnp.concatenate([dv_tiles[0] + dv_tiles[1], dv_tiles[2] + dv_tiles[3]], axis=1)
        dprev_ref[...] = jnp.concatenate([dk[:WINDOW], dv[:WINDOW]], axis=1)
        dcur_ref[...] = jnp.concatenate([dk[WINDOW:], dv[WINDOW:]], axis=1)
        dsink = jnp.concatenate(dsink_parts, axis=1)

        @pl.when(first)
        def _():
            dsink_ref[...] = dsink

        @pl.when(n > 0)
        def _():
            dsink_ref[...] += dsink

    tbl = pl.BlockSpec((WINDOW, kvw), lambda n: (n, 0))
    tbl_prev = pl.BlockSpec((WINDOW, kvw), lambda n: (jnp.maximum(n - 1, 0), 0))
    kv_blk = pl.BlockSpec((WINDOW, 2 * kvw), lambda n: (n, 0))
    return _call_with_adamw(
        body, "attention_bwd", (nb,),
        [pl.BlockSpec((WINDOW, qw + 2 * kvw), lambda n: (n, 0)),
         pl.BlockSpec((WINDOW, 2 * kvw), lambda n: (jnp.maximum(n - 1, 0), (qw // (2 * kvw)))),
         pl.BlockSpec((WINDOW, qw), lambda n: (n, 0)),
         tbl, tbl, tbl_prev, tbl_prev, _VMEM, _ANY],
        [pl.BlockSpec((WINDOW, qw), lambda n: (n, 0)), kv_blk, kv_blk, pl.BlockSpec((1, nq), lambda n: (0, 0))],
        [jax.ShapeDtypeStruct((s, qw), F32), jax.ShapeDtypeStruct((s, 2 * kvw), F32),
         jax.ShapeDtypeStruct((s, 2 * kvw), F32), jax.ShapeDtypeStruct((1, nq), F32)],
        [], ("arbitrary",), (proj, proj, d_ac, cos_t, sin_t, cos_t, sin_t, sinks, after), shard)


def _dproj_assemble(proj, d_ac, dq, dkv_cur, dkv_prev, cos_t, sin_t, cw_full):
    s, in_w = proj.shape
    cw = dq.shape[1]
    kvw = N_KV_HEADS * HEAD_DIM
    blk_w = in_w // 3
    tb = WINDOW
    nb = s // tb

    def body(lo_ref, hi_ref, lo_p_ref, hi_p_ref, lo_n_ref, hi_n_ref, dconv_ref, dconv_n_ref,
             dq_ref, dcur_ref, dprev_n_ref, cos_ref, sin_ref, cw_ref, dproj_ref, gcw_ref):
        i = pl.program_id(0)
        last = i == nb - 1
        c_gate, b_gate, u = _split_cbu(lo_ref[...], hi_ref[...], cw)
        c_p, _, u_p = _split_cbu(lo_p_ref[...], hi_p_ref[...], cw)
        _, b_n, _ = _split_cbu(lo_n_ref[...], hi_n_ref[...], cw)
        z = c_gate * u
        z_p = jnp.where(i == 0, 0.0, c_p * u_p)
        z1 = _shift_down(z, z_p, 1)
        z2 = _shift_down(z, z_p, 2)
        w0, w1, w2 = _conv_taps(cw_ref)
        y = w0 * z2 + w1 * z1 + w2 * z
        d_conv = dconv_ref[...]
        d_b = d_conv * y
        d_y = d_conv * b_gate
        d_y_n = jnp.where(last, 0.0, dconv_n_ref[...] * b_n[:dconv_n_ref.shape[0]])
        d_z = w2 * d_y + w1 * _shift_up(d_y, d_y_n, 1) + w0 * _shift_up(d_y, d_y_n, 2)
        d_c = d_z * u
        d_u = d_z * c_gate
        gcw = jnp.concatenate([jnp.sum(d_y * z2, axis=0, keepdims=True), jnp.sum(d_y * z1, axis=0, keepdims=True),
                               jnp.sum(d_y * z, axis=0, keepdims=True)], axis=0)

        @pl.when(i == 0)
        def _():
            gcw_ref[...] = gcw

        @pl.when(i > 0)
        def _():
            gcw_ref[...] += gcw

        dkv = dcur_ref[...] + jnp.where(last, 0.0, dprev_n_ref[...])
        dk = _rope(dkv[:, :kvw], cos_ref[...], sin_ref[...], -1.0)
        dproj_ref[...] = jnp.concatenate([dq_ref[...], dk, dkv[:, kvw:], d_c, d_b, d_u], axis=1).astype(BF16)

    prev_halo = lambda i: jnp.maximum(i * (tb // HALO_ROWS) - 1, 0)
    next_halo = lambda i: jnp.minimum((i + 1) * (tb // HALO_ROWS), s // HALO_ROWS - 1)
    next8 = lambda i: jnp.minimum((i + 1) * (tb // 8), s // 8 - 1)
    nxt = lambda i: jnp.minimum(i + 1, nb - 1)
    return pl.pallas_call(
        body, name="dproj_assemble", grid=(nb,),
        in_specs=[pl.BlockSpec((tb, blk_w), lambda i: (i, 1)),
                  pl.BlockSpec((tb, blk_w), lambda i: (i, 2)),
                  pl.BlockSpec((HALO_ROWS, blk_w), lambda i: (prev_halo(i), 1)),
                  pl.BlockSpec((HALO_ROWS, blk_w), lambda i: (prev_halo(i), 2)),
                  pl.BlockSpec((HALO_ROWS, blk_w), lambda i: (next_halo(i), 1)),
                  pl.BlockSpec((HALO_ROWS, blk_w), lambda i: (next_halo(i), 2)),
                  pl.BlockSpec((tb, cw), lambda i: (i, 1)),
                  pl.BlockSpec((8, cw), lambda i: (next8(i), 1)),
                  pl.BlockSpec((tb, cw), lambda i: (i, 0)),
                  pl.BlockSpec((tb, 2 * kvw), lambda i: (i, 0)),
                  pl.BlockSpec((tb, 2 * kvw), lambda i: (nxt(i), 0)),
                  pl.BlockSpec((tb, kvw), lambda i: (i, 0)),
                  pl.BlockSpec((tb, kvw), lambda i: (i, 0)),
                  _VMEM],
        out_specs=[pl.BlockSpec((tb, in_w), lambda i: (i, 0)),
                   pl.BlockSpec((3, cw), lambda i: (0, 0))],
        out_shape=[jax.ShapeDtypeStruct((s, in_w), BF16), jax.ShapeDtypeStruct((3, cw), F32)],
        compiler_params=_params(("arbitrary",)),
    )(proj, proj, proj, proj, proj, proj, d_ac, d_ac, dq, dkv_cur, dkv_prev, cos_t, sin_t, cw_full)


def _dx(d_proj, w_in_g, dpre1, after, shard):
    s, in_w = d_proj.shape
    ns, d, ncol = w_in_g.shape
    tm = min(TM, s)

    def body(dp_ref, w_ref, r_ref, after_ref, o_ref, acc):
        j = pl.program_id(1)
        _accumulate(acc, lambda: _dot_nt(dp_ref[...], w_ref[...]), j, ns)

        @pl.when(j == ns - 1)
        def _():
            o_ref[...] = acc[...] + ALPHA * r_ref[...]

    return _call_with_adamw(
        body, "dx", (s // tm, ns),
        [pl.BlockSpec((tm, ncol), lambda i, j: (i, j)),
         pl.BlockSpec((None, d, ncol), lambda i, j: (j, 0, 0)),
         pl.BlockSpec((tm, d), lambda i, j: (i, 0)), _ANY],
        [pl.BlockSpec((None, tm, d), lambda i, j: (0, i, 0))], [jax.ShapeDtypeStruct((1, s, d), F32)],
        [pltpu.VMEM((tm, d), F32)], ("parallel", "arbitrary"), (d_proj, w_in_g, dpre1, after), shard)


def kernel(x, positions, w_in, conv_w, sinks, g_attn, g_conv, w_out, ln1_g, ln1_b, w_gate, w_up, w_down, ln2_g, ln2_b, loss_target, m_w_in, m_conv_w, m_sinks, m_g_attn, m_g_conv, m_w_out, m_ln1_g, m_ln1_b, m_w_gate, m_w_up, m_w_down, m_ln2_g, m_ln2_b, v_w_in, v_conv_w, v_sinks, v_g_attn, v_g_conv, v_w_out, v_ln1_g, v_ln1_b, v_w_gate, v_w_up, v_w_down, v_ln2_g, v_ln2_b):
    s = x.shape[1]
    d = x.shape[2]

    chip_vec = _chip_id(lax.axis_index("x"), lax.axis_index("y")).astype(jnp.int32).reshape(1)
    wnames = ["w_in", "w_out", "w_gu", "w_down"]
    buf_in = _cast_weight(w_in, chip_vec, chip_vec, "cast_w_in")
    flight_in, token_in = _gather_start([buf_in], chip_vec, "gather_start_w_in")
    cw_buf = lax.dynamic_update_slice(jnp.zeros((N_CHIPS,) + conv_w.shape[1:], F32), conv_w, (chip_vec[0], 0, 0))
    cw_flight = _flight_start("conv_w_start", [cw_buf], _conv_w_plan(), 3, token_in)
    started = cw_flight[2][0]
    buf_gu = _cast_weight(w_gate, chip_vec, started, "cast_w_gate", 0, 2)
    buf_gu = _cast_weight(w_up, chip_vec, buf_gu, "cast_w_up", 1, 2)
    bufs = [_cast_weight(w_out, chip_vec, started, "cast_w_out"), buf_gu,
            _cast_weight(w_down, chip_vec, started, "cast_w_down")]
    flights_rest, token = _gather_start(bufs, token_in, "gather_start_rest")
    flights = flight_in + flights_rest

    def gathered(i, after):
        send_sems, recv_sems, buf = flights[i]
        buf = _gather_wait(send_sems, recv_sems, buf, after, "gather_wait_" + wnames[i])
        return _sibling_fill(buf, "sibling_fill_" + wnames[i])

    g_ac = jnp.concatenate([g_attn, g_conv], axis=1)

    proj_own = _in_proj(x, _after(flights[0][2], token), chip_vec, 1, None, "in_proj_own")
    cos_t, sin_t = _rope_tables(positions.reshape(s, 1) + token[0:1, 0:1].astype(jnp.int32))
    w_in_g = gathered(0, _after(cos_t, proj_own))
    proj = _in_proj(x, w_in_g, chip_vec + 1, N_CHIPS - 1, proj_own, "in_proj_rest")
    send_sems, recv_sems, buf_out = flights[1]
    buf_out = _gather_wait(send_sems, recv_sems, buf_out, proj, "gather_wait_w_out")
    fill_out = _flight_start("fill_start_w_out", [buf_out], _fill_plan(1), 3, chip_vec)
    attn = _attention_fwd(_after(proj, fill_out[2][0]), cos_t, sin_t, sinks)
    (cw_full,) = _flight_wait("conv_w_wait", cw_flight, _conv_w_plan(), attn)
    mixed, ac, rstd_ac = _conv_norm(proj, attn, cw_full, g_ac)
    (w_out_g,) = _flight_wait("fill_wait_w_out", fill_out, _fill_plan(1), mixed)
    w_out_full = w_out_g.reshape(d, d)
    xhat1, h1, rstd1 = _out_proj_ln(mixed, w_out_full, x, ln1_g, ln1_b)
    send_sems, recv_sems, buf_gu = flights[2]
    buf_gu = _gather_wait(send_sems, recv_sems, buf_gu, h1, "gather_wait_w_gu")
    fill_gu = _flight_start("fill_start_w_gu", [buf_gu], _fill_plan(1), 3, chip_vec)
    own = _gate_up(h1, fill_gu[2][0], chip_vec, 1, None, "gate_up_own")
    (w_gu_g,) = _flight_wait("fill_wait_w_gu", fill_gu, _fill_plan(1), own[0])
    some = _gate_up(h1, w_gu_g, chip_vec + 1, N_CHIPS - 2, own, "gate_up_rest")
    send_sems, recv_sems, buf_down = flights[3]
    buf_down = _gather_wait(send_sems, recv_sems, buf_down, some[0], "gather_wait_w_down")
    fill_down = _flight_start("fill_start_w_down", [buf_down], _fill_plan(1), 3, chip_vec)
    act, ab = _gate_up(h1, _after(w_gu_g, fill_down[2][0]), chip_vec + N_CHIPS - 1, 1, some, "gate_up_last")
    (w_down_g,) = _flight_wait("fill_wait_w_down", fill_down, _fill_plan(1), act)
    w_down_full = w_down_g.reshape(-1, d)
    dpre2, dpre2_16, loss_part, g_ln2_g, g_ln2_b = _down_ln_loss(act, w_down_full, xhat1, ln1_g, ln1_b, ln2_g, ln2_b,
                                                                 loss_target)

    cvec = lax.axis_index("c").astype(jnp.int32).reshape(1)

    def exchange_begin(parts, nme):
        bufs = []
        for part in parts:
            ns, r, cdim = part.shape
            bufs.extend([part, lax.empty((ns, r // 2, cdim), part.dtype)])
        return _flight_start("exchange_start_" + nme, bufs, _exchange_plan(len(parts)), len(parts), cvec)

    def exchange_end(flight, n_parts, after, nme):
        bufs = _flight_wait("exchange_wait_" + nme, flight, _exchange_plan(n_parts), after)
        return [(bufs[2 * w], bufs[2 * w + 1]) for w in range(n_parts)]

    def scatter_begin(part, got, nme):
        return _scatter_start(_add_halves(part, got, cvec, "add_halves_" + nme), "scatter_start_" + nme)

    d_gu = _dact_silu_bwd(dpre2_16, w_down_full, ab)
    p_down = _grad_rows(act, dpre2_16, d_gu, "grad_w_down")
    x_down = exchange_begin([p_down], "w_down")
    (p_gu,) = _grad_cols(h1, [d_gu], x_down[2][0], "grad_w_gate_up")
    ((p_down, got),) = exchange_end(x_down, 1, p_gu, "w_down")
    f_down = scatter_begin(p_down, got, "w_down")
    x_gu = exchange_begin([_after(p_gu, f_down[2])], "w_gu")
    dpre1, g_ln1_g, g_ln1_b = _dh1_ln_bwd(d_gu, w_gu_g, dpre2, xhat1, rstd1, ln1_g, x_gu[2][0])
    ((p_gu, got),) = exchange_end(x_gu, 1, dpre1, "w_gu")
    f_gu = scatter_begin(p_gu, got, "w_gu")
    d_ac, g_g_ac = _dmixed_rms_bwd(_after(dpre1, f_gu[2]), w_out_full, ac, rstd_ac, g_ac)
    pos_vec = jnp.concatenate([chip_vec, cvec])
    sums, land = _scatter_wait(*f_down, d_ac, "scatter_wait_w_down")
    c_down = _flight_start("complete_start_w_down", [sums, land], _complete_plan(1), 4, cvec)
    p_out = _grad_rows(mixed, dpre1, c_down[2][1], "grad_w_out")
    x_out = exchange_begin([p_out], "w_out")
    sums, land = _flight_wait("complete_wait_w_down", c_down, _complete_plan(1), x_out[2][0])
    dq, dkv_cur, dkv_prev, g_sinks, *new_w_down = _attention_bwd(
        proj, d_ac, cos_t, sin_t, sinks, x_out[2][0], (w_down, m_w_down, v_w_down, land, sums, pos_vec, 0))
    ((p_out, got),) = exchange_end(x_out, 1, dq, "w_out")
    f_out = scatter_begin(p_out, got, "w_out")
    sums, land = _scatter_wait(*f_gu, f_out[2], "scatter_wait_w_gu")
    c_gu = _flight_start("complete_start_w_gu", [sums, land], _complete_plan(1), 4, cvec)
    d_proj, g_conv_w = _dproj_assemble(proj, _after(d_ac, c_gu[2][1]), dq, dkv_cur, dkv_prev, cos_t, sin_t, cw_full)
    small_parts = _small_pack(g_ln2_g, g_ln2_b, g_ln1_g, g_ln1_b, g_g_ac, g_conv_w, g_sinks, loss_part)
    f_small = _flight_start("small_start", [small_parts], _small_plan(), N_DEVICES - 1, cvec)
    sums_gu, land_gu = _flight_wait("complete_wait_w_gu", c_gu, _complete_plan(1), f_small[2][0])
    p_in, *new_w_gate = _grad_cols(x, [d_proj], f_small[2][0], "grad_w_in", a_3d=True,
                                   shard=(w_gate, m_w_gate, v_w_gate, land_gu, sums_gu, pos_vec, 0))
    (small_parts,) = _flight_wait("small_wait", f_small, _small_plan(), p_in)
    red = _small_sum(small_parts)
    x_in = exchange_begin([_after(p_in, red)], "w_in")
    sums, land = _scatter_wait(*f_out, x_in[2][0], "scatter_wait_w_out")
    c_out = _flight_start("complete_start_w_out", [sums, land], _complete_plan(1), 4, cvec)
    new_w_up = _adamw_shard(w_up, m_w_up, v_w_up, _after(land_gu, c_out[2][1]), sums_gu, pos_vec, "adamw_w_up", 1)
    ((p_in, got),) = exchange_end(x_in, 1, new_w_up[0], "w_in")
    f_in = scatter_begin(p_in, got, "w_in")
    (grad_x,) = _dx(d_proj, w_in_g, dpre1, f_in[2], None)

    big = {"w_down": new_w_down, "w_gate": new_w_gate, "w_up": new_w_up}
    sums, land = _scatter_wait(*f_in, grad_x, "scatter_wait_w_in")
    c_in = _flight_start("complete_start_w_in", [sums, land], _complete_plan(1), 4, cvec)
    sums, land = _flight_wait("complete_wait_w_out", c_out, _complete_plan(1), c_in[2][1])
    big["w_out"] = _adamw_shard(w_out, m_w_out, v_w_out, land, sums, pos_vec, "adamw_w_out")
    sums, land = _flight_wait("complete_wait_w_in", c_in, _complete_plan(1), big["w_out"][0])
    big["w_in"] = _adamw_shard(w_in, m_w_in, v_w_in, land, sums, pos_vec, "adamw_w_in")
    small = _adamw_small(red, {
        "sinks": (sinks, m_sinks, v_sinks), "g_attn": (g_attn, m_g_attn, v_g_attn),
        "g_conv": (g_conv, m_g_conv, v_g_conv), "ln1_g": (ln1_g, m_ln1_g, v_ln1_g),
        "ln1_b": (ln1_b, m_ln1_b, v_ln1_b), "ln2_g": (ln2_g, m_ln2_g, v_ln2_g),
        "ln2_b": (ln2_b, m_ln2_b, v_ln2_b), "conv_w": (conv_w, m_conv_w, v_conv_w)})
    res = {**big, **small}
    order = ["w_in", "conv_w", "sinks", "g_attn", "g_conv", "w_out", "ln1_g", "ln1_b", "w_gate", "w_up", "w_down",
             "ln2_g", "ln2_b"]
    loss = red[6, d // 2 + 128]
    return (loss, grad_x, *[res[n][0] for n in order], *[res[n][1] for n in order],
            *[res[n][2] for n in order], *[res[n][3] for n in order])
```

```python
import functools

import numpy as np
import jax
import jax.numpy as jnp
from jax import lax
from jax.experimental import pallas as pl
from jax.experimental.pallas import tpu as pltpu

F32 = jnp.float32
BF16 = jnp.bfloat16
MESH = pl.DeviceIdType.MESH

HEAD_DIM = 64
N_KV_HEADS = 4
GROUP = 4
WINDOW = 128
ROT_DIM = 16
ROPE_THETA = 500000.0
ATTN_SCALE = HEAD_DIM ** -0.5
ALPHA = 2.0 ** 0.25
LN_EPS = 1e-5
RMS_EPS = 1e-6
ADAM_LR = 0.001
ADAM_B1 = 0.9
ADAM_B2 = 0.999
ADAM_EPS = 1e-08
ADAM_WD = 0.01
ADAM_STEP = 10
N_CHIPS = 4
NEG_BIG = -1e30

V7X_VMEM_BYTES = 64 * 1024 * 1024
VMEM_LIMIT = V7X_VMEM_BYTES - 6 * 1024 * 1024

TM = 512
TK_TOK = 1024
TB_CONV = 256
TR_ELT = 256
ROW_CHUNK = 128
HALO_ROWS = 16


def _params(sem):
    return pltpu.CompilerParams(dimension_semantics=sem, vmem_limit_bytes=VMEM_LIMIT)


def _row_tile(rows, target):
    best = None
    for t in range(16, min(rows, target) + 1, 16):
        if rows % t == 0:
            best = t
    assert best is not None, (rows, target)
    return best


def _dot(a, b):
    return jnp.dot(a, b, preferred_element_type=F32)


def _dot_nt(a, b):
    return lax.dot_general(a, b, (((1,), (1,)), ((), ())), preferred_element_type=F32)


def _dot_tn(a, b):
    return lax.dot_general(a, b, (((0,), (0,)), ((), ())), preferred_element_type=F32)


def _mesh_pos():
    x, y, c = lax.axis_index("x"), lax.axis_index("y"), lax.axis_index("c")
    chips = [(1 - x, y), (x, 1 - y), (1 - x, 1 - y)]
    return x, y, c, chips


def _chip_id(px, py):
    return 2 * px + py


def _rope(t, cos, sgn_sin, sign):
    w = t.shape[1]
    lane = lax.broadcasted_iota(jnp.int32, t.shape, 1) & (HEAD_DIM - 1)
    partner = jnp.where(lane < ROT_DIM // 2, pltpu.roll(t, w - ROT_DIM // 2, 1), pltpu.roll(t, ROT_DIM // 2, 1))
    return t * cos + sign * (partner * sgn_sin)


def _tile_lanes(t, n):
    return jnp.concatenate([t] * n, axis=1)


def _sigmoid(g):
    return 1.0 / (1.0 + jnp.exp(-g))


def _for_row_chunks(n_rows, fn):
    def step(r, carry):
        fn(pl.ds(pl.multiple_of(r * ROW_CHUNK, ROW_CHUNK), ROW_CHUNK))
        return carry

    lax.fori_loop(0, n_rows // ROW_CHUNK, step, 0)


def _accumulate(acc, make_val, k, nk):
    if nk == 1:
        acc[...] = make_val()
        return

    @pl.when(k == 0)
    def _():
        acc[...] = jnp.zeros_like(acc)

    acc[...] += make_val()


def _ln_fwd(pre):
    mu = jnp.mean(pre, axis=-1, keepdims=True)
    cen = pre - mu
    var = jnp.mean(cen * cen, axis=-1, keepdims=True)
    rstd = lax.rsqrt(var + LN_EPS)
    return cen * rstd, rstd


def _ln_bwd(dy, xhat, rstd, g):
    dxhat = dy * g
    m1 = jnp.mean(dxhat, axis=-1, keepdims=True)
    m2 = jnp.mean(dxhat * xhat, axis=-1, keepdims=True)
    return rstd * (dxhat - m1 - xhat * m2)


def _cast_weight(w, chip_vec, after, name, col_block=0, n_col_blocks=1):
    _, r, c = w.shape
    tr = _row_tile(r, TR_ELT)

    def body(chip_ref, w_ref, after_ref, o_ref):
        o_ref[...] = w_ref[...].astype(BF16)

    grid_spec = pltpu.PrefetchScalarGridSpec(
        num_scalar_prefetch=1, grid=(r // tr,),
        in_specs=[pl.BlockSpec((None, tr, c), lambda i, chip_ref: (0, i, 0)), _ANY],
        out_specs=pl.BlockSpec((None, tr, c), lambda i, chip_ref: (chip_ref[0], i, col_block)))
    return pl.pallas_call(
        body, name=name, grid_spec=grid_spec,
        out_shape=jax.ShapeDtypeStruct((N_CHIPS, r, n_col_blocks * c), BF16),
        input_output_aliases={2: 0} if col_block else {},
        compiler_params=_params(("parallel",)),
    )(chip_vec, w, after)


_HBM = pl.BlockSpec(memory_space=pltpu.HBM)
_VMEM = pl.BlockSpec(memory_space=pltpu.VMEM)


_SEM = pl.BlockSpec(memory_space=pltpu.SEMAPHORE)
_ANY = pl.BlockSpec(memory_space=pl.ANY)
_EFFECT = pltpu.SideEffectType.DATAFLOW_SIDE_EFFECTING


def _chip_copy(buf, k, chip_of_src, half_rows, send_sems, recv_sems, to):
    part = buf.at[chip_of_src, half_rows]
    return pltpu.make_async_remote_copy(
        src_ref=part, dst_ref=part, send_sem=send_sems.at[k], recv_sem=recv_sems.at[k], device_id=to, device_id_type=MESH)


def _half_rows(buf, which):
    hr = buf.shape[1] // 2
    return pl.ds(which * hr, hr)


def _after(value, dep):
    return lax.optimization_barrier((value, dep))[0]


def _flight_start(name, bufs, plan, n_sems, after):
    n = len(bufs)

    def body(*refs):
        sends, _ = plan(refs[:n], refs[n + 1], refs[n + 2])
        for cp in sends:
            cp.start()

    outs = pl.pallas_call(
        body, name=name,
        in_specs=[_HBM] * n + [_ANY], out_specs=[_SEM, _SEM] + [_HBM] * n,
        out_shape=[pltpu.SemaphoreType.DMA((n_sems,))] * 2 + [pltpu.HBM(b.shape, b.dtype) for b in bufs],
        input_output_aliases={i: 2 + i for i in range(n)},
        compiler_params=pltpu.CompilerParams(has_side_effects=_EFFECT),
    )(*[pltpu.with_memory_space_constraint(b, pltpu.HBM) for b in bufs], after)
    return outs[0], outs[1], list(outs[2:])


def _flight_wait(name, flight, plan, after):
    send_sems, recv_sems, bufs = flight
    n = len(bufs)

    def body(*refs):
        sends, recvs = plan(refs[:n], refs[n], refs[n + 1])
        for cp in sends:
            cp.wait_send()
        for cp in recvs:
            cp.wait_recv()

    outs = pl.pallas_call(
        body, name=name,
        in_specs=[_HBM] * n + [_SEM, _SEM, _ANY], out_specs=[_HBM] * n,
        out_shape=[pltpu.HBM(b.shape, b.dtype) for b in bufs],
        input_output_aliases={i: i for i in range(n)},
        compiler_params=pltpu.CompilerParams(has_side_effects=_EFFECT),
    )(*bufs, send_sems, recv_sems, after)
    return list(outs)


def _fill_plan(n_bufs):
    def plan(refs, send_sems, recv_sems):
        x, y, c, chips = _mesh_pos()
        sibling = (x, y, 1 - c)
        sends, recvs = [], []
        for w in range(n_bufs):
            for k, chip in enumerate(chips):
                slot = _chip_id(*chip)
                sends.append(_chip_copy(refs[w], 3 * w + k, slot, _half_rows(refs[w], c), send_sems, recv_sems, sibling))
                recvs.append(_chip_copy(refs[w], 3 * w + k, slot, _half_rows(refs[w], 1 - c), send_sems, recv_sems,
                                        sibling))
        return sends, recvs
    return plan


def _conv_w_plan():
    def plan(refs, send_sems, recv_sems):
        x, y, c, chips = _mesh_pos()
        me = _chip_id(x, y)
        (buf,) = refs
        sends, recvs = [], []
        for k, chip in enumerate(chips):
            for slot, into in ((me, sends), (_chip_id(*chip), recvs)):
                into.append(pltpu.make_async_remote_copy(
                    src_ref=buf.at[slot], dst_ref=buf.at[slot], send_sem=send_sems.at[k], recv_sem=recv_sems.at[k],
                    device_id=(*chip, c), device_id_type=MESH))
        return sends, recvs
    return plan


def _exchange_plan(n_parts):
    def plan(refs, send_sems, recv_sems):
        x, y, c, _ = _mesh_pos()
        copies = []
        for w in range(n_parts):
            part, got = refs[2 * w], refs[2 * w + 1]
            hr = got.shape[1]
            copies.append(pltpu.make_async_remote_copy(
                src_ref=part.at[:, pl.ds((1 - c) * hr, hr)], dst_ref=got, send_sem=send_sems.at[w],
                recv_sem=recv_sems.at[w], device_id=(x, y, 1 - c), device_id_type=MESH))
        return copies, copies
    return plan


def _gather_start(bufs, after, name):
    n = len(bufs)

    def body(*refs):
        ins = refs[:n]
        sends, recvs = refs[n + 1:2 * n + 1], refs[2 * n + 1:3 * n + 1]
        token = refs[4 * n + 1]
        x, y, c, chips = _mesh_pos()
        me = _chip_id(x, y)
        for w in range(n):
            for k, chip in enumerate(chips):
                _chip_copy(ins[w], k, me, _half_rows(ins[w], c), sends[w], recvs[w], (*chip, c)).start()
        token[...] = jnp.zeros_like(token)

    outs = pl.pallas_call(
        body, name=name,
        in_specs=[_HBM] * n + [_ANY],
        out_specs=[_SEM] * (2 * n) + [_HBM] * n + [_VMEM],
        out_shape=[pltpu.SemaphoreType.DMA((3,))] * (2 * n) + [pltpu.HBM(b.shape, b.dtype) for b in bufs]
        + [jax.ShapeDtypeStruct((8, 128), F32)],
        input_output_aliases={w: 2 * n + w for w in range(n)},
        compiler_params=pltpu.CompilerParams(has_side_effects=_EFFECT),
    )(*[pltpu.with_memory_space_constraint(b, pltpu.HBM) for b in bufs], after)
    return [(outs[w], outs[n + w], outs[2 * n + w]) for w in range(n)], outs[3 * n]


def _gather_wait(send_sems, recv_sems, buf, after, name):
    def body(buf_ref, send_ref, recv_ref, after_ref, out_ref):
        x, y, c, chips = _mesh_pos()
        me = _chip_id(x, y)
        for k, chip in enumerate(chips):
            _chip_copy(buf_ref, k, me, _half_rows(buf_ref, c), send_ref, recv_ref, (*chip, c)).wait_send()
        for k, chip in enumerate(chips):
            _chip_copy(buf_ref, k, _chip_id(*chip), _half_rows(buf_ref, c), send_ref, recv_ref, (*chip, c)).wait_recv()

    return pl.pallas_call(
        body, name=name,
        in_specs=[_HBM, _SEM, _SEM, _ANY], out_specs=_HBM,
        out_shape=pltpu.HBM(buf.shape, buf.dtype),
        input_output_aliases={0: 0},
        compiler_params=pltpu.CompilerParams(has_side_effects=_EFFECT),
    )(buf, send_sems, recv_sems, after)


def _sibling_fill(buf, name, own_too=False):
    n_copies = 4 if own_too else 3

    def body(buf_ref, out_ref, send_sems, recv_sems):
        x, y, c, chips = _mesh_pos()
        sibling = (x, y, 1 - c)
        slots = [_chip_id(*chip) for chip in chips] + ([_chip_id(x, y)] if own_too else [])
        copies = []
        for k, slot in enumerate(slots):
            cp = _chip_copy(out_ref, k, slot, _half_rows(out_ref, c), send_sems, recv_sems, sibling)
            cp.start()
            copies.append(cp)
        for k, slot in enumerate(slots):
            _chip_copy(out_ref, k, slot, _half_rows(out_ref, 1 - c), send_sems, recv_sems, sibling).wait_recv()
        for cp in copies:
            cp.wait_send()

    return pl.pallas_call(
        body, name=name,
        in_specs=[_HBM], out_specs=_HBM,
        out_shape=jax.ShapeDtypeStruct(buf.shape, buf.dtype),
        input_output_aliases={0: 0},
        scratch_shapes=[pltpu.SemaphoreType.DMA((n_copies,)), pltpu.SemaphoreType.DMA((n_copies,))],
    )(buf)


def _add_halves(part, got, cvec, name):
    ns, r, cdim = part.shape
    hr = r // 2
    tr = _row_tile(hr, TR_ELT)
    nblk = hr // tr

    def body(c_ref, a_ref, b_ref, o_ref):
        o_ref[...] = a_ref[...] + b_ref[...]

    grid_spec = pltpu.PrefetchScalarGridSpec(
        num_scalar_prefetch=1, grid=(ns, nblk),
        in_specs=[pl.BlockSpec((None, tr, cdim), lambda s, i, c_ref: (s, c_ref[0] * nblk + i, 0)),
                  pl.BlockSpec((None, tr, cdim), lambda s, i, c_ref: (s, i, 0))],
        out_specs=pl.BlockSpec((None, tr, cdim), lambda s, i, c_ref: (s, i, 0)))
    return pl.pallas_call(
        body, name=name, grid_spec=grid_spec,
        out_shape=jax.ShapeDtypeStruct((ns, hr, cdim), BF16),
        compiler_params=_params(("parallel", "parallel")),
    )(cvec, part, got)


def _scatter_copy(sums_ref, land_ref, k, src_slot, dst_slot, c, send_sems, recv_sems, to):
    return pltpu.make_async_remote_copy(
        src_ref=sums_ref.at[src_slot], dst_ref=land_ref.at[dst_slot, _half_rows(land_ref, c)],
        send_sem=send_sems.at[k], recv_sem=recv_sems.at[k], device_id=to, device_id_type=MESH)


def _scatter_start(sums, name):
    ns, hr, cdim = sums.shape
    land = lax.empty((ns, 2 * hr, cdim), sums.dtype)

    def body(sums_ref, land_ref, send_sems, recv_sems, sums_thru, land_thru):
        x, y, c, chips = _mesh_pos()
        me = _chip_id(x, y)
        for k, chip in enumerate(chips):
            _scatter_copy(sums_ref, land_ref, k, _chip_id(*chip), me, c, send_sems, recv_sems, (*chip, c)).start()

    return pl.pallas_call(
        body, name=name,
        in_specs=[_HBM, _HBM], out_specs=[_SEM, _SEM, _HBM, _HBM],
        out_shape=[pltpu.SemaphoreType.DMA((3,)), pltpu.SemaphoreType.DMA((3,)),
                   pltpu.HBM(sums.shape, sums.dtype), pltpu.HBM(land.shape, land.dtype)],
        input_output_aliases={0: 2, 1: 3},
        compiler_params=pltpu.CompilerParams(has_side_effects=_EFFECT),
    )(pltpu.with_memory_space_constraint(sums, pltpu.HBM), pltpu.with_memory_space_constraint(land, pltpu.HBM))


def _scatter_wait(send_sems, recv_sems, sums, land, after, name):
    def body(sums_ref, land_ref, send_ref, recv_ref, after_ref, sums_out, land_out):
        x, y, c, chips = _mesh_pos()
        me = _chip_id(x, y)
        for k, chip in enumerate(chips):
            _scatter_copy(sums_ref, land_ref, k, _chip_id(*chip), me, c, send_ref, recv_ref, (*chip, c)).wait_send()
        for k, chip in enumerate(chips):
            _scatter_copy(sums_ref, land_ref, k, me, _chip_id(*chip), c, send_ref, recv_ref, (*chip, c)).wait_recv()

    return pl.pallas_call(
        body, name=name,
        in_specs=[_HBM, _HBM, _SEM, _SEM, _ANY], out_specs=[_HBM, _HBM],
        out_shape=[pltpu.HBM(sums.shape, sums.dtype), pltpu.HBM(land.shape, land.dtype)],
        input_output_aliases={0: 0, 1: 1},
        compiler_params=pltpu.CompilerParams(has_side_effects=_EFFECT),
    )(sums, land, send_sems, recv_sems, after)


def _complete_plan(n_weights):
    def plan(refs, send_sems, recv_sems):
        x, y, c, chips = _mesh_pos()
        me = _chip_id(x, y)
        sibling = (x, y, 1 - c)
        sends, recvs = [], []
        for w in range(n_weights):
            sums, land = refs[2 * w], refs[2 * w + 1]
            sends.append(_scatter_copy(sums, land, 4 * w + 3, me, me, c, send_sems, recv_sems, sibling))
            recvs.append(_scatter_copy(sums, land, 4 * w + 3, me, me, 1 - c, send_sems, recv_sems, sibling))
            for k, chip in enumerate(chips):
                slot = _chip_id(*chip)
                sends.append(_chip_copy(land, 4 * w + k, slot, _half_rows(land, c), send_sems, recv_sems, sibling))
                recvs.append(_chip_copy(land, 4 * w + k, slot, _half_rows(land, 1 - c), send_sems, recv_sems, sibling))
        return sends, recvs
    return plan


SMALL_ROWS = 8


N_DEVICES = 8


def _small_pack(gl2g, gl2b, gl1g, gl1b, g_ac, gcw, gsink, loss):
    d = gl2g.shape[1]
    hd = d // 2
    nq = gsink.shape[1]

    def body(a_ref, b_ref, c_ref, d_ref, e_ref, cw_ref, sk_ref, ls_ref, out_ref, mine):
        x, y, c, _ = _mesh_pos()
        me = 4 * x + 2 * y + c
        mine[...] = jnp.zeros_like(mine)
        mine[0:1, :] = a_ref[...]
        mine[1:2, :] = b_ref[...]
        mine[2:3, :] = c_ref[...]
        mine[3:4, :] = d_ref[...]
        mine[4:5, :] = e_ref[...]
        mine[5:6, 0:hd] = cw_ref[0:1, :]
        mine[5:6, hd:d] = cw_ref[1:2, :]
        mine[6:7, 0:hd] = cw_ref[2:3, :]
        mine[6:7, hd:hd + nq] = sk_ref[...]
        mine[6:7, hd + 128:hd + 256] = ls_ref[...]
        out_ref[...] = jnp.zeros_like(out_ref)
        out_ref[pl.ds(me, 1)] = mine[...][None]

    return pl.pallas_call(
        body, name="small_pack",
        in_specs=[_VMEM] * 8, out_specs=_VMEM,
        out_shape=jax.ShapeDtypeStruct((N_DEVICES, SMALL_ROWS, d), F32),
        scratch_shapes=[pltpu.VMEM((SMALL_ROWS, d), F32)],
    )(gl2g, gl2b, gl1g, gl1b, g_ac, gcw, gsink, loss)


def _small_plan():
    def plan(refs, send_sems, recv_sems):
        x, y, c, _ = _mesh_pos()
        me = 4 * x + 2 * y + c
        (gath,) = refs
        sends, recvs = [], []
        for r in range(1, N_DEVICES):
            peer = ((1 - x) if r & 4 else x, (1 - y) if r & 2 else y, (1 - c) if r & 1 else c)
            peer_id = 4 * peer[0] + 2 * peer[1] + peer[2]
            for slot, into in ((me, sends), (peer_id, recvs)):
                into.append(pltpu.make_async_remote_copy(
                    src_ref=gath.at[slot], dst_ref=gath.at[slot], send_sem=send_sems.at[r - 1],
                    recv_sem=recv_sems.at[r - 1], device_id=peer, device_id_type=MESH))
        return sends, recvs
    return plan


def _small_sum(gath):
    def body(gath_ref, out_ref):
        total = gath_ref[0]
        for dev in range(1, N_DEVICES):
            total = total + gath_ref[dev]
        out_ref[...] = total

    return pl.pallas_call(
        body, name="small_sum", in_specs=[_VMEM], out_specs=_VMEM,
        out_shape=jax.ShapeDtypeStruct(gath.shape[1:], F32),
    )(gath)


def _adamw(w, g, m, v):
    m = ADAM_B1 * m + (1.0 - ADAM_B1) * g
    v = ADAM_B2 * v + (1.0 - ADAM_B2) * (g * g)
    m_hat = m / (1.0 - ADAM_B1 ** ADAM_STEP)
    v_hat = v / (1.0 - ADAM_B2 ** ADAM_STEP)
    delta = -ADAM_LR * (m_hat / (jnp.sqrt(v_hat) + ADAM_EPS) + ADAM_WD * w)
    return delta, m, v


def _adamw_shard(w, m, v, land, own, pos_vec, name, col_block=0):
    tr = _row_tile(w.shape[1] // 2, TR_ELT)
    grid = (w.shape[1] // tr,)
    body, in_specs, out_specs, out_shape = _adamw_passenger(w.shape, tr, grid, col_block)
    grid_spec = pltpu.PrefetchScalarGridSpec(num_scalar_prefetch=1, grid=grid, in_specs=in_specs, out_specs=out_specs)
    return pl.pallas_call(
        body, name=name, grid_spec=grid_spec, out_shape=out_shape,
        compiler_params=_params(("parallel",)),
    )(pos_vec, w, m, v, land, land, land, land, own)


def _adamw_passenger(shape, tr, grid, col_block):
    _, r, c = shape
    nh = r // 2 // tr
    n_blocks = 2 * nh
    n_steps = int(np.prod(grid))
    assert nh * tr * 2 == r and n_blocks <= n_steps

    def step_of(ids):
        step = ids[0]
        for n, i in zip(grid[1:], ids[1:]):
            step = step * n + i
        return step

    def block_of(ids):
        return jnp.minimum(step_of(ids), n_blocks - 1)

    def update(pos_ref, w_ref, m_ref, v_ref, l0, l1, l2, l3, own_ref, g_out, d_out, m_out, v_out):
        i = block_of([pl.program_id(a) for a in range(len(grid))])
        mine = (i // nh) == pos_ref[1]
        own_blk = own_ref[...].astype(F32)
        g = None
        for s, l_ref in enumerate([l0, l1, l2, l3]):
            term = jnp.where(mine & (pos_ref[0] == s), own_blk, l_ref[...].astype(F32))
            g = term if g is None else g + term
        delta, nm, nv = _adamw(w_ref[...], g, m_ref[...], v_ref[...])
        g_out[...] = g
        d_out[...] = delta
        m_out[...] = nm
        v_out[...] = nv

    def body(*refs):
        if n_blocks == n_steps:
            update(*refs)
        else:
            pl.when(step_of([pl.program_id(a) for a in range(len(grid))]) < n_blocks)(lambda: update(*refs))

    def land_spec(s):
        def index(*args):
            i, pos_ref = block_of(args[:-1]), args[-1]
            skip = (pos_ref[0] == s) & ((i // nh) == pos_ref[1])
            return (s, jnp.where(skip, (i + nh) % n_blocks, i), col_block)
        return pl.BlockSpec((None, tr, c), index)

    blk = pl.BlockSpec((None, tr, c), lambda *args: (0, block_of(args[:-1]), 0))
    in_specs = ([blk, blk, blk] + [land_spec(s) for s in range(N_CHIPS)]
                + [pl.BlockSpec((None, tr, c), lambda *args: (args[-1][0], block_of(args[:-1]) % nh, col_block))])
    return body, in_specs, [blk] * 4, [jax.ShapeDtypeStruct((1, r, c), F32)] * 4


def _call_with_adamw(body, name, grid, in_specs, out_specs, out_shape, scratch_shapes, semantics, operands, shard):
    if shard is None:
        return pl.pallas_call(
            body, name=name, grid=grid, in_specs=in_specs, out_specs=out_specs, out_shape=out_shape,
            scratch_shapes=scratch_shapes, compiler_params=_params(semantics))(*operands)
    w, m, v, land, own, pos_vec, col_block = shard
    n_steps = int(np.prod(grid))
    hr = w.shape[1] // 2
    tr = min(t for t in range(16, hr + 1, 16) if hr % t == 0 and 2 * (hr // t) <= n_steps)
    adam_body, adam_in, adam_out, adam_shape = _adamw_passenger(w.shape, tr, grid, col_block)
    n_in, n_out = len(in_specs), len(out_specs)

    def with_pos(spec):
        if spec.index_map is None:
            return spec
        return pl.BlockSpec(spec.block_shape, lambda *args: spec.index_map(*args[:-1]))

    def both(pos_ref, *refs):
        ins, adam_ins = refs[:n_in], refs[n_in:n_in + len(adam_in)]
        refs = refs[n_in + len(adam_in):]
        outs, adam_outs, scratch = refs[:n_out], refs[n_out:n_out + len(adam_out)], refs[n_out + len(adam_out):]
        body(*ins, *outs, *scratch)
        adam_body(pos_ref, *adam_ins, *adam_outs)

    grid_spec = pltpu.PrefetchScalarGridSpec(
        num_scalar_prefetch=1, grid=grid, in_specs=[with_pos(sp) for sp in in_specs] + adam_in,
        out_specs=[with_pos(sp) for sp in out_specs] + adam_out, scratch_shapes=scratch_shapes)
    return pl.pallas_call(
        both, name=name, grid_spec=grid_spec, out_shape=list(out_shape) + adam_shape,
        compiler_params=_params(semantics),
    )(pos_vec, *operands, w, m, v, land, land, land, land, own)


def _adamw_small(red, params):
    names = ["sinks", "g_attn", "g_conv", "ln1_g", "ln1_b", "ln2_g", "ln2_b", "conv_w"]
    d = red.shape[1]
    hd = d // 2
    flat = []
    for nme in names:
        flat.extend(params[nme])
    nq = params["sinks"][0].shape[1]
    cs = params["conv_w"][0].shape[2]

    def body(*refs):
        red_ref = refs[0]
        ins = refs[1:1 + 3 * len(names)]
        outs = refs[1 + 3 * len(names):]
        x, y, _, _ = _mesh_pos()
        me = _chip_id(x, y)

        def conv_tap(row, base):
            picked = red_ref[row:row + 1, base:base + cs]
            for s in range(1, N_CHIPS):
                picked = jnp.where(me == s, red_ref[row:row + 1, base + s * cs:base + (s + 1) * cs], picked)
            return picked

        grads = {
            "sinks": red_ref[6:7, hd:hd + nq],
            "g_attn": red_ref[4:5, 0:hd],
            "g_conv": red_ref[4:5, hd:d],
            "ln1_g": red_ref[2:3, :],
            "ln1_b": red_ref[3:4, :],
            "ln2_g": red_ref[0:1, :],
            "ln2_b": red_ref[1:2, :],
        }
        for i, nme in enumerate(names):
            w_ref, m_ref, v_ref = ins[3 * i:3 * i + 3]
            g_out, d_out, m_out, v_out = outs[4 * i:4 * i + 4]
            if nme == "conv_w":
                for tap, (row, base) in enumerate([(5, 0), (5, hd), (6, 0)]):
                    g = conv_tap(row, base)
                    delta, nm, nv = _adamw(w_ref[0, tap:tap + 1, :], g, m_ref[0, tap:tap + 1, :], v_ref[0, tap:tap + 1, :])
                    g_out[0, tap:tap + 1, :] = g
                    d_out[0, tap:tap + 1, :] = delta
                    m_out[0, tap:tap + 1, :] = nm
                    v_out[0, tap:tap + 1, :] = nv
            else:
                g = grads[nme]
                delta, nm, nv = _adamw(w_ref[...], g, m_ref[...], v_ref[...])
                g_out[...] = g
                d_out[...] = delta
                m_out[...] = nm
                v_out[...] = nv

    out_shape = []
    for nme in names:
        out_shape.extend([jax.ShapeDtypeStruct(params[nme][0].shape, F32)] * 4)
    outs = pl.pallas_call(
        body, name="adamw_small",
        in_specs=[_VMEM] * (1 + len(flat)), out_specs=[_VMEM] * len(out_shape),
        out_shape=out_shape,
    )(red, *flat)
    return {nme: tuple(outs[4 * i:4 * i + 4]) for i, nme in enumerate(names)}


def _rope_tables(pos_col):
    s = pos_col.shape[0]
    w = N_KV_HEADS * HEAD_DIM
    tb = min(512, s)
    inv_freq = (ROPE_THETA ** (-np.arange(0, ROT_DIM, 2, dtype=np.float32) / ROT_DIM)).astype(np.float32)

    def body(pos_ref, cos_ref, sin_ref):
        pos = pos_ref[...].astype(F32)
        lane = lax.broadcasted_iota(jnp.int32, (tb, PAIR), 1) & (HEAD_DIM - 1)
        fidx = lane & (ROT_DIM // 2 - 1)
        inv = jnp.zeros((tb, PAIR), F32)
        for k in range(ROT_DIM // 2):
            inv = jnp.where(fidx == k, float(inv_freq[k]), inv)
        ang = pos * inv
        rot = lane < ROT_DIM
        sin_v = jnp.sin(ang)
        cos_ref[...] = _tile_lanes(jnp.where(rot, jnp.cos(ang), 1.0), w // PAIR)
        sin_ref[...] = _tile_lanes(jnp.where(lane < ROT_DIM // 2, -sin_v, jnp.where(rot, sin_v, 0.0)), w // PAIR)

    return pl.pallas_call(
        body, name="rope_tables", grid=(s // tb,),
        in_specs=[pl.BlockSpec((tb, 1), lambda i: (i, 0))],
        out_specs=[pl.BlockSpec((tb, w), lambda i: (i, 0))] * 2,
        out_shape=[jax.ShapeDtypeStruct((s, w), F32)] * 2,
        compiler_params=_params(("parallel",)),
    )(pos_col)


def _in_proj(x, w_in_g, first_vec, n_shards, into, name):
    _, s, d = x.shape
    ns, _, ncol = w_in_g.shape
    tm = min(2 * TM, s)
    first_call = into is None
    assert n_shards == 1 or not first_call

    def body(first_ref, x_ref, w_ref, into_ref, o_ref, *x16_ref):
        xb = x_ref[...].astype(BF16)
        o_ref[...] = _dot(xb, w_ref[...]).astype(BF16)
        for ref in x16_ref:
            ref[...] = xb

    shard = lambda j, first_ref: lax.rem(first_ref[0] + j, ns)
    x_spec = pl.BlockSpec((None, tm, d), lambda i, j, first_ref: (0, i, 0))
    grid_spec = pltpu.PrefetchScalarGridSpec(
        num_scalar_prefetch=1, grid=(s // tm, n_shards),
        in_specs=[x_spec, pl.BlockSpec((None, d, ncol), lambda i, j, first_ref: (shard(j, first_ref), 0, 0)), _ANY],
        out_specs=[pl.BlockSpec((tm, ncol), lambda i, j, first_ref: (i, shard(j, first_ref)))] + [x_spec] * first_call)
    return pl.pallas_call(
        body, name=name, grid_spec=grid_spec,
        out_shape=[jax.ShapeDtypeStruct((s, ns * ncol), BF16)] + [jax.ShapeDtypeStruct((1, s, d), BF16)] * first_call,
        input_output_aliases={} if first_call else {3: 0},
        compiler_params=_params(("parallel", "arbitrary")),
    )(first_vec, x, w_in_g, first_vec if first_call else into)


PAIR = 2 * HEAD_DIM
KEYS = 2 * WINDOW


def _pair_operand(t_all, h):
    col = (h // 2) * PAIR
    lane = lax.broadcasted_iota(jnp.int32, (KEYS, PAIR), 1)
    own_low = h % 2 == 0
    mine = jnp.where((lane < HEAD_DIM) if own_low else (lane >= HEAD_DIM), t_all[:, col:col + PAIR], 0.0)
    other = pltpu.roll(mine, HEAD_DIM, 1)
    low, high = (mine, other) if own_low else (other, mine)
    return jnp.concatenate([low, high], axis=0).astype(BF16)


def _pair_grad(acc, h):
    lane = lax.broadcasted_iota(jnp.int32, (KEYS, PAIR), 1)
    low = jnp.where(lane < HEAD_DIM, acc[:KEYS], 0.0)
    high = jnp.where(lane >= HEAD_DIM, acc[KEYS:], 0.0)
    if h % 2 == 0:
        return low + pltpu.roll(high, HEAD_DIM, 1)
    return high + pltpu.roll(low, HEAD_DIM, 1)


N_PAIRS = N_KV_HEADS * GROUP // 2


def _all_probs(q, kk2s, first, sinks_ref):
    assert ATTN_SCALE == 0.125
    q = q * ATTN_SCALE
    qps, scores = [], []
    for pair in range(N_PAIRS):
        qp = q[:, pair * PAIR:(pair + 1) * PAIR].astype(BF16)
        qps.append(qp)
        scores.append(_dot_nt(qp, kk2s[pair // (GROUP // 2)]))
    qi = lax.broadcasted_iota(jnp.int32, (WINDOW, 2 * KEYS), 0)
    kj = lax.broadcasted_iota(jnp.int32, (WINDOW, 2 * KEYS), 1) & (KEYS - 1)
    rel = qi + WINDOW - kj
    valid = (rel >= 0) & (rel < WINDOW) & jnp.logical_not(first & (kj < WINDOW))
    bias = jnp.where(valid, 0.0, NEG_BIG)
    s = (jnp.stack(scores, axis=0) + bias[None]).reshape(N_PAIRS * WINDOW, 2 * KEYS)
    probs, p_sinks = [], []
    for t in range(2):
        st = s[:, t * KEYS:(t + 1) * KEYS]
        sink = jnp.concatenate([jnp.broadcast_to(sinks_ref[0:1, 2 * pair + t:2 * pair + t + 1], (WINDOW, 1))
                                for pair in range(N_PAIRS)], axis=0)
        m = jnp.maximum(jnp.max(st, axis=1, keepdims=True), sink)
        e = jnp.exp(st - m)
        e_sink = jnp.exp(sink - m)
        inv_l = 1.0 / (jnp.sum(e, axis=1, keepdims=True) + e_sink)
        probs.append(e * inv_l)
        p_sinks.append(e_sink * inv_l)
    return qps, jnp.concatenate(probs, axis=1), p_sinks


def _roped_qkv(cur_ref, prev_ref, cos_ref, sin_ref, cosp_ref, sinp_ref, qw, kvw):
    cur = cur_ref[...].astype(F32)
    cos, sin = cos_ref[...], sin_ref[...]
    cos_q, sin_q = _tile_lanes(cos, GROUP), _tile_lanes(sin, GROUP)
    q = _rope(cur[:, :qw], cos_q, sin_q, 1.0)
    prev = prev_ref[...].astype(F32)
    k_all = jnp.concatenate([_rope(prev[:, :kvw], cosp_ref[...], sinp_ref[...], 1.0),
                             _rope(cur[:, qw:qw + kvw], cos, sin, 1.0)], axis=0)
    v_all = jnp.concatenate([prev[:, kvw:], cur[:, qw + kvw:]], axis=0)
    return q, k_all, v_all, cos_q, sin_q


def _attention_fwd(proj, cos_t, sin_t, sinks):
    s = proj.shape[0]
    qw = GROUP * N_KV_HEADS * HEAD_DIM
    kvw = N_KV_HEADS * HEAD_DIM
    nb = s // WINDOW

    def body(cur_ref, prev_ref, cos_ref, sin_ref, cosp_ref, sinp_ref, sinks_ref, o_ref):
        first = pl.program_id(0) == 0
        q, k_all, v_all, _, _ = _roped_qkv(cur_ref, prev_ref, cos_ref, sin_ref, cosp_ref, sinp_ref, qw, kvw)
        kk2s = [_pair_operand(k_all, h) for h in range(N_KV_HEADS)]
        vv2s = [_pair_operand(v_all, h) for h in range(N_KV_HEADS)]
        _, probs, _ = _all_probs(q, kk2s, first, sinks_ref)
        probs = probs.astype(BF16)
        outs = [_dot(probs[pair * WINDOW:(pair + 1) * WINDOW], vv2s[pair // (GROUP // 2)]) for pair in range(N_PAIRS)]
        o_ref[...] = jnp.concatenate(outs, axis=1)

    tbl = pl.BlockSpec((WINDOW, kvw), lambda n: (n, 0))
    tbl_prev = pl.BlockSpec((WINDOW, kvw), lambda n: (jnp.maximum(n - 1, 0), 0))
    return pl.pallas_call(
        body, name="attention_fwd", grid=(nb,),
        in_specs=[pl.BlockSpec((WINDOW, qw + 2 * kvw), lambda n: (n, 0)),
                  pl.BlockSpec((WINDOW, 2 * kvw), lambda n: (jnp.maximum(n - 1, 0), (qw // (2 * kvw)))),
                  tbl, tbl, tbl_prev, tbl_prev, _VMEM],
        out_specs=pl.BlockSpec((WINDOW, qw), lambda n: (n, 0)),
        out_shape=jax.ShapeDtypeStruct((s, qw), F32),
        compiler_params=_params(("parallel",)),
    )(proj, proj, cos_t, sin_t, cos_t, sin_t, sinks)


def _conv_taps(cw_ref):
    return [jnp.concatenate([cw_ref[s, k:k + 1, :] for s in range(N_CHIPS)], axis=1) for k in range(3)]


def _shift_down(z, halo, steps):
    last = halo.shape[0]
    row = lax.broadcasted_iota(jnp.int32, z.shape, 0)
    out = pltpu.roll(z, steps, 0)
    for r in range(steps):
        out = jnp.where(row == r, halo[last - steps + r:last - steps + r + 1, :], out)
    return out


def _shift_up(z, halo, steps):
    rows = z.shape[0]
    row = lax.broadcasted_iota(jnp.int32, z.shape, 0)
    out = pltpu.roll(z, rows - steps, 0)
    for r in range(steps):
        out = jnp.where(row == rows - steps + r, halo[r:r + 1, :], out)
    return out


def _split_cbu(lo, hi, cw):
    lo, hi = lo.astype(F32), hi.astype(F32)
    c_gate = lo[:, :cw]
    b_gate = jnp.concatenate([lo[:, cw:], hi[:, :2 * cw - lo.shape[1]]], axis=1)
    u = hi[:, 2 * cw - lo.shape[1]:]
    return c_gate, b_gate, u


def _conv_norm(proj, attn, cw_full, g_ac):
    s, in_w = proj.shape
    cw = attn.shape[1]
    blk_w = in_w // 3
    tb = min(TB_CONV, s)

    def body(lo_ref, hi_ref, lo_h_ref, hi_h_ref, attn_ref, cw_ref, g_ref, mixed_ref, ac_ref, rstd_ref):
        i = pl.program_id(0)
        c_gate, b_gate, u = _split_cbu(lo_ref[...], hi_ref[...], cw)
        c_h, _, u_h = _split_cbu(lo_h_ref[...], hi_h_ref[...], cw)
        z = c_gate * u
        z_h = jnp.where(i == 0, 0.0, c_h * u_h)
        w0, w1, w2 = _conv_taps(cw_ref)
        y = w0 * _shift_down(z, z_h, 2) + w1 * _shift_down(z, z_h, 1) + w2 * z
        conv = b_gate * y
        a = attn_ref[...]
        r_a = lax.rsqrt(jnp.mean(a * a, axis=-1, keepdims=True) + RMS_EPS)
        r_c = lax.rsqrt(jnp.mean(conv * conv, axis=-1, keepdims=True) + RMS_EPS)
        g = g_ref[...]
        mixed_ref[...] = jnp.concatenate([a * r_a * g[:, :cw], conv * r_c * g[:, cw:]], axis=1).astype(BF16)
        ac_ref[...] = jnp.concatenate([a, conv], axis=1)
        rstd_ref[0] = r_a
        rstd_ref[1] = r_c

    halo_idx = lambda i: jnp.maximum(i * (tb // HALO_ROWS) - 1, 0)
    return pl.pallas_call(
        body, name="conv_norm", grid=(s // tb,),
        in_specs=[pl.BlockSpec((tb, blk_w), lambda i: (i, 1)),
                  pl.BlockSpec((tb, blk_w), lambda i: (i, 2)),
                  pl.BlockSpec((HALO_ROWS, blk_w), lambda i: (halo_idx(i), 1)),
                  pl.BlockSpec((HALO_ROWS, blk_w), lambda i: (halo_idx(i), 2)),
                  pl.BlockSpec((tb, cw), lambda i: (i, 0)),
                  _VMEM, _VMEM],
        out_specs=[pl.BlockSpec((tb, 2 * cw), lambda i: (i, 0)),
                   pl.BlockSpec((tb, 2 * cw), lambda i: (i, 0)),
                   pl.BlockSpec((2, tb, 1), lambda i: (0, i, 0))],
        out_shape=[jax.ShapeDtypeStruct((s, 2 * cw), BF16), jax.ShapeDtypeStruct((s, 2 * cw), F32),
                   jax.ShapeDtypeStruct((2, s, 1), F32)],
        compiler_params=_params(("parallel",)),
    )(proj, proj, proj, proj, attn, cw_full, g_ac)


def _out_proj_ln(mixed, w_out_g, x, ln_g, ln_b):
    s, d = mixed.shape
    tm = min(TM, s)
    tk = d
    nk = d // tk

    def body(a_ref, w_ref, x_ref, g_ref, b_ref, xhat_ref, h_ref, rstd_ref, acc):
        k = pl.program_id(1)
        _accumulate(acc, lambda: _dot(a_ref[...], w_ref[...]), k, nk)

        @pl.when(k == nk - 1)
        def _():
            def rows_fn(rows):
                xhat, rstd = _ln_fwd(ALPHA * x_ref[rows, :] + acc[rows, :])
                xhat_ref[rows, :] = xhat
                h_ref[rows, :] = (xhat * g_ref[...] + b_ref[...]).astype(BF16)
                rstd_ref[rows, :] = rstd

            _for_row_chunks(tm, rows_fn)

    row = pl.BlockSpec((tm, d), lambda i, k: (i, 0))
    return pl.pallas_call(
        body, name="out_proj_ln", grid=(s // tm, nk),
        in_specs=[pl.BlockSpec((tm, tk), lambda i, k: (i, k)),
                  pl.BlockSpec((tk, d), lambda i, k: (k, 0)),
                  pl.BlockSpec((None, tm, d), lambda i, k: (0, i, 0)),
                  _VMEM, _VMEM],
        out_specs=[row, row, pl.BlockSpec((tm, 1), lambda i, k: (i, 0))],
        out_shape=[jax.ShapeDtypeStruct((s, d), F32), jax.ShapeDtypeStruct((s, d), BF16),
                   jax.ShapeDtypeStruct((s, 1), F32)],
        scratch_shapes=[pltpu.VMEM((tm, d), F32)],
        compiler_params=_params(("parallel", "arbitrary")),
    )(mixed, w_out_g, x, ln_g, ln_b)


def _gate_up(h1, w_gu_g, first_vec, n_shards, into, name):
    s, d = h1.shape
    ns, _, fs2 = w_gu_g.shape
    fs = fs2 // 2
    tm = min(TM, s)

    def body(first_ref, h_ref, w_ref, act_in, ab_in, act_ref, ab_ref):
        gu = _dot(h_ref[...], w_ref[...])
        g, u = gu[:, :fs], gu[:, fs:]
        sg = _sigmoid(g)
        silu = g * sg
        act_ref[...] = (silu * u).astype(BF16)
        ab_ref[:, :fs] = (u * (sg * (1.0 + g * (1.0 - sg)))).astype(BF16)
        ab_ref[:, fs:] = silu.astype(BF16)

    shard = lambda j, first_ref: lax.rem(first_ref[0] + j, ns)
    grid_spec = pltpu.PrefetchScalarGridSpec(
        num_scalar_prefetch=1, grid=(s // tm, n_shards),
        in_specs=[pl.BlockSpec((tm, d), lambda i, j, first_ref: (i, 0)),
                  pl.BlockSpec((None, d, fs2), lambda i, j, first_ref: (shard(j, first_ref), 0, 0)), _ANY, _ANY],
        out_specs=[pl.BlockSpec((tm, fs), lambda i, j, first_ref: (i, shard(j, first_ref))),
                   pl.BlockSpec((tm, fs2), lambda i, j, first_ref: (i, shard(j, first_ref)))])
    return pl.pallas_call(
        body, name=name, grid_spec=grid_spec,
        out_shape=[jax.ShapeDtypeStruct((s, ns * fs), BF16), jax.ShapeDtypeStruct((s, ns * fs2), BF16)],
        input_output_aliases={} if into is None else {3: 0, 4: 1},
        compiler_params=_params(("parallel", "arbitrary")),
    )(first_vec, h1, w_gu_g, *((first_vec, first_vec) if into is None else into))


def _down_ln_loss(act, w_down_g, xhat1, ln1_g, ln1_b, ln2_g, ln2_b, target):
    s, f = act.shape
    d = xhat1.shape[1]
    tm = min(TM, s)
    tk = f // N_CHIPS
    nk = f // tk

    def body(a_ref, w_ref, xh_ref, g1_ref, b1_ref, g2_ref, b2_ref, t_ref, dpre_ref, dpre16_ref, loss_ref, gg_ref, gb_ref,
             acc):
        i, k = pl.program_id(0), pl.program_id(1)
        _accumulate(acc, lambda: _dot(a_ref[...], w_ref[...]), k, nk)

        @pl.when(k == nk - 1)
        def _():
            @pl.when(i == 0)
            def _():
                loss_ref[...] = jnp.zeros_like(loss_ref)
                gg_ref[...] = jnp.zeros_like(gg_ref)
                gb_ref[...] = jnp.zeros_like(gb_ref)

            def rows_fn(rows):
                h1 = xh_ref[rows, :] * g1_ref[...] + b1_ref[...]
                xhat, rstd = _ln_fwd(ALPHA * h1 + acc[rows, :])
                g2 = g2_ref[...]
                diff = xhat * g2 + b2_ref[...] - t_ref[rows, :]
                dy = diff * (1.0 / d)
                dpre = _ln_bwd(dy, xhat, rstd, g2)
                dpre_ref[rows, :] = dpre
                dpre16_ref[rows, :] = dpre.astype(BF16)
                sq = jnp.sum(jnp.sum(diff * diff, axis=1, keepdims=True), axis=0, keepdims=True)
                loss_ref[...] += jnp.broadcast_to(sq * (0.5 / d), (1, 128))
                gg_ref[...] += jnp.sum(dy * xhat, axis=0, keepdims=True)
                gb_ref[...] += jnp.sum(dy, axis=0, keepdims=True)

            _for_row_chunks(tm, rows_fn)

    row = pl.BlockSpec((tm, d), lambda i, k: (i, 0))
    vec = pl.BlockSpec((1, d), lambda i, k: (0, 0))
    return pl.pallas_call(
        body, name="down_ln_loss", grid=(s // tm, nk),
        in_specs=[pl.BlockSpec((tm, tk), lambda i, k: (i, k)),
                  pl.BlockSpec((tk, d), lambda i, k: (k, 0)),
                  row, _VMEM, _VMEM, _VMEM, _VMEM,
                  pl.BlockSpec((None, tm, d), lambda i, k: (0, i, 0))],
        out_specs=[row, row, pl.BlockSpec((1, 128), lambda i, k: (0, 0)), vec, vec],
        out_shape=[jax.ShapeDtypeStruct((s, d), F32), jax.ShapeDtypeStruct((s, d), BF16),
                   jax.ShapeDtypeStruct((1, 128), F32), jax.ShapeDtypeStruct((1, d), F32),
                   jax.ShapeDtypeStruct((1, d), F32)],
        scratch_shapes=[pltpu.VMEM((tm, d), F32)],
        compiler_params=_params(("arbitrary", "arbitrary")),
    )(act, w_down_g, xhat1, ln1_g, ln1_b, ln2_g, ln2_b, target)


def _dact_silu_bwd(dpre2, w_down_g, ab):
    s, d = dpre2.shape
    fs2 = ab.shape[1] // N_CHIPS
    fs = fs2 // 2
    tm = min(TM, s)

    def body(dp_ref, w_ref, ab_ref, dgu_ref):
        d_act = _dot_nt(dp_ref[...], w_ref[...])
        dgu_ref[:, :fs] = (d_act * ab_ref[:, :fs].astype(F32)).astype(BF16)
        dgu_ref[:, fs:] = (d_act * ab_ref[:, fs:].astype(F32)).astype(BF16)

    blk = pl.BlockSpec((tm, fs2), lambda j, i: (i, j))
    return pl.pallas_call(
        body, name="dact_silu_bwd", grid=(N_CHIPS, s // tm),
        in_specs=[pl.BlockSpec((tm, d), lambda j, i: (i, 0)),
                  pl.BlockSpec((fs, d), lambda j, i: (j, 0)), blk],
        out_specs=blk,
        out_shape=jax.ShapeDtypeStruct(ab.shape, BF16),
        compiler_params=_params(("parallel", "parallel")),
    )(dpre2, w_down_g, ab)


def _grad_rows(a, b, after, name, row_blocks=1):
    s, m = a.shape
    n = b.shape[1]
    ms = m // N_CHIPS
    tmw = ms // row_blocks
    tk = min(TK_TOK, s)
    nk = s // tk

    def body(a_ref, b_ref, after_ref, o_ref, acc):
        k = pl.program_id(2)
        _accumulate(acc, lambda: _dot_tn(a_ref[...].astype(BF16), b_ref[...].astype(BF16)), k, nk)

        @pl.when(k == nk - 1)
        def _():
            o_ref[...] = acc[...].astype(BF16)

    return pl.pallas_call(
        body, name=name, grid=(N_CHIPS, row_blocks, nk),
        in_specs=[pl.BlockSpec((tk, tmw), lambda j, r, k: (k, j * row_blocks + r)),
                  pl.BlockSpec((tk, n), lambda j, r, k: (k, 0)), _ANY],
        out_specs=pl.BlockSpec((None, tmw, n), lambda j, r, k: (j, r, 0)),
        out_shape=jax.ShapeDtypeStruct((N_CHIPS, ms, n), BF16),
        scratch_shapes=[pltpu.VMEM((tmw, n), F32)],
        compiler_params=_params(("parallel", "parallel", "arbitrary")),
    )(a, b, after)


def _grad_cols(a, bs, after, name, a_3d=False, row_blocks=2, shard=None):
    s, m = a.shape[-2:]
    n = bs[0].shape[1]
    ns = n // N_CHIPS
    nb = len(bs)
    tmw = m // row_blocks
    tk = min(TK_TOK, s)
    nk = s // tk

    def body(*refs):
        a_ref, b_refs, o_refs, accs = refs[0], refs[1:1 + nb], refs[2 + nb:2 + 2 * nb], refs[2 + 2 * nb:]
        k = pl.program_id(2)
        for b_ref, acc in zip(b_refs, accs):
            _accumulate(acc, lambda b_ref=b_ref: _dot_tn(a_ref[...].astype(BF16), b_ref[...].astype(BF16)), k, nk)

        @pl.when(k == nk - 1)
        def _():
            for o_ref, acc in zip(o_refs, accs):
                o_ref[...] = acc[...].astype(BF16)

    if a_3d:
        a_spec = pl.BlockSpec((None, tk, tmw), lambda j, r, k: (0, k, r))
    else:
        a_spec = pl.BlockSpec((tk, tmw), lambda j, r, k: (k, r))
    return _call_with_adamw(
        body, name, (N_CHIPS, row_blocks, nk),
        [a_spec] + [pl.BlockSpec((tk, ns), lambda j, r, k: (k, j))] * nb + [_ANY],
        [pl.BlockSpec((None, tmw, ns), lambda j, r, k: (j, r, 0))] * nb,
        [jax.ShapeDtypeStruct((N_CHIPS, m, ns), BF16)] * nb,
        [pltpu.VMEM((tmw, ns), F32)] * nb, ("parallel", "parallel", "arbitrary"), (a, *bs, after), shard)


def _dh1_ln_bwd(d_gu, w_gu_g, dpre2, xhat1, rstd1, ln1_g, after):
    s = d_gu.shape[0]
    d = dpre2.shape[1]
    hd = d // 2
    fs = w_gu_g.shape[2]
    tm = min(TM, s)

    def body(dgu_ref, w_ref, dp2_ref, xh_ref, rs_ref, g_ref, after_ref, dpre_ref, gg_ref, gb_ref, acc_lo, acc_hi):
        i, j, half = pl.program_id(0), pl.program_id(1), pl.program_id(2)

        def product():
            return _dot_nt(dgu_ref[...], w_ref[...])

        @pl.when(half == 0)
        def _():
            _accumulate(acc_lo, product, j, N_CHIPS)

        @pl.when(half == 1)
        def _():
            _accumulate(acc_hi, product, j, N_CHIPS)

        @pl.when((j == N_CHIPS - 1) & (half == 1))
        def _():
            @pl.when(i == 0)
            def _():
                gg_ref[...] = jnp.zeros_like(gg_ref)
                gb_ref[...] = jnp.zeros_like(gb_ref)

            def rows_fn(rows):
                dh = jnp.concatenate([acc_lo[rows, :], acc_hi[rows, :]], axis=1) + ALPHA * dp2_ref[rows, :]
                xhat = xh_ref[rows, :]
                dpre_ref[rows, :] = _ln_bwd(dh, xhat, rs_ref[rows, :], g_ref[...])
                gg_ref[...] += jnp.sum(dh * xhat, axis=0, keepdims=True)
                gb_ref[...] += jnp.sum(dh, axis=0, keepdims=True)

            _for_row_chunks(tm, rows_fn)

    row = pl.BlockSpec((tm, d), lambda i, j, h: (i, 0))
    vec = pl.BlockSpec((1, d), lambda i, j, h: (0, 0))
    act_blk = pl.BlockSpec((tm, fs), lambda i, j, h: (i, j))
    w_blk = pl.BlockSpec((None, hd, fs), lambda i, j, h: (j, h, 0))
    return pl.pallas_call(
        body, name="dh1_ln_bwd", grid=(s // tm, N_CHIPS, 2),
        in_specs=[act_blk, w_blk, row, row, pl.BlockSpec((tm, 1), lambda i, j, h: (i, 0)), _VMEM, _ANY],
        out_specs=[row, vec, vec],
        out_shape=[jax.ShapeDtypeStruct((s, d), F32), jax.ShapeDtypeStruct((1, d), F32),
                   jax.ShapeDtypeStruct((1, d), F32)],
        scratch_shapes=[pltpu.VMEM((tm, hd), F32)] * 2,
        compiler_params=_params(("arbitrary", "arbitrary", "arbitrary")),
    )(d_gu, w_gu_g, dpre2, xhat1, rstd1, ln1_g, after)


def _dmixed_rms_bwd(dpre1, w_out_g, ac, rstd, g_ac):
    s, d = dpre1.shape
    hd = d // 2
    tm = min(TM, s)

    def body(dp_ref, w_ref, ac_ref, rs_ref, g_ref, dac_ref, gg_ref):
        i = pl.program_id(1)
        dm = _dot_nt(dp_ref[...].astype(BF16), w_ref[...])
        pre = ac_ref[...]
        r = rs_ref[...]
        gdm = dm * g_ref[...]
        dac_ref[...] = r * gdm - pre * (r * r * r) * jnp.mean(gdm * pre, axis=-1, keepdims=True)
        gg = jnp.sum(dm * pre * r, axis=0, keepdims=True)

        @pl.when(i == 0)
        def _():
            gg_ref[...] = gg

        @pl.when(i > 0)
        def _():
            gg_ref[...] += gg

    return pl.pallas_call(
        body, name="dmixed_rms_bwd", grid=(2, s // tm),
        in_specs=[pl.BlockSpec((tm, d), lambda h, i: (i, 0)),
                  pl.BlockSpec((hd, d), lambda h, i: (h, 0)),
                  pl.BlockSpec((tm, hd), lambda h, i: (i, h)),
                  pl.BlockSpec((None, tm, 1), lambda h, i: (h, i, 0)),
                  pl.BlockSpec((1, hd), lambda h, i: (0, h))],
        out_specs=[pl.BlockSpec((tm, hd), lambda h, i: (i, h)),
                   pl.BlockSpec((1, hd), lambda h, i: (0, h))],
        out_shape=[jax.ShapeDtypeStruct((s, d), F32), jax.ShapeDtypeStruct((1, d), F32)],
        compiler_params=_params(("arbitrary", "arbitrary")),
    )(dpre1, w_out_g, ac, rstd, g_ac)


def _attention_bwd(proj, d_ac, cos_t, sin_t, sinks, after, shard):
    s = proj.shape[0]
    qw = GROUP * N_KV_HEADS * HEAD_DIM
    kvw = N_KV_HEADS * HEAD_DIM
    nb = s // WINDOW
    nq = GROUP * N_KV_HEADS

    def body(cur_ref, prev_ref, do_ref, cos_ref, sin_ref, cosp_ref, sinp_ref, sinks_ref, after_ref,
             dq_ref, dcur_ref, dprev_ref, dsink_ref):
        n = pl.program_id(0)
        first = n == 0
        q, k_all, v_all, cos_q, sin_q = _roped_qkv(cur_ref, prev_ref, cos_ref, sin_ref, cosp_ref, sinp_ref, qw, kvw)
        kk2s = [_pair_operand(k_all, h) for h in range(N_KV_HEADS)]
        vv2s = [_pair_operand(v_all, h) for h in range(N_KV_HEADS)]
        qps, probs, p_sinks = _all_probs(q, kk2s, first, sinks_ref)
        dops = [do_ref[:, pair * PAIR:(pair + 1) * PAIR].astype(BF16) for pair in range(N_PAIRS)]
        d_probs = jnp.concatenate([_dot_nt(dops[pair], vv2s[pair // (GROUP // 2)]) for pair in range(N_PAIRS)], axis=0)
        d_s, ds_sinks = [], []
        for t in range(2):
            cols = slice(t * KEYS, (t + 1) * KEYS)
            delta = jnp.sum(probs[:, cols] * d_probs[:, cols], axis=1, keepdims=True)
            d_s.append(probs[:, cols] * (d_probs[:, cols] - delta))
            ds_sinks.append(-p_sinks[t] * delta)
        d_s = jnp.concatenate(d_s, axis=1).astype(BF16)
        probs = probs.astype(BF16)
        dq_parts, dk_tiles, dv_tiles, dsink_parts = [], [], [], []
        for h in range(N_KV_HEADS):
            dkk2, dvv2 = None, None
            for p in range(GROUP // 2):
                pair = (GROUP // 2) * h + p
                rows = slice(pair * WINDOW, (pair + 1) * WINDOW)
                dq_parts.append(_dot(d_s[rows], kk2s[h]) * ATTN_SCALE)
                dk_term = _dot_tn(d_s[rows], qps[pair])
                dv_term = _dot_tn(probs[rows], dops[pair])
                dkk2 = dk_term if dkk2 is None else dkk2 + dk_term
                dvv2 = dv_term if dvv2 is None else dvv2 + dv_term
                dsink_parts.extend([jnp.sum(ds_sinks[t][rows], axis=0, keepdims=True) for t in range(2)])
            dk_tiles.append(_pair_grad(dkk2, h))
            dv_tiles.append(_pair_grad(dvv2, h))
        dq_ref[...] = _rope(jnp.concatenate(dq_parts, axis=1), cos_q, sin_q, -1.0)
        dk = jnp.concatenate([dk_tiles[0] + dk_tiles[1], dk_tiles[2] + dk_tiles[3]], axis=1)
        dv = jnp.concatenate([dv_tiles[0] + dv_tiles[1], dv_tiles[2] + dv_tiles[3]], axis=1)
        dprev_ref[...] = jnp.concatenate([dk[:WINDOW], dv[:WINDOW]], axis=1)
        dcur_ref[...] = jnp.concatenate([dk[WINDOW:], dv[WINDOW:]], axis=1)
        dsink = jnp.concatenate(dsink_parts, axis=1)

        @pl.when(first)
        def _():
            dsink_ref[...] = dsink

        @pl.when(n > 0)
        def _():
            dsink_ref[...] += dsink

    tbl = pl.BlockSpec((WINDOW, kvw), lambda n: (n, 0))
    tbl_prev = pl.BlockSpec((WINDOW, kvw), lambda n: (jnp.maximum(n - 1, 0), 0))
    kv_blk = pl.BlockSpec((WINDOW, 2 * kvw), lambda n: (n, 0))
    return _call_with_adamw(
        body, "attention_bwd", (nb,),
        [pl.BlockSpec((WINDOW, qw + 2 * kvw), lambda n: (n, 0)),
         pl.BlockSpec((WINDOW, 2 * kvw), lambda n: (jnp.maximum(n - 1, 0), (qw // (2 * kvw)))),
         pl.BlockSpec((WINDOW, qw), lambda n: (n, 0)),
         tbl, tbl, tbl_prev, tbl_prev, _VMEM, _ANY],
        [pl.BlockSpec((WINDOW, qw), lambda n: (n, 0)), kv_blk, kv_blk, pl.BlockSpec((1, nq), lambda n: (0, 0))],
        [jax.ShapeDtypeStruct((s, qw), F32), jax.ShapeDtypeStruct((s, 2 * kvw), F32),
         jax.ShapeDtypeStruct((s, 2 * kvw), F32), jax.ShapeDtypeStruct((1, nq), F32)],
        [], ("arbitrary",), (proj, proj, d_ac, cos_t, sin_t, cos_t, sin_t, sinks, after), shard)


def _dproj_assemble(proj, d_ac, dq, dkv_cur, dkv_prev, cos_t, sin_t, cw_full):
    s, in_w = proj.shape
    cw = dq.shape[1]
    kvw = N_KV_HEADS * HEAD_DIM
    blk_w = in_w // 3
    tb = WINDOW
    nb = s // tb

    def body(lo_ref, hi_ref, lo_p_ref, hi_p_ref, lo_n_ref, hi_n_ref, dconv_ref, dconv_n_ref,
             dq_ref, dcur_ref, dprev_n_ref, cos_ref, sin_ref, cw_ref, dproj_ref, gcw_ref):
        i = pl.program_id(0)
        last = i == nb - 1
        c_gate, b_gate, u = _split_cbu(lo_ref[...], hi_ref[...], cw)
        c_p, _, u_p = _split_cbu(lo_p_ref[...], hi_p_ref[...], cw)
        _, b_n, _ = _split_cbu(lo_n_ref[...], hi_n_ref[...], cw)
        z = c_gate * u
        z_p = jnp.where(i == 0, 0.0, c_p * u_p)
        z1 = _shift_down(z, z_p, 1)
        z2 = _shift_down(z, z_p, 2)
        w0, w1, w2 = _conv_taps(cw_ref)
        y = w0 * z2 + w1 * z1 + w2 * z
        d_conv = dconv_ref[...]
        d_b = d_conv * y
        d_y = d_conv * b_gate
        d_y_n = jnp.where(last, 0.0, dconv_n_ref[...] * b_n[:dconv_n_ref.shape[0]])
        d_z = w2 * d_y + w1 * _shift_up(d_y, d_y_n, 1) + w0 * _shift_up(d_y, d_y_n, 2)
        d_c = d_z * u
        d_u = d_z * c_gate
        gcw = jnp.concatenate([jnp.sum(d_y * z2, axis=0, keepdims=True), jnp.sum(d_y * z1, axis=0, keepdims=True),
                               jnp.sum(d_y * z, axis=0, keepdims=True)], axis=0)

        @pl.when(i == 0)
        def _():
            gcw_ref[...] = gcw

        @pl.when(i > 0)
        def _():
            gcw_ref[...] += gcw

        dkv = dcur_ref[...] + jnp.where(last, 0.0, dprev_n_ref[...])
        dk = _rope(dkv[:, :kvw], cos_ref[...], sin_ref[...], -1.0)
        dproj_ref[...] = jnp.concatenate([dq_ref[...], dk, dkv[:, kvw:], d_c, d_b, d_u], axis=1).astype(BF16)

    prev_halo = lambda i: jnp.maximum(i * (tb // HALO_ROWS) - 1, 0)
    next_halo = lambda i: jnp.minimum((i + 1) * (tb // HALO_ROWS), s // HALO_ROWS - 1)
    next8 = lambda i: jnp.minimum((i + 1) * (tb // 8), s // 8 - 1)
    nxt = lambda i: jnp.minimum(i + 1, nb - 1)
    return pl.pallas_call(
        body, name="dproj_assemble", grid=(nb,),
        in_specs=[pl.BlockSpec((tb, blk_w), lambda i: (i, 1)),
                  pl.BlockSpec((tb, blk_w), lambda i: (i, 2)),
                  pl.BlockSpec((HALO_ROWS, blk_w), lambda i: (prev_halo(i), 1)),
                  pl.BlockSpec((HALO_ROWS, blk_w), lambda i: (prev_halo(i), 2)),
                  pl.BlockSpec((HALO_ROWS, blk_w), lambda i: (next_halo(i), 1)),
                  pl.BlockSpec((HALO_ROWS, blk_w), lambda i: (next_halo(i), 2)),
                  pl.BlockSpec((tb, cw), lambda i: (i, 1)),
                  pl.BlockSpec((8, cw), lambda i: (next8(i), 1)),
                  pl.BlockSpec((tb, cw), lambda i: (i, 0)),
                  pl.BlockSpec((tb, 2 * kvw), lambda i: (i, 0)),
                  pl.BlockSpec((tb, 2 * kvw), lambda i: (nxt(i), 0)),
                  pl.BlockSpec((tb, kvw), lambda i: (i, 0)),
                  pl.BlockSpec((tb, kvw), lambda i: (i, 0)),
                  _VMEM],
        out_specs=[pl.BlockSpec((tb, in_w), lambda i: (i, 0)),
                   pl.BlockSpec((3, cw), lambda i: (0, 0))],
        out_shape=[jax.ShapeDtypeStruct((s, in_w), BF16), jax.ShapeDtypeStruct((3, cw), F32)],
        compiler_params=_params(("arbitrary",)),
    )(proj, proj, proj, proj, proj, proj, d_ac, d_ac, dq, dkv_cur, dkv_prev, cos_t, sin_t, cw_full)


def _dx(d_proj, w_in_g, dpre1, after, shard):
    s, in_w = d_proj.shape
    ns, d, ncol = w_in_g.shape
    tm = min(TM, s)

    def body(dp_ref, w_ref, r_ref, after_ref, o_ref, acc):
        j = pl.program_id(1)
        _accumulate(acc, lambda: _dot_nt(dp_ref[...], w_ref[...]), j, ns)

        @pl.when(j == ns - 1)
        def _():
            o_ref[...] = acc[...] + ALPHA * r_ref[...]

    return _call_with_adamw(
        body, "dx", (s // tm, ns),
        [pl.BlockSpec((tm, ncol), lambda i, j: (i, j)),
         pl.BlockSpec((None, d, ncol), lambda i, j: (j, 0, 0)),
         pl.BlockSpec((tm, d), lambda i, j: (i, 0)), _ANY],
        [pl.BlockSpec((None, tm, d), lambda i, j: (0, i, 0))], [jax.ShapeDtypeStruct((1, s, d), F32)],
        [pltpu.VMEM((tm, d), F32)], ("parallel", "arbitrary"), (d_proj, w_in_g, dpre1, after), shard)


def kernel(x, positions, w_in, conv_w, sinks, g_attn, g_conv, w_out, ln1_g, ln1_b, w_gate, w_up, w_down, ln2_g, ln2_b, loss_target, m_w_in, m_conv_w, m_sinks, m_g_attn, m_g_conv, m_w_out, m_ln1_g, m_ln1_b, m_w_gate, m_w_up, m_w_down, m_ln2_g, m_ln2_b, v_w_in, v_conv_w, v_sinks, v_g_attn, v_g_conv, v_w_out, v_ln1_g, v_ln1_b, v_w_gate, v_w_up, v_w_down, v_ln2_g, v_ln2_b):
    s = x.shape[1]
    d = x.shape[2]

    chip_vec = _chip_id(lax.axis_index("x"), lax.axis_index("y")).astype(jnp.int32).reshape(1)
    wnames = ["w_in", "w_out", "w_gu", "w_down"]
    buf_in = _cast_weight(w_in, chip_vec, chip_vec, "cast_w_in")
    flight_in, token_in = _gather_start([buf_in], chip_vec, "gather_start_w_in")
    cw_buf = lax.dynamic_update_slice(jnp.zeros((N_CHIPS,) + conv_w.shape[1:], F32), conv_w, (chip_vec[0], 0, 0))
    cw_flight = _flight_start("conv_w_start", [cw_buf], _conv_w_plan(), 3, token_in)
    started = cw_flight[2][0]
    buf_gu = _cast_weight(w_gate, chip_vec, started, "cast_w_gate", 0, 2)
    buf_gu = _cast_weight(w_up, chip_vec, buf_gu, "cast_w_up", 1, 2)
    bufs = [_cast_weight(w_out, chip_vec, started, "cast_w_out"), buf_gu,
            _cast_weight(w_down, chip_vec, started, "cast_w_down")]
    flights_rest, token = _gather_start(bufs, token_in, "gather_start_rest")
    flights = flight_in + flights_rest

    def gathered(i, after):
        send_sems, recv_sems, buf = flights[i]
        buf = _gather_wait(send_sems, recv_sems, buf, after, "gather_wait_" + wnames[i])
        return _sibling_fill(buf, "sibling_fill_" + wnames[i])

    g_ac = jnp.concatenate([g_attn, g_conv], axis=1)

    proj_own, x16 = _in_proj(x, _after(flights[0][2], token), chip_vec, 1, None, "in_proj_own")
    cos_t, sin_t = _rope_tables(positions.reshape(s, 1) + token[0:1, 0:1].astype(jnp.int32))
    w_in_g = gathered(0, _after(cos_t, proj_own))
    (proj,) = _in_proj(x16, w_in_g, chip_vec + 1, N_CHIPS - 1, proj_own, "in_proj_rest")
    send_sems, recv_sems, buf_out = flights[1]
    buf_out = _gather_wait(send_sems, recv_sems, buf_out, proj, "gather_wait_w_out")
    fill_out = _flight_start("fill_start_w_out", [buf_out], _fill_plan(1), 3, chip_vec)
    attn = _attention_fwd(_after(proj, fill_out[2][0]), cos_t, sin_t, sinks)
    (cw_full,) = _flight_wait("conv_w_wait", cw_flight, _conv_w_plan(), attn)
    mixed, ac, rstd_ac = _conv_norm(proj, attn, cw_full, g_ac)
    (w_out_g,) = _flight_wait("fill_wait_w_out", fill_out, _fill_plan(1), mixed)
    w_out_full = w_out_g.reshape(d, d)
    xhat1, h1, rstd1 = _out_proj_ln(mixed, w_out_full, x, ln1_g, ln1_b)
    send_sems, recv_sems, buf_gu = flights[2]
    buf_gu = _gather_wait(send_sems, recv_sems, buf_gu, h1, "gather_wait_w_gu")
    fill_gu = _flight_start("fill_start_w_gu", [buf_gu], _fill_plan(1), 3, chip_vec)
    own = _gate_up(h1, fill_gu[2][0], chip_vec, 1, None, "gate_up_own")
    (w_gu_g,) = _flight_wait("fill_wait_w_gu", fill_gu, _fill_plan(1), own[0])
    some = _gate_up(h1, w_gu_g, chip_vec + 1, N_CHIPS - 2, own, "gate_up_rest")
    send_sems, recv_sems, buf_down = flights[3]
    buf_down = _gather_wait(send_sems, recv_sems, buf_down, some[0], "gather_wait_w_down")
    fill_down = _flight_start("fill_start_w_down", [buf_down], _fill_plan(1), 3, chip_vec)
    act, ab = _gate_up(h1, _after(w_gu_g, fill_down[2][0]), chip_vec + N_CHIPS - 1, 1, some, "gate_up_last")
    (w_down_g,) = _flight_wait("fill_wait_w_down", fill_down, _fill_plan(1), act)
    w_down_full = w_down_g.reshape(-1, d)
    dpre2, dpre2_16, loss_part, g_ln2_g, g_ln2_b = _down_ln_loss(act, w_down_full, xhat1, ln1_g, ln1_b, ln2_g, ln2_b,
                                                                 loss_target)

    cvec = lax.axis_index("c").astype(jnp.int32).reshape(1)

    def exchange_begin(parts, nme):
        bufs = []
        for part in parts:
            ns, r, cdim = part.shape
            bufs.extend([part, lax.empty((ns, r // 2, cdim), part.dtype)])
        return _flight_start("exchange_start_" + nme, bufs, _exchange_plan(len(parts)), len(parts), cvec)

    def exchange_end(flight, n_parts, after, nme):
        bufs = _flight_wait("exchange_wait_" + nme, flight, _exchange_plan(n_parts), after)
        return [(bufs[2 * w], bufs[2 * w + 1]) for w in range(n_parts)]

    def scatter_begin(part, got, nme):
        return _scatter_start(_add_halves(part, got, cvec, "add_halves_" + nme), "scatter_start_" + nme)

    d_gu = _dact_silu_bwd(dpre2_16, w_down_full, ab)
    p_down = _grad_rows(act, dpre2_16, d_gu, "grad_w_down")
    x_down = exchange_begin([p_down], "w_down")
    (p_gu,) = _grad_cols(h1, [d_gu], x_down[2][0], "grad_w_gate_up")
    ((p_down, got),) = exchange_end(x_down, 1, p_gu, "w_down")
    f_down = scatter_begin(p_down, got, "w_down")
    x_gu = exchange_begin([_after(p_gu, f_down[2])], "w_gu")
    dpre1, g_ln1_g, g_ln1_b = _dh1_ln_bwd(d_gu, w_gu_g, dpre2, xhat1, rstd1, ln1_g, x_gu[2][0])
    ((p_gu, got),) = exchange_end(x_gu, 1, dpre1, "w_gu")
    f_gu = scatter_begin(p_gu, got, "w_gu")
    d_ac, g_g_ac = _dmixed_rms_bwd(_after(dpre1, f_gu[2]), w_out_full, ac, rstd_ac, g_ac)
    pos_vec = jnp.concatenate([chip_vec, cvec])
    sums, land = _scatter_wait(*f_down, d_ac, "scatter_wait_w_down")
    c_down = _flight_start("complete_start_w_down", [sums, land], _complete_plan(1), 4, cvec)
    p_out = _grad_rows(mixed, dpre1, c_down[2][1], "grad_w_out")
    x_out = exchange_begin([p_out], "w_out")
    sums, land = _flight_wait("complete_wait_w_down", c_down, _complete_plan(1), x_out[2][0])
    dq, dkv_cur, dkv_prev, g_sinks, *new_w_down = _attention_bwd(
        proj, d_ac, cos_t, sin_t, sinks, x_out[2][0], (w_down, m_w_down, v_w_down, land, sums, pos_vec, 0))
    ((p_out, got),) = exchange_end(x_out, 1, dq, "w_out")
    f_out = scatter_begin(p_out, got, "w_out")
    sums, land = _scatter_wait(*f_gu, f_out[2], "scatter_wait_w_gu")
    c_gu = _flight_start("complete_start_w_gu", [sums, land], _complete_plan(1), 4, cvec)
    d_proj, g_conv_w = _dproj_assemble(proj, _after(d_ac, c_gu[2][1]), dq, dkv_cur, dkv_prev, cos_t, sin_t, cw_full)
    small_parts = _small_pack(g_ln2_g, g_ln2_b, g_ln1_g, g_ln1_b, g_g_ac, g_conv_w, g_sinks, loss_part)
    f_small = _flight_start("small_start", [small_parts], _small_plan(), N_DEVICES - 1, cvec)
    sums_gu, land_gu = _flight_wait("complete_wait_w_gu", c_gu, _complete_plan(1), f_small[2][0])
    p_in, *new_w_gate = _grad_cols(x16, [d_proj], f_small[2][0], "grad_w_in", a_3d=True,
                                   shard=(w_gate, m_w_gate, v_w_gate, land_gu, sums_gu, pos_vec, 0))
    (small_parts,) = _flight_wait("small_wait", f_small, _small_plan(), p_in)
    red = _small_sum(small_parts)
    x_in = exchange_begin([_after(p_in, red)], "w_in")
    sums, land = _scatter_wait(*f_out, x_in[2][0], "scatter_wait_w_out")
    c_out = _flight_start("complete_start_w_out", [sums, land], _complete_plan(1), 4, cvec)
    new_w_up = _adamw_shard(w_up, m_w_up, v_w_up, _after(land_gu, c_out[2][1]), sums_gu, pos_vec, "adamw_w_up", 1)
    ((p_in, got),) = exchange_end(x_in, 1, new_w_up[0], "w_in")
    f_in = scatter_begin(p_in, got, "w_in")
    (grad_x,) = _dx(d_proj, w_in_g, dpre1, f_in[2], None)

    big = {"w_down": new_w_down, "w_gate": new_w_gate, "w_up": new_w_up}
    sums, land = _scatter_wait(*f_in, grad_x, "scatter_wait_w_in")
    c_in = _flight_start("complete_start_w_in", [sums, land], _complete_plan(1), 4, cvec)
    sums, land = _flight_wait("complete_wait_w_out", c_out, _complete_plan(1), c_in[2][1])
    big["w_out"] = _adamw_shard(w_out, m_w_out, v_w_out, land, sums, pos_vec, "adamw_w_out")
    sums, land = _flight_wait("complete_wait_w_in", c_in, _complete_plan(1), big["w_out"][0])
    big["w_in"] = _adamw_shard(w_in, m_w_in, v_w_in, land, sums, pos_vec, "adamw_w_in")
    small = _adamw_small(red, {
        "sinks": (sinks, m_sinks, v_sinks), "g_attn": (g_attn, m_g_attn, v_g_attn),
        "g_conv": (g_conv, m_g_conv, v_g_conv), "ln1_g": (ln1_g, m_ln1_g, v_ln1_g),
        "ln1_b": (ln1_b, m_ln1_b, v_ln1_b), "ln2_g": (ln2_g, m_ln2_g, v_ln2_g),
        "ln2_b": (ln2_b, m_ln2_b, v_ln2_b), "conv_w": (conv_w, m_conv_w, v_conv_w)})
    res = {**big, **small}
    order = ["w_in", "conv_w", "sinks", "g_attn", "g_conv", "w_out", "ln1_g", "ln1_b", "w_gate", "w_up", "w_down",
             "ln2_g", "ln2_b"]
    loss = red[6, d // 2 + 128]
    return (loss, grad_x, *[res[n][0] for n in order], *[res[n][1] for n in order],
            *[res[n][2] for n in order], *[res[n][3] for n in order])
```

```python
import functools

import numpy as np
import jax
import jax.numpy as jnp
from jax import lax
from jax.experimental import pallas as pl
from jax.experimental.pallas import tpu as pltpu

F32 = jnp.float32
BF16 = jnp.bfloat16
MESH = pl.DeviceIdType.MESH

HEAD_DIM = 64
N_KV_HEADS = 4
GROUP = 4
WINDOW = 128
ROT_DIM = 16
ROPE_THETA = 500000.0
ATTN_SCALE = HEAD_DIM ** -0.5
ALPHA = 2.0 ** 0.25
LN_EPS = 1e-5
RMS_EPS = 1e-6
ADAM_LR = 0.001
ADAM_B1 = 0.9
ADAM_B2 = 0.999
ADAM_EPS = 1e-08
ADAM_WD = 0.01
ADAM_STEP = 10
N_CHIPS = 4
NEG_BIG = -1e30

V7X_VMEM_BYTES = 64 * 1024 * 1024
VMEM_LIMIT = V7X_VMEM_BYTES - 6 * 1024 * 1024

TM = 512
TK_TOK = 1024
TB_CONV = 256
TR_ELT = 256
ROW_CHUNK = 128
HALO_ROWS = 16


def _params(sem):
    return pltpu.CompilerParams(dimension_semantics=sem, vmem_limit_bytes=VMEM_LIMIT)


def _row_tile(rows, target):
    best = None
    for t in range(16, min(rows, target) + 1, 16):
        if rows % t == 0:
            best = t
    assert best is not None, (rows, target)
    return best


def _dot(a, b):
    return jnp.dot(a, b, preferred_element_type=F32)


def _dot_nt(a, b):
    return lax.dot_general(a, b, (((1,), (1,)), ((), ())), preferred_element_type=F32)


def _dot_tn(a, b):
    return lax.dot_general(a, b, (((0,), (0,)), ((), ())), preferred_element_type=F32)


def _mesh_pos():
    x, y, c = lax.axis_index("x"), lax.axis_index("y"), lax.axis_index("c")
    chips = [(1 - x, y), (x, 1 - y), (1 - x, 1 - y)]
    return x, y, c, chips


def _chip_id(px, py):
    return 2 * px + py


def _rope(t, cos, sgn_sin, sign):
    w = t.shape[1]
    lane = lax.broadcasted_iota(jnp.int32, t.shape, 1) & (HEAD_DIM - 1)
    partner = jnp.where(lane < ROT_DIM // 2, pltpu.roll(t, w - ROT_DIM // 2, 1), pltpu.roll(t, ROT_DIM // 2, 1))
    return t * cos + sign * (partner * sgn_sin)


def _tile_lanes(t, n):
    return jnp.concatenate([t] * n, axis=1)


def _sigmoid(g):
    return 1.0 / (1.0 + jnp.exp(-g))


def _for_row_chunks(n_rows, fn):
    def step(r, carry):
        fn(pl.ds(pl.multiple_of(r * ROW_CHUNK, ROW_CHUNK), ROW_CHUNK))
        return carry

    lax.fori_loop(0, n_rows // ROW_CHUNK, step, 0)


def _accumulate(acc, make_val, k, nk):
    if nk == 1:
        acc[...] = make_val()
        return

    @pl.when(k == 0)
    def _():
        acc[...] = jnp.zeros_like(acc)

    acc[...] += make_val()


def _ln_fwd(pre):
    mu = jnp.mean(pre, axis=-1, keepdims=True)
    cen = pre - mu
    var = jnp.mean(cen * cen, axis=-1, keepdims=True)
    rstd = lax.rsqrt(var + LN_EPS)
    return cen * rstd, rstd


def _ln_bwd(dy, xhat, rstd, g):
    dxhat = dy * g
    m1 = jnp.mean(dxhat, axis=-1, keepdims=True)
    m2 = jnp.mean(dxhat * xhat, axis=-1, keepdims=True)
    return rstd * (dxhat - m1 - xhat * m2)


def _cast_weight(w, chip_vec, after, name, col_block=0, n_col_blocks=1):
    _, r, c = w.shape
    tr = _row_tile(r, TR_ELT)

    def body(chip_ref, w_ref, after_ref, o_ref):
        o_ref[...] = w_ref[...].astype(BF16)

    grid_spec = pltpu.PrefetchScalarGridSpec(
        num_scalar_prefetch=1, grid=(r // tr,),
        in_specs=[pl.BlockSpec((None, tr, c), lambda i, chip_ref: (0, i, 0)), _ANY],
        out_specs=pl.BlockSpec((None, tr, c), lambda i, chip_ref: (chip_ref[0], i, col_block)))
    return pl.pallas_call(
        body, name=name, grid_spec=grid_spec,
        out_shape=jax.ShapeDtypeStruct((N_CHIPS, r, n_col_blocks * c), BF16),
        input_output_aliases={2: 0} if col_block else {},
        compiler_params=_params(("parallel",)),
    )(chip_vec, w, after)


_HBM = pl.BlockSpec(memory_space=pltpu.HBM)
_VMEM = pl.BlockSpec(memory_space=pltpu.VMEM)


_SEM = pl.BlockSpec(memory_space=pltpu.SEMAPHORE)
_ANY = pl.BlockSpec(memory_space=pl.ANY)
_EFFECT = pltpu.SideEffectType.DATAFLOW_SIDE_EFFECTING


def _chip_copy(buf, k, chip_of_src, half_rows, send_sems, recv_sems, to):
    part = buf.at[chip_of_src, half_rows]
    return pltpu.make_async_remote_copy(
        src_ref=part, dst_ref=part, send_sem=send_sems.at[k], recv_sem=recv_sems.at[k], device_id=to, device_id_type=MESH)


def _half_rows(buf, which):
    hr = buf.shape[1] // 2
    return pl.ds(which * hr, hr)


def _after(value, dep):
    return lax.optimization_barrier((value, dep))[0]


def _flight_start(name, bufs, plan, n_sems, after):
    n = len(bufs)

    def body(*refs):
        sends, _ = plan(refs[:n], refs[n + 1], refs[n + 2])
        for cp in sends:
            cp.start()

    outs = pl.pallas_call(
        body, name=name,
        in_specs=[_HBM] * n + [_ANY], out_specs=[_SEM, _SEM] + [_HBM] * n,
        out_shape=[pltpu.SemaphoreType.DMA((n_sems,))] * 2 + [pltpu.HBM(b.shape, b.dtype) for b in bufs],
        input_output_aliases={i: 2 + i for i in range(n)},
        compiler_params=pltpu.CompilerParams(has_side_effects=_EFFECT),
    )(*[pltpu.with_memory_space_constraint(b, pltpu.HBM) for b in bufs], after)
    return outs[0], outs[1], list(outs[2:])


def _flight_wait(name, flight, plan, after):
    send_sems, recv_sems, bufs = flight
    n = len(bufs)

    def body(*refs):
        sends, recvs = plan(refs[:n], refs[n], refs[n + 1])
        for cp in sends:
            cp.wait_send()
        for cp in recvs:
            cp.wait_recv()

    outs = pl.pallas_call(
        body, name=name,
        in_specs=[_HBM] * n + [_SEM, _SEM, _ANY], out_specs=[_HBM] * n,
        out_shape=[pltpu.HBM(b.shape, b.dtype) for b in bufs],
        input_output_aliases={i: i for i in range(n)},
        compiler_params=pltpu.CompilerParams(has_side_effects=_EFFECT),
    )(*bufs, send_sems, recv_sems, after)
    return list(outs)


def _fill_plan(n_bufs):
    def plan(refs, send_sems, recv_sems):
        x, y, c, chips = _mesh_pos()
        sibling = (x, y, 1 - c)
        sends, recvs = [], []
        for w in range(n_bufs):
            for k, chip in enumerate(chips):
                slot = _chip_id(*chip)
                sends.append(_chip_copy(refs[w], 3 * w + k, slot, _half_rows(refs[w], c), send_sems, recv_sems, sibling))
                recvs.append(_chip_copy(refs[w], 3 * w + k, slot, _half_rows(refs[w], 1 - c), send_sems, recv_sems,
                                        sibling))
        return sends, recvs
    return plan


def _conv_w_plan():
    def plan(refs, send_sems, recv_sems):
        x, y, c, chips = _mesh_pos()
        me = _chip_id(x, y)
        (buf,) = refs
        sends, recvs = [], []
        for k, chip in enumerate(chips):
            for slot, into in ((me, sends), (_chip_id(*chip), recvs)):
                into.append(pltpu.make_async_remote_copy(
                    src_ref=buf.at[slot], dst_ref=buf.at[slot], send_sem=send_sems.at[k], recv_sem=recv_sems.at[k],
                    device_id=(*chip, c), device_id_type=MESH))
        return sends, recvs
    return plan


def _exchange_plan(n_parts):
    def plan(refs, send_sems, recv_sems):
        x, y, c, _ = _mesh_pos()
        copies = []
        for w in range(n_parts):
            part, got = refs[2 * w], refs[2 * w + 1]
            hr = got.shape[1]
            copies.append(pltpu.make_async_remote_copy(
                src_ref=part.at[:, pl.ds((1 - c) * hr, hr)], dst_ref=got, send_sem=send_sems.at[w],
                recv_sem=recv_sems.at[w], device_id=(x, y, 1 - c), device_id_type=MESH))
        return copies, copies
    return plan


def _gather_start(bufs, after, name):
    n = len(bufs)

    def body(*refs):
        ins = refs[:n]
        sends, recvs = refs[n + 1:2 * n + 1], refs[2 * n + 1:3 * n + 1]
        token = refs[4 * n + 1]
        x, y, c, chips = _mesh_pos()
        me = _chip_id(x, y)
        for w in range(n):
            for k, chip in enumerate(chips):
                _chip_copy(ins[w], k, me, _half_rows(ins[w], c), sends[w], recvs[w], (*chip, c)).start()
        token[...] = jnp.zeros_like(token)

    outs = pl.pallas_call(
        body, name=name,
        in_specs=[_HBM] * n + [_ANY],
        out_specs=[_SEM] * (2 * n) + [_HBM] * n + [_VMEM],
        out_shape=[pltpu.SemaphoreType.DMA((3,))] * (2 * n) + [pltpu.HBM(b.shape, b.dtype) for b in bufs]
        + [jax.ShapeDtypeStruct((8, 128), F32)],
        input_output_aliases={w: 2 * n + w for w in range(n)},
        compiler_params=pltpu.CompilerParams(has_side_effects=_EFFECT),
    )(*[pltpu.with_memory_space_constraint(b, pltpu.HBM) for b in bufs], after)
    return [(outs[w], outs[n + w], outs[2 * n + w]) for w in range(n)], outs[3 * n]


def _gather_wait(send_sems, recv_sems, buf, after, name):
    def body(buf_ref, send_ref, recv_ref, after_ref, out_ref):
        x, y, c, chips = _mesh_pos()
        me = _chip_id(x, y)
        for k, chip in enumerate(chips):
            _chip_copy(buf_ref, k, me, _half_rows(buf_ref, c), send_ref, recv_ref, (*chip, c)).wait_send()
        for k, chip in enumerate(chips):
            _chip_copy(buf_ref, k, _chip_id(*chip), _half_rows(buf_ref, c), send_ref, recv_ref, (*chip, c)).wait_recv()

    return pl.pallas_call(
        body, name=name,
        in_specs=[_HBM, _SEM, _SEM, _ANY], out_specs=_HBM,
        out_shape=pltpu.HBM(buf.shape, buf.dtype),
        input_output_aliases={0: 0},
        compiler_params=pltpu.CompilerParams(has_side_effects=_EFFECT),
    )(buf, send_sems, recv_sems, after)


def _sibling_fill(buf, name, own_too=False):
    n_copies = 4 if own_too else 3

    def body(buf_ref, out_ref, send_sems, recv_sems):
        x, y, c, chips = _mesh_pos()
        sibling = (x, y, 1 - c)
        slots = [_chip_id(*chip) for chip in chips] + ([_chip_id(x, y)] if own_too else [])
        copies = []
        for k, slot in enumerate(slots):
            cp = _chip_copy(out_ref, k, slot, _half_rows(out_ref, c), send_sems, recv_sems, sibling)
            cp.start()
            copies.append(cp)
        for k, slot in enumerate(slots):
            _chip_copy(out_ref, k, slot, _half_rows(out_ref, 1 - c), send_sems, recv_sems, sibling).wait_recv()
        for cp in copies:
            cp.wait_send()

    return pl.pallas_call(
        body, name=name,
        in_specs=[_HBM], out_specs=_HBM,
        out_shape=jax.ShapeDtypeStruct(buf.shape, buf.dtype),
        input_output_aliases={0: 0},
        scratch_shapes=[pltpu.SemaphoreType.DMA((n_copies,)), pltpu.SemaphoreType.DMA((n_copies,))],
    )(buf)


def _add_halves(part, got, cvec, name):
    ns, r, cdim = part.shape
    hr = r // 2
    tr = _row_tile(hr, TR_ELT)
    nblk = hr // tr

    def body(c_ref, a_ref, b_ref, o_ref):
        o_ref[...] = a_ref[...] + b_ref[...]

    grid_spec = pltpu.PrefetchScalarGridSpec(
        num_scalar_prefetch=1, grid=(ns, nblk),
        in_specs=[pl.BlockSpec((None, tr, cdim), lambda s, i, c_ref: (s, c_ref[0] * nblk + i, 0)),
                  pl.BlockSpec((None, tr, cdim), lambda s, i, c_ref: (s, i, 0))],
        out_specs=pl.BlockSpec((None, tr, cdim), lambda s, i, c_ref: (s, i, 0)))
    return pl.pallas_call(
        body, name=name, grid_spec=grid_spec,
        out_shape=jax.ShapeDtypeStruct((ns, hr, cdim), BF16),
        compiler_params=_params(("parallel", "parallel")),
    )(cvec, part, got)


def _scatter_copy(sums_ref, land_ref, k, src_slot, dst_slot, c, send_sems, recv_sems, to):
    return pltpu.make_async_remote_copy(
        src_ref=sums_ref.at[src_slot], dst_ref=land_ref.at[dst_slot, _half_rows(land_ref, c)],
        send_sem=send_sems.at[k], recv_sem=recv_sems.at[k], device_id=to, device_id_type=MESH)


def _scatter_start(sums, name):
    ns, hr, cdim = sums.shape
    land = lax.empty((ns, 2 * hr, cdim), sums.dtype)

    def body(sums_ref, land_ref, send_sems, recv_sems, sums_thru, land_thru):
        x, y, c, chips = _mesh_pos()
        me = _chip_id(x, y)
        for k, chip in enumerate(chips):
            _scatter_copy(sums_ref, land_ref, k, _chip_id(*chip), me, c, send_sems, recv_sems, (*chip, c)).start()

    return pl.pallas_call(
        body, name=name,
        in_specs=[_HBM, _HBM], out_specs=[_SEM, _SEM, _HBM, _HBM],
        out_shape=[pltpu.SemaphoreType.DMA((3,)), pltpu.SemaphoreType.DMA((3,)),
                   pltpu.HBM(sums.shape, sums.dtype), pltpu.HBM(land.shape, land.dtype)],
        input_output_aliases={0: 2, 1: 3},
        compiler_params=pltpu.CompilerParams(has_side_effects=_EFFECT),
    )(pltpu.with_memory_space_constraint(sums, pltpu.HBM), pltpu.with_memory_space_constraint(land, pltpu.HBM))


def _scatter_wait(send_sems, recv_sems, sums, land, after, name):
    def body(sums_ref, land_ref, send_ref, recv_ref, after_ref, sums_out, land_out):
        x, y, c, chips = _mesh_pos()
        me = _chip_id(x, y)
        for k, chip in enumerate(chips):
            _scatter_copy(sums_ref, land_ref, k, _chip_id(*chip), me, c, send_ref, recv_ref, (*chip, c)).wait_send()
        for k, chip in enumerate(chips):
            _scatter_copy(sums_ref, land_ref, k, me, _chip_id(*chip), c, send_ref, recv_ref, (*chip, c)).wait_recv()

    return pl.pallas_call(
        body, name=name,
        in_specs=[_HBM, _HBM, _SEM, _SEM, _ANY], out_specs=[_HBM, _HBM],
        out_shape=[pltpu.HBM(sums.shape, sums.dtype), pltpu.HBM(land.shape, land.dtype)],
        input_output_aliases={0: 0, 1: 1},
        compiler_params=pltpu.CompilerParams(has_side_effects=_EFFECT),
    )(sums, land, send_sems, recv_sems, after)


def _complete_plan(n_weights):
    def plan(refs, send_sems, recv_sems):
        x, y, c, chips = _mesh_pos()
        me = _chip_id(x, y)
        sibling = (x, y, 1 - c)
        sends, recvs = [], []
        for w in range(n_weights):
            sums, land = refs[2 * w], refs[2 * w + 1]
            sends.append(_scatter_copy(sums, land, 4 * w + 3, me, me, c, send_sems, recv_sems, sibling))
            recvs.append(_scatter_copy(sums, land, 4 * w + 3, me, me, 1 - c, send_sems, recv_sems, sibling))
            for k, chip in enumerate(chips):
                slot = _chip_id(*chip)
                sends.append(_chip_copy(land, 4 * w + k, slot, _half_rows(land, c), send_sems, recv_sems, sibling))
                recvs.append(_chip_copy(land, 4 * w + k, slot, _half_rows(land, 1 - c), send_sems, recv_sems, sibling))
        return sends, recvs
    return plan


SMALL_ROWS = 8


N_DEVICES = 8


def _small_pack(gl2g, gl2b, gl1g, gl1b, g_ac, gcw, gsink, loss):
    d = gl2g.shape[1]
    hd = d // 2
    nq = gsink.shape[1]

    def body(a_ref, b_ref, c_ref, d_ref, e_ref, cw_ref, sk_ref, ls_ref, out_ref, mine):
        x, y, c, _ = _mesh_pos()
        me = 4 * x + 2 * y + c
        mine[...] = jnp.zeros_like(mine)
        mine[0:1, :] = a_ref[...]
        mine[1:2, :] = b_ref[...]
        mine[2:3, :] = c_ref[...]
        mine[3:4, :] = d_ref[...]
        mine[4:5, :] = e_ref[...]
        mine[5:6, 0:hd] = cw_ref[0:1, :]
        mine[5:6, hd:d] = cw_ref[1:2, :]
        mine[6:7, 0:hd] = cw_ref[2:3, :]
        mine[6:7, hd:hd + nq] = sk_ref[...]
        mine[6:7, hd + 128:hd + 256] = ls_ref[...]
        out_ref[...] = jnp.zeros_like(out_ref)
        out_ref[pl.ds(me, 1)] = mine[...][None]

    return pl.pallas_call(
        body, name="small_pack",
        in_specs=[_VMEM] * 8, out_specs=_VMEM,
        out_shape=jax.ShapeDtypeStruct((N_DEVICES, SMALL_ROWS, d), F32),
        scratch_shapes=[pltpu.VMEM((SMALL_ROWS, d), F32)],
    )(gl2g, gl2b, gl1g, gl1b, g_ac, gcw, gsink, loss)


def _small_plan():
    def plan(refs, send_sems, recv_sems):
        x, y, c, _ = _mesh_pos()
        me = 4 * x + 2 * y + c
        (gath,) = refs
        sends, recvs = [], []
        for r in range(1, N_DEVICES):
            peer = ((1 - x) if r & 4 else x, (1 - y) if r & 2 else y, (1 - c) if r & 1 else c)
            peer_id = 4 * peer[0] + 2 * peer[1] + peer[2]
            for slot, into in ((me, sends), (peer_id, recvs)):
                into.append(pltpu.make_async_remote_copy(
                    src_ref=gath.at[slot], dst_ref=gath.at[slot], send_sem=send_sems.at[r - 1],
                    recv_sem=recv_sems.at[r - 1], device_id=peer, device_id_type=MESH))
        return sends, recvs
    return plan


def _small_sum(gath):
    def body(gath_ref, out_ref):
        total = gath_ref[0]
        for dev in range(1, N_DEVICES):
            total = total + gath_ref[dev]
        out_ref[...] = total

    return pl.pallas_call(
        body, name="small_sum", in_specs=[_VMEM], out_specs=_VMEM,
        out_shape=jax.ShapeDtypeStruct(gath.shape[1:], F32),
    )(gath)


def _adamw(w, g, m, v):
    m = ADAM_B1 * m + (1.0 - ADAM_B1) * g
    v = ADAM_B2 * v + (1.0 - ADAM_B2) * (g * g)
    m_hat = m / (1.0 - ADAM_B1 ** ADAM_STEP)
    v_hat = v / (1.0 - ADAM_B2 ** ADAM_STEP)
    delta = -ADAM_LR * (m_hat / (jnp.sqrt(v_hat) + ADAM_EPS) + ADAM_WD * w)
    return delta, m, v


def _adamw_shard(w, m, v, land, own, pos_vec, name, col_block=0):
    tr = _row_tile(w.shape[1] // 2, TR_ELT)
    grid = (w.shape[1] // tr,)
    body, in_specs, out_specs, out_shape = _adamw_passenger(w.shape, tr, grid, col_block)
    grid_spec = pltpu.PrefetchScalarGridSpec(num_scalar_prefetch=1, grid=grid, in_specs=in_specs, out_specs=out_specs)
    return pl.pallas_call(
        body, name=name, grid_spec=grid_spec, out_shape=out_shape,
        compiler_params=_params(("parallel",)),
    )(pos_vec, w, m, v, land, land, land, land, own)


def _adamw_passenger(shape, tr, grid, col_block):
    _, r, c = shape
    nh = r // 2 // tr
    n_blocks = 2 * nh
    n_steps = int(np.prod(grid))
    assert nh * tr * 2 == r and n_blocks <= n_steps

    def step_of(ids):
        step = ids[0]
        for n, i in zip(grid[1:], ids[1:]):
            step = step * n + i
        return step

    def block_of(ids):
        return jnp.minimum(step_of(ids), n_blocks - 1)

    def update(pos_ref, w_ref, m_ref, v_ref, l0, l1, l2, l3, own_ref, g_out, d_out, m_out, v_out):
        i = block_of([pl.program_id(a) for a in range(len(grid))])
        mine = (i // nh) == pos_ref[1]
        own_blk = own_ref[...].astype(F32)
        g = None
        for s, l_ref in enumerate([l0, l1, l2, l3]):
            term = jnp.where(mine & (pos_ref[0] == s), own_blk, l_ref[...].astype(F32))
            g = term if g is None else g + term
        delta, nm, nv = _adamw(w_ref[...], g, m_ref[...], v_ref[...])
        g_out[...] = g
        d_out[...] = delta
        m_out[...] = nm
        v_out[...] = nv

    def body(*refs):
        if n_blocks == n_steps:
            update(*refs)
        else:
            pl.when(step_of([pl.program_id(a) for a in range(len(grid))]) < n_blocks)(lambda: update(*refs))

    def land_spec(s):
        def index(*args):
            i, pos_ref = block_of(args[:-1]), args[-1]
            skip = (pos_ref[0] == s) & ((i // nh) == pos_ref[1])
            return (s, jnp.where(skip, (i + nh) % n_blocks, i), col_block)
        return pl.BlockSpec((None, tr, c), index)

    blk = pl.BlockSpec((None, tr, c), lambda *args: (0, block_of(args[:-1]), 0))
    in_specs = ([blk, blk, blk] + [land_spec(s) for s in range(N_CHIPS)]
                + [pl.BlockSpec((None, tr, c), lambda *args: (args[-1][0], block_of(args[:-1]) % nh, col_block))])
    return body, in_specs, [blk] * 4, [jax.ShapeDtypeStruct((1, r, c), F32)] * 4


def _call_with_adamw(body, name, grid, in_specs, out_specs, out_shape, scratch_shapes, semantics, operands, shard):
    if shard is None:
        return pl.pallas_call(
            body, name=name, grid=grid, in_specs=in_specs, out_specs=out_specs, out_shape=out_shape,
            scratch_shapes=scratch_shapes, compiler_params=_params(semantics))(*operands)
    w, m, v, land, own, pos_vec, col_block = shard
    n_steps = int(np.prod(grid))
    hr = w.shape[1] // 2
    tr = min(t for t in range(16, hr + 1, 16) if hr % t == 0 and 2 * (hr // t) <= n_steps)
    adam_body, adam_in, adam_out, adam_shape = _adamw_passenger(w.shape, tr, grid, col_block)
    n_in, n_out = len(in_specs), len(out_specs)

    def with_pos(spec):
        if spec.index_map is None:
            return spec
        return pl.BlockSpec(spec.block_shape, lambda *args: spec.index_map(*args[:-1]))

    def both(pos_ref, *refs):
        ins, adam_ins = refs[:n_in], refs[n_in:n_in + len(adam_in)]
        refs = refs[n_in + len(adam_in):]
        outs, adam_outs, scratch = refs[:n_out], refs[n_out:n_out + len(adam_out)], refs[n_out + len(adam_out):]
        body(*ins, *outs, *scratch)
        adam_body(pos_ref, *adam_ins, *adam_outs)

    grid_spec = pltpu.PrefetchScalarGridSpec(
        num_scalar_prefetch=1, grid=grid, in_specs=[with_pos(sp) for sp in in_specs] + adam_in,
        out_specs=[with_pos(sp) for sp in out_specs] + adam_out, scratch_shapes=scratch_shapes)
    return pl.pallas_call(
        both, name=name, grid_spec=grid_spec, out_shape=list(out_shape) + adam_shape,
        compiler_params=_params(semantics),
    )(pos_vec, *operands, w, m, v, land, land, land, land, own)


def _adamw_small(red, params):
    names = ["sinks", "g_attn", "g_conv", "ln1_g", "ln1_b", "ln2_g", "ln2_b", "conv_w"]
    d = red.shape[1]
    hd = d // 2
    flat = []
    for nme in names:
        flat.extend(params[nme])
    nq = params["sinks"][0].shape[1]
    cs = params["conv_w"][0].shape[2]

    def body(*refs):
        red_ref = refs[0]
        ins = refs[1:1 + 3 * len(names)]
        outs = refs[1 + 3 * len(names):]
        x, y, _, _ = _mesh_pos()
        me = _chip_id(x, y)

        def conv_tap(row, base):
            picked = red_ref[row:row + 1, base:base + cs]
            for s in range(1, N_CHIPS):
                picked = jnp.where(me == s, red_ref[row:row + 1, base + s * cs:base + (s + 1) * cs], picked)
            return picked

        grads = {
            "sinks": red_ref[6:7, hd:hd + nq],
            "g_attn": red_ref[4:5, 0:hd],
            "g_conv": red_ref[4:5, hd:d],
            "ln1_g": red_ref[2:3, :],
            "ln1_b": red_ref[3:4, :],
            "ln2_g": red_ref[0:1, :],
            "ln2_b": red_ref[1:2, :],
        }
        for i, nme in enumerate(names):
            w_ref, m_ref, v_ref = ins[3 * i:3 * i + 3]
            g_out, d_out, m_out, v_out = outs[4 * i:4 * i + 4]
            if nme == "conv_w":
                for tap, (row, base) in enumerate([(5, 0), (5, hd), (6, 0)]):
                    g = conv_tap(row, base)
                    delta, nm, nv = _adamw(w_ref[0, tap:tap + 1, :], g, m_ref[0, tap:tap + 1, :], v_ref[0, tap:tap + 1, :])
                    g_out[0, tap:tap + 1, :] = g
                    d_out[0, tap:tap + 1, :] = delta
                    m_out[0, tap:tap + 1, :] = nm
                    v_out[0, tap:tap + 1, :] = nv
            else:
                g = grads[nme]
                delta, nm, nv = _adamw(w_ref[...], g, m_ref[...], v_ref[...])
                g_out[...] = g
                d_out[...] = delta
                m_out[...] = nm
                v_out[...] = nv

    out_shape = []
    for nme in names:
        out_shape.extend([jax.ShapeDtypeStruct(params[nme][0].shape, F32)] * 4)
    outs = pl.pallas_call(
        body, name="adamw_small",
        in_specs=[_VMEM] * (1 + len(flat)), out_specs=[_VMEM] * len(out_shape),
        out_shape=out_shape,
    )(red, *flat)
    return {nme: tuple(outs[4 * i:4 * i + 4]) for i, nme in enumerate(names)}


def _rope_tables(pos_col):
    s = pos_col.shape[0]
    w = N_KV_HEADS * HEAD_DIM
    tb = min(512, s)
    inv_freq = (ROPE_THETA ** (-np.arange(0, ROT_DIM, 2, dtype=np.float32) / ROT_DIM)).astype(np.float32)

    def body(pos_ref, cos_ref, sin_ref):
        pos = pos_ref[...].astype(F32)
        lane = lax.broadcasted_iota(jnp.int32, (tb, PAIR), 1) & (HEAD_DIM - 1)
        fidx = lane & (ROT_DIM // 2 - 1)
        inv = jnp.zeros((tb, PAIR), F32)
        for k in range(ROT_DIM // 2):
            inv = jnp.where(fidx == k, float(inv_freq[k]), inv)
        ang = pos * inv
        rot = lane < ROT_DIM
        sin_v = jnp.sin(ang)
        cos_ref[...] = _tile_lanes(jnp.where(rot, jnp.cos(ang), 1.0), w // PAIR)
        sin_ref[...] = _tile_lanes(jnp.where(lane < ROT_DIM // 2, -sin_v, jnp.where(rot, sin_v, 0.0)), w // PAIR)

    return pl.pallas_call(
        body, name="rope_tables", grid=(s // tb,),
        in_specs=[pl.BlockSpec((tb, 1), lambda i: (i, 0))],
        out_specs=[pl.BlockSpec((tb, w), lambda i: (i, 0))] * 2,
        out_shape=[jax.ShapeDtypeStruct((s, w), F32)] * 2,
        compiler_params=_params(("parallel",)),
    )(pos_col)


def _in_proj(x, w_in_g, first_vec, n_shards, into, name):
    _, s, d = x.shape
    ns, _, ncol = w_in_g.shape
    tm = min(2 * TM, s)
    first_call = into is None
    assert n_shards == 1 or not first_call

    def body(first_ref, x_ref, w_ref, into_ref, o_ref, *x16_ref):
        xb = x_ref[...].astype(BF16)
        o_ref[...] = _dot(xb, w_ref[...]).astype(BF16)
        for ref in x16_ref:
            ref[...] = xb

    shard = lambda j, first_ref: lax.rem(first_ref[0] + j, ns)
    x_spec = pl.BlockSpec((None, tm, d), lambda i, j, first_ref: (0, i, 0))
    grid_spec = pltpu.PrefetchScalarGridSpec(
        num_scalar_prefetch=1, grid=(s // tm, n_shards),
        in_specs=[x_spec, pl.BlockSpec((None, d, ncol), lambda i, j, first_ref: (shard(j, first_ref), 0, 0)), _ANY],
        out_specs=[pl.BlockSpec((tm, ncol), lambda i, j, first_ref: (i, shard(j, first_ref)))] + [x_spec] * first_call)
    return pl.pallas_call(
        body, name=name, grid_spec=grid_spec,
        out_shape=[jax.ShapeDtypeStruct((s, ns * ncol), BF16)] + [jax.ShapeDtypeStruct((1, s, d), BF16)] * first_call,
        input_output_aliases={} if first_call else {3: 0},
        compiler_params=_params(("parallel", "arbitrary")),
    )(first_vec, x, w_in_g, first_vec if first_call else into)


PAIR = 2 * HEAD_DIM
KEYS = 2 * WINDOW


def _pair_operand(t_all, h):
    col = (h // 2) * PAIR
    lane = lax.broadcasted_iota(jnp.int32, (KEYS, PAIR), 1)
    own_low = h % 2 == 0
    mine = jnp.where((lane < HEAD_DIM) if own_low else (lane >= HEAD_DIM), t_all[:, col:col + PAIR], 0.0)
    other = pltpu.roll(mine, HEAD_DIM, 1)
    low, high = (mine, other) if own_low else (other, mine)
    return jnp.concatenate([low, high], axis=0).astype(BF16)


def _pair_grad(acc, h):
    lane = lax.broadcasted_iota(jnp.int32, (KEYS, PAIR), 1)
    low = jnp.where(lane < HEAD_DIM, acc[:KEYS], 0.0)
    high = jnp.where(lane >= HEAD_DIM, acc[KEYS:], 0.0)
    if h % 2 == 0:
        return low + pltpu.roll(high, HEAD_DIM, 1)
    return high + pltpu.roll(low, HEAD_DIM, 1)


N_PAIRS = N_KV_HEADS * GROUP // 2


def _all_probs(q, kk2s, first, sinks_ref):
    assert ATTN_SCALE == 0.125
    q = q * ATTN_SCALE
    qps, scores = [], []
    for pair in range(N_PAIRS):
        qp = q[:, pair * PAIR:(pair + 1) * PAIR].astype(BF16)
        qps.append(qp)
        scores.append(_dot_nt(qp, kk2s[pair // (GROUP // 2)]))
    qi = lax.broadcasted_iota(jnp.int32, (WINDOW, 2 * KEYS), 0)
    kj = lax.broadcasted_iota(jnp.int32, (WINDOW, 2 * KEYS), 1) & (KEYS - 1)
    rel = qi + WINDOW - kj
    valid = (rel >= 0) & (rel < WINDOW) & jnp.logical_not(first & (kj < WINDOW))
    bias = jnp.where(valid, 0.0, NEG_BIG)
    s = (jnp.stack(scores, axis=0) + bias[None]).reshape(N_PAIRS * WINDOW, 2 * KEYS)
    probs, p_sinks = [], []
    for t in range(2):
        st = s[:, t * KEYS:(t + 1) * KEYS]
        sink = jnp.concatenate([jnp.broadcast_to(sinks_ref[0:1, 2 * pair + t:2 * pair + t + 1], (WINDOW, 1))
                                for pair in range(N_PAIRS)], axis=0)
        m = jnp.maximum(jnp.max(st, axis=1, keepdims=True), sink)
        e = jnp.exp(st - m)
        e_sink = jnp.exp(sink - m)
        inv_l = 1.0 / (jnp.sum(e, axis=1, keepdims=True) + e_sink)
        probs.append(e * inv_l)
        p_sinks.append(e_sink * inv_l)
    return qps, jnp.concatenate(probs, axis=1), p_sinks


def _roped_qkv(cur_ref, prev_ref, cos_ref, sin_ref, cosp_ref, sinp_ref, qw, kvw):
    cur = cur_ref[...].astype(F32)
    cos, sin = cos_ref[...], sin_ref[...]
    cos_q, sin_q = _tile_lanes(cos, GROUP), _tile_lanes(sin, GROUP)
    q = _rope(cur[:, :qw], cos_q, sin_q, 1.0)
    prev = prev_ref[...].astype(F32)
    k_all = jnp.concatenate([_rope(prev[:, :kvw], cosp_ref[...], sinp_ref[...], 1.0),
                             _rope(cur[:, qw:qw + kvw], cos, sin, 1.0)], axis=0)
    v_all = jnp.concatenate([prev[:, kvw:], cur[:, qw + kvw:]], axis=0)
    return q, k_all, v_all, cos_q, sin_q


def _attention_fwd(proj, cos_t, sin_t, sinks):
    s = proj.shape[0]
    qw = GROUP * N_KV_HEADS * HEAD_DIM
    kvw = N_KV_HEADS * HEAD_DIM
    nb = s // WINDOW

    def body(cur_ref, prev_ref, cos_ref, sin_ref, cosp_ref, sinp_ref, sinks_ref, o_ref):
        first = pl.program_id(0) == 0
        q, k_all, v_all, _, _ = _roped_qkv(cur_ref, prev_ref, cos_ref, sin_ref, cosp_ref, sinp_ref, qw, kvw)
        kk2s = [_pair_operand(k_all, h) for h in range(N_KV_HEADS)]
        vv2s = [_pair_operand(v_all, h) for h in range(N_KV_HEADS)]
        _, probs, _ = _all_probs(q, kk2s, first, sinks_ref)
        probs = probs.astype(BF16)
        outs = [_dot(probs[pair * WINDOW:(pair + 1) * WINDOW], vv2s[pair // (GROUP // 2)]) for pair in range(N_PAIRS)]
        o_ref[...] = jnp.concatenate(outs, axis=1)

    tbl = pl.BlockSpec((WINDOW, kvw), lambda n: (n, 0))
    tbl_prev = pl.BlockSpec((WINDOW, kvw), lambda n: (jnp.maximum(n - 1, 0), 0))
    return pl.pallas_call(
        body, name="attention_fwd", grid=(nb,),
        in_specs=[pl.BlockSpec((WINDOW, qw + 2 * kvw), lambda n: (n, 0)),
                  pl.BlockSpec((WINDOW, 2 * kvw), lambda n: (jnp.maximum(n - 1, 0), (qw // (2 * kvw)))),
                  tbl, tbl, tbl_prev, tbl_prev, _VMEM],
        out_specs=pl.BlockSpec((WINDOW, qw), lambda n: (n, 0)),
        out_shape=jax.ShapeDtypeStruct((s, qw), F32),
        compiler_params=_params(("parallel",)),
    )(proj, proj, cos_t, sin_t, cos_t, sin_t, sinks)


def _conv_taps(cw_ref):
    return [jnp.concatenate([cw_ref[s, k:k + 1, :] for s in range(N_CHIPS)], axis=1) for k in range(3)]


def _shift_down(z, halo, steps):
    last = halo.shape[0]
    row = lax.broadcasted_iota(jnp.int32, z.shape, 0)
    out = pltpu.roll(z, steps, 0)
    for r in range(steps):
        out = jnp.where(row == r, halo[last - steps + r:last - steps + r + 1, :], out)
    return out


def _shift_up(z, halo, steps):
    rows = z.shape[0]
    row = lax.broadcasted_iota(jnp.int32, z.shape, 0)
    out = pltpu.roll(z, rows - steps, 0)
    for r in range(steps):
        out = jnp.where(row == rows - steps + r, halo[r:r + 1, :], out)
    return out


def _split_cbu(lo, hi, cw):
    lo, hi = lo.astype(F32), hi.astype(F32)
    c_gate = lo[:, :cw]
    b_gate = jnp.concatenate([lo[:, cw:], hi[:, :2 * cw - lo.shape[1]]], axis=1)
    u = hi[:, 2 * cw - lo.shape[1]:]
    return c_gate, b_gate, u


def _conv_norm(proj, attn, cw_full, g_ac):
    s, in_w = proj.shape
    cw = attn.shape[1]
    blk_w = in_w // 3
    tb = min(TB_CONV, s)

    def body(lo_ref, hi_ref, lo_h_ref, hi_h_ref, attn_ref, cw_ref, g_ref, mixed_ref, ac_ref, rstd_ref):
        i = pl.program_id(0)
        c_gate, b_gate, u = _split_cbu(lo_ref[...], hi_ref[...], cw)
        c_h, _, u_h = _split_cbu(lo_h_ref[...], hi_h_ref[...], cw)
        z = c_gate * u
        z_h = jnp.where(i == 0, 0.0, c_h * u_h)
        w0, w1, w2 = _conv_taps(cw_ref)
        y = w0 * _shift_down(z, z_h, 2) + w1 * _shift_down(z, z_h, 1) + w2 * z
        conv = b_gate * y
        a = attn_ref[...]
        r_a = lax.rsqrt(jnp.mean(a * a, axis=-1, keepdims=True) + RMS_EPS)
        r_c = lax.rsqrt(jnp.mean(conv * conv, axis=-1, keepdims=True) + RMS_EPS)
        g = g_ref[...]
        mixed_ref[...] = jnp.concatenate([a * r_a * g[:, :cw], conv * r_c * g[:, cw:]], axis=1).astype(BF16)
        ac_ref[...] = jnp.concatenate([a, conv], axis=1)
        rstd_ref[0] = r_a
        rstd_ref[1] = r_c

    halo_idx = lambda i: jnp.maximum(i * (tb // HALO_ROWS) - 1, 0)
    return pl.pallas_call(
        body, name="conv_norm", grid=(s // tb,),
        in_specs=[pl.BlockSpec((tb, blk_w), lambda i: (i, 1)),
                  pl.BlockSpec((tb, blk_w), lambda i: (i, 2)),
                  pl.BlockSpec((HALO_ROWS, blk_w), lambda i: (halo_idx(i), 1)),
                  pl.BlockSpec((HALO_ROWS, blk_w), lambda i: (halo_idx(i), 2)),
                  pl.BlockSpec((tb, cw), lambda i: (i, 0)),
                  _VMEM, _VMEM],
        out_specs=[pl.BlockSpec((tb, 2 * cw), lambda i: (i, 0)),
                   pl.BlockSpec((tb, 2 * cw), lambda i: (i, 0)),
                   pl.BlockSpec((2, tb, 1), lambda i: (0, i, 0))],
        out_shape=[jax.ShapeDtypeStruct((s, 2 * cw), BF16), jax.ShapeDtypeStruct((s, 2 * cw), F32),
                   jax.ShapeDtypeStruct((2, s, 1), F32)],
        compiler_params=_params(("parallel",)),
    )(proj, proj, proj, proj, attn, cw_full, g_ac)


def _out_proj_ln(mixed, w_out_g, x, ln_g, ln_b):
    s, d = mixed.shape
    tm = min(TM, s)
    tk = d
    nk = d // tk

    def body(a_ref, w_ref, x_ref, g_ref, b_ref, xhat_ref, h_ref, rstd_ref, acc):
        k = pl.program_id(1)
        _accumulate(acc, lambda: _dot(a_ref[...], w_ref[...]), k, nk)

        @pl.when(k == nk - 1)
        def _():
            def rows_fn(rows):
                xhat, rstd = _ln_fwd(ALPHA * x_ref[rows, :] + acc[rows, :])
                xhat_ref[rows, :] = xhat
                h_ref[rows, :] = (xhat * g_ref[...] + b_ref[...]).astype(BF16)
                rstd_ref[rows, :] = rstd

            _for_row_chunks(tm, rows_fn)

    row = pl.BlockSpec((tm, d), lambda i, k: (i, 0))
    return pl.pallas_call(
        body, name="out_proj_ln", grid=(s // tm, nk),
        in_specs=[pl.BlockSpec((tm, tk), lambda i, k: (i, k)),
                  pl.BlockSpec((tk, d), lambda i, k: (k, 0)),
                  pl.BlockSpec((None, tm, d), lambda i, k: (0, i, 0)),
                  _VMEM, _VMEM],
        out_specs=[row, row, pl.BlockSpec((tm, 1), lambda i, k: (i, 0))],
        out_shape=[jax.ShapeDtypeStruct((s, d), F32), jax.ShapeDtypeStruct((s, d), BF16),
                   jax.ShapeDtypeStruct((s, 1), F32)],
        scratch_shapes=[pltpu.VMEM((tm, d), F32)],
        compiler_params=_params(("parallel", "arbitrary")),
    )(mixed, w_out_g, x, ln_g, ln_b)


def _gate_up(h1, w_gu_g, first_vec, n_shards, into, name):
    s, d = h1.shape
    ns, _, fs2 = w_gu_g.shape
    fs = fs2 // 2
    tm = min(TM, s)

    def body(first_ref, h_ref, w_ref, act_in, ab_in, act_ref, ab_ref):
        gu = _dot(h_ref[...], w_ref[...])
        g, u = gu[:, :fs], gu[:, fs:]
        sg = _sigmoid(g)
        silu = g * sg
        act_ref[...] = (silu * u).astype(BF16)
        ab_ref[:, :fs] = (u * (sg * (1.0 + g * (1.0 - sg)))).astype(BF16)
        ab_ref[:, fs:] = silu.astype(BF16)

    shard = lambda j, first_ref: lax.rem(first_ref[0] + j, ns)
    grid_spec = pltpu.PrefetchScalarGridSpec(
        num_scalar_prefetch=1, grid=(s // tm, n_shards),
        in_specs=[pl.BlockSpec((tm, d), lambda i, j, first_ref: (i, 0)),
                  pl.BlockSpec((None, d, fs2), lambda i, j, first_ref: (shard(j, first_ref), 0, 0)), _ANY, _ANY],
        out_specs=[pl.BlockSpec((tm, fs), lambda i, j, first_ref: (i, shard(j, first_ref))),
                   pl.BlockSpec((tm, fs2), lambda i, j, first_ref: (i, shard(j, first_ref)))])
    return pl.pallas_call(
        body, name=name, grid_spec=grid_spec,
        out_shape=[jax.ShapeDtypeStruct((s, ns * fs), BF16), jax.ShapeDtypeStruct((s, ns * fs2), BF16)],
        input_output_aliases={} if into is None else {3: 0, 4: 1},
        compiler_params=_params(("parallel", "arbitrary")),
    )(first_vec, h1, w_gu_g, *((first_vec, first_vec) if into is None else into))


def _down_ln_loss(act, w_down_g, xhat1, ln1_g, ln1_b, ln2_g, ln2_b, target):
    s, f = act.shape
    d = xhat1.shape[1]
    tm = min(TM, s)
    tk = f // N_CHIPS
    nk = f // tk

    def body(a_ref, w_ref, xh_ref, g1_ref, b1_ref, g2_ref, b2_ref, t_ref, dpre_ref, dpre16_ref, loss_ref, gg_ref, gb_ref,
             acc):
        i, k = pl.program_id(0), pl.program_id(1)
        _accumulate(acc, lambda: _dot(a_ref[...], w_ref[...]), k, nk)

        @pl.when(k == nk - 1)
        def _():
            @pl.when(i == 0)
            def _():
                loss_ref[...] = jnp.zeros_like(loss_ref)
                gg_ref[...] = jnp.zeros_like(gg_ref)
                gb_ref[...] = jnp.zeros_like(gb_ref)

            def rows_fn(rows):
                h1 = xh_ref[rows, :] * g1_ref[...] + b1_ref[...]
                xhat, rstd = _ln_fwd(ALPHA * h1 + acc[rows, :])
                g2 = g2_ref[...]
                diff = xhat * g2 + b2_ref[...] - t_ref[rows, :]
                dy = diff * (1.0 / d)
                dpre = _ln_bwd(dy, xhat, rstd, g2)
                dpre_ref[rows, :] = dpre
                dpre16_ref[rows, :] = dpre.astype(BF16)
                sq = jnp.sum(jnp.sum(diff * diff, axis=1, keepdims=True), axis=0, keepdims=True)
                loss_ref[...] += jnp.broadcast_to(sq * (0.5 / d), (1, 128))
                gg_ref[...] += jnp.sum(dy * xhat, axis=0, keepdims=True)
                gb_ref[...] += jnp.sum(dy, axis=0, keepdims=True)

            _for_row_chunks(tm, rows_fn)

    row = pl.BlockSpec((tm, d), lambda i, k: (i, 0))
    vec = pl.BlockSpec((1, d), lambda i, k: (0, 0))
    return pl.pallas_call(
        body, name="down_ln_loss", grid=(s // tm, nk),
        in_specs=[pl.BlockSpec((tm, tk), lambda i, k: (i, k)),
                  pl.BlockSpec((tk, d), lambda i, k: (k, 0)),
                  row, _VMEM, _VMEM, _VMEM, _VMEM,
                  pl.BlockSpec((None, tm, d), lambda i, k: (0, i, 0))],
        out_specs=[row, row, pl.BlockSpec((1, 128), lambda i, k: (0, 0)), vec, vec],
        out_shape=[jax.ShapeDtypeStruct((s, d), F32), jax.ShapeDtypeStruct((s, d), BF16),
                   jax.ShapeDtypeStruct((1, 128), F32), jax.ShapeDtypeStruct((1, d), F32),
                   jax.ShapeDtypeStruct((1, d), F32)],
        scratch_shapes=[pltpu.VMEM((tm, d), F32)],
        compiler_params=_params(("arbitrary", "arbitrary")),
    )(act, w_down_g, xhat1, ln1_g, ln1_b, ln2_g, ln2_b, target)


def _dact_silu_bwd(dpre2, w_down_g, ab):
    s, d = dpre2.shape
    fs2 = ab.shape[1] // N_CHIPS
    fs = fs2 // 2
    tm = min(TM, s)

    def body(dp_ref, w_ref, ab_ref, dgu_ref):
        d_act = _dot_nt(dp_ref[...], w_ref[...])
        dgu_ref[:, :fs] = (d_act * ab_ref[:, :fs].astype(F32)).astype(BF16)
        dgu_ref[:, fs:] = (d_act * ab_ref[:, fs:].astype(F32)).astype(BF16)

    blk = pl.BlockSpec((tm, fs2), lambda j, i: (i, j))
    return pl.pallas_call(
        body, name="dact_silu_bwd", grid=(N_CHIPS, s // tm),
        in_specs=[pl.BlockSpec((tm, d), lambda j, i: (i, 0)),
                  pl.BlockSpec((fs, d), lambda j, i: (j, 0)), blk],
        out_specs=blk,
        out_shape=jax.ShapeDtypeStruct(ab.shape, BF16),
        compiler_params=_params(("parallel", "parallel")),
    )(dpre2, w_down_g, ab)


def _grad_rows(a, b, after, name, row_blocks=1):
    s, m = a.shape
    n = b.shape[1]
    ms = m // N_CHIPS
    tmw = ms // row_blocks
    tk = min(TK_TOK, s)
    nk = s // tk

    def body(a_ref, b_ref, after_ref, o_ref, acc):
        k = pl.program_id(2)
        _accumulate(acc, lambda: _dot_tn(a_ref[...].astype(BF16), b_ref[...].astype(BF16)), k, nk)

        @pl.when(k == nk - 1)
        def _():
            o_ref[...] = acc[...].astype(BF16)

    return pl.pallas_call(
        body, name=name, grid=(N_CHIPS, row_blocks, nk),
        in_specs=[pl.BlockSpec((tk, tmw), lambda j, r, k: (k, j * row_blocks + r)),
                  pl.BlockSpec((tk, n), lambda j, r, k: (k, 0)), _ANY],
        out_specs=pl.BlockSpec((None, tmw, n), lambda j, r, k: (j, r, 0)),
        out_shape=jax.ShapeDtypeStruct((N_CHIPS, ms, n), BF16),
        scratch_shapes=[pltpu.VMEM((tmw, n), F32)],
        compiler_params=_params(("parallel", "parallel", "arbitrary")),
    )(a, b, after)


def _grad_cols(a, bs, after, name, a_3d=False, row_blocks=2, shard=None):
    s, m = a.shape[-2:]
    n = bs[0].shape[1]
    ns = n // N_CHIPS
    nb = len(bs)
    tmw = m // row_blocks
    tk = min(TK_TOK, s)
    nk = s // tk

    def body(*refs):
        a_ref, b_refs, o_refs, accs = refs[0], refs[1:1 + nb], refs[2 + nb:2 + 2 * nb], refs[2 + 2 * nb:]
        k = pl.program_id(2)
        for b_ref, acc in zip(b_refs, accs):
            _accumulate(acc, lambda b_ref=b_ref: _dot_tn(a_ref[...].astype(BF16), b_ref[...].astype(BF16)), k, nk)

        @pl.when(k == nk - 1)
        def _():
            for o_ref, acc in zip(o_refs, accs):
                o_ref[...] = acc[...].astype(BF16)

    if a_3d:
        a_spec = pl.BlockSpec((None, tk, tmw), lambda j, r, k: (0, k, r))
    else:
        a_spec = pl.BlockSpec((tk, tmw), lambda j, r, k: (k, r))
    return _call_with_adamw(
        body, name, (N_CHIPS, row_blocks, nk),
        [a_spec] + [pl.BlockSpec((tk, ns), lambda j, r, k: (k, j))] * nb + [_ANY],
        [pl.BlockSpec((None, tmw, ns), lambda j, r, k: (j, r, 0))] * nb,
        [jax.ShapeDtypeStruct((N_CHIPS, m, ns), BF16)] * nb,
        [pltpu.VMEM((tmw, ns), F32)] * nb, ("parallel", "parallel", "arbitrary"), (a, *bs, after), shard)


def _dh1_ln_bwd(d_gu, w_gu_g, dpre2, xhat1, rstd1, ln1_g, after):
    s = d_gu.shape[0]
    d = dpre2.shape[1]
    hd = d // 2
    fs = w_gu_g.shape[2]
    tm = min(TM, s)

    def body(dgu_ref, w_ref, dp2_ref, xh_ref, rs_ref, g_ref, after_ref, dpre_ref, dpre16_ref, gg_ref, gb_ref, acc_lo,
             acc_hi):
        i, j, half = pl.program_id(0), pl.program_id(1), pl.program_id(2)

        def product():
            return _dot_nt(dgu_ref[...], w_ref[...])

        @pl.when(half == 0)
        def _():
            _accumulate(acc_lo, product, j, N_CHIPS)

        @pl.when(half == 1)
        def _():
            _accumulate(acc_hi, product, j, N_CHIPS)

        @pl.when((j == N_CHIPS - 1) & (half == 1))
        def _():
            @pl.when(i == 0)
            def _():
                gg_ref[...] = jnp.zeros_like(gg_ref)
                gb_ref[...] = jnp.zeros_like(gb_ref)

            def rows_fn(rows):
                dh = jnp.concatenate([acc_lo[rows, :], acc_hi[rows, :]], axis=1) + ALPHA * dp2_ref[rows, :]
                xhat = xh_ref[rows, :]
                dpre = _ln_bwd(dh, xhat, rs_ref[rows, :], g_ref[...])
                dpre_ref[rows, :] = dpre
                dpre16_ref[rows, :] = dpre.astype(BF16)
                gg_ref[...] += jnp.sum(dh * xhat, axis=0, keepdims=True)
                gb_ref[...] += jnp.sum(dh, axis=0, keepdims=True)

            _for_row_chunks(tm, rows_fn)

    row = pl.BlockSpec((tm, d), lambda i, j, h: (i, 0))
    vec = pl.BlockSpec((1, d), lambda i, j, h: (0, 0))
    act_blk = pl.BlockSpec((tm, fs), lambda i, j, h: (i, j))
    w_blk = pl.BlockSpec((None, hd, fs), lambda i, j, h: (j, h, 0))
    return pl.pallas_call(
        body, name="dh1_ln_bwd", grid=(s // tm, N_CHIPS, 2),
        in_specs=[act_blk, w_blk, row, row, pl.BlockSpec((tm, 1), lambda i, j, h: (i, 0)), _VMEM, _ANY],
        out_specs=[row, row, vec, vec],
        out_shape=[jax.ShapeDtypeStruct((s, d), F32), jax.ShapeDtypeStruct((s, d), BF16),
                   jax.ShapeDtypeStruct((1, d), F32), jax.ShapeDtypeStruct((1, d), F32)],
        scratch_shapes=[pltpu.VMEM((tm, hd), F32)] * 2,
        compiler_params=_params(("arbitrary", "arbitrary", "arbitrary")),
    )(d_gu, w_gu_g, dpre2, xhat1, rstd1, ln1_g, after)


def _dmixed_rms_bwd(dpre1, w_out_g, ac, rstd, g_ac):
    s, d = dpre1.shape
    hd = d // 2
    tm = min(TM, s)

    def body(dp_ref, w_ref, ac_ref, rs_ref, g_ref, dac_ref, gg_ref):
        i = pl.program_id(1)
        dm = _dot_nt(dp_ref[...].astype(BF16), w_ref[...])
        pre = ac_ref[...]
        r = rs_ref[...]
        gdm = dm * g_ref[...]
        dac_ref[...] = r * gdm - pre * (r * r * r) * jnp.mean(gdm * pre, axis=-1, keepdims=True)
        gg = jnp.sum(dm * pre * r, axis=0, keepdims=True)

        @pl.when(i == 0)
        def _():
            gg_ref[...] = gg

        @pl.when(i > 0)
        def _():
            gg_ref[...] += gg

    return pl.pallas_call(
        body, name="dmixed_rms_bwd", grid=(2, s // tm),
        in_specs=[pl.BlockSpec((tm, d), lambda h, i: (i, 0)),
                  pl.BlockSpec((hd, d), lambda h, i: (h, 0)),
                  pl.BlockSpec((tm, hd), lambda h, i: (i, h)),
                  pl.BlockSpec((None, tm, 1), lambda h, i: (h, i, 0)),
                  pl.BlockSpec((1, hd), lambda h, i: (0, h))],
        out_specs=[pl.BlockSpec((tm, hd), lambda h, i: (i, h)),
                   pl.BlockSpec((1, hd), lambda h, i: (0, h))],
        out_shape=[jax.ShapeDtypeStruct((s, d), F32), jax.ShapeDtypeStruct((1, d), F32)],
        compiler_params=_params(("arbitrary", "arbitrary")),
    )(dpre1, w_out_g, ac, rstd, g_ac)


def _attention_bwd(proj, d_ac, cos_t, sin_t, sinks, after, shard):
    s = proj.shape[0]
    qw = GROUP * N_KV_HEADS * HEAD_DIM
    kvw = N_KV_HEADS * HEAD_DIM
    nb = s // WINDOW
    nq = GROUP * N_KV_HEADS

    def body(cur_ref, prev_ref, do_ref, cos_ref, sin_ref, cosp_ref, sinp_ref, sinks_ref, after_ref,
             dq_ref, dcur_ref, dprev_ref, dsink_ref):
        n = pl.program_id(0)
        first = n == 0
        q, k_all, v_all, cos_q, sin_q = _roped_qkv(cur_ref, prev_ref, cos_ref, sin_ref, cosp_ref, sinp_ref, qw, kvw)
        kk2s = [_pair_operand(k_all, h) for h in range(N_KV_HEADS)]
        vv2s = [_pair_operand(v_all, h) for h in range(N_KV_HEADS)]
        qps, probs, p_sinks = _all_probs(q, kk2s, first, sinks_ref)
        dops = [do_ref[:, pair * PAIR:(pair + 1) * PAIR].astype(BF16) for pair in range(N_PAIRS)]
        d_probs = jnp.concatenate([_dot_nt(dops[pair], vv2s[pair // (GROUP // 2)]) for pair in range(N_PAIRS)], axis=0)
        d_s, ds_sinks = [], []
        for t in range(2):
            cols = slice(t * KEYS, (t + 1) * KEYS)
            delta = jnp.sum(probs[:, cols] * d_probs[:, cols], axis=1, keepdims=True)
            d_s.append(probs[:, cols] * (d_probs[:, cols] - delta))
            ds_sinks.append(-p_sinks[t] * delta)
        d_s = jnp.concatenate(d_s, axis=1).astype(BF16)
        probs = probs.astype(BF16)
        dq_parts, dk_tiles, dv_tiles, dsink_parts = [], [], [], []
        for h in range(N_KV_HEADS):
            dkk2, dvv2 = None, None
            for p in range(GROUP // 2):
                pair = (GROUP // 2) * h + p
                rows = slice(pair * WINDOW, (pair + 1) * WINDOW)
                dq_parts.append(_dot(d_s[rows], kk2s[h]) * ATTN_SCALE)
                dk_term = _dot_tn(d_s[rows], qps[pair])
                dv_term = _dot_tn(probs[rows], dops[pair])
                dkk2 = dk_term if dkk2 is None else dkk2 + dk_term
                dvv2 = dv_term if dvv2 is None else dvv2 + dv_term
                dsink_parts.extend([jnp.sum(ds_sinks[t][rows], axis=0, keepdims=True) for t in range(2)])
            dk_tiles.append(_pair_grad(dkk2, h))
            dv_tiles.append(_pair_grad(dvv2, h))
        dq_ref[...] = _rope(jnp.concatenate(dq_parts, axis=1), cos_q, sin_q, -1.0)
        dk = jnp.concatenate([dk_tiles[0] + dk_tiles[1], dk_tiles[2] + dk_tiles[3]], axis=1)
        dv = jnp.concatenate([dv_tiles[0] + dv_tiles[1], dv_tiles[2] + dv_tiles[3]], axis=1)
        dprev_ref[...] = jnp.concatenate([dk[:WINDOW], dv[:WINDOW]], axis=1)
        dcur_ref[...] = jnp.concatenate([dk[WINDOW:], dv[WINDOW:]], axis=1)
        dsink = jnp.concatenate(dsink_parts, axis=1)

        @pl.when(first)
        def _():
            dsink_ref[...] = dsink

        @pl.when(n > 0)
        def _():
            dsink_ref[...] += dsink

    tbl = pl.BlockSpec((WINDOW, kvw), lambda n: (n, 0))
    tbl_prev = pl.BlockSpec((WINDOW, kvw), lambda n: (jnp.maximum(n - 1, 0), 0))
    kv_blk = pl.BlockSpec((WINDOW, 2 * kvw), lambda n: (n, 0))
    return _call_with_adamw(
        body, "attention_bwd", (nb,),
        [pl.BlockSpec((WINDOW, qw + 2 * kvw), lambda n: (n, 0)),
         pl.BlockSpec((WINDOW, 2 * kvw), lambda n: (jnp.maximum(n - 1, 0), (qw // (2 * kvw)))),
         pl.BlockSpec((WINDOW, qw), lambda n: (n, 0)),
         tbl, tbl, tbl_prev, tbl_prev, _VMEM, _ANY],
        [pl.BlockSpec((WINDOW, qw), lambda n: (n, 0)), kv_blk, kv_blk, pl.BlockSpec((1, nq), lambda n: (0, 0))],
        [jax.ShapeDtypeStruct((s, qw), F32), jax.ShapeDtypeStruct((s, 2 * kvw), F32),
         jax.ShapeDtypeStruct((s, 2 * kvw), F32), jax.ShapeDtypeStruct((1, nq), F32)],
        [], ("arbitrary",), (proj, proj, d_ac, cos_t, sin_t, cos_t, sin_t, sinks, after), shard)


def _dproj_assemble(proj, d_ac, dq, dkv_cur, dkv_prev, cos_t, sin_t, cw_full):
    s, in_w = proj.shape
    cw = dq.shape[1]
    kvw = N_KV_HEADS * HEAD_DIM
    blk_w = in_w // 3
    tb = WINDOW
    nb = s // tb

    def body(lo_ref, hi_ref, lo_p_ref, hi_p_ref, lo_n_ref, hi_n_ref, dconv_ref, dconv_n_ref,
             dq_ref, dcur_ref, dprev_n_ref, cos_ref, sin_ref, cw_ref, dproj_ref, gcw_ref):
        i = pl.program_id(0)
        last = i == nb - 1
        c_gate, b_gate, u = _split_cbu(lo_ref[...], hi_ref[...], cw)
        c_p, _, u_p = _split_cbu(lo_p_ref[...], hi_p_ref[...], cw)
        _, b_n, _ = _split_cbu(lo_n_ref[...], hi_n_ref[...], cw)
        z = c_gate * u
        z_p = jnp.where(i == 0, 0.0, c_p * u_p)
        z1 = _shift_down(z, z_p, 1)
        z2 = _shift_down(z, z_p, 2)
        w0, w1, w2 = _conv_taps(cw_ref)
        y = w0 * z2 + w1 * z1 + w2 * z
        d_conv = dconv_ref[...]
        d_b = d_conv * y
        d_y = d_conv * b_gate
        d_y_n = jnp.where(last, 0.0, dconv_n_ref[...] * b_n[:dconv_n_ref.shape[0]])
        d_z = w2 * d_y + w1 * _shift_up(d_y, d_y_n, 1) + w0 * _shift_up(d_y, d_y_n, 2)
        d_c = d_z * u
        d_u = d_z * c_gate
        gcw = jnp.concatenate([jnp.sum(d_y * z2, axis=0, keepdims=True), jnp.sum(d_y * z1, axis=0, keepdims=True),
                               jnp.sum(d_y * z, axis=0, keepdims=True)], axis=0)

        @pl.when(i == 0)
        def _():
            gcw_ref[...] = gcw

        @pl.when(i > 0)
        def _():
            gcw_ref[...] += gcw

        dkv = dcur_ref[...] + jnp.where(last, 0.0, dprev_n_ref[...])
        dk = _rope(dkv[:, :kvw], cos_ref[...], sin_ref[...], -1.0)
        dproj_ref[...] = jnp.concatenate([dq_ref[...], dk, dkv[:, kvw:], d_c, d_b, d_u], axis=1).astype(BF16)

    prev_halo = lambda i: jnp.maximum(i * (tb // HALO_ROWS) - 1, 0)
    next_halo = lambda i: jnp.minimum((i + 1) * (tb // HALO_ROWS), s // HALO_ROWS - 1)
    next8 = lambda i: jnp.minimum((i + 1) * (tb // 8), s // 8 - 1)
    nxt = lambda i: jnp.minimum(i + 1, nb - 1)
    return pl.pallas_call(
        body, name="dproj_assemble", grid=(nb,),
        in_specs=[pl.BlockSpec((tb, blk_w), lambda i: (i, 1)),
                  pl.BlockSpec((tb, blk_w), lambda i: (i, 2)),
                  pl.BlockSpec((HALO_ROWS, blk_w), lambda i: (prev_halo(i), 1)),
                  pl.BlockSpec((HALO_ROWS, blk_w), lambda i: (prev_halo(i), 2)),
                  pl.BlockSpec((HALO_ROWS, blk_w), lambda i: (next_halo(i), 1)),
                  pl.BlockSpec((HALO_ROWS, blk_w), lambda i: (next_halo(i), 2)),
                  pl.BlockSpec((tb, cw), lambda i: (i, 1)),
                  pl.BlockSpec((8, cw), lambda i: (next8(i), 1)),
                  pl.BlockSpec((tb, cw), lambda i: (i, 0)),
                  pl.BlockSpec((tb, 2 * kvw), lambda i: (i, 0)),
                  pl.BlockSpec((tb, 2 * kvw), lambda i: (nxt(i), 0)),
                  pl.BlockSpec((tb, kvw), lambda i: (i, 0)),
                  pl.BlockSpec((tb, kvw), lambda i: (i, 0)),
                  _VMEM],
        out_specs=[pl.BlockSpec((tb, in_w), lambda i: (i, 0)),
                   pl.BlockSpec((3, cw), lambda i: (0, 0))],
        out_shape=[jax.ShapeDtypeStruct((s, in_w), BF16), jax.ShapeDtypeStruct((3, cw), F32)],
        compiler_params=_params(("arbitrary",)),
    )(proj, proj, proj, proj, proj, proj, d_ac, d_ac, dq, dkv_cur, dkv_prev, cos_t, sin_t, cw_full)


def _dx(d_proj, w_in_g, dpre1, after, shard):
    s, in_w = d_proj.shape
    ns, d, ncol = w_in_g.shape
    tm = min(TM, s)

    def body(dp_ref, w_ref, r_ref, after_ref, o_ref, acc):
        j = pl.program_id(1)
        _accumulate(acc, lambda: _dot_nt(dp_ref[...], w_ref[...]), j, ns)

        @pl.when(j == ns - 1)
        def _():
            o_ref[...] = acc[...] + ALPHA * r_ref[...]

    return _call_with_adamw(
        body, "dx", (s // tm, ns),
        [pl.BlockSpec((tm, ncol), lambda i, j: (i, j)),
         pl.BlockSpec((None, d, ncol), lambda i, j: (j, 0, 0)),
         pl.BlockSpec((tm, d), lambda i, j: (i, 0)), _ANY],
        [pl.BlockSpec((None, tm, d), lambda i, j: (0, i, 0))], [jax.ShapeDtypeStruct((1, s, d), F32)],
        [pltpu.VMEM((tm, d), F32)], ("parallel", "arbitrary"), (d_proj, w_in_g, dpre1, after), shard)


def kernel(x, positions, w_in, conv_w, sinks, g_attn, g_conv, w_out, ln1_g, ln1_b, w_gate, w_up, w_down, ln2_g, ln2_b, loss_target, m_w_in, m_conv_w, m_sinks, m_g_attn, m_g_conv, m_w_out, m_ln1_g, m_ln1_b, m_w_gate, m_w_up, m_w_down, m_ln2_g, m_ln2_b, v_w_in, v_conv_w, v_sinks, v_g_attn, v_g_conv, v_w_out, v_ln1_g, v_ln1_b, v_w_gate, v_w_up, v_w_down, v_ln2_g, v_ln2_b):
    s = x.shape[1]
    d = x.shape[2]

    chip_vec = _chip_id(lax.axis_index("x"), lax.axis_index("y")).astype(jnp.int32).reshape(1)
    wnames = ["w_in", "w_out", "w_gu", "w_down"]
    buf_in = _cast_weight(w_in, chip_vec, chip_vec, "cast_w_in")
    flight_in, token_in = _gather_start([buf_in], chip_vec, "gather_start_w_in")
    cw_buf = lax.dynamic_update_slice(jnp.zeros((N_CHIPS,) + conv_w.shape[1:], F32), conv_w, (chip_vec[0], 0, 0))
    cw_flight = _flight_start("conv_w_start", [cw_buf], _conv_w_plan(), 3, token_in)
    started = cw_flight[2][0]
    buf_gu = _cast_weight(w_gate, chip_vec, started, "cast_w_gate", 0, 2)
    buf_gu = _cast_weight(w_up, chip_vec, buf_gu, "cast_w_up", 1, 2)
    bufs = [_cast_weight(w_out, chip_vec, started, "cast_w_out"), buf_gu,
            _cast_weight(w_down, chip_vec, started, "cast_w_down")]
    flights_rest, token = _gather_start(bufs, token_in, "gather_start_rest")
    flights = flight_in + flights_rest

    def gathered(i, after):
        send_sems, recv_sems, buf = flights[i]
        buf = _gather_wait(send_sems, recv_sems, buf, after, "gather_wait_" + wnames[i])
        return _sibling_fill(buf, "sibling_fill_" + wnames[i])

    g_ac = jnp.concatenate([g_attn, g_conv], axis=1)

    proj_own, x16 = _in_proj(x, _after(flights[0][2], token), chip_vec, 1, None, "in_proj_own")
    cos_t, sin_t = _rope_tables(positions.reshape(s, 1) + token[0:1, 0:1].astype(jnp.int32))
    w_in_g = gathered(0, _after(cos_t, proj_own))
    (proj,) = _in_proj(x16, w_in_g, chip_vec + 1, N_CHIPS - 1, proj_own, "in_proj_rest")
    send_sems, recv_sems, buf_out = flights[1]
    buf_out = _gather_wait(send_sems, recv_sems, buf_out, proj, "gather_wait_w_out")
    fill_out = _flight_start("fill_start_w_out", [buf_out], _fill_plan(1), 3, chip_vec)
    attn = _attention_fwd(_after(proj, fill_out[2][0]), cos_t, sin_t, sinks)
    (cw_full,) = _flight_wait("conv_w_wait", cw_flight, _conv_w_plan(), attn)
    mixed, ac, rstd_ac = _conv_norm(proj, attn, cw_full, g_ac)
    (w_out_g,) = _flight_wait("fill_wait_w_out", fill_out, _fill_plan(1), mixed)
    w_out_full = w_out_g.reshape(d, d)
    xhat1, h1, rstd1 = _out_proj_ln(mixed, w_out_full, x, ln1_g, ln1_b)
    send_sems, recv_sems, buf_gu = flights[2]
    buf_gu = _gather_wait(send_sems, recv_sems, buf_gu, h1, "gather_wait_w_gu")
    fill_gu = _flight_start("fill_start_w_gu", [buf_gu], _fill_plan(1), 3, chip_vec)
    own = _gate_up(h1, fill_gu[2][0], chip_vec, 1, None, "gate_up_own")
    (w_gu_g,) = _flight_wait("fill_wait_w_gu", fill_gu, _fill_plan(1), own[0])
    some = _gate_up(h1, w_gu_g, chip_vec + 1, N_CHIPS - 2, own, "gate_up_rest")
    send_sems, recv_sems, buf_down = flights[3]
    buf_down = _gather_wait(send_sems, recv_sems, buf_down, some[0], "gather_wait_w_down")
    fill_down = _flight_start("fill_start_w_down", [buf_down], _fill_plan(1), 3, chip_vec)
    act, ab = _gate_up(h1, _after(w_gu_g, fill_down[2][0]), chip_vec + N_CHIPS - 1, 1, some, "gate_up_last")
    (w_down_g,) = _flight_wait("fill_wait_w_down", fill_down, _fill_plan(1), act)
    w_down_full = w_down_g.reshape(-1, d)
    dpre2, dpre2_16, loss_part, g_ln2_g, g_ln2_b = _down_ln_loss(act, w_down_full, xhat1, ln1_g, ln1_b, ln2_g, ln2_b,
                                                                 loss_target)

    cvec = lax.axis_index("c").astype(jnp.int32).reshape(1)

    def exchange_begin(parts, nme):
        bufs = []
        for part in parts:
            ns, r, cdim = part.shape
            bufs.extend([part, lax.empty((ns, r // 2, cdim), part.dtype)])
        return _flight_start("exchange_start_" + nme, bufs, _exchange_plan(len(parts)), len(parts), cvec)

    def exchange_end(flight, n_parts, after, nme):
        bufs = _flight_wait("exchange_wait_" + nme, flight, _exchange_plan(n_parts), after)
        return [(bufs[2 * w], bufs[2 * w + 1]) for w in range(n_parts)]

    def scatter_begin(part, got, nme):
        return _scatter_start(_add_halves(part, got, cvec, "add_halves_" + nme), "scatter_start_" + nme)

    d_gu = _dact_silu_bwd(dpre2_16, w_down_full, ab)
    p_down = _grad_rows(act, dpre2_16, d_gu, "grad_w_down")
    x_down = exchange_begin([p_down], "w_down")
    (p_gu,) = _grad_cols(h1, [d_gu], x_down[2][0], "grad_w_gate_up")
    ((p_down, got),) = exchange_end(x_down, 1, p_gu, "w_down")
    f_down = scatter_begin(p_down, got, "w_down")
    x_gu = exchange_begin([_after(p_gu, f_down[2])], "w_gu")
    dpre1, dpre1_16, g_ln1_g, g_ln1_b = _dh1_ln_bwd(d_gu, w_gu_g, dpre2, xhat1, rstd1, ln1_g, x_gu[2][0])
    ((p_gu, got),) = exchange_end(x_gu, 1, dpre1, "w_gu")
    f_gu = scatter_begin(p_gu, got, "w_gu")
    d_ac, g_g_ac = _dmixed_rms_bwd(_after(dpre1_16, f_gu[2]), w_out_full, ac, rstd_ac, g_ac)
    pos_vec = jnp.concatenate([chip_vec, cvec])
    sums, land = _scatter_wait(*f_down, d_ac, "scatter_wait_w_down")
    c_down = _flight_start("complete_start_w_down", [sums, land], _complete_plan(1), 4, cvec)
    p_out = _grad_rows(mixed, dpre1_16, c_down[2][1], "grad_w_out")
    x_out = exchange_begin([p_out], "w_out")
    sums, land = _flight_wait("complete_wait_w_down", c_down, _complete_plan(1), x_out[2][0])
    dq, dkv_cur, dkv_prev, g_sinks, *new_w_down = _attention_bwd(
        proj, d_ac, cos_t, sin_t, sinks, x_out[2][0], (w_down, m_w_down, v_w_down, land, sums, pos_vec, 0))
    ((p_out, got),) = exchange_end(x_out, 1, dq, "w_out")
    f_out = scatter_begin(p_out, got, "w_out")
    sums, land = _scatter_wait(*f_gu, f_out[2], "scatter_wait_w_gu")
    c_gu = _flight_start("complete_start_w_gu", [sums, land], _complete_plan(1), 4, cvec)
    d_proj, g_conv_w = _dproj_assemble(proj, _after(d_ac, c_gu[2][1]), dq, dkv_cur, dkv_prev, cos_t, sin_t, cw_full)
    small_parts = _small_pack(g_ln2_g, g_ln2_b, g_ln1_g, g_ln1_b, g_g_ac, g_conv_w, g_sinks, loss_part)
    f_small = _flight_start("small_start", [small_parts], _small_plan(), N_DEVICES - 1, cvec)
    sums_gu, land_gu = _flight_wait("complete_wait_w_gu", c_gu, _complete_plan(1), f_small[2][0])
    p_in, *new_w_gate = _grad_cols(x16, [d_proj], f_small[2][0], "grad_w_in", a_3d=True,
                                   shard=(w_gate, m_w_gate, v_w_gate, land_gu, sums_gu, pos_vec, 0))
    (small_parts,) = _flight_wait("small_wait", f_small, _small_plan(), p_in)
    red = _small_sum(small_parts)
    x_in = exchange_begin([_after(p_in, red)], "w_in")
    sums, land = _scatter_wait(*f_out, x_in[2][0], "scatter_wait_w_out")
    c_out = _flight_start("complete_start_w_out", [sums, land], _complete_plan(1), 4, cvec)
    new_w_up = _adamw_shard(w_up, m_w_up, v_w_up, _after(land_gu, c_out[2][1]), sums_gu, pos_vec, "adamw_w_up", 1)
    ((p_in, got),) = exchange_end(x_in, 1, new_w_up[0], "w_in")
    f_in = scatter_begin(p_in, got, "w_in")
    (grad_x,) = _dx(d_proj, w_in_g, dpre1, f_in[2], None)

    big = {"w_down": new_w_down, "w_gate": new_w_gate, "w_up": new_w_up}
    sums, land = _scatter_wait(*f_in, grad_x, "scatter_wait_w_in")
    c_in = _flight_start("complete_start_w_in", [sums, land], _complete_plan(1), 4, cvec)
    sums, land = _flight_wait("complete_wait_w_out", c_out, _complete_plan(1), c_in[2][1])
    big["w_out"] = _adamw_shard(w_out, m_w_out, v_w_out, land, sums, pos_vec, "adamw_w_out")
    sums, land = _flight_wait("complete_wait_w_in", c_in, _complete_plan(1), big["w_out"][0])
    big["w_in"] = _adamw_shard(w_in, m_w_in, v_w_in, land, sums, pos_vec, "adamw_w_in")
    small = _adamw_small(red, {
        "sinks": (sinks, m_sinks, v_sinks), "g_attn": (g_attn, m_g_attn, v_g_attn),
        "g_conv": (g_conv, m_g_conv, v_g_conv), "ln1_g": (ln1_g, m_ln1_g, v_ln1_g),
        "ln1_b": (ln1_b, m_ln1_b, v_ln1_b), "ln2_g": (ln2_g, m_ln2_g, v_ln2_g),
        "ln2_b": (ln2_b, m_ln2_b, v_ln2_b), "conv_w": (conv_w, m_conv_w, v_conv_w)})
    res = {**big, **small}
    order = ["w_in", "conv_w", "sinks", "g_attn", "g_conv", "w_out", "ln1_g", "ln1_b", "w_gate", "w_up", "w_down",
             "ln2_g", "ln2_b"]
    loss = red[6, d // 2 + 128]
    return (loss, grad_x, *[res[n][0] for n in order], *[res[n][1] for n in order],
            *[res[n][2] for n in order], *[res[n][3] for n in order])
```

```python
import functools

import numpy as np
import jax
import jax.numpy as jnp
from jax import lax
from jax.experimental import pallas as pl
from jax.experimental.pallas import tpu as pltpu

F32 = jnp.float32
BF16 = jnp.bfloat16
MESH = pl.DeviceIdType.MESH

HEAD_DIM = 64
N_KV_HEADS = 4
GROUP = 4
WINDOW = 128
ROT_DIM = 16
ROPE_THETA = 500000.0
ATTN_SCALE = HEAD_DIM ** -0.5
ALPHA = 2.0 ** 0.25
LN_EPS = 1e-5
RMS_EPS = 1e-6
ADAM_LR = 0.001
ADAM_B1 = 0.9
ADAM_B2 = 0.999
ADAM_EPS = 1e-08
ADAM_WD = 0.01
ADAM_STEP = 10
N_CHIPS = 4
NEG_BIG = -1e30

V7X_VMEM_BYTES = 64 * 1024 * 1024
VMEM_LIMIT = V7X_VMEM_BYTES - 6 * 1024 * 1024

TM = 512
TK_TOK = 1024
TB_CONV = 512
TR_ELT = 256
ROW_CHUNK = 128
HALO_ROWS = 16


def _params(sem):
    return pltpu.CompilerParams(dimension_semantics=sem, vmem_limit_bytes=VMEM_LIMIT)


def _row_tile(rows, target):
    best = None
    for t in range(16, min(rows, target) + 1, 16):
        if rows % t == 0:
            best = t
    assert best is not None, (rows, target)
    return best


def _dot(a, b):
    return jnp.dot(a, b, preferred_element_type=F32)


def _dot_nt(a, b):
    return lax.dot_general(a, b, (((1,), (1,)), ((), ())), preferred_element_type=F32)


def _dot_tn(a, b):
    return lax.dot_general(a, b, (((0,), (0,)), ((), ())), preferred_element_type=F32)


def _mesh_pos():
    x, y, c = lax.axis_index("x"), lax.axis_index("y"), lax.axis_index("c")
    chips = [(1 - x, y), (x, 1 - y), (1 - x, 1 - y)]
    return x, y, c, chips


def _chip_id(px, py):
    return 2 * px + py


def _rope(t, cos, sgn_sin, sign):
    w = t.shape[1]
    lane = lax.broadcasted_iota(jnp.int32, t.shape, 1) & (HEAD_DIM - 1)
    partner = jnp.where(lane < ROT_DIM // 2, pltpu.roll(t, w - ROT_DIM // 2, 1), pltpu.roll(t, ROT_DIM // 2, 1))
    return t * cos + sign * (partner * sgn_sin)


def _tile_lanes(t, n):
    return jnp.concatenate([t] * n, axis=1)


def _sigmoid(g):
    return 1.0 / (1.0 + jnp.exp(-g))


def _for_row_chunks(n_rows, fn):
    def step(r, carry):
        fn(pl.ds(pl.multiple_of(r * ROW_CHUNK, ROW_CHUNK), ROW_CHUNK))
        return carry

    lax.fori_loop(0, n_rows // ROW_CHUNK, step, 0)


def _accumulate(acc, make_val, k, nk):
    if nk == 1:
        acc[...] = make_val()
        return

    @pl.when(k == 0)
    def _():
        acc[...] = jnp.zeros_like(acc)

    acc[...] += make_val()


def _ln_fwd(pre):
    mu = jnp.mean(pre, axis=-1, keepdims=True)
    cen = pre - mu
    var = jnp.mean(cen * cen, axis=-1, keepdims=True)
    rstd = lax.rsqrt(var + LN_EPS)
    return cen * rstd, rstd


def _ln_bwd(dy, xhat, rstd, g):
    dxhat = dy * g
    m1 = jnp.mean(dxhat, axis=-1, keepdims=True)
    m2 = jnp.mean(dxhat * xhat, axis=-1, keepdims=True)
    return rstd * (dxhat - m1 - xhat * m2)


def _cast_weight(w, chip_vec, after, name, col_block=0, n_col_blocks=1):
    _, r, c = w.shape
    tr = _row_tile(r, TR_ELT)

    def body(chip_ref, w_ref, after_ref, o_ref):
        o_ref[...] = w_ref[...].astype(BF16)

    grid_spec = pltpu.PrefetchScalarGridSpec(
        num_scalar_prefetch=1, grid=(r // tr,),
        in_specs=[pl.BlockSpec((None, tr, c), lambda i, chip_ref: (0, i, 0)), _ANY],
        out_specs=pl.BlockSpec((None, tr, c), lambda i, chip_ref: (chip_ref[0], i, col_block)))
    return pl.pallas_call(
        body, name=name, grid_spec=grid_spec,
        out_shape=jax.ShapeDtypeStruct((N_CHIPS, r, n_col_blocks * c), BF16),
        input_output_aliases={2: 0} if col_block else {},
        compiler_params=_params(("parallel",)),
    )(chip_vec, w, after)


_HBM = pl.BlockSpec(memory_space=pltpu.HBM)
_VMEM = pl.BlockSpec(memory_space=pltpu.VMEM)


_SEM = pl.BlockSpec(memory_space=pltpu.SEMAPHORE)
_ANY = pl.BlockSpec(memory_space=pl.ANY)
_EFFECT = pltpu.SideEffectType.DATAFLOW_SIDE_EFFECTING


def _chip_copy(buf, k, chip_of_src, half_rows, send_sems, recv_sems, to):
    part = buf.at[chip_of_src, half_rows]
    return pltpu.make_async_remote_copy(
        src_ref=part, dst_ref=part, send_sem=send_sems.at[k], recv_sem=recv_sems.at[k], device_id=to, device_id_type=MESH)


def _half_rows(buf, which):
    hr = buf.shape[1] // 2
    return pl.ds(which * hr, hr)


def _after(value, dep):
    return lax.optimization_barrier((value, dep))[0]


def _flight_start(name, bufs, plan, n_sems, after):
    n = len(bufs)

    def body(*refs):
        sends, _ = plan(refs[:n], refs[n + 1], refs[n + 2])
        for cp in sends:
            cp.start()

    outs = pl.pallas_call(
        body, name=name,
        in_specs=[_HBM] * n + [_ANY], out_specs=[_SEM, _SEM] + [_HBM] * n,
        out_shape=[pltpu.SemaphoreType.DMA((n_sems,))] * 2 + [pltpu.HBM(b.shape, b.dtype) for b in bufs],
        input_output_aliases={i: 2 + i for i in range(n)},
        compiler_params=pltpu.CompilerParams(has_side_effects=_EFFECT),
    )(*[pltpu.with_memory_space_constraint(b, pltpu.HBM) for b in bufs], after)
    return outs[0], outs[1], list(outs[2:])


def _flight_wait(name, flight, plan, after):
    send_sems, recv_sems, bufs = flight
    n = len(bufs)

    def body(*refs):
        sends, recvs = plan(refs[:n], refs[n], refs[n + 1])
        for cp in sends:
            cp.wait_send()
        for cp in recvs:
            cp.wait_recv()

    outs = pl.pallas_call(
        body, name=name,
        in_specs=[_HBM] * n + [_SEM, _SEM, _ANY], out_specs=[_HBM] * n,
        out_shape=[pltpu.HBM(b.shape, b.dtype) for b in bufs],
        input_output_aliases={i: i for i in range(n)},
        compiler_params=pltpu.CompilerParams(has_side_effects=_EFFECT),
    )(*bufs, send_sems, recv_sems, after)
    return list(outs)


def _fill_plan(n_bufs):
    def plan(refs, send_sems, recv_sems):
        x, y, c, chips = _mesh_pos()
        sibling = (x, y, 1 - c)
        sends, recvs = [], []
        for w in range(n_bufs):
            for k, chip in enumerate(chips):
                slot = _chip_id(*chip)
                sends.append(_chip_copy(refs[w], 3 * w + k, slot, _half_rows(refs[w], c), send_sems, recv_sems, sibling))
                recvs.append(_chip_copy(refs[w], 3 * w + k, slot, _half_rows(refs[w], 1 - c), send_sems, recv_sems,
                                        sibling))
        return sends, recvs
    return plan


def _conv_w_plan():
    def plan(refs, send_sems, recv_sems):
        x, y, c, chips = _mesh_pos()
        me = _chip_id(x, y)
        (buf,) = refs
        sends, recvs = [], []
        for k, chip in enumerate(chips):
            for slot, into in ((me, sends), (_chip_id(*chip), recvs)):
                into.append(pltpu.make_async_remote_copy(
                    src_ref=buf.at[slot], dst_ref=buf.at[slot], send_sem=send_sems.at[k], recv_sem=recv_sems.at[k],
                    device_id=(*chip, c), device_id_type=MESH))
        return sends, recvs
    return plan


def _exchange_plan(n_parts):
    def plan(refs, send_sems, recv_sems):
        x, y, c, _ = _mesh_pos()
        copies = []
        for w in range(n_parts):
            part, got = refs[2 * w], refs[2 * w + 1]
            hr = got.shape[1]
            copies.append(pltpu.make_async_remote_copy(
                src_ref=part.at[:, pl.ds((1 - c) * hr, hr)], dst_ref=got, send_sem=send_sems.at[w],
                recv_sem=recv_sems.at[w], device_id=(x, y, 1 - c), device_id_type=MESH))
        return copies, copies
    return plan


def _gather_start(bufs, after, name):
    n = len(bufs)

    def body(*refs):
        ins = refs[:n]
        sends, recvs = refs[n + 1:2 * n + 1], refs[2 * n + 1:3 * n + 1]
        token = refs[4 * n + 1]
        x, y, c, chips = _mesh_pos()
        me = _chip_id(x, y)
        for w in range(n):
            for k, chip in enumerate(chips):
                _chip_copy(ins[w], k, me, _half_rows(ins[w], c), sends[w], recvs[w], (*chip, c)).start()
        token[...] = jnp.zeros_like(token)

    outs = pl.pallas_call(
        body, name=name,
        in_specs=[_HBM] * n + [_ANY],
        out_specs=[_SEM] * (2 * n) + [_HBM] * n + [_VMEM],
        out_shape=[pltpu.SemaphoreType.DMA((3,))] * (2 * n) + [pltpu.HBM(b.shape, b.dtype) for b in bufs]
        + [jax.ShapeDtypeStruct((8, 128), F32)],
        input_output_aliases={w: 2 * n + w for w in range(n)},
        compiler_params=pltpu.CompilerParams(has_side_effects=_EFFECT),
    )(*[pltpu.with_memory_space_constraint(b, pltpu.HBM) for b in bufs], after)
    return [(outs[w], outs[n + w], outs[2 * n + w]) for w in range(n)], outs[3 * n]


def _gather_wait(send_sems, recv_sems, buf, after, name):
    def body(buf_ref, send_ref, recv_ref, after_ref, out_ref):
        x, y, c, chips = _mesh_pos()
        me = _chip_id(x, y)
        for k, chip in enumerate(chips):
            _chip_copy(buf_ref, k, me, _half_rows(buf_ref, c), send_ref, recv_ref, (*chip, c)).wait_send()
        for k, chip in enumerate(chips):
            _chip_copy(buf_ref, k, _chip_id(*chip), _half_rows(buf_ref, c), send_ref, recv_ref, (*chip, c)).wait_recv()

    return pl.pallas_call(
        body, name=name,
        in_specs=[_HBM, _SEM, _SEM, _ANY], out_specs=_HBM,
        out_shape=pltpu.HBM(buf.shape, buf.dtype),
        input_output_aliases={0: 0},
        compiler_params=pltpu.CompilerParams(has_side_effects=_EFFECT),
    )(buf, send_sems, recv_sems, after)


def _sibling_fill(buf, name, own_too=False):
    n_copies = 4 if own_too else 3

    def body(buf_ref, out_ref, send_sems, recv_sems):
        x, y, c, chips = _mesh_pos()
        sibling = (x, y, 1 - c)
        slots = [_chip_id(*chip) for chip in chips] + ([_chip_id(x, y)] if own_too else [])
        copies = []
        for k, slot in enumerate(slots):
            cp = _chip_copy(out_ref, k, slot, _half_rows(out_ref, c), send_sems, recv_sems, sibling)
            cp.start()
            copies.append(cp)
        for k, slot in enumerate(slots):
            _chip_copy(out_ref, k, slot, _half_rows(out_ref, 1 - c), send_sems, recv_sems, sibling).wait_recv()
        for cp in copies:
            cp.wait_send()

    return pl.pallas_call(
        body, name=name,
        in_specs=[_HBM], out_specs=_HBM,
        out_shape=jax.ShapeDtypeStruct(buf.shape, buf.dtype),
        input_output_aliases={0: 0},
        scratch_shapes=[pltpu.SemaphoreType.DMA((n_copies,)), pltpu.SemaphoreType.DMA((n_copies,))],
    )(buf)


def _add_halves(part, got, cvec, name):
    ns, r, cdim = part.shape
    hr = r // 2
    tr = _row_tile(hr, TR_ELT)
    nblk = hr // tr

    def body(c_ref, a_ref, b_ref, o_ref):
        o_ref[...] = a_ref[...] + b_ref[...]

    grid_spec = pltpu.PrefetchScalarGridSpec(
        num_scalar_prefetch=1, grid=(ns, nblk),
        in_specs=[pl.BlockSpec((None, tr, cdim), lambda s, i, c_ref: (s, c_ref[0] * nblk + i, 0)),
                  pl.BlockSpec((None, tr, cdim), lambda s, i, c_ref: (s, i, 0))],
        out_specs=pl.BlockSpec((None, tr, cdim), lambda s, i, c_ref: (s, i, 0)))
    return pl.pallas_call(
        body, name=name, grid_spec=grid_spec,
        out_shape=jax.ShapeDtypeStruct((ns, hr, cdim), BF16),
        compiler_params=_params(("parallel", "parallel")),
    )(cvec, part, got)


def _scatter_copy(sums_ref, land_ref, k, src_slot, dst_slot, c, send_sems, recv_sems, to):
    return pltpu.make_async_remote_copy(
        src_ref=sums_ref.at[src_slot], dst_ref=land_ref.at[dst_slot, _half_rows(land_ref, c)],
        send_sem=send_sems.at[k], recv_sem=recv_sems.at[k], device_id=to, device_id_type=MESH)


def _scatter_start(sums, name):
    ns, hr, cdim = sums.shape
    land = lax.empty((ns, 2 * hr, cdim), sums.dtype)

    def body(sums_ref, land_ref, send_sems, recv_sems, sums_thru, land_thru):
        x, y, c, chips = _mesh_pos()
        me = _chip_id(x, y)
        for k, chip in enumerate(chips):
            _scatter_copy(sums_ref, land_ref, k, _chip_id(*chip), me, c, send_sems, recv_sems, (*chip, c)).start()

    return pl.pallas_call(
        body, name=name,
        in_specs=[_HBM, _HBM], out_specs=[_SEM, _SEM, _HBM, _HBM],
        out_shape=[pltpu.SemaphoreType.DMA((3,)), pltpu.SemaphoreType.DMA((3,)),
                   pltpu.HBM(sums.shape, sums.dtype), pltpu.HBM(land.shape, land.dtype)],
        input_output_aliases={0: 2, 1: 3},
        compiler_params=pltpu.CompilerParams(has_side_effects=_EFFECT),
    )(pltpu.with_memory_space_constraint(sums, pltpu.HBM), pltpu.with_memory_space_constraint(land, pltpu.HBM))


def _scatter_wait(send_sems, recv_sems, sums, land, after, name):
    def body(sums_ref, land_ref, send_ref, recv_ref, after_ref, sums_out, land_out):
        x, y, c, chips = _mesh_pos()
        me = _chip_id(x, y)
        for k, chip in enumerate(chips):
            _scatter_copy(sums_ref, land_ref, k, _chip_id(*chip), me, c, send_ref, recv_ref, (*chip, c)).wait_send()
        for k, chip in enumerate(chips):
            _scatter_copy(sums_ref, land_ref, k, me, _chip_id(*chip), c, send_ref, recv_ref, (*chip, c)).wait_recv()

    return pl.pallas_call(
        body, name=name,
        in_specs=[_HBM, _HBM, _SEM, _SEM, _ANY], out_specs=[_HBM, _HBM],
        out_shape=[pltpu.HBM(sums.shape, sums.dtype), pltpu.HBM(land.shape, land.dtype)],
        input_output_aliases={0: 0, 1: 1},
        compiler_params=pltpu.CompilerParams(has_side_effects=_EFFECT),
    )(sums, land, send_sems, recv_sems, after)


def _complete_plan(n_weights):
    def plan(refs, send_sems, recv_sems):
        x, y, c, chips = _mesh_pos()
        me = _chip_id(x, y)
        sibling = (x, y, 1 - c)
        sends, recvs = [], []
        for w in range(n_weights):
            sums, land = refs[2 * w], refs[2 * w + 1]
            sends.append(_scatter_copy(sums, land, 4 * w + 3, me, me, c, send_sems, recv_sems, sibling))
            recvs.append(_scatter_copy(sums, land, 4 * w + 3, me, me, 1 - c, send_sems, recv_sems, sibling))
            for k, chip in enumerate(chips):
                slot = _chip_id(*chip)
                sends.append(_chip_copy(land, 4 * w + k, slot, _half_rows(land, c), send_sems, recv_sems, sibling))
                recvs.append(_chip_copy(land, 4 * w + k, slot, _half_rows(land, 1 - c), send_sems, recv_sems, sibling))
        return sends, recvs
    return plan


SMALL_ROWS = 8


N_DEVICES = 8


def _small_pack(gl2g, gl2b, gl1g, gl1b, g_ac, gcw, gsink, loss):
    d = gl2g.shape[1]
    hd = d // 2
    nq = gsink.shape[1]

    def body(a_ref, b_ref, c_ref, d_ref, e_ref, cw_ref, sk_ref, ls_ref, out_ref, mine):
        x, y, c, _ = _mesh_pos()
        me = 4 * x + 2 * y + c
        mine[...] = jnp.zeros_like(mine)
        mine[0:1, :] = a_ref[...]
        mine[1:2, :] = b_ref[...]
        mine[2:3, :] = c_ref[...]
        mine[3:4, :] = d_ref[...]
        mine[4:5, :] = e_ref[...]
        mine[5:6, 0:hd] = cw_ref[0:1, :]
        mine[5:6, hd:d] = cw_ref[1:2, :]
        mine[6:7, 0:hd] = cw_ref[2:3, :]
        mine[6:7, hd:hd + nq] = sk_ref[...]
        mine[6:7, hd + 128:hd + 256] = ls_ref[...]
        out_ref[...] = jnp.zeros_like(out_ref)
        out_ref[pl.ds(me, 1)] = mine[...][None]

    return pl.pallas_call(
        body, name="small_pack",
        in_specs=[_VMEM] * 8, out_specs=_VMEM,
        out_shape=jax.ShapeDtypeStruct((N_DEVICES, SMALL_ROWS, d), F32),
        scratch_shapes=[pltpu.VMEM((SMALL_ROWS, d), F32)],
    )(gl2g, gl2b, gl1g, gl1b, g_ac, gcw, gsink, loss)


def _small_plan():
    def plan(refs, send_sems, recv_sems):
        x, y, c, _ = _mesh_pos()
        me = 4 * x + 2 * y + c
        (gath,) = refs
        sends, recvs = [], []
        for r in range(1, N_DEVICES):
            peer = ((1 - x) if r & 4 else x, (1 - y) if r & 2 else y, (1 - c) if r & 1 else c)
            peer_id = 4 * peer[0] + 2 * peer[1] + peer[2]
            for slot, into in ((me, sends), (peer_id, recvs)):
                into.append(pltpu.make_async_remote_copy(
                    src_ref=gath.at[slot], dst_ref=gath.at[slot], send_sem=send_sems.at[r - 1],
                    recv_sem=recv_sems.at[r - 1], device_id=peer, device_id_type=MESH))
        return sends, recvs
    return plan


def _small_sum(gath):
    def body(gath_ref, out_ref):
        total = gath_ref[0]
        for dev in range(1, N_DEVICES):
            total = total + gath_ref[dev]
        out_ref[...] = total

    return pl.pallas_call(
        body, name="small_sum", in_specs=[_VMEM], out_specs=_VMEM,
        out_shape=jax.ShapeDtypeStruct(gath.shape[1:], F32),
    )(gath)


def _adamw(w, g, m, v):
    m = ADAM_B1 * m + (1.0 - ADAM_B1) * g
    v = ADAM_B2 * v + (1.0 - ADAM_B2) * (g * g)
    m_hat = m / (1.0 - ADAM_B1 ** ADAM_STEP)
    v_hat = v / (1.0 - ADAM_B2 ** ADAM_STEP)
    delta = -ADAM_LR * (m_hat / (jnp.sqrt(v_hat) + ADAM_EPS) + ADAM_WD * w)
    return delta, m, v


def _adamw_shard(w, m, v, land, own, pos_vec, name, col_block=0):
    tr = _row_tile(w.shape[1] // 2, TR_ELT)
    grid = (w.shape[1] // tr,)
    body, in_specs, out_specs, out_shape = _adamw_passenger(w.shape, tr, grid, col_block)
    grid_spec = pltpu.PrefetchScalarGridSpec(num_scalar_prefetch=1, grid=grid, in_specs=in_specs, out_specs=out_specs)
    return pl.pallas_call(
        body, name=name, grid_spec=grid_spec, out_shape=out_shape,
        compiler_params=_params(("parallel",)),
    )(pos_vec, w, m, v, land, land, land, land, own)


def _adamw_passenger(shape, tr, grid, col_block):
    _, r, c = shape
    nh = r // 2 // tr
    n_blocks = 2 * nh
    n_steps = int(np.prod(grid))
    assert nh * tr * 2 == r and n_blocks <= n_steps

    def step_of(ids):
        step = ids[0]
        for n, i in zip(grid[1:], ids[1:]):
            step = step * n + i
        return step

    def block_of(ids):
        return jnp.minimum(step_of(ids), n_blocks - 1)

    def update(pos_ref, w_ref, m_ref, v_ref, l0, l1, l2, l3, own_ref, g_out, d_out, m_out, v_out):
        i = block_of([pl.program_id(a) for a in range(len(grid))])
        mine = (i // nh) == pos_ref[1]
        own_blk = own_ref[...].astype(F32)
        g = None
        for s, l_ref in enumerate([l0, l1, l2, l3]):
            term = jnp.where(mine & (pos_ref[0] == s), own_blk, l_ref[...].astype(F32))
            g = term if g is None else g + term
        delta, nm, nv = _adamw(w_ref[...], g, m_ref[...], v_ref[...])
        g_out[...] = g
        d_out[...] = delta
        m_out[...] = nm
        v_out[...] = nv

    def body(*refs):
        if n_blocks == n_steps:
            update(*refs)
        else:
            pl.when(step_of([pl.program_id(a) for a in range(len(grid))]) < n_blocks)(lambda: update(*refs))

    def land_spec(s):
        def index(*args):
            i, pos_ref = block_of(args[:-1]), args[-1]
            skip = (pos_ref[0] == s) & ((i // nh) == pos_ref[1])
            return (s, jnp.where(skip, (i + nh) % n_blocks, i), col_block)
        return pl.BlockSpec((None, tr, c), index)

    blk = pl.BlockSpec((None, tr, c), lambda *args: (0, block_of(args[:-1]), 0))
    in_specs = ([blk, blk, blk] + [land_spec(s) for s in range(N_CHIPS)]
                + [pl.BlockSpec((None, tr, c), lambda *args: (args[-1][0], block_of(args[:-1]) % nh, col_block))])
    return body, in_specs, [blk] * 4, [jax.ShapeDtypeStruct((1, r, c), F32)] * 4


def _call_with_adamw(body, name, grid, in_specs, out_specs, out_shape, scratch_shapes, semantics, operands, shard):
    if shard is None:
        return pl.pallas_call(
            body, name=name, grid=grid, in_specs=in_specs, out_specs=out_specs, out_shape=out_shape,
            scratch_shapes=scratch_shapes, compiler_params=_params(semantics))(*operands)
    w, m, v, land, own, pos_vec, col_block = shard
    n_steps = int(np.prod(grid))
    hr = w.shape[1] // 2
    tr = min(t for t in range(16, hr + 1, 16) if hr % t == 0 and 2 * (hr // t) <= n_steps)
    adam_body, adam_in, adam_out, adam_shape = _adamw_passenger(w.shape, tr, grid, col_block)
    n_in, n_out = len(in_specs), len(out_specs)

    def with_pos(spec):
        if spec.index_map is None:
            return spec
        return pl.BlockSpec(spec.block_shape, lambda *args: spec.index_map(*args[:-1]))

    def both(pos_ref, *refs):
        ins, adam_ins = refs[:n_in], refs[n_in:n_in + len(adam_in)]
        refs = refs[n_in + len(adam_in):]
        outs, adam_outs, scratch = refs[:n_out], refs[n_out:n_out + len(adam_out)], refs[n_out + len(adam_out):]
        body(*ins, *outs, *scratch)
        adam_body(pos_ref, *adam_ins, *adam_outs)

    grid_spec = pltpu.PrefetchScalarGridSpec(
        num_scalar_prefetch=1, grid=grid, in_specs=[with_pos(sp) for sp in in_specs] + adam_in,
        out_specs=[with_pos(sp) for sp in out_specs] + adam_out, scratch_shapes=scratch_shapes)
    return pl.pallas_call(
        both, name=name, grid_spec=grid_spec, out_shape=list(out_shape) + adam_shape,
        compiler_params=_params(semantics),
    )(pos_vec, *operands, w, m, v, land, land, land, land, own)


def _adamw_small(red, params):
    names = ["sinks", "g_attn", "g_conv", "ln1_g", "ln1_b", "ln2_g", "ln2_b", "conv_w"]
    d = red.shape[1]
    hd = d // 2
    flat = []
    for nme in names:
        flat.extend(params[nme])
    nq = params["sinks"][0].shape[1]
    cs = params["conv_w"][0].shape[2]

    def body(*refs):
        red_ref = refs[0]
        ins = refs[1:1 + 3 * len(names)]
        outs = refs[1 + 3 * len(names):]
        x, y, _, _ = _mesh_pos()
        me = _chip_id(x, y)

        def conv_tap(row, base):
            picked = red_ref[row:row + 1, base:base + cs]
            for s in range(1, N_CHIPS):
                picked = jnp.where(me == s, red_ref[row:row + 1, base + s * cs:base + (s + 1) * cs], picked)
            return picked

        grads = {
            "sinks": red_ref[6:7, hd:hd + nq],
            "g_attn": red_ref[4:5, 0:hd],
            "g_conv": red_ref[4:5, hd:d],
            "ln1_g": red_ref[2:3, :],
            "ln1_b": red_ref[3:4, :],
            "ln2_g": red_ref[0:1, :],
            "ln2_b": red_ref[1:2, :],
        }
        for i, nme in enumerate(names):
            w_ref, m_ref, v_ref = ins[3 * i:3 * i + 3]
            g_out, d_out, m_out, v_out = outs[4 * i:4 * i + 4]
            if nme == "conv_w":
                for tap, (row, base) in enumerate([(5, 0), (5, hd), (6, 0)]):
                    g = conv_tap(row, base)
                    delta, nm, nv = _adamw(w_ref[0, tap:tap + 1, :], g, m_ref[0, tap:tap + 1, :], v_ref[0, tap:tap + 1, :])
                    g_out[0, tap:tap + 1, :] = g
                    d_out[0, tap:tap + 1, :] = delta
                    m_out[0, tap:tap + 1, :] = nm
                    v_out[0, tap:tap + 1, :] = nv
            else:
                g = grads[nme]
                delta, nm, nv = _adamw(w_ref[...], g, m_ref[...], v_ref[...])
                g_out[...] = g
                d_out[...] = delta
                m_out[...] = nm
                v_out[...] = nv

    out_shape = []
    for nme in names:
        out_shape.extend([jax.ShapeDtypeStruct(params[nme][0].shape, F32)] * 4)
    outs = pl.pallas_call(
        body, name="adamw_small",
        in_specs=[_VMEM] * (1 + len(flat)), out_specs=[_VMEM] * len(out_shape),
        out_shape=out_shape,
    )(red, *flat)
    return {nme: tuple(outs[4 * i:4 * i + 4]) for i, nme in enumerate(names)}


def _rope_tables(pos_col):
    s = pos_col.shape[0]
    w = N_KV_HEADS * HEAD_DIM
    tb = min(512, s)
    inv_freq = (ROPE_THETA ** (-np.arange(0, ROT_DIM, 2, dtype=np.float32) / ROT_DIM)).astype(np.float32)

    def body(pos_ref, cos_ref, sin_ref):
        pos = pos_ref[...].astype(F32)
        lane = lax.broadcasted_iota(jnp.int32, (tb, PAIR), 1) & (HEAD_DIM - 1)
        fidx = lane & (ROT_DIM // 2 - 1)
        inv = jnp.zeros((tb, PAIR), F32)
        for k in range(ROT_DIM // 2):
            inv = jnp.where(fidx == k, float(inv_freq[k]), inv)
        ang = pos * inv
        rot = lane < ROT_DIM
        sin_v = jnp.sin(ang)
        cos_ref[...] = _tile_lanes(jnp.where(rot, jnp.cos(ang), 1.0), w // PAIR)
        sin_ref[...] = _tile_lanes(jnp.where(lane < ROT_DIM // 2, -sin_v, jnp.where(rot, sin_v, 0.0)), w // PAIR)

    return pl.pallas_call(
        body, name="rope_tables", grid=(s // tb,),
        in_specs=[pl.BlockSpec((tb, 1), lambda i: (i, 0))],
        out_specs=[pl.BlockSpec((tb, w), lambda i: (i, 0))] * 2,
        out_shape=[jax.ShapeDtypeStruct((s, w), F32)] * 2,
        compiler_params=_params(("parallel",)),
    )(pos_col)


def _in_proj(x, w_in_g, first_vec, n_shards, into, name):
    _, s, d = x.shape
    ns, _, ncol = w_in_g.shape
    tm = min(2 * TM, s)
    first_call = into is None
    assert n_shards == 1 or not first_call

    def body(first_ref, x_ref, w_ref, into_ref, o_ref, *x16_ref):
        xb = x_ref[...].astype(BF16)
        o_ref[...] = _dot(xb, w_ref[...]).astype(BF16)
        for ref in x16_ref:
            ref[...] = xb

    shard = lambda j, first_ref: lax.rem(first_ref[0] + j, ns)
    x_spec = pl.BlockSpec((None, tm, d), lambda i, j, first_ref: (0, i, 0))
    grid_spec = pltpu.PrefetchScalarGridSpec(
        num_scalar_prefetch=1, grid=(s // tm, n_shards),
        in_specs=[x_spec, pl.BlockSpec((None, d, ncol), lambda i, j, first_ref: (shard(j, first_ref), 0, 0)), _ANY],
        out_specs=[pl.BlockSpec((tm, ncol), lambda i, j, first_ref: (i, shard(j, first_ref)))] + [x_spec] * first_call)
    return pl.pallas_call(
        body, name=name, grid_spec=grid_spec,
        out_shape=[jax.ShapeDtypeStruct((s, ns * ncol), BF16)] + [jax.ShapeDtypeStruct((1, s, d), BF16)] * first_call,
        input_output_aliases={} if first_call else {3: 0},
        compiler_params=_params(("parallel", "arbitrary")),
    )(first_vec, x, w_in_g, first_vec if first_call else into)


PAIR = 2 * HEAD_DIM
KEYS = 2 * WINDOW


def _pair_operand(t_all, h):
    col = (h // 2) * PAIR
    lane = lax.broadcasted_iota(jnp.int32, (KEYS, PAIR), 1)
    own_low = h % 2 == 0
    mine = jnp.where((lane < HEAD_DIM) if own_low else (lane >= HEAD_DIM), t_all[:, col:col + PAIR], 0.0)
    other = pltpu.roll(mine, HEAD_DIM, 1)
    low, high = (mine, other) if own_low else (other, mine)
    return jnp.concatenate([low, high], axis=0).astype(BF16)


def _pair_grad(acc, h):
    lane = lax.broadcasted_iota(jnp.int32, (KEYS, PAIR), 1)
    low = jnp.where(lane < HEAD_DIM, acc[:KEYS], 0.0)
    high = jnp.where(lane >= HEAD_DIM, acc[KEYS:], 0.0)
    if h % 2 == 0:
        return low + pltpu.roll(high, HEAD_DIM, 1)
    return high + pltpu.roll(low, HEAD_DIM, 1)


N_PAIRS = N_KV_HEADS * GROUP // 2


def _all_probs(q, kk2s, first, sinks_ref):
    assert ATTN_SCALE == 0.125
    q = q * ATTN_SCALE
    qps, scores = [], []
    for pair in range(N_PAIRS):
        qp = q[:, pair * PAIR:(pair + 1) * PAIR].astype(BF16)
        qps.append(qp)
        scores.append(_dot_nt(qp, kk2s[pair // (GROUP // 2)]))
    qi = lax.broadcasted_iota(jnp.int32, (WINDOW, 2 * KEYS), 0)
    kj = lax.broadcasted_iota(jnp.int32, (WINDOW, 2 * KEYS), 1) & (KEYS - 1)
    rel = qi + WINDOW - kj
    valid = (rel >= 0) & (rel < WINDOW) & jnp.logical_not(first & (kj < WINDOW))
    bias = jnp.where(valid, 0.0, NEG_BIG)
    s = (jnp.stack(scores, axis=0) + bias[None]).reshape(N_PAIRS * WINDOW, 2 * KEYS)
    probs, p_sinks = [], []
    for t in range(2):
        st = s[:, t * KEYS:(t + 1) * KEYS]
        sink = jnp.concatenate([jnp.broadcast_to(sinks_ref[0:1, 2 * pair + t:2 * pair + t + 1], (WINDOW, 1))
                                for pair in range(N_PAIRS)], axis=0)
        m = jnp.maximum(jnp.max(st, axis=1, keepdims=True), sink)
        e = jnp.exp(st - m)
        e_sink = jnp.exp(sink - m)
        inv_l = 1.0 / (jnp.sum(e, axis=1, keepdims=True) + e_sink)
        probs.append(e * inv_l)
        p_sinks.append(e_sink * inv_l)
    return qps, jnp.concatenate(probs, axis=1), p_sinks


def _roped_qkv(cur_ref, prev_ref, cos_ref, sin_ref, cosp_ref, sinp_ref, qw, kvw):
    cur = cur_ref[...].astype(F32)
    cos, sin = cos_ref[...], sin_ref[...]
    cos_q, sin_q = _tile_lanes(cos, GROUP), _tile_lanes(sin, GROUP)
    q = _rope(cur[:, :qw], cos_q, sin_q, 1.0)
    prev = prev_ref[...].astype(F32)
    k_all = jnp.concatenate([_rope(prev[:, :kvw], cosp_ref[...], sinp_ref[...], 1.0),
                             _rope(cur[:, qw:qw + kvw], cos, sin, 1.0)], axis=0)
    v_all = jnp.concatenate([prev[:, kvw:], cur[:, qw + kvw:]], axis=0)
    return q, k_all, v_all, cos_q, sin_q


def _attention_fwd(proj, cos_t, sin_t, sinks):
    s = proj.shape[0]
    qw = GROUP * N_KV_HEADS * HEAD_DIM
    kvw = N_KV_HEADS * HEAD_DIM
    nb = s // WINDOW

    def body(cur_ref, prev_ref, cos_ref, sin_ref, cosp_ref, sinp_ref, sinks_ref, o_ref):
        first = pl.program_id(0) == 0
        q, k_all, v_all, _, _ = _roped_qkv(cur_ref, prev_ref, cos_ref, sin_ref, cosp_ref, sinp_ref, qw, kvw)
        kk2s = [_pair_operand(k_all, h) for h in range(N_KV_HEADS)]
        vv2s = [_pair_operand(v_all, h) for h in range(N_KV_HEADS)]
        _, probs, _ = _all_probs(q, kk2s, first, sinks_ref)
        probs = probs.astype(BF16)
        outs = [_dot(probs[pair * WINDOW:(pair + 1) * WINDOW], vv2s[pair // (GROUP // 2)]) for pair in range(N_PAIRS)]
        o_ref[...] = jnp.concatenate(outs, axis=1)

    tbl = pl.BlockSpec((WINDOW, kvw), lambda n: (n, 0))
    tbl_prev = pl.BlockSpec((WINDOW, kvw), lambda n: (jnp.maximum(n - 1, 0), 0))
    return pl.pallas_call(
        body, name="attention_fwd", grid=(nb,),
        in_specs=[pl.BlockSpec((WINDOW, qw + 2 * kvw), lambda n: (n, 0)),
                  pl.BlockSpec((WINDOW, 2 * kvw), lambda n: (jnp.maximum(n - 1, 0), (qw // (2 * kvw)))),
                  tbl, tbl, tbl_prev, tbl_prev, _VMEM],
        out_specs=pl.BlockSpec((WINDOW, qw), lambda n: (n, 0)),
        out_shape=jax.ShapeDtypeStruct((s, qw), F32),
        compiler_params=_params(("parallel",)),
    )(proj, proj, cos_t, sin_t, cos_t, sin_t, sinks)


def _conv_taps(cw_ref):
    return [jnp.concatenate([cw_ref[s, k:k + 1, :] for s in range(N_CHIPS)], axis=1) for k in range(3)]


def _shift_down(z, halo, steps):
    last = halo.shape[0]
    row = lax.broadcasted_iota(jnp.int32, z.shape, 0)
    out = pltpu.roll(z, steps, 0)
    for r in range(steps):
        out = jnp.where(row == r, halo[last - steps + r:last - steps + r + 1, :], out)
    return out


def _shift_up(z, halo, steps):
    rows = z.shape[0]
    row = lax.broadcasted_iota(jnp.int32, z.shape, 0)
    out = pltpu.roll(z, rows - steps, 0)
    for r in range(steps):
        out = jnp.where(row == rows - steps + r, halo[r:r + 1, :], out)
    return out


def _split_cbu(lo, hi, cw):
    lo, hi = lo.astype(F32), hi.astype(F32)
    c_gate = lo[:, :cw]
    b_gate = jnp.concatenate([lo[:, cw:], hi[:, :2 * cw - lo.shape[1]]], axis=1)
    u = hi[:, 2 * cw - lo.shape[1]:]
    return c_gate, b_gate, u


def _conv_norm(proj, attn, cw_full, g_ac):
    s, in_w = proj.shape
    cw = attn.shape[1]
    blk_w = in_w // 3
    tb = min(TB_CONV, s)

    def body(lo_ref, hi_ref, lo_h_ref, hi_h_ref, attn_ref, cw_ref, g_ref, mixed_ref, ac_ref, rstd_ref):
        i = pl.program_id(0)
        c_gate, b_gate, u = _split_cbu(lo_ref[...], hi_ref[...], cw)
        c_h, _, u_h = _split_cbu(lo_h_ref[...], hi_h_ref[...], cw)
        z = c_gate * u
        z_h = jnp.where(i == 0, 0.0, c_h * u_h)
        w0, w1, w2 = _conv_taps(cw_ref)
        y = w0 * _shift_down(z, z_h, 2) + w1 * _shift_down(z, z_h, 1) + w2 * z
        conv = b_gate * y
        a = attn_ref[...]
        r_a = lax.rsqrt(jnp.mean(a * a, axis=-1, keepdims=True) + RMS_EPS)
        r_c = lax.rsqrt(jnp.mean(conv * conv, axis=-1, keepdims=True) + RMS_EPS)
        g = g_ref[...]
        mixed_ref[...] = jnp.concatenate([a * r_a * g[:, :cw], conv * r_c * g[:, cw:]], axis=1).astype(BF16)
        ac_ref[...] = jnp.concatenate([a, conv], axis=1)
        rstd_ref[0] = r_a
        rstd_ref[1] = r_c

    halo_idx = lambda i: jnp.maximum(i * (tb // HALO_ROWS) - 1, 0)
    return pl.pallas_call(
        body, name="conv_norm", grid=(s // tb,),
        in_specs=[pl.BlockSpec((tb, blk_w), lambda i: (i, 1)),
                  pl.BlockSpec((tb, blk_w), lambda i: (i, 2)),
                  pl.BlockSpec((HALO_ROWS, blk_w), lambda i: (halo_idx(i), 1)),
                  pl.BlockSpec((HALO_ROWS, blk_w), lambda i: (halo_idx(i), 2)),
                  pl.BlockSpec((tb, cw), lambda i: (i, 0)),
                  _VMEM, _VMEM],
        out_specs=[pl.BlockSpec((tb, 2 * cw), lambda i: (i, 0)),
                   pl.BlockSpec((tb, 2 * cw), lambda i: (i, 0)),
                   pl.BlockSpec((2, tb, 1), lambda i: (0, i, 0))],
        out_shape=[jax.ShapeDtypeStruct((s, 2 * cw), BF16), jax.ShapeDtypeStruct((s, 2 * cw), F32),
                   jax.ShapeDtypeStruct((2, s, 1), F32)],
        compiler_params=_params(("parallel",)),
    )(proj, proj, proj, proj, attn, cw_full, g_ac)


def _out_proj_ln(mixed, w_out_g, x, ln_g, ln_b):
    s, d = mixed.shape
    tm = min(TM, s)
    tk = d
    nk = d // tk

    def body(a_ref, w_ref, x_ref, g_ref, b_ref, xhat_ref, h_ref, rstd_ref, acc):
        k = pl.program_id(1)
        _accumulate(acc, lambda: _dot(a_ref[...], w_ref[...]), k, nk)

        @pl.when(k == nk - 1)
        def _():
            def rows_fn(rows):
                xhat, rstd = _ln_fwd(ALPHA * x_ref[rows, :] + acc[rows, :])
                xhat_ref[rows, :] = xhat
                h_ref[rows, :] = (xhat * g_ref[...] + b_ref[...]).astype(BF16)
                rstd_ref[rows, :] = rstd

            _for_row_chunks(tm, rows_fn)

    row = pl.BlockSpec((tm, d), lambda i, k: (i, 0))
    return pl.pallas_call(
        body, name="out_proj_ln", grid=(s // tm, nk),
        in_specs=[pl.BlockSpec((tm, tk), lambda i, k: (i, k)),
                  pl.BlockSpec((tk, d), lambda i, k: (k, 0)),
                  pl.BlockSpec((None, tm, d), lambda i, k: (0, i, 0)),
                  _VMEM, _VMEM],
        out_specs=[row, row, pl.BlockSpec((tm, 1), lambda i, k: (i, 0))],
        out_shape=[jax.ShapeDtypeStruct((s, d), F32), jax.ShapeDtypeStruct((s, d), BF16),
                   jax.ShapeDtypeStruct((s, 1), F32)],
        scratch_shapes=[pltpu.VMEM((tm, d), F32)],
        compiler_params=_params(("parallel", "arbitrary")),
    )(mixed, w_out_g, x, ln_g, ln_b)


def _gate_up(h1, w_gu_g, first_vec, n_shards, into, name):
    s, d = h1.shape
    ns, _, fs2 = w_gu_g.shape
    fs = fs2 // 2
    tm = min(TM, s)

    def body(first_ref, h_ref, w_ref, act_in, ab_in, act_ref, ab_ref):
        gu = _dot(h_ref[...], w_ref[...])
        g, u = gu[:, :fs], gu[:, fs:]
        sg = _sigmoid(g)
        silu = g * sg
        act_ref[...] = (silu * u).astype(BF16)
        ab_ref[:, :fs] = (u * (sg * (1.0 + g * (1.0 - sg)))).astype(BF16)
        ab_ref[:, fs:] = silu.astype(BF16)

    shard = lambda j, first_ref: lax.rem(first_ref[0] + j, ns)
    grid_spec = pltpu.PrefetchScalarGridSpec(
        num_scalar_prefetch=1, grid=(s // tm, n_shards),
        in_specs=[pl.BlockSpec((tm, d), lambda i, j, first_ref: (i, 0)),
                  pl.BlockSpec((None, d, fs2), lambda i, j, first_ref: (shard(j, first_ref), 0, 0)), _ANY, _ANY],
        out_specs=[pl.BlockSpec((tm, fs), lambda i, j, first_ref: (i, shard(j, first_ref))),
                   pl.BlockSpec((tm, fs2), lambda i, j, first_ref: (i, shard(j, first_ref)))])
    return pl.pallas_call(
        body, name=name, grid_spec=grid_spec,
        out_shape=[jax.ShapeDtypeStruct((s, ns * fs), BF16), jax.ShapeDtypeStruct((s, ns * fs2), BF16)],
        input_output_aliases={} if into is None else {3: 0, 4: 1},
        compiler_params=_params(("parallel", "arbitrary")),
    )(first_vec, h1, w_gu_g, *((first_vec, first_vec) if into is None else into))


def _down_ln_loss(act, w_down_g, xhat1, ln1_g, ln1_b, ln2_g, ln2_b, target):
    s, f = act.shape
    d = xhat1.shape[1]
    tm = min(TM, s)
    tk = f // N_CHIPS
    nk = f // tk

    def body(a_ref, w_ref, xh_ref, g1_ref, b1_ref, g2_ref, b2_ref, t_ref, dpre_ref, dpre16_ref, loss_ref, gg_ref, gb_ref,
             acc):
        i, k = pl.program_id(0), pl.program_id(1)
        _accumulate(acc, lambda: _dot(a_ref[...], w_ref[...]), k, nk)

        @pl.when(k == nk - 1)
        def _():
            @pl.when(i == 0)
            def _():
                loss_ref[...] = jnp.zeros_like(loss_ref)
                gg_ref[...] = jnp.zeros_like(gg_ref)
                gb_ref[...] = jnp.zeros_like(gb_ref)

            def rows_fn(rows):
                h1 = xh_ref[rows, :] * g1_ref[...] + b1_ref[...]
                xhat, rstd = _ln_fwd(ALPHA * h1 + acc[rows, :])
                g2 = g2_ref[...]
                diff = xhat * g2 + b2_ref[...] - t_ref[rows, :]
                dy = diff * (1.0 / d)
                dpre = _ln_bwd(dy, xhat, rstd, g2)
                dpre_ref[rows, :] = dpre
                dpre16_ref[rows, :] = dpre.astype(BF16)
                sq = jnp.sum(jnp.sum(diff * diff, axis=1, keepdims=True), axis=0, keepdims=True)
                loss_ref[...] += jnp.broadcast_to(sq * (0.5 / d), (1, 128))
                gg_ref[...] += jnp.sum(dy * xhat, axis=0, keepdims=True)
                gb_ref[...] += jnp.sum(dy, axis=0, keepdims=True)

            _for_row_chunks(tm, rows_fn)

    row = pl.BlockSpec((tm, d), lambda i, k: (i, 0))
    vec = pl.BlockSpec((1, d), lambda i, k: (0, 0))
    return pl.pallas_call(
        body, name="down_ln_loss", grid=(s // tm, nk),
        in_specs=[pl.BlockSpec((tm, tk), lambda i, k: (i, k)),
                  pl.BlockSpec((tk, d), lambda i, k: (k, 0)),
                  row, _VMEM, _VMEM, _VMEM, _VMEM,
                  pl.BlockSpec((None, tm, d), lambda i, k: (0, i, 0))],
        out_specs=[row, row, pl.BlockSpec((1, 128), lambda i, k: (0, 0)), vec, vec],
        out_shape=[jax.ShapeDtypeStruct((s, d), F32), jax.ShapeDtypeStruct((s, d), BF16),
                   jax.ShapeDtypeStruct((1, 128), F32), jax.ShapeDtypeStruct((1, d), F32),
                   jax.ShapeDtypeStruct((1, d), F32)],
        scratch_shapes=[pltpu.VMEM((tm, d), F32)],
        compiler_params=_params(("arbitrary", "arbitrary")),
    )(act, w_down_g, xhat1, ln1_g, ln1_b, ln2_g, ln2_b, target)


def _dact_silu_bwd(dpre2, w_down_g, ab):
    s, d = dpre2.shape
    fs2 = ab.shape[1] // N_CHIPS
    fs = fs2 // 2
    tm = min(TM, s)

    def body(dp_ref, w_ref, ab_ref, dgu_ref):
        d_act = _dot_nt(dp_ref[...], w_ref[...])
        dgu_ref[:, :fs] = (d_act * ab_ref[:, :fs].astype(F32)).astype(BF16)
        dgu_ref[:, fs:] = (d_act * ab_ref[:, fs:].astype(F32)).astype(BF16)

    blk = pl.BlockSpec((tm, fs2), lambda j, i: (i, j))
    return pl.pallas_call(
        body, name="dact_silu_bwd", grid=(N_CHIPS, s // tm),
        in_specs=[pl.BlockSpec((tm, d), lambda j, i: (i, 0)),
                  pl.BlockSpec((fs, d), lambda j, i: (j, 0)), blk],
        out_specs=blk,
        out_shape=jax.ShapeDtypeStruct(ab.shape, BF16),
        compiler_params=_params(("parallel", "parallel")),
    )(dpre2, w_down_g, ab)


def _grad_rows(a, b, after, name, row_blocks=1):
    s, m = a.shape
    n = b.shape[1]
    ms = m // N_CHIPS
    tmw = ms // row_blocks
    tk = min(TK_TOK, s)
    nk = s // tk

    def body(a_ref, b_ref, after_ref, o_ref, acc):
        k = pl.program_id(2)
        _accumulate(acc, lambda: _dot_tn(a_ref[...].astype(BF16), b_ref[...].astype(BF16)), k, nk)

        @pl.when(k == nk - 1)
        def _():
            o_ref[...] = acc[...].astype(BF16)

    return pl.pallas_call(
        body, name=name, grid=(N_CHIPS, row_blocks, nk),
        in_specs=[pl.BlockSpec((tk, tmw), lambda j, r, k: (k, j * row_blocks + r)),
                  pl.BlockSpec((tk, n), lambda j, r, k: (k, 0)), _ANY],
        out_specs=pl.BlockSpec((None, tmw, n), lambda j, r, k: (j, r, 0)),
        out_shape=jax.ShapeDtypeStruct((N_CHIPS, ms, n), BF16),
        scratch_shapes=[pltpu.VMEM((tmw, n), F32)],
        compiler_params=_params(("parallel", "parallel", "arbitrary")),
    )(a, b, after)


def _grad_cols(a, bs, after, name, a_3d=False, row_blocks=2, shard=None):
    s, m = a.shape[-2:]
    n = bs[0].shape[1]
    ns = n // N_CHIPS
    nb = len(bs)
    tmw = m // row_blocks
    tk = min(TK_TOK, s)
    nk = s // tk

    def body(*refs):
        a_ref, b_refs, o_refs, accs = refs[0], refs[1:1 + nb], refs[2 + nb:2 + 2 * nb], refs[2 + 2 * nb:]
        k = pl.program_id(2)
        for b_ref, acc in zip(b_refs, accs):
            _accumulate(acc, lambda b_ref=b_ref: _dot_tn(a_ref[...].astype(BF16), b_ref[...].astype(BF16)), k, nk)

        @pl.when(k == nk - 1)
        def _():
            for o_ref, acc in zip(o_refs, accs):
                o_ref[...] = acc[...].astype(BF16)

    if a_3d:
        a_spec = pl.BlockSpec((None, tk, tmw), lambda j, r, k: (0, k, r))
    else:
        a_spec = pl.BlockSpec((tk, tmw), lambda j, r, k: (k, r))
    return _call_with_adamw(
        body, name, (N_CHIPS, row_blocks, nk),
        [a_spec] + [pl.BlockSpec((tk, ns), lambda j, r, k: (k, j))] * nb + [_ANY],
        [pl.BlockSpec((None, tmw, ns), lambda j, r, k: (j, r, 0))] * nb,
        [jax.ShapeDtypeStruct((N_CHIPS, m, ns), BF16)] * nb,
        [pltpu.VMEM((tmw, ns), F32)] * nb, ("parallel", "parallel", "arbitrary"), (a, *bs, after), shard)


def _dh1_ln_bwd(d_gu, w_gu_g, dpre2, xhat1, rstd1, ln1_g, after):
    s = d_gu.shape[0]
    d = dpre2.shape[1]
    hd = d // 2
    fs = w_gu_g.shape[2]
    tm = min(TM, s)

    def body(dgu_ref, w_ref, dp2_ref, xh_ref, rs_ref, g_ref, after_ref, dpre_ref, dpre16_ref, gg_ref, gb_ref, acc_lo,
             acc_hi):
        i, j, half = pl.program_id(0), pl.program_id(1), pl.program_id(2)

        def product():
            return _dot_nt(dgu_ref[...], w_ref[...])

        @pl.when(half == 0)
        def _():
            _accumulate(acc_lo, product, j, N_CHIPS)

        @pl.when(half == 1)
        def _():
            _accumulate(acc_hi, product, j, N_CHIPS)

        @pl.when((j == N_CHIPS - 1) & (half == 1))
        def _():
            @pl.when(i == 0)
            def _():
                gg_ref[...] = jnp.zeros_like(gg_ref)
                gb_ref[...] = jnp.zeros_like(gb_ref)

            def rows_fn(rows):
                dh = jnp.concatenate([acc_lo[rows, :], acc_hi[rows, :]], axis=1) + ALPHA * dp2_ref[rows, :]
                xhat = xh_ref[rows, :]
                dpre = _ln_bwd(dh, xhat, rs_ref[rows, :], g_ref[...])
                dpre_ref[rows, :] = dpre
                dpre16_ref[rows, :] = dpre.astype(BF16)
                gg_ref[...] += jnp.sum(dh * xhat, axis=0, keepdims=True)
                gb_ref[...] += jnp.sum(dh, axis=0, keepdims=True)

            _for_row_chunks(tm, rows_fn)

    row = pl.BlockSpec((tm, d), lambda i, j, h: (i, 0))
    vec = pl.BlockSpec((1, d), lambda i, j, h: (0, 0))
    act_blk = pl.BlockSpec((tm, fs), lambda i, j, h: (i, j))
    w_blk = pl.BlockSpec((None, hd, fs), lambda i, j, h: (j, h, 0))
    return pl.pallas_call(
        body, name="dh1_ln_bwd", grid=(s // tm, N_CHIPS, 2),
        in_specs=[act_blk, w_blk, row, row, pl.BlockSpec((tm, 1), lambda i, j, h: (i, 0)), _VMEM, _ANY],
        out_specs=[row, row, vec, vec],
        out_shape=[jax.ShapeDtypeStruct((s, d), F32), jax.ShapeDtypeStruct((s, d), BF16),
                   jax.ShapeDtypeStruct((1, d), F32), jax.ShapeDtypeStruct((1, d), F32)],
        scratch_shapes=[pltpu.VMEM((tm, hd), F32)] * 2,
        compiler_params=_params(("arbitrary", "arbitrary", "arbitrary")),
    )(d_gu, w_gu_g, dpre2, xhat1, rstd1, ln1_g, after)


def _dmixed_rms_bwd(dpre1, w_out_g, ac, rstd, g_ac):
    s, d = dpre1.shape
    hd = d // 2
    tm = min(TM, s)

    def body(dp_ref, w_ref, ac_ref, rs_ref, g_ref, dac_ref, gg_ref):
        i = pl.program_id(1)
        dm = _dot_nt(dp_ref[...].astype(BF16), w_ref[...])
        pre = ac_ref[...]
        r = rs_ref[...]
        gdm = dm * g_ref[...]
        dac_ref[...] = r * gdm - pre * (r * r * r) * jnp.mean(gdm * pre, axis=-1, keepdims=True)
        gg = jnp.sum(dm * pre * r, axis=0, keepdims=True)

        @pl.when(i == 0)
        def _():
            gg_ref[...] = gg

        @pl.when(i > 0)
        def _():
            gg_ref[...] += gg

    return pl.pallas_call(
        body, name="dmixed_rms_bwd", grid=(2, s // tm),
        in_specs=[pl.BlockSpec((tm, d), lambda h, i: (i, 0)),
                  pl.BlockSpec((hd, d), lambda h, i: (h, 0)),
                  pl.BlockSpec((tm, hd), lambda h, i: (i, h)),
                  pl.BlockSpec((None, tm, 1), lambda h, i: (h, i, 0)),
                  pl.BlockSpec((1, hd), lambda h, i: (0, h))],
        out_specs=[pl.BlockSpec((tm, hd), lambda h, i: (i, h)),
                   pl.BlockSpec((1, hd), lambda h, i: (0, h))],
        out_shape=[jax.ShapeDtypeStruct((s, d), F32), jax.ShapeDtypeStruct((1, d), F32)],
        compiler_params=_params(("arbitrary", "arbitrary")),
    )(dpre1, w_out_g, ac, rstd, g_ac)


def _attention_bwd(proj, d_ac, cos_t, sin_t, sinks, after, shard):
    s = proj.shape[0]
    qw = GROUP * N_KV_HEADS * HEAD_DIM
    kvw = N_KV_HEADS * HEAD_DIM
    nb = s // WINDOW
    nq = GROUP * N_KV_HEADS

    def body(cur_ref, prev_ref, do_ref, cos_ref, sin_ref, cosp_ref, sinp_ref, sinks_ref, after_ref,
             dq_ref, dcur_ref, dprev_ref, dsink_ref):
        n = pl.program_id(0)
        first = n == 0
        q, k_all, v_all, cos_q, sin_q = _roped_qkv(cur_ref, prev_ref, cos_ref, sin_ref, cosp_ref, sinp_ref, qw, kvw)
        kk2s = [_pair_operand(k_all, h) for h in range(N_KV_HEADS)]
        vv2s = [_pair_operand(v_all, h) for h in range(N_KV_HEADS)]
        qps, probs, p_sinks = _all_probs(q, kk2s, first, sinks_ref)
        dops = [do_ref[:, pair * PAIR:(pair + 1) * PAIR].astype(BF16) for pair in range(N_PAIRS)]
        d_probs = jnp.concatenate([_dot_nt(dops[pair], vv2s[pair // (GROUP // 2)]) for pair in range(N_PAIRS)], axis=0)
        d_s, ds_sinks = [], []
        for t in range(2):
            cols = slice(t * KEYS, (t + 1) * KEYS)
            delta = jnp.sum(probs[:, cols] * d_probs[:, cols], axis=1, keepdims=True)
            d_s.append(probs[:, cols] * (d_probs[:, cols] - delta))
            ds_sinks.append(-p_sinks[t] * delta)
        d_s = jnp.concatenate(d_s, axis=1).astype(BF16)
        probs = probs.astype(BF16)
        dq_parts, dk_tiles, dv_tiles, dsink_parts = [], [], [], []
        for h in range(N_KV_HEADS):
            dkk2, dvv2 = None, None
            for p in range(GROUP // 2):
                pair = (GROUP // 2) * h + p
                rows = slice(pair * WINDOW, (pair + 1) * WINDOW)
                dq_parts.append(_dot(d_s[rows], kk2s[h]) * ATTN_SCALE)
                dk_term = _dot_tn(d_s[rows], qps[pair])
                dv_term = _dot_tn(probs[rows], dops[pair])
                dkk2 = dk_term if dkk2 is None else dkk2 + dk_term
                dvv2 = dv_term if dvv2 is None else dvv2 + dv_term
                dsink_parts.extend([jnp.sum(ds_sinks[t][rows], axis=0, keepdims=True) for t in range(2)])
            dk_tiles.append(_pair_grad(dkk2, h))
            dv_tiles.append(_pair_grad(dvv2, h))
        dq_ref[...] = _rope(jnp.concatenate(dq_parts, axis=1), cos_q, sin_q, -1.0)
        dk = jnp.concatenate([dk_tiles[0] + dk_tiles[1], dk_tiles[2] + dk_tiles[3]], axis=1)
        dv = jnp.concatenate([dv_tiles[0] + dv_tiles[1], dv_tiles[2] + dv_tiles[3]], axis=1)
        dprev_ref[...] = jnp.concatenate([dk[:WINDOW], dv[:WINDOW]], axis=1)
        dcur_ref[...] = jnp.concatenate([dk[WINDOW:], dv[WINDOW:]], axis=1)
        dsink = jnp.concatenate(dsink_parts, axis=1)

        @pl.when(first)
        def _():
            dsink_ref[...] = dsink

        @pl.when(n > 0)
        def _():
            dsink_ref[...] += dsink

    tbl = pl.BlockSpec((WINDOW, kvw), lambda n: (n, 0))
    tbl_prev = pl.BlockSpec((WINDOW, kvw), lambda n: (jnp.maximum(n - 1, 0), 0))
    kv_blk = pl.BlockSpec((WINDOW, 2 * kvw), lambda n: (n, 0))
    return _call_with_adamw(
        body, "attention_bwd", (nb,),
        [pl.BlockSpec((WINDOW, qw + 2 * kvw), lambda n: (n, 0)),
         pl.BlockSpec((WINDOW, 2 * kvw), lambda n: (jnp.maximum(n - 1, 0), (qw // (2 * kvw)))),
         pl.BlockSpec((WINDOW, qw), lambda n: (n, 0)),
         tbl, tbl, tbl_prev, tbl_prev, _VMEM, _ANY],
        [pl.BlockSpec((WINDOW, qw), lambda n: (n, 0)), kv_blk, kv_blk, pl.BlockSpec((1, nq), lambda n: (0, 0))],
        [jax.ShapeDtypeStruct((s, qw), F32), jax.ShapeDtypeStruct((s, 2 * kvw), F32),
         jax.ShapeDtypeStruct((s, 2 * kvw), F32), jax.ShapeDtypeStruct((1, nq), F32)],
        [], ("arbitrary",), (proj, proj, d_ac, cos_t, sin_t, cos_t, sin_t, sinks, after), shard)


def _dproj_assemble(proj, d_ac, dq, dkv_cur, dkv_prev, cos_t, sin_t, cw_full):
    s, in_w = proj.shape
    cw = dq.shape[1]
    kvw = N_KV_HEADS * HEAD_DIM
    blk_w = in_w // 3
    tb = WINDOW
    nb = s // tb

    def body(lo_ref, hi_ref, lo_p_ref, hi_p_ref, lo_n_ref, hi_n_ref, dconv_ref, dconv_n_ref,
             dq_ref, dcur_ref, dprev_n_ref, cos_ref, sin_ref, cw_ref, dproj_ref, gcw_ref):
        i = pl.program_id(0)
        last = i == nb - 1
        c_gate, b_gate, u = _split_cbu(lo_ref[...], hi_ref[...], cw)
        c_p, _, u_p = _split_cbu(lo_p_ref[...], hi_p_ref[...], cw)
        _, b_n, _ = _split_cbu(lo_n_ref[...], hi_n_ref[...], cw)
        z = c_gate * u
        z_p = jnp.where(i == 0, 0.0, c_p * u_p)
        z1 = _shift_down(z, z_p, 1)
        z2 = _shift_down(z, z_p, 2)
        w0, w1, w2 = _conv_taps(cw_ref)
        y = w0 * z2 + w1 * z1 + w2 * z
        d_conv = dconv_ref[...]
        d_b = d_conv * y
        d_y = d_conv * b_gate
        d_y_n = jnp.where(last, 0.0, dconv_n_ref[...] * b_n[:dconv_n_ref.shape[0]])
        d_z = w2 * d_y + w1 * _shift_up(d_y, d_y_n, 1) + w0 * _shift_up(d_y, d_y_n, 2)
        d_c = d_z * u
        d_u = d_z * c_gate
        gcw = jnp.concatenate([jnp.sum(d_y * z2, axis=0, keepdims=True), jnp.sum(d_y * z1, axis=0, keepdims=True),
                               jnp.sum(d_y * z, axis=0, keepdims=True)], axis=0)

        @pl.when(i == 0)
        def _():
            gcw_ref[...] = gcw

        @pl.when(i > 0)
        def _():
            gcw_ref[...] += gcw

        dkv = dcur_ref[...] + jnp.where(last, 0.0, dprev_n_ref[...])
        dk = _rope(dkv[:, :kvw], cos_ref[...], sin_ref[...], -1.0)
        dproj_ref[...] = jnp.concatenate([dq_ref[...], dk, dkv[:, kvw:], d_c, d_b, d_u], axis=1).astype(BF16)

    prev_halo = lambda i: jnp.maximum(i * (tb // HALO_ROWS) - 1, 0)
    next_halo = lambda i: jnp.minimum((i + 1) * (tb // HALO_ROWS), s // HALO_ROWS - 1)
    next8 = lambda i: jnp.minimum((i + 1) * (tb // 8), s // 8 - 1)
    nxt = lambda i: jnp.minimum(i + 1, nb - 1)
    return pl.pallas_call(
        body, name="dproj_assemble", grid=(nb,),
        in_specs=[pl.BlockSpec((tb, blk_w), lambda i: (i, 1)),
                  pl.BlockSpec((tb, blk_w), lambda i: (i, 2)),
                  pl.BlockSpec((HALO_ROWS, blk_w), lambda i: (prev_halo(i), 1)),
                  pl.BlockSpec((HALO_ROWS, blk_w), lambda i: (prev_halo(i), 2)),
                  pl.BlockSpec((HALO_ROWS, blk_w), lambda i: (next_halo(i), 1)),
                  pl.BlockSpec((HALO_ROWS, blk_w), lambda i: (next_halo(i), 2)),
                  pl.BlockSpec((tb, cw), lambda i: (i, 1)),
                  pl.BlockSpec((8, cw), lambda i: (next8(i), 1)),
                  pl.BlockSpec((tb, cw), lambda i: (i, 0)),
                  pl.BlockSpec((tb, 2 * kvw), lambda i: (i, 0)),
                  pl.BlockSpec((tb, 2 * kvw), lambda i: (nxt(i), 0)),
                  pl.BlockSpec((tb, kvw), lambda i: (i, 0)),
                  pl.BlockSpec((tb, kvw), lambda i: (i, 0)),
                  _VMEM],
        out_specs=[pl.BlockSpec((tb, in_w), lambda i: (i, 0)),
                   pl.BlockSpec((3, cw), lambda i: (0, 0))],
        out_shape=[jax.ShapeDtypeStruct((s, in_w), BF16), jax.ShapeDtypeStruct((3, cw), F32)],
        compiler_params=_params(("arbitrary",)),
    )(proj, proj, proj, proj, proj, proj, d_ac, d_ac, dq, dkv_cur, dkv_prev, cos_t, sin_t, cw_full)


def _dx(d_proj, w_in_g, dpre1, after, shard):
    s, in_w = d_proj.shape
    ns, d, ncol = w_in_g.shape
    tm = min(TM, s)

    def body(dp_ref, w_ref, r_ref, after_ref, o_ref, acc):
        j = pl.program_id(1)
        _accumulate(acc, lambda: _dot_nt(dp_ref[...], w_ref[...]), j, ns)

        @pl.when(j == ns - 1)
        def _():
            o_ref[...] = acc[...] + ALPHA * r_ref[...]

    return _call_with_adamw(
        body, "dx", (s // tm, ns),
        [pl.BlockSpec((tm, ncol), lambda i, j: (i, j)),
         pl.BlockSpec((None, d, ncol), lambda i, j: (j, 0, 0)),
         pl.BlockSpec((tm, d), lambda i, j: (i, 0)), _ANY],
        [pl.BlockSpec((None, tm, d), lambda i, j: (0, i, 0))], [jax.ShapeDtypeStruct((1, s, d), F32)],
        [pltpu.VMEM((tm, d), F32)], ("parallel", "arbitrary"), (d_proj, w_in_g, dpre1, after), shard)


def kernel(x, positions, w_in, conv_w, sinks, g_attn, g_conv, w_out, ln1_g, ln1_b, w_gate, w_up, w_down, ln2_g, ln2_b, loss_target, m_w_in, m_conv_w, m_sinks, m_g_attn, m_g_conv, m_w_out, m_ln1_g, m_ln1_b, m_w_gate, m_w_up, m_w_down, m_ln2_g, m_ln2_b, v_w_in, v_conv_w, v_sinks, v_g_attn, v_g_conv, v_w_out, v_ln1_g, v_ln1_b, v_w_gate, v_w_up, v_w_down, v_ln2_g, v_ln2_b):
    s = x.shape[1]
    d = x.shape[2]

    chip_vec = _chip_id(lax.axis_index("x"), lax.axis_index("y")).astype(jnp.int32).reshape(1)
    wnames = ["w_in", "w_out", "w_gu", "w_down"]
    buf_in = _cast_weight(w_in, chip_vec, chip_vec, "cast_w_in")
    flight_in, token_in = _gather_start([buf_in], chip_vec, "gather_start_w_in")
    cw_buf = lax.dynamic_update_slice(jnp.zeros((N_CHIPS,) + conv_w.shape[1:], F32), conv_w, (chip_vec[0], 0, 0))
    cw_flight = _flight_start("conv_w_start", [cw_buf], _conv_w_plan(), 3, token_in)
    started = cw_flight[2][0]
    buf_gu = _cast_weight(w_gate, chip_vec, started, "cast_w_gate", 0, 2)
    buf_gu = _cast_weight(w_up, chip_vec, buf_gu, "cast_w_up", 1, 2)
    bufs = [_cast_weight(w_out, chip_vec, started, "cast_w_out"), buf_gu,
            _cast_weight(w_down, chip_vec, started, "cast_w_down")]
    flights_rest, token = _gather_start(bufs, token_in, "gather_start_rest")
    flights = flight_in + flights_rest

    def gathered(i, after):
        send_sems, recv_sems, buf = flights[i]
        buf = _gather_wait(send_sems, recv_sems, buf, after, "gather_wait_" + wnames[i])
        return _sibling_fill(buf, "sibling_fill_" + wnames[i])

    g_ac = jnp.concatenate([g_attn, g_conv], axis=1)

    proj_own, x16 = _in_proj(x, _after(flights[0][2], token), chip_vec, 1, None, "in_proj_own")
    cos_t, sin_t = _rope_tables(positions.reshape(s, 1) + token[0:1, 0:1].astype(jnp.int32))
    w_in_g = gathered(0, _after(cos_t, proj_own))
    (proj,) = _in_proj(x16, w_in_g, chip_vec + 1, N_CHIPS - 1, proj_own, "in_proj_rest")
    send_sems, recv_sems, buf_out = flights[1]
    buf_out = _gather_wait(send_sems, recv_sems, buf_out, proj, "gather_wait_w_out")
    fill_out = _flight_start("fill_start_w_out", [buf_out], _fill_plan(1), 3, chip_vec)
    attn = _attention_fwd(_after(proj, fill_out[2][0]), cos_t, sin_t, sinks)
    (cw_full,) = _flight_wait("conv_w_wait", cw_flight, _conv_w_plan(), attn)
    mixed, ac, rstd_ac = _conv_norm(proj, attn, cw_full, g_ac)
    (w_out_g,) = _flight_wait("fill_wait_w_out", fill_out, _fill_plan(1), mixed)
    w_out_full = w_out_g.reshape(d, d)
    xhat1, h1, rstd1 = _out_proj_ln(mixed, w_out_full, x, ln1_g, ln1_b)
    send_sems, recv_sems, buf_gu = flights[2]
    buf_gu = _gather_wait(send_sems, recv_sems, buf_gu, h1, "gather_wait_w_gu")
    fill_gu = _flight_start("fill_start_w_gu", [buf_gu], _fill_plan(1), 3, chip_vec)
    own = _gate_up(h1, fill_gu[2][0], chip_vec, 1, None, "gate_up_own")
    (w_gu_g,) = _flight_wait("fill_wait_w_gu", fill_gu, _fill_plan(1), own[0])
    some = _gate_up(h1, w_gu_g, chip_vec + 1, N_CHIPS - 2, own, "gate_up_rest")
    send_sems, recv_sems, buf_down = flights[3]
    buf_down = _gather_wait(send_sems, recv_sems, buf_down, some[0], "gather_wait_w_down")
    fill_down = _flight_start("fill_start_w_down", [buf_down], _fill_plan(1), 3, chip_vec)
    act, ab = _gate_up(h1, _after(w_gu_g, fill_down[2][0]), chip_vec + N_CHIPS - 1, 1, some, "gate_up_last")
    (w_down_g,) = _flight_wait("fill_wait_w_down", fill_down, _fill_plan(1), act)
    w_down_full = w_down_g.reshape(-1, d)
    dpre2, dpre2_16, loss_part, g_ln2_g, g_ln2_b = _down_ln_loss(act, w_down_full, xhat1, ln1_g, ln1_b, ln2_g, ln2_b,
                                                                 loss_target)

    cvec = lax.axis_index("c").astype(jnp.int32).reshape(1)

    def exchange_begin(parts, nme):
        bufs = []
        for part in parts:
            ns, r, cdim = part.shape
            bufs.extend([part, lax.empty((ns, r // 2, cdim), part.dtype)])
        return _flight_start("exchange_start_" + nme, bufs, _exchange_plan(len(parts)), len(parts), cvec)

    def exchange_end(flight, n_parts, after, nme):
        bufs = _flight_wait("exchange_wait_" + nme, flight, _exchange_plan(n_parts), after)
        return [(bufs[2 * w], bufs[2 * w + 1]) for w in range(n_parts)]

    def scatter_begin(part, got, nme):
        return _scatter_start(_add_halves(part, got, cvec, "add_halves_" + nme), "scatter_start_" + nme)

    d_gu = _dact_silu_bwd(dpre2_16, w_down_full, ab)
    p_down = _grad_rows(act, dpre2_16, d_gu, "grad_w_down")
    x_down = exchange_begin([p_down], "w_down")
    (p_gu,) = _grad_cols(h1, [d_gu], x_down[2][0], "grad_w_gate_up")
    ((p_down, got),) = exchange_end(x_down, 1, p_gu, "w_down")
    f_down = scatter_begin(p_down, got, "w_down")
    x_gu = exchange_begin([_after(p_gu, f_down[2])], "w_gu")
    dpre1, dpre1_16, g_ln1_g, g_ln1_b = _dh1_ln_bwd(d_gu, w_gu_g, dpre2, xhat1, rstd1, ln1_g, x_gu[2][0])
    ((p_gu, got),) = exchange_end(x_gu, 1, dpre1, "w_gu")
    f_gu = scatter_begin(p_gu, got, "w_gu")
    d_ac, g_g_ac = _dmixed_rms_bwd(_after(dpre1_16, f_gu[2]), w_out_full, ac, rstd_ac, g_ac)
    pos_vec = jnp.concatenate([chip_vec, cvec])
    sums, land = _scatter_wait(*f_down, d_ac, "scatter_wait_w_down")
    c_down = _flight_start("complete_start_w_down", [sums, land], _complete_plan(1), 4, cvec)
    p_out = _grad_rows(mixed, dpre1_16, c_down[2][1], "grad_w_out")
    x_out = exchange_begin([p_out], "w_out")
    sums, land = _flight_wait("complete_wait_w_down", c_down, _complete_plan(1), x_out[2][0])
    dq, dkv_cur, dkv_prev, g_sinks, *new_w_down = _attention_bwd(
        proj, d_ac, cos_t, sin_t, sinks, x_out[2][0], (w_down, m_w_down, v_w_down, land, sums, pos_vec, 0))
    ((p_out, got),) = exchange_end(x_out, 1, dq, "w_out")
    f_out = scatter_begin(p_out, got, "w_out")
    sums, land = _scatter_wait(*f_gu, f_out[2], "scatter_wait_w_gu")
    c_gu = _flight_start("complete_start_w_gu", [sums, land], _complete_plan(1), 4, cvec)
    d_proj, g_conv_w = _dproj_assemble(proj, _after(d_ac, c_gu[2][1]), dq, dkv_cur, dkv_prev, cos_t, sin_t, cw_full)
    small_parts = _small_pack(g_ln2_g, g_ln2_b, g_ln1_g, g_ln1_b, g_g_ac, g_conv_w, g_sinks, loss_part)
    f_small = _flight_start("small_start", [small_parts], _small_plan(), N_DEVICES - 1, cvec)
    sums_gu, land_gu = _flight_wait("complete_wait_w_gu", c_gu, _complete_plan(1), f_small[2][0])
    p_in, *new_w_gate = _grad_cols(x16, [d_proj], f_small[2][0], "grad_w_in", a_3d=True,
                                   shard=(w_gate, m_w_gate, v_w_gate, land_gu, sums_gu, pos_vec, 0))
    (small_parts,) = _flight_wait("small_wait", f_small, _small_plan(), p_in)
    red = _small_sum(small_parts)
    x_in = exchange_begin([_after(p_in, red)], "w_in")
    sums, land = _scatter_wait(*f_out, x_in[2][0], "scatter_wait_w_out")
    c_out = _flight_start("complete_start_w_out", [sums, land], _complete_plan(1), 4, cvec)
    new_w_up = _adamw_shard(w_up, m_w_up, v_w_up, _after(land_gu, c_out[2][1]), sums_gu, pos_vec, "adamw_w_up", 1)
    ((p_in, got),) = exchange_end(x_in, 1, new_w_up[0], "w_in")
    f_in = scatter_begin(p_in, got, "w_in")
    (grad_x,) = _dx(d_proj, w_in_g, dpre1, f_in[2], None)

    big = {"w_down": new_w_down, "w_gate": new_w_gate, "w_up": new_w_up}
    sums, land = _scatter_wait(*f_in, grad_x, "scatter_wait_w_in")
    c_in = _flight_start("complete_start_w_in", [sums, land], _complete_plan(1), 4, cvec)
    sums, land = _flight_wait("complete_wait_w_out", c_out, _complete_plan(1), c_in[2][1])
    big["w_out"] = _adamw_shard(w_out, m_w_out, v_w_out, land, sums, pos_vec, "adamw_w_out")
    sums, land = _flight_wait("complete_wait_w_in", c_in, _complete_plan(1), big["w_out"][0])
    big["w_in"] = _adamw_shard(w_in, m_w_in, v_w_in, land, sums, pos_vec, "adamw_w_in")
    small = _adamw_small(red, {
        "sinks": (sinks, m_sinks, v_sinks), "g_attn": (g_attn, m_g_attn, v_g_attn),
        "g_conv": (g_conv, m_g_conv, v_g_conv), "ln1_g": (ln1_g, m_ln1_g, v_ln1_g),
        "ln1_b": (ln1_b, m_ln1_b, v_ln1_b), "ln2_g": (ln2_g, m_ln2_g, v_ln2_g),
        "ln2_b": (ln2_b, m_ln2_b, v_ln2_b), "conv_w": (conv_w, m_conv_w, v_conv_w)})
    res = {**big, **small}
    order = ["w_in", "conv_w", "sinks", "g_attn", "g_conv", "w_out", "ln1_g", "ln1_b", "w_gate", "w_up", "w_down",
             "ln2_g", "ln2_b"]
    loss = red[6, d // 2 + 128]
    return (loss, grad_x, *[res[n][0] for n in order], *[res[n][1] for n in order],
            *[res[n][2] for n in order], *[res[n][3] for n in order])
```

```python
import functools

import numpy as np
import jax
import jax.numpy as jnp
from jax import lax
from jax.experimental import pallas as pl
from jax.experimental.pallas import tpu as pltpu

F32 = jnp.float32
BF16 = jnp.bfloat16
MESH = pl.DeviceIdType.MESH

HEAD_DIM = 64
N_KV_HEADS = 4
GROUP = 4
WINDOW = 128
ROT_DIM = 16
ROPE_THETA = 500000.0
ATTN_SCALE = HEAD_DIM ** -0.5
ALPHA = 2.0 ** 0.25
LN_EPS = 1e-5
RMS_EPS = 1e-6
ADAM_LR = 0.001
ADAM_B1 = 0.9
ADAM_B2 = 0.999
ADAM_EPS = 1e-08
ADAM_WD = 0.01
ADAM_STEP = 10
N_CHIPS = 4
NEG_BIG = -1e30

V7X_VMEM_BYTES = 64 * 1024 * 1024
VMEM_LIMIT = V7X_VMEM_BYTES - 6 * 1024 * 1024

TM = 512
TK_TOK = 1024
TB_CONV = 512
TR_ELT = 256
ROW_CHUNK = 128
HALO_ROWS = 16


def _params(sem):
    return pltpu.CompilerParams(dimension_semantics=sem, vmem_limit_bytes=VMEM_LIMIT)


def _row_tile(rows, target):
    best = None
    for t in range(16, min(rows, target) + 1, 16):
        if rows % t == 0:
            best = t
    assert best is not None, (rows, target)
    return best


def _dot(a, b):
    return jnp.dot(a, b, preferred_element_type=F32)


def _dot_nt(a, b):
    return lax.dot_general(a, b, (((1,), (1,)), ((), ())), preferred_element_type=F32)


def _dot_tn(a, b):
    return lax.dot_general(a, b, (((0,), (0,)), ((), ())), preferred_element_type=F32)


def _mesh_pos():
    x, y, c = lax.axis_index("x"), lax.axis_index("y"), lax.axis_index("c")
    chips = [(1 - x, y), (x, 1 - y), (1 - x, 1 - y)]
    return x, y, c, chips


def _chip_id(px, py):
    return 2 * px + py


def _rope(t, cos, sgn_sin, sign):
    w = t.shape[1]
    lane = lax.broadcasted_iota(jnp.int32, t.shape, 1) & (HEAD_DIM - 1)
    partner = jnp.where(lane < ROT_DIM // 2, pltpu.roll(t, w - ROT_DIM // 2, 1), pltpu.roll(t, ROT_DIM // 2, 1))
    return t * cos + sign * (partner * sgn_sin)


def _tile_lanes(t, n):
    return jnp.concatenate([t] * n, axis=1)


def _sigmoid(g):
    return 1.0 / (1.0 + jnp.exp(-g))


def _for_row_chunks(n_rows, fn):
    def step(r, carry):
        fn(pl.ds(pl.multiple_of(r * ROW_CHUNK, ROW_CHUNK), ROW_CHUNK))
        return carry

    lax.fori_loop(0, n_rows // ROW_CHUNK, step, 0)


def _accumulate(acc, make_val, k, nk):
    if nk == 1:
        acc[...] = make_val()
        return

    @pl.when(k == 0)
    def _():
        acc[...] = jnp.zeros_like(acc)

    acc[...] += make_val()


def _ln_fwd(pre):
    mu = jnp.mean(pre, axis=-1, keepdims=True)
    cen = pre - mu
    var = jnp.mean(cen * cen, axis=-1, keepdims=True)
    rstd = lax.rsqrt(var + LN_EPS)
    return cen * rstd, rstd


def _ln_bwd(dy, xhat, rstd, g):
    dxhat = dy * g
    m1 = jnp.mean(dxhat, axis=-1, keepdims=True)
    m2 = jnp.mean(dxhat * xhat, axis=-1, keepdims=True)
    return rstd * (dxhat - m1 - xhat * m2)


def _cast_weight(w, chip_vec, after, name, col_block=0, n_col_blocks=1):
    _, r, c = w.shape
    tr = _row_tile(r, TR_ELT)

    def body(chip_ref, w_ref, after_ref, o_ref):
        o_ref[...] = w_ref[...].astype(BF16)

    grid_spec = pltpu.PrefetchScalarGridSpec(
        num_scalar_prefetch=1, grid=(r // tr,),
        in_specs=[pl.BlockSpec((None, tr, c), lambda i, chip_ref: (0, i, 0)), _ANY],
        out_specs=pl.BlockSpec((None, tr, c), lambda i, chip_ref: (chip_ref[0], i, col_block)))
    return pl.pallas_call(
        body, name=name, grid_spec=grid_spec,
        out_shape=jax.ShapeDtypeStruct((N_CHIPS, r, n_col_blocks * c), BF16),
        input_output_aliases={2: 0} if col_block else {},
        compiler_params=_params(("parallel",)),
    )(chip_vec, w, after)


_HBM = pl.BlockSpec(memory_space=pltpu.HBM)
_VMEM = pl.BlockSpec(memory_space=pltpu.VMEM)


_SEM = pl.BlockSpec(memory_space=pltpu.SEMAPHORE)
_ANY = pl.BlockSpec(memory_space=pl.ANY)
_EFFECT = pltpu.SideEffectType.DATAFLOW_SIDE_EFFECTING


def _chip_copy(buf, k, chip_of_src, half_rows, send_sems, recv_sems, to):
    part = buf.at[chip_of_src, half_rows]
    return pltpu.make_async_remote_copy(
        src_ref=part, dst_ref=part, send_sem=send_sems.at[k], recv_sem=recv_sems.at[k], device_id=to, device_id_type=MESH)


def _half_rows(buf, which):
    hr = buf.shape[1] // 2
    return pl.ds(which * hr, hr)


def _after(value, dep):
    return lax.optimization_barrier((value, dep))[0]


def _flight_start(name, bufs, plan, n_sems, after):
    n = len(bufs)

    def body(*refs):
        sends, _ = plan(refs[:n], refs[n + 1], refs[n + 2])
        for cp in sends:
            cp.start()

    outs = pl.pallas_call(
        body, name=name,
        in_specs=[_HBM] * n + [_ANY], out_specs=[_SEM, _SEM] + [_HBM] * n,
        out_shape=[pltpu.SemaphoreType.DMA((n_sems,))] * 2 + [pltpu.HBM(b.shape, b.dtype) for b in bufs],
        input_output_aliases={i: 2 + i for i in range(n)},
        compiler_params=pltpu.CompilerParams(has_side_effects=_EFFECT),
    )(*[pltpu.with_memory_space_constraint(b, pltpu.HBM) for b in bufs], after)
    return outs[0], outs[1], list(outs[2:])


def _flight_wait(name, flight, plan, after):
    send_sems, recv_sems, bufs = flight
    n = len(bufs)

    def body(*refs):
        sends, recvs = plan(refs[:n], refs[n], refs[n + 1])
        for cp in sends:
            cp.wait_send()
        for cp in recvs:
            cp.wait_recv()

    outs = pl.pallas_call(
        body, name=name,
        in_specs=[_HBM] * n + [_SEM, _SEM, _ANY], out_specs=[_HBM] * n,
        out_shape=[pltpu.HBM(b.shape, b.dtype) for b in bufs],
        input_output_aliases={i: i for i in range(n)},
        compiler_params=pltpu.CompilerParams(has_side_effects=_EFFECT),
    )(*bufs, send_sems, recv_sems, after)
    return list(outs)


def _fill_plan(n_bufs):
    def plan(refs, send_sems, recv_sems):
        x, y, c, chips = _mesh_pos()
        sibling = (x, y, 1 - c)
        sends, recvs = [], []
        for w in range(n_bufs):
            for k, chip in enumerate(chips):
                slot = _chip_id(*chip)
                sends.append(_chip_copy(refs[w], 3 * w + k, slot, _half_rows(refs[w], c), send_sems, recv_sems, sibling))
                recvs.append(_chip_copy(refs[w], 3 * w + k, slot, _half_rows(refs[w], 1 - c), send_sems, recv_sems,
                                        sibling))
        return sends, recvs
    return plan


def _conv_w_plan():
    def plan(refs, send_sems, recv_sems):
        x, y, c, chips = _mesh_pos()
        me = _chip_id(x, y)
        (buf,) = refs
        sends, recvs = [], []
        for k, chip in enumerate(chips):
            for slot, into in ((me, sends), (_chip_id(*chip), recvs)):
                into.append(pltpu.make_async_remote_copy(
                    src_ref=buf.at[slot], dst_ref=buf.at[slot], send_sem=send_sems.at[k], recv_sem=recv_sems.at[k],
                    device_id=(*chip, c), device_id_type=MESH))
        return sends, recvs
    return plan


def _exchange_plan(n_parts):
    def plan(refs, send_sems, recv_sems):
        x, y, c, _ = _mesh_pos()
        copies = []
        for w in range(n_parts):
            part, got = refs[2 * w], refs[2 * w + 1]
            hr = got.shape[1]
            copies.append(pltpu.make_async_remote_copy(
                src_ref=part.at[:, pl.ds((1 - c) * hr, hr)], dst_ref=got, send_sem=send_sems.at[w],
                recv_sem=recv_sems.at[w], device_id=(x, y, 1 - c), device_id_type=MESH))
        return copies, copies
    return plan


def _gather_start(bufs, after, name):
    n = len(bufs)

    def body(*refs):
        ins = refs[:n]
        sends, recvs = refs[n + 1:2 * n + 1], refs[2 * n + 1:3 * n + 1]
        token = refs[4 * n + 1]
        x, y, c, chips = _mesh_pos()
        me = _chip_id(x, y)
        for w in range(n):
            for k, chip in enumerate(chips):
                _chip_copy(ins[w], k, me, _half_rows(ins[w], c), sends[w], recvs[w], (*chip, c)).start()
        token[...] = jnp.zeros_like(token)

    outs = pl.pallas_call(
        body, name=name,
        in_specs=[_HBM] * n + [_ANY],
        out_specs=[_SEM] * (2 * n) + [_HBM] * n + [_VMEM],
        out_shape=[pltpu.SemaphoreType.DMA((3,))] * (2 * n) + [pltpu.HBM(b.shape, b.dtype) for b in bufs]
        + [jax.ShapeDtypeStruct((8, 128), F32)],
        input_output_aliases={w: 2 * n + w for w in range(n)},
        compiler_params=pltpu.CompilerParams(has_side_effects=_EFFECT),
    )(*[pltpu.with_memory_space_constraint(b, pltpu.HBM) for b in bufs], after)
    return [(outs[w], outs[n + w], outs[2 * n + w]) for w in range(n)], outs[3 * n]


def _gather_wait(send_sems, recv_sems, buf, after, name):
    def body(buf_ref, send_ref, recv_ref, after_ref, out_ref):
        x, y, c, chips = _mesh_pos()
        me = _chip_id(x, y)
        for k, chip in enumerate(chips):
            _chip_copy(buf_ref, k, me, _half_rows(buf_ref, c), send_ref, recv_ref, (*chip, c)).wait_send()
        for k, chip in enumerate(chips):
            _chip_copy(buf_ref, k, _chip_id(*chip), _half_rows(buf_ref, c), send_ref, recv_ref, (*chip, c)).wait_recv()

    return pl.pallas_call(
        body, name=name,
        in_specs=[_HBM, _SEM, _SEM, _ANY], out_specs=_HBM,
        out_shape=pltpu.HBM(buf.shape, buf.dtype),
        input_output_aliases={0: 0},
        compiler_params=pltpu.CompilerParams(has_side_effects=_EFFECT),
    )(buf, send_sems, recv_sems, after)


def _sibling_fill(buf, name, own_too=False):
    n_copies = 4 if own_too else 3

    def body(buf_ref, out_ref, send_sems, recv_sems):
        x, y, c, chips = _mesh_pos()
        sibling = (x, y, 1 - c)
        slots = [_chip_id(*chip) for chip in chips] + ([_chip_id(x, y)] if own_too else [])
        copies = []
        for k, slot in enumerate(slots):
            cp = _chip_copy(out_ref, k, slot, _half_rows(out_ref, c), send_sems, recv_sems, sibling)
            cp.start()
            copies.append(cp)
        for k, slot in enumerate(slots):
            _chip_copy(out_ref, k, slot, _half_rows(out_ref, 1 - c), send_sems, recv_sems, sibling).wait_recv()
        for cp in copies:
            cp.wait_send()

    return pl.pallas_call(
        body, name=name,
        in_specs=[_HBM], out_specs=_HBM,
        out_shape=jax.ShapeDtypeStruct(buf.shape, buf.dtype),
        input_output_aliases={0: 0},
        scratch_shapes=[pltpu.SemaphoreType.DMA((n_copies,)), pltpu.SemaphoreType.DMA((n_copies,))],
    )(buf)


def _add_halves(part, got, cvec, name):
    ns, r, cdim = part.shape
    hr = r // 2
    tr = _row_tile(hr, TR_ELT)
    nblk = hr // tr

    def body(c_ref, a_ref, b_ref, o_ref):
        o_ref[...] = a_ref[...] + b_ref[...]

    grid_spec = pltpu.PrefetchScalarGridSpec(
        num_scalar_prefetch=1, grid=(ns, nblk),
        in_specs=[pl.BlockSpec((None, tr, cdim), lambda s, i, c_ref: (s, c_ref[0] * nblk + i, 0)),
                  pl.BlockSpec((None, tr, cdim), lambda s, i, c_ref: (s, i, 0))],
        out_specs=pl.BlockSpec((None, tr, cdim), lambda s, i, c_ref: (s, i, 0)))
    return pl.pallas_call(
        body, name=name, grid_spec=grid_spec,
        out_shape=jax.ShapeDtypeStruct((ns, hr, cdim), BF16),
        compiler_params=_params(("parallel", "parallel")),
    )(cvec, part, got)


def _scatter_copy(sums_ref, land_ref, k, src_slot, dst_slot, c, send_sems, recv_sems, to):
    return pltpu.make_async_remote_copy(
        src_ref=sums_ref.at[src_slot], dst_ref=land_ref.at[dst_slot, _half_rows(land_ref, c)],
        send_sem=send_sems.at[k], recv_sem=recv_sems.at[k], device_id=to, device_id_type=MESH)


def _scatter_start(sums, name):
    ns, hr, cdim = sums.shape
    land = lax.empty((ns, 2 * hr, cdim), sums.dtype)

    def body(sums_ref, land_ref, send_sems, recv_sems, sums_thru, land_thru):
        x, y, c, chips = _mesh_pos()
        me = _chip_id(x, y)
        for k, chip in enumerate(chips):
            _scatter_copy(sums_ref, land_ref, k, _chip_id(*chip), me, c, send_sems, recv_sems, (*chip, c)).start()

    return pl.pallas_call(
        body, name=name,
        in_specs=[_HBM, _HBM], out_specs=[_SEM, _SEM, _HBM, _HBM],
        out_shape=[pltpu.SemaphoreType.DMA((3,)), pltpu.SemaphoreType.DMA((3,)),
                   pltpu.HBM(sums.shape, sums.dtype), pltpu.HBM(land.shape, land.dtype)],
        input_output_aliases={0: 2, 1: 3},
        compiler_params=pltpu.CompilerParams(has_side_effects=_EFFECT),
    )(pltpu.with_memory_space_constraint(sums, pltpu.HBM), pltpu.with_memory_space_constraint(land, pltpu.HBM))


def _scatter_wait(send_sems, recv_sems, sums, land, after, name):
    def body(sums_ref, land_ref, send_ref, recv_ref, after_ref, sums_out, land_out):
        x, y, c, chips = _mesh_pos()
        me = _chip_id(x, y)
        for k, chip in enumerate(chips):
            _scatter_copy(sums_ref, land_ref, k, _chip_id(*chip), me, c, send_ref, recv_ref, (*chip, c)).wait_send()
        for k, chip in enumerate(chips):
            _scatter_copy(sums_ref, land_ref, k, me, _chip_id(*chip), c, send_ref, recv_ref, (*chip, c)).wait_recv()

    return pl.pallas_call(
        body, name=name,
        in_specs=[_HBM, _HBM, _SEM, _SEM, _ANY], out_specs=[_HBM, _HBM],
        out_shape=[pltpu.HBM(sums.shape, sums.dtype), pltpu.HBM(land.shape, land.dtype)],
        input_output_aliases={0: 0, 1: 1},
        compiler_params=pltpu.CompilerParams(has_side_effects=_EFFECT),
    )(sums, land, send_sems, recv_sems, after)


def _complete_plan(n_weights):
    def plan(refs, send_sems, recv_sems):
        x, y, c, chips = _mesh_pos()
        me = _chip_id(x, y)
        sibling = (x, y, 1 - c)
        sends, recvs = [], []
        for w in range(n_weights):
            sums, land = refs[2 * w], refs[2 * w + 1]
            sends.append(_scatter_copy(sums, land, 4 * w + 3, me, me, c, send_sems, recv_sems, sibling))
            recvs.append(_scatter_copy(sums, land, 4 * w + 3, me, me, 1 - c, send_sems, recv_sems, sibling))
            for k, chip in enumerate(chips):
                slot = _chip_id(*chip)
                sends.append(_chip_copy(land, 4 * w + k, slot, _half_rows(land, c), send_sems, recv_sems, sibling))
                recvs.append(_chip_copy(land, 4 * w + k, slot, _half_rows(land, 1 - c), send_sems, recv_sems, sibling))
        return sends, recvs
    return plan


SMALL_ROWS = 8


N_DEVICES = 8


def _small_pack(gl2g, gl2b, gl1g, gl1b, g_ac, gcw, gsink, loss):
    d = gl2g.shape[1]
    hd = d // 2
    nq = gsink.shape[1]

    def body(a_ref, b_ref, c_ref, d_ref, e_ref, cw_ref, sk_ref, ls_ref, out_ref, mine):
        x, y, c, _ = _mesh_pos()
        me = 4 * x + 2 * y + c
        mine[...] = jnp.zeros_like(mine)
        mine[0:1, :] = a_ref[...]
        mine[1:2, :] = b_ref[...]
        mine[2:3, :] = c_ref[...]
        mine[3:4, :] = d_ref[...]
        mine[4:5, :] = e_ref[...]
        mine[5:6, 0:hd] = cw_ref[0:1, :]
        mine[5:6, hd:d] = cw_ref[1:2, :]
        mine[6:7, 0:hd] = cw_ref[2:3, :]
        mine[6:7, hd:hd + nq] = sk_ref[...]
        mine[6:7, hd + 128:hd + 256] = ls_ref[...]
        out_ref[...] = jnp.zeros_like(out_ref)
        out_ref[pl.ds(me, 1)] = mine[...][None]

    return pl.pallas_call(
        body, name="small_pack",
        in_specs=[_VMEM] * 8, out_specs=_VMEM,
        out_shape=jax.ShapeDtypeStruct((N_DEVICES, SMALL_ROWS, d), F32),
        scratch_shapes=[pltpu.VMEM((SMALL_ROWS, d), F32)],
    )(gl2g, gl2b, gl1g, gl1b, g_ac, gcw, gsink, loss)


def _small_plan():
    def plan(refs, send_sems, recv_sems):
        x, y, c, _ = _mesh_pos()
        me = 4 * x + 2 * y + c
        (gath,) = refs
        sends, recvs = [], []
        for r in range(1, N_DEVICES):
            peer = ((1 - x) if r & 4 else x, (1 - y) if r & 2 else y, (1 - c) if r & 1 else c)
            peer_id = 4 * peer[0] + 2 * peer[1] + peer[2]
            for slot, into in ((me, sends), (peer_id, recvs)):
                into.append(pltpu.make_async_remote_copy(
                    src_ref=gath.at[slot], dst_ref=gath.at[slot], send_sem=send_sems.at[r - 1],
                    recv_sem=recv_sems.at[r - 1], device_id=peer, device_id_type=MESH))
        return sends, recvs
    return plan


def _small_sum(gath):
    def body(gath_ref, out_ref):
        total = gath_ref[0]
        for dev in range(1, N_DEVICES):
            total = total + gath_ref[dev]
        out_ref[...] = total

    return pl.pallas_call(
        body, name="small_sum", in_specs=[_VMEM], out_specs=_VMEM,
        out_shape=jax.ShapeDtypeStruct(gath.shape[1:], F32),
    )(gath)


def _adamw(w, g, m, v):
    m = ADAM_B1 * m + (1.0 - ADAM_B1) * g
    v = ADAM_B2 * v + (1.0 - ADAM_B2) * (g * g)
    m_hat = m / (1.0 - ADAM_B1 ** ADAM_STEP)
    v_hat = v / (1.0 - ADAM_B2 ** ADAM_STEP)
    delta = -ADAM_LR * (m_hat / (jnp.sqrt(v_hat) + ADAM_EPS) + ADAM_WD * w)
    return delta, m, v


def _adamw_shard(w, m, v, land, own, pos_vec, name, col_block=0):
    tr = _row_tile(w.shape[1] // 2, TR_ELT)
    grid = (w.shape[1] // tr,)
    body, in_specs, out_specs, out_shape = _adamw_passenger(w.shape, tr, grid, col_block)
    grid_spec = pltpu.PrefetchScalarGridSpec(num_scalar_prefetch=1, grid=grid, in_specs=in_specs, out_specs=out_specs)
    return pl.pallas_call(
        body, name=name, grid_spec=grid_spec, out_shape=out_shape,
        compiler_params=_params(("parallel",)),
    )(pos_vec, w, m, v, land, land, land, land, own)


def _adamw_passenger(shape, tr, grid, col_block):
    _, r, c = shape
    nh = r // 2 // tr
    n_blocks = 2 * nh
    n_steps = int(np.prod(grid))
    assert nh * tr * 2 == r and n_blocks <= n_steps

    def step_of(ids):
        step = ids[0]
        for n, i in zip(grid[1:], ids[1:]):
            step = step * n + i
        return step

    def block_of(ids):
        return jnp.minimum(step_of(ids), n_blocks - 1)

    def update(pos_ref, w_ref, m_ref, v_ref, l0, l1, l2, l3, own_ref, g_out, d_out, m_out, v_out):
        i = block_of([pl.program_id(a) for a in range(len(grid))])
        mine = (i // nh) == pos_ref[1]
        own_blk = own_ref[...].astype(F32)
        g = None
        for s, l_ref in enumerate([l0, l1, l2, l3]):
            term = jnp.where(mine & (pos_ref[0] == s), own_blk, l_ref[...].astype(F32))
            g = term if g is None else g + term
        delta, nm, nv = _adamw(w_ref[...], g, m_ref[...], v_ref[...])
        g_out[...] = g
        d_out[...] = delta
        m_out[...] = nm
        v_out[...] = nv

    def body(*refs):
        if n_blocks == n_steps:
            update(*refs)
        else:
            pl.when(step_of([pl.program_id(a) for a in range(len(grid))]) < n_blocks)(lambda: update(*refs))

    def land_spec(s):
        def index(*args):
            i, pos_ref = block_of(args[:-1]), args[-1]
            skip = (pos_ref[0] == s) & ((i // nh) == pos_ref[1])
            return (s, jnp.where(skip, (i + nh) % n_blocks, i), col_block)
        return pl.BlockSpec((None, tr, c), index)

    blk = pl.BlockSpec((None, tr, c), lambda *args: (0, block_of(args[:-1]), 0))
    in_specs = ([blk, blk, blk] + [land_spec(s) for s in range(N_CHIPS)]
                + [pl.BlockSpec((None, tr, c), lambda *args: (args[-1][0], block_of(args[:-1]) % nh, col_block))])
    return body, in_specs, [blk] * 4, [jax.ShapeDtypeStruct((1, r, c), F32)] * 4


def _call_with_adamw(body, name, grid, in_specs, out_specs, out_shape, scratch_shapes, semantics, operands, shard):
    if shard is None:
        return pl.pallas_call(
            body, name=name, grid=grid, in_specs=in_specs, out_specs=out_specs, out_shape=out_shape,
            scratch_shapes=scratch_shapes, compiler_params=_params(semantics))(*operands)
    w, m, v, land, own, pos_vec, col_block = shard
    n_steps = int(np.prod(grid))
    hr = w.shape[1] // 2
    tr = min(t for t in range(16, hr + 1, 16) if hr % t == 0 and 2 * (hr // t) <= n_steps)
    adam_body, adam_in, adam_out, adam_shape = _adamw_passenger(w.shape, tr, grid, col_block)
    n_in, n_out = len(in_specs), len(out_specs)

    def with_pos(spec):
        if spec.index_map is None:
            return spec
        return pl.BlockSpec(spec.block_shape, lambda *args: spec.index_map(*args[:-1]))

    def both(pos_ref, *refs):
        ins, adam_ins = refs[:n_in], refs[n_in:n_in + len(adam_in)]
        refs = refs[n_in + len(adam_in):]
        outs, adam_outs, scratch = refs[:n_out], refs[n_out:n_out + len(adam_out)], refs[n_out + len(adam_out):]
        body(*ins, *outs, *scratch)
        adam_body(pos_ref, *adam_ins, *adam_outs)

    grid_spec = pltpu.PrefetchScalarGridSpec(
        num_scalar_prefetch=1, grid=grid, in_specs=[with_pos(sp) for sp in in_specs] + adam_in,
        out_specs=[with_pos(sp) for sp in out_specs] + adam_out, scratch_shapes=scratch_shapes)
    return pl.pallas_call(
        both, name=name, grid_spec=grid_spec, out_shape=list(out_shape) + adam_shape,
        compiler_params=_params(semantics),
    )(pos_vec, *operands, w, m, v, land, land, land, land, own)


def _adamw_small(red, params):
    names = ["sinks", "g_attn", "g_conv", "ln1_g", "ln1_b", "ln2_g", "ln2_b", "conv_w"]
    d = red.shape[1]
    hd = d // 2
    flat = []
    for nme in names:
        flat.extend(params[nme])
    nq = params["sinks"][0].shape[1]
    cs = params["conv_w"][0].shape[2]

    def body(*refs):
        red_ref = refs[0]
        ins = refs[1:1 + 3 * len(names)]
        outs = refs[1 + 3 * len(names):]
        x, y, _, _ = _mesh_pos()
        me = _chip_id(x, y)

        def conv_tap(row, base):
            picked = red_ref[row:row + 1, base:base + cs]
            for s in range(1, N_CHIPS):
                picked = jnp.where(me == s, red_ref[row:row + 1, base + s * cs:base + (s + 1) * cs], picked)
            return picked

        grads = {
            "sinks": red_ref[6:7, hd:hd + nq],
            "g_attn": red_ref[4:5, 0:hd],
            "g_conv": red_ref[4:5, hd:d],
            "ln1_g": red_ref[2:3, :],
            "ln1_b": red_ref[3:4, :],
            "ln2_g": red_ref[0:1, :],
            "ln2_b": red_ref[1:2, :],
        }
        for i, nme in enumerate(names):
            w_ref, m_ref, v_ref = ins[3 * i:3 * i + 3]
            g_out, d_out, m_out, v_out = outs[4 * i:4 * i + 4]
            if nme == "conv_w":
                for tap, (row, base) in enumerate([(5, 0), (5, hd), (6, 0)]):
                    g = conv_tap(row, base)
                    delta, nm, nv = _adamw(w_ref[0, tap:tap + 1, :], g, m_ref[0, tap:tap + 1, :], v_ref[0, tap:tap + 1, :])
                    g_out[0, tap:tap + 1, :] = g
                    d_out[0, tap:tap + 1, :] = delta
                    m_out[0, tap:tap + 1, :] = nm
                    v_out[0, tap:tap + 1, :] = nv
            else:
                g = grads[nme]
                delta, nm, nv = _adamw(w_ref[...], g, m_ref[...], v_ref[...])
                g_out[...] = g
                d_out[...] = delta
                m_out[...] = nm
                v_out[...] = nv

    out_shape = []
    for nme in names:
        out_shape.extend([jax.ShapeDtypeStruct(params[nme][0].shape, F32)] * 4)
    outs = pl.pallas_call(
        body, name="adamw_small",
        in_specs=[_VMEM] * (1 + len(flat)), out_specs=[_VMEM] * len(out_shape),
        out_shape=out_shape,
    )(red, *flat)
    return {nme: tuple(outs[4 * i:4 * i + 4]) for i, nme in enumerate(names)}


def _rope_tables(pos_col):
    s = pos_col.shape[0]
    w = N_KV_HEADS * HEAD_DIM
    tb = min(512, s)
    inv_freq = (ROPE_THETA ** (-np.arange(0, ROT_DIM, 2, dtype=np.float32) / ROT_DIM)).astype(np.float32)

    def body(pos_ref, cos_ref, sin_ref):
        pos = pos_ref[...].astype(F32)
        lane = lax.broadcasted_iota(jnp.int32, (tb, PAIR), 1) & (HEAD_DIM - 1)
        fidx = lane & (ROT_DIM // 2 - 1)
        inv = jnp.zeros((tb, PAIR), F32)
        for k in range(ROT_DIM // 2):
            inv = jnp.where(fidx == k, float(inv_freq[k]), inv)
        ang = pos * inv
        rot = lane < ROT_DIM
        sin_v = jnp.sin(ang)
        cos_ref[...] = _tile_lanes(jnp.where(rot, jnp.cos(ang), 1.0), w // PAIR)
        sin_ref[...] = _tile_lanes(jnp.where(lane < ROT_DIM // 2, -sin_v, jnp.where(rot, sin_v, 0.0)), w // PAIR)

    return pl.pallas_call(
        body, name="rope_tables", grid=(s // tb,),
        in_specs=[pl.BlockSpec((tb, 1), lambda i: (i, 0))],
        out_specs=[pl.BlockSpec((tb, w), lambda i: (i, 0))] * 2,
        out_shape=[jax.ShapeDtypeStruct((s, w), F32)] * 2,
        compiler_params=_params(("parallel",)),
    )(pos_col)


def _in_proj(x, w_in_g, first_vec, n_shards, into, name):
    _, s, d = x.shape
    ns, _, ncol = w_in_g.shape
    tm = min(2 * TM, s)
    first_call = into is None
    assert n_shards == 1 or not first_call

    def body(first_ref, x_ref, w_ref, into_ref, o_ref, *x16_ref):
        xb = x_ref[...].astype(BF16)
        o_ref[...] = _dot(xb, w_ref[...]).astype(BF16)
        for ref in x16_ref:
            ref[...] = xb

    shard = lambda j, first_ref: lax.rem(first_ref[0] + j, ns)
    x_spec = pl.BlockSpec((None, tm, d), lambda i, j, first_ref: (0, i, 0))
    grid_spec = pltpu.PrefetchScalarGridSpec(
        num_scalar_prefetch=1, grid=(s // tm, n_shards),
        in_specs=[x_spec, pl.BlockSpec((None, d, ncol), lambda i, j, first_ref: (shard(j, first_ref), 0, 0)), _ANY],
        out_specs=[pl.BlockSpec((tm, ncol), lambda i, j, first_ref: (i, shard(j, first_ref)))] + [x_spec] * first_call)
    return pl.pallas_call(
        body, name=name, grid_spec=grid_spec,
        out_shape=[jax.ShapeDtypeStruct((s, ns * ncol), BF16)] + [jax.ShapeDtypeStruct((1, s, d), BF16)] * first_call,
        input_output_aliases={} if first_call else {3: 0},
        compiler_params=_params(("parallel", "arbitrary")),
    )(first_vec, x, w_in_g, first_vec if first_call else into)


PAIR = 2 * HEAD_DIM
KEYS = 2 * WINDOW


def _pair_operand(t_all, h):
    col = (h // 2) * PAIR
    lane = lax.broadcasted_iota(jnp.int32, (KEYS, PAIR), 1)
    own_low = h % 2 == 0
    mine = jnp.where((lane < HEAD_DIM) if own_low else (lane >= HEAD_DIM), t_all[:, col:col + PAIR], 0.0)
    other = pltpu.roll(mine, HEAD_DIM, 1)
    low, high = (mine, other) if own_low else (other, mine)
    return jnp.concatenate([low, high], axis=0).astype(BF16)


def _pair_grad(acc, h):
    lane = lax.broadcasted_iota(jnp.int32, (KEYS, PAIR), 1)
    low = jnp.where(lane < HEAD_DIM, acc[:KEYS], 0.0)
    high = jnp.where(lane >= HEAD_DIM, acc[KEYS:], 0.0)
    if h % 2 == 0:
        return low + pltpu.roll(high, HEAD_DIM, 1)
    return high + pltpu.roll(low, HEAD_DIM, 1)


N_PAIRS = N_KV_HEADS * GROUP // 2


def _all_probs(q, kk2s, first, sinks_ref):
    assert ATTN_SCALE == 0.125
    q = q * ATTN_SCALE
    qps, scores = [], []
    for pair in range(N_PAIRS):
        qp = q[:, pair * PAIR:(pair + 1) * PAIR].astype(BF16)
        qps.append(qp)
        scores.append(_dot_nt(qp, kk2s[pair // (GROUP // 2)]))
    qi = lax.broadcasted_iota(jnp.int32, (WINDOW, 2 * KEYS), 0)
    kj = lax.broadcasted_iota(jnp.int32, (WINDOW, 2 * KEYS), 1) & (KEYS - 1)
    rel = qi + WINDOW - kj
    valid = (rel >= 0) & (rel < WINDOW) & jnp.logical_not(first & (kj < WINDOW))
    bias = jnp.where(valid, 0.0, NEG_BIG)
    s = (jnp.stack(scores, axis=0) + bias[None]).reshape(N_PAIRS * WINDOW, 2 * KEYS)
    probs, p_sinks = [], []
    for t in range(2):
        st = s[:, t * KEYS:(t + 1) * KEYS]
        sink = jnp.concatenate([jnp.broadcast_to(sinks_ref[0:1, 2 * pair + t:2 * pair + t + 1], (WINDOW, 1))
                                for pair in range(N_PAIRS)], axis=0)
        m = jnp.maximum(jnp.max(st, axis=1, keepdims=True), sink)
        e = jnp.exp(st - m)
        e_sink = jnp.exp(sink - m)
        inv_l = 1.0 / (jnp.sum(e, axis=1, keepdims=True) + e_sink)
        probs.append(e * inv_l)
        p_sinks.append(e_sink * inv_l)
    return qps, jnp.concatenate(probs, axis=1), p_sinks


def _roped_qkv(cur_ref, prev_ref, cos_ref, sin_ref, cosp_ref, sinp_ref, qw, kvw):
    cur = cur_ref[...].astype(F32)
    cos, sin = cos_ref[...], sin_ref[...]
    cos_q, sin_q = _tile_lanes(cos, GROUP), _tile_lanes(sin, GROUP)
    q = _rope(cur[:, :qw], cos_q, sin_q, 1.0)
    prev = prev_ref[...].astype(F32)
    k_all = jnp.concatenate([_rope(prev[:, :kvw], cosp_ref[...], sinp_ref[...], 1.0),
                             _rope(cur[:, qw:qw + kvw], cos, sin, 1.0)], axis=0)
    v_all = jnp.concatenate([prev[:, kvw:], cur[:, qw + kvw:]], axis=0)
    return q, k_all, v_all, cos_q, sin_q


def _attention_fwd(proj, cos_t, sin_t, sinks):
    s = proj.shape[0]
    qw = GROUP * N_KV_HEADS * HEAD_DIM
    kvw = N_KV_HEADS * HEAD_DIM
    nb = s // WINDOW

    def body(cur_ref, prev_ref, cos_ref, sin_ref, cosp_ref, sinp_ref, sinks_ref, o_ref):
        first = pl.program_id(0) == 0
        q, k_all, v_all, _, _ = _roped_qkv(cur_ref, prev_ref, cos_ref, sin_ref, cosp_ref, sinp_ref, qw, kvw)
        kk2s = [_pair_operand(k_all, h) for h in range(N_KV_HEADS)]
        vv2s = [_pair_operand(v_all, h) for h in range(N_KV_HEADS)]
        _, probs, _ = _all_probs(q, kk2s, first, sinks_ref)
        probs = probs.astype(BF16)
        outs = [_dot(probs[pair * WINDOW:(pair + 1) * WINDOW], vv2s[pair // (GROUP // 2)]) for pair in range(N_PAIRS)]
        o_ref[...] = jnp.concatenate(outs, axis=1)

    tbl = pl.BlockSpec((WINDOW, kvw), lambda n: (n, 0))
    tbl_prev = pl.BlockSpec((WINDOW, kvw), lambda n: (jnp.maximum(n - 1, 0), 0))
    return pl.pallas_call(
        body, name="attention_fwd", grid=(nb,),
        in_specs=[pl.BlockSpec((WINDOW, qw + 2 * kvw), lambda n: (n, 0)),
                  pl.BlockSpec((WINDOW, 2 * kvw), lambda n: (jnp.maximum(n - 1, 0), (qw // (2 * kvw)))),
                  tbl, tbl, tbl_prev, tbl_prev, _VMEM],
        out_specs=pl.BlockSpec((WINDOW, qw), lambda n: (n, 0)),
        out_shape=jax.ShapeDtypeStruct((s, qw), F32),
        compiler_params=_params(("parallel",)),
    )(proj, proj, cos_t, sin_t, cos_t, sin_t, sinks)


def _conv_taps(cw_ref):
    return [jnp.concatenate([cw_ref[s, k:k + 1, :] for s in range(N_CHIPS)], axis=1) for k in range(3)]


def _shift_down(z, halo, steps):
    last = halo.shape[0]
    row = lax.broadcasted_iota(jnp.int32, z.shape, 0)
    out = pltpu.roll(z, steps, 0)
    for r in range(steps):
        out = jnp.where(row == r, halo[last - steps + r:last - steps + r + 1, :], out)
    return out


def _shift_up(z, halo, steps):
    rows = z.shape[0]
    row = lax.broadcasted_iota(jnp.int32, z.shape, 0)
    out = pltpu.roll(z, rows - steps, 0)
    for r in range(steps):
        out = jnp.where(row == rows - steps + r, halo[r:r + 1, :], out)
    return out


def _split_cbu(lo, hi, cw):
    lo, hi = lo.astype(F32), hi.astype(F32)
    c_gate = lo[:, :cw]
    b_gate = jnp.concatenate([lo[:, cw:], hi[:, :2 * cw - lo.shape[1]]], axis=1)
    u = hi[:, 2 * cw - lo.shape[1]:]
    return c_gate, b_gate, u


def _conv_norm(proj, attn, cw_full, g_ac):
    s, in_w = proj.shape
    cw = attn.shape[1]
    blk_w = in_w // 3
    tb = min(TB_CONV, s)

    def body(lo_ref, hi_ref, lo_h_ref, hi_h_ref, attn_ref, cw_ref, g_ref, mixed_ref, ac_ref, rstd_ref):
        i = pl.program_id(0)
        c_gate, b_gate, u = _split_cbu(lo_ref[...], hi_ref[...], cw)
        c_h, _, u_h = _split_cbu(lo_h_ref[...], hi_h_ref[...], cw)
        z = c_gate * u
        z_h = jnp.where(i == 0, 0.0, c_h * u_h)
        w0, w1, w2 = _conv_taps(cw_ref)
        y = w0 * _shift_down(z, z_h, 2) + w1 * _shift_down(z, z_h, 1) + w2 * z
        conv = b_gate * y
        a = attn_ref[...]
        r_a = lax.rsqrt(jnp.mean(a * a, axis=-1, keepdims=True) + RMS_EPS)
        r_c = lax.rsqrt(jnp.mean(conv * conv, axis=-1, keepdims=True) + RMS_EPS)
        g = g_ref[...]
        mixed_ref[...] = jnp.concatenate([a * r_a * g[:, :cw], conv * r_c * g[:, cw:]], axis=1).astype(BF16)
        ac_ref[...] = jnp.concatenate([a, conv], axis=1)
        rstd_ref[0] = r_a
        rstd_ref[1] = r_c

    halo_idx = lambda i: jnp.maximum(i * (tb // HALO_ROWS) - 1, 0)
    return pl.pallas_call(
        body, name="conv_norm", grid=(s // tb,),
        in_specs=[pl.BlockSpec((tb, blk_w), lambda i: (i, 1)),
                  pl.BlockSpec((tb, blk_w), lambda i: (i, 2)),
                  pl.BlockSpec((HALO_ROWS, blk_w), lambda i: (halo_idx(i), 1)),
                  pl.BlockSpec((HALO_ROWS, blk_w), lambda i: (halo_idx(i), 2)),
                  pl.BlockSpec((tb, cw), lambda i: (i, 0)),
                  _VMEM, _VMEM],
        out_specs=[pl.BlockSpec((tb, 2 * cw), lambda i: (i, 0)),
                   pl.BlockSpec((tb, 2 * cw), lambda i: (i, 0)),
                   pl.BlockSpec((2, tb, 1), lambda i: (0, i, 0))],
        out_shape=[jax.ShapeDtypeStruct((s, 2 * cw), BF16), jax.ShapeDtypeStruct((s, 2 * cw), F32),
                   jax.ShapeDtypeStruct((2, s, 1), F32)],
        compiler_params=_params(("parallel",)),
    )(proj, proj, proj, proj, attn, cw_full, g_ac)


def _out_proj_ln(mixed, w_out_g, x, ln_g, ln_b):
    s, d = mixed.shape
    tm = min(TM, s)
    tk = d
    nk = d // tk

    def body(a_ref, w_ref, x_ref, g_ref, b_ref, xhat_ref, h_ref, rstd_ref, acc):
        k = pl.program_id(1)
        _accumulate(acc, lambda: _dot(a_ref[...], w_ref[...]), k, nk)

        @pl.when(k == nk - 1)
        def _():
            def rows_fn(rows):
                xhat, rstd = _ln_fwd(ALPHA * x_ref[rows, :] + acc[rows, :])
                xhat_ref[rows, :] = xhat
                h_ref[rows, :] = (xhat * g_ref[...] + b_ref[...]).astype(BF16)
                rstd_ref[rows, :] = rstd

            _for_row_chunks(tm, rows_fn)

    row = pl.BlockSpec((tm, d), lambda i, k: (i, 0))
    return pl.pallas_call(
        body, name="out_proj_ln", grid=(s // tm, nk),
        in_specs=[pl.BlockSpec((tm, tk), lambda i, k: (i, k)),
                  pl.BlockSpec((tk, d), lambda i, k: (k, 0)),
                  pl.BlockSpec((None, tm, d), lambda i, k: (0, i, 0)),
                  _VMEM, _VMEM],
        out_specs=[row, row, pl.BlockSpec((tm, 1), lambda i, k: (i, 0))],
        out_shape=[jax.ShapeDtypeStruct((s, d), F32), jax.ShapeDtypeStruct((s, d), BF16),
                   jax.ShapeDtypeStruct((s, 1), F32)],
        scratch_shapes=[pltpu.VMEM((tm, d), F32)],
        compiler_params=_params(("parallel", "arbitrary")),
    )(mixed, w_out_g, x, ln_g, ln_b)


def _gate_up(h1, w_gu_g, first_vec, n_shards, into, name):
    s, d = h1.shape
    ns, _, fs2 = w_gu_g.shape
    fs = fs2 // 2
    tm = min(TM, s)

    def body(first_ref, h_ref, w_ref, act_in, ab_in, act_ref, ab_ref):
        gu = _dot(h_ref[...], w_ref[...])
        g, u = gu[:, :fs], gu[:, fs:]
        sg = _sigmoid(g)
        silu = g * sg
        act_ref[...] = (silu * u).astype(BF16)
        ab_ref[:, :fs] = (u * (sg * (1.0 + g * (1.0 - sg)))).astype(BF16)
        ab_ref[:, fs:] = silu.astype(BF16)

    shard = lambda j, first_ref: lax.rem(first_ref[0] + j, ns)
    grid_spec = pltpu.PrefetchScalarGridSpec(
        num_scalar_prefetch=1, grid=(s // tm, n_shards),
        in_specs=[pl.BlockSpec((tm, d), lambda i, j, first_ref: (i, 0)),
                  pl.BlockSpec((None, d, fs2), lambda i, j, first_ref: (shard(j, first_ref), 0, 0)), _ANY, _ANY],
        out_specs=[pl.BlockSpec((tm, fs), lambda i, j, first_ref: (i, shard(j, first_ref))),
                   pl.BlockSpec((tm, fs2), lambda i, j, first_ref: (i, shard(j, first_ref)))])
    return pl.pallas_call(
        body, name=name, grid_spec=grid_spec,
        out_shape=[jax.ShapeDtypeStruct((s, ns * fs), BF16), jax.ShapeDtypeStruct((s, ns * fs2), BF16)],
        input_output_aliases={} if into is None else {3: 0, 4: 1},
        compiler_params=_params(("parallel", "arbitrary")),
    )(first_vec, h1, w_gu_g, *((first_vec, first_vec) if into is None else into))


def _down_ln_loss(act, w_down_g, xhat1, ln1_g, ln1_b, ln2_g, ln2_b, target):
    s, f = act.shape
    d = xhat1.shape[1]
    tm = min(TM, s)
    tk = f // N_CHIPS
    nk = f // tk

    def body(a_ref, w_ref, xh_ref, g1_ref, b1_ref, g2_ref, b2_ref, t_ref, dpre_ref, dpre16_ref, loss_ref, gg_ref, gb_ref,
             acc):
        i, k = pl.program_id(0), pl.program_id(1)
        _accumulate(acc, lambda: _dot(a_ref[...], w_ref[...]), k, nk)

        @pl.when(k == nk - 1)
        def _():
            @pl.when(i == 0)
            def _():
                loss_ref[...] = jnp.zeros_like(loss_ref)
                gg_ref[...] = jnp.zeros_like(gg_ref)
                gb_ref[...] = jnp.zeros_like(gb_ref)

            def rows_fn(rows):
                h1 = xh_ref[rows, :] * g1_ref[...] + b1_ref[...]
                xhat, rstd = _ln_fwd(ALPHA * h1 + acc[rows, :])
                g2 = g2_ref[...]
                diff = xhat * g2 + b2_ref[...] - t_ref[rows, :]
                dy = diff * (1.0 / d)
                dpre = _ln_bwd(dy, xhat, rstd, g2)
                dpre_ref[rows, :] = dpre
                dpre16_ref[rows, :] = dpre.astype(BF16)
                sq = jnp.sum(jnp.sum(diff * diff, axis=1, keepdims=True), axis=0, keepdims=True)
                loss_ref[...] += jnp.broadcast_to(sq * (0.5 / d), (1, 128))
                gg_ref[...] += jnp.sum(dy * xhat, axis=0, keepdims=True)
                gb_ref[...] += jnp.sum(dy, axis=0, keepdims=True)

            _for_row_chunks(tm, rows_fn)

    row = pl.BlockSpec((tm, d), lambda i, k: (i, 0))
    vec = pl.BlockSpec((1, d), lambda i, k: (0, 0))
    return pl.pallas_call(
        body, name="down_ln_loss", grid=(s // tm, nk),
        in_specs=[pl.BlockSpec((tm, tk), lambda i, k: (i, k)),
                  pl.BlockSpec((tk, d), lambda i, k: (k, 0)),
                  row, _VMEM, _VMEM, _VMEM, _VMEM,
                  pl.BlockSpec((None, tm, d), lambda i, k: (0, i, 0))],
        out_specs=[row, row, pl.BlockSpec((1, 128), lambda i, k: (0, 0)), vec, vec],
        out_shape=[jax.ShapeDtypeStruct((s, d), F32), jax.ShapeDtypeStruct((s, d), BF16),
                   jax.ShapeDtypeStruct((1, 128), F32), jax.ShapeDtypeStruct((1, d), F32),
                   jax.ShapeDtypeStruct((1, d), F32)],
        scratch_shapes=[pltpu.VMEM((tm, d), F32)],
        compiler_params=_params(("arbitrary", "arbitrary")),
    )(act, w_down_g, xhat1, ln1_g, ln1_b, ln2_g, ln2_b, target)


def _dact_silu_bwd(dpre2, w_down_g, ab):
    s, d = dpre2.shape
    fs2 = ab.shape[1] // N_CHIPS
    fs = fs2 // 2
    tm = min(TM, s)

    def body(dp_ref, w_ref, ab_ref, dgu_ref):
        d_act = _dot_nt(dp_ref[...], w_ref[...])
        dgu_ref[:, :fs] = (d_act * ab_ref[:, :fs].astype(F32)).astype(BF16)
        dgu_ref[:, fs:] = (d_act * ab_ref[:, fs:].astype(F32)).astype(BF16)

    blk = pl.BlockSpec((tm, fs2), lambda j, i: (i, j))
    return pl.pallas_call(
        body, name="dact_silu_bwd", grid=(N_CHIPS, s // tm),
        in_specs=[pl.BlockSpec((tm, d), lambda j, i: (i, 0)),
                  pl.BlockSpec((fs, d), lambda j, i: (j, 0)), blk],
        out_specs=blk,
        out_shape=jax.ShapeDtypeStruct(ab.shape, BF16),
        compiler_params=_params(("parallel", "parallel")),
    )(dpre2, w_down_g, ab)


def _grad_rows(a, b, after, name, row_blocks=1):
    s, m = a.shape
    n = b.shape[1]
    ms = m // N_CHIPS
    tmw = ms // row_blocks
    tk = min(TK_TOK, s)
    nk = s // tk

    def body(a_ref, b_ref, after_ref, o_ref, acc):
        k = pl.program_id(2)
        _accumulate(acc, lambda: _dot_tn(a_ref[...].astype(BF16), b_ref[...].astype(BF16)), k, nk)

        @pl.when(k == nk - 1)
        def _():
            o_ref[...] = acc[...].astype(BF16)

    return pl.pallas_call(
        body, name=name, grid=(N_CHIPS, row_blocks, nk),
        in_specs=[pl.BlockSpec((tk, tmw), lambda j, r, k: (k, j * row_blocks + r)),
                  pl.BlockSpec((tk, n), lambda j, r, k: (k, 0)), _ANY],
        out_specs=pl.BlockSpec((None, tmw, n), lambda j, r, k: (j, r, 0)),
        out_shape=jax.ShapeDtypeStruct((N_CHIPS, ms, n), BF16),
        scratch_shapes=[pltpu.VMEM((tmw, n), F32)],
        compiler_params=_params(("parallel", "parallel", "arbitrary")),
    )(a, b, after)


def _grad_cols(a, bs, after, name, a_3d=False, row_blocks=2, shard=None):
    s, m = a.shape[-2:]
    n = bs[0].shape[1]
    ns = n // N_CHIPS
    nb = len(bs)
    tmw = m // row_blocks
    tk = min(TK_TOK, s)
    nk = s // tk

    def body(*refs):
        a_ref, b_refs, o_refs, accs = refs[0], refs[1:1 + nb], refs[2 + nb:2 + 2 * nb], refs[2 + 2 * nb:]
        k = pl.program_id(2)
        for b_ref, acc in zip(b_refs, accs):
            _accumulate(acc, lambda b_ref=b_ref: _dot_tn(a_ref[...].astype(BF16), b_ref[...].astype(BF16)), k, nk)

        @pl.when(k == nk - 1)
        def _():
            for o_ref, acc in zip(o_refs, accs):
                o_ref[...] = acc[...].astype(BF16)

    if a_3d:
        a_spec = pl.BlockSpec((None, tk, tmw), lambda j, r, k: (0, k, r))
    else:
        a_spec = pl.BlockSpec((tk, tmw), lambda j, r, k: (k, r))
    return _call_with_adamw(
        body, name, (N_CHIPS, row_blocks, nk),
        [a_spec] + [pl.BlockSpec((tk, ns), lambda j, r, k: (k, j))] * nb + [_ANY],
        [pl.BlockSpec((None, tmw, ns), lambda j, r, k: (j, r, 0))] * nb,
        [jax.ShapeDtypeStruct((N_CHIPS, m, ns), BF16)] * nb,
        [pltpu.VMEM((tmw, ns), F32)] * nb, ("parallel", "parallel", "arbitrary"), (a, *bs, after), shard)


def _dh1_ln_bwd(d_gu, w_gu_g, dpre2, xhat1, rstd1, ln1_g, after):
    s = d_gu.shape[0]
    d = dpre2.shape[1]
    hd = d // 2
    fs = w_gu_g.shape[2]
    tm = min(TM, s)

    def body(dgu_ref, w_ref, dp2_ref, xh_ref, rs_ref, g_ref, after_ref, dpre_ref, dpre16_ref, gg_ref, gb_ref, acc_lo,
             acc_hi):
        i, j, half = pl.program_id(0), pl.program_id(1), pl.program_id(2)

        def product():
            return _dot_nt(dgu_ref[...], w_ref[...])

        @pl.when(half == 0)
        def _():
            _accumulate(acc_lo, product, j, N_CHIPS)

        @pl.when(half == 1)
        def _():
            _accumulate(acc_hi, product, j, N_CHIPS)

        @pl.when((j == N_CHIPS - 1) & (half == 1))
        def _():
            @pl.when(i == 0)
            def _():
                gg_ref[...] = jnp.zeros_like(gg_ref)
                gb_ref[...] = jnp.zeros_like(gb_ref)

            def rows_fn(rows):
                dh = jnp.concatenate([acc_lo[rows, :], acc_hi[rows, :]], axis=1) + ALPHA * dp2_ref[rows, :]
                xhat = xh_ref[rows, :]
                dpre = _ln_bwd(dh, xhat, rs_ref[rows, :], g_ref[...])
                dpre_ref[rows, :] = dpre
                dpre16_ref[rows, :] = dpre.astype(BF16)
                gg_ref[...] += jnp.sum(dh * xhat, axis=0, keepdims=True)
                gb_ref[...] += jnp.sum(dh, axis=0, keepdims=True)

            _for_row_chunks(tm, rows_fn)

    row = pl.BlockSpec((tm, d), lambda i, j, h: (i, 0))
    vec = pl.BlockSpec((1, d), lambda i, j, h: (0, 0))
    act_blk = pl.BlockSpec((tm, fs), lambda i, j, h: (i, j))
    w_blk = pl.BlockSpec((None, hd, fs), lambda i, j, h: (j, h, 0))
    return pl.pallas_call(
        body, name="dh1_ln_bwd", grid=(s // tm, N_CHIPS, 2),
        in_specs=[act_blk, w_blk, row, row, pl.BlockSpec((tm, 1), lambda i, j, h: (i, 0)), _VMEM, _ANY],
        out_specs=[row, row, vec, vec],
        out_shape=[jax.ShapeDtypeStruct((s, d), F32), jax.ShapeDtypeStruct((s, d), BF16),
                   jax.ShapeDtypeStruct((1, d), F32), jax.ShapeDtypeStruct((1, d), F32)],
        scratch_shapes=[pltpu.VMEM((tm, hd), F32)] * 2,
        compiler_params=_params(("arbitrary", "arbitrary", "arbitrary")),
    )(d_gu, w_gu_g, dpre2, xhat1, rstd1, ln1_g, after)


def _dmixed_rms_bwd(dpre1, w_out_g, ac, rstd, g_ac):
    s, d = dpre1.shape
    hd = d // 2
    tm = min(TM, s)

    def body(dp_ref, w_ref, ac_ref, rs_ref, g_ref, dac_ref, gg_ref):
        i = pl.program_id(1)
        dm = _dot_nt(dp_ref[...].astype(BF16), w_ref[...])
        pre = ac_ref[...]
        r = rs_ref[...]
        gdm = dm * g_ref[...]
        dac_ref[...] = r * gdm - pre * (r * r * r) * jnp.mean(gdm * pre, axis=-1, keepdims=True)
        gg = jnp.sum(dm * pre * r, axis=0, keepdims=True)

        @pl.when(i == 0)
        def _():
            gg_ref[...] = gg

        @pl.when(i > 0)
        def _():
            gg_ref[...] += gg

    return pl.pallas_call(
        body, name="dmixed_rms_bwd", grid=(2, s // tm),
        in_specs=[pl.BlockSpec((tm, d), lambda h, i: (i, 0)),
                  pl.BlockSpec((hd, d), lambda h, i: (h, 0)),
                  pl.BlockSpec((tm, hd), lambda h, i: (i, h)),
                  pl.BlockSpec((None, tm, 1), lambda h, i: (h, i, 0)),
                  pl.BlockSpec((1, hd), lambda h, i: (0, h))],
        out_specs=[pl.BlockSpec((tm, hd), lambda h, i: (i, h)),
                   pl.BlockSpec((1, hd), lambda h, i: (0, h))],
        out_shape=[jax.ShapeDtypeStruct((s, d), F32), jax.ShapeDtypeStruct((1, d), F32)],
        compiler_params=_params(("arbitrary", "arbitrary")),
    )(dpre1, w_out_g, ac, rstd, g_ac)


def _attention_bwd(proj, d_ac, cos_t, sin_t, sinks, after, shard):
    s = proj.shape[0]
    qw = GROUP * N_KV_HEADS * HEAD_DIM
    kvw = N_KV_HEADS * HEAD_DIM
    nb = s // WINDOW
    nq = GROUP * N_KV_HEADS

    def body(cur_ref, prev_ref, do_ref, cos_ref, sin_ref, cosp_ref, sinp_ref, sinks_ref, after_ref,
             dq_ref, dcur_ref, dprev_ref, dsink_ref):
        n = pl.program_id(0)
        first = n == 0
        q, k_all, v_all, cos_q, sin_q = _roped_qkv(cur_ref, prev_ref, cos_ref, sin_ref, cosp_ref, sinp_ref, qw, kvw)
        kk2s = [_pair_operand(k_all, h) for h in range(N_KV_HEADS)]
        vv2s = [_pair_operand(v_all, h) for h in range(N_KV_HEADS)]
        qps, probs, p_sinks = _all_probs(q, kk2s, first, sinks_ref)
        dops = [do_ref[:, pair * PAIR:(pair + 1) * PAIR].astype(BF16) for pair in range(N_PAIRS)]
        d_probs = jnp.concatenate([_dot_nt(dops[pair], vv2s[pair // (GROUP // 2)]) for pair in range(N_PAIRS)], axis=0)
        d_s, ds_sinks = [], []
        for t in range(2):
            cols = slice(t * KEYS, (t + 1) * KEYS)
            delta = jnp.sum(probs[:, cols] * d_probs[:, cols], axis=1, keepdims=True)
            d_s.append(probs[:, cols] * (d_probs[:, cols] - delta))
            ds_sinks.append(-p_sinks[t] * delta)
        d_s = jnp.concatenate(d_s, axis=1).astype(BF16)
        probs = probs.astype(BF16)
        dq_parts, dk_tiles, dv_tiles, dsink_parts = [], [], [], []
        for h in range(N_KV_HEADS):
            dkk2, dvv2 = None, None
            for p in range(GROUP // 2):
                pair = (GROUP // 2) * h + p
                rows = slice(pair * WINDOW, (pair + 1) * WINDOW)
                dq_parts.append(_dot(d_s[rows], kk2s[h]) * ATTN_SCALE)
                dk_term = _dot_tn(d_s[rows], qps[pair])
                dv_term = _dot_tn(probs[rows], dops[pair])
                dkk2 = dk_term if dkk2 is None else dkk2 + dk_term
                dvv2 = dv_term if dvv2 is None else dvv2 + dv_term
                dsink_parts.extend([jnp.sum(ds_sinks[t][rows], axis=0, keepdims=True) for t in range(2)])
            dk_tiles.append(_pair_grad(dkk2, h))
            dv_tiles.append(_pair_grad(dvv2, h))
        dq_ref[...] = _rope(jnp.concatenate(dq_parts, axis=1), cos_q, sin_q, -1.0)
        dk = jnp.concatenate([dk_tiles[0] + dk_tiles[1], dk_tiles[2] + dk_tiles[3]], axis=1)
        dv = jnp.concatenate([dv_tiles[0] + dv_tiles[1], dv_tiles[2] + dv_tiles[3]], axis=1)
        dprev_ref[...] = jnp.concatenate([dk[:WINDOW], dv[:WINDOW]], axis=1)
        dcur_ref[...] = jnp.concatenate([dk[WINDOW:], dv[WINDOW:]], axis=1)
        dsink = jnp.concatenate(dsink_parts, axis=1)

        @pl.when(first)
        def _():
            dsink_ref[...] = dsink

        @pl.when(n > 0)
        def _():
            dsink_ref[...] += dsink

    tbl = pl.BlockSpec((WINDOW, kvw), lambda n: (n, 0))
    tbl_prev = pl.BlockSpec((WINDOW, kvw), lambda n: (jnp.maximum(n - 1, 0), 0))
    kv_blk = pl.BlockSpec((WINDOW, 2 * kvw), lambda n: (n, 0))
    return _call_with_adamw(
        body, "attention_bwd", (nb,),
        [pl.BlockSpec((WINDOW, qw + 2 * kvw), lambda n: (n, 0)),
         pl.BlockSpec((WINDOW, 2 * kvw), lambda n: (jnp.maximum(n - 1, 0), (qw // (2 * kvw)))),
         pl.BlockSpec((WINDOW, qw), lambda n: (n, 0)),
         tbl, tbl, tbl_prev, tbl_prev, _VMEM, _ANY],
        [pl.BlockSpec((WINDOW, qw), lambda n: (n, 0)), kv_blk, kv_blk, pl.BlockSpec((1, nq), lambda n: (0, 0))],
        [jax.ShapeDtypeStruct((s, qw), F32), jax.ShapeDtypeStruct((s, 2 * kvw), F32),
         jax.ShapeDtypeStruct((s, 2 * kvw), F32), jax.ShapeDtypeStruct((1, nq), F32)],
        [], ("arbitrary",), (proj, proj, d_ac, cos_t, sin_t, cos_t, sin_t, sinks, after), shard)


def _dproj_assemble(proj, d_ac, dq, dkv_cur, dkv_prev, cos_t, sin_t, cw_full):
    s, in_w = proj.shape
    cw = dq.shape[1]
    kvw = N_KV_HEADS * HEAD_DIM
    blk_w = in_w // 3
    tb = WINDOW
    nb = s // tb

    def body(lo_ref, hi_ref, lo_p_ref, hi_p_ref, lo_n_ref, hi_n_ref, dconv_ref, dconv_n_ref,
             dq_ref, dcur_ref, dprev_n_ref, cos_ref, sin_ref, cw_ref, dproj_ref, gcw_ref):
        i = pl.program_id(0)
        last = i == nb - 1
        c_gate, b_gate, u = _split_cbu(lo_ref[...], hi_ref[...], cw)
        c_p, _, u_p = _split_cbu(lo_p_ref[...], hi_p_ref[...], cw)
        _, b_n, _ = _split_cbu(lo_n_ref[...], hi_n_ref[...], cw)
        z = c_gate * u
        z_p = jnp.where(i == 0, 0.0, c_p * u_p)
        z1 = _shift_down(z, z_p, 1)
        z2 = _shift_down(z, z_p, 2)
        w0, w1, w2 = _conv_taps(cw_ref)
        y = w0 * z2 + w1 * z1 + w2 * z
        d_conv = dconv_ref[...]
        d_b = d_conv * y
        d_y = d_conv * b_gate
        d_y_n = jnp.where(last, 0.0, dconv_n_ref[...] * b_n[:dconv_n_ref.shape[0]])
        d_z = w2 * d_y + w1 * _shift_up(d_y, d_y_n, 1) + w0 * _shift_up(d_y, d_y_n, 2)
        d_c = d_z * u
        d_u = d_z * c_gate
        gcw = jnp.concatenate([jnp.sum(d_y * z2, axis=0, keepdims=True), jnp.sum(d_y * z1, axis=0, keepdims=True),
                               jnp.sum(d_y * z, axis=0, keepdims=True)], axis=0)

        @pl.when(i == 0)
        def _():
            gcw_ref[...] = gcw

        @pl.when(i > 0)
        def _():
            gcw_ref[...] += gcw

        dkv = dcur_ref[...] + jnp.where(last, 0.0, dprev_n_ref[...])
        dk = _rope(dkv[:, :kvw], cos_ref[...], sin_ref[...], -1.0)
        dproj_ref[...] = jnp.concatenate([dq_ref[...], dk, dkv[:, kvw:], d_c, d_b, d_u], axis=1).astype(BF16)

    prev_halo = lambda i: jnp.maximum(i * (tb // HALO_ROWS) - 1, 0)
    next_halo = lambda i: jnp.minimum((i + 1) * (tb // HALO_ROWS), s // HALO_ROWS - 1)
    next8 = lambda i: jnp.minimum((i + 1) * (tb // 8), s // 8 - 1)
    nxt = lambda i: jnp.minimum(i + 1, nb - 1)
    return pl.pallas_call(
        body, name="dproj_assemble", grid=(nb,),
        in_specs=[pl.BlockSpec((tb, blk_w), lambda i: (i, 1)),
                  pl.BlockSpec((tb, blk_w), lambda i: (i, 2)),
                  pl.BlockSpec((HALO_ROWS, blk_w), lambda i: (prev_halo(i), 1)),
                  pl.BlockSpec((HALO_ROWS, blk_w), lambda i: (prev_halo(i), 2)),
                  pl.BlockSpec((HALO_ROWS, blk_w), lambda i: (next_halo(i), 1)),
                  pl.BlockSpec((HALO_ROWS, blk_w), lambda i: (next_halo(i), 2)),
                  pl.BlockSpec((tb, cw), lambda i: (i, 1)),
                  pl.BlockSpec((8, cw), lambda i: (next8(i), 1)),
                  pl.BlockSpec((tb, cw), lambda i: (i, 0)),
                  pl.BlockSpec((tb, 2 * kvw), lambda i: (i, 0)),
                  pl.BlockSpec((tb, 2 * kvw), lambda i: (nxt(i), 0)),
                  pl.BlockSpec((tb, kvw), lambda i: (i, 0)),
                  pl.BlockSpec((tb, kvw), lambda i: (i, 0)),
                  _VMEM],
        out_specs=[pl.BlockSpec((tb, in_w), lambda i: (i, 0)),
                   pl.BlockSpec((3, cw), lambda i: (0, 0))],
        out_shape=[jax.ShapeDtypeStruct((s, in_w), BF16), jax.ShapeDtypeStruct((3, cw), F32)],
        compiler_params=_params(("arbitrary",)),
    )(proj, proj, proj, proj, proj, proj, d_ac, d_ac, dq, dkv_cur, dkv_prev, cos_t, sin_t, cw_full)


def _dx(d_proj, w_in_g, dpre1, after, shard):
    s, in_w = d_proj.shape
    ns, d, ncol = w_in_g.shape
    tm = min(TM, s)

    def body(dp_ref, w_ref, r_ref, after_ref, o_ref, acc):
        j = pl.program_id(1)
        @pl.when(j == 0)
        def _():
            acc[...] = ALPHA * r_ref[...]

        acc[...] += _dot_nt(dp_ref[...], w_ref[...])

        @pl.when(j == ns - 1)
        def _():
            o_ref[...] = acc[...]

    return _call_with_adamw(
        body, "dx", (s // tm, ns),
        [pl.BlockSpec((tm, ncol), lambda i, j: (i, j)),
         pl.BlockSpec((None, d, ncol), lambda i, j: (j, 0, 0)),
         pl.BlockSpec((tm, d), lambda i, j: (i, 0)), _ANY],
        [pl.BlockSpec((None, tm, d), lambda i, j: (0, i, 0))], [jax.ShapeDtypeStruct((1, s, d), F32)],
        [pltpu.VMEM((tm, d), F32)], ("parallel", "arbitrary"), (d_proj, w_in_g, dpre1, after), shard)


def kernel(x, positions, w_in, conv_w, sinks, g_attn, g_conv, w_out, ln1_g, ln1_b, w_gate, w_up, w_down, ln2_g, ln2_b, loss_target, m_w_in, m_conv_w, m_sinks, m_g_attn, m_g_conv, m_w_out, m_ln1_g, m_ln1_b, m_w_gate, m_w_up, m_w_down, m_ln2_g, m_ln2_b, v_w_in, v_conv_w, v_sinks, v_g_attn, v_g_conv, v_w_out, v_ln1_g, v_ln1_b, v_w_gate, v_w_up, v_w_down, v_ln2_g, v_ln2_b):
    s = x.shape[1]
    d = x.shape[2]

    chip_vec = _chip_id(lax.axis_index("x"), lax.axis_index("y")).astype(jnp.int32).reshape(1)
    wnames = ["w_in", "w_out", "w_gu", "w_down"]
    buf_in = _cast_weight(w_in, chip_vec, chip_vec, "cast_w_in")
    flight_in, token_in = _gather_start([buf_in], chip_vec, "gather_start_w_in")
    cw_buf = lax.dynamic_update_slice(jnp.zeros((N_CHIPS,) + conv_w.shape[1:], F32), conv_w, (chip_vec[0], 0, 0))
    cw_flight = _flight_start("conv_w_start", [cw_buf], _conv_w_plan(), 3, token_in)
    started = cw_flight[2][0]
    buf_gu = _cast_weight(w_gate, chip_vec, started, "cast_w_gate", 0, 2)
    buf_gu = _cast_weight(w_up, chip_vec, buf_gu, "cast_w_up", 1, 2)
    bufs = [_cast_weight(w_out, chip_vec, started, "cast_w_out"), buf_gu,
            _cast_weight(w_down, chip_vec, started, "cast_w_down")]
    flights_rest, token = _gather_start(bufs, token_in, "gather_start_rest")
    flights = flight_in + flights_rest

    def gathered(i, after):
        send_sems, recv_sems, buf = flights[i]
        buf = _gather_wait(send_sems, recv_sems, buf, after, "gather_wait_" + wnames[i])
        return _sibling_fill(buf, "sibling_fill_" + wnames[i])

    g_ac = jnp.concatenate([g_attn, g_conv], axis=1)

    proj_own, x16 = _in_proj(x, _after(flights[0][2], token), chip_vec, 1, None, "in_proj_own")
    cos_t, sin_t = _rope_tables(positions.reshape(s, 1) + token[0:1, 0:1].astype(jnp.int32))
    w_in_g = gathered(0, _after(cos_t, proj_own))
    (proj,) = _in_proj(x16, w_in_g, chip_vec + 1, N_CHIPS - 1, proj_own, "in_proj_rest")
    send_sems, recv_sems, buf_out = flights[1]
    buf_out = _gather_wait(send_sems, recv_sems, buf_out, proj, "gather_wait_w_out")
    fill_out = _flight_start("fill_start_w_out", [buf_out], _fill_plan(1), 3, chip_vec)
    attn = _attention_fwd(_after(proj, fill_out[2][0]), cos_t, sin_t, sinks)
    (cw_full,) = _flight_wait("conv_w_wait", cw_flight, _conv_w_plan(), attn)
    mixed, ac, rstd_ac = _conv_norm(proj, attn, cw_full, g_ac)
    (w_out_g,) = _flight_wait("fill_wait_w_out", fill_out, _fill_plan(1), mixed)
    w_out_full = w_out_g.reshape(d, d)
    xhat1, h1, rstd1 = _out_proj_ln(mixed, w_out_full, x, ln1_g, ln1_b)
    send_sems, recv_sems, buf_gu = flights[2]
    buf_gu = _gather_wait(send_sems, recv_sems, buf_gu, h1, "gather_wait_w_gu")
    fill_gu = _flight_start("fill_start_w_gu", [buf_gu], _fill_plan(1), 3, chip_vec)
    own = _gate_up(h1, fill_gu[2][0], chip_vec, 1, None, "gate_up_own")
    (w_gu_g,) = _flight_wait("fill_wait_w_gu", fill_gu, _fill_plan(1), own[0])
    some = _gate_up(h1, w_gu_g, chip_vec + 1, N_CHIPS - 2, own, "gate_up_rest")
    send_sems, recv_sems, buf_down = flights[3]
    buf_down = _gather_wait(send_sems, recv_sems, buf_down, some[0], "gather_wait_w_down")
    fill_down = _flight_start("fill_start_w_down", [buf_down], _fill_plan(1), 3, chip_vec)
    act, ab = _gate_up(h1, _after(w_gu_g, fill_down[2][0]), chip_vec + N_CHIPS - 1, 1, some, "gate_up_last")
    (w_down_g,) = _flight_wait("fill_wait_w_down", fill_down, _fill_plan(1), act)
    w_down_full = w_down_g.reshape(-1, d)
    dpre2, dpre2_16, loss_part, g_ln2_g, g_ln2_b = _down_ln_loss(act, w_down_full, xhat1, ln1_g, ln1_b, ln2_g, ln2_b,
                                                                 loss_target)

    cvec = lax.axis_index("c").astype(jnp.int32).reshape(1)

    def exchange_begin(parts, nme):
        bufs = []
        for part in parts:
            ns, r, cdim = part.shape
            bufs.extend([part, lax.empty((ns, r // 2, cdim), part.dtype)])
        return _flight_start("exchange_start_" + nme, bufs, _exchange_plan(len(parts)), len(parts), cvec)

    def exchange_end(flight, n_parts, after, nme):
        bufs = _flight_wait("exchange_wait_" + nme, flight, _exchange_plan(n_parts), after)
        return [(bufs[2 * w], bufs[2 * w + 1]) for w in range(n_parts)]

    def scatter_begin(part, got, nme):
        return _scatter_start(_add_halves(part, got, cvec, "add_halves_" + nme), "scatter_start_" + nme)

    d_gu = _dact_silu_bwd(dpre2_16, w_down_full, ab)
    p_down = _grad_rows(act, dpre2_16, d_gu, "grad_w_down")
    x_down = exchange_begin([p_down], "w_down")
    (p_gu,) = _grad_cols(h1, [d_gu], x_down[2][0], "grad_w_gate_up")
    ((p_down, got),) = exchange_end(x_down, 1, p_gu, "w_down")
    f_down = scatter_begin(p_down, got, "w_down")
    x_gu = exchange_begin([_after(p_gu, f_down[2])], "w_gu")
    dpre1, dpre1_16, g_ln1_g, g_ln1_b = _dh1_ln_bwd(d_gu, w_gu_g, dpre2, xhat1, rstd1, ln1_g, x_gu[2][0])
    ((p_gu, got),) = exchange_end(x_gu, 1, dpre1, "w_gu")
    f_gu = scatter_begin(p_gu, got, "w_gu")
    d_ac, g_g_ac = _dmixed_rms_bwd(_after(dpre1_16, f_gu[2]), w_out_full, ac, rstd_ac, g_ac)
    pos_vec = jnp.concatenate([chip_vec, cvec])
    sums, land = _scatter_wait(*f_down, d_ac, "scatter_wait_w_down")
    c_down = _flight_start("complete_start_w_down", [sums, land], _complete_plan(1), 4, cvec)
    p_out = _grad_rows(mixed, dpre1_16, c_down[2][1], "grad_w_out")
    x_out = exchange_begin([p_out], "w_out")
    sums, land = _flight_wait("complete_wait_w_down", c_down, _complete_plan(1), x_out[2][0])
    dq, dkv_cur, dkv_prev, g_sinks, *new_w_down = _attention_bwd(
        proj, d_ac, cos_t, sin_t, sinks, x_out[2][0], (w_down, m_w_down, v_w_down, land, sums, pos_vec, 0))
    ((p_out, got),) = exchange_end(x_out, 1, dq, "w_out")
    f_out = scatter_begin(p_out, got, "w_out")
    sums, land = _scatter_wait(*f_gu, f_out[2], "scatter_wait_w_gu")
    c_gu = _flight_start("complete_start_w_gu", [sums, land], _complete_plan(1), 4, cvec)
    d_proj, g_conv_w = _dproj_assemble(proj, _after(d_ac, c_gu[2][1]), dq, dkv_cur, dkv_prev, cos_t, sin_t, cw_full)
    small_parts = _small_pack(g_ln2_g, g_ln2_b, g_ln1_g, g_ln1_b, g_g_ac, g_conv_w, g_sinks, loss_part)
    f_small = _flight_start("small_start", [small_parts], _small_plan(), N_DEVICES - 1, cvec)
    sums_gu, land_gu = _flight_wait("complete_wait_w_gu", c_gu, _complete_plan(1), f_small[2][0])
    p_in, *new_w_gate = _grad_cols(x16, [d_proj], f_small[2][0], "grad_w_in", a_3d=True,
                                   shard=(w_gate, m_w_gate, v_w_gate, land_gu, sums_gu, pos_vec, 0))
    (small_parts,) = _flight_wait("small_wait", f_small, _small_plan(), p_in)
    red = _small_sum(small_parts)
    x_in = exchange_begin([_after(p_in, red)], "w_in")
    sums, land = _scatter_wait(*f_out, x_in[2][0], "scatter_wait_w_out")
    c_out = _flight_start("complete_start_w_out", [sums, land], _complete_plan(1), 4, cvec)
    new_w_up = _adamw_shard(w_up, m_w_up, v_w_up, _after(land_gu, c_out[2][1]), sums_gu, pos_vec, "adamw_w_up", 1)
    ((p_in, got),) = exchange_end(x_in, 1, new_w_up[0], "w_in")
    f_in = scatter_begin(p_in, got, "w_in")
    (grad_x,) = _dx(d_proj, w_in_g, dpre1, f_in[2], None)

    big = {"w_down": new_w_down, "w_gate": new_w_gate, "w_up": new_w_up}
    sums, land = _scatter_wait(*f_in, grad_x, "scatter_wait_w_in")
    c_in = _flight_start("complete_start_w_in", [sums, land], _complete_plan(1), 4, cvec)
    sums, land = _flight_wait("complete_wait_w_out", c_out, _complete_plan(1), c_in[2][1])
    big["w_out"] = _adamw_shard(w_out, m_w_out, v_w_out, land, sums, pos_vec, "adamw_w_out")
    sums, land = _flight_wait("complete_wait_w_in", c_in, _complete_plan(1), big["w_out"][0])
    big["w_in"] = _adamw_shard(w_in, m_w_in, v_w_in, land, sums, pos_vec, "adamw_w_in")
    small = _adamw_small(red, {
        "sinks": (sinks, m_sinks, v_sinks), "g_attn": (g_attn, m_g_attn, v_g_attn),
        "g_conv": (g_conv, m_g_conv, v_g_conv), "ln1_g": (ln1_g, m_ln1_g, v_ln1_g),
        "ln1_b": (ln1_b, m_ln1_b, v_ln1_b), "ln2_g": (ln2_g, m_ln2_g, v_ln2_g),
        "ln2_b": (ln2_b, m_ln2_b, v_ln2_b), "conv_w": (conv_w, m_conv_w, v_conv_w)})
    res = {**big, **small}
    order = ["w_in", "conv_w", "sinks", "g_attn", "g_conv", "w_out", "ln1_g", "ln1_b", "w_gate", "w_up", "w_down",
             "ln2_g", "ln2_b"]
    loss = red[6, d // 2 + 128]
    return (loss, grad_x, *[res[n][0] for n in order], *[res[n][1] for n in order],
            *[res[n][2] for n in order], *[res[n][3] for n in order])
```

```python
import functools

import numpy as np
import jax
import jax.numpy as jnp
from jax import lax
from jax.experimental import pallas as pl
from jax.experimental.pallas import tpu as pltpu

F32 = jnp.float32
BF16 = jnp.bfloat16
MESH = pl.DeviceIdType.MESH

HEAD_DIM = 64
N_KV_HEADS = 4
GROUP = 4
WINDOW = 128
ROT_DIM = 16
ROPE_THETA = 500000.0
ATTN_SCALE = HEAD_DIM ** -0.5
ALPHA = 2.0 ** 0.25
LN_EPS = 1e-5
RMS_EPS = 1e-6
ADAM_LR = 0.001
ADAM_B1 = 0.9
ADAM_B2 = 0.999
ADAM_EPS = 1e-08
ADAM_WD = 0.01
ADAM_STEP = 10
N_CHIPS = 4
NEG_BIG = -1e30

V7X_VMEM_BYTES = 64 * 1024 * 1024
VMEM_LIMIT = V7X_VMEM_BYTES - 6 * 1024 * 1024

TM = 512
TK_TOK = 1024
TB_CONV = 512
TR_ELT = 256
ROW_CHUNK = 128
HALO_ROWS = 16


def _params(sem):
    return pltpu.CompilerParams(dimension_semantics=sem, vmem_limit_bytes=VMEM_LIMIT)


def _row_tile(rows, target):
    best = None
    for t in range(16, min(rows, target) + 1, 16):
        if rows % t == 0:
            best = t
    assert best is not None, (rows, target)
    return best


def _dot(a, b):
    return jnp.dot(a, b, preferred_element_type=F32)


def _dot_nt(a, b):
    return lax.dot_general(a, b, (((1,), (1,)), ((), ())), preferred_element_type=F32)


def _dot_tn(a, b):
    return lax.dot_general(a, b, (((0,), (0,)), ((), ())), preferred_element_type=F32)


def _mesh_pos():
    x, y, c = lax.axis_index("x"), lax.axis_index("y"), lax.axis_index("c")
    chips = [(1 - x, y), (x, 1 - y), (1 - x, 1 - y)]
    return x, y, c, chips


def _chip_id(px, py):
    return 2 * px + py


def _rope(t, cos, sgn_sin, sign):
    w = t.shape[1]
    lane = lax.broadcasted_iota(jnp.int32, t.shape, 1) & (HEAD_DIM - 1)
    partner = jnp.where(lane < ROT_DIM // 2, pltpu.roll(t, w - ROT_DIM // 2, 1), pltpu.roll(t, ROT_DIM // 2, 1))
    return t * cos + sign * (partner * sgn_sin)


def _tile_lanes(t, n):
    return jnp.concatenate([t] * n, axis=1)


def _sigmoid(g):
    return 1.0 / (1.0 + jnp.exp(-g))


def _for_row_chunks(n_rows, fn):
    def step(r, carry):
        fn(pl.ds(pl.multiple_of(r * ROW_CHUNK, ROW_CHUNK), ROW_CHUNK))
        return carry

    lax.fori_loop(0, n_rows // ROW_CHUNK, step, 0)


def _accumulate(acc, make_val, k, nk):
    if nk == 1:
        acc[...] = make_val()
        return

    @pl.when(k == 0)
    def _():
        acc[...] = jnp.zeros_like(acc)

    acc[...] += make_val()


def _ln_fwd(pre):
    mu = jnp.mean(pre, axis=-1, keepdims=True)
    cen = pre - mu
    var = jnp.mean(cen * cen, axis=-1, keepdims=True)
    rstd = lax.rsqrt(var + LN_EPS)
    return cen * rstd, rstd


def _ln_bwd(dy, xhat, rstd, g):
    dxhat = dy * g
    m1 = jnp.mean(dxhat, axis=-1, keepdims=True)
    m2 = jnp.mean(dxhat * xhat, axis=-1, keepdims=True)
    return rstd * (dxhat - m1 - xhat * m2)


def _cast_weight(w, chip_vec, after, name, col_block=0, n_col_blocks=1):
    _, r, c = w.shape
    tr = _row_tile(r, TR_ELT)

    def body(chip_ref, w_ref, after_ref, o_ref):
        o_ref[...] = w_ref[...].astype(BF16)

    grid_spec = pltpu.PrefetchScalarGridSpec(
        num_scalar_prefetch=1, grid=(r // tr,),
        in_specs=[pl.BlockSpec((None, tr, c), lambda i, chip_ref: (0, i, 0)), _ANY],
        out_specs=pl.BlockSpec((None, tr, c), lambda i, chip_ref: (chip_ref[0], i, col_block)))
    return pl.pallas_call(
        body, name=name, grid_spec=grid_spec,
        out_shape=jax.ShapeDtypeStruct((N_CHIPS, r, n_col_blocks * c), BF16),
        input_output_aliases={2: 0} if col_block else {},
        compiler_params=_params(("parallel",)),
    )(chip_vec, w, after)


_HBM = pl.BlockSpec(memory_space=pltpu.HBM)
_VMEM = pl.BlockSpec(memory_space=pltpu.VMEM)


_SEM = pl.BlockSpec(memory_space=pltpu.SEMAPHORE)
_ANY = pl.BlockSpec(memory_space=pl.ANY)
_EFFECT = pltpu.SideEffectType.DATAFLOW_SIDE_EFFECTING


def _chip_copy(buf, k, chip_of_src, half_rows, send_sems, recv_sems, to):
    part = buf.at[chip_of_src, half_rows]
    return pltpu.make_async_remote_copy(
        src_ref=part, dst_ref=part, send_sem=send_sems.at[k], recv_sem=recv_sems.at[k], device_id=to, device_id_type=MESH)


def _half_rows(buf, which):
    hr = buf.shape[1] // 2
    return pl.ds(which * hr, hr)


def _after(value, dep):
    return lax.optimization_barrier((value, dep))[0]


def _flight_start(name, bufs, plan, n_sems, after):
    n = len(bufs)

    def body(*refs):
        sends, _ = plan(refs[:n], refs[n + 1], refs[n + 2])
        for cp in sends:
            cp.start()

    outs = pl.pallas_call(
        body, name=name,
        in_specs=[_HBM] * n + [_ANY], out_specs=[_SEM, _SEM] + [_HBM] * n,
        out_shape=[pltpu.SemaphoreType.DMA((n_sems,))] * 2 + [pltpu.HBM(b.shape, b.dtype) for b in bufs],
        input_output_aliases={i: 2 + i for i in range(n)},
        compiler_params=pltpu.CompilerParams(has_side_effects=_EFFECT),
    )(*[pltpu.with_memory_space_constraint(b, pltpu.HBM) for b in bufs], after)
    return outs[0], outs[1], list(outs[2:])


def _flight_wait(name, flight, plan, after):
    send_sems, recv_sems, bufs = flight
    n = len(bufs)

    def body(*refs):
        sends, recvs = plan(refs[:n], refs[n], refs[n + 1])
        for cp in sends:
            cp.wait_send()
        for cp in recvs:
            cp.wait_recv()

    outs = pl.pallas_call(
        body, name=name,
        in_specs=[_HBM] * n + [_SEM, _SEM, _ANY], out_specs=[_HBM] * n,
        out_shape=[pltpu.HBM(b.shape, b.dtype) for b in bufs],
        input_output_aliases={i: i for i in range(n)},
        compiler_params=pltpu.CompilerParams(has_side_effects=_EFFECT),
    )(*bufs, send_sems, recv_sems, after)
    return list(outs)


def _fill_plan(n_bufs):
    def plan(refs, send_sems, recv_sems):
        x, y, c, chips = _mesh_pos()
        sibling = (x, y, 1 - c)
        sends, recvs = [], []
        for w in range(n_bufs):
            for k, chip in enumerate(chips):
                slot = _chip_id(*chip)
                sends.append(_chip_copy(refs[w], 3 * w + k, slot, _half_rows(refs[w], c), send_sems, recv_sems, sibling))
                recvs.append(_chip_copy(refs[w], 3 * w + k, slot, _half_rows(refs[w], 1 - c), send_sems, recv_sems,
                                        sibling))
        return sends, recvs
    return plan


def _conv_w_plan():
    def plan(refs, send_sems, recv_sems):
        x, y, c, chips = _mesh_pos()
        me = _chip_id(x, y)
        (buf,) = refs
        sends, recvs = [], []
        for k, chip in enumerate(chips):
            for slot, into in ((me, sends), (_chip_id(*chip), recvs)):
                into.append(pltpu.make_async_remote_copy(
                    src_ref=buf.at[slot], dst_ref=buf.at[slot], send_sem=send_sems.at[k], recv_sem=recv_sems.at[k],
                    device_id=(*chip, c), device_id_type=MESH))
        return sends, recvs
    return plan


def _exchange_plan(n_parts):
    def plan(refs, send_sems, recv_sems):
        x, y, c, _ = _mesh_pos()
        copies = []
        for w in range(n_parts):
            part, got = refs[2 * w], refs[2 * w + 1]
            hr = got.shape[1]
            copies.append(pltpu.make_async_remote_copy(
                src_ref=part.at[:, pl.ds((1 - c) * hr, hr)], dst_ref=got, send_sem=send_sems.at[w],
                recv_sem=recv_sems.at[w], device_id=(x, y, 1 - c), device_id_type=MESH))
        return copies, copies
    return plan


def _gather_start(bufs, after, name):
    n = len(bufs)

    def body(*refs):
        ins = refs[:n]
        sends, recvs = refs[n + 1:2 * n + 1], refs[2 * n + 1:3 * n + 1]
        token = refs[4 * n + 1]
        x, y, c, chips = _mesh_pos()
        me = _chip_id(x, y)
        for w in range(n):
            for k, chip in enumerate(chips):
                _chip_copy(ins[w], k, me, _half_rows(ins[w], c), sends[w], recvs[w], (*chip, c)).start()
        token[...] = jnp.zeros_like(token)

    outs = pl.pallas_call(
        body, name=name,
        in_specs=[_HBM] * n + [_ANY],
        out_specs=[_SEM] * (2 * n) + [_HBM] * n + [_VMEM],
        out_shape=[pltpu.SemaphoreType.DMA((3,))] * (2 * n) + [pltpu.HBM(b.shape, b.dtype) for b in bufs]
        + [jax.ShapeDtypeStruct((8, 128), F32)],
        input_output_aliases={w: 2 * n + w for w in range(n)},
        compiler_params=pltpu.CompilerParams(has_side_effects=_EFFECT),
    )(*[pltpu.with_memory_space_constraint(b, pltpu.HBM) for b in bufs], after)
    return [(outs[w], outs[n + w], outs[2 * n + w]) for w in range(n)], outs[3 * n]


def _gather_wait(send_sems, recv_sems, buf, after, name):
    def body(buf_ref, send_ref, recv_ref, after_ref, out_ref):
        x, y, c, chips = _mesh_pos()
        me = _chip_id(x, y)
        for k, chip in enumerate(chips):
            _chip_copy(buf_ref, k, me, _half_rows(buf_ref, c), send_ref, recv_ref, (*chip, c)).wait_send()
        for k, chip in enumerate(chips):
            _chip_copy(buf_ref, k, _chip_id(*chip), _half_rows(buf_ref, c), send_ref, recv_ref, (*chip, c)).wait_recv()

    return pl.pallas_call(
        body, name=name,
        in_specs=[_HBM, _SEM, _SEM, _ANY], out_specs=_HBM,
        out_shape=pltpu.HBM(buf.shape, buf.dtype),
        input_output_aliases={0: 0},
        compiler_params=pltpu.CompilerParams(has_side_effects=_EFFECT),
    )(buf, send_sems, recv_sems, after)


def _sibling_fill(buf, name, own_too=False):
    n_copies = 4 if own_too else 3

    def body(buf_ref, out_ref, send_sems, recv_sems):
        x, y, c, chips = _mesh_pos()
        sibling = (x, y, 1 - c)
        slots = [_chip_id(*chip) for chip in chips] + ([_chip_id(x, y)] if own_too else [])
        copies = []
        for k, slot in enumerate(slots):
            cp = _chip_copy(out_ref, k, slot, _half_rows(out_ref, c), send_sems, recv_sems, sibling)
            cp.start()
            copies.append(cp)
        for k, slot in enumerate(slots):
            _chip_copy(out_ref, k, slot, _half_rows(out_ref, 1 - c), send_sems, recv_sems, sibling).wait_recv()
        for cp in copies:
            cp.wait_send()

    return pl.pallas_call(
        body, name=name,
        in_specs=[_HBM], out_specs=_HBM,
        out_shape=jax.ShapeDtypeStruct(buf.shape, buf.dtype),
        input_output_aliases={0: 0},
        scratch_shapes=[pltpu.SemaphoreType.DMA((n_copies,)), pltpu.SemaphoreType.DMA((n_copies,))],
    )(buf)


def _add_halves(part, got, cvec, name):
    ns, r, cdim = part.shape
    hr = r // 2
    tr = _row_tile(hr, TR_ELT)
    nblk = hr // tr

    def body(c_ref, a_ref, b_ref, o_ref):
        o_ref[...] = a_ref[...] + b_ref[...]

    grid_spec = pltpu.PrefetchScalarGridSpec(
        num_scalar_prefetch=1, grid=(ns, nblk),
        in_specs=[pl.BlockSpec((None, tr, cdim), lambda s, i, c_ref: (s, c_ref[0] * nblk + i, 0)),
                  pl.BlockSpec((None, tr, cdim), lambda s, i, c_ref: (s, i, 0))],
        out_specs=pl.BlockSpec((None, tr, cdim), lambda s, i, c_ref: (s, i, 0)))
    return pl.pallas_call(
        body, name=name, grid_spec=grid_spec,
        out_shape=jax.ShapeDtypeStruct((ns, hr, cdim), BF16),
        compiler_params=_params(("parallel", "parallel")),
    )(cvec, part, got)


def _scatter_copy(sums_ref, land_ref, k, src_slot, dst_slot, c, send_sems, recv_sems, to):
    return pltpu.make_async_remote_copy(
        src_ref=sums_ref.at[src_slot], dst_ref=land_ref.at[dst_slot, _half_rows(land_ref, c)],
        send_sem=send_sems.at[k], recv_sem=recv_sems.at[k], device_id=to, device_id_type=MESH)


def _scatter_start(sums, name):
    ns, hr, cdim = sums.shape
    land = lax.empty((ns, 2 * hr, cdim), sums.dtype)

    def body(sums_ref, land_ref, send_sems, recv_sems, sums_thru, land_thru):
        x, y, c, chips = _mesh_pos()
        me = _chip_id(x, y)
        for k, chip in enumerate(chips):
            _scatter_copy(sums_ref, land_ref, k, _chip_id(*chip), me, c, send_sems, recv_sems, (*chip, c)).start()

    return pl.pallas_call(
        body, name=name,
        in_specs=[_HBM, _HBM], out_specs=[_SEM, _SEM, _HBM, _HBM],
        out_shape=[pltpu.SemaphoreType.DMA((3,)), pltpu.SemaphoreType.DMA((3,)),
                   pltpu.HBM(sums.shape, sums.dtype), pltpu.HBM(land.shape, land.dtype)],
        input_output_aliases={0: 2, 1: 3},
        compiler_params=pltpu.CompilerParams(has_side_effects=_EFFECT),
    )(pltpu.with_memory_space_constraint(sums, pltpu.HBM), pltpu.with_memory_space_constraint(land, pltpu.HBM))


def _scatter_wait(send_sems, recv_sems, sums, land, after, name):
    def body(sums_ref, land_ref, send_ref, recv_ref, after_ref, sums_out, land_out):
        x, y, c, chips = _mesh_pos()
        me = _chip_id(x, y)
        for k, chip in enumerate(chips):
            _scatter_copy(sums_ref, land_ref, k, _chip_id(*chip), me, c, send_ref, recv_ref, (*chip, c)).wait_send()
        for k, chip in enumerate(chips):
            _scatter_copy(sums_ref, land_ref, k, me, _chip_id(*chip), c, send_ref, recv_ref, (*chip, c)).wait_recv()

    return pl.pallas_call(
        body, name=name,
        in_specs=[_HBM, _HBM, _SEM, _SEM, _ANY], out_specs=[_HBM, _HBM],
        out_shape=[pltpu.HBM(sums.shape, sums.dtype), pltpu.HBM(land.shape, land.dtype)],
        input_output_aliases={0: 0, 1: 1},
        compiler_params=pltpu.CompilerParams(has_side_effects=_EFFECT),
    )(sums, land, send_sems, recv_sems, after)


def _complete_plan(n_weights):
    def plan(refs, send_sems, recv_sems):
        x, y, c, chips = _mesh_pos()
        me = _chip_id(x, y)
        sibling = (x, y, 1 - c)
        sends, recvs = [], []
        for w in range(n_weights):
            sums, land = refs[2 * w], refs[2 * w + 1]
            sends.append(_scatter_copy(sums, land, 4 * w + 3, me, me, c, send_sems, recv_sems, sibling))
            recvs.append(_scatter_copy(sums, land, 4 * w + 3, me, me, 1 - c, send_sems, recv_sems, sibling))
            for k, chip in enumerate(chips):
                slot = _chip_id(*chip)
                sends.append(_chip_copy(land, 4 * w + k, slot, _half_rows(land, c), send_sems, recv_sems, sibling))
                recvs.append(_chip_copy(land, 4 * w + k, slot, _half_rows(land, 1 - c), send_sems, recv_sems, sibling))
        return sends, recvs
    return plan


SMALL_ROWS = 8


N_DEVICES = 8


def _small_pack(gl2g, gl2b, gl1g, gl1b, g_ac, gcw, gsink, loss):
    d = gl2g.shape[1]
    hd = d // 2
    nq = gsink.shape[1]

    def body(a_ref, b_ref, c_ref, d_ref, e_ref, cw_ref, sk_ref, ls_ref, out_ref, mine):
        x, y, c, _ = _mesh_pos()
        me = 4 * x + 2 * y + c
        mine[...] = jnp.zeros_like(mine)
        mine[0:1, :] = a_ref[...]
        mine[1:2, :] = b_ref[...]
        mine[2:3, :] = c_ref[...]
        mine[3:4, :] = d_ref[...]
        mine[4:5, :] = e_ref[...]
        mine[5:6, 0:hd] = cw_ref[0:1, :]
        mine[5:6, hd:d] = cw_ref[1:2, :]
        mine[6:7, 0:hd] = cw_ref[2:3, :]
        mine[6:7, hd:hd + nq] = sk_ref[...]
        mine[6:7, hd + 128:hd + 256] = ls_ref[...]
        out_ref[...] = jnp.zeros_like(out_ref)
        out_ref[pl.ds(me, 1)] = mine[...][None]

    return pl.pallas_call(
        body, name="small_pack",
        in_specs=[_VMEM] * 8, out_specs=_VMEM,
        out_shape=jax.ShapeDtypeStruct((N_DEVICES, SMALL_ROWS, d), F32),
        scratch_shapes=[pltpu.VMEM((SMALL_ROWS, d), F32)],
    )(gl2g, gl2b, gl1g, gl1b, g_ac, gcw, gsink, loss)


def _small_plan():
    def plan(refs, send_sems, recv_sems):
        x, y, c, _ = _mesh_pos()
        me = 4 * x + 2 * y + c
        (gath,) = refs
        sends, recvs = [], []
        for r in range(1, N_DEVICES):
            peer = ((1 - x) if r & 4 else x, (1 - y) if r & 2 else y, (1 - c) if r & 1 else c)
            peer_id = 4 * peer[0] + 2 * peer[1] + peer[2]
            for slot, into in ((me, sends), (peer_id, recvs)):
                into.append(pltpu.make_async_remote_copy(
                    src_ref=gath.at[slot], dst_ref=gath.at[slot], send_sem=send_sems.at[r - 1],
                    recv_sem=recv_sems.at[r - 1], device_id=peer, device_id_type=MESH))
        return sends, recvs
    return plan


def _small_sum(gath):
    def body(gath_ref, out_ref):
        total = gath_ref[0]
        for dev in range(1, N_DEVICES):
            total = total + gath_ref[dev]
        out_ref[...] = total

    return pl.pallas_call(
        body, name="small_sum", in_specs=[_VMEM], out_specs=_VMEM,
        out_shape=jax.ShapeDtypeStruct(gath.shape[1:], F32),
    )(gath)


def _adamw(w, g, m, v):
    m = ADAM_B1 * m + (1.0 - ADAM_B1) * g
    v = ADAM_B2 * v + (1.0 - ADAM_B2) * (g * g)
    m_hat = m / (1.0 - ADAM_B1 ** ADAM_STEP)
    v_hat = v / (1.0 - ADAM_B2 ** ADAM_STEP)
    delta = -ADAM_LR * (m_hat / (jnp.sqrt(v_hat) + ADAM_EPS) + ADAM_WD * w)
    return delta, m, v


def _adamw_shard(w, m, v, land, own, pos_vec, name, col_block=0):
    tr = _row_tile(w.shape[1] // 2, TR_ELT)
    grid = (w.shape[1] // tr,)
    body, in_specs, out_specs, out_shape = _adamw_passenger(w.shape, tr, grid, col_block)
    grid_spec = pltpu.PrefetchScalarGridSpec(num_scalar_prefetch=1, grid=grid, in_specs=in_specs, out_specs=out_specs)
    return pl.pallas_call(
        body, name=name, grid_spec=grid_spec, out_shape=out_shape,
        compiler_params=_params(("parallel",)),
    )(pos_vec, w, m, v, land, land, land, land, own)


def _adamw_passenger(shape, tr, grid, col_block):
    _, r, c = shape
    nh = r // 2 // tr
    n_blocks = 2 * nh
    n_steps = int(np.prod(grid))
    assert nh * tr * 2 == r and n_blocks <= n_steps

    def step_of(ids):
        step = ids[0]
        for n, i in zip(grid[1:], ids[1:]):
            step = step * n + i
        return step

    def block_of(ids):
        return jnp.minimum(step_of(ids), n_blocks - 1)

    def update(pos_ref, w_ref, m_ref, v_ref, l0, l1, l2, l3, own_ref, g_out, d_out, m_out, v_out):
        i = block_of([pl.program_id(a) for a in range(len(grid))])
        mine = (i // nh) == pos_ref[1]
        own_blk = own_ref[...].astype(F32)
        g = None
        for s, l_ref in enumerate([l0, l1, l2, l3]):
            term = jnp.where(mine & (pos_ref[0] == s), own_blk, l_ref[...].astype(F32))
            g = term if g is None else g + term
        delta, nm, nv = _adamw(w_ref[...], g, m_ref[...], v_ref[...])
        g_out[...] = g
        d_out[...] = delta
        m_out[...] = nm
        v_out[...] = nv

    def body(*refs):
        if n_blocks == n_steps:
            update(*refs)
        else:
            pl.when(step_of([pl.program_id(a) for a in range(len(grid))]) < n_blocks)(lambda: update(*refs))

    def land_spec(s):
        def index(*args):
            i, pos_ref = block_of(args[:-1]), args[-1]
            skip = (pos_ref[0] == s) & ((i // nh) == pos_ref[1])
            return (s, jnp.where(skip, (i + nh) % n_blocks, i), col_block)
        return pl.BlockSpec((None, tr, c), index)

    blk = pl.BlockSpec((None, tr, c), lambda *args: (0, block_of(args[:-1]), 0))
    in_specs = ([blk, blk, blk] + [land_spec(s) for s in range(N_CHIPS)]
                + [pl.BlockSpec((None, tr, c), lambda *args: (args[-1][0], block_of(args[:-1]) % nh, col_block))])
    return body, in_specs, [blk] * 4, [jax.ShapeDtypeStruct((1, r, c), F32)] * 4


def _call_with_adamw(body, name, grid, in_specs, out_specs, out_shape, scratch_shapes, semantics, operands, shard):
    if shard is None:
        return pl.pallas_call(
            body, name=name, grid=grid, in_specs=in_specs, out_specs=out_specs, out_shape=out_shape,
            scratch_shapes=scratch_shapes, compiler_params=_params(semantics))(*operands)
    w, m, v, land, own, pos_vec, col_block = shard
    n_steps = int(np.prod(grid))
    hr = w.shape[1] // 2
    tr = min(t for t in range(16, hr + 1, 16) if hr % t == 0 and 2 * (hr // t) <= n_steps)
    adam_body, adam_in, adam_out, adam_shape = _adamw_passenger(w.shape, tr, grid, col_block)
    n_in, n_out = len(in_specs), len(out_specs)

    def with_pos(spec):
        if spec.index_map is None:
            return spec
        return pl.BlockSpec(spec.block_shape, lambda *args: spec.index_map(*args[:-1]))

    def both(pos_ref, *refs):
        ins, adam_ins = refs[:n_in], refs[n_in:n_in + len(adam_in)]
        refs = refs[n_in + len(adam_in):]
        outs, adam_outs, scratch = refs[:n_out], refs[n_out:n_out + len(adam_out)], refs[n_out + len(adam_out):]
        body(*ins, *outs, *scratch)
        adam_body(pos_ref, *adam_ins, *adam_outs)

    grid_spec = pltpu.PrefetchScalarGridSpec(
        num_scalar_prefetch=1, grid=grid, in_specs=[with_pos(sp) for sp in in_specs] + adam_in,
        out_specs=[with_pos(sp) for sp in out_specs] + adam_out, scratch_shapes=scratch_shapes)
    return pl.pallas_call(
        both, name=name, grid_spec=grid_spec, out_shape=list(out_shape) + adam_shape,
        compiler_params=_params(semantics),
    )(pos_vec, *operands, w, m, v, land, land, land, land, own)


def _adamw_small(red, params):
    names = ["sinks", "g_attn", "g_conv", "ln1_g", "ln1_b", "ln2_g", "ln2_b", "conv_w"]
    d = red.shape[1]
    hd = d // 2
    flat = []
    for nme in names:
        flat.extend(params[nme])
    nq = params["sinks"][0].shape[1]
    cs = params["conv_w"][0].shape[2]

    def body(*refs):
        red_ref = refs[0]
        ins = refs[1:1 + 3 * len(names)]
        outs = refs[1 + 3 * len(names):]
        x, y, _, _ = _mesh_pos()
        me = _chip_id(x, y)

        def conv_tap(row, base):
            picked = red_ref[row:row + 1, base:base + cs]
            for s in range(1, N_CHIPS):
                picked = jnp.where(me == s, red_ref[row:row + 1, base + s * cs:base + (s + 1) * cs], picked)
            return picked

        grads = {
            "sinks": red_ref[6:7, hd:hd + nq],
            "g_attn": red_ref[4:5, 0:hd],
            "g_conv": red_ref[4:5, hd:d],
            "ln1_g": red_ref[2:3, :],
            "ln1_b": red_ref[3:4, :],
            "ln2_g": red_ref[0:1, :],
            "ln2_b": red_ref[1:2, :],
        }
        for i, nme in enumerate(names):
            w_ref, m_ref, v_ref = ins[3 * i:3 * i + 3]
            g_out, d_out, m_out, v_out = outs[4 * i:4 * i + 4]
            if nme == "conv_w":
                for tap, (row, base) in enumerate([(5, 0), (5, hd), (6, 0)]):
                    g = conv_tap(row, base)
                    delta, nm, nv = _adamw(w_ref[0, tap:tap + 1, :], g, m_ref[0, tap:tap + 1, :], v_ref[0, tap:tap + 1, :])
                    g_out[0, tap:tap + 1, :] = g
                    d_out[0, tap:tap + 1, :] = delta
                    m_out[0, tap:tap + 1, :] = nm
                    v_out[0, tap:tap + 1, :] = nv
            else:
                g = grads[nme]
                delta, nm, nv = _adamw(w_ref[...], g, m_ref[...], v_ref[...])
                g_out[...] = g
                d_out[...] = delta
                m_out[...] = nm
                v_out[...] = nv

    out_shape = []
    for nme in names:
        out_shape.extend([jax.ShapeDtypeStruct(params[nme][0].shape, F32)] * 4)
    outs = pl.pallas_call(
        body, name="adamw_small",
        in_specs=[_VMEM] * (1 + len(flat)), out_specs=[_VMEM] * len(out_shape),
        out_shape=out_shape,
    )(red, *flat)
    return {nme: tuple(outs[4 * i:4 * i + 4]) for i, nme in enumerate(names)}


def _rope_tables(pos_col):
    s = pos_col.shape[0]
    w = N_KV_HEADS * HEAD_DIM
    tb = min(512, s)
    inv_freq = (ROPE_THETA ** (-np.arange(0, ROT_DIM, 2, dtype=np.float32) / ROT_DIM)).astype(np.float32)

    def body(pos_ref, cos_ref, sin_ref):
        pos = pos_ref[...].astype(F32)
        lane = lax.broadcasted_iota(jnp.int32, (tb, PAIR), 1) & (HEAD_DIM - 1)
        fidx = lane & (ROT_DIM // 2 - 1)
        inv = jnp.zeros((tb, PAIR), F32)
        for k in range(ROT_DIM // 2):
            inv = jnp.where(fidx == k, float(inv_freq[k]), inv)
        ang = pos * inv
        rot = lane < ROT_DIM
        sin_v = jnp.sin(ang)
        cos_ref[...] = _tile_lanes(jnp.where(rot, jnp.cos(ang), 1.0), w // PAIR)
        sin_ref[...] = _tile_lanes(jnp.where(lane < ROT_DIM // 2, -sin_v, jnp.where(rot, sin_v, 0.0)), w // PAIR)

    return pl.pallas_call(
        body, name="rope_tables", grid=(s // tb,),
        in_specs=[pl.BlockSpec((tb, 1), lambda i: (i, 0))],
        out_specs=[pl.BlockSpec((tb, w), lambda i: (i, 0))] * 2,
        out_shape=[jax.ShapeDtypeStruct((s, w), F32)] * 2,
        compiler_params=_params(("parallel",)),
    )(pos_col)


def _in_proj(x, w_in_g, first_vec, n_shards, into, name):
    _, s, d = x.shape
    ns, _, ncol = w_in_g.shape
    tm = min(2 * TM, s)
    first_call = into is None
    assert n_shards == 1 or not first_call

    def body(first_ref, x_ref, w_ref, into_ref, o_ref, *x16_ref):
        xb = x_ref[...].astype(BF16)
        o_ref[...] = _dot(xb, w_ref[...]).astype(BF16)
        for ref in x16_ref:
            ref[...] = xb

    shard = lambda j, first_ref: lax.rem(first_ref[0] + j, ns)
    x_spec = pl.BlockSpec((None, tm, d), lambda i, j, first_ref: (0, i, 0))
    grid_spec = pltpu.PrefetchScalarGridSpec(
        num_scalar_prefetch=1, grid=(s // tm, n_shards),
        in_specs=[x_spec, pl.BlockSpec((None, d, ncol), lambda i, j, first_ref: (shard(j, first_ref), 0, 0)), _ANY],
        out_specs=[pl.BlockSpec((tm, ncol), lambda i, j, first_ref: (i, shard(j, first_ref)))] + [x_spec] * first_call)
    return pl.pallas_call(
        body, name=name, grid_spec=grid_spec,
        out_shape=[jax.ShapeDtypeStruct((s, ns * ncol), BF16)] + [jax.ShapeDtypeStruct((1, s, d), BF16)] * first_call,
        input_output_aliases={} if first_call else {3: 0},
        compiler_params=_params(("parallel", "arbitrary")),
    )(first_vec, x, w_in_g, first_vec if first_call else into)


PAIR = 2 * HEAD_DIM
KEYS = 2 * WINDOW


def _pair_operand(t_all, h):
    col = (h // 2) * PAIR
    lane = lax.broadcasted_iota(jnp.int32, (KEYS, PAIR), 1)
    own_low = h % 2 == 0
    mine = jnp.where((lane < HEAD_DIM) if own_low else (lane >= HEAD_DIM), t_all[:, col:col + PAIR], 0.0)
    other = pltpu.roll(mine, HEAD_DIM, 1)
    low, high = (mine, other) if own_low else (other, mine)
    return jnp.concatenate([low, high], axis=0).astype(BF16)


def _pair_grad(acc, h):
    lane = lax.broadcasted_iota(jnp.int32, (KEYS, PAIR), 1)
    low = jnp.where(lane < HEAD_DIM, acc[:KEYS], 0.0)
    high = jnp.where(lane >= HEAD_DIM, acc[KEYS:], 0.0)
    if h % 2 == 0:
        return low + pltpu.roll(high, HEAD_DIM, 1)
    return high + pltpu.roll(low, HEAD_DIM, 1)


N_PAIRS = N_KV_HEADS * GROUP // 2


def _all_probs(q, kk2s, first, sinks_ref):
    assert ATTN_SCALE == 0.125
    q = q * ATTN_SCALE
    qps, scores = [], []
    for pair in range(N_PAIRS):
        qp = q[:, pair * PAIR:(pair + 1) * PAIR].astype(BF16)
        qps.append(qp)
        scores.append(_dot_nt(qp, kk2s[pair // (GROUP // 2)]))
    qi = lax.broadcasted_iota(jnp.int32, (WINDOW, 2 * KEYS), 0)
    kj = lax.broadcasted_iota(jnp.int32, (WINDOW, 2 * KEYS), 1) & (KEYS - 1)
    rel = qi + WINDOW - kj
    valid = (rel >= 0) & (rel < WINDOW) & jnp.logical_not(first & (kj < WINDOW))
    bias = jnp.where(valid, 0.0, NEG_BIG)
    s = (jnp.stack(scores, axis=0) + bias[None]).reshape(N_PAIRS * WINDOW, 2 * KEYS)
    probs, p_sinks = [], []
    for t in range(2):
        st = s[:, t * KEYS:(t + 1) * KEYS]
        sink = jnp.concatenate([jnp.broadcast_to(sinks_ref[0:1, 2 * pair + t:2 * pair + t + 1], (WINDOW, 1))
                                for pair in range(N_PAIRS)], axis=0)
        m = jnp.maximum(jnp.max(st, axis=1, keepdims=True), sink)
        e = jnp.exp(st - m)
        e_sink = jnp.exp(sink - m)
        inv_l = 1.0 / (jnp.sum(e, axis=1, keepdims=True) + e_sink)
        probs.append(e * inv_l)
        p_sinks.append(e_sink * inv_l)
    return qps, jnp.concatenate(probs, axis=1), p_sinks


def _roped_qkv(cur_ref, prev_ref, cos_ref, sin_ref, cosp_ref, sinp_ref, qw, kvw):
    cur = cur_ref[...].astype(F32)
    cos, sin = cos_ref[...], sin_ref[...]
    cos_q, sin_q = _tile_lanes(cos, GROUP), _tile_lanes(sin, GROUP)
    q = _rope(cur[:, :qw], cos_q, sin_q, 1.0)
    prev = prev_ref[...].astype(F32)
    k_all = jnp.concatenate([_rope(prev[:, :kvw], cosp_ref[...], sinp_ref[...], 1.0),
                             _rope(cur[:, qw:qw + kvw], cos, sin, 1.0)], axis=0)
    v_all = jnp.concatenate([prev[:, kvw:], cur[:, qw + kvw:]], axis=0)
    return q, k_all, v_all, cos_q, sin_q


def _attention_fwd(proj, cos_t, sin_t, sinks):
    s = proj.shape[0]
    qw = GROUP * N_KV_HEADS * HEAD_DIM
    kvw = N_KV_HEADS * HEAD_DIM
    nb = s // WINDOW

    def body(cur_ref, prev_ref, cos_ref, sin_ref, cosp_ref, sinp_ref, sinks_ref, o_ref):
        first = pl.program_id(0) == 0
        q, k_all, v_all, _, _ = _roped_qkv(cur_ref, prev_ref, cos_ref, sin_ref, cosp_ref, sinp_ref, qw, kvw)
        kk2s = [_pair_operand(k_all, h) for h in range(N_KV_HEADS)]
        vv2s = [_pair_operand(v_all, h) for h in range(N_KV_HEADS)]
        _, probs, _ = _all_probs(q, kk2s, first, sinks_ref)
        probs = probs.astype(BF16)
        outs = [_dot(probs[pair * WINDOW:(pair + 1) * WINDOW], vv2s[pair // (GROUP // 2)]) for pair in range(N_PAIRS)]
        o_ref[...] = jnp.concatenate(outs, axis=1)

    tbl = pl.BlockSpec((WINDOW, kvw), lambda n: (n, 0))
    tbl_prev = pl.BlockSpec((WINDOW, kvw), lambda n: (jnp.maximum(n - 1, 0), 0))
    return pl.pallas_call(
        body, name="attention_fwd", grid=(nb,),
        in_specs=[pl.BlockSpec((WINDOW, qw + 2 * kvw), lambda n: (n, 0)),
                  pl.BlockSpec((WINDOW, 2 * kvw), lambda n: (jnp.maximum(n - 1, 0), (qw // (2 * kvw)))),
                  tbl, tbl, tbl_prev, tbl_prev, _VMEM],
        out_specs=pl.BlockSpec((WINDOW, qw), lambda n: (n, 0)),
        out_shape=jax.ShapeDtypeStruct((s, qw), F32),
        compiler_params=_params(("parallel",)),
    )(proj, proj, cos_t, sin_t, cos_t, sin_t, sinks)


def _conv_taps(cw_ref):
    return [jnp.concatenate([cw_ref[s, k:k + 1, :] for s in range(N_CHIPS)], axis=1) for k in range(3)]


def _shift_down(z, halo, steps):
    last = halo.shape[0]
    row = lax.broadcasted_iota(jnp.int32, z.shape, 0)
    out = pltpu.roll(z, steps, 0)
    for r in range(steps):
        out = jnp.where(row == r, halo[last - steps + r:last - steps + r + 1, :], out)
    return out


def _shift_up(z, halo, steps):
    rows = z.shape[0]
    row = lax.broadcasted_iota(jnp.int32, z.shape, 0)
    out = pltpu.roll(z, rows - steps, 0)
    for r in range(steps):
        out = jnp.where(row == rows - steps + r, halo[r:r + 1, :], out)
    return out


def _split_cbu(lo, hi, cw):
    lo, hi = lo.astype(F32), hi.astype(F32)
    c_gate = lo[:, :cw]
    b_gate = jnp.concatenate([lo[:, cw:], hi[:, :2 * cw - lo.shape[1]]], axis=1)
    u = hi[:, 2 * cw - lo.shape[1]:]
    return c_gate, b_gate, u


def _conv_norm(proj, attn, cw_full, g_ac):
    s, in_w = proj.shape
    cw = attn.shape[1]
    blk_w = in_w // 3
    tb = min(TB_CONV, s)

    def body(lo_ref, hi_ref, lo_h_ref, hi_h_ref, attn_ref, cw_ref, g_ref, mixed_ref, ac_ref, rstd_ref):
        i = pl.program_id(0)
        c_gate, b_gate, u = _split_cbu(lo_ref[...], hi_ref[...], cw)
        c_h, _, u_h = _split_cbu(lo_h_ref[...], hi_h_ref[...], cw)
        z = c_gate * u
        z_h = jnp.where(i == 0, 0.0, c_h * u_h)
        w0, w1, w2 = _conv_taps(cw_ref)
        y = w0 * _shift_down(z, z_h, 2) + w1 * _shift_down(z, z_h, 1) + w2 * z
        conv = b_gate * y
        a = attn_ref[...]
        r_a = lax.rsqrt(jnp.mean(a * a, axis=-1, keepdims=True) + RMS_EPS)
        r_c = lax.rsqrt(jnp.mean(conv * conv, axis=-1, keepdims=True) + RMS_EPS)
        g = g_ref[...]
        mixed_ref[...] = jnp.concatenate([a * r_a * g[:, :cw], conv * r_c * g[:, cw:]], axis=1).astype(BF16)
        ac_ref[...] = jnp.concatenate([a, conv], axis=1)
        rstd_ref[0] = r_a
        rstd_ref[1] = r_c

    halo_idx = lambda i: jnp.maximum(i * (tb // HALO_ROWS) - 1, 0)
    return pl.pallas_call(
        body, name="conv_norm", grid=(s // tb,),
        in_specs=[pl.BlockSpec((tb, blk_w), lambda i: (i, 1)),
                  pl.BlockSpec((tb, blk_w), lambda i: (i, 2)),
                  pl.BlockSpec((HALO_ROWS, blk_w), lambda i: (halo_idx(i), 1)),
                  pl.BlockSpec((HALO_ROWS, blk_w), lambda i: (halo_idx(i), 2)),
                  pl.BlockSpec((tb, cw), lambda i: (i, 0)),
                  _VMEM, _VMEM],
        out_specs=[pl.BlockSpec((tb, 2 * cw), lambda i: (i, 0)),
                   pl.BlockSpec((tb, 2 * cw), lambda i: (i, 0)),
                   pl.BlockSpec((2, tb, 1), lambda i: (0, i, 0))],
        out_shape=[jax.ShapeDtypeStruct((s, 2 * cw), BF16), jax.ShapeDtypeStruct((s, 2 * cw), F32),
                   jax.ShapeDtypeStruct((2, s, 1), F32)],
        compiler_params=_params(("parallel",)),
    )(proj, proj, proj, proj, attn, cw_full, g_ac)


def _out_proj_ln(mixed, w_out_g, x, ln_g, ln_b):
    s, d = mixed.shape
    tm = min(TM, s)
    tk = d
    nk = d // tk

    def body(a_ref, w_ref, x_ref, g_ref, b_ref, xhat_ref, h_ref, rstd_ref, acc):
        k = pl.program_id(1)
        _accumulate(acc, lambda: _dot(a_ref[...], w_ref[...]), k, nk)

        @pl.when(k == nk - 1)
        def _():
            def rows_fn(rows):
                xhat, rstd = _ln_fwd(ALPHA * x_ref[rows, :] + acc[rows, :])
                xhat_ref[rows, :] = xhat
                h_ref[rows, :] = (xhat * g_ref[...] + b_ref[...]).astype(BF16)
                rstd_ref[rows, :] = rstd

            _for_row_chunks(tm, rows_fn)

    row = pl.BlockSpec((tm, d), lambda i, k: (i, 0))
    return pl.pallas_call(
        body, name="out_proj_ln", grid=(s // tm, nk),
        in_specs=[pl.BlockSpec((tm, tk), lambda i, k: (i, k)),
                  pl.BlockSpec((tk, d), lambda i, k: (k, 0)),
                  pl.BlockSpec((None, tm, d), lambda i, k: (0, i, 0)),
                  _VMEM, _VMEM],
        out_specs=[row, row, pl.BlockSpec((tm, 1), lambda i, k: (i, 0))],
        out_shape=[jax.ShapeDtypeStruct((s, d), F32), jax.ShapeDtypeStruct((s, d), BF16),
                   jax.ShapeDtypeStruct((s, 1), F32)],
        scratch_shapes=[pltpu.VMEM((tm, d), F32)],
        compiler_params=_params(("parallel", "arbitrary")),
    )(mixed, w_out_g, x, ln_g, ln_b)


def _gate_up(h1, w_gu_g, first_vec, n_shards, into, name):
    s, d = h1.shape
    ns, _, fs2 = w_gu_g.shape
    fs = fs2 // 2
    tm = min(TM, s)

    def body(first_ref, h_ref, w_ref, act_in, ab_in, act_ref, ab_ref):
        gu = _dot(h_ref[...], w_ref[...])
        g, u = gu[:, :fs], gu[:, fs:]
        sg = _sigmoid(g)
        silu = g * sg
        act_ref[...] = (silu * u).astype(BF16)
        ab_ref[:, :fs] = (u * (sg * (1.0 + g * (1.0 - sg)))).astype(BF16)
        ab_ref[:, fs:] = silu.astype(BF16)

    shard = lambda j, first_ref: lax.rem(first_ref[0] + j, ns)
    grid_spec = pltpu.PrefetchScalarGridSpec(
        num_scalar_prefetch=1, grid=(s // tm, n_shards),
        in_specs=[pl.BlockSpec((tm, d), lambda i, j, first_ref: (i, 0)),
                  pl.BlockSpec((None, d, fs2), lambda i, j, first_ref: (shard(j, first_ref), 0, 0)), _ANY, _ANY],
        out_specs=[pl.BlockSpec((tm, fs), lambda i, j, first_ref: (i, shard(j, first_ref))),
                   pl.BlockSpec((tm, fs2), lambda i, j, first_ref: (i, shard(j, first_ref)))])
    return pl.pallas_call(
        body, name=name, grid_spec=grid_spec,
        out_shape=[jax.ShapeDtypeStruct((s, ns * fs), BF16), jax.ShapeDtypeStruct((s, ns * fs2), BF16)],
        input_output_aliases={} if into is None else {3: 0, 4: 1},
        compiler_params=_params(("parallel", "arbitrary")),
    )(first_vec, h1, w_gu_g, *((first_vec, first_vec) if into is None else into))


def _down_ln_loss(act, w_down_g, xhat1, ln1_g, ln1_b, ln2_g, ln2_b, target):
    s, f = act.shape
    d = xhat1.shape[1]
    tm = min(TM, s)
    tk = f // N_CHIPS
    nk = f // tk

    def body(a_ref, w_ref, xh_ref, g1_ref, b1_ref, g2_ref, b2_ref, t_ref, dpre_ref, dpre16_ref, loss_ref, gg_ref, gb_ref,
             acc):
        i, k = pl.program_id(0), pl.program_id(1)
        _accumulate(acc, lambda: _dot(a_ref[...], w_ref[...]), k, nk)

        @pl.when(k == nk - 1)
        def _():
            @pl.when(i == 0)
            def _():
                loss_ref[...] = jnp.zeros_like(loss_ref)
                gg_ref[...] = jnp.zeros_like(gg_ref)
                gb_ref[...] = jnp.zeros_like(gb_ref)

            def rows_fn(rows):
                h1 = xh_ref[rows, :] * g1_ref[...] + b1_ref[...]
                xhat, rstd = _ln_fwd(ALPHA * h1 + acc[rows, :])
                g2 = g2_ref[...]
                diff = xhat * g2 + b2_ref[...] - t_ref[rows, :]
                dy = diff * (1.0 / d)
                dpre = _ln_bwd(dy, xhat, rstd, g2)
                dpre_ref[rows, :] = dpre
                dpre16_ref[rows, :] = dpre.astype(BF16)
                sq = jnp.sum(jnp.sum(diff * diff, axis=1, keepdims=True), axis=0, keepdims=True)
                loss_ref[...] += jnp.broadcast_to(sq * (0.5 / d), (1, 128))
                gg_ref[...] += jnp.sum(dy * xhat, axis=0, keepdims=True)
                gb_ref[...] += jnp.sum(dy, axis=0, keepdims=True)

            _for_row_chunks(tm, rows_fn)

    row = pl.BlockSpec((tm, d), lambda i, k: (i, 0))
    vec = pl.BlockSpec((1, d), lambda i, k: (0, 0))
    return pl.pallas_call(
        body, name="down_ln_loss", grid=(s // tm, nk),
        in_specs=[pl.BlockSpec((tm, tk), lambda i, k: (i, k)),
                  pl.BlockSpec((tk, d), lambda i, k: (k, 0)),
                  row, _VMEM, _VMEM, _VMEM, _VMEM,
                  pl.BlockSpec((None, tm, d), lambda i, k: (0, i, 0))],
        out_specs=[row, row, pl.BlockSpec((1, 128), lambda i, k: (0, 0)), vec, vec],
        out_shape=[jax.ShapeDtypeStruct((s, d), F32), jax.ShapeDtypeStruct((s, d), BF16),
                   jax.ShapeDtypeStruct((1, 128), F32), jax.ShapeDtypeStruct((1, d), F32),
                   jax.ShapeDtypeStruct((1, d), F32)],
        scratch_shapes=[pltpu.VMEM((tm, d), F32)],
        compiler_params=_params(("arbitrary", "arbitrary")),
    )(act, w_down_g, xhat1, ln1_g, ln1_b, ln2_g, ln2_b, target)


def _dact_silu_bwd(dpre2, w_down_g, ab):
    s, d = dpre2.shape
    fs2 = ab.shape[1] // N_CHIPS
    fs = fs2 // 2
    tm = min(TM, s)

    def body(dp_ref, w_ref, ab_ref, dgu_ref):
        d_act = _dot_nt(dp_ref[...], w_ref[...])
        dgu_ref[:, :fs] = (d_act * ab_ref[:, :fs].astype(F32)).astype(BF16)
        dgu_ref[:, fs:] = (d_act * ab_ref[:, fs:].astype(F32)).astype(BF16)

    blk = pl.BlockSpec((tm, fs2), lambda j, i: (i, j))
    return pl.pallas_call(
        body, name="dact_silu_bwd", grid=(N_CHIPS, s // tm),
        in_specs=[pl.BlockSpec((tm, d), lambda j, i: (i, 0)),
                  pl.BlockSpec((fs, d), lambda j, i: (j, 0)), blk],
        out_specs=blk,
        out_shape=jax.ShapeDtypeStruct(ab.shape, BF16),
        compiler_params=_params(("parallel", "parallel")),
    )(dpre2, w_down_g, ab)


def _grad_rows(a, b, after, name, row_blocks=1):
    s, m = a.shape
    n = b.shape[1]
    ms = m // N_CHIPS
    tmw = ms // row_blocks
    tk = min(TK_TOK, s)
    nk = s // tk

    def body(a_ref, b_ref, after_ref, o_ref, acc):
        k = pl.program_id(2)
        _accumulate(acc, lambda: _dot_tn(a_ref[...].astype(BF16), b_ref[...].astype(BF16)), k, nk)

        @pl.when(k == nk - 1)
        def _():
            o_ref[...] = acc[...].astype(BF16)

    return pl.pallas_call(
        body, name=name, grid=(N_CHIPS, row_blocks, nk),
        in_specs=[pl.BlockSpec((tk, tmw), lambda j, r, k: (k, j * row_blocks + r)),
                  pl.BlockSpec((tk, n), lambda j, r, k: (k, 0)), _ANY],
        out_specs=pl.BlockSpec((None, tmw, n), lambda j, r, k: (j, r, 0)),
        out_shape=jax.ShapeDtypeStruct((N_CHIPS, ms, n), BF16),
        scratch_shapes=[pltpu.VMEM((tmw, n), F32)],
        compiler_params=_params(("parallel", "parallel", "arbitrary")),
    )(a, b, after)


def _grad_cols(a, bs, after, name, a_3d=False, row_blocks=2, shard=None):
    s, m = a.shape[-2:]
    n = bs[0].shape[1]
    ns = n // N_CHIPS
    nb = len(bs)
    tmw = m // row_blocks
    tk = min(TK_TOK, s)
    nk = s // tk

    def body(*refs):
        a_ref, b_refs, o_refs, accs = refs[0], refs[1:1 + nb], refs[2 + nb:2 + 2 * nb], refs[2 + 2 * nb:]
        k = pl.program_id(2)
        for b_ref, acc in zip(b_refs, accs):
            _accumulate(acc, lambda b_ref=b_ref: _dot_tn(a_ref[...].astype(BF16), b_ref[...].astype(BF16)), k, nk)

        @pl.when(k == nk - 1)
        def _():
            for o_ref, acc in zip(o_refs, accs):
                o_ref[...] = acc[...].astype(BF16)

    if a_3d:
        a_spec = pl.BlockSpec((None, tk, tmw), lambda j, r, k: (0, k, r))
    else:
        a_spec = pl.BlockSpec((tk, tmw), lambda j, r, k: (k, r))
    return _call_with_adamw(
        body, name, (N_CHIPS, row_blocks, nk),
        [a_spec] + [pl.BlockSpec((tk, ns), lambda j, r, k: (k, j))] * nb + [_ANY],
        [pl.BlockSpec((None, tmw, ns), lambda j, r, k: (j, r, 0))] * nb,
        [jax.ShapeDtypeStruct((N_CHIPS, m, ns), BF16)] * nb,
        [pltpu.VMEM((tmw, ns), F32)] * nb, ("parallel", "parallel", "arbitrary"), (a, *bs, after), shard)


def _dh1_ln_bwd(d_gu, w_gu_g, dpre2, xhat1, rstd1, ln1_g, after):
    s = d_gu.shape[0]
    d = dpre2.shape[1]
    hd = d // 2
    fs = w_gu_g.shape[2]
    tm = min(TM, s)

    def body(dgu_ref, w_ref, dp2_ref, xh_ref, rs_ref, g_ref, after_ref, dpre_ref, dpre16_ref, gg_ref, gb_ref, acc_lo,
             acc_hi):
        i, j, half = pl.program_id(0), pl.program_id(1), pl.program_id(2)

        def product():
            return _dot_nt(dgu_ref[...], w_ref[...])

        @pl.when(half == 0)
        def _():
            @pl.when(j == 0)
            def _():
                acc_lo[...] = ALPHA * dp2_ref[:, :hd]

            acc_lo[...] += product()

        @pl.when(half == 1)
        def _():
            @pl.when(j == 0)
            def _():
                acc_hi[...] = ALPHA * dp2_ref[:, hd:]

            acc_hi[...] += product()

        @pl.when((j == N_CHIPS - 1) & (half == 1))
        def _():
            @pl.when(i == 0)
            def _():
                gg_ref[...] = jnp.zeros_like(gg_ref)
                gb_ref[...] = jnp.zeros_like(gb_ref)

            def rows_fn(rows):
                dh = jnp.concatenate([acc_lo[rows, :], acc_hi[rows, :]], axis=1)
                xhat = xh_ref[rows, :]
                dpre = _ln_bwd(dh, xhat, rs_ref[rows, :], g_ref[...])
                dpre_ref[rows, :] = dpre
                dpre16_ref[rows, :] = dpre.astype(BF16)
                gg_ref[...] += jnp.sum(dh * xhat, axis=0, keepdims=True)
                gb_ref[...] += jnp.sum(dh, axis=0, keepdims=True)

            _for_row_chunks(tm, rows_fn)

    row = pl.BlockSpec((tm, d), lambda i, j, h: (i, 0))
    vec = pl.BlockSpec((1, d), lambda i, j, h: (0, 0))
    act_blk = pl.BlockSpec((tm, fs), lambda i, j, h: (i, j))
    w_blk = pl.BlockSpec((None, hd, fs), lambda i, j, h: (j, h, 0))
    return pl.pallas_call(
        body, name="dh1_ln_bwd", grid=(s // tm, N_CHIPS, 2),
        in_specs=[act_blk, w_blk, row, row, pl.BlockSpec((tm, 1), lambda i, j, h: (i, 0)), _VMEM, _ANY],
        out_specs=[row, row, vec, vec],
        out_shape=[jax.ShapeDtypeStruct((s, d), F32), jax.ShapeDtypeStruct((s, d), BF16),
                   jax.ShapeDtypeStruct((1, d), F32), jax.ShapeDtypeStruct((1, d), F32)],
        scratch_shapes=[pltpu.VMEM((tm, hd), F32)] * 2,
        compiler_params=_params(("arbitrary", "arbitrary", "arbitrary")),
    )(d_gu, w_gu_g, dpre2, xhat1, rstd1, ln1_g, after)


def _dmixed_rms_bwd(dpre1, w_out_g, ac, rstd, g_ac):
    s, d = dpre1.shape
    hd = d // 2
    tm = min(TM, s)

    def body(dp_ref, w_ref, ac_ref, rs_ref, g_ref, dac_ref, gg_ref):
        i = pl.program_id(1)
        dm = _dot_nt(dp_ref[...].astype(BF16), w_ref[...])
        pre = ac_ref[...]
        r = rs_ref[...]
        gdm = dm * g_ref[...]
        dac_ref[...] = r * gdm - pre * (r * r * r) * jnp.mean(gdm * pre, axis=-1, keepdims=True)
        gg = jnp.sum(dm * pre * r, axis=0, keepdims=True)

        @pl.when(i == 0)
        def _():
            gg_ref[...] = gg

        @pl.when(i > 0)
        def _():
            gg_ref[...] += gg

    return pl.pallas_call(
        body, name="dmixed_rms_bwd", grid=(2, s // tm),
        in_specs=[pl.BlockSpec((tm, d), lambda h, i: (i, 0)),
                  pl.BlockSpec((hd, d), lambda h, i: (h, 0)),
                  pl.BlockSpec((tm, hd), lambda h, i: (i, h)),
                  pl.BlockSpec((None, tm, 1), lambda h, i: (h, i, 0)),
                  pl.BlockSpec((1, hd), lambda h, i: (0, h))],
        out_specs=[pl.BlockSpec((tm, hd), lambda h, i: (i, h)),
                   pl.BlockSpec((1, hd), lambda h, i: (0, h))],
        out_shape=[jax.ShapeDtypeStruct((s, d), F32), jax.ShapeDtypeStruct((1, d), F32)],
        compiler_params=_params(("arbitrary", "arbitrary")),
    )(dpre1, w_out_g, ac, rstd, g_ac)


def _attention_bwd(proj, d_ac, cos_t, sin_t, sinks, after, shard):
    s = proj.shape[0]
    qw = GROUP * N_KV_HEADS * HEAD_DIM
    kvw = N_KV_HEADS * HEAD_DIM
    nb = s // WINDOW
    nq = GROUP * N_KV_HEADS

    def body(cur_ref, prev_ref, do_ref, cos_ref, sin_ref, cosp_ref, sinp_ref, sinks_ref, after_ref,
             dq_ref, dcur_ref, dprev_ref, dsink_ref):
        n = pl.program_id(0)
        first = n == 0
        q, k_all, v_all, cos_q, sin_q = _roped_qkv(cur_ref, prev_ref, cos_ref, sin_ref, cosp_ref, sinp_ref, qw, kvw)
        kk2s = [_pair_operand(k_all, h) for h in range(N_KV_HEADS)]
        vv2s = [_pair_operand(v_all, h) for h in range(N_KV_HEADS)]
        qps, probs, p_sinks = _all_probs(q, kk2s, first, sinks_ref)
        dops = [do_ref[:, pair * PAIR:(pair + 1) * PAIR].astype(BF16) for pair in range(N_PAIRS)]
        d_probs = jnp.concatenate([_dot_nt(dops[pair], vv2s[pair // (GROUP // 2)]) for pair in range(N_PAIRS)], axis=0)
        d_s, ds_sinks = [], []
        for t in range(2):
            cols = slice(t * KEYS, (t + 1) * KEYS)
            delta = jnp.sum(probs[:, cols] * d_probs[:, cols], axis=1, keepdims=True)
            d_s.append(probs[:, cols] * (d_probs[:, cols] - delta))
            ds_sinks.append(-p_sinks[t] * delta)
        d_s = jnp.concatenate(d_s, axis=1).astype(BF16)
        probs = probs.astype(BF16)
        dq_parts, dk_tiles, dv_tiles, dsink_parts = [], [], [], []
        for h in range(N_KV_HEADS):
            dkk2, dvv2 = None, None
            for p in range(GROUP // 2):
                pair = (GROUP // 2) * h + p
                rows = slice(pair * WINDOW, (pair + 1) * WINDOW)
                dq_parts.append(_dot(d_s[rows], kk2s[h]) * ATTN_SCALE)
                dk_term = _dot_tn(d_s[rows], qps[pair])
                dv_term = _dot_tn(probs[rows], dops[pair])
                dkk2 = dk_term if dkk2 is None else dkk2 + dk_term
                dvv2 = dv_term if dvv2 is None else dvv2 + dv_term
                dsink_parts.extend([jnp.sum(ds_sinks[t][rows], axis=0, keepdims=True) for t in range(2)])
            dk_tiles.append(_pair_grad(dkk2, h))
            dv_tiles.append(_pair_grad(dvv2, h))
        dq_ref[...] = _rope(jnp.concatenate(dq_parts, axis=1), cos_q, sin_q, -1.0)
        dk = jnp.concatenate([dk_tiles[0] + dk_tiles[1], dk_tiles[2] + dk_tiles[3]], axis=1)
        dv = jnp.concatenate([dv_tiles[0] + dv_tiles[1], dv_tiles[2] + dv_tiles[3]], axis=1)
        dprev_ref[...] = jnp.concatenate([dk[:WINDOW], dv[:WINDOW]], axis=1)
        dcur_ref[...] = jnp.concatenate([dk[WINDOW:], dv[WINDOW:]], axis=1)
        dsink = jnp.concatenate(dsink_parts, axis=1)

        @pl.when(first)
        def _():
            dsink_ref[...] = dsink

        @pl.when(n > 0)
        def _():
            dsink_ref[...] += dsink

    tbl = pl.BlockSpec((WINDOW, kvw), lambda n: (n, 0))
    tbl_prev = pl.BlockSpec((WINDOW, kvw), lambda n: (jnp.maximum(n - 1, 0), 0))
    kv_blk = pl.BlockSpec((WINDOW, 2 * kvw), lambda n: (n, 0))
    return _call_with_adamw(
        body, "attention_bwd", (nb,),
        [pl.BlockSpec((WINDOW, qw + 2 * kvw), lambda n: (n, 0)),
         pl.BlockSpec((WINDOW, 2 * kvw), lambda n: (jnp.maximum(n - 1, 0), (qw // (2 * kvw)))),
         pl.BlockSpec((WINDOW, qw), lambda n: (n, 0)),
         tbl, tbl, tbl_prev, tbl_prev, _VMEM, _ANY],
        [pl.BlockSpec((WINDOW, qw), lambda n: (n, 0)), kv_blk, kv_blk, pl.BlockSpec((1, nq), lambda n: (0, 0))],
        [jax.ShapeDtypeStruct((s, qw), F32), jax.ShapeDtypeStruct((s, 2 * kvw), F32),
         jax.ShapeDtypeStruct((s, 2 * kvw), F32), jax.ShapeDtypeStruct((1, nq), F32)],
        [], ("arbitrary",), (proj, proj, d_ac, cos_t, sin_t, cos_t, sin_t, sinks, after), shard)


def _dproj_assemble(proj, d_ac, dq, dkv_cur, dkv_prev, cos_t, sin_t, cw_full):
    s, in_w = proj.shape
    cw = dq.shape[1]
    kvw = N_KV_HEADS * HEAD_DIM
    blk_w = in_w // 3
    tb = WINDOW
    nb = s // tb

    def body(lo_ref, hi_ref, lo_p_ref, hi_p_ref, lo_n_ref, hi_n_ref, dconv_ref, dconv_n_ref,
             dq_ref, dcur_ref, dprev_n_ref, cos_ref, sin_ref, cw_ref, dproj_ref, gcw_ref):
        i = pl.program_id(0)
        last = i == nb - 1
        c_gate, b_gate, u = _split_cbu(lo_ref[...], hi_ref[...], cw)
        c_p, _, u_p = _split_cbu(lo_p_ref[...], hi_p_ref[...], cw)
        _, b_n, _ = _split_cbu(lo_n_ref[...], hi_n_ref[...], cw)
        z = c_gate * u
        z_p = jnp.where(i == 0, 0.0, c_p * u_p)
        z1 = _shift_down(z, z_p, 1)
        z2 = _shift_down(z, z_p, 2)
        w0, w1, w2 = _conv_taps(cw_ref)
        y = w0 * z2 + w1 * z1 + w2 * z
        d_conv = dconv_ref[...]
        d_b = d_conv * y
        d_y = d_conv * b_gate
        d_y_n = jnp.where(last, 0.0, dconv_n_ref[...] * b_n[:dconv_n_ref.shape[0]])
        d_z = w2 * d_y + w1 * _shift_up(d_y, d_y_n, 1) + w0 * _shift_up(d_y, d_y_n, 2)
        d_c = d_z * u
        d_u = d_z * c_gate
        gcw = jnp.concatenate([jnp.sum(d_y * z2, axis=0, keepdims=True), jnp.sum(d_y * z1, axis=0, keepdims=True),
                               jnp.sum(d_y * z, axis=0, keepdims=True)], axis=0)

        @pl.when(i == 0)
        def _():
            gcw_ref[...] = gcw

        @pl.when(i > 0)
        def _():
            gcw_ref[...] += gcw

        dkv = dcur_ref[...] + jnp.where(last, 0.0, dprev_n_ref[...])
        dk = _rope(dkv[:, :kvw], cos_ref[...], sin_ref[...], -1.0)
        dproj_ref[...] = jnp.concatenate([dq_ref[...], dk, dkv[:, kvw:], d_c, d_b, d_u], axis=1).astype(BF16)

    prev_halo = lambda i: jnp.maximum(i * (tb // HALO_ROWS) - 1, 0)
    next_halo = lambda i: jnp.minimum((i + 1) * (tb // HALO_ROWS), s // HALO_ROWS - 1)
    next8 = lambda i: jnp.minimum((i + 1) * (tb // 8), s // 8 - 1)
    nxt = lambda i: jnp.minimum(i + 1, nb - 1)
    return pl.pallas_call(
        body, name="dproj_assemble", grid=(nb,),
        in_specs=[pl.BlockSpec((tb, blk_w), lambda i: (i, 1)),
                  pl.BlockSpec((tb, blk_w), lambda i: (i, 2)),
                  pl.BlockSpec((HALO_ROWS, blk_w), lambda i: (prev_halo(i), 1)),
                  pl.BlockSpec((HALO_ROWS, blk_w), lambda i: (prev_halo(i), 2)),
                  pl.BlockSpec((HALO_ROWS, blk_w), lambda i: (next_halo(i), 1)),
                  pl.BlockSpec((HALO_ROWS, blk_w), lambda i: (next_halo(i), 2)),
                  pl.BlockSpec((tb, cw), lambda i: (i, 1)),
                  pl.BlockSpec((8, cw), lambda i: (next8(i), 1)),
                  pl.BlockSpec((tb, cw), lambda i: (i, 0)),
                  pl.BlockSpec((tb, 2 * kvw), lambda i: (i, 0)),
                  pl.BlockSpec((tb, 2 * kvw), lambda i: (nxt(i), 0)),
                  pl.BlockSpec((tb, kvw), lambda i: (i, 0)),
                  pl.BlockSpec((tb, kvw), lambda i: (i, 0)),
                  _VMEM],
        out_specs=[pl.BlockSpec((tb, in_w), lambda i: (i, 0)),
                   pl.BlockSpec((3, cw), lambda i: (0, 0))],
        out_shape=[jax.ShapeDtypeStruct((s, in_w), BF16), jax.ShapeDtypeStruct((3, cw), F32)],
        compiler_params=_params(("arbitrary",)),
    )(proj, proj, proj, proj, proj, proj, d_ac, d_ac, dq, dkv_cur, dkv_prev, cos_t, sin_t, cw_full)


def _dx(d_proj, w_in_g, dpre1, after, shard):
    s, in_w = d_proj.shape
    ns, d, ncol = w_in_g.shape
    tm = min(TM, s)

    def body(dp_ref, w_ref, r_ref, after_ref, o_ref, acc):
        j = pl.program_id(1)
        @pl.when(j == 0)
        def _():
            acc[...] = ALPHA * r_ref[...]

        acc[...] += _dot_nt(dp_ref[...], w_ref[...])

        @pl.when(j == ns - 1)
        def _():
            o_ref[...] = acc[...]

    return _call_with_adamw(
        body, "dx", (s // tm, ns),
        [pl.BlockSpec((tm, ncol), lambda i, j: (i, j)),
         pl.BlockSpec((None, d, ncol), lambda i, j: (j, 0, 0)),
         pl.BlockSpec((tm, d), lambda i, j: (i, 0)), _ANY],
        [pl.BlockSpec((None, tm, d), lambda i, j: (0, i, 0))], [jax.ShapeDtypeStruct((1, s, d), F32)],
        [pltpu.VMEM((tm, d), F32)], ("parallel", "arbitrary"), (d_proj, w_in_g, dpre1, after), shard)


def kernel(x, positions, w_in, conv_w, sinks, g_attn, g_conv, w_out, ln1_g, ln1_b, w_gate, w_up, w_down, ln2_g, ln2_b, loss_target, m_w_in, m_conv_w, m_sinks, m_g_attn, m_g_conv, m_w_out, m_ln1_g, m_ln1_b, m_w_gate, m_w_up, m_w_down, m_ln2_g, m_ln2_b, v_w_in, v_conv_w, v_sinks, v_g_attn, v_g_conv, v_w_out, v_ln1_g, v_ln1_b, v_w_gate, v_w_up, v_w_down, v_ln2_g, v_ln2_b):
    s = x.shape[1]
    d = x.shape[2]

    chip_vec = _chip_id(lax.axis_index("x"), lax.axis_index("y")).astype(jnp.int32).reshape(1)
    wnames = ["w_in", "w_out", "w_gu", "w_down"]
    buf_in = _cast_weight(w_in, chip_vec, chip_vec, "cast_w_in")
    flight_in, token_in = _gather_start([buf_in], chip_vec, "gather_start_w_in")
    cw_buf = lax.dynamic_update_slice(jnp.zeros((N_CHIPS,) + conv_w.shape[1:], F32), conv_w, (chip_vec[0], 0, 0))
    cw_flight = _flight_start("conv_w_start", [cw_buf], _conv_w_plan(), 3, token_in)
    started = cw_flight[2][0]
    buf_gu = _cast_weight(w_gate, chip_vec, started, "cast_w_gate", 0, 2)
    buf_gu = _cast_weight(w_up, chip_vec, buf_gu, "cast_w_up", 1, 2)
    bufs = [_cast_weight(w_out, chip_vec, started, "cast_w_out"), buf_gu,
            _cast_weight(w_down, chip_vec, started, "cast_w_down")]
    flights_rest, token = _gather_start(bufs, token_in, "gather_start_rest")
    flights = flight_in + flights_rest

    def gathered(i, after):
        send_sems, recv_sems, buf = flights[i]
        buf = _gather_wait(send_sems, recv_sems, buf, after, "gather_wait_" + wnames[i])
        return _sibling_fill(buf, "sibling_fill_" + wnames[i])

    g_ac = jnp.concatenate([g_attn, g_conv], axis=1)

    proj_own, x16 = _in_proj(x, _after(flights[0][2], token), chip_vec, 1, None, "in_proj_own")
    cos_t, sin_t = _rope_tables(positions.reshape(s, 1) + token[0:1, 0:1].astype(jnp.int32))
    w_in_g = gathered(0, _after(cos_t, proj_own))
    (proj,) = _in_proj(x16, w_in_g, chip_vec + 1, N_CHIPS - 1, proj_own, "in_proj_rest")
    send_sems, recv_sems, buf_out = flights[1]
    buf_out = _gather_wait(send_sems, recv_sems, buf_out, proj, "gather_wait_w_out")
    fill_out = _flight_start("fill_start_w_out", [buf_out], _fill_plan(1), 3, chip_vec)
    attn = _attention_fwd(_after(proj, fill_out[2][0]), cos_t, sin_t, sinks)
    (cw_full,) = _flight_wait("conv_w_wait", cw_flight, _conv_w_plan(), attn)
    mixed, ac, rstd_ac = _conv_norm(proj, attn, cw_full, g_ac)
    (w_out_g,) = _flight_wait("fill_wait_w_out", fill_out, _fill_plan(1), mixed)
    w_out_full = w_out_g.reshape(d, d)
    xhat1, h1, rstd1 = _out_proj_ln(mixed, w_out_full, x, ln1_g, ln1_b)
    send_sems, recv_sems, buf_gu = flights[2]
    buf_gu = _gather_wait(send_sems, recv_sems, buf_gu, h1, "gather_wait_w_gu")
    fill_gu = _flight_start("fill_start_w_gu", [buf_gu], _fill_plan(1), 3, chip_vec)
    own = _gate_up(h1, fill_gu[2][0], chip_vec, 1, None, "gate_up_own")
    (w_gu_g,) = _flight_wait("fill_wait_w_gu", fill_gu, _fill_plan(1), own[0])
    some = _gate_up(h1, w_gu_g, chip_vec + 1, N_CHIPS - 2, own, "gate_up_rest")
    send_sems, recv_sems, buf_down = flights[3]
    buf_down = _gather_wait(send_sems, recv_sems, buf_down, some[0], "gather_wait_w_down")
    fill_down = _flight_start("fill_start_w_down", [buf_down], _fill_plan(1), 3, chip_vec)
    act, ab = _gate_up(h1, _after(w_gu_g, fill_down[2][0]), chip_vec + N_CHIPS - 1, 1, some, "gate_up_last")
    (w_down_g,) = _flight_wait("fill_wait_w_down", fill_down, _fill_plan(1), act)
    w_down_full = w_down_g.reshape(-1, d)
    dpre2, dpre2_16, loss_part, g_ln2_g, g_ln2_b = _down_ln_loss(act, w_down_full, xhat1, ln1_g, ln1_b, ln2_g, ln2_b,
                                                                 loss_target)

    cvec = lax.axis_index("c").astype(jnp.int32).reshape(1)

    def exchange_begin(parts, nme):
        bufs = []
        for part in parts:
            ns, r, cdim = part.shape
            bufs.extend([part, lax.empty((ns, r // 2, cdim), part.dtype)])
        return _flight_start("exchange_start_" + nme, bufs, _exchange_plan(len(parts)), len(parts), cvec)

    def exchange_end(flight, n_parts, after, nme):
        bufs = _flight_wait("exchange_wait_" + nme, flight, _exchange_plan(n_parts), after)
        return [(bufs[2 * w], bufs[2 * w + 1]) for w in range(n_parts)]

    def scatter_begin(part, got, nme):
        return _scatter_start(_add_halves(part, got, cvec, "add_halves_" + nme), "scatter_start_" + nme)

    d_gu = _dact_silu_bwd(dpre2_16, w_down_full, ab)
    p_down = _grad_rows(act, dpre2_16, d_gu, "grad_w_down")
    x_down = exchange_begin([p_down], "w_down")
    (p_gu,) = _grad_cols(h1, [d_gu], x_down[2][0], "grad_w_gate_up")
    ((p_down, got),) = exchange_end(x_down, 1, p_gu, "w_down")
    f_down = scatter_begin(p_down, got, "w_down")
    x_gu = exchange_begin([_after(p_gu, f_down[2])], "w_gu")
    dpre1, dpre1_16, g_ln1_g, g_ln1_b = _dh1_ln_bwd(d_gu, w_gu_g, dpre2, xhat1, rstd1, ln1_g, x_gu[2][0])
    ((p_gu, got),) = exchange_end(x_gu, 1, dpre1, "w_gu")
    f_gu = scatter_begin(p_gu, got, "w_gu")
    d_ac, g_g_ac = _dmixed_rms_bwd(_after(dpre1_16, f_gu[2]), w_out_full, ac, rstd_ac, g_ac)
    pos_vec = jnp.concatenate([chip_vec, cvec])
    sums, land = _scatter_wait(*f_down, d_ac, "scatter_wait_w_down")
    c_down = _flight_start("complete_start_w_down", [sums, land], _complete_plan(1), 4, cvec)
    p_out = _grad_rows(mixed, dpre1_16, c_down[2][1], "grad_w_out")
    x_out = exchange_begin([p_out], "w_out")
    sums, land = _flight_wait("complete_wait_w_down", c_down, _complete_plan(1), x_out[2][0])
    dq, dkv_cur, dkv_prev, g_sinks, *new_w_down = _attention_bwd(
        proj, d_ac, cos_t, sin_t, sinks, x_out[2][0], (w_down, m_w_down, v_w_down, land, sums, pos_vec, 0))
    ((p_out, got),) = exchange_end(x_out, 1, dq, "w_out")
    f_out = scatter_begin(p_out, got, "w_out")
    sums, land = _scatter_wait(*f_gu, f_out[2], "scatter_wait_w_gu")
    c_gu = _flight_start("complete_start_w_gu", [sums, land], _complete_plan(1), 4, cvec)
    d_proj, g_conv_w = _dproj_assemble(proj, _after(d_ac, c_gu[2][1]), dq, dkv_cur, dkv_prev, cos_t, sin_t, cw_full)
    small_parts = _small_pack(g_ln2_g, g_ln2_b, g_ln1_g, g_ln1_b, g_g_ac, g_conv_w, g_sinks, loss_part)
    f_small = _flight_start("small_start", [small_parts], _small_plan(), N_DEVICES - 1, cvec)
    sums_gu, land_gu = _flight_wait("complete_wait_w_gu", c_gu, _complete_plan(1), f_small[2][0])
    p_in, *new_w_gate = _grad_cols(x16, [d_proj], f_small[2][0], "grad_w_in", a_3d=True,
                                   shard=(w_gate, m_w_gate, v_w_gate, land_gu, sums_gu, pos_vec, 0))
    (small_parts,) = _flight_wait("small_wait", f_small, _small_plan(), p_in)
    red = _small_sum(small_parts)
    x_in = exchange_begin([_after(p_in, red)], "w_in")
    sums, land = _scatter_wait(*f_out, x_in[2][0], "scatter_wait_w_out")
    c_out = _flight_start("complete_start_w_out", [sums, land], _complete_plan(1), 4, cvec)
    new_w_up = _adamw_shard(w_up, m_w_up, v_w_up, _after(land_gu, c_out[2][1]), sums_gu, pos_vec, "adamw_w_up", 1)
    ((p_in, got),) = exchange_end(x_in, 1, new_w_up[0], "w_in")
    f_in = scatter_begin(p_in, got, "w_in")
    (grad_x,) = _dx(d_proj, w_in_g, dpre1, f_in[2], None)

    big = {"w_down": new_w_down, "w_gate": new_w_gate, "w_up": new_w_up}
    sums, land = _scatter_wait(*f_in, grad_x, "scatter_wait_w_in")
    c_in = _flight_start("complete_start_w_in", [sums, land], _complete_plan(1), 4, cvec)
    sums, land = _flight_wait("complete_wait_w_out", c_out, _complete_plan(1), c_in[2][1])
    big["w_out"] = _adamw_shard(w_out, m_w_out, v_w_out, land, sums, pos_vec, "adamw_w_out")
    sums, land = _flight_wait("complete_wait_w_in", c_in, _complete_plan(1), big["w_out"][0])
    big["w_in"] = _adamw_shard(w_in, m_w_in, v_w_in, land, sums, pos_vec, "adamw_w_in")
    small = _adamw_small(red, {
        "sinks": (sinks, m_sinks, v_sinks), "g_attn": (g_attn, m_g_attn, v_g_attn),
        "g_conv": (g_conv, m_g_conv, v_g_conv), "ln1_g": (ln1_g, m_ln1_g, v_ln1_g),
        "ln1_b": (ln1_b, m_ln1_b, v_ln1_b), "ln2_g": (ln2_g, m_ln2_g, v_ln2_g),
        "ln2_b": (ln2_b, m_ln2_b, v_ln2_b), "conv_w": (conv_w, m_conv_w, v_conv_w)})
    res = {**big, **small}
    order = ["w_in", "conv_w", "sinks", "g_attn", "g_conv", "w_out", "ln1_g", "ln1_b", "w_gate", "w_up", "w_down",
             "ln2_g", "ln2_b"]
    loss = red[6, d // 2 + 128]
    return (loss, grad_x, *[res[n][0] for n in order], *[res[n][1] for n in order],
            *[res[n][2] for n in order], *[res[n][3] for n in order])
```
